```python
import math
import jax, jax.numpy as jnp
from jax import lax
import numpy as np

D_MODEL = 1024
BATCH = 16
SEQ = 2048
DEPTH = 2

N_EVEN = (DEPTH + 1) // 2
N_ODD = DEPTH // 2
EPS = 1e-6
RET_HEADS = 4
RET_HEAD_DIM = D_MODEL // 8
RET_WIDTH = RET_HEADS * RET_HEAD_DIM
RET_CHUNK = 128
ROPE_BASE = 10000.0
LRU_WIDTH = D_MODEL // 2
LRU_BLOCKS = 4
LRU_BLOCK_DIM = LRU_WIDTH // LRU_BLOCKS
CONV_WIDTH = 4
LRU_C = 8.0
IN_EVEN_WIDTH = 4 * RET_WIDTH + 2 * LRU_WIDTH
S5_GROUP = 16
S5_GROUPS = D_MODEL // S5_GROUP
S5_STATE = 64
S5_CHUNK = 128
DT_MIN = 0.001
DT_MAX = 0.1
D_FF = 2816

kernel_name = "hybrid_retention_rglru_s5_macaron"


def rmsnorm(x, g):
    xf = x.astype(jnp.float32)
    y = xf * lax.rsqrt(jnp.mean(xf * xf, axis=-1, keepdims=True) + EPS)
    return (y * g.astype(jnp.float32)).astype(x.dtype)


def swiglu(x, w1, w3, w2):
    return (jax.nn.silu(x @ w1) * (x @ w3)) @ w2


def rope(x):
    S, Dh = x.shape[1], x.shape[-1]
    half = Dh // 2
    inv = ROPE_BASE ** (-jnp.arange(half, dtype=jnp.float32) / half)
    ang = jnp.arange(S, dtype=jnp.float32)[:, None] * inv[None, :]
    cos = jnp.cos(ang)[None, :, None, :]
    sin = jnp.sin(ang)[None, :, None, :]
    x1, x2 = x[..., :half], x[..., half:]
    return jnp.concatenate([x1 * cos - x2 * sin, x1 * sin + x2 * cos], axis=-1)


def retention_chunkwise(q, k, v):
    B_, S, H, Dh = q.shape
    C = RET_CHUNK
    N = S // C
    q = rope(q)
    k = rope(k) * (Dh ** -0.5)
    log_gamma = jnp.log1p(-jnp.power(2.0, -5.0 - jnp.arange(H, dtype=jnp.float32)))
    pos = jnp.arange(C, dtype=jnp.float32)
    diff = pos[:, None] - pos[None, :]
    decay = jnp.where(diff >= 0, jnp.exp(log_gamma[:, None, None] * jnp.maximum(diff, 0.0)), 0.0)
    qc = q.reshape(B_, N, C, H, Dh)
    kc = k.reshape(B_, N, C, H, Dh)
    vc = v.reshape(B_, N, C, H, Dh)
    scores = jnp.einsum('bnihd,bnjhd->bhnij', qc, kc) * decay[None, :, None, :, :]
    intra = jnp.einsum('bhnij,bnjhd->bnihd', scores, vc)
    k_decay = jnp.exp(log_gamma[:, None] * (C - 1.0 - pos)[None, :])
    kv = jnp.einsum('bnjhd,hj,bnjhe->nbhde', kc, k_decay, vc)
    chunk_decay = jnp.exp(log_gamma * C)[None, :, None, None]

    def step(R, kv_n):
        return R * chunk_decay + kv_n, R

    _, R_prev = lax.scan(step, jnp.zeros((B_, H, Dh, Dh), jnp.float32), kv)
    q_decay = jnp.exp(log_gamma[:, None] * (pos + 1.0)[None, :])
    cross = jnp.einsum('bnihd,hi,nbhde->bnihe', qc, q_decay, R_prev)
    return (intra + cross).reshape(B_, S, H, Dh)


def head_layernorm(x, g):
    mu = jnp.mean(x, axis=-1, keepdims=True)
    var = jnp.mean(jnp.square(x - mu), axis=-1, keepdims=True)
    y = (x - mu) * lax.rsqrt(var + EPS)
    B_, S, H, Dh = x.shape
    return y.reshape(B_, S, H * Dh) * g.astype(jnp.float32)


def causal_depthwise_conv(x, w, b):
    K = w.shape[0]
    S = x.shape[1]
    xp = jnp.pad(x, ((0, 0), (K - 1, 0), (0, 0)))
    out = b
    for tap in range(K):
        out = out + xp[:, tap:tap + S, :] * w[tap]
    return out


def rg_lru(x, w_a, b_a, w_i, b_i, lam):
    B_, S, W = x.shape
    xb = x.reshape(B_, S, LRU_BLOCKS, LRU_BLOCK_DIM)
    r = jax.nn.sigmoid(jnp.einsum('bsgi,gio->bsgo', xb, w_a.astype(jnp.float32)).reshape(B_, S, W) + b_a.astype(jnp.float32))
    i = jax.nn.sigmoid(jnp.einsum('bsgi,gio->bsgo', xb, w_i.astype(jnp.float32)).reshape(B_, S, W) + b_i.astype(jnp.float32))
    log_a = -LRU_C * r * jax.nn.softplus(-lam.astype(jnp.float32))
    a = jnp.exp(log_a)
    mult = jnp.sqrt(-jnp.expm1(2.0 * log_a))
    bx = mult * i * x

    def comb(c1, c2):
        a1, b1 = c1
        a2, b2 = c2
        return a1 * a2, a2 * b1 + b2

    _, h = lax.associative_scan(comb, (a, bx), axis=1)
    return h


def s5_ssm(u, lam_re, lam_im, log_dt, b_re, b_im, c_re, c_im, d):
    B_, S, H = u.shape
    G, P = lam_re.shape
    L = S5_CHUNK
    N = S // L
    lam_re = lam_re.astype(jnp.float32)
    lam_im = lam_im.astype(jnp.float32)
    dt = jnp.exp(log_dt.astype(jnp.float32))[:, None]
    mag = jnp.exp(lam_re * dt)
    lbar_re = mag * jnp.cos(lam_im * dt)
    lbar_im = mag * jnp.sin(lam_im * dt)
    den = lam_re * lam_re + lam_im * lam_im
    nr = lbar_re - 1.0
    ni = lbar_im
    f_re = ((nr * lam_re + ni * lam_im) / den)[..., None]
    f_im = ((ni * lam_re - nr * lam_im) / den)[..., None]
    b_re = b_re.astype(jnp.float32)
    b_im = b_im.astype(jnp.float32)
    bbar_re = f_re * b_re - f_im * b_im
    bbar_im = f_re * b_im + f_im * b_re
    c_re = c_re.astype(jnp.float32)
    c_im = c_im.astype(jnp.float32)
    a_re = jnp.broadcast_to(lbar_re, (B_, L, G, P))
    a_im = jnp.broadcast_to(lbar_im, (B_, L, G, P))
    uc = jnp.swapaxes(u.reshape(B_, N, L, G, S5_GROUP), 0, 1)

    def comb(c1, c2):
        ar1, ai1, br1, bi1 = c1
        ar2, ai2, br2, bi2 = c2
        return (ar1 * ar2 - ai1 * ai2,
                ar1 * ai2 + ai1 * ar2,
                ar2 * br1 - ai2 * bi1 + br2,
                ar2 * bi1 + ai2 * br1 + bi2)

    def chunk_step(h0, u_n):
        h0_re, h0_im = h0
        bu_re = jnp.einsum('blgc,gpc->blgp', u_n, bbar_re)
        bu_im = jnp.einsum('blgc,gpc->blgp', u_n, bbar_im)
        p_re, p_im, hl_re, hl_im = lax.associative_scan(comb, (a_re, a_im, bu_re, bu_im), axis=1)
        h_re = hl_re + p_re * h0_re[:, None] - p_im * h0_im[:, None]
        h_im = hl_im + p_re * h0_im[:, None] + p_im * h0_re[:, None]
        y = jnp.einsum('blgp,gcp->blgc', h_re, c_re) - jnp.einsum('blgp,gcp->blgc', h_im, c_im)
        return (h_re[:, -1], h_im[:, -1]), y

    h_init = (jnp.zeros((B_, G, P), jnp.float32), jnp.zeros((B_, G, P), jnp.float32))
    _, y = lax.scan(chunk_step, h_init, uc)
    y = jnp.swapaxes(y, 0, 1).reshape(B_, S, H)
    return y + d.astype(jnp.float32) * u


def even_mixer(h, w_in, w_out, ret_norm_g, conv_w, conv_b, w_a, b_a, w_i, b_i, lam):
    B_, S, _ = h.shape
    proj = (h @ w_in).astype(jnp.float32)
    q, k, v, g_ret, x_lru, g_lru = jnp.split(
        proj, [RET_WIDTH, 2 * RET_WIDTH, 3 * RET_WIDTH, 4 * RET_WIDTH, 4 * RET_WIDTH + LRU_WIDTH], axis=-1)
    shp = (B_, S, RET_HEADS, RET_HEAD_DIM)
    ret = retention_chunkwise(q.reshape(shp), k.reshape(shp), v.reshape(shp))
    ret = head_layernorm(ret, ret_norm_g) * jax.nn.silu(g_ret)
    xc = causal_depthwise_conv(x_lru, conv_w.astype(jnp.float32), conv_b.astype(jnp.float32))
    lru = rg_lru(xc, w_a, b_a, w_i, b_i, lam) * jax.nn.gelu(g_lru)
    merged = jnp.concatenate([ret, lru], axis=-1).astype(h.dtype)
    return merged @ w_out


def odd_mixer(h, lam_re, lam_im, log_dt, b_re, b_im, c_re, c_im, d, glu_w_a, glu_w_b):
    y = s5_ssm(h.astype(jnp.float32), lam_re, lam_im, log_dt, b_re, b_im, c_re, c_im, d)
    y = jax.nn.gelu(y).astype(h.dtype)
    return (y @ glu_w_a) * jax.nn.sigmoid(y @ glu_w_b)


def _fwd_setup_inputs(seed: int = 0) -> dict:
    key = jax.random.key(seed)
    ks = jax.random.split(key, 32)
    f32 = jnp.float32

    def nrm(k, shape, scale):
        return jax.random.normal(k, shape, f32) * scale

    x = nrm(ks[0], (BATCH, SEQ, D_MODEL), 1.0)
    ffn_norm_g = 1.0 + nrm(ks[1], (DEPTH, 2, D_MODEL), 0.02)
    ffn_w1 = nrm(ks[2], (DEPTH, 2, D_MODEL, D_FF), D_MODEL ** -0.5)
    ffn_w3 = nrm(ks[3], (DEPTH, 2, D_MODEL, D_FF), D_MODEL ** -0.5)
    ffn_w2 = nrm(ks[4], (DEPTH, 2, D_FF, D_MODEL), D_FF ** -0.5)
    mix_norm_g = 1.0 + nrm(ks[5], (DEPTH, D_MODEL), 0.02)
    w_in_even = nrm(ks[6], (N_EVEN, D_MODEL, IN_EVEN_WIDTH), D_MODEL ** -0.5)
    w_out_even = nrm(ks[7], (N_EVEN, RET_WIDTH + LRU_WIDTH, D_MODEL), (RET_WIDTH + LRU_WIDTH) ** -0.5)
    ret_norm_g = 1.0 + nrm(ks[8], (N_EVEN, RET_WIDTH), 0.02)
    conv_w = nrm(ks[9], (N_EVEN, CONV_WIDTH, LRU_WIDTH), CONV_WIDTH ** -0.5)
    conv_b = nrm(ks[10], (N_EVEN, LRU_WIDTH), 0.01)
    lru_w_a = nrm(ks[11], (N_EVEN, LRU_BLOCKS, LRU_BLOCK_DIM, LRU_BLOCK_DIM), LRU_BLOCK_DIM ** -0.5)
    lru_b_a = nrm(ks[12], (N_EVEN, LRU_WIDTH), 0.01)
    lru_w_i = nrm(ks[13], (N_EVEN, LRU_BLOCKS, LRU_BLOCK_DIM, LRU_BLOCK_DIM), LRU_BLOCK_DIM ** -0.5)
    lru_b_i = nrm(ks[14], (N_EVEN, LRU_WIDTH), 0.01)
    a_c = jax.random.uniform(ks[15], (N_EVEN, LRU_WIDTH), f32, 0.9, 0.999)
    s = a_c ** (1.0 / LRU_C)
    lru_lambda = jnp.log(s) - jnp.log1p(-s)
    n = jnp.arange(S5_STATE, dtype=f32)
    s5_lambda_re = -0.5 + nrm(ks[16], (N_ODD, S5_GROUPS, S5_STATE), 0.01)
    s5_lambda_im = math.pi * n + nrm(ks[17], (N_ODD, S5_GROUPS, S5_STATE), 0.01)
    s5_log_dt = jax.random.uniform(ks[18], (N_ODD, S5_GROUPS), f32, math.log(DT_MIN), math.log(DT_MAX))
    s5_b_re = nrm(ks[19], (N_ODD, S5_GROUPS, S5_STATE, S5_GROUP), (2 * S5_GROUP) ** -0.5)
    s5_b_im = nrm(ks[20], (N_ODD, S5_GROUPS, S5_STATE, S5_GROUP), (2 * S5_GROUP) ** -0.5)
    s5_c_re = nrm(ks[21], (N_ODD, S5_GROUPS, S5_GROUP, S5_STATE), (2 * S5_STATE) ** -0.5)
    s5_c_im = nrm(ks[22], (N_ODD, S5_GROUPS, S5_GROUP, S5_STATE), (2 * S5_STATE) ** -0.5)
    s5_d = nrm(ks[23], (N_ODD, D_MODEL), 1.0)
    glu_w_a = nrm(ks[24], (N_ODD, D_MODEL, D_MODEL), D_MODEL ** -0.5)
    glu_w_b = nrm(ks[25], (N_ODD, D_MODEL, D_MODEL), D_MODEL ** -0.5)
    final_norm_g = 1.0 + nrm(ks[26], (D_MODEL,), 0.02)
    return {"x": x, "ffn_norm_g": ffn_norm_g, "ffn_w1": ffn_w1, "ffn_w3": ffn_w3, "ffn_w2": ffn_w2,
            "mix_norm_g": mix_norm_g, "w_in_even": w_in_even, "w_out_even": w_out_even,
            "ret_norm_g": ret_norm_g, "conv_w": conv_w, "conv_b": conv_b,
            "lru_w_a": lru_w_a, "lru_b_a": lru_b_a, "lru_w_i": lru_w_i, "lru_b_i": lru_b_i,
            "lru_lambda": lru_lambda, "s5_lambda_re": s5_lambda_re, "s5_lambda_im": s5_lambda_im,
            "s5_log_dt": s5_log_dt, "s5_b_re": s5_b_re, "s5_b_im": s5_b_im,
            "s5_c_re": s5_c_re, "s5_c_im": s5_c_im, "s5_d": s5_d,
            "glu_w_a": glu_w_a, "glu_w_b": glu_w_b, "final_norm_g": final_norm_g}


def _fwd_reference(x, ffn_norm_g, ffn_w1, ffn_w3, ffn_w2, mix_norm_g, w_in_even, w_out_even,
              ret_norm_g, conv_w, conv_b, lru_w_a, lru_b_a, lru_w_i, lru_b_i, lru_lambda,
              s5_lambda_re, s5_lambda_im, s5_log_dt, s5_b_re, s5_b_im, s5_c_re, s5_c_im, s5_d,
              glu_w_a, glu_w_b, final_norm_g):
    for layer in range(DEPTH):
        x = x + 0.5 * swiglu(rmsnorm(x, ffn_norm_g[layer, 0]), ffn_w1[layer, 0], ffn_w3[layer, 0], ffn_w2[layer, 0])
        h = rmsnorm(x, mix_norm_g[layer])
        if layer % 2 == 0:
            e = layer // 2
            x = x + even_mixer(h, w_in_even[e], w_out_even[e], ret_norm_g[e], conv_w[e], conv_b[e],
                               lru_w_a[e], lru_b_a[e], lru_w_i[e], lru_b_i[e], lru_lambda[e])
        else:
            o = layer // 2
            x = x + odd_mixer(h, s5_lambda_re[o], s5_lambda_im[o], s5_log_dt[o], s5_b_re[o], s5_b_im[o],
                              s5_c_re[o], s5_c_im[o], s5_d[o], glu_w_a[o], glu_w_b[o])
        x = x + 0.5 * swiglu(rmsnorm(x, ffn_norm_g[layer, 1]), ffn_w1[layer, 1], ffn_w3[layer, 1], ffn_w2[layer, 1])
    return rmsnorm(x, final_norm_g)


import jax as _jax
import jax.numpy as _jnp

TWIN_FORMAT = 'train_step'
FWD_PARAMS = ['x', 'ffn_norm_g', 'ffn_w1', 'ffn_w3', 'ffn_w2', 'mix_norm_g', 'w_in_even', 'w_out_even', 'ret_norm_g', 'conv_w', 'conv_b', 'lru_w_a', 'lru_b_a', 'lru_w_i', 'lru_b_i', 'lru_lambda', 's5_lambda_re', 's5_lambda_im', 's5_log_dt', 's5_b_re', 's5_b_im', 's5_c_re', 's5_c_im', 's5_d', 'glu_w_a', 'glu_w_b', 'final_norm_g']
TWIN_WEIGHTS = ['ffn_norm_g', 'ffn_w1', 'ffn_w3', 'ffn_w2', 'mix_norm_g', 'w_in_even', 'w_out_even', 'ret_norm_g', 'conv_w', 'conv_b', 'lru_w_a', 'lru_b_a', 'lru_w_i', 'lru_b_i', 'lru_lambda', 's5_lambda_re', 's5_lambda_im', 's5_log_dt', 's5_b_re', 's5_b_im', 's5_c_re', 's5_c_im', 's5_d', 'glu_w_a', 'glu_w_b', 'final_norm_g']
TWIN_DIFF_INPUT = 'x'
TWIN_INPUTS = ['x', 'ffn_norm_g', 'ffn_w1', 'ffn_w3', 'ffn_w2', 'mix_norm_g', 'w_in_even', 'w_out_even', 'ret_norm_g', 'conv_w', 'conv_b', 'lru_w_a', 'lru_b_a', 'lru_w_i', 'lru_b_i', 'lru_lambda', 's5_lambda_re', 's5_lambda_im', 's5_log_dt', 's5_b_re', 's5_b_im', 's5_c_re', 's5_c_im', 's5_d', 'glu_w_a', 'glu_w_b', 'final_norm_g', 'loss_target', 'm_ffn_norm_g', 'm_ffn_w1', 'm_ffn_w3', 'm_ffn_w2', 'm_mix_norm_g', 'm_w_in_even', 'm_w_out_even', 'm_ret_norm_g', 'm_conv_w', 'm_conv_b', 'm_lru_w_a', 'm_lru_b_a', 'm_lru_w_i', 'm_lru_b_i', 'm_lru_lambda', 'm_s5_lambda_re', 'm_s5_lambda_im', 'm_s5_log_dt', 'm_s5_b_re', 'm_s5_b_im', 'm_s5_c_re', 'm_s5_c_im', 'm_s5_d', 'm_glu_w_a', 'm_glu_w_b', 'm_final_norm_g', 'v_ffn_norm_g', 'v_ffn_w1', 'v_ffn_w3', 'v_ffn_w2', 'v_mix_norm_g', 'v_w_in_even', 'v_w_out_even', 'v_ret_norm_g', 'v_conv_w', 'v_conv_b', 'v_lru_w_a', 'v_lru_b_a', 'v_lru_w_i', 'v_lru_b_i', 'v_lru_lambda', 'v_s5_lambda_re', 'v_s5_lambda_im', 'v_s5_log_dt', 'v_s5_b_re', 'v_s5_b_im', 'v_s5_c_re', 'v_s5_c_im', 'v_s5_d', 'v_glu_w_a', 'v_glu_w_b', 'v_final_norm_g']
TWIN_OUTPUTS = ['loss', 'grad_x', 'grad_ffn_norm_g', 'grad_ffn_w1', 'grad_ffn_w3', 'grad_ffn_w2', 'grad_mix_norm_g', 'grad_w_in_even', 'grad_w_out_even', 'grad_ret_norm_g', 'grad_conv_w', 'grad_conv_b', 'grad_lru_w_a', 'grad_lru_b_a', 'grad_lru_w_i', 'grad_lru_b_i', 'grad_lru_lambda', 'grad_s5_lambda_re', 'grad_s5_lambda_im', 'grad_s5_log_dt', 'grad_s5_b_re', 'grad_s5_b_im', 'grad_s5_c_re', 'grad_s5_c_im', 'grad_s5_d', 'grad_glu_w_a', 'grad_glu_w_b', 'grad_final_norm_g', 'delta_ffn_norm_g', 'delta_ffn_w1', 'delta_ffn_w3', 'delta_ffn_w2', 'delta_mix_norm_g', 'delta_w_in_even', 'delta_w_out_even', 'delta_ret_norm_g', 'delta_conv_w', 'delta_conv_b', 'delta_lru_w_a', 'delta_lru_b_a', 'delta_lru_w_i', 'delta_lru_b_i', 'delta_lru_lambda', 'delta_s5_lambda_re', 'delta_s5_lambda_im', 'delta_s5_log_dt', 'delta_s5_b_re', 'delta_s5_b_im', 'delta_s5_c_re', 'delta_s5_c_im', 'delta_s5_d', 'delta_glu_w_a', 'delta_glu_w_b', 'delta_final_norm_g', 'new_m_ffn_norm_g', 'new_m_ffn_w1', 'new_m_ffn_w3', 'new_m_ffn_w2', 'new_m_mix_norm_g', 'new_m_w_in_even', 'new_m_w_out_even', 'new_m_ret_norm_g', 'new_m_conv_w', 'new_m_conv_b', 'new_m_lru_w_a', 'new_m_lru_b_a', 'new_m_lru_w_i', 'new_m_lru_b_i', 'new_m_lru_lambda', 'new_m_s5_lambda_re', 'new_m_s5_lambda_im', 'new_m_s5_log_dt', 'new_m_s5_b_re', 'new_m_s5_b_im', 'new_m_s5_c_re', 'new_m_s5_c_im', 'new_m_s5_d', 'new_m_glu_w_a', 'new_m_glu_w_b', 'new_m_final_norm_g', 'new_v_ffn_norm_g', 'new_v_ffn_w1', 'new_v_ffn_w3', 'new_v_ffn_w2', 'new_v_mix_norm_g', 'new_v_w_in_even', 'new_v_w_out_even', 'new_v_ret_norm_g', 'new_v_conv_w', 'new_v_conv_b', 'new_v_lru_w_a', 'new_v_lru_b_a', 'new_v_lru_w_i', 'new_v_lru_b_i', 'new_v_lru_lambda', 'new_v_s5_lambda_re', 'new_v_s5_lambda_im', 'new_v_s5_log_dt', 'new_v_s5_b_re', 'new_v_s5_b_im', 'new_v_s5_c_re', 'new_v_s5_c_im', 'new_v_s5_d', 'new_v_glu_w_a', 'new_v_glu_w_b', 'new_v_final_norm_g']
TWIN_LEAF_KINDS = {'loss': 'loss', 'grad_x': 'grad_x', 'grad_ffn_norm_g': 'grad_w', 'grad_ffn_w1': 'grad_w', 'grad_ffn_w3': 'grad_w', 'grad_ffn_w2': 'grad_w', 'grad_mix_norm_g': 'grad_w', 'grad_w_in_even': 'grad_w', 'grad_w_out_even': 'grad_w', 'grad_ret_norm_g': 'grad_w', 'grad_conv_w': 'grad_w', 'grad_conv_b': 'grad_w', 'grad_lru_w_a': 'grad_w', 'grad_lru_b_a': 'grad_w', 'grad_lru_w_i': 'grad_w', 'grad_lru_b_i': 'grad_w', 'grad_lru_lambda': 'grad_w', 'grad_s5_lambda_re': 'grad_w', 'grad_s5_lambda_im': 'grad_w', 'grad_s5_log_dt': 'grad_w', 'grad_s5_b_re': 'grad_w', 'grad_s5_b_im': 'grad_w', 'grad_s5_c_re': 'grad_w', 'grad_s5_c_im': 'grad_w', 'grad_s5_d': 'grad_w', 'grad_glu_w_a': 'grad_w', 'grad_glu_w_b': 'grad_w', 'grad_final_norm_g': 'grad_w', 'delta_ffn_norm_g': 'delta_w', 'delta_ffn_w1': 'delta_w', 'delta_ffn_w3': 'delta_w', 'delta_ffn_w2': 'delta_w', 'delta_mix_norm_g': 'delta_w', 'delta_w_in_even': 'delta_w', 'delta_w_out_even': 'delta_w', 'delta_ret_norm_g': 'delta_w', 'delta_conv_w': 'delta_w', 'delta_conv_b': 'delta_w', 'delta_lru_w_a': 'delta_w', 'delta_lru_b_a': 'delta_w', 'delta_lru_w_i': 'delta_w', 'delta_lru_b_i': 'delta_w', 'delta_lru_lambda': 'delta_w', 'delta_s5_lambda_re': 'delta_w', 'delta_s5_lambda_im': 'delta_w', 'delta_s5_log_dt': 'delta_w', 'delta_s5_b_re': 'delta_w', 'delta_s5_b_im': 'delta_w', 'delta_s5_c_re': 'delta_w', 'delta_s5_c_im': 'delta_w', 'delta_s5_d': 'delta_w', 'delta_glu_w_a': 'delta_w', 'delta_glu_w_b': 'delta_w', 'delta_final_norm_g': 'delta_w', 'new_m_ffn_norm_g': 'new_m', 'new_m_ffn_w1': 'new_m', 'new_m_ffn_w3': 'new_m', 'new_m_ffn_w2': 'new_m', 'new_m_mix_norm_g': 'new_m', 'new_m_w_in_even': 'new_m', 'new_m_w_out_even': 'new_m', 'new_m_ret_norm_g': 'new_m', 'new_m_conv_w': 'new_m', 'new_m_conv_b': 'new_m', 'new_m_lru_w_a': 'new_m', 'new_m_lru_b_a': 'new_m', 'new_m_lru_w_i': 'new_m', 'new_m_lru_b_i': 'new_m', 'new_m_lru_lambda': 'new_m', 'new_m_s5_lambda_re': 'new_m', 'new_m_s5_lambda_im': 'new_m', 'new_m_s5_log_dt': 'new_m', 'new_m_s5_b_re': 'new_m', 'new_m_s5_b_im': 'new_m', 'new_m_s5_c_re': 'new_m', 'new_m_s5_c_im': 'new_m', 'new_m_s5_d': 'new_m', 'new_m_glu_w_a': 'new_m', 'new_m_glu_w_b': 'new_m', 'new_m_final_norm_g': 'new_m', 'new_v_ffn_norm_g': 'new_v', 'new_v_ffn_w1': 'new_v', 'new_v_ffn_w3': 'new_v', 'new_v_ffn_w2': 'new_v', 'new_v_mix_norm_g': 'new_v', 'new_v_w_in_even': 'new_v', 'new_v_w_out_even': 'new_v', 'new_v_ret_norm_g': 'new_v', 'new_v_conv_w': 'new_v', 'new_v_conv_b': 'new_v', 'new_v_lru_w_a': 'new_v', 'new_v_lru_b_a': 'new_v', 'new_v_lru_w_i': 'new_v', 'new_v_lru_b_i': 'new_v', 'new_v_lru_lambda': 'new_v', 'new_v_s5_lambda_re': 'new_v', 'new_v_s5_lambda_im': 'new_v', 'new_v_s5_log_dt': 'new_v', 'new_v_s5_b_re': 'new_v', 'new_v_s5_b_im': 'new_v', 'new_v_s5_c_re': 'new_v', 'new_v_s5_c_im': 'new_v', 'new_v_s5_d': 'new_v', 'new_v_glu_w_a': 'new_v', 'new_v_glu_w_b': 'new_v', 'new_v_final_norm_g': 'new_v'}


def _forward(args):
    return _fwd_reference(*[args[k] for k in FWD_PARAMS])


def _output_shape():
    out = _jax.eval_shape(lambda: _forward(_fwd_setup_inputs(0)))
    return out.shape, out.dtype

N_MICROBATCH = 1
ADAM_LR = 0.001
ADAM_B1 = 0.9
ADAM_B2 = 0.999
ADAM_EPS = 1e-08
ADAM_WD = 0.01
ADAM_STEP = 10
PER_EXAMPLE_BATCH_AXIS = {'x': 0, 'loss_target': 0}
SHARED_INPUTS = []
_WEIGHT_DTYPES = {'ffn_norm_g': _jnp.float32, 'ffn_w1': _jnp.float32, 'ffn_w3': _jnp.float32, 'ffn_w2': _jnp.float32, 'mix_norm_g': _jnp.float32, 'w_in_even': _jnp.float32, 'w_out_even': _jnp.float32, 'ret_norm_g': _jnp.float32, 'conv_w': _jnp.float32, 'conv_b': _jnp.float32, 'lru_w_a': _jnp.float32, 'lru_b_a': _jnp.float32, 'lru_w_i': _jnp.float32, 'lru_b_i': _jnp.float32, 'lru_lambda': _jnp.float32, 's5_lambda_re': _jnp.float32, 's5_lambda_im': _jnp.float32, 's5_log_dt': _jnp.float32, 's5_b_re': _jnp.float32, 's5_b_im': _jnp.float32, 's5_c_re': _jnp.float32, 's5_c_im': _jnp.float32, 's5_d': _jnp.float32, 'glu_w_a': _jnp.float32, 'glu_w_b': _jnp.float32, 'final_norm_g': _jnp.float32}
MOMENT_SCALE = {'ffn_norm_g': 7.638243e-02, 'ffn_w1': 3.164189e-02, 'ffn_w3': 3.064556e-02, 'ffn_w2': 5.087366e-02, 'mix_norm_g': 1.194251e-01, 'w_in_even': 9.330394e-02, 'w_out_even': 8.864960e-02, 'ret_norm_g': 9.703508e-02, 'conv_w': 7.032955e-02, 'conv_b': 9.614348e-01, 'lru_w_a': 2.677125e-02, 'lru_b_a': 2.040804e-02, 'lru_w_i': 4.897216e-02, 'lru_b_i': 2.375457e-02, 'lru_lambda': 3.869370e-02, 's5_lambda_re': 2.939816e-03, 's5_lambda_im': 3.057168e-03, 's5_log_dt': 2.238443e+00, 's5_b_re': 1.901602e-03, 's5_b_im': 1.921775e-03, 's5_c_re': 3.709984e-03, 's5_c_im': 3.823542e-03, 's5_d': 5.335229e-02, 'glu_w_a': 4.943836e-02, 'glu_w_b': 1.518533e-02, 'final_norm_g': 3.196176e+01}


def _to_microbatches(a, axis):
    t = _jnp.moveaxis(a, axis, 0)
    t = t.reshape((N_MICROBATCH, t.shape[0] // N_MICROBATCH) + t.shape[1:])
    return _jnp.moveaxis(t, 1, axis + 1)


def setup_inputs(seed: int = 0) -> dict:
    inp = _fwd_setup_inputs(seed)
    key = _jax.random.fold_in(_jax.random.key(seed), 7919)
    shape, _ = _output_shape()
    out = dict(inp)
    out["loss_target"] = _jax.random.normal(_jax.random.fold_in(key, 0), shape, _jnp.float32)
    for i, name in enumerate(TWIN_WEIGHTS):
        w = inp[name].astype(_jnp.float32)
        if MOMENT_SCALE is None:
            s = _jnp.sqrt(_jnp.mean(_jnp.square(w)) + 1e-30)
        else:
            s = MOMENT_SCALE[name]
        km, kv = _jax.random.split(_jax.random.fold_in(key, i + 1))
        out[name] = w
        out["m_" + name] = s * _jax.random.normal(km, w.shape, _jnp.float32)
        out["v_" + name] = (s * s) * _jax.random.uniform(kv, w.shape, _jnp.float32, 0.5, 1.5)
    if N_MICROBATCH > 1:
        for name, axis in PER_EXAMPLE_BATCH_AXIS.items():
            out[name] = _to_microbatches(out[name], axis)
    return {'x': out['x'], 'ffn_norm_g': out['ffn_norm_g'], 'ffn_w1': out['ffn_w1'], 'ffn_w3': out['ffn_w3'], 'ffn_w2': out['ffn_w2'], 'mix_norm_g': out['mix_norm_g'], 'w_in_even': out['w_in_even'], 'w_out_even': out['w_out_even'], 'ret_norm_g': out['ret_norm_g'], 'conv_w': out['conv_w'], 'conv_b': out['conv_b'], 'lru_w_a': out['lru_w_a'], 'lru_b_a': out['lru_b_a'], 'lru_w_i': out['lru_w_i'], 'lru_b_i': out['lru_b_i'], 'lru_lambda': out['lru_lambda'], 's5_lambda_re': out['s5_lambda_re'], 's5_lambda_im': out['s5_lambda_im'], 's5_log_dt': out['s5_log_dt'], 's5_b_re': out['s5_b_re'], 's5_b_im': out['s5_b_im'], 's5_c_re': out['s5_c_re'], 's5_c_im': out['s5_c_im'], 's5_d': out['s5_d'], 'glu_w_a': out['glu_w_a'], 'glu_w_b': out['glu_w_b'], 'final_norm_g': out['final_norm_g'], 'loss_target': out['loss_target'], 'm_ffn_norm_g': out['m_ffn_norm_g'], 'm_ffn_w1': out['m_ffn_w1'], 'm_ffn_w3': out['m_ffn_w3'], 'm_ffn_w2': out['m_ffn_w2'], 'm_mix_norm_g': out['m_mix_norm_g'], 'm_w_in_even': out['m_w_in_even'], 'm_w_out_even': out['m_w_out_even'], 'm_ret_norm_g': out['m_ret_norm_g'], 'm_conv_w': out['m_conv_w'], 'm_conv_b': out['m_conv_b'], 'm_lru_w_a': out['m_lru_w_a'], 'm_lru_b_a': out['m_lru_b_a'], 'm_lru_w_i': out['m_lru_w_i'], 'm_lru_b_i': out['m_lru_b_i'], 'm_lru_lambda': out['m_lru_lambda'], 'm_s5_lambda_re': out['m_s5_lambda_re'], 'm_s5_lambda_im': out['m_s5_lambda_im'], 'm_s5_log_dt': out['m_s5_log_dt'], 'm_s5_b_re': out['m_s5_b_re'], 'm_s5_b_im': out['m_s5_b_im'], 'm_s5_c_re': out['m_s5_c_re'], 'm_s5_c_im': out['m_s5_c_im'], 'm_s5_d': out['m_s5_d'], 'm_glu_w_a': out['m_glu_w_a'], 'm_glu_w_b': out['m_glu_w_b'], 'm_final_norm_g': out['m_final_norm_g'], 'v_ffn_norm_g': out['v_ffn_norm_g'], 'v_ffn_w1': out['v_ffn_w1'], 'v_ffn_w3': out['v_ffn_w3'], 'v_ffn_w2': out['v_ffn_w2'], 'v_mix_norm_g': out['v_mix_norm_g'], 'v_w_in_even': out['v_w_in_even'], 'v_w_out_even': out['v_w_out_even'], 'v_ret_norm_g': out['v_ret_norm_g'], 'v_conv_w': out['v_conv_w'], 'v_conv_b': out['v_conv_b'], 'v_lru_w_a': out['v_lru_w_a'], 'v_lru_b_a': out['v_lru_b_a'], 'v_lru_w_i': out['v_lru_w_i'], 'v_lru_b_i': out['v_lru_b_i'], 'v_lru_lambda': out['v_lru_lambda'], 'v_s5_lambda_re': out['v_s5_lambda_re'], 'v_s5_lambda_im': out['v_s5_lambda_im'], 'v_s5_log_dt': out['v_s5_log_dt'], 'v_s5_b_re': out['v_s5_b_re'], 'v_s5_b_im': out['v_s5_b_im'], 'v_s5_c_re': out['v_s5_c_re'], 'v_s5_c_im': out['v_s5_c_im'], 'v_s5_d': out['v_s5_d'], 'v_glu_w_a': out['v_glu_w_a'], 'v_glu_w_b': out['v_glu_w_b'], 'v_final_norm_g': out['v_final_norm_g']}


def _loss(weights, diff, rest, loss_target):
    with _jax.named_scope("forward"):
        args = {**rest, TWIN_DIFF_INPUT: diff, **{k: w.astype(_WEIGHT_DTYPES[k]) for k, w in weights.items()}}
        y = _forward(args)
    with _jax.named_scope("loss_head"):
        err = _jnp.square(y.astype(_jnp.float32) - loss_target)
        return 0.5 * _jnp.sum(_jnp.mean(err, axis=-1)) if err.ndim else 0.5 * err


def _adamw(w, g, m, v):
    m = ADAM_B1 * m + (1.0 - ADAM_B1) * g
    v = ADAM_B2 * v + (1.0 - ADAM_B2) * _jnp.square(g)
    m_hat = m / (1.0 - ADAM_B1 ** ADAM_STEP)
    v_hat = v / (1.0 - ADAM_B2 ** ADAM_STEP)
    delta = -ADAM_LR * (m_hat / (_jnp.sqrt(v_hat) + ADAM_EPS) + ADAM_WD * w)
    return delta, m, v


def reference(x, ffn_norm_g, ffn_w1, ffn_w3, ffn_w2, mix_norm_g, w_in_even, w_out_even, ret_norm_g, conv_w, conv_b, lru_w_a, lru_b_a, lru_w_i, lru_b_i, lru_lambda, s5_lambda_re, s5_lambda_im, s5_log_dt, s5_b_re, s5_b_im, s5_c_re, s5_c_im, s5_d, glu_w_a, glu_w_b, final_norm_g, loss_target, m_ffn_norm_g, m_ffn_w1, m_ffn_w3, m_ffn_w2, m_mix_norm_g, m_w_in_even, m_w_out_even, m_ret_norm_g, m_conv_w, m_conv_b, m_lru_w_a, m_lru_b_a, m_lru_w_i, m_lru_b_i, m_lru_lambda, m_s5_lambda_re, m_s5_lambda_im, m_s5_log_dt, m_s5_b_re, m_s5_b_im, m_s5_c_re, m_s5_c_im, m_s5_d, m_glu_w_a, m_glu_w_b, m_final_norm_g, v_ffn_norm_g, v_ffn_w1, v_ffn_w3, v_ffn_w2, v_mix_norm_g, v_w_in_even, v_w_out_even, v_ret_norm_g, v_conv_w, v_conv_b, v_lru_w_a, v_lru_b_a, v_lru_w_i, v_lru_b_i, v_lru_lambda, v_s5_lambda_re, v_s5_lambda_im, v_s5_log_dt, v_s5_b_re, v_s5_b_im, v_s5_c_re, v_s5_c_im, v_s5_d, v_glu_w_a, v_glu_w_b, v_final_norm_g):
    given = dict(x=x, ffn_norm_g=ffn_norm_g, ffn_w1=ffn_w1, ffn_w3=ffn_w3, ffn_w2=ffn_w2, mix_norm_g=mix_norm_g, w_in_even=w_in_even, w_out_even=w_out_even, ret_norm_g=ret_norm_g, conv_w=conv_w, conv_b=conv_b, lru_w_a=lru_w_a, lru_b_a=lru_b_a, lru_w_i=lru_w_i, lru_b_i=lru_b_i, lru_lambda=lru_lambda, s5_lambda_re=s5_lambda_re, s5_lambda_im=s5_lambda_im, s5_log_dt=s5_log_dt, s5_b_re=s5_b_re, s5_b_im=s5_b_im, s5_c_re=s5_c_re, s5_c_im=s5_c_im, s5_d=s5_d, glu_w_a=glu_w_a, glu_w_b=glu_w_b, final_norm_g=final_norm_g, loss_target=loss_target, m_ffn_norm_g=m_ffn_norm_g, m_ffn_w1=m_ffn_w1, m_ffn_w3=m_ffn_w3, m_ffn_w2=m_ffn_w2, m_mix_norm_g=m_mix_norm_g, m_w_in_even=m_w_in_even, m_w_out_even=m_w_out_even, m_ret_norm_g=m_ret_norm_g, m_conv_w=m_conv_w, m_conv_b=m_conv_b, m_lru_w_a=m_lru_w_a, m_lru_b_a=m_lru_b_a, m_lru_w_i=m_lru_w_i, m_lru_b_i=m_lru_b_i, m_lru_lambda=m_lru_lambda, m_s5_lambda_re=m_s5_lambda_re, m_s5_lambda_im=m_s5_lambda_im, m_s5_log_dt=m_s5_log_dt, m_s5_b_re=m_s5_b_re, m_s5_b_im=m_s5_b_im, m_s5_c_re=m_s5_c_re, m_s5_c_im=m_s5_c_im, m_s5_d=m_s5_d, m_glu_w_a=m_glu_w_a, m_glu_w_b=m_glu_w_b, m_final_norm_g=m_final_norm_g, v_ffn_norm_g=v_ffn_norm_g, v_ffn_w1=v_ffn_w1, v_ffn_w3=v_ffn_w3, v_ffn_w2=v_ffn_w2, v_mix_norm_g=v_mix_norm_g, v_w_in_even=v_w_in_even, v_w_out_even=v_w_out_even, v_ret_norm_g=v_ret_norm_g, v_conv_w=v_conv_w, v_conv_b=v_conv_b, v_lru_w_a=v_lru_w_a, v_lru_b_a=v_lru_b_a, v_lru_w_i=v_lru_w_i, v_lru_b_i=v_lru_b_i, v_lru_lambda=v_lru_lambda, v_s5_lambda_re=v_s5_lambda_re, v_s5_lambda_im=v_s5_lambda_im, v_s5_log_dt=v_s5_log_dt, v_s5_b_re=v_s5_b_re, v_s5_b_im=v_s5_b_im, v_s5_c_re=v_s5_c_re, v_s5_c_im=v_s5_c_im, v_s5_d=v_s5_d, v_glu_w_a=v_glu_w_a, v_glu_w_b=v_glu_w_b, v_final_norm_g=v_final_norm_g)
    weights = {n: given[n] for n in TWIN_WEIGHTS}
    shared = {n: given[n] for n in SHARED_INPUTS}
    per_example = {n: given[n] for n in ['x']}
    grad_fn = _jax.value_and_grad(_loss, argnums=(0, 1))

    def one_microbatch(ex, loss_target):
        ex = dict(ex)
        diff = ex.pop(TWIN_DIFF_INPUT)
        return grad_fn(weights, diff, {**shared, **ex}, loss_target)

    if N_MICROBATCH == 1:
        loss, (grad_w, grad_x) = one_microbatch(per_example, given["loss_target"])
    else:
        def body(carry, xs):
            loss_sum, grad_sum = carry
            l_k, (gw_k, gx_k) = one_microbatch(xs[0], xs[1])
            with _jax.named_scope("update"):
                return (loss_sum + l_k, _jax.tree.map(_jnp.add, grad_sum, gw_k)), gx_k

        init = (_jnp.zeros((), _jnp.float32), _jax.tree.map(_jnp.zeros_like, weights))
        (loss, grad_w), grad_x = _jax.lax.scan(body, init, (per_example, given["loss_target"]))
    with _jax.named_scope("update"):
        delta_w, new_m, new_v = {}, {}, {}
        for n in TWIN_WEIGHTS:
            delta_w[n], new_m[n], new_v[n] = _adamw(weights[n], grad_w[n], given["m_" + n], given["v_" + n])
    return (loss, grad_x, *[grad_w[n] for n in TWIN_WEIGHTS], *[delta_w[n] for n in TWIN_WEIGHTS],
            *[new_m[n] for n in TWIN_WEIGHTS], *[new_v[n] for n in TWIN_WEIGHTS])
```

```python
import functools
import math

import numpy as np
import jax
import jax.numpy as jnp
from jax import lax
from jax.experimental import pallas as pl
from jax.experimental.pallas import tpu as pltpu

f32 = jnp.float32
bf16 = jnp.bfloat16

D_MODEL = 1024
N_DEV = 8
EPS = 1e-6
RET_HEADS = 4
HEAD_DIM = 128
RET_WIDTH = 512
RET_CHUNK = 128
ROPE_BASE = 10000.0
LRU_WIDTH = 512
LRU_BLOCKS = 4
LRU_C = 8.0
S5_GROUP = 16
S5_GROUPS = 64
S5_STATE = 64
S5_CHUNK = 128
S5_BLOCKS = 8
S5_BLOCK_STATES = 512
D_FF = 2816
FF_SHARD = D_FF // N_DEV
IN_SHARD = 3072 // N_DEV
ADAM_LR = 0.001
ADAM_B1 = 0.9
ADAM_B2 = 0.999
ADAM_EPS = 1e-08
ADAM_WD = 0.01
ADAM_STEP = 10

VMEM_LIMIT = 48 * 1024 * 1024
VMEM_SPEC = pl.BlockSpec(memory_space=pltpu.VMEM)
ANY_SPEC = pl.BlockSpec(memory_space=pl.ANY)
MESH = pl.DeviceIdType.MESH


def _cp(*sem):
    return pltpu.CompilerParams(dimension_semantics=sem, vmem_limit_bytes=VMEM_LIMIT)


def _nn(a, b):
    return jnp.dot(a, b, preferred_element_type=f32)


def _nt(a, b):
    return lax.dot_general(a, b, (((1,), (1,)), ((), ())), preferred_element_type=f32)


def _tn(a, b):
    return lax.dot_general(a, b, (((0,), (0,)), ((), ())), preferred_element_type=f32)


def _rms_fwd(x, g):
    r = lax.rsqrt(jnp.mean(x * x, axis=-1, keepdims=True) + EPS)
    xn = x * r
    return xn * g, xn, r


def _rms_bwd(dh, xn, r, g):
    dxn = dh * g
    dx = r * (dxn - xn * jnp.mean(dxn * xn, axis=-1, keepdims=True))
    dg = jnp.sum(dh * xn, axis=0, keepdims=True)
    return dx, dg


def _shift_dn(v, d, row, fill=0.0):
    return jnp.where(row >= d, pltpu.roll(v, d, 0), fill)


def _shift_up(v, d, row, fill=0.0):
    n = v.shape[0]
    return jnp.where(row < n - d, pltpu.roll(v, n - d, 0), fill)


def _ew(name, fn, ins, outs, tm=512):
    t = ins[0].shape[0]
    n_in = len(ins)

    def body(*refs):
        res = fn(*[r[...] for r in refs[:n_in]])
        for o, v in zip(refs[n_in:], res):
            o[...] = v.astype(o.dtype)

    return pl.pallas_call(
        body, name=name, grid=(t // tm,),
        in_specs=[pl.BlockSpec((tm, a.shape[1]), lambda i: (i, 0)) for a in ins],
        out_specs=[pl.BlockSpec((tm, n), lambda i: (i, 0)) for n, _ in outs],
        out_shape=[jax.ShapeDtypeStruct((t, n), dt) for n, dt in outs],
        compiler_params=_cp("parallel"),
    )(*ins)


def _mm(name, x, w, kind, extras=(), epilogue=None, outs=None, tm=512, tn=512):
    t = x.shape[0]
    n = w.shape[1] if kind == "nn" else w.shape[0]
    tn = min(tn, n)
    outs = outs or [f32]
    n_ex = len(extras)

    def body(x_ref, w_ref, *refs):
        acc = _nn(x_ref[...], w_ref[...]) if kind == "nn" else _nt(x_ref[...], w_ref[...])
        res = epilogue(acc, *[r[...] for r in refs[:n_ex]]) if epilogue else (acc,)
        for o, v in zip(refs[n_ex:], res):
            o[...] = v.astype(o.dtype)

    w_spec = (pl.BlockSpec((w.shape[0], tn), lambda i, j: (0, j)) if kind == "nn"
              else pl.BlockSpec((tn, w.shape[1]), lambda i, j: (j, 0)))
    tile = pl.BlockSpec((tm, tn), lambda i, j: (i, j))
    return pl.pallas_call(
        body, name=name, grid=(t // tm, n // tn),
        in_specs=[pl.BlockSpec((tm, x.shape[1]), lambda i, j: (i, 0)), w_spec] + [tile] * n_ex,
        out_specs=[tile] * len(outs),
        out_shape=[jax.ShapeDtypeStruct((t, n), dt) for dt in outs],
        compiler_params=_cp("parallel", "parallel"),
    )(x, w, *extras)


def _mm_tn(name, x, y, tk=1024, tn=512, tt=512):
    t, k = x.shape
    n = y.shape[1]
    tk, tn = min(tk, k), min(tn, n)

    def body(x_ref, y_ref, o_ref):
        @pl.when(pl.program_id(2) == 0)
        def _():
            o_ref[...] = jnp.zeros_like(o_ref)
        o_ref[...] += _tn(x_ref[...], y_ref[...])

    return pl.pallas_call(
        body, name=name, grid=(k // tk, n // tn, t // tt),
        in_specs=[pl.BlockSpec((tt, tk), lambda i, j, s: (s, i)), pl.BlockSpec((tt, tn), lambda i, j, s: (s, j))],
        out_specs=pl.BlockSpec((tk, tn), lambda i, j, s: (i, j)),
        out_shape=jax.ShapeDtypeStruct((k, n), f32),
        compiler_params=_cp("parallel", "parallel", "arbitrary"),
    )(x, y)


def _norm_fwd(name, x, g, tm=512):
    t, d = x.shape

    def body(x_ref, g_ref, h_ref, hb_ref):
        h, _, _ = _rms_fwd(x_ref[...], g_ref[...])
        h_ref[...] = h
        hb_ref[...] = h.astype(bf16)

    row = pl.BlockSpec((tm, d), lambda i: (i, 0))
    return pl.pallas_call(
        body, name=name, grid=(t // tm,),
        in_specs=[row, pl.BlockSpec((1, d), lambda i: (0, 0))],
        out_specs=[row, row],
        out_shape=[jax.ShapeDtypeStruct((t, d), f32), jax.ShapeDtypeStruct((t, d), bf16)],
        compiler_params=_cp("parallel"),
    )(x, g)


def _norm_bwd(name, dh, x, g, dres, tm=512):
    t, d = x.shape

    def body(dh_ref, x_ref, g_ref, dres_ref, dx_ref, dg_ref):
        gv = g_ref[...]
        _, xn, r = _rms_fwd(x_ref[...], gv)
        dx, dg = _rms_bwd(dh_ref[...], xn, r, gv)
        dx_ref[...] = dres_ref[...] + dx

        @pl.when(pl.program_id(0) == 0)
        def _():
            dg_ref[...] = jnp.zeros_like(dg_ref)
        dg_ref[...] += dg

    row = pl.BlockSpec((tm, d), lambda i: (i, 0))
    vec = pl.BlockSpec((1, d), lambda i: (0, 0))
    return pl.pallas_call(
        body, name=name, grid=(t // tm,),
        in_specs=[row, row, vec, row],
        out_specs=[row, vec],
        out_shape=[jax.ShapeDtypeStruct((t, d), f32), jax.ShapeDtypeStruct((1, d), f32)],
        compiler_params=_cp("arbitrary"),
    )(dh, x, g, dres)


def _final_loss(x, g, target, tm=512):
    t, d = x.shape

    def body(x_ref, g_ref, t_ref, loss_ref, dx_ref, dg_ref):
        gv = g_ref[...]
        y, xn, r = _rms_fwd(x_ref[...], gv)
        err = y - t_ref[...]
        dy = err * (1.0 / d)
        dx, dg = _rms_bwd(dy, xn, r, gv)
        dx_ref[...] = dx

        @pl.when(pl.program_id(0) == 0)
        def _():
            dg_ref[...] = jnp.zeros_like(dg_ref)
            loss_ref[...] = jnp.zeros_like(loss_ref)
        dg_ref[...] += dg
        loss_ref[...] += jnp.full((1, 128), 0.5 / d, f32) * jnp.sum(err * err)

    row = pl.BlockSpec((tm, d), lambda i: (i, 0))
    vec = pl.BlockSpec((1, d), lambda i: (0, 0))
    return pl.pallas_call(
        body, name="final_loss", grid=(t // tm,),
        in_specs=[row, vec, row],
        out_specs=[pl.BlockSpec((1, 128), lambda i: (0, 0)), row, vec],
        out_shape=[jax.ShapeDtypeStruct((1, 128), f32), jax.ShapeDtypeStruct((t, d), f32),
                   jax.ShapeDtypeStruct((1, d), f32)],
        compiler_params=_cp("arbitrary"),
    )(x, g, target)


def _ffn_fwd(name, x, g, w1, w3, w2, tm=256):
    t, d = x.shape
    nj, _, ff = w1.shape

    def body(x_ref, g_ref, w1_ref, w3_ref, w2_ref, y_ref, a_ref, b_ref):
        xv = x_ref[...]
        h, _, _ = _rms_fwd(xv, g_ref[...])
        hb = h.astype(bf16)
        acc = jnp.zeros((tm, d), f32)
        for j in range(nj):
            a = _nn(hb, w1_ref[j])
            b = _nn(hb, w3_ref[j])
            a_ref[j] = a.astype(bf16)
            b_ref[j] = b.astype(bf16)
            u = (a * jax.nn.sigmoid(a) * b).astype(bf16)
            acc = acc + _nn(u, w2_ref[j])
        y_ref[...] = xv + 0.5 * acc

    row = pl.BlockSpec((tm, d), lambda i: (i, 0))
    mid = pl.BlockSpec((nj, tm, ff), lambda i: (0, i, 0))
    return pl.pallas_call(
        body, name=name, grid=(t // tm,),
        in_specs=[row, pl.BlockSpec((1, d), lambda i: (0, 0)), VMEM_SPEC, VMEM_SPEC, VMEM_SPEC],
        out_specs=[row, mid, mid],
        out_shape=[jax.ShapeDtypeStruct((t, d), f32), jax.ShapeDtypeStruct((nj, t, ff), bf16),
                   jax.ShapeDtypeStruct((nj, t, ff), bf16)],
        compiler_params=_cp("parallel"),
    )(x, g, w1, w3, w2)


def _ffn_dx(name, dy, x, g, w1, w3, w2, a, b, tm=256):
    t, d = x.shape
    nj, _, ff = w1.shape

    def body(dy_ref, x_ref, g_ref, w1_ref, w3_ref, w2_ref, a_ref, b_ref,
             dx_ref, dg_ref, hb_ref, dyh_ref, u_ref, da_ref, db_ref):
        gv = g_ref[...]
        h, xn, r = _rms_fwd(x_ref[...], gv)
        hb_ref[...] = h.astype(bf16)
        dyv = dy_ref[...]
        dyh = (0.5 * dyv).astype(bf16)
        dyh_ref[...] = dyh
        dh = jnp.zeros((tm, d), f32)
        for j in range(nj):
            av = a_ref[j].astype(f32)
            bv = b_ref[j].astype(f32)
            s = jax.nn.sigmoid(av)
            silu = av * s
            u_ref[j] = (silu * bv).astype(bf16)
            du = _nt(dyh, w2_ref[j])
            dab = (du * bv * (s * (1.0 + av * (1.0 - s)))).astype(bf16)
            dbb = (du * silu).astype(bf16)
            da_ref[j] = dab
            db_ref[j] = dbb
            dh = dh + _nt(dab, w1_ref[j]) + _nt(dbb, w3_ref[j])
        dx, dg = _rms_bwd(dh, xn, r, gv)
        dx_ref[...] = dyv + dx

        @pl.when(pl.program_id(0) == 0)
        def _():
            dg_ref[...] = jnp.zeros_like(dg_ref)
        dg_ref[...] += dg

    row = pl.BlockSpec((tm, d), lambda i: (i, 0))
    vec = pl.BlockSpec((1, d), lambda i: (0, 0))
    mid = pl.BlockSpec((nj, tm, ff), lambda i: (0, i, 0))
    mid_shape = jax.ShapeDtypeStruct((nj, t, ff), bf16)
    return pl.pallas_call(
        body, name=name, grid=(t // tm,),
        in_specs=[row, row, vec, VMEM_SPEC, VMEM_SPEC, VMEM_SPEC, mid, mid],
        out_specs=[row, vec, row, row, mid, mid, mid],
        out_shape=[jax.ShapeDtypeStruct((t, d), f32), jax.ShapeDtypeStruct((1, d), f32),
                   jax.ShapeDtypeStruct((t, d), bf16), jax.ShapeDtypeStruct((t, d), bf16),
                   mid_shape, mid_shape, mid_shape],
        compiler_params=_cp("arbitrary"),
    )(dy, x, g, w1, w3, w2, a, b)


def _ffn_dw(name, hb, dyh, u, da, db, tt=512):
    t, d = hb.shape
    nj, _, ff = u.shape

    def body(hb_ref, dyh_ref, u_ref, da_ref, db_ref, dw1_ref, dw3_ref, dw2_ref):
        @pl.when(pl.program_id(1) == 0)
        def _():
            dw1_ref[...] = jnp.zeros_like(dw1_ref)
            dw3_ref[...] = jnp.zeros_like(dw3_ref)
            dw2_ref[...] = jnp.zeros_like(dw2_ref)
        hv = hb_ref[...]
        dw1_ref[0] += _tn(hv, da_ref[0])
        dw3_ref[0] += _tn(hv, db_ref[0])
        dw2_ref[0] += _tn(u_ref[0], dyh_ref[...])

    row = pl.BlockSpec((tt, d), lambda j, s: (s, 0))
    mid = pl.BlockSpec((1, tt, ff), lambda j, s: (j, s, 0))
    return pl.pallas_call(
        body, name=name, grid=(nj, t // tt),
        in_specs=[row, row, mid, mid, mid],
        out_specs=[pl.BlockSpec((1, d, ff), lambda j, s: (j, 0, 0)), pl.BlockSpec((1, d, ff), lambda j, s: (j, 0, 0)),
                   pl.BlockSpec((1, ff, d), lambda j, s: (j, 0, 0))],
        out_shape=[jax.ShapeDtypeStruct((nj, d, ff), f32), jax.ShapeDtypeStruct((nj, d, ff), f32),
                   jax.ShapeDtypeStruct((nj, ff, d), f32)],
        compiler_params=_cp("parallel", "arbitrary"),
    )(hb, dyh, u, da, db)


_LOG_GAMMA = [float(np.log1p(-np.float32(2.0) ** np.float32(-5.0 - h))) for h in range(RET_HEADS)]


def _ret_consts(h):
    lg = jnp.where(h == 0, _LOG_GAMMA[0], jnp.where(h == 1, _LOG_GAMMA[1],
                   jnp.where(h == 2, _LOG_GAMMA[2], _LOG_GAMMA[3]))).astype(f32)
    c = RET_CHUNK
    r = lax.broadcasted_iota(jnp.int32, (c, c), 0)
    cc = lax.broadcasted_iota(jnp.int32, (c, c), 1)
    decay = jnp.where(r >= cc, jnp.exp(lg * jnp.maximum((r - cc).astype(f32), 0.0)), 0.0)
    pos = lax.broadcasted_iota(jnp.int32, (c, 1), 0).astype(f32)
    kd = jnp.exp(lg * (c - 1.0 - pos))
    qd = jnp.exp(lg * (pos + 1.0))
    gc = jnp.exp(lg * c)
    return decay, kd, qd, gc


def _rope(x, cos, sin):
    return x * cos + pltpu.roll(x, HEAD_DIM // 2, 1) * sin


def _rope_t(g, cos, sin):
    return g * cos + pltpu.roll(g * sin, HEAD_DIM // 2, 1)


def _rope_tables(s):
    half = HEAD_DIM // 2
    inv = ROPE_BASE ** (-jnp.arange(half, dtype=f32) / half)
    ang = jnp.arange(s, dtype=f32)[:, None] * inv[None, :]
    cos, sin = jnp.cos(ang), jnp.sin(ang)
    return jnp.concatenate([cos, cos], axis=1), jnp.concatenate([-sin, sin], axis=1)


def _head_ln(o):
    mu = jnp.mean(o, axis=-1, keepdims=True)
    oc = o - mu
    rs = lax.rsqrt(jnp.mean(oc * oc, axis=-1, keepdims=True) + EPS)
    return oc * rs, rs


def _ret_fwd(proj, cos, sin, ret_g, nb, s):
    c = RET_CHUNK
    nc = s // c
    t = nb * s
    scale = HEAD_DIM ** -0.5

    def body(q_ref, k_ref, v_ref, gate_ref, cos_ref, sin_ref, g_ref, o_ref, rprev_ref, m_ref, r_acc):
        n = pl.program_id(2)

        @pl.when(n == 0)
        def _():
            r_acc[...] = jnp.zeros_like(r_acc)
        decay, kd, qd, gc = _ret_consts(pl.program_id(1))
        cs, sn = cos_ref[...], sin_ref[...]
        q = _rope(q_ref[...], cs, sn)
        k = _rope(k_ref[...], cs, sn) * scale
        vb = v_ref[...].astype(bf16)
        sc = _nt(q.astype(bf16), k.astype(bf16)) * decay
        rv = r_acc[...]
        rprev_ref[0] = rv
        o = _nn(sc.astype(bf16), vb) + _nn((q * qd).astype(bf16), rv.astype(bf16))
        o_ref[...] = o
        r_acc[...] = rv * gc + _tn((k * kd).astype(bf16), vb)
        y, _ = _head_ln(o)
        gate = gate_ref[...]
        m_ref[...] = y * g_ref[...] * (gate * jax.nn.sigmoid(gate))

    def col(off):
        return pl.BlockSpec((c, HEAD_DIM), lambda b, h, n: (b * nc + n, off + h))

    tab = pl.BlockSpec((c, HEAD_DIM), lambda b, h, n: (n, 0))
    return pl.pallas_call(
        body, name="ret_fwd", grid=(nb, RET_HEADS, nc),
        in_specs=[col(0), col(4), col(8), col(12), tab, tab, pl.BlockSpec((1, HEAD_DIM), lambda b, h, n: (0, h))],
        out_specs=[col(0), pl.BlockSpec((1, HEAD_DIM, HEAD_DIM), lambda b, h, n: ((b * RET_HEADS + h) * nc + n, 0, 0)),
                   col(0)],
        out_shape=[jax.ShapeDtypeStruct((t, RET_WIDTH), f32),
                   jax.ShapeDtypeStruct((nb * RET_HEADS * nc, HEAD_DIM, HEAD_DIM), f32),
                   jax.ShapeDtypeStruct((t, RET_WIDTH), f32)],
        scratch_shapes=[pltpu.VMEM((HEAD_DIM, HEAD_DIM), f32)],
        compiler_params=_cp("parallel", "parallel", "arbitrary"),
    )(proj, proj, proj, proj, cos, sin, ret_g)


def _ret_bwd(dmerged, o_raw, rprev, proj, cos, sin, ret_g, nb, s):
    c = RET_CHUNK
    nc = s // c
    t = nb * s
    scale = HEAD_DIM ** -0.5

    def body(dm_ref, o_ref, rprev_ref, q_ref, k_ref, v_ref, gate_ref, cos_ref, sin_ref, g_ref,
             dq_ref, dk_ref, dv_ref, dgate_ref, dg_ref, dr_acc):
        b, n = pl.program_id(1), pl.program_id(2)

        @pl.when(n == 0)
        def _():
            dr_acc[...] = jnp.zeros_like(dr_acc)

        @pl.when((n == 0) & (b == 0))
        def _():
            dg_ref[...] = jnp.zeros_like(dg_ref)
        decay, kd, qd, gc = _ret_consts(pl.program_id(0))
        cs, sn = cos_ref[...], sin_ref[...]
        q = _rope(q_ref[...], cs, sn)
        k = _rope(k_ref[...], cs, sn) * scale
        qb, kb = q.astype(bf16), k.astype(bf16)
        vb = v_ref[...].astype(bf16)
        sc = _nt(qb, kb) * decay
        gv = g_ref[...]
        y, rs = _head_ln(o_ref[...])
        gate = gate_ref[...]
        sg = jax.nn.sigmoid(gate)
        silu = gate * sg
        dm = dm_ref[...]
        dgate_ref[...] = dm * y * gv * (sg * (1.0 + gate * (1.0 - sg)))
        dyl = dm * gv * silu
        dg_ref[...] += jnp.sum(dm * y * silu, axis=0, keepdims=True)
        do = rs * (dyl - jnp.mean(dyl, axis=-1, keepdims=True) - y * jnp.mean(dyl * y, axis=-1, keepdims=True))
        dob = do.astype(bf16)
        rv = rprev_ref[0]
        drn = dr_acc[...]
        drb = drn.astype(bf16)
        ds = (_nt(dob, vb) * decay).astype(bf16)
        kdb = (k * kd).astype(bf16)
        qdb = (q * qd).astype(bf16)
        dq_r = _nn(ds, kb) + _nt(dob, rv.astype(bf16)) * qd
        dk_r = _tn(ds, qb) + _nt(vb, drb) * kd
        dv_ref[...] = _tn(sc.astype(bf16), dob) + _nn(kdb, drb)
        dr_acc[...] = drn * gc + _tn(qdb, dob)
        dq_ref[...] = _rope_t(dq_r, cs, sn)
        dk_ref[...] = _rope_t(dk_r * scale, cs, sn)

    def col(off):
        return pl.BlockSpec((c, HEAD_DIM), lambda h, b, n: (b * nc + nc - 1 - n, off + h))

    tab = pl.BlockSpec((c, HEAD_DIM), lambda h, b, n: (nc - 1 - n, 0))
    gsp = pl.BlockSpec((1, HEAD_DIM), lambda h, b, n: (0, h))
    out_t = jax.ShapeDtypeStruct((t, RET_WIDTH), f32)
    return pl.pallas_call(
        body, name="ret_bwd", grid=(RET_HEADS, nb, nc),
        in_specs=[col(0), col(0),
                  pl.BlockSpec((1, HEAD_DIM, HEAD_DIM), lambda h, b, n: ((b * RET_HEADS + h) * nc + nc - 1 - n, 0, 0)),
                  col(0), col(4), col(8), col(12), tab, tab, gsp],
        out_specs=[col(0), col(0), col(0), col(0), gsp],
        out_shape=[out_t, out_t, out_t, out_t, jax.ShapeDtypeStruct((1, RET_WIDTH), f32)],
        scratch_shapes=[pltpu.VMEM((HEAD_DIM, HEAD_DIM), f32)],
        compiler_params=_cp("parallel", "arbitrary", "arbitrary"),
    )(dmerged, o_raw, rprev, proj, proj, proj, proj, cos, sin, ret_g)


def _neg_expm1(z):
    series = -(z * (1.0 + z * (0.5 + z * (1.0 / 6.0 + z * (1.0 / 24.0)))))
    return jnp.where(z > -0.01, series, 1.0 - jnp.exp(z))


def _lru_gates(xc, pa, pi, lam):
    r = jax.nn.sigmoid(pa)
    i = jax.nn.sigmoid(pi)
    log_a = -LRU_C * r * jax.nn.softplus(-lam)
    a = jnp.exp(log_a)
    bx = jnp.sqrt(_neg_expm1(2.0 * log_a)) * i * xc
    return a, bx


def _scan_fwd(a, b, row):
    d = 1
    while d < a.shape[0]:
        b = a * _shift_dn(b, d, row) + b
        a = a * _shift_dn(a, d, row, 1.0)
        d *= 2
    return b


def _scan_bwd(c, b, row):
    d = 1
    while d < c.shape[0]:
        b = c * _shift_up(b, d, row) + b
        c = c * _shift_up(c, d, row, 1.0)
        d *= 2
    return b


def _conv_fwd(x, cw, cb, row):
    return (cb + cw[3:4] * x + cw[2:3] * _shift_dn(x, 1, row) + cw[1:2] * _shift_dn(x, 2, row)
            + cw[0:1] * _shift_dn(x, 3, row))


def _lru_specs(s, order):
    def im(f):
        return (lambda b, g: f(b, g)) if order == "bg" else (lambda g, b: f(b, g))
    seq = lambda off: pl.BlockSpec((s, 128), im(lambda b, g: (b, off + g)))
    vec = pl.BlockSpec((1, 128), im(lambda b, g: (0, g)))
    cw = pl.BlockSpec((4, 128), im(lambda b, g: (0, g)))
    mat = pl.BlockSpec((1, 128, 128), im(lambda b, g: (g, 0, 0)))
    return seq, vec, cw, mat


def _lru_fwd(proj, conv_w, conv_b, w_a, b_a, w_i, b_i, lam, nb, s):
    def body(x_ref, gt_ref, cw_ref, cb_ref, wa_ref, ba_ref, wi_ref, bi_ref, lam_ref, out_ref):
        row = lax.broadcasted_iota(jnp.int32, (s, 128), 0)
        xc = _conv_fwd(x_ref[...], cw_ref[...], cb_ref[...], row)
        xcb = xc.astype(bf16)
        pa = _nn(xcb, wa_ref[0].astype(bf16)) + ba_ref[...]
        pi = _nn(xcb, wi_ref[0].astype(bf16)) + bi_ref[...]
        a, bx = _lru_gates(xc, pa, pi, lam_ref[...])
        h = _scan_fwd(a, bx, row)
        out_ref[...] = h * jax.nn.gelu(gt_ref[...])

    seq, vec, cw, mat = _lru_specs(s, "bg")
    return pl.pallas_call(
        body, name="lru_fwd", grid=(nb, LRU_BLOCKS),
        in_specs=[seq(16), seq(20), cw, vec, mat, vec, mat, vec, vec],
        out_specs=seq(0),
        out_shape=jax.ShapeDtypeStruct((nb * s, LRU_WIDTH), f32),
        compiler_params=_cp("parallel", "parallel"),
    )(proj, proj, conv_w, conv_b, w_a, b_a, w_i, b_i, lam)


def _lru_bwd(dmerged, proj, conv_w, conv_b, w_a, b_a, w_i, b_i, lam, nb, s):
    def body(dout_ref, x_ref, gt_ref, cw_ref, cb_ref, wa_ref, ba_ref, wi_ref, bi_ref, lam_ref,
             dx_ref, dgt_ref, dcw_ref, dcb_ref, dwa_ref, dba_ref, dwi_ref, dbi_ref, dlam_ref):
        row = lax.broadcasted_iota(jnp.int32, (s, 128), 0)
        x = x_ref[...]
        cwv = cw_ref[...]
        xc = _conv_fwd(x, cwv, cb_ref[...], row)
        xcb = xc.astype(bf16)
        wab, wib = wa_ref[0].astype(bf16), wi_ref[0].astype(bf16)
        pa = _nn(xcb, wab) + ba_ref[...]
        pi = _nn(xcb, wib) + bi_ref[...]
        (a, bx), gates_vjp = jax.vjp(_lru_gates, xc, pa, pi, lam_ref[...])
        h = _scan_fwd(a, bx, row)
        ge, gelu_vjp = jax.vjp(jax.nn.gelu, gt_ref[...])
        dout = dout_ref[...]
        dgt_ref[...] = gelu_vjp(dout * h)[0]
        adj = _scan_bwd(_shift_up(a, 1, row), dout * ge, row)
        dxc, dpa, dpi, dlam = gates_vjp((adj * _shift_dn(h, 1, row), adj))
        dpab, dpib = dpa.astype(bf16), dpi.astype(bf16)
        dxc = dxc + _nt(dpab, wab) + _nt(dpib, wib)
        dx_ref[...] = (cwv[3:4] * dxc + cwv[2:3] * _shift_up(dxc, 1, row) + cwv[1:2] * _shift_up(dxc, 2, row)
                       + cwv[0:1] * _shift_up(dxc, 3, row))

        @pl.when(pl.program_id(1) == 0)
        def _():
            for r in (dcw_ref, dcb_ref, dwa_ref, dba_ref, dwi_ref, dbi_ref, dlam_ref):
                r[...] = jnp.zeros_like(r)
        rsum = lambda v: jnp.sum(v, axis=0, keepdims=True)
        dcw_ref[...] += jnp.concatenate([rsum(dxc * _shift_dn(x, 3, row)), rsum(dxc * _shift_dn(x, 2, row)),
                                         rsum(dxc * _shift_dn(x, 1, row)), rsum(dxc * x)], axis=0)
        dcb_ref[...] += rsum(dxc)
        dwa_ref[0] += _tn(xcb, dpab)
        dwi_ref[0] += _tn(xcb, dpib)
        dba_ref[...] += rsum(dpa)
        dbi_ref[...] += rsum(dpi)
        dlam_ref[...] += dlam

    seq, vec, cw, mat = _lru_specs(s, "gb")
    t = nb * s
    vshape = jax.ShapeDtypeStruct((1, LRU_WIDTH), f32)
    mshape = jax.ShapeDtypeStruct((LRU_BLOCKS, 128, 128), f32)
    return pl.pallas_call(
        body, name="lru_bwd", grid=(LRU_BLOCKS, nb),
        in_specs=[seq(4), seq(16), seq(20), cw, vec, mat, vec, mat, vec, vec],
        out_specs=[seq(0), seq(0), cw, vec, mat, vec, mat, vec, vec],
        out_shape=[jax.ShapeDtypeStruct((t, LRU_WIDTH), f32), jax.ShapeDtypeStruct((t, LRU_WIDTH), f32),
                   jax.ShapeDtypeStruct((4, LRU_WIDTH), f32), vshape, mshape, vshape, mshape, vshape, vshape],
        compiler_params=_cp("parallel", "arbitrary"),
    )(dmerged, proj, proj, conv_w, conv_b, w_a, b_a, w_i, b_i, lam)


def _s5_disc(lr, li, ldt, bre, bim):
    dt = jnp.exp(ldt)
    mag = jnp.exp(lr * dt)
    lbr = mag * jnp.cos(li * dt)
    lbi = mag * jnp.sin(li * dt)
    den = lr * lr + li * li
    nr = lbr - 1.0
    fr = (nr * lr + lbi * li) / den
    fi = (lbi * lr - nr * li) / den
    bbr = fr[:, None, :] * bre - fi[:, None, :] * bim
    bbi = fr[:, None, :] * bim + fi[:, None, :] * bre
    return lbr, lbi, bbr, bbi


def _s5_prep(lr, li, ldt, bre, bim):
    def body(lr_ref, li_ref, ldt_ref, bre_ref, bim_ref, o1, o2, o3, o4):
        o1[...], o2[...], o3[...], o4[...] = _s5_disc(lr_ref[...], li_ref[...], ldt_ref[...], bre_ref[...], bim_ref[...])

    return pl.pallas_call(
        body, name="s5_prep", in_specs=[VMEM_SPEC] * 5, out_specs=[VMEM_SPEC] * 4,
        out_shape=[jax.ShapeDtypeStruct(lr.shape, f32), jax.ShapeDtypeStruct(lr.shape, f32),
                   jax.ShapeDtypeStruct(bre.shape, f32), jax.ShapeDtypeStruct(bre.shape, f32)],
    )(lr, li, ldt, bre, bim)


def _s5_prep_bwd(lr, li, ldt, bre, bim, cts):
    def body(lr_ref, li_ref, ldt_ref, bre_ref, bim_ref, g1, g2, g3, g4, o1, o2, o3, o4, o5):
        _, vjp = jax.vjp(_s5_disc, lr_ref[...], li_ref[...], ldt_ref[...], bre_ref[...], bim_ref[...])
        o1[...], o2[...], o3[...], o4[...], o5[...] = vjp((g1[...], g2[...], g3[...], g4[...]))

    return pl.pallas_call(
        body, name="s5_prep_bwd", in_specs=[VMEM_SPEC] * 9, out_specs=[VMEM_SPEC] * 5,
        out_shape=[jax.ShapeDtypeStruct(v.shape, f32) for v in (lr, li, ldt, bre, bim)],
    )(lr, li, ldt, bre, bim, *cts)


def _cmul(ar, ai, br, bi):
    return ar * br - ai * bi, ar * bi + ai * br


def _s5_pow_table(lr, li, n, row, up):
    ar = jnp.broadcast_to(lr, (n, lr.shape[1]))
    ai = jnp.broadcast_to(li, (n, li.shape[1]))
    shift = _shift_up if up else _shift_dn
    d = 1
    while d < n:
        ar, ai = _cmul(ar, ai, shift(ar, d, row, 1.0), shift(ai, d, row, 0.0))
        d *= 2
    return ar, ai


def _s5_local_scan(br, bi, lr, li, row, up):
    shift = _shift_up if up else _shift_dn
    pr, pi = lr, li
    d = 1
    while d < br.shape[0]:
        sr, si = _cmul(pr, pi, shift(br, d, row), shift(bi, d, row))
        br, bi = br + sr, bi + si
        pr, pi = _cmul(pr, pi, pr, pi)
        d *= 2
    return br, bi


def _s5_specs(s, nc, order):
    def im(f):
        return (lambda b, k: f(b, k)) if order == "bk" else (lambda k, b: f(b, k))
    seq = pl.BlockSpec((s, 128), im(lambda b, k: (b, k)))
    lvec = pl.BlockSpec((1, S5_BLOCK_STATES), im(lambda b, k: (0, k)))
    dvec = pl.BlockSpec((1, 128), im(lambda b, k: (0, k)))
    wmat = pl.BlockSpec((1, 128, S5_BLOCK_STATES), im(lambda b, k: (k, 0, 0)))
    h0 = pl.BlockSpec((1, nc, 2, S5_BLOCK_STATES), im(lambda b, k: (b * S5_BLOCKS + k, 0, 0, 0)))
    return seq, lvec, dvec, wmat, h0


def _s5_fwd(u, lbr, lbi, wbr, wbi, wcr, wci, dskip, nb, s):
    ln = S5_CHUNK
    nc = s // ln

    def body(u_ref, lr_ref, li_ref, wbr_ref, wbi_ref, wcr_ref, wci_ref, d_ref, yg_ref, y_ref, h0_ref):
        row = lax.broadcasted_iota(jnp.int32, (ln, S5_BLOCK_STATES), 0)
        lr, li = lr_ref[...], li_ref[...]
        pr, pi = _s5_pow_table(lr, li, ln, row, False)
        dv = d_ref[...]

        def step(n, carry):
            h0r, h0i = carry
            st = pl.multiple_of(n * ln, ln)
            uc = u_ref[pl.ds(st, ln), :]
            ub = uc.astype(bf16)
            br, bi = _s5_local_scan(_nn(ub, wbr_ref[0]), _nn(ub, wbi_ref[0]), lr, li, row, False)
            cr, ci = _cmul(pr, pi, h0r, h0i)
            hr, hi = br + cr, bi + ci
            h0_ref[0, n, 0:1, :] = h0r
            h0_ref[0, n, 1:2, :] = h0i
            y = _nt(hr.astype(bf16), wcr_ref[0]) - _nt(hi.astype(bf16), wci_ref[0]) + dv * uc
            y_ref[pl.ds(st, ln), :] = y
            yg_ref[pl.ds(st, ln), :] = jax.nn.gelu(y)
            return hr[ln - 1:ln, :], hi[ln - 1:ln, :]

        z = jnp.zeros((1, S5_BLOCK_STATES), f32)
        lax.fori_loop(0, nc, step, (z, z))

    seq, lvec, dvec, wmat, h0 = _s5_specs(s, nc, "bk")
    t = nb * s
    return pl.pallas_call(
        body, name="s5_fwd", grid=(nb, S5_BLOCKS),
        in_specs=[seq, lvec, lvec, wmat, wmat, wmat, wmat, dvec],
        out_specs=[seq, seq, h0],
        out_shape=[jax.ShapeDtypeStruct((t, D_MODEL), f32), jax.ShapeDtypeStruct((t, D_MODEL), f32),
                   jax.ShapeDtypeStruct((nb * S5_BLOCKS, nc, 2, S5_BLOCK_STATES), f32)],
        compiler_params=_cp("parallel", "parallel"),
    )(u, lbr, lbi, wbr, wbi, wcr, wci, dskip)


def _s5_bwd(dyg, y, u, h0, lbr, lbi, wbr, wbi, wcr, wci, dskip, nb, s):
    ln = S5_CHUNK
    nc = s // ln

    def body(dyg_ref, y_ref, u_ref, h0_ref, lr_ref, li_ref, wbr_ref, wbi_ref, wcr_ref, wci_ref, d_ref,
             du_ref, dlr_ref, dli_ref, dwbr_ref, dwbi_ref, dwcr_ref, dwci_ref, dd_ref):
        @pl.when(pl.program_id(1) == 0)
        def _():
            for r in (dlr_ref, dli_ref, dwbr_ref, dwbi_ref, dwcr_ref, dwci_ref, dd_ref):
                r[...] = jnp.zeros_like(r)
        row = lax.broadcasted_iota(jnp.int32, (ln, S5_BLOCK_STATES), 0)
        lr, li = lr_ref[...], li_ref[...]
        pr, pi = _s5_pow_table(lr, li, ln, row, False)
        qr, qi = _s5_pow_table(lr, -li, ln, row, True)
        dv = d_ref[...]
        rsum = lambda v: jnp.sum(v, axis=0, keepdims=True)

        def step(i, carry):
            gnr, gni = carry
            n = nc - 1 - i
            st = pl.multiple_of(n * ln, ln)
            uc = u_ref[pl.ds(st, ln), :]
            ub = uc.astype(bf16)
            h0v = h0_ref[0, n]
            h0r, h0i = h0v[0:1], h0v[1:2]
            br, bi = _s5_local_scan(_nn(ub, wbr_ref[0]), _nn(ub, wbi_ref[0]), lr, li, row, False)
            cr, ci = _cmul(pr, pi, h0r, h0i)
            hr, hi = br + cr, bi + ci
            dy = jax.vjp(jax.nn.gelu, y_ref[pl.ds(st, ln), :])[1](dyg_ref[pl.ds(st, ln), :])[0]
            dyb = dy.astype(bf16)
            dd_ref[...] += rsum(dy * uc)
            gr, gi = _s5_local_scan(_nn(dyb, wcr_ref[0]), -_nn(dyb, wci_ref[0]), lr, -li, row, True)
            cr, ci = _cmul(qr, qi, gnr, gni)
            gr, gi = gr + cr, gi + ci
            hpr = jnp.where(row >= 1, pltpu.roll(hr, 1, 0), h0r)
            hpi = jnp.where(row >= 1, pltpu.roll(hi, 1, 0), h0i)
            dlr_ref[...] += rsum(gr * hpr + gi * hpi)
            dli_ref[...] += rsum(gi * hpr - gr * hpi)
            grb, gib = gr.astype(bf16), gi.astype(bf16)
            dwbr_ref[0] += _tn(ub, grb)
            dwbi_ref[0] += _tn(ub, gib)
            dwcr_ref[0] += _tn(dyb, hr.astype(bf16))
            dwci_ref[0] -= _tn(dyb, hi.astype(bf16))
            du_ref[pl.ds(st, ln), :] = _nt(grb, wbr_ref[0]) + _nt(gib, wbi_ref[0]) + dv * dy
            return gr[0:1, :], gi[0:1, :]

        z = jnp.zeros((1, S5_BLOCK_STATES), f32)
        lax.fori_loop(0, nc, step, (z, z))

    seq, lvec, dvec, wmat, h0s = _s5_specs(s, nc, "kb")
    t = nb * s
    lshape = jax.ShapeDtypeStruct((1, S5_BLOCKS * S5_BLOCK_STATES), f32)
    wshape = jax.ShapeDtypeStruct((S5_BLOCKS, 128, S5_BLOCK_STATES), f32)
    return pl.pallas_call(
        body, name="s5_bwd", grid=(S5_BLOCKS, nb),
        in_specs=[seq, seq, seq, h0s, lvec, lvec, wmat, wmat, wmat, wmat, dvec],
        out_specs=[seq, lvec, lvec, wmat, wmat, wmat, wmat, dvec],
        out_shape=[jax.ShapeDtypeStruct((t, D_MODEL), f32), lshape, lshape, wshape, wshape, wshape, wshape,
                   jax.ShapeDtypeStruct((1, D_MODEL), f32)],
        compiler_params=_cp("parallel", "arbitrary"),
    )(dyg, y, u, h0, lbr, lbi, wbr, wbi, wcr, wci, dskip)


def _blockdiag(w):
    w4 = w.reshape(S5_BLOCKS, 8, S5_GROUP, S5_STATE)
    same_group = jnp.eye(8, dtype=bool)[None, :, None, :, None]
    return jnp.where(same_group, w4[:, :, :, None, :], 0.0).reshape(S5_BLOCKS, 128, S5_BLOCK_STATES)


def _blockdiag_t(dw):
    d5 = dw.reshape(S5_BLOCKS, 8, S5_GROUP, 8, S5_STATE)
    diag = jnp.diagonal(d5, axis1=1, axis2=3)
    return jnp.moveaxis(diag, 3, 1).reshape(S5_GROUPS, S5_GROUP, S5_STATE)


def _glu_fwd(ygb, wa, wb, x, tm=512, tn=512):
    t, d = x.shape

    def body(y_ref, wa_ref, wb_ref, x_ref, o_ref, p_ref, q_ref):
        p = _nn(y_ref[...], wa_ref[...])
        q = _nn(y_ref[...], wb_ref[...])
        p_ref[...] = p
        q_ref[...] = q
        o_ref[...] = x_ref[...] + p * jax.nn.sigmoid(q)

    tile = pl.BlockSpec((tm, tn), lambda i, j: (i, j))
    wsp = pl.BlockSpec((d, tn), lambda i, j: (0, j))
    out = jax.ShapeDtypeStruct((t, d), f32)
    return pl.pallas_call(
        body, name="glu_fwd", grid=(t // tm, d // tn),
        in_specs=[pl.BlockSpec((tm, d), lambda i, j: (i, 0)), wsp, wsp, tile],
        out_specs=[tile, tile, tile], out_shape=[out, out, out],
        compiler_params=_cp("parallel", "parallel"),
    )(ygb, wa, wb, x)


def _place():
    x, y, c = lax.axis_index("x"), lax.axis_index("y"), lax.axis_index("c")
    return x, y, c, [(1 - x, y), (x, 1 - y), (1 - x, 1 - y)]


def _all_gather(name, arrays):
    n = len(arrays)

    def body(*refs):
        ins, outs = refs[:n], refs[n:2 * n]
        send_sems, recv_sems, local_sems = refs[2 * n:]
        x, y, c, chips = _place()
        me, sib = (x, y, c), (x, y, 1 - c)

        def copy(i, k, block, to, src=None):
            dst = outs[i].at[4 * block[0] + 2 * block[1] + block[2]]
            return pltpu.make_async_remote_copy(
                src_ref=dst if src is None else src, dst_ref=dst,
                send_sem=send_sems.at[i * 7 + k], recv_sem=recv_sems.at[i * 7 + k],
                device_id=to, device_id_type=MESH)

        mine = [pltpu.make_async_copy(ins[i], outs[i].at[4 * x + 2 * y + c], local_sems.at[i]) for i in range(n)]
        for m in mine:
            m.start()
        first = []
        for i in range(n):
            first.append(copy(i, 0, me, sib, src=ins[i]))
            first += [copy(i, 1 + j, me, (*chip, c), src=ins[i]) for j, chip in enumerate(chips)]
        for cp in first:
            cp.start()
        passed = []
        for j, chip in enumerate(chips):
            for i in range(n):
                copy(i, 1 + j, (*chip, c), me).wait_recv()
                fwd = copy(i, 4 + j, (*chip, c), sib)
                fwd.start()
                passed.append(fwd)
        for i in range(n):
            copy(i, 0, sib, me).wait_recv()
        for j, chip in enumerate(chips):
            for i in range(n):
                copy(i, 4 + j, (*chip, 1 - c), me).wait_recv()
        for cp in first + passed:
            cp.wait_send()
        for m in mine:
            m.wait()

    return pl.pallas_call(
        body, name=name,
        in_specs=[ANY_SPEC] * n, out_specs=[ANY_SPEC] * n,
        out_shape=[jax.ShapeDtypeStruct((N_DEV,) + a.shape, a.dtype) for a in arrays],
        scratch_shapes=[pltpu.SemaphoreType.DMA((7 * n,)), pltpu.SemaphoreType.DMA((7 * n,)),
                        pltpu.SemaphoreType.DMA((n,))],
    )(*arrays)


def _rs_sibling(name, arrays):
    n = len(arrays)

    def body(*refs):
        ins, outs = refs[:n], refs[n:2 * n]
        send_sems, recv_sems = refs[2 * n:]
        x, y, c, _ = _place()
        copies = []
        for i in range(n):
            for q in range(4):
                copies.append(pltpu.make_async_remote_copy(
                    src_ref=ins[i].at[2 * q + 1 - c], dst_ref=outs[i].at[q],
                    send_sem=send_sems.at[i * 4 + q], recv_sem=recv_sems.at[i * 4 + q],
                    device_id=(x, y, 1 - c), device_id_type=MESH))
        for cp in copies:
            cp.start()
        for cp in copies:
            cp.wait_recv()
        for cp in copies:
            cp.wait_send()

    return pl.pallas_call(
        body, name=name,
        in_specs=[ANY_SPEC] * n, out_specs=[ANY_SPEC] * n,
        out_shape=[jax.ShapeDtypeStruct((4,) + a.shape[1:], a.dtype) for a in arrays],
        scratch_shapes=[pltpu.SemaphoreType.DMA((4 * n,)), pltpu.SemaphoreType.DMA((4 * n,))],
    )(*arrays)


def _rs_chips(name, arrays):
    n = len(arrays)

    def body(*refs):
        ins, outs = refs[:n], refs[n:2 * n]
        send_sems, recv_sems = refs[2 * n:]
        x, y, c, chips = _place()
        copies = []
        for i in range(n):
            for j, chip in enumerate(chips):
                copies.append(pltpu.make_async_remote_copy(
                    src_ref=ins[i].at[2 * chip[0] + chip[1]], dst_ref=outs[i].at[j],
                    send_sem=send_sems.at[i * 3 + j], recv_sem=recv_sems.at[i * 3 + j],
                    device_id=(*chip, c), device_id_type=MESH))
        for cp in copies:
            cp.start()
        for cp in copies:
            cp.wait_recv()
        for cp in copies:
            cp.wait_send()

    return pl.pallas_call(
        body, name=name,
        in_specs=[ANY_SPEC] * n, out_specs=[ANY_SPEC] * n,
        out_shape=[jax.ShapeDtypeStruct((3,) + a.shape[1:], a.dtype) for a in arrays],
        scratch_shapes=[pltpu.SemaphoreType.DMA((3 * n,)), pltpu.SemaphoreType.DMA((3 * n,))],
    )(*arrays)


def _rows(a):
    return a.reshape(-1, a.shape[-1])


def _row_tile(r):
    for tm in (512, 256, 128, 64, 32, 16, 8):
        if r % tm == 0:
            return tm
    return r


def _add2(name, a, b):
    a2, b2 = _rows(a), _rows(b)
    out = _ew(name, lambda u, v: (u.astype(f32) + v.astype(f32),), [a2, b2], [(a2.shape[1], f32)],
              tm=_row_tile(a2.shape[0]))[0]
    return out.reshape(a.shape)


def _sum8(name, gathered):
    _, r, n = gathered.shape
    tm = _row_tile(r)

    def body(g_ref, o_ref):
        acc = g_ref[0]
        for k in range(1, N_DEV):
            acc = acc + g_ref[k]
        o_ref[...] = acc

    return pl.pallas_call(
        body, name=name, grid=(r // tm,),
        in_specs=[pl.BlockSpec((N_DEV, tm, n), lambda i: (0, i, 0))],
        out_specs=pl.BlockSpec((tm, n), lambda i: (i, 0)),
        out_shape=jax.ShapeDtypeStruct((r, n), f32),
        compiler_params=_cp("parallel"),
    )(gathered)


def _adamw(name, w, m, v, gparts):
    shape = w.shape
    w2, m2, v2 = _rows(w), _rows(m), _rows(v)
    parts = [_rows(p) for p in gparts]
    c1 = 1.0 - ADAM_B1 ** ADAM_STEP
    c2 = 1.0 - ADAM_B2 ** ADAM_STEP

    def fn(wv, mv, vv, *gs):
        g = gs[0].astype(f32)
        for p in gs[1:]:
            g = g + p.astype(f32)
        mn = ADAM_B1 * mv + (1.0 - ADAM_B1) * g
        vn = ADAM_B2 * vv + (1.0 - ADAM_B2) * (g * g)
        delta = -ADAM_LR * ((mn / c1) / (jnp.sqrt(vn / c2) + ADAM_EPS) + ADAM_WD * wv)
        return g, delta, mn, vn

    n = w2.shape[1]
    outs = _ew(name, fn, [w2, m2, v2] + parts, [(n, f32)] * 4, tm=_row_tile(w2.shape[0]))
    return [o.reshape(shape) for o in outs]


def _pack(arrays):
    flat = jnp.concatenate([a.reshape(-1).astype(f32) for a in arrays])
    pad = (-flat.shape[0]) % 1024
    return jnp.pad(flat, (0, pad)).reshape(-1, 128)


def _unpack(packed, shapes):
    flat = packed.reshape(-1)
    out, off = [], 0
    for s in shapes:
        n = math.prod(s)
        out.append(flat[off:off + n].reshape(s))
        off += n
    return out


def _local_step(x, target, w, nb, s):
    cos, sin = _rope_tables(s)
    g = {}
    ffn_saved = {}

    def ffn(xin, l, h):
        y, a, b = _ffn_fwd(f"ffn_fwd_{l}{h}", xin, w["ffn_g"][l][h], w["ffn_w1"][l][h], w["ffn_w3"][l][h], w["ffn_w2"][l][h])
        ffn_saved[(l, h)] = (xin, a, b)
        return y

    def ffn_back(dy, l, h):
        xin, a, b = ffn_saved[(l, h)]
        dx, dg, hb, dyh, u, da, db = _ffn_dx(f"ffn_dx_{l}{h}", dy, xin, w["ffn_g"][l][h], w["ffn_w1"][l][h],
                                            w["ffn_w3"][l][h], w["ffn_w2"][l][h], a, b)
        g[f"ffn_w1_{l}{h}"], g[f"ffn_w3_{l}{h}"], g[f"ffn_w2_{l}{h}"] = _ffn_dw(f"ffn_dw_{l}{h}", hb, dyh, u, da, db)
        g[f"ffn_g_{l}{h}"] = dg
        return dx

    x1 = ffn(x, 0, 0)
    _, h0b = _norm_fwd("mix_norm_0", x1, w["mix_g"][0])
    proj = _mm("in_proj", h0b, w["w_in"], "nn", tn=768)[0]
    o_raw, rprev, mret = _ret_fwd(proj, cos, sin, w["ret_g"], nb, s)
    lru = _lru_fwd(proj, w["conv_w"], w["conv_b"], w["lru_w_a"], w["lru_b_a"], w["lru_w_i"], w["lru_b_i"], w["lru_lam"], nb, s)
    merged = _ew("merge", lambda a, b: (jnp.concatenate([a, b], axis=1),), [mret, lru], [(D_MODEL, bf16)])[0]
    x2 = _mm("out_proj", merged, w["w_out"], "nn", extras=[x1], epilogue=lambda acc, r: (acc + r,))[0]
    x3 = ffn(x2, 0, 1)
    x4 = ffn(x3, 1, 0)
    u, _ = _norm_fwd("mix_norm_1", x4, w["mix_g"][1])
    lbr, lbi, bbr, bbi = _s5_prep(w["s5_lr"], w["s5_li"], w["s5_ldt"], w["s5_bre"], w["s5_bim"])
    lbr_f, lbi_f = lbr.reshape(1, -1), lbi.reshape(1, -1)
    wbr, wbi = _blockdiag(bbr).astype(bf16), _blockdiag(bbi).astype(bf16)
    wcr, wci = _blockdiag(w["s5_cre"]).astype(bf16), _blockdiag(w["s5_cim"]).astype(bf16)
    yg, ypre, h0s = _s5_fwd(u, lbr_f, lbi_f, wbr, wbi, wcr, wci, w["s5_d"], nb, s)
    ygb = yg.astype(bf16)
    x5, gp, gq = _glu_fwd(ygb, w["glu_a"], w["glu_b"], x4)
    x6 = ffn(x5, 1, 1)
    loss, dx6, g["final_g"] = _final_loss(x6, w["final_g"], target)

    dx5 = ffn_back(dx6, 1, 1)

    def glu_bwd(d, p, q):
        sg = jax.nn.sigmoid(q)
        return d * sg, d * p * sg * (1.0 - sg)

    dp, dq = _ew("glu_bwd", glu_bwd, [dx5, gp, gq], [(D_MODEL, bf16), (D_MODEL, bf16)])
    dyg = _mm("glu_dy_a", dp, w["glu_a"], "nt")[0]
    dyg = _mm("glu_dy_b", dq, w["glu_b"], "nt", extras=[dyg], epilogue=lambda acc, r: (acc + r,))[0]
    g["glu_a"] = _mm_tn("glu_dw_a", ygb, dp)
    g["glu_b"] = _mm_tn("glu_dw_b", ygb, dq)
    du, dlr, dli, dwbr, dwbi, dwcr, dwci, g["s5_d"] = _s5_bwd(dyg, ypre, u, h0s, lbr_f, lbi_f, wbr, wbi, wcr, wci, w["s5_d"], nb, s)
    g["s5_cre"], g["s5_cim"] = _blockdiag_t(dwcr), _blockdiag_t(dwci)
    g["s5_lr"], g["s5_li"], g["s5_ldt"], g["s5_bre"], g["s5_bim"] = _s5_prep_bwd(
        w["s5_lr"], w["s5_li"], w["s5_ldt"], w["s5_bre"], w["s5_bim"],
        (dlr.reshape(S5_GROUPS, S5_STATE), dli.reshape(S5_GROUPS, S5_STATE), _blockdiag_t(dwbr), _blockdiag_t(dwbi)))
    dx4, g["mix_g_1"] = _norm_bwd("mix_norm_1_bwd", du, x4, w["mix_g"][1], dx5)
    dx3 = ffn_back(dx4, 1, 0)
    dx2 = ffn_back(dx3, 0, 1)
    dx2b = dx2.astype(bf16)
    dmerged = _mm("out_proj_dx", dx2b, w["w_out"], "nt")[0]
    g["w_out"] = _mm_tn("out_proj_dw", merged, dx2b)
    dq_, dk_, dv_, dgate, g["ret_g"] = _ret_bwd(dmerged, o_raw, rprev, proj, cos, sin, w["ret_g"], nb, s)
    (dxl, dgl, g["conv_w"], g["conv_b"], g["lru_w_a"], g["lru_b_a"], g["lru_w_i"], g["lru_b_i"], g["lru_lam"]) = _lru_bwd(
        dmerged, proj, w["conv_w"], w["conv_b"], w["lru_w_a"], w["lru_b_a"], w["lru_w_i"], w["lru_b_i"], w["lru_lam"], nb, s)
    dproj = _ew("dproj", lambda *p: (jnp.concatenate(p, axis=1),), [dq_, dk_, dv_, dgate, dxl, dgl], [(3072, bf16)])[0]
    dh0 = _mm("in_proj_dx", dproj, w["w_in"], "nt")[0]
    g["w_in"] = _mm_tn("in_proj_dw", h0b, dproj)
    dx1, g["mix_g_0"] = _norm_bwd("mix_norm_0_bwd", dh0, x1, w["mix_g"][0], dx2)
    dx0 = ffn_back(dx1, 0, 0)
    return loss, dx0, g


_WEIGHTS = ["ffn_norm_g", "ffn_w1", "ffn_w3", "ffn_w2", "mix_norm_g", "w_in_even", "w_out_even", "ret_norm_g", "conv_w",
            "conv_b", "lru_w_a", "lru_b_a", "lru_w_i", "lru_b_i", "lru_lambda", "s5_lambda_re", "s5_lambda_im", "s5_log_dt",
            "s5_b_re", "s5_b_im", "s5_c_re", "s5_c_im", "s5_d", "glu_w_a", "glu_w_b", "final_norm_g"]
_BIG = ["ffn_w1", "ffn_w3", "ffn_w2", "w_in_even", "w_out_even", "glu_w_a", "glu_w_b"]
_SMALL_SHARDED = ["ffn_norm_g", "conv_w", "s5_d"]
_SMALL = [n for n in _WEIGHTS if n not in _BIG]


def kernel(x, ffn_norm_g, ffn_w1, ffn_w3, ffn_w2, mix_norm_g, w_in_even, w_out_even, ret_norm_g, conv_w, conv_b, lru_w_a, lru_b_a, lru_w_i, lru_b_i, lru_lambda, s5_lambda_re, s5_lambda_im, s5_log_dt, s5_b_re, s5_b_im, s5_c_re, s5_c_im, s5_d, glu_w_a, glu_w_b, final_norm_g, loss_target, m_ffn_norm_g, m_ffn_w1, m_ffn_w3, m_ffn_w2, m_mix_norm_g, m_w_in_even, m_w_out_even, m_ret_norm_g, m_conv_w, m_conv_b, m_lru_w_a, m_lru_b_a, m_lru_w_i, m_lru_b_i, m_lru_lambda, m_s5_lambda_re, m_s5_lambda_im, m_s5_log_dt, m_s5_b_re, m_s5_b_im, m_s5_c_re, m_s5_c_im, m_s5_d, m_glu_w_a, m_glu_w_b, m_final_norm_g, v_ffn_norm_g, v_ffn_w1, v_ffn_w3, v_ffn_w2, v_mix_norm_g, v_w_in_even, v_w_out_even, v_ret_norm_g, v_conv_w, v_conv_b, v_lru_w_a, v_lru_b_a, v_lru_w_i, v_lru_b_i, v_lru_lambda, v_s5_lambda_re, v_s5_lambda_im, v_s5_log_dt, v_s5_b_re, v_s5_b_im, v_s5_c_re, v_s5_c_im, v_s5_d, v_glu_w_a, v_glu_w_b, v_final_norm_g):
    a = dict(locals())
    nb, s, d = x.shape
    ax, ay, ac = lax.axis_index("x"), lax.axis_index("y"), lax.axis_index("c")
    dev = 4 * ax + 2 * ay + ac
    chip = 2 * ax + ay

    (sm,) = _all_gather("ag_small_weights", [_pack([ffn_norm_g, conv_w, s5_d])])
    sm = sm.reshape(N_DEV, -1)
    ffn_g_full = jnp.transpose(sm[:, :512].reshape(N_DEV, 2, 2, 128), (1, 2, 0, 3)).reshape(2, 2, D_MODEL)
    conv_w_full = jnp.transpose(sm[:, 512:768].reshape(N_DEV, 4, 64), (1, 0, 2)).reshape(4, LRU_WIDTH)
    s5_d_full = sm[:, 768:896].reshape(1, D_MODEL)
    gw1, gw3, gw2, gwin, gwout, gglua, gglub = _all_gather(
        "ag_weights", [ffn_w1.astype(bf16), ffn_w3.astype(bf16), ffn_w2.astype(bf16), w_in_even[0].astype(bf16),
                       w_out_even[0].astype(bf16), glu_w_a[0].astype(bf16), glu_w_b[0].astype(bf16)])
    w = {
        "ffn_g": [[ffn_g_full[l, h].reshape(1, D_MODEL) for h in range(2)] for l in range(2)],
        "ffn_w1": [[gw1[:, l, h] for h in range(2)] for l in range(2)],
        "ffn_w3": [[gw3[:, l, h] for h in range(2)] for l in range(2)],
        "ffn_w2": [[gw2[:, l, h] for h in range(2)] for l in range(2)],
        "mix_g": [mix_norm_g[0:1], mix_norm_g[1:2]],
        "w_in": jnp.transpose(gwin, (1, 0, 2)).reshape(D_MODEL, N_DEV * IN_SHARD),
        "w_out": gwout.reshape(D_MODEL, D_MODEL),
        "ret_g": ret_norm_g, "conv_w": conv_w_full, "conv_b": conv_b,
        "lru_w_a": lru_w_a[0], "lru_b_a": lru_b_a, "lru_w_i": lru_w_i[0], "lru_b_i": lru_b_i, "lru_lam": lru_lambda,
        "s5_lr": s5_lambda_re[0], "s5_li": s5_lambda_im[0], "s5_ldt": s5_log_dt.reshape(S5_GROUPS, 1),
        "s5_bre": jnp.swapaxes(s5_b_re[0], 1, 2), "s5_bim": jnp.swapaxes(s5_b_im[0], 1, 2),
        "s5_cre": s5_c_re[0], "s5_cim": s5_c_im[0], "s5_d": s5_d_full,
        "glu_a": gglua.reshape(D_MODEL, D_MODEL), "glu_b": gglub.reshape(D_MODEL, D_MODEL),
        "final_g": final_norm_g.reshape(1, D_MODEL),
    }

    loss_part, dx, g = _local_step(x.reshape(nb * s, d), loss_target.reshape(nb * s, d), w, nb, s)
    loss = lax.psum(loss_part[0, 0], ("x", "y", "c"))

    part = {
        "ffn_norm_g": jnp.stack([jnp.stack([g[f"ffn_g_{l}{h}"][0] for h in range(2)]) for l in range(2)]),
        "mix_norm_g": jnp.concatenate([g["mix_g_0"], g["mix_g_1"]], axis=0),
        "ret_norm_g": g["ret_g"], "conv_w": g["conv_w"][None], "conv_b": g["conv_b"],
        "lru_w_a": g["lru_w_a"][None], "lru_b_a": g["lru_b_a"], "lru_w_i": g["lru_w_i"][None], "lru_b_i": g["lru_b_i"],
        "lru_lambda": g["lru_lam"], "s5_lambda_re": g["s5_lr"][None], "s5_lambda_im": g["s5_li"][None],
        "s5_log_dt": g["s5_ldt"].reshape(1, S5_GROUPS),
        "s5_b_re": jnp.swapaxes(g["s5_bre"], 1, 2)[None], "s5_b_im": jnp.swapaxes(g["s5_bim"], 1, 2)[None],
        "s5_c_re": g["s5_cre"][None], "s5_c_im": g["s5_cim"][None], "s5_d": g["s5_d"], "final_norm_g": g["final_g"][0],
    }
    (gath,) = _all_gather("ag_small_grads", [_pack([part[n] for n in _SMALL])])
    full = dict(zip(_SMALL, _unpack(_sum8("sum_small_grads", gath), [part[n].shape for n in _SMALL])))
    for n in _SMALL_SHARDED:
        width = a[n].shape[-1]
        full[n] = lax.dynamic_slice_in_dim(full[n], dev * width, width, axis=full[n].ndim - 1)
    shapes = [a[n].shape for n in _SMALL]
    packed = _adamw("adamw_small", _pack([a[n] for n in _SMALL]), _pack([a["m_" + n] for n in _SMALL]),
                    _pack([a["v_" + n] for n in _SMALL]), [_pack([full[n] for n in _SMALL])])
    res = {n: vals for n, vals in zip(_SMALL, zip(*[_unpack(p, shapes) for p in packed]))}

    def lh(name):
        return jnp.stack([jnp.stack([g[f"{name}_{l}{h}"] for h in range(2)], axis=1) for l in range(2)], axis=1)

    grads = [lh("ffn_w1"), lh("ffn_w3"), lh("ffn_w2"),
             jnp.transpose(g["w_in"].reshape(D_MODEL, N_DEV, IN_SHARD), (1, 0, 2)),
             g["w_out"].reshape(N_DEV, D_MODEL // N_DEV, D_MODEL), g["glu_a"].reshape(N_DEV, D_MODEL // N_DEV, D_MODEL),
             g["glu_b"].reshape(N_DEV, D_MODEL // N_DEV, D_MODEL)]
    from_sibling = _rs_sibling("rs_sibling", [t.astype(bf16) for t in grads])
    chip_sums = []
    for n, t, r in zip(_BIG, grads, from_sibling):
        mine = lax.dynamic_index_in_dim(t.reshape((4, 2) + t.shape[1:]), ac, axis=1, keepdims=False)
        chip_sums.append(_add2("chip_sum_" + n, mine, r))
    from_chips = _rs_chips("rs_chips", [t.astype(bf16) for t in chip_sums])
    for n, t, r in zip(_BIG, chip_sums, from_chips):
        own = lax.dynamic_index_in_dim(t, chip, axis=0, keepdims=False).reshape(a[n].shape)
        parts = [own] + [r[j].reshape(a[n].shape) for j in range(3)]
        res[n] = _adamw("adamw_" + n, a[n], a["m_" + n], a["v_" + n], parts)

    out = [loss, dx.reshape(nb, s, d)]
    for k in range(4):
        out += [res[n][k] for n in _WEIGHTS]
    return tuple(out)
```

```python
import functools
import math

import numpy as np
import jax
import jax.numpy as jnp
from jax import lax
from jax.experimental import pallas as pl
from jax.experimental.pallas import tpu as pltpu

f32 = jnp.float32
bf16 = jnp.bfloat16

D_MODEL = 1024
N_DEV = 8
EPS = 1e-6
RET_HEADS = 4
HEAD_DIM = 128
RET_WIDTH = 512
RET_CHUNK = 128
ROPE_BASE = 10000.0
LRU_WIDTH = 512
LRU_BLOCKS = 4
LRU_C = 8.0
S5_GROUP = 16
S5_GROUPS = 64
S5_STATE = 64
S5_CHUNK = 128
S5_BLOCKS = 8
S5_BLOCK_STATES = 512
D_FF = 2816
FF_SHARD = D_FF // N_DEV
IN_SHARD = 3072 // N_DEV
ADAM_LR = 0.001
ADAM_B1 = 0.9
ADAM_B2 = 0.999
ADAM_EPS = 1e-08
ADAM_WD = 0.01
ADAM_STEP = 10

VMEM_LIMIT = 48 * 1024 * 1024
VMEM_SPEC = pl.BlockSpec(memory_space=pltpu.VMEM)
ANY_SPEC = pl.BlockSpec(memory_space=pl.ANY)
MESH = pl.DeviceIdType.MESH


def _cp(*sem):
    return pltpu.CompilerParams(dimension_semantics=sem, vmem_limit_bytes=VMEM_LIMIT)


def _nn(a, b):
    return jnp.dot(a, b, preferred_element_type=f32)


def _nt(a, b):
    return lax.dot_general(a, b, (((1,), (1,)), ((), ())), preferred_element_type=f32)


def _tn(a, b):
    return lax.dot_general(a, b, (((0,), (0,)), ((), ())), preferred_element_type=f32)


def _rms_fwd(x, g):
    r = lax.rsqrt(jnp.mean(x * x, axis=-1, keepdims=True) + EPS)
    xn = x * r
    return xn * g, xn, r


def _rms_bwd(dh, xn, r, g):
    dxn = dh * g
    dx = r * (dxn - xn * jnp.mean(dxn * xn, axis=-1, keepdims=True))
    dg = jnp.sum(dh * xn, axis=0, keepdims=True)
    return dx, dg


def _shift_dn(v, d, row, fill=0.0):
    return jnp.where(row >= d, pltpu.roll(v, d, 0), fill)


def _shift_up(v, d, row, fill=0.0):
    n = v.shape[0]
    return jnp.where(row < n - d, pltpu.roll(v, n - d, 0), fill)


def _ew(name, fn, ins, outs, tm=512):
    t = ins[0].shape[0]
    n_in = len(ins)

    def body(*refs):
        res = fn(*[r[...] for r in refs[:n_in]])
        for o, v in zip(refs[n_in:], res):
            o[...] = v.astype(o.dtype)

    return pl.pallas_call(
        body, name=name, grid=(t // tm,),
        in_specs=[pl.BlockSpec((tm, a.shape[1]), lambda i: (i, 0)) for a in ins],
        out_specs=[pl.BlockSpec((tm, n), lambda i: (i, 0)) for n, _ in outs],
        out_shape=[jax.ShapeDtypeStruct((t, n), dt) for n, dt in outs],
        compiler_params=_cp("parallel"),
    )(*ins)


def _mm(name, x, w, kind, extras=(), epilogue=None, outs=None, tm=512, tn=512):
    t = x.shape[0]
    n = w.shape[1] if kind == "nn" else w.shape[0]
    tn = min(tn, n)
    outs = outs or [f32]
    n_ex = len(extras)

    def body(x_ref, w_ref, *refs):
        xb = x_ref[...].astype(bf16)
        acc = _nn(xb, w_ref[...]) if kind == "nn" else _nt(xb, w_ref[...])
        res = epilogue(acc, *[r[...] for r in refs[:n_ex]]) if epilogue else (acc,)
        for o, v in zip(refs[n_ex:], res):
            o[...] = v.astype(o.dtype)

    w_spec = (pl.BlockSpec((w.shape[0], tn), lambda i, j: (0, j)) if kind == "nn"
              else pl.BlockSpec((tn, w.shape[1]), lambda i, j: (j, 0)))
    tile = pl.BlockSpec((tm, tn), lambda i, j: (i, j))
    return pl.pallas_call(
        body, name=name, grid=(t // tm, n // tn),
        in_specs=[pl.BlockSpec((tm, x.shape[1]), lambda i, j: (i, 0)), w_spec] + [tile] * n_ex,
        out_specs=[tile] * len(outs),
        out_shape=[jax.ShapeDtypeStruct((t, n), dt) for dt in outs],
        compiler_params=_cp("parallel", "parallel"),
    )(x, w, *extras)


def _mm_tn(name, x, y, tk=1024, tn=512, tt=512):
    t, k = x.shape
    n = y.shape[1]
    tk, tn = min(tk, k), min(tn, n)

    def body(x_ref, y_ref, o_ref, ob_ref):
        @pl.when(pl.program_id(2) == 0)
        def _():
            o_ref[...] = jnp.zeros_like(o_ref)
        o_ref[...] += _tn(x_ref[...].astype(bf16), y_ref[...].astype(bf16))

        @pl.when(pl.program_id(2) == pl.num_programs(2) - 1)
        def _():
            ob_ref[...] = o_ref[...].astype(bf16)

    out = pl.BlockSpec((tk, tn), lambda i, j, s: (i, j))
    return pl.pallas_call(
        body, name=name, grid=(k // tk, n // tn, t // tt),
        in_specs=[pl.BlockSpec((tt, tk), lambda i, j, s: (s, i)), pl.BlockSpec((tt, tn), lambda i, j, s: (s, j))],
        out_specs=[out, out],
        out_shape=[jax.ShapeDtypeStruct((k, n), f32), jax.ShapeDtypeStruct((k, n), bf16)],
        compiler_params=_cp("parallel", "parallel", "arbitrary"),
    )(x, y)


def _norm_fwd(name, x, g, tm=512):
    t, d = x.shape

    def body(x_ref, g_ref, h_ref, hb_ref):
        h, _, _ = _rms_fwd(x_ref[...], g_ref[...])
        h_ref[...] = h
        hb_ref[...] = h.astype(bf16)

    row = pl.BlockSpec((tm, d), lambda i: (i, 0))
    return pl.pallas_call(
        body, name=name, grid=(t // tm,),
        in_specs=[row, pl.BlockSpec((1, d), lambda i: (0, 0))],
        out_specs=[row, row],
        out_shape=[jax.ShapeDtypeStruct((t, d), f32), jax.ShapeDtypeStruct((t, d), bf16)],
        compiler_params=_cp("parallel"),
    )(x, g)


def _norm_bwd(name, dh, x, g, dres, tm=512):
    t, d = x.shape

    def body(dh_ref, x_ref, g_ref, dres_ref, dx_ref, dg_ref):
        gv = g_ref[...]
        _, xn, r = _rms_fwd(x_ref[...], gv)
        dx, dg = _rms_bwd(dh_ref[...], xn, r, gv)
        dx_ref[...] = dres_ref[...] + dx

        @pl.when(pl.program_id(0) == 0)
        def _():
            dg_ref[...] = jnp.zeros_like(dg_ref)
        dg_ref[...] += dg

    row = pl.BlockSpec((tm, d), lambda i: (i, 0))
    vec = pl.BlockSpec((1, d), lambda i: (0, 0))
    return pl.pallas_call(
        body, name=name, grid=(t // tm,),
        in_specs=[row, row, vec, row],
        out_specs=[row, vec],
        out_shape=[jax.ShapeDtypeStruct((t, d), f32), jax.ShapeDtypeStruct((1, d), f32)],
        compiler_params=_cp("arbitrary"),
    )(dh, x, g, dres)


def _final_loss(x, g, target, tm=512):
    t, d = x.shape

    def body(x_ref, g_ref, t_ref, loss_ref, dx_ref, dg_ref):
        gv = g_ref[...]
        y, xn, r = _rms_fwd(x_ref[...], gv)
        err = y - t_ref[...]
        dy = err * (1.0 / d)
        dx, dg = _rms_bwd(dy, xn, r, gv)
        dx_ref[...] = dx

        @pl.when(pl.program_id(0) == 0)
        def _():
            dg_ref[...] = jnp.zeros_like(dg_ref)
            loss_ref[...] = jnp.zeros_like(loss_ref)
        dg_ref[...] += dg
        loss_ref[...] += jnp.full((1, 128), 0.5 / d, f32) * jnp.sum(err * err)

    row = pl.BlockSpec((tm, d), lambda i: (i, 0))
    vec = pl.BlockSpec((1, d), lambda i: (0, 0))
    return pl.pallas_call(
        body, name="final_loss", grid=(t // tm,),
        in_specs=[row, vec, row],
        out_specs=[pl.BlockSpec((1, 128), lambda i: (0, 0)), row, vec],
        out_shape=[jax.ShapeDtypeStruct((1, 128), f32), jax.ShapeDtypeStruct((t, d), f32),
                   jax.ShapeDtypeStruct((1, d), f32)],
        compiler_params=_cp("arbitrary"),
    )(x, g, target)


def _load_ffn_weights(hbm_refs, vmem_refs, sems, l, h):
    @pl.when(pl.program_id(0) == 0)
    def _():
        copies = [pltpu.make_async_copy(src.at[:, l, h], dst, sems.at[k])
                  for k, (src, dst) in enumerate(zip(hbm_refs, vmem_refs))]
        for cp in copies:
            cp.start()
        for cp in copies:
            cp.wait()


def _ffn_weight_scratch(nj, d, ff):
    return [pltpu.VMEM((nj, d, ff), bf16), pltpu.VMEM((nj, d, ff), bf16), pltpu.VMEM((nj, ff, d), bf16),
            pltpu.SemaphoreType.DMA((3,))]


def _ffn_fwd(name, x, g, w1, w3, w2, layer, half, tm=256):
    t, d = x.shape
    nj, ff = w1.shape[0], w1.shape[-1]

    def body(x_ref, g_ref, w1_hbm, w3_hbm, w2_hbm, y_ref, a_ref, b_ref, w1_ref, w3_ref, w2_ref, sems):
        _load_ffn_weights((w1_hbm, w3_hbm, w2_hbm), (w1_ref, w3_ref, w2_ref), sems, layer, half)
        xv = x_ref[...]
        h, _, _ = _rms_fwd(xv, g_ref[...])
        hb = h.astype(bf16)
        acc = jnp.zeros((tm, d), f32)
        for j in range(nj):
            a = _nn(hb, w1_ref[j])
            b = _nn(hb, w3_ref[j])
            a_ref[j] = a.astype(bf16)
            b_ref[j] = b.astype(bf16)
            u = (a * jax.nn.sigmoid(a) * b).astype(bf16)
            acc = acc + _nn(u, w2_ref[j])
        y_ref[...] = xv + 0.5 * acc

    row = pl.BlockSpec((tm, d), lambda i: (i, 0))
    mid = pl.BlockSpec((nj, tm, ff), lambda i: (0, i, 0))
    return pl.pallas_call(
        body, name=name, grid=(t // tm,),
        in_specs=[row, pl.BlockSpec((1, d), lambda i: (0, 0)), ANY_SPEC, ANY_SPEC, ANY_SPEC],
        out_specs=[row, mid, mid],
        out_shape=[jax.ShapeDtypeStruct((t, d), f32), jax.ShapeDtypeStruct((nj, t, ff), bf16),
                   jax.ShapeDtypeStruct((nj, t, ff), bf16)],
        scratch_shapes=_ffn_weight_scratch(nj, d, ff),
        compiler_params=_cp("arbitrary"),
    )(x, g, w1, w3, w2)


def _ffn_dx(name, dy, x, g, w1, w3, w2, a, b, layer, half, tm=256):
    t, d = x.shape
    nj, ff = w1.shape[0], w1.shape[-1]

    def body(dy_ref, x_ref, g_ref, w1_hbm, w3_hbm, w2_hbm, a_ref, b_ref,
             dx_ref, dg_ref, hb_ref, dyh_ref, u_ref, da_ref, db_ref, w1_ref, w3_ref, w2_ref, sems):
        _load_ffn_weights((w1_hbm, w3_hbm, w2_hbm), (w1_ref, w3_ref, w2_ref), sems, layer, half)
        gv = g_ref[...]
        h, xn, r = _rms_fwd(x_ref[...], gv)
        hb_ref[...] = h.astype(bf16)
        dyv = dy_ref[...]
        dyh = (0.5 * dyv).astype(bf16)
        dyh_ref[...] = dyh
        dh = jnp.zeros((tm, d), f32)
        for j in range(nj):
            av = a_ref[j].astype(f32)
            bv = b_ref[j].astype(f32)
            s = jax.nn.sigmoid(av)
            silu = av * s
            u_ref[j] = (silu * bv).astype(bf16)
            du = _nt(dyh, w2_ref[j])
            dab = (du * bv * (s * (1.0 + av * (1.0 - s)))).astype(bf16)
            dbb = (du * silu).astype(bf16)
            da_ref[j] = dab
            db_ref[j] = dbb
            dh = dh + _nt(dab, w1_ref[j]) + _nt(dbb, w3_ref[j])
        dx, dg = _rms_bwd(dh, xn, r, gv)
        dx_ref[...] = dyv + dx

        @pl.when(pl.program_id(0) == 0)
        def _():
            dg_ref[...] = jnp.zeros_like(dg_ref)
        dg_ref[...] += dg

    row = pl.BlockSpec((tm, d), lambda i: (i, 0))
    vec = pl.BlockSpec((1, d), lambda i: (0, 0))
    mid = pl.BlockSpec((nj, tm, ff), lambda i: (0, i, 0))
    mid_shape = jax.ShapeDtypeStruct((nj, t, ff), bf16)
    return pl.pallas_call(
        body, name=name, grid=(t // tm,),
        in_specs=[row, row, vec, ANY_SPEC, ANY_SPEC, ANY_SPEC, mid, mid],
        out_specs=[row, vec, row, row, mid, mid, mid],
        out_shape=[jax.ShapeDtypeStruct((t, d), f32), jax.ShapeDtypeStruct((1, d), f32),
                   jax.ShapeDtypeStruct((t, d), bf16), jax.ShapeDtypeStruct((t, d), bf16),
                   mid_shape, mid_shape, mid_shape],
        scratch_shapes=_ffn_weight_scratch(nj, d, ff),
        compiler_params=_cp("arbitrary"),
    )(dy, x, g, w1, w3, w2, a, b)


def _ffn_dw(name, hb, dyh, u, da, db, bufs, l, h, tt=512):
    t, d = hb.shape
    nj, _, ff = u.shape

    def body(hb_ref, dyh_ref, u_ref, da_ref, db_ref, *refs):
        dw1_ref, dw3_ref, dw2_ref, dw1b_ref, dw3b_ref, dw2b_ref = refs[6:]

        @pl.when(pl.program_id(1) == 0)
        def _():
            dw1_ref[...] = jnp.zeros_like(dw1_ref)
            dw3_ref[...] = jnp.zeros_like(dw3_ref)
            dw2_ref[...] = jnp.zeros_like(dw2_ref)
        hv = hb_ref[...]
        dw1_ref[0] += _tn(hv, da_ref[0])
        dw3_ref[0] += _tn(hv, db_ref[0])
        dw2_ref[0] += _tn(u_ref[0], dyh_ref[...])

        @pl.when(pl.program_id(1) == pl.num_programs(1) - 1)
        def _():
            dw1b_ref[...] = dw1_ref[...].astype(bf16)
            dw3b_ref[...] = dw3_ref[...].astype(bf16)
            dw2b_ref[...] = dw2_ref[...].astype(bf16)

    row = pl.BlockSpec((tt, d), lambda j, s: (s, 0))
    mid = pl.BlockSpec((1, tt, ff), lambda j, s: (j, s, 0))
    slab_in = pl.BlockSpec((1, None, None, d, ff), lambda j, s: (j, l, h, 0, 0))
    slab_out = pl.BlockSpec((1, None, None, ff, d), lambda j, s: (j, l, h, 0, 0))
    slabs = [slab_in, slab_in, slab_out] * 2
    return pl.pallas_call(
        body, name=name, grid=(nj, t // tt),
        in_specs=[row, row, mid, mid, mid] + [ANY_SPEC] * 6,
        out_specs=slabs,
        out_shape=[jax.ShapeDtypeStruct(b.shape, b.dtype) for b in bufs],
        input_output_aliases={5 + k: k for k in range(6)},
        compiler_params=_cp("parallel", "arbitrary"),
    )(hb, dyh, u, da, db, *bufs)


_LOG_GAMMA = [float(np.log1p(-np.float32(2.0) ** np.float32(-5.0 - h))) for h in range(RET_HEADS)]


def _ret_consts(h):
    lg = jnp.where(h == 0, _LOG_GAMMA[0], jnp.where(h == 1, _LOG_GAMMA[1],
                   jnp.where(h == 2, _LOG_GAMMA[2], _LOG_GAMMA[3]))).astype(f32)
    c = RET_CHUNK
    r = lax.broadcasted_iota(jnp.int32, (c, c), 0)
    cc = lax.broadcasted_iota(jnp.int32, (c, c), 1)
    decay = jnp.where(r >= cc, jnp.exp(lg * jnp.maximum((r - cc).astype(f32), 0.0)), 0.0)
    pos = lax.broadcasted_iota(jnp.int32, (c, 1), 0).astype(f32)
    kd = jnp.exp(lg * (c - 1.0 - pos))
    qd = jnp.exp(lg * (pos + 1.0))
    gc = jnp.exp(lg * c)
    return decay, kd, qd, gc


def _rope(x, cos, sin):
    return x * cos + pltpu.roll(x, HEAD_DIM // 2, 1) * sin


def _rope_t(g, cos, sin):
    return g * cos + pltpu.roll(g * sin, HEAD_DIM // 2, 1)


def _rope_tables(s):
    half = HEAD_DIM // 2
    inv = ROPE_BASE ** (-jnp.arange(half, dtype=f32) / half)
    ang = jnp.arange(s, dtype=f32)[:, None] * inv[None, :]
    cos, sin = jnp.cos(ang), jnp.sin(ang)
    return jnp.concatenate([cos, cos], axis=1), jnp.concatenate([-sin, sin], axis=1)


def _head_ln(o):
    mu = jnp.mean(o, axis=-1, keepdims=True)
    oc = o - mu
    rs = lax.rsqrt(jnp.mean(oc * oc, axis=-1, keepdims=True) + EPS)
    return oc * rs, rs


def _ret_fwd(proj, cos, sin, ret_g, nb, s):
    c = RET_CHUNK
    nc = s // c
    t = nb * s
    scale = HEAD_DIM ** -0.5

    def body(q_ref, k_ref, v_ref, gate_ref, cos_ref, sin_ref, g_ref, o_ref, rprev_ref, m_ref, r_acc):
        n = pl.program_id(2)

        @pl.when(n == 0)
        def _():
            r_acc[...] = jnp.zeros_like(r_acc)
        decay, kd, qd, gc = _ret_consts(pl.program_id(1))
        cs, sn = cos_ref[...], sin_ref[...]
        q = _rope(q_ref[...], cs, sn)
        k = _rope(k_ref[...], cs, sn) * scale
        vb = v_ref[...].astype(bf16)
        sc = _nt(q.astype(bf16), k.astype(bf16)) * decay
        rv = r_acc[...]
        rprev_ref[0] = rv
        o = _nn(sc.astype(bf16), vb) + _nn((q * qd).astype(bf16), rv.astype(bf16))
        o_ref[...] = o
        r_acc[...] = rv * gc + _tn((k * kd).astype(bf16), vb)
        y, _ = _head_ln(o)
        gate = gate_ref[...]
        m_ref[...] = y * g_ref[...] * (gate * jax.nn.sigmoid(gate))

    def col(off):
        return pl.BlockSpec((c, HEAD_DIM), lambda b, h, n: (b * nc + n, off + h))

    tab = pl.BlockSpec((c, HEAD_DIM), lambda b, h, n: (n, 0))
    return pl.pallas_call(
        body, name="ret_fwd", grid=(nb, RET_HEADS, nc),
        in_specs=[col(0), col(4), col(8), col(12), tab, tab, pl.BlockSpec((1, HEAD_DIM), lambda b, h, n: (0, h))],
        out_specs=[col(0), pl.BlockSpec((1, HEAD_DIM, HEAD_DIM), lambda b, h, n: ((b * RET_HEADS + h) * nc + n, 0, 0)),
                   col(0)],
        out_shape=[jax.ShapeDtypeStruct((t, RET_WIDTH), f32),
                   jax.ShapeDtypeStruct((nb * RET_HEADS * nc, HEAD_DIM, HEAD_DIM), f32),
                   jax.ShapeDtypeStruct((t, RET_WIDTH), f32)],
        scratch_shapes=[pltpu.VMEM((HEAD_DIM, HEAD_DIM), f32)],
        compiler_params=_cp("parallel", "parallel", "arbitrary"),
    )(proj, proj, proj, proj, cos, sin, ret_g)


def _ret_bwd(dmerged, o_raw, rprev, proj, cos, sin, ret_g, nb, s):
    c = RET_CHUNK
    nc = s // c
    t = nb * s
    scale = HEAD_DIM ** -0.5

    def body(dm_ref, o_ref, rprev_ref, q_ref, k_ref, v_ref, gate_ref, cos_ref, sin_ref, g_ref,
             dq_ref, dk_ref, dv_ref, dgate_ref, dg_ref, dr_acc):
        b, n = pl.program_id(1), pl.program_id(2)

        @pl.when(n == 0)
        def _():
            dr_acc[...] = jnp.zeros_like(dr_acc)

        @pl.when((n == 0) & (b == 0))
        def _():
            dg_ref[...] = jnp.zeros_like(dg_ref)
        decay, kd, qd, gc = _ret_consts(pl.program_id(0))
        cs, sn = cos_ref[...], sin_ref[...]
        q = _rope(q_ref[...], cs, sn)
        k = _rope(k_ref[...], cs, sn) * scale
        qb, kb = q.astype(bf16), k.astype(bf16)
        vb = v_ref[...].astype(bf16)
        sc = _nt(qb, kb) * decay
        gv = g_ref[...]
        y, rs = _head_ln(o_ref[...])
        gate = gate_ref[...]
        sg = jax.nn.sigmoid(gate)
        silu = gate * sg
        dm = dm_ref[...]
        dgate_ref[...] = dm * y * gv * (sg * (1.0 + gate * (1.0 - sg)))
        dyl = dm * gv * silu
        dg_ref[...] += jnp.sum(dm * y * silu, axis=0, keepdims=True)
        do = rs * (dyl - jnp.mean(dyl, axis=-1, keepdims=True) - y * jnp.mean(dyl * y, axis=-1, keepdims=True))
        dob = do.astype(bf16)
        rv = rprev_ref[0]
        drn = dr_acc[...]
        drb = drn.astype(bf16)
        ds = (_nt(dob, vb) * decay).astype(bf16)
        kdb = (k * kd).astype(bf16)
        qdb = (q * qd).astype(bf16)
        dq_r = _nn(ds, kb) + _nt(dob, rv.astype(bf16)) * qd
        dk_r = _tn(ds, qb) + _nt(vb, drb) * kd
        dv_ref[...] = _tn(sc.astype(bf16), dob) + _nn(kdb, drb)
        dr_acc[...] = drn * gc + _tn(qdb, dob)
        dq_ref[...] = _rope_t(dq_r, cs, sn)
        dk_ref[...] = _rope_t(dk_r * scale, cs, sn)

    def col(off):
        return pl.BlockSpec((c, HEAD_DIM), lambda h, b, n: (b * nc + nc - 1 - n, off + h))

    tab = pl.BlockSpec((c, HEAD_DIM), lambda h, b, n: (nc - 1 - n, 0))
    gsp = pl.BlockSpec((1, HEAD_DIM), lambda h, b, n: (0, h))
    out_t = jax.ShapeDtypeStruct((t, RET_WIDTH), f32)
    return pl.pallas_call(
        body, name="ret_bwd", grid=(RET_HEADS, nb, nc),
        in_specs=[col(0), col(0),
                  pl.BlockSpec((1, HEAD_DIM, HEAD_DIM), lambda h, b, n: ((b * RET_HEADS + h) * nc + nc - 1 - n, 0, 0)),
                  col(0), col(4), col(8), col(12), tab, tab, gsp],
        out_specs=[col(0), col(0), col(0), col(0), gsp],
        out_shape=[out_t, out_t, out_t, out_t, jax.ShapeDtypeStruct((1, RET_WIDTH), f32)],
        scratch_shapes=[pltpu.VMEM((HEAD_DIM, HEAD_DIM), f32)],
        compiler_params=_cp("parallel", "arbitrary", "arbitrary"),
    )(dmerged, o_raw, rprev, proj, proj, proj, proj, cos, sin, ret_g)


def _neg_expm1(z):
    series = -(z * (1.0 + z * (0.5 + z * (1.0 / 6.0 + z * (1.0 / 24.0)))))
    return jnp.where(z > -0.01, series, 1.0 - jnp.exp(z))


def _lru_gates(xc, pa, pi, lam):
    r = jax.nn.sigmoid(pa)
    i = jax.nn.sigmoid(pi)
    log_a = -LRU_C * r * jax.nn.softplus(-lam)
    a = jnp.exp(log_a)
    bx = jnp.sqrt(_neg_expm1(2.0 * log_a)) * i * xc
    return a, bx


def _scan_fwd(a, b, row):
    d = 1
    while d < a.shape[0]:
        b = a * _shift_dn(b, d, row) + b
        a = a * _shift_dn(a, d, row, 1.0)
        d *= 2
    return b


def _scan_bwd(c, b, row):
    d = 1
    while d < c.shape[0]:
        b = c * _shift_up(b, d, row) + b
        c = c * _shift_up(c, d, row, 1.0)
        d *= 2
    return b


def _conv_fwd(x, cw, cb, row):
    return (cb + cw[3:4] * x + cw[2:3] * _shift_dn(x, 1, row) + cw[1:2] * _shift_dn(x, 2, row)
            + cw[0:1] * _shift_dn(x, 3, row))


def _lru_specs(s, order):
    def im(f):
        return (lambda b, g: f(b, g)) if order == "bg" else (lambda g, b: f(b, g))
    seq = lambda off: pl.BlockSpec((s, 128), im(lambda b, g: (b, off + g)))
    vec = pl.BlockSpec((1, 128), im(lambda b, g: (0, g)))
    cw = pl.BlockSpec((4, 128), im(lambda b, g: (0, g)))
    mat = pl.BlockSpec((1, 128, 128), im(lambda b, g: (g, 0, 0)))
    return seq, vec, cw, mat


def _lru_fwd(proj, conv_w, conv_b, w_a, b_a, w_i, b_i, lam, nb, s):
    def body(x_ref, gt_ref, cw_ref, cb_ref, wa_ref, ba_ref, wi_ref, bi_ref, lam_ref, out_ref):
        row = lax.broadcasted_iota(jnp.int32, (s, 128), 0)
        xc = _conv_fwd(x_ref[...], cw_ref[...], cb_ref[...], row)
        xcb = xc.astype(bf16)
        pa = _nn(xcb, wa_ref[0].astype(bf16)) + ba_ref[...]
        pi = _nn(xcb, wi_ref[0].astype(bf16)) + bi_ref[...]
        a, bx = _lru_gates(xc, pa, pi, lam_ref[...])
        h = _scan_fwd(a, bx, row)
        out_ref[...] = h * jax.nn.gelu(gt_ref[...])

    seq, vec, cw, mat = _lru_specs(s, "bg")
    return pl.pallas_call(
        body, name="lru_fwd", grid=(nb, LRU_BLOCKS),
        in_specs=[seq(16), seq(20), cw, vec, mat, vec, mat, vec, vec],
        out_specs=seq(0),
        out_shape=jax.ShapeDtypeStruct((nb * s, LRU_WIDTH), f32),
        compiler_params=_cp("parallel", "parallel"),
    )(proj, proj, conv_w, conv_b, w_a, b_a, w_i, b_i, lam)


def _lru_bwd(dmerged, proj, conv_w, conv_b, w_a, b_a, w_i, b_i, lam, nb, s):
    def body(dout_ref, x_ref, gt_ref, cw_ref, cb_ref, wa_ref, ba_ref, wi_ref, bi_ref, lam_ref,
             dx_ref, dgt_ref, dcw_ref, dcb_ref, dwa_ref, dba_ref, dwi_ref, dbi_ref, dlam_ref):
        row = lax.broadcasted_iota(jnp.int32, (s, 128), 0)
        x = x_ref[...]
        cwv = cw_ref[...]
        xc = _conv_fwd(x, cwv, cb_ref[...], row)
        xcb = xc.astype(bf16)
        wab, wib = wa_ref[0].astype(bf16), wi_ref[0].astype(bf16)
        pa = _nn(xcb, wab) + ba_ref[...]
        pi = _nn(xcb, wib) + bi_ref[...]
        (a, bx), gates_vjp = jax.vjp(_lru_gates, xc, pa, pi, lam_ref[...])
        h = _scan_fwd(a, bx, row)
        ge, gelu_vjp = jax.vjp(jax.nn.gelu, gt_ref[...])
        dout = dout_ref[...]
        dgt_ref[...] = gelu_vjp(dout * h)[0]
        adj = _scan_bwd(_shift_up(a, 1, row), dout * ge, row)
        dxc, dpa, dpi, dlam = gates_vjp((adj * _shift_dn(h, 1, row), adj))
        dpab, dpib = dpa.astype(bf16), dpi.astype(bf16)
        dxc = dxc + _nt(dpab, wab) + _nt(dpib, wib)
        dx_ref[...] = (cwv[3:4] * dxc + cwv[2:3] * _shift_up(dxc, 1, row) + cwv[1:2] * _shift_up(dxc, 2, row)
                       + cwv[0:1] * _shift_up(dxc, 3, row))

        @pl.when(pl.program_id(1) == 0)
        def _():
            for r in (dcw_ref, dcb_ref, dwa_ref, dba_ref, dwi_ref, dbi_ref, dlam_ref):
                r[...] = jnp.zeros_like(r)
        rsum = lambda v: jnp.sum(v, axis=0, keepdims=True)
        dcw_ref[...] += jnp.concatenate([rsum(dxc * _shift_dn(x, 3, row)), rsum(dxc * _shift_dn(x, 2, row)),
                                         rsum(dxc * _shift_dn(x, 1, row)), rsum(dxc * x)], axis=0)
        dcb_ref[...] += rsum(dxc)
        dwa_ref[0] += _tn(xcb, dpab)
        dwi_ref[0] += _tn(xcb, dpib)
        dba_ref[...] += rsum(dpa)
        dbi_ref[...] += rsum(dpi)
        dlam_ref[...] += dlam

    seq, vec, cw, mat = _lru_specs(s, "gb")
    t = nb * s
    vshape = jax.ShapeDtypeStruct((1, LRU_WIDTH), f32)
    mshape = jax.ShapeDtypeStruct((LRU_BLOCKS, 128, 128), f32)
    return pl.pallas_call(
        body, name="lru_bwd", grid=(LRU_BLOCKS, nb),
        in_specs=[seq(4), seq(16), seq(20), cw, vec, mat, vec, mat, vec, vec],
        out_specs=[seq(0), seq(0), cw, vec, mat, vec, mat, vec, vec],
        out_shape=[jax.ShapeDtypeStruct((t, LRU_WIDTH), f32), jax.ShapeDtypeStruct((t, LRU_WIDTH), f32),
                   jax.ShapeDtypeStruct((4, LRU_WIDTH), f32), vshape, mshape, vshape, mshape, vshape, vshape],
        compiler_params=_cp("parallel", "arbitrary"),
    )(dmerged, proj, proj, conv_w, conv_b, w_a, b_a, w_i, b_i, lam)


def _s5_disc(lr, li, ldt, bre, bim):
    dt = jnp.exp(ldt)
    mag = jnp.exp(lr * dt)
    lbr = mag * jnp.cos(li * dt)
    lbi = mag * jnp.sin(li * dt)
    den = lr * lr + li * li
    nr = lbr - 1.0
    fr = (nr * lr + lbi * li) / den
    fi = (lbi * lr - nr * li) / den
    bbr = fr[:, None, :] * bre - fi[:, None, :] * bim
    bbi = fr[:, None, :] * bim + fi[:, None, :] * bre
    return lbr, lbi, bbr, bbi


def _s5_prep(lr, li, ldt, bre, bim):
    def body(lr_ref, li_ref, ldt_ref, bre_ref, bim_ref, o1, o2, o3, o4):
        o1[...], o2[...], o3[...], o4[...] = _s5_disc(lr_ref[...], li_ref[...], ldt_ref[...], bre_ref[...], bim_ref[...])

    return pl.pallas_call(
        body, name="s5_prep", in_specs=[VMEM_SPEC] * 5, out_specs=[VMEM_SPEC] * 4,
        out_shape=[jax.ShapeDtypeStruct(lr.shape, f32), jax.ShapeDtypeStruct(lr.shape, f32),
                   jax.ShapeDtypeStruct(bre.shape, f32), jax.ShapeDtypeStruct(bre.shape, f32)],
    )(lr, li, ldt, bre, bim)


def _s5_prep_bwd(lr, li, ldt, bre, bim, cts):
    def body(lr_ref, li_ref, ldt_ref, bre_ref, bim_ref, g1, g2, g3, g4, o1, o2, o3, o4, o5):
        _, vjp = jax.vjp(_s5_disc, lr_ref[...], li_ref[...], ldt_ref[...], bre_ref[...], bim_ref[...])
        o1[...], o2[...], o3[...], o4[...], o5[...] = vjp((g1[...], g2[...], g3[...], g4[...]))

    return pl.pallas_call(
        body, name="s5_prep_bwd", in_specs=[VMEM_SPEC] * 9, out_specs=[VMEM_SPEC] * 5,
        out_shape=[jax.ShapeDtypeStruct(v.shape, f32) for v in (lr, li, ldt, bre, bim)],
    )(lr, li, ldt, bre, bim, *cts)


def _cmul(ar, ai, br, bi):
    return ar * br - ai * bi, ar * bi + ai * br


def _s5_pow_table(lr, li, n, row, up):
    ar = jnp.broadcast_to(lr, (n, lr.shape[1]))
    ai = jnp.broadcast_to(li, (n, li.shape[1]))
    shift = _shift_up if up else _shift_dn
    d = 1
    while d < n:
        ar, ai = _cmul(ar, ai, shift(ar, d, row, 1.0), shift(ai, d, row, 0.0))
        d *= 2
    return ar, ai


def _s5_local_scan(br, bi, lr, li, row, up):
    shift = _shift_up if up else _shift_dn
    pr, pi = lr, li
    d = 1
    while d < br.shape[0]:
        sr, si = _cmul(pr, pi, shift(br, d, row), shift(bi, d, row))
        br, bi = br + sr, bi + si
        pr, pi = _cmul(pr, pi, pr, pi)
        d *= 2
    return br, bi


def _s5_specs(s, nc, order):
    def im(f):
        return (lambda b, k: f(b, k)) if order == "bk" else (lambda k, b: f(b, k))
    seq = pl.BlockSpec((s, 128), im(lambda b, k: (b, k)))
    lvec = pl.BlockSpec((1, S5_BLOCK_STATES), im(lambda b, k: (0, k)))
    dvec = pl.BlockSpec((1, 128), im(lambda b, k: (0, k)))
    wmat = pl.BlockSpec((1, 128, S5_BLOCK_STATES), im(lambda b, k: (k, 0, 0)))
    h0 = pl.BlockSpec((1, nc, 2, S5_BLOCK_STATES), im(lambda b, k: (b * S5_BLOCKS + k, 0, 0, 0)))
    return seq, lvec, dvec, wmat, h0


def _s5_fwd(u, lbr, lbi, wbr, wbi, wcr, wci, dskip, nb, s):
    ln = S5_CHUNK
    nc = s // ln

    def body(u_ref, lr_ref, li_ref, wbr_ref, wbi_ref, wcr_ref, wci_ref, d_ref, yg_ref, y_ref, h0_ref):
        row = lax.broadcasted_iota(jnp.int32, (ln, S5_BLOCK_STATES), 0)
        lr, li = lr_ref[...], li_ref[...]
        pr, pi = _s5_pow_table(lr, li, ln, row, False)
        dv = d_ref[...]

        def step(n, carry):
            h0r, h0i = carry
            st = pl.multiple_of(n * ln, ln)
            uc = u_ref[pl.ds(st, ln), :]
            ub = uc.astype(bf16)
            br, bi = _s5_local_scan(_nn(ub, wbr_ref[0]), _nn(ub, wbi_ref[0]), lr, li, row, False)
            cr, ci = _cmul(pr, pi, h0r, h0i)
            hr, hi = br + cr, bi + ci
            h0_ref[0, n, 0:1, :] = h0r
            h0_ref[0, n, 1:2, :] = h0i
            y = _nt(hr.astype(bf16), wcr_ref[0]) - _nt(hi.astype(bf16), wci_ref[0]) + dv * uc
            y_ref[pl.ds(st, ln), :] = y
            yg_ref[pl.ds(st, ln), :] = jax.nn.gelu(y).astype(bf16)
            return hr[ln - 1:ln, :], hi[ln - 1:ln, :]

        z = jnp.zeros((1, S5_BLOCK_STATES), f32)
        lax.fori_loop(0, nc, step, (z, z))

    seq, lvec, dvec, wmat, h0 = _s5_specs(s, nc, "bk")
    t = nb * s
    return pl.pallas_call(
        body, name="s5_fwd", grid=(nb, S5_BLOCKS),
        in_specs=[seq, lvec, lvec, wmat, wmat, wmat, wmat, dvec],
        out_specs=[seq, seq, h0],
        out_shape=[jax.ShapeDtypeStruct((t, D_MODEL), bf16), jax.ShapeDtypeStruct((t, D_MODEL), f32),
                   jax.ShapeDtypeStruct((nb * S5_BLOCKS, nc, 2, S5_BLOCK_STATES), f32)],
        compiler_params=_cp("parallel", "parallel"),
    )(u, lbr, lbi, wbr, wbi, wcr, wci, dskip)


def _s5_bwd(dyg, y, u, h0, lbr, lbi, wbr, wbi, wcr, wci, dskip, nb, s):
    ln = S5_CHUNK
    nc = s // ln

    def body(dyg_ref, y_ref, u_ref, h0_ref, lr_ref, li_ref, wbr_ref, wbi_ref, wcr_ref, wci_ref, d_ref,
             du_ref, dlr_ref, dli_ref, dwbr_ref, dwbi_ref, dwcr_ref, dwci_ref, dd_ref):
        @pl.when(pl.program_id(1) == 0)
        def _():
            for r in (dlr_ref, dli_ref, dwbr_ref, dwbi_ref, dwcr_ref, dwci_ref, dd_ref):
                r[...] = jnp.zeros_like(r)
        row = lax.broadcasted_iota(jnp.int32, (ln, S5_BLOCK_STATES), 0)
        lr, li = lr_ref[...], li_ref[...]
        pr, pi = _s5_pow_table(lr, li, ln, row, False)
        qr, qi = _s5_pow_table(lr, -li, ln, row, True)
        dv = d_ref[...]
        rsum = lambda v: jnp.sum(v, axis=0, keepdims=True)

        def step(i, carry):
            gnr, gni = carry
            n = nc - 1 - i
            st = pl.multiple_of(n * ln, ln)
            uc = u_ref[pl.ds(st, ln), :]
            ub = uc.astype(bf16)
            h0v = h0_ref[0, n]
            h0r, h0i = h0v[0:1], h0v[1:2]
            br, bi = _s5_local_scan(_nn(ub, wbr_ref[0]), _nn(ub, wbi_ref[0]), lr, li, row, False)
            cr, ci = _cmul(pr, pi, h0r, h0i)
            hr, hi = br + cr, bi + ci
            dy = jax.vjp(jax.nn.gelu, y_ref[pl.ds(st, ln), :])[1](dyg_ref[pl.ds(st, ln), :])[0]
            dyb = dy.astype(bf16)
            dd_ref[...] += rsum(dy * uc)
            gr, gi = _s5_local_scan(_nn(dyb, wcr_ref[0]), -_nn(dyb, wci_ref[0]), lr, -li, row, True)
            cr, ci = _cmul(qr, qi, gnr, gni)
            gr, gi = gr + cr, gi + ci
            hpr = jnp.where(row >= 1, pltpu.roll(hr, 1, 0), h0r)
            hpi = jnp.where(row >= 1, pltpu.roll(hi, 1, 0), h0i)
            dlr_ref[...] += rsum(gr * hpr + gi * hpi)
            dli_ref[...] += rsum(gi * hpr - gr * hpi)
            grb, gib = gr.astype(bf16), gi.astype(bf16)
            dwbr_ref[0] += _tn(ub, grb)
            dwbi_ref[0] += _tn(ub, gib)
            dwcr_ref[0] += _tn(dyb, hr.astype(bf16))
            dwci_ref[0] -= _tn(dyb, hi.astype(bf16))
            du_ref[pl.ds(st, ln), :] = _nt(grb, wbr_ref[0]) + _nt(gib, wbi_ref[0]) + dv * dy
            return gr[0:1, :], gi[0:1, :]

        z = jnp.zeros((1, S5_BLOCK_STATES), f32)
        lax.fori_loop(0, nc, step, (z, z))

    seq, lvec, dvec, wmat, h0s = _s5_specs(s, nc, "kb")
    t = nb * s
    lshape = jax.ShapeDtypeStruct((1, S5_BLOCKS * S5_BLOCK_STATES), f32)
    wshape = jax.ShapeDtypeStruct((S5_BLOCKS, 128, S5_BLOCK_STATES), f32)
    return pl.pallas_call(
        body, name="s5_bwd", grid=(S5_BLOCKS, nb),
        in_specs=[seq, seq, seq, h0s, lvec, lvec, wmat, wmat, wmat, wmat, dvec],
        out_specs=[seq, lvec, lvec, wmat, wmat, wmat, wmat, dvec],
        out_shape=[jax.ShapeDtypeStruct((t, D_MODEL), f32), lshape, lshape, wshape, wshape, wshape, wshape,
                   jax.ShapeDtypeStruct((1, D_MODEL), f32)],
        compiler_params=_cp("parallel", "arbitrary"),
    )(dyg, y, u, h0, lbr, lbi, wbr, wbi, wcr, wci, dskip)


def _blockdiag(w):
    w4 = w.reshape(S5_BLOCKS, 8, S5_GROUP, S5_STATE)
    same_group = jnp.eye(8, dtype=bool)[None, :, None, :, None]
    return jnp.where(same_group, w4[:, :, :, None, :], 0.0).reshape(S5_BLOCKS, 128, S5_BLOCK_STATES)


def _blockdiag_t(dw):
    d5 = dw.reshape(S5_BLOCKS, 8, S5_GROUP, 8, S5_STATE)
    diag = jnp.diagonal(d5, axis1=1, axis2=3)
    return jnp.moveaxis(diag, 3, 1).reshape(S5_GROUPS, S5_GROUP, S5_STATE)


def _glu_fwd(ygb, wa, wb, x, tm=512, tn=512):
    t, d = x.shape

    def body(y_ref, wa_ref, wb_ref, x_ref, o_ref, p_ref, q_ref):
        p = _nn(y_ref[...], wa_ref[...])
        q = _nn(y_ref[...], wb_ref[...])
        p_ref[...] = p
        q_ref[...] = q
        o_ref[...] = x_ref[...] + p * jax.nn.sigmoid(q)

    tile = pl.BlockSpec((tm, tn), lambda i, j: (i, j))
    wsp = pl.BlockSpec((d, tn), lambda i, j: (0, j))
    out = jax.ShapeDtypeStruct((t, d), f32)
    return pl.pallas_call(
        body, name="glu_fwd", grid=(t // tm, d // tn),
        in_specs=[pl.BlockSpec((tm, d), lambda i, j: (i, 0)), wsp, wsp, tile],
        out_specs=[tile, tile, tile], out_shape=[out, out, out],
        compiler_params=_cp("parallel", "parallel"),
    )(ygb, wa, wb, x)


def _place():
    x, y, c = lax.axis_index("x"), lax.axis_index("y"), lax.axis_index("c")
    return x, y, c, [(1 - x, y), (x, 1 - y), (1 - x, 1 - y)]


def _all_gather(name, arrays):
    n = len(arrays)

    def body(*refs):
        ins, outs = refs[:n], refs[n:2 * n]
        send_sems, recv_sems, local_sems = refs[2 * n:]
        x, y, c, chips = _place()
        me, sib = (x, y, c), (x, y, 1 - c)

        def copy(i, k, block, to, src=None):
            dst = outs[i].at[4 * block[0] + 2 * block[1] + block[2]]
            return pltpu.make_async_remote_copy(
                src_ref=dst if src is None else src, dst_ref=dst,
                send_sem=send_sems.at[i * 7 + k], recv_sem=recv_sems.at[i * 7 + k],
                device_id=to, device_id_type=MESH)

        mine = [pltpu.make_async_copy(ins[i], outs[i].at[4 * x + 2 * y + c], local_sems.at[i]) for i in range(n)]
        for m in mine:
            m.start()
        first = []
        for i in range(n):
            first.append(copy(i, 0, me, sib, src=ins[i]))
            first += [copy(i, 1 + j, me, (*chip, c), src=ins[i]) for j, chip in enumerate(chips)]
        for cp in first:
            cp.start()
        passed = []
        for j, chip in enumerate(chips):
            for i in range(n):
                copy(i, 1 + j, (*chip, c), me).wait_recv()
                fwd = copy(i, 4 + j, (*chip, c), sib)
                fwd.start()
                passed.append(fwd)
        for i in range(n):
            copy(i, 0, sib, me).wait_recv()
        for j, chip in enumerate(chips):
            for i in range(n):
                copy(i, 4 + j, (*chip, 1 - c), me).wait_recv()
        for cp in first + passed:
            cp.wait_send()
        for m in mine:
            m.wait()

    return pl.pallas_call(
        body, name=name,
        in_specs=[ANY_SPEC] * n, out_specs=[ANY_SPEC] * n,
        out_shape=[jax.ShapeDtypeStruct((N_DEV,) + a.shape, a.dtype) for a in arrays],
        scratch_shapes=[pltpu.SemaphoreType.DMA((7 * n,)), pltpu.SemaphoreType.DMA((7 * n,)),
                        pltpu.SemaphoreType.DMA((n,))],
    )(*arrays)


def _rs_sibling(name, arrays):
    n = len(arrays)

    def body(*refs):
        ins, outs = refs[:n], refs[n:2 * n]
        send_sems, recv_sems = refs[2 * n:]
        x, y, c, _ = _place()
        copies = []
        for i in range(n):
            for q in range(4):
                copies.append(pltpu.make_async_remote_copy(
                    src_ref=ins[i].at[2 * q + 1 - c], dst_ref=outs[i].at[q],
                    send_sem=send_sems.at[i * 4 + q], recv_sem=recv_sems.at[i * 4 + q],
                    device_id=(x, y, 1 - c), device_id_type=MESH))
        for cp in copies:
            cp.start()
        for cp in copies:
            cp.wait_recv()
        for cp in copies:
            cp.wait_send()

    return pl.pallas_call(
        body, name=name,
        in_specs=[ANY_SPEC] * n, out_specs=[ANY_SPEC] * n,
        out_shape=[jax.ShapeDtypeStruct((4,) + a.shape[1:], a.dtype) for a in arrays],
        scratch_shapes=[pltpu.SemaphoreType.DMA((4 * n,)), pltpu.SemaphoreType.DMA((4 * n,))],
    )(*arrays)


def _rs_chips(name, arrays):
    n = len(arrays)

    def body(*refs):
        ins, outs = refs[:n], refs[n:2 * n]
        send_sems, recv_sems = refs[2 * n:]
        x, y, c, chips = _place()
        copies = []
        for i in range(n):
            for j, chip in enumerate(chips):
                copies.append(pltpu.make_async_remote_copy(
                    src_ref=ins[i].at[2 * chip[0] + chip[1]], dst_ref=outs[i].at[j],
                    send_sem=send_sems.at[i * 3 + j], recv_sem=recv_sems.at[i * 3 + j],
                    device_id=(*chip, c), device_id_type=MESH))
        for cp in copies:
            cp.start()
        for cp in copies:
            cp.wait_recv()
        for cp in copies:
            cp.wait_send()

    return pl.pallas_call(
        body, name=name,
        in_specs=[ANY_SPEC] * n, out_specs=[ANY_SPEC] * n,
        out_shape=[jax.ShapeDtypeStruct((3,) + a.shape[1:], a.dtype) for a in arrays],
        scratch_shapes=[pltpu.SemaphoreType.DMA((3 * n,)), pltpu.SemaphoreType.DMA((3 * n,))],
    )(*arrays)


def _rows(a):
    return a.reshape(-1, a.shape[-1])


def _row_tile(r):
    for tm in (512, 256, 128, 64, 32, 16, 8):
        if r % tm == 0:
            return tm
    return r


def _add2(name, a, b):
    a2, b2 = _rows(a), _rows(b)

    def fn(u, v):
        s = u.astype(f32) + v.astype(f32)
        return s, s

    out, outb = _ew(name, fn, [a2, b2], [(a2.shape[1], f32), (a2.shape[1], bf16)], tm=_row_tile(a2.shape[0]))
    return out.reshape(a.shape), outb.reshape(a.shape)


def _sum8(name, gathered):
    _, r, n = gathered.shape
    tm = _row_tile(r)

    def body(g_ref, o_ref):
        acc = g_ref[0]
        for k in range(1, N_DEV):
            acc = acc + g_ref[k]
        o_ref[...] = acc

    return pl.pallas_call(
        body, name=name, grid=(r // tm,),
        in_specs=[pl.BlockSpec((N_DEV, tm, n), lambda i: (0, i, 0))],
        out_specs=pl.BlockSpec((tm, n), lambda i: (i, 0)),
        out_shape=jax.ShapeDtypeStruct((r, n), f32),
        compiler_params=_cp("parallel"),
    )(gathered)


def _adamw(name, w, m, v, gparts):
    shape = w.shape
    w2, m2, v2 = _rows(w), _rows(m), _rows(v)
    parts = [_rows(p) for p in gparts]
    c1 = 1.0 - ADAM_B1 ** ADAM_STEP
    c2 = 1.0 - ADAM_B2 ** ADAM_STEP

    def fn(wv, mv, vv, *gs):
        g = gs[0].astype(f32)
        for p in gs[1:]:
            g = g + p.astype(f32)
        mn = ADAM_B1 * mv + (1.0 - ADAM_B1) * g
        vn = ADAM_B2 * vv + (1.0 - ADAM_B2) * (g * g)
        delta = -ADAM_LR * ((mn / c1) / (jnp.sqrt(vn / c2) + ADAM_EPS) + ADAM_WD * wv)
        return g, delta, mn, vn

    n = w2.shape[1]
    outs = _ew(name, fn, [w2, m2, v2] + parts, [(n, f32)] * 4, tm=_row_tile(w2.shape[0]))
    return [o.reshape(shape) for o in outs]


def _pack(arrays):
    flat = jnp.concatenate([a.reshape(-1).astype(f32) for a in arrays])
    pad = (-flat.shape[0]) % (128 * (512 if flat.shape[0] > 128 * 512 else 8))
    return jnp.pad(flat, (0, pad)).reshape(-1, 128)


def _unpack(packed, shapes):
    flat = packed.reshape(-1)
    out, off = [], 0
    for s in shapes:
        n = math.prod(s)
        out.append(flat[off:off + n].reshape(s))
        off += n
    return out


def _local_step(x, target, w, nb, s):
    cos, sin = _rope_tables(s)
    g = {}
    ffn_saved = {}
    ffn_w = (w["ffn_w1"], w["ffn_w3"], w["ffn_w2"])
    ffn_bufs = [lax.empty(t.shape, dt) for dt in (f32, bf16) for t in ffn_w]

    def ffn(xin, l, h):
        y, a, b = _ffn_fwd(f"ffn_fwd_{l}{h}", xin, w["ffn_g"][l][h], *ffn_w, l, h)
        ffn_saved[(l, h)] = (xin, a, b)
        return y

    def ffn_back(dy, l, h):
        xin, a, b = ffn_saved[(l, h)]
        dx, dg, hb, dyh, u, da, db = _ffn_dx(f"ffn_dx_{l}{h}", dy, xin, w["ffn_g"][l][h], *ffn_w, a, b, l, h)
        ffn_bufs[:] = _ffn_dw(f"ffn_dw_{l}{h}", hb, dyh, u, da, db, ffn_bufs, l, h)
        g[f"ffn_g_{l}{h}"] = dg
        return dx

    x1 = ffn(x, 0, 0)
    _, h0b = _norm_fwd("mix_norm_0", x1, w["mix_g"][0])
    proj = _mm("in_proj", h0b, w["w_in"], "nn", tn=768)[0]
    o_raw, rprev, mret = _ret_fwd(proj, cos, sin, w["ret_g"], nb, s)
    lru = _lru_fwd(proj, w["conv_w"], w["conv_b"], w["lru_w_a"], w["lru_b_a"], w["lru_w_i"], w["lru_b_i"], w["lru_lam"], nb, s)
    merged = _ew("merge", lambda a, b: (jnp.concatenate([a, b], axis=1),), [mret, lru], [(D_MODEL, bf16)])[0]
    x2 = _mm("out_proj", merged, w["w_out"], "nn", extras=[x1], epilogue=lambda acc, r: (acc + r,))[0]
    x3 = ffn(x2, 0, 1)
    x4 = ffn(x3, 1, 0)
    u, _ = _norm_fwd("mix_norm_1", x4, w["mix_g"][1])
    lbr, lbi, bbr, bbi = _s5_prep(w["s5_lr"], w["s5_li"], w["s5_ldt"], w["s5_bre"], w["s5_bim"])
    lbr_f, lbi_f = lbr.reshape(1, -1), lbi.reshape(1, -1)
    wbr, wbi = _blockdiag(bbr).astype(bf16), _blockdiag(bbi).astype(bf16)
    wcr, wci = _blockdiag(w["s5_cre"]).astype(bf16), _blockdiag(w["s5_cim"]).astype(bf16)
    ygb, ypre, h0s = _s5_fwd(u, lbr_f, lbi_f, wbr, wbi, wcr, wci, w["s5_d"], nb, s)
    x5, gp, gq = _glu_fwd(ygb, w["glu_a"], w["glu_b"], x4)
    x6 = ffn(x5, 1, 1)
    loss, dx6, g["final_g"] = _final_loss(x6, w["final_g"], target)

    dx5 = ffn_back(dx6, 1, 1)

    def glu_bwd(d, p, q):
        sg = jax.nn.sigmoid(q)
        return d * sg, d * p * sg * (1.0 - sg)

    dp, dq = _ew("glu_bwd", glu_bwd, [dx5, gp, gq], [(D_MODEL, bf16), (D_MODEL, bf16)])
    dyg = _mm("glu_dy_a", dp, w["glu_a"], "nt")[0]
    dyg = _mm("glu_dy_b", dq, w["glu_b"], "nt", extras=[dyg], epilogue=lambda acc, r: (acc + r,))[0]
    g["glu_a"], g["glu_a_b"] = _mm_tn("glu_dw_a", ygb, dp)
    g["glu_b"], g["glu_b_b"] = _mm_tn("glu_dw_b", ygb, dq)
    du, dlr, dli, dwbr, dwbi, dwcr, dwci, g["s5_d"] = _s5_bwd(dyg, ypre, u, h0s, lbr_f, lbi_f, wbr, wbi, wcr, wci, w["s5_d"], nb, s)
    g["s5_cre"], g["s5_cim"] = _blockdiag_t(dwcr), _blockdiag_t(dwci)
    g["s5_lr"], g["s5_li"], g["s5_ldt"], g["s5_bre"], g["s5_bim"] = _s5_prep_bwd(
        w["s5_lr"], w["s5_li"], w["s5_ldt"], w["s5_bre"], w["s5_bim"],
        (dlr.reshape(S5_GROUPS, S5_STATE), dli.reshape(S5_GROUPS, S5_STATE), _blockdiag_t(dwbr), _blockdiag_t(dwbi)))
    dx4, g["mix_g_1"] = _norm_bwd("mix_norm_1_bwd", du, x4, w["mix_g"][1], dx5)
    dx3 = ffn_back(dx4, 1, 0)
    dx2 = ffn_back(dx3, 0, 1)
    dmerged = _mm("out_proj_dx", dx2, w["w_out"], "nt")[0]
    g["w_out"], g["w_out_b"] = _mm_tn("out_proj_dw", merged, dx2)
    dq_, dk_, dv_, dgate, g["ret_g"] = _ret_bwd(dmerged, o_raw, rprev, proj, cos, sin, w["ret_g"], nb, s)
    (dxl, dgl, g["conv_w"], g["conv_b"], g["lru_w_a"], g["lru_b_a"], g["lru_w_i"], g["lru_b_i"], g["lru_lam"]) = _lru_bwd(
        dmerged, proj, w["conv_w"], w["conv_b"], w["lru_w_a"], w["lru_b_a"], w["lru_w_i"], w["lru_b_i"], w["lru_lam"], nb, s)
    dproj = _ew("dproj", lambda *p: (jnp.concatenate(p, axis=1),), [dq_, dk_, dv_, dgate, dxl, dgl], [(3072, bf16)])[0]
    dh0 = _mm("in_proj_dx", dproj, w["w_in"], "nt")[0]
    g["w_in"], g["w_in_b"] = _mm_tn("in_proj_dw", h0b, dproj)
    dx1, g["mix_g_0"] = _norm_bwd("mix_norm_0_bwd", dh0, x1, w["mix_g"][0], dx2)
    dx0 = ffn_back(dx1, 0, 0)
    for k, n in enumerate(("ffn_w1", "ffn_w3", "ffn_w2")):
        g[n], g[n + "_b"] = ffn_bufs[k], ffn_bufs[3 + k]
    return loss, dx0, g


_WEIGHTS = ["ffn_norm_g", "ffn_w1", "ffn_w3", "ffn_w2", "mix_norm_g", "w_in_even", "w_out_even", "ret_norm_g", "conv_w",
            "conv_b", "lru_w_a", "lru_b_a", "lru_w_i", "lru_b_i", "lru_lambda", "s5_lambda_re", "s5_lambda_im", "s5_log_dt",
            "s5_b_re", "s5_b_im", "s5_c_re", "s5_c_im", "s5_d", "glu_w_a", "glu_w_b", "final_norm_g"]
_BIG = ["ffn_w1", "ffn_w3", "ffn_w2", "w_in_even", "w_out_even", "glu_w_a", "glu_w_b"]
_SMALL_SHARDED = ["ffn_norm_g", "conv_w", "s5_d"]
_SMALL = [n for n in _WEIGHTS if n not in _BIG]


def kernel(x, ffn_norm_g, ffn_w1, ffn_w3, ffn_w2, mix_norm_g, w_in_even, w_out_even, ret_norm_g, conv_w, conv_b, lru_w_a, lru_b_a, lru_w_i, lru_b_i, lru_lambda, s5_lambda_re, s5_lambda_im, s5_log_dt, s5_b_re, s5_b_im, s5_c_re, s5_c_im, s5_d, glu_w_a, glu_w_b, final_norm_g, loss_target, m_ffn_norm_g, m_ffn_w1, m_ffn_w3, m_ffn_w2, m_mix_norm_g, m_w_in_even, m_w_out_even, m_ret_norm_g, m_conv_w, m_conv_b, m_lru_w_a, m_lru_b_a, m_lru_w_i, m_lru_b_i, m_lru_lambda, m_s5_lambda_re, m_s5_lambda_im, m_s5_log_dt, m_s5_b_re, m_s5_b_im, m_s5_c_re, m_s5_c_im, m_s5_d, m_glu_w_a, m_glu_w_b, m_final_norm_g, v_ffn_norm_g, v_ffn_w1, v_ffn_w3, v_ffn_w2, v_mix_norm_g, v_w_in_even, v_w_out_even, v_ret_norm_g, v_conv_w, v_conv_b, v_lru_w_a, v_lru_b_a, v_lru_w_i, v_lru_b_i, v_lru_lambda, v_s5_lambda_re, v_s5_lambda_im, v_s5_log_dt, v_s5_b_re, v_s5_b_im, v_s5_c_re, v_s5_c_im, v_s5_d, v_glu_w_a, v_glu_w_b, v_final_norm_g):
    a = dict(locals())
    nb, s, d = x.shape
    ax, ay, ac = lax.axis_index("x"), lax.axis_index("y"), lax.axis_index("c")
    dev = 4 * ax + 2 * ay + ac
    chip = 2 * ax + ay

    (sm,) = _all_gather("ag_small_weights", [_pack([ffn_norm_g, conv_w, s5_d])])
    sm = sm.reshape(N_DEV, -1)
    ffn_g_full = jnp.transpose(sm[:, :512].reshape(N_DEV, 2, 2, 128), (1, 2, 0, 3)).reshape(2, 2, D_MODEL)
    conv_w_full = jnp.transpose(sm[:, 512:768].reshape(N_DEV, 4, 64), (1, 0, 2)).reshape(4, LRU_WIDTH)
    s5_d_full = sm[:, 768:896].reshape(1, D_MODEL)
    gw1, gw3, gw2, gwin, gwout, gglua, gglub = _all_gather(
        "ag_weights", [ffn_w1.astype(bf16), ffn_w3.astype(bf16), ffn_w2.astype(bf16), w_in_even[0].astype(bf16),
                       w_out_even[0].astype(bf16), glu_w_a[0].astype(bf16), glu_w_b[0].astype(bf16)])
    w = {
        "ffn_g": [[ffn_g_full[l, h].reshape(1, D_MODEL) for h in range(2)] for l in range(2)],
        "ffn_w1": gw1, "ffn_w3": gw3, "ffn_w2": gw2,
        "mix_g": [mix_norm_g[0:1], mix_norm_g[1:2]],
        "w_in": jnp.transpose(gwin, (1, 0, 2)).reshape(D_MODEL, N_DEV * IN_SHARD),
        "w_out": gwout.reshape(D_MODEL, D_MODEL),
        "ret_g": ret_norm_g, "conv_w": conv_w_full, "conv_b": conv_b,
        "lru_w_a": lru_w_a[0], "lru_b_a": lru_b_a, "lru_w_i": lru_w_i[0], "lru_b_i": lru_b_i, "lru_lam": lru_lambda,
        "s5_lr": s5_lambda_re[0], "s5_li": s5_lambda_im[0], "s5_ldt": s5_log_dt.reshape(S5_GROUPS, 1),
        "s5_bre": jnp.swapaxes(s5_b_re[0], 1, 2), "s5_bim": jnp.swapaxes(s5_b_im[0], 1, 2),
        "s5_cre": s5_c_re[0], "s5_cim": s5_c_im[0], "s5_d": s5_d_full,
        "glu_a": gglua.reshape(D_MODEL, D_MODEL), "glu_b": gglub.reshape(D_MODEL, D_MODEL),
        "final_g": final_norm_g.reshape(1, D_MODEL),
    }

    loss_part, dx, g = _local_step(x.reshape(nb * s, d), loss_target.reshape(nb * s, d), w, nb, s)
    loss = lax.psum(loss_part[0, 0], ("x", "y", "c"))

    part = {
        "ffn_norm_g": jnp.stack([jnp.stack([g[f"ffn_g_{l}{h}"][0] for h in range(2)]) for l in range(2)]),
        "mix_norm_g": jnp.concatenate([g["mix_g_0"], g["mix_g_1"]], axis=0),
        "ret_norm_g": g["ret_g"], "conv_w": g["conv_w"][None], "conv_b": g["conv_b"],
        "lru_w_a": g["lru_w_a"][None], "lru_b_a": g["lru_b_a"], "lru_w_i": g["lru_w_i"][None], "lru_b_i": g["lru_b_i"],
        "lru_lambda": g["lru_lam"], "s5_lambda_re": g["s5_lr"][None], "s5_lambda_im": g["s5_li"][None],
        "s5_log_dt": g["s5_ldt"].reshape(1, S5_GROUPS),
        "s5_b_re": jnp.swapaxes(g["s5_bre"], 1, 2)[None], "s5_b_im": jnp.swapaxes(g["s5_bim"], 1, 2)[None],
        "s5_c_re": g["s5_cre"][None], "s5_c_im": g["s5_cim"][None], "s5_d": g["s5_d"], "final_norm_g": g["final_g"][0],
    }
    (gath,) = _all_gather("ag_small_grads", [_pack([part[n] for n in _SMALL])])
    full = dict(zip(_SMALL, _unpack(_sum8("sum_small_grads", gath), [part[n].shape for n in _SMALL])))
    for n in _SMALL_SHARDED:
        width = a[n].shape[-1]
        full[n] = lax.dynamic_slice_in_dim(full[n], dev * width, width, axis=full[n].ndim - 1)
    shapes = [a[n].shape for n in _SMALL]
    packed = _adamw("adamw_small", _pack([a[n] for n in _SMALL]), _pack([a["m_" + n] for n in _SMALL]),
                    _pack([a["v_" + n] for n in _SMALL]), [_pack([full[n] for n in _SMALL])])
    res = {n: vals for n, vals in zip(_SMALL, zip(*[_unpack(p, shapes) for p in packed]))}

    def slots(name, sfx):
        t = g[name + sfx]
        if name == "w_in":
            return jnp.transpose(t.reshape(D_MODEL, N_DEV, IN_SHARD), (1, 0, 2))
        return t if name.startswith("ffn") else t.reshape(N_DEV, D_MODEL // N_DEV, D_MODEL)

    names = ["ffn_w1", "ffn_w3", "ffn_w2", "w_in", "w_out", "glu_a", "glu_b"]
    grads = [slots(n, "") for n in names]
    from_sibling = _rs_sibling("rs_sibling", [slots(n, "_b") for n in names])
    chip_sums, chip_sums_b = [], []
    for n, t, r in zip(_BIG, grads, from_sibling):
        mine = lax.dynamic_index_in_dim(t.reshape((4, 2) + t.shape[1:]), ac, axis=1, keepdims=False)
        cs, csb = _add2("chip_sum_" + n, mine, r)
        chip_sums.append(cs)
        chip_sums_b.append(csb)
    from_chips = _rs_chips("rs_chips", chip_sums_b)
    for n, t, r in zip(_BIG, chip_sums, from_chips):
        own = lax.dynamic_index_in_dim(t, chip, axis=0, keepdims=False).reshape(a[n].shape)
        parts = [own] + [r[j].reshape(a[n].shape) for j in range(3)]
        res[n] = _adamw("adamw_" + n, a[n], a["m_" + n], a["v_" + n], parts)

    out = [loss, dx.reshape(nb, s, d)]
    for k in range(4):
        out += [res[n][k] for n in _WEIGHTS]
    return tuple(out)
```

```python
import functools
import math

import numpy as np
import jax
import jax.numpy as jnp
from jax import lax
from jax.experimental import pallas as pl
from jax.experimental.pallas import tpu as pltpu

f32 = jnp.float32
bf16 = jnp.bfloat16

D_MODEL = 1024
N_DEV = 8
EPS = 1e-6
RET_HEADS = 4
HEAD_DIM = 128
RET_WIDTH = 512
RET_CHUNK = 128
ROPE_BASE = 10000.0
LRU_WIDTH = 512
LRU_BLOCKS = 4
LRU_C = 8.0
S5_GROUP = 16
S5_GROUPS = 64
S5_STATE = 64
S5_CHUNK = 128
S5_BLOCKS = 8
S5_BLOCK_STATES = 512
D_FF = 2816
FF_SHARD = D_FF // N_DEV
IN_SHARD = 3072 // N_DEV
ADAM_LR = 0.001
ADAM_B1 = 0.9
ADAM_B2 = 0.999
ADAM_EPS = 1e-08
ADAM_WD = 0.01
ADAM_STEP = 10

VMEM_LIMIT = 48 * 1024 * 1024
VMEM_SPEC = pl.BlockSpec(memory_space=pltpu.VMEM)
ANY_SPEC = pl.BlockSpec(memory_space=pl.ANY)
HBM_SPEC = pl.BlockSpec(memory_space=pltpu.HBM)
SEM_SPEC = pl.BlockSpec(memory_space=pltpu.SEMAPHORE)
SIDE_EFFECT = pltpu.SideEffectType.DATAFLOW_SIDE_EFFECTING
MESH = pl.DeviceIdType.MESH


def _cp(*sem):
    return pltpu.CompilerParams(dimension_semantics=sem, vmem_limit_bytes=VMEM_LIMIT)


def _nn(a, b):
    return jnp.dot(a, b, preferred_element_type=f32)


def _nt(a, b):
    return lax.dot_general(a, b, (((1,), (1,)), ((), ())), preferred_element_type=f32)


def _tn(a, b):
    return lax.dot_general(a, b, (((0,), (0,)), ((), ())), preferred_element_type=f32)


def _rms_fwd(x, g):
    r = lax.rsqrt(jnp.mean(x * x, axis=-1, keepdims=True) + EPS)
    xn = x * r
    return xn * g, xn, r


def _rms_bwd(dh, xn, r, g):
    dxn = dh * g
    dx = r * (dxn - xn * jnp.mean(dxn * xn, axis=-1, keepdims=True))
    dg = jnp.sum(dh * xn, axis=0, keepdims=True)
    return dx, dg


def _shift_dn(v, d, row, fill=0.0):
    return jnp.where(row >= d, pltpu.roll(v, d, 0), fill)


def _shift_up(v, d, row, fill=0.0):
    n = v.shape[0]
    return jnp.where(row < n - d, pltpu.roll(v, n - d, 0), fill)


def _ew(name, fn, ins, outs, tm=512):
    t = ins[0].shape[0]
    n_in = len(ins)

    def body(*refs):
        res = fn(*[r[...] for r in refs[:n_in]])
        for o, v in zip(refs[n_in:], res):
            o[...] = v.astype(o.dtype)

    return pl.pallas_call(
        body, name=name, grid=(t // tm,),
        in_specs=[pl.BlockSpec((tm, a.shape[1]), lambda i: (i, 0)) for a in ins],
        out_specs=[pl.BlockSpec((tm, n), lambda i: (i, 0)) for n, _ in outs],
        out_shape=[jax.ShapeDtypeStruct((t, n), dt) for n, dt in outs],
        compiler_params=_cp("parallel"),
    )(*ins)


def _mm(name, x, w, kind, extras=(), epilogue=None, outs=None, tm=512, tn=512):
    t = x.shape[0]
    n = w.shape[1] if kind == "nn" else w.shape[0]
    tn = min(tn, n)
    outs = outs or [f32]
    n_ex = len(extras)

    def body(x_ref, w_ref, *refs):
        xb = x_ref[...].astype(bf16)
        acc = _nn(xb, w_ref[...]) if kind == "nn" else _nt(xb, w_ref[...])
        res = epilogue(acc, *[r[...] for r in refs[:n_ex]]) if epilogue else (acc,)
        for o, v in zip(refs[n_ex:], res):
            o[...] = v.astype(o.dtype)

    w_spec = (pl.BlockSpec((w.shape[0], tn), lambda i, j: (0, j)) if kind == "nn"
              else pl.BlockSpec((tn, w.shape[1]), lambda i, j: (j, 0)))
    tile = pl.BlockSpec((tm, tn), lambda i, j: (i, j))
    return pl.pallas_call(
        body, name=name, grid=(t // tm, n // tn),
        in_specs=[pl.BlockSpec((tm, x.shape[1]), lambda i, j: (i, 0)), w_spec] + [tile] * n_ex,
        out_specs=[tile] * len(outs),
        out_shape=[jax.ShapeDtypeStruct((t, n), dt) for dt in outs],
        compiler_params=_cp("parallel", "parallel"),
    )(x, w, *extras)


def _mm_tn(name, x, y, tk=1024, tn=512, tt=512):
    t, k = x.shape
    n = y.shape[1]
    tk, tn = min(tk, k), min(tn, n)

    def body(x_ref, y_ref, o_ref, ob_ref):
        @pl.when(pl.program_id(2) == 0)
        def _():
            o_ref[...] = jnp.zeros_like(o_ref)
        o_ref[...] += _tn(x_ref[...].astype(bf16), y_ref[...].astype(bf16))

        @pl.when(pl.program_id(2) == pl.num_programs(2) - 1)
        def _():
            ob_ref[...] = o_ref[...].astype(bf16)

    out = pl.BlockSpec((tk, tn), lambda i, j, s: (i, j))
    return pl.pallas_call(
        body, name=name, grid=(k // tk, n // tn, t // tt),
        in_specs=[pl.BlockSpec((tt, tk), lambda i, j, s: (s, i)), pl.BlockSpec((tt, tn), lambda i, j, s: (s, j))],
        out_specs=[out, out],
        out_shape=[jax.ShapeDtypeStruct((k, n), f32), jax.ShapeDtypeStruct((k, n), bf16)],
        compiler_params=_cp("parallel", "parallel", "arbitrary"),
    )(x, y)


def _norm_fwd(name, x, g, tm=512):
    t, d = x.shape

    def body(x_ref, g_ref, h_ref, hb_ref):
        h, _, _ = _rms_fwd(x_ref[...], g_ref[...])
        h_ref[...] = h
        hb_ref[...] = h.astype(bf16)

    row = pl.BlockSpec((tm, d), lambda i: (i, 0))
    return pl.pallas_call(
        body, name=name, grid=(t // tm,),
        in_specs=[row, pl.BlockSpec((1, d), lambda i: (0, 0))],
        out_specs=[row, row],
        out_shape=[jax.ShapeDtypeStruct((t, d), f32), jax.ShapeDtypeStruct((t, d), bf16)],
        compiler_params=_cp("parallel"),
    )(x, g)


def _norm_bwd(name, dh, x, g, dres, tm=512):
    t, d = x.shape

    def body(dh_ref, x_ref, g_ref, dres_ref, dx_ref, dg_ref):
        gv = g_ref[...]
        _, xn, r = _rms_fwd(x_ref[...], gv)
        dx, dg = _rms_bwd(dh_ref[...], xn, r, gv)
        dx_ref[...] = dres_ref[...] + dx

        @pl.when(pl.program_id(0) == 0)
        def _():
            dg_ref[...] = jnp.zeros_like(dg_ref)
        dg_ref[...] += dg

    row = pl.BlockSpec((tm, d), lambda i: (i, 0))
    vec = pl.BlockSpec((1, d), lambda i: (0, 0))
    return pl.pallas_call(
        body, name=name, grid=(t // tm,),
        in_specs=[row, row, vec, row],
        out_specs=[row, vec],
        out_shape=[jax.ShapeDtypeStruct((t, d), f32), jax.ShapeDtypeStruct((1, d), f32)],
        compiler_params=_cp("arbitrary"),
    )(dh, x, g, dres)


def _final_loss(x, g, target, tm=512):
    t, d = x.shape

    def body(x_ref, g_ref, t_ref, loss_ref, dx_ref, dg_ref):
        gv = g_ref[...]
        y, xn, r = _rms_fwd(x_ref[...], gv)
        err = y - t_ref[...]
        dy = err * (1.0 / d)
        dx, dg = _rms_bwd(dy, xn, r, gv)
        dx_ref[...] = dx

        @pl.when(pl.program_id(0) == 0)
        def _():
            dg_ref[...] = jnp.zeros_like(dg_ref)
            loss_ref[...] = jnp.zeros_like(loss_ref)
        dg_ref[...] += dg
        loss_ref[...] += jnp.full((1, 128), 0.5 / d, f32) * jnp.sum(err * err)

    row = pl.BlockSpec((tm, d), lambda i: (i, 0))
    vec = pl.BlockSpec((1, d), lambda i: (0, 0))
    return pl.pallas_call(
        body, name="final_loss", grid=(t // tm,),
        in_specs=[row, vec, row],
        out_specs=[pl.BlockSpec((1, 128), lambda i: (0, 0)), row, vec],
        out_shape=[jax.ShapeDtypeStruct((1, 128), f32), jax.ShapeDtypeStruct((t, d), f32),
                   jax.ShapeDtypeStruct((1, d), f32)],
        compiler_params=_cp("arbitrary"),
    )(x, g, target)


def _load_ffn_weights(hbm_refs, vmem_refs, sems):
    @pl.when(pl.program_id(0) == 0)
    def _():
        copies = [pltpu.make_async_copy(src, dst, sems.at[k])
                  for k, (src, dst) in enumerate(zip(hbm_refs, vmem_refs))]
        for cp in copies:
            cp.start()
        for cp in copies:
            cp.wait()


def _ffn_weight_scratch(nj, d, ff):
    return [pltpu.VMEM((nj, d, ff), bf16), pltpu.VMEM((nj, d, ff), bf16), pltpu.VMEM((nj, ff, d), bf16),
            pltpu.SemaphoreType.DMA((3,))]


def _ffn_fwd(name, x, g, w1, w3, w2, tm=256):
    t, d = x.shape
    nj, ff = w1.shape[0], w1.shape[-1]

    def body(x_ref, g_ref, w1_hbm, w3_hbm, w2_hbm, y_ref, a_ref, b_ref, w1_ref, w3_ref, w2_ref, sems):
        _load_ffn_weights((w1_hbm, w3_hbm, w2_hbm), (w1_ref, w3_ref, w2_ref), sems)
        xv = x_ref[...]
        h, _, _ = _rms_fwd(xv, g_ref[...])
        hb = h.astype(bf16)
        acc = jnp.zeros((tm, d), f32)
        for j in range(nj):
            a = _nn(hb, w1_ref[j])
            b = _nn(hb, w3_ref[j])
            a_ref[j] = a.astype(bf16)
            b_ref[j] = b.astype(bf16)
            u = (a * jax.nn.sigmoid(a) * b).astype(bf16)
            acc = acc + _nn(u, w2_ref[j])
        y_ref[...] = xv + 0.5 * acc

    row = pl.BlockSpec((tm, d), lambda i: (i, 0))
    mid = pl.BlockSpec((nj, tm, ff), lambda i: (0, i, 0))
    return pl.pallas_call(
        body, name=name, grid=(t // tm,),
        in_specs=[row, pl.BlockSpec((1, d), lambda i: (0, 0)), ANY_SPEC, ANY_SPEC, ANY_SPEC],
        out_specs=[row, mid, mid],
        out_shape=[jax.ShapeDtypeStruct((t, d), f32), jax.ShapeDtypeStruct((nj, t, ff), bf16),
                   jax.ShapeDtypeStruct((nj, t, ff), bf16)],
        scratch_shapes=_ffn_weight_scratch(nj, d, ff),
        compiler_params=_cp("arbitrary"),
    )(x, g, w1, w3, w2)


def _ffn_dx(name, dy, x, g, w1, w3, w2, a, b, tm=256):
    t, d = x.shape
    nj, ff = w1.shape[0], w1.shape[-1]

    def body(dy_ref, x_ref, g_ref, w1_hbm, w3_hbm, w2_hbm, a_ref, b_ref,
             dx_ref, dg_ref, hb_ref, dyh_ref, u_ref, da_ref, db_ref, w1_ref, w3_ref, w2_ref, sems):
        _load_ffn_weights((w1_hbm, w3_hbm, w2_hbm), (w1_ref, w3_ref, w2_ref), sems)
        gv = g_ref[...]
        h, xn, r = _rms_fwd(x_ref[...], gv)
        hb_ref[...] = h.astype(bf16)
        dyv = dy_ref[...]
        dyh = (0.5 * dyv).astype(bf16)
        dyh_ref[...] = dyh
        dh = jnp.zeros((tm, d), f32)
        for j in range(nj):
            av = a_ref[j].astype(f32)
            bv = b_ref[j].astype(f32)
            s = jax.nn.sigmoid(av)
            silu = av * s
            u_ref[j] = (silu * bv).astype(bf16)
            du = _nt(dyh, w2_ref[j])
            dab = (du * bv * (s * (1.0 + av * (1.0 - s)))).astype(bf16)
            dbb = (du * silu).astype(bf16)
            da_ref[j] = dab
            db_ref[j] = dbb
            dh = dh + _nt(dab, w1_ref[j]) + _nt(dbb, w3_ref[j])
        dx, dg = _rms_bwd(dh, xn, r, gv)
        dx_ref[...] = dyv + dx

        @pl.when(pl.program_id(0) == 0)
        def _():
            dg_ref[...] = jnp.zeros_like(dg_ref)
        dg_ref[...] += dg

    row = pl.BlockSpec((tm, d), lambda i: (i, 0))
    vec = pl.BlockSpec((1, d), lambda i: (0, 0))
    mid = pl.BlockSpec((nj, tm, ff), lambda i: (0, i, 0))
    mid_shape = jax.ShapeDtypeStruct((nj, t, ff), bf16)
    return pl.pallas_call(
        body, name=name, grid=(t // tm,),
        in_specs=[row, row, vec, ANY_SPEC, ANY_SPEC, ANY_SPEC, mid, mid],
        out_specs=[row, vec, row, row, mid, mid, mid],
        out_shape=[jax.ShapeDtypeStruct((t, d), f32), jax.ShapeDtypeStruct((1, d), f32),
                   jax.ShapeDtypeStruct((t, d), bf16), jax.ShapeDtypeStruct((t, d), bf16),
                   mid_shape, mid_shape, mid_shape],
        scratch_shapes=_ffn_weight_scratch(nj, d, ff),
        compiler_params=_cp("arbitrary"),
    )(dy, x, g, w1, w3, w2, a, b)


def _ffn_dw(name, hb, dyh, u, da, db, bufs, l, h, tt=512):
    t, d = hb.shape
    nj, _, ff = u.shape

    def body(hb_ref, dyh_ref, u_ref, da_ref, db_ref, *refs):
        dw1_ref, dw3_ref, dw2_ref, dw1b_ref, dw3b_ref, dw2b_ref = refs[3:]

        @pl.when(pl.program_id(1) == 0)
        def _():
            dw1_ref[...] = jnp.zeros_like(dw1_ref)
            dw3_ref[...] = jnp.zeros_like(dw3_ref)
            dw2_ref[...] = jnp.zeros_like(dw2_ref)
        hv = hb_ref[...]
        dw1_ref[0] += _tn(hv, da_ref[0])
        dw3_ref[0] += _tn(hv, db_ref[0])
        dw2_ref[0] += _tn(u_ref[0], dyh_ref[...])

        @pl.when(pl.program_id(1) == pl.num_programs(1) - 1)
        def _():
            dw1b_ref[...] = dw1_ref[...].astype(bf16)
            dw3b_ref[...] = dw3_ref[...].astype(bf16)
            dw2b_ref[...] = dw2_ref[...].astype(bf16)

    row = pl.BlockSpec((tt, d), lambda j, s: (s, 0))
    mid = pl.BlockSpec((1, tt, ff), lambda j, s: (j, s, 0))
    slab_in = pl.BlockSpec((1, None, None, d, ff), lambda j, s: (j, l, h, 0, 0))
    slab_out = pl.BlockSpec((1, None, None, ff, d), lambda j, s: (j, l, h, 0, 0))
    blk_in = pl.BlockSpec((1, d, ff), lambda j, s: (j, 0, 0))
    blk_out = pl.BlockSpec((1, ff, d), lambda j, s: (j, 0, 0))
    outs = pl.pallas_call(
        body, name=name, grid=(nj, t // tt),
        in_specs=[row, row, mid, mid, mid] + [ANY_SPEC] * 3,
        out_specs=[slab_in, slab_in, slab_out, blk_in, blk_in, blk_out],
        out_shape=[jax.ShapeDtypeStruct(b.shape, b.dtype) for b in bufs]
        + [jax.ShapeDtypeStruct((nj, d, ff), bf16), jax.ShapeDtypeStruct((nj, d, ff), bf16),
           jax.ShapeDtypeStruct((nj, ff, d), bf16)],
        input_output_aliases={5 + k: k for k in range(3)},
        compiler_params=_cp("parallel", "arbitrary"),
    )(hb, dyh, u, da, db, *bufs)
    return outs[:3], outs[3:]


_LOG_GAMMA = [float(np.log1p(-np.float32(2.0) ** np.float32(-5.0 - h))) for h in range(RET_HEADS)]


def _ret_consts(h):
    lg = jnp.where(h == 0, _LOG_GAMMA[0], jnp.where(h == 1, _LOG_GAMMA[1],
                   jnp.where(h == 2, _LOG_GAMMA[2], _LOG_GAMMA[3]))).astype(f32)
    c = RET_CHUNK
    r = lax.broadcasted_iota(jnp.int32, (c, c), 0)
    cc = lax.broadcasted_iota(jnp.int32, (c, c), 1)
    decay = jnp.where(r >= cc, jnp.exp(lg * jnp.maximum((r - cc).astype(f32), 0.0)), 0.0)
    pos = lax.broadcasted_iota(jnp.int32, (c, 1), 0).astype(f32)
    kd = jnp.exp(lg * (c - 1.0 - pos))
    qd = jnp.exp(lg * (pos + 1.0))
    gc = jnp.exp(lg * c)
    return decay, kd, qd, gc


def _rope(x, cos, sin):
    return x * cos + pltpu.roll(x, HEAD_DIM // 2, 1) * sin


def _rope_t(g, cos, sin):
    return g * cos + pltpu.roll(g * sin, HEAD_DIM // 2, 1)


def _rope_tables(s):
    half = HEAD_DIM // 2
    inv = ROPE_BASE ** (-jnp.arange(half, dtype=f32) / half)
    ang = jnp.arange(s, dtype=f32)[:, None] * inv[None, :]
    cos, sin = jnp.cos(ang), jnp.sin(ang)
    return jnp.concatenate([cos, cos], axis=1), jnp.concatenate([-sin, sin], axis=1)


def _head_ln(o):
    mu = jnp.mean(o, axis=-1, keepdims=True)
    oc = o - mu
    rs = lax.rsqrt(jnp.mean(oc * oc, axis=-1, keepdims=True) + EPS)
    return oc * rs, rs


def _ret_fwd(proj, cos, sin, ret_g, nb, s):
    c = RET_CHUNK
    nc = s // c
    t = nb * s
    scale = HEAD_DIM ** -0.5

    def body(q_ref, k_ref, v_ref, gate_ref, cos_ref, sin_ref, g_ref, o_ref, rprev_ref, m_ref, r_acc):
        n = pl.program_id(2)

        @pl.when(n == 0)
        def _():
            r_acc[...] = jnp.zeros_like(r_acc)
        decay, kd, qd, gc = _ret_consts(pl.program_id(1))
        cs, sn = cos_ref[...], sin_ref[...]
        q = _rope(q_ref[...], cs, sn)
        k = _rope(k_ref[...], cs, sn) * scale
        vb = v_ref[...].astype(bf16)
        sc = _nt(q.astype(bf16), k.astype(bf16)) * decay
        rv = r_acc[...]
        rprev_ref[0] = rv
        o = _nn(sc.astype(bf16), vb) + _nn((q * qd).astype(bf16), rv.astype(bf16))
        o_ref[...] = o
        r_acc[...] = rv * gc + _tn((k * kd).astype(bf16), vb)
        y, _ = _head_ln(o)
        gate = gate_ref[...]
        m_ref[...] = y * g_ref[...] * (gate * jax.nn.sigmoid(gate))

    def col(off):
        return pl.BlockSpec((c, HEAD_DIM), lambda b, h, n: (b * nc + n, off + h))

    tab = pl.BlockSpec((c, HEAD_DIM), lambda b, h, n: (n, 0))
    return pl.pallas_call(
        body, name="ret_fwd", grid=(nb, RET_HEADS, nc),
        in_specs=[col(0), col(4), col(8), col(12), tab, tab, pl.BlockSpec((1, HEAD_DIM), lambda b, h, n: (0, h))],
        out_specs=[col(0), pl.BlockSpec((1, HEAD_DIM, HEAD_DIM), lambda b, h, n: ((b * RET_HEADS + h) * nc + n, 0, 0)),
                   col(0)],
        out_shape=[jax.ShapeDtypeStruct((t, RET_WIDTH), f32),
                   jax.ShapeDtypeStruct((nb * RET_HEADS * nc, HEAD_DIM, HEAD_DIM), f32),
                   jax.ShapeDtypeStruct((t, RET_WIDTH), f32)],
        scratch_shapes=[pltpu.VMEM((HEAD_DIM, HEAD_DIM), f32)],
        compiler_params=_cp("parallel", "parallel", "arbitrary"),
    )(proj, proj, proj, proj, cos, sin, ret_g)


def _ret_bwd(dmerged, o_raw, rprev, proj, cos, sin, ret_g, nb, s):
    c = RET_CHUNK
    nc = s // c
    t = nb * s
    scale = HEAD_DIM ** -0.5

    def body(dm_ref, o_ref, rprev_ref, q_ref, k_ref, v_ref, gate_ref, cos_ref, sin_ref, g_ref,
             dq_ref, dk_ref, dv_ref, dgate_ref, dg_ref, dr_acc):
        b, n = pl.program_id(1), pl.program_id(2)

        @pl.when(n == 0)
        def _():
            dr_acc[...] = jnp.zeros_like(dr_acc)

        @pl.when((n == 0) & (b == 0))
        def _():
            dg_ref[...] = jnp.zeros_like(dg_ref)
        decay, kd, qd, gc = _ret_consts(pl.program_id(0))
        cs, sn = cos_ref[...], sin_ref[...]
        q = _rope(q_ref[...], cs, sn)
        k = _rope(k_ref[...], cs, sn) * scale
        qb, kb = q.astype(bf16), k.astype(bf16)
        vb = v_ref[...].astype(bf16)
        sc = _nt(qb, kb) * decay
        gv = g_ref[...]
        y, rs = _head_ln(o_ref[...])
        gate = gate_ref[...]
        sg = jax.nn.sigmoid(gate)
        silu = gate * sg
        dm = dm_ref[...]
        dgate_ref[...] = dm * y * gv * (sg * (1.0 + gate * (1.0 - sg)))
        dyl = dm * gv * silu
        dg_ref[...] += jnp.sum(dm * y * silu, axis=0, keepdims=True)
        do = rs * (dyl - jnp.mean(dyl, axis=-1, keepdims=True) - y * jnp.mean(dyl * y, axis=-1, keepdims=True))
        dob = do.astype(bf16)
        rv = rprev_ref[0]
        drn = dr_acc[...]
        drb = drn.astype(bf16)
        ds = (_nt(dob, vb) * decay).astype(bf16)
        kdb = (k * kd).astype(bf16)
        qdb = (q * qd).astype(bf16)
        dq_r = _nn(ds, kb) + _nt(dob, rv.astype(bf16)) * qd
        dk_r = _tn(ds, qb) + _nt(vb, drb) * kd
        dv_ref[...] = _tn(sc.astype(bf16), dob) + _nn(kdb, drb)
        dr_acc[...] = drn * gc + _tn(qdb, dob)
        dq_ref[...] = _rope_t(dq_r, cs, sn)
        dk_ref[...] = _rope_t(dk_r * scale, cs, sn)

    def col(off):
        return pl.BlockSpec((c, HEAD_DIM), lambda h, b, n: (b * nc + nc - 1 - n, off + h))

    tab = pl.BlockSpec((c, HEAD_DIM), lambda h, b, n: (nc - 1 - n, 0))
    gsp = pl.BlockSpec((1, HEAD_DIM), lambda h, b, n: (0, h))
    out_t = jax.ShapeDtypeStruct((t, RET_WIDTH), f32)
    return pl.pallas_call(
        body, name="ret_bwd", grid=(RET_HEADS, nb, nc),
        in_specs=[col(0), col(0),
                  pl.BlockSpec((1, HEAD_DIM, HEAD_DIM), lambda h, b, n: ((b * RET_HEADS + h) * nc + nc - 1 - n, 0, 0)),
                  col(0), col(4), col(8), col(12), tab, tab, gsp],
        out_specs=[col(0), col(0), col(0), col(0), gsp],
        out_shape=[out_t, out_t, out_t, out_t, jax.ShapeDtypeStruct((1, RET_WIDTH), f32)],
        scratch_shapes=[pltpu.VMEM((HEAD_DIM, HEAD_DIM), f32)],
        compiler_params=_cp("parallel", "arbitrary", "arbitrary"),
    )(dmerged, o_raw, rprev, proj, proj, proj, proj, cos, sin, ret_g)


def _neg_expm1(z):
    series = -(z * (1.0 + z * (0.5 + z * (1.0 / 6.0 + z * (1.0 / 24.0)))))
    return jnp.where(z > -0.01, series, 1.0 - jnp.exp(z))


def _lru_gates(xc, pa, pi, lam):
    r = jax.nn.sigmoid(pa)
    i = jax.nn.sigmoid(pi)
    log_a = -LRU_C * r * jax.nn.softplus(-lam)
    a = jnp.exp(log_a)
    bx = jnp.sqrt(_neg_expm1(2.0 * log_a)) * i * xc
    return a, bx


def _scan_fwd(a, b, row):
    d = 1
    while d < a.shape[0]:
        b = a * _shift_dn(b, d, row) + b
        a = a * _shift_dn(a, d, row, 1.0)
        d *= 2
    return b


def _scan_bwd(c, b, row):
    d = 1
    while d < c.shape[0]:
        b = c * _shift_up(b, d, row) + b
        c = c * _shift_up(c, d, row, 1.0)
        d *= 2
    return b


def _conv_fwd(x, cw, cb, row):
    return (cb + cw[3:4] * x + cw[2:3] * _shift_dn(x, 1, row) + cw[1:2] * _shift_dn(x, 2, row)
            + cw[0:1] * _shift_dn(x, 3, row))


def _lru_specs(s, order):
    def im(f):
        return (lambda b, g: f(b, g)) if order == "bg" else (lambda g, b: f(b, g))
    seq = lambda off: pl.BlockSpec((s, 128), im(lambda b, g: (b, off + g)))
    vec = pl.BlockSpec((1, 128), im(lambda b, g: (0, g)))
    cw = pl.BlockSpec((4, 128), im(lambda b, g: (0, g)))
    mat = pl.BlockSpec((1, 128, 128), im(lambda b, g: (g, 0, 0)))
    return seq, vec, cw, mat


def _lru_fwd(proj, conv_w, conv_b, w_a, b_a, w_i, b_i, lam, nb, s):
    def body(x_ref, gt_ref, cw_ref, cb_ref, wa_ref, ba_ref, wi_ref, bi_ref, lam_ref, out_ref):
        row = lax.broadcasted_iota(jnp.int32, (s, 128), 0)
        xc = _conv_fwd(x_ref[...], cw_ref[...], cb_ref[...], row)
        xcb = xc.astype(bf16)
        pa = _nn(xcb, wa_ref[0].astype(bf16)) + ba_ref[...]
        pi = _nn(xcb, wi_ref[0].astype(bf16)) + bi_ref[...]
        a, bx = _lru_gates(xc, pa, pi, lam_ref[...])
        h = _scan_fwd(a, bx, row)
        out_ref[...] = h * jax.nn.gelu(gt_ref[...])

    seq, vec, cw, mat = _lru_specs(s, "bg")
    return pl.pallas_call(
        body, name="lru_fwd", grid=(nb, LRU_BLOCKS),
        in_specs=[seq(16), seq(20), cw, vec, mat, vec, mat, vec, vec],
        out_specs=seq(0),
        out_shape=jax.ShapeDtypeStruct((nb * s, LRU_WIDTH), f32),
        compiler_params=_cp("parallel", "parallel"),
    )(proj, proj, conv_w, conv_b, w_a, b_a, w_i, b_i, lam)


def _lru_bwd(dmerged, proj, conv_w, conv_b, w_a, b_a, w_i, b_i, lam, nb, s):
    def body(dout_ref, x_ref, gt_ref, cw_ref, cb_ref, wa_ref, ba_ref, wi_ref, bi_ref, lam_ref,
             dx_ref, dgt_ref, dcw_ref, dcb_ref, dwa_ref, dba_ref, dwi_ref, dbi_ref, dlam_ref):
        row = lax.broadcasted_iota(jnp.int32, (s, 128), 0)
        x = x_ref[...]
        cwv = cw_ref[...]
        xc = _conv_fwd(x, cwv, cb_ref[...], row)
        xcb = xc.astype(bf16)
        wab, wib = wa_ref[0].astype(bf16), wi_ref[0].astype(bf16)
        pa = _nn(xcb, wab) + ba_ref[...]
        pi = _nn(xcb, wib) + bi_ref[...]
        (a, bx), gates_vjp = jax.vjp(_lru_gates, xc, pa, pi, lam_ref[...])
        h = _scan_fwd(a, bx, row)
        ge, gelu_vjp = jax.vjp(jax.nn.gelu, gt_ref[...])
        dout = dout_ref[...]
        dgt_ref[...] = gelu_vjp(dout * h)[0]
        adj = _scan_bwd(_shift_up(a, 1, row), dout * ge, row)
        dxc, dpa, dpi, dlam = gates_vjp((adj * _shift_dn(h, 1, row), adj))
        dpab, dpib = dpa.astype(bf16), dpi.astype(bf16)
        dxc = dxc + _nt(dpab, wab) + _nt(dpib, wib)
        dx_ref[...] = (cwv[3:4] * dxc + cwv[2:3] * _shift_up(dxc, 1, row) + cwv[1:2] * _shift_up(dxc, 2, row)
                       + cwv[0:1] * _shift_up(dxc, 3, row))

        @pl.when(pl.program_id(1) == 0)
        def _():
            for r in (dcw_ref, dcb_ref, dwa_ref, dba_ref, dwi_ref, dbi_ref, dlam_ref):
                r[...] = jnp.zeros_like(r)
        rsum = lambda v: jnp.sum(v, axis=0, keepdims=True)
        dcw_ref[...] += jnp.concatenate([rsum(dxc * _shift_dn(x, 3, row)), rsum(dxc * _shift_dn(x, 2, row)),
                                         rsum(dxc * _shift_dn(x, 1, row)), rsum(dxc * x)], axis=0)
        dcb_ref[...] += rsum(dxc)
        dwa_ref[0] += _tn(xcb, dpab)
        dwi_ref[0] += _tn(xcb, dpib)
        dba_ref[...] += rsum(dpa)
        dbi_ref[...] += rsum(dpi)
        dlam_ref[...] += dlam

    seq, vec, cw, mat = _lru_specs(s, "gb")
    t = nb * s
    vshape = jax.ShapeDtypeStruct((1, LRU_WIDTH), f32)
    mshape = jax.ShapeDtypeStruct((LRU_BLOCKS, 128, 128), f32)
    return pl.pallas_call(
        body, name="lru_bwd", grid=(LRU_BLOCKS, nb),
        in_specs=[seq(4), seq(16), seq(20), cw, vec, mat, vec, mat, vec, vec],
        out_specs=[seq(0), seq(0), cw, vec, mat, vec, mat, vec, vec],
        out_shape=[jax.ShapeDtypeStruct((t, LRU_WIDTH), f32), jax.ShapeDtypeStruct((t, LRU_WIDTH), f32),
                   jax.ShapeDtypeStruct((4, LRU_WIDTH), f32), vshape, mshape, vshape, mshape, vshape, vshape],
        compiler_params=_cp("parallel", "arbitrary"),
    )(dmerged, proj, proj, conv_w, conv_b, w_a, b_a, w_i, b_i, lam)


def _s5_disc(lr, li, ldt, bre, bim):
    dt = jnp.exp(ldt)
    mag = jnp.exp(lr * dt)
    lbr = mag * jnp.cos(li * dt)
    lbi = mag * jnp.sin(li * dt)
    den = lr * lr + li * li
    nr = lbr - 1.0
    fr = (nr * lr + lbi * li) / den
    fi = (lbi * lr - nr * li) / den
    bbr = fr[:, None, :] * bre - fi[:, None, :] * bim
    bbi = fr[:, None, :] * bim + fi[:, None, :] * bre
    return lbr, lbi, bbr, bbi


def _s5_prep(lr, li, ldt, bre, bim):
    def body(lr_ref, li_ref, ldt_ref, bre_ref, bim_ref, o1, o2, o3, o4):
        o1[...], o2[...], o3[...], o4[...] = _s5_disc(lr_ref[...], li_ref[...], ldt_ref[...], bre_ref[...], bim_ref[...])

    return pl.pallas_call(
        body, name="s5_prep", in_specs=[VMEM_SPEC] * 5, out_specs=[VMEM_SPEC] * 4,
        out_shape=[jax.ShapeDtypeStruct(lr.shape, f32), jax.ShapeDtypeStruct(lr.shape, f32),
                   jax.ShapeDtypeStruct(bre.shape, f32), jax.ShapeDtypeStruct(bre.shape, f32)],
    )(lr, li, ldt, bre, bim)


def _s5_prep_bwd(lr, li, ldt, bre, bim, cts):
    def body(lr_ref, li_ref, ldt_ref, bre_ref, bim_ref, g1, g2, g3, g4, o1, o2, o3, o4, o5):
        _, vjp = jax.vjp(_s5_disc, lr_ref[...], li_ref[...], ldt_ref[...], bre_ref[...], bim_ref[...])
        o1[...], o2[...], o3[...], o4[...], o5[...] = vjp((g1[...], g2[...], g3[...], g4[...]))

    return pl.pallas_call(
        body, name="s5_prep_bwd", in_specs=[VMEM_SPEC] * 9, out_specs=[VMEM_SPEC] * 5,
        out_shape=[jax.ShapeDtypeStruct(v.shape, f32) for v in (lr, li, ldt, bre, bim)],
    )(lr, li, ldt, bre, bim, *cts)


def _cmul(ar, ai, br, bi):
    return ar * br - ai * bi, ar * bi + ai * br


def _s5_pow_table(lr, li, n, row, up):
    ar = jnp.broadcast_to(lr, (n, lr.shape[1]))
    ai = jnp.broadcast_to(li, (n, li.shape[1]))
    shift = _shift_up if up else _shift_dn
    d = 1
    while d < n:
        ar, ai = _cmul(ar, ai, shift(ar, d, row, 1.0), shift(ai, d, row, 0.0))
        d *= 2
    return ar, ai


def _s5_local_scan(br, bi, lr, li, row, up):
    shift = _shift_up if up else _shift_dn
    pr, pi = lr, li
    d = 1
    while d < br.shape[0]:
        sr, si = _cmul(pr, pi, shift(br, d, row), shift(bi, d, row))
        br, bi = br + sr, bi + si
        pr, pi = _cmul(pr, pi, pr, pi)
        d *= 2
    return br, bi


def _s5_specs(s, nc, order):
    def im(f):
        return (lambda b, k: f(b, k)) if order == "bk" else (lambda k, b: f(b, k))
    seq = pl.BlockSpec((s, 128), im(lambda b, k: (b, k)))
    lvec = pl.BlockSpec((1, S5_BLOCK_STATES), im(lambda b, k: (0, k)))
    dvec = pl.BlockSpec((1, 128), im(lambda b, k: (0, k)))
    wmat = pl.BlockSpec((1, 128, S5_BLOCK_STATES), im(lambda b, k: (k, 0, 0)))
    h0 = pl.BlockSpec((1, nc, 2, S5_BLOCK_STATES), im(lambda b, k: (b * S5_BLOCKS + k, 0, 0, 0)))
    return seq, lvec, dvec, wmat, h0


def _s5_fwd(u, lbr, lbi, wbr, wbi, wcr, wci, dskip, nb, s):
    ln = S5_CHUNK
    nc = s // ln

    def body(u_ref, lr_ref, li_ref, wbr_ref, wbi_ref, wcr_ref, wci_ref, d_ref, yg_ref, y_ref, h0_ref):
        row = lax.broadcasted_iota(jnp.int32, (ln, S5_BLOCK_STATES), 0)
        lr, li = lr_ref[...], li_ref[...]
        pr, pi = _s5_pow_table(lr, li, ln, row, False)
        dv = d_ref[...]

        def step(n, carry):
            h0r, h0i = carry
            st = pl.multiple_of(n * ln, ln)
            uc = u_ref[pl.ds(st, ln), :]
            ub = uc.astype(bf16)
            br, bi = _s5_local_scan(_nn(ub, wbr_ref[0]), _nn(ub, wbi_ref[0]), lr, li, row, False)
            cr, ci = _cmul(pr, pi, h0r, h0i)
            hr, hi = br + cr, bi + ci
            h0_ref[0, n, 0:1, :] = h0r
            h0_ref[0, n, 1:2, :] = h0i
            y = _nt(hr.astype(bf16), wcr_ref[0]) - _nt(hi.astype(bf16), wci_ref[0]) + dv * uc
            y_ref[pl.ds(st, ln), :] = y
            yg_ref[pl.ds(st, ln), :] = jax.nn.gelu(y).astype(bf16)
            return hr[ln - 1:ln, :], hi[ln - 1:ln, :]

        z = jnp.zeros((1, S5_BLOCK_STATES), f32)
        lax.fori_loop(0, nc, step, (z, z))

    seq, lvec, dvec, wmat, h0 = _s5_specs(s, nc, "bk")
    t = nb * s
    return pl.pallas_call(
        body, name="s5_fwd", grid=(nb, S5_BLOCKS),
        in_specs=[seq, lvec, lvec, wmat, wmat, wmat, wmat, dvec],
        out_specs=[seq, seq, h0],
        out_shape=[jax.ShapeDtypeStruct((t, D_MODEL), bf16), jax.ShapeDtypeStruct((t, D_MODEL), f32),
                   jax.ShapeDtypeStruct((nb * S5_BLOCKS, nc, 2, S5_BLOCK_STATES), f32)],
        compiler_params=_cp("parallel", "parallel"),
    )(u, lbr, lbi, wbr, wbi, wcr, wci, dskip)


def _s5_bwd(dyg, y, u, h0, lbr, lbi, wbr, wbi, wcr, wci, dskip, nb, s):
    ln = S5_CHUNK
    nc = s // ln

    def body(dyg_ref, y_ref, u_ref, h0_ref, lr_ref, li_ref, wbr_ref, wbi_ref, wcr_ref, wci_ref, d_ref,
             du_ref, dlr_ref, dli_ref, dwbr_ref, dwbi_ref, dwcr_ref, dwci_ref, dd_ref):
        @pl.when(pl.program_id(1) == 0)
        def _():
            for r in (dlr_ref, dli_ref, dwbr_ref, dwbi_ref, dwcr_ref, dwci_ref, dd_ref):
                r[...] = jnp.zeros_like(r)
        row = lax.broadcasted_iota(jnp.int32, (ln, S5_BLOCK_STATES), 0)
        lr, li = lr_ref[...], li_ref[...]
        pr, pi = _s5_pow_table(lr, li, ln, row, False)
        qr, qi = _s5_pow_table(lr, -li, ln, row, True)
        dv = d_ref[...]
        rsum = lambda v: jnp.sum(v, axis=0, keepdims=True)

        def step(i, carry):
            gnr, gni = carry
            n = nc - 1 - i
            st = pl.multiple_of(n * ln, ln)
            uc = u_ref[pl.ds(st, ln), :]
            ub = uc.astype(bf16)
            h0v = h0_ref[0, n]
            h0r, h0i = h0v[0:1], h0v[1:2]
            br, bi = _s5_local_scan(_nn(ub, wbr_ref[0]), _nn(ub, wbi_ref[0]), lr, li, row, False)
            cr, ci = _cmul(pr, pi, h0r, h0i)
            hr, hi = br + cr, bi + ci
            dy = jax.vjp(jax.nn.gelu, y_ref[pl.ds(st, ln), :])[1](dyg_ref[pl.ds(st, ln), :])[0]
            dyb = dy.astype(bf16)
            dd_ref[...] += rsum(dy * uc)
            gr, gi = _s5_local_scan(_nn(dyb, wcr_ref[0]), -_nn(dyb, wci_ref[0]), lr, -li, row, True)
            cr, ci = _cmul(qr, qi, gnr, gni)
            gr, gi = gr + cr, gi + ci
            hpr = jnp.where(row >= 1, pltpu.roll(hr, 1, 0), h0r)
            hpi = jnp.where(row >= 1, pltpu.roll(hi, 1, 0), h0i)
            dlr_ref[...] += rsum(gr * hpr + gi * hpi)
            dli_ref[...] += rsum(gi * hpr - gr * hpi)
            grb, gib = gr.astype(bf16), gi.astype(bf16)
            dwbr_ref[0] += _tn(ub, grb)
            dwbi_ref[0] += _tn(ub, gib)
            dwcr_ref[0] += _tn(dyb, hr.astype(bf16))
            dwci_ref[0] -= _tn(dyb, hi.astype(bf16))
            du_ref[pl.ds(st, ln), :] = _nt(grb, wbr_ref[0]) + _nt(gib, wbi_ref[0]) + dv * dy
            return gr[0:1, :], gi[0:1, :]

        z = jnp.zeros((1, S5_BLOCK_STATES), f32)
        lax.fori_loop(0, nc, step, (z, z))

    seq, lvec, dvec, wmat, h0s = _s5_specs(s, nc, "kb")
    t = nb * s
    lshape = jax.ShapeDtypeStruct((1, S5_BLOCKS * S5_BLOCK_STATES), f32)
    wshape = jax.ShapeDtypeStruct((S5_BLOCKS, 128, S5_BLOCK_STATES), f32)
    return pl.pallas_call(
        body, name="s5_bwd", grid=(S5_BLOCKS, nb),
        in_specs=[seq, seq, seq, h0s, lvec, lvec, wmat, wmat, wmat, wmat, dvec],
        out_specs=[seq, lvec, lvec, wmat, wmat, wmat, wmat, dvec],
        out_shape=[jax.ShapeDtypeStruct((t, D_MODEL), f32), lshape, lshape, wshape, wshape, wshape, wshape,
                   jax.ShapeDtypeStruct((1, D_MODEL), f32)],
        compiler_params=_cp("parallel", "arbitrary"),
    )(dyg, y, u, h0, lbr, lbi, wbr, wbi, wcr, wci, dskip)


def _blockdiag(w):
    w4 = w.reshape(S5_BLOCKS, 8, S5_GROUP, S5_STATE)
    same_group = jnp.eye(8, dtype=bool)[None, :, None, :, None]
    return jnp.where(same_group, w4[:, :, :, None, :], 0.0).reshape(S5_BLOCKS, 128, S5_BLOCK_STATES)


def _blockdiag_t(dw):
    d5 = dw.reshape(S5_BLOCKS, 8, S5_GROUP, 8, S5_STATE)
    diag = jnp.diagonal(d5, axis1=1, axis2=3)
    return jnp.moveaxis(diag, 3, 1).reshape(S5_GROUPS, S5_GROUP, S5_STATE)


def _glu_fwd(ygb, wa, wb, x, tm=512, tn=512):
    t, d = x.shape

    def body(y_ref, wa_ref, wb_ref, x_ref, o_ref, p_ref, q_ref):
        p = _nn(y_ref[...], wa_ref[...])
        q = _nn(y_ref[...], wb_ref[...])
        p_ref[...] = p
        q_ref[...] = q
        o_ref[...] = x_ref[...] + p * jax.nn.sigmoid(q)

    tile = pl.BlockSpec((tm, tn), lambda i, j: (i, j))
    wsp = pl.BlockSpec((d, tn), lambda i, j: (0, j))
    out = jax.ShapeDtypeStruct((t, d), f32)
    return pl.pallas_call(
        body, name="glu_fwd", grid=(t // tm, d // tn),
        in_specs=[pl.BlockSpec((tm, d), lambda i, j: (i, 0)), wsp, wsp, tile],
        out_specs=[tile, tile, tile], out_shape=[out, out, out],
        compiler_params=_cp("parallel", "parallel"),
    )(ygb, wa, wb, x)


def _place():
    x, y, c = lax.axis_index("x"), lax.axis_index("y"), lax.axis_index("c")
    return x, y, c, [(1 - x, y), (x, 1 - y), (1 - x, 1 - y)]


def _all_gather(name, arrays):
    n = len(arrays)

    def body(*refs):
        ins, outs = refs[:n], refs[n:2 * n]
        send_sems, recv_sems, local_sems = refs[2 * n:]
        x, y, c, chips = _place()
        me, sib = (x, y, c), (x, y, 1 - c)

        def copy(i, k, block, to, src=None):
            dst = outs[i].at[4 * block[0] + 2 * block[1] + block[2]]
            return pltpu.make_async_remote_copy(
                src_ref=dst if src is None else src, dst_ref=dst,
                send_sem=send_sems.at[i * 7 + k], recv_sem=recv_sems.at[i * 7 + k],
                device_id=to, device_id_type=MESH)

        mine = [pltpu.make_async_copy(ins[i], outs[i].at[4 * x + 2 * y + c], local_sems.at[i]) for i in range(n)]
        for m in mine:
            m.start()
        first = []
        for i in range(n):
            first.append(copy(i, 0, me, sib, src=ins[i]))
            first += [copy(i, 1 + j, me, (*chip, c), src=ins[i]) for j, chip in enumerate(chips)]
        for cp in first:
            cp.start()
        passed = []
        for j, chip in enumerate(chips):
            for i in range(n):
                copy(i, 1 + j, (*chip, c), me).wait_recv()
                fwd = copy(i, 4 + j, (*chip, c), sib)
                fwd.start()
                passed.append(fwd)
        for i in range(n):
            copy(i, 0, sib, me).wait_recv()
        for j, chip in enumerate(chips):
            for i in range(n):
                copy(i, 4 + j, (*chip, 1 - c), me).wait_recv()
        for cp in first + passed:
            cp.wait_send()
        for m in mine:
            m.wait()

    return pl.pallas_call(
        body, name=name,
        in_specs=[ANY_SPEC] * n, out_specs=[ANY_SPEC] * n,
        out_shape=[jax.ShapeDtypeStruct((N_DEV,) + a.shape, a.dtype) for a in arrays],
        scratch_shapes=[pltpu.SemaphoreType.DMA((7 * n,)), pltpu.SemaphoreType.DMA((7 * n,)),
                        pltpu.SemaphoreType.DMA((n,))],
    )(*arrays)


def _place_own(name, arrays):
    n = len(arrays)

    def body(*refs):
        ins, outs, sems = refs[:n], refs[n:2 * n], refs[2 * n]
        x, y, c, _ = _place()
        copies = [pltpu.make_async_copy(ins[i], outs[i].at[4 * x + 2 * y + c], sems.at[i]) for i in range(n)]
        for cp in copies:
            cp.start()
        for cp in copies:
            cp.wait()

    return pl.pallas_call(
        body, name=name, in_specs=[ANY_SPEC] * n, out_specs=[ANY_SPEC] * n,
        out_shape=[jax.ShapeDtypeStruct((N_DEV,) + a.shape, a.dtype) for a in arrays],
        scratch_shapes=[pltpu.SemaphoreType.DMA((n,))],
    )(*arrays)


def _xchg_copies(kind, srcs, lands, suffixes, send_sems, recv_sems):
    x, y, c, _ = _place()
    copies = []
    for i, (src, land, sfx) in enumerate(zip(srcs, lands, suffixes)):
        for k in range(N_DEV - 1):
            r = k + 1
            peer = (1 - x if r & 4 else x, 1 - y if r & 2 else y, 1 - c if r & 1 else c)
            if kind == "gather":
                s_ref, d_ref = src, land.at[(4 * x + 2 * y + c,) + sfx]
            else:
                s_ref, d_ref = src.at[4 * peer[0] + 2 * peer[1] + peer[2]], land.at[(k,) + sfx]
            copies.append(pltpu.make_async_remote_copy(
                src_ref=s_ref, dst_ref=d_ref, send_sem=send_sems.at[i * 7 + k], recv_sem=recv_sems.at[i * 7 + k],
                device_id=peer, device_id_type=MESH))
    return copies


def _xchg_start(name, kind, srcs, lands, suffixes=None):
    n = len(srcs)
    suffixes = suffixes or [()] * n

    def body(*refs):
        src, land = refs[:n], refs[n:2 * n]
        send_sems, recv_sems, token = refs[2 * n], refs[2 * n + 1], refs[-1]
        for cp in _xchg_copies(kind, src, land, suffixes, send_sems, recv_sems):
            cp.start()
        token[...] = jnp.zeros_like(token)

    arrays = list(srcs) + list(lands)
    outs = pl.pallas_call(
        body, name=name,
        out_shape=(pltpu.SemaphoreType.DMA((7 * n,)), pltpu.SemaphoreType.DMA((7 * n,)),
                   *[pltpu.HBM(a.shape, a.dtype) for a in arrays], jax.ShapeDtypeStruct((8, 128), f32)),
        in_specs=[HBM_SPEC] * (2 * n),
        out_specs=(SEM_SPEC, SEM_SPEC, *[HBM_SPEC] * (2 * n), VMEM_SPEC),
        input_output_aliases={i: 2 + i for i in range(2 * n)},
        compiler_params=pltpu.CompilerParams(has_side_effects=SIDE_EFFECT),
    )(*[pltpu.with_memory_space_constraint(a, pltpu.HBM) for a in arrays])
    return dict(kind=kind, n=n, suffixes=suffixes, send=outs[0], recv=outs[1], srcs=list(outs[2:2 + n]),
                lands=list(outs[2 + n:2 + 2 * n]), token=outs[-1])


def _xchg_wait(name, h, after, lands=None):
    n = h["n"]
    lands = h["lands"] if lands is None else lands

    def body(*refs):
        src, land = refs[:n], refs[n:2 * n]
        for cp in _xchg_copies(h["kind"], src, land, h["suffixes"], refs[2 * n], refs[2 * n + 1]):
            cp.wait_send()
            cp.wait_recv()

    arrays = h["srcs"] + list(lands)
    outs = pl.pallas_call(
        body, name=name,
        out_shape=tuple(pltpu.HBM(a.shape, a.dtype) for a in arrays),
        in_specs=[HBM_SPEC] * (2 * n) + [SEM_SPEC, SEM_SPEC] + [ANY_SPEC] * len(after),
        out_specs=tuple([HBM_SPEC] * (2 * n)),
        input_output_aliases={i: i for i in range(2 * n)},
        compiler_params=pltpu.CompilerParams(has_side_effects=SIDE_EFFECT),
    )(*arrays, h["send"], h["recv"], *after)
    return list(outs[n:])


def _rows(a):
    return a.reshape(-1, a.shape[-1])


def _row_tile(r):
    for tm in (512, 256, 128, 64, 32, 16, 8):
        if r % tm == 0:
            return tm
    return r


def _sum8(name, gathered):
    _, r, n = gathered.shape
    tm = _row_tile(r)

    def body(g_ref, o_ref):
        acc = g_ref[0]
        for k in range(1, N_DEV):
            acc = acc + g_ref[k]
        o_ref[...] = acc

    return pl.pallas_call(
        body, name=name, grid=(r // tm,),
        in_specs=[pl.BlockSpec((N_DEV, tm, n), lambda i: (0, i, 0))],
        out_specs=pl.BlockSpec((tm, n), lambda i: (i, 0)),
        out_shape=jax.ShapeDtypeStruct((r, n), f32),
        compiler_params=_cp("parallel"),
    )(gathered)


def _adamw(name, w, m, v, own, landed=None):
    shape = w.shape
    w2, m2, v2, o2 = _rows(w), _rows(m), _rows(v), _rows(own)
    r, n = w2.shape
    tm = _row_tile(r)
    c1 = 1.0 - ADAM_B1 ** ADAM_STEP
    c2 = 1.0 - ADAM_B2 ** ADAM_STEP
    extra = [] if landed is None else [landed.reshape(landed.shape[0], r, n)]

    def body(w_ref, m_ref, v_ref, o_ref, *refs):
        g = o_ref[...]
        if extra:
            for k in range(extra[0].shape[0]):
                g = g + refs[0][k].astype(f32)
        g_ref, d_ref, mn_ref, vn_ref = refs[len(extra):]
        mn = ADAM_B1 * m_ref[...] + (1.0 - ADAM_B1) * g
        vn = ADAM_B2 * v_ref[...] + (1.0 - ADAM_B2) * (g * g)
        g_ref[...] = g
        d_ref[...] = -ADAM_LR * ((mn / c1) / (jnp.sqrt(vn / c2) + ADAM_EPS) + ADAM_WD * w_ref[...])
        mn_ref[...] = mn
        vn_ref[...] = vn

    row = pl.BlockSpec((tm, n), lambda i: (i, 0))
    outs = pl.pallas_call(
        body, name=name, grid=(r // tm,),
        in_specs=[row] * 4 + [pl.BlockSpec((e.shape[0], tm, n), lambda i: (0, i, 0)) for e in extra],
        out_specs=[row] * 4, out_shape=[jax.ShapeDtypeStruct((r, n), f32)] * 4,
        compiler_params=_cp("parallel"),
    )(w2, m2, v2, o2, *extra)
    return [o.reshape(shape) for o in outs]


def _pack(arrays):
    flat = jnp.concatenate([a.reshape(-1).astype(f32) for a in arrays])
    pad = (-flat.shape[0]) % (128 * (512 if flat.shape[0] > 128 * 512 else 8))
    return jnp.pad(flat, (0, pad)).reshape(-1, 128)


def _unpack(packed, shapes):
    flat = packed.reshape(-1)
    out, off = [], 0
    for s in shapes:
        n = math.prod(s)
        out.append(flat[off:off + n].reshape(s))
        off += n
    return out


def _local_step(x, target, w, weights_of, send, nb, s):
    cos, sin = _rope_tables(s)
    g = {}
    ffn_saved = {}
    ffn_bufs = [lax.empty((N_DEV, 2, 2) + shp, f32)
                for shp in ((D_MODEL, FF_SHARD), (D_MODEL, FF_SHARD), (FF_SHARD, D_MODEL))]

    def ffn(xin, l, h, wts):
        y, a, b = _ffn_fwd(f"ffn_fwd_{l}{h}", xin, w["ffn_g"][l][h], *wts)
        ffn_saved[(l, h)] = (xin, a, b, wts)
        return y

    def ffn_back(dy, l, h):
        xin, a, b, wts = ffn_saved[(l, h)]
        dx, dg, hb, dyh, u, da, db = _ffn_dx(f"ffn_dx_{l}{h}", dy, xin, w["ffn_g"][l][h], *wts, a, b)
        ffn_bufs[:], halves = _ffn_dw(f"ffn_dw_{l}{h}", hb, dyh, u, da, db, ffn_bufs, l, h)
        g[f"ffn_g_{l}{h}"] = dg
        return send(f"ffn_{l}{h}", dict(zip(("ffn_w1", "ffn_w3", "ffn_w2"), halves)), dx)

    def slots(t):
        return t.reshape(N_DEV, D_MODEL // N_DEV, D_MODEL)

    x1 = ffn(x, 0, 0, weights_of(0, [])["ffn"])
    wg = weights_of(1, [x1])
    w_in, w_out = wg["w_in"], wg["w_out"]
    _, h0b = _norm_fwd("mix_norm_0", x1, w["mix_g"][0])
    proj = _mm("in_proj", h0b, w_in, "nn", tn=768)[0]
    o_raw, rprev, mret = _ret_fwd(proj, cos, sin, w["ret_g"], nb, s)
    lru = _lru_fwd(proj, w["conv_w"], w["conv_b"], w["lru_w_a"], w["lru_b_a"], w["lru_w_i"], w["lru_b_i"], w["lru_lam"], nb, s)
    merged = _ew("merge", lambda a, b: (jnp.concatenate([a, b], axis=1),), [mret, lru], [(D_MODEL, bf16)])[0]
    x2 = _mm("out_proj", merged, w_out, "nn", extras=[x1], epilogue=lambda acc, r: (acc + r,))[0]
    x3 = ffn(x2, 0, 1, wg["ffn"])
    wg = weights_of(2, [x3])
    glu_a, glu_b = wg["glu_a"], wg["glu_b"]
    x4 = ffn(x3, 1, 0, wg["ffn"])
    u, _ = _norm_fwd("mix_norm_1", x4, w["mix_g"][1])
    lbr, lbi, bbr, bbi = _s5_prep(w["s5_lr"], w["s5_li"], w["s5_ldt"], w["s5_bre"], w["s5_bim"])
    lbr_f, lbi_f = lbr.reshape(1, -1), lbi.reshape(1, -1)
    wbr, wbi = _blockdiag(bbr).astype(bf16), _blockdiag(bbi).astype(bf16)
    wcr, wci = _blockdiag(w["s5_cre"]).astype(bf16), _blockdiag(w["s5_cim"]).astype(bf16)
    ygb, ypre, h0s = _s5_fwd(u, lbr_f, lbi_f, wbr, wbi, wcr, wci, w["s5_d"], nb, s)
    x5, gp, gq = _glu_fwd(ygb, glu_a, glu_b, x4)
    x6 = ffn(x5, 1, 1, weights_of(3, [x5])["ffn"])
    loss, dx6, g["final_g"] = _final_loss(x6, w["final_g"], target)

    dx5 = ffn_back(dx6, 1, 1)

    def glu_bwd(d, p, q):
        sg = jax.nn.sigmoid(q)
        return d * sg, d * p * sg * (1.0 - sg)

    dp, dq = _ew("glu_bwd", glu_bwd, [dx5, gp, gq], [(D_MODEL, bf16), (D_MODEL, bf16)])
    dyg = _mm("glu_dy_a", dp, glu_a, "nt")[0]
    dyg = _mm("glu_dy_b", dq, glu_b, "nt", extras=[dyg], epilogue=lambda acc, r: (acc + r,))[0]
    g["glu_a"], ga_half = _mm_tn("glu_dw_a", ygb, dp)
    g["glu_b"], gb_half = _mm_tn("glu_dw_b", ygb, dq)
    dyg = send("glu", {"glu_a": slots(ga_half), "glu_b": slots(gb_half)}, dyg)
    du, dlr, dli, dwbr, dwbi, dwcr, dwci, g["s5_d"] = _s5_bwd(dyg, ypre, u, h0s, lbr_f, lbi_f, wbr, wbi, wcr, wci, w["s5_d"], nb, s)
    g["s5_cre"], g["s5_cim"] = _blockdiag_t(dwcr), _blockdiag_t(dwci)
    g["s5_lr"], g["s5_li"], g["s5_ldt"], g["s5_bre"], g["s5_bim"] = _s5_prep_bwd(
        w["s5_lr"], w["s5_li"], w["s5_ldt"], w["s5_bre"], w["s5_bim"],
        (dlr.reshape(S5_GROUPS, S5_STATE), dli.reshape(S5_GROUPS, S5_STATE), _blockdiag_t(dwbr), _blockdiag_t(dwbi)))
    dx4, g["mix_g_1"] = _norm_bwd("mix_norm_1_bwd", du, x4, w["mix_g"][1], dx5)
    dx3 = ffn_back(dx4, 1, 0)
    dx2 = ffn_back(dx3, 0, 1)
    dmerged = _mm("out_proj_dx", dx2, w_out, "nt")[0]
    g["w_out"], wo_half = _mm_tn("out_proj_dw", merged, dx2)
    dmerged = send("w_out", {"w_out": slots(wo_half)}, dmerged)
    dq_, dk_, dv_, dgate, g["ret_g"] = _ret_bwd(dmerged, o_raw, rprev, proj, cos, sin, w["ret_g"], nb, s)
    (dxl, dgl, g["conv_w"], g["conv_b"], g["lru_w_a"], g["lru_b_a"], g["lru_w_i"], g["lru_b_i"], g["lru_lam"]) = _lru_bwd(
        dmerged, proj, w["conv_w"], w["conv_b"], w["lru_w_a"], w["lru_b_a"], w["lru_w_i"], w["lru_b_i"], w["lru_lam"], nb, s)
    dproj = _ew("dproj", lambda *p: (jnp.concatenate(p, axis=1),), [dq_, dk_, dv_, dgate, dxl, dgl], [(3072, bf16)])[0]
    dh0 = _mm("in_proj_dx", dproj, w_in, "nt")[0]
    g["w_in"], wi_half = _mm_tn("in_proj_dw", h0b, dproj)
    dh0 = send("w_in", {"w_in": jnp.transpose(wi_half.reshape(D_MODEL, N_DEV, IN_SHARD), (1, 0, 2))}, dh0)
    dx1, g["mix_g_0"] = _norm_bwd("mix_norm_0_bwd", dh0, x1, w["mix_g"][0], dx2)
    dx0 = ffn_back(dx1, 0, 0)
    g["ffn_w1"], g["ffn_w3"], g["ffn_w2"] = ffn_bufs
    return loss, dx0, g


_WEIGHTS = ["ffn_norm_g", "ffn_w1", "ffn_w3", "ffn_w2", "mix_norm_g", "w_in_even", "w_out_even", "ret_norm_g", "conv_w",
            "conv_b", "lru_w_a", "lru_b_a", "lru_w_i", "lru_b_i", "lru_lambda", "s5_lambda_re", "s5_lambda_im", "s5_log_dt",
            "s5_b_re", "s5_b_im", "s5_c_re", "s5_c_im", "s5_d", "glu_w_a", "glu_w_b", "final_norm_g"]
_BIG = ["ffn_w1", "ffn_w3", "ffn_w2", "w_in_even", "w_out_even", "glu_w_a", "glu_w_b"]
_SMALL_SHARDED = ["ffn_norm_g", "conv_w", "s5_d"]
_SMALL = [n for n in _WEIGHTS if n not in _BIG]


def kernel(x, ffn_norm_g, ffn_w1, ffn_w3, ffn_w2, mix_norm_g, w_in_even, w_out_even, ret_norm_g, conv_w, conv_b, lru_w_a, lru_b_a, lru_w_i, lru_b_i, lru_lambda, s5_lambda_re, s5_lambda_im, s5_log_dt, s5_b_re, s5_b_im, s5_c_re, s5_c_im, s5_d, glu_w_a, glu_w_b, final_norm_g, loss_target, m_ffn_norm_g, m_ffn_w1, m_ffn_w3, m_ffn_w2, m_mix_norm_g, m_w_in_even, m_w_out_even, m_ret_norm_g, m_conv_w, m_conv_b, m_lru_w_a, m_lru_b_a, m_lru_w_i, m_lru_b_i, m_lru_lambda, m_s5_lambda_re, m_s5_lambda_im, m_s5_log_dt, m_s5_b_re, m_s5_b_im, m_s5_c_re, m_s5_c_im, m_s5_d, m_glu_w_a, m_glu_w_b, m_final_norm_g, v_ffn_norm_g, v_ffn_w1, v_ffn_w3, v_ffn_w2, v_mix_norm_g, v_w_in_even, v_w_out_even, v_ret_norm_g, v_conv_w, v_conv_b, v_lru_w_a, v_lru_b_a, v_lru_w_i, v_lru_b_i, v_lru_lambda, v_s5_lambda_re, v_s5_lambda_im, v_s5_log_dt, v_s5_b_re, v_s5_b_im, v_s5_c_re, v_s5_c_im, v_s5_d, v_glu_w_a, v_glu_w_b, v_final_norm_g):
    a = dict(locals())
    nb, s, d = x.shape
    ax, ay, ac = lax.axis_index("x"), lax.axis_index("y"), lax.axis_index("c")
    dev = 4 * ax + 2 * ay + ac
    chip = 2 * ax + ay

    (sm,) = _all_gather("ag_small_weights", [_pack([ffn_norm_g, conv_w, s5_d])])
    sm = sm.reshape(N_DEV, -1)
    ffn_g_full = jnp.transpose(sm[:, :512].reshape(N_DEV, 2, 2, 128), (1, 2, 0, 3)).reshape(2, 2, D_MODEL)
    conv_w_full = jnp.transpose(sm[:, 512:768].reshape(N_DEV, 4, 64), (1, 0, 2)).reshape(4, LRU_WIDTH)
    s5_d_full = sm[:, 768:896].reshape(1, D_MODEL)

    def ffn_shards(l, h):
        return [ffn_w1[l, h].astype(bf16), ffn_w3[l, h].astype(bf16), ffn_w2[l, h].astype(bf16)]

    ag_src = [ffn_shards(0, 0),
              [w_in_even[0].astype(bf16), w_out_even[0].astype(bf16)] + ffn_shards(0, 1),
              ffn_shards(1, 0) + [glu_w_a[0].astype(bf16), glu_w_b[0].astype(bf16)],
              ffn_shards(1, 1)]
    ag_lands = _place_own("place_weights", [t for grp in ag_src for t in grp])
    ag, token, off = [], None, 0
    for k, grp in enumerate(ag_src):
        if token is not None:
            grp, _ = lax.optimization_barrier((grp, token))
        ag.append(_xchg_start(f"ag_start_{k}", "gather", grp, ag_lands[off:off + len(grp)]))
        token = ag[-1]["token"]
        off += len(grp)

    def weights_of(k, after):
        after = after if k else [h["token"] for h in ag]
        got = _xchg_wait(f"ag_wait_{k}", ag[k], after)
        if k == 1:
            return {"w_in": jnp.transpose(got[0], (1, 0, 2)).reshape(D_MODEL, N_DEV * IN_SHARD),
                    "w_out": got[1].reshape(D_MODEL, D_MODEL), "ffn": got[2:]}
        if k == 2:
            return {"ffn": got[:3], "glu_a": got[3].reshape(D_MODEL, D_MODEL), "glu_b": got[4].reshape(D_MODEL, D_MODEL)}
        return {"ffn": got}

    ffn_lands = [lax.empty((N_DEV - 1, 2, 2) + shp, bf16)
                 for shp in ((D_MODEL, FF_SHARD), (D_MODEL, FF_SHARD), (FF_SHARD, D_MODEL))]
    rs = []

    def send(group, arrays, carry):
        srcs = list(arrays.values())
        if group.startswith("ffn_"):
            sfx = [(int(group[4]), int(group[5]))] * 3
            h = _xchg_start("rs_start_" + group, "scatter", srcs, ffn_lands, sfx)
            ffn_lands[:] = h["lands"]
        else:
            h = _xchg_start("rs_start_" + group, "scatter", srcs,
                            [lax.empty((N_DEV - 1,) + t.shape[1:], bf16) for t in srcs])
        rs.append((group, list(arrays), h))
        return lax.optimization_barrier((carry, h["token"]))[0]

    w = {
        "ffn_g": [[ffn_g_full[l, h].reshape(1, D_MODEL) for h in range(2)] for l in range(2)],
        "mix_g": [mix_norm_g[0:1], mix_norm_g[1:2]],
        "ret_g": ret_norm_g, "conv_w": conv_w_full, "conv_b": conv_b,
        "lru_w_a": lru_w_a[0], "lru_b_a": lru_b_a, "lru_w_i": lru_w_i[0], "lru_b_i": lru_b_i, "lru_lam": lru_lambda,
        "s5_lr": s5_lambda_re[0], "s5_li": s5_lambda_im[0], "s5_ldt": s5_log_dt.reshape(S5_GROUPS, 1),
        "s5_bre": jnp.swapaxes(s5_b_re[0], 1, 2), "s5_bim": jnp.swapaxes(s5_b_im[0], 1, 2),
        "s5_cre": s5_c_re[0], "s5_cim": s5_c_im[0], "s5_d": s5_d_full,
        "final_g": final_norm_g.reshape(1, D_MODEL),
    }

    loss_part, dx, g = _local_step(x.reshape(nb * s, d), loss_target.reshape(nb * s, d), w, weights_of, send, nb, s)
    loss = lax.psum(loss_part[0, 0], ("x", "y", "c"))

    part = {
        "ffn_norm_g": jnp.stack([jnp.stack([g[f"ffn_g_{l}{h}"][0] for h in range(2)]) for l in range(2)]),
        "mix_norm_g": jnp.concatenate([g["mix_g_0"], g["mix_g_1"]], axis=0),
        "ret_norm_g": g["ret_g"], "conv_w": g["conv_w"][None], "conv_b": g["conv_b"],
        "lru_w_a": g["lru_w_a"][None], "lru_b_a": g["lru_b_a"], "lru_w_i": g["lru_w_i"][None], "lru_b_i": g["lru_b_i"],
        "lru_lambda": g["lru_lam"], "s5_lambda_re": g["s5_lr"][None], "s5_lambda_im": g["s5_li"][None],
        "s5_log_dt": g["s5_ldt"].reshape(1, S5_GROUPS),
        "s5_b_re": jnp.swapaxes(g["s5_bre"], 1, 2)[None], "s5_b_im": jnp.swapaxes(g["s5_bim"], 1, 2)[None],
        "s5_c_re": g["s5_cre"][None], "s5_c_im": g["s5_cim"][None], "s5_d": g["s5_d"], "final_norm_g": g["final_g"][0],
    }
    (gath,) = _all_gather("ag_small_grads", [_pack([part[n] for n in _SMALL])])
    full = dict(zip(_SMALL, _unpack(_sum8("sum_small_grads", gath), [part[n].shape for n in _SMALL])))
    for n in _SMALL_SHARDED:
        width = a[n].shape[-1]
        full[n] = lax.dynamic_slice_in_dim(full[n], dev * width, width, axis=full[n].ndim - 1)
    shapes = [a[n].shape for n in _SMALL]
    packed = _adamw("adamw_small", _pack([a[n] for n in _SMALL]), _pack([a["m_" + n] for n in _SMALL]),
                    _pack([a["v_" + n] for n in _SMALL]), _pack([full[n] for n in _SMALL]))
    res = {n: vals for n, vals in zip(_SMALL, zip(*[_unpack(p, shapes) for p in packed]))}

    landed = {}
    for group, names, h in rs:
        if group.startswith("ffn_"):
            ffn_lands[:] = _xchg_wait("rs_wait_" + group, h, [dx], ffn_lands)
        else:
            landed.update(zip(names, _xchg_wait("rs_wait_" + group, h, [dx])))
    landed.update(zip(("ffn_w1", "ffn_w3", "ffn_w2"), ffn_lands))
    own = {n: lax.dynamic_index_in_dim(g[n], dev, axis=0, keepdims=False) for n in ("ffn_w1", "ffn_w3", "ffn_w2")}
    own["w_in"] = lax.dynamic_slice_in_dim(g["w_in"], dev * IN_SHARD, IN_SHARD, axis=1)
    for n in ("w_out", "glu_a", "glu_b"):
        own[n] = lax.dynamic_slice_in_dim(g[n], dev * (D_MODEL // N_DEV), D_MODEL // N_DEV, axis=0)
    for n, short in zip(_BIG, ("ffn_w1", "ffn_w3", "ffn_w2", "w_in", "w_out", "glu_a", "glu_b")):
        res[n] = _adamw("adamw_" + n, a[n], a["m_" + n], a["v_" + n], own[short].reshape(a[n].shape),
                        landed[short].reshape((N_DEV - 1,) + a[n].shape))

    out = [loss, dx.reshape(nb, s, d)]
    for k in range(4):
        out += [res[n][k] for n in _WEIGHTS]
    return tuple(out)
```

```python
import functools
import math

import numpy as np
import jax
import jax.numpy as jnp
from jax import lax
from jax.experimental import pallas as pl
from jax.experimental.pallas import tpu as pltpu

f32 = jnp.float32
bf16 = jnp.bfloat16

D_MODEL = 1024
N_DEV = 8
EPS = 1e-6
RET_HEADS = 4
HEAD_DIM = 128
RET_WIDTH = 512
RET_CHUNK = 128
ROPE_BASE = 10000.0
LRU_WIDTH = 512
LRU_BLOCKS = 4
LRU_C = 8.0
S5_GROUP = 16
S5_GROUPS = 64
S5_STATE = 64
S5_CHUNK = 128
S5_BLOCKS = 8
S5_BLOCK_STATES = 512
D_FF = 2816
FF_SHARD = D_FF // N_DEV
IN_SHARD = 3072 // N_DEV
ADAM_LR = 0.001
ADAM_B1 = 0.9
ADAM_B2 = 0.999
ADAM_EPS = 1e-08
ADAM_WD = 0.01
ADAM_STEP = 10

VMEM_LIMIT = 48 * 1024 * 1024
VMEM_SPEC = pl.BlockSpec(memory_space=pltpu.VMEM)
ANY_SPEC = pl.BlockSpec(memory_space=pl.ANY)
HBM_SPEC = pl.BlockSpec(memory_space=pltpu.HBM)
SEM_SPEC = pl.BlockSpec(memory_space=pltpu.SEMAPHORE)
SIDE_EFFECT = pltpu.SideEffectType.DATAFLOW_SIDE_EFFECTING
MESH = pl.DeviceIdType.MESH


def _cp(*sem):
    return pltpu.CompilerParams(dimension_semantics=sem, vmem_limit_bytes=VMEM_LIMIT)


def _nn(a, b):
    return jnp.dot(a, b, preferred_element_type=f32)


def _nt(a, b):
    return lax.dot_general(a, b, (((1,), (1,)), ((), ())), preferred_element_type=f32)


def _tn(a, b):
    return lax.dot_general(a, b, (((0,), (0,)), ((), ())), preferred_element_type=f32)


def _rms_fwd(x, g):
    r = lax.rsqrt(jnp.mean(x * x, axis=-1, keepdims=True) + EPS)
    xn = x * r
    return xn * g, xn, r


def _rms_bwd(dh, xn, r, g):
    dxn = dh * g
    dx = r * (dxn - xn * jnp.mean(dxn * xn, axis=-1, keepdims=True))
    dg = jnp.sum(dh * xn, axis=0, keepdims=True)
    return dx, dg


def _shift_dn(v, d, row, fill=0.0):
    return jnp.where(row >= d, pltpu.roll(v, d, 0), fill)


def _shift_up(v, d, row, fill=0.0):
    n = v.shape[0]
    return jnp.where(row < n - d, pltpu.roll(v, n - d, 0), fill)


def _ew(name, fn, ins, outs, tm=512):
    t = ins[0].shape[0]
    n_in = len(ins)

    def body(*refs):
        res = fn(*[r[...] for r in refs[:n_in]])
        for o, v in zip(refs[n_in:], res):
            o[...] = v.astype(o.dtype)

    return pl.pallas_call(
        body, name=name, grid=(t // tm,),
        in_specs=[pl.BlockSpec((tm, a.shape[1]), lambda i: (i, 0)) for a in ins],
        out_specs=[pl.BlockSpec((tm, n), lambda i: (i, 0)) for n, _ in outs],
        out_shape=[jax.ShapeDtypeStruct((t, n), dt) for n, dt in outs],
        compiler_params=_cp("parallel"),
    )(*ins)


def _mm(name, x, w, kind, extras=(), epilogue=None, outs=None, tm=512, tn=512):
    t = x.shape[0]
    n = w.shape[1] if kind == "nn" else w.shape[0]
    tn = min(tn, n)
    outs = outs or [f32]
    n_ex = len(extras)

    def body(x_ref, w_ref, *refs):
        xb = x_ref[...].astype(bf16)
        acc = _nn(xb, w_ref[...]) if kind == "nn" else _nt(xb, w_ref[...])
        res = epilogue(acc, *[r[...] for r in refs[:n_ex]]) if epilogue else (acc,)
        for o, v in zip(refs[n_ex:], res):
            o[...] = v.astype(o.dtype)

    w_spec = (pl.BlockSpec((w.shape[0], tn), lambda i, j: (0, j)) if kind == "nn"
              else pl.BlockSpec((tn, w.shape[1]), lambda i, j: (j, 0)))
    tile = pl.BlockSpec((tm, tn), lambda i, j: (i, j))
    return pl.pallas_call(
        body, name=name, grid=(t // tm, n // tn),
        in_specs=[pl.BlockSpec((tm, x.shape[1]), lambda i, j: (i, 0)), w_spec] + [tile] * n_ex,
        out_specs=[tile] * len(outs),
        out_shape=[jax.ShapeDtypeStruct((t, n), dt) for dt in outs],
        compiler_params=_cp("parallel", "parallel"),
    )(x, w, *extras)


def _mm_tn(name, x, y, tk=1024, tn=512, tt=512):
    t, k = x.shape
    n = y.shape[1]
    tk, tn = min(tk, k), min(tn, n)

    def body(x_ref, y_ref, o_ref, ob_ref):
        @pl.when(pl.program_id(2) == 0)
        def _():
            o_ref[...] = jnp.zeros_like(o_ref)
        o_ref[...] += _tn(x_ref[...].astype(bf16), y_ref[...].astype(bf16))

        @pl.when(pl.program_id(2) == pl.num_programs(2) - 1)
        def _():
            ob_ref[...] = o_ref[...].astype(bf16)

    out = pl.BlockSpec((tk, tn), lambda i, j, s: (i, j))
    return pl.pallas_call(
        body, name=name, grid=(k // tk, n // tn, t // tt),
        in_specs=[pl.BlockSpec((tt, tk), lambda i, j, s: (s, i)), pl.BlockSpec((tt, tn), lambda i, j, s: (s, j))],
        out_specs=[out, out],
        out_shape=[jax.ShapeDtypeStruct((k, n), f32), jax.ShapeDtypeStruct((k, n), bf16)],
        compiler_params=_cp("parallel", "parallel", "arbitrary"),
    )(x, y)


def _norm_fwd(name, x, g, tm=512):
    t, d = x.shape

    def body(x_ref, g_ref, h_ref, hb_ref):
        h, _, _ = _rms_fwd(x_ref[...], g_ref[...])
        h_ref[...] = h
        hb_ref[...] = h.astype(bf16)

    row = pl.BlockSpec((tm, d), lambda i: (i, 0))
    return pl.pallas_call(
        body, name=name, grid=(t // tm,),
        in_specs=[row, pl.BlockSpec((1, d), lambda i: (0, 0))],
        out_specs=[row, row],
        out_shape=[jax.ShapeDtypeStruct((t, d), f32), jax.ShapeDtypeStruct((t, d), bf16)],
        compiler_params=_cp("parallel"),
    )(x, g)


def _norm_bwd(name, dh, x, g, dres, tm=512):
    t, d = x.shape

    def body(dh_ref, x_ref, g_ref, dres_ref, dx_ref, dg_ref):
        gv = g_ref[...]
        _, xn, r = _rms_fwd(x_ref[...], gv)
        dx, dg = _rms_bwd(dh_ref[...], xn, r, gv)
        dx_ref[...] = dres_ref[...] + dx

        @pl.when(pl.program_id(0) == 0)
        def _():
            dg_ref[...] = jnp.zeros_like(dg_ref)
        dg_ref[...] += dg

    row = pl.BlockSpec((tm, d), lambda i: (i, 0))
    vec = pl.BlockSpec((1, d), lambda i: (0, 0))
    return pl.pallas_call(
        body, name=name, grid=(t // tm,),
        in_specs=[row, row, vec, row],
        out_specs=[row, vec],
        out_shape=[jax.ShapeDtypeStruct((t, d), f32), jax.ShapeDtypeStruct((1, d), f32)],
        compiler_params=_cp("arbitrary"),
    )(dh, x, g, dres)


def _final_loss(x, g, target, tm=512):
    t, d = x.shape

    def body(x_ref, g_ref, t_ref, loss_ref, dx_ref, dg_ref):
        gv = g_ref[...]
        y, xn, r = _rms_fwd(x_ref[...], gv)
        err = y - t_ref[...]
        dy = err * (1.0 / d)
        dx, dg = _rms_bwd(dy, xn, r, gv)
        dx_ref[...] = dx

        @pl.when(pl.program_id(0) == 0)
        def _():
            dg_ref[...] = jnp.zeros_like(dg_ref)
            loss_ref[...] = jnp.zeros_like(loss_ref)
        dg_ref[...] += dg
        loss_ref[...] += jnp.full((1, 128), 0.5 / d, f32) * jnp.sum(err * err)

    row = pl.BlockSpec((tm, d), lambda i: (i, 0))
    vec = pl.BlockSpec((1, d), lambda i: (0, 0))
    return pl.pallas_call(
        body, name="final_loss", grid=(t // tm,),
        in_specs=[row, vec, row],
        out_specs=[pl.BlockSpec((1, 128), lambda i: (0, 0)), row, vec],
        out_shape=[jax.ShapeDtypeStruct((1, 128), f32), jax.ShapeDtypeStruct((t, d), f32),
                   jax.ShapeDtypeStruct((1, d), f32)],
        compiler_params=_cp("arbitrary"),
    )(x, g, target)


def _load_ffn_weights(hbm_refs, vmem_refs, sems):
    @pl.when(pl.program_id(0) == 0)
    def _():
        copies = [pltpu.make_async_copy(src, dst, sems.at[k])
                  for k, (src, dst) in enumerate(zip(hbm_refs, vmem_refs))]
        for cp in copies:
            cp.start()
        for cp in copies:
            cp.wait()


def _ffn_weight_scratch(nj, d, ff):
    return [pltpu.VMEM((nj, d, ff), bf16), pltpu.VMEM((nj, d, ff), bf16), pltpu.VMEM((nj, ff, d), bf16),
            pltpu.SemaphoreType.DMA((3,))]


def _ffn_fwd(name, x, g, w1, w3, w2, tm=256):
    t, d = x.shape
    nj, ff = w1.shape[0], w1.shape[-1]

    def body(x_ref, g_ref, w1_hbm, w3_hbm, w2_hbm, y_ref, a_ref, b_ref, w1_ref, w3_ref, w2_ref, sems):
        _load_ffn_weights((w1_hbm, w3_hbm, w2_hbm), (w1_ref, w3_ref, w2_ref), sems)
        xv = x_ref[...]
        h, _, _ = _rms_fwd(xv, g_ref[...])
        hb = h.astype(bf16)
        acc = jnp.zeros((tm, d), f32)
        for j in range(nj):
            a = _nn(hb, w1_ref[j])
            b = _nn(hb, w3_ref[j])
            a_ref[j] = a.astype(bf16)
            b_ref[j] = b.astype(bf16)
            u = (a * jax.nn.sigmoid(a) * b).astype(bf16)
            acc = acc + _nn(u, w2_ref[j])
        y_ref[...] = xv + 0.5 * acc

    row = pl.BlockSpec((tm, d), lambda i: (i, 0))
    mid = pl.BlockSpec((nj, tm, ff), lambda i: (0, i, 0))
    return pl.pallas_call(
        body, name=name, grid=(t // tm,),
        in_specs=[row, pl.BlockSpec((1, d), lambda i: (0, 0)), ANY_SPEC, ANY_SPEC, ANY_SPEC],
        out_specs=[row, mid, mid],
        out_shape=[jax.ShapeDtypeStruct((t, d), f32), jax.ShapeDtypeStruct((nj, t, ff), bf16),
                   jax.ShapeDtypeStruct((nj, t, ff), bf16)],
        scratch_shapes=_ffn_weight_scratch(nj, d, ff),
        compiler_params=_cp("arbitrary"),
    )(x, g, w1, w3, w2)


def _ffn_dx(name, dy, x, g, w1, w3, w2, a, b, tm=256):
    t, d = x.shape
    nj, ff = w1.shape[0], w1.shape[-1]

    def body(dy_ref, x_ref, g_ref, w1_hbm, w3_hbm, w2_hbm, a_ref, b_ref,
             dx_ref, dg_ref, hb_ref, dyh_ref, u_ref, da_ref, db_ref, w1_ref, w3_ref, w2_ref, sems):
        _load_ffn_weights((w1_hbm, w3_hbm, w2_hbm), (w1_ref, w3_ref, w2_ref), sems)
        gv = g_ref[...]
        h, xn, r = _rms_fwd(x_ref[...], gv)
        hb_ref[...] = h.astype(bf16)
        dyv = dy_ref[...]
        dyh = (0.5 * dyv).astype(bf16)
        dyh_ref[...] = dyh
        dh = jnp.zeros((tm, d), f32)
        for j in range(nj):
            av = a_ref[j].astype(f32)
            bv = b_ref[j].astype(f32)
            s = jax.nn.sigmoid(av)
            silu = av * s
            u_ref[j] = (silu * bv).astype(bf16)
            du = _nt(dyh, w2_ref[j])
            dab = (du * bv * (s * (1.0 + av * (1.0 - s)))).astype(bf16)
            dbb = (du * silu).astype(bf16)
            da_ref[j] = dab
            db_ref[j] = dbb
            dh = dh + _nt(dab, w1_ref[j]) + _nt(dbb, w3_ref[j])
        dx, dg = _rms_bwd(dh, xn, r, gv)
        dx_ref[...] = dyv + dx

        @pl.when(pl.program_id(0) == 0)
        def _():
            dg_ref[...] = jnp.zeros_like(dg_ref)
        dg_ref[...] += dg

    row = pl.BlockSpec((tm, d), lambda i: (i, 0))
    vec = pl.BlockSpec((1, d), lambda i: (0, 0))
    mid = pl.BlockSpec((nj, tm, ff), lambda i: (0, i, 0))
    mid_shape = jax.ShapeDtypeStruct((nj, t, ff), bf16)
    return pl.pallas_call(
        body, name=name, grid=(t // tm,),
        in_specs=[row, row, vec, ANY_SPEC, ANY_SPEC, ANY_SPEC, mid, mid],
        out_specs=[row, vec, row, row, mid, mid, mid],
        out_shape=[jax.ShapeDtypeStruct((t, d), f32), jax.ShapeDtypeStruct((1, d), f32),
                   jax.ShapeDtypeStruct((t, d), bf16), jax.ShapeDtypeStruct((t, d), bf16),
                   mid_shape, mid_shape, mid_shape],
        scratch_shapes=_ffn_weight_scratch(nj, d, ff),
        compiler_params=_cp("arbitrary"),
    )(dy, x, g, w1, w3, w2, a, b)


def _ffn_dw(name, hb, dyh, u, da, db, bufs, l, h, tt=512):
    t, d = hb.shape
    nj, _, ff = u.shape

    def body(hb_ref, dyh_ref, u_ref, da_ref, db_ref, *refs):
        dw1_ref, dw3_ref, dw2_ref, dw1b_ref, dw3b_ref, dw2b_ref = refs[3:]

        @pl.when(pl.program_id(1) == 0)
        def _():
            dw1_ref[...] = jnp.zeros_like(dw1_ref)
            dw3_ref[...] = jnp.zeros_like(dw3_ref)
            dw2_ref[...] = jnp.zeros_like(dw2_ref)
        hv = hb_ref[...]
        dw1_ref[0] += _tn(hv, da_ref[0])
        dw3_ref[0] += _tn(hv, db_ref[0])
        dw2_ref[0] += _tn(u_ref[0], dyh_ref[...])

        @pl.when(pl.program_id(1) == pl.num_programs(1) - 1)
        def _():
            dw1b_ref[...] = dw1_ref[...].astype(bf16)
            dw3b_ref[...] = dw3_ref[...].astype(bf16)
            dw2b_ref[...] = dw2_ref[...].astype(bf16)

    row = pl.BlockSpec((tt, d), lambda j, s: (s, 0))
    mid = pl.BlockSpec((1, tt, ff), lambda j, s: (j, s, 0))
    slab_in = pl.BlockSpec((1, None, None, d, ff), lambda j, s: (j, l, h, 0, 0))
    slab_out = pl.BlockSpec((1, None, None, ff, d), lambda j, s: (j, l, h, 0, 0))
    blk_in = pl.BlockSpec((1, d, ff), lambda j, s: (j, 0, 0))
    blk_out = pl.BlockSpec((1, ff, d), lambda j, s: (j, 0, 0))
    outs = pl.pallas_call(
        body, name=name, grid=(nj, t // tt),
        in_specs=[row, row, mid, mid, mid] + [ANY_SPEC] * 3,
        out_specs=[slab_in, slab_in, slab_out, blk_in, blk_in, blk_out],
        out_shape=[jax.ShapeDtypeStruct(b.shape, b.dtype) for b in bufs]
        + [jax.ShapeDtypeStruct((nj, d, ff), bf16), jax.ShapeDtypeStruct((nj, d, ff), bf16),
           jax.ShapeDtypeStruct((nj, ff, d), bf16)],
        input_output_aliases={5 + k: k for k in range(3)},
        compiler_params=_cp("parallel", "arbitrary"),
    )(hb, dyh, u, da, db, *bufs)
    return outs[:3], outs[3:]


_LOG_GAMMA = [float(np.log1p(-np.float32(2.0) ** np.float32(-5.0 - h))) for h in range(RET_HEADS)]


def _ret_consts(h):
    lg = jnp.where(h == 0, _LOG_GAMMA[0], jnp.where(h == 1, _LOG_GAMMA[1],
                   jnp.where(h == 2, _LOG_GAMMA[2], _LOG_GAMMA[3]))).astype(f32)
    c = RET_CHUNK
    r = lax.broadcasted_iota(jnp.int32, (c, c), 0)
    cc = lax.broadcasted_iota(jnp.int32, (c, c), 1)
    decay = jnp.where(r >= cc, jnp.exp(lg * jnp.maximum((r - cc).astype(f32), 0.0)), 0.0)
    pos = lax.broadcasted_iota(jnp.int32, (c, 1), 0).astype(f32)
    kd = jnp.exp(lg * (c - 1.0 - pos))
    qd = jnp.exp(lg * (pos + 1.0))
    gc = jnp.exp(lg * c)
    return decay, kd, qd, gc


def _rope(x, cos, sin):
    return x * cos + pltpu.roll(x, HEAD_DIM // 2, 1) * sin


def _rope_t(g, cos, sin):
    return g * cos + pltpu.roll(g * sin, HEAD_DIM // 2, 1)


def _rope_tables(s):
    half = HEAD_DIM // 2
    inv = ROPE_BASE ** (-jnp.arange(half, dtype=f32) / half)
    ang = jnp.arange(s, dtype=f32)[:, None] * inv[None, :]
    cos, sin = jnp.cos(ang), jnp.sin(ang)
    return jnp.concatenate([cos, cos], axis=1), jnp.concatenate([-sin, sin], axis=1)


def _head_ln(o):
    mu = jnp.mean(o, axis=-1, keepdims=True)
    oc = o - mu
    rs = lax.rsqrt(jnp.mean(oc * oc, axis=-1, keepdims=True) + EPS)
    return oc * rs, rs


def _ret_fwd(proj, cos, sin, ret_g, nb, s):
    c = RET_CHUNK
    nc = s // c
    t = nb * s
    scale = HEAD_DIM ** -0.5

    def body(q_ref, k_ref, v_ref, gate_ref, cos_ref, sin_ref, g_ref, o_ref, rprev_ref, m_ref, r_acc):
        n = pl.program_id(2)

        @pl.when(n == 0)
        def _():
            r_acc[...] = jnp.zeros_like(r_acc)
        decay, kd, qd, gc = _ret_consts(pl.program_id(1))
        cs, sn = cos_ref[...], sin_ref[...]
        q = _rope(q_ref[...], cs, sn)
        k = _rope(k_ref[...], cs, sn) * scale
        vb = v_ref[...].astype(bf16)
        sc = _nt(q.astype(bf16), k.astype(bf16)) * decay
        rv = r_acc[...]
        rprev_ref[0] = rv
        o = _nn(sc.astype(bf16), vb) + _nn((q * qd).astype(bf16), rv.astype(bf16))
        o_ref[...] = o
        r_acc[...] = rv * gc + _tn((k * kd).astype(bf16), vb)
        y, _ = _head_ln(o)
        gate = gate_ref[...]
        m_ref[...] = y * g_ref[...] * (gate * jax.nn.sigmoid(gate))

    def col(off):
        return pl.BlockSpec((c, HEAD_DIM), lambda b, h, n: (b * nc + n, off + h))

    tab = pl.BlockSpec((c, HEAD_DIM), lambda b, h, n: (n, 0))
    return pl.pallas_call(
        body, name="ret_fwd", grid=(nb, RET_HEADS, nc),
        in_specs=[col(0), col(4), col(8), col(12), tab, tab, pl.BlockSpec((1, HEAD_DIM), lambda b, h, n: (0, h))],
        out_specs=[col(0), pl.BlockSpec((1, HEAD_DIM, HEAD_DIM), lambda b, h, n: ((b * RET_HEADS + h) * nc + n, 0, 0)),
                   col(0)],
        out_shape=[jax.ShapeDtypeStruct((t, RET_WIDTH), f32),
                   jax.ShapeDtypeStruct((nb * RET_HEADS * nc, HEAD_DIM, HEAD_DIM), f32),
                   jax.ShapeDtypeStruct((t, RET_WIDTH), f32)],
        scratch_shapes=[pltpu.VMEM((HEAD_DIM, HEAD_DIM), f32)],
        compiler_params=_cp("parallel", "parallel", "arbitrary"),
    )(proj, proj, proj, proj, cos, sin, ret_g)


def _ret_bwd(dmerged, o_raw, rprev, proj, cos, sin, ret_g, nb, s):
    c = RET_CHUNK
    nc = s // c
    t = nb * s
    scale = HEAD_DIM ** -0.5

    def body(dm_ref, o_ref, rprev_ref, q_ref, k_ref, v_ref, gate_ref, cos_ref, sin_ref, g_ref,
             dq_ref, dk_ref, dv_ref, dgate_ref, dg_ref, dr_acc):
        b, n = pl.program_id(1), pl.program_id(2)

        @pl.when(n == 0)
        def _():
            dr_acc[...] = jnp.zeros_like(dr_acc)

        @pl.when((n == 0) & (b == 0))
        def _():
            dg_ref[...] = jnp.zeros_like(dg_ref)
        decay, kd, qd, gc = _ret_consts(pl.program_id(0))
        cs, sn = cos_ref[...], sin_ref[...]
        q = _rope(q_ref[...], cs, sn)
        k = _rope(k_ref[...], cs, sn) * scale
        qb, kb = q.astype(bf16), k.astype(bf16)
        vb = v_ref[...].astype(bf16)
        sc = _nt(qb, kb) * decay
        gv = g_ref[...]
        y, rs = _head_ln(o_ref[...])
        gate = gate_ref[...]
        sg = jax.nn.sigmoid(gate)
        silu = gate * sg
        dm = dm_ref[...]
        dgate_ref[...] = dm * y * gv * (sg * (1.0 + gate * (1.0 - sg)))
        dyl = dm * gv * silu
        dg_ref[...] += jnp.sum(dm * y * silu, axis=0, keepdims=True)
        do = rs * (dyl - jnp.mean(dyl, axis=-1, keepdims=True) - y * jnp.mean(dyl * y, axis=-1, keepdims=True))
        dob = do.astype(bf16)
        rv = rprev_ref[0]
        drn = dr_acc[...]
        drb = drn.astype(bf16)
        ds = (_nt(dob, vb) * decay).astype(bf16)
        kdb = (k * kd).astype(bf16)
        qdb = (q * qd).astype(bf16)
        dq_r = _nn(ds, kb) + _nt(dob, rv.astype(bf16)) * qd
        dk_r = _tn(ds, qb) + _nt(vb, drb) * kd
        dv_ref[...] = _tn(sc.astype(bf16), dob) + _nn(kdb, drb)
        dr_acc[...] = drn * gc + _tn(qdb, dob)
        dq_ref[...] = _rope_t(dq_r, cs, sn)
        dk_ref[...] = _rope_t(dk_r * scale, cs, sn)

    def col(off):
        return pl.BlockSpec((c, HEAD_DIM), lambda h, b, n: (b * nc + nc - 1 - n, off + h))

    tab = pl.BlockSpec((c, HEAD_DIM), lambda h, b, n: (nc - 1 - n, 0))
    gsp = pl.BlockSpec((1, HEAD_DIM), lambda h, b, n: (0, h))
    out_t = jax.ShapeDtypeStruct((t, RET_WIDTH), f32)
    return pl.pallas_call(
        body, name="ret_bwd", grid=(RET_HEADS, nb, nc),
        in_specs=[col(0), col(0),
                  pl.BlockSpec((1, HEAD_DIM, HEAD_DIM), lambda h, b, n: ((b * RET_HEADS + h) * nc + nc - 1 - n, 0, 0)),
                  col(0), col(4), col(8), col(12), tab, tab, gsp],
        out_specs=[col(0), col(0), col(0), col(0), gsp],
        out_shape=[out_t, out_t, out_t, out_t, jax.ShapeDtypeStruct((1, RET_WIDTH), f32)],
        scratch_shapes=[pltpu.VMEM((HEAD_DIM, HEAD_DIM), f32)],
        compiler_params=_cp("parallel", "arbitrary", "arbitrary"),
    )(dmerged, o_raw, rprev, proj, proj, proj, proj, cos, sin, ret_g)


def _neg_expm1(z):
    series = -(z * (1.0 + z * (0.5 + z * (1.0 / 6.0 + z * (1.0 / 24.0)))))
    return jnp.where(z > -0.01, series, 1.0 - jnp.exp(z))


def _lru_gates(xc, pa, pi, lam):
    r = jax.nn.sigmoid(pa)
    i = jax.nn.sigmoid(pi)
    log_a = -LRU_C * r * jax.nn.softplus(-lam)
    a = jnp.exp(log_a)
    bx = jnp.sqrt(_neg_expm1(2.0 * log_a)) * i * xc
    return a, bx


def _scan_fwd(a, b, row):
    d = 1
    while d < a.shape[0]:
        b = a * _shift_dn(b, d, row) + b
        a = a * _shift_dn(a, d, row, 1.0)
        d *= 2
    return b


def _scan_bwd(c, b, row):
    d = 1
    while d < c.shape[0]:
        b = c * _shift_up(b, d, row) + b
        c = c * _shift_up(c, d, row, 1.0)
        d *= 2
    return b


def _conv_fwd(x, cw, cb, row):
    return (cb + cw[3:4] * x + cw[2:3] * _shift_dn(x, 1, row) + cw[1:2] * _shift_dn(x, 2, row)
            + cw[0:1] * _shift_dn(x, 3, row))


def _lru_specs(s, order):
    def im(f):
        return (lambda b, g: f(b, g)) if order == "bg" else (lambda g, b: f(b, g))
    seq = lambda off: pl.BlockSpec((s, 128), im(lambda b, g: (b, off + g)))
    vec = pl.BlockSpec((1, 128), im(lambda b, g: (0, g)))
    cw = pl.BlockSpec((4, 128), im(lambda b, g: (0, g)))
    mat = pl.BlockSpec((1, 128, 128), im(lambda b, g: (g, 0, 0)))
    return seq, vec, cw, mat


def _lru_fwd(proj, conv_w, conv_b, w_a, b_a, w_i, b_i, lam, nb, s):
    def body(x_ref, gt_ref, cw_ref, cb_ref, wa_ref, ba_ref, wi_ref, bi_ref, lam_ref, out_ref):
        row = lax.broadcasted_iota(jnp.int32, (s, 128), 0)
        xc = _conv_fwd(x_ref[...], cw_ref[...], cb_ref[...], row)
        xcb = xc.astype(bf16)
        pa = _nn(xcb, wa_ref[0].astype(bf16)) + ba_ref[...]
        pi = _nn(xcb, wi_ref[0].astype(bf16)) + bi_ref[...]
        a, bx = _lru_gates(xc, pa, pi, lam_ref[...])
        h = _scan_fwd(a, bx, row)
        out_ref[...] = h * jax.nn.gelu(gt_ref[...])

    seq, vec, cw, mat = _lru_specs(s, "bg")
    return pl.pallas_call(
        body, name="lru_fwd", grid=(nb, LRU_BLOCKS),
        in_specs=[seq(16), seq(20), cw, vec, mat, vec, mat, vec, vec],
        out_specs=seq(0),
        out_shape=jax.ShapeDtypeStruct((nb * s, LRU_WIDTH), f32),
        compiler_params=_cp("parallel", "parallel"),
    )(proj, proj, conv_w, conv_b, w_a, b_a, w_i, b_i, lam)


def _lru_bwd(dmerged, proj, conv_w, conv_b, w_a, b_a, w_i, b_i, lam, nb, s):
    def body(dout_ref, x_ref, gt_ref, cw_ref, cb_ref, wa_ref, ba_ref, wi_ref, bi_ref, lam_ref,
             dx_ref, dgt_ref, dcw_ref, dcb_ref, dwa_ref, dba_ref, dwi_ref, dbi_ref, dlam_ref):
        row = lax.broadcasted_iota(jnp.int32, (s, 128), 0)
        x = x_ref[...]
        cwv = cw_ref[...]
        xc = _conv_fwd(x, cwv, cb_ref[...], row)
        xcb = xc.astype(bf16)
        wab, wib = wa_ref[0].astype(bf16), wi_ref[0].astype(bf16)
        pa = _nn(xcb, wab) + ba_ref[...]
        pi = _nn(xcb, wib) + bi_ref[...]
        (a, bx), gates_vjp = jax.vjp(_lru_gates, xc, pa, pi, lam_ref[...])
        h = _scan_fwd(a, bx, row)
        ge, gelu_vjp = jax.vjp(jax.nn.gelu, gt_ref[...])
        dout = dout_ref[...]
        dgt_ref[...] = gelu_vjp(dout * h)[0]
        adj = _scan_bwd(_shift_up(a, 1, row), dout * ge, row)
        dxc, dpa, dpi, dlam = gates_vjp((adj * _shift_dn(h, 1, row), adj))
        dpab, dpib = dpa.astype(bf16), dpi.astype(bf16)
        dxc = dxc + _nt(dpab, wab) + _nt(dpib, wib)
        dx_ref[...] = (cwv[3:4] * dxc + cwv[2:3] * _shift_up(dxc, 1, row) + cwv[1:2] * _shift_up(dxc, 2, row)
                       + cwv[0:1] * _shift_up(dxc, 3, row))

        @pl.when(pl.program_id(1) == 0)
        def _():
            for r in (dcw_ref, dcb_ref, dwa_ref, dba_ref, dwi_ref, dbi_ref, dlam_ref):
                r[...] = jnp.zeros_like(r)
        rsum = lambda v: jnp.sum(v, axis=0, keepdims=True)
        dcw_ref[...] += jnp.concatenate([rsum(dxc * _shift_dn(x, 3, row)), rsum(dxc * _shift_dn(x, 2, row)),
                                         rsum(dxc * _shift_dn(x, 1, row)), rsum(dxc * x)], axis=0)
        dcb_ref[...] += rsum(dxc)
        dwa_ref[0] += _tn(xcb, dpab)
        dwi_ref[0] += _tn(xcb, dpib)
        dba_ref[...] += rsum(dpa)
        dbi_ref[...] += rsum(dpi)
        dlam_ref[...] += dlam

    seq, vec, cw, mat = _lru_specs(s, "gb")
    t = nb * s
    vshape = jax.ShapeDtypeStruct((1, LRU_WIDTH), f32)
    mshape = jax.ShapeDtypeStruct((LRU_BLOCKS, 128, 128), f32)
    return pl.pallas_call(
        body, name="lru_bwd", grid=(LRU_BLOCKS, nb),
        in_specs=[seq(4), seq(16), seq(20), cw, vec, mat, vec, mat, vec, vec],
        out_specs=[seq(0), seq(0), cw, vec, mat, vec, mat, vec, vec],
        out_shape=[jax.ShapeDtypeStruct((t, LRU_WIDTH), f32), jax.ShapeDtypeStruct((t, LRU_WIDTH), f32),
                   jax.ShapeDtypeStruct((4, LRU_WIDTH), f32), vshape, mshape, vshape, mshape, vshape, vshape],
        compiler_params=_cp("parallel", "arbitrary"),
    )(dmerged, proj, proj, conv_w, conv_b, w_a, b_a, w_i, b_i, lam)


def _s5_disc(lr, li, ldt, bre, bim):
    dt = jnp.exp(ldt)
    mag = jnp.exp(lr * dt)
    lbr = mag * jnp.cos(li * dt)
    lbi = mag * jnp.sin(li * dt)
    den = lr * lr + li * li
    nr = lbr - 1.0
    fr = (nr * lr + lbi * li) / den
    fi = (lbi * lr - nr * li) / den
    bbr = fr[:, None, :] * bre - fi[:, None, :] * bim
    bbi = fr[:, None, :] * bim + fi[:, None, :] * bre
    return lbr, lbi, bbr, bbi


def _s5_prep(lr, li, ldt, bre, bim):
    def body(lr_ref, li_ref, ldt_ref, bre_ref, bim_ref, o1, o2, o3, o4):
        o1[...], o2[...], o3[...], o4[...] = _s5_disc(lr_ref[...], li_ref[...], ldt_ref[...], bre_ref[...], bim_ref[...])

    return pl.pallas_call(
        body, name="s5_prep", in_specs=[VMEM_SPEC] * 5, out_specs=[VMEM_SPEC] * 4,
        out_shape=[jax.ShapeDtypeStruct(lr.shape, f32), jax.ShapeDtypeStruct(lr.shape, f32),
                   jax.ShapeDtypeStruct(bre.shape, f32), jax.ShapeDtypeStruct(bre.shape, f32)],
    )(lr, li, ldt, bre, bim)


def _s5_prep_bwd(lr, li, ldt, bre, bim, cts):
    def body(lr_ref, li_ref, ldt_ref, bre_ref, bim_ref, g1, g2, g3, g4, o1, o2, o3, o4, o5):
        _, vjp = jax.vjp(_s5_disc, lr_ref[...], li_ref[...], ldt_ref[...], bre_ref[...], bim_ref[...])
        o1[...], o2[...], o3[...], o4[...], o5[...] = vjp((g1[...], g2[...], g3[...], g4[...]))

    return pl.pallas_call(
        body, name="s5_prep_bwd", in_specs=[VMEM_SPEC] * 9, out_specs=[VMEM_SPEC] * 5,
        out_shape=[jax.ShapeDtypeStruct(v.shape, f32) for v in (lr, li, ldt, bre, bim)],
    )(lr, li, ldt, bre, bim, *cts)


def _cmul(ar, ai, br, bi):
    return ar * br - ai * bi, ar * bi + ai * br


def _s5_pow_table(lr, li, n, row, up):
    ar = jnp.broadcast_to(lr, (n, lr.shape[1]))
    ai = jnp.broadcast_to(li, (n, li.shape[1]))
    shift = _shift_up if up else _shift_dn
    d = 1
    while d < n:
        ar, ai = _cmul(ar, ai, shift(ar, d, row, 1.0), shift(ai, d, row, 0.0))
        d *= 2
    return ar, ai


def _s5_local_scan(br, bi, lr, li, row, up):
    shift = _shift_up if up else _shift_dn
    pr, pi = lr, li
    d = 1
    while d < br.shape[0]:
        sr, si = _cmul(pr, pi, shift(br, d, row), shift(bi, d, row))
        br, bi = br + sr, bi + si
        pr, pi = _cmul(pr, pi, pr, pi)
        d *= 2
    return br, bi


def _s5_specs(s, nc, order):
    def im(f):
        return (lambda b, k: f(b, k)) if order == "bk" else (lambda k, b: f(b, k))
    seq = pl.BlockSpec((s, 128), im(lambda b, k: (b, k)))
    lvec = pl.BlockSpec((1, S5_BLOCK_STATES), im(lambda b, k: (0, k)))
    dvec = pl.BlockSpec((1, 128), im(lambda b, k: (0, k)))
    wmat = pl.BlockSpec((1, 128, S5_BLOCK_STATES), im(lambda b, k: (k, 0, 0)))
    h0 = pl.BlockSpec((1, nc, 2, S5_BLOCK_STATES), im(lambda b, k: (b * S5_BLOCKS + k, 0, 0, 0)))
    return seq, lvec, dvec, wmat, h0


def _s5_fwd(u, lbr, lbi, wbr, wbi, wcr, wci, dskip, nb, s):
    ln = S5_CHUNK
    nc = s // ln

    def body(u_ref, lr_ref, li_ref, wbr_ref, wbi_ref, wcr_ref, wci_ref, d_ref, yg_ref, y_ref, h0_ref):
        row = lax.broadcasted_iota(jnp.int32, (ln, S5_BLOCK_STATES), 0)
        lr, li = lr_ref[...], li_ref[...]
        pr, pi = _s5_pow_table(lr, li, ln, row, False)
        dv = d_ref[...]

        def step(n, carry):
            h0r, h0i = carry
            st = pl.multiple_of(n * ln, ln)
            uc = u_ref[pl.ds(st, ln), :]
            ub = uc.astype(bf16)
            br, bi = _s5_local_scan(_nn(ub, wbr_ref[0]), _nn(ub, wbi_ref[0]), lr, li, row, False)
            cr, ci = _cmul(pr, pi, h0r, h0i)
            hr, hi = br + cr, bi + ci
            h0_ref[0, n, 0:1, :] = h0r
            h0_ref[0, n, 1:2, :] = h0i
            y = _nt(hr.astype(bf16), wcr_ref[0]) - _nt(hi.astype(bf16), wci_ref[0]) + dv * uc
            y_ref[pl.ds(st, ln), :] = y
            yg_ref[pl.ds(st, ln), :] = jax.nn.gelu(y).astype(bf16)
            return hr[ln - 1:ln, :], hi[ln - 1:ln, :]

        z = jnp.zeros((1, S5_BLOCK_STATES), f32)
        lax.fori_loop(0, nc, step, (z, z))

    seq, lvec, dvec, wmat, h0 = _s5_specs(s, nc, "bk")
    t = nb * s
    return pl.pallas_call(
        body, name="s5_fwd", grid=(nb, S5_BLOCKS),
        in_specs=[seq, lvec, lvec, wmat, wmat, wmat, wmat, dvec],
        out_specs=[seq, seq, h0],
        out_shape=[jax.ShapeDtypeStruct((t, D_MODEL), bf16), jax.ShapeDtypeStruct((t, D_MODEL), f32),
                   jax.ShapeDtypeStruct((nb * S5_BLOCKS, nc, 2, S5_BLOCK_STATES), f32)],
        compiler_params=_cp("parallel", "parallel"),
    )(u, lbr, lbi, wbr, wbi, wcr, wci, dskip)


def _s5_bwd(dyg, y, u, h0, lbr, lbi, wbr, wbi, wcr, wci, dskip, nb, s):
    ln = S5_CHUNK
    nc = s // ln

    def body(dyg_ref, y_ref, u_ref, h0_ref, lr_ref, li_ref, wbr_ref, wbi_ref, wcr_ref, wci_ref, d_ref,
             du_ref, dlr_ref, dli_ref, dwbr_ref, dwbi_ref, dwcr_ref, dwci_ref, dd_ref):
        @pl.when(pl.program_id(1) == 0)
        def _():
            for r in (dlr_ref, dli_ref, dwbr_ref, dwbi_ref, dwcr_ref, dwci_ref, dd_ref):
                r[...] = jnp.zeros_like(r)
        row = lax.broadcasted_iota(jnp.int32, (ln, S5_BLOCK_STATES), 0)
        lr, li = lr_ref[...], li_ref[...]
        pr, pi = _s5_pow_table(lr, li, ln, row, False)
        qr, qi = _s5_pow_table(lr, -li, ln, row, True)
        dv = d_ref[...]
        rsum = lambda v: jnp.sum(v, axis=0, keepdims=True)

        def step(i, carry):
            gnr, gni = carry
            n = nc - 1 - i
            st = pl.multiple_of(n * ln, ln)
            uc = u_ref[pl.ds(st, ln), :]
            ub = uc.astype(bf16)
            h0v = h0_ref[0, n]
            h0r, h0i = h0v[0:1], h0v[1:2]
            br, bi = _s5_local_scan(_nn(ub, wbr_ref[0]), _nn(ub, wbi_ref[0]), lr, li, row, False)
            cr, ci = _cmul(pr, pi, h0r, h0i)
            hr, hi = br + cr, bi + ci
            dy = jax.vjp(jax.nn.gelu, y_ref[pl.ds(st, ln), :])[1](dyg_ref[pl.ds(st, ln), :])[0]
            dyb = dy.astype(bf16)
            dd_ref[...] += rsum(dy * uc)
            gr, gi = _s5_local_scan(_nn(dyb, wcr_ref[0]), -_nn(dyb, wci_ref[0]), lr, -li, row, True)
            cr, ci = _cmul(qr, qi, gnr, gni)
            gr, gi = gr + cr, gi + ci
            hpr = jnp.where(row >= 1, pltpu.roll(hr, 1, 0), h0r)
            hpi = jnp.where(row >= 1, pltpu.roll(hi, 1, 0), h0i)
            dlr_ref[...] += rsum(gr * hpr + gi * hpi)
            dli_ref[...] += rsum(gi * hpr - gr * hpi)
            grb, gib = gr.astype(bf16), gi.astype(bf16)
            dwbr_ref[0] += _tn(ub, grb)
            dwbi_ref[0] += _tn(ub, gib)
            dwcr_ref[0] += _tn(dyb, hr.astype(bf16))
            dwci_ref[0] -= _tn(dyb, hi.astype(bf16))
            du_ref[pl.ds(st, ln), :] = _nt(grb, wbr_ref[0]) + _nt(gib, wbi_ref[0]) + dv * dy
            return gr[0:1, :], gi[0:1, :]

        z = jnp.zeros((1, S5_BLOCK_STATES), f32)
        lax.fori_loop(0, nc, step, (z, z))

    seq, lvec, dvec, wmat, h0s = _s5_specs(s, nc, "kb")
    t = nb * s
    lshape = jax.ShapeDtypeStruct((1, S5_BLOCKS * S5_BLOCK_STATES), f32)
    wshape = jax.ShapeDtypeStruct((S5_BLOCKS, 128, S5_BLOCK_STATES), f32)
    return pl.pallas_call(
        body, name="s5_bwd", grid=(S5_BLOCKS, nb),
        in_specs=[seq, seq, seq, h0s, lvec, lvec, wmat, wmat, wmat, wmat, dvec],
        out_specs=[seq, lvec, lvec, wmat, wmat, wmat, wmat, dvec],
        out_shape=[jax.ShapeDtypeStruct((t, D_MODEL), f32), lshape, lshape, wshape, wshape, wshape, wshape,
                   jax.ShapeDtypeStruct((1, D_MODEL), f32)],
        compiler_params=_cp("parallel", "arbitrary"),
    )(dyg, y, u, h0, lbr, lbi, wbr, wbi, wcr, wci, dskip)


def _blockdiag(w):
    w4 = w.reshape(S5_BLOCKS, 8, S5_GROUP, S5_STATE)
    same_group = jnp.eye(8, dtype=bool)[None, :, None, :, None]
    return jnp.where(same_group, w4[:, :, :, None, :], 0.0).reshape(S5_BLOCKS, 128, S5_BLOCK_STATES)


def _blockdiag_t(dw):
    d5 = dw.reshape(S5_BLOCKS, 8, S5_GROUP, 8, S5_STATE)
    diag = jnp.diagonal(d5, axis1=1, axis2=3)
    return jnp.moveaxis(diag, 3, 1).reshape(S5_GROUPS, S5_GROUP, S5_STATE)


def _glu_fwd(ygb, wa, wb, x, tm=512, tn=512):
    t, d = x.shape

    def body(y_ref, wa_ref, wb_ref, x_ref, o_ref, p_ref, q_ref):
        p = _nn(y_ref[...], wa_ref[...])
        q = _nn(y_ref[...], wb_ref[...])
        p_ref[...] = p
        q_ref[...] = q
        o_ref[...] = x_ref[...] + p * jax.nn.sigmoid(q)

    tile = pl.BlockSpec((tm, tn), lambda i, j: (i, j))
    wsp = pl.BlockSpec((d, tn), lambda i, j: (0, j))
    out = jax.ShapeDtypeStruct((t, d), f32)
    return pl.pallas_call(
        body, name="glu_fwd", grid=(t // tm, d // tn),
        in_specs=[pl.BlockSpec((tm, d), lambda i, j: (i, 0)), wsp, wsp, tile],
        out_specs=[tile, tile, tile], out_shape=[out, out, out],
        compiler_params=_cp("parallel", "parallel"),
    )(ygb, wa, wb, x)


def _place():
    x, y, c = lax.axis_index("x"), lax.axis_index("y"), lax.axis_index("c")
    return x, y, c, [(1 - x, y), (x, 1 - y), (1 - x, 1 - y)]


def _all_gather(name, arrays):
    n = len(arrays)

    def body(*refs):
        ins, outs = refs[:n], refs[n:2 * n]
        send_sems, recv_sems, local_sems = refs[2 * n:]
        x, y, c, chips = _place()
        me, sib = (x, y, c), (x, y, 1 - c)

        def copy(i, k, block, to, src=None):
            dst = outs[i].at[4 * block[0] + 2 * block[1] + block[2]]
            return pltpu.make_async_remote_copy(
                src_ref=dst if src is None else src, dst_ref=dst,
                send_sem=send_sems.at[i * 7 + k], recv_sem=recv_sems.at[i * 7 + k],
                device_id=to, device_id_type=MESH)

        mine = [pltpu.make_async_copy(ins[i], outs[i].at[4 * x + 2 * y + c], local_sems.at[i]) for i in range(n)]
        for m in mine:
            m.start()
        first = []
        for i in range(n):
            first.append(copy(i, 0, me, sib, src=ins[i]))
            first += [copy(i, 1 + j, me, (*chip, c), src=ins[i]) for j, chip in enumerate(chips)]
        for cp in first:
            cp.start()
        passed = []
        for j, chip in enumerate(chips):
            for i in range(n):
                copy(i, 1 + j, (*chip, c), me).wait_recv()
                fwd = copy(i, 4 + j, (*chip, c), sib)
                fwd.start()
                passed.append(fwd)
        for i in range(n):
            copy(i, 0, sib, me).wait_recv()
        for j, chip in enumerate(chips):
            for i in range(n):
                copy(i, 4 + j, (*chip, 1 - c), me).wait_recv()
        for cp in first + passed:
            cp.wait_send()
        for m in mine:
            m.wait()

    return pl.pallas_call(
        body, name=name,
        in_specs=[ANY_SPEC] * n, out_specs=[ANY_SPEC] * n,
        out_shape=[jax.ShapeDtypeStruct((N_DEV,) + a.shape, a.dtype) for a in arrays],
        scratch_shapes=[pltpu.SemaphoreType.DMA((7 * n,)), pltpu.SemaphoreType.DMA((7 * n,)),
                        pltpu.SemaphoreType.DMA((n,))],
    )(*arrays)


def _tie(name, x, deps):
    def body(*refs):
        pass

    return pl.pallas_call(
        body, name=name, in_specs=[ANY_SPEC] * (1 + len(deps)), out_specs=ANY_SPEC,
        out_shape=jax.ShapeDtypeStruct(x.shape, x.dtype), input_output_aliases={0: 0},
    )(x, *deps)


def _xchg_copies(kind, srcs, lands, suffixes, send_sems, recv_sems):
    x, y, c, _ = _place()
    copies = []
    for i, (src, land, sfx) in enumerate(zip(srcs, lands, suffixes)):
        for k in range(N_DEV - 1):
            r = k + 1
            peer = (1 - x if r & 4 else x, 1 - y if r & 2 else y, 1 - c if r & 1 else c)
            if kind == "gather":
                s_ref, d_ref = src, land.at[(4 * x + 2 * y + c,) + sfx]
            else:
                s_ref, d_ref = src.at[4 * peer[0] + 2 * peer[1] + peer[2]], land.at[(k,) + sfx]
            copies.append(pltpu.make_async_remote_copy(
                src_ref=s_ref, dst_ref=d_ref, send_sem=send_sems.at[i * 7 + k], recv_sem=recv_sems.at[i * 7 + k],
                device_id=peer, device_id_type=MESH))
    return copies


def _xchg_start(name, kind, srcs, lands, suffixes=None):
    n = len(srcs)
    suffixes = suffixes or [()] * n

    def body(*refs):
        src, land = refs[:n], refs[n:2 * n]
        send_sems, recv_sems, token = refs[2 * n], refs[2 * n + 1], refs[-1]
        for cp in _xchg_copies(kind, src, land, suffixes, send_sems, recv_sems):
            cp.start()
        token[...] = jnp.zeros_like(token)

    arrays = list(srcs) + list(lands)
    outs = pl.pallas_call(
        body, name=name,
        out_shape=(pltpu.SemaphoreType.DMA((7 * n,)), pltpu.SemaphoreType.DMA((7 * n,)),
                   *[pltpu.HBM(a.shape, a.dtype) for a in arrays], jax.ShapeDtypeStruct((8, 128), f32)),
        in_specs=[HBM_SPEC] * (2 * n),
        out_specs=(SEM_SPEC, SEM_SPEC, *[HBM_SPEC] * (2 * n), VMEM_SPEC),
        input_output_aliases={i: 2 + i for i in range(2 * n)},
        compiler_params=pltpu.CompilerParams(has_side_effects=SIDE_EFFECT),
    )(*[pltpu.with_memory_space_constraint(a, pltpu.HBM) for a in arrays])
    return dict(kind=kind, n=n, suffixes=suffixes, send=outs[0], recv=outs[1], srcs=list(outs[2:2 + n]),
                lands=list(outs[2 + n:2 + 2 * n]), token=outs[-1])


def _xchg_wait(name, h, after, lands=None):
    n = h["n"]
    lands = h["lands"] if lands is None else lands

    def body(*refs):
        src, land = refs[:n], refs[n:2 * n]
        for cp in _xchg_copies(h["kind"], src, land, h["suffixes"], refs[2 * n], refs[2 * n + 1]):
            cp.wait_send()
            cp.wait_recv()

    arrays = h["srcs"] + list(lands)
    outs = pl.pallas_call(
        body, name=name,
        out_shape=tuple(pltpu.HBM(a.shape, a.dtype) for a in arrays),
        in_specs=[HBM_SPEC] * (2 * n) + [SEM_SPEC, SEM_SPEC] + [ANY_SPEC] * len(after),
        out_specs=tuple([HBM_SPEC] * (2 * n)),
        input_output_aliases={i: i for i in range(2 * n)},
        compiler_params=pltpu.CompilerParams(has_side_effects=SIDE_EFFECT),
    )(*arrays, h["send"], h["recv"], *after)
    return list(outs[n:])


def _rows(a):
    return a.reshape(-1, a.shape[-1])


def _row_tile(r):
    for tm in (512, 256, 128, 64, 32, 16, 8):
        if r % tm == 0:
            return tm
    return r


def _sum8(name, gathered):
    _, r, n = gathered.shape
    tm = _row_tile(r)

    def body(g_ref, o_ref):
        acc = g_ref[0]
        for k in range(1, N_DEV):
            acc = acc + g_ref[k]
        o_ref[...] = acc

    return pl.pallas_call(
        body, name=name, grid=(r // tm,),
        in_specs=[pl.BlockSpec((N_DEV, tm, n), lambda i: (0, i, 0))],
        out_specs=pl.BlockSpec((tm, n), lambda i: (i, 0)),
        out_shape=jax.ShapeDtypeStruct((r, n), f32),
        compiler_params=_cp("parallel"),
    )(gathered)


def _adamw(name, w, m, v, own, landed=None):
    shape = w.shape
    w2, m2, v2, o2 = _rows(w), _rows(m), _rows(v), _rows(own)
    r, n = w2.shape
    tm = _row_tile(r)
    c1 = 1.0 - ADAM_B1 ** ADAM_STEP
    c2 = 1.0 - ADAM_B2 ** ADAM_STEP
    extra = [] if landed is None else [landed.reshape(landed.shape[0], r, n)]

    def body(w_ref, m_ref, v_ref, o_ref, *refs):
        g = o_ref[...]
        if extra:
            for k in range(extra[0].shape[0]):
                g = g + refs[0][k].astype(f32)
        g_ref, d_ref, mn_ref, vn_ref = refs[len(extra):]
        mn = ADAM_B1 * m_ref[...] + (1.0 - ADAM_B1) * g
        vn = ADAM_B2 * v_ref[...] + (1.0 - ADAM_B2) * (g * g)
        g_ref[...] = g
        d_ref[...] = -ADAM_LR * ((mn / c1) / (jnp.sqrt(vn / c2) + ADAM_EPS) + ADAM_WD * w_ref[...])
        mn_ref[...] = mn
        vn_ref[...] = vn

    row = pl.BlockSpec((tm, n), lambda i: (i, 0))
    outs = pl.pallas_call(
        body, name=name, grid=(r // tm,),
        in_specs=[row] * 4 + [pl.BlockSpec((e.shape[0], tm, n), lambda i: (0, i, 0)) for e in extra],
        out_specs=[row] * 4, out_shape=[jax.ShapeDtypeStruct((r, n), f32)] * 4,
        compiler_params=_cp("parallel"),
    )(w2, m2, v2, o2, *extra)
    return [o.reshape(shape) for o in outs]


def _pack(arrays):
    flat = jnp.concatenate([a.reshape(-1).astype(f32) for a in arrays])
    pad = (-flat.shape[0]) % (128 * (512 if flat.shape[0] > 128 * 512 else 8))
    return jnp.pad(flat, (0, pad)).reshape(-1, 128)


def _unpack(packed, shapes):
    flat = packed.reshape(-1)
    out, off = [], 0
    for s in shapes:
        n = math.prod(s)
        out.append(flat[off:off + n].reshape(s))
        off += n
    return out


def _local_step(x, target, w, weights_of, send, nb, s):
    cos, sin = _rope_tables(s)
    g = {}
    ffn_saved = {}
    ffn_bufs = [lax.empty((N_DEV, 2, 2) + shp, f32)
                for shp in ((D_MODEL, FF_SHARD), (D_MODEL, FF_SHARD), (FF_SHARD, D_MODEL))]

    def ffn(xin, l, h, wts):
        y, a, b = _ffn_fwd(f"ffn_fwd_{l}{h}", xin, w["ffn_g"][l][h], *wts)
        ffn_saved[(l, h)] = (xin, a, b, wts)
        return y

    def ffn_back(dy, l, h):
        xin, a, b, wts = ffn_saved[(l, h)]
        dx, dg, hb, dyh, u, da, db = _ffn_dx(f"ffn_dx_{l}{h}", dy, xin, w["ffn_g"][l][h], *wts, a, b)
        ffn_bufs[:], halves = _ffn_dw(f"ffn_dw_{l}{h}", hb, dyh, u, da, db, ffn_bufs, l, h)
        g[f"ffn_g_{l}{h}"] = dg
        return send(f"ffn_{l}{h}", dict(zip(("ffn_w1", "ffn_w3", "ffn_w2"), halves)), dx)

    def slots(t):
        return t.reshape(N_DEV, D_MODEL // N_DEV, D_MODEL)

    x1 = ffn(x, 0, 0, weights_of(0, [])["ffn"])
    wg = weights_of(1, [x1])
    w_in, w_out = wg["w_in"], wg["w_out"]
    _, h0b = _norm_fwd("mix_norm_0", x1, w["mix_g"][0])
    proj = _mm("in_proj", h0b, w_in, "nn", tn=768)[0]
    o_raw, rprev, mret = _ret_fwd(proj, cos, sin, w["ret_g"], nb, s)
    lru = _lru_fwd(proj, w["conv_w"], w["conv_b"], w["lru_w_a"], w["lru_b_a"], w["lru_w_i"], w["lru_b_i"], w["lru_lam"], nb, s)
    merged = _ew("merge", lambda a, b: (jnp.concatenate([a, b], axis=1),), [mret, lru], [(D_MODEL, bf16)])[0]
    x2 = _mm("out_proj", merged, w_out, "nn", extras=[x1], epilogue=lambda acc, r: (acc + r,))[0]
    x3 = ffn(x2, 0, 1, wg["ffn"])
    wg = weights_of(2, [x3])
    glu_a, glu_b = wg["glu_a"], wg["glu_b"]
    x4 = ffn(x3, 1, 0, wg["ffn"])
    u, _ = _norm_fwd("mix_norm_1", x4, w["mix_g"][1])
    lbr, lbi, bbr, bbi = _s5_prep(w["s5_lr"], w["s5_li"], w["s5_ldt"], w["s5_bre"], w["s5_bim"])
    lbr_f, lbi_f = lbr.reshape(1, -1), lbi.reshape(1, -1)
    wbr, wbi = _blockdiag(bbr).astype(bf16), _blockdiag(bbi).astype(bf16)
    wcr, wci = _blockdiag(w["s5_cre"]).astype(bf16), _blockdiag(w["s5_cim"]).astype(bf16)
    ygb, ypre, h0s = _s5_fwd(u, lbr_f, lbi_f, wbr, wbi, wcr, wci, w["s5_d"], nb, s)
    x5, gp, gq = _glu_fwd(ygb, glu_a, glu_b, x4)
    x6 = ffn(x5, 1, 1, weights_of(3, [x5])["ffn"])
    loss, dx6, g["final_g"] = _final_loss(x6, w["final_g"], target)

    dx5 = ffn_back(dx6, 1, 1)

    def glu_bwd(d, p, q):
        sg = jax.nn.sigmoid(q)
        return d * sg, d * p * sg * (1.0 - sg)

    dp, dq = _ew("glu_bwd", glu_bwd, [dx5, gp, gq], [(D_MODEL, bf16), (D_MODEL, bf16)])
    dyg = _mm("glu_dy_a", dp, glu_a, "nt")[0]
    dyg = _mm("glu_dy_b", dq, glu_b, "nt", extras=[dyg], epilogue=lambda acc, r: (acc + r,))[0]
    g["glu_a"], ga_half = _mm_tn("glu_dw_a", ygb, dp)
    g["glu_b"], gb_half = _mm_tn("glu_dw_b", ygb, dq)
    dyg = send("glu", {"glu_a": slots(ga_half), "glu_b": slots(gb_half)}, dyg)
    du, dlr, dli, dwbr, dwbi, dwcr, dwci, g["s5_d"] = _s5_bwd(dyg, ypre, u, h0s, lbr_f, lbi_f, wbr, wbi, wcr, wci, w["s5_d"], nb, s)
    g["s5_cre"], g["s5_cim"] = _blockdiag_t(dwcr), _blockdiag_t(dwci)
    g["s5_lr"], g["s5_li"], g["s5_ldt"], g["s5_bre"], g["s5_bim"] = _s5_prep_bwd(
        w["s5_lr"], w["s5_li"], w["s5_ldt"], w["s5_bre"], w["s5_bim"],
        (dlr.reshape(S5_GROUPS, S5_STATE), dli.reshape(S5_GROUPS, S5_STATE), _blockdiag_t(dwbr), _blockdiag_t(dwbi)))
    dx4, g["mix_g_1"] = _norm_bwd("mix_norm_1_bwd", du, x4, w["mix_g"][1], dx5)
    dx3 = ffn_back(dx4, 1, 0)
    dx2 = ffn_back(dx3, 0, 1)
    dmerged = _mm("out_proj_dx", dx2, w_out, "nt")[0]
    g["w_out"], wo_half = _mm_tn("out_proj_dw", merged, dx2)
    dmerged = send("w_out", {"w_out": slots(wo_half)}, dmerged)
    dq_, dk_, dv_, dgate, g["ret_g"] = _ret_bwd(dmerged, o_raw, rprev, proj, cos, sin, w["ret_g"], nb, s)
    (dxl, dgl, g["conv_w"], g["conv_b"], g["lru_w_a"], g["lru_b_a"], g["lru_w_i"], g["lru_b_i"], g["lru_lam"]) = _lru_bwd(
        dmerged, proj, w["conv_w"], w["conv_b"], w["lru_w_a"], w["lru_b_a"], w["lru_w_i"], w["lru_b_i"], w["lru_lam"], nb, s)
    dproj = _ew("dproj", lambda *p: (jnp.concatenate(p, axis=1),), [dq_, dk_, dv_, dgate, dxl, dgl], [(3072, bf16)])[0]
    dh0 = _mm("in_proj_dx", dproj, w_in, "nt")[0]
    g["w_in"], wi_half = _mm_tn("in_proj_dw", h0b, dproj)
    dh0 = send("w_in", {"w_in": jnp.transpose(wi_half.reshape(D_MODEL, N_DEV, IN_SHARD), (1, 0, 2))}, dh0)
    dx1, g["mix_g_0"] = _norm_bwd("mix_norm_0_bwd", dh0, x1, w["mix_g"][0], dx2)
    dx0 = ffn_back(dx1, 0, 0)
    g["ffn_w1"], g["ffn_w3"], g["ffn_w2"] = ffn_bufs
    return loss, dx0, g


_WEIGHTS = ["ffn_norm_g", "ffn_w1", "ffn_w3", "ffn_w2", "mix_norm_g", "w_in_even", "w_out_even", "ret_norm_g", "conv_w",
            "conv_b", "lru_w_a", "lru_b_a", "lru_w_i", "lru_b_i", "lru_lambda", "s5_lambda_re", "s5_lambda_im", "s5_log_dt",
            "s5_b_re", "s5_b_im", "s5_c_re", "s5_c_im", "s5_d", "glu_w_a", "glu_w_b", "final_norm_g"]
_BIG = ["ffn_w1", "ffn_w3", "ffn_w2", "w_in_even", "w_out_even", "glu_w_a", "glu_w_b"]
_SMALL_SHARDED = ["ffn_norm_g", "conv_w", "s5_d"]
_SMALL = [n for n in _WEIGHTS if n not in _BIG]


def kernel(x, ffn_norm_g, ffn_w1, ffn_w3, ffn_w2, mix_norm_g, w_in_even, w_out_even, ret_norm_g, conv_w, conv_b, lru_w_a, lru_b_a, lru_w_i, lru_b_i, lru_lambda, s5_lambda_re, s5_lambda_im, s5_log_dt, s5_b_re, s5_b_im, s5_c_re, s5_c_im, s5_d, glu_w_a, glu_w_b, final_norm_g, loss_target, m_ffn_norm_g, m_ffn_w1, m_ffn_w3, m_ffn_w2, m_mix_norm_g, m_w_in_even, m_w_out_even, m_ret_norm_g, m_conv_w, m_conv_b, m_lru_w_a, m_lru_b_a, m_lru_w_i, m_lru_b_i, m_lru_lambda, m_s5_lambda_re, m_s5_lambda_im, m_s5_log_dt, m_s5_b_re, m_s5_b_im, m_s5_c_re, m_s5_c_im, m_s5_d, m_glu_w_a, m_glu_w_b, m_final_norm_g, v_ffn_norm_g, v_ffn_w1, v_ffn_w3, v_ffn_w2, v_mix_norm_g, v_w_in_even, v_w_out_even, v_ret_norm_g, v_conv_w, v_conv_b, v_lru_w_a, v_lru_b_a, v_lru_w_i, v_lru_b_i, v_lru_lambda, v_s5_lambda_re, v_s5_lambda_im, v_s5_log_dt, v_s5_b_re, v_s5_b_im, v_s5_c_re, v_s5_c_im, v_s5_d, v_glu_w_a, v_glu_w_b, v_final_norm_g):
    a = dict(locals())
    nb, s, d = x.shape
    ax, ay, ac = lax.axis_index("x"), lax.axis_index("y"), lax.axis_index("c")
    dev = 4 * ax + 2 * ay + ac
    chip = 2 * ax + ay

    (sm,) = _all_gather("ag_small_weights", [_pack([ffn_norm_g, conv_w, s5_d])])
    sm = sm.reshape(N_DEV, -1)
    ffn_g_full = jnp.transpose(sm[:, :512].reshape(N_DEV, 2, 2, 128), (1, 2, 0, 3)).reshape(2, 2, D_MODEL)
    conv_w_full = jnp.transpose(sm[:, 512:768].reshape(N_DEV, 4, 64), (1, 0, 2)).reshape(4, LRU_WIDTH)
    s5_d_full = sm[:, 768:896].reshape(1, D_MODEL)

    def ffn_shards(l, h):
        return [ffn_w1[l, h].astype(bf16), ffn_w3[l, h].astype(bf16), ffn_w2[l, h].astype(bf16)]

    ag_src = [ffn_shards(0, 0),
              [w_in_even[0].astype(bf16), w_out_even[0].astype(bf16)] + ffn_shards(0, 1),
              ffn_shards(1, 0) + [glu_w_a[0].astype(bf16), glu_w_b[0].astype(bf16)],
              ffn_shards(1, 1)]
    ag, token = [], sm
    for k, grp in enumerate(ag_src):
        grp[0] = _tie(f"tie_ag_{k}", grp[0], [token])
        lands = [lax.dynamic_update_index_in_dim(lax.empty((N_DEV,) + t.shape, bf16), t, dev, 0) for t in grp]
        ag.append(_xchg_start(f"ag_start_{k}", "gather", grp, lands))
        token = ag[-1]["token"]

    def weights_of(k, after):
        after = after if k else [h["token"] for h in ag]
        got = _xchg_wait(f"ag_wait_{k}", ag[k], after)
        if k == 1:
            return {"w_in": jnp.transpose(got[0], (1, 0, 2)).reshape(D_MODEL, N_DEV * IN_SHARD),
                    "w_out": got[1].reshape(D_MODEL, D_MODEL), "ffn": got[2:]}
        if k == 2:
            return {"ffn": got[:3], "glu_a": got[3].reshape(D_MODEL, D_MODEL), "glu_b": got[4].reshape(D_MODEL, D_MODEL)}
        return {"ffn": got}

    ffn_lands = [lax.empty((N_DEV - 1, 2, 2) + shp, bf16)
                 for shp in ((D_MODEL, FF_SHARD), (D_MODEL, FF_SHARD), (FF_SHARD, D_MODEL))]
    rs = []

    def send(group, arrays, carry):
        srcs = list(arrays.values())
        if group.startswith("ffn_"):
            sfx = [(int(group[4]), int(group[5]))] * 3
            h = _xchg_start("rs_start_" + group, "scatter", srcs, ffn_lands, sfx)
            ffn_lands[:] = h["lands"]
        else:
            h = _xchg_start("rs_start_" + group, "scatter", srcs,
                            [lax.empty((N_DEV - 1,) + t.shape[1:], bf16) for t in srcs])
        rs.append((group, list(arrays), h))
        return _tie("tie_" + group, carry, [h["token"]])

    w = {
        "ffn_g": [[ffn_g_full[l, h].reshape(1, D_MODEL) for h in range(2)] for l in range(2)],
        "mix_g": [mix_norm_g[0:1], mix_norm_g[1:2]],
        "ret_g": ret_norm_g, "conv_w": conv_w_full, "conv_b": conv_b,
        "lru_w_a": lru_w_a[0], "lru_b_a": lru_b_a, "lru_w_i": lru_w_i[0], "lru_b_i": lru_b_i, "lru_lam": lru_lambda,
        "s5_lr": s5_lambda_re[0], "s5_li": s5_lambda_im[0], "s5_ldt": s5_log_dt.reshape(S5_GROUPS, 1),
        "s5_bre": jnp.swapaxes(s5_b_re[0], 1, 2), "s5_bim": jnp.swapaxes(s5_b_im[0], 1, 2),
        "s5_cre": s5_c_re[0], "s5_cim": s5_c_im[0], "s5_d": s5_d_full,
        "final_g": final_norm_g.reshape(1, D_MODEL),
    }

    loss_part, dx, g = _local_step(x.reshape(nb * s, d), loss_target.reshape(nb * s, d), w, weights_of, send, nb, s)
    loss = lax.psum(loss_part[0, 0], ("x", "y", "c"))

    part = {
        "ffn_norm_g": jnp.stack([jnp.stack([g[f"ffn_g_{l}{h}"][0] for h in range(2)]) for l in range(2)]),
        "mix_norm_g": jnp.concatenate([g["mix_g_0"], g["mix_g_1"]], axis=0),
        "ret_norm_g": g["ret_g"], "conv_w": g["conv_w"][None], "conv_b": g["conv_b"],
        "lru_w_a": g["lru_w_a"][None], "lru_b_a": g["lru_b_a"], "lru_w_i": g["lru_w_i"][None], "lru_b_i": g["lru_b_i"],
        "lru_lambda": g["lru_lam"], "s5_lambda_re": g["s5_lr"][None], "s5_lambda_im": g["s5_li"][None],
        "s5_log_dt": g["s5_ldt"].reshape(1, S5_GROUPS),
        "s5_b_re": jnp.swapaxes(g["s5_bre"], 1, 2)[None], "s5_b_im": jnp.swapaxes(g["s5_bim"], 1, 2)[None],
        "s5_c_re": g["s5_cre"][None], "s5_c_im": g["s5_cim"][None], "s5_d": g["s5_d"], "final_norm_g": g["final_g"][0],
    }
    (gath,) = _all_gather("ag_small_grads", [_pack([part[n] for n in _SMALL])])
    full = dict(zip(_SMALL, _unpack(_sum8("sum_small_grads", gath), [part[n].shape for n in _SMALL])))
    for n in _SMALL_SHARDED:
        width = a[n].shape[-1]
        full[n] = lax.dynamic_slice_in_dim(full[n], dev * width, width, axis=full[n].ndim - 1)
    shapes = [a[n].shape for n in _SMALL]
    packed = _adamw("adamw_small", _pack([a[n] for n in _SMALL]), _pack([a["m_" + n] for n in _SMALL]),
                    _pack([a["v_" + n] for n in _SMALL]), _pack([full[n] for n in _SMALL]))
    res = {n: vals for n, vals in zip(_SMALL, zip(*[_unpack(p, shapes) for p in packed]))}

    landed = {}
    for group, names, h in rs:
        if group.startswith("ffn_"):
            ffn_lands[:] = _xchg_wait("rs_wait_" + group, h, [dx], ffn_lands)
        else:
            landed.update(zip(names, _xchg_wait("rs_wait_" + group, h, [dx])))
    landed.update(zip(("ffn_w1", "ffn_w3", "ffn_w2"), ffn_lands))
    own = {n: lax.dynamic_index_in_dim(g[n], dev, axis=0, keepdims=False) for n in ("ffn_w1", "ffn_w3", "ffn_w2")}
    own["w_in"] = lax.dynamic_slice_in_dim(g["w_in"], dev * IN_SHARD, IN_SHARD, axis=1)
    for n in ("w_out", "glu_a", "glu_b"):
        own[n] = lax.dynamic_slice_in_dim(g[n], dev * (D_MODEL // N_DEV), D_MODEL // N_DEV, axis=0)
    for n, short in zip(_BIG, ("ffn_w1", "ffn_w3", "ffn_w2", "w_in", "w_out", "glu_a", "glu_b")):
        res[n] = _adamw("adamw_" + n, a[n], a["m_" + n], a["v_" + n], own[short].reshape(a[n].shape),
                        landed[short].reshape((N_DEV - 1,) + a[n].shape))

    out = [loss, dx.reshape(nb, s, d)]
    for k in range(4):
        out += [res[n][k] for n in _WEIGHTS]
    return tuple(out)
```

```python
import functools
import math

import numpy as np
import jax
import jax.numpy as jnp
from jax import lax
from jax.experimental import pallas as pl
from jax.experimental.pallas import tpu as pltpu

f32 = jnp.float32
bf16 = jnp.bfloat16

D_MODEL = 1024
N_DEV = 8
EPS = 1e-6
RET_HEADS = 4
HEAD_DIM = 128
RET_WIDTH = 512
RET_CHUNK = 128
ROPE_BASE = 10000.0
LRU_WIDTH = 512
LRU_BLOCKS = 4
LRU_C = 8.0
S5_GROUP = 16
S5_GROUPS = 64
S5_STATE = 64
S5_CHUNK = 128
S5_BLOCKS = 8
S5_BLOCK_STATES = 512
S5_SUBLANES = 8
D_FF = 2816
FF_SHARD = D_FF // N_DEV
IN_SHARD = 3072 // N_DEV
ADAM_LR = 0.001
ADAM_B1 = 0.9
ADAM_B2 = 0.999
ADAM_EPS = 1e-08
ADAM_WD = 0.01
ADAM_STEP = 10

VMEM_LIMIT = 48 * 1024 * 1024
VMEM_SPEC = pl.BlockSpec(memory_space=pltpu.VMEM)
ANY_SPEC = pl.BlockSpec(memory_space=pl.ANY)
HBM_SPEC = pl.BlockSpec(memory_space=pltpu.HBM)
SEM_SPEC = pl.BlockSpec(memory_space=pltpu.SEMAPHORE)
SIDE_EFFECT = pltpu.SideEffectType.DATAFLOW_SIDE_EFFECTING
MESH = pl.DeviceIdType.MESH


def _cp(*sem):
    return pltpu.CompilerParams(dimension_semantics=sem, vmem_limit_bytes=VMEM_LIMIT)


def _nn(a, b):
    return jnp.dot(a, b, preferred_element_type=f32)


def _nt(a, b):
    return lax.dot_general(a, b, (((1,), (1,)), ((), ())), preferred_element_type=f32)


def _tn(a, b):
    return lax.dot_general(a, b, (((0,), (0,)), ((), ())), preferred_element_type=f32)


def _rms_fwd(x, g):
    r = lax.rsqrt(jnp.mean(x * x, axis=-1, keepdims=True) + EPS)
    xn = x * r
    return xn * g, xn, r


def _rms_bwd(dh, xn, r, g):
    dxn = dh * g
    dx = r * (dxn - xn * jnp.mean(dxn * xn, axis=-1, keepdims=True))
    dg = jnp.sum(dh * xn, axis=0, keepdims=True)
    return dx, dg


def _shift_dn(v, d, row, fill=0.0):
    return jnp.where(row >= d, pltpu.roll(v, d, 0), fill)


def _shift_up(v, d, row, fill=0.0):
    n = v.shape[0]
    return jnp.where(row < n - d, pltpu.roll(v, n - d, 0), fill)


def _ew(name, fn, ins, outs, tm=512):
    t = ins[0].shape[0]
    n_in = len(ins)

    def body(*refs):
        res = fn(*[r[...] for r in refs[:n_in]])
        for o, v in zip(refs[n_in:], res):
            o[...] = v.astype(o.dtype)

    return pl.pallas_call(
        body, name=name, grid=(t // tm,),
        in_specs=[pl.BlockSpec((tm, a.shape[1]), lambda i: (i, 0)) for a in ins],
        out_specs=[pl.BlockSpec((tm, n), lambda i: (i, 0)) for n, _ in outs],
        out_shape=[jax.ShapeDtypeStruct((t, n), dt) for n, dt in outs],
        compiler_params=_cp("parallel"),
    )(*ins)


def _mm(name, x, w, kind, extras=(), epilogue=None, outs=None, tm=512, tn=512):
    t = x.shape[0]
    n = w.shape[1] if kind == "nn" else w.shape[0]
    tn = min(tn, n)
    outs = outs or [f32]
    n_ex = len(extras)

    def body(x_ref, w_ref, *refs):
        xb = x_ref[...].astype(bf16)
        acc = _nn(xb, w_ref[...]) if kind == "nn" else _nt(xb, w_ref[...])
        res = epilogue(acc, *[r[...] for r in refs[:n_ex]]) if epilogue else (acc,)
        for o, v in zip(refs[n_ex:], res):
            o[...] = v.astype(o.dtype)

    w_spec = (pl.BlockSpec((w.shape[0], tn), lambda i, j: (0, j)) if kind == "nn"
              else pl.BlockSpec((tn, w.shape[1]), lambda i, j: (j, 0)))
    tile = pl.BlockSpec((tm, tn), lambda i, j: (i, j))
    return pl.pallas_call(
        body, name=name, grid=(t // tm, n // tn),
        in_specs=[pl.BlockSpec((tm, x.shape[1]), lambda i, j: (i, 0)), w_spec] + [tile] * n_ex,
        out_specs=[tile] * len(outs),
        out_shape=[jax.ShapeDtypeStruct((t, n), dt) for dt in outs],
        compiler_params=_cp("parallel", "parallel"),
    )(x, w, *extras)


def _mm_tn(name, x, y, tk=1024, tn=512, tt=512):
    t, k = x.shape
    n = y.shape[1]
    tk, tn = min(tk, k), min(tn, n)

    def body(x_ref, y_ref, o_ref, ob_ref):
        @pl.when(pl.program_id(2) == 0)
        def _():
            o_ref[...] = jnp.zeros_like(o_ref)
        o_ref[...] += _tn(x_ref[...].astype(bf16), y_ref[...].astype(bf16))

        @pl.when(pl.program_id(2) == pl.num_programs(2) - 1)
        def _():
            ob_ref[...] = o_ref[...].astype(bf16)

    out = pl.BlockSpec((tk, tn), lambda i, j, s: (i, j))
    return pl.pallas_call(
        body, name=name, grid=(k // tk, n // tn, t // tt),
        in_specs=[pl.BlockSpec((tt, tk), lambda i, j, s: (s, i)), pl.BlockSpec((tt, tn), lambda i, j, s: (s, j))],
        out_specs=[out, out],
        out_shape=[jax.ShapeDtypeStruct((k, n), f32), jax.ShapeDtypeStruct((k, n), bf16)],
        compiler_params=_cp("parallel", "parallel", "arbitrary"),
    )(x, y)


def _norm_fwd(name, x, g, tm=512):
    t, d = x.shape

    def body(x_ref, g_ref, h_ref, hb_ref):
        h, _, _ = _rms_fwd(x_ref[...], g_ref[...])
        h_ref[...] = h
        hb_ref[...] = h.astype(bf16)

    row = pl.BlockSpec((tm, d), lambda i: (i, 0))
    return pl.pallas_call(
        body, name=name, grid=(t // tm,),
        in_specs=[row, pl.BlockSpec((1, d), lambda i: (0, 0))],
        out_specs=[row, row],
        out_shape=[jax.ShapeDtypeStruct((t, d), f32), jax.ShapeDtypeStruct((t, d), bf16)],
        compiler_params=_cp("parallel"),
    )(x, g)


def _norm_bwd(name, dh, x, g, dres, tm=512):
    t, d = x.shape

    def body(dh_ref, x_ref, g_ref, dres_ref, dx_ref, dg_ref):
        gv = g_ref[...]
        _, xn, r = _rms_fwd(x_ref[...], gv)
        dx, dg = _rms_bwd(dh_ref[...], xn, r, gv)
        dx_ref[...] = dres_ref[...] + dx

        @pl.when(pl.program_id(0) == 0)
        def _():
            dg_ref[...] = jnp.zeros_like(dg_ref)
        dg_ref[...] += dg

    row = pl.BlockSpec((tm, d), lambda i: (i, 0))
    vec = pl.BlockSpec((1, d), lambda i: (0, 0))
    return pl.pallas_call(
        body, name=name, grid=(t // tm,),
        in_specs=[row, row, vec, row],
        out_specs=[row, vec],
        out_shape=[jax.ShapeDtypeStruct((t, d), f32), jax.ShapeDtypeStruct((1, d), f32)],
        compiler_params=_cp("arbitrary"),
    )(dh, x, g, dres)


def _final_loss(x, g, target, tm=512):
    t, d = x.shape

    def body(x_ref, g_ref, t_ref, loss_ref, dx_ref, dg_ref):
        gv = g_ref[...]
        y, xn, r = _rms_fwd(x_ref[...], gv)
        err = y - t_ref[...]
        dy = err * (1.0 / d)
        dx, dg = _rms_bwd(dy, xn, r, gv)
        dx_ref[...] = dx

        @pl.when(pl.program_id(0) == 0)
        def _():
            dg_ref[...] = jnp.zeros_like(dg_ref)
            loss_ref[...] = jnp.zeros_like(loss_ref)
        dg_ref[...] += dg
        loss_ref[...] += jnp.full((1, 128), 0.5 / d, f32) * jnp.sum(err * err)

    row = pl.BlockSpec((tm, d), lambda i: (i, 0))
    vec = pl.BlockSpec((1, d), lambda i: (0, 0))
    return pl.pallas_call(
        body, name="final_loss", grid=(t // tm,),
        in_specs=[row, vec, row],
        out_specs=[pl.BlockSpec((1, 128), lambda i: (0, 0)), row, vec],
        out_shape=[jax.ShapeDtypeStruct((1, 128), f32), jax.ShapeDtypeStruct((t, d), f32),
                   jax.ShapeDtypeStruct((1, d), f32)],
        compiler_params=_cp("arbitrary"),
    )(x, g, target)


def _load_ffn_weights(hbm_refs, vmem_refs, sems):
    @pl.when(pl.program_id(0) == 0)
    def _():
        copies = [pltpu.make_async_copy(src, dst, sems.at[k])
                  for k, (src, dst) in enumerate(zip(hbm_refs, vmem_refs))]
        for cp in copies:
            cp.start()
        for cp in copies:
            cp.wait()


def _ffn_weight_scratch(nj, d, ff):
    return [pltpu.VMEM((nj, d, ff), bf16), pltpu.VMEM((nj, d, ff), bf16), pltpu.VMEM((nj, ff, d), bf16),
            pltpu.SemaphoreType.DMA((3,))]


def _ffn_fwd(name, x, g, w1, w3, w2, tm=256):
    t, d = x.shape
    nj, ff = w1.shape[0], w1.shape[-1]

    def body(x_ref, g_ref, w1_hbm, w3_hbm, w2_hbm, y_ref, a_ref, b_ref, w1_ref, w3_ref, w2_ref, sems):
        _load_ffn_weights((w1_hbm, w3_hbm, w2_hbm), (w1_ref, w3_ref, w2_ref), sems)
        xv = x_ref[...]
        h, _, _ = _rms_fwd(xv, g_ref[...])
        hb = h.astype(bf16)
        acc = jnp.zeros((tm, d), f32)
        for j in range(nj):
            a = _nn(hb, w1_ref[j])
            b = _nn(hb, w3_ref[j])
            a_ref[j] = a.astype(bf16)
            b_ref[j] = b.astype(bf16)
            u = (a * jax.nn.sigmoid(a) * b).astype(bf16)
            acc = acc + _nn(u, w2_ref[j])
        y_ref[...] = xv + 0.5 * acc

    row = pl.BlockSpec((tm, d), lambda i: (i, 0))
    mid = pl.BlockSpec((nj, tm, ff), lambda i: (0, i, 0))
    return pl.pallas_call(
        body, name=name, grid=(t // tm,),
        in_specs=[row, pl.BlockSpec((1, d), lambda i: (0, 0)), ANY_SPEC, ANY_SPEC, ANY_SPEC],
        out_specs=[row, mid, mid],
        out_shape=[jax.ShapeDtypeStruct((t, d), f32), jax.ShapeDtypeStruct((nj, t, ff), bf16),
                   jax.ShapeDtypeStruct((nj, t, ff), bf16)],
        scratch_shapes=_ffn_weight_scratch(nj, d, ff),
        compiler_params=_cp("arbitrary"),
    )(x, g, w1, w3, w2)


def _ffn_dx(name, dy, x, g, w1, w3, w2, a, b, tm=256):
    t, d = x.shape
    nj, ff = w1.shape[0], w1.shape[-1]

    def body(dy_ref, x_ref, g_ref, w1_hbm, w3_hbm, w2_hbm, a_ref, b_ref,
             dx_ref, dg_ref, hb_ref, dyh_ref, u_ref, da_ref, db_ref, w1_ref, w3_ref, w2_ref, sems):
        _load_ffn_weights((w1_hbm, w3_hbm, w2_hbm), (w1_ref, w3_ref, w2_ref), sems)
        gv = g_ref[...]
        h, xn, r = _rms_fwd(x_ref[...], gv)
        hb_ref[...] = h.astype(bf16)
        dyv = dy_ref[...]
        dyh = (0.5 * dyv).astype(bf16)
        dyh_ref[...] = dyh
        dh = jnp.zeros((tm, d), f32)
        for j in range(nj):
            av = a_ref[j].astype(f32)
            bv = b_ref[j].astype(f32)
            s = jax.nn.sigmoid(av)
            silu = av * s
            u_ref[j] = (silu * bv).astype(bf16)
            du = _nt(dyh, w2_ref[j])
            dab = (du * bv * (s * (1.0 + av * (1.0 - s)))).astype(bf16)
            dbb = (du * silu).astype(bf16)
            da_ref[j] = dab
            db_ref[j] = dbb
            dh = dh + _nt(dab, w1_ref[j]) + _nt(dbb, w3_ref[j])
        dx, dg = _rms_bwd(dh, xn, r, gv)
        dx_ref[...] = dyv + dx

        @pl.when(pl.program_id(0) == 0)
        def _():
            dg_ref[...] = jnp.zeros_like(dg_ref)
        dg_ref[...] += dg

    row = pl.BlockSpec((tm, d), lambda i: (i, 0))
    vec = pl.BlockSpec((1, d), lambda i: (0, 0))
    mid = pl.BlockSpec((nj, tm, ff), lambda i: (0, i, 0))
    mid_shape = jax.ShapeDtypeStruct((nj, t, ff), bf16)
    return pl.pallas_call(
        body, name=name, grid=(t // tm,),
        in_specs=[row, row, vec, ANY_SPEC, ANY_SPEC, ANY_SPEC, mid, mid],
        out_specs=[row, vec, row, row, mid, mid, mid],
        out_shape=[jax.ShapeDtypeStruct((t, d), f32), jax.ShapeDtypeStruct((1, d), f32),
                   jax.ShapeDtypeStruct((t, d), bf16), jax.ShapeDtypeStruct((t, d), bf16),
                   mid_shape, mid_shape, mid_shape],
        scratch_shapes=_ffn_weight_scratch(nj, d, ff),
        compiler_params=_cp("arbitrary"),
    )(dy, x, g, w1, w3, w2, a, b)


def _ffn_dw(name, hb, dyh, u, da, db, bufs, l, h, tt=512):
    t, d = hb.shape
    nj, _, ff = u.shape

    def body(hb_ref, dyh_ref, u_ref, da_ref, db_ref, *refs):
        dw1_ref, dw3_ref, dw2_ref, dw1b_ref, dw3b_ref, dw2b_ref = refs[3:]

        @pl.when(pl.program_id(1) == 0)
        def _():
            dw1_ref[...] = jnp.zeros_like(dw1_ref)
            dw3_ref[...] = jnp.zeros_like(dw3_ref)
            dw2_ref[...] = jnp.zeros_like(dw2_ref)
        hv = hb_ref[...]
        dw1_ref[0] += _tn(hv, da_ref[0])
        dw3_ref[0] += _tn(hv, db_ref[0])
        dw2_ref[0] += _tn(u_ref[0], dyh_ref[...])

        @pl.when(pl.program_id(1) == pl.num_programs(1) - 1)
        def _():
            dw1b_ref[...] = dw1_ref[...].astype(bf16)
            dw3b_ref[...] = dw3_ref[...].astype(bf16)
            dw2b_ref[...] = dw2_ref[...].astype(bf16)

    row = pl.BlockSpec((tt, d), lambda j, s: (s, 0))
    mid = pl.BlockSpec((1, tt, ff), lambda j, s: (j, s, 0))
    slab_in = pl.BlockSpec((1, None, None, d, ff), lambda j, s: (j, l, h, 0, 0))
    slab_out = pl.BlockSpec((1, None, None, ff, d), lambda j, s: (j, l, h, 0, 0))
    blk_in = pl.BlockSpec((1, d, ff), lambda j, s: (j, 0, 0))
    blk_out = pl.BlockSpec((1, ff, d), lambda j, s: (j, 0, 0))
    outs = pl.pallas_call(
        body, name=name, grid=(nj, t // tt),
        in_specs=[row, row, mid, mid, mid] + [ANY_SPEC] * 3,
        out_specs=[slab_in, slab_in, slab_out, blk_in, blk_in, blk_out],
        out_shape=[jax.ShapeDtypeStruct(b.shape, b.dtype) for b in bufs]
        + [jax.ShapeDtypeStruct((nj, d, ff), bf16), jax.ShapeDtypeStruct((nj, d, ff), bf16),
           jax.ShapeDtypeStruct((nj, ff, d), bf16)],
        input_output_aliases={5 + k: k for k in range(3)},
        compiler_params=_cp("parallel", "arbitrary"),
    )(hb, dyh, u, da, db, *bufs)
    return outs[:3], outs[3:]


_LOG_GAMMA = [float(np.log1p(-np.float32(2.0) ** np.float32(-5.0 - h))) for h in range(RET_HEADS)]


def _ret_consts(h):
    lg = jnp.where(h == 0, _LOG_GAMMA[0], jnp.where(h == 1, _LOG_GAMMA[1],
                   jnp.where(h == 2, _LOG_GAMMA[2], _LOG_GAMMA[3]))).astype(f32)
    c = RET_CHUNK
    r = lax.broadcasted_iota(jnp.int32, (c, c), 0)
    cc = lax.broadcasted_iota(jnp.int32, (c, c), 1)
    decay = jnp.where(r >= cc, jnp.exp(lg * jnp.maximum((r - cc).astype(f32), 0.0)), 0.0)
    pos = lax.broadcasted_iota(jnp.int32, (c, 1), 0).astype(f32)
    kd = jnp.exp(lg * (c - 1.0 - pos))
    qd = jnp.exp(lg * (pos + 1.0))
    gc = jnp.exp(lg * c)
    return decay, kd, qd, gc


def _rope(x, cos, sin):
    return x * cos + pltpu.roll(x, HEAD_DIM // 2, 1) * sin


def _rope_t(g, cos, sin):
    return g * cos + pltpu.roll(g * sin, HEAD_DIM // 2, 1)


def _rope_tables(s):
    half = HEAD_DIM // 2
    inv = ROPE_BASE ** (-jnp.arange(half, dtype=f32) / half)
    ang = jnp.arange(s, dtype=f32)[:, None] * inv[None, :]
    cos, sin = jnp.cos(ang), jnp.sin(ang)
    return jnp.concatenate([cos, cos], axis=1), jnp.concatenate([-sin, sin], axis=1)


def _head_ln(o):
    mu = jnp.mean(o, axis=-1, keepdims=True)
    oc = o - mu
    rs = lax.rsqrt(jnp.mean(oc * oc, axis=-1, keepdims=True) + EPS)
    return oc * rs, rs


def _ret_fwd(proj, cos, sin, ret_g, nb, s):
    c = RET_CHUNK
    nc = s // c
    t = nb * s
    scale = HEAD_DIM ** -0.5

    def body(q_ref, k_ref, v_ref, gate_ref, cos_ref, sin_ref, g_ref, o_ref, rprev_ref, m_ref, r_acc):
        n = pl.program_id(2)

        @pl.when(n == 0)
        def _():
            r_acc[...] = jnp.zeros_like(r_acc)
        decay, kd, qd, gc = _ret_consts(pl.program_id(1))
        cs, sn = cos_ref[...], sin_ref[...]
        q = _rope(q_ref[...], cs, sn)
        k = _rope(k_ref[...], cs, sn) * scale
        vb = v_ref[...].astype(bf16)
        sc = _nt(q.astype(bf16), k.astype(bf16)) * decay
        rv = r_acc[...]
        rprev_ref[0] = rv
        o = _nn(sc.astype(bf16), vb) + _nn((q * qd).astype(bf16), rv.astype(bf16))
        o_ref[...] = o
        r_acc[...] = rv * gc + _tn((k * kd).astype(bf16), vb)
        y, _ = _head_ln(o)
        gate = gate_ref[...]
        m_ref[...] = y * g_ref[...] * (gate * jax.nn.sigmoid(gate))

    def col(off):
        return pl.BlockSpec((c, HEAD_DIM), lambda b, h, n: (b * nc + n, off + h))

    tab = pl.BlockSpec((c, HEAD_DIM), lambda b, h, n: (n, 0))
    return pl.pallas_call(
        body, name="ret_fwd", grid=(nb, RET_HEADS, nc),
        in_specs=[col(0), col(4), col(8), col(12), tab, tab, pl.BlockSpec((1, HEAD_DIM), lambda b, h, n: (0, h))],
        out_specs=[col(0), pl.BlockSpec((1, HEAD_DIM, HEAD_DIM), lambda b, h, n: ((b * RET_HEADS + h) * nc + n, 0, 0)),
                   col(0)],
        out_shape=[jax.ShapeDtypeStruct((t, RET_WIDTH), f32),
                   jax.ShapeDtypeStruct((nb * RET_HEADS * nc, HEAD_DIM, HEAD_DIM), f32),
                   jax.ShapeDtypeStruct((t, RET_WIDTH), f32)],
        scratch_shapes=[pltpu.VMEM((HEAD_DIM, HEAD_DIM), f32)],
        compiler_params=_cp("parallel", "parallel", "arbitrary"),
    )(proj, proj, proj, proj, cos, sin, ret_g)


def _ret_bwd(dmerged, o_raw, rprev, proj, cos, sin, ret_g, nb, s):
    c = RET_CHUNK
    nc = s // c
    t = nb * s
    scale = HEAD_DIM ** -0.5

    def body(dm_ref, o_ref, rprev_ref, q_ref, k_ref, v_ref, gate_ref, cos_ref, sin_ref, g_ref,
             dq_ref, dk_ref, dv_ref, dgate_ref, dg_ref, dr_acc):
        b, n = pl.program_id(1), pl.program_id(2)

        @pl.when(n == 0)
        def _():
            dr_acc[...] = jnp.zeros_like(dr_acc)

        @pl.when((n == 0) & (b == 0))
        def _():
            dg_ref[...] = jnp.zeros_like(dg_ref)
        decay, kd, qd, gc = _ret_consts(pl.program_id(0))
        cs, sn = cos_ref[...], sin_ref[...]
        q = _rope(q_ref[...], cs, sn)
        k = _rope(k_ref[...], cs, sn) * scale
        qb, kb = q.astype(bf16), k.astype(bf16)
        vb = v_ref[...].astype(bf16)
        sc = _nt(qb, kb) * decay
        gv = g_ref[...]
        y, rs = _head_ln(o_ref[...])
        gate = gate_ref[...]
        sg = jax.nn.sigmoid(gate)
        silu = gate * sg
        dm = dm_ref[...]
        dgate_ref[...] = dm * y * gv * (sg * (1.0 + gate * (1.0 - sg)))
        dyl = dm * gv * silu
        dg_ref[...] += jnp.sum(dm * y * silu, axis=0, keepdims=True)
        do = rs * (dyl - jnp.mean(dyl, axis=-1, keepdims=True) - y * jnp.mean(dyl * y, axis=-1, keepdims=True))
        dob = do.astype(bf16)
        rv = rprev_ref[0]
        drn = dr_acc[...]
        drb = drn.astype(bf16)
        ds = (_nt(dob, vb) * decay).astype(bf16)
        kdb = (k * kd).astype(bf16)
        qdb = (q * qd).astype(bf16)
        dq_r = _nn(ds, kb) + _nt(dob, rv.astype(bf16)) * qd
        dk_r = _tn(ds, qb) + _nt(vb, drb) * kd
        dv_ref[...] = _tn(sc.astype(bf16), dob) + _nn(kdb, drb)
        dr_acc[...] = drn * gc + _tn(qdb, dob)
        dq_ref[...] = _rope_t(dq_r, cs, sn)
        dk_ref[...] = _rope_t(dk_r * scale, cs, sn)

    def col(off):
        return pl.BlockSpec((c, HEAD_DIM), lambda h, b, n: (b * nc + nc - 1 - n, off + h))

    tab = pl.BlockSpec((c, HEAD_DIM), lambda h, b, n: (nc - 1 - n, 0))
    gsp = pl.BlockSpec((1, HEAD_DIM), lambda h, b, n: (0, h))
    out_t = jax.ShapeDtypeStruct((t, RET_WIDTH), f32)
    return pl.pallas_call(
        body, name="ret_bwd", grid=(RET_HEADS, nb, nc),
        in_specs=[col(0), col(0),
                  pl.BlockSpec((1, HEAD_DIM, HEAD_DIM), lambda h, b, n: ((b * RET_HEADS + h) * nc + nc - 1 - n, 0, 0)),
                  col(0), col(4), col(8), col(12), tab, tab, gsp],
        out_specs=[col(0), col(0), col(0), col(0), gsp],
        out_shape=[out_t, out_t, out_t, out_t, jax.ShapeDtypeStruct((1, RET_WIDTH), f32)],
        scratch_shapes=[pltpu.VMEM((HEAD_DIM, HEAD_DIM), f32)],
        compiler_params=_cp("parallel", "arbitrary", "arbitrary"),
    )(dmerged, o_raw, rprev, proj, proj, proj, proj, cos, sin, ret_g)


def _neg_expm1(z):
    series = -(z * (1.0 + z * (0.5 + z * (1.0 / 6.0 + z * (1.0 / 24.0)))))
    return jnp.where(z > -0.01, series, 1.0 - jnp.exp(z))


def _lru_gates(xc, pa, pi, lam):
    r = jax.nn.sigmoid(pa)
    i = jax.nn.sigmoid(pi)
    log_a = -LRU_C * r * jax.nn.softplus(-lam)
    a = jnp.exp(log_a)
    bx = jnp.sqrt(_neg_expm1(2.0 * log_a)) * i * xc
    return a, bx


def _scan_fwd(a, b, row):
    d = 1
    while d < a.shape[0]:
        b = a * _shift_dn(b, d, row) + b
        a = a * _shift_dn(a, d, row, 1.0)
        d *= 2
    return b


def _scan_bwd(c, b, row):
    d = 1
    while d < c.shape[0]:
        b = c * _shift_up(b, d, row) + b
        c = c * _shift_up(c, d, row, 1.0)
        d *= 2
    return b


def _conv_fwd(x, cw, cb, row):
    return (cb + cw[3:4] * x + cw[2:3] * _shift_dn(x, 1, row) + cw[1:2] * _shift_dn(x, 2, row)
            + cw[0:1] * _shift_dn(x, 3, row))


def _lru_specs(s, order):
    def im(f):
        return (lambda b, g: f(b, g)) if order == "bg" else (lambda g, b: f(b, g))
    seq = lambda off: pl.BlockSpec((s, 128), im(lambda b, g: (b, off + g)))
    vec = pl.BlockSpec((1, 128), im(lambda b, g: (0, g)))
    cw = pl.BlockSpec((4, 128), im(lambda b, g: (0, g)))
    mat = pl.BlockSpec((1, 128, 128), im(lambda b, g: (g, 0, 0)))
    return seq, vec, cw, mat


def _lru_fwd(proj, conv_w, conv_b, w_a, b_a, w_i, b_i, lam, nb, s):
    def body(x_ref, gt_ref, cw_ref, cb_ref, wa_ref, ba_ref, wi_ref, bi_ref, lam_ref, out_ref):
        row = lax.broadcasted_iota(jnp.int32, (s, 128), 0)
        xc = _conv_fwd(x_ref[...], cw_ref[...], cb_ref[...], row)
        xcb = xc.astype(bf16)
        pa = _nn(xcb, wa_ref[0].astype(bf16)) + ba_ref[...]
        pi = _nn(xcb, wi_ref[0].astype(bf16)) + bi_ref[...]
        a, bx = _lru_gates(xc, pa, pi, lam_ref[...])
        h = _scan_fwd(a, bx, row)
        out_ref[...] = h * jax.nn.gelu(gt_ref[...])

    seq, vec, cw, mat = _lru_specs(s, "bg")
    return pl.pallas_call(
        body, name="lru_fwd", grid=(nb, LRU_BLOCKS),
        in_specs=[seq(16), seq(20), cw, vec, mat, vec, mat, vec, vec],
        out_specs=seq(0),
        out_shape=jax.ShapeDtypeStruct((nb * s, LRU_WIDTH), f32),
        compiler_params=_cp("parallel", "parallel"),
    )(proj, proj, conv_w, conv_b, w_a, b_a, w_i, b_i, lam)


def _lru_bwd(dmerged, proj, conv_w, conv_b, w_a, b_a, w_i, b_i, lam, nb, s):
    def body(dout_ref, x_ref, gt_ref, cw_ref, cb_ref, wa_ref, ba_ref, wi_ref, bi_ref, lam_ref,
             dx_ref, dgt_ref, dcw_ref, dcb_ref, dwa_ref, dba_ref, dwi_ref, dbi_ref, dlam_ref):
        row = lax.broadcasted_iota(jnp.int32, (s, 128), 0)
        x = x_ref[...]
        cwv = cw_ref[...]
        xc = _conv_fwd(x, cwv, cb_ref[...], row)
        xcb = xc.astype(bf16)
        wab, wib = wa_ref[0].astype(bf16), wi_ref[0].astype(bf16)
        pa = _nn(xcb, wab) + ba_ref[...]
        pi = _nn(xcb, wib) + bi_ref[...]
        (a, bx), gates_vjp = jax.vjp(_lru_gates, xc, pa, pi, lam_ref[...])
        h = _scan_fwd(a, bx, row)
        ge, gelu_vjp = jax.vjp(jax.nn.gelu, gt_ref[...])
        dout = dout_ref[...]
        dgt_ref[...] = gelu_vjp(dout * h)[0]
        adj = _scan_bwd(_shift_up(a, 1, row), dout * ge, row)
        dxc, dpa, dpi, dlam = gates_vjp((adj * _shift_dn(h, 1, row), adj))
        dpab, dpib = dpa.astype(bf16), dpi.astype(bf16)
        dxc = dxc + _nt(dpab, wab) + _nt(dpib, wib)
        dx_ref[...] = (cwv[3:4] * dxc + cwv[2:3] * _shift_up(dxc, 1, row) + cwv[1:2] * _shift_up(dxc, 2, row)
                       + cwv[0:1] * _shift_up(dxc, 3, row))

        @pl.when(pl.program_id(1) == 0)
        def _():
            for r in (dcw_ref, dcb_ref, dwa_ref, dba_ref, dwi_ref, dbi_ref, dlam_ref):
                r[...] = jnp.zeros_like(r)
        rsum = lambda v: jnp.sum(v, axis=0, keepdims=True)
        dcw_ref[...] += jnp.concatenate([rsum(dxc * _shift_dn(x, 3, row)), rsum(dxc * _shift_dn(x, 2, row)),
                                         rsum(dxc * _shift_dn(x, 1, row)), rsum(dxc * x)], axis=0)
        dcb_ref[...] += rsum(dxc)
        dwa_ref[0] += _tn(xcb, dpab)
        dwi_ref[0] += _tn(xcb, dpib)
        dba_ref[...] += rsum(dpa)
        dbi_ref[...] += rsum(dpi)
        dlam_ref[...] += dlam

    seq, vec, cw, mat = _lru_specs(s, "gb")
    t = nb * s
    vshape = jax.ShapeDtypeStruct((1, LRU_WIDTH), f32)
    mshape = jax.ShapeDtypeStruct((LRU_BLOCKS, 128, 128), f32)
    return pl.pallas_call(
        body, name="lru_bwd", grid=(LRU_BLOCKS, nb),
        in_specs=[seq(4), seq(16), seq(20), cw, vec, mat, vec, mat, vec, vec],
        out_specs=[seq(0), seq(0), cw, vec, mat, vec, mat, vec, vec],
        out_shape=[jax.ShapeDtypeStruct((t, LRU_WIDTH), f32), jax.ShapeDtypeStruct((t, LRU_WIDTH), f32),
                   jax.ShapeDtypeStruct((4, LRU_WIDTH), f32), vshape, mshape, vshape, mshape, vshape, vshape],
        compiler_params=_cp("parallel", "arbitrary"),
    )(dmerged, proj, proj, conv_w, conv_b, w_a, b_a, w_i, b_i, lam)


def _s5_disc(lr, li, ldt, bre, bim):
    dt = jnp.exp(ldt)
    mag = jnp.exp(lr * dt)
    lbr = mag * jnp.cos(li * dt)
    lbi = mag * jnp.sin(li * dt)
    den = lr * lr + li * li
    nr = lbr - 1.0
    fr = (nr * lr + lbi * li) / den
    fi = (lbi * lr - nr * li) / den
    bbr = fr[:, None, :] * bre - fi[:, None, :] * bim
    bbi = fr[:, None, :] * bim + fi[:, None, :] * bre
    return lbr, lbi, bbr, bbi


def _s5_prep(lr, li, ldt, bre, bim):
    def body(lr_ref, li_ref, ldt_ref, bre_ref, bim_ref, o1, o2, o3, o4):
        o1[...], o2[...], o3[...], o4[...] = _s5_disc(lr_ref[...], li_ref[...], ldt_ref[...], bre_ref[...], bim_ref[...])

    return pl.pallas_call(
        body, name="s5_prep", in_specs=[VMEM_SPEC] * 5, out_specs=[VMEM_SPEC] * 4,
        out_shape=[jax.ShapeDtypeStruct(lr.shape, f32), jax.ShapeDtypeStruct(lr.shape, f32),
                   jax.ShapeDtypeStruct(bre.shape, f32), jax.ShapeDtypeStruct(bre.shape, f32)],
    )(lr, li, ldt, bre, bim)


def _s5_prep_bwd(lr, li, ldt, bre, bim, cts):
    def body(lr_ref, li_ref, ldt_ref, bre_ref, bim_ref, g1, g2, g3, g4, o1, o2, o3, o4, o5):
        _, vjp = jax.vjp(_s5_disc, lr_ref[...], li_ref[...], ldt_ref[...], bre_ref[...], bim_ref[...])
        o1[...], o2[...], o3[...], o4[...], o5[...] = vjp((g1[...], g2[...], g3[...], g4[...]))

    return pl.pallas_call(
        body, name="s5_prep_bwd", in_specs=[VMEM_SPEC] * 9, out_specs=[VMEM_SPEC] * 5,
        out_shape=[jax.ShapeDtypeStruct(v.shape, f32) for v in (lr, li, ldt, bre, bim)],
    )(lr, li, ldt, bre, bim, *cts)


def _cmul(ar, ai, br, bi):
    return ar * br - ai * bi, ar * bi + ai * br


def _s5_pow_table(lr, li, n, row, up):
    ar = jnp.broadcast_to(lr, (n, lr.shape[1]))
    ai = jnp.broadcast_to(li, (n, li.shape[1]))
    shift = _shift_up if up else _shift_dn
    d = 1
    while d < n:
        ar, ai = _cmul(ar, ai, shift(ar, d, row, 1.0), shift(ai, d, row, 0.0))
        d *= 2
    return ar, ai


def _s5_scan(br, bi, lr, li, tab_r, tab_i, cr, ci, row, up):
    n = br.shape[0]
    sub = row & (S5_SUBLANES - 1)
    pr, pi = lr, li
    d = 1
    while d < S5_SUBLANES:
        keep = (sub < S5_SUBLANES - d) if up else (sub >= d)
        shift = n - d if up else d
        tr, ti = _cmul(pr, pi, pltpu.roll(br, shift, 0), pltpu.roll(bi, shift, 0))
        br, bi = br + jnp.where(keep, tr, 0.0), bi + jnp.where(keep, ti, 0.0)
        pr, pi = _cmul(pr, pi, pr, pi)
        d *= 2
    groups = list(range(n // S5_SUBLANES))
    out_r, out_i = [None] * len(groups), [None] * len(groups)
    edge = slice(0, 1) if up else slice(S5_SUBLANES - 1, S5_SUBLANES)
    for g in (reversed(groups) if up else groups):
        rows = slice(g * S5_SUBLANES, (g + 1) * S5_SUBLANES)
        tr, ti = _cmul(tab_r, tab_i, cr, ci)
        hr, hi = br[rows] + tr, bi[rows] + ti
        out_r[g], out_i[g] = hr, hi
        cr, ci = hr[edge], hi[edge]
    return jnp.concatenate(out_r, axis=0), jnp.concatenate(out_i, axis=0)


def _s5_specs(s, nc, order):
    def im(f):
        return (lambda b, k: f(b, k)) if order == "bk" else (lambda k, b: f(b, k))
    seq = pl.BlockSpec((s, 128), im(lambda b, k: (b, k)))
    lvec = pl.BlockSpec((1, S5_BLOCK_STATES), im(lambda b, k: (0, k)))
    dvec = pl.BlockSpec((1, 128), im(lambda b, k: (0, k)))
    wmat = pl.BlockSpec((1, 128, S5_BLOCK_STATES), im(lambda b, k: (k, 0, 0)))
    h0 = pl.BlockSpec((1, nc, 2, S5_BLOCK_STATES), im(lambda b, k: (b * S5_BLOCKS + k, 0, 0, 0)))
    return seq, lvec, dvec, wmat, h0


def _s5_fwd(u, lbr, lbi, wbr, wbi, wcr, wci, dskip, nb, s):
    ln = S5_CHUNK
    nc = s // ln

    def body(u_ref, lr_ref, li_ref, wbr_ref, wbi_ref, wcr_ref, wci_ref, d_ref, yg_ref, y_ref, h0_ref):
        row = lax.broadcasted_iota(jnp.int32, (ln, S5_BLOCK_STATES), 0)
        lr, li = lr_ref[...], li_ref[...]
        pr, pi = _s5_pow_table(lr, li, S5_SUBLANES, row[:S5_SUBLANES], False)
        dv = d_ref[...]

        def step(n, carry):
            h0r, h0i = carry
            st = pl.multiple_of(n * ln, ln)
            uc = u_ref[pl.ds(st, ln), :]
            ub = uc.astype(bf16)
            hr, hi = _s5_scan(_nn(ub, wbr_ref[0]), _nn(ub, wbi_ref[0]), lr, li, pr, pi, h0r, h0i, row, False)
            h0_ref[0, n, 0:1, :] = h0r
            h0_ref[0, n, 1:2, :] = h0i
            y = _nt(hr.astype(bf16), wcr_ref[0]) - _nt(hi.astype(bf16), wci_ref[0]) + dv * uc
            y_ref[pl.ds(st, ln), :] = y
            yg_ref[pl.ds(st, ln), :] = jax.nn.gelu(y).astype(bf16)
            return hr[ln - 1:ln, :], hi[ln - 1:ln, :]

        z = jnp.zeros((1, S5_BLOCK_STATES), f32)
        lax.fori_loop(0, nc, step, (z, z))

    seq, lvec, dvec, wmat, h0 = _s5_specs(s, nc, "bk")
    t = nb * s
    return pl.pallas_call(
        body, name="s5_fwd", grid=(nb, S5_BLOCKS),
        in_specs=[seq, lvec, lvec, wmat, wmat, wmat, wmat, dvec],
        out_specs=[seq, seq, h0],
        out_shape=[jax.ShapeDtypeStruct((t, D_MODEL), bf16), jax.ShapeDtypeStruct((t, D_MODEL), f32),
                   jax.ShapeDtypeStruct((nb * S5_BLOCKS, nc, 2, S5_BLOCK_STATES), f32)],
        compiler_params=_cp("parallel", "parallel"),
    )(u, lbr, lbi, wbr, wbi, wcr, wci, dskip)


def _s5_bwd(dyg, y, u, h0, lbr, lbi, wbr, wbi, wcr, wci, dskip, nb, s):
    ln = S5_CHUNK
    nc = s // ln

    def body(dyg_ref, y_ref, u_ref, h0_ref, lr_ref, li_ref, wbr_ref, wbi_ref, wcr_ref, wci_ref, d_ref,
             du_ref, dlr_ref, dli_ref, dwbr_ref, dwbi_ref, dwcr_ref, dwci_ref, dd_ref):
        @pl.when(pl.program_id(1) == 0)
        def _():
            for r in (dlr_ref, dli_ref, dwbr_ref, dwbi_ref, dwcr_ref, dwci_ref, dd_ref):
                r[...] = jnp.zeros_like(r)
        row = lax.broadcasted_iota(jnp.int32, (ln, S5_BLOCK_STATES), 0)
        lr, li = lr_ref[...], li_ref[...]
        pr, pi = _s5_pow_table(lr, li, S5_SUBLANES, row[:S5_SUBLANES], False)
        qr, qi = _s5_pow_table(lr, -li, S5_SUBLANES, row[:S5_SUBLANES], True)
        dv = d_ref[...]
        rsum = lambda v: jnp.sum(v, axis=0, keepdims=True)

        def step(i, carry):
            gnr, gni = carry
            n = nc - 1 - i
            st = pl.multiple_of(n * ln, ln)
            uc = u_ref[pl.ds(st, ln), :]
            ub = uc.astype(bf16)
            h0v = h0_ref[0, n]
            h0r, h0i = h0v[0:1], h0v[1:2]
            hr, hi = _s5_scan(_nn(ub, wbr_ref[0]), _nn(ub, wbi_ref[0]), lr, li, pr, pi, h0r, h0i, row, False)
            dy = jax.vjp(jax.nn.gelu, y_ref[pl.ds(st, ln), :])[1](dyg_ref[pl.ds(st, ln), :])[0]
            dyb = dy.astype(bf16)
            dd_ref[...] += rsum(dy * uc)
            gr, gi = _s5_scan(_nn(dyb, wcr_ref[0]), -_nn(dyb, wci_ref[0]), lr, -li, qr, qi, gnr, gni, row, True)
            hpr = jnp.where(row >= 1, pltpu.roll(hr, 1, 0), h0r)
            hpi = jnp.where(row >= 1, pltpu.roll(hi, 1, 0), h0i)
            dlr_ref[...] += rsum(gr * hpr + gi * hpi)
            dli_ref[...] += rsum(gi * hpr - gr * hpi)
            grb, gib = gr.astype(bf16), gi.astype(bf16)
            dwbr_ref[0] += _tn(ub, grb)
            dwbi_ref[0] += _tn(ub, gib)
            dwcr_ref[0] += _tn(dyb, hr.astype(bf16))
            dwci_ref[0] -= _tn(dyb, hi.astype(bf16))
            du_ref[pl.ds(st, ln), :] = _nt(grb, wbr_ref[0]) + _nt(gib, wbi_ref[0]) + dv * dy
            return gr[0:1, :], gi[0:1, :]

        z = jnp.zeros((1, S5_BLOCK_STATES), f32)
        lax.fori_loop(0, nc, step, (z, z))

    seq, lvec, dvec, wmat, h0s = _s5_specs(s, nc, "kb")
    t = nb * s
    lshape = jax.ShapeDtypeStruct((1, S5_BLOCKS * S5_BLOCK_STATES), f32)
    wshape = jax.ShapeDtypeStruct((S5_BLOCKS, 128, S5_BLOCK_STATES), f32)
    return pl.pallas_call(
        body, name="s5_bwd", grid=(S5_BLOCKS, nb),
        in_specs=[seq, seq, seq, h0s, lvec, lvec, wmat, wmat, wmat, wmat, dvec],
        out_specs=[seq, lvec, lvec, wmat, wmat, wmat, wmat, dvec],
        out_shape=[jax.ShapeDtypeStruct((t, D_MODEL), f32), lshape, lshape, wshape, wshape, wshape, wshape,
                   jax.ShapeDtypeStruct((1, D_MODEL), f32)],
        compiler_params=_cp("parallel", "arbitrary"),
    )(dyg, y, u, h0, lbr, lbi, wbr, wbi, wcr, wci, dskip)


def _blockdiag(w):
    w4 = w.reshape(S5_BLOCKS, 8, S5_GROUP, S5_STATE)
    same_group = jnp.eye(8, dtype=bool)[None, :, None, :, None]
    return jnp.where(same_group, w4[:, :, :, None, :], 0.0).reshape(S5_BLOCKS, 128, S5_BLOCK_STATES)


def _blockdiag_t(dw):
    d5 = dw.reshape(S5_BLOCKS, 8, S5_GROUP, 8, S5_STATE)
    diag = jnp.diagonal(d5, axis1=1, axis2=3)
    return jnp.moveaxis(diag, 3, 1).reshape(S5_GROUPS, S5_GROUP, S5_STATE)


def _glu_fwd(ygb, wa, wb, x, tm=512, tn=512):
    t, d = x.shape

    def body(y_ref, wa_ref, wb_ref, x_ref, o_ref, p_ref, q_ref):
        p = _nn(y_ref[...], wa_ref[...])
        q = _nn(y_ref[...], wb_ref[...])
        p_ref[...] = p
        q_ref[...] = q
        o_ref[...] = x_ref[...] + p * jax.nn.sigmoid(q)

    tile = pl.BlockSpec((tm, tn), lambda i, j: (i, j))
    wsp = pl.BlockSpec((d, tn), lambda i, j: (0, j))
    out = jax.ShapeDtypeStruct((t, d), f32)
    return pl.pallas_call(
        body, name="glu_fwd", grid=(t // tm, d // tn),
        in_specs=[pl.BlockSpec((tm, d), lambda i, j: (i, 0)), wsp, wsp, tile],
        out_specs=[tile, tile, tile], out_shape=[out, out, out],
        compiler_params=_cp("parallel", "parallel"),
    )(ygb, wa, wb, x)


def _place():
    x, y, c = lax.axis_index("x"), lax.axis_index("y"), lax.axis_index("c")
    return x, y, c, [(1 - x, y), (x, 1 - y), (1 - x, 1 - y)]


def _all_gather(name, arrays):
    n = len(arrays)

    def body(*refs):
        ins, outs = refs[:n], refs[n:2 * n]
        send_sems, recv_sems, local_sems = refs[2 * n:]
        x, y, c, chips = _place()
        me, sib = (x, y, c), (x, y, 1 - c)

        def copy(i, k, block, to, src=None):
            dst = outs[i].at[4 * block[0] + 2 * block[1] + block[2]]
            return pltpu.make_async_remote_copy(
                src_ref=dst if src is None else src, dst_ref=dst,
                send_sem=send_sems.at[i * 7 + k], recv_sem=recv_sems.at[i * 7 + k],
                device_id=to, device_id_type=MESH)

        mine = [pltpu.make_async_copy(ins[i], outs[i].at[4 * x + 2 * y + c], local_sems.at[i]) for i in range(n)]
        for m in mine:
            m.start()
        first = []
        for i in range(n):
            first.append(copy(i, 0, me, sib, src=ins[i]))
            first += [copy(i, 1 + j, me, (*chip, c), src=ins[i]) for j, chip in enumerate(chips)]
        for cp in first:
            cp.start()
        passed = []
        for j, chip in enumerate(chips):
            for i in range(n):
                copy(i, 1 + j, (*chip, c), me).wait_recv()
                fwd = copy(i, 4 + j, (*chip, c), sib)
                fwd.start()
                passed.append(fwd)
        for i in range(n):
            copy(i, 0, sib, me).wait_recv()
        for j, chip in enumerate(chips):
            for i in range(n):
                copy(i, 4 + j, (*chip, 1 - c), me).wait_recv()
        for cp in first + passed:
            cp.wait_send()
        for m in mine:
            m.wait()

    return pl.pallas_call(
        body, name=name,
        in_specs=[ANY_SPEC] * n, out_specs=[ANY_SPEC] * n,
        out_shape=[jax.ShapeDtypeStruct((N_DEV,) + a.shape, a.dtype) for a in arrays],
        scratch_shapes=[pltpu.SemaphoreType.DMA((7 * n,)), pltpu.SemaphoreType.DMA((7 * n,)),
                        pltpu.SemaphoreType.DMA((n,))],
    )(*arrays)


def _tie(name, x, deps):
    def body(*refs):
        pass

    return pl.pallas_call(
        body, name=name, in_specs=[ANY_SPEC] * (1 + len(deps)), out_specs=ANY_SPEC,
        out_shape=jax.ShapeDtypeStruct(x.shape, x.dtype), input_output_aliases={0: 0},
    )(x, *deps)


def _xchg_copies(kind, srcs, lands, suffixes, send_sems, recv_sems):
    x, y, c, _ = _place()
    copies = []
    for i, (src, land, sfx) in enumerate(zip(srcs, lands, suffixes)):
        for k in range(N_DEV - 1):
            r = k + 1
            peer = (1 - x if r & 4 else x, 1 - y if r & 2 else y, 1 - c if r & 1 else c)
            if kind == "gather":
                s_ref, d_ref = src, land.at[(4 * x + 2 * y + c,) + sfx]
            else:
                s_ref, d_ref = src.at[4 * peer[0] + 2 * peer[1] + peer[2]], land.at[(k,) + sfx]
            copies.append(pltpu.make_async_remote_copy(
                src_ref=s_ref, dst_ref=d_ref, send_sem=send_sems.at[i * 7 + k], recv_sem=recv_sems.at[i * 7 + k],
                device_id=peer, device_id_type=MESH))
    return copies


def _xchg_start(name, kind, srcs, lands, suffixes=None):
    n = len(srcs)
    suffixes = suffixes or [()] * n

    def body(*refs):
        src, land = refs[:n], refs[n:2 * n]
        send_sems, recv_sems, token = refs[2 * n], refs[2 * n + 1], refs[-1]
        for cp in _xchg_copies(kind, src, land, suffixes, send_sems, recv_sems):
            cp.start()
        token[...] = jnp.zeros_like(token)

    arrays = list(srcs) + list(lands)
    outs = pl.pallas_call(
        body, name=name,
        out_shape=(pltpu.SemaphoreType.DMA((7 * n,)), pltpu.SemaphoreType.DMA((7 * n,)),
                   *[pltpu.HBM(a.shape, a.dtype) for a in arrays], jax.ShapeDtypeStruct((8, 128), f32)),
        in_specs=[HBM_SPEC] * (2 * n),
        out_specs=(SEM_SPEC, SEM_SPEC, *[HBM_SPEC] * (2 * n), VMEM_SPEC),
        input_output_aliases={i: 2 + i for i in range(2 * n)},
        compiler_params=pltpu.CompilerParams(has_side_effects=SIDE_EFFECT),
    )(*[pltpu.with_memory_space_constraint(a, pltpu.HBM) for a in arrays])
    return dict(kind=kind, n=n, suffixes=suffixes, send=outs[0], recv=outs[1], srcs=list(outs[2:2 + n]),
                lands=list(outs[2 + n:2 + 2 * n]), token=outs[-1])


def _xchg_wait(name, h, after, lands=None):
    n = h["n"]
    lands = h["lands"] if lands is None else lands

    def body(*refs):
        src, land = refs[:n], refs[n:2 * n]
        for cp in _xchg_copies(h["kind"], src, land, h["suffixes"], refs[2 * n], refs[2 * n + 1]):
            cp.wait_send()
            cp.wait_recv()

    arrays = h["srcs"] + list(lands)
    outs = pl.pallas_call(
        body, name=name,
        out_shape=tuple(pltpu.HBM(a.shape, a.dtype) for a in arrays),
        in_specs=[HBM_SPEC] * (2 * n) + [SEM_SPEC, SEM_SPEC] + [ANY_SPEC] * len(after),
        out_specs=tuple([HBM_SPEC] * (2 * n)),
        input_output_aliases={i: i for i in range(2 * n)},
        compiler_params=pltpu.CompilerParams(has_side_effects=SIDE_EFFECT),
    )(*arrays, h["send"], h["recv"], *after)
    return list(outs[n:])


def _rows(a):
    return a.reshape(-1, a.shape[-1])


def _row_tile(r):
    for tm in (512, 256, 128, 64, 32, 16, 8):
        if r % tm == 0:
            return tm
    return r


def _sum8(name, gathered):
    _, r, n = gathered.shape
    tm = _row_tile(r)

    def body(g_ref, o_ref):
        acc = g_ref[0]
        for k in range(1, N_DEV):
            acc = acc + g_ref[k]
        o_ref[...] = acc

    return pl.pallas_call(
        body, name=name, grid=(r // tm,),
        in_specs=[pl.BlockSpec((N_DEV, tm, n), lambda i: (0, i, 0))],
        out_specs=pl.BlockSpec((tm, n), lambda i: (i, 0)),
        out_shape=jax.ShapeDtypeStruct((r, n), f32),
        compiler_params=_cp("parallel"),
    )(gathered)


def _adamw(name, w, m, v, own, landed=None):
    shape = w.shape
    w2, m2, v2, o2 = _rows(w), _rows(m), _rows(v), _rows(own)
    r, n = w2.shape
    tm = _row_tile(r)
    c1 = 1.0 - ADAM_B1 ** ADAM_STEP
    c2 = 1.0 - ADAM_B2 ** ADAM_STEP
    extra = [] if landed is None else [landed.reshape(landed.shape[0], r, n)]

    def body(w_ref, m_ref, v_ref, o_ref, *refs):
        g = o_ref[...]
        if extra:
            for k in range(extra[0].shape[0]):
                g = g + refs[0][k].astype(f32)
        g_ref, d_ref, mn_ref, vn_ref = refs[len(extra):]
        mn = ADAM_B1 * m_ref[...] + (1.0 - ADAM_B1) * g
        vn = ADAM_B2 * v_ref[...] + (1.0 - ADAM_B2) * (g * g)
        g_ref[...] = g
        d_ref[...] = -ADAM_LR * ((mn / c1) / (jnp.sqrt(vn / c2) + ADAM_EPS) + ADAM_WD * w_ref[...])
        mn_ref[...] = mn
        vn_ref[...] = vn

    row = pl.BlockSpec((tm, n), lambda i: (i, 0))
    outs = pl.pallas_call(
        body, name=name, grid=(r // tm,),
        in_specs=[row] * 4 + [pl.BlockSpec((e.shape[0], tm, n), lambda i: (0, i, 0)) for e in extra],
        out_specs=[row] * 4, out_shape=[jax.ShapeDtypeStruct((r, n), f32)] * 4,
        compiler_params=_cp("parallel"),
    )(w2, m2, v2, o2, *extra)
    return [o.reshape(shape) for o in outs]


def _pack(arrays):
    flat = jnp.concatenate([a.reshape(-1).astype(f32) for a in arrays])
    pad = (-flat.shape[0]) % (128 * (512 if flat.shape[0] > 128 * 512 else 8))
    return jnp.pad(flat, (0, pad)).reshape(-1, 128)


def _unpack(packed, shapes):
    flat = packed.reshape(-1)
    out, off = [], 0
    for s in shapes:
        n = math.prod(s)
        out.append(flat[off:off + n].reshape(s))
        off += n
    return out


def _local_step(x, target, w, weights_of, send, nb, s):
    cos, sin = _rope_tables(s)
    g = {}
    ffn_saved = {}
    ffn_bufs = [lax.empty((N_DEV, 2, 2) + shp, f32)
                for shp in ((D_MODEL, FF_SHARD), (D_MODEL, FF_SHARD), (FF_SHARD, D_MODEL))]

    def ffn(xin, l, h, wts):
        y, a, b = _ffn_fwd(f"ffn_fwd_{l}{h}", xin, w["ffn_g"][l][h], *wts)
        ffn_saved[(l, h)] = (xin, a, b, wts)
        return y

    def ffn_back(dy, l, h):
        xin, a, b, wts = ffn_saved[(l, h)]
        dx, dg, hb, dyh, u, da, db = _ffn_dx(f"ffn_dx_{l}{h}", dy, xin, w["ffn_g"][l][h], *wts, a, b)
        ffn_bufs[:], halves = _ffn_dw(f"ffn_dw_{l}{h}", hb, dyh, u, da, db, ffn_bufs, l, h)
        g[f"ffn_g_{l}{h}"] = dg
        return send(f"ffn_{l}{h}", dict(zip(("ffn_w1", "ffn_w3", "ffn_w2"), halves)), dx)

    def slots(t):
        return t.reshape(N_DEV, D_MODEL // N_DEV, D_MODEL)

    x1 = ffn(x, 0, 0, weights_of(0, [])["ffn"])
    wg = weights_of(1, [x1])
    w_in, w_out = wg["w_in"], wg["w_out"]
    _, h0b = _norm_fwd("mix_norm_0", x1, w["mix_g"][0])
    proj = _mm("in_proj", h0b, w_in, "nn", tn=768)[0]
    o_raw, rprev, mret = _ret_fwd(proj, cos, sin, w["ret_g"], nb, s)
    lru = _lru_fwd(proj, w["conv_w"], w["conv_b"], w["lru_w_a"], w["lru_b_a"], w["lru_w_i"], w["lru_b_i"], w["lru_lam"], nb, s)
    merged = _ew("merge", lambda a, b: (jnp.concatenate([a, b], axis=1),), [mret, lru], [(D_MODEL, bf16)])[0]
    x2 = _mm("out_proj", merged, w_out, "nn", extras=[x1], epilogue=lambda acc, r: (acc + r,))[0]
    x3 = ffn(x2, 0, 1, weights_of(2, [x2])["ffn"])
    wg = weights_of(3, [x3])
    glu_a, glu_b = wg["glu_a"], wg["glu_b"]
    x4 = ffn(x3, 1, 0, wg["ffn"])
    u, _ = _norm_fwd("mix_norm_1", x4, w["mix_g"][1])
    lbr, lbi, bbr, bbi = _s5_prep(w["s5_lr"], w["s5_li"], w["s5_ldt"], w["s5_bre"], w["s5_bim"])
    lbr_f, lbi_f = lbr.reshape(1, -1), lbi.reshape(1, -1)
    wbr, wbi = _blockdiag(bbr).astype(bf16), _blockdiag(bbi).astype(bf16)
    wcr, wci = _blockdiag(w["s5_cre"]).astype(bf16), _blockdiag(w["s5_cim"]).astype(bf16)
    ygb, ypre, h0s = _s5_fwd(u, lbr_f, lbi_f, wbr, wbi, wcr, wci, w["s5_d"], nb, s)
    x5, gp, gq = _glu_fwd(ygb, glu_a, glu_b, x4)
    x6 = ffn(x5, 1, 1, weights_of(4, [x5])["ffn"])
    loss, dx6, g["final_g"] = _final_loss(x6, w["final_g"], target)

    dx5 = ffn_back(dx6, 1, 1)

    def glu_bwd(d, p, q):
        sg = jax.nn.sigmoid(q)
        return d * sg, d * p * sg * (1.0 - sg)

    dp, dq = _ew("glu_bwd", glu_bwd, [dx5, gp, gq], [(D_MODEL, bf16), (D_MODEL, bf16)])
    dyg = _mm("glu_dy_a", dp, glu_a, "nt")[0]
    dyg = _mm("glu_dy_b", dq, glu_b, "nt", extras=[dyg], epilogue=lambda acc, r: (acc + r,))[0]
    g["glu_a"], ga_half = _mm_tn("glu_dw_a", ygb, dp)
    g["glu_b"], gb_half = _mm_tn("glu_dw_b", ygb, dq)
    dyg = send("glu", {"glu_a": slots(ga_half), "glu_b": slots(gb_half)}, dyg)
    du, dlr, dli, dwbr, dwbi, dwcr, dwci, g["s5_d"] = _s5_bwd(dyg, ypre, u, h0s, lbr_f, lbi_f, wbr, wbi, wcr, wci, w["s5_d"], nb, s)
    g["s5_cre"], g["s5_cim"] = _blockdiag_t(dwcr), _blockdiag_t(dwci)
    g["s5_lr"], g["s5_li"], g["s5_ldt"], g["s5_bre"], g["s5_bim"] = _s5_prep_bwd(
        w["s5_lr"], w["s5_li"], w["s5_ldt"], w["s5_bre"], w["s5_bim"],
        (dlr.reshape(S5_GROUPS, S5_STATE), dli.reshape(S5_GROUPS, S5_STATE), _blockdiag_t(dwbr), _blockdiag_t(dwbi)))
    dx4, g["mix_g_1"] = _norm_bwd("mix_norm_1_bwd", du, x4, w["mix_g"][1], dx5)
    dx3 = ffn_back(dx4, 1, 0)
    dx2 = ffn_back(dx3, 0, 1)
    dmerged = _mm("out_proj_dx", dx2, w_out, "nt")[0]
    g["w_out"], wo_half = _mm_tn("out_proj_dw", merged, dx2)
    dmerged = send("w_out", {"w_out": slots(wo_half)}, dmerged)
    dq_, dk_, dv_, dgate, g["ret_g"] = _ret_bwd(dmerged, o_raw, rprev, proj, cos, sin, w["ret_g"], nb, s)
    (dxl, dgl, g["conv_w"], g["conv_b"], g["lru_w_a"], g["lru_b_a"], g["lru_w_i"], g["lru_b_i"], g["lru_lam"]) = _lru_bwd(
        dmerged, proj, w["conv_w"], w["conv_b"], w["lru_w_a"], w["lru_b_a"], w["lru_w_i"], w["lru_b_i"], w["lru_lam"], nb, s)
    dproj = _ew("dproj", lambda *p: (jnp.concatenate(p, axis=1),), [dq_, dk_, dv_, dgate, dxl, dgl], [(3072, bf16)])[0]
    dh0 = _mm("in_proj_dx", dproj, w_in, "nt")[0]
    g["w_in"], wi_half = _mm_tn("in_proj_dw", h0b, dproj)
    dh0 = send("w_in", {"w_in": jnp.transpose(wi_half.reshape(D_MODEL, N_DEV, IN_SHARD), (1, 0, 2))}, dh0)
    dx1, g["mix_g_0"] = _norm_bwd("mix_norm_0_bwd", dh0, x1, w["mix_g"][0], dx2)
    dx0 = ffn_back(dx1, 0, 0)
    g["ffn_w1"], g["ffn_w3"], g["ffn_w2"] = ffn_bufs
    return loss, dx0, g


_WEIGHTS = ["ffn_norm_g", "ffn_w1", "ffn_w3", "ffn_w2", "mix_norm_g", "w_in_even", "w_out_even", "ret_norm_g", "conv_w",
            "conv_b", "lru_w_a", "lru_b_a", "lru_w_i", "lru_b_i", "lru_lambda", "s5_lambda_re", "s5_lambda_im", "s5_log_dt",
            "s5_b_re", "s5_b_im", "s5_c_re", "s5_c_im", "s5_d", "glu_w_a", "glu_w_b", "final_norm_g"]
_BIG = ["ffn_w1", "ffn_w3", "ffn_w2", "w_in_even", "w_out_even", "glu_w_a", "glu_w_b"]
_SMALL_SHARDED = ["ffn_norm_g", "conv_w", "s5_d"]
_SMALL = [n for n in _WEIGHTS if n not in _BIG]


def kernel(x, ffn_norm_g, ffn_w1, ffn_w3, ffn_w2, mix_norm_g, w_in_even, w_out_even, ret_norm_g, conv_w, conv_b, lru_w_a, lru_b_a, lru_w_i, lru_b_i, lru_lambda, s5_lambda_re, s5_lambda_im, s5_log_dt, s5_b_re, s5_b_im, s5_c_re, s5_c_im, s5_d, glu_w_a, glu_w_b, final_norm_g, loss_target, m_ffn_norm_g, m_ffn_w1, m_ffn_w3, m_ffn_w2, m_mix_norm_g, m_w_in_even, m_w_out_even, m_ret_norm_g, m_conv_w, m_conv_b, m_lru_w_a, m_lru_b_a, m_lru_w_i, m_lru_b_i, m_lru_lambda, m_s5_lambda_re, m_s5_lambda_im, m_s5_log_dt, m_s5_b_re, m_s5_b_im, m_s5_c_re, m_s5_c_im, m_s5_d, m_glu_w_a, m_glu_w_b, m_final_norm_g, v_ffn_norm_g, v_ffn_w1, v_ffn_w3, v_ffn_w2, v_mix_norm_g, v_w_in_even, v_w_out_even, v_ret_norm_g, v_conv_w, v_conv_b, v_lru_w_a, v_lru_b_a, v_lru_w_i, v_lru_b_i, v_lru_lambda, v_s5_lambda_re, v_s5_lambda_im, v_s5_log_dt, v_s5_b_re, v_s5_b_im, v_s5_c_re, v_s5_c_im, v_s5_d, v_glu_w_a, v_glu_w_b, v_final_norm_g):
    a = dict(locals())
    nb, s, d = x.shape
    ax, ay, ac = lax.axis_index("x"), lax.axis_index("y"), lax.axis_index("c")
    dev = 4 * ax + 2 * ay + ac
    chip = 2 * ax + ay

    (sm,) = _all_gather("ag_small_weights", [_pack([ffn_norm_g, conv_w, s5_d])])
    sm = sm.reshape(N_DEV, -1)
    ffn_g_full = jnp.transpose(sm[:, :512].reshape(N_DEV, 2, 2, 128), (1, 2, 0, 3)).reshape(2, 2, D_MODEL)
    conv_w_full = jnp.transpose(sm[:, 512:768].reshape(N_DEV, 4, 64), (1, 0, 2)).reshape(4, LRU_WIDTH)
    s5_d_full = sm[:, 768:896].reshape(1, D_MODEL)

    def ffn_shards(l, h):
        return [ffn_w1[l, h].astype(bf16), ffn_w3[l, h].astype(bf16), ffn_w2[l, h].astype(bf16)]

    first = _all_gather("ag_ffn_00", ffn_shards(0, 0))
    ag_src = [None, [w_in_even[0].astype(bf16), w_out_even[0].astype(bf16)], ffn_shards(0, 1),
              ffn_shards(1, 0) + [glu_w_a[0].astype(bf16), glu_w_b[0].astype(bf16)], ffn_shards(1, 1)]
    ag, token = [None], first[0]
    for k, grp in enumerate(ag_src):
        if grp is None:
            continue
        grp[0] = _tie(f"tie_ag_{k}", grp[0], [token])
        lands = [lax.dynamic_update_index_in_dim(lax.empty((N_DEV,) + t.shape, bf16), t, dev, 0) for t in grp]
        ag.append(_xchg_start(f"ag_start_{k}", "gather", grp, lands))
        token = ag[-1]["token"]

    def weights_of(k, after):
        if k == 0:
            return {"ffn": [first[0], _tie("tie_ag_started", first[1], [h["token"] for h in ag[1:]]), first[2]]}
        got = _xchg_wait(f"ag_wait_{k}", ag[k], after)
        if k == 1:
            return {"w_in": jnp.transpose(got[0], (1, 0, 2)).reshape(D_MODEL, N_DEV * IN_SHARD),
                    "w_out": got[1].reshape(D_MODEL, D_MODEL)}
        if k == 3:
            return {"ffn": got[:3], "glu_a": got[3].reshape(D_MODEL, D_MODEL), "glu_b": got[4].reshape(D_MODEL, D_MODEL)}
        return {"ffn": got}

    ffn_lands = [lax.empty((N_DEV - 1, 2, 2) + shp, bf16)
                 for shp in ((D_MODEL, FF_SHARD), (D_MODEL, FF_SHARD), (FF_SHARD, D_MODEL))]
    rs = []

    def send(group, arrays, carry):
        srcs = list(arrays.values())
        if group.startswith("ffn_"):
            sfx = [(int(group[4]), int(group[5]))] * 3
            h = _xchg_start("rs_start_" + group, "scatter", srcs, ffn_lands, sfx)
            ffn_lands[:] = h["lands"]
        else:
            h = _xchg_start("rs_start_" + group, "scatter", srcs,
                            [lax.empty((N_DEV - 1,) + t.shape[1:], bf16) for t in srcs])
        rs.append((group, list(arrays), h))
        return _tie("tie_" + group, carry, [h["token"]])

    w = {
        "ffn_g": [[ffn_g_full[l, h].reshape(1, D_MODEL) for h in range(2)] for l in range(2)],
        "mix_g": [mix_norm_g[0:1], mix_norm_g[1:2]],
        "ret_g": ret_norm_g, "conv_w": conv_w_full, "conv_b": conv_b,
        "lru_w_a": lru_w_a[0], "lru_b_a": lru_b_a, "lru_w_i": lru_w_i[0], "lru_b_i": lru_b_i, "lru_lam": lru_lambda,
        "s5_lr": s5_lambda_re[0], "s5_li": s5_lambda_im[0], "s5_ldt": s5_log_dt.reshape(S5_GROUPS, 1),
        "s5_bre": jnp.swapaxes(s5_b_re[0], 1, 2), "s5_bim": jnp.swapaxes(s5_b_im[0], 1, 2),
        "s5_cre": s5_c_re[0], "s5_cim": s5_c_im[0], "s5_d": s5_d_full,
        "final_g": final_norm_g.reshape(1, D_MODEL),
    }

    loss_part, dx, g = _local_step(x.reshape(nb * s, d), loss_target.reshape(nb * s, d), w, weights_of, send, nb, s)
    loss = lax.psum(loss_part[0, 0], ("x", "y", "c"))

    part = {
        "ffn_norm_g": jnp.stack([jnp.stack([g[f"ffn_g_{l}{h}"][0] for h in range(2)]) for l in range(2)]),
        "mix_norm_g": jnp.concatenate([g["mix_g_0"], g["mix_g_1"]], axis=0),
        "ret_norm_g": g["ret_g"], "conv_w": g["conv_w"][None], "conv_b": g["conv_b"],
        "lru_w_a": g["lru_w_a"][None], "lru_b_a": g["lru_b_a"], "lru_w_i": g["lru_w_i"][None], "lru_b_i": g["lru_b_i"],
        "lru_lambda": g["lru_lam"], "s5_lambda_re": g["s5_lr"][None], "s5_lambda_im": g["s5_li"][None],
        "s5_log_dt": g["s5_ldt"].reshape(1, S5_GROUPS),
        "s5_b_re": jnp.swapaxes(g["s5_bre"], 1, 2)[None], "s5_b_im": jnp.swapaxes(g["s5_bim"], 1, 2)[None],
        "s5_c_re": g["s5_cre"][None], "s5_c_im": g["s5_cim"][None], "s5_d": g["s5_d"], "final_norm_g": g["final_g"][0],
    }
    small = _tie("tie_small_grads", _pack([part[n] for n in _SMALL]), [rs[-1][2]["token"]])
    (gath,) = _all_gather("ag_small_grads", [small])
    full = dict(zip(_SMALL, _unpack(_sum8("sum_small_grads", gath), [part[n].shape for n in _SMALL])))
    for n in _SMALL_SHARDED:
        width = a[n].shape[-1]
        full[n] = lax.dynamic_slice_in_dim(full[n], dev * width, width, axis=full[n].ndim - 1)
    shapes = [a[n].shape for n in _SMALL]
    packed = _adamw("adamw_small", _pack([a[n] for n in _SMALL]), _pack([a["m_" + n] for n in _SMALL]),
                    _pack([a["v_" + n] for n in _SMALL]), _pack([full[n] for n in _SMALL]))
    res = {n: vals for n, vals in zip(_SMALL, zip(*[_unpack(p, shapes) for p in packed]))}

    landed = {}
    for group, names, h in rs:
        if not group.startswith("ffn_"):
            landed.update(zip(names, _xchg_wait("rs_wait_" + group, h, [dx])))
    own = {n: lax.dynamic_index_in_dim(g[n], dev, axis=0, keepdims=False) for n in ("ffn_w1", "ffn_w3", "ffn_w2")}
    own["w_in"] = lax.dynamic_slice_in_dim(g["w_in"], dev * IN_SHARD, IN_SHARD, axis=1)
    for n in ("w_out", "glu_a", "glu_b"):
        own[n] = lax.dynamic_slice_in_dim(g[n], dev * (D_MODEL // N_DEV), D_MODEL // N_DEV, axis=0)

    def update(n, short):
        res[n] = _adamw("adamw_" + n, a[n], a["m_" + n], a["v_" + n], own[short].reshape(a[n].shape),
                        landed[short].reshape((N_DEV - 1,) + a[n].shape))

    for n, short in zip(_BIG[3:], ("w_in", "w_out", "glu_a", "glu_b")):
        update(n, short)
    after = [dx, packed[0]] + [res[n][0] for n in _BIG[3:]]
    for group, names, h in rs:
        if group.startswith("ffn_"):
            ffn_lands[:] = _xchg_wait("rs_wait_" + group, h, after, ffn_lands)
    landed.update(zip(("ffn_w1", "ffn_w3", "ffn_w2"), ffn_lands))
    for n in _BIG[:3]:
        update(n, n)

    out = [loss, dx.reshape(nb, s, d)]
    for k in range(4):
        out += [res[n][k] for n in _WEIGHTS]
    return tuple(out)
```

```python
import functools
import math

import numpy as np
import jax
import jax.numpy as jnp
from jax import lax
from jax.experimental import pallas as pl
from jax.experimental.pallas import tpu as pltpu

f32 = jnp.float32
bf16 = jnp.bfloat16

D_MODEL = 1024
N_DEV = 8
EPS = 1e-6
RET_HEADS = 4
HEAD_DIM = 128
RET_WIDTH = 512
RET_CHUNK = 128
ROPE_BASE = 10000.0
LRU_WIDTH = 512
LRU_BLOCKS = 4
LRU_C = 8.0
S5_GROUP = 16
S5_GROUPS = 64
S5_STATE = 64
S5_CHUNK = 128
S5_BLOCKS = 8
S5_BLOCK_STATES = 512
S5_SUBLANES = 8
D_FF = 2816
FF_SHARD = D_FF // N_DEV
IN_SHARD = 3072 // N_DEV
ADAM_LR = 0.001
ADAM_B1 = 0.9
ADAM_B2 = 0.999
ADAM_EPS = 1e-08
ADAM_WD = 0.01
ADAM_STEP = 10

VMEM_LIMIT = 48 * 1024 * 1024
VMEM_SPEC = pl.BlockSpec(memory_space=pltpu.VMEM)
ANY_SPEC = pl.BlockSpec(memory_space=pl.ANY)
HBM_SPEC = pl.BlockSpec(memory_space=pltpu.HBM)
SEM_SPEC = pl.BlockSpec(memory_space=pltpu.SEMAPHORE)
SIDE_EFFECT = pltpu.SideEffectType.DATAFLOW_SIDE_EFFECTING
MESH = pl.DeviceIdType.MESH


def _cp(*sem):
    return pltpu.CompilerParams(dimension_semantics=sem, vmem_limit_bytes=VMEM_LIMIT)


def _nn(a, b):
    return jnp.dot(a, b, preferred_element_type=f32)


def _nt(a, b):
    return lax.dot_general(a, b, (((1,), (1,)), ((), ())), preferred_element_type=f32)


def _tn(a, b):
    return lax.dot_general(a, b, (((0,), (0,)), ((), ())), preferred_element_type=f32)


def _rms_fwd(x, g):
    r = lax.rsqrt(jnp.mean(x * x, axis=-1, keepdims=True) + EPS)
    xn = x * r
    return xn * g, xn, r


def _rms_bwd(dh, xn, r, g):
    dxn = dh * g
    dx = r * (dxn - xn * jnp.mean(dxn * xn, axis=-1, keepdims=True))
    dg = jnp.sum(dh * xn, axis=0, keepdims=True)
    return dx, dg


def _shift_dn(v, d, row, fill=0.0):
    return jnp.where(row >= d, pltpu.roll(v, d, 0), fill)


def _shift_up(v, d, row, fill=0.0):
    n = v.shape[0]
    return jnp.where(row < n - d, pltpu.roll(v, n - d, 0), fill)


def _ew(name, fn, ins, outs, tm=512):
    t = ins[0].shape[0]
    n_in = len(ins)

    def body(*refs):
        res = fn(*[r[...] for r in refs[:n_in]])
        for o, v in zip(refs[n_in:], res):
            o[...] = v.astype(o.dtype)

    return pl.pallas_call(
        body, name=name, grid=(t // tm,),
        in_specs=[pl.BlockSpec((tm, a.shape[1]), lambda i: (i, 0)) for a in ins],
        out_specs=[pl.BlockSpec((tm, n), lambda i: (i, 0)) for n, _ in outs],
        out_shape=[jax.ShapeDtypeStruct((t, n), dt) for n, dt in outs],
        compiler_params=_cp("parallel"),
    )(*ins)


def _mm(name, x, w, kind, extras=(), epilogue=None, outs=None, tm=512, tn=512):
    t = x.shape[0]
    n = w.shape[1] if kind == "nn" else w.shape[0]
    tn = min(tn, n)
    outs = outs or [f32]
    n_ex = len(extras)

    def body(x_ref, w_ref, *refs):
        xb = x_ref[...].astype(bf16)
        acc = _nn(xb, w_ref[...]) if kind == "nn" else _nt(xb, w_ref[...])
        res = epilogue(acc, *[r[...] for r in refs[:n_ex]]) if epilogue else (acc,)
        for o, v in zip(refs[n_ex:], res):
            o[...] = v.astype(o.dtype)

    w_spec = (pl.BlockSpec((w.shape[0], tn), lambda i, j: (0, j)) if kind == "nn"
              else pl.BlockSpec((tn, w.shape[1]), lambda i, j: (j, 0)))
    tile = pl.BlockSpec((tm, tn), lambda i, j: (i, j))
    return pl.pallas_call(
        body, name=name, grid=(t // tm, n // tn),
        in_specs=[pl.BlockSpec((tm, x.shape[1]), lambda i, j: (i, 0)), w_spec] + [tile] * n_ex,
        out_specs=[tile] * len(outs),
        out_shape=[jax.ShapeDtypeStruct((t, n), dt) for dt in outs],
        compiler_params=_cp("parallel", "parallel"),
    )(x, w, *extras)


def _mm_tn(name, x, y, tk=1024, tn=512, tt=512):
    t, k = x.shape
    n = y.shape[1]
    tk, tn = min(tk, k), min(tn, n)

    def body(x_ref, y_ref, o_ref, ob_ref):
        @pl.when(pl.program_id(2) == 0)
        def _():
            o_ref[...] = jnp.zeros_like(o_ref)
        o_ref[...] += _tn(x_ref[...].astype(bf16), y_ref[...].astype(bf16))

        @pl.when(pl.program_id(2) == pl.num_programs(2) - 1)
        def _():
            ob_ref[...] = o_ref[...].astype(bf16)

    out = pl.BlockSpec((tk, tn), lambda i, j, s: (i, j))
    return pl.pallas_call(
        body, name=name, grid=(k // tk, n // tn, t // tt),
        in_specs=[pl.BlockSpec((tt, tk), lambda i, j, s: (s, i)), pl.BlockSpec((tt, tn), lambda i, j, s: (s, j))],
        out_specs=[out, out],
        out_shape=[jax.ShapeDtypeStruct((k, n), f32), jax.ShapeDtypeStruct((k, n), bf16)],
        compiler_params=_cp("parallel", "parallel", "arbitrary"),
    )(x, y)


def _norm_fwd(name, x, g, tm=512):
    t, d = x.shape

    def body(x_ref, g_ref, h_ref, hb_ref):
        h, _, _ = _rms_fwd(x_ref[...], g_ref[...])
        h_ref[...] = h
        hb_ref[...] = h.astype(bf16)

    row = pl.BlockSpec((tm, d), lambda i: (i, 0))
    return pl.pallas_call(
        body, name=name, grid=(t // tm,),
        in_specs=[row, pl.BlockSpec((1, d), lambda i: (0, 0))],
        out_specs=[row, row],
        out_shape=[jax.ShapeDtypeStruct((t, d), f32), jax.ShapeDtypeStruct((t, d), bf16)],
        compiler_params=_cp("parallel"),
    )(x, g)


def _norm_bwd(name, dh, x, g, dres, tm=512):
    t, d = x.shape

    def body(dh_ref, x_ref, g_ref, dres_ref, dx_ref, dg_ref):
        gv = g_ref[...]
        _, xn, r = _rms_fwd(x_ref[...], gv)
        dx, dg = _rms_bwd(dh_ref[...], xn, r, gv)
        dx_ref[...] = dres_ref[...] + dx

        @pl.when(pl.program_id(0) == 0)
        def _():
            dg_ref[...] = jnp.zeros_like(dg_ref)
        dg_ref[...] += dg

    row = pl.BlockSpec((tm, d), lambda i: (i, 0))
    vec = pl.BlockSpec((1, d), lambda i: (0, 0))
    return pl.pallas_call(
        body, name=name, grid=(t // tm,),
        in_specs=[row, row, vec, row],
        out_specs=[row, vec],
        out_shape=[jax.ShapeDtypeStruct((t, d), f32), jax.ShapeDtypeStruct((1, d), f32)],
        compiler_params=_cp("arbitrary"),
    )(dh, x, g, dres)


def _final_loss(x, g, target, tm=512):
    t, d = x.shape

    def body(x_ref, g_ref, t_ref, loss_ref, dx_ref, dg_ref):
        gv = g_ref[...]
        y, xn, r = _rms_fwd(x_ref[...], gv)
        err = y - t_ref[...]
        dy = err * (1.0 / d)
        dx, dg = _rms_bwd(dy, xn, r, gv)
        dx_ref[...] = dx

        @pl.when(pl.program_id(0) == 0)
        def _():
            dg_ref[...] = jnp.zeros_like(dg_ref)
            loss_ref[...] = jnp.zeros_like(loss_ref)
        dg_ref[...] += dg
        loss_ref[...] += jnp.full((1, 128), 0.5 / d, f32) * jnp.sum(err * err)

    row = pl.BlockSpec((tm, d), lambda i: (i, 0))
    vec = pl.BlockSpec((1, d), lambda i: (0, 0))
    return pl.pallas_call(
        body, name="final_loss", grid=(t // tm,),
        in_specs=[row, vec, row],
        out_specs=[pl.BlockSpec((1, 128), lambda i: (0, 0)), row, vec],
        out_shape=[jax.ShapeDtypeStruct((1, 128), f32), jax.ShapeDtypeStruct((t, d), f32),
                   jax.ShapeDtypeStruct((1, d), f32)],
        compiler_params=_cp("arbitrary"),
    )(x, g, target)


def _load_ffn_weights(hbm_refs, vmem_refs, sems):
    @pl.when(pl.program_id(0) == 0)
    def _():
        copies = [pltpu.make_async_copy(src, dst, sems.at[k])
                  for k, (src, dst) in enumerate(zip(hbm_refs, vmem_refs))]
        for cp in copies:
            cp.start()
        for cp in copies:
            cp.wait()


def _ffn_weight_scratch(nj, d, ff):
    return [pltpu.VMEM((nj, d, ff), bf16), pltpu.VMEM((nj, d, ff), bf16), pltpu.VMEM((nj, ff, d), bf16),
            pltpu.SemaphoreType.DMA((3,))]


def _ffn_fwd(name, x, g, w1, w3, w2, tm=256):
    t, d = x.shape
    nj, ff = w1.shape[0], w1.shape[-1]

    def body(x_ref, g_ref, w1_hbm, w3_hbm, w2_hbm, y_ref, a_ref, b_ref, w1_ref, w3_ref, w2_ref, sems):
        _load_ffn_weights((w1_hbm, w3_hbm, w2_hbm), (w1_ref, w3_ref, w2_ref), sems)
        xv = x_ref[...]
        h, _, _ = _rms_fwd(xv, g_ref[...])
        hb = h.astype(bf16)
        acc = jnp.zeros((tm, d), f32)
        for j in range(nj):
            a = _nn(hb, w1_ref[j])
            b = _nn(hb, w3_ref[j])
            a_ref[j] = a.astype(bf16)
            b_ref[j] = b.astype(bf16)
            u = (a * jax.nn.sigmoid(a) * b).astype(bf16)
            acc = acc + _nn(u, w2_ref[j])
        y_ref[...] = xv + 0.5 * acc

    row = pl.BlockSpec((tm, d), lambda i: (i, 0))
    mid = pl.BlockSpec((nj, tm, ff), lambda i: (0, i, 0))
    return pl.pallas_call(
        body, name=name, grid=(t // tm,),
        in_specs=[row, pl.BlockSpec((1, d), lambda i: (0, 0)), ANY_SPEC, ANY_SPEC, ANY_SPEC],
        out_specs=[row, mid, mid],
        out_shape=[jax.ShapeDtypeStruct((t, d), f32), jax.ShapeDtypeStruct((nj, t, ff), bf16),
                   jax.ShapeDtypeStruct((nj, t, ff), bf16)],
        scratch_shapes=_ffn_weight_scratch(nj, d, ff),
        compiler_params=_cp("arbitrary"),
    )(x, g, w1, w3, w2)


def _ffn_dx(name, dy, x, g, w1, w3, w2, a, b, tm=256):
    t, d = x.shape
    nj, ff = w1.shape[0], w1.shape[-1]

    def body(dy_ref, x_ref, g_ref, w1_hbm, w3_hbm, w2_hbm, a_ref, b_ref,
             dx_ref, dg_ref, hb_ref, dyh_ref, u_ref, da_ref, db_ref, w1_ref, w3_ref, w2_ref, sems):
        _load_ffn_weights((w1_hbm, w3_hbm, w2_hbm), (w1_ref, w3_ref, w2_ref), sems)
        gv = g_ref[...]
        h, xn, r = _rms_fwd(x_ref[...], gv)
        hb_ref[...] = h.astype(bf16)
        dyv = dy_ref[...]
        dyh = (0.5 * dyv).astype(bf16)
        dyh_ref[...] = dyh
        dh = jnp.zeros((tm, d), f32)
        for j in range(nj):
            av = a_ref[j].astype(f32)
            bv = b_ref[j].astype(f32)
            s = jax.nn.sigmoid(av)
            silu = av * s
            u_ref[j] = (silu * bv).astype(bf16)
            du = _nt(dyh, w2_ref[j])
            dab = (du * bv * (s * (1.0 + av * (1.0 - s)))).astype(bf16)
            dbb = (du * silu).astype(bf16)
            da_ref[j] = dab
            db_ref[j] = dbb
            dh = dh + _nt(dab, w1_ref[j]) + _nt(dbb, w3_ref[j])
        dx, dg = _rms_bwd(dh, xn, r, gv)
        dx_ref[...] = dyv + dx

        @pl.when(pl.program_id(0) == 0)
        def _():
            dg_ref[...] = jnp.zeros_like(dg_ref)
        dg_ref[...] += dg

    row = pl.BlockSpec((tm, d), lambda i: (i, 0))
    vec = pl.BlockSpec((1, d), lambda i: (0, 0))
    mid = pl.BlockSpec((nj, tm, ff), lambda i: (0, i, 0))
    mid_shape = jax.ShapeDtypeStruct((nj, t, ff), bf16)
    return pl.pallas_call(
        body, name=name, grid=(t // tm,),
        in_specs=[row, row, vec, ANY_SPEC, ANY_SPEC, ANY_SPEC, mid, mid],
        out_specs=[row, vec, row, row, mid, mid, mid],
        out_shape=[jax.ShapeDtypeStruct((t, d), f32), jax.ShapeDtypeStruct((1, d), f32),
                   jax.ShapeDtypeStruct((t, d), bf16), jax.ShapeDtypeStruct((t, d), bf16),
                   mid_shape, mid_shape, mid_shape],
        scratch_shapes=_ffn_weight_scratch(nj, d, ff),
        compiler_params=_cp("arbitrary"),
    )(dy, x, g, w1, w3, w2, a, b)


def _ffn_dw_one(name, x, y, buf, l, h, tt=512):
    nj = buf.shape[0]
    k, n = x.shape[-1], y.shape[-1]
    t = x.shape[-2]

    def body(x_ref, y_ref, buf_ref, o_ref, ob_ref):
        @pl.when(pl.program_id(1) == 0)
        def _():
            o_ref[...] = jnp.zeros_like(o_ref)
        xv = x_ref[0] if x.ndim == 3 else x_ref[...]
        yv = y_ref[0] if y.ndim == 3 else y_ref[...]
        o_ref[0] += _tn(xv, yv)

        @pl.when(pl.program_id(1) == pl.num_programs(1) - 1)
        def _():
            ob_ref[...] = o_ref[...].astype(bf16)

    def spec(a):
        if a.ndim == 3:
            return pl.BlockSpec((1, tt, a.shape[-1]), lambda j, s: (j, s, 0))
        return pl.BlockSpec((tt, a.shape[-1]), lambda j, s: (s, 0))

    return pl.pallas_call(
        body, name=name, grid=(nj, t // tt),
        in_specs=[spec(x), spec(y), ANY_SPEC],
        out_specs=[pl.BlockSpec((1, None, None, k, n), lambda j, s: (j, l, h, 0, 0)),
                   pl.BlockSpec((1, k, n), lambda j, s: (j, 0, 0))],
        out_shape=[jax.ShapeDtypeStruct(buf.shape, buf.dtype), jax.ShapeDtypeStruct((nj, k, n), bf16)],
        input_output_aliases={2: 0},
        compiler_params=_cp("parallel", "arbitrary"),
    )(x, y, buf)


def _ffn_dw(name, hb, dyh, u, da, db, bufs, l, h, tt=512):
    t, d = hb.shape
    nj, _, ff = u.shape

    def body(hb_ref, dyh_ref, u_ref, da_ref, db_ref, *refs):
        dw1_ref, dw3_ref, dw2_ref, dw1b_ref, dw3b_ref, dw2b_ref = refs[3:]

        @pl.when(pl.program_id(1) == 0)
        def _():
            dw1_ref[...] = jnp.zeros_like(dw1_ref)
            dw3_ref[...] = jnp.zeros_like(dw3_ref)
            dw2_ref[...] = jnp.zeros_like(dw2_ref)
        hv = hb_ref[...]
        dw1_ref[0] += _tn(hv, da_ref[0])
        dw3_ref[0] += _tn(hv, db_ref[0])
        dw2_ref[0] += _tn(u_ref[0], dyh_ref[...])

        @pl.when(pl.program_id(1) == pl.num_programs(1) - 1)
        def _():
            dw1b_ref[...] = dw1_ref[...].astype(bf16)
            dw3b_ref[...] = dw3_ref[...].astype(bf16)
            dw2b_ref[...] = dw2_ref[...].astype(bf16)

    row = pl.BlockSpec((tt, d), lambda j, s: (s, 0))
    mid = pl.BlockSpec((1, tt, ff), lambda j, s: (j, s, 0))
    slab_in = pl.BlockSpec((1, None, None, d, ff), lambda j, s: (j, l, h, 0, 0))
    slab_out = pl.BlockSpec((1, None, None, ff, d), lambda j, s: (j, l, h, 0, 0))
    blk_in = pl.BlockSpec((1, d, ff), lambda j, s: (j, 0, 0))
    blk_out = pl.BlockSpec((1, ff, d), lambda j, s: (j, 0, 0))
    outs = pl.pallas_call(
        body, name=name, grid=(nj, t // tt),
        in_specs=[row, row, mid, mid, mid] + [ANY_SPEC] * 3,
        out_specs=[slab_in, slab_in, slab_out, blk_in, blk_in, blk_out],
        out_shape=[jax.ShapeDtypeStruct(b.shape, b.dtype) for b in bufs]
        + [jax.ShapeDtypeStruct((nj, d, ff), bf16), jax.ShapeDtypeStruct((nj, d, ff), bf16),
           jax.ShapeDtypeStruct((nj, ff, d), bf16)],
        input_output_aliases={5 + k: k for k in range(3)},
        compiler_params=_cp("parallel", "arbitrary"),
    )(hb, dyh, u, da, db, *bufs)
    return outs[:3], outs[3:]


_LOG_GAMMA = [float(np.log1p(-np.float32(2.0) ** np.float32(-5.0 - h))) for h in range(RET_HEADS)]


def _ret_consts(h):
    lg = jnp.where(h == 0, _LOG_GAMMA[0], jnp.where(h == 1, _LOG_GAMMA[1],
                   jnp.where(h == 2, _LOG_GAMMA[2], _LOG_GAMMA[3]))).astype(f32)
    c = RET_CHUNK
    r = lax.broadcasted_iota(jnp.int32, (c, c), 0)
    cc = lax.broadcasted_iota(jnp.int32, (c, c), 1)
    decay = jnp.where(r >= cc, jnp.exp(lg * jnp.maximum((r - cc).astype(f32), 0.0)), 0.0)
    pos = lax.broadcasted_iota(jnp.int32, (c, 1), 0).astype(f32)
    kd = jnp.exp(lg * (c - 1.0 - pos))
    qd = jnp.exp(lg * (pos + 1.0))
    gc = jnp.exp(lg * c)
    return decay, kd, qd, gc


def _rope(x, cos, sin):
    return x * cos + pltpu.roll(x, HEAD_DIM // 2, 1) * sin


def _rope_t(g, cos, sin):
    return g * cos + pltpu.roll(g * sin, HEAD_DIM // 2, 1)


def _rope_tables(s):
    half = HEAD_DIM // 2
    inv = ROPE_BASE ** (-jnp.arange(half, dtype=f32) / half)
    ang = jnp.arange(s, dtype=f32)[:, None] * inv[None, :]
    cos, sin = jnp.cos(ang), jnp.sin(ang)
    return jnp.concatenate([cos, cos], axis=1), jnp.concatenate([-sin, sin], axis=1)


def _head_ln(o):
    mu = jnp.mean(o, axis=-1, keepdims=True)
    oc = o - mu
    rs = lax.rsqrt(jnp.mean(oc * oc, axis=-1, keepdims=True) + EPS)
    return oc * rs, rs


def _ret_fwd(proj, cos, sin, ret_g, nb, s):
    c = RET_CHUNK
    nc = s // c
    t = nb * s
    scale = HEAD_DIM ** -0.5

    def body(q_ref, k_ref, v_ref, gate_ref, cos_ref, sin_ref, g_ref, o_ref, rprev_ref, m_ref, r_acc):
        n = pl.program_id(2)

        @pl.when(n == 0)
        def _():
            r_acc[...] = jnp.zeros_like(r_acc)
        decay, kd, qd, gc = _ret_consts(pl.program_id(1))
        cs, sn = cos_ref[...], sin_ref[...]
        q = _rope(q_ref[...], cs, sn)
        k = _rope(k_ref[...], cs, sn) * scale
        vb = v_ref[...].astype(bf16)
        sc = _nt(q.astype(bf16), k.astype(bf16)) * decay
        rv = r_acc[...]
        rprev_ref[0] = rv
        o = _nn(sc.astype(bf16), vb) + _nn((q * qd).astype(bf16), rv.astype(bf16))
        o_ref[...] = o
        r_acc[...] = rv * gc + _tn((k * kd).astype(bf16), vb)
        y, _ = _head_ln(o)
        gate = gate_ref[...]
        m_ref[...] = y * g_ref[...] * (gate * jax.nn.sigmoid(gate))

    def col(off):
        return pl.BlockSpec((c, HEAD_DIM), lambda b, h, n: (b * nc + n, off + h))

    tab = pl.BlockSpec((c, HEAD_DIM), lambda b, h, n: (n, 0))
    return pl.pallas_call(
        body, name="ret_fwd", grid=(nb, RET_HEADS, nc),
        in_specs=[col(0), col(4), col(8), col(12), tab, tab, pl.BlockSpec((1, HEAD_DIM), lambda b, h, n: (0, h))],
        out_specs=[col(0), pl.BlockSpec((1, HEAD_DIM, HEAD_DIM), lambda b, h, n: ((b * RET_HEADS + h) * nc + n, 0, 0)),
                   col(0)],
        out_shape=[jax.ShapeDtypeStruct((t, RET_WIDTH), f32),
                   jax.ShapeDtypeStruct((nb * RET_HEADS * nc, HEAD_DIM, HEAD_DIM), f32),
                   jax.ShapeDtypeStruct((t, RET_WIDTH), f32)],
        scratch_shapes=[pltpu.VMEM((HEAD_DIM, HEAD_DIM), f32)],
        compiler_params=_cp("parallel", "parallel", "arbitrary"),
    )(proj, proj, proj, proj, cos, sin, ret_g)


def _ret_bwd(dmerged, o_raw, rprev, proj, cos, sin, ret_g, nb, s):
    c = RET_CHUNK
    nc = s // c
    t = nb * s
    scale = HEAD_DIM ** -0.5

    def body(dm_ref, o_ref, rprev_ref, q_ref, k_ref, v_ref, gate_ref, cos_ref, sin_ref, g_ref,
             dq_ref, dk_ref, dv_ref, dgate_ref, dg_ref, dr_acc):
        b, n = pl.program_id(1), pl.program_id(2)

        @pl.when(n == 0)
        def _():
            dr_acc[...] = jnp.zeros_like(dr_acc)

        @pl.when((n == 0) & (b == 0))
        def _():
            dg_ref[...] = jnp.zeros_like(dg_ref)
        decay, kd, qd, gc = _ret_consts(pl.program_id(0))
        cs, sn = cos_ref[...], sin_ref[...]
        q = _rope(q_ref[...], cs, sn)
        k = _rope(k_ref[...], cs, sn) * scale
        qb, kb = q.astype(bf16), k.astype(bf16)
        vb = v_ref[...].astype(bf16)
        sc = _nt(qb, kb) * decay
        gv = g_ref[...]
        y, rs = _head_ln(o_ref[...])
        gate = gate_ref[...]
        sg = jax.nn.sigmoid(gate)
        silu = gate * sg
        dm = dm_ref[...]
        dgate_ref[...] = dm * y * gv * (sg * (1.0 + gate * (1.0 - sg)))
        dyl = dm * gv * silu
        dg_ref[...] += jnp.sum(dm * y * silu, axis=0, keepdims=True)
        do = rs * (dyl - jnp.mean(dyl, axis=-1, keepdims=True) - y * jnp.mean(dyl * y, axis=-1, keepdims=True))
        dob = do.astype(bf16)
        rv = rprev_ref[0]
        drn = dr_acc[...]
        drb = drn.astype(bf16)
        ds = (_nt(dob, vb) * decay).astype(bf16)
        kdb = (k * kd).astype(bf16)
        qdb = (q * qd).astype(bf16)
        dq_r = _nn(ds, kb) + _nt(dob, rv.astype(bf16)) * qd
        dk_r = _tn(ds, qb) + _nt(vb, drb) * kd
        dv_ref[...] = _tn(sc.astype(bf16), dob) + _nn(kdb, drb)
        dr_acc[...] = drn * gc + _tn(qdb, dob)
        dq_ref[...] = _rope_t(dq_r, cs, sn)
        dk_ref[...] = _rope_t(dk_r * scale, cs, sn)

    def col(off):
        return pl.BlockSpec((c, HEAD_DIM), lambda h, b, n: (b * nc + nc - 1 - n, off + h))

    tab = pl.BlockSpec((c, HEAD_DIM), lambda h, b, n: (nc - 1 - n, 0))
    gsp = pl.BlockSpec((1, HEAD_DIM), lambda h, b, n: (0, h))
    out_t = jax.ShapeDtypeStruct((t, RET_WIDTH), f32)
    return pl.pallas_call(
        body, name="ret_bwd", grid=(RET_HEADS, nb, nc),
        in_specs=[col(0), col(0),
                  pl.BlockSpec((1, HEAD_DIM, HEAD_DIM), lambda h, b, n: ((b * RET_HEADS + h) * nc + nc - 1 - n, 0, 0)),
                  col(0), col(4), col(8), col(12), tab, tab, gsp],
        out_specs=[col(0), col(0), col(0), col(0), gsp],
        out_shape=[out_t, out_t, out_t, out_t, jax.ShapeDtypeStruct((1, RET_WIDTH), f32)],
        scratch_shapes=[pltpu.VMEM((HEAD_DIM, HEAD_DIM), f32)],
        compiler_params=_cp("parallel", "arbitrary", "arbitrary"),
    )(dmerged, o_raw, rprev, proj, proj, proj, proj, cos, sin, ret_g)


def _neg_expm1(z):
    series = -(z * (1.0 + z * (0.5 + z * (1.0 / 6.0 + z * (1.0 / 24.0)))))
    return jnp.where(z > -0.01, series, 1.0 - jnp.exp(z))


def _lru_gates(xc, pa, pi, lam):
    r = jax.nn.sigmoid(pa)
    i = jax.nn.sigmoid(pi)
    log_a = -LRU_C * r * jax.nn.softplus(-lam)
    a = jnp.exp(log_a)
    bx = jnp.sqrt(_neg_expm1(2.0 * log_a)) * i * xc
    return a, bx


def _scan_fwd(a, b, row):
    d = 1
    while d < a.shape[0]:
        b = a * _shift_dn(b, d, row) + b
        a = a * _shift_dn(a, d, row, 1.0)
        d *= 2
    return b


def _scan_bwd(c, b, row):
    d = 1
    while d < c.shape[0]:
        b = c * _shift_up(b, d, row) + b
        c = c * _shift_up(c, d, row, 1.0)
        d *= 2
    return b


def _conv_fwd(x, cw, cb, row):
    return (cb + cw[3:4] * x + cw[2:3] * _shift_dn(x, 1, row) + cw[1:2] * _shift_dn(x, 2, row)
            + cw[0:1] * _shift_dn(x, 3, row))


def _lru_specs(s, order):
    def im(f):
        return (lambda b, g: f(b, g)) if order == "bg" else (lambda g, b: f(b, g))
    seq = lambda off: pl.BlockSpec((s, 128), im(lambda b, g: (b, off + g)))
    vec = pl.BlockSpec((1, 128), im(lambda b, g: (0, g)))
    cw = pl.BlockSpec((4, 128), im(lambda b, g: (0, g)))
    mat = pl.BlockSpec((1, 128, 128), im(lambda b, g: (g, 0, 0)))
    return seq, vec, cw, mat


def _lru_fwd(proj, conv_w, conv_b, w_a, b_a, w_i, b_i, lam, nb, s):
    def body(x_ref, gt_ref, cw_ref, cb_ref, wa_ref, ba_ref, wi_ref, bi_ref, lam_ref, out_ref):
        row = lax.broadcasted_iota(jnp.int32, (s, 128), 0)
        xc = _conv_fwd(x_ref[...], cw_ref[...], cb_ref[...], row)
        xcb = xc.astype(bf16)
        pa = _nn(xcb, wa_ref[0].astype(bf16)) + ba_ref[...]
        pi = _nn(xcb, wi_ref[0].astype(bf16)) + bi_ref[...]
        a, bx = _lru_gates(xc, pa, pi, lam_ref[...])
        h = _scan_fwd(a, bx, row)
        out_ref[...] = h * jax.nn.gelu(gt_ref[...])

    seq, vec, cw, mat = _lru_specs(s, "bg")
    return pl.pallas_call(
        body, name="lru_fwd", grid=(nb, LRU_BLOCKS),
        in_specs=[seq(16), seq(20), cw, vec, mat, vec, mat, vec, vec],
        out_specs=seq(0),
        out_shape=jax.ShapeDtypeStruct((nb * s, LRU_WIDTH), f32),
        compiler_params=_cp("parallel", "parallel"),
    )(proj, proj, conv_w, conv_b, w_a, b_a, w_i, b_i, lam)


def _lru_bwd(dmerged, proj, conv_w, conv_b, w_a, b_a, w_i, b_i, lam, nb, s):
    def body(dout_ref, x_ref, gt_ref, cw_ref, cb_ref, wa_ref, ba_ref, wi_ref, bi_ref, lam_ref,
             dx_ref, dgt_ref, dcw_ref, dcb_ref, dwa_ref, dba_ref, dwi_ref, dbi_ref, dlam_ref):
        row = lax.broadcasted_iota(jnp.int32, (s, 128), 0)
        x = x_ref[...]
        cwv = cw_ref[...]
        xc = _conv_fwd(x, cwv, cb_ref[...], row)
        xcb = xc.astype(bf16)
        wab, wib = wa_ref[0].astype(bf16), wi_ref[0].astype(bf16)
        pa = _nn(xcb, wab) + ba_ref[...]
        pi = _nn(xcb, wib) + bi_ref[...]
        (a, bx), gates_vjp = jax.vjp(_lru_gates, xc, pa, pi, lam_ref[...])
        h = _scan_fwd(a, bx, row)
        ge, gelu_vjp = jax.vjp(jax.nn.gelu, gt_ref[...])
        dout = dout_ref[...]
        dgt_ref[...] = gelu_vjp(dout * h)[0]
        adj = _scan_bwd(_shift_up(a, 1, row), dout * ge, row)
        dxc, dpa, dpi, dlam = gates_vjp((adj * _shift_dn(h, 1, row), adj))
        dpab, dpib = dpa.astype(bf16), dpi.astype(bf16)
        dxc = dxc + _nt(dpab, wab) + _nt(dpib, wib)
        dx_ref[...] = (cwv[3:4] * dxc + cwv[2:3] * _shift_up(dxc, 1, row) + cwv[1:2] * _shift_up(dxc, 2, row)
                       + cwv[0:1] * _shift_up(dxc, 3, row))

        @pl.when(pl.program_id(1) == 0)
        def _():
            for r in (dcw_ref, dcb_ref, dwa_ref, dba_ref, dwi_ref, dbi_ref, dlam_ref):
                r[...] = jnp.zeros_like(r)
        rsum = lambda v: jnp.sum(v, axis=0, keepdims=True)
        dcw_ref[...] += jnp.concatenate([rsum(dxc * _shift_dn(x, 3, row)), rsum(dxc * _shift_dn(x, 2, row)),
                                         rsum(dxc * _shift_dn(x, 1, row)), rsum(dxc * x)], axis=0)
        dcb_ref[...] += rsum(dxc)
        dwa_ref[0] += _tn(xcb, dpab)
        dwi_ref[0] += _tn(xcb, dpib)
        dba_ref[...] += rsum(dpa)
        dbi_ref[...] += rsum(dpi)
        dlam_ref[...] += dlam

    seq, vec, cw, mat = _lru_specs(s, "gb")
    t = nb * s
    vshape = jax.ShapeDtypeStruct((1, LRU_WIDTH), f32)
    mshape = jax.ShapeDtypeStruct((LRU_BLOCKS, 128, 128), f32)
    return pl.pallas_call(
        body, name="lru_bwd", grid=(LRU_BLOCKS, nb),
        in_specs=[seq(4), seq(16), seq(20), cw, vec, mat, vec, mat, vec, vec],
        out_specs=[seq(0), seq(0), cw, vec, mat, vec, mat, vec, vec],
        out_shape=[jax.ShapeDtypeStruct((t, LRU_WIDTH), f32), jax.ShapeDtypeStruct((t, LRU_WIDTH), f32),
                   jax.ShapeDtypeStruct((4, LRU_WIDTH), f32), vshape, mshape, vshape, mshape, vshape, vshape],
        compiler_params=_cp("parallel", "arbitrary"),
    )(dmerged, proj, proj, conv_w, conv_b, w_a, b_a, w_i, b_i, lam)


def _s5_disc(lr, li, ldt, bre, bim):
    dt = jnp.exp(ldt)
    mag = jnp.exp(lr * dt)
    lbr = mag * jnp.cos(li * dt)
    lbi = mag * jnp.sin(li * dt)
    den = lr * lr + li * li
    nr = lbr - 1.0
    fr = (nr * lr + lbi * li) / den
    fi = (lbi * lr - nr * li) / den
    bbr = fr[:, None, :] * bre - fi[:, None, :] * bim
    bbi = fr[:, None, :] * bim + fi[:, None, :] * bre
    return lbr, lbi, bbr, bbi


def _s5_prep(lr, li, ldt, bre, bim):
    def body(lr_ref, li_ref, ldt_ref, bre_ref, bim_ref, o1, o2, o3, o4):
        o1[...], o2[...], o3[...], o4[...] = _s5_disc(lr_ref[...], li_ref[...], ldt_ref[...], bre_ref[...], bim_ref[...])

    return pl.pallas_call(
        body, name="s5_prep", in_specs=[VMEM_SPEC] * 5, out_specs=[VMEM_SPEC] * 4,
        out_shape=[jax.ShapeDtypeStruct(lr.shape, f32), jax.ShapeDtypeStruct(lr.shape, f32),
                   jax.ShapeDtypeStruct(bre.shape, f32), jax.ShapeDtypeStruct(bre.shape, f32)],
    )(lr, li, ldt, bre, bim)


def _s5_prep_bwd(lr, li, ldt, bre, bim, cts):
    def body(lr_ref, li_ref, ldt_ref, bre_ref, bim_ref, g1, g2, g3, g4, o1, o2, o3, o4, o5):
        _, vjp = jax.vjp(_s5_disc, lr_ref[...], li_ref[...], ldt_ref[...], bre_ref[...], bim_ref[...])
        o1[...], o2[...], o3[...], o4[...], o5[...] = vjp((g1[...], g2[...], g3[...], g4[...]))

    return pl.pallas_call(
        body, name="s5_prep_bwd", in_specs=[VMEM_SPEC] * 9, out_specs=[VMEM_SPEC] * 5,
        out_shape=[jax.ShapeDtypeStruct(v.shape, f32) for v in (lr, li, ldt, bre, bim)],
    )(lr, li, ldt, bre, bim, *cts)


def _cmul(ar, ai, br, bi):
    return ar * br - ai * bi, ar * bi + ai * br


def _s5_pow_table(lr, li, n, row, up):
    ar = jnp.broadcast_to(lr, (n, lr.shape[1]))
    ai = jnp.broadcast_to(li, (n, li.shape[1]))
    shift = _shift_up if up else _shift_dn
    d = 1
    while d < n:
        ar, ai = _cmul(ar, ai, shift(ar, d, row, 1.0), shift(ai, d, row, 0.0))
        d *= 2
    return ar, ai


def _s5_scan(br, bi, lr, li, tab_r, tab_i, cr, ci, row, up):
    n = br.shape[0]
    sub = row & (S5_SUBLANES - 1)
    pr, pi = lr, li
    d = 1
    while d < S5_SUBLANES:
        keep = (sub < S5_SUBLANES - d) if up else (sub >= d)
        shift = n - d if up else d
        tr, ti = _cmul(pr, pi, pltpu.roll(br, shift, 0), pltpu.roll(bi, shift, 0))
        br, bi = br + jnp.where(keep, tr, 0.0), bi + jnp.where(keep, ti, 0.0)
        pr, pi = _cmul(pr, pi, pr, pi)
        d *= 2
    groups = list(range(n // S5_SUBLANES))
    out_r, out_i = [None] * len(groups), [None] * len(groups)
    edge = slice(0, 1) if up else slice(S5_SUBLANES - 1, S5_SUBLANES)
    for g in (reversed(groups) if up else groups):
        rows = slice(g * S5_SUBLANES, (g + 1) * S5_SUBLANES)
        tr, ti = _cmul(tab_r, tab_i, cr, ci)
        hr, hi = br[rows] + tr, bi[rows] + ti
        out_r[g], out_i[g] = hr, hi
        cr, ci = hr[edge], hi[edge]
    return jnp.concatenate(out_r, axis=0), jnp.concatenate(out_i, axis=0)


def _s5_specs(s, nc, order):
    def im(f):
        return (lambda b, k: f(b, k)) if order == "bk" else (lambda k, b: f(b, k))
    seq = pl.BlockSpec((s, 128), im(lambda b, k: (b, k)))
    lvec = pl.BlockSpec((1, S5_BLOCK_STATES), im(lambda b, k: (0, k)))
    dvec = pl.BlockSpec((1, 128), im(lambda b, k: (0, k)))
    wmat = pl.BlockSpec((1, 128, S5_BLOCK_STATES), im(lambda b, k: (k, 0, 0)))
    h0 = pl.BlockSpec((1, nc, 2, S5_BLOCK_STATES), im(lambda b, k: (b * S5_BLOCKS + k, 0, 0, 0)))
    return seq, lvec, dvec, wmat, h0


def _s5_fwd(u, lbr, lbi, wbr, wbi, wcr, wci, dskip, nb, s):
    ln = S5_CHUNK
    nc = s // ln

    def body(u_ref, lr_ref, li_ref, wbr_ref, wbi_ref, wcr_ref, wci_ref, d_ref, yg_ref, y_ref, h0_ref):
        row = lax.broadcasted_iota(jnp.int32, (ln, S5_BLOCK_STATES), 0)
        lr, li = lr_ref[...], li_ref[...]
        pr, pi = _s5_pow_table(lr, li, S5_SUBLANES, row[:S5_SUBLANES], False)
        dv = d_ref[...]

        def step(n, carry):
            h0r, h0i = carry
            st = pl.multiple_of(n * ln, ln)
            uc = u_ref[pl.ds(st, ln), :]
            ub = uc.astype(bf16)
            hr, hi = _s5_scan(_nn(ub, wbr_ref[0]), _nn(ub, wbi_ref[0]), lr, li, pr, pi, h0r, h0i, row, False)
            h0_ref[0, n, 0:1, :] = h0r
            h0_ref[0, n, 1:2, :] = h0i
            y = _nt(hr.astype(bf16), wcr_ref[0]) - _nt(hi.astype(bf16), wci_ref[0]) + dv * uc
            y_ref[pl.ds(st, ln), :] = y
            yg_ref[pl.ds(st, ln), :] = jax.nn.gelu(y).astype(bf16)
            return hr[ln - 1:ln, :], hi[ln - 1:ln, :]

        z = jnp.zeros((1, S5_BLOCK_STATES), f32)
        lax.fori_loop(0, nc, step, (z, z))

    seq, lvec, dvec, wmat, h0 = _s5_specs(s, nc, "bk")
    t = nb * s
    return pl.pallas_call(
        body, name="s5_fwd", grid=(nb, S5_BLOCKS),
        in_specs=[seq, lvec, lvec, wmat, wmat, wmat, wmat, dvec],
        out_specs=[seq, seq, h0],
        out_shape=[jax.ShapeDtypeStruct((t, D_MODEL), bf16), jax.ShapeDtypeStruct((t, D_MODEL), f32),
                   jax.ShapeDtypeStruct((nb * S5_BLOCKS, nc, 2, S5_BLOCK_STATES), f32)],
        compiler_params=_cp("parallel", "parallel"),
    )(u, lbr, lbi, wbr, wbi, wcr, wci, dskip)


def _s5_bwd(dyg, y, u, h0, lbr, lbi, wbr, wbi, wcr, wci, dskip, nb, s):
    ln = S5_CHUNK
    nc = s // ln

    def body(dyg_ref, y_ref, u_ref, h0_ref, lr_ref, li_ref, wbr_ref, wbi_ref, wcr_ref, wci_ref, d_ref,
             du_ref, dlr_ref, dli_ref, dwbr_ref, dwbi_ref, dwcr_ref, dwci_ref, dd_ref):
        @pl.when(pl.program_id(1) == 0)
        def _():
            for r in (dlr_ref, dli_ref, dwbr_ref, dwbi_ref, dwcr_ref, dwci_ref, dd_ref):
                r[...] = jnp.zeros_like(r)
        row = lax.broadcasted_iota(jnp.int32, (ln, S5_BLOCK_STATES), 0)
        lr, li = lr_ref[...], li_ref[...]
        pr, pi = _s5_pow_table(lr, li, S5_SUBLANES, row[:S5_SUBLANES], False)
        qr, qi = _s5_pow_table(lr, -li, S5_SUBLANES, row[:S5_SUBLANES], True)
        dv = d_ref[...]
        rsum = lambda v: jnp.sum(v, axis=0, keepdims=True)

        def step(i, carry):
            gnr, gni = carry
            n = nc - 1 - i
            st = pl.multiple_of(n * ln, ln)
            uc = u_ref[pl.ds(st, ln), :]
            ub = uc.astype(bf16)
            h0v = h0_ref[0, n]
            h0r, h0i = h0v[0:1], h0v[1:2]
            hr, hi = _s5_scan(_nn(ub, wbr_ref[0]), _nn(ub, wbi_ref[0]), lr, li, pr, pi, h0r, h0i, row, False)
            dy = jax.vjp(jax.nn.gelu, y_ref[pl.ds(st, ln), :])[1](dyg_ref[pl.ds(st, ln), :])[0]
            dyb = dy.astype(bf16)
            dd_ref[...] += rsum(dy * uc)
            gr, gi = _s5_scan(_nn(dyb, wcr_ref[0]), -_nn(dyb, wci_ref[0]), lr, -li, qr, qi, gnr, gni, row, True)
            hpr = jnp.where(row >= 1, pltpu.roll(hr, 1, 0), h0r)
            hpi = jnp.where(row >= 1, pltpu.roll(hi, 1, 0), h0i)
            dlr_ref[...] += rsum(gr * hpr + gi * hpi)
            dli_ref[...] += rsum(gi * hpr - gr * hpi)
            grb, gib = gr.astype(bf16), gi.astype(bf16)
            dwbr_ref[0] += _tn(ub, grb)
            dwbi_ref[0] += _tn(ub, gib)
            dwcr_ref[0] += _tn(dyb, hr.astype(bf16))
            dwci_ref[0] -= _tn(dyb, hi.astype(bf16))
            du_ref[pl.ds(st, ln), :] = _nt(grb, wbr_ref[0]) + _nt(gib, wbi_ref[0]) + dv * dy
            return gr[0:1, :], gi[0:1, :]

        z = jnp.zeros((1, S5_BLOCK_STATES), f32)
        lax.fori_loop(0, nc, step, (z, z))

    seq, lvec, dvec, wmat, h0s = _s5_specs(s, nc, "kb")
    t = nb * s
    lshape = jax.ShapeDtypeStruct((1, S5_BLOCKS * S5_BLOCK_STATES), f32)
    wshape = jax.ShapeDtypeStruct((S5_BLOCKS, 128, S5_BLOCK_STATES), f32)
    return pl.pallas_call(
        body, name="s5_bwd", grid=(S5_BLOCKS, nb),
        in_specs=[seq, seq, seq, h0s, lvec, lvec, wmat, wmat, wmat, wmat, dvec],
        out_specs=[seq, lvec, lvec, wmat, wmat, wmat, wmat, dvec],
        out_shape=[jax.ShapeDtypeStruct((t, D_MODEL), f32), lshape, lshape, wshape, wshape, wshape, wshape,
                   jax.ShapeDtypeStruct((1, D_MODEL), f32)],
        compiler_params=_cp("parallel", "arbitrary"),
    )(dyg, y, u, h0, lbr, lbi, wbr, wbi, wcr, wci, dskip)


def _blockdiag(w):
    w4 = w.reshape(S5_BLOCKS, 8, S5_GROUP, S5_STATE)
    same_group = jnp.eye(8, dtype=bool)[None, :, None, :, None]
    return jnp.where(same_group, w4[:, :, :, None, :], 0.0).reshape(S5_BLOCKS, 128, S5_BLOCK_STATES)


def _blockdiag_t(dw):
    d5 = dw.reshape(S5_BLOCKS, 8, S5_GROUP, 8, S5_STATE)
    diag = jnp.diagonal(d5, axis1=1, axis2=3)
    return jnp.moveaxis(diag, 3, 1).reshape(S5_GROUPS, S5_GROUP, S5_STATE)


def _glu_fwd(ygb, wa, wb, x, tm=512, tn=512):
    t, d = x.shape

    def body(y_ref, wa_ref, wb_ref, x_ref, o_ref, p_ref, q_ref):
        p = _nn(y_ref[...], wa_ref[...])
        q = _nn(y_ref[...], wb_ref[...])
        p_ref[...] = p
        q_ref[...] = q
        o_ref[...] = x_ref[...] + p * jax.nn.sigmoid(q)

    tile = pl.BlockSpec((tm, tn), lambda i, j: (i, j))
    wsp = pl.BlockSpec((d, tn), lambda i, j: (0, j))
    out = jax.ShapeDtypeStruct((t, d), f32)
    return pl.pallas_call(
        body, name="glu_fwd", grid=(t // tm, d // tn),
        in_specs=[pl.BlockSpec((tm, d), lambda i, j: (i, 0)), wsp, wsp, tile],
        out_specs=[tile, tile, tile], out_shape=[out, out, out],
        compiler_params=_cp("parallel", "parallel"),
    )(ygb, wa, wb, x)


def _place():
    x, y, c = lax.axis_index("x"), lax.axis_index("y"), lax.axis_index("c")
    return x, y, c, [(1 - x, y), (x, 1 - y), (1 - x, 1 - y)]


def _all_gather(name, arrays):
    n = len(arrays)

    def body(*refs):
        ins, outs = refs[:n], refs[n:2 * n]
        send_sems, recv_sems, local_sems = refs[2 * n:]
        x, y, c, chips = _place()
        me, sib = (x, y, c), (x, y, 1 - c)

        def copy(i, k, block, to, src=None):
            dst = outs[i].at[4 * block[0] + 2 * block[1] + block[2]]
            return pltpu.make_async_remote_copy(
                src_ref=dst if src is None else src, dst_ref=dst,
                send_sem=send_sems.at[i * 7 + k], recv_sem=recv_sems.at[i * 7 + k],
                device_id=to, device_id_type=MESH)

        mine = [pltpu.make_async_copy(ins[i], outs[i].at[4 * x + 2 * y + c], local_sems.at[i]) for i in range(n)]
        for m in mine:
            m.start()
        first = []
        for i in range(n):
            first.append(copy(i, 0, me, sib, src=ins[i]))
            first += [copy(i, 1 + j, me, (*chip, c), src=ins[i]) for j, chip in enumerate(chips)]
        for cp in first:
            cp.start()
        passed = []
        for j, chip in enumerate(chips):
            for i in range(n):
                copy(i, 1 + j, (*chip, c), me).wait_recv()
                fwd = copy(i, 4 + j, (*chip, c), sib)
                fwd.start()
                passed.append(fwd)
        for i in range(n):
            copy(i, 0, sib, me).wait_recv()
        for j, chip in enumerate(chips):
            for i in range(n):
                copy(i, 4 + j, (*chip, 1 - c), me).wait_recv()
        for cp in first + passed:
            cp.wait_send()
        for m in mine:
            m.wait()

    return pl.pallas_call(
        body, name=name,
        in_specs=[ANY_SPEC] * n, out_specs=[ANY_SPEC] * n,
        out_shape=[jax.ShapeDtypeStruct((N_DEV,) + a.shape, a.dtype) for a in arrays],
        scratch_shapes=[pltpu.SemaphoreType.DMA((7 * n,)), pltpu.SemaphoreType.DMA((7 * n,)),
                        pltpu.SemaphoreType.DMA((n,))],
    )(*arrays)


def _tie(name, x, deps):
    def body(*refs):
        pass

    return pl.pallas_call(
        body, name=name, in_specs=[ANY_SPEC] * (1 + len(deps)), out_specs=ANY_SPEC,
        out_shape=jax.ShapeDtypeStruct(x.shape, x.dtype), input_output_aliases={0: 0},
    )(x, *deps)


def _xchg_copies(kind, srcs, lands, suffixes, send_sems, recv_sems):
    x, y, c, _ = _place()
    copies = []
    for i, (src, land, sfx) in enumerate(zip(srcs, lands, suffixes)):
        for k in range(N_DEV - 1):
            r = k + 1
            peer = (1 - x if r & 4 else x, 1 - y if r & 2 else y, 1 - c if r & 1 else c)
            if kind == "gather":
                s_ref, d_ref = src, land.at[(4 * x + 2 * y + c,) + sfx]
            else:
                s_ref, d_ref = src.at[4 * peer[0] + 2 * peer[1] + peer[2]], land.at[(k,) + sfx]
            copies.append(pltpu.make_async_remote_copy(
                src_ref=s_ref, dst_ref=d_ref, send_sem=send_sems.at[i * 7 + k], recv_sem=recv_sems.at[i * 7 + k],
                device_id=peer, device_id_type=MESH))
    return copies


def _xchg_start(name, kind, srcs, lands, suffixes=None):
    n = len(srcs)
    suffixes = suffixes or [()] * n

    def body(*refs):
        src, land = refs[:n], refs[n:2 * n]
        send_sems, recv_sems, token = refs[2 * n], refs[2 * n + 1], refs[-1]
        for cp in _xchg_copies(kind, src, land, suffixes, send_sems, recv_sems):
            cp.start()
        token[...] = jnp.zeros_like(token)

    arrays = list(srcs) + list(lands)
    outs = pl.pallas_call(
        body, name=name,
        out_shape=(pltpu.SemaphoreType.DMA((7 * n,)), pltpu.SemaphoreType.DMA((7 * n,)),
                   *[pltpu.HBM(a.shape, a.dtype) for a in arrays], jax.ShapeDtypeStruct((8, 128), f32)),
        in_specs=[HBM_SPEC] * (2 * n),
        out_specs=(SEM_SPEC, SEM_SPEC, *[HBM_SPEC] * (2 * n), VMEM_SPEC),
        input_output_aliases={i: 2 + i for i in range(2 * n)},
        compiler_params=pltpu.CompilerParams(has_side_effects=SIDE_EFFECT),
    )(*[pltpu.with_memory_space_constraint(a, pltpu.HBM) for a in arrays])
    return dict(kind=kind, n=n, suffixes=suffixes, send=outs[0], recv=outs[1], srcs=list(outs[2:2 + n]),
                lands=list(outs[2 + n:2 + 2 * n]), token=outs[-1])


def _xchg_wait(name, h, after, lands=None):
    n = h["n"]
    lands = h["lands"] if lands is None else lands

    def body(*refs):
        src, land = refs[:n], refs[n:2 * n]
        for cp in _xchg_copies(h["kind"], src, land, h["suffixes"], refs[2 * n], refs[2 * n + 1]):
            cp.wait_send()
            cp.wait_recv()

    arrays = h["srcs"] + list(lands)
    outs = pl.pallas_call(
        body, name=name,
        out_shape=tuple(pltpu.HBM(a.shape, a.dtype) for a in arrays),
        in_specs=[HBM_SPEC] * (2 * n) + [SEM_SPEC, SEM_SPEC] + [ANY_SPEC] * len(after),
        out_specs=tuple([HBM_SPEC] * (2 * n)),
        input_output_aliases={i: i for i in range(2 * n)},
        compiler_params=pltpu.CompilerParams(has_side_effects=SIDE_EFFECT),
    )(*arrays, h["send"], h["recv"], *after)
    return list(outs[n:])


def _rows(a):
    return a.reshape(-1, a.shape[-1])


def _row_tile(r):
    for tm in (512, 256, 128, 64, 32, 16, 8):
        if r % tm == 0:
            return tm
    return r


def _sum8(name, gathered):
    _, r, n = gathered.shape
    tm = _row_tile(r)

    def body(g_ref, o_ref):
        acc = g_ref[0]
        for k in range(1, N_DEV):
            acc = acc + g_ref[k]
        o_ref[...] = acc

    return pl.pallas_call(
        body, name=name, grid=(r // tm,),
        in_specs=[pl.BlockSpec((N_DEV, tm, n), lambda i: (0, i, 0))],
        out_specs=pl.BlockSpec((tm, n), lambda i: (i, 0)),
        out_shape=jax.ShapeDtypeStruct((r, n), f32),
        compiler_params=_cp("parallel"),
    )(gathered)


def _adamw(name, w, m, v, own, landed=None):
    shape = w.shape
    w2, m2, v2, o2 = _rows(w), _rows(m), _rows(v), _rows(own)
    r, n = w2.shape
    tm = _row_tile(r)
    c1 = 1.0 - ADAM_B1 ** ADAM_STEP
    c2 = 1.0 - ADAM_B2 ** ADAM_STEP
    extra = [] if landed is None else [landed.reshape(landed.shape[0], r, n)]

    def body(w_ref, m_ref, v_ref, o_ref, *refs):
        g = o_ref[...]
        if extra:
            for k in range(extra[0].shape[0]):
                g = g + refs[0][k].astype(f32)
        g_ref, d_ref, mn_ref, vn_ref = refs[len(extra):]
        mn = ADAM_B1 * m_ref[...] + (1.0 - ADAM_B1) * g
        vn = ADAM_B2 * v_ref[...] + (1.0 - ADAM_B2) * (g * g)
        g_ref[...] = g
        d_ref[...] = -ADAM_LR * ((mn / c1) / (jnp.sqrt(vn / c2) + ADAM_EPS) + ADAM_WD * w_ref[...])
        mn_ref[...] = mn
        vn_ref[...] = vn

    row = pl.BlockSpec((tm, n), lambda i: (i, 0))
    outs = pl.pallas_call(
        body, name=name, grid=(r // tm,),
        in_specs=[row] * 4 + [pl.BlockSpec((e.shape[0], tm, n), lambda i: (0, i, 0)) for e in extra],
        out_specs=[row] * 4, out_shape=[jax.ShapeDtypeStruct((r, n), f32)] * 4,
        compiler_params=_cp("parallel"),
    )(w2, m2, v2, o2, *extra)
    return [o.reshape(shape) for o in outs]


def _pack(arrays):
    flat = jnp.concatenate([a.reshape(-1).astype(f32) for a in arrays])
    pad = (-flat.shape[0]) % (128 * (512 if flat.shape[0] > 128 * 512 else 8))
    return jnp.pad(flat, (0, pad)).reshape(-1, 128)


def _unpack(packed, shapes):
    flat = packed.reshape(-1)
    out, off = [], 0
    for s in shapes:
        n = math.prod(s)
        out.append(flat[off:off + n].reshape(s))
        off += n
    return out


def _local_step(x, target, w, weights_of, send, last_small, nb, s):
    cos, sin = _rope_tables(s)
    g = {}
    ffn_saved = {}
    ffn_bufs = [lax.empty((N_DEV, 2, 2) + shp, f32)
                for shp in ((D_MODEL, FF_SHARD), (D_MODEL, FF_SHARD), (FF_SHARD, D_MODEL))]

    def ffn(xin, l, h, wts):
        y, a, b = _ffn_fwd(f"ffn_fwd_{l}{h}", xin, w["ffn_g"][l][h], *wts)
        ffn_saved[(l, h)] = (xin, a, b, wts)
        return y

    def ffn_back(dy, l, h):
        xin, a, b, wts = ffn_saved[(l, h)]
        dx, dg, hb, dyh, u, da, db = _ffn_dx(f"ffn_dx_{l}{h}", dy, xin, w["ffn_g"][l][h], *wts, a, b)
        g[f"ffn_g_{l}{h}"] = dg
        if (l, h) != (0, 0):
            ffn_bufs[:], halves = _ffn_dw(f"ffn_dw_{l}{h}", hb, dyh, u, da, db, ffn_bufs, l, h)
            return send(f"ffn_{l}{h}", dict(zip(("ffn_w1", "ffn_w3", "ffn_w2"), halves)), dx)
        hb = last_small(g, hb)
        ffn_bufs[0], half = _ffn_dw_one("ffn_dw_00_w1", hb, da, ffn_bufs[0], l, h)
        hb = send("ffn_00_w1", {"ffn_w1": half}, hb)
        ffn_bufs[1], half = _ffn_dw_one("ffn_dw_00_w3", hb, db, ffn_bufs[1], l, h)
        u = send("ffn_00_w3", {"ffn_w3": half}, u)
        ffn_bufs[2], half = _ffn_dw_one("ffn_dw_00_w2", u, dyh, ffn_bufs[2], l, h)
        return send("ffn_00_w2", {"ffn_w2": half}, dx)

    def slots(t):
        return t.reshape(N_DEV, D_MODEL // N_DEV, D_MODEL)

    x1 = ffn(x, 0, 0, weights_of(0, [])["ffn"])
    wg = weights_of(1, [x1])
    w_in, w_out = wg["w_in"], wg["w_out"]
    _, h0b = _norm_fwd("mix_norm_0", x1, w["mix_g"][0])
    proj = _mm("in_proj", h0b, w_in, "nn", tn=768)[0]
    o_raw, rprev, mret = _ret_fwd(proj, cos, sin, w["ret_g"], nb, s)
    lru = _lru_fwd(proj, w["conv_w"], w["conv_b"], w["lru_w_a"], w["lru_b_a"], w["lru_w_i"], w["lru_b_i"], w["lru_lam"], nb, s)
    merged = _ew("merge", lambda a, b: (jnp.concatenate([a, b], axis=1),), [mret, lru], [(D_MODEL, bf16)])[0]
    x2 = _mm("out_proj", merged, w_out, "nn", extras=[x1], epilogue=lambda acc, r: (acc + r,))[0]
    x3 = ffn(x2, 0, 1, weights_of(2, [x2])["ffn"])
    wg = weights_of(3, [x3])
    glu_a, glu_b = wg["glu_a"], wg["glu_b"]
    x4 = ffn(x3, 1, 0, wg["ffn"])
    u, _ = _norm_fwd("mix_norm_1", x4, w["mix_g"][1])
    lbr, lbi, bbr, bbi = _s5_prep(w["s5_lr"], w["s5_li"], w["s5_ldt"], w["s5_bre"], w["s5_bim"])
    lbr_f, lbi_f = lbr.reshape(1, -1), lbi.reshape(1, -1)
    wbr, wbi = _blockdiag(bbr).astype(bf16), _blockdiag(bbi).astype(bf16)
    wcr, wci = _blockdiag(w["s5_cre"]).astype(bf16), _blockdiag(w["s5_cim"]).astype(bf16)
    ygb, ypre, h0s = _s5_fwd(u, lbr_f, lbi_f, wbr, wbi, wcr, wci, w["s5_d"], nb, s)
    x5, gp, gq = _glu_fwd(ygb, glu_a, glu_b, x4)
    x6 = ffn(x5, 1, 1, weights_of(4, [x5])["ffn"])
    loss, dx6, g["final_g"] = _final_loss(x6, w["final_g"], target)

    dx5 = ffn_back(dx6, 1, 1)

    def glu_bwd(d, p, q):
        sg = jax.nn.sigmoid(q)
        return d * sg, d * p * sg * (1.0 - sg)

    dp, dq = _ew("glu_bwd", glu_bwd, [dx5, gp, gq], [(D_MODEL, bf16), (D_MODEL, bf16)])
    dyg = _mm("glu_dy_a", dp, glu_a, "nt")[0]
    dyg = _mm("glu_dy_b", dq, glu_b, "nt", extras=[dyg], epilogue=lambda acc, r: (acc + r,))[0]
    g["glu_a"], ga_half = _mm_tn("glu_dw_a", ygb, dp)
    g["glu_b"], gb_half = _mm_tn("glu_dw_b", ygb, dq)
    dyg = send("glu", {"glu_a": slots(ga_half), "glu_b": slots(gb_half)}, dyg)
    du, dlr, dli, dwbr, dwbi, dwcr, dwci, g["s5_d"] = _s5_bwd(dyg, ypre, u, h0s, lbr_f, lbi_f, wbr, wbi, wcr, wci, w["s5_d"], nb, s)
    g["s5_cre"], g["s5_cim"] = _blockdiag_t(dwcr), _blockdiag_t(dwci)
    g["s5_lr"], g["s5_li"], g["s5_ldt"], g["s5_bre"], g["s5_bim"] = _s5_prep_bwd(
        w["s5_lr"], w["s5_li"], w["s5_ldt"], w["s5_bre"], w["s5_bim"],
        (dlr.reshape(S5_GROUPS, S5_STATE), dli.reshape(S5_GROUPS, S5_STATE), _blockdiag_t(dwbr), _blockdiag_t(dwbi)))
    dx4, g["mix_g_1"] = _norm_bwd("mix_norm_1_bwd", du, x4, w["mix_g"][1], dx5)
    dx3 = ffn_back(dx4, 1, 0)
    dx2 = ffn_back(dx3, 0, 1)
    dmerged = _mm("out_proj_dx", dx2, w_out, "nt")[0]
    g["w_out"], wo_half = _mm_tn("out_proj_dw", merged, dx2)
    dmerged = send("w_out", {"w_out": slots(wo_half)}, dmerged)
    dq_, dk_, dv_, dgate, g["ret_g"] = _ret_bwd(dmerged, o_raw, rprev, proj, cos, sin, w["ret_g"], nb, s)
    (dxl, dgl, g["conv_w"], g["conv_b"], g["lru_w_a"], g["lru_b_a"], g["lru_w_i"], g["lru_b_i"], g["lru_lam"]) = _lru_bwd(
        dmerged, proj, w["conv_w"], w["conv_b"], w["lru_w_a"], w["lru_b_a"], w["lru_w_i"], w["lru_b_i"], w["lru_lam"], nb, s)
    dproj = _ew("dproj", lambda *p: (jnp.concatenate(p, axis=1),), [dq_, dk_, dv_, dgate, dxl, dgl], [(3072, bf16)])[0]
    dh0 = _mm("in_proj_dx", dproj, w_in, "nt")[0]
    g["w_in"], wi_half = _mm_tn("in_proj_dw", h0b, dproj)
    dh0 = send("w_in", {"w_in": jnp.transpose(wi_half.reshape(D_MODEL, N_DEV, IN_SHARD), (1, 0, 2))}, dh0)
    dx1, g["mix_g_0"] = _norm_bwd("mix_norm_0_bwd", dh0, x1, w["mix_g"][0], dx2)
    dx0 = ffn_back(dx1, 0, 0)
    g["ffn_w1"], g["ffn_w3"], g["ffn_w2"] = ffn_bufs
    return loss, dx0, g


_WEIGHTS = ["ffn_norm_g", "ffn_w1", "ffn_w3", "ffn_w2", "mix_norm_g", "w_in_even", "w_out_even", "ret_norm_g", "conv_w",
            "conv_b", "lru_w_a", "lru_b_a", "lru_w_i", "lru_b_i", "lru_lambda", "s5_lambda_re", "s5_lambda_im", "s5_log_dt",
            "s5_b_re", "s5_b_im", "s5_c_re", "s5_c_im", "s5_d", "glu_w_a", "glu_w_b", "final_norm_g"]
_BIG = ["ffn_w1", "ffn_w3", "ffn_w2", "w_in_even", "w_out_even", "glu_w_a", "glu_w_b"]
_SMALL_SHARDED = ["ffn_norm_g", "conv_w", "s5_d"]
_SMALL = [n for n in _WEIGHTS if n not in _BIG]


def kernel(x, ffn_norm_g, ffn_w1, ffn_w3, ffn_w2, mix_norm_g, w_in_even, w_out_even, ret_norm_g, conv_w, conv_b, lru_w_a, lru_b_a, lru_w_i, lru_b_i, lru_lambda, s5_lambda_re, s5_lambda_im, s5_log_dt, s5_b_re, s5_b_im, s5_c_re, s5_c_im, s5_d, glu_w_a, glu_w_b, final_norm_g, loss_target, m_ffn_norm_g, m_ffn_w1, m_ffn_w3, m_ffn_w2, m_mix_norm_g, m_w_in_even, m_w_out_even, m_ret_norm_g, m_conv_w, m_conv_b, m_lru_w_a, m_lru_b_a, m_lru_w_i, m_lru_b_i, m_lru_lambda, m_s5_lambda_re, m_s5_lambda_im, m_s5_log_dt, m_s5_b_re, m_s5_b_im, m_s5_c_re, m_s5_c_im, m_s5_d, m_glu_w_a, m_glu_w_b, m_final_norm_g, v_ffn_norm_g, v_ffn_w1, v_ffn_w3, v_ffn_w2, v_mix_norm_g, v_w_in_even, v_w_out_even, v_ret_norm_g, v_conv_w, v_conv_b, v_lru_w_a, v_lru_b_a, v_lru_w_i, v_lru_b_i, v_lru_lambda, v_s5_lambda_re, v_s5_lambda_im, v_s5_log_dt, v_s5_b_re, v_s5_b_im, v_s5_c_re, v_s5_c_im, v_s5_d, v_glu_w_a, v_glu_w_b, v_final_norm_g):
    a = dict(locals())
    nb, s, d = x.shape
    ax, ay, ac = lax.axis_index("x"), lax.axis_index("y"), lax.axis_index("c")
    dev = 4 * ax + 2 * ay + ac
    chip = 2 * ax + ay

    def ffn_shards(l, h):
        return [ffn_w1[l, h].astype(bf16), ffn_w3[l, h].astype(bf16), ffn_w2[l, h].astype(bf16)]

    first = _all_gather("ag_first", ffn_shards(0, 0) + [_pack([ffn_norm_g, conv_w, s5_d])])
    sm = first[3].reshape(N_DEV, -1)
    ffn_g_full = jnp.transpose(sm[:, :512].reshape(N_DEV, 2, 2, 128), (1, 2, 0, 3)).reshape(2, 2, D_MODEL)
    conv_w_full = jnp.transpose(sm[:, 512:768].reshape(N_DEV, 4, 64), (1, 0, 2)).reshape(4, LRU_WIDTH)
    s5_d_full = sm[:, 768:896].reshape(1, D_MODEL)

    ag_src = [None, [w_in_even[0].astype(bf16), w_out_even[0].astype(bf16)], ffn_shards(0, 1),
              ffn_shards(1, 0) + [glu_w_a[0].astype(bf16), glu_w_b[0].astype(bf16)], ffn_shards(1, 1)]
    ag, token = [None], first[0]
    for k, grp in enumerate(ag_src):
        if grp is None:
            continue
        grp[0] = _tie(f"tie_ag_{k}", grp[0], [token])
        lands = [lax.dynamic_update_index_in_dim(lax.empty((N_DEV,) + t.shape, bf16), t, dev, 0) for t in grp]
        ag.append(_xchg_start(f"ag_start_{k}", "gather", grp, lands))
        token = ag[-1]["token"]

    def weights_of(k, after):
        if k == 0:
            return {"ffn": [first[0], _tie("tie_ag_started", first[1], [h["token"] for h in ag[1:]]), first[2]]}
        got = _xchg_wait(f"ag_wait_{k}", ag[k], after)
        if k == 1:
            return {"w_in": jnp.transpose(got[0], (1, 0, 2)).reshape(D_MODEL, N_DEV * IN_SHARD),
                    "w_out": got[1].reshape(D_MODEL, D_MODEL)}
        if k == 3:
            return {"ffn": got[:3], "glu_a": got[3].reshape(D_MODEL, D_MODEL), "glu_b": got[4].reshape(D_MODEL, D_MODEL)}
        return {"ffn": got}

    ffn_lands = [lax.empty((N_DEV - 1, 2, 2) + shp, bf16)
                 for shp in ((D_MODEL, FF_SHARD), (D_MODEL, FF_SHARD), (FF_SHARD, D_MODEL))]
    rs = []

    ffn_names = ("ffn_w1", "ffn_w3", "ffn_w2")

    def send(group, arrays, carry):
        srcs = list(arrays.values())
        if group.startswith("ffn_"):
            which = [ffn_names.index(n) for n in arrays]
            sfx = [(int(group[4]), int(group[5]))] * len(which)
            h = _xchg_start("rs_start_" + group, "scatter", srcs, [ffn_lands[k] for k in which], sfx)
            for k, land in zip(which, h["lands"]):
                ffn_lands[k] = land
        else:
            h = _xchg_start("rs_start_" + group, "scatter", srcs,
                            [lax.empty((N_DEV - 1,) + t.shape[1:], bf16) for t in srcs])
        rs.append((group, list(arrays), h))
        return _tie("tie_" + group, carry, [h["token"]])

    w = {
        "ffn_g": [[ffn_g_full[l, h].reshape(1, D_MODEL) for h in range(2)] for l in range(2)],
        "mix_g": [mix_norm_g[0:1], mix_norm_g[1:2]],
        "ret_g": ret_norm_g, "conv_w": conv_w_full, "conv_b": conv_b,
        "lru_w_a": lru_w_a[0], "lru_b_a": lru_b_a, "lru_w_i": lru_w_i[0], "lru_b_i": lru_b_i, "lru_lam": lru_lambda,
        "s5_lr": s5_lambda_re[0], "s5_li": s5_lambda_im[0], "s5_ldt": s5_log_dt.reshape(S5_GROUPS, 1),
        "s5_bre": jnp.swapaxes(s5_b_re[0], 1, 2), "s5_bim": jnp.swapaxes(s5_b_im[0], 1, 2),
        "s5_cre": s5_c_re[0], "s5_cim": s5_c_im[0], "s5_d": s5_d_full,
        "final_g": final_norm_g.reshape(1, D_MODEL),
    }

    small_grads = {}

    def last_small(g, carry):
        part = _small_partials(g)
        mine = _pack([part[n] for n in _SMALL])
        land = lax.dynamic_update_index_in_dim(lax.empty((N_DEV,) + mine.shape, f32), mine, dev, 0)
        h = _xchg_start("ag_start_small_grads", "gather", [mine], [land])
        small_grads.update(h=h, shapes=[part[n].shape for n in _SMALL])
        return _tie("tie_small_grads", carry, [h["token"]])

    loss_part, dx, g = _local_step(x.reshape(nb * s, d), loss_target.reshape(nb * s, d), w, weights_of, send, last_small,
                                   nb, s)
    loss = lax.psum(loss_part[0, 0], ("x", "y", "c"))
    (gath,) = _xchg_wait("ag_wait_small_grads", small_grads["h"], [dx])
    full = dict(zip(_SMALL, _unpack(_sum8("sum_small_grads", gath), small_grads["shapes"])))
    for n in _SMALL_SHARDED:
        width = a[n].shape[-1]
        full[n] = lax.dynamic_slice_in_dim(full[n], dev * width, width, axis=full[n].ndim - 1)
    shapes = [a[n].shape for n in _SMALL]
    packed = _adamw("adamw_small", _pack([a[n] for n in _SMALL]), _pack([a["m_" + n] for n in _SMALL]),
                    _pack([a["v_" + n] for n in _SMALL]), _pack([full[n] for n in _SMALL]))
    res = {n: vals for n, vals in zip(_SMALL, zip(*[_unpack(p, shapes) for p in packed]))}
    return _finish(a, g, dx, loss, res, packed, rs, ffn_lands, dev, nb, s, d)


def _small_partials(g):
    return {
        "ffn_norm_g": jnp.stack([jnp.stack([g[f"ffn_g_{l}{h}"][0] for h in range(2)]) for l in range(2)]),
        "mix_norm_g": jnp.concatenate([g["mix_g_0"], g["mix_g_1"]], axis=0),
        "ret_norm_g": g["ret_g"], "conv_w": g["conv_w"][None], "conv_b": g["conv_b"],
        "lru_w_a": g["lru_w_a"][None], "lru_b_a": g["lru_b_a"], "lru_w_i": g["lru_w_i"][None], "lru_b_i": g["lru_b_i"],
        "lru_lambda": g["lru_lam"], "s5_lambda_re": g["s5_lr"][None], "s5_lambda_im": g["s5_li"][None],
        "s5_log_dt": g["s5_ldt"].reshape(1, S5_GROUPS),
        "s5_b_re": jnp.swapaxes(g["s5_bre"], 1, 2)[None], "s5_b_im": jnp.swapaxes(g["s5_bim"], 1, 2)[None],
        "s5_c_re": g["s5_cre"][None], "s5_c_im": g["s5_cim"][None], "s5_d": g["s5_d"], "final_norm_g": g["final_g"][0],
    }


def _finish(a, g, dx, loss, res, packed, rs, ffn_lands, dev, nb, s, d):
    landed = {}
    for group, names, h in rs:
        if not group.startswith("ffn_"):
            landed.update(zip(names, _xchg_wait("rs_wait_" + group, h, [dx])))
    own = {n: lax.dynamic_index_in_dim(g[n], dev, axis=0, keepdims=False) for n in ("ffn_w1", "ffn_w3", "ffn_w2")}
    own["w_in"] = lax.dynamic_slice_in_dim(g["w_in"], dev * IN_SHARD, IN_SHARD, axis=1)
    for n in ("w_out", "glu_a", "glu_b"):
        own[n] = lax.dynamic_slice_in_dim(g[n], dev * (D_MODEL // N_DEV), D_MODEL // N_DEV, axis=0)

    def update(n, short):
        res[n] = _adamw("adamw_" + n, a[n], a["m_" + n], a["v_" + n], own[short].reshape(a[n].shape),
                        landed[short].reshape((N_DEV - 1,) + a[n].shape))

    for n, short in zip(_BIG[3:], ("w_in", "w_out", "glu_a", "glu_b")):
        update(n, short)
    after = [dx, packed[0]] + [res[n][0] for n in _BIG[3:]]
    ffn_names = ("ffn_w1", "ffn_w3", "ffn_w2")
    for group, names, h in rs:
        if group.startswith("ffn_"):
            which = [ffn_names.index(n) for n in names]
            for k, land in zip(which, _xchg_wait("rs_wait_" + group, h, after, [ffn_lands[k] for k in which])):
                ffn_lands[k] = land
    landed.update(zip(ffn_names, ffn_lands))
    for n in _BIG[:3]:
        update(n, n)

    out = [loss, dx.reshape(nb, s, d)]
    for k in range(4):
        out += [res[n][k] for n in _WEIGHTS]
    return tuple(out)
```

```python
import functools
import math

import numpy as np
import jax
import jax.numpy as jnp
from jax import lax
from jax.experimental import pallas as pl
from jax.experimental.pallas import tpu as pltpu

f32 = jnp.float32
bf16 = jnp.bfloat16

D_MODEL = 1024
N_DEV = 8
EPS = 1e-6
RET_HEADS = 4
HEAD_DIM = 128
RET_WIDTH = 512
RET_CHUNK = 128
ROPE_BASE = 10000.0
LRU_WIDTH = 512
LRU_BLOCKS = 4
LRU_C = 8.0
S5_GROUP = 16
S5_GROUPS = 64
S5_STATE = 64
S5_CHUNK = 128
S5_BLOCKS = 8
S5_BLOCK_STATES = 512
S5_SUBLANES = 8
D_FF = 2816
FF_SHARD = D_FF // N_DEV
FF_PAD = 384
IN_SHARD = 3072 // N_DEV
ADAM_LR = 0.001
ADAM_B1 = 0.9
ADAM_B2 = 0.999
ADAM_EPS = 1e-08
ADAM_WD = 0.01
ADAM_STEP = 10

VMEM_LIMIT = 56 * 1024 * 1024
VMEM_SPEC = pl.BlockSpec(memory_space=pltpu.VMEM)
ANY_SPEC = pl.BlockSpec(memory_space=pl.ANY)
HBM_SPEC = pl.BlockSpec(memory_space=pltpu.HBM)
SEM_SPEC = pl.BlockSpec(memory_space=pltpu.SEMAPHORE)
SIDE_EFFECT = pltpu.SideEffectType.DATAFLOW_SIDE_EFFECTING
MESH = pl.DeviceIdType.MESH


def _cp(*sem):
    return pltpu.CompilerParams(dimension_semantics=sem, vmem_limit_bytes=VMEM_LIMIT)


def _nn(a, b):
    return jnp.dot(a, b, preferred_element_type=f32)


def _nt(a, b):
    return lax.dot_general(a, b, (((1,), (1,)), ((), ())), preferred_element_type=f32)


def _tn(a, b):
    return lax.dot_general(a, b, (((0,), (0,)), ((), ())), preferred_element_type=f32)


def _rms_fwd(x, g):
    r = lax.rsqrt(jnp.mean(x * x, axis=-1, keepdims=True) + EPS)
    xn = x * r
    return xn * g, xn, r


def _rms_bwd(dh, xn, r, g):
    dxn = dh * g
    dx = r * (dxn - xn * jnp.mean(dxn * xn, axis=-1, keepdims=True))
    dg = jnp.sum(dh * xn, axis=0, keepdims=True)
    return dx, dg


def _shift_dn(v, d, row, fill=0.0):
    return jnp.where(row >= d, pltpu.roll(v, d, 0), fill)


def _shift_up(v, d, row, fill=0.0):
    n = v.shape[0]
    return jnp.where(row < n - d, pltpu.roll(v, n - d, 0), fill)


def _ew(name, fn, ins, outs, tm=512):
    t = ins[0].shape[0]
    n_in = len(ins)

    def body(*refs):
        res = fn(*[r[...] for r in refs[:n_in]])
        for o, v in zip(refs[n_in:], res):
            o[...] = v.astype(o.dtype)

    return pl.pallas_call(
        body, name=name, grid=(t // tm,),
        in_specs=[pl.BlockSpec((tm, a.shape[1]), lambda i: (i, 0)) for a in ins],
        out_specs=[pl.BlockSpec((tm, n), lambda i: (i, 0)) for n, _ in outs],
        out_shape=[jax.ShapeDtypeStruct((t, n), dt) for n, dt in outs],
        compiler_params=_cp("parallel"),
    )(*ins)


def _mm(name, x, w, kind, extras=(), epilogue=None, outs=None, tm=512, tn=512):
    t = x.shape[0]
    n = w.shape[1] if kind == "nn" else w.shape[0]
    tn = min(tn, n)
    outs = outs or [f32]
    n_ex = len(extras)

    def body(x_ref, w_ref, *refs):
        xb = x_ref[...].astype(bf16)
        acc = _nn(xb, w_ref[...]) if kind == "nn" else _nt(xb, w_ref[...])
        res = epilogue(acc, *[r[...] for r in refs[:n_ex]]) if epilogue else (acc,)
        for o, v in zip(refs[n_ex:], res):
            o[...] = v.astype(o.dtype)

    w_spec = (pl.BlockSpec((w.shape[0], tn), lambda i, j: (0, j)) if kind == "nn"
              else pl.BlockSpec((tn, w.shape[1]), lambda i, j: (j, 0)))
    tile = pl.BlockSpec((tm, tn), lambda i, j: (i, j))
    return pl.pallas_call(
        body, name=name, grid=(t // tm, n // tn),
        in_specs=[pl.BlockSpec((tm, x.shape[1]), lambda i, j: (i, 0)), w_spec] + [tile] * n_ex,
        out_specs=[tile] * len(outs),
        out_shape=[jax.ShapeDtypeStruct((t, n), dt) for dt in outs],
        compiler_params=_cp("parallel", "parallel"),
    )(x, w, *extras)


def _mm_tn(name, x, y, tk=1024, tn=512, tt=512):
    t, k = x.shape
    n = y.shape[1]
    tk, tn = min(tk, k), min(tn, n)

    def body(x_ref, y_ref, o_ref, ob_ref):
        @pl.when(pl.program_id(2) == 0)
        def _():
            o_ref[...] = jnp.zeros_like(o_ref)
        o_ref[...] += _tn(x_ref[...].astype(bf16), y_ref[...].astype(bf16))

        @pl.when(pl.program_id(2) == pl.num_programs(2) - 1)
        def _():
            ob_ref[...] = o_ref[...].astype(bf16)

    out = pl.BlockSpec((tk, tn), lambda i, j, s: (i, j))
    return pl.pallas_call(
        body, name=name, grid=(k // tk, n // tn, t // tt),
        in_specs=[pl.BlockSpec((tt, tk), lambda i, j, s: (s, i)), pl.BlockSpec((tt, tn), lambda i, j, s: (s, j))],
        out_specs=[out, out],
        out_shape=[jax.ShapeDtypeStruct((k, n), f32), jax.ShapeDtypeStruct((k, n), bf16)],
        compiler_params=_cp("parallel", "parallel", "arbitrary"),
    )(x, y)


def _norm_fwd(name, x, g, tm=512):
    t, d = x.shape

    def body(x_ref, g_ref, h_ref, hb_ref):
        h, _, _ = _rms_fwd(x_ref[...], g_ref[...])
        h_ref[...] = h
        hb_ref[...] = h.astype(bf16)

    row = pl.BlockSpec((tm, d), lambda i: (i, 0))
    return pl.pallas_call(
        body, name=name, grid=(t // tm,),
        in_specs=[row, pl.BlockSpec((1, d), lambda i: (0, 0))],
        out_specs=[row, row],
        out_shape=[jax.ShapeDtypeStruct((t, d), f32), jax.ShapeDtypeStruct((t, d), bf16)],
        compiler_params=_cp("parallel"),
    )(x, g)


def _norm_bwd(name, dh, x, g, dres, tm=512):
    t, d = x.shape

    def body(dh_ref, x_ref, g_ref, dres_ref, dx_ref, dg_ref):
        gv = g_ref[...]
        _, xn, r = _rms_fwd(x_ref[...], gv)
        dx, dg = _rms_bwd(dh_ref[...], xn, r, gv)
        dx_ref[...] = dres_ref[...] + dx

        @pl.when(pl.program_id(0) == 0)
        def _():
            dg_ref[...] = jnp.zeros_like(dg_ref)
        dg_ref[...] += dg

    row = pl.BlockSpec((tm, d), lambda i: (i, 0))
    vec = pl.BlockSpec((1, d), lambda i: (0, 0))
    return pl.pallas_call(
        body, name=name, grid=(t // tm,),
        in_specs=[row, row, vec, row],
        out_specs=[row, vec],
        out_shape=[jax.ShapeDtypeStruct((t, d), f32), jax.ShapeDtypeStruct((1, d), f32)],
        compiler_params=_cp("arbitrary"),
    )(dh, x, g, dres)


def _final_loss(x, g, target, tm=512):
    t, d = x.shape

    def body(x_ref, g_ref, t_ref, loss_ref, dx_ref, dg_ref):
        gv = g_ref[...]
        y, xn, r = _rms_fwd(x_ref[...], gv)
        err = y - t_ref[...]
        dy = err * (1.0 / d)
        dx, dg = _rms_bwd(dy, xn, r, gv)
        dx_ref[...] = dx

        @pl.when(pl.program_id(0) == 0)
        def _():
            dg_ref[...] = jnp.zeros_like(dg_ref)
            loss_ref[...] = jnp.zeros_like(loss_ref)
        dg_ref[...] += dg
        loss_ref[...] += jnp.full((1, 128), 0.5 / d, f32) * jnp.sum(err * err)

    row = pl.BlockSpec((tm, d), lambda i: (i, 0))
    vec = pl.BlockSpec((1, d), lambda i: (0, 0))
    return pl.pallas_call(
        body, name="final_loss", grid=(t // tm,),
        in_specs=[row, vec, row],
        out_specs=[pl.BlockSpec((1, 128), lambda i: (0, 0)), row, vec],
        out_shape=[jax.ShapeDtypeStruct((1, 128), f32), jax.ShapeDtypeStruct((t, d), f32),
                   jax.ShapeDtypeStruct((1, d), f32)],
        compiler_params=_cp("arbitrary"),
    )(x, g, target)


def _load_ffn_weights(hbm_refs, vmem_refs, sems):
    @pl.when(pl.program_id(0) == 0)
    def _():
        copies = []
        for k, (src, dst) in enumerate(zip(hbm_refs, vmem_refs)):
            for j in range(N_DEV):
                half = pl.ds((j % 2) * FF_PAD, FF_PAD)
                window = dst.at[j // 2, half, :] if k == 2 else dst.at[j // 2, :, half]
                copies.append(pltpu.make_async_copy(src.at[j], window, sems.at[k * N_DEV + j]))
        for cp in copies:
            cp.start()
        for cp in copies:
            cp.wait()


def _ffn_weight_scratch(nj, d, ff):
    return [pltpu.VMEM((nj, d, ff), bf16), pltpu.VMEM((nj, d, ff), bf16), pltpu.VMEM((nj, ff, d), bf16),
            pltpu.SemaphoreType.DMA((3 * N_DEV,))]


def _ffn_fwd(name, x, g, w1, w3, w2, tm=256):
    t, d = x.shape
    nj, ff = N_DEV // 2, 2 * FF_PAD

    def body(x_ref, g_ref, w1_hbm, w3_hbm, w2_hbm, y_ref, a_ref, b_ref, w1_ref, w3_ref, w2_ref, sems):
        _load_ffn_weights((w1_hbm, w3_hbm, w2_hbm), (w1_ref, w3_ref, w2_ref), sems)
        xv = x_ref[...]
        h, _, _ = _rms_fwd(xv, g_ref[...])
        hb = h.astype(bf16)
        acc = jnp.zeros((tm, d), f32)
        for j in range(nj):
            a = _nn(hb, w1_ref[j])
            b = _nn(hb, w3_ref[j])
            a_ref[j] = a.astype(bf16)
            b_ref[j] = b.astype(bf16)
            u = (a * jax.nn.sigmoid(a) * b).astype(bf16)
            acc = acc + _nn(u, w2_ref[j])
        y_ref[...] = xv + 0.5 * acc

    row = pl.BlockSpec((tm, d), lambda i: (i, 0))
    mid = pl.BlockSpec((nj, tm, ff), lambda i: (0, i, 0))
    return pl.pallas_call(
        body, name=name, grid=(t // tm,),
        in_specs=[row, pl.BlockSpec((1, d), lambda i: (0, 0)), ANY_SPEC, ANY_SPEC, ANY_SPEC],
        out_specs=[row, mid, mid],
        out_shape=[jax.ShapeDtypeStruct((t, d), f32), jax.ShapeDtypeStruct((nj, t, ff), bf16),
                   jax.ShapeDtypeStruct((nj, t, ff), bf16)],
        scratch_shapes=_ffn_weight_scratch(nj, d, ff),
        compiler_params=_cp("arbitrary"),
    )(x, g, w1, w3, w2)


def _ffn_dx(name, dy, x, g, w1, w3, w2, a, b, tm=256):
    t, d = x.shape
    nj, ff = N_DEV // 2, 2 * FF_PAD

    def body(dy_ref, x_ref, g_ref, w1_hbm, w3_hbm, w2_hbm, a_ref, b_ref,
             dx_ref, dg_ref, hb_ref, dyh_ref, u_ref, da_ref, db_ref, w1_ref, w3_ref, w2_ref, sems):
        _load_ffn_weights((w1_hbm, w3_hbm, w2_hbm), (w1_ref, w3_ref, w2_ref), sems)
        gv = g_ref[...]
        h, xn, r = _rms_fwd(x_ref[...], gv)
        hb_ref[...] = h.astype(bf16)
        dyv = dy_ref[...]
        dyh = (0.5 * dyv).astype(bf16)
        dyh_ref[...] = dyh
        dh = jnp.zeros((tm, d), f32)
        for j in range(nj):
            av = a_ref[j].astype(f32)
            bv = b_ref[j].astype(f32)
            s = jax.nn.sigmoid(av)
            silu = av * s
            u_ref[j] = (silu * bv).astype(bf16)
            du = _nt(dyh, w2_ref[j])
            dab = (du * bv * (s * (1.0 + av * (1.0 - s)))).astype(bf16)
            dbb = (du * silu).astype(bf16)
            da_ref[j] = dab
            db_ref[j] = dbb
            dh = dh + _nt(dab, w1_ref[j]) + _nt(dbb, w3_ref[j])
        dx, dg = _rms_bwd(dh, xn, r, gv)
        dx_ref[...] = dyv + dx

        @pl.when(pl.program_id(0) == 0)
        def _():
            dg_ref[...] = jnp.zeros_like(dg_ref)
        dg_ref[...] += dg

    row = pl.BlockSpec((tm, d), lambda i: (i, 0))
    vec = pl.BlockSpec((1, d), lambda i: (0, 0))
    mid = pl.BlockSpec((nj, tm, ff), lambda i: (0, i, 0))
    mid_shape = jax.ShapeDtypeStruct((nj, t, ff), bf16)
    return pl.pallas_call(
        body, name=name, grid=(t // tm,),
        in_specs=[row, row, vec, ANY_SPEC, ANY_SPEC, ANY_SPEC, mid, mid],
        out_specs=[row, vec, row, row, mid, mid, mid],
        out_shape=[jax.ShapeDtypeStruct((t, d), f32), jax.ShapeDtypeStruct((1, d), f32),
                   jax.ShapeDtypeStruct((t, d), bf16), jax.ShapeDtypeStruct((t, d), bf16),
                   mid_shape, mid_shape, mid_shape],
        scratch_shapes=_ffn_weight_scratch(nj, d, ff),
        compiler_params=_cp("arbitrary"),
    )(dy, x, g, w1, w3, w2, a, b)


def _ffn_dw_one(name, x, y, buf, l, h, tt=512):
    return _ffn_dw_calls(name, [(x, y)], [buf], l, h, tt)


def _ffn_dw(name, hb, dyh, u, da, db, bufs, l, h, tt=512):
    outs = _ffn_dw_calls(name, [(hb, da), (hb, db), (u, dyh)], bufs, l, h, tt)
    return outs[:3], outs[3:]


def _ffn_dw_calls(name, products, bufs, l, h, tt):
    n = len(products)
    t = products[0][0].shape[-2]
    pairs = N_DEV // 2
    cut_cols = [x.ndim == 2 for x, _ in products]

    def body(*refs):
        ins, outs, accs = refs[:2 * n], refs[3 * n:5 * n], refs[5 * n:]
        s = pl.program_id(1)
        for k in range(n):
            x_ref, y_ref = ins[2 * k], ins[2 * k + 1]
            xv = x_ref[0] if len(x_ref.shape) == 3 else x_ref[...]
            yv = y_ref[0] if len(y_ref.shape) == 3 else y_ref[...]
            prod = _tn(xv, yv)

            @pl.when(s == 0)
            def _():
                accs[k][...] = prod

            @pl.when(s > 0)
            def _():
                accs[k][...] += prod

        @pl.when(s == pl.num_programs(1) - 1)
        def _():
            for k in range(n):
                acc = accs[k][...]
                for e in range(2):
                    lo = e * FF_PAD
                    part = acc[:, lo:lo + FF_SHARD] if cut_cols[k] else acc[lo:lo + FF_SHARD, :]
                    outs[k][e] = part
                    outs[n + k][e] = part.astype(bf16)

    def in_spec(a):
        if a.ndim == 3:
            return pl.BlockSpec((1, tt, a.shape[-1]), lambda p, s: (p, s, 0))
        return pl.BlockSpec((tt, a.shape[-1]), lambda p, s: (s, 0))

    in_specs, out_f32, out_b16, shapes_b16, scratch = [], [], [], [], []
    for (x, y), buf in zip(products, bufs):
        in_specs += [in_spec(x), in_spec(y)]
        k_, n_ = buf.shape[-2:]
        out_f32.append(pl.BlockSpec((2, None, None, k_, n_), lambda p, s: (p, l, h, 0, 0)))
        out_b16.append(pl.BlockSpec((2, k_, n_), lambda p, s: (p, 0, 0)))
        shapes_b16.append(jax.ShapeDtypeStruct((N_DEV, k_, n_), bf16))
        scratch.append(pltpu.VMEM((x.shape[-1], y.shape[-1]), f32))
    flat = [a for xy in products for a in xy]
    return pl.pallas_call(
        body, name=name, grid=(pairs, t // tt),
        in_specs=in_specs + [ANY_SPEC] * n,
        out_specs=out_f32 + out_b16,
        out_shape=[jax.ShapeDtypeStruct(b.shape, b.dtype) for b in bufs] + shapes_b16,
        input_output_aliases={2 * n + k: k for k in range(n)},
        scratch_shapes=scratch,
        compiler_params=_cp("parallel", "arbitrary"),
    )(*flat, *bufs)


_LOG_GAMMA = [float(np.log1p(-np.float32(2.0) ** np.float32(-5.0 - h))) for h in range(RET_HEADS)]


def _ret_consts(h):
    lg = jnp.where(h == 0, _LOG_GAMMA[0], jnp.where(h == 1, _LOG_GAMMA[1],
                   jnp.where(h == 2, _LOG_GAMMA[2], _LOG_GAMMA[3]))).astype(f32)
    c = RET_CHUNK
    r = lax.broadcasted_iota(jnp.int32, (c, c), 0)
    cc = lax.broadcasted_iota(jnp.int32, (c, c), 1)
    decay = jnp.where(r >= cc, jnp.exp(lg * jnp.maximum((r - cc).astype(f32), 0.0)), 0.0)
    pos = lax.broadcasted_iota(jnp.int32, (c, 1), 0).astype(f32)
    kd = jnp.exp(lg * (c - 1.0 - pos))
    qd = jnp.exp(lg * (pos + 1.0))
    gc = jnp.exp(lg * c)
    return decay, kd, qd, gc


def _rope(x, cos, sin):
    return x * cos + pltpu.roll(x, HEAD_DIM // 2, 1) * sin


def _rope_t(g, cos, sin):
    return g * cos + pltpu.roll(g * sin, HEAD_DIM // 2, 1)


def _rope_tables(s):
    half = HEAD_DIM // 2
    inv = ROPE_BASE ** (-jnp.arange(half, dtype=f32) / half)
    ang = jnp.arange(s, dtype=f32)[:, None] * inv[None, :]
    cos, sin = jnp.cos(ang), jnp.sin(ang)
    return jnp.concatenate([cos, cos], axis=1), jnp.concatenate([-sin, sin], axis=1)


def _head_ln(o):
    mu = jnp.mean(o, axis=-1, keepdims=True)
    oc = o - mu
    rs = lax.rsqrt(jnp.mean(oc * oc, axis=-1, keepdims=True) + EPS)
    return oc * rs, rs


def _ret_fwd(proj, cos, sin, ret_g, nb, s):
    c = RET_CHUNK
    nc = s // c
    t = nb * s
    scale = HEAD_DIM ** -0.5

    def body(q_ref, k_ref, v_ref, gate_ref, cos_ref, sin_ref, g_ref, o_ref, rprev_ref, m_ref, r_acc):
        n = pl.program_id(2)

        @pl.when(n == 0)
        def _():
            r_acc[...] = jnp.zeros_like(r_acc)
        decay, kd, qd, gc = _ret_consts(pl.program_id(1))
        cs, sn = cos_ref[...], sin_ref[...]
        q = _rope(q_ref[...], cs, sn)
        k = _rope(k_ref[...], cs, sn) * scale
        vb = v_ref[...].astype(bf16)
        sc = _nt(q.astype(bf16), k.astype(bf16)) * decay
        rv = r_acc[...]
        rprev_ref[0] = rv
        o = _nn(sc.astype(bf16), vb) + _nn((q * qd).astype(bf16), rv.astype(bf16))
        o_ref[...] = o
        r_acc[...] = rv * gc + _tn((k * kd).astype(bf16), vb)
        y, _ = _head_ln(o)
        gate = gate_ref[...]
        m_ref[...] = y * g_ref[...] * (gate * jax.nn.sigmoid(gate))

    def col(off):
        return pl.BlockSpec((c, HEAD_DIM), lambda b, h, n: (b * nc + n, off + h))

    tab = pl.BlockSpec((c, HEAD_DIM), lambda b, h, n: (n, 0))
    return pl.pallas_call(
        body, name="ret_fwd", grid=(nb, RET_HEADS, nc),
        in_specs=[col(0), col(4), col(8), col(12), tab, tab, pl.BlockSpec((1, HEAD_DIM), lambda b, h, n: (0, h))],
        out_specs=[col(0), pl.BlockSpec((1, HEAD_DIM, HEAD_DIM), lambda b, h, n: ((b * RET_HEADS + h) * nc + n, 0, 0)),
                   col(0)],
        out_shape=[jax.ShapeDtypeStruct((t, RET_WIDTH), f32),
                   jax.ShapeDtypeStruct((nb * RET_HEADS * nc, HEAD_DIM, HEAD_DIM), f32),
                   jax.ShapeDtypeStruct((t, RET_WIDTH), f32)],
        scratch_shapes=[pltpu.VMEM((HEAD_DIM, HEAD_DIM), f32)],
        compiler_params=_cp("parallel", "parallel", "arbitrary"),
    )(proj, proj, proj, proj, cos, sin, ret_g)


def _ret_bwd(dmerged, o_raw, rprev, proj, cos, sin, ret_g, nb, s):
    c = RET_CHUNK
    nc = s // c
    t = nb * s
    scale = HEAD_DIM ** -0.5

    def body(dm_ref, o_ref, rprev_ref, q_ref, k_ref, v_ref, gate_ref, cos_ref, sin_ref, g_ref,
             dq_ref, dk_ref, dv_ref, dgate_ref, dg_ref, dr_acc):
        b, n = pl.program_id(1), pl.program_id(2)

        @pl.when(n == 0)
        def _():
            dr_acc[...] = jnp.zeros_like(dr_acc)

        @pl.when((n == 0) & (b == 0))
        def _():
            dg_ref[...] = jnp.zeros_like(dg_ref)
        decay, kd, qd, gc = _ret_consts(pl.program_id(0))
        cs, sn = cos_ref[...], sin_ref[...]
        q = _rope(q_ref[...], cs, sn)
        k = _rope(k_ref[...], cs, sn) * scale
        qb, kb = q.astype(bf16), k.astype(bf16)
        vb = v_ref[...].astype(bf16)
        sc = _nt(qb, kb) * decay
        gv = g_ref[...]
        y, rs = _head_ln(o_ref[...])
        gate = gate_ref[...]
        sg = jax.nn.sigmoid(gate)
        silu = gate * sg
        dm = dm_ref[...]
        dgate_ref[...] = dm * y * gv * (sg * (1.0 + gate * (1.0 - sg)))
        dyl = dm * gv * silu
        dg_ref[...] += jnp.sum(dm * y * silu, axis=0, keepdims=True)
        do = rs * (dyl - jnp.mean(dyl, axis=-1, keepdims=True) - y * jnp.mean(dyl * y, axis=-1, keepdims=True))
        dob = do.astype(bf16)
        rv = rprev_ref[0]
        drn = dr_acc[...]
        drb = drn.astype(bf16)
        ds = (_nt(dob, vb) * decay).astype(bf16)
        kdb = (k * kd).astype(bf16)
        qdb = (q * qd).astype(bf16)
        dq_r = _nn(ds, kb) + _nt(dob, rv.astype(bf16)) * qd
        dk_r = _tn(ds, qb) + _nt(vb, drb) * kd
        dv_ref[...] = _tn(sc.astype(bf16), dob) + _nn(kdb, drb)
        dr_acc[...] = drn * gc + _tn(qdb, dob)
        dq_ref[...] = _rope_t(dq_r, cs, sn)
        dk_ref[...] = _rope_t(dk_r * scale, cs, sn)

    def col(off):
        return pl.BlockSpec((c, HEAD_DIM), lambda h, b, n: (b * nc + nc - 1 - n, off + h))

    tab = pl.BlockSpec((c, HEAD_DIM), lambda h, b, n: (nc - 1 - n, 0))
    gsp = pl.BlockSpec((1, HEAD_DIM), lambda h, b, n: (0, h))
    out_t = jax.ShapeDtypeStruct((t, RET_WIDTH), f32)
    return pl.pallas_call(
        body, name="ret_bwd", grid=(RET_HEADS, nb, nc),
        in_specs=[col(0), col(0),
                  pl.BlockSpec((1, HEAD_DIM, HEAD_DIM), lambda h, b, n: ((b * RET_HEADS + h) * nc + nc - 1 - n, 0, 0)),
                  col(0), col(4), col(8), col(12), tab, tab, gsp],
        out_specs=[col(0), col(0), col(0), col(0), gsp],
        out_shape=[out_t, out_t, out_t, out_t, jax.ShapeDtypeStruct((1, RET_WIDTH), f32)],
        scratch_shapes=[pltpu.VMEM((HEAD_DIM, HEAD_DIM), f32)],
        compiler_params=_cp("parallel", "arbitrary", "arbitrary"),
    )(dmerged, o_raw, rprev, proj, proj, proj, proj, cos, sin, ret_g)


def _neg_expm1(z):
    series = -(z * (1.0 + z * (0.5 + z * (1.0 / 6.0 + z * (1.0 / 24.0)))))
    return jnp.where(z > -0.01, series, 1.0 - jnp.exp(z))


def _lru_gates(xc, pa, pi, lam):
    r = jax.nn.sigmoid(pa)
    i = jax.nn.sigmoid(pi)
    log_a = -LRU_C * r * jax.nn.softplus(-lam)
    a = jnp.exp(log_a)
    bx = jnp.sqrt(_neg_expm1(2.0 * log_a)) * i * xc
    return a, bx


def _scan_fwd(a, b, row):
    d = 1
    while d < a.shape[0]:
        b = a * _shift_dn(b, d, row) + b
        a = a * _shift_dn(a, d, row, 1.0)
        d *= 2
    return b


def _scan_bwd(c, b, row):
    d = 1
    while d < c.shape[0]:
        b = c * _shift_up(b, d, row) + b
        c = c * _shift_up(c, d, row, 1.0)
        d *= 2
    return b


def _conv_fwd(x, cw, cb, row):
    return (cb + cw[3:4] * x + cw[2:3] * _shift_dn(x, 1, row) + cw[1:2] * _shift_dn(x, 2, row)
            + cw[0:1] * _shift_dn(x, 3, row))


def _lru_specs(s, order):
    def im(f):
        return (lambda b, g: f(b, g)) if order == "bg" else (lambda g, b: f(b, g))
    seq = lambda off: pl.BlockSpec((s, 128), im(lambda b, g: (b, off + g)))
    vec = pl.BlockSpec((1, 128), im(lambda b, g: (0, g)))
    cw = pl.BlockSpec((4, 128), im(lambda b, g: (0, g)))
    mat = pl.BlockSpec((1, 128, 128), im(lambda b, g: (g, 0, 0)))
    return seq, vec, cw, mat


def _lru_fwd(proj, conv_w, conv_b, w_a, b_a, w_i, b_i, lam, nb, s):
    def body(x_ref, gt_ref, cw_ref, cb_ref, wa_ref, ba_ref, wi_ref, bi_ref, lam_ref, out_ref):
        row = lax.broadcasted_iota(jnp.int32, (s, 128), 0)
        xc = _conv_fwd(x_ref[...], cw_ref[...], cb_ref[...], row)
        xcb = xc.astype(bf16)
        pa = _nn(xcb, wa_ref[0].astype(bf16)) + ba_ref[...]
        pi = _nn(xcb, wi_ref[0].astype(bf16)) + bi_ref[...]
        a, bx = _lru_gates(xc, pa, pi, lam_ref[...])
        h = _scan_fwd(a, bx, row)
        out_ref[...] = h * jax.nn.gelu(gt_ref[...])

    seq, vec, cw, mat = _lru_specs(s, "bg")
    return pl.pallas_call(
        body, name="lru_fwd", grid=(nb, LRU_BLOCKS),
        in_specs=[seq(16), seq(20), cw, vec, mat, vec, mat, vec, vec],
        out_specs=seq(0),
        out_shape=jax.ShapeDtypeStruct((nb * s, LRU_WIDTH), f32),
        compiler_params=_cp("parallel", "parallel"),
    )(proj, proj, conv_w, conv_b, w_a, b_a, w_i, b_i, lam)


def _lru_bwd(dmerged, proj, conv_w, conv_b, w_a, b_a, w_i, b_i, lam, nb, s):
    def body(dout_ref, x_ref, gt_ref, cw_ref, cb_ref, wa_ref, ba_ref, wi_ref, bi_ref, lam_ref,
             dx_ref, dgt_ref, dcw_ref, dcb_ref, dwa_ref, dba_ref, dwi_ref, dbi_ref, dlam_ref):
        row = lax.broadcasted_iota(jnp.int32, (s, 128), 0)
        x = x_ref[...]
        cwv = cw_ref[...]
        xc = _conv_fwd(x, cwv, cb_ref[...], row)
        xcb = xc.astype(bf16)
        wab, wib = wa_ref[0].astype(bf16), wi_ref[0].astype(bf16)
        pa = _nn(xcb, wab) + ba_ref[...]
        pi = _nn(xcb, wib) + bi_ref[...]
        (a, bx), gates_vjp = jax.vjp(_lru_gates, xc, pa, pi, lam_ref[...])
        h = _scan_fwd(a, bx, row)
        ge, gelu_vjp = jax.vjp(jax.nn.gelu, gt_ref[...])
        dout = dout_ref[...]
        dgt_ref[...] = gelu_vjp(dout * h)[0]
        adj = _scan_bwd(_shift_up(a, 1, row), dout * ge, row)
        dxc, dpa, dpi, dlam = gates_vjp((adj * _shift_dn(h, 1, row), adj))
        dpab, dpib = dpa.astype(bf16), dpi.astype(bf16)
        dxc = dxc + _nt(dpab, wab) + _nt(dpib, wib)
        dx_ref[...] = (cwv[3:4] * dxc + cwv[2:3] * _shift_up(dxc, 1, row) + cwv[1:2] * _shift_up(dxc, 2, row)
                       + cwv[0:1] * _shift_up(dxc, 3, row))

        @pl.when(pl.program_id(1) == 0)
        def _():
            for r in (dcw_ref, dcb_ref, dwa_ref, dba_ref, dwi_ref, dbi_ref, dlam_ref):
                r[...] = jnp.zeros_like(r)
        rsum = lambda v: jnp.sum(v, axis=0, keepdims=True)
        dcw_ref[...] += jnp.concatenate([rsum(dxc * _shift_dn(x, 3, row)), rsum(dxc * _shift_dn(x, 2, row)),
                                         rsum(dxc * _shift_dn(x, 1, row)), rsum(dxc * x)], axis=0)
        dcb_ref[...] += rsum(dxc)
        dwa_ref[0] += _tn(xcb, dpab)
        dwi_ref[0] += _tn(xcb, dpib)
        dba_ref[...] += rsum(dpa)
        dbi_ref[...] += rsum(dpi)
        dlam_ref[...] += dlam

    seq, vec, cw, mat = _lru_specs(s, "gb")
    t = nb * s
    vshape = jax.ShapeDtypeStruct((1, LRU_WIDTH), f32)
    mshape = jax.ShapeDtypeStruct((LRU_BLOCKS, 128, 128), f32)
    return pl.pallas_call(
        body, name="lru_bwd", grid=(LRU_BLOCKS, nb),
        in_specs=[seq(4), seq(16), seq(20), cw, vec, mat, vec, mat, vec, vec],
        out_specs=[seq(0), seq(0), cw, vec, mat, vec, mat, vec, vec],
        out_shape=[jax.ShapeDtypeStruct((t, LRU_WIDTH), f32), jax.ShapeDtypeStruct((t, LRU_WIDTH), f32),
                   jax.ShapeDtypeStruct((4, LRU_WIDTH), f32), vshape, mshape, vshape, mshape, vshape, vshape],
        compiler_params=_cp("parallel", "arbitrary"),
    )(dmerged, proj, proj, conv_w, conv_b, w_a, b_a, w_i, b_i, lam)


def _s5_disc(lr, li, ldt, bre, bim):
    dt = jnp.exp(ldt)
    mag = jnp.exp(lr * dt)
    lbr = mag * jnp.cos(li * dt)
    lbi = mag * jnp.sin(li * dt)
    den = lr * lr + li * li
    nr = lbr - 1.0
    fr = (nr * lr + lbi * li) / den
    fi = (lbi * lr - nr * li) / den
    bbr = fr[:, None, :] * bre - fi[:, None, :] * bim
    bbi = fr[:, None, :] * bim + fi[:, None, :] * bre
    return lbr, lbi, bbr, bbi


def _s5_prep(lr, li, ldt, bre, bim):
    def body(lr_ref, li_ref, ldt_ref, bre_ref, bim_ref, o1, o2, o3, o4):
        o1[...], o2[...], o3[...], o4[...] = _s5_disc(lr_ref[...], li_ref[...], ldt_ref[...], bre_ref[...], bim_ref[...])

    return pl.pallas_call(
        body, name="s5_prep", in_specs=[VMEM_SPEC] * 5, out_specs=[VMEM_SPEC] * 4,
        out_shape=[jax.ShapeDtypeStruct(lr.shape, f32), jax.ShapeDtypeStruct(lr.shape, f32),
                   jax.ShapeDtypeStruct(bre.shape, f32), jax.ShapeDtypeStruct(bre.shape, f32)],
    )(lr, li, ldt, bre, bim)


def _s5_prep_bwd(lr, li, ldt, bre, bim, cts):
    def body(lr_ref, li_ref, ldt_ref, bre_ref, bim_ref, g1, g2, g3, g4, o1, o2, o3, o4, o5):
        _, vjp = jax.vjp(_s5_disc, lr_ref[...], li_ref[...], ldt_ref[...], bre_ref[...], bim_ref[...])
        o1[...], o2[...], o3[...], o4[...], o5[...] = vjp((g1[...], g2[...], g3[...], g4[...]))

    return pl.pallas_call(
        body, name="s5_prep_bwd", in_specs=[VMEM_SPEC] * 9, out_specs=[VMEM_SPEC] * 5,
        out_shape=[jax.ShapeDtypeStruct(v.shape, f32) for v in (lr, li, ldt, bre, bim)],
    )(lr, li, ldt, bre, bim, *cts)


def _cmul(ar, ai, br, bi):
    return ar * br - ai * bi, ar * bi + ai * br


def _s5_pow_table(lr, li, n, row, up):
    ar = jnp.broadcast_to(lr, (n, lr.shape[1]))
    ai = jnp.broadcast_to(li, (n, li.shape[1]))
    shift = _shift_up if up else _shift_dn
    d = 1
    while d < n:
        ar, ai = _cmul(ar, ai, shift(ar, d, row, 1.0), shift(ai, d, row, 0.0))
        d *= 2
    return ar, ai


def _s5_scan(br, bi, lr, li, tab_r, tab_i, cr, ci, row, up):
    n = br.shape[0]
    sub = row & (S5_SUBLANES - 1)
    pr, pi = lr, li
    d = 1
    while d < S5_SUBLANES:
        keep = (sub < S5_SUBLANES - d) if up else (sub >= d)
        shift = n - d if up else d
        tr, ti = _cmul(pr, pi, pltpu.roll(br, shift, 0), pltpu.roll(bi, shift, 0))
        br, bi = br + jnp.where(keep, tr, 0.0), bi + jnp.where(keep, ti, 0.0)
        pr, pi = _cmul(pr, pi, pr, pi)
        d *= 2
    groups = list(range(n // S5_SUBLANES))
    out_r, out_i = [None] * len(groups), [None] * len(groups)
    edge = slice(0, 1) if up else slice(S5_SUBLANES - 1, S5_SUBLANES)
    for g in (reversed(groups) if up else groups):
        rows = slice(g * S5_SUBLANES, (g + 1) * S5_SUBLANES)
        tr, ti = _cmul(tab_r, tab_i, cr, ci)
        hr, hi = br[rows] + tr, bi[rows] + ti
        out_r[g], out_i[g] = hr, hi
        cr, ci = hr[edge], hi[edge]
    return jnp.concatenate(out_r, axis=0), jnp.concatenate(out_i, axis=0)


def _s5_specs(s, nc, order):
    def im(f):
        return (lambda b, k: f(b, k)) if order == "bk" else (lambda k, b: f(b, k))
    seq = pl.BlockSpec((s, 128), im(lambda b, k: (b, k)))
    lvec = pl.BlockSpec((1, S5_BLOCK_STATES), im(lambda b, k: (0, k)))
    dvec = pl.BlockSpec((1, 128), im(lambda b, k: (0, k)))
    wmat = pl.BlockSpec((1, 128, S5_BLOCK_STATES), im(lambda b, k: (k, 0, 0)))
    h0 = pl.BlockSpec((1, nc, 2, S5_BLOCK_STATES), im(lambda b, k: (b * S5_BLOCKS + k, 0, 0, 0)))
    return seq, lvec, dvec, wmat, h0


def _s5_fwd(u, lbr, lbi, wbr, wbi, wcr, wci, dskip, nb, s):
    ln = S5_CHUNK
    nc = s // ln

    def body(u_ref, lr_ref, li_ref, wbr_ref, wbi_ref, wcr_ref, wci_ref, d_ref, yg_ref, y_ref, h0_ref):
        row = lax.broadcasted_iota(jnp.int32, (ln, S5_BLOCK_STATES), 0)
        lr, li = lr_ref[...], li_ref[...]
        pr, pi = _s5_pow_table(lr, li, S5_SUBLANES, row[:S5_SUBLANES], False)
        dv = d_ref[...]

        def step(n, carry):
            h0r, h0i = carry
            st = pl.multiple_of(n * ln, ln)
            uc = u_ref[pl.ds(st, ln), :]
            ub = uc.astype(bf16)
            hr, hi = _s5_scan(_nn(ub, wbr_ref[0]), _nn(ub, wbi_ref[0]), lr, li, pr, pi, h0r, h0i, row, False)
            h0_ref[0, n, 0:1, :] = h0r
            h0_ref[0, n, 1:2, :] = h0i
            y = _nt(hr.astype(bf16), wcr_ref[0]) - _nt(hi.astype(bf16), wci_ref[0]) + dv * uc
            y_ref[pl.ds(st, ln), :] = y
            yg_ref[pl.ds(st, ln), :] = jax.nn.gelu(y).astype(bf16)
            return hr[ln - 1:ln, :], hi[ln - 1:ln, :]

        z = jnp.zeros((1, S5_BLOCK_STATES), f32)
        lax.fori_loop(0, nc, step, (z, z))

    seq, lvec, dvec, wmat, h0 = _s5_specs(s, nc, "bk")
    t = nb * s
    return pl.pallas_call(
        body, name="s5_fwd", grid=(nb, S5_BLOCKS),
        in_specs=[seq, lvec, lvec, wmat, wmat, wmat, wmat, dvec],
        out_specs=[seq, seq, h0],
        out_shape=[jax.ShapeDtypeStruct((t, D_MODEL), bf16), jax.ShapeDtypeStruct((t, D_MODEL), f32),
                   jax.ShapeDtypeStruct((nb * S5_BLOCKS, nc, 2, S5_BLOCK_STATES), f32)],
        compiler_params=_cp("parallel", "parallel"),
    )(u, lbr, lbi, wbr, wbi, wcr, wci, dskip)


def _s5_bwd(dyg, y, u, h0, lbr, lbi, wbr, wbi, wcr, wci, dskip, nb, s):
    ln = S5_CHUNK
    nc = s // ln

    def body(dyg_ref, y_ref, u_ref, h0_ref, lr_ref, li_ref, wbr_ref, wbi_ref, wcr_ref, wci_ref, d_ref,
             du_ref, dlr_ref, dli_ref, dwbr_ref, dwbi_ref, dwcr_ref, dwci_ref, dd_ref):
        @pl.when(pl.program_id(1) == 0)
        def _():
            for r in (dlr_ref, dli_ref, dwbr_ref, dwbi_ref, dwcr_ref, dwci_ref, dd_ref):
                r[...] = jnp.zeros_like(r)
        row = lax.broadcasted_iota(jnp.int32, (ln, S5_BLOCK_STATES), 0)
        lr, li = lr_ref[...], li_ref[...]
        pr, pi = _s5_pow_table(lr, li, S5_SUBLANES, row[:S5_SUBLANES], False)
        qr, qi = _s5_pow_table(lr, -li, S5_SUBLANES, row[:S5_SUBLANES], True)
        dv = d_ref[...]
        rsum = lambda v: jnp.sum(v, axis=0, keepdims=True)

        def step(i, carry):
            gnr, gni = carry
            n = nc - 1 - i
            st = pl.multiple_of(n * ln, ln)
            uc = u_ref[pl.ds(st, ln), :]
            ub = uc.astype(bf16)
            h0v = h0_ref[0, n]
            h0r, h0i = h0v[0:1], h0v[1:2]
            hr, hi = _s5_scan(_nn(ub, wbr_ref[0]), _nn(ub, wbi_ref[0]), lr, li, pr, pi, h0r, h0i, row, False)
            dy = jax.vjp(jax.nn.gelu, y_ref[pl.ds(st, ln), :])[1](dyg_ref[pl.ds(st, ln), :])[0]
            dyb = dy.astype(bf16)
            dd_ref[...] += rsum(dy * uc)
            gr, gi = _s5_scan(_nn(dyb, wcr_ref[0]), -_nn(dyb, wci_ref[0]), lr, -li, qr, qi, gnr, gni, row, True)
            hpr = jnp.where(row >= 1, pltpu.roll(hr, 1, 0), h0r)
            hpi = jnp.where(row >= 1, pltpu.roll(hi, 1, 0), h0i)
            dlr_ref[...] += rsum(gr * hpr + gi * hpi)
            dli_ref[...] += rsum(gi * hpr - gr * hpi)
            grb, gib = gr.astype(bf16), gi.astype(bf16)
            dwbr_ref[0] += _tn(ub, grb)
            dwbi_ref[0] += _tn(ub, gib)
            dwcr_ref[0] += _tn(dyb, hr.astype(bf16))
            dwci_ref[0] -= _tn(dyb, hi.astype(bf16))
            du_ref[pl.ds(st, ln), :] = _nt(grb, wbr_ref[0]) + _nt(gib, wbi_ref[0]) + dv * dy
            return gr[0:1, :], gi[0:1, :]

        z = jnp.zeros((1, S5_BLOCK_STATES), f32)
        lax.fori_loop(0, nc, step, (z, z))

    seq, lvec, dvec, wmat, h0s = _s5_specs(s, nc, "kb")
    t = nb * s
    lshape = jax.ShapeDtypeStruct((1, S5_BLOCKS * S5_BLOCK_STATES), f32)
    wshape = jax.ShapeDtypeStruct((S5_BLOCKS, 128, S5_BLOCK_STATES), f32)
    return pl.pallas_call(
        body, name="s5_bwd", grid=(S5_BLOCKS, nb),
        in_specs=[seq, seq, seq, h0s, lvec, lvec, wmat, wmat, wmat, wmat, dvec],
        out_specs=[seq, lvec, lvec, wmat, wmat, wmat, wmat, dvec],
        out_shape=[jax.ShapeDtypeStruct((t, D_MODEL), f32), lshape, lshape, wshape, wshape, wshape, wshape,
                   jax.ShapeDtypeStruct((1, D_MODEL), f32)],
        compiler_params=_cp("parallel", "arbitrary"),
    )(dyg, y, u, h0, lbr, lbi, wbr, wbi, wcr, wci, dskip)


def _blockdiag(w):
    w4 = w.reshape(S5_BLOCKS, 8, S5_GROUP, S5_STATE)
    same_group = jnp.eye(8, dtype=bool)[None, :, None, :, None]
    return jnp.where(same_group, w4[:, :, :, None, :], 0.0).reshape(S5_BLOCKS, 128, S5_BLOCK_STATES)


def _blockdiag_t(dw):
    d5 = dw.reshape(S5_BLOCKS, 8, S5_GROUP, 8, S5_STATE)
    diag = jnp.diagonal(d5, axis1=1, axis2=3)
    return jnp.moveaxis(diag, 3, 1).reshape(S5_GROUPS, S5_GROUP, S5_STATE)


def _glu_fwd(ygb, wa, wb, x, tm=512, tn=512):
    t, d = x.shape

    def body(y_ref, wa_ref, wb_ref, x_ref, o_ref, p_ref, q_ref):
        p = _nn(y_ref[...], wa_ref[...])
        q = _nn(y_ref[...], wb_ref[...])
        p_ref[...] = p
        q_ref[...] = q
        o_ref[...] = x_ref[...] + p * jax.nn.sigmoid(q)

    tile = pl.BlockSpec((tm, tn), lambda i, j: (i, j))
    wsp = pl.BlockSpec((d, tn), lambda i, j: (0, j))
    out = jax.ShapeDtypeStruct((t, d), f32)
    return pl.pallas_call(
        body, name="glu_fwd", grid=(t // tm, d // tn),
        in_specs=[pl.BlockSpec((tm, d), lambda i, j: (i, 0)), wsp, wsp, tile],
        out_specs=[tile, tile, tile], out_shape=[out, out, out],
        compiler_params=_cp("parallel", "parallel"),
    )(ygb, wa, wb, x)


def _place():
    x, y, c = lax.axis_index("x"), lax.axis_index("y"), lax.axis_index("c")
    return x, y, c, [(1 - x, y), (x, 1 - y), (1 - x, 1 - y)]


def _all_gather(name, arrays):
    n = len(arrays)

    def body(*refs):
        ins, outs = refs[:n], refs[n:2 * n]
        send_sems, recv_sems, local_sems = refs[2 * n:]
        x, y, c, chips = _place()
        me, sib = (x, y, c), (x, y, 1 - c)

        def copy(i, k, block, to, src=None):
            dst = outs[i].at[4 * block[0] + 2 * block[1] + block[2]]
            return pltpu.make_async_remote_copy(
                src_ref=dst if src is None else src, dst_ref=dst,
                send_sem=send_sems.at[i * 7 + k], recv_sem=recv_sems.at[i * 7 + k],
                device_id=to, device_id_type=MESH)

        mine = [pltpu.make_async_copy(ins[i], outs[i].at[4 * x + 2 * y + c], local_sems.at[i]) for i in range(n)]
        for m in mine:
            m.start()
        first = []
        for i in range(n):
            first.append(copy(i, 0, me, sib, src=ins[i]))
            first += [copy(i, 1 + j, me, (*chip, c), src=ins[i]) for j, chip in enumerate(chips)]
        for cp in first:
            cp.start()
        passed = []
        for j, chip in enumerate(chips):
            for i in range(n):
                copy(i, 1 + j, (*chip, c), me).wait_recv()
                fwd = copy(i, 4 + j, (*chip, c), sib)
                fwd.start()
                passed.append(fwd)
        for i in range(n):
            copy(i, 0, sib, me).wait_recv()
        for j, chip in enumerate(chips):
            for i in range(n):
                copy(i, 4 + j, (*chip, 1 - c), me).wait_recv()
        for cp in first + passed:
            cp.wait_send()
        for m in mine:
            m.wait()

    return pl.pallas_call(
        body, name=name,
        in_specs=[ANY_SPEC] * n, out_specs=[ANY_SPEC] * n,
        out_shape=[jax.ShapeDtypeStruct((N_DEV,) + a.shape, a.dtype) for a in arrays],
        scratch_shapes=[pltpu.SemaphoreType.DMA((7 * n,)), pltpu.SemaphoreType.DMA((7 * n,)),
                        pltpu.SemaphoreType.DMA((n,))],
    )(*arrays)


def _tie(name, x, deps):
    def body(*refs):
        pass

    return pl.pallas_call(
        body, name=name, in_specs=[ANY_SPEC] * (1 + len(deps)), out_specs=ANY_SPEC,
        out_shape=jax.ShapeDtypeStruct(x.shape, x.dtype), input_output_aliases={0: 0},
    )(x, *deps)


def _xchg_copies(kind, srcs, lands, suffixes, send_sems, recv_sems):
    x, y, c, _ = _place()
    copies = []
    for i, (src, land, sfx) in enumerate(zip(srcs, lands, suffixes)):
        for k in range(N_DEV - 1):
            r = k + 1
            peer = (1 - x if r & 4 else x, 1 - y if r & 2 else y, 1 - c if r & 1 else c)
            if kind == "gather":
                s_ref, d_ref = src, land.at[(4 * x + 2 * y + c,) + sfx]
            else:
                s_ref, d_ref = src.at[4 * peer[0] + 2 * peer[1] + peer[2]], land.at[(k,) + sfx]
            copies.append(pltpu.make_async_remote_copy(
                src_ref=s_ref, dst_ref=d_ref, send_sem=send_sems.at[i * 7 + k], recv_sem=recv_sems.at[i * 7 + k],
                device_id=peer, device_id_type=MESH))
    return copies


def _xchg_start(name, kind, srcs, lands, suffixes=None):
    n = len(srcs)
    suffixes = suffixes or [()] * n

    def body(*refs):
        src, land = refs[:n], refs[n:2 * n]
        send_sems, recv_sems, token = refs[2 * n], refs[2 * n + 1], refs[-1]
        for cp in _xchg_copies(kind, src, land, suffixes, send_sems, recv_sems):
            cp.start()
        token[...] = jnp.zeros_like(token)

    arrays = list(srcs) + list(lands)
    outs = pl.pallas_call(
        body, name=name,
        out_shape=(pltpu.SemaphoreType.DMA((7 * n,)), pltpu.SemaphoreType.DMA((7 * n,)),
                   *[pltpu.HBM(a.shape, a.dtype) for a in arrays], jax.ShapeDtypeStruct((8, 128), f32)),
        in_specs=[HBM_SPEC] * (2 * n),
        out_specs=(SEM_SPEC, SEM_SPEC, *[HBM_SPEC] * (2 * n), VMEM_SPEC),
        input_output_aliases={i: 2 + i for i in range(2 * n)},
        compiler_params=pltpu.CompilerParams(has_side_effects=SIDE_EFFECT),
    )(*[pltpu.with_memory_space_constraint(a, pltpu.HBM) for a in arrays])
    return dict(kind=kind, n=n, suffixes=suffixes, send=outs[0], recv=outs[1], srcs=list(outs[2:2 + n]),
                lands=list(outs[2 + n:2 + 2 * n]), token=outs[-1])


def _xchg_wait(name, h, after, lands=None):
    n = h["n"]
    lands = h["lands"] if lands is None else lands

    def body(*refs):
        src, land = refs[:n], refs[n:2 * n]
        for cp in _xchg_copies(h["kind"], src, land, h["suffixes"], refs[2 * n], refs[2 * n + 1]):
            cp.wait_send()
            cp.wait_recv()

    arrays = h["srcs"] + list(lands)
    outs = pl.pallas_call(
        body, name=name,
        out_shape=tuple(pltpu.HBM(a.shape, a.dtype) for a in arrays),
        in_specs=[HBM_SPEC] * (2 * n) + [SEM_SPEC, SEM_SPEC] + [ANY_SPEC] * len(after),
        out_specs=tuple([HBM_SPEC] * (2 * n)),
        input_output_aliases={i: i for i in range(2 * n)},
        compiler_params=pltpu.CompilerParams(has_side_effects=SIDE_EFFECT),
    )(*arrays, h["send"], h["recv"], *after)
    return list(outs[n:])


def _rows(a):
    return a.reshape(-1, a.shape[-1])


def _row_tile(r):
    for tm in (512, 256, 128, 64, 32, 16, 8):
        if r % tm == 0:
            return tm
    return r


def _sum8(name, gathered):
    _, r, n = gathered.shape
    tm = _row_tile(r)

    def body(g_ref, o_ref):
        acc = g_ref[0]
        for k in range(1, N_DEV):
            acc = acc + g_ref[k]
        o_ref[...] = acc

    return pl.pallas_call(
        body, name=name, grid=(r // tm,),
        in_specs=[pl.BlockSpec((N_DEV, tm, n), lambda i: (0, i, 0))],
        out_specs=pl.BlockSpec((tm, n), lambda i: (i, 0)),
        out_shape=jax.ShapeDtypeStruct((r, n), f32),
        compiler_params=_cp("parallel"),
    )(gathered)


def _adamw(name, w, m, v, own, landed=None):
    shape = w.shape
    w2, m2, v2, o2 = _rows(w), _rows(m), _rows(v), _rows(own)
    r, n = w2.shape
    tm = _row_tile(r)
    c1 = 1.0 - ADAM_B1 ** ADAM_STEP
    c2 = 1.0 - ADAM_B2 ** ADAM_STEP
    extra = [] if landed is None else [landed.reshape(landed.shape[0], r, n)]

    def body(w_ref, m_ref, v_ref, o_ref, *refs):
        g = o_ref[...]
        if extra:
            for k in range(extra[0].shape[0]):
                g = g + refs[0][k].astype(f32)
        g_ref, d_ref, mn_ref, vn_ref = refs[len(extra):]
        mn = ADAM_B1 * m_ref[...] + (1.0 - ADAM_B1) * g
        vn = ADAM_B2 * v_ref[...] + (1.0 - ADAM_B2) * (g * g)
        g_ref[...] = g
        d_ref[...] = -ADAM_LR * ((mn / c1) / (jnp.sqrt(vn / c2) + ADAM_EPS) + ADAM_WD * w_ref[...])
        mn_ref[...] = mn
        vn_ref[...] = vn

    row = pl.BlockSpec((tm, n), lambda i: (i, 0))
    outs = pl.pallas_call(
        body, name=name, grid=(r // tm,),
        in_specs=[row] * 4 + [pl.BlockSpec((e.shape[0], tm, n), lambda i: (0, i, 0)) for e in extra],
        out_specs=[row] * 4, out_shape=[jax.ShapeDtypeStruct((r, n), f32)] * 4,
        compiler_params=_cp("parallel"),
    )(w2, m2, v2, o2, *extra)
    return [o.reshape(shape) for o in outs]


def _pack(arrays):
    flat = jnp.concatenate([a.reshape(-1).astype(f32) for a in arrays])
    pad = (-flat.shape[0]) % (128 * (512 if flat.shape[0] > 128 * 512 else 8))
    return jnp.pad(flat, (0, pad)).reshape(-1, 128)


def _unpack(packed, shapes):
    flat = packed.reshape(-1)
    out, off = [], 0
    for s in shapes:
        n = math.prod(s)
        out.append(flat[off:off + n].reshape(s))
        off += n
    return out


def _local_step(x, target, w, weights_of, send, last_small, nb, s):
    cos, sin = _rope_tables(s)
    g = {}
    ffn_saved = {}
    ffn_bufs = [lax.empty((N_DEV, 2, 2) + shp, f32)
                for shp in ((D_MODEL, FF_SHARD), (D_MODEL, FF_SHARD), (FF_SHARD, D_MODEL))]

    def ffn(xin, l, h, wts):
        y, a, b = _ffn_fwd(f"ffn_fwd_{l}{h}", xin, w["ffn_g"][l][h], *wts)
        ffn_saved[(l, h)] = (xin, a, b, wts)
        return y

    def ffn_back(dy, l, h):
        xin, a, b, wts = ffn_saved[(l, h)]
        dx, dg, hb, dyh, u, da, db = _ffn_dx(f"ffn_dx_{l}{h}", dy, xin, w["ffn_g"][l][h], *wts, a, b)
        g[f"ffn_g_{l}{h}"] = dg
        if (l, h) != (0, 0):
            ffn_bufs[:], halves = _ffn_dw(f"ffn_dw_{l}{h}", hb, dyh, u, da, db, ffn_bufs, l, h)
            return send(f"ffn_{l}{h}", dict(zip(("ffn_w1", "ffn_w3", "ffn_w2"), halves)), dx)
        hb = last_small(g, hb)
        ffn_bufs[0], half = _ffn_dw_one("ffn_dw_00_w1", hb, da, ffn_bufs[0], l, h)
        hb = send("ffn_00_w1", {"ffn_w1": half}, hb)
        ffn_bufs[1], half = _ffn_dw_one("ffn_dw_00_w3", hb, db, ffn_bufs[1], l, h)
        u = send("ffn_00_w3", {"ffn_w3": half}, u)
        ffn_bufs[2], half = _ffn_dw_one("ffn_dw_00_w2", u, dyh, ffn_bufs[2], l, h)
        return send("ffn_00_w2", {"ffn_w2": half}, dx)

    def slots(t):
        return t.reshape(N_DEV, D_MODEL // N_DEV, D_MODEL)

    x1 = ffn(x, 0, 0, weights_of(0, [])["ffn"])
    wg = weights_of(1, [x1])
    w_in, w_out = wg["w_in"], wg["w_out"]
    _, h0b = _norm_fwd("mix_norm_0", x1, w["mix_g"][0])
    proj = _mm("in_proj", h0b, w_in, "nn", tn=768)[0]
    o_raw, rprev, mret = _ret_fwd(proj, cos, sin, w["ret_g"], nb, s)
    lru = _lru_fwd(proj, w["conv_w"], w["conv_b"], w["lru_w_a"], w["lru_b_a"], w["lru_w_i"], w["lru_b_i"], w["lru_lam"], nb, s)
    merged = _ew("merge", lambda a, b: (jnp.concatenate([a, b], axis=1),), [mret, lru], [(D_MODEL, bf16)])[0]
    x2 = _mm("out_proj", merged, w_out, "nn", extras=[x1], epilogue=lambda acc, r: (acc + r,))[0]
    x3 = ffn(x2, 0, 1, weights_of(2, [x2])["ffn"])
    wg = weights_of(3, [x3])
    glu_a, glu_b = wg["glu_a"], wg["glu_b"]
    x4 = ffn(x3, 1, 0, wg["ffn"])
    u, _ = _norm_fwd("mix_norm_1", x4, w["mix_g"][1])
    lbr, lbi, bbr, bbi = _s5_prep(w["s5_lr"], w["s5_li"], w["s5_ldt"], w["s5_bre"], w["s5_bim"])
    lbr_f, lbi_f = lbr.reshape(1, -1), lbi.reshape(1, -1)
    wbr, wbi = _blockdiag(bbr).astype(bf16), _blockdiag(bbi).astype(bf16)
    wcr, wci = _blockdiag(w["s5_cre"]).astype(bf16), _blockdiag(w["s5_cim"]).astype(bf16)
    ygb, ypre, h0s = _s5_fwd(u, lbr_f, lbi_f, wbr, wbi, wcr, wci, w["s5_d"], nb, s)
    x5, gp, gq = _glu_fwd(ygb, glu_a, glu_b, x4)
    x6 = ffn(x5, 1, 1, weights_of(4, [x5])["ffn"])
    loss, dx6, g["final_g"] = _final_loss(x6, w["final_g"], target)

    dx5 = ffn_back(dx6, 1, 1)

    def glu_bwd(d, p, q):
        sg = jax.nn.sigmoid(q)
        return d * sg, d * p * sg * (1.0 - sg)

    dp, dq = _ew("glu_bwd", glu_bwd, [dx5, gp, gq], [(D_MODEL, bf16), (D_MODEL, bf16)])
    dyg = _mm("glu_dy_a", dp, glu_a, "nt")[0]
    dyg = _mm("glu_dy_b", dq, glu_b, "nt", extras=[dyg], epilogue=lambda acc, r: (acc + r,))[0]
    g["glu_a"], ga_half = _mm_tn("glu_dw_a", ygb, dp)
    g["glu_b"], gb_half = _mm_tn("glu_dw_b", ygb, dq)
    dyg = send("glu", {"glu_a": slots(ga_half), "glu_b": slots(gb_half)}, dyg)
    du, dlr, dli, dwbr, dwbi, dwcr, dwci, g["s5_d"] = _s5_bwd(dyg, ypre, u, h0s, lbr_f, lbi_f, wbr, wbi, wcr, wci, w["s5_d"], nb, s)
    g["s5_cre"], g["s5_cim"] = _blockdiag_t(dwcr), _blockdiag_t(dwci)
    g["s5_lr"], g["s5_li"], g["s5_ldt"], g["s5_bre"], g["s5_bim"] = _s5_prep_bwd(
        w["s5_lr"], w["s5_li"], w["s5_ldt"], w["s5_bre"], w["s5_bim"],
        (dlr.reshape(S5_GROUPS, S5_STATE), dli.reshape(S5_GROUPS, S5_STATE), _blockdiag_t(dwbr), _blockdiag_t(dwbi)))
    dx4, g["mix_g_1"] = _norm_bwd("mix_norm_1_bwd", du, x4, w["mix_g"][1], dx5)
    dx3 = ffn_back(dx4, 1, 0)
    dx2 = ffn_back(dx3, 0, 1)
    dmerged = _mm("out_proj_dx", dx2, w_out, "nt")[0]
    g["w_out"], wo_half = _mm_tn("out_proj_dw", merged, dx2)
    dmerged = send("w_out", {"w_out": slots(wo_half)}, dmerged)
    dq_, dk_, dv_, dgate, g["ret_g"] = _ret_bwd(dmerged, o_raw, rprev, proj, cos, sin, w["ret_g"], nb, s)
    (dxl, dgl, g["conv_w"], g["conv_b"], g["lru_w_a"], g["lru_b_a"], g["lru_w_i"], g["lru_b_i"], g["lru_lam"]) = _lru_bwd(
        dmerged, proj, w["conv_w"], w["conv_b"], w["lru_w_a"], w["lru_b_a"], w["lru_w_i"], w["lru_b_i"], w["lru_lam"], nb, s)
    dproj = _ew("dproj", lambda *p: (jnp.concatenate(p, axis=1),), [dq_, dk_, dv_, dgate, dxl, dgl], [(3072, bf16)])[0]
    dh0 = _mm("in_proj_dx", dproj, w_in, "nt")[0]
    g["w_in"], wi_half = _mm_tn("in_proj_dw", h0b, dproj)
    dh0 = send("w_in", {"w_in": jnp.transpose(wi_half.reshape(D_MODEL, N_DEV, IN_SHARD), (1, 0, 2))}, dh0)
    dx1, g["mix_g_0"] = _norm_bwd("mix_norm_0_bwd", dh0, x1, w["mix_g"][0], dx2)
    dx0 = ffn_back(dx1, 0, 0)
    g["ffn_w1"], g["ffn_w3"], g["ffn_w2"] = ffn_bufs
    return loss, dx0, g


_WEIGHTS = ["ffn_norm_g", "ffn_w1", "ffn_w3", "ffn_w2", "mix_norm_g", "w_in_even", "w_out_even", "ret_norm_g", "conv_w",
            "conv_b", "lru_w_a", "lru_b_a", "lru_w_i", "lru_b_i", "lru_lambda", "s5_lambda_re", "s5_lambda_im", "s5_log_dt",
            "s5_b_re", "s5_b_im", "s5_c_re", "s5_c_im", "s5_d", "glu_w_a", "glu_w_b", "final_norm_g"]
_BIG = ["ffn_w1", "ffn_w3", "ffn_w2", "w_in_even", "w_out_even", "glu_w_a", "glu_w_b"]
_SMALL_SHARDED = ["ffn_norm_g", "conv_w", "s5_d"]
_SMALL = [n for n in _WEIGHTS if n not in _BIG]


def kernel(x, ffn_norm_g, ffn_w1, ffn_w3, ffn_w2, mix_norm_g, w_in_even, w_out_even, ret_norm_g, conv_w, conv_b, lru_w_a, lru_b_a, lru_w_i, lru_b_i, lru_lambda, s5_lambda_re, s5_lambda_im, s5_log_dt, s5_b_re, s5_b_im, s5_c_re, s5_c_im, s5_d, glu_w_a, glu_w_b, final_norm_g, loss_target, m_ffn_norm_g, m_ffn_w1, m_ffn_w3, m_ffn_w2, m_mix_norm_g, m_w_in_even, m_w_out_even, m_ret_norm_g, m_conv_w, m_conv_b, m_lru_w_a, m_lru_b_a, m_lru_w_i, m_lru_b_i, m_lru_lambda, m_s5_lambda_re, m_s5_lambda_im, m_s5_log_dt, m_s5_b_re, m_s5_b_im, m_s5_c_re, m_s5_c_im, m_s5_d, m_glu_w_a, m_glu_w_b, m_final_norm_g, v_ffn_norm_g, v_ffn_w1, v_ffn_w3, v_ffn_w2, v_mix_norm_g, v_w_in_even, v_w_out_even, v_ret_norm_g, v_conv_w, v_conv_b, v_lru_w_a, v_lru_b_a, v_lru_w_i, v_lru_b_i, v_lru_lambda, v_s5_lambda_re, v_s5_lambda_im, v_s5_log_dt, v_s5_b_re, v_s5_b_im, v_s5_c_re, v_s5_c_im, v_s5_d, v_glu_w_a, v_glu_w_b, v_final_norm_g):
    a = dict(locals())
    nb, s, d = x.shape
    ax, ay, ac = lax.axis_index("x"), lax.axis_index("y"), lax.axis_index("c")
    dev = 4 * ax + 2 * ay + ac
    chip = 2 * ax + ay

    def ffn_shards(l, h):
        extra = FF_PAD - FF_SHARD
        return [jnp.pad(ffn_w1[l, h].astype(bf16), ((0, 0), (0, extra))), jnp.pad(ffn_w3[l, h].astype(bf16), ((0, 0), (0, extra))),
                jnp.pad(ffn_w2[l, h].astype(bf16), ((0, extra), (0, 0)))]

    first = _all_gather("ag_first", ffn_shards(0, 0) + [_pack([ffn_norm_g, conv_w, s5_d])])
    sm = first[3].reshape(N_DEV, -1)
    ffn_g_full = jnp.transpose(sm[:, :512].reshape(N_DEV, 2, 2, 128), (1, 2, 0, 3)).reshape(2, 2, D_MODEL)
    conv_w_full = jnp.transpose(sm[:, 512:768].reshape(N_DEV, 4, 64), (1, 0, 2)).reshape(4, LRU_WIDTH)
    s5_d_full = sm[:, 768:896].reshape(1, D_MODEL)

    ag_src = [None, [w_in_even[0].astype(bf16), w_out_even[0].astype(bf16)], ffn_shards(0, 1),
              ffn_shards(1, 0) + [glu_w_a[0].astype(bf16), glu_w_b[0].astype(bf16)], ffn_shards(1, 1)]
    ag, token = [None], first[0]
    for k, grp in enumerate(ag_src):
        if grp is None:
            continue
        grp[0] = _tie(f"tie_ag_{k}", grp[0], [token])
        lands = [lax.dynamic_update_index_in_dim(lax.empty((N_DEV,) + t.shape, bf16), t, dev, 0) for t in grp]
        ag.append(_xchg_start(f"ag_start_{k}", "gather", grp, lands))
        token = ag[-1]["token"]

    def weights_of(k, after):
        if k == 0:
            return {"ffn": [first[0], _tie("tie_ag_started", first[1], [h["token"] for h in ag[1:]]), first[2]]}
        got = _xchg_wait(f"ag_wait_{k}", ag[k], after)
        if k == 1:
            return {"w_in": jnp.transpose(got[0], (1, 0, 2)).reshape(D_MODEL, N_DEV * IN_SHARD),
                    "w_out": got[1].reshape(D_MODEL, D_MODEL)}
        if k == 3:
            return {"ffn": got[:3], "glu_a": got[3].reshape(D_MODEL, D_MODEL), "glu_b": got[4].reshape(D_MODEL, D_MODEL)}
        return {"ffn": got}

    ffn_lands = [lax.empty((N_DEV - 1, 2, 2) + shp, bf16)
                 for shp in ((D_MODEL, FF_SHARD), (D_MODEL, FF_SHARD), (FF_SHARD, D_MODEL))]
    rs = []

    ffn_names = ("ffn_w1", "ffn_w3", "ffn_w2")

    def send(group, arrays, carry):
        srcs = list(arrays.values())
        if group.startswith("ffn_"):
            which = [ffn_names.index(n) for n in arrays]
            sfx = [(int(group[4]), int(group[5]))] * len(which)
            h = _xchg_start("rs_start_" + group, "scatter", srcs, [ffn_lands[k] for k in which], sfx)
            for k, land in zip(which, h["lands"]):
                ffn_lands[k] = land
        else:
            h = _xchg_start("rs_start_" + group, "scatter", srcs,
                            [lax.empty((N_DEV - 1,) + t.shape[1:], bf16) for t in srcs])
        rs.append((group, list(arrays), h))
        return _tie("tie_" + group, carry, [h["token"]])

    w = {
        "ffn_g": [[ffn_g_full[l, h].reshape(1, D_MODEL) for h in range(2)] for l in range(2)],
        "mix_g": [mix_norm_g[0:1], mix_norm_g[1:2]],
        "ret_g": ret_norm_g, "conv_w": conv_w_full, "conv_b": conv_b,
        "lru_w_a": lru_w_a[0], "lru_b_a": lru_b_a, "lru_w_i": lru_w_i[0], "lru_b_i": lru_b_i, "lru_lam": lru_lambda,
        "s5_lr": s5_lambda_re[0], "s5_li": s5_lambda_im[0], "s5_ldt": s5_log_dt.reshape(S5_GROUPS, 1),
        "s5_bre": jnp.swapaxes(s5_b_re[0], 1, 2), "s5_bim": jnp.swapaxes(s5_b_im[0], 1, 2),
        "s5_cre": s5_c_re[0], "s5_cim": s5_c_im[0], "s5_d": s5_d_full,
        "final_g": final_norm_g.reshape(1, D_MODEL),
    }

    small_grads = {}

    def last_small(g, carry):
        part = _small_partials(g)
        mine = _pack([part[n] for n in _SMALL])
        land = lax.dynamic_update_index_in_dim(lax.empty((N_DEV,) + mine.shape, f32), mine, dev, 0)
        h = _xchg_start("ag_start_small_grads", "gather", [mine], [land])
        small_grads.update(h=h, shapes=[part[n].shape for n in _SMALL])
        return _tie("tie_small_grads", carry, [h["token"]])

    loss_part, dx, g = _local_step(x.reshape(nb * s, d), loss_target.reshape(nb * s, d), w, weights_of, send, last_small,
                                   nb, s)
    loss = lax.psum(loss_part[0, 0], ("x", "y", "c"))
    (gath,) = _xchg_wait("ag_wait_small_grads", small_grads["h"], [dx])
    full = dict(zip(_SMALL, _unpack(_sum8("sum_small_grads", gath), small_grads["shapes"])))
    for n in _SMALL_SHARDED:
        width = a[n].shape[-1]
        full[n] = lax.dynamic_slice_in_dim(full[n], dev * width, width, axis=full[n].ndim - 1)
    shapes = [a[n].shape for n in _SMALL]
    packed = _adamw("adamw_small", _pack([a[n] for n in _SMALL]), _pack([a["m_" + n] for n in _SMALL]),
                    _pack([a["v_" + n] for n in _SMALL]), _pack([full[n] for n in _SMALL]))
    res = {n: vals for n, vals in zip(_SMALL, zip(*[_unpack(p, shapes) for p in packed]))}
    return _finish(a, g, dx, loss, res, packed, rs, ffn_lands, dev, nb, s, d)


def _small_partials(g):
    return {
        "ffn_norm_g": jnp.stack([jnp.stack([g[f"ffn_g_{l}{h}"][0] for h in range(2)]) for l in range(2)]),
        "mix_norm_g": jnp.concatenate([g["mix_g_0"], g["mix_g_1"]], axis=0),
        "ret_norm_g": g["ret_g"], "conv_w": g["conv_w"][None], "conv_b": g["conv_b"],
        "lru_w_a": g["lru_w_a"][None], "lru_b_a": g["lru_b_a"], "lru_w_i": g["lru_w_i"][None], "lru_b_i": g["lru_b_i"],
        "lru_lambda": g["lru_lam"], "s5_lambda_re": g["s5_lr"][None], "s5_lambda_im": g["s5_li"][None],
        "s5_log_dt": g["s5_ldt"].reshape(1, S5_GROUPS),
        "s5_b_re": jnp.swapaxes(g["s5_bre"], 1, 2)[None], "s5_b_im": jnp.swapaxes(g["s5_bim"], 1, 2)[None],
        "s5_c_re": g["s5_cre"][None], "s5_c_im": g["s5_cim"][None], "s5_d": g["s5_d"], "final_norm_g": g["final_g"][0],
    }


def _finish(a, g, dx, loss, res, packed, rs, ffn_lands, dev, nb, s, d):
    landed = {}
    for group, names, h in rs:
        if not group.startswith("ffn_"):
            landed.update(zip(names, _xchg_wait("rs_wait_" + group, h, [dx])))
    own = {n: lax.dynamic_index_in_dim(g[n], dev, axis=0, keepdims=False) for n in ("ffn_w1", "ffn_w3", "ffn_w2")}
    own["w_in"] = lax.dynamic_slice_in_dim(g["w_in"], dev * IN_SHARD, IN_SHARD, axis=1)
    for n in ("w_out", "glu_a", "glu_b"):
        own[n] = lax.dynamic_slice_in_dim(g[n], dev * (D_MODEL // N_DEV), D_MODEL // N_DEV, axis=0)

    def update(n, short):
        res[n] = _adamw("adamw_" + n, a[n], a["m_" + n], a["v_" + n], own[short].reshape(a[n].shape),
                        landed[short].reshape((N_DEV - 1,) + a[n].shape))

    for n, short in zip(_BIG[3:], ("w_in", "w_out", "glu_a", "glu_b")):
        update(n, short)
    after = [dx, packed[0]] + [res[n][0] for n in _BIG[3:]]
    ffn_names = ("ffn_w1", "ffn_w3", "ffn_w2")
    for group, names, h in rs:
        if group.startswith("ffn_"):
            which = [ffn_names.index(n) for n in names]
            for k, land in zip(which, _xchg_wait("rs_wait_" + group, h, after, [ffn_lands[k] for k in which])):
                ffn_lands[k] = land
    landed.update(zip(ffn_names, ffn_lands))
    for n in _BIG[:3]:
        update(n, n)

    out = [loss, dx.reshape(nb, s, d)]
    for k in range(4):
        out += [res[n][k] for n in _WEIGHTS]
    return tuple(out)
```

```python
import functools
import math

import numpy as np
import jax
import jax.numpy as jnp
from jax import lax
from jax.experimental import pallas as pl
from jax.experimental.pallas import tpu as pltpu

f32 = jnp.float32
bf16 = jnp.bfloat16

D_MODEL = 1024
N_DEV = 8
EPS = 1e-6
RET_HEADS = 4
HEAD_DIM = 128
RET_WIDTH = 512
RET_CHUNK = 128
ROPE_BASE = 10000.0
LRU_WIDTH = 512
LRU_BLOCKS = 4
LRU_C = 8.0
S5_GROUP = 16
S5_GROUPS = 64
S5_STATE = 64
S5_CHUNK = 128
S5_BLOCKS = 8
S5_BLOCK_STATES = 512
S5_SUBLANES = 8
D_FF = 2816
FF_SHARD = D_FF // N_DEV
FF_PAD = 384
IN_SHARD = 3072 // N_DEV
ADAM_LR = 0.001
ADAM_B1 = 0.9
ADAM_B2 = 0.999
ADAM_EPS = 1e-08
ADAM_WD = 0.01
ADAM_STEP = 10

VMEM_LIMIT = 56 * 1024 * 1024
VMEM_SPEC = pl.BlockSpec(memory_space=pltpu.VMEM)
ANY_SPEC = pl.BlockSpec(memory_space=pl.ANY)
HBM_SPEC = pl.BlockSpec(memory_space=pltpu.HBM)
SEM_SPEC = pl.BlockSpec(memory_space=pltpu.SEMAPHORE)
SIDE_EFFECT = pltpu.SideEffectType.DATAFLOW_SIDE_EFFECTING
MESH = pl.DeviceIdType.MESH


def _cp(*sem):
    return pltpu.CompilerParams(dimension_semantics=sem, vmem_limit_bytes=VMEM_LIMIT)


def _nn(a, b):
    return jnp.dot(a, b, preferred_element_type=f32)


def _nt(a, b):
    return lax.dot_general(a, b, (((1,), (1,)), ((), ())), preferred_element_type=f32)


def _tn(a, b):
    return lax.dot_general(a, b, (((0,), (0,)), ((), ())), preferred_element_type=f32)


def _rms_fwd(x, g):
    r = lax.rsqrt(jnp.mean(x * x, axis=-1, keepdims=True) + EPS)
    xn = x * r
    return xn * g, xn, r


def _rms_bwd(dh, xn, r, g):
    dxn = dh * g
    dx = r * (dxn - xn * jnp.mean(dxn * xn, axis=-1, keepdims=True))
    dg = jnp.sum(dh * xn, axis=0, keepdims=True)
    return dx, dg


def _shift_dn(v, d, row, fill=0.0):
    return jnp.where(row >= d, pltpu.roll(v, d, 0), fill)


def _shift_up(v, d, row, fill=0.0):
    n = v.shape[0]
    return jnp.where(row < n - d, pltpu.roll(v, n - d, 0), fill)


def _ew(name, fn, ins, outs, tm=512):
    t = ins[0].shape[0]
    n_in = len(ins)

    def body(*refs):
        res = fn(*[r[...] for r in refs[:n_in]])
        for o, v in zip(refs[n_in:], res):
            o[...] = v.astype(o.dtype)

    return pl.pallas_call(
        body, name=name, grid=(t // tm,),
        in_specs=[pl.BlockSpec((tm, a.shape[1]), lambda i: (i, 0)) for a in ins],
        out_specs=[pl.BlockSpec((tm, n), lambda i: (i, 0)) for n, _ in outs],
        out_shape=[jax.ShapeDtypeStruct((t, n), dt) for n, dt in outs],
        compiler_params=_cp("parallel"),
    )(*ins)


def _mm(name, x, w, kind, extras=(), epilogue=None, outs=None, tm=512, tn=512):
    t = x.shape[0]
    n = w.shape[1] if kind == "nn" else w.shape[0]
    tn = min(tn, n)
    outs = outs or [f32]
    n_ex = len(extras)

    def body(x_ref, w_ref, *refs):
        xb = x_ref[...].astype(bf16)
        acc = _nn(xb, w_ref[...]) if kind == "nn" else _nt(xb, w_ref[...])
        res = epilogue(acc, *[r[...] for r in refs[:n_ex]]) if epilogue else (acc,)
        for o, v in zip(refs[n_ex:], res):
            o[...] = v.astype(o.dtype)

    w_spec = (pl.BlockSpec((w.shape[0], tn), lambda i, j: (0, j)) if kind == "nn"
              else pl.BlockSpec((tn, w.shape[1]), lambda i, j: (j, 0)))
    tile = pl.BlockSpec((tm, tn), lambda i, j: (i, j))
    return pl.pallas_call(
        body, name=name, grid=(t // tm, n // tn),
        in_specs=[pl.BlockSpec((tm, x.shape[1]), lambda i, j: (i, 0)), w_spec] + [tile] * n_ex,
        out_specs=[tile] * len(outs),
        out_shape=[jax.ShapeDtypeStruct((t, n), dt) for dt in outs],
        compiler_params=_cp("parallel", "parallel"),
    )(x, w, *extras)


def _mm_tn(name, x, y, tk=1024, tn=512, tt=512):
    t, k = x.shape
    n = y.shape[1]
    tk, tn = min(tk, k), min(tn, n)

    def body(x_ref, y_ref, o_ref, ob_ref):
        @pl.when(pl.program_id(2) == 0)
        def _():
            o_ref[...] = jnp.zeros_like(o_ref)
        o_ref[...] += _tn(x_ref[...].astype(bf16), y_ref[...].astype(bf16))

        @pl.when(pl.program_id(2) == pl.num_programs(2) - 1)
        def _():
            ob_ref[...] = o_ref[...].astype(bf16)

    out = pl.BlockSpec((tk, tn), lambda i, j, s: (i, j))
    return pl.pallas_call(
        body, name=name, grid=(k // tk, n // tn, t // tt),
        in_specs=[pl.BlockSpec((tt, tk), lambda i, j, s: (s, i)), pl.BlockSpec((tt, tn), lambda i, j, s: (s, j))],
        out_specs=[out, out],
        out_shape=[jax.ShapeDtypeStruct((k, n), f32), jax.ShapeDtypeStruct((k, n), bf16)],
        compiler_params=_cp("parallel", "parallel", "arbitrary"),
    )(x, y)


def _norm_fwd(name, x, g, tm=512):
    t, d = x.shape

    def body(x_ref, g_ref, h_ref, hb_ref):
        h, _, _ = _rms_fwd(x_ref[...], g_ref[...])
        h_ref[...] = h
        hb_ref[...] = h.astype(bf16)

    row = pl.BlockSpec((tm, d), lambda i: (i, 0))
    return pl.pallas_call(
        body, name=name, grid=(t // tm,),
        in_specs=[row, pl.BlockSpec((1, d), lambda i: (0, 0))],
        out_specs=[row, row],
        out_shape=[jax.ShapeDtypeStruct((t, d), f32), jax.ShapeDtypeStruct((t, d), bf16)],
        compiler_params=_cp("parallel"),
    )(x, g)


def _norm_bwd(name, dh, x, g, dres, tm=512):
    t, d = x.shape

    def body(dh_ref, x_ref, g_ref, dres_ref, dx_ref, dg_ref):
        gv = g_ref[...]
        _, xn, r = _rms_fwd(x_ref[...], gv)
        dx, dg = _rms_bwd(dh_ref[...], xn, r, gv)
        dx_ref[...] = dres_ref[...] + dx

        @pl.when(pl.program_id(0) == 0)
        def _():
            dg_ref[...] = jnp.zeros_like(dg_ref)
        dg_ref[...] += dg

    row = pl.BlockSpec((tm, d), lambda i: (i, 0))
    vec = pl.BlockSpec((1, d), lambda i: (0, 0))
    return pl.pallas_call(
        body, name=name, grid=(t // tm,),
        in_specs=[row, row, vec, row],
        out_specs=[row, vec],
        out_shape=[jax.ShapeDtypeStruct((t, d), f32), jax.ShapeDtypeStruct((1, d), f32)],
        compiler_params=_cp("arbitrary"),
    )(dh, x, g, dres)


def _final_loss(x, g, target, tm=512):
    t, d = x.shape

    def body(x_ref, g_ref, t_ref, loss_ref, dx_ref, dg_ref):
        gv = g_ref[...]
        y, xn, r = _rms_fwd(x_ref[...], gv)
        err = y - t_ref[...]
        dy = err * (1.0 / d)
        dx, dg = _rms_bwd(dy, xn, r, gv)
        dx_ref[...] = dx

        @pl.when(pl.program_id(0) == 0)
        def _():
            dg_ref[...] = jnp.zeros_like(dg_ref)
            loss_ref[...] = jnp.zeros_like(loss_ref)
        dg_ref[...] += dg
        loss_ref[...] += jnp.full((1, 128), 0.5 / d, f32) * jnp.sum(err * err)

    row = pl.BlockSpec((tm, d), lambda i: (i, 0))
    vec = pl.BlockSpec((1, d), lambda i: (0, 0))
    return pl.pallas_call(
        body, name="final_loss", grid=(t // tm,),
        in_specs=[row, vec, row],
        out_specs=[pl.BlockSpec((1, 128), lambda i: (0, 0)), row, vec],
        out_shape=[jax.ShapeDtypeStruct((1, 128), f32), jax.ShapeDtypeStruct((t, d), f32),
                   jax.ShapeDtypeStruct((1, d), f32)],
        compiler_params=_cp("arbitrary"),
    )(x, g, target)


def _load_ffn_weights(hbm_refs, vmem_refs, sems):
    @pl.when(pl.program_id(0) == 0)
    def _():
        copies = []
        for k, (src, dst) in enumerate(zip(hbm_refs, vmem_refs)):
            for j in range(N_DEV):
                half = pl.ds((j % 2) * FF_PAD, FF_PAD)
                window = dst.at[j // 2, half, :] if k == 2 else dst.at[j // 2, :, half]
                copies.append(pltpu.make_async_copy(src.at[j], window, sems.at[k * N_DEV + j]))
        for cp in copies:
            cp.start()
        for cp in copies:
            cp.wait()


def _ffn_weight_scratch(nj, d, ff):
    return [pltpu.VMEM((nj, d, ff), bf16), pltpu.VMEM((nj, d, ff), bf16), pltpu.VMEM((nj, ff, d), bf16),
            pltpu.SemaphoreType.DMA((3 * N_DEV,))]


def _ffn_fwd(name, x, g, w1, w3, w2, tm=256):
    t, d = x.shape
    nj, ff = N_DEV // 2, 2 * FF_PAD

    def body(x_ref, g_ref, w1_hbm, w3_hbm, w2_hbm, y_ref, a_ref, b_ref, w1_ref, w3_ref, w2_ref, sems):
        _load_ffn_weights((w1_hbm, w3_hbm, w2_hbm), (w1_ref, w3_ref, w2_ref), sems)
        xv = x_ref[...]
        h, _, _ = _rms_fwd(xv, g_ref[...])
        hb = h.astype(bf16)
        acc = jnp.zeros((tm, d), f32)
        for j in range(nj):
            a = _nn(hb, w1_ref[j])
            b = _nn(hb, w3_ref[j])
            a_ref[j] = a.astype(bf16)
            b_ref[j] = b.astype(bf16)
            u = (a * jax.nn.sigmoid(a) * b).astype(bf16)
            acc = acc + _nn(u, w2_ref[j])
        y_ref[...] = xv + 0.5 * acc

    row = pl.BlockSpec((tm, d), lambda i: (i, 0))
    mid = pl.BlockSpec((nj, tm, ff), lambda i: (0, i, 0))
    return pl.pallas_call(
        body, name=name, grid=(t // tm,),
        in_specs=[row, pl.BlockSpec((1, d), lambda i: (0, 0)), ANY_SPEC, ANY_SPEC, ANY_SPEC],
        out_specs=[row, mid, mid],
        out_shape=[jax.ShapeDtypeStruct((t, d), f32), jax.ShapeDtypeStruct((nj, t, ff), bf16),
                   jax.ShapeDtypeStruct((nj, t, ff), bf16)],
        scratch_shapes=_ffn_weight_scratch(nj, d, ff),
        compiler_params=_cp("arbitrary"),
    )(x, g, w1, w3, w2)


def _ffn_dx(name, dy, x, g, w1, w3, w2, a, b, tm=256):
    t, d = x.shape
    nj, ff = N_DEV // 2, 2 * FF_PAD

    def body(dy_ref, x_ref, g_ref, w1_hbm, w3_hbm, w2_hbm, a_ref, b_ref,
             dx_ref, dg_ref, hb_ref, dyh_ref, u_ref, da_ref, db_ref, w1_ref, w3_ref, w2_ref, sems):
        _load_ffn_weights((w1_hbm, w3_hbm, w2_hbm), (w1_ref, w3_ref, w2_ref), sems)
        gv = g_ref[...]
        h, xn, r = _rms_fwd(x_ref[...], gv)
        hb_ref[...] = h.astype(bf16)
        dyv = dy_ref[...]
        dyh = (0.5 * dyv).astype(bf16)
        dyh_ref[...] = dyh
        dh = jnp.zeros((tm, d), f32)
        for j in range(nj):
            av = a_ref[j].astype(f32)
            bv = b_ref[j].astype(f32)
            s = jax.nn.sigmoid(av)
            silu = av * s
            u_ref[j] = (silu * bv).astype(bf16)
            du = _nt(dyh, w2_ref[j])
            dab = (du * bv * (s * (1.0 + av * (1.0 - s)))).astype(bf16)
            dbb = (du * silu).astype(bf16)
            da_ref[j] = dab
            db_ref[j] = dbb
            dh = dh + _nt(dab, w1_ref[j]) + _nt(dbb, w3_ref[j])
        dx, dg = _rms_bwd(dh, xn, r, gv)
        dx_ref[...] = dyv + dx

        @pl.when(pl.program_id(0) == 0)
        def _():
            dg_ref[...] = jnp.zeros_like(dg_ref)
        dg_ref[...] += dg

    row = pl.BlockSpec((tm, d), lambda i: (i, 0))
    vec = pl.BlockSpec((1, d), lambda i: (0, 0))
    mid = pl.BlockSpec((nj, tm, ff), lambda i: (0, i, 0))
    mid_shape = jax.ShapeDtypeStruct((nj, t, ff), bf16)
    return pl.pallas_call(
        body, name=name, grid=(t // tm,),
        in_specs=[row, row, vec, ANY_SPEC, ANY_SPEC, ANY_SPEC, mid, mid],
        out_specs=[row, vec, row, row, mid, mid, mid],
        out_shape=[jax.ShapeDtypeStruct((t, d), f32), jax.ShapeDtypeStruct((1, d), f32),
                   jax.ShapeDtypeStruct((t, d), bf16), jax.ShapeDtypeStruct((t, d), bf16),
                   mid_shape, mid_shape, mid_shape],
        scratch_shapes=_ffn_weight_scratch(nj, d, ff),
        compiler_params=_cp("arbitrary"),
    )(dy, x, g, w1, w3, w2, a, b)


def _ffn_dw_one(name, x, y, buf, l, h, tt=1024):
    return _ffn_dw_calls(name, [(x, y)], [buf], l, h, tt)


def _ffn_dw_calls(name, products, bufs, l, h, tt):
    n = len(products)
    t = products[0][0].shape[-2]
    tt = min(tt, t)
    pairs = N_DEV // 2
    cut_cols = [x.ndim == 2 for x, _ in products]

    def body(*refs):
        ins, outs, accs = refs[:2 * n], refs[3 * n:5 * n], refs[5 * n:]
        s = pl.program_id(1)
        for k in range(n):
            x_ref, y_ref = ins[2 * k], ins[2 * k + 1]
            xv = x_ref[0] if len(x_ref.shape) == 3 else x_ref[...]
            yv = y_ref[0] if len(y_ref.shape) == 3 else y_ref[...]
            prod = _tn(xv, yv)

            @pl.when(s == 0)
            def _():
                accs[k][...] = prod

            @pl.when(s > 0)
            def _():
                accs[k][...] += prod

        @pl.when(s == pl.num_programs(1) - 1)
        def _():
            for k in range(n):
                acc = accs[k][...]
                for e in range(2):
                    lo = e * FF_PAD
                    part = acc[:, lo:lo + FF_SHARD] if cut_cols[k] else acc[lo:lo + FF_SHARD, :]
                    outs[k][e] = part
                    outs[n + k][e] = part.astype(bf16)

    def in_spec(a):
        if a.ndim == 3:
            return pl.BlockSpec((1, tt, a.shape[-1]), lambda p, s: (p, s, 0))
        return pl.BlockSpec((tt, a.shape[-1]), lambda p, s: (s, 0))

    in_specs, out_f32, out_b16, shapes_b16, scratch = [], [], [], [], []
    for (x, y), buf in zip(products, bufs):
        in_specs += [in_spec(x), in_spec(y)]
        k_, n_ = buf.shape[-2:]
        out_f32.append(pl.BlockSpec((2, None, None, k_, n_), lambda p, s: (p, l, h, 0, 0)))
        out_b16.append(pl.BlockSpec((2, k_, n_), lambda p, s: (p, 0, 0)))
        shapes_b16.append(jax.ShapeDtypeStruct((N_DEV, k_, n_), bf16))
        scratch.append(pltpu.VMEM((x.shape[-1], y.shape[-1]), f32))
    flat = [a for xy in products for a in xy]
    return pl.pallas_call(
        body, name=name, grid=(pairs, t // tt),
        in_specs=in_specs + [ANY_SPEC] * n,
        out_specs=out_f32 + out_b16,
        out_shape=[jax.ShapeDtypeStruct(b.shape, b.dtype) for b in bufs] + shapes_b16,
        input_output_aliases={2 * n + k: k for k in range(n)},
        scratch_shapes=scratch,
        compiler_params=_cp("parallel", "arbitrary"),
    )(*flat, *bufs)


_LOG_GAMMA = [float(np.log1p(-np.float32(2.0) ** np.float32(-5.0 - h))) for h in range(RET_HEADS)]


def _ret_consts(h):
    lg = jnp.where(h == 0, _LOG_GAMMA[0], jnp.where(h == 1, _LOG_GAMMA[1],
                   jnp.where(h == 2, _LOG_GAMMA[2], _LOG_GAMMA[3]))).astype(f32)
    c = RET_CHUNK
    r = lax.broadcasted_iota(jnp.int32, (c, c), 0)
    cc = lax.broadcasted_iota(jnp.int32, (c, c), 1)
    decay = jnp.where(r >= cc, jnp.exp(lg * jnp.maximum((r - cc).astype(f32), 0.0)), 0.0)
    pos = lax.broadcasted_iota(jnp.int32, (c, 1), 0).astype(f32)
    kd = jnp.exp(lg * (c - 1.0 - pos))
    qd = jnp.exp(lg * (pos + 1.0))
    gc = jnp.exp(lg * c)
    return decay, kd, qd, gc


def _rope(x, cos, sin):
    return x * cos + pltpu.roll(x, HEAD_DIM // 2, 1) * sin


def _rope_t(g, cos, sin):
    return g * cos + pltpu.roll(g * sin, HEAD_DIM // 2, 1)


def _rope_tables(s):
    half = HEAD_DIM // 2
    inv = ROPE_BASE ** (-jnp.arange(half, dtype=f32) / half)
    ang = jnp.arange(s, dtype=f32)[:, None] * inv[None, :]
    cos, sin = jnp.cos(ang), jnp.sin(ang)
    return jnp.concatenate([cos, cos], axis=1), jnp.concatenate([-sin, sin], axis=1)


def _head_ln(o):
    mu = jnp.mean(o, axis=-1, keepdims=True)
    oc = o - mu
    rs = lax.rsqrt(jnp.mean(oc * oc, axis=-1, keepdims=True) + EPS)
    return oc * rs, rs


def _ret_fwd(proj, cos, sin, ret_g, nb, s):
    c = RET_CHUNK
    nc = s // c
    t = nb * s
    scale = HEAD_DIM ** -0.5

    def body(q_ref, k_ref, v_ref, gate_ref, cos_ref, sin_ref, g_ref, o_ref, rprev_ref, m_ref, r_acc):
        n = pl.program_id(2)

        @pl.when(n == 0)
        def _():
            r_acc[...] = jnp.zeros_like(r_acc)
        decay, kd, qd, gc = _ret_consts(pl.program_id(1))
        cs, sn = cos_ref[...], sin_ref[...]
        q = _rope(q_ref[...], cs, sn)
        k = _rope(k_ref[...], cs, sn) * scale
        vb = v_ref[...].astype(bf16)
        sc = _nt(q.astype(bf16), k.astype(bf16)) * decay
        rv = r_acc[...]
        rprev_ref[0] = rv
        o = _nn(sc.astype(bf16), vb) + _nn((q * qd).astype(bf16), rv.astype(bf16))
        o_ref[...] = o
        r_acc[...] = rv * gc + _tn((k * kd).astype(bf16), vb)
        y, _ = _head_ln(o)
        gate = gate_ref[...]
        m_ref[...] = y * g_ref[...] * (gate * jax.nn.sigmoid(gate))

    def col(off):
        return pl.BlockSpec((c, HEAD_DIM), lambda b, h, n: (b * nc + n, off + h))

    tab = pl.BlockSpec((c, HEAD_DIM), lambda b, h, n: (n, 0))
    return pl.pallas_call(
        body, name="ret_fwd", grid=(nb, RET_HEADS, nc),
        in_specs=[col(0), col(4), col(8), col(12), tab, tab, pl.BlockSpec((1, HEAD_DIM), lambda b, h, n: (0, h))],
        out_specs=[col(0), pl.BlockSpec((1, HEAD_DIM, HEAD_DIM), lambda b, h, n: ((b * RET_HEADS + h) * nc + n, 0, 0)),
                   col(0)],
        out_shape=[jax.ShapeDtypeStruct((t, RET_WIDTH), f32),
                   jax.ShapeDtypeStruct((nb * RET_HEADS * nc, HEAD_DIM, HEAD_DIM), f32),
                   jax.ShapeDtypeStruct((t, RET_WIDTH), f32)],
        scratch_shapes=[pltpu.VMEM((HEAD_DIM, HEAD_DIM), f32)],
        compiler_params=_cp("parallel", "parallel", "arbitrary"),
    )(proj, proj, proj, proj, cos, sin, ret_g)


def _ret_bwd(dmerged, o_raw, rprev, proj, cos, sin, ret_g, nb, s):
    c = RET_CHUNK
    nc = s // c
    t = nb * s
    scale = HEAD_DIM ** -0.5

    def body(dm_ref, o_ref, rprev_ref, q_ref, k_ref, v_ref, gate_ref, cos_ref, sin_ref, g_ref,
             dq_ref, dk_ref, dv_ref, dgate_ref, dg_ref, dr_acc):
        b, n = pl.program_id(1), pl.program_id(2)

        @pl.when(n == 0)
        def _():
            dr_acc[...] = jnp.zeros_like(dr_acc)

        @pl.when((n == 0) & (b == 0))
        def _():
            dg_ref[...] = jnp.zeros_like(dg_ref)
        decay, kd, qd, gc = _ret_consts(pl.program_id(0))
        cs, sn = cos_ref[...], sin_ref[...]
        q = _rope(q_ref[...], cs, sn)
        k = _rope(k_ref[...], cs, sn) * scale
        qb, kb = q.astype(bf16), k.astype(bf16)
        vb = v_ref[...].astype(bf16)
        sc = _nt(qb, kb) * decay
        gv = g_ref[...]
        y, rs = _head_ln(o_ref[...])
        gate = gate_ref[...]
        sg = jax.nn.sigmoid(gate)
        silu = gate * sg
        dm = dm_ref[...]
        dgate_ref[...] = dm * y * gv * (sg * (1.0 + gate * (1.0 - sg)))
        dyl = dm * gv * silu
        dg_ref[...] += jnp.sum(dm * y * silu, axis=0, keepdims=True)
        do = rs * (dyl - jnp.mean(dyl, axis=-1, keepdims=True) - y * jnp.mean(dyl * y, axis=-1, keepdims=True))
        dob = do.astype(bf16)
        rv = rprev_ref[0]
        drn = dr_acc[...]
        drb = drn.astype(bf16)
        ds = (_nt(dob, vb) * decay).astype(bf16)
        kdb = (k * kd).astype(bf16)
        qdb = (q * qd).astype(bf16)
        dq_r = _nn(ds, kb) + _nt(dob, rv.astype(bf16)) * qd
        dk_r = _tn(ds, qb) + _nt(vb, drb) * kd
        dv_ref[...] = _tn(sc.astype(bf16), dob) + _nn(kdb, drb)
        dr_acc[...] = drn * gc + _tn(qdb, dob)
        dq_ref[...] = _rope_t(dq_r, cs, sn)
        dk_ref[...] = _rope_t(dk_r * scale, cs, sn)

    def col(off):
        return pl.BlockSpec((c, HEAD_DIM), lambda h, b, n: (b * nc + nc - 1 - n, off + h))

    tab = pl.BlockSpec((c, HEAD_DIM), lambda h, b, n: (nc - 1 - n, 0))
    gsp = pl.BlockSpec((1, HEAD_DIM), lambda h, b, n: (0, h))
    out_t = jax.ShapeDtypeStruct((t, RET_WIDTH), f32)
    return pl.pallas_call(
        body, name="ret_bwd", grid=(RET_HEADS, nb, nc),
        in_specs=[col(0), col(0),
                  pl.BlockSpec((1, HEAD_DIM, HEAD_DIM), lambda h, b, n: ((b * RET_HEADS + h) * nc + nc - 1 - n, 0, 0)),
                  col(0), col(4), col(8), col(12), tab, tab, gsp],
        out_specs=[col(0), col(0), col(0), col(0), gsp],
        out_shape=[out_t, out_t, out_t, out_t, jax.ShapeDtypeStruct((1, RET_WIDTH), f32)],
        scratch_shapes=[pltpu.VMEM((HEAD_DIM, HEAD_DIM), f32)],
        compiler_params=_cp("parallel", "arbitrary", "arbitrary"),
    )(dmerged, o_raw, rprev, proj, proj, proj, proj, cos, sin, ret_g)


def _neg_expm1(z):
    series = -(z * (1.0 + z * (0.5 + z * (1.0 / 6.0 + z * (1.0 / 24.0)))))
    return jnp.where(z > -0.01, series, 1.0 - jnp.exp(z))


def _lru_gates(xc, pa, pi, lam):
    r = jax.nn.sigmoid(pa)
    i = jax.nn.sigmoid(pi)
    log_a = -LRU_C * r * jax.nn.softplus(-lam)
    a = jnp.exp(log_a)
    bx = jnp.sqrt(_neg_expm1(2.0 * log_a)) * i * xc
    return a, bx


def _scan_fwd(a, b, row):
    d = 1
    while d < a.shape[0]:
        b = a * _shift_dn(b, d, row) + b
        a = a * _shift_dn(a, d, row, 1.0)
        d *= 2
    return b


def _scan_bwd(c, b, row):
    d = 1
    while d < c.shape[0]:
        b = c * _shift_up(b, d, row) + b
        c = c * _shift_up(c, d, row, 1.0)
        d *= 2
    return b


def _conv_fwd(x, cw, cb, row):
    return (cb + cw[3:4] * x + cw[2:3] * _shift_dn(x, 1, row) + cw[1:2] * _shift_dn(x, 2, row)
            + cw[0:1] * _shift_dn(x, 3, row))


def _lru_specs(s, order):
    def im(f):
        return (lambda b, g: f(b, g)) if order == "bg" else (lambda g, b: f(b, g))
    seq = lambda off: pl.BlockSpec((s, 128), im(lambda b, g: (b, off + g)))
    vec = pl.BlockSpec((1, 128), im(lambda b, g: (0, g)))
    cw = pl.BlockSpec((4, 128), im(lambda b, g: (0, g)))
    mat = pl.BlockSpec((1, 128, 128), im(lambda b, g: (g, 0, 0)))
    return seq, vec, cw, mat


def _lru_fwd(proj, conv_w, conv_b, w_a, b_a, w_i, b_i, lam, nb, s):
    def body(x_ref, gt_ref, cw_ref, cb_ref, wa_ref, ba_ref, wi_ref, bi_ref, lam_ref, out_ref):
        row = lax.broadcasted_iota(jnp.int32, (s, 128), 0)
        xc = _conv_fwd(x_ref[...], cw_ref[...], cb_ref[...], row)
        xcb = xc.astype(bf16)
        pa = _nn(xcb, wa_ref[0].astype(bf16)) + ba_ref[...]
        pi = _nn(xcb, wi_ref[0].astype(bf16)) + bi_ref[...]
        a, bx = _lru_gates(xc, pa, pi, lam_ref[...])
        h = _scan_fwd(a, bx, row)
        out_ref[...] = h * jax.nn.gelu(gt_ref[...])

    seq, vec, cw, mat = _lru_specs(s, "bg")
    return pl.pallas_call(
        body, name="lru_fwd", grid=(nb, LRU_BLOCKS),
        in_specs=[seq(16), seq(20), cw, vec, mat, vec, mat, vec, vec],
        out_specs=seq(0),
        out_shape=jax.ShapeDtypeStruct((nb * s, LRU_WIDTH), f32),
        compiler_params=_cp("parallel", "parallel"),
    )(proj, proj, conv_w, conv_b, w_a, b_a, w_i, b_i, lam)


def _lru_bwd(dmerged, proj, conv_w, conv_b, w_a, b_a, w_i, b_i, lam, nb, s):
    def body(dout_ref, x_ref, gt_ref, cw_ref, cb_ref, wa_ref, ba_ref, wi_ref, bi_ref, lam_ref,
             dx_ref, dgt_ref, dcw_ref, dcb_ref, dwa_ref, dba_ref, dwi_ref, dbi_ref, dlam_ref):
        row = lax.broadcasted_iota(jnp.int32, (s, 128), 0)
        x = x_ref[...]
        cwv = cw_ref[...]
        xc = _conv_fwd(x, cwv, cb_ref[...], row)
        xcb = xc.astype(bf16)
        wab, wib = wa_ref[0].astype(bf16), wi_ref[0].astype(bf16)
        pa = _nn(xcb, wab) + ba_ref[...]
        pi = _nn(xcb, wib) + bi_ref[...]
        (a, bx), gates_vjp = jax.vjp(_lru_gates, xc, pa, pi, lam_ref[...])
        h = _scan_fwd(a, bx, row)
        ge, gelu_vjp = jax.vjp(jax.nn.gelu, gt_ref[...])
        dout = dout_ref[...]
        dgt_ref[...] = gelu_vjp(dout * h)[0]
        adj = _scan_bwd(_shift_up(a, 1, row), dout * ge, row)
        dxc, dpa, dpi, dlam = gates_vjp((adj * _shift_dn(h, 1, row), adj))
        dpab, dpib = dpa.astype(bf16), dpi.astype(bf16)
        dxc = dxc + _nt(dpab, wab) + _nt(dpib, wib)
        dx_ref[...] = (cwv[3:4] * dxc + cwv[2:3] * _shift_up(dxc, 1, row) + cwv[1:2] * _shift_up(dxc, 2, row)
                       + cwv[0:1] * _shift_up(dxc, 3, row))

        @pl.when(pl.program_id(1) == 0)
        def _():
            for r in (dcw_ref, dcb_ref, dwa_ref, dba_ref, dwi_ref, dbi_ref, dlam_ref):
                r[...] = jnp.zeros_like(r)
        rsum = lambda v: jnp.sum(v, axis=0, keepdims=True)
        dcw_ref[...] += jnp.concatenate([rsum(dxc * _shift_dn(x, 3, row)), rsum(dxc * _shift_dn(x, 2, row)),
                                         rsum(dxc * _shift_dn(x, 1, row)), rsum(dxc * x)], axis=0)
        dcb_ref[...] += rsum(dxc)
        dwa_ref[0] += _tn(xcb, dpab)
        dwi_ref[0] += _tn(xcb, dpib)
        dba_ref[...] += rsum(dpa)
        dbi_ref[...] += rsum(dpi)
        dlam_ref[...] += dlam

    seq, vec, cw, mat = _lru_specs(s, "gb")
    t = nb * s
    vshape = jax.ShapeDtypeStruct((1, LRU_WIDTH), f32)
    mshape = jax.ShapeDtypeStruct((LRU_BLOCKS, 128, 128), f32)
    return pl.pallas_call(
        body, name="lru_bwd", grid=(LRU_BLOCKS, nb),
        in_specs=[seq(4), seq(16), seq(20), cw, vec, mat, vec, mat, vec, vec],
        out_specs=[seq(0), seq(0), cw, vec, mat, vec, mat, vec, vec],
        out_shape=[jax.ShapeDtypeStruct((t, LRU_WIDTH), f32), jax.ShapeDtypeStruct((t, LRU_WIDTH), f32),
                   jax.ShapeDtypeStruct((4, LRU_WIDTH), f32), vshape, mshape, vshape, mshape, vshape, vshape],
        compiler_params=_cp("parallel", "arbitrary"),
    )(dmerged, proj, proj, conv_w, conv_b, w_a, b_a, w_i, b_i, lam)


def _s5_disc(lr, li, ldt, bre, bim):
    dt = jnp.exp(ldt)
    mag = jnp.exp(lr * dt)
    lbr = mag * jnp.cos(li * dt)
    lbi = mag * jnp.sin(li * dt)
    den = lr * lr + li * li
    nr = lbr - 1.0
    fr = (nr * lr + lbi * li) / den
    fi = (lbi * lr - nr * li) / den
    bbr = fr[:, None, :] * bre - fi[:, None, :] * bim
    bbi = fr[:, None, :] * bim + fi[:, None, :] * bre
    return lbr, lbi, bbr, bbi


def _s5_prep(lr, li, ldt, bre, bim):
    def body(lr_ref, li_ref, ldt_ref, bre_ref, bim_ref, o1, o2, o3, o4):
        o1[...], o2[...], o3[...], o4[...] = _s5_disc(lr_ref[...], li_ref[...], ldt_ref[...], bre_ref[...], bim_ref[...])

    return pl.pallas_call(
        body, name="s5_prep", in_specs=[VMEM_SPEC] * 5, out_specs=[VMEM_SPEC] * 4,
        out_shape=[jax.ShapeDtypeStruct(lr.shape, f32), jax.ShapeDtypeStruct(lr.shape, f32),
                   jax.ShapeDtypeStruct(bre.shape, f32), jax.ShapeDtypeStruct(bre.shape, f32)],
    )(lr, li, ldt, bre, bim)


def _s5_prep_bwd(lr, li, ldt, bre, bim, cts):
    def body(lr_ref, li_ref, ldt_ref, bre_ref, bim_ref, g1, g2, g3, g4, o1, o2, o3, o4, o5):
        _, vjp = jax.vjp(_s5_disc, lr_ref[...], li_ref[...], ldt_ref[...], bre_ref[...], bim_ref[...])
        o1[...], o2[...], o3[...], o4[...], o5[...] = vjp((g1[...], g2[...], g3[...], g4[...]))

    return pl.pallas_call(
        body, name="s5_prep_bwd", in_specs=[VMEM_SPEC] * 9, out_specs=[VMEM_SPEC] * 5,
        out_shape=[jax.ShapeDtypeStruct(v.shape, f32) for v in (lr, li, ldt, bre, bim)],
    )(lr, li, ldt, bre, bim, *cts)


def _cmul(ar, ai, br, bi):
    return ar * br - ai * bi, ar * bi + ai * br


def _s5_pow_table(lr, li, n, row, up):
    ar = jnp.broadcast_to(lr, (n, lr.shape[1]))
    ai = jnp.broadcast_to(li, (n, li.shape[1]))
    shift = _shift_up if up else _shift_dn
    d = 1
    while d < n:
        ar, ai = _cmul(ar, ai, shift(ar, d, row, 1.0), shift(ai, d, row, 0.0))
        d *= 2
    return ar, ai


def _s5_step_factors(lr, li, row, up):
    sub = row & (S5_SUBLANES - 1)
    out, pr, pi, d = [], lr, li, 1
    while d < S5_SUBLANES:
        keep = (sub < S5_SUBLANES - d) if up else (sub >= d)
        out.append((jnp.where(keep, pr, 0.0), jnp.where(keep, pi, 0.0)))
        pr, pi = _cmul(pr, pi, pr, pi)
        d *= 2
    return out


def _s5_scan(br, bi, steps, tab_r, tab_i, cr, ci, up):
    n = br.shape[0]
    for k, (mr, mi) in enumerate(steps):
        shift = n - (1 << k) if up else 1 << k
        tr, ti = _cmul(mr, mi, pltpu.roll(br, shift, 0), pltpu.roll(bi, shift, 0))
        br, bi = br + tr, bi + ti
    groups = list(range(n // S5_SUBLANES))
    out_r, out_i = [None] * len(groups), [None] * len(groups)
    edge = slice(0, 1) if up else slice(S5_SUBLANES - 1, S5_SUBLANES)
    for g in (reversed(groups) if up else groups):
        rows = slice(g * S5_SUBLANES, (g + 1) * S5_SUBLANES)
        tr, ti = _cmul(tab_r, tab_i, cr, ci)
        hr, hi = br[rows] + tr, bi[rows] + ti
        out_r[g], out_i[g] = hr, hi
        cr, ci = hr[edge], hi[edge]
    return jnp.concatenate(out_r, axis=0), jnp.concatenate(out_i, axis=0)


def _s5_specs(s, nc, order):
    def im(f):
        return (lambda b, k: f(b, k)) if order == "bk" else (lambda k, b: f(b, k))
    seq = pl.BlockSpec((s, 128), im(lambda b, k: (b, k)))
    lvec = pl.BlockSpec((1, S5_BLOCK_STATES), im(lambda b, k: (0, k)))
    dvec = pl.BlockSpec((1, 128), im(lambda b, k: (0, k)))
    wmat = pl.BlockSpec((1, 128, S5_BLOCK_STATES), im(lambda b, k: (k, 0, 0)))
    h0 = pl.BlockSpec((1, nc, 2, S5_BLOCK_STATES), im(lambda b, k: (b * S5_BLOCKS + k, 0, 0, 0)))
    return seq, lvec, dvec, wmat, h0


def _s5_fwd(u, lbr, lbi, wbr, wbi, wcr, wci, dskip, nb, s):
    ln = S5_CHUNK
    nc = s // ln

    def body(u_ref, lr_ref, li_ref, wbr_ref, wbi_ref, wcr_ref, wci_ref, d_ref, yg_ref, y_ref, h0_ref):
        row = lax.broadcasted_iota(jnp.int32, (ln, S5_BLOCK_STATES), 0)
        lr, li = lr_ref[...], li_ref[...]
        pr, pi = _s5_pow_table(lr, li, S5_SUBLANES, row[:S5_SUBLANES], False)
        steps = _s5_step_factors(lr, li, row, False)
        dv = d_ref[...]

        def step(n, carry):
            h0r, h0i = carry
            st = pl.multiple_of(n * ln, ln)
            uc = u_ref[pl.ds(st, ln), :]
            ub = uc.astype(bf16)
            hr, hi = _s5_scan(_nn(ub, wbr_ref[0]), _nn(ub, wbi_ref[0]), steps, pr, pi, h0r, h0i, False)
            h0_ref[0, n, 0:1, :] = h0r
            h0_ref[0, n, 1:2, :] = h0i
            y = _nt(hr.astype(bf16), wcr_ref[0]) - _nt(hi.astype(bf16), wci_ref[0]) + dv * uc
            y_ref[pl.ds(st, ln), :] = y
            yg_ref[pl.ds(st, ln), :] = jax.nn.gelu(y).astype(bf16)
            return hr[ln - 1:ln, :], hi[ln - 1:ln, :]

        z = jnp.zeros((1, S5_BLOCK_STATES), f32)
        lax.fori_loop(0, nc, step, (z, z))

    seq, lvec, dvec, wmat, h0 = _s5_specs(s, nc, "bk")
    t = nb * s
    return pl.pallas_call(
        body, name="s5_fwd", grid=(nb, S5_BLOCKS),
        in_specs=[seq, lvec, lvec, wmat, wmat, wmat, wmat, dvec],
        out_specs=[seq, seq, h0],
        out_shape=[jax.ShapeDtypeStruct((t, D_MODEL), bf16), jax.ShapeDtypeStruct((t, D_MODEL), f32),
                   jax.ShapeDtypeStruct((nb * S5_BLOCKS, nc, 2, S5_BLOCK_STATES), f32)],
        compiler_params=_cp("parallel", "parallel"),
    )(u, lbr, lbi, wbr, wbi, wcr, wci, dskip)


def _s5_bwd(dyg, y, u, h0, lbr, lbi, wbr, wbi, wcr, wci, dskip, nb, s):
    ln = S5_CHUNK
    nc = s // ln

    def body(dyg_ref, y_ref, u_ref, h0_ref, lr_ref, li_ref, wbr_ref, wbi_ref, wcr_ref, wci_ref, d_ref,
             du_ref, dlr_ref, dli_ref, dwbr_ref, dwbi_ref, dwcr_ref, dwci_ref, dd_ref):
        @pl.when(pl.program_id(1) == 0)
        def _():
            for r in (dlr_ref, dli_ref, dwbr_ref, dwbi_ref, dwcr_ref, dwci_ref, dd_ref):
                r[...] = jnp.zeros_like(r)
        row = lax.broadcasted_iota(jnp.int32, (ln, S5_BLOCK_STATES), 0)
        lr, li = lr_ref[...], li_ref[...]
        pr, pi = _s5_pow_table(lr, li, S5_SUBLANES, row[:S5_SUBLANES], False)
        qr, qi = _s5_pow_table(lr, -li, S5_SUBLANES, row[:S5_SUBLANES], True)
        steps_dn, steps_up = _s5_step_factors(lr, li, row, False), _s5_step_factors(lr, -li, row, True)
        dv = d_ref[...]
        rsum = lambda v: jnp.sum(v, axis=0, keepdims=True)

        def step(i, carry):
            gnr, gni = carry
            n = nc - 1 - i
            st = pl.multiple_of(n * ln, ln)
            uc = u_ref[pl.ds(st, ln), :]
            ub = uc.astype(bf16)
            h0v = h0_ref[0, n]
            h0r, h0i = h0v[0:1], h0v[1:2]
            hr, hi = _s5_scan(_nn(ub, wbr_ref[0]), _nn(ub, wbi_ref[0]), steps_dn, pr, pi, h0r, h0i, False)
            dy = jax.vjp(jax.nn.gelu, y_ref[pl.ds(st, ln), :])[1](dyg_ref[pl.ds(st, ln), :])[0]
            dyb = dy.astype(bf16)
            dd_ref[...] += rsum(dy * uc)
            gr, gi = _s5_scan(_nn(dyb, wcr_ref[0]), -_nn(dyb, wci_ref[0]), steps_up, qr, qi, gnr, gni, True)
            hpr = jnp.where(row >= 1, pltpu.roll(hr, 1, 0), h0r)
            hpi = jnp.where(row >= 1, pltpu.roll(hi, 1, 0), h0i)
            dlr_ref[...] += rsum(gr * hpr + gi * hpi)
            dli_ref[...] += rsum(gi * hpr - gr * hpi)
            grb, gib = gr.astype(bf16), gi.astype(bf16)
            dwbr_ref[0] += _tn(ub, grb)
            dwbi_ref[0] += _tn(ub, gib)
            dwcr_ref[0] += _tn(dyb, hr.astype(bf16))
            dwci_ref[0] -= _tn(dyb, hi.astype(bf16))
            du_ref[pl.ds(st, ln), :] = _nt(grb, wbr_ref[0]) + _nt(gib, wbi_ref[0]) + dv * dy
            return gr[0:1, :], gi[0:1, :]

        z = jnp.zeros((1, S5_BLOCK_STATES), f32)
        lax.fori_loop(0, nc, step, (z, z))

    seq, lvec, dvec, wmat, h0s = _s5_specs(s, nc, "kb")
    t = nb * s
    lshape = jax.ShapeDtypeStruct((1, S5_BLOCKS * S5_BLOCK_STATES), f32)
    wshape = jax.ShapeDtypeStruct((S5_BLOCKS, 128, S5_BLOCK_STATES), f32)
    return pl.pallas_call(
        body, name="s5_bwd", grid=(S5_BLOCKS, nb),
        in_specs=[seq, seq, seq, h0s, lvec, lvec, wmat, wmat, wmat, wmat, dvec],
        out_specs=[seq, lvec, lvec, wmat, wmat, wmat, wmat, dvec],
        out_shape=[jax.ShapeDtypeStruct((t, D_MODEL), f32), lshape, lshape, wshape, wshape, wshape, wshape,
                   jax.ShapeDtypeStruct((1, D_MODEL), f32)],
        compiler_params=_cp("parallel", "arbitrary"),
    )(dyg, y, u, h0, lbr, lbi, wbr, wbi, wcr, wci, dskip)


def _blockdiag(w):
    w4 = w.reshape(S5_BLOCKS, 8, S5_GROUP, S5_STATE)
    same_group = jnp.eye(8, dtype=bool)[None, :, None, :, None]
    return jnp.where(same_group, w4[:, :, :, None, :], 0.0).reshape(S5_BLOCKS, 128, S5_BLOCK_STATES)


def _blockdiag_t(dw):
    d5 = dw.reshape(S5_BLOCKS, 8, S5_GROUP, 8, S5_STATE)
    diag = jnp.diagonal(d5, axis1=1, axis2=3)
    return jnp.moveaxis(diag, 3, 1).reshape(S5_GROUPS, S5_GROUP, S5_STATE)


def _glu_fwd(ygb, wa, wb, x, tm=512, tn=512):
    t, d = x.shape

    def body(y_ref, wa_ref, wb_ref, x_ref, o_ref, p_ref, q_ref):
        p = _nn(y_ref[...], wa_ref[...])
        q = _nn(y_ref[...], wb_ref[...])
        p_ref[...] = p
        q_ref[...] = q
        o_ref[...] = x_ref[...] + p * jax.nn.sigmoid(q)

    tile = pl.BlockSpec((tm, tn), lambda i, j: (i, j))
    wsp = pl.BlockSpec((d, tn), lambda i, j: (0, j))
    out = jax.ShapeDtypeStruct((t, d), f32)
    return pl.pallas_call(
        body, name="glu_fwd", grid=(t // tm, d // tn),
        in_specs=[pl.BlockSpec((tm, d), lambda i, j: (i, 0)), wsp, wsp, tile],
        out_specs=[tile, tile, tile], out_shape=[out, out, out],
        compiler_params=_cp("parallel", "parallel"),
    )(ygb, wa, wb, x)


def _place():
    x, y, c = lax.axis_index("x"), lax.axis_index("y"), lax.axis_index("c")
    return x, y, c, [(1 - x, y), (x, 1 - y), (1 - x, 1 - y)]


def _all_gather(name, arrays):
    n = len(arrays)

    def body(*refs):
        ins, outs = refs[:n], refs[n:2 * n]
        send_sems, recv_sems, local_sems = refs[2 * n:]
        x, y, c, chips = _place()
        me, sib = (x, y, c), (x, y, 1 - c)

        def copy(i, k, block, to, src=None):
            dst = outs[i].at[4 * block[0] + 2 * block[1] + block[2]]
            return pltpu.make_async_remote_copy(
                src_ref=dst if src is None else src, dst_ref=dst,
                send_sem=send_sems.at[i * 7 + k], recv_sem=recv_sems.at[i * 7 + k],
                device_id=to, device_id_type=MESH)

        mine = [pltpu.make_async_copy(ins[i], outs[i].at[4 * x + 2 * y + c], local_sems.at[i]) for i in range(n)]
        for m in mine:
            m.start()
        first = []
        for i in range(n):
            first.append(copy(i, 0, me, sib, src=ins[i]))
            first += [copy(i, 1 + j, me, (*chip, c), src=ins[i]) for j, chip in enumerate(chips)]
        for cp in first:
            cp.start()
        passed = []
        for j, chip in enumerate(chips):
            for i in range(n):
                copy(i, 1 + j, (*chip, c), me).wait_recv()
                fwd = copy(i, 4 + j, (*chip, c), sib)
                fwd.start()
                passed.append(fwd)
        for i in range(n):
            copy(i, 0, sib, me).wait_recv()
        for j, chip in enumerate(chips):
            for i in range(n):
                copy(i, 4 + j, (*chip, 1 - c), me).wait_recv()
        for cp in first + passed:
            cp.wait_send()
        for m in mine:
            m.wait()

    return pl.pallas_call(
        body, name=name,
        in_specs=[ANY_SPEC] * n, out_specs=[ANY_SPEC] * n,
        out_shape=[jax.ShapeDtypeStruct((N_DEV,) + a.shape, a.dtype) for a in arrays],
        scratch_shapes=[pltpu.SemaphoreType.DMA((7 * n,)), pltpu.SemaphoreType.DMA((7 * n,)),
                        pltpu.SemaphoreType.DMA((n,))],
    )(*arrays)


def _tie(name, x, deps):
    def body(*refs):
        pass

    return pl.pallas_call(
        body, name=name, in_specs=[ANY_SPEC] * (1 + len(deps)), out_specs=ANY_SPEC,
        out_shape=jax.ShapeDtypeStruct(x.shape, x.dtype), input_output_aliases={0: 0},
    )(x, *deps)


def _xchg_copies(kind, srcs, lands, suffixes, send_sems, recv_sems):
    x, y, c, _ = _place()
    copies = []
    for i, (src, land, sfx) in enumerate(zip(srcs, lands, suffixes)):
        for k in range(N_DEV - 1):
            r = k + 1
            peer = (1 - x if r & 4 else x, 1 - y if r & 2 else y, 1 - c if r & 1 else c)
            if kind == "gather":
                s_ref, d_ref = src, land.at[(4 * x + 2 * y + c,) + sfx]
            else:
                s_ref, d_ref = src.at[4 * peer[0] + 2 * peer[1] + peer[2]], land.at[(k,) + sfx]
            copies.append(pltpu.make_async_remote_copy(
                src_ref=s_ref, dst_ref=d_ref, send_sem=send_sems.at[i * 7 + k], recv_sem=recv_sems.at[i * 7 + k],
                device_id=peer, device_id_type=MESH))
    return copies


def _xchg_start(name, kind, srcs, lands, suffixes=None):
    n = len(srcs)
    suffixes = suffixes or [()] * n

    def body(*refs):
        src, land = refs[:n], refs[n:2 * n]
        send_sems, recv_sems, token = refs[2 * n], refs[2 * n + 1], refs[-1]
        for cp in _xchg_copies(kind, src, land, suffixes, send_sems, recv_sems):
            cp.start()
        token[...] = jnp.zeros_like(token)

    arrays = list(srcs) + list(lands)
    outs = pl.pallas_call(
        body, name=name,
        out_shape=(pltpu.SemaphoreType.DMA((7 * n,)), pltpu.SemaphoreType.DMA((7 * n,)),
                   *[pltpu.HBM(a.shape, a.dtype) for a in arrays], jax.ShapeDtypeStruct((8, 128), f32)),
        in_specs=[HBM_SPEC] * (2 * n),
        out_specs=(SEM_SPEC, SEM_SPEC, *[HBM_SPEC] * (2 * n), VMEM_SPEC),
        input_output_aliases={i: 2 + i for i in range(2 * n)},
        compiler_params=pltpu.CompilerParams(has_side_effects=SIDE_EFFECT),
    )(*[pltpu.with_memory_space_constraint(a, pltpu.HBM) for a in arrays])
    return dict(kind=kind, n=n, suffixes=suffixes, send=outs[0], recv=outs[1], srcs=list(outs[2:2 + n]),
                lands=list(outs[2 + n:2 + 2 * n]), token=outs[-1])


def _xchg_wait(name, h, after, lands=None):
    n = h["n"]
    lands = h["lands"] if lands is None else lands

    def body(*refs):
        src, land = refs[:n], refs[n:2 * n]
        for cp in _xchg_copies(h["kind"], src, land, h["suffixes"], refs[2 * n], refs[2 * n + 1]):
            cp.wait_send()
            cp.wait_recv()

    arrays = h["srcs"] + list(lands)
    outs = pl.pallas_call(
        body, name=name,
        out_shape=tuple(pltpu.HBM(a.shape, a.dtype) for a in arrays),
        in_specs=[HBM_SPEC] * (2 * n) + [SEM_SPEC, SEM_SPEC] + [ANY_SPEC] * len(after),
        out_specs=tuple([HBM_SPEC] * (2 * n)),
        input_output_aliases={i: i for i in range(2 * n)},
        compiler_params=pltpu.CompilerParams(has_side_effects=SIDE_EFFECT),
    )(*arrays, h["send"], h["recv"], *after)
    return list(outs[n:])


def _rows(a):
    return a.reshape(-1, a.shape[-1])


def _row_tile(r):
    for tm in (512, 256, 128, 64, 32, 16, 8):
        if r % tm == 0:
            return tm
    return r


def _sum8(name, gathered):
    _, r, n = gathered.shape
    tm = _row_tile(r)

    def body(g_ref, o_ref):
        acc = g_ref[0]
        for k in range(1, N_DEV):
            acc = acc + g_ref[k]
        o_ref[...] = acc

    return pl.pallas_call(
        body, name=name, grid=(r // tm,),
        in_specs=[pl.BlockSpec((N_DEV, tm, n), lambda i: (0, i, 0))],
        out_specs=pl.BlockSpec((tm, n), lambda i: (i, 0)),
        out_shape=jax.ShapeDtypeStruct((r, n), f32),
        compiler_params=_cp("parallel"),
    )(gathered)


def _adamw(name, w, m, v, own, landed=None):
    shape = w.shape
    w2, m2, v2, o2 = _rows(w), _rows(m), _rows(v), _rows(own)
    r, n = w2.shape
    tm = _row_tile(r)
    c1 = 1.0 - ADAM_B1 ** ADAM_STEP
    c2 = 1.0 - ADAM_B2 ** ADAM_STEP
    extra = [] if landed is None else [landed.reshape(landed.shape[0], r, n)]

    def body(w_ref, m_ref, v_ref, o_ref, *refs):
        g = o_ref[...]
        if extra:
            for k in range(extra[0].shape[0]):
                g = g + refs[0][k].astype(f32)
        g_ref, d_ref, mn_ref, vn_ref = refs[len(extra):]
        mn = ADAM_B1 * m_ref[...] + (1.0 - ADAM_B1) * g
        vn = ADAM_B2 * v_ref[...] + (1.0 - ADAM_B2) * (g * g)
        g_ref[...] = g
        d_ref[...] = -ADAM_LR * ((mn / c1) / (jnp.sqrt(vn / c2) + ADAM_EPS) + ADAM_WD * w_ref[...])
        mn_ref[...] = mn
        vn_ref[...] = vn

    row = pl.BlockSpec((tm, n), lambda i: (i, 0))
    outs = pl.pallas_call(
        body, name=name, grid=(r // tm,),
        in_specs=[row] * 4 + [pl.BlockSpec((e.shape[0], tm, n), lambda i: (0, i, 0)) for e in extra],
        out_specs=[row] * 4, out_shape=[jax.ShapeDtypeStruct((r, n), f32)] * 4,
        compiler_params=_cp("parallel"),
    )(w2, m2, v2, o2, *extra)
    return [o.reshape(shape) for o in outs]


def _pack(arrays):
    flat = jnp.concatenate([a.reshape(-1).astype(f32) for a in arrays])
    pad = (-flat.shape[0]) % (128 * (512 if flat.shape[0] > 128 * 512 else 8))
    return jnp.pad(flat, (0, pad)).reshape(-1, 128)


def _unpack(packed, shapes):
    flat = packed.reshape(-1)
    out, off = [], 0
    for s in shapes:
        n = math.prod(s)
        out.append(flat[off:off + n].reshape(s))
        off += n
    return out


def _local_step(x, target, w, weights_of, send, last_small, nb, s):
    cos, sin = _rope_tables(s)
    g = {}
    ffn_saved = {}
    ffn_bufs = [lax.empty((N_DEV, 2, 2) + shp, f32)
                for shp in ((D_MODEL, FF_SHARD), (D_MODEL, FF_SHARD), (FF_SHARD, D_MODEL))]

    def ffn(xin, l, h, wts):
        y, a, b = _ffn_fwd(f"ffn_fwd_{l}{h}", xin, w["ffn_g"][l][h], *wts)
        ffn_saved[(l, h)] = (xin, a, b, wts)
        return y

    def ffn_back(dy, l, h):
        xin, a, b, wts = ffn_saved[(l, h)]
        dx, dg, hb, dyh, u, da, db = _ffn_dx(f"ffn_dx_{l}{h}", dy, xin, w["ffn_g"][l][h], *wts, a, b)
        g[f"ffn_g_{l}{h}"] = dg
        if (l, h) == (0, 0):
            hb = last_small(g, hb)
        ffn_bufs[0], half = _ffn_dw_one(f"ffn_dw_{l}{h}_w1", hb, da, ffn_bufs[0], l, h)
        hb = send(f"ffn_{l}{h}_w1", {"ffn_w1": half}, hb)
        ffn_bufs[1], half = _ffn_dw_one(f"ffn_dw_{l}{h}_w3", hb, db, ffn_bufs[1], l, h)
        u = send(f"ffn_{l}{h}_w3", {"ffn_w3": half}, u)
        ffn_bufs[2], half = _ffn_dw_one(f"ffn_dw_{l}{h}_w2", u, dyh, ffn_bufs[2], l, h)
        return send(f"ffn_{l}{h}_w2", {"ffn_w2": half}, dx)

    def slots(t):
        return t.reshape(N_DEV, D_MODEL // N_DEV, D_MODEL)

    x1 = ffn(x, 0, 0, weights_of(0, [])["ffn"])
    wg = weights_of(1, [x1])
    w_in, w_out = wg["w_in"], wg["w_out"]
    _, h0b = _norm_fwd("mix_norm_0", x1, w["mix_g"][0])
    proj = _mm("in_proj", h0b, w_in, "nn", tn=768)[0]
    o_raw, rprev, mret = _ret_fwd(proj, cos, sin, w["ret_g"], nb, s)
    lru = _lru_fwd(proj, w["conv_w"], w["conv_b"], w["lru_w_a"], w["lru_b_a"], w["lru_w_i"], w["lru_b_i"], w["lru_lam"], nb, s)
    merged = _ew("merge", lambda a, b: (jnp.concatenate([a, b], axis=1),), [mret, lru], [(D_MODEL, bf16)])[0]
    x2 = _mm("out_proj", merged, w_out, "nn", extras=[x1], epilogue=lambda acc, r: (acc + r,))[0]
    x3 = ffn(x2, 0, 1, weights_of(2, [x2])["ffn"])
    wg = weights_of(3, [x3])
    glu_a, glu_b = wg["glu_a"], wg["glu_b"]
    x4 = ffn(x3, 1, 0, wg["ffn"])
    u, _ = _norm_fwd("mix_norm_1", x4, w["mix_g"][1])
    lbr, lbi, bbr, bbi = _s5_prep(w["s5_lr"], w["s5_li"], w["s5_ldt"], w["s5_bre"], w["s5_bim"])
    lbr_f, lbi_f = lbr.reshape(1, -1), lbi.reshape(1, -1)
    wbr, wbi = _blockdiag(bbr).astype(bf16), _blockdiag(bbi).astype(bf16)
    wcr, wci = _blockdiag(w["s5_cre"]).astype(bf16), _blockdiag(w["s5_cim"]).astype(bf16)
    ygb, ypre, h0s = _s5_fwd(u, lbr_f, lbi_f, wbr, wbi, wcr, wci, w["s5_d"], nb, s)
    x5, gp, gq = _glu_fwd(ygb, glu_a, glu_b, x4)
    x6 = ffn(x5, 1, 1, weights_of(4, [x5])["ffn"])
    loss, dx6, g["final_g"] = _final_loss(x6, w["final_g"], target)

    dx5 = ffn_back(dx6, 1, 1)

    def glu_bwd(d, p, q):
        sg = jax.nn.sigmoid(q)
        return d * sg, d * p * sg * (1.0 - sg)

    dp, dq = _ew("glu_bwd", glu_bwd, [dx5, gp, gq], [(D_MODEL, bf16), (D_MODEL, bf16)])
    dyg = _mm("glu_dy_a", dp, glu_a, "nt")[0]
    dyg = _mm("glu_dy_b", dq, glu_b, "nt", extras=[dyg], epilogue=lambda acc, r: (acc + r,))[0]
    g["glu_a"], ga_half = _mm_tn("glu_dw_a", ygb, dp)
    g["glu_b"], gb_half = _mm_tn("glu_dw_b", ygb, dq)
    dyg = send("glu", {"glu_a": slots(ga_half), "glu_b": slots(gb_half)}, dyg)
    du, dlr, dli, dwbr, dwbi, dwcr, dwci, g["s5_d"] = _s5_bwd(dyg, ypre, u, h0s, lbr_f, lbi_f, wbr, wbi, wcr, wci, w["s5_d"], nb, s)
    g["s5_cre"], g["s5_cim"] = _blockdiag_t(dwcr), _blockdiag_t(dwci)
    g["s5_lr"], g["s5_li"], g["s5_ldt"], g["s5_bre"], g["s5_bim"] = _s5_prep_bwd(
        w["s5_lr"], w["s5_li"], w["s5_ldt"], w["s5_bre"], w["s5_bim"],
        (dlr.reshape(S5_GROUPS, S5_STATE), dli.reshape(S5_GROUPS, S5_STATE), _blockdiag_t(dwbr), _blockdiag_t(dwbi)))
    dx4, g["mix_g_1"] = _norm_bwd("mix_norm_1_bwd", du, x4, w["mix_g"][1], dx5)
    dx3 = ffn_back(dx4, 1, 0)
    dx2 = ffn_back(dx3, 0, 1)
    dmerged = _mm("out_proj_dx", dx2, w_out, "nt")[0]
    g["w_out"], wo_half = _mm_tn("out_proj_dw", merged, dx2)
    dmerged = send("w_out", {"w_out": slots(wo_half)}, dmerged)
    dq_, dk_, dv_, dgate, g["ret_g"] = _ret_bwd(dmerged, o_raw, rprev, proj, cos, sin, w["ret_g"], nb, s)
    (dxl, dgl, g["conv_w"], g["conv_b"], g["lru_w_a"], g["lru_b_a"], g["lru_w_i"], g["lru_b_i"], g["lru_lam"]) = _lru_bwd(
        dmerged, proj, w["conv_w"], w["conv_b"], w["lru_w_a"], w["lru_b_a"], w["lru_w_i"], w["lru_b_i"], w["lru_lam"], nb, s)
    dproj = _ew("dproj", lambda *p: (jnp.concatenate(p, axis=1),), [dq_, dk_, dv_, dgate, dxl, dgl], [(3072, bf16)])[0]
    dh0 = _mm("in_proj_dx", dproj, w_in, "nt")[0]
    g["w_in"], wi_half = _mm_tn("in_proj_dw", h0b, dproj)
    dh0 = send("w_in", {"w_in": jnp.transpose(wi_half.reshape(D_MODEL, N_DEV, IN_SHARD), (1, 0, 2))}, dh0)
    dx1, g["mix_g_0"] = _norm_bwd("mix_norm_0_bwd", dh0, x1, w["mix_g"][0], dx2)
    dx0 = ffn_back(dx1, 0, 0)
    g["ffn_w1"], g["ffn_w3"], g["ffn_w2"] = ffn_bufs
    return loss, dx0, g


_WEIGHTS = ["ffn_norm_g", "ffn_w1", "ffn_w3", "ffn_w2", "mix_norm_g", "w_in_even", "w_out_even", "ret_norm_g", "conv_w",
            "conv_b", "lru_w_a", "lru_b_a", "lru_w_i", "lru_b_i", "lru_lambda", "s5_lambda_re", "s5_lambda_im", "s5_log_dt",
            "s5_b_re", "s5_b_im", "s5_c_re", "s5_c_im", "s5_d", "glu_w_a", "glu_w_b", "final_norm_g"]
_BIG = ["ffn_w1", "ffn_w3", "ffn_w2", "w_in_even", "w_out_even", "glu_w_a", "glu_w_b"]
_SMALL_SHARDED = ["ffn_norm_g", "conv_w", "s5_d"]
_SMALL = [n for n in _WEIGHTS if n not in _BIG]


def kernel(x, ffn_norm_g, ffn_w1, ffn_w3, ffn_w2, mix_norm_g, w_in_even, w_out_even, ret_norm_g, conv_w, conv_b, lru_w_a, lru_b_a, lru_w_i, lru_b_i, lru_lambda, s5_lambda_re, s5_lambda_im, s5_log_dt, s5_b_re, s5_b_im, s5_c_re, s5_c_im, s5_d, glu_w_a, glu_w_b, final_norm_g, loss_target, m_ffn_norm_g, m_ffn_w1, m_ffn_w3, m_ffn_w2, m_mix_norm_g, m_w_in_even, m_w_out_even, m_ret_norm_g, m_conv_w, m_conv_b, m_lru_w_a, m_lru_b_a, m_lru_w_i, m_lru_b_i, m_lru_lambda, m_s5_lambda_re, m_s5_lambda_im, m_s5_log_dt, m_s5_b_re, m_s5_b_im, m_s5_c_re, m_s5_c_im, m_s5_d, m_glu_w_a, m_glu_w_b, m_final_norm_g, v_ffn_norm_g, v_ffn_w1, v_ffn_w3, v_ffn_w2, v_mix_norm_g, v_w_in_even, v_w_out_even, v_ret_norm_g, v_conv_w, v_conv_b, v_lru_w_a, v_lru_b_a, v_lru_w_i, v_lru_b_i, v_lru_lambda, v_s5_lambda_re, v_s5_lambda_im, v_s5_log_dt, v_s5_b_re, v_s5_b_im, v_s5_c_re, v_s5_c_im, v_s5_d, v_glu_w_a, v_glu_w_b, v_final_norm_g):
    a = dict(locals())
    nb, s, d = x.shape
    ax, ay, ac = lax.axis_index("x"), lax.axis_index("y"), lax.axis_index("c")
    dev = 4 * ax + 2 * ay + ac
    chip = 2 * ax + ay

    def ffn_shards(l, h):
        extra = FF_PAD - FF_SHARD
        return [jnp.pad(ffn_w1[l, h].astype(bf16), ((0, 0), (0, extra))), jnp.pad(ffn_w3[l, h].astype(bf16), ((0, 0), (0, extra))),
                jnp.pad(ffn_w2[l, h].astype(bf16), ((0, extra), (0, 0)))]

    first = _all_gather("ag_first", ffn_shards(0, 0) + [_pack([ffn_norm_g, conv_w, s5_d])])
    sm = first[3].reshape(N_DEV, -1)
    ffn_g_full = jnp.transpose(sm[:, :512].reshape(N_DEV, 2, 2, 128), (1, 2, 0, 3)).reshape(2, 2, D_MODEL)
    conv_w_full = jnp.transpose(sm[:, 512:768].reshape(N_DEV, 4, 64), (1, 0, 2)).reshape(4, LRU_WIDTH)
    s5_d_full = sm[:, 768:896].reshape(1, D_MODEL)

    ag_src = [None, [w_in_even[0].astype(bf16), w_out_even[0].astype(bf16)], ffn_shards(0, 1),
              ffn_shards(1, 0) + [glu_w_a[0].astype(bf16), glu_w_b[0].astype(bf16)], ffn_shards(1, 1)]
    ag, token = [None], first[0]
    for k, grp in enumerate(ag_src):
        if grp is None:
            continue
        grp[0] = _tie(f"tie_ag_{k}", grp[0], [token])
        lands = [lax.dynamic_update_index_in_dim(lax.empty((N_DEV,) + t.shape, bf16), t, dev, 0) for t in grp]
        ag.append(_xchg_start(f"ag_start_{k}", "gather", grp, lands))
        token = ag[-1]["token"]

    def weights_of(k, after):
        if k == 0:
            return {"ffn": [first[0], _tie("tie_ag_started", first[1], [h["token"] for h in ag[1:]]), first[2]]}
        got = _xchg_wait(f"ag_wait_{k}", ag[k], after)
        if k == 1:
            return {"w_in": jnp.transpose(got[0], (1, 0, 2)).reshape(D_MODEL, N_DEV * IN_SHARD),
                    "w_out": got[1].reshape(D_MODEL, D_MODEL)}
        if k == 3:
            return {"ffn": got[:3], "glu_a": got[3].reshape(D_MODEL, D_MODEL), "glu_b": got[4].reshape(D_MODEL, D_MODEL)}
        return {"ffn": got}

    ffn_lands = [lax.empty((N_DEV - 1, 2, 2) + shp, bf16)
                 for shp in ((D_MODEL, FF_SHARD), (D_MODEL, FF_SHARD), (FF_SHARD, D_MODEL))]
    rs = []

    ffn_names = ("ffn_w1", "ffn_w3", "ffn_w2")

    def send(group, arrays, carry):
        srcs = list(arrays.values())
        if group.startswith("ffn_"):
            which = [ffn_names.index(n) for n in arrays]
            sfx = [(int(group[4]), int(group[5]))] * len(which)
            h = _xchg_start("rs_start_" + group, "scatter", srcs, [ffn_lands[k] for k in which], sfx)
            for k, land in zip(which, h["lands"]):
                ffn_lands[k] = land
        else:
            h = _xchg_start("rs_start_" + group, "scatter", srcs,
                            [lax.empty((N_DEV - 1,) + t.shape[1:], bf16) for t in srcs])
        rs.append((group, list(arrays), h))
        return _tie("tie_" + group, carry, [h["token"]])

    w = {
        "ffn_g": [[ffn_g_full[l, h].reshape(1, D_MODEL) for h in range(2)] for l in range(2)],
        "mix_g": [mix_norm_g[0:1], mix_norm_g[1:2]],
        "ret_g": ret_norm_g, "conv_w": conv_w_full, "conv_b": conv_b,
        "lru_w_a": lru_w_a[0], "lru_b_a": lru_b_a, "lru_w_i": lru_w_i[0], "lru_b_i": lru_b_i, "lru_lam": lru_lambda,
        "s5_lr": s5_lambda_re[0], "s5_li": s5_lambda_im[0], "s5_ldt": s5_log_dt.reshape(S5_GROUPS, 1),
        "s5_bre": jnp.swapaxes(s5_b_re[0], 1, 2), "s5_bim": jnp.swapaxes(s5_b_im[0], 1, 2),
        "s5_cre": s5_c_re[0], "s5_cim": s5_c_im[0], "s5_d": s5_d_full,
        "final_g": final_norm_g.reshape(1, D_MODEL),
    }

    small_grads = {}

    def last_small(g, carry):
        part = _small_partials(g)
        mine = _pack([part[n] for n in _SMALL])
        land = lax.dynamic_update_index_in_dim(lax.empty((N_DEV,) + mine.shape, f32), mine, dev, 0)
        h = _xchg_start("ag_start_small_grads", "gather", [mine], [land])
        small_grads.update(h=h, shapes=[part[n].shape for n in _SMALL])
        return _tie("tie_small_grads", carry, [h["token"]])

    loss_part, dx, g = _local_step(x.reshape(nb * s, d), loss_target.reshape(nb * s, d), w, weights_of, send, last_small,
                                   nb, s)
    loss = lax.psum(loss_part[0, 0], ("x", "y", "c"))
    (gath,) = _xchg_wait("ag_wait_small_grads", small_grads["h"], [dx])
    full = dict(zip(_SMALL, _unpack(_sum8("sum_small_grads", gath), small_grads["shapes"])))
    for n in _SMALL_SHARDED:
        width = a[n].shape[-1]
        full[n] = lax.dynamic_slice_in_dim(full[n], dev * width, width, axis=full[n].ndim - 1)
    shapes = [a[n].shape for n in _SMALL]
    packed = _adamw("adamw_small", _pack([a[n] for n in _SMALL]), _pack([a["m_" + n] for n in _SMALL]),
                    _pack([a["v_" + n] for n in _SMALL]), _pack([full[n] for n in _SMALL]))
    res = {n: vals for n, vals in zip(_SMALL, zip(*[_unpack(p, shapes) for p in packed]))}
    return _finish(a, g, dx, loss, res, packed, rs, ffn_lands, dev, nb, s, d)


def _small_partials(g):
    return {
        "ffn_norm_g": jnp.stack([jnp.stack([g[f"ffn_g_{l}{h}"][0] for h in range(2)]) for l in range(2)]),
        "mix_norm_g": jnp.concatenate([g["mix_g_0"], g["mix_g_1"]], axis=0),
        "ret_norm_g": g["ret_g"], "conv_w": g["conv_w"][None], "conv_b": g["conv_b"],
        "lru_w_a": g["lru_w_a"][None], "lru_b_a": g["lru_b_a"], "lru_w_i": g["lru_w_i"][None], "lru_b_i": g["lru_b_i"],
        "lru_lambda": g["lru_lam"], "s5_lambda_re": g["s5_lr"][None], "s5_lambda_im": g["s5_li"][None],
        "s5_log_dt": g["s5_ldt"].reshape(1, S5_GROUPS),
        "s5_b_re": jnp.swapaxes(g["s5_bre"], 1, 2)[None], "s5_b_im": jnp.swapaxes(g["s5_bim"], 1, 2)[None],
        "s5_c_re": g["s5_cre"][None], "s5_c_im": g["s5_cim"][None], "s5_d": g["s5_d"], "final_norm_g": g["final_g"][0],
    }


def _finish(a, g, dx, loss, res, packed, rs, ffn_lands, dev, nb, s, d):
    landed = {}
    for group, names, h in rs:
        if not group.startswith("ffn_"):
            landed.update(zip(names, _xchg_wait("rs_wait_" + group, h, [dx])))
    own = {n: lax.dynamic_index_in_dim(g[n], dev, axis=0, keepdims=False) for n in ("ffn_w1", "ffn_w3", "ffn_w2")}
    own["w_in"] = lax.dynamic_slice_in_dim(g["w_in"], dev * IN_SHARD, IN_SHARD, axis=1)
    for n in ("w_out", "glu_a", "glu_b"):
        own[n] = lax.dynamic_slice_in_dim(g[n], dev * (D_MODEL // N_DEV), D_MODEL // N_DEV, axis=0)

    def update(n, short):
        res[n] = _adamw("adamw_" + n, a[n], a["m_" + n], a["v_" + n], own[short].reshape(a[n].shape),
                        landed[short].reshape((N_DEV - 1,) + a[n].shape))

    for n, short in zip(_BIG[3:], ("w_in", "w_out", "glu_a", "glu_b")):
        update(n, short)
    after = [dx, packed[0]] + [res[n][0] for n in _BIG[3:]]
    ffn_names = ("ffn_w1", "ffn_w3", "ffn_w2")
    for group, names, h in rs:
        if group.startswith("ffn_"):
            which = [ffn_names.index(n) for n in names]
            for k, land in zip(which, _xchg_wait("rs_wait_" + group, h, after, [ffn_lands[k] for k in which])):
                ffn_lands[k] = land
    landed.update(zip(ffn_names, ffn_lands))
    for n in _BIG[:3]:
        update(n, n)

    out = [loss, dx.reshape(nb, s, d)]
    for k in range(4):
        out += [res[n][k] for n in _WEIGHTS]
    return tuple(out)
```

```python
import functools
import math

import numpy as np
import jax
import jax.numpy as jnp
from jax import lax
from jax.experimental import pallas as pl
from jax.experimental.pallas import tpu as pltpu

f32 = jnp.float32
bf16 = jnp.bfloat16

D_MODEL = 1024
N_DEV = 8
EPS = 1e-6
RET_HEADS = 4
HEAD_DIM = 128
RET_WIDTH = 512
RET_CHUNK = 128
ROPE_BASE = 10000.0
LRU_WIDTH = 512
LRU_BLOCKS = 4
LRU_C = 8.0
S5_GROUP = 16
S5_GROUPS = 64
S5_STATE = 64
S5_CHUNK = 128
S5_BLOCKS = 8
S5_BLOCK_STATES = 512
SUBLANES = 8
D_FF = 2816
FF_SHARD = D_FF // N_DEV
FF_PAD = 384
IN_SHARD = 3072 // N_DEV
ADAM_LR = 0.001
ADAM_B1 = 0.9
ADAM_B2 = 0.999
ADAM_EPS = 1e-08
ADAM_WD = 0.01
ADAM_STEP = 10

VMEM_LIMIT = 56 * 1024 * 1024
VMEM_SPEC = pl.BlockSpec(memory_space=pltpu.VMEM)
ANY_SPEC = pl.BlockSpec(memory_space=pl.ANY)
HBM_SPEC = pl.BlockSpec(memory_space=pltpu.HBM)
SEM_SPEC = pl.BlockSpec(memory_space=pltpu.SEMAPHORE)
SIDE_EFFECT = pltpu.SideEffectType.DATAFLOW_SIDE_EFFECTING
MESH = pl.DeviceIdType.MESH


def _cp(*sem):
    return pltpu.CompilerParams(dimension_semantics=sem, vmem_limit_bytes=VMEM_LIMIT)


def _nn(a, b):
    return jnp.dot(a, b, preferred_element_type=f32)


def _nt(a, b):
    return lax.dot_general(a, b, (((1,), (1,)), ((), ())), preferred_element_type=f32)


def _tn(a, b):
    return lax.dot_general(a, b, (((0,), (0,)), ((), ())), preferred_element_type=f32)


def _rms_fwd(x, g):
    r = lax.rsqrt(jnp.mean(x * x, axis=-1, keepdims=True) + EPS)
    xn = x * r
    return xn * g, xn, r


def _rms_bwd(dh, xn, r, g):
    dxn = dh * g
    dx = r * (dxn - xn * jnp.mean(dxn * xn, axis=-1, keepdims=True))
    dg = jnp.sum(dh * xn, axis=0, keepdims=True)
    return dx, dg


def _shift_dn(v, d, row, fill=0.0):
    return jnp.where(row >= d, pltpu.roll(v, d, 0), fill)


def _shift_up(v, d, row, fill=0.0):
    n = v.shape[0]
    return jnp.where(row < n - d, pltpu.roll(v, n - d, 0), fill)


def _ew(name, fn, ins, outs, tm=512):
    t = ins[0].shape[0]
    n_in = len(ins)

    def body(*refs):
        res = fn(*[r[...] for r in refs[:n_in]])
        for o, v in zip(refs[n_in:], res):
            o[...] = v.astype(o.dtype)

    return pl.pallas_call(
        body, name=name, grid=(t // tm,),
        in_specs=[pl.BlockSpec((tm, a.shape[1]), lambda i: (i, 0)) for a in ins],
        out_specs=[pl.BlockSpec((tm, n), lambda i: (i, 0)) for n, _ in outs],
        out_shape=[jax.ShapeDtypeStruct((t, n), dt) for n, dt in outs],
        compiler_params=_cp("parallel"),
    )(*ins)


def _mm(name, x, w, kind, extras=(), epilogue=None, outs=None, tm=512, tn=512):
    t = x.shape[0]
    n = w.shape[1] if kind == "nn" else w.shape[0]
    tn = min(tn, n)
    outs = outs or [f32]
    n_ex = len(extras)

    def body(x_ref, w_ref, *refs):
        xb = x_ref[...].astype(bf16)
        acc = _nn(xb, w_ref[...]) if kind == "nn" else _nt(xb, w_ref[...])
        res = epilogue(acc, *[r[...] for r in refs[:n_ex]]) if epilogue else (acc,)
        for o, v in zip(refs[n_ex:], res):
            o[...] = v.astype(o.dtype)

    w_spec = (pl.BlockSpec((w.shape[0], tn), lambda i, j: (0, j)) if kind == "nn"
              else pl.BlockSpec((tn, w.shape[1]), lambda i, j: (j, 0)))
    tile = pl.BlockSpec((tm, tn), lambda i, j: (i, j))
    return pl.pallas_call(
        body, name=name, grid=(t // tm, n // tn),
        in_specs=[pl.BlockSpec((tm, x.shape[1]), lambda i, j: (i, 0)), w_spec] + [tile] * n_ex,
        out_specs=[tile] * len(outs),
        out_shape=[jax.ShapeDtypeStruct((t, n), dt) for dt in outs],
        compiler_params=_cp("parallel", "parallel"),
    )(x, w, *extras)


def _mm_tn(name, x, y, tk=1024, tn=512, tt=512):
    t, k = x.shape
    n = y.shape[1]
    tk, tn = min(tk, k), min(tn, n)

    def body(x_ref, y_ref, o_ref, ob_ref):
        @pl.when(pl.program_id(2) == 0)
        def _():
            o_ref[...] = jnp.zeros_like(o_ref)
        o_ref[...] += _tn(x_ref[...].astype(bf16), y_ref[...].astype(bf16))

        @pl.when(pl.program_id(2) == pl.num_programs(2) - 1)
        def _():
            ob_ref[...] = o_ref[...].astype(bf16)

    out = pl.BlockSpec((tk, tn), lambda i, j, s: (i, j))
    return pl.pallas_call(
        body, name=name, grid=(k // tk, n // tn, t // tt),
        in_specs=[pl.BlockSpec((tt, tk), lambda i, j, s: (s, i)), pl.BlockSpec((tt, tn), lambda i, j, s: (s, j))],
        out_specs=[out, out],
        out_shape=[jax.ShapeDtypeStruct((k, n), f32), jax.ShapeDtypeStruct((k, n), bf16)],
        compiler_params=_cp("parallel", "parallel", "arbitrary"),
    )(x, y)


def _norm_fwd(name, x, g, tm=512):
    t, d = x.shape

    def body(x_ref, g_ref, h_ref, hb_ref):
        h, _, _ = _rms_fwd(x_ref[...], g_ref[...])
        h_ref[...] = h
        hb_ref[...] = h.astype(bf16)

    row = pl.BlockSpec((tm, d), lambda i: (i, 0))
    return pl.pallas_call(
        body, name=name, grid=(t // tm,),
        in_specs=[row, pl.BlockSpec((1, d), lambda i: (0, 0))],
        out_specs=[row, row],
        out_shape=[jax.ShapeDtypeStruct((t, d), f32), jax.ShapeDtypeStruct((t, d), bf16)],
        compiler_params=_cp("parallel"),
    )(x, g)


def _norm_bwd(name, dh, x, g, dres, tm=512):
    t, d = x.shape

    def body(dh_ref, x_ref, g_ref, dres_ref, dx_ref, dg_ref):
        gv = g_ref[...]
        _, xn, r = _rms_fwd(x_ref[...], gv)
        dx, dg = _rms_bwd(dh_ref[...], xn, r, gv)
        dx_ref[...] = dres_ref[...] + dx

        @pl.when(pl.program_id(0) == 0)
        def _():
            dg_ref[...] = jnp.zeros_like(dg_ref)
        dg_ref[...] += dg

    row = pl.BlockSpec((tm, d), lambda i: (i, 0))
    vec = pl.BlockSpec((1, d), lambda i: (0, 0))
    return pl.pallas_call(
        body, name=name, grid=(t // tm,),
        in_specs=[row, row, vec, row],
        out_specs=[row, vec],
        out_shape=[jax.ShapeDtypeStruct((t, d), f32), jax.ShapeDtypeStruct((1, d), f32)],
        compiler_params=_cp("arbitrary"),
    )(dh, x, g, dres)


def _final_loss(x, g, target, tm=512):
    t, d = x.shape

    def body(x_ref, g_ref, t_ref, loss_ref, dx_ref, dg_ref):
        gv = g_ref[...]
        y, xn, r = _rms_fwd(x_ref[...], gv)
        err = y - t_ref[...]
        dy = err * (1.0 / d)
        dx, dg = _rms_bwd(dy, xn, r, gv)
        dx_ref[...] = dx

        @pl.when(pl.program_id(0) == 0)
        def _():
            dg_ref[...] = jnp.zeros_like(dg_ref)
            loss_ref[...] = jnp.zeros_like(loss_ref)
        dg_ref[...] += dg
        loss_ref[...] += jnp.full((1, 128), 0.5 / d, f32) * jnp.sum(err * err)

    row = pl.BlockSpec((tm, d), lambda i: (i, 0))
    vec = pl.BlockSpec((1, d), lambda i: (0, 0))
    return pl.pallas_call(
        body, name="final_loss", grid=(t // tm,),
        in_specs=[row, vec, row],
        out_specs=[pl.BlockSpec((1, 128), lambda i: (0, 0)), row, vec],
        out_shape=[jax.ShapeDtypeStruct((1, 128), f32), jax.ShapeDtypeStruct((t, d), f32),
                   jax.ShapeDtypeStruct((1, d), f32)],
        compiler_params=_cp("arbitrary"),
    )(x, g, target)


def _load_ffn_weights(hbm_refs, vmem_refs, sems):
    @pl.when(pl.program_id(0) == 0)
    def _():
        copies = []
        for k, (src, dst) in enumerate(zip(hbm_refs, vmem_refs)):
            for j in range(N_DEV):
                half = pl.ds((j % 2) * FF_PAD, FF_PAD)
                window = dst.at[j // 2, half, :] if k == 2 else dst.at[j // 2, :, half]
                copies.append(pltpu.make_async_copy(src.at[j], window, sems.at[k * N_DEV + j]))
        for cp in copies:
            cp.start()
        for cp in copies:
            cp.wait()


def _ffn_weight_scratch(nj, d, ff):
    return [pltpu.VMEM((nj, d, ff), bf16), pltpu.VMEM((nj, d, ff), bf16), pltpu.VMEM((nj, ff, d), bf16),
            pltpu.SemaphoreType.DMA((3 * N_DEV,))]


def _ffn_fwd(name, x, g, w1, w3, w2, tm=256):
    t, d = x.shape
    nj, ff = N_DEV // 2, 2 * FF_PAD

    def body(x_ref, g_ref, w1_hbm, w3_hbm, w2_hbm, y_ref, a_ref, b_ref, w1_ref, w3_ref, w2_ref, sems):
        _load_ffn_weights((w1_hbm, w3_hbm, w2_hbm), (w1_ref, w3_ref, w2_ref), sems)
        xv = x_ref[...]
        h, _, _ = _rms_fwd(xv, g_ref[...])
        hb = h.astype(bf16)
        acc = jnp.zeros((tm, d), f32)
        for j in range(nj):
            a = _nn(hb, w1_ref[j])
            b = _nn(hb, w3_ref[j])
            a_ref[j] = a.astype(bf16)
            b_ref[j] = b.astype(bf16)
            u = (a * jax.nn.sigmoid(a) * b).astype(bf16)
            acc = acc + _nn(u, w2_ref[j])
        y_ref[...] = xv + 0.5 * acc

    row = pl.BlockSpec((tm, d), lambda i: (i, 0))
    mid = pl.BlockSpec((nj, tm, ff), lambda i: (0, i, 0))
    return pl.pallas_call(
        body, name=name, grid=(t // tm,),
        in_specs=[row, pl.BlockSpec((1, d), lambda i: (0, 0)), ANY_SPEC, ANY_SPEC, ANY_SPEC],
        out_specs=[row, mid, mid],
        out_shape=[jax.ShapeDtypeStruct((t, d), f32), jax.ShapeDtypeStruct((nj, t, ff), bf16),
                   jax.ShapeDtypeStruct((nj, t, ff), bf16)],
        scratch_shapes=_ffn_weight_scratch(nj, d, ff),
        compiler_params=_cp("arbitrary"),
    )(x, g, w1, w3, w2)


def _ffn_dx(name, dy, x, g, w1, w3, w2, a, b, tm=256):
    t, d = x.shape
    nj, ff = N_DEV // 2, 2 * FF_PAD

    def body(dy_ref, x_ref, g_ref, w1_hbm, w3_hbm, w2_hbm, a_ref, b_ref,
             dx_ref, dg_ref, hb_ref, dyh_ref, u_ref, da_ref, db_ref, w1_ref, w3_ref, w2_ref, sems):
        _load_ffn_weights((w1_hbm, w3_hbm, w2_hbm), (w1_ref, w3_ref, w2_ref), sems)
        gv = g_ref[...]
        h, xn, r = _rms_fwd(x_ref[...], gv)
        hb_ref[...] = h.astype(bf16)
        dyv = dy_ref[...]
        dyh = (0.5 * dyv).astype(bf16)
        dyh_ref[...] = dyh
        dh = jnp.zeros((tm, d), f32)
        for j in range(nj):
            av = a_ref[j].astype(f32)
            bv = b_ref[j].astype(f32)
            s = jax.nn.sigmoid(av)
            silu = av * s
            u_ref[j] = (silu * bv).astype(bf16)
            du = _nt(dyh, w2_ref[j])
            dab = (du * bv * (s * (1.0 + av * (1.0 - s)))).astype(bf16)
            dbb = (du * silu).astype(bf16)
            da_ref[j] = dab
            db_ref[j] = dbb
            dh = dh + _nt(dab, w1_ref[j]) + _nt(dbb, w3_ref[j])
        dx, dg = _rms_bwd(dh, xn, r, gv)
        dx_ref[...] = dyv + dx

        @pl.when(pl.program_id(0) == 0)
        def _():
            dg_ref[...] = jnp.zeros_like(dg_ref)
        dg_ref[...] += dg

    row = pl.BlockSpec((tm, d), lambda i: (i, 0))
    vec = pl.BlockSpec((1, d), lambda i: (0, 0))
    mid = pl.BlockSpec((nj, tm, ff), lambda i: (0, i, 0))
    mid_shape = jax.ShapeDtypeStruct((nj, t, ff), bf16)
    return pl.pallas_call(
        body, name=name, grid=(t // tm,),
        in_specs=[row, row, vec, ANY_SPEC, ANY_SPEC, ANY_SPEC, mid, mid],
        out_specs=[row, vec, row, row, mid, mid, mid],
        out_shape=[jax.ShapeDtypeStruct((t, d), f32), jax.ShapeDtypeStruct((1, d), f32),
                   jax.ShapeDtypeStruct((t, d), bf16), jax.ShapeDtypeStruct((t, d), bf16),
                   mid_shape, mid_shape, mid_shape],
        scratch_shapes=_ffn_weight_scratch(nj, d, ff),
        compiler_params=_cp("arbitrary"),
    )(dy, x, g, w1, w3, w2, a, b)


def _ffn_dw_one(name, x, y, buf, l, h, tt=1024):
    return _ffn_dw_calls(name, [(x, y)], [buf], l, h, tt)


def _ffn_dw_calls(name, products, bufs, l, h, tt):
    n = len(products)
    t = products[0][0].shape[-2]
    tt = min(tt, t)
    pairs = N_DEV // 2
    cut_cols = [x.ndim == 2 for x, _ in products]

    def body(*refs):
        ins, outs, accs = refs[:2 * n], refs[3 * n:5 * n], refs[5 * n:]
        s = pl.program_id(1)
        for k in range(n):
            x_ref, y_ref = ins[2 * k], ins[2 * k + 1]
            xv = x_ref[0] if len(x_ref.shape) == 3 else x_ref[...]
            yv = y_ref[0] if len(y_ref.shape) == 3 else y_ref[...]
            prod = _tn(xv, yv)

            @pl.when(s == 0)
            def _():
                accs[k][...] = prod

            @pl.when(s > 0)
            def _():
                accs[k][...] += prod

        @pl.when(s == pl.num_programs(1) - 1)
        def _():
            for k in range(n):
                acc = accs[k][...]
                for e in range(2):
                    lo = e * FF_PAD
                    part = acc[:, lo:lo + FF_SHARD] if cut_cols[k] else acc[lo:lo + FF_SHARD, :]
                    outs[k][e] = part
                    outs[n + k][e] = part.astype(bf16)

    def in_spec(a):
        if a.ndim == 3:
            return pl.BlockSpec((1, tt, a.shape[-1]), lambda p, s: (p, s, 0))
        return pl.BlockSpec((tt, a.shape[-1]), lambda p, s: (s, 0))

    in_specs, out_f32, out_b16, shapes_b16, scratch = [], [], [], [], []
    for (x, y), buf in zip(products, bufs):
        in_specs += [in_spec(x), in_spec(y)]
        k_, n_ = buf.shape[-2:]
        out_f32.append(pl.BlockSpec((2, None, None, k_, n_), lambda p, s: (p, l, h, 0, 0)))
        out_b16.append(pl.BlockSpec((2, k_, n_), lambda p, s: (p, 0, 0)))
        shapes_b16.append(jax.ShapeDtypeStruct((N_DEV, k_, n_), bf16))
        scratch.append(pltpu.VMEM((x.shape[-1], y.shape[-1]), f32))
    flat = [a for xy in products for a in xy]
    return pl.pallas_call(
        body, name=name, grid=(pairs, t // tt),
        in_specs=in_specs + [ANY_SPEC] * n,
        out_specs=out_f32 + out_b16,
        out_shape=[jax.ShapeDtypeStruct(b.shape, b.dtype) for b in bufs] + shapes_b16,
        input_output_aliases={2 * n + k: k for k in range(n)},
        scratch_shapes=scratch,
        compiler_params=_cp("parallel", "arbitrary"),
    )(*flat, *bufs)


_LOG_GAMMA = [float(np.log1p(-np.float32(2.0) ** np.float32(-5.0 - h))) for h in range(RET_HEADS)]


def _ret_consts(h):
    lg = jnp.where(h == 0, _LOG_GAMMA[0], jnp.where(h == 1, _LOG_GAMMA[1],
                   jnp.where(h == 2, _LOG_GAMMA[2], _LOG_GAMMA[3]))).astype(f32)
    c = RET_CHUNK
    r = lax.broadcasted_iota(jnp.int32, (c, c), 0)
    cc = lax.broadcasted_iota(jnp.int32, (c, c), 1)
    decay = jnp.where(r >= cc, jnp.exp(lg * jnp.maximum((r - cc).astype(f32), 0.0)), 0.0)
    pos = lax.broadcasted_iota(jnp.int32, (c, 1), 0).astype(f32)
    kd = jnp.exp(lg * (c - 1.0 - pos))
    qd = jnp.exp(lg * (pos + 1.0))
    gc = jnp.exp(lg * c)
    return decay, kd, qd, gc


def _rope(x, cos, sin):
    return x * cos + pltpu.roll(x, HEAD_DIM // 2, 1) * sin


def _rope_t(g, cos, sin):
    return g * cos + pltpu.roll(g * sin, HEAD_DIM // 2, 1)


def _rope_tables(s):
    half = HEAD_DIM // 2
    inv = ROPE_BASE ** (-jnp.arange(half, dtype=f32) / half)
    ang = jnp.arange(s, dtype=f32)[:, None] * inv[None, :]
    cos, sin = jnp.cos(ang), jnp.sin(ang)
    return jnp.concatenate([cos, cos], axis=1), jnp.concatenate([-sin, sin], axis=1)


def _head_ln(o):
    mu = jnp.mean(o, axis=-1, keepdims=True)
    oc = o - mu
    rs = lax.rsqrt(jnp.mean(oc * oc, axis=-1, keepdims=True) + EPS)
    return oc * rs, rs


def _ret_fwd(proj, cos, sin, ret_g, nb, s):
    c = RET_CHUNK
    nc = s // c
    t = nb * s
    scale = HEAD_DIM ** -0.5

    def body(q_ref, k_ref, v_ref, gate_ref, cos_ref, sin_ref, g_ref, o_ref, rprev_ref, m_ref, r_acc):
        n = pl.program_id(2)

        @pl.when(n == 0)
        def _():
            r_acc[...] = jnp.zeros_like(r_acc)
        decay, kd, qd, gc = _ret_consts(pl.program_id(1))
        cs, sn = cos_ref[...], sin_ref[...]
        q = _rope(q_ref[...], cs, sn)
        k = _rope(k_ref[...], cs, sn) * scale
        vb = v_ref[...].astype(bf16)
        sc = _nt(q.astype(bf16), k.astype(bf16)) * decay
        rv = r_acc[...]
        rprev_ref[0] = rv
        o = _nn(sc.astype(bf16), vb) + _nn((q * qd).astype(bf16), rv.astype(bf16))
        o_ref[...] = o
        r_acc[...] = rv * gc + _tn((k * kd).astype(bf16), vb)
        y, _ = _head_ln(o)
        gate = gate_ref[...]
        m_ref[...] = y * g_ref[...] * (gate * jax.nn.sigmoid(gate))

    def col(off):
        return pl.BlockSpec((c, HEAD_DIM), lambda b, h, n: (b * nc + n, off + h))

    tab = pl.BlockSpec((c, HEAD_DIM), lambda b, h, n: (n, 0))
    return pl.pallas_call(
        body, name="ret_fwd", grid=(nb, RET_HEADS, nc),
        in_specs=[col(0), col(4), col(8), col(12), tab, tab, pl.BlockSpec((1, HEAD_DIM), lambda b, h, n: (0, h))],
        out_specs=[col(0), pl.BlockSpec((1, HEAD_DIM, HEAD_DIM), lambda b, h, n: ((b * RET_HEADS + h) * nc + n, 0, 0)),
                   col(0)],
        out_shape=[jax.ShapeDtypeStruct((t, RET_WIDTH), f32),
                   jax.ShapeDtypeStruct((nb * RET_HEADS * nc, HEAD_DIM, HEAD_DIM), f32),
                   jax.ShapeDtypeStruct((t, RET_WIDTH), f32)],
        scratch_shapes=[pltpu.VMEM((HEAD_DIM, HEAD_DIM), f32)],
        compiler_params=_cp("parallel", "parallel", "arbitrary"),
    )(proj, proj, proj, proj, cos, sin, ret_g)


def _ret_bwd(dmerged, o_raw, rprev, proj, cos, sin, ret_g, nb, s):
    c = RET_CHUNK
    nc = s // c
    t = nb * s
    scale = HEAD_DIM ** -0.5

    def body(dm_ref, o_ref, rprev_ref, q_ref, k_ref, v_ref, gate_ref, cos_ref, sin_ref, g_ref,
             dq_ref, dk_ref, dv_ref, dgate_ref, dg_ref, dr_acc):
        b, n = pl.program_id(1), pl.program_id(2)

        @pl.when(n == 0)
        def _():
            dr_acc[...] = jnp.zeros_like(dr_acc)

        @pl.when((n == 0) & (b == 0))
        def _():
            dg_ref[...] = jnp.zeros_like(dg_ref)
        decay, kd, qd, gc = _ret_consts(pl.program_id(0))
        cs, sn = cos_ref[...], sin_ref[...]
        q = _rope(q_ref[...], cs, sn)
        k = _rope(k_ref[...], cs, sn) * scale
        qb, kb = q.astype(bf16), k.astype(bf16)
        vb = v_ref[...].astype(bf16)
        sc = _nt(qb, kb) * decay
        gv = g_ref[...]
        y, rs = _head_ln(o_ref[...])
        gate = gate_ref[...]
        sg = jax.nn.sigmoid(gate)
        silu = gate * sg
        dm = dm_ref[...]
        dgate_ref[...] = dm * y * gv * (sg * (1.0 + gate * (1.0 - sg)))
        dyl = dm * gv * silu
        dg_ref[...] += jnp.sum(dm * y * silu, axis=0, keepdims=True)
        do = rs * (dyl - jnp.mean(dyl, axis=-1, keepdims=True) - y * jnp.mean(dyl * y, axis=-1, keepdims=True))
        dob = do.astype(bf16)
        rv = rprev_ref[0]
        drn = dr_acc[...]
        drb = drn.astype(bf16)
        ds = (_nt(dob, vb) * decay).astype(bf16)
        kdb = (k * kd).astype(bf16)
        qdb = (q * qd).astype(bf16)
        dq_r = _nn(ds, kb) + _nt(dob, rv.astype(bf16)) * qd
        dk_r = _tn(ds, qb) + _nt(vb, drb) * kd
        dv_ref[...] = _tn(sc.astype(bf16), dob) + _nn(kdb, drb)
        dr_acc[...] = drn * gc + _tn(qdb, dob)
        dq_ref[...] = _rope_t(dq_r, cs, sn)
        dk_ref[...] = _rope_t(dk_r * scale, cs, sn)

    def col(off):
        return pl.BlockSpec((c, HEAD_DIM), lambda h, b, n: (b * nc + nc - 1 - n, off + h))

    tab = pl.BlockSpec((c, HEAD_DIM), lambda h, b, n: (nc - 1 - n, 0))
    gsp = pl.BlockSpec((1, HEAD_DIM), lambda h, b, n: (0, h))
    out_t = jax.ShapeDtypeStruct((t, RET_WIDTH), f32)
    return pl.pallas_call(
        body, name="ret_bwd", grid=(RET_HEADS, nb, nc),
        in_specs=[col(0), col(0),
                  pl.BlockSpec((1, HEAD_DIM, HEAD_DIM), lambda h, b, n: ((b * RET_HEADS + h) * nc + nc - 1 - n, 0, 0)),
                  col(0), col(4), col(8), col(12), tab, tab, gsp],
        out_specs=[col(0), col(0), col(0), col(0), gsp],
        out_shape=[out_t, out_t, out_t, out_t, jax.ShapeDtypeStruct((1, RET_WIDTH), f32)],
        scratch_shapes=[pltpu.VMEM((HEAD_DIM, HEAD_DIM), f32)],
        compiler_params=_cp("parallel", "arbitrary", "arbitrary"),
    )(dmerged, o_raw, rprev, proj, proj, proj, proj, cos, sin, ret_g)


def _neg_expm1(z):
    series = -(z * (1.0 + z * (0.5 + z * (1.0 / 6.0 + z * (1.0 / 24.0)))))
    return jnp.where(z > -0.01, series, 1.0 - jnp.exp(z))


def _lru_gates(xc, pa, pi, lam):
    r = jax.nn.sigmoid(pa)
    i = jax.nn.sigmoid(pi)
    log_a = -LRU_C * r * jax.nn.softplus(-lam)
    a = jnp.exp(log_a)
    bx = jnp.sqrt(_neg_expm1(2.0 * log_a)) * i * xc
    return a, bx


def _scan_rows(a, b, row, up):
    sub = row[:SUBLANES] & (SUBLANES - 1)
    groups = list(range(a.shape[0] // SUBLANES))
    out = [None] * len(groups)
    edge = slice(0, 1) if up else slice(SUBLANES - 1, SUBLANES)
    carry = jnp.zeros((1, a.shape[1]), f32)
    for g in (reversed(groups) if up else groups):
        rows = slice(g * SUBLANES, (g + 1) * SUBLANES)
        xa, xb = a[rows], b[rows]
        d = 1
        while d < SUBLANES:
            keep = (sub < SUBLANES - d) if up else (sub >= d)
            shift = SUBLANES - d if up else d
            xb = xa * jnp.where(keep, pltpu.roll(xb, shift, 0), 0.0) + xb
            xa = xa * jnp.where(keep, pltpu.roll(xa, shift, 0), 1.0)
            d *= 2
        out[g] = xb + xa * carry
        carry = out[g][edge]
    return jnp.concatenate(out, axis=0)


def _scan_fwd(a, b, row):
    return _scan_rows(a, b, row, False)


def _scan_bwd(c, b, row):
    return _scan_rows(c, b, row, True)


def _conv_fwd(x, cw, cb, row):
    return (cb + cw[3:4] * x + cw[2:3] * _shift_dn(x, 1, row) + cw[1:2] * _shift_dn(x, 2, row)
            + cw[0:1] * _shift_dn(x, 3, row))


def _lru_specs(s, order):
    def im(f):
        return (lambda b, g: f(b, g)) if order == "bg" else (lambda g, b: f(b, g))
    seq = lambda off: pl.BlockSpec((s, 128), im(lambda b, g: (b, off + g)))
    vec = pl.BlockSpec((1, 128), im(lambda b, g: (0, g)))
    cw = pl.BlockSpec((4, 128), im(lambda b, g: (0, g)))
    mat = pl.BlockSpec((1, 128, 128), im(lambda b, g: (g, 0, 0)))
    return seq, vec, cw, mat


def _lru_fwd(proj, conv_w, conv_b, w_a, b_a, w_i, b_i, lam, nb, s):
    def body(x_ref, gt_ref, cw_ref, cb_ref, wa_ref, ba_ref, wi_ref, bi_ref, lam_ref, out_ref):
        row = lax.broadcasted_iota(jnp.int32, (s, 128), 0)
        xc = _conv_fwd(x_ref[...], cw_ref[...], cb_ref[...], row)
        xcb = xc.astype(bf16)
        pa = _nn(xcb, wa_ref[0].astype(bf16)) + ba_ref[...]
        pi = _nn(xcb, wi_ref[0].astype(bf16)) + bi_ref[...]
        a, bx = _lru_gates(xc, pa, pi, lam_ref[...])
        h = _scan_fwd(a, bx, row)
        out_ref[...] = h * jax.nn.gelu(gt_ref[...])

    seq, vec, cw, mat = _lru_specs(s, "bg")
    return pl.pallas_call(
        body, name="lru_fwd", grid=(nb, LRU_BLOCKS),
        in_specs=[seq(16), seq(20), cw, vec, mat, vec, mat, vec, vec],
        out_specs=seq(0),
        out_shape=jax.ShapeDtypeStruct((nb * s, LRU_WIDTH), f32),
        compiler_params=_cp("parallel", "parallel"),
    )(proj, proj, conv_w, conv_b, w_a, b_a, w_i, b_i, lam)


def _lru_bwd(dmerged, proj, conv_w, conv_b, w_a, b_a, w_i, b_i, lam, nb, s):
    def body(dout_ref, x_ref, gt_ref, cw_ref, cb_ref, wa_ref, ba_ref, wi_ref, bi_ref, lam_ref,
             dx_ref, dgt_ref, dcw_ref, dcb_ref, dwa_ref, dba_ref, dwi_ref, dbi_ref, dlam_ref):
        row = lax.broadcasted_iota(jnp.int32, (s, 128), 0)
        x = x_ref[...]
        cwv = cw_ref[...]
        xc = _conv_fwd(x, cwv, cb_ref[...], row)
        xcb = xc.astype(bf16)
        wab, wib = wa_ref[0].astype(bf16), wi_ref[0].astype(bf16)
        pa = _nn(xcb, wab) + ba_ref[...]
        pi = _nn(xcb, wib) + bi_ref[...]
        (a, bx), gates_vjp = jax.vjp(_lru_gates, xc, pa, pi, lam_ref[...])
        h = _scan_fwd(a, bx, row)
        ge, gelu_vjp = jax.vjp(jax.nn.gelu, gt_ref[...])
        dout = dout_ref[...]
        dgt_ref[...] = gelu_vjp(dout * h)[0]
        adj = _scan_bwd(_shift_up(a, 1, row), dout * ge, row)
        dxc, dpa, dpi, dlam = gates_vjp((adj * _shift_dn(h, 1, row), adj))
        dpab, dpib = dpa.astype(bf16), dpi.astype(bf16)
        dxc = dxc + _nt(dpab, wab) + _nt(dpib, wib)
        dx_ref[...] = (cwv[3:4] * dxc + cwv[2:3] * _shift_up(dxc, 1, row) + cwv[1:2] * _shift_up(dxc, 2, row)
                       + cwv[0:1] * _shift_up(dxc, 3, row))

        @pl.when(pl.program_id(1) == 0)
        def _():
            for r in (dcw_ref, dcb_ref, dwa_ref, dba_ref, dwi_ref, dbi_ref, dlam_ref):
                r[...] = jnp.zeros_like(r)
        rsum = lambda v: jnp.sum(v, axis=0, keepdims=True)
        dcw_ref[...] += jnp.concatenate([rsum(dxc * _shift_dn(x, 3, row)), rsum(dxc * _shift_dn(x, 2, row)),
                                         rsum(dxc * _shift_dn(x, 1, row)), rsum(dxc * x)], axis=0)
        dcb_ref[...] += rsum(dxc)
        dwa_ref[0] += _tn(xcb, dpab)
        dwi_ref[0] += _tn(xcb, dpib)
        dba_ref[...] += rsum(dpa)
        dbi_ref[...] += rsum(dpi)
        dlam_ref[...] += dlam

    seq, vec, cw, mat = _lru_specs(s, "gb")
    t = nb * s
    vshape = jax.ShapeDtypeStruct((1, LRU_WIDTH), f32)
    mshape = jax.ShapeDtypeStruct((LRU_BLOCKS, 128, 128), f32)
    return pl.pallas_call(
        body, name="lru_bwd", grid=(LRU_BLOCKS, nb),
        in_specs=[seq(4), seq(16), seq(20), cw, vec, mat, vec, mat, vec, vec],
        out_specs=[seq(0), seq(0), cw, vec, mat, vec, mat, vec, vec],
        out_shape=[jax.ShapeDtypeStruct((t, LRU_WIDTH), f32), jax.ShapeDtypeStruct((t, LRU_WIDTH), f32),
                   jax.ShapeDtypeStruct((4, LRU_WIDTH), f32), vshape, mshape, vshape, mshape, vshape, vshape],
        compiler_params=_cp("parallel", "arbitrary"),
    )(dmerged, proj, proj, conv_w, conv_b, w_a, b_a, w_i, b_i, lam)


def _s5_disc(lr, li, ldt, bre, bim):
    dt = jnp.exp(ldt)
    mag = jnp.exp(lr * dt)
    lbr = mag * jnp.cos(li * dt)
    lbi = mag * jnp.sin(li * dt)
    den = lr * lr + li * li
    nr = lbr - 1.0
    fr = (nr * lr + lbi * li) / den
    fi = (lbi * lr - nr * li) / den
    bbr = fr[:, None, :] * bre - fi[:, None, :] * bim
    bbi = fr[:, None, :] * bim + fi[:, None, :] * bre
    return lbr, lbi, bbr, bbi


def _s5_prep(lr, li, ldt, bre, bim):
    def body(lr_ref, li_ref, ldt_ref, bre_ref, bim_ref, o1, o2, o3, o4):
        o1[...], o2[...], o3[...], o4[...] = _s5_disc(lr_ref[...], li_ref[...], ldt_ref[...], bre_ref[...], bim_ref[...])

    return pl.pallas_call(
        body, name="s5_prep", in_specs=[VMEM_SPEC] * 5, out_specs=[VMEM_SPEC] * 4,
        out_shape=[jax.ShapeDtypeStruct(lr.shape, f32), jax.ShapeDtypeStruct(lr.shape, f32),
                   jax.ShapeDtypeStruct(bre.shape, f32), jax.ShapeDtypeStruct(bre.shape, f32)],
    )(lr, li, ldt, bre, bim)


def _s5_prep_bwd(lr, li, ldt, bre, bim, cts):
    def body(lr_ref, li_ref, ldt_ref, bre_ref, bim_ref, g1, g2, g3, g4, o1, o2, o3, o4, o5):
        _, vjp = jax.vjp(_s5_disc, lr_ref[...], li_ref[...], ldt_ref[...], bre_ref[...], bim_ref[...])
        o1[...], o2[...], o3[...], o4[...], o5[...] = vjp((g1[...], g2[...], g3[...], g4[...]))

    return pl.pallas_call(
        body, name="s5_prep_bwd", in_specs=[VMEM_SPEC] * 9, out_specs=[VMEM_SPEC] * 5,
        out_shape=[jax.ShapeDtypeStruct(v.shape, f32) for v in (lr, li, ldt, bre, bim)],
    )(lr, li, ldt, bre, bim, *cts)


def _cmul(ar, ai, br, bi):
    return ar * br - ai * bi, ar * bi + ai * br


def _s5_pow_table(lr, li, n, row, up):
    ar = jnp.broadcast_to(lr, (n, lr.shape[1]))
    ai = jnp.broadcast_to(li, (n, li.shape[1]))
    shift = _shift_up if up else _shift_dn
    d = 1
    while d < n:
        ar, ai = _cmul(ar, ai, shift(ar, d, row, 1.0), shift(ai, d, row, 0.0))
        d *= 2
    return ar, ai


def _s5_step_factors(lr, li, row, up):
    sub = row & (SUBLANES - 1)
    out, pr, pi, d = [], lr, li, 1
    while d < SUBLANES:
        keep = (sub < SUBLANES - d) if up else (sub >= d)
        out.append((jnp.where(keep, pr, 0.0), jnp.where(keep, pi, 0.0)))
        pr, pi = _cmul(pr, pi, pr, pi)
        d *= 2
    return out


def _s5_scan(br, bi, steps, tab_r, tab_i, cr, ci, up):
    groups = list(range(br.shape[0] // SUBLANES))
    out_r, out_i = [None] * len(groups), [None] * len(groups)
    edge = slice(0, 1) if up else slice(SUBLANES - 1, SUBLANES)
    for g in (reversed(groups) if up else groups):
        rows = slice(g * SUBLANES, (g + 1) * SUBLANES)
        xr, xi = br[rows], bi[rows]
        for k, (mr, mi) in enumerate(steps):
            shift = SUBLANES - (1 << k) if up else 1 << k
            tr, ti = _cmul(mr, mi, pltpu.roll(xr, shift, 0), pltpu.roll(xi, shift, 0))
            xr, xi = xr + tr, xi + ti
        tr, ti = _cmul(tab_r, tab_i, cr, ci)
        hr, hi = xr + tr, xi + ti
        out_r[g], out_i[g] = hr, hi
        cr, ci = hr[edge], hi[edge]
    return jnp.concatenate(out_r, axis=0), jnp.concatenate(out_i, axis=0)


def _s5_specs(s, nc, order):
    def im(f):
        return (lambda b, k: f(b, k)) if order == "bk" else (lambda k, b: f(b, k))
    seq = pl.BlockSpec((s, 128), im(lambda b, k: (b, k)))
    lvec = pl.BlockSpec((1, S5_BLOCK_STATES), im(lambda b, k: (0, k)))
    dvec = pl.BlockSpec((1, 128), im(lambda b, k: (0, k)))
    wmat = pl.BlockSpec((1, 128, S5_BLOCK_STATES), im(lambda b, k: (k, 0, 0)))
    h0 = pl.BlockSpec((1, nc, 2, S5_BLOCK_STATES), im(lambda b, k: (b * S5_BLOCKS + k, 0, 0, 0)))
    return seq, lvec, dvec, wmat, h0


def _s5_fwd(u, lbr, lbi, wbr, wbi, wcr, wci, dskip, nb, s):
    ln = S5_CHUNK
    nc = s // ln

    def body(u_ref, lr_ref, li_ref, wbr_ref, wbi_ref, wcr_ref, wci_ref, d_ref, yg_ref, y_ref, h0_ref):
        row = lax.broadcasted_iota(jnp.int32, (ln, S5_BLOCK_STATES), 0)
        lr, li = lr_ref[...], li_ref[...]
        pr, pi = _s5_pow_table(lr, li, SUBLANES, row[:SUBLANES], False)
        steps = _s5_step_factors(lr, li, row[:SUBLANES], False)
        dv = d_ref[...]

        def step(n, carry):
            h0r, h0i = carry
            st = pl.multiple_of(n * ln, ln)
            uc = u_ref[pl.ds(st, ln), :]
            ub = uc.astype(bf16)
            hr, hi = _s5_scan(_nn(ub, wbr_ref[0]), _nn(ub, wbi_ref[0]), steps, pr, pi, h0r, h0i, False)
            h0_ref[0, n, 0:1, :] = h0r
            h0_ref[0, n, 1:2, :] = h0i
            y = _nt(hr.astype(bf16), wcr_ref[0]) - _nt(hi.astype(bf16), wci_ref[0]) + dv * uc
            y_ref[pl.ds(st, ln), :] = y
            yg_ref[pl.ds(st, ln), :] = jax.nn.gelu(y).astype(bf16)
            return hr[ln - 1:ln, :], hi[ln - 1:ln, :]

        z = jnp.zeros((1, S5_BLOCK_STATES), f32)
        lax.fori_loop(0, nc, step, (z, z))

    seq, lvec, dvec, wmat, h0 = _s5_specs(s, nc, "bk")
    t = nb * s
    return pl.pallas_call(
        body, name="s5_fwd", grid=(nb, S5_BLOCKS),
        in_specs=[seq, lvec, lvec, wmat, wmat, wmat, wmat, dvec],
        out_specs=[seq, seq, h0],
        out_shape=[jax.ShapeDtypeStruct((t, D_MODEL), bf16), jax.ShapeDtypeStruct((t, D_MODEL), f32),
                   jax.ShapeDtypeStruct((nb * S5_BLOCKS, nc, 2, S5_BLOCK_STATES), f32)],
        compiler_params=_cp("parallel", "parallel"),
    )(u, lbr, lbi, wbr, wbi, wcr, wci, dskip)


def _s5_bwd(dyg, y, u, h0, lbr, lbi, wbr, wbi, wcr, wci, dskip, nb, s):
    ln = S5_CHUNK
    nc = s // ln

    def body(dyg_ref, y_ref, u_ref, h0_ref, lr_ref, li_ref, wbr_ref, wbi_ref, wcr_ref, wci_ref, d_ref,
             du_ref, dlr_ref, dli_ref, dwbr_ref, dwbi_ref, dwcr_ref, dwci_ref, dd_ref):
        @pl.when(pl.program_id(1) == 0)
        def _():
            for r in (dlr_ref, dli_ref, dwbr_ref, dwbi_ref, dwcr_ref, dwci_ref, dd_ref):
                r[...] = jnp.zeros_like(r)
        row = lax.broadcasted_iota(jnp.int32, (ln, S5_BLOCK_STATES), 0)
        lr, li = lr_ref[...], li_ref[...]
        pr, pi = _s5_pow_table(lr, li, SUBLANES, row[:SUBLANES], False)
        qr, qi = _s5_pow_table(lr, -li, SUBLANES, row[:SUBLANES], True)
        row8 = row[:SUBLANES]
        steps_dn, steps_up = _s5_step_factors(lr, li, row8, False), _s5_step_factors(lr, -li, row8, True)
        dv = d_ref[...]
        rsum = lambda v: jnp.sum(v, axis=0, keepdims=True)

        def step(i, carry):
            gnr, gni = carry
            n = nc - 1 - i
            st = pl.multiple_of(n * ln, ln)
            uc = u_ref[pl.ds(st, ln), :]
            ub = uc.astype(bf16)
            h0v = h0_ref[0, n]
            h0r, h0i = h0v[0:1], h0v[1:2]
            hr, hi = _s5_scan(_nn(ub, wbr_ref[0]), _nn(ub, wbi_ref[0]), steps_dn, pr, pi, h0r, h0i, False)
            dy = jax.vjp(jax.nn.gelu, y_ref[pl.ds(st, ln), :])[1](dyg_ref[pl.ds(st, ln), :])[0]
            dyb = dy.astype(bf16)
            dd_ref[...] += rsum(dy * uc)
            gr, gi = _s5_scan(_nn(dyb, wcr_ref[0]), -_nn(dyb, wci_ref[0]), steps_up, qr, qi, gnr, gni, True)
            hpr = jnp.where(row >= 1, pltpu.roll(hr, 1, 0), h0r)
            hpi = jnp.where(row >= 1, pltpu.roll(hi, 1, 0), h0i)
            dlr_ref[...] += rsum(gr * hpr + gi * hpi)
            dli_ref[...] += rsum(gi * hpr - gr * hpi)
            grb, gib = gr.astype(bf16), gi.astype(bf16)
            dwbr_ref[0] += _tn(ub, grb)
            dwbi_ref[0] += _tn(ub, gib)
            dwcr_ref[0] += _tn(dyb, hr.astype(bf16))
            dwci_ref[0] -= _tn(dyb, hi.astype(bf16))
            du_ref[pl.ds(st, ln), :] = _nt(grb, wbr_ref[0]) + _nt(gib, wbi_ref[0]) + dv * dy
            return gr[0:1, :], gi[0:1, :]

        z = jnp.zeros((1, S5_BLOCK_STATES), f32)
        lax.fori_loop(0, nc, step, (z, z))

    seq, lvec, dvec, wmat, h0s = _s5_specs(s, nc, "kb")
    t = nb * s
    lshape = jax.ShapeDtypeStruct((1, S5_BLOCKS * S5_BLOCK_STATES), f32)
    wshape = jax.ShapeDtypeStruct((S5_BLOCKS, 128, S5_BLOCK_STATES), f32)
    return pl.pallas_call(
        body, name="s5_bwd", grid=(S5_BLOCKS, nb),
        in_specs=[seq, seq, seq, h0s, lvec, lvec, wmat, wmat, wmat, wmat, dvec],
        out_specs=[seq, lvec, lvec, wmat, wmat, wmat, wmat, dvec],
        out_shape=[jax.ShapeDtypeStruct((t, D_MODEL), f32), lshape, lshape, wshape, wshape, wshape, wshape,
                   jax.ShapeDtypeStruct((1, D_MODEL), f32)],
        compiler_params=_cp("parallel", "arbitrary"),
    )(dyg, y, u, h0, lbr, lbi, wbr, wbi, wcr, wci, dskip)


def _blockdiag(w):
    w4 = w.reshape(S5_BLOCKS, 8, S5_GROUP, S5_STATE)
    same_group = jnp.eye(8, dtype=bool)[None, :, None, :, None]
    return jnp.where(same_group, w4[:, :, :, None, :], 0.0).reshape(S5_BLOCKS, 128, S5_BLOCK_STATES)


def _blockdiag_t(dw):
    d5 = dw.reshape(S5_BLOCKS, 8, S5_GROUP, 8, S5_STATE)
    diag = jnp.diagonal(d5, axis1=1, axis2=3)
    return jnp.moveaxis(diag, 3, 1).reshape(S5_GROUPS, S5_GROUP, S5_STATE)


def _glu_fwd(ygb, wa, wb, x, tm=512, tn=512):
    t, d = x.shape

    def body(y_ref, wa_ref, wb_ref, x_ref, o_ref, p_ref, q_ref):
        p = _nn(y_ref[...], wa_ref[...])
        q = _nn(y_ref[...], wb_ref[...])
        p_ref[...] = p
        q_ref[...] = q
        o_ref[...] = x_ref[...] + p * jax.nn.sigmoid(q)

    tile = pl.BlockSpec((tm, tn), lambda i, j: (i, j))
    wsp = pl.BlockSpec((d, tn), lambda i, j: (0, j))
    out = jax.ShapeDtypeStruct((t, d), f32)
    return pl.pallas_call(
        body, name="glu_fwd", grid=(t // tm, d // tn),
        in_specs=[pl.BlockSpec((tm, d), lambda i, j: (i, 0)), wsp, wsp, tile],
        out_specs=[tile, tile, tile], out_shape=[out, out, out],
        compiler_params=_cp("parallel", "parallel"),
    )(ygb, wa, wb, x)


def _place():
    x, y, c = lax.axis_index("x"), lax.axis_index("y"), lax.axis_index("c")
    return x, y, c, [(1 - x, y), (x, 1 - y), (1 - x, 1 - y)]


def _all_gather(name, arrays):
    n = len(arrays)

    def body(*refs):
        ins, outs = refs[:n], refs[n:2 * n]
        send_sems, recv_sems, local_sems = refs[2 * n:]
        x, y, c, chips = _place()
        me, sib = (x, y, c), (x, y, 1 - c)

        def copy(i, k, block, to, src=None):
            dst = outs[i].at[4 * block[0] + 2 * block[1] + block[2]]
            return pltpu.make_async_remote_copy(
                src_ref=dst if src is None else src, dst_ref=dst,
                send_sem=send_sems.at[i * 7 + k], recv_sem=recv_sems.at[i * 7 + k],
                device_id=to, device_id_type=MESH)

        mine = [pltpu.make_async_copy(ins[i], outs[i].at[4 * x + 2 * y + c], local_sems.at[i]) for i in range(n)]
        for m in mine:
            m.start()
        first = []
        for i in range(n):
            first.append(copy(i, 0, me, sib, src=ins[i]))
            first += [copy(i, 1 + j, me, (*chip, c), src=ins[i]) for j, chip in enumerate(chips)]
        for cp in first:
            cp.start()
        passed = []
        for j, chip in enumerate(chips):
            for i in range(n):
                copy(i, 1 + j, (*chip, c), me).wait_recv()
                fwd = copy(i, 4 + j, (*chip, c), sib)
                fwd.start()
                passed.append(fwd)
        for i in range(n):
            copy(i, 0, sib, me).wait_recv()
        for j, chip in enumerate(chips):
            for i in range(n):
                copy(i, 4 + j, (*chip, 1 - c), me).wait_recv()
        for cp in first + passed:
            cp.wait_send()
        for m in mine:
            m.wait()

    return pl.pallas_call(
        body, name=name,
        in_specs=[ANY_SPEC] * n, out_specs=[ANY_SPEC] * n,
        out_shape=[jax.ShapeDtypeStruct((N_DEV,) + a.shape, a.dtype) for a in arrays],
        scratch_shapes=[pltpu.SemaphoreType.DMA((7 * n,)), pltpu.SemaphoreType.DMA((7 * n,)),
                        pltpu.SemaphoreType.DMA((n,))],
    )(*arrays)


def _tie(name, x, deps):
    def body(*refs):
        pass

    return pl.pallas_call(
        body, name=name, in_specs=[ANY_SPEC] * (1 + len(deps)), out_specs=ANY_SPEC,
        out_shape=jax.ShapeDtypeStruct(x.shape, x.dtype), input_output_aliases={0: 0},
    )(x, *deps)


def _xchg_copies(kind, srcs, lands, suffixes, send_sems, recv_sems):
    x, y, c, _ = _place()
    copies = []
    for i, (src, land, sfx) in enumerate(zip(srcs, lands, suffixes)):
        for k in range(N_DEV - 1):
            r = k + 1
            peer = (1 - x if r & 4 else x, 1 - y if r & 2 else y, 1 - c if r & 1 else c)
            if kind == "gather":
                s_ref, d_ref = src, land.at[(4 * x + 2 * y + c,) + sfx]
            else:
                s_ref, d_ref = src.at[4 * peer[0] + 2 * peer[1] + peer[2]], land.at[(k,) + sfx]
            copies.append(pltpu.make_async_remote_copy(
                src_ref=s_ref, dst_ref=d_ref, send_sem=send_sems.at[i * 7 + k], recv_sem=recv_sems.at[i * 7 + k],
                device_id=peer, device_id_type=MESH))
    return copies


def _xchg_start(name, kind, srcs, lands, suffixes=None):
    n = len(srcs)
    suffixes = suffixes or [()] * n

    def body(*refs):
        src, land = refs[:n], refs[n:2 * n]
        send_sems, recv_sems, token = refs[2 * n], refs[2 * n + 1], refs[-1]
        for cp in _xchg_copies(kind, src, land, suffixes, send_sems, recv_sems):
            cp.start()
        token[...] = jnp.zeros_like(token)

    arrays = list(srcs) + list(lands)
    outs = pl.pallas_call(
        body, name=name,
        out_shape=(pltpu.SemaphoreType.DMA((7 * n,)), pltpu.SemaphoreType.DMA((7 * n,)),
                   *[pltpu.HBM(a.shape, a.dtype) for a in arrays], jax.ShapeDtypeStruct((8, 128), f32)),
        in_specs=[HBM_SPEC] * (2 * n),
        out_specs=(SEM_SPEC, SEM_SPEC, *[HBM_SPEC] * (2 * n), VMEM_SPEC),
        input_output_aliases={i: 2 + i for i in range(2 * n)},
        compiler_params=pltpu.CompilerParams(has_side_effects=SIDE_EFFECT),
    )(*[pltpu.with_memory_space_constraint(a, pltpu.HBM) for a in arrays])
    return dict(kind=kind, n=n, suffixes=suffixes, send=outs[0], recv=outs[1], srcs=list(outs[2:2 + n]),
                lands=list(outs[2 + n:2 + 2 * n]), token=outs[-1])


def _xchg_wait(name, h, after, lands=None):
    n = h["n"]
    lands = h["lands"] if lands is None else lands

    def body(*refs):
        src, land = refs[:n], refs[n:2 * n]
        for cp in _xchg_copies(h["kind"], src, land, h["suffixes"], refs[2 * n], refs[2 * n + 1]):
            cp.wait_send()
            cp.wait_recv()

    arrays = h["srcs"] + list(lands)
    outs = pl.pallas_call(
        body, name=name,
        out_shape=tuple(pltpu.HBM(a.shape, a.dtype) for a in arrays),
        in_specs=[HBM_SPEC] * (2 * n) + [SEM_SPEC, SEM_SPEC] + [ANY_SPEC] * len(after),
        out_specs=tuple([HBM_SPEC] * (2 * n)),
        input_output_aliases={i: i for i in range(2 * n)},
        compiler_params=pltpu.CompilerParams(has_side_effects=SIDE_EFFECT),
    )(*arrays, h["send"], h["recv"], *after)
    return list(outs[n:])


def _rows(a):
    return a.reshape(-1, a.shape[-1])


def _row_tile(r):
    for tm in (512, 256, 128, 64, 32, 16, 8):
        if r % tm == 0:
            return tm
    return r


def _sum8(name, gathered):
    _, r, n = gathered.shape
    tm = _row_tile(r)

    def body(g_ref, o_ref):
        acc = g_ref[0]
        for k in range(1, N_DEV):
            acc = acc + g_ref[k]
        o_ref[...] = acc

    return pl.pallas_call(
        body, name=name, grid=(r // tm,),
        in_specs=[pl.BlockSpec((N_DEV, tm, n), lambda i: (0, i, 0))],
        out_specs=pl.BlockSpec((tm, n), lambda i: (i, 0)),
        out_shape=jax.ShapeDtypeStruct((r, n), f32),
        compiler_params=_cp("parallel"),
    )(gathered)


def _adamw(name, w, m, v, own, landed=None):
    shape = w.shape
    w2, m2, v2, o2 = _rows(w), _rows(m), _rows(v), _rows(own)
    r, n = w2.shape
    tm = _row_tile(r)
    c1 = 1.0 - ADAM_B1 ** ADAM_STEP
    c2 = 1.0 - ADAM_B2 ** ADAM_STEP
    extra = [] if landed is None else [landed.reshape(landed.shape[0], r, n)]

    def body(w_ref, m_ref, v_ref, o_ref, *refs):
        g = o_ref[...]
        if extra:
            for k in range(extra[0].shape[0]):
                g = g + refs[0][k].astype(f32)
        g_ref, d_ref, mn_ref, vn_ref = refs[len(extra):]
        mn = ADAM_B1 * m_ref[...] + (1.0 - ADAM_B1) * g
        vn = ADAM_B2 * v_ref[...] + (1.0 - ADAM_B2) * (g * g)
        g_ref[...] = g
        d_ref[...] = -ADAM_LR * ((mn / c1) / (jnp.sqrt(vn / c2) + ADAM_EPS) + ADAM_WD * w_ref[...])
        mn_ref[...] = mn
        vn_ref[...] = vn

    row = pl.BlockSpec((tm, n), lambda i: (i, 0))
    outs = pl.pallas_call(
        body, name=name, grid=(r // tm,),
        in_specs=[row] * 4 + [pl.BlockSpec((e.shape[0], tm, n), lambda i: (0, i, 0)) for e in extra],
        out_specs=[row] * 4, out_shape=[jax.ShapeDtypeStruct((r, n), f32)] * 4,
        compiler_params=_cp("parallel"),
    )(w2, m2, v2, o2, *extra)
    return [o.reshape(shape) for o in outs]


def _pack(arrays):
    flat = jnp.concatenate([a.reshape(-1).astype(f32) for a in arrays])
    pad = (-flat.shape[0]) % (128 * (512 if flat.shape[0] > 128 * 512 else 8))
    return jnp.pad(flat, (0, pad)).reshape(-1, 128)


def _unpack(packed, shapes):
    flat = packed.reshape(-1)
    out, off = [], 0
    for s in shapes:
        n = math.prod(s)
        out.append(flat[off:off + n].reshape(s))
        off += n
    return out


def _local_step(x, target, w, weights_of, send, last_small, nb, s):
    cos, sin = _rope_tables(s)
    g = {}
    ffn_saved = {}
    ffn_bufs = [lax.empty((N_DEV, 2, 2) + shp, f32)
                for shp in ((D_MODEL, FF_SHARD), (D_MODEL, FF_SHARD), (FF_SHARD, D_MODEL))]

    def ffn(xin, l, h, wts):
        y, a, b = _ffn_fwd(f"ffn_fwd_{l}{h}", xin, w["ffn_g"][l][h], *wts)
        ffn_saved[(l, h)] = (xin, a, b, wts)
        return y

    def ffn_back(dy, l, h):
        xin, a, b, wts = ffn_saved[(l, h)]
        dx, dg, hb, dyh, u, da, db = _ffn_dx(f"ffn_dx_{l}{h}", dy, xin, w["ffn_g"][l][h], *wts, a, b)
        g[f"ffn_g_{l}{h}"] = dg
        if (l, h) == (0, 0):
            hb = last_small(g, hb)
        ffn_bufs[0], half = _ffn_dw_one(f"ffn_dw_{l}{h}_w1", hb, da, ffn_bufs[0], l, h)
        hb = send(f"ffn_{l}{h}_w1", {"ffn_w1": half}, hb)
        ffn_bufs[1], half = _ffn_dw_one(f"ffn_dw_{l}{h}_w3", hb, db, ffn_bufs[1], l, h)
        u = send(f"ffn_{l}{h}_w3", {"ffn_w3": half}, u)
        ffn_bufs[2], half = _ffn_dw_one(f"ffn_dw_{l}{h}_w2", u, dyh, ffn_bufs[2], l, h)
        return send(f"ffn_{l}{h}_w2", {"ffn_w2": half}, dx)

    def slots(t):
        return t.reshape(N_DEV, D_MODEL // N_DEV, D_MODEL)

    x1 = ffn(x, 0, 0, weights_of(0, [])["ffn"])
    wg = weights_of(1, [x1])
    w_in, w_out = wg["w_in"], wg["w_out"]
    _, h0b = _norm_fwd("mix_norm_0", x1, w["mix_g"][0])
    proj = _mm("in_proj", h0b, w_in, "nn", tn=768)[0]
    o_raw, rprev, mret = _ret_fwd(proj, cos, sin, w["ret_g"], nb, s)
    lru = _lru_fwd(proj, w["conv_w"], w["conv_b"], w["lru_w_a"], w["lru_b_a"], w["lru_w_i"], w["lru_b_i"], w["lru_lam"], nb, s)
    merged = _ew("merge", lambda a, b: (jnp.concatenate([a, b], axis=1),), [mret, lru], [(D_MODEL, bf16)])[0]
    x2 = _mm("out_proj", merged, w_out, "nn", extras=[x1], epilogue=lambda acc, r: (acc + r,))[0]
    x3 = ffn(x2, 0, 1, weights_of(2, [x2])["ffn"])
    wg = weights_of(3, [x3])
    glu_a, glu_b = wg["glu_a"], wg["glu_b"]
    x4 = ffn(x3, 1, 0, wg["ffn"])
    u, _ = _norm_fwd("mix_norm_1", x4, w["mix_g"][1])
    lbr, lbi, bbr, bbi = _s5_prep(w["s5_lr"], w["s5_li"], w["s5_ldt"], w["s5_bre"], w["s5_bim"])
    lbr_f, lbi_f = lbr.reshape(1, -1), lbi.reshape(1, -1)
    wbr, wbi = _blockdiag(bbr).astype(bf16), _blockdiag(bbi).astype(bf16)
    wcr, wci = _blockdiag(w["s5_cre"]).astype(bf16), _blockdiag(w["s5_cim"]).astype(bf16)
    ygb, ypre, h0s = _s5_fwd(u, lbr_f, lbi_f, wbr, wbi, wcr, wci, w["s5_d"], nb, s)
    x5, gp, gq = _glu_fwd(ygb, glu_a, glu_b, x4)
    x6 = ffn(x5, 1, 1, weights_of(4, [x5])["ffn"])
    loss, dx6, g["final_g"] = _final_loss(x6, w["final_g"], target)

    dx5 = ffn_back(dx6, 1, 1)

    def glu_bwd(d, p, q):
        sg = jax.nn.sigmoid(q)
        return d * sg, d * p * sg * (1.0 - sg)

    dp, dq = _ew("glu_bwd", glu_bwd, [dx5, gp, gq], [(D_MODEL, bf16), (D_MODEL, bf16)])
    dyg = _mm("glu_dy_a", dp, glu_a, "nt")[0]
    dyg = _mm("glu_dy_b", dq, glu_b, "nt", extras=[dyg], epilogue=lambda acc, r: (acc + r,))[0]
    g["glu_a"], ga_half = _mm_tn("glu_dw_a", ygb, dp)
    g["glu_b"], gb_half = _mm_tn("glu_dw_b", ygb, dq)
    dyg = send("glu", {"glu_a": slots(ga_half), "glu_b": slots(gb_half)}, dyg)
    du, dlr, dli, dwbr, dwbi, dwcr, dwci, g["s5_d"] = _s5_bwd(dyg, ypre, u, h0s, lbr_f, lbi_f, wbr, wbi, wcr, wci, w["s5_d"], nb, s)
    g["s5_cre"], g["s5_cim"] = _blockdiag_t(dwcr), _blockdiag_t(dwci)
    g["s5_lr"], g["s5_li"], g["s5_ldt"], g["s5_bre"], g["s5_bim"] = _s5_prep_bwd(
        w["s5_lr"], w["s5_li"], w["s5_ldt"], w["s5_bre"], w["s5_bim"],
        (dlr.reshape(S5_GROUPS, S5_STATE), dli.reshape(S5_GROUPS, S5_STATE), _blockdiag_t(dwbr), _blockdiag_t(dwbi)))
    dx4, g["mix_g_1"] = _norm_bwd("mix_norm_1_bwd", du, x4, w["mix_g"][1], dx5)
    dx3 = ffn_back(dx4, 1, 0)
    dx2 = ffn_back(dx3, 0, 1)
    dmerged = _mm("out_proj_dx", dx2, w_out, "nt")[0]
    g["w_out"], wo_half = _mm_tn("out_proj_dw", merged, dx2)
    dmerged = send("w_out", {"w_out": slots(wo_half)}, dmerged)
    dq_, dk_, dv_, dgate, g["ret_g"] = _ret_bwd(dmerged, o_raw, rprev, proj, cos, sin, w["ret_g"], nb, s)
    (dxl, dgl, g["conv_w"], g["conv_b"], g["lru_w_a"], g["lru_b_a"], g["lru_w_i"], g["lru_b_i"], g["lru_lam"]) = _lru_bwd(
        dmerged, proj, w["conv_w"], w["conv_b"], w["lru_w_a"], w["lru_b_a"], w["lru_w_i"], w["lru_b_i"], w["lru_lam"], nb, s)
    dproj = _ew("dproj", lambda *p: (jnp.concatenate(p, axis=1),), [dq_, dk_, dv_, dgate, dxl, dgl], [(3072, bf16)])[0]
    dh0 = _mm("in_proj_dx", dproj, w_in, "nt")[0]
    g["w_in"], wi_half = _mm_tn("in_proj_dw", h0b, dproj)
    dh0 = send("w_in", {"w_in": jnp.transpose(wi_half.reshape(D_MODEL, N_DEV, IN_SHARD), (1, 0, 2))}, dh0)
    dx1, g["mix_g_0"] = _norm_bwd("mix_norm_0_bwd", dh0, x1, w["mix_g"][0], dx2)
    dx0 = ffn_back(dx1, 0, 0)
    g["ffn_w1"], g["ffn_w3"], g["ffn_w2"] = ffn_bufs
    return loss, dx0, g


_WEIGHTS = ["ffn_norm_g", "ffn_w1", "ffn_w3", "ffn_w2", "mix_norm_g", "w_in_even", "w_out_even", "ret_norm_g", "conv_w",
            "conv_b", "lru_w_a", "lru_b_a", "lru_w_i", "lru_b_i", "lru_lambda", "s5_lambda_re", "s5_lambda_im", "s5_log_dt",
            "s5_b_re", "s5_b_im", "s5_c_re", "s5_c_im", "s5_d", "glu_w_a", "glu_w_b", "final_norm_g"]
_BIG = ["ffn_w1", "ffn_w3", "ffn_w2", "w_in_even", "w_out_even", "glu_w_a", "glu_w_b"]
_SMALL_SHARDED = ["ffn_norm_g", "conv_w", "s5_d"]
_SMALL = [n for n in _WEIGHTS if n not in _BIG]


def kernel(x, ffn_norm_g, ffn_w1, ffn_w3, ffn_w2, mix_norm_g, w_in_even, w_out_even, ret_norm_g, conv_w, conv_b, lru_w_a, lru_b_a, lru_w_i, lru_b_i, lru_lambda, s5_lambda_re, s5_lambda_im, s5_log_dt, s5_b_re, s5_b_im, s5_c_re, s5_c_im, s5_d, glu_w_a, glu_w_b, final_norm_g, loss_target, m_ffn_norm_g, m_ffn_w1, m_ffn_w3, m_ffn_w2, m_mix_norm_g, m_w_in_even, m_w_out_even, m_ret_norm_g, m_conv_w, m_conv_b, m_lru_w_a, m_lru_b_a, m_lru_w_i, m_lru_b_i, m_lru_lambda, m_s5_lambda_re, m_s5_lambda_im, m_s5_log_dt, m_s5_b_re, m_s5_b_im, m_s5_c_re, m_s5_c_im, m_s5_d, m_glu_w_a, m_glu_w_b, m_final_norm_g, v_ffn_norm_g, v_ffn_w1, v_ffn_w3, v_ffn_w2, v_mix_norm_g, v_w_in_even, v_w_out_even, v_ret_norm_g, v_conv_w, v_conv_b, v_lru_w_a, v_lru_b_a, v_lru_w_i, v_lru_b_i, v_lru_lambda, v_s5_lambda_re, v_s5_lambda_im, v_s5_log_dt, v_s5_b_re, v_s5_b_im, v_s5_c_re, v_s5_c_im, v_s5_d, v_glu_w_a, v_glu_w_b, v_final_norm_g):
    a = dict(locals())
    nb, s, d = x.shape
    ax, ay, ac = lax.axis_index("x"), lax.axis_index("y"), lax.axis_index("c")
    dev = 4 * ax + 2 * ay + ac
    chip = 2 * ax + ay

    def ffn_shards(l, h):
        extra = FF_PAD - FF_SHARD
        return [jnp.pad(ffn_w1[l, h].astype(bf16), ((0, 0), (0, extra))), jnp.pad(ffn_w3[l, h].astype(bf16), ((0, 0), (0, extra))),
                jnp.pad(ffn_w2[l, h].astype(bf16), ((0, extra), (0, 0)))]

    first = _all_gather("ag_first", ffn_shards(0, 0) + [_pack([ffn_norm_g, conv_w, s5_d])])
    sm = first[3].reshape(N_DEV, -1)
    ffn_g_full = jnp.transpose(sm[:, :512].reshape(N_DEV, 2, 2, 128), (1, 2, 0, 3)).reshape(2, 2, D_MODEL)
    conv_w_full = jnp.transpose(sm[:, 512:768].reshape(N_DEV, 4, 64), (1, 0, 2)).reshape(4, LRU_WIDTH)
    s5_d_full = sm[:, 768:896].reshape(1, D_MODEL)

    ag_src = [None, [w_in_even[0].astype(bf16), w_out_even[0].astype(bf16)], ffn_shards(0, 1),
              ffn_shards(1, 0) + [glu_w_a[0].astype(bf16), glu_w_b[0].astype(bf16)], ffn_shards(1, 1)]
    ag, token = [None], first[0]
    for k, grp in enumerate(ag_src):
        if grp is None:
            continue
        grp[0] = _tie(f"tie_ag_{k}", grp[0], [token])
        lands = [lax.dynamic_update_index_in_dim(lax.empty((N_DEV,) + t.shape, bf16), t, dev, 0) for t in grp]
        ag.append(_xchg_start(f"ag_start_{k}", "gather", grp, lands))
        token = ag[-1]["token"]

    def weights_of(k, after):
        if k == 0:
            return {"ffn": [first[0], _tie("tie_ag_started", first[1], [h["token"] for h in ag[1:]]), first[2]]}
        got = _xchg_wait(f"ag_wait_{k}", ag[k], after)
        if k == 1:
            return {"w_in": jnp.transpose(got[0], (1, 0, 2)).reshape(D_MODEL, N_DEV * IN_SHARD),
                    "w_out": got[1].reshape(D_MODEL, D_MODEL)}
        if k == 3:
            return {"ffn": got[:3], "glu_a": got[3].reshape(D_MODEL, D_MODEL), "glu_b": got[4].reshape(D_MODEL, D_MODEL)}
        return {"ffn": got}

    ffn_lands = [lax.empty((N_DEV - 1, 2, 2) + shp, bf16)
                 for shp in ((D_MODEL, FF_SHARD), (D_MODEL, FF_SHARD), (FF_SHARD, D_MODEL))]
    rs = []

    ffn_names = ("ffn_w1", "ffn_w3", "ffn_w2")

    def send(group, arrays, carry):
        srcs = list(arrays.values())
        if group.startswith("ffn_"):
            which = [ffn_names.index(n) for n in arrays]
            sfx = [(int(group[4]), int(group[5]))] * len(which)
            h = _xchg_start("rs_start_" + group, "scatter", srcs, [ffn_lands[k] for k in which], sfx)
            for k, land in zip(which, h["lands"]):
                ffn_lands[k] = land
        else:
            h = _xchg_start("rs_start_" + group, "scatter", srcs,
                            [lax.empty((N_DEV - 1,) + t.shape[1:], bf16) for t in srcs])
        rs.append((group, list(arrays), h))
        return _tie("tie_" + group, carry, [h["token"]])

    w = {
        "ffn_g": [[ffn_g_full[l, h].reshape(1, D_MODEL) for h in range(2)] for l in range(2)],
        "mix_g": [mix_norm_g[0:1], mix_norm_g[1:2]],
        "ret_g": ret_norm_g, "conv_w": conv_w_full, "conv_b": conv_b,
        "lru_w_a": lru_w_a[0], "lru_b_a": lru_b_a, "lru_w_i": lru_w_i[0], "lru_b_i": lru_b_i, "lru_lam": lru_lambda,
        "s5_lr": s5_lambda_re[0], "s5_li": s5_lambda_im[0], "s5_ldt": s5_log_dt.reshape(S5_GROUPS, 1),
        "s5_bre": jnp.swapaxes(s5_b_re[0], 1, 2), "s5_bim": jnp.swapaxes(s5_b_im[0], 1, 2),
        "s5_cre": s5_c_re[0], "s5_cim": s5_c_im[0], "s5_d": s5_d_full,
        "final_g": final_norm_g.reshape(1, D_MODEL),
    }

    small_grads = {}

    def last_small(g, carry):
        part = _small_partials(g)
        mine = _pack([part[n] for n in _SMALL])
        land = lax.dynamic_update_index_in_dim(lax.empty((N_DEV,) + mine.shape, f32), mine, dev, 0)
        h = _xchg_start("ag_start_small_grads", "gather", [mine], [land])
        small_grads.update(h=h, shapes=[part[n].shape for n in _SMALL])
        return _tie("tie_small_grads", carry, [h["token"]])

    loss_part, dx, g = _local_step(x.reshape(nb * s, d), loss_target.reshape(nb * s, d), w, weights_of, send, last_small,
                                   nb, s)
    loss = lax.psum(loss_part[0, 0], ("x", "y", "c"))
    (gath,) = _xchg_wait("ag_wait_small_grads", small_grads["h"], [dx])
    full = dict(zip(_SMALL, _unpack(_sum8("sum_small_grads", gath), small_grads["shapes"])))
    for n in _SMALL_SHARDED:
        width = a[n].shape[-1]
        full[n] = lax.dynamic_slice_in_dim(full[n], dev * width, width, axis=full[n].ndim - 1)
    shapes = [a[n].shape for n in _SMALL]
    packed = _adamw("adamw_small", _pack([a[n] for n in _SMALL]), _pack([a["m_" + n] for n in _SMALL]),
                    _pack([a["v_" + n] for n in _SMALL]), _pack([full[n] for n in _SMALL]))
    res = {n: vals for n, vals in zip(_SMALL, zip(*[_unpack(p, shapes) for p in packed]))}
    return _finish(a, g, dx, loss, res, packed, rs, ffn_lands, dev, nb, s, d)


def _small_partials(g):
    return {
        "ffn_norm_g": jnp.stack([jnp.stack([g[f"ffn_g_{l}{h}"][0] for h in range(2)]) for l in range(2)]),
        "mix_norm_g": jnp.concatenate([g["mix_g_0"], g["mix_g_1"]], axis=0),
        "ret_norm_g": g["ret_g"], "conv_w": g["conv_w"][None], "conv_b": g["conv_b"],
        "lru_w_a": g["lru_w_a"][None], "lru_b_a": g["lru_b_a"], "lru_w_i": g["lru_w_i"][None], "lru_b_i": g["lru_b_i"],
        "lru_lambda": g["lru_lam"], "s5_lambda_re": g["s5_lr"][None], "s5_lambda_im": g["s5_li"][None],
        "s5_log_dt": g["s5_ldt"].reshape(1, S5_GROUPS),
        "s5_b_re": jnp.swapaxes(g["s5_bre"], 1, 2)[None], "s5_b_im": jnp.swapaxes(g["s5_bim"], 1, 2)[None],
        "s5_c_re": g["s5_cre"][None], "s5_c_im": g["s5_cim"][None], "s5_d": g["s5_d"], "final_norm_g": g["final_g"][0],
    }


def _finish(a, g, dx, loss, res, packed, rs, ffn_lands, dev, nb, s, d):
    landed = {}
    for group, names, h in rs:
        if not group.startswith("ffn_"):
            landed.update(zip(names, _xchg_wait("rs_wait_" + group, h, [dx])))
    own = {n: lax.dynamic_index_in_dim(g[n], dev, axis=0, keepdims=False) for n in ("ffn_w1", "ffn_w3", "ffn_w2")}
    own["w_in"] = lax.dynamic_slice_in_dim(g["w_in"], dev * IN_SHARD, IN_SHARD, axis=1)
    for n in ("w_out", "glu_a", "glu_b"):
        own[n] = lax.dynamic_slice_in_dim(g[n], dev * (D_MODEL // N_DEV), D_MODEL // N_DEV, axis=0)

    def update(n, short):
        res[n] = _adamw("adamw_" + n, a[n], a["m_" + n], a["v_" + n], own[short].reshape(a[n].shape),
                        landed[short].reshape((N_DEV - 1,) + a[n].shape))

    for n, short in zip(_BIG[3:], ("w_in", "w_out", "glu_a", "glu_b")):
        update(n, short)
    after = [dx, packed[0]] + [res[n][0] for n in _BIG[3:]]
    ffn_names = ("ffn_w1", "ffn_w3", "ffn_w2")
    for group, names, h in rs:
        if group.startswith("ffn_"):
            which = [ffn_names.index(n) for n in names]
            for k, land in zip(which, _xchg_wait("rs_wait_" + group, h, after, [ffn_lands[k] for k in which])):
                ffn_lands[k] = land
    landed.update(zip(ffn_names, ffn_lands))
    for n in _BIG[:3]:
        update(n, n)

    out = [loss, dx.reshape(nb, s, d)]
    for k in range(4):
        out += [res[n][k] for n in _WEIGHTS]
    return tuple(out)
```

```python
import functools
import math

import numpy as np
import jax
import jax.numpy as jnp
from jax import lax
from jax.experimental import pallas as pl
from jax.experimental.pallas import tpu as pltpu

f32 = jnp.float32
bf16 = jnp.bfloat16

D_MODEL = 1024
N_DEV = 8
EPS = 1e-6
RET_HEADS = 4
HEAD_DIM = 128
RET_WIDTH = 512
RET_CHUNK = 128
ROPE_BASE = 10000.0
LRU_WIDTH = 512
LRU_BLOCKS = 4
LRU_C = 8.0
S5_GROUP = 16
S5_GROUPS = 64
S5_STATE = 64
S5_CHUNK = 128
S5_BLOCKS = 8
S5_BLOCK_STATES = 512
SUBLANES = 8
D_FF = 2816
FF_SHARD = D_FF // N_DEV
FF_PAD = 384
IN_SHARD = 3072 // N_DEV
ADAM_LR = 0.001
ADAM_B1 = 0.9
ADAM_B2 = 0.999
ADAM_EPS = 1e-08
ADAM_WD = 0.01
ADAM_STEP = 10

VMEM_LIMIT = 56 * 1024 * 1024
VMEM_SPEC = pl.BlockSpec(memory_space=pltpu.VMEM)
ANY_SPEC = pl.BlockSpec(memory_space=pl.ANY)
HBM_SPEC = pl.BlockSpec(memory_space=pltpu.HBM)
SEM_SPEC = pl.BlockSpec(memory_space=pltpu.SEMAPHORE)
SIDE_EFFECT = pltpu.SideEffectType.DATAFLOW_SIDE_EFFECTING
MESH = pl.DeviceIdType.MESH


def _cp(*sem):
    return pltpu.CompilerParams(dimension_semantics=sem, vmem_limit_bytes=VMEM_LIMIT)


def _nn(a, b):
    return jnp.dot(a, b, preferred_element_type=f32)


def _nt(a, b):
    return lax.dot_general(a, b, (((1,), (1,)), ((), ())), preferred_element_type=f32)


def _tn(a, b):
    return lax.dot_general(a, b, (((0,), (0,)), ((), ())), preferred_element_type=f32)


def _rms_fwd(x, g):
    r = lax.rsqrt(jnp.mean(x * x, axis=-1, keepdims=True) + EPS)
    xn = x * r
    return xn * g, xn, r


def _rms_bwd(dh, xn, r, g):
    dxn = dh * g
    dx = r * (dxn - xn * jnp.mean(dxn * xn, axis=-1, keepdims=True))
    dg = jnp.sum(dh * xn, axis=0, keepdims=True)
    return dx, dg


def _shift_dn(v, d, row, fill=0.0):
    return jnp.where(row >= d, pltpu.roll(v, d, 0), fill)


def _shift_up(v, d, row, fill=0.0):
    n = v.shape[0]
    return jnp.where(row < n - d, pltpu.roll(v, n - d, 0), fill)


def _ew(name, fn, ins, outs, tm=512):
    t = ins[0].shape[0]
    n_in = len(ins)

    def body(*refs):
        res = fn(*[r[...] for r in refs[:n_in]])
        for o, v in zip(refs[n_in:], res):
            o[...] = v.astype(o.dtype)

    return pl.pallas_call(
        body, name=name, grid=(t // tm,),
        in_specs=[pl.BlockSpec((tm, a.shape[1]), lambda i: (i, 0)) for a in ins],
        out_specs=[pl.BlockSpec((tm, n), lambda i: (i, 0)) for n, _ in outs],
        out_shape=[jax.ShapeDtypeStruct((t, n), dt) for n, dt in outs],
        compiler_params=_cp("parallel"),
    )(*ins)


def _mm(name, x, w, kind, extras=(), epilogue=None, outs=None, tm=512, tn=512):
    t = x.shape[0]
    n = w.shape[1] if kind == "nn" else w.shape[0]
    tn = min(tn, n)
    outs = outs or [f32]
    n_ex = len(extras)

    def body(x_ref, w_ref, *refs):
        xb = x_ref[...].astype(bf16)
        acc = _nn(xb, w_ref[...]) if kind == "nn" else _nt(xb, w_ref[...])
        res = epilogue(acc, *[r[...] for r in refs[:n_ex]]) if epilogue else (acc,)
        for o, v in zip(refs[n_ex:], res):
            o[...] = v.astype(o.dtype)

    w_spec = (pl.BlockSpec((w.shape[0], tn), lambda i, j: (0, j)) if kind == "nn"
              else pl.BlockSpec((tn, w.shape[1]), lambda i, j: (j, 0)))
    tile = pl.BlockSpec((tm, tn), lambda i, j: (i, j))
    return pl.pallas_call(
        body, name=name, grid=(t // tm, n // tn),
        in_specs=[pl.BlockSpec((tm, x.shape[1]), lambda i, j: (i, 0)), w_spec] + [tile] * n_ex,
        out_specs=[tile] * len(outs),
        out_shape=[jax.ShapeDtypeStruct((t, n), dt) for dt in outs],
        compiler_params=_cp("parallel", "parallel"),
    )(x, w, *extras)


def _mm_tn(name, x, y, tk=1024, tn=512, tt=512):
    t, k = x.shape
    n = y.shape[1]
    tk, tn = min(tk, k), min(tn, n)

    def body(x_ref, y_ref, o_ref, ob_ref):
        @pl.when(pl.program_id(2) == 0)
        def _():
            o_ref[...] = jnp.zeros_like(o_ref)
        o_ref[...] += _tn(x_ref[...].astype(bf16), y_ref[...].astype(bf16))

        @pl.when(pl.program_id(2) == pl.num_programs(2) - 1)
        def _():
            ob_ref[...] = o_ref[...].astype(bf16)

    out = pl.BlockSpec((tk, tn), lambda i, j, s: (i, j))
    return pl.pallas_call(
        body, name=name, grid=(k // tk, n // tn, t // tt),
        in_specs=[pl.BlockSpec((tt, tk), lambda i, j, s: (s, i)), pl.BlockSpec((tt, tn), lambda i, j, s: (s, j))],
        out_specs=[out, out],
        out_shape=[jax.ShapeDtypeStruct((k, n), f32), jax.ShapeDtypeStruct((k, n), bf16)],
        compiler_params=_cp("parallel", "parallel", "arbitrary"),
    )(x, y)


def _norm_fwd(name, x, g, tm=512):
    t, d = x.shape

    def body(x_ref, g_ref, h_ref, hb_ref):
        h, _, _ = _rms_fwd(x_ref[...], g_ref[...])
        h_ref[...] = h
        hb_ref[...] = h.astype(bf16)

    row = pl.BlockSpec((tm, d), lambda i: (i, 0))
    return pl.pallas_call(
        body, name=name, grid=(t // tm,),
        in_specs=[row, pl.BlockSpec((1, d), lambda i: (0, 0))],
        out_specs=[row, row],
        out_shape=[jax.ShapeDtypeStruct((t, d), f32), jax.ShapeDtypeStruct((t, d), bf16)],
        compiler_params=_cp("parallel"),
    )(x, g)


def _norm_bwd(name, dh, x, g, dres, tm=512):
    t, d = x.shape

    def body(dh_ref, x_ref, g_ref, dres_ref, dx_ref, dg_ref):
        gv = g_ref[...]
        _, xn, r = _rms_fwd(x_ref[...], gv)
        dx, dg = _rms_bwd(dh_ref[...], xn, r, gv)
        dx_ref[...] = dres_ref[...] + dx

        @pl.when(pl.program_id(0) == 0)
        def _():
            dg_ref[...] = jnp.zeros_like(dg_ref)
        dg_ref[...] += dg

    row = pl.BlockSpec((tm, d), lambda i: (i, 0))
    vec = pl.BlockSpec((1, d), lambda i: (0, 0))
    return pl.pallas_call(
        body, name=name, grid=(t // tm,),
        in_specs=[row, row, vec, row],
        out_specs=[row, vec],
        out_shape=[jax.ShapeDtypeStruct((t, d), f32), jax.ShapeDtypeStruct((1, d), f32)],
        compiler_params=_cp("arbitrary"),
    )(dh, x, g, dres)


def _final_loss(x, g, target, tm=512):
    t, d = x.shape

    def body(x_ref, g_ref, t_ref, loss_ref, dx_ref, dg_ref):
        gv = g_ref[...]
        y, xn, r = _rms_fwd(x_ref[...], gv)
        err = y - t_ref[...]
        dy = err * (1.0 / d)
        dx, dg = _rms_bwd(dy, xn, r, gv)
        dx_ref[...] = dx

        @pl.when(pl.program_id(0) == 0)
        def _():
            dg_ref[...] = jnp.zeros_like(dg_ref)
            loss_ref[...] = jnp.zeros_like(loss_ref)
        dg_ref[...] += dg
        loss_ref[...] += jnp.full((1, 128), 0.5 / d, f32) * jnp.sum(err * err)

    row = pl.BlockSpec((tm, d), lambda i: (i, 0))
    vec = pl.BlockSpec((1, d), lambda i: (0, 0))
    return pl.pallas_call(
        body, name="final_loss", grid=(t // tm,),
        in_specs=[row, vec, row],
        out_specs=[pl.BlockSpec((1, 128), lambda i: (0, 0)), row, vec],
        out_shape=[jax.ShapeDtypeStruct((1, 128), f32), jax.ShapeDtypeStruct((t, d), f32),
                   jax.ShapeDtypeStruct((1, d), f32)],
        compiler_params=_cp("arbitrary"),
    )(x, g, target)


def _load_ffn_weights(hbm_refs, vmem_refs, sems):
    @pl.when(pl.program_id(0) == 0)
    def _():
        copies = []
        for k, (src, dst) in enumerate(zip(hbm_refs, vmem_refs)):
            for j in range(N_DEV):
                half = pl.ds((j % 2) * FF_PAD, FF_PAD)
                window = dst.at[j // 2, half, :] if k == 2 else dst.at[j // 2, :, half]
                copies.append(pltpu.make_async_copy(src.at[j], window, sems.at[k * N_DEV + j]))
        for cp in copies:
            cp.start()
        for cp in copies:
            cp.wait()


def _ffn_weight_scratch(nj, d, ff):
    return [pltpu.VMEM((nj, d, ff), bf16), pltpu.VMEM((nj, d, ff), bf16), pltpu.VMEM((nj, ff, d), bf16),
            pltpu.SemaphoreType.DMA((3 * N_DEV,))]


def _ffn_fwd(name, x, g, w1, w3, w2, tm=256):
    t, d = x.shape
    nj, ff = N_DEV // 2, 2 * FF_PAD

    def body(x_ref, g_ref, w1_hbm, w3_hbm, w2_hbm, y_ref, a_ref, b_ref, w1_ref, w3_ref, w2_ref, sems):
        _load_ffn_weights((w1_hbm, w3_hbm, w2_hbm), (w1_ref, w3_ref, w2_ref), sems)
        xv = x_ref[...]
        h, _, _ = _rms_fwd(xv, g_ref[...])
        hb = h.astype(bf16)
        acc = jnp.zeros((tm, d), f32)
        for j in range(nj):
            a = _nn(hb, w1_ref[j])
            b = _nn(hb, w3_ref[j])
            a_ref[j] = a.astype(bf16)
            b_ref[j] = b.astype(bf16)
            u = (a * jax.nn.sigmoid(a) * b).astype(bf16)
            acc = acc + _nn(u, w2_ref[j])
        y_ref[...] = xv + 0.5 * acc

    row = pl.BlockSpec((tm, d), lambda i: (i, 0))
    mid = pl.BlockSpec((nj, tm, ff), lambda i: (0, i, 0))
    return pl.pallas_call(
        body, name=name, grid=(t // tm,),
        in_specs=[row, pl.BlockSpec((1, d), lambda i: (0, 0)), ANY_SPEC, ANY_SPEC, ANY_SPEC],
        out_specs=[row, mid, mid],
        out_shape=[jax.ShapeDtypeStruct((t, d), f32), jax.ShapeDtypeStruct((nj, t, ff), bf16),
                   jax.ShapeDtypeStruct((nj, t, ff), bf16)],
        scratch_shapes=_ffn_weight_scratch(nj, d, ff),
        compiler_params=_cp("arbitrary"),
    )(x, g, w1, w3, w2)


def _ffn_dx(name, dy, x, g, w1, w3, w2, a, b, tm=256):
    t, d = x.shape
    nj, ff = N_DEV // 2, 2 * FF_PAD

    def body(dy_ref, x_ref, g_ref, w1_hbm, w3_hbm, w2_hbm, a_ref, b_ref,
             dx_ref, dg_ref, hb_ref, dyh_ref, u_ref, da_ref, db_ref, w1_ref, w3_ref, w2_ref, sems):
        _load_ffn_weights((w1_hbm, w3_hbm, w2_hbm), (w1_ref, w3_ref, w2_ref), sems)
        gv = g_ref[...]
        h, xn, r = _rms_fwd(x_ref[...], gv)
        hb_ref[...] = h.astype(bf16)
        dyv = dy_ref[...]
        dyh = (0.5 * dyv).astype(bf16)
        dyh_ref[...] = dyh
        dh = jnp.zeros((tm, d), f32)
        for j in range(nj):
            av = a_ref[j].astype(f32)
            bv = b_ref[j].astype(f32)
            s = jax.nn.sigmoid(av)
            silu = av * s
            u_ref[j] = (silu * bv).astype(bf16)
            du = _nt(dyh, w2_ref[j])
            dab = (du * bv * (s * (1.0 + av * (1.0 - s)))).astype(bf16)
            dbb = (du * silu).astype(bf16)
            da_ref[j] = dab
            db_ref[j] = dbb
            dh = dh + _nt(dab, w1_ref[j]) + _nt(dbb, w3_ref[j])
        dx, dg = _rms_bwd(dh, xn, r, gv)
        dx_ref[...] = dyv + dx

        @pl.when(pl.program_id(0) == 0)
        def _():
            dg_ref[...] = jnp.zeros_like(dg_ref)
        dg_ref[...] += dg

    row = pl.BlockSpec((tm, d), lambda i: (i, 0))
    vec = pl.BlockSpec((1, d), lambda i: (0, 0))
    mid = pl.BlockSpec((nj, tm, ff), lambda i: (0, i, 0))
    mid_shape = jax.ShapeDtypeStruct((nj, t, ff), bf16)
    return pl.pallas_call(
        body, name=name, grid=(t // tm,),
        in_specs=[row, row, vec, ANY_SPEC, ANY_SPEC, ANY_SPEC, mid, mid],
        out_specs=[row, vec, row, row, mid, mid, mid],
        out_shape=[jax.ShapeDtypeStruct((t, d), f32), jax.ShapeDtypeStruct((1, d), f32),
                   jax.ShapeDtypeStruct((t, d), bf16), jax.ShapeDtypeStruct((t, d), bf16),
                   mid_shape, mid_shape, mid_shape],
        scratch_shapes=_ffn_weight_scratch(nj, d, ff),
        compiler_params=_cp("arbitrary"),
    )(dy, x, g, w1, w3, w2, a, b)


def _ffn_dw_one(name, x, y, buf, l, h, tt=1024):
    return _ffn_dw_calls(name, [(x, y)], [buf], l, h, tt)


def _ffn_dw_calls(name, products, bufs, l, h, tt):
    n = len(products)
    t = products[0][0].shape[-2]
    tt = min(tt, t)
    pairs = N_DEV // 2
    cut_cols = [x.ndim == 2 for x, _ in products]

    def body(*refs):
        ins, outs, accs = refs[:2 * n], refs[3 * n:5 * n], refs[5 * n:]
        s = pl.program_id(1)
        for k in range(n):
            x_ref, y_ref = ins[2 * k], ins[2 * k + 1]
            xv = x_ref[0] if len(x_ref.shape) == 3 else x_ref[...]
            yv = y_ref[0] if len(y_ref.shape) == 3 else y_ref[...]
            prod = _tn(xv, yv)

            @pl.when(s == 0)
            def _():
                accs[k][...] = prod

            @pl.when(s > 0)
            def _():
                accs[k][...] += prod

        @pl.when(s == pl.num_programs(1) - 1)
        def _():
            for k in range(n):
                acc = accs[k][...]
                for e in range(2):
                    lo = e * FF_PAD
                    part = acc[:, lo:lo + FF_SHARD] if cut_cols[k] else acc[lo:lo + FF_SHARD, :]
                    outs[k][e] = part
                    outs[n + k][e] = part.astype(bf16)

    def in_spec(a):
        if a.ndim == 3:
            return pl.BlockSpec((1, tt, a.shape[-1]), lambda p, s: (p, s, 0))
        return pl.BlockSpec((tt, a.shape[-1]), lambda p, s: (s, 0))

    in_specs, out_f32, out_b16, shapes_b16, scratch = [], [], [], [], []
    for (x, y), buf in zip(products, bufs):
        in_specs += [in_spec(x), in_spec(y)]
        k_, n_ = buf.shape[-2:]
        out_f32.append(pl.BlockSpec((2, None, None, k_, n_), lambda p, s: (p, l, h, 0, 0)))
        out_b16.append(pl.BlockSpec((2, k_, n_), lambda p, s: (p, 0, 0)))
        shapes_b16.append(jax.ShapeDtypeStruct((N_DEV, k_, n_), bf16))
        scratch.append(pltpu.VMEM((x.shape[-1], y.shape[-1]), f32))
    flat = [a for xy in products for a in xy]
    return pl.pallas_call(
        body, name=name, grid=(pairs, t // tt),
        in_specs=in_specs + [ANY_SPEC] * n,
        out_specs=out_f32 + out_b16,
        out_shape=[jax.ShapeDtypeStruct(b.shape, b.dtype) for b in bufs] + shapes_b16,
        input_output_aliases={2 * n + k: k for k in range(n)},
        scratch_shapes=scratch,
        compiler_params=_cp("parallel", "arbitrary"),
    )(*flat, *bufs)


_LOG_GAMMA = [float(np.log1p(-np.float32(2.0) ** np.float32(-5.0 - h))) for h in range(RET_HEADS)]


def _ret_consts(h):
    lg = jnp.where(h == 0, _LOG_GAMMA[0], jnp.where(h == 1, _LOG_GAMMA[1],
                   jnp.where(h == 2, _LOG_GAMMA[2], _LOG_GAMMA[3]))).astype(f32)
    c = RET_CHUNK
    r = lax.broadcasted_iota(jnp.int32, (c, c), 0)
    cc = lax.broadcasted_iota(jnp.int32, (c, c), 1)
    decay = jnp.where(r >= cc, jnp.exp(lg * jnp.maximum((r - cc).astype(f32), 0.0)), 0.0)
    pos = lax.broadcasted_iota(jnp.int32, (c, 1), 0).astype(f32)
    kd = jnp.exp(lg * (c - 1.0 - pos))
    qd = jnp.exp(lg * (pos + 1.0))
    gc = jnp.exp(lg * c)
    return decay, kd, qd, gc


def _rope(x, cos, sin):
    return x * cos + pltpu.roll(x, HEAD_DIM // 2, 1) * sin


def _rope_t(g, cos, sin):
    return g * cos + pltpu.roll(g * sin, HEAD_DIM // 2, 1)


def _rope_tables(s):
    half = HEAD_DIM // 2
    inv = ROPE_BASE ** (-jnp.arange(half, dtype=f32) / half)
    ang = jnp.arange(s, dtype=f32)[:, None] * inv[None, :]
    cos, sin = jnp.cos(ang), jnp.sin(ang)
    return jnp.concatenate([cos, cos], axis=1), jnp.concatenate([-sin, sin], axis=1)


def _head_ln(o):
    mu = jnp.mean(o, axis=-1, keepdims=True)
    oc = o - mu
    rs = lax.rsqrt(jnp.mean(oc * oc, axis=-1, keepdims=True) + EPS)
    return oc * rs, rs


def _ret_fwd(proj, cos, sin, ret_g, nb, s):
    c = RET_CHUNK
    nc = s // c
    t = nb * s
    scale = HEAD_DIM ** -0.5

    def body(q_ref, k_ref, v_ref, gate_ref, cos_ref, sin_ref, g_ref, o_ref, rprev_ref, m_ref):
        decay, kd, qd, gc = _ret_consts(pl.program_id(1))
        gv = g_ref[...]

        def step(n, rv):
            rows = pl.ds(pl.multiple_of(n * c, c), c)
            cs, sn = cos_ref[rows, :], sin_ref[rows, :]
            q = _rope(q_ref[rows, :], cs, sn)
            k = _rope(k_ref[rows, :], cs, sn) * scale
            vb = v_ref[rows, :].astype(bf16)
            sc = _nt(q.astype(bf16), k.astype(bf16)) * decay
            rprev_ref[n] = rv
            o = _nn(sc.astype(bf16), vb) + _nn((q * qd).astype(bf16), rv.astype(bf16))
            o_ref[rows, :] = o
            y, _ = _head_ln(o)
            gate = gate_ref[rows, :]
            m_ref[rows, :] = y * gv * (gate * jax.nn.sigmoid(gate))
            return rv * gc + _tn((k * kd).astype(bf16), vb)

        lax.fori_loop(0, nc, step, jnp.zeros((HEAD_DIM, HEAD_DIM), f32))

    def col(off):
        return pl.BlockSpec((s, HEAD_DIM), lambda b, h: (b, off + h))

    tab = pl.BlockSpec((s, HEAD_DIM), lambda b, h: (0, 0))
    return pl.pallas_call(
        body, name="ret_fwd", grid=(nb, RET_HEADS),
        in_specs=[col(0), col(4), col(8), col(12), tab, tab, pl.BlockSpec((1, HEAD_DIM), lambda b, h: (0, h))],
        out_specs=[col(0), pl.BlockSpec((nc, HEAD_DIM, HEAD_DIM), lambda b, h: (b * RET_HEADS + h, 0, 0)), col(0)],
        out_shape=[jax.ShapeDtypeStruct((t, RET_WIDTH), f32),
                   jax.ShapeDtypeStruct((nb * RET_HEADS * nc, HEAD_DIM, HEAD_DIM), f32),
                   jax.ShapeDtypeStruct((t, RET_WIDTH), f32)],
        compiler_params=_cp("parallel", "parallel"),
    )(proj, proj, proj, proj, cos, sin, ret_g)


def _ret_bwd(dmerged, o_raw, rprev, proj, cos, sin, ret_g, nb, s):
    c = RET_CHUNK
    nc = s // c
    t = nb * s
    scale = HEAD_DIM ** -0.5

    def body(dm_ref, o_ref, rprev_ref, q_ref, k_ref, v_ref, gate_ref, cos_ref, sin_ref, g_ref,
             dq_ref, dk_ref, dv_ref, dgate_ref, dg_ref):
        @pl.when(pl.program_id(1) == 0)
        def _():
            dg_ref[...] = jnp.zeros_like(dg_ref)
        decay, kd, qd, gc = _ret_consts(pl.program_id(0))
        gv = g_ref[...]

        def step(i, carry):
            drn, dg = carry
            n = nc - 1 - i
            rows = pl.ds(pl.multiple_of(n * c, c), c)
            cs, sn = cos_ref[rows, :], sin_ref[rows, :]
            q = _rope(q_ref[rows, :], cs, sn)
            k = _rope(k_ref[rows, :], cs, sn) * scale
            qb, kb = q.astype(bf16), k.astype(bf16)
            vb = v_ref[rows, :].astype(bf16)
            sc = _nt(qb, kb) * decay
            y, rs = _head_ln(o_ref[rows, :])
            gate = gate_ref[rows, :]
            sg = jax.nn.sigmoid(gate)
            silu = gate * sg
            dm = dm_ref[rows, :]
            dgate_ref[rows, :] = dm * y * gv * (sg * (1.0 + gate * (1.0 - sg)))
            dyl = dm * gv * silu
            dg = dg + jnp.sum(dm * y * silu, axis=0, keepdims=True)
            do = rs * (dyl - jnp.mean(dyl, axis=-1, keepdims=True) - y * jnp.mean(dyl * y, axis=-1, keepdims=True))
            dob = do.astype(bf16)
            rv = rprev_ref[n]
            drb = drn.astype(bf16)
            ds = (_nt(dob, vb) * decay).astype(bf16)
            kdb = (k * kd).astype(bf16)
            qdb = (q * qd).astype(bf16)
            dq_r = _nn(ds, kb) + _nt(dob, rv.astype(bf16)) * qd
            dk_r = _tn(ds, qb) + _nt(vb, drb) * kd
            dv_ref[rows, :] = _tn(sc.astype(bf16), dob) + _nn(kdb, drb)
            dq_ref[rows, :] = _rope_t(dq_r, cs, sn)
            dk_ref[rows, :] = _rope_t(dk_r * scale, cs, sn)
            return drn * gc + _tn(qdb, dob), dg

        _, dg = lax.fori_loop(0, nc, step, (jnp.zeros((HEAD_DIM, HEAD_DIM), f32), jnp.zeros((1, HEAD_DIM), f32)))
        dg_ref[...] += dg

    def col(off):
        return pl.BlockSpec((s, HEAD_DIM), lambda h, b: (b, off + h))

    tab = pl.BlockSpec((s, HEAD_DIM), lambda h, b: (0, 0))
    gsp = pl.BlockSpec((1, HEAD_DIM), lambda h, b: (0, h))
    out_t = jax.ShapeDtypeStruct((t, RET_WIDTH), f32)
    return pl.pallas_call(
        body, name="ret_bwd", grid=(RET_HEADS, nb),
        in_specs=[col(0), col(0), pl.BlockSpec((nc, HEAD_DIM, HEAD_DIM), lambda h, b: (b * RET_HEADS + h, 0, 0)),
                  col(0), col(4), col(8), col(12), tab, tab, gsp],
        out_specs=[col(0), col(0), col(0), col(0), gsp],
        out_shape=[out_t, out_t, out_t, out_t, jax.ShapeDtypeStruct((1, RET_WIDTH), f32)],
        compiler_params=_cp("parallel", "arbitrary"),
    )(dmerged, o_raw, rprev, proj, proj, proj, proj, cos, sin, ret_g)


def _neg_expm1(z):
    series = -(z * (1.0 + z * (0.5 + z * (1.0 / 6.0 + z * (1.0 / 24.0)))))
    return jnp.where(z > -0.01, series, 1.0 - jnp.exp(z))


def _lru_gates(xc, pa, pi, lam):
    r = jax.nn.sigmoid(pa)
    i = jax.nn.sigmoid(pi)
    log_a = -LRU_C * r * jax.nn.softplus(-lam)
    a = jnp.exp(log_a)
    bx = jnp.sqrt(_neg_expm1(2.0 * log_a)) * i * xc
    return a, bx


def _scan_rows(a, b, row, up):
    sub = row[:SUBLANES] & (SUBLANES - 1)
    groups = list(range(a.shape[0] // SUBLANES))
    out = [None] * len(groups)
    edge = slice(0, 1) if up else slice(SUBLANES - 1, SUBLANES)
    carry = jnp.zeros((1, a.shape[1]), f32)
    for g in (reversed(groups) if up else groups):
        rows = slice(g * SUBLANES, (g + 1) * SUBLANES)
        xa, xb = a[rows], b[rows]
        d = 1
        while d < SUBLANES:
            keep = (sub < SUBLANES - d) if up else (sub >= d)
            shift = SUBLANES - d if up else d
            xb = xa * jnp.where(keep, pltpu.roll(xb, shift, 0), 0.0) + xb
            xa = xa * jnp.where(keep, pltpu.roll(xa, shift, 0), 1.0)
            d *= 2
        out[g] = xb + xa * carry
        carry = out[g][edge]
    return jnp.concatenate(out, axis=0)


def _scan_fwd(a, b, row):
    return _scan_rows(a, b, row, False)


def _scan_bwd(c, b, row):
    return _scan_rows(c, b, row, True)


def _conv_fwd(x, cw, cb, row):
    return (cb + cw[3:4] * x + cw[2:3] * _shift_dn(x, 1, row) + cw[1:2] * _shift_dn(x, 2, row)
            + cw[0:1] * _shift_dn(x, 3, row))


def _lru_specs(s, order):
    def im(f):
        return (lambda b, g: f(b, g)) if order == "bg" else (lambda g, b: f(b, g))
    seq = lambda off: pl.BlockSpec((s, 128), im(lambda b, g: (b, off + g)))
    vec = pl.BlockSpec((1, 128), im(lambda b, g: (0, g)))
    cw = pl.BlockSpec((4, 128), im(lambda b, g: (0, g)))
    mat = pl.BlockSpec((1, 128, 128), im(lambda b, g: (g, 0, 0)))
    return seq, vec, cw, mat


def _lru_fwd(proj, conv_w, conv_b, w_a, b_a, w_i, b_i, lam, nb, s):
    def body(x_ref, gt_ref, cw_ref, cb_ref, wa_ref, ba_ref, wi_ref, bi_ref, lam_ref, out_ref):
        row = lax.broadcasted_iota(jnp.int32, (s, 128), 0)
        xc = _conv_fwd(x_ref[...], cw_ref[...], cb_ref[...], row)
        xcb = xc.astype(bf16)
        pa = _nn(xcb, wa_ref[0].astype(bf16)) + ba_ref[...]
        pi = _nn(xcb, wi_ref[0].astype(bf16)) + bi_ref[...]
        a, bx = _lru_gates(xc, pa, pi, lam_ref[...])
        h = _scan_fwd(a, bx, row)
        out_ref[...] = h * jax.nn.gelu(gt_ref[...])

    seq, vec, cw, mat = _lru_specs(s, "bg")
    return pl.pallas_call(
        body, name="lru_fwd", grid=(nb, LRU_BLOCKS),
        in_specs=[seq(16), seq(20), cw, vec, mat, vec, mat, vec, vec],
        out_specs=seq(0),
        out_shape=jax.ShapeDtypeStruct((nb * s, LRU_WIDTH), f32),
        compiler_params=_cp("parallel", "parallel"),
    )(proj, proj, conv_w, conv_b, w_a, b_a, w_i, b_i, lam)


def _lru_bwd(dmerged, proj, conv_w, conv_b, w_a, b_a, w_i, b_i, lam, nb, s):
    def body(dout_ref, x_ref, gt_ref, cw_ref, cb_ref, wa_ref, ba_ref, wi_ref, bi_ref, lam_ref,
             dx_ref, dgt_ref, dcw_ref, dcb_ref, dwa_ref, dba_ref, dwi_ref, dbi_ref, dlam_ref):
        row = lax.broadcasted_iota(jnp.int32, (s, 128), 0)
        x = x_ref[...]
        cwv = cw_ref[...]
        xc = _conv_fwd(x, cwv, cb_ref[...], row)
        xcb = xc.astype(bf16)
        wab, wib = wa_ref[0].astype(bf16), wi_ref[0].astype(bf16)
        pa = _nn(xcb, wab) + ba_ref[...]
        pi = _nn(xcb, wib) + bi_ref[...]
        (a, bx), gates_vjp = jax.vjp(_lru_gates, xc, pa, pi, lam_ref[...])
        h = _scan_fwd(a, bx, row)
        ge, gelu_vjp = jax.vjp(jax.nn.gelu, gt_ref[...])
        dout = dout_ref[...]
        dgt_ref[...] = gelu_vjp(dout * h)[0]
        adj = _scan_bwd(_shift_up(a, 1, row), dout * ge, row)
        dxc, dpa, dpi, dlam = gates_vjp((adj * _shift_dn(h, 1, row), adj))
        dpab, dpib = dpa.astype(bf16), dpi.astype(bf16)
        dxc = dxc + _nt(dpab, wab) + _nt(dpib, wib)
        dx_ref[...] = (cwv[3:4] * dxc + cwv[2:3] * _shift_up(dxc, 1, row) + cwv[1:2] * _shift_up(dxc, 2, row)
                       + cwv[0:1] * _shift_up(dxc, 3, row))

        @pl.when(pl.program_id(1) == 0)
        def _():
            for r in (dcw_ref, dcb_ref, dwa_ref, dba_ref, dwi_ref, dbi_ref, dlam_ref):
                r[...] = jnp.zeros_like(r)
        rsum = lambda v: jnp.sum(v, axis=0, keepdims=True)
        dcw_ref[...] += jnp.concatenate([rsum(dxc * _shift_dn(x, 3, row)), rsum(dxc * _shift_dn(x, 2, row)),
                                         rsum(dxc * _shift_dn(x, 1, row)), rsum(dxc * x)], axis=0)
        dcb_ref[...] += rsum(dxc)
        dwa_ref[0] += _tn(xcb, dpab)
        dwi_ref[0] += _tn(xcb, dpib)
        dba_ref[...] += rsum(dpa)
        dbi_ref[...] += rsum(dpi)
        dlam_ref[...] += dlam

    seq, vec, cw, mat = _lru_specs(s, "gb")
    t = nb * s
    vshape = jax.ShapeDtypeStruct((1, LRU_WIDTH), f32)
    mshape = jax.ShapeDtypeStruct((LRU_BLOCKS, 128, 128), f32)
    return pl.pallas_call(
        body, name="lru_bwd", grid=(LRU_BLOCKS, nb),
        in_specs=[seq(4), seq(16), seq(20), cw, vec, mat, vec, mat, vec, vec],
        out_specs=[seq(0), seq(0), cw, vec, mat, vec, mat, vec, vec],
        out_shape=[jax.ShapeDtypeStruct((t, LRU_WIDTH), f32), jax.ShapeDtypeStruct((t, LRU_WIDTH), f32),
                   jax.ShapeDtypeStruct((4, LRU_WIDTH), f32), vshape, mshape, vshape, mshape, vshape, vshape],
        compiler_params=_cp("parallel", "arbitrary"),
    )(dmerged, proj, proj, conv_w, conv_b, w_a, b_a, w_i, b_i, lam)


def _s5_disc(lr, li, ldt, bre, bim):
    dt = jnp.exp(ldt)
    mag = jnp.exp(lr * dt)
    lbr = mag * jnp.cos(li * dt)
    lbi = mag * jnp.sin(li * dt)
    den = lr * lr + li * li
    nr = lbr - 1.0
    fr = (nr * lr + lbi * li) / den
    fi = (lbi * lr - nr * li) / den
    bbr = fr[:, None, :] * bre - fi[:, None, :] * bim
    bbi = fr[:, None, :] * bim + fi[:, None, :] * bre
    return lbr, lbi, bbr, bbi


def _s5_prep(lr, li, ldt, bre, bim):
    def body(lr_ref, li_ref, ldt_ref, bre_ref, bim_ref, o1, o2, o3, o4):
        o1[...], o2[...], o3[...], o4[...] = _s5_disc(lr_ref[...], li_ref[...], ldt_ref[...], bre_ref[...], bim_ref[...])

    return pl.pallas_call(
        body, name="s5_prep", in_specs=[VMEM_SPEC] * 5, out_specs=[VMEM_SPEC] * 4,
        out_shape=[jax.ShapeDtypeStruct(lr.shape, f32), jax.ShapeDtypeStruct(lr.shape, f32),
                   jax.ShapeDtypeStruct(bre.shape, f32), jax.ShapeDtypeStruct(bre.shape, f32)],
    )(lr, li, ldt, bre, bim)


def _s5_prep_bwd(lr, li, ldt, bre, bim, cts):
    def body(lr_ref, li_ref, ldt_ref, bre_ref, bim_ref, g1, g2, g3, g4, o1, o2, o3, o4, o5):
        _, vjp = jax.vjp(_s5_disc, lr_ref[...], li_ref[...], ldt_ref[...], bre_ref[...], bim_ref[...])
        o1[...], o2[...], o3[...], o4[...], o5[...] = vjp((g1[...], g2[...], g3[...], g4[...]))

    return pl.pallas_call(
        body, name="s5_prep_bwd", in_specs=[VMEM_SPEC] * 9, out_specs=[VMEM_SPEC] * 5,
        out_shape=[jax.ShapeDtypeStruct(v.shape, f32) for v in (lr, li, ldt, bre, bim)],
    )(lr, li, ldt, bre, bim, *cts)


def _cmul(ar, ai, br, bi):
    return ar * br - ai * bi, ar * bi + ai * br


def _s5_pow_table(lr, li, n, row, up):
    ar = jnp.broadcast_to(lr, (n, lr.shape[1]))
    ai = jnp.broadcast_to(li, (n, li.shape[1]))
    shift = _shift_up if up else _shift_dn
    d = 1
    while d < n:
        ar, ai = _cmul(ar, ai, shift(ar, d, row, 1.0), shift(ai, d, row, 0.0))
        d *= 2
    return ar, ai


def _s5_step_factors(lr, li, row, up):
    sub = row & (SUBLANES - 1)
    out, pr, pi, d = [], lr, li, 1
    while d < SUBLANES:
        keep = (sub < SUBLANES - d) if up else (sub >= d)
        out.append((jnp.where(keep, pr, 0.0), jnp.where(keep, pi, 0.0)))
        pr, pi = _cmul(pr, pi, pr, pi)
        d *= 2
    return out


def _s5_scan(br, bi, steps, tab_r, tab_i, cr, ci, up):
    groups = list(range(br.shape[0] // SUBLANES))
    out_r, out_i = [None] * len(groups), [None] * len(groups)
    edge = slice(0, 1) if up else slice(SUBLANES - 1, SUBLANES)
    for g in (reversed(groups) if up else groups):
        rows = slice(g * SUBLANES, (g + 1) * SUBLANES)
        xr, xi = br[rows], bi[rows]
        for k, (mr, mi) in enumerate(steps):
            shift = SUBLANES - (1 << k) if up else 1 << k
            tr, ti = _cmul(mr, mi, pltpu.roll(xr, shift, 0), pltpu.roll(xi, shift, 0))
            xr, xi = xr + tr, xi + ti
        tr, ti = _cmul(tab_r, tab_i, cr, ci)
        hr, hi = xr + tr, xi + ti
        out_r[g], out_i[g] = hr, hi
        cr, ci = hr[edge], hi[edge]
    return jnp.concatenate(out_r, axis=0), jnp.concatenate(out_i, axis=0)


def _s5_specs(t, nb, nc):
    seq = pl.BlockSpec((t, 128), lambda k: (0, k))
    lvec = pl.BlockSpec((1, S5_BLOCK_STATES), lambda k: (0, k))
    dvec = pl.BlockSpec((1, 128), lambda k: (0, k))
    wmat = pl.BlockSpec((1, 128, S5_BLOCK_STATES), lambda k: (k, 0, 0))
    h0 = pl.BlockSpec((nb, None, nc, 2, S5_BLOCK_STATES), lambda k: (0, k, 0, 0, 0))
    return seq, lvec, dvec, wmat, h0


def _s5_fwd(u, lbr, lbi, wbr, wbi, wcr, wci, dskip, nb, s):
    ln = S5_CHUNK
    nc = s // ln

    def body(u_ref, lr_ref, li_ref, wbr_ref, wbi_ref, wcr_ref, wci_ref, d_ref, yg_ref, y_ref, h0_ref):
        row = lax.broadcasted_iota(jnp.int32, (ln, S5_BLOCK_STATES), 0)
        lr, li = lr_ref[...], li_ref[...]
        pr, pi = _s5_pow_table(lr, li, SUBLANES, row[:SUBLANES], False)
        steps = _s5_step_factors(lr, li, row[:SUBLANES], False)
        dv = d_ref[...]

        def chunk(b, n, h0r, h0i):
            st = pl.multiple_of(b * s + n * ln, ln)
            uc = u_ref[pl.ds(st, ln), :]
            ub = uc.astype(bf16)
            hr, hi = _s5_scan(_nn(ub, wbr_ref[0]), _nn(ub, wbi_ref[0]), steps, pr, pi, h0r, h0i, False)
            h0_ref[b, n, 0:1, :] = h0r
            h0_ref[b, n, 1:2, :] = h0i
            y = _nt(hr.astype(bf16), wcr_ref[0]) - _nt(hi.astype(bf16), wci_ref[0]) + dv * uc
            y_ref[pl.ds(st, ln), :] = y
            yg_ref[pl.ds(st, ln), :] = jax.nn.gelu(y).astype(bf16)
            return hr[ln - 1:ln, :], hi[ln - 1:ln, :]

        def step(n, carry):
            return tuple(chunk(b, n, *carry[b]) for b in range(nb))

        z = jnp.zeros((1, S5_BLOCK_STATES), f32)
        lax.fori_loop(0, nc, step, ((z, z),) * nb)

    t = nb * s
    seq, lvec, dvec, wmat, h0 = _s5_specs(t, nb, nc)
    return pl.pallas_call(
        body, name="s5_fwd", grid=(S5_BLOCKS,),
        in_specs=[seq, lvec, lvec, wmat, wmat, wmat, wmat, dvec],
        out_specs=[seq, seq, h0],
        out_shape=[jax.ShapeDtypeStruct((t, D_MODEL), bf16), jax.ShapeDtypeStruct((t, D_MODEL), f32),
                   jax.ShapeDtypeStruct((nb, S5_BLOCKS, nc, 2, S5_BLOCK_STATES), f32)],
        compiler_params=_cp("parallel"),
    )(u, lbr, lbi, wbr, wbi, wcr, wci, dskip)


def _s5_bwd(dyg, y, u, h0, lbr, lbi, wbr, wbi, wcr, wci, dskip, nb, s):
    ln = S5_CHUNK
    nc = s // ln

    def body(dyg_ref, y_ref, u_ref, h0_ref, lr_ref, li_ref, wbr_ref, wbi_ref, wcr_ref, wci_ref, d_ref,
             du_ref, dlr_ref, dli_ref, dwbr_ref, dwbi_ref, dwcr_ref, dwci_ref, dd_ref):
        for r in (dlr_ref, dli_ref, dwbr_ref, dwbi_ref, dwcr_ref, dwci_ref, dd_ref):
            r[...] = jnp.zeros_like(r)
        row = lax.broadcasted_iota(jnp.int32, (ln, S5_BLOCK_STATES), 0)
        lr, li = lr_ref[...], li_ref[...]
        pr, pi = _s5_pow_table(lr, li, SUBLANES, row[:SUBLANES], False)
        qr, qi = _s5_pow_table(lr, -li, SUBLANES, row[:SUBLANES], True)
        row8 = row[:SUBLANES]
        steps_dn, steps_up = _s5_step_factors(lr, li, row8, False), _s5_step_factors(lr, -li, row8, True)
        dv = d_ref[...]
        rsum = lambda v: jnp.sum(v, axis=0, keepdims=True)

        def chunk(b, n, gnr, gni):
            st = pl.multiple_of(b * s + n * ln, ln)
            uc = u_ref[pl.ds(st, ln), :]
            ub = uc.astype(bf16)
            h0v = h0_ref[b, n]
            h0r, h0i = h0v[0:1], h0v[1:2]
            hr, hi = _s5_scan(_nn(ub, wbr_ref[0]), _nn(ub, wbi_ref[0]), steps_dn, pr, pi, h0r, h0i, False)
            dy = jax.vjp(jax.nn.gelu, y_ref[pl.ds(st, ln), :])[1](dyg_ref[pl.ds(st, ln), :])[0]
            dyb = dy.astype(bf16)
            dd_ref[...] += rsum(dy * uc)
            gr, gi = _s5_scan(_nn(dyb, wcr_ref[0]), -_nn(dyb, wci_ref[0]), steps_up, qr, qi, gnr, gni, True)
            hpr = jnp.where(row >= 1, pltpu.roll(hr, 1, 0), h0r)
            hpi = jnp.where(row >= 1, pltpu.roll(hi, 1, 0), h0i)
            dlr_ref[...] += rsum(gr * hpr + gi * hpi)
            dli_ref[...] += rsum(gi * hpr - gr * hpi)
            grb, gib = gr.astype(bf16), gi.astype(bf16)
            dwbr_ref[0] += _tn(ub, grb)
            dwbi_ref[0] += _tn(ub, gib)
            dwcr_ref[0] += _tn(dyb, hr.astype(bf16))
            dwci_ref[0] -= _tn(dyb, hi.astype(bf16))
            du_ref[pl.ds(st, ln), :] = _nt(grb, wbr_ref[0]) + _nt(gib, wbi_ref[0]) + dv * dy
            return gr[0:1, :], gi[0:1, :]

        def step(i, carry):
            return tuple(chunk(b, nc - 1 - i, *carry[b]) for b in range(nb))

        z = jnp.zeros((1, S5_BLOCK_STATES), f32)
        lax.fori_loop(0, nc, step, ((z, z),) * nb)

    t = nb * s
    seq, lvec, dvec, wmat, h0s = _s5_specs(t, nb, nc)
    lshape = jax.ShapeDtypeStruct((1, S5_BLOCKS * S5_BLOCK_STATES), f32)
    wshape = jax.ShapeDtypeStruct((S5_BLOCKS, 128, S5_BLOCK_STATES), f32)
    return pl.pallas_call(
        body, name="s5_bwd", grid=(S5_BLOCKS,),
        in_specs=[seq, seq, seq, h0s, lvec, lvec, wmat, wmat, wmat, wmat, dvec],
        out_specs=[seq, lvec, lvec, wmat, wmat, wmat, wmat, dvec],
        out_shape=[jax.ShapeDtypeStruct((t, D_MODEL), f32), lshape, lshape, wshape, wshape, wshape, wshape,
                   jax.ShapeDtypeStruct((1, D_MODEL), f32)],
        compiler_params=_cp("parallel"),
    )(dyg, y, u, h0, lbr, lbi, wbr, wbi, wcr, wci, dskip)


def _blockdiag(w):
    w4 = w.reshape(S5_BLOCKS, 8, S5_GROUP, S5_STATE)
    same_group = jnp.eye(8, dtype=bool)[None, :, None, :, None]
    return jnp.where(same_group, w4[:, :, :, None, :], 0.0).reshape(S5_BLOCKS, 128, S5_BLOCK_STATES)


def _blockdiag_t(dw):
    d5 = dw.reshape(S5_BLOCKS, 8, S5_GROUP, 8, S5_STATE)
    diag = jnp.diagonal(d5, axis1=1, axis2=3)
    return jnp.moveaxis(diag, 3, 1).reshape(S5_GROUPS, S5_GROUP, S5_STATE)


def _glu_fwd(ygb, wa, wb, x, tm=512, tn=512):
    t, d = x.shape

    def body(y_ref, wa_ref, wb_ref, x_ref, o_ref, p_ref, q_ref):
        p = _nn(y_ref[...], wa_ref[...])
        q = _nn(y_ref[...], wb_ref[...])
        p_ref[...] = p
        q_ref[...] = q
        o_ref[...] = x_ref[...] + p * jax.nn.sigmoid(q)

    tile = pl.BlockSpec((tm, tn), lambda i, j: (i, j))
    wsp = pl.BlockSpec((d, tn), lambda i, j: (0, j))
    out = jax.ShapeDtypeStruct((t, d), f32)
    return pl.pallas_call(
        body, name="glu_fwd", grid=(t // tm, d // tn),
        in_specs=[pl.BlockSpec((tm, d), lambda i, j: (i, 0)), wsp, wsp, tile],
        out_specs=[tile, tile, tile], out_shape=[out, out, out],
        compiler_params=_cp("parallel", "parallel"),
    )(ygb, wa, wb, x)


def _place():
    x, y, c = lax.axis_index("x"), lax.axis_index("y"), lax.axis_index("c")
    return x, y, c, [(1 - x, y), (x, 1 - y), (1 - x, 1 - y)]


def _all_gather(name, arrays):
    n = len(arrays)

    def body(*refs):
        ins, outs = refs[:n], refs[n:2 * n]
        send_sems, recv_sems, local_sems = refs[2 * n:]
        x, y, c, chips = _place()
        me, sib = (x, y, c), (x, y, 1 - c)

        def copy(i, k, block, to, src=None):
            dst = outs[i].at[4 * block[0] + 2 * block[1] + block[2]]
            return pltpu.make_async_remote_copy(
                src_ref=dst if src is None else src, dst_ref=dst,
                send_sem=send_sems.at[i * 7 + k], recv_sem=recv_sems.at[i * 7 + k],
                device_id=to, device_id_type=MESH)

        mine = [pltpu.make_async_copy(ins[i], outs[i].at[4 * x + 2 * y + c], local_sems.at[i]) for i in range(n)]
        for m in mine:
            m.start()
        first = []
        for i in range(n):
            first.append(copy(i, 0, me, sib, src=ins[i]))
            first += [copy(i, 1 + j, me, (*chip, c), src=ins[i]) for j, chip in enumerate(chips)]
        for cp in first:
            cp.start()
        passed = []
        for j, chip in enumerate(chips):
            for i in range(n):
                copy(i, 1 + j, (*chip, c), me).wait_recv()
                fwd = copy(i, 4 + j, (*chip, c), sib)
                fwd.start()
                passed.append(fwd)
        for i in range(n):
            copy(i, 0, sib, me).wait_recv()
        for j, chip in enumerate(chips):
            for i in range(n):
                copy(i, 4 + j, (*chip, 1 - c), me).wait_recv()
        for cp in first + passed:
            cp.wait_send()
        for m in mine:
            m.wait()

    return pl.pallas_call(
        body, name=name,
        in_specs=[ANY_SPEC] * n, out_specs=[ANY_SPEC] * n,
        out_shape=[jax.ShapeDtypeStruct((N_DEV,) + a.shape, a.dtype) for a in arrays],
        scratch_shapes=[pltpu.SemaphoreType.DMA((7 * n,)), pltpu.SemaphoreType.DMA((7 * n,)),
                        pltpu.SemaphoreType.DMA((n,))],
    )(*arrays)


def _tie(name, x, deps):
    def body(*refs):
        pass

    return pl.pallas_call(
        body, name=name, in_specs=[ANY_SPEC] * (1 + len(deps)), out_specs=ANY_SPEC,
        out_shape=jax.ShapeDtypeStruct(x.shape, x.dtype), input_output_aliases={0: 0},
    )(x, *deps)


def _xchg_copies(kind, srcs, lands, suffixes, send_sems, recv_sems):
    x, y, c, _ = _place()
    copies = []
    for i, (src, land, sfx) in enumerate(zip(srcs, lands, suffixes)):
        for k in range(N_DEV - 1):
            r = k + 1
            peer = (1 - x if r & 4 else x, 1 - y if r & 2 else y, 1 - c if r & 1 else c)
            if kind == "gather":
                s_ref, d_ref = src, land.at[(4 * x + 2 * y + c,) + sfx]
            else:
                s_ref, d_ref = src.at[4 * peer[0] + 2 * peer[1] + peer[2]], land.at[(k,) + sfx]
            copies.append(pltpu.make_async_remote_copy(
                src_ref=s_ref, dst_ref=d_ref, send_sem=send_sems.at[i * 7 + k], recv_sem=recv_sems.at[i * 7 + k],
                device_id=peer, device_id_type=MESH))
    return copies


def _xchg_start(name, kind, srcs, lands, suffixes=None):
    n = len(srcs)
    suffixes = suffixes or [()] * n

    def body(*refs):
        src, land = refs[:n], refs[n:2 * n]
        send_sems, recv_sems, token = refs[2 * n], refs[2 * n + 1], refs[-1]
        for cp in _xchg_copies(kind, src, land, suffixes, send_sems, recv_sems):
            cp.start()
        token[...] = jnp.zeros_like(token)

    arrays = list(srcs) + list(lands)
    outs = pl.pallas_call(
        body, name=name,
        out_shape=(pltpu.SemaphoreType.DMA((7 * n,)), pltpu.SemaphoreType.DMA((7 * n,)),
                   *[pltpu.HBM(a.shape, a.dtype) for a in arrays], jax.ShapeDtypeStruct((8, 128), f32)),
        in_specs=[HBM_SPEC] * (2 * n),
        out_specs=(SEM_SPEC, SEM_SPEC, *[HBM_SPEC] * (2 * n), VMEM_SPEC),
        input_output_aliases={i: 2 + i for i in range(2 * n)},
        compiler_params=pltpu.CompilerParams(has_side_effects=SIDE_EFFECT),
    )(*[pltpu.with_memory_space_constraint(a, pltpu.HBM) for a in arrays])
    return dict(kind=kind, n=n, suffixes=suffixes, send=outs[0], recv=outs[1], srcs=list(outs[2:2 + n]),
                lands=list(outs[2 + n:2 + 2 * n]), token=outs[-1])


def _xchg_wait(name, h, after, lands=None):
    n = h["n"]
    lands = h["lands"] if lands is None else lands

    def body(*refs):
        src, land = refs[:n], refs[n:2 * n]
        for cp in _xchg_copies(h["kind"], src, land, h["suffixes"], refs[2 * n], refs[2 * n + 1]):
            cp.wait_send()
            cp.wait_recv()

    arrays = h["srcs"] + list(lands)
    outs = pl.pallas_call(
        body, name=name,
        out_shape=tuple(pltpu.HBM(a.shape, a.dtype) for a in arrays),
        in_specs=[HBM_SPEC] * (2 * n) + [SEM_SPEC, SEM_SPEC] + [ANY_SPEC] * len(after),
        out_specs=tuple([HBM_SPEC] * (2 * n)),
        input_output_aliases={i: i for i in range(2 * n)},
        compiler_params=pltpu.CompilerParams(has_side_effects=SIDE_EFFECT),
    )(*arrays, h["send"], h["recv"], *after)
    return list(outs[n:])


def _rows(a):
    return a.reshape(-1, a.shape[-1])


def _row_tile(r):
    for tm in (512, 256, 128, 64, 32, 16, 8):
        if r % tm == 0:
            return tm
    return r


def _sum8(name, gathered):
    _, r, n = gathered.shape
    tm = _row_tile(r)

    def body(g_ref, o_ref):
        acc = g_ref[0]
        for k in range(1, N_DEV):
            acc = acc + g_ref[k]
        o_ref[...] = acc

    return pl.pallas_call(
        body, name=name, grid=(r // tm,),
        in_specs=[pl.BlockSpec((N_DEV, tm, n), lambda i: (0, i, 0))],
        out_specs=pl.BlockSpec((tm, n), lambda i: (i, 0)),
        out_shape=jax.ShapeDtypeStruct((r, n), f32),
        compiler_params=_cp("parallel"),
    )(gathered)


def _adamw(name, w, m, v, own, landed=None):
    shape = w.shape
    w2, m2, v2, o2 = _rows(w), _rows(m), _rows(v), _rows(own)
    r, n = w2.shape
    tm = _row_tile(r)
    c1 = 1.0 - ADAM_B1 ** ADAM_STEP
    c2 = 1.0 - ADAM_B2 ** ADAM_STEP
    extra = [] if landed is None else [landed.reshape(landed.shape[0], r, n)]

    def body(w_ref, m_ref, v_ref, o_ref, *refs):
        g = o_ref[...]
        if extra:
            for k in range(extra[0].shape[0]):
                g = g + refs[0][k].astype(f32)
        g_ref, d_ref, mn_ref, vn_ref = refs[len(extra):]
        mn = ADAM_B1 * m_ref[...] + (1.0 - ADAM_B1) * g
        vn = ADAM_B2 * v_ref[...] + (1.0 - ADAM_B2) * (g * g)
        g_ref[...] = g
        d_ref[...] = -ADAM_LR * ((mn / c1) / (jnp.sqrt(vn / c2) + ADAM_EPS) + ADAM_WD * w_ref[...])
        mn_ref[...] = mn
        vn_ref[...] = vn

    row = pl.BlockSpec((tm, n), lambda i: (i, 0))
    outs = pl.pallas_call(
        body, name=name, grid=(r // tm,),
        in_specs=[row] * 4 + [pl.BlockSpec((e.shape[0], tm, n), lambda i: (0, i, 0)) for e in extra],
        out_specs=[row] * 4, out_shape=[jax.ShapeDtypeStruct((r, n), f32)] * 4,
        compiler_params=_cp("parallel"),
    )(w2, m2, v2, o2, *extra)
    return [o.reshape(shape) for o in outs]


def _pack(arrays):
    flat = jnp.concatenate([a.reshape(-1).astype(f32) for a in arrays])
    pad = (-flat.shape[0]) % (128 * (512 if flat.shape[0] > 128 * 512 else 8))
    return jnp.pad(flat, (0, pad)).reshape(-1, 128)


def _unpack(packed, shapes):
    flat = packed.reshape(-1)
    out, off = [], 0
    for s in shapes:
        n = math.prod(s)
        out.append(flat[off:off + n].reshape(s))
        off += n
    return out


def _local_step(x, target, w, weights_of, send, last_small, nb, s):
    cos, sin = _rope_tables(s)
    g = {}
    ffn_saved = {}
    ffn_bufs = [lax.empty((N_DEV, 2, 2) + shp, f32)
                for shp in ((D_MODEL, FF_SHARD), (D_MODEL, FF_SHARD), (FF_SHARD, D_MODEL))]

    def ffn(xin, l, h, wts):
        y, a, b = _ffn_fwd(f"ffn_fwd_{l}{h}", xin, w["ffn_g"][l][h], *wts)
        ffn_saved[(l, h)] = (xin, a, b, wts)
        return y

    def ffn_back(dy, l, h):
        xin, a, b, wts = ffn_saved[(l, h)]
        dx, dg, hb, dyh, u, da, db = _ffn_dx(f"ffn_dx_{l}{h}", dy, xin, w["ffn_g"][l][h], *wts, a, b)
        g[f"ffn_g_{l}{h}"] = dg
        if (l, h) == (0, 0):
            hb = last_small(g, hb)
        ffn_bufs[0], half = _ffn_dw_one(f"ffn_dw_{l}{h}_w1", hb, da, ffn_bufs[0], l, h)
        hb = send(f"ffn_{l}{h}_w1", {"ffn_w1": half}, hb)
        ffn_bufs[1], half = _ffn_dw_one(f"ffn_dw_{l}{h}_w3", hb, db, ffn_bufs[1], l, h)
        u = send(f"ffn_{l}{h}_w3", {"ffn_w3": half}, u)
        ffn_bufs[2], half = _ffn_dw_one(f"ffn_dw_{l}{h}_w2", u, dyh, ffn_bufs[2], l, h)
        return send(f"ffn_{l}{h}_w2", {"ffn_w2": half}, dx)

    def slots(t):
        return t.reshape(N_DEV, D_MODEL // N_DEV, D_MODEL)

    x1 = ffn(x, 0, 0, weights_of(0, [])["ffn"])
    wg = weights_of(1, [x1])
    w_in, w_out = wg["w_in"], wg["w_out"]
    _, h0b = _norm_fwd("mix_norm_0", x1, w["mix_g"][0])
    proj = _mm("in_proj", h0b, w_in, "nn", tn=768)[0]
    o_raw, rprev, mret = _ret_fwd(proj, cos, sin, w["ret_g"], nb, s)
    lru = _lru_fwd(proj, w["conv_w"], w["conv_b"], w["lru_w_a"], w["lru_b_a"], w["lru_w_i"], w["lru_b_i"], w["lru_lam"], nb, s)
    merged = _ew("merge", lambda a, b: (jnp.concatenate([a, b], axis=1),), [mret, lru], [(D_MODEL, bf16)])[0]
    x2 = _mm("out_proj", merged, w_out, "nn", extras=[x1], epilogue=lambda acc, r: (acc + r,))[0]
    x3 = ffn(x2, 0, 1, weights_of(2, [x2])["ffn"])
    wg = weights_of(3, [x3])
    glu_a, glu_b = wg["glu_a"], wg["glu_b"]
    x4 = ffn(x3, 1, 0, wg["ffn"])
    u, _ = _norm_fwd("mix_norm_1", x4, w["mix_g"][1])
    lbr, lbi, bbr, bbi = _s5_prep(w["s5_lr"], w["s5_li"], w["s5_ldt"], w["s5_bre"], w["s5_bim"])
    lbr_f, lbi_f = lbr.reshape(1, -1), lbi.reshape(1, -1)
    wbr, wbi = _blockdiag(bbr).astype(bf16), _blockdiag(bbi).astype(bf16)
    wcr, wci = _blockdiag(w["s5_cre"]).astype(bf16), _blockdiag(w["s5_cim"]).astype(bf16)
    ygb, ypre, h0s = _s5_fwd(u, lbr_f, lbi_f, wbr, wbi, wcr, wci, w["s5_d"], nb, s)
    x5, gp, gq = _glu_fwd(ygb, glu_a, glu_b, x4)
    x6 = ffn(x5, 1, 1, weights_of(4, [x5])["ffn"])
    loss, dx6, g["final_g"] = _final_loss(x6, w["final_g"], target)

    dx5 = ffn_back(dx6, 1, 1)

    def glu_bwd(d, p, q):
        sg = jax.nn.sigmoid(q)
        return d * sg, d * p * sg * (1.0 - sg)

    dp, dq = _ew("glu_bwd", glu_bwd, [dx5, gp, gq], [(D_MODEL, bf16), (D_MODEL, bf16)])
    dyg = _mm("glu_dy_a", dp, glu_a, "nt")[0]
    dyg = _mm("glu_dy_b", dq, glu_b, "nt", extras=[dyg], epilogue=lambda acc, r: (acc + r,))[0]
    g["glu_a"], ga_half = _mm_tn("glu_dw_a", ygb, dp)
    g["glu_b"], gb_half = _mm_tn("glu_dw_b", ygb, dq)
    dyg = send("glu", {"glu_a": slots(ga_half), "glu_b": slots(gb_half)}, dyg)
    du, dlr, dli, dwbr, dwbi, dwcr, dwci, g["s5_d"] = _s5_bwd(dyg, ypre, u, h0s, lbr_f, lbi_f, wbr, wbi, wcr, wci, w["s5_d"], nb, s)
    g["s5_cre"], g["s5_cim"] = _blockdiag_t(dwcr), _blockdiag_t(dwci)
    g["s5_lr"], g["s5_li"], g["s5_ldt"], g["s5_bre"], g["s5_bim"] = _s5_prep_bwd(
        w["s5_lr"], w["s5_li"], w["s5_ldt"], w["s5_bre"], w["s5_bim"],
        (dlr.reshape(S5_GROUPS, S5_STATE), dli.reshape(S5_GROUPS, S5_STATE), _blockdiag_t(dwbr), _blockdiag_t(dwbi)))
    dx4, g["mix_g_1"] = _norm_bwd("mix_norm_1_bwd", du, x4, w["mix_g"][1], dx5)
    dx3 = ffn_back(dx4, 1, 0)
    dx2 = ffn_back(dx3, 0, 1)
    dmerged = _mm("out_proj_dx", dx2, w_out, "nt")[0]
    g["w_out"], wo_half = _mm_tn("out_proj_dw", merged, dx2)
    dmerged = send("w_out", {"w_out": slots(wo_half)}, dmerged)
    dq_, dk_, dv_, dgate, g["ret_g"] = _ret_bwd(dmerged, o_raw, rprev, proj, cos, sin, w["ret_g"], nb, s)
    (dxl, dgl, g["conv_w"], g["conv_b"], g["lru_w_a"], g["lru_b_a"], g["lru_w_i"], g["lru_b_i"], g["lru_lam"]) = _lru_bwd(
        dmerged, proj, w["conv_w"], w["conv_b"], w["lru_w_a"], w["lru_b_a"], w["lru_w_i"], w["lru_b_i"], w["lru_lam"], nb, s)
    dproj = _ew("dproj", lambda *p: (jnp.concatenate(p, axis=1),), [dq_, dk_, dv_, dgate, dxl, dgl], [(3072, bf16)])[0]
    dh0 = _mm("in_proj_dx", dproj, w_in, "nt")[0]
    g["w_in"], wi_half = _mm_tn("in_proj_dw", h0b, dproj)
    dh0 = send("w_in", {"w_in": jnp.transpose(wi_half.reshape(D_MODEL, N_DEV, IN_SHARD), (1, 0, 2))}, dh0)
    dx1, g["mix_g_0"] = _norm_bwd("mix_norm_0_bwd", dh0, x1, w["mix_g"][0], dx2)
    dx0 = ffn_back(dx1, 0, 0)
    g["ffn_w1"], g["ffn_w3"], g["ffn_w2"] = ffn_bufs
    return loss, dx0, g


_WEIGHTS = ["ffn_norm_g", "ffn_w1", "ffn_w3", "ffn_w2", "mix_norm_g", "w_in_even", "w_out_even", "ret_norm_g", "conv_w",
            "conv_b", "lru_w_a", "lru_b_a", "lru_w_i", "lru_b_i", "lru_lambda", "s5_lambda_re", "s5_lambda_im", "s5_log_dt",
            "s5_b_re", "s5_b_im", "s5_c_re", "s5_c_im", "s5_d", "glu_w_a", "glu_w_b", "final_norm_g"]
_BIG = ["ffn_w1", "ffn_w3", "ffn_w2", "w_in_even", "w_out_even", "glu_w_a", "glu_w_b"]
_SMALL_SHARDED = ["ffn_norm_g", "conv_w", "s5_d"]
_SMALL = [n for n in _WEIGHTS if n not in _BIG]


def kernel(x, ffn_norm_g, ffn_w1, ffn_w3, ffn_w2, mix_norm_g, w_in_even, w_out_even, ret_norm_g, conv_w, conv_b, lru_w_a, lru_b_a, lru_w_i, lru_b_i, lru_lambda, s5_lambda_re, s5_lambda_im, s5_log_dt, s5_b_re, s5_b_im, s5_c_re, s5_c_im, s5_d, glu_w_a, glu_w_b, final_norm_g, loss_target, m_ffn_norm_g, m_ffn_w1, m_ffn_w3, m_ffn_w2, m_mix_norm_g, m_w_in_even, m_w_out_even, m_ret_norm_g, m_conv_w, m_conv_b, m_lru_w_a, m_lru_b_a, m_lru_w_i, m_lru_b_i, m_lru_lambda, m_s5_lambda_re, m_s5_lambda_im, m_s5_log_dt, m_s5_b_re, m_s5_b_im, m_s5_c_re, m_s5_c_im, m_s5_d, m_glu_w_a, m_glu_w_b, m_final_norm_g, v_ffn_norm_g, v_ffn_w1, v_ffn_w3, v_ffn_w2, v_mix_norm_g, v_w_in_even, v_w_out_even, v_ret_norm_g, v_conv_w, v_conv_b, v_lru_w_a, v_lru_b_a, v_lru_w_i, v_lru_b_i, v_lru_lambda, v_s5_lambda_re, v_s5_lambda_im, v_s5_log_dt, v_s5_b_re, v_s5_b_im, v_s5_c_re, v_s5_c_im, v_s5_d, v_glu_w_a, v_glu_w_b, v_final_norm_g):
    a = dict(locals())
    nb, s, d = x.shape
    ax, ay, ac = lax.axis_index("x"), lax.axis_index("y"), lax.axis_index("c")
    dev = 4 * ax + 2 * ay + ac
    chip = 2 * ax + ay

    def ffn_shards(l, h):
        extra = FF_PAD - FF_SHARD
        return [jnp.pad(ffn_w1[l, h].astype(bf16), ((0, 0), (0, extra))), jnp.pad(ffn_w3[l, h].astype(bf16), ((0, 0), (0, extra))),
                jnp.pad(ffn_w2[l, h].astype(bf16), ((0, extra), (0, 0)))]

    first = _all_gather("ag_first", ffn_shards(0, 0) + [_pack([ffn_norm_g, conv_w, s5_d])])
    sm = first[3].reshape(N_DEV, -1)
    ffn_g_full = jnp.transpose(sm[:, :512].reshape(N_DEV, 2, 2, 128), (1, 2, 0, 3)).reshape(2, 2, D_MODEL)
    conv_w_full = jnp.transpose(sm[:, 512:768].reshape(N_DEV, 4, 64), (1, 0, 2)).reshape(4, LRU_WIDTH)
    s5_d_full = sm[:, 768:896].reshape(1, D_MODEL)

    ag_src = [None, [w_in_even[0].astype(bf16), w_out_even[0].astype(bf16)], ffn_shards(0, 1),
              ffn_shards(1, 0) + [glu_w_a[0].astype(bf16), glu_w_b[0].astype(bf16)], ffn_shards(1, 1)]
    ag, token = [None], first[0]
    for k, grp in enumerate(ag_src):
        if grp is None:
            continue
        grp[0] = _tie(f"tie_ag_{k}", grp[0], [token])
        lands = [lax.dynamic_update_index_in_dim(lax.empty((N_DEV,) + t.shape, bf16), t, dev, 0) for t in grp]
        ag.append(_xchg_start(f"ag_start_{k}", "gather", grp, lands))
        token = ag[-1]["token"]

    def weights_of(k, after):
        if k == 0:
            return {"ffn": [first[0], _tie("tie_ag_started", first[1], [h["token"] for h in ag[1:]]), first[2]]}
        got = _xchg_wait(f"ag_wait_{k}", ag[k], after)
        if k == 1:
            return {"w_in": jnp.transpose(got[0], (1, 0, 2)).reshape(D_MODEL, N_DEV * IN_SHARD),
                    "w_out": got[1].reshape(D_MODEL, D_MODEL)}
        if k == 3:
            return {"ffn": got[:3], "glu_a": got[3].reshape(D_MODEL, D_MODEL), "glu_b": got[4].reshape(D_MODEL, D_MODEL)}
        return {"ffn": got}

    ffn_lands = [lax.empty((N_DEV - 1, 2, 2) + shp, bf16)
                 for shp in ((D_MODEL, FF_SHARD), (D_MODEL, FF_SHARD), (FF_SHARD, D_MODEL))]
    rs = []

    ffn_names = ("ffn_w1", "ffn_w3", "ffn_w2")

    def send(group, arrays, carry):
        srcs = list(arrays.values())
        if group.startswith("ffn_"):
            which = [ffn_names.index(n) for n in arrays]
            sfx = [(int(group[4]), int(group[5]))] * len(which)
            h = _xchg_start("rs_start_" + group, "scatter", srcs, [ffn_lands[k] for k in which], sfx)
            for k, land in zip(which, h["lands"]):
                ffn_lands[k] = land
        else:
            h = _xchg_start("rs_start_" + group, "scatter", srcs,
                            [lax.empty((N_DEV - 1,) + t.shape[1:], bf16) for t in srcs])
        rs.append((group, list(arrays), h))
        return _tie("tie_" + group, carry, [h["token"]])

    w = {
        "ffn_g": [[ffn_g_full[l, h].reshape(1, D_MODEL) for h in range(2)] for l in range(2)],
        "mix_g": [mix_norm_g[0:1], mix_norm_g[1:2]],
        "ret_g": ret_norm_g, "conv_w": conv_w_full, "conv_b": conv_b,
        "lru_w_a": lru_w_a[0], "lru_b_a": lru_b_a, "lru_w_i": lru_w_i[0], "lru_b_i": lru_b_i, "lru_lam": lru_lambda,
        "s5_lr": s5_lambda_re[0], "s5_li": s5_lambda_im[0], "s5_ldt": s5_log_dt.reshape(S5_GROUPS, 1),
        "s5_bre": jnp.swapaxes(s5_b_re[0], 1, 2), "s5_bim": jnp.swapaxes(s5_b_im[0], 1, 2),
        "s5_cre": s5_c_re[0], "s5_cim": s5_c_im[0], "s5_d": s5_d_full,
        "final_g": final_norm_g.reshape(1, D_MODEL),
    }

    small_grads = {}

    def last_small(g, carry):
        part = _small_partials(g)
        mine = _pack([part[n] for n in _SMALL])
        land = lax.dynamic_update_index_in_dim(lax.empty((N_DEV,) + mine.shape, f32), mine, dev, 0)
        h = _xchg_start("ag_start_small_grads", "gather", [mine], [land])
        small_grads.update(h=h, shapes=[part[n].shape for n in _SMALL])
        return _tie("tie_small_grads", carry, [h["token"]])

    loss_part, dx, g = _local_step(x.reshape(nb * s, d), loss_target.reshape(nb * s, d), w, weights_of, send, last_small,
                                   nb, s)
    loss = lax.psum(loss_part[0, 0], ("x", "y", "c"))
    (gath,) = _xchg_wait("ag_wait_small_grads", small_grads["h"], [dx])
    full = dict(zip(_SMALL, _unpack(_sum8("sum_small_grads", gath), small_grads["shapes"])))
    for n in _SMALL_SHARDED:
        width = a[n].shape[-1]
        full[n] = lax.dynamic_slice_in_dim(full[n], dev * width, width, axis=full[n].ndim - 1)
    shapes = [a[n].shape for n in _SMALL]
    packed = _adamw("adamw_small", _pack([a[n] for n in _SMALL]), _pack([a["m_" + n] for n in _SMALL]),
                    _pack([a["v_" + n] for n in _SMALL]), _pack([full[n] for n in _SMALL]))
    res = {n: vals for n, vals in zip(_SMALL, zip(*[_unpack(p, shapes) for p in packed]))}
    return _finish(a, g, dx, loss, res, packed, rs, ffn_lands, dev, nb, s, d)


def _small_partials(g):
    return {
        "ffn_norm_g": jnp.stack([jnp.stack([g[f"ffn_g_{l}{h}"][0] for h in range(2)]) for l in range(2)]),
        "mix_norm_g": jnp.concatenate([g["mix_g_0"], g["mix_g_1"]], axis=0),
        "ret_norm_g": g["ret_g"], "conv_w": g["conv_w"][None], "conv_b": g["conv_b"],
        "lru_w_a": g["lru_w_a"][None], "lru_b_a": g["lru_b_a"], "lru_w_i": g["lru_w_i"][None], "lru_b_i": g["lru_b_i"],
        "lru_lambda": g["lru_lam"], "s5_lambda_re": g["s5_lr"][None], "s5_lambda_im": g["s5_li"][None],
        "s5_log_dt": g["s5_ldt"].reshape(1, S5_GROUPS),
        "s5_b_re": jnp.swapaxes(g["s5_bre"], 1, 2)[None], "s5_b_im": jnp.swapaxes(g["s5_bim"], 1, 2)[None],
        "s5_c_re": g["s5_cre"][None], "s5_c_im": g["s5_cim"][None], "s5_d": g["s5_d"], "final_norm_g": g["final_g"][0],
    }


def _finish(a, g, dx, loss, res, packed, rs, ffn_lands, dev, nb, s, d):
    landed = {}
    for group, names, h in rs:
        if not group.startswith("ffn_"):
            landed.update(zip(names, _xchg_wait("rs_wait_" + group, h, [dx])))
    own = {n: lax.dynamic_index_in_dim(g[n], dev, axis=0, keepdims=False) for n in ("ffn_w1", "ffn_w3", "ffn_w2")}
    own["w_in"] = lax.dynamic_slice_in_dim(g["w_in"], dev * IN_SHARD, IN_SHARD, axis=1)
    for n in ("w_out", "glu_a", "glu_b"):
        own[n] = lax.dynamic_slice_in_dim(g[n], dev * (D_MODEL // N_DEV), D_MODEL // N_DEV, axis=0)

    def update(n, short):
        res[n] = _adamw("adamw_" + n, a[n], a["m_" + n], a["v_" + n], own[short].reshape(a[n].shape),
                        landed[short].reshape((N_DEV - 1,) + a[n].shape))

    for n, short in zip(_BIG[3:], ("w_in", "w_out", "glu_a", "glu_b")):
        update(n, short)
    after = [dx, packed[0]] + [res[n][0] for n in _BIG[3:]]
    for k, n in enumerate(("ffn_w1", "ffn_w3", "ffn_w2")):
        for group, names, h in rs:
            if group.startswith("ffn_") and names == [n]:
                (ffn_lands[k],) = _xchg_wait("rs_wait_" + group, h, after, [ffn_lands[k]])
        landed[n] = ffn_lands[k]
        update(n, n)
        after = after + [res[n][0]]

    out = [loss, dx.reshape(nb, s, d)]
    for k in range(4):
        out += [res[n][k] for n in _WEIGHTS]
    return tuple(out)
```

```python
import functools
import math

import numpy as np
import jax
import jax.numpy as jnp
from jax import lax
from jax.experimental import pallas as pl
from jax.experimental.pallas import tpu as pltpu

f32 = jnp.float32
bf16 = jnp.bfloat16

D_MODEL = 1024
N_DEV = 8
EPS = 1e-6
RET_HEADS = 4
HEAD_DIM = 128
RET_WIDTH = 512
RET_CHUNK = 128
ROPE_BASE = 10000.0
LRU_WIDTH = 512
LRU_BLOCKS = 4
LRU_C = 8.0
S5_GROUP = 16
S5_GROUPS = 64
S5_STATE = 64
S5_CHUNK = 128
S5_BLOCKS = 8
S5_BLOCK_STATES = 512
SUBLANES = 8
D_FF = 2816
FF_SHARD = D_FF // N_DEV
FF_PAD = 384
IN_SHARD = 3072 // N_DEV
ADAM_LR = 0.001
ADAM_B1 = 0.9
ADAM_B2 = 0.999
ADAM_EPS = 1e-08
ADAM_WD = 0.01
ADAM_STEP = 10

VMEM_LIMIT = 56 * 1024 * 1024
VMEM_SPEC = pl.BlockSpec(memory_space=pltpu.VMEM)
ANY_SPEC = pl.BlockSpec(memory_space=pl.ANY)
HBM_SPEC = pl.BlockSpec(memory_space=pltpu.HBM)
SEM_SPEC = pl.BlockSpec(memory_space=pltpu.SEMAPHORE)
SIDE_EFFECT = pltpu.SideEffectType.DATAFLOW_SIDE_EFFECTING
MESH = pl.DeviceIdType.MESH


def _cp(*sem):
    return pltpu.CompilerParams(dimension_semantics=sem, vmem_limit_bytes=VMEM_LIMIT)


def _nn(a, b):
    return jnp.dot(a, b, preferred_element_type=f32)


def _nt(a, b):
    return lax.dot_general(a, b, (((1,), (1,)), ((), ())), preferred_element_type=f32)


def _tn(a, b):
    return lax.dot_general(a, b, (((0,), (0,)), ((), ())), preferred_element_type=f32)


def _rms_fwd(x, g):
    r = lax.rsqrt(jnp.mean(x * x, axis=-1, keepdims=True) + EPS)
    xn = x * r
    return xn * g, xn, r


def _rms_bwd(dh, xn, r, g):
    dxn = dh * g
    dx = r * (dxn - xn * jnp.mean(dxn * xn, axis=-1, keepdims=True))
    dg = jnp.sum(dh * xn, axis=0, keepdims=True)
    return dx, dg


def _shift_dn(v, d, row, fill=0.0):
    return jnp.where(row >= d, pltpu.roll(v, d, 0), fill)


def _shift_up(v, d, row, fill=0.0):
    n = v.shape[0]
    return jnp.where(row < n - d, pltpu.roll(v, n - d, 0), fill)


def _ew(name, fn, ins, outs, tm=512):
    t = ins[0].shape[0]
    n_in = len(ins)

    def body(*refs):
        res = fn(*[r[...] for r in refs[:n_in]])
        for o, v in zip(refs[n_in:], res):
            o[...] = v.astype(o.dtype)

    return pl.pallas_call(
        body, name=name, grid=(t // tm,),
        in_specs=[pl.BlockSpec((tm, a.shape[1]), lambda i: (i, 0)) for a in ins],
        out_specs=[pl.BlockSpec((tm, n), lambda i: (i, 0)) for n, _ in outs],
        out_shape=[jax.ShapeDtypeStruct((t, n), dt) for n, dt in outs],
        compiler_params=_cp("parallel"),
    )(*ins)


def _mm(name, x, w, kind, extras=(), epilogue=None, outs=None, tm=512, tn=512):
    t = x.shape[0]
    n = w.shape[1] if kind == "nn" else w.shape[0]
    tn = min(tn, n)
    outs = outs or [f32]
    n_ex = len(extras)

    def body(x_ref, w_ref, *refs):
        xb = x_ref[...].astype(bf16)
        acc = _nn(xb, w_ref[...]) if kind == "nn" else _nt(xb, w_ref[...])
        res = epilogue(acc, *[r[...] for r in refs[:n_ex]]) if epilogue else (acc,)
        for o, v in zip(refs[n_ex:], res):
            o[...] = v.astype(o.dtype)

    w_spec = (pl.BlockSpec((w.shape[0], tn), lambda i, j: (0, j)) if kind == "nn"
              else pl.BlockSpec((tn, w.shape[1]), lambda i, j: (j, 0)))
    tile = pl.BlockSpec((tm, tn), lambda i, j: (i, j))
    return pl.pallas_call(
        body, name=name, grid=(t // tm, n // tn),
        in_specs=[pl.BlockSpec((tm, x.shape[1]), lambda i, j: (i, 0)), w_spec] + [tile] * n_ex,
        out_specs=[tile] * len(outs),
        out_shape=[jax.ShapeDtypeStruct((t, n), dt) for dt in outs],
        compiler_params=_cp("parallel", "parallel"),
    )(x, w, *extras)


def _mm_tn(name, x, y, tk=1024, tn=512, tt=512):
    t, k = x.shape
    n = y.shape[1]
    tk, tn = min(tk, k), min(tn, n)

    def body(x_ref, y_ref, o_ref, ob_ref):
        @pl.when(pl.program_id(2) == 0)
        def _():
            o_ref[...] = jnp.zeros_like(o_ref)
        o_ref[...] += _tn(x_ref[...].astype(bf16), y_ref[...].astype(bf16))

        @pl.when(pl.program_id(2) == pl.num_programs(2) - 1)
        def _():
            ob_ref[...] = o_ref[...].astype(bf16)

    out = pl.BlockSpec((tk, tn), lambda i, j, s: (i, j))
    return pl.pallas_call(
        body, name=name, grid=(k // tk, n // tn, t // tt),
        in_specs=[pl.BlockSpec((tt, tk), lambda i, j, s: (s, i)), pl.BlockSpec((tt, tn), lambda i, j, s: (s, j))],
        out_specs=[out, out],
        out_shape=[jax.ShapeDtypeStruct((k, n), f32), jax.ShapeDtypeStruct((k, n), bf16)],
        compiler_params=_cp("parallel", "parallel", "arbitrary"),
    )(x, y)


def _norm_fwd(name, x, g, tm=512):
    t, d = x.shape

    def body(x_ref, g_ref, h_ref, hb_ref):
        h, _, _ = _rms_fwd(x_ref[...], g_ref[...])
        h_ref[...] = h
        hb_ref[...] = h.astype(bf16)

    row = pl.BlockSpec((tm, d), lambda i: (i, 0))
    return pl.pallas_call(
        body, name=name, grid=(t // tm,),
        in_specs=[row, pl.BlockSpec((1, d), lambda i: (0, 0))],
        out_specs=[row, row],
        out_shape=[jax.ShapeDtypeStruct((t, d), f32), jax.ShapeDtypeStruct((t, d), bf16)],
        compiler_params=_cp("parallel"),
    )(x, g)


def _norm_bwd(name, dh, x, g, dres, tm=512):
    t, d = x.shape

    def body(dh_ref, x_ref, g_ref, dres_ref, dx_ref, dg_ref):
        gv = g_ref[...]
        _, xn, r = _rms_fwd(x_ref[...], gv)
        dx, dg = _rms_bwd(dh_ref[...], xn, r, gv)
        dx_ref[...] = dres_ref[...] + dx

        @pl.when(pl.program_id(0) == 0)
        def _():
            dg_ref[...] = jnp.zeros_like(dg_ref)
        dg_ref[...] += dg

    row = pl.BlockSpec((tm, d), lambda i: (i, 0))
    vec = pl.BlockSpec((1, d), lambda i: (0, 0))
    return pl.pallas_call(
        body, name=name, grid=(t // tm,),
        in_specs=[row, row, vec, row],
        out_specs=[row, vec],
        out_shape=[jax.ShapeDtypeStruct((t, d), f32), jax.ShapeDtypeStruct((1, d), f32)],
        compiler_params=_cp("arbitrary"),
    )(dh, x, g, dres)


def _final_loss(x, g, target, tm=512):
    t, d = x.shape

    def body(x_ref, g_ref, t_ref, loss_ref, dx_ref, dg_ref):
        gv = g_ref[...]
        y, xn, r = _rms_fwd(x_ref[...], gv)
        err = y - t_ref[...]
        dy = err * (1.0 / d)
        dx, dg = _rms_bwd(dy, xn, r, gv)
        dx_ref[...] = dx

        @pl.when(pl.program_id(0) == 0)
        def _():
            dg_ref[...] = jnp.zeros_like(dg_ref)
            loss_ref[...] = jnp.zeros_like(loss_ref)
        dg_ref[...] += dg
        loss_ref[...] += jnp.full((1, 128), 0.5 / d, f32) * jnp.sum(err * err)

    row = pl.BlockSpec((tm, d), lambda i: (i, 0))
    vec = pl.BlockSpec((1, d), lambda i: (0, 0))
    return pl.pallas_call(
        body, name="final_loss", grid=(t // tm,),
        in_specs=[row, vec, row],
        out_specs=[pl.BlockSpec((1, 128), lambda i: (0, 0)), row, vec],
        out_shape=[jax.ShapeDtypeStruct((1, 128), f32), jax.ShapeDtypeStruct((t, d), f32),
                   jax.ShapeDtypeStruct((1, d), f32)],
        compiler_params=_cp("arbitrary"),
    )(x, g, target)


def _load_ffn_weights(hbm_refs, vmem_refs, sems):
    @pl.when(pl.program_id(0) == 0)
    def _():
        copies = []
        for k, (src, dst) in enumerate(zip(hbm_refs, vmem_refs)):
            for j in range(N_DEV):
                half = pl.ds((j % 2) * FF_PAD, FF_PAD)
                window = dst.at[j // 2, half, :] if k == 2 else dst.at[j // 2, :, half]
                copies.append(pltpu.make_async_copy(src.at[j], window, sems.at[k * N_DEV + j]))
        for cp in copies:
            cp.start()
        for cp in copies:
            cp.wait()


def _ffn_weight_scratch(nj, d, ff):
    return [pltpu.VMEM((nj, d, ff), bf16), pltpu.VMEM((nj, d, ff), bf16), pltpu.VMEM((nj, ff, d), bf16),
            pltpu.SemaphoreType.DMA((3 * N_DEV,))]


def _ffn_fwd(name, x, g, w1, w3, w2, tm=512):
    t, d = x.shape
    nj, ff = N_DEV // 2, 2 * FF_PAD

    def body(x_ref, g_ref, w1_hbm, w3_hbm, w2_hbm, y_ref, a_ref, b_ref, w1_ref, w3_ref, w2_ref, sems):
        _load_ffn_weights((w1_hbm, w3_hbm, w2_hbm), (w1_ref, w3_ref, w2_ref), sems)
        xv = x_ref[...]
        h, _, _ = _rms_fwd(xv, g_ref[...])
        hb = h.astype(bf16)
        acc = jnp.zeros((tm, d), f32)
        for j in range(nj):
            a = _nn(hb, w1_ref[j])
            b = _nn(hb, w3_ref[j])
            a_ref[j] = a.astype(bf16)
            b_ref[j] = b.astype(bf16)
            u = (a * jax.nn.sigmoid(a) * b).astype(bf16)
            acc = acc + _nn(u, w2_ref[j])
        y_ref[...] = xv + 0.5 * acc

    row = pl.BlockSpec((tm, d), lambda i: (i, 0))
    mid = pl.BlockSpec((nj, tm, ff), lambda i: (0, i, 0))
    return pl.pallas_call(
        body, name=name, grid=(t // tm,),
        in_specs=[row, pl.BlockSpec((1, d), lambda i: (0, 0)), ANY_SPEC, ANY_SPEC, ANY_SPEC],
        out_specs=[row, mid, mid],
        out_shape=[jax.ShapeDtypeStruct((t, d), f32), jax.ShapeDtypeStruct((nj, t, ff), bf16),
                   jax.ShapeDtypeStruct((nj, t, ff), bf16)],
        scratch_shapes=_ffn_weight_scratch(nj, d, ff),
        compiler_params=_cp("arbitrary"),
    )(x, g, w1, w3, w2)


def _ffn_dx(name, dy, x, g, w1, w3, w2, a, b, tm=256):
    t, d = x.shape
    nj, ff = N_DEV // 2, 2 * FF_PAD

    def body(dy_ref, x_ref, g_ref, w1_hbm, w3_hbm, w2_hbm, a_ref, b_ref,
             dx_ref, dg_ref, hbt_ref, dyh_ref, ut_ref, da_ref, db_ref, w1_ref, w3_ref, w2_ref, sems):
        _load_ffn_weights((w1_hbm, w3_hbm, w2_hbm), (w1_ref, w3_ref, w2_ref), sems)
        gv = g_ref[...]
        h, xn, r = _rms_fwd(x_ref[...], gv)
        hbt_ref[...] = h.astype(bf16).T
        dyv = dy_ref[...]
        dyh = (0.5 * dyv).astype(bf16)
        dyh_ref[...] = dyh
        dh = jnp.zeros((tm, d), f32)
        for j in range(nj):
            av = a_ref[j].astype(f32)
            bv = b_ref[j].astype(f32)
            s = jax.nn.sigmoid(av)
            silu = av * s
            ut_ref[j] = (silu * bv).astype(bf16).T
            du = _nt(dyh, w2_ref[j])
            dab = (du * bv * (s * (1.0 + av * (1.0 - s)))).astype(bf16)
            dbb = (du * silu).astype(bf16)
            da_ref[j] = dab
            db_ref[j] = dbb
            dh = dh + _nt(dab, w1_ref[j]) + _nt(dbb, w3_ref[j])
        dx, dg = _rms_bwd(dh, xn, r, gv)
        dx_ref[...] = dyv + dx

        @pl.when(pl.program_id(0) == 0)
        def _():
            dg_ref[...] = jnp.zeros_like(dg_ref)
        dg_ref[...] += dg

    row = pl.BlockSpec((tm, d), lambda i: (i, 0))
    vec = pl.BlockSpec((1, d), lambda i: (0, 0))
    mid = pl.BlockSpec((nj, tm, ff), lambda i: (0, i, 0))
    mid_shape = jax.ShapeDtypeStruct((nj, t, ff), bf16)
    return pl.pallas_call(
        body, name=name, grid=(t // tm,),
        in_specs=[row, row, vec, ANY_SPEC, ANY_SPEC, ANY_SPEC, mid, mid],
        out_specs=[row, vec, pl.BlockSpec((d, tm), lambda i: (0, i)), row,
                   pl.BlockSpec((nj, ff, tm), lambda i: (0, 0, i)), mid, mid],
        out_shape=[jax.ShapeDtypeStruct((t, d), f32), jax.ShapeDtypeStruct((1, d), f32),
                   jax.ShapeDtypeStruct((d, t), bf16), jax.ShapeDtypeStruct((t, d), bf16),
                   jax.ShapeDtypeStruct((nj, ff, t), bf16), mid_shape, mid_shape],
        scratch_shapes=_ffn_weight_scratch(nj, d, ff),
        compiler_params=_cp("arbitrary"),
    )(dy, x, g, w1, w3, w2, a, b)


def _ffn_dw_one(name, xt, y, buf, l, h, tt=1024):
    t = y.shape[-2]
    tt = min(tt, t)
    cut_cols = xt.ndim == 2

    def body(x_ref, y_ref, buf_ref, o_ref, ob_ref, acc):
        s = pl.program_id(1)
        prod = _nn(x_ref[0] if xt.ndim == 3 else x_ref[...], y_ref[0] if y.ndim == 3 else y_ref[...])

        @pl.when(s == 0)
        def _():
            acc[...] = prod

        @pl.when(s > 0)
        def _():
            acc[...] += prod

        @pl.when(s == pl.num_programs(1) - 1)
        def _():
            total = acc[...]
            for e in range(2):
                lo = e * FF_PAD
                part = total[:, lo:lo + FF_SHARD] if cut_cols else total[lo:lo + FF_SHARD, :]
                o_ref[e] = part
                ob_ref[e] = part.astype(bf16)

    x_spec = (pl.BlockSpec((1, xt.shape[1], tt), lambda p, s: (p, 0, s)) if xt.ndim == 3
              else pl.BlockSpec((xt.shape[0], tt), lambda p, s: (0, s)))
    y_spec = (pl.BlockSpec((1, tt, y.shape[2]), lambda p, s: (p, s, 0)) if y.ndim == 3
              else pl.BlockSpec((tt, y.shape[1]), lambda p, s: (s, 0)))
    k_, n_ = buf.shape[-2:]
    return pl.pallas_call(
        body, name=name, grid=(N_DEV // 2, t // tt),
        in_specs=[x_spec, y_spec, ANY_SPEC],
        out_specs=[pl.BlockSpec((2, None, None, k_, n_), lambda p, s: (p, l, h, 0, 0)),
                   pl.BlockSpec((2, k_, n_), lambda p, s: (p, 0, 0))],
        out_shape=[jax.ShapeDtypeStruct(buf.shape, buf.dtype), jax.ShapeDtypeStruct((N_DEV, k_, n_), bf16)],
        input_output_aliases={2: 0},
        scratch_shapes=[pltpu.VMEM((xt.shape[-2], y.shape[-1]), f32)],
        compiler_params=_cp("parallel", "arbitrary"),
    )(xt, y, buf)


_LOG_GAMMA = [float(np.log1p(-np.float32(2.0) ** np.float32(-5.0 - h))) for h in range(RET_HEADS)]


def _ret_consts(h):
    lg = jnp.where(h == 0, _LOG_GAMMA[0], jnp.where(h == 1, _LOG_GAMMA[1],
                   jnp.where(h == 2, _LOG_GAMMA[2], _LOG_GAMMA[3]))).astype(f32)
    c = RET_CHUNK
    r = lax.broadcasted_iota(jnp.int32, (c, c), 0)
    cc = lax.broadcasted_iota(jnp.int32, (c, c), 1)
    decay = jnp.where(r >= cc, jnp.exp(lg * jnp.maximum((r - cc).astype(f32), 0.0)), 0.0)
    pos = lax.broadcasted_iota(jnp.int32, (c, 1), 0).astype(f32)
    kd = jnp.exp(lg * (c - 1.0 - pos))
    qd = jnp.exp(lg * (pos + 1.0))
    gc = jnp.exp(lg * c)
    return decay, kd, qd, gc


def _rope(x, cos, sin):
    return x * cos + pltpu.roll(x, HEAD_DIM // 2, 1) * sin


def _rope_t(g, cos, sin):
    return g * cos + pltpu.roll(g * sin, HEAD_DIM // 2, 1)


def _rope_tables(s):
    half = HEAD_DIM // 2
    inv = ROPE_BASE ** (-jnp.arange(half, dtype=f32) / half)
    ang = jnp.arange(s, dtype=f32)[:, None] * inv[None, :]
    cos, sin = jnp.cos(ang), jnp.sin(ang)
    return jnp.concatenate([cos, cos], axis=1), jnp.concatenate([-sin, sin], axis=1)


def _head_ln(o):
    mu = jnp.mean(o, axis=-1, keepdims=True)
    oc = o - mu
    rs = lax.rsqrt(jnp.mean(oc * oc, axis=-1, keepdims=True) + EPS)
    return oc * rs, rs


def _ret_fwd(proj, cos, sin, ret_g, nb, s):
    c = RET_CHUNK
    nc = s // c
    t = nb * s
    scale = HEAD_DIM ** -0.5

    def body(q_ref, k_ref, v_ref, gate_ref, cos_ref, sin_ref, g_ref, o_ref, rprev_ref, m_ref):
        decay, kd, qd, gc = _ret_consts(pl.program_id(1))
        gv = g_ref[...]

        def step(n, rv):
            rows = pl.ds(pl.multiple_of(n * c, c), c)
            cs, sn = cos_ref[rows, :], sin_ref[rows, :]
            q = _rope(q_ref[rows, :], cs, sn)
            k = _rope(k_ref[rows, :], cs, sn) * scale
            vb = v_ref[rows, :].astype(bf16)
            sc = _nt(q.astype(bf16), k.astype(bf16)) * decay
            rprev_ref[n] = rv
            o = _nn(sc.astype(bf16), vb) + _nn((q * qd).astype(bf16), rv.astype(bf16))
            o_ref[rows, :] = o
            y, _ = _head_ln(o)
            gate = gate_ref[rows, :]
            m_ref[rows, :] = y * gv * (gate * jax.nn.sigmoid(gate))
            return rv * gc + _tn((k * kd).astype(bf16), vb)

        lax.fori_loop(0, nc, step, jnp.zeros((HEAD_DIM, HEAD_DIM), f32))

    def col(off):
        return pl.BlockSpec((s, HEAD_DIM), lambda b, h: (b, off + h))

    tab = pl.BlockSpec((s, HEAD_DIM), lambda b, h: (0, 0))
    return pl.pallas_call(
        body, name="ret_fwd", grid=(nb, RET_HEADS),
        in_specs=[col(0), col(4), col(8), col(12), tab, tab, pl.BlockSpec((1, HEAD_DIM), lambda b, h: (0, h))],
        out_specs=[col(0), pl.BlockSpec((nc, HEAD_DIM, HEAD_DIM), lambda b, h: (b * RET_HEADS + h, 0, 0)), col(0)],
        out_shape=[jax.ShapeDtypeStruct((t, RET_WIDTH), f32),
                   jax.ShapeDtypeStruct((nb * RET_HEADS * nc, HEAD_DIM, HEAD_DIM), f32),
                   jax.ShapeDtypeStruct((t, RET_WIDTH), f32)],
        compiler_params=_cp("parallel", "parallel"),
    )(proj, proj, proj, proj, cos, sin, ret_g)


def _ret_bwd(dmerged, o_raw, rprev, proj, cos, sin, ret_g, nb, s):
    c = RET_CHUNK
    nc = s // c
    t = nb * s
    scale = HEAD_DIM ** -0.5

    def body(dm_ref, o_ref, rprev_ref, q_ref, k_ref, v_ref, gate_ref, cos_ref, sin_ref, g_ref,
             dq_ref, dk_ref, dv_ref, dgate_ref, dg_ref):
        @pl.when(pl.program_id(1) == 0)
        def _():
            dg_ref[...] = jnp.zeros_like(dg_ref)
        decay, kd, qd, gc = _ret_consts(pl.program_id(0))
        gv = g_ref[...]

        def step(i, carry):
            drn, dg = carry
            n = nc - 1 - i
            rows = pl.ds(pl.multiple_of(n * c, c), c)
            cs, sn = cos_ref[rows, :], sin_ref[rows, :]
            q = _rope(q_ref[rows, :], cs, sn)
            k = _rope(k_ref[rows, :], cs, sn) * scale
            qb, kb = q.astype(bf16), k.astype(bf16)
            vb = v_ref[rows, :].astype(bf16)
            sc = _nt(qb, kb) * decay
            y, rs = _head_ln(o_ref[rows, :])
            gate = gate_ref[rows, :]
            sg = jax.nn.sigmoid(gate)
            silu = gate * sg
            dm = dm_ref[rows, :]
            dgate_ref[rows, :] = dm * y * gv * (sg * (1.0 + gate * (1.0 - sg)))
            dyl = dm * gv * silu
            dg = dg + jnp.sum(dm * y * silu, axis=0, keepdims=True)
            do = rs * (dyl - jnp.mean(dyl, axis=-1, keepdims=True) - y * jnp.mean(dyl * y, axis=-1, keepdims=True))
            dob = do.astype(bf16)
            rv = rprev_ref[n]
            drb = drn.astype(bf16)
            ds = (_nt(dob, vb) * decay).astype(bf16)
            kdb = (k * kd).astype(bf16)
            qdb = (q * qd).astype(bf16)
            dq_r = _nn(ds, kb) + _nt(dob, rv.astype(bf16)) * qd
            dk_r = _tn(ds, qb) + _nt(vb, drb) * kd
            dv_ref[rows, :] = _tn(sc.astype(bf16), dob) + _nn(kdb, drb)
            dq_ref[rows, :] = _rope_t(dq_r, cs, sn)
            dk_ref[rows, :] = _rope_t(dk_r * scale, cs, sn)
            return drn * gc + _tn(qdb, dob), dg

        _, dg = lax.fori_loop(0, nc, step, (jnp.zeros((HEAD_DIM, HEAD_DIM), f32), jnp.zeros((1, HEAD_DIM), f32)))
        dg_ref[...] += dg

    def col(off):
        return pl.BlockSpec((s, HEAD_DIM), lambda h, b: (b, off + h))

    tab = pl.BlockSpec((s, HEAD_DIM), lambda h, b: (0, 0))
    gsp = pl.BlockSpec((1, HEAD_DIM), lambda h, b: (0, h))
    out_t = jax.ShapeDtypeStruct((t, RET_WIDTH), f32)
    return pl.pallas_call(
        body, name="ret_bwd", grid=(RET_HEADS, nb),
        in_specs=[col(0), col(0), pl.BlockSpec((nc, HEAD_DIM, HEAD_DIM), lambda h, b: (b * RET_HEADS + h, 0, 0)),
                  col(0), col(4), col(8), col(12), tab, tab, gsp],
        out_specs=[col(0), col(0), col(0), col(0), gsp],
        out_shape=[out_t, out_t, out_t, out_t, jax.ShapeDtypeStruct((1, RET_WIDTH), f32)],
        compiler_params=_cp("parallel", "arbitrary"),
    )(dmerged, o_raw, rprev, proj, proj, proj, proj, cos, sin, ret_g)


def _neg_expm1(z):
    series = -(z * (1.0 + z * (0.5 + z * (1.0 / 6.0 + z * (1.0 / 24.0)))))
    return jnp.where(z > -0.01, series, 1.0 - jnp.exp(z))


def _lru_gates(xc, pa, pi, lam):
    r = jax.nn.sigmoid(pa)
    i = jax.nn.sigmoid(pi)
    log_a = -LRU_C * r * jax.nn.softplus(-lam)
    a = jnp.exp(log_a)
    bx = jnp.sqrt(_neg_expm1(2.0 * log_a)) * i * xc
    return a, bx


def _scan_rows(a, b, row, up):
    sub = row[:SUBLANES] & (SUBLANES - 1)
    groups = list(range(a.shape[0] // SUBLANES))
    out = [None] * len(groups)
    edge = slice(0, 1) if up else slice(SUBLANES - 1, SUBLANES)
    carry = jnp.zeros((1, a.shape[1]), f32)
    for g in (reversed(groups) if up else groups):
        rows = slice(g * SUBLANES, (g + 1) * SUBLANES)
        xa, xb = a[rows], b[rows]
        d = 1
        while d < SUBLANES:
            keep = (sub < SUBLANES - d) if up else (sub >= d)
            shift = SUBLANES - d if up else d
            xb = xa * jnp.where(keep, pltpu.roll(xb, shift, 0), 0.0) + xb
            xa = xa * jnp.where(keep, pltpu.roll(xa, shift, 0), 1.0)
            d *= 2
        out[g] = xb + xa * carry
        carry = out[g][edge]
    return jnp.concatenate(out, axis=0)


def _scan_fwd(a, b, row):
    return _scan_rows(a, b, row, False)


def _scan_bwd(c, b, row):
    return _scan_rows(c, b, row, True)


def _conv_fwd(x, cw, cb, row):
    return (cb + cw[3:4] * x + cw[2:3] * _shift_dn(x, 1, row) + cw[1:2] * _shift_dn(x, 2, row)
            + cw[0:1] * _shift_dn(x, 3, row))


def _lru_specs(s, order):
    def im(f):
        return (lambda b, g: f(b, g)) if order == "bg" else (lambda g, b: f(b, g))
    seq = lambda off: pl.BlockSpec((s, 128), im(lambda b, g: (b, off + g)))
    vec = pl.BlockSpec((1, 128), im(lambda b, g: (0, g)))
    cw = pl.BlockSpec((4, 128), im(lambda b, g: (0, g)))
    mat = pl.BlockSpec((1, 128, 128), im(lambda b, g: (g, 0, 0)))
    return seq, vec, cw, mat


def _lru_fwd(proj, conv_w, conv_b, w_a, b_a, w_i, b_i, lam, nb, s):
    def body(x_ref, gt_ref, cw_ref, cb_ref, wa_ref, ba_ref, wi_ref, bi_ref, lam_ref, out_ref):
        row = lax.broadcasted_iota(jnp.int32, (s, 128), 0)
        xc = _conv_fwd(x_ref[...], cw_ref[...], cb_ref[...], row)
        xcb = xc.astype(bf16)
        pa = _nn(xcb, wa_ref[0].astype(bf16)) + ba_ref[...]
        pi = _nn(xcb, wi_ref[0].astype(bf16)) + bi_ref[...]
        a, bx = _lru_gates(xc, pa, pi, lam_ref[...])
        h = _scan_fwd(a, bx, row)
        out_ref[...] = h * jax.nn.gelu(gt_ref[...])

    seq, vec, cw, mat = _lru_specs(s, "bg")
    return pl.pallas_call(
        body, name="lru_fwd", grid=(nb, LRU_BLOCKS),
        in_specs=[seq(16), seq(20), cw, vec, mat, vec, mat, vec, vec],
        out_specs=seq(0),
        out_shape=jax.ShapeDtypeStruct((nb * s, LRU_WIDTH), f32),
        compiler_params=_cp("parallel", "parallel"),
    )(proj, proj, conv_w, conv_b, w_a, b_a, w_i, b_i, lam)


def _lru_bwd(dmerged, proj, conv_w, conv_b, w_a, b_a, w_i, b_i, lam, nb, s):
    def body(dout_ref, x_ref, gt_ref, cw_ref, cb_ref, wa_ref, ba_ref, wi_ref, bi_ref, lam_ref,
             dx_ref, dgt_ref, dcw_ref, dcb_ref, dwa_ref, dba_ref, dwi_ref, dbi_ref, dlam_ref):
        row = lax.broadcasted_iota(jnp.int32, (s, 128), 0)
        x = x_ref[...]
        cwv = cw_ref[...]
        xc = _conv_fwd(x, cwv, cb_ref[...], row)
        xcb = xc.astype(bf16)
        wab, wib = wa_ref[0].astype(bf16), wi_ref[0].astype(bf16)
        pa = _nn(xcb, wab) + ba_ref[...]
        pi = _nn(xcb, wib) + bi_ref[...]
        (a, bx), gates_vjp = jax.vjp(_lru_gates, xc, pa, pi, lam_ref[...])
        h = _scan_fwd(a, bx, row)
        ge, gelu_vjp = jax.vjp(jax.nn.gelu, gt_ref[...])
        dout = dout_ref[...]
        dgt_ref[...] = gelu_vjp(dout * h)[0]
        adj = _scan_bwd(_shift_up(a, 1, row), dout * ge, row)
        dxc, dpa, dpi, dlam = gates_vjp((adj * _shift_dn(h, 1, row), adj))
        dpab, dpib = dpa.astype(bf16), dpi.astype(bf16)
        dxc = dxc + _nt(dpab, wab) + _nt(dpib, wib)
        dx_ref[...] = (cwv[3:4] * dxc + cwv[2:3] * _shift_up(dxc, 1, row) + cwv[1:2] * _shift_up(dxc, 2, row)
                       + cwv[0:1] * _shift_up(dxc, 3, row))

        @pl.when(pl.program_id(1) == 0)
        def _():
            for r in (dcw_ref, dcb_ref, dwa_ref, dba_ref, dwi_ref, dbi_ref, dlam_ref):
                r[...] = jnp.zeros_like(r)
        rsum = lambda v: jnp.sum(v, axis=0, keepdims=True)
        dcw_ref[...] += jnp.concatenate([rsum(dxc * _shift_dn(x, 3, row)), rsum(dxc * _shift_dn(x, 2, row)),
                                         rsum(dxc * _shift_dn(x, 1, row)), rsum(dxc * x)], axis=0)
        dcb_ref[...] += rsum(dxc)
        dwa_ref[0] += _tn(xcb, dpab)
        dwi_ref[0] += _tn(xcb, dpib)
        dba_ref[...] += rsum(dpa)
        dbi_ref[...] += rsum(dpi)
        dlam_ref[...] += dlam

    seq, vec, cw, mat = _lru_specs(s, "gb")
    t = nb * s
    vshape = jax.ShapeDtypeStruct((1, LRU_WIDTH), f32)
    mshape = jax.ShapeDtypeStruct((LRU_BLOCKS, 128, 128), f32)
    return pl.pallas_call(
        body, name="lru_bwd", grid=(LRU_BLOCKS, nb),
        in_specs=[seq(4), seq(16), seq(20), cw, vec, mat, vec, mat, vec, vec],
        out_specs=[seq(0), seq(0), cw, vec, mat, vec, mat, vec, vec],
        out_shape=[jax.ShapeDtypeStruct((t, LRU_WIDTH), f32), jax.ShapeDtypeStruct((t, LRU_WIDTH), f32),
                   jax.ShapeDtypeStruct((4, LRU_WIDTH), f32), vshape, mshape, vshape, mshape, vshape, vshape],
        compiler_params=_cp("parallel", "arbitrary"),
    )(dmerged, proj, proj, conv_w, conv_b, w_a, b_a, w_i, b_i, lam)


def _s5_disc(lr, li, ldt, bre, bim):
    dt = jnp.exp(ldt)
    mag = jnp.exp(lr * dt)
    lbr = mag * jnp.cos(li * dt)
    lbi = mag * jnp.sin(li * dt)
    den = lr * lr + li * li
    nr = lbr - 1.0
    fr = (nr * lr + lbi * li) / den
    fi = (lbi * lr - nr * li) / den
    bbr = fr[:, None, :] * bre - fi[:, None, :] * bim
    bbi = fr[:, None, :] * bim + fi[:, None, :] * bre
    return lbr, lbi, bbr, bbi


def _s5_prep(lr, li, ldt, bre, bim):
    def body(lr_ref, li_ref, ldt_ref, bre_ref, bim_ref, o1, o2, o3, o4):
        o1[...], o2[...], o3[...], o4[...] = _s5_disc(lr_ref[...], li_ref[...], ldt_ref[...], bre_ref[...], bim_ref[...])

    return pl.pallas_call(
        body, name="s5_prep", in_specs=[VMEM_SPEC] * 5, out_specs=[VMEM_SPEC] * 4,
        out_shape=[jax.ShapeDtypeStruct(lr.shape, f32), jax.ShapeDtypeStruct(lr.shape, f32),
                   jax.ShapeDtypeStruct(bre.shape, f32), jax.ShapeDtypeStruct(bre.shape, f32)],
    )(lr, li, ldt, bre, bim)


def _s5_prep_bwd(lr, li, ldt, bre, bim, cts):
    def body(lr_ref, li_ref, ldt_ref, bre_ref, bim_ref, g1, g2, g3, g4, o1, o2, o3, o4, o5):
        _, vjp = jax.vjp(_s5_disc, lr_ref[...], li_ref[...], ldt_ref[...], bre_ref[...], bim_ref[...])
        o1[...], o2[...], o3[...], o4[...], o5[...] = vjp((g1[...], g2[...], g3[...], g4[...]))

    return pl.pallas_call(
        body, name="s5_prep_bwd", in_specs=[VMEM_SPEC] * 9, out_specs=[VMEM_SPEC] * 5,
        out_shape=[jax.ShapeDtypeStruct(v.shape, f32) for v in (lr, li, ldt, bre, bim)],
    )(lr, li, ldt, bre, bim, *cts)


def _cmul(ar, ai, br, bi):
    return ar * br - ai * bi, ar * bi + ai * br


def _s5_pow_table(lr, li, n, row, up):
    ar = jnp.broadcast_to(lr, (n, lr.shape[1]))
    ai = jnp.broadcast_to(li, (n, li.shape[1]))
    shift = _shift_up if up else _shift_dn
    d = 1
    while d < n:
        ar, ai = _cmul(ar, ai, shift(ar, d, row, 1.0), shift(ai, d, row, 0.0))
        d *= 2
    return ar, ai


def _s5_step_factors(lr, li, row, up):
    sub = row & (SUBLANES - 1)
    out, pr, pi, d = [], lr, li, 1
    while d < SUBLANES:
        keep = (sub < SUBLANES - d) if up else (sub >= d)
        out.append((jnp.where(keep, pr, 0.0), jnp.where(keep, pi, 0.0)))
        pr, pi = _cmul(pr, pi, pr, pi)
        d *= 2
    return out


def _s5_scan(br, bi, steps, tab_r, tab_i, cr, ci, up):
    groups = list(range(br.shape[0] // SUBLANES))
    out_r, out_i = [None] * len(groups), [None] * len(groups)
    edge = slice(0, 1) if up else slice(SUBLANES - 1, SUBLANES)
    for g in (reversed(groups) if up else groups):
        rows = slice(g * SUBLANES, (g + 1) * SUBLANES)
        xr, xi = br[rows], bi[rows]
        for k, (mr, mi) in enumerate(steps):
            shift = SUBLANES - (1 << k) if up else 1 << k
            tr, ti = _cmul(mr, mi, pltpu.roll(xr, shift, 0), pltpu.roll(xi, shift, 0))
            xr, xi = xr + tr, xi + ti
        tr, ti = _cmul(tab_r, tab_i, cr, ci)
        hr, hi = xr + tr, xi + ti
        out_r[g], out_i[g] = hr, hi
        cr, ci = hr[edge], hi[edge]
    return jnp.concatenate(out_r, axis=0), jnp.concatenate(out_i, axis=0)


def _s5_specs(t, nb, nc):
    seq = pl.BlockSpec((t, 128), lambda k: (0, k))
    lvec = pl.BlockSpec((1, S5_BLOCK_STATES), lambda k: (0, k))
    dvec = pl.BlockSpec((1, 128), lambda k: (0, k))
    wmat = pl.BlockSpec((1, 128, S5_BLOCK_STATES), lambda k: (k, 0, 0))
    h0 = pl.BlockSpec((nb, None, nc, 2, S5_BLOCK_STATES), lambda k: (0, k, 0, 0, 0))
    return seq, lvec, dvec, wmat, h0


def _s5_fwd(u, lbr, lbi, wbr, wbi, wcr, wci, dskip, nb, s):
    ln = S5_CHUNK
    nc = s // ln

    def body(u_ref, lr_ref, li_ref, wbr_ref, wbi_ref, wcr_ref, wci_ref, d_ref, yg_ref, y_ref, h0_ref):
        row = lax.broadcasted_iota(jnp.int32, (ln, S5_BLOCK_STATES), 0)
        lr, li = lr_ref[...], li_ref[...]
        pr, pi = _s5_pow_table(lr, li, SUBLANES, row[:SUBLANES], False)
        steps = _s5_step_factors(lr, li, row[:SUBLANES], False)
        dv = d_ref[...]

        def chunk(b, n, h0r, h0i):
            st = pl.multiple_of(b * s + n * ln, ln)
            uc = u_ref[pl.ds(st, ln), :]
            ub = uc.astype(bf16)
            hr, hi = _s5_scan(_nn(ub, wbr_ref[0]), _nn(ub, wbi_ref[0]), steps, pr, pi, h0r, h0i, False)
            h0_ref[b, n, 0:1, :] = h0r
            h0_ref[b, n, 1:2, :] = h0i
            y = _nt(hr.astype(bf16), wcr_ref[0]) - _nt(hi.astype(bf16), wci_ref[0]) + dv * uc
            y_ref[pl.ds(st, ln), :] = y
            yg_ref[pl.ds(st, ln), :] = jax.nn.gelu(y).astype(bf16)
            return hr[ln - 1:ln, :], hi[ln - 1:ln, :]

        def step(n, carry):
            return tuple(chunk(b, n, *carry[b]) for b in range(nb))

        z = jnp.zeros((1, S5_BLOCK_STATES), f32)
        lax.fori_loop(0, nc, step, ((z, z),) * nb)

    t = nb * s
    seq, lvec, dvec, wmat, h0 = _s5_specs(t, nb, nc)
    return pl.pallas_call(
        body, name="s5_fwd", grid=(S5_BLOCKS,),
        in_specs=[seq, lvec, lvec, wmat, wmat, wmat, wmat, dvec],
        out_specs=[seq, seq, h0],
        out_shape=[jax.ShapeDtypeStruct((t, D_MODEL), bf16), jax.ShapeDtypeStruct((t, D_MODEL), f32),
                   jax.ShapeDtypeStruct((nb, S5_BLOCKS, nc, 2, S5_BLOCK_STATES), f32)],
        compiler_params=_cp("parallel"),
    )(u, lbr, lbi, wbr, wbi, wcr, wci, dskip)


def _s5_bwd(dyg, y, u, h0, lbr, lbi, wbr, wbi, wcr, wci, dskip, nb, s):
    ln = S5_CHUNK
    nc = s // ln

    def body(dyg_ref, y_ref, u_ref, h0_ref, lr_ref, li_ref, wbr_ref, wbi_ref, wcr_ref, wci_ref, d_ref,
             du_ref, dlr_ref, dli_ref, dwbr_ref, dwbi_ref, dwcr_ref, dwci_ref, dd_ref):
        for r in (dlr_ref, dli_ref, dwbr_ref, dwbi_ref, dwcr_ref, dwci_ref, dd_ref):
            r[...] = jnp.zeros_like(r)
        row = lax.broadcasted_iota(jnp.int32, (ln, S5_BLOCK_STATES), 0)
        lr, li = lr_ref[...], li_ref[...]
        pr, pi = _s5_pow_table(lr, li, SUBLANES, row[:SUBLANES], False)
        qr, qi = _s5_pow_table(lr, -li, SUBLANES, row[:SUBLANES], True)
        row8 = row[:SUBLANES]
        steps_dn, steps_up = _s5_step_factors(lr, li, row8, False), _s5_step_factors(lr, -li, row8, True)
        dv = d_ref[...]
        rsum = lambda v: jnp.sum(v, axis=0, keepdims=True)

        def chunk(b, n, gnr, gni):
            st = pl.multiple_of(b * s + n * ln, ln)
            uc = u_ref[pl.ds(st, ln), :]
            ub = uc.astype(bf16)
            h0v = h0_ref[b, n]
            h0r, h0i = h0v[0:1], h0v[1:2]
            hr, hi = _s5_scan(_nn(ub, wbr_ref[0]), _nn(ub, wbi_ref[0]), steps_dn, pr, pi, h0r, h0i, False)
            dy = jax.vjp(jax.nn.gelu, y_ref[pl.ds(st, ln), :])[1](dyg_ref[pl.ds(st, ln), :])[0]
            dyb = dy.astype(bf16)
            dd_ref[...] += rsum(dy * uc)
            gr, gi = _s5_scan(_nn(dyb, wcr_ref[0]), -_nn(dyb, wci_ref[0]), steps_up, qr, qi, gnr, gni, True)
            hpr = jnp.where(row >= 1, pltpu.roll(hr, 1, 0), h0r)
            hpi = jnp.where(row >= 1, pltpu.roll(hi, 1, 0), h0i)
            dlr_ref[...] += rsum(gr * hpr + gi * hpi)
            dli_ref[...] += rsum(gi * hpr - gr * hpi)
            grb, gib = gr.astype(bf16), gi.astype(bf16)
            dwbr_ref[0] += _tn(ub, grb)
            dwbi_ref[0] += _tn(ub, gib)
            dwcr_ref[0] += _tn(dyb, hr.astype(bf16))
            dwci_ref[0] -= _tn(dyb, hi.astype(bf16))
            du_ref[pl.ds(st, ln), :] = _nt(grb, wbr_ref[0]) + _nt(gib, wbi_ref[0]) + dv * dy
            return gr[0:1, :], gi[0:1, :]

        def step(i, carry):
            return tuple(chunk(b, nc - 1 - i, *carry[b]) for b in range(nb))

        z = jnp.zeros((1, S5_BLOCK_STATES), f32)
        lax.fori_loop(0, nc, step, ((z, z),) * nb)

    t = nb * s
    seq, lvec, dvec, wmat, h0s = _s5_specs(t, nb, nc)
    lshape = jax.ShapeDtypeStruct((1, S5_BLOCKS * S5_BLOCK_STATES), f32)
    wshape = jax.ShapeDtypeStruct((S5_BLOCKS, 128, S5_BLOCK_STATES), f32)
    return pl.pallas_call(
        body, name="s5_bwd", grid=(S5_BLOCKS,),
        in_specs=[seq, seq, seq, h0s, lvec, lvec, wmat, wmat, wmat, wmat, dvec],
        out_specs=[seq, lvec, lvec, wmat, wmat, wmat, wmat, dvec],
        out_shape=[jax.ShapeDtypeStruct((t, D_MODEL), f32), lshape, lshape, wshape, wshape, wshape, wshape,
                   jax.ShapeDtypeStruct((1, D_MODEL), f32)],
        compiler_params=_cp("parallel"),
    )(dyg, y, u, h0, lbr, lbi, wbr, wbi, wcr, wci, dskip)


def _blockdiag(w):
    w4 = w.reshape(S5_BLOCKS, 8, S5_GROUP, S5_STATE)
    same_group = jnp.eye(8, dtype=bool)[None, :, None, :, None]
    return jnp.where(same_group, w4[:, :, :, None, :], 0.0).reshape(S5_BLOCKS, 128, S5_BLOCK_STATES)


def _blockdiag_t(dw):
    d5 = dw.reshape(S5_BLOCKS, 8, S5_GROUP, 8, S5_STATE)
    diag = jnp.diagonal(d5, axis1=1, axis2=3)
    return jnp.moveaxis(diag, 3, 1).reshape(S5_GROUPS, S5_GROUP, S5_STATE)


def _glu_fwd(ygb, wa, wb, x, tm=512, tn=512):
    t, d = x.shape

    def body(y_ref, wa_ref, wb_ref, x_ref, o_ref, p_ref, q_ref):
        p = _nn(y_ref[...], wa_ref[...])
        q = _nn(y_ref[...], wb_ref[...])
        p_ref[...] = p
        q_ref[...] = q
        o_ref[...] = x_ref[...] + p * jax.nn.sigmoid(q)

    tile = pl.BlockSpec((tm, tn), lambda i, j: (i, j))
    wsp = pl.BlockSpec((d, tn), lambda i, j: (0, j))
    out = jax.ShapeDtypeStruct((t, d), f32)
    return pl.pallas_call(
        body, name="glu_fwd", grid=(t // tm, d // tn),
        in_specs=[pl.BlockSpec((tm, d), lambda i, j: (i, 0)), wsp, wsp, tile],
        out_specs=[tile, tile, tile], out_shape=[out, out, out],
        compiler_params=_cp("parallel", "parallel"),
    )(ygb, wa, wb, x)


def _place():
    x, y, c = lax.axis_index("x"), lax.axis_index("y"), lax.axis_index("c")
    return x, y, c, [(1 - x, y), (x, 1 - y), (1 - x, 1 - y)]


def _all_gather(name, arrays):
    n = len(arrays)

    def body(*refs):
        ins, outs = refs[:n], refs[n:2 * n]
        send_sems, recv_sems, local_sems = refs[2 * n:]
        x, y, c, chips = _place()
        me, sib = (x, y, c), (x, y, 1 - c)

        def copy(i, k, block, to, src=None):
            dst = outs[i].at[4 * block[0] + 2 * block[1] + block[2]]
            return pltpu.make_async_remote_copy(
                src_ref=dst if src is None else src, dst_ref=dst,
                send_sem=send_sems.at[i * 7 + k], recv_sem=recv_sems.at[i * 7 + k],
                device_id=to, device_id_type=MESH)

        mine = [pltpu.make_async_copy(ins[i], outs[i].at[4 * x + 2 * y + c], local_sems.at[i]) for i in range(n)]
        for m in mine:
            m.start()
        first = []
        for i in range(n):
            first.append(copy(i, 0, me, sib, src=ins[i]))
            first += [copy(i, 1 + j, me, (*chip, c), src=ins[i]) for j, chip in enumerate(chips)]
        for cp in first:
            cp.start()
        passed = []
        for j, chip in enumerate(chips):
            for i in range(n):
                copy(i, 1 + j, (*chip, c), me).wait_recv()
                fwd = copy(i, 4 + j, (*chip, c), sib)
                fwd.start()
                passed.append(fwd)
        for i in range(n):
            copy(i, 0, sib, me).wait_recv()
        for j, chip in enumerate(chips):
            for i in range(n):
                copy(i, 4 + j, (*chip, 1 - c), me).wait_recv()
        for cp in first + passed:
            cp.wait_send()
        for m in mine:
            m.wait()

    return pl.pallas_call(
        body, name=name,
        in_specs=[ANY_SPEC] * n, out_specs=[ANY_SPEC] * n,
        out_shape=[jax.ShapeDtypeStruct((N_DEV,) + a.shape, a.dtype) for a in arrays],
        scratch_shapes=[pltpu.SemaphoreType.DMA((7 * n,)), pltpu.SemaphoreType.DMA((7 * n,)),
                        pltpu.SemaphoreType.DMA((n,))],
    )(*arrays)


def _tie(name, x, deps):
    def body(*refs):
        pass

    return pl.pallas_call(
        body, name=name, in_specs=[ANY_SPEC] * (1 + len(deps)), out_specs=ANY_SPEC,
        out_shape=jax.ShapeDtypeStruct(x.shape, x.dtype), input_output_aliases={0: 0},
    )(x, *deps)


def _xchg_copies(kind, srcs, lands, suffixes, send_sems, recv_sems):
    x, y, c, _ = _place()
    copies = []
    for i, (src, land, sfx) in enumerate(zip(srcs, lands, suffixes)):
        for k in range(N_DEV - 1):
            r = k + 1
            peer = (1 - x if r & 4 else x, 1 - y if r & 2 else y, 1 - c if r & 1 else c)
            if kind == "gather":
                s_ref, d_ref = src, land.at[(4 * x + 2 * y + c,) + sfx]
            else:
                s_ref, d_ref = src.at[4 * peer[0] + 2 * peer[1] + peer[2]], land.at[(k,) + sfx]
            copies.append(pltpu.make_async_remote_copy(
                src_ref=s_ref, dst_ref=d_ref, send_sem=send_sems.at[i * 7 + k], recv_sem=recv_sems.at[i * 7 + k],
                device_id=peer, device_id_type=MESH))
    return copies


def _xchg_start(name, kind, srcs, lands, suffixes=None):
    n = len(srcs)
    suffixes = suffixes or [()] * n

    def body(*refs):
        src, land = refs[:n], refs[n:2 * n]
        send_sems, recv_sems, token = refs[2 * n], refs[2 * n + 1], refs[-1]
        for cp in _xchg_copies(kind, src, land, suffixes, send_sems, recv_sems):
            cp.start()
        token[...] = jnp.zeros_like(token)

    arrays = list(srcs) + list(lands)
    outs = pl.pallas_call(
        body, name=name,
        out_shape=(pltpu.SemaphoreType.DMA((7 * n,)), pltpu.SemaphoreType.DMA((7 * n,)),
                   *[pltpu.HBM(a.shape, a.dtype) for a in arrays], jax.ShapeDtypeStruct((8, 128), f32)),
        in_specs=[HBM_SPEC] * (2 * n),
        out_specs=(SEM_SPEC, SEM_SPEC, *[HBM_SPEC] * (2 * n), VMEM_SPEC),
        input_output_aliases={i: 2 + i for i in range(2 * n)},
        compiler_params=pltpu.CompilerParams(has_side_effects=SIDE_EFFECT),
    )(*[pltpu.with_memory_space_constraint(a, pltpu.HBM) for a in arrays])
    return dict(kind=kind, n=n, suffixes=suffixes, send=outs[0], recv=outs[1], srcs=list(outs[2:2 + n]),
                lands=list(outs[2 + n:2 + 2 * n]), token=outs[-1])


def _xchg_wait(name, h, after, lands=None):
    n = h["n"]
    lands = h["lands"] if lands is None else lands

    def body(*refs):
        src, land = refs[:n], refs[n:2 * n]
        for cp in _xchg_copies(h["kind"], src, land, h["suffixes"], refs[2 * n], refs[2 * n + 1]):
            cp.wait_send()
            cp.wait_recv()

    arrays = h["srcs"] + list(lands)
    outs = pl.pallas_call(
        body, name=name,
        out_shape=tuple(pltpu.HBM(a.shape, a.dtype) for a in arrays),
        in_specs=[HBM_SPEC] * (2 * n) + [SEM_SPEC, SEM_SPEC] + [ANY_SPEC] * len(after),
        out_specs=tuple([HBM_SPEC] * (2 * n)),
        input_output_aliases={i: i for i in range(2 * n)},
        compiler_params=pltpu.CompilerParams(has_side_effects=SIDE_EFFECT),
    )(*arrays, h["send"], h["recv"], *after)
    return list(outs[n:])


def _rows(a):
    return a.reshape(-1, a.shape[-1])


def _row_tile(r):
    for tm in (512, 256, 128, 64, 32, 16, 8):
        if r % tm == 0:
            return tm
    return r


def _sum8(name, gathered):
    _, r, n = gathered.shape
    tm = _row_tile(r)

    def body(g_ref, o_ref):
        acc = g_ref[0]
        for k in range(1, N_DEV):
            acc = acc + g_ref[k]
        o_ref[...] = acc

    return pl.pallas_call(
        body, name=name, grid=(r // tm,),
        in_specs=[pl.BlockSpec((N_DEV, tm, n), lambda i: (0, i, 0))],
        out_specs=pl.BlockSpec((tm, n), lambda i: (i, 0)),
        out_shape=jax.ShapeDtypeStruct((r, n), f32),
        compiler_params=_cp("parallel"),
    )(gathered)


def _adamw(name, w, m, v, own, landed=None):
    shape = w.shape
    w2, m2, v2, o2 = _rows(w), _rows(m), _rows(v), _rows(own)
    r, n = w2.shape
    tm = _row_tile(r)
    c1 = 1.0 - ADAM_B1 ** ADAM_STEP
    c2 = 1.0 - ADAM_B2 ** ADAM_STEP
    extra = [] if landed is None else [landed.reshape(landed.shape[0], r, n)]

    def body(w_ref, m_ref, v_ref, o_ref, *refs):
        g = o_ref[...]
        if extra:
            for k in range(extra[0].shape[0]):
                g = g + refs[0][k].astype(f32)
        g_ref, d_ref, mn_ref, vn_ref = refs[len(extra):]
        mn = ADAM_B1 * m_ref[...] + (1.0 - ADAM_B1) * g
        vn = ADAM_B2 * v_ref[...] + (1.0 - ADAM_B2) * (g * g)
        g_ref[...] = g
        d_ref[...] = -ADAM_LR * ((mn / c1) / (jnp.sqrt(vn / c2) + ADAM_EPS) + ADAM_WD * w_ref[...])
        mn_ref[...] = mn
        vn_ref[...] = vn

    row = pl.BlockSpec((tm, n), lambda i: (i, 0))
    outs = pl.pallas_call(
        body, name=name, grid=(r // tm,),
        in_specs=[row] * 4 + [pl.BlockSpec((e.shape[0], tm, n), lambda i: (0, i, 0)) for e in extra],
        out_specs=[row] * 4, out_shape=[jax.ShapeDtypeStruct((r, n), f32)] * 4,
        compiler_params=_cp("parallel"),
    )(w2, m2, v2, o2, *extra)
    return [o.reshape(shape) for o in outs]


def _pack(arrays):
    flat = jnp.concatenate([a.reshape(-1).astype(f32) for a in arrays])
    pad = (-flat.shape[0]) % (128 * (512 if flat.shape[0] > 128 * 512 else 8))
    return jnp.pad(flat, (0, pad)).reshape(-1, 128)


def _unpack(packed, shapes):
    flat = packed.reshape(-1)
    out, off = [], 0
    for s in shapes:
        n = math.prod(s)
        out.append(flat[off:off + n].reshape(s))
        off += n
    return out


def _local_step(x, target, w, weights_of, send, last_small, nb, s):
    cos, sin = _rope_tables(s)
    g = {}
    ffn_saved = {}
    ffn_bufs = [lax.empty((N_DEV, 2, 2) + shp, f32)
                for shp in ((D_MODEL, FF_SHARD), (D_MODEL, FF_SHARD), (FF_SHARD, D_MODEL))]

    def ffn(xin, l, h, wts):
        y, a, b = _ffn_fwd(f"ffn_fwd_{l}{h}", xin, w["ffn_g"][l][h], *wts)
        ffn_saved[(l, h)] = (xin, a, b, wts)
        return y

    def ffn_back(dy, l, h):
        xin, a, b, wts = ffn_saved[(l, h)]
        dx, dg, hb, dyh, u, da, db = _ffn_dx(f"ffn_dx_{l}{h}", dy, xin, w["ffn_g"][l][h], *wts, a, b)
        g[f"ffn_g_{l}{h}"] = dg
        if (l, h) == (0, 0):
            hb = last_small(g, hb)
        ffn_bufs[0], half = _ffn_dw_one(f"ffn_dw_{l}{h}_w1", hb, da, ffn_bufs[0], l, h)
        hb = send(f"ffn_{l}{h}_w1", {"ffn_w1": half}, hb)
        ffn_bufs[1], half = _ffn_dw_one(f"ffn_dw_{l}{h}_w3", hb, db, ffn_bufs[1], l, h)
        u = send(f"ffn_{l}{h}_w3", {"ffn_w3": half}, u)
        ffn_bufs[2], half = _ffn_dw_one(f"ffn_dw_{l}{h}_w2", u, dyh, ffn_bufs[2], l, h)
        return send(f"ffn_{l}{h}_w2", {"ffn_w2": half}, dx)

    def slots(t):
        return t.reshape(N_DEV, D_MODEL // N_DEV, D_MODEL)

    x1 = ffn(x, 0, 0, weights_of(0, [])["ffn"])
    wg = weights_of(1, [x1])
    w_in, w_out = wg["w_in"], wg["w_out"]
    _, h0b = _norm_fwd("mix_norm_0", x1, w["mix_g"][0])
    proj = _mm("in_proj", h0b, w_in, "nn", tn=768)[0]
    o_raw, rprev, mret = _ret_fwd(proj, cos, sin, w["ret_g"], nb, s)
    lru = _lru_fwd(proj, w["conv_w"], w["conv_b"], w["lru_w_a"], w["lru_b_a"], w["lru_w_i"], w["lru_b_i"], w["lru_lam"], nb, s)
    merged = _ew("merge", lambda a, b: (jnp.concatenate([a, b], axis=1),), [mret, lru], [(D_MODEL, bf16)])[0]
    x2 = _mm("out_proj", merged, w_out, "nn", extras=[x1], epilogue=lambda acc, r: (acc + r,))[0]
    x3 = ffn(x2, 0, 1, weights_of(2, [x2])["ffn"])
    wg = weights_of(3, [x3])
    glu_a, glu_b = wg["glu_a"], wg["glu_b"]
    x4 = ffn(x3, 1, 0, wg["ffn"])
    u, _ = _norm_fwd("mix_norm_1", x4, w["mix_g"][1])
    lbr, lbi, bbr, bbi = _s5_prep(w["s5_lr"], w["s5_li"], w["s5_ldt"], w["s5_bre"], w["s5_bim"])
    lbr_f, lbi_f = lbr.reshape(1, -1), lbi.reshape(1, -1)
    wbr, wbi = _blockdiag(bbr).astype(bf16), _blockdiag(bbi).astype(bf16)
    wcr, wci = _blockdiag(w["s5_cre"]).astype(bf16), _blockdiag(w["s5_cim"]).astype(bf16)
    ygb, ypre, h0s = _s5_fwd(u, lbr_f, lbi_f, wbr, wbi, wcr, wci, w["s5_d"], nb, s)
    x5, gp, gq = _glu_fwd(ygb, glu_a, glu_b, x4)
    x6 = ffn(x5, 1, 1, weights_of(4, [x5])["ffn"])
    loss, dx6, g["final_g"] = _final_loss(x6, w["final_g"], target)

    dx5 = ffn_back(dx6, 1, 1)

    def glu_bwd(d, p, q):
        sg = jax.nn.sigmoid(q)
        return d * sg, d * p * sg * (1.0 - sg)

    dp, dq = _ew("glu_bwd", glu_bwd, [dx5, gp, gq], [(D_MODEL, bf16), (D_MODEL, bf16)])
    dyg = _mm("glu_dy_a", dp, glu_a, "nt")[0]
    dyg = _mm("glu_dy_b", dq, glu_b, "nt", extras=[dyg], epilogue=lambda acc, r: (acc + r,))[0]
    g["glu_a"], ga_half = _mm_tn("glu_dw_a", ygb, dp)
    g["glu_b"], gb_half = _mm_tn("glu_dw_b", ygb, dq)
    dyg = send("glu", {"glu_a": slots(ga_half), "glu_b": slots(gb_half)}, dyg)
    du, dlr, dli, dwbr, dwbi, dwcr, dwci, g["s5_d"] = _s5_bwd(dyg, ypre, u, h0s, lbr_f, lbi_f, wbr, wbi, wcr, wci, w["s5_d"], nb, s)
    g["s5_cre"], g["s5_cim"] = _blockdiag_t(dwcr), _blockdiag_t(dwci)
    g["s5_lr"], g["s5_li"], g["s5_ldt"], g["s5_bre"], g["s5_bim"] = _s5_prep_bwd(
        w["s5_lr"], w["s5_li"], w["s5_ldt"], w["s5_bre"], w["s5_bim"],
        (dlr.reshape(S5_GROUPS, S5_STATE), dli.reshape(S5_GROUPS, S5_STATE), _blockdiag_t(dwbr), _blockdiag_t(dwbi)))
    dx4, g["mix_g_1"] = _norm_bwd("mix_norm_1_bwd", du, x4, w["mix_g"][1], dx5)
    dx3 = ffn_back(dx4, 1, 0)
    dx2 = ffn_back(dx3, 0, 1)
    dmerged = _mm("out_proj_dx", dx2, w_out, "nt")[0]
    g["w_out"], wo_half = _mm_tn("out_proj_dw", merged, dx2)
    dmerged = send("w_out", {"w_out": slots(wo_half)}, dmerged)
    dq_, dk_, dv_, dgate, g["ret_g"] = _ret_bwd(dmerged, o_raw, rprev, proj, cos, sin, w["ret_g"], nb, s)
    (dxl, dgl, g["conv_w"], g["conv_b"], g["lru_w_a"], g["lru_b_a"], g["lru_w_i"], g["lru_b_i"], g["lru_lam"]) = _lru_bwd(
        dmerged, proj, w["conv_w"], w["conv_b"], w["lru_w_a"], w["lru_b_a"], w["lru_w_i"], w["lru_b_i"], w["lru_lam"], nb, s)
    dproj = _ew("dproj", lambda *p: (jnp.concatenate(p, axis=1),), [dq_, dk_, dv_, dgate, dxl, dgl], [(3072, bf16)])[0]
    dh0 = _mm("in_proj_dx", dproj, w_in, "nt")[0]
    g["w_in"], wi_half = _mm_tn("in_proj_dw", h0b, dproj)
    dh0 = send("w_in", {"w_in": jnp.transpose(wi_half.reshape(D_MODEL, N_DEV, IN_SHARD), (1, 0, 2))}, dh0)
    dx1, g["mix_g_0"] = _norm_bwd("mix_norm_0_bwd", dh0, x1, w["mix_g"][0], dx2)
    dx0 = ffn_back(dx1, 0, 0)
    g["ffn_w1"], g["ffn_w3"], g["ffn_w2"] = ffn_bufs
    return loss, dx0, g


_WEIGHTS = ["ffn_norm_g", "ffn_w1", "ffn_w3", "ffn_w2", "mix_norm_g", "w_in_even", "w_out_even", "ret_norm_g", "conv_w",
            "conv_b", "lru_w_a", "lru_b_a", "lru_w_i", "lru_b_i", "lru_lambda", "s5_lambda_re", "s5_lambda_im", "s5_log_dt",
            "s5_b_re", "s5_b_im", "s5_c_re", "s5_c_im", "s5_d", "glu_w_a", "glu_w_b", "final_norm_g"]
_BIG = ["ffn_w1", "ffn_w3", "ffn_w2", "w_in_even", "w_out_even", "glu_w_a", "glu_w_b"]
_SMALL_SHARDED = ["ffn_norm_g", "conv_w", "s5_d"]
_SMALL = [n for n in _WEIGHTS if n not in _BIG]


def kernel(x, ffn_norm_g, ffn_w1, ffn_w3, ffn_w2, mix_norm_g, w_in_even, w_out_even, ret_norm_g, conv_w, conv_b, lru_w_a, lru_b_a, lru_w_i, lru_b_i, lru_lambda, s5_lambda_re, s5_lambda_im, s5_log_dt, s5_b_re, s5_b_im, s5_c_re, s5_c_im, s5_d, glu_w_a, glu_w_b, final_norm_g, loss_target, m_ffn_norm_g, m_ffn_w1, m_ffn_w3, m_ffn_w2, m_mix_norm_g, m_w_in_even, m_w_out_even, m_ret_norm_g, m_conv_w, m_conv_b, m_lru_w_a, m_lru_b_a, m_lru_w_i, m_lru_b_i, m_lru_lambda, m_s5_lambda_re, m_s5_lambda_im, m_s5_log_dt, m_s5_b_re, m_s5_b_im, m_s5_c_re, m_s5_c_im, m_s5_d, m_glu_w_a, m_glu_w_b, m_final_norm_g, v_ffn_norm_g, v_ffn_w1, v_ffn_w3, v_ffn_w2, v_mix_norm_g, v_w_in_even, v_w_out_even, v_ret_norm_g, v_conv_w, v_conv_b, v_lru_w_a, v_lru_b_a, v_lru_w_i, v_lru_b_i, v_lru_lambda, v_s5_lambda_re, v_s5_lambda_im, v_s5_log_dt, v_s5_b_re, v_s5_b_im, v_s5_c_re, v_s5_c_im, v_s5_d, v_glu_w_a, v_glu_w_b, v_final_norm_g):
    a = dict(locals())
    nb, s, d = x.shape
    ax, ay, ac = lax.axis_index("x"), lax.axis_index("y"), lax.axis_index("c")
    dev = 4 * ax + 2 * ay + ac
    chip = 2 * ax + ay

    def ffn_shards(l, h):
        extra = FF_PAD - FF_SHARD
        return [jnp.pad(ffn_w1[l, h].astype(bf16), ((0, 0), (0, extra))), jnp.pad(ffn_w3[l, h].astype(bf16), ((0, 0), (0, extra))),
                jnp.pad(ffn_w2[l, h].astype(bf16), ((0, extra), (0, 0)))]

    first = _all_gather("ag_first", ffn_shards(0, 0) + [_pack([ffn_norm_g, conv_w, s5_d])])
    sm = first[3].reshape(N_DEV, -1)
    ffn_g_full = jnp.transpose(sm[:, :512].reshape(N_DEV, 2, 2, 128), (1, 2, 0, 3)).reshape(2, 2, D_MODEL)
    conv_w_full = jnp.transpose(sm[:, 512:768].reshape(N_DEV, 4, 64), (1, 0, 2)).reshape(4, LRU_WIDTH)
    s5_d_full = sm[:, 768:896].reshape(1, D_MODEL)

    ag_src = [None, [w_in_even[0].astype(bf16), w_out_even[0].astype(bf16)], ffn_shards(0, 1),
              ffn_shards(1, 0) + [glu_w_a[0].astype(bf16), glu_w_b[0].astype(bf16)], ffn_shards(1, 1)]
    ag, token = [None], first[0]
    for k, grp in enumerate(ag_src):
        if grp is None:
            continue
        grp[0] = _tie(f"tie_ag_{k}", grp[0], [token])
        lands = [lax.dynamic_update_index_in_dim(lax.empty((N_DEV,) + t.shape, bf16), t, dev, 0) for t in grp]
        ag.append(_xchg_start(f"ag_start_{k}", "gather", grp, lands))
        token = ag[-1]["token"]

    def weights_of(k, after):
        if k == 0:
            return {"ffn": [first[0], _tie("tie_ag_started", first[1], [h["token"] for h in ag[1:]]), first[2]]}
        got = _xchg_wait(f"ag_wait_{k}", ag[k], after)
        if k == 1:
            return {"w_in": jnp.transpose(got[0], (1, 0, 2)).reshape(D_MODEL, N_DEV * IN_SHARD),
                    "w_out": got[1].reshape(D_MODEL, D_MODEL)}
        if k == 3:
            return {"ffn": got[:3], "glu_a": got[3].reshape(D_MODEL, D_MODEL), "glu_b": got[4].reshape(D_MODEL, D_MODEL)}
        return {"ffn": got}

    ffn_lands = [lax.empty((N_DEV - 1, 2, 2) + shp, bf16)
                 for shp in ((D_MODEL, FF_SHARD), (D_MODEL, FF_SHARD), (FF_SHARD, D_MODEL))]
    rs = []

    ffn_names = ("ffn_w1", "ffn_w3", "ffn_w2")

    def send(group, arrays, carry):
        srcs = list(arrays.values())
        if group.startswith("ffn_"):
            which = [ffn_names.index(n) for n in arrays]
            sfx = [(int(group[4]), int(group[5]))] * len(which)
            h = _xchg_start("rs_start_" + group, "scatter", srcs, [ffn_lands[k] for k in which], sfx)
            for k, land in zip(which, h["lands"]):
                ffn_lands[k] = land
        else:
            h = _xchg_start("rs_start_" + group, "scatter", srcs,
                            [lax.empty((N_DEV - 1,) + t.shape[1:], bf16) for t in srcs])
        rs.append((group, list(arrays), h))
        return _tie("tie_" + group, carry, [h["token"]])

    w = {
        "ffn_g": [[ffn_g_full[l, h].reshape(1, D_MODEL) for h in range(2)] for l in range(2)],
        "mix_g": [mix_norm_g[0:1], mix_norm_g[1:2]],
        "ret_g": ret_norm_g, "conv_w": conv_w_full, "conv_b": conv_b,
        "lru_w_a": lru_w_a[0], "lru_b_a": lru_b_a, "lru_w_i": lru_w_i[0], "lru_b_i": lru_b_i, "lru_lam": lru_lambda,
        "s5_lr": s5_lambda_re[0], "s5_li": s5_lambda_im[0], "s5_ldt": s5_log_dt.reshape(S5_GROUPS, 1),
        "s5_bre": jnp.swapaxes(s5_b_re[0], 1, 2), "s5_bim": jnp.swapaxes(s5_b_im[0], 1, 2),
        "s5_cre": s5_c_re[0], "s5_cim": s5_c_im[0], "s5_d": s5_d_full,
        "final_g": final_norm_g.reshape(1, D_MODEL),
    }

    small_grads = {}

    def last_small(g, carry):
        part = _small_partials(g)
        mine = _pack([part[n] for n in _SMALL])
        land = lax.dynamic_update_index_in_dim(lax.empty((N_DEV,) + mine.shape, f32), mine, dev, 0)
        h = _xchg_start("ag_start_small_grads", "gather", [mine], [land])
        small_grads.update(h=h, shapes=[part[n].shape for n in _SMALL])
        return _tie("tie_small_grads", carry, [h["token"]])

    loss_part, dx, g = _local_step(x.reshape(nb * s, d), loss_target.reshape(nb * s, d), w, weights_of, send, last_small,
                                   nb, s)
    loss = lax.psum(loss_part[0, 0], ("x", "y", "c"))
    (gath,) = _xchg_wait("ag_wait_small_grads", small_grads["h"], [dx])
    full = dict(zip(_SMALL, _unpack(_sum8("sum_small_grads", gath), small_grads["shapes"])))
    for n in _SMALL_SHARDED:
        width = a[n].shape[-1]
        full[n] = lax.dynamic_slice_in_dim(full[n], dev * width, width, axis=full[n].ndim - 1)
    shapes = [a[n].shape for n in _SMALL]
    packed = _adamw("adamw_small", _pack([a[n] for n in _SMALL]), _pack([a["m_" + n] for n in _SMALL]),
                    _pack([a["v_" + n] for n in _SMALL]), _pack([full[n] for n in _SMALL]))
    res = {n: vals for n, vals in zip(_SMALL, zip(*[_unpack(p, shapes) for p in packed]))}
    return _finish(a, g, dx, loss, res, packed, rs, ffn_lands, dev, nb, s, d)


def _small_partials(g):
    return {
        "ffn_norm_g": jnp.stack([jnp.stack([g[f"ffn_g_{l}{h}"][0] for h in range(2)]) for l in range(2)]),
        "mix_norm_g": jnp.concatenate([g["mix_g_0"], g["mix_g_1"]], axis=0),
        "ret_norm_g": g["ret_g"], "conv_w": g["conv_w"][None], "conv_b": g["conv_b"],
        "lru_w_a": g["lru_w_a"][None], "lru_b_a": g["lru_b_a"], "lru_w_i": g["lru_w_i"][None], "lru_b_i": g["lru_b_i"],
        "lru_lambda": g["lru_lam"], "s5_lambda_re": g["s5_lr"][None], "s5_lambda_im": g["s5_li"][None],
        "s5_log_dt": g["s5_ldt"].reshape(1, S5_GROUPS),
        "s5_b_re": jnp.swapaxes(g["s5_bre"], 1, 2)[None], "s5_b_im": jnp.swapaxes(g["s5_bim"], 1, 2)[None],
        "s5_c_re": g["s5_cre"][None], "s5_c_im": g["s5_cim"][None], "s5_d": g["s5_d"], "final_norm_g": g["final_g"][0],
    }


def _finish(a, g, dx, loss, res, packed, rs, ffn_lands, dev, nb, s, d):
    landed = {}
    for group, names, h in rs:
        if not group.startswith("ffn_"):
            landed.update(zip(names, _xchg_wait("rs_wait_" + group, h, [dx])))
    own = {n: lax.dynamic_index_in_dim(g[n], dev, axis=0, keepdims=False) for n in ("ffn_w1", "ffn_w3", "ffn_w2")}
    own["w_in"] = lax.dynamic_slice_in_dim(g["w_in"], dev * IN_SHARD, IN_SHARD, axis=1)
    for n in ("w_out", "glu_a", "glu_b"):
        own[n] = lax.dynamic_slice_in_dim(g[n], dev * (D_MODEL // N_DEV), D_MODEL // N_DEV, axis=0)

    def update(n, short):
        res[n] = _adamw("adamw_" + n, a[n], a["m_" + n], a["v_" + n], own[short].reshape(a[n].shape),
                        landed[short].reshape((N_DEV - 1,) + a[n].shape))

    for n, short in zip(_BIG[3:], ("w_in", "w_out", "glu_a", "glu_b")):
        update(n, short)
    after = [dx, packed[0]] + [res[n][0] for n in _BIG[3:]]
    for k, n in enumerate(("ffn_w1", "ffn_w3", "ffn_w2")):
        for group, names, h in rs:
            if group.startswith("ffn_") and names == [n]:
                (ffn_lands[k],) = _xchg_wait("rs_wait_" + group, h, after, [ffn_lands[k]])
        landed[n] = ffn_lands[k]
        update(n, n)
        after = after + [res[n][0]]

    out = [loss, dx.reshape(nb, s, d)]
    for k in range(4):
        out += [res[n][k] for n in _WEIGHTS]
    return tuple(out)
```

```python
import functools
import math

import numpy as np
import jax
import jax.numpy as jnp
from jax import lax
from jax.experimental import pallas as pl
from jax.experimental.pallas import tpu as pltpu

f32 = jnp.float32
bf16 = jnp.bfloat16

D_MODEL = 1024
N_DEV = 8
EPS = 1e-6
RET_HEADS = 4
HEAD_DIM = 128
RET_WIDTH = 512
RET_CHUNK = 128
ROPE_BASE = 10000.0
LRU_WIDTH = 512
LRU_BLOCKS = 4
LRU_C = 8.0
S5_GROUP = 16
S5_GROUPS = 64
S5_STATE = 64
S5_CHUNK = 128
S5_BLOCKS = 8
S5_BLOCK_STATES = 512
SUBLANES = 8
D_FF = 2816
FF_SHARD = D_FF // N_DEV
FF_PAD = 384
IN_SHARD = 3072 // N_DEV
ADAM_LR = 0.001
ADAM_B1 = 0.9
ADAM_B2 = 0.999
ADAM_EPS = 1e-08
ADAM_WD = 0.01
ADAM_STEP = 10

VMEM_LIMIT = 56 * 1024 * 1024
VMEM_SPEC = pl.BlockSpec(memory_space=pltpu.VMEM)
ANY_SPEC = pl.BlockSpec(memory_space=pl.ANY)
HBM_SPEC = pl.BlockSpec(memory_space=pltpu.HBM)
SEM_SPEC = pl.BlockSpec(memory_space=pltpu.SEMAPHORE)
SIDE_EFFECT = pltpu.SideEffectType.DATAFLOW_SIDE_EFFECTING
MESH = pl.DeviceIdType.MESH


def _cp(*sem):
    return pltpu.CompilerParams(dimension_semantics=sem, vmem_limit_bytes=VMEM_LIMIT)


def _nn(a, b):
    return jnp.dot(a, b, preferred_element_type=f32)


def _nt(a, b):
    return lax.dot_general(a, b, (((1,), (1,)), ((), ())), preferred_element_type=f32)


def _tn(a, b):
    return lax.dot_general(a, b, (((0,), (0,)), ((), ())), preferred_element_type=f32)


def _rms_fwd(x, g):
    r = lax.rsqrt(jnp.mean(x * x, axis=-1, keepdims=True) + EPS)
    xn = x * r
    return xn * g, xn, r


def _rms_bwd(dh, xn, r, g):
    dxn = dh * g
    dx = r * (dxn - xn * jnp.mean(dxn * xn, axis=-1, keepdims=True))
    dg = jnp.sum(dh * xn, axis=0, keepdims=True)
    return dx, dg


def _shift_dn(v, d, row, fill=0.0):
    return jnp.where(row >= d, pltpu.roll(v, d, 0), fill)


def _shift_up(v, d, row, fill=0.0):
    n = v.shape[0]
    return jnp.where(row < n - d, pltpu.roll(v, n - d, 0), fill)


def _ew(name, fn, ins, outs, tm=512):
    t = ins[0].shape[0]
    n_in = len(ins)

    def body(*refs):
        res = fn(*[r[...] for r in refs[:n_in]])
        for o, v in zip(refs[n_in:], res):
            o[...] = v.astype(o.dtype)

    return pl.pallas_call(
        body, name=name, grid=(t // tm,),
        in_specs=[pl.BlockSpec((tm, a.shape[1]), lambda i: (i, 0)) for a in ins],
        out_specs=[pl.BlockSpec((tm, n), lambda i: (i, 0)) for n, _ in outs],
        out_shape=[jax.ShapeDtypeStruct((t, n), dt) for n, dt in outs],
        compiler_params=_cp("parallel"),
    )(*ins)


def _mm(name, x, w, kind, extras=(), epilogue=None, outs=None, tm=512, tn=1024):
    t = x.shape[0]
    n = w.shape[1] if kind == "nn" else w.shape[0]
    tn = min(tn, n)
    outs = outs or [f32]
    n_ex = len(extras)

    def body(x_ref, w_ref, *refs):
        xb = x_ref[...].astype(bf16)
        acc = _nn(xb, w_ref[...]) if kind == "nn" else _nt(xb, w_ref[...])
        res = epilogue(acc, *[r[...] for r in refs[:n_ex]]) if epilogue else (acc,)
        for o, v in zip(refs[n_ex:], res):
            o[...] = v.astype(o.dtype)

    w_spec = (pl.BlockSpec((w.shape[0], tn), lambda i, j: (0, j)) if kind == "nn"
              else pl.BlockSpec((tn, w.shape[1]), lambda i, j: (j, 0)))
    tile = pl.BlockSpec((tm, tn), lambda i, j: (i, j))
    return pl.pallas_call(
        body, name=name, grid=(t // tm, n // tn),
        in_specs=[pl.BlockSpec((tm, x.shape[1]), lambda i, j: (i, 0)), w_spec] + [tile] * n_ex,
        out_specs=[tile] * len(outs),
        out_shape=[jax.ShapeDtypeStruct((t, n), dt) for dt in outs],
        compiler_params=_cp("parallel", "parallel"),
    )(x, w, *extras)


def _mm_tn(name, x, y, tk=1024, tn=1024, tt=1024):
    t, k = x.shape
    n = y.shape[1]
    tk, tn, tt = min(tk, k), min(tn, n), min(tt, t)

    def body(x_ref, y_ref, o_ref, ob_ref):
        @pl.when(pl.program_id(2) == 0)
        def _():
            o_ref[...] = jnp.zeros_like(o_ref)
        o_ref[...] += _tn(x_ref[...].astype(bf16), y_ref[...].astype(bf16))

        @pl.when(pl.program_id(2) == pl.num_programs(2) - 1)
        def _():
            ob_ref[...] = o_ref[...].astype(bf16)

    out = pl.BlockSpec((tk, tn), lambda i, j, s: (i, j))
    return pl.pallas_call(
        body, name=name, grid=(k // tk, n // tn, t // tt),
        in_specs=[pl.BlockSpec((tt, tk), lambda i, j, s: (s, i)), pl.BlockSpec((tt, tn), lambda i, j, s: (s, j))],
        out_specs=[out, out],
        out_shape=[jax.ShapeDtypeStruct((k, n), f32), jax.ShapeDtypeStruct((k, n), bf16)],
        compiler_params=_cp("parallel", "parallel", "arbitrary"),
    )(x, y)


def _norm_fwd(name, x, g, tm=512):
    t, d = x.shape

    def body(x_ref, g_ref, h_ref, hb_ref):
        h, _, _ = _rms_fwd(x_ref[...], g_ref[...])
        h_ref[...] = h
        hb_ref[...] = h.astype(bf16)

    row = pl.BlockSpec((tm, d), lambda i: (i, 0))
    return pl.pallas_call(
        body, name=name, grid=(t // tm,),
        in_specs=[row, pl.BlockSpec((1, d), lambda i: (0, 0))],
        out_specs=[row, row],
        out_shape=[jax.ShapeDtypeStruct((t, d), f32), jax.ShapeDtypeStruct((t, d), bf16)],
        compiler_params=_cp("parallel"),
    )(x, g)


def _norm_bwd(name, dh, x, g, dres, tm=512):
    t, d = x.shape

    def body(dh_ref, x_ref, g_ref, dres_ref, dx_ref, dg_ref):
        gv = g_ref[...]
        _, xn, r = _rms_fwd(x_ref[...], gv)
        dx, dg = _rms_bwd(dh_ref[...], xn, r, gv)
        dx_ref[...] = dres_ref[...] + dx

        @pl.when(pl.program_id(0) == 0)
        def _():
            dg_ref[...] = jnp.zeros_like(dg_ref)
        dg_ref[...] += dg

    row = pl.BlockSpec((tm, d), lambda i: (i, 0))
    vec = pl.BlockSpec((1, d), lambda i: (0, 0))
    return pl.pallas_call(
        body, name=name, grid=(t // tm,),
        in_specs=[row, row, vec, row],
        out_specs=[row, vec],
        out_shape=[jax.ShapeDtypeStruct((t, d), f32), jax.ShapeDtypeStruct((1, d), f32)],
        compiler_params=_cp("arbitrary"),
    )(dh, x, g, dres)


def _final_loss(x, g, target, tm=512):
    t, d = x.shape

    def body(x_ref, g_ref, t_ref, loss_ref, dx_ref, dg_ref):
        gv = g_ref[...]
        y, xn, r = _rms_fwd(x_ref[...], gv)
        err = y - t_ref[...]
        dy = err * (1.0 / d)
        dx, dg = _rms_bwd(dy, xn, r, gv)
        dx_ref[...] = dx

        @pl.when(pl.program_id(0) == 0)
        def _():
            dg_ref[...] = jnp.zeros_like(dg_ref)
            loss_ref[...] = jnp.zeros_like(loss_ref)
        dg_ref[...] += dg
        loss_ref[...] += jnp.full((1, 128), 0.5 / d, f32) * jnp.sum(err * err)

    row = pl.BlockSpec((tm, d), lambda i: (i, 0))
    vec = pl.BlockSpec((1, d), lambda i: (0, 0))
    return pl.pallas_call(
        body, name="final_loss", grid=(t // tm,),
        in_specs=[row, vec, row],
        out_specs=[pl.BlockSpec((1, 128), lambda i: (0, 0)), row, vec],
        out_shape=[jax.ShapeDtypeStruct((1, 128), f32), jax.ShapeDtypeStruct((t, d), f32),
                   jax.ShapeDtypeStruct((1, d), f32)],
        compiler_params=_cp("arbitrary"),
    )(x, g, target)


def _load_ffn_weights(hbm_refs, vmem_refs, sems):
    @pl.when(pl.program_id(0) == 0)
    def _():
        copies = []
        for k, (src, dst) in enumerate(zip(hbm_refs, vmem_refs)):
            for j in range(N_DEV):
                half = pl.ds((j % 2) * FF_PAD, FF_PAD)
                window = dst.at[j // 2, half, :] if k == 2 else dst.at[j // 2, :, half]
                copies.append(pltpu.make_async_copy(src.at[j], window, sems.at[k * N_DEV + j]))
        for cp in copies:
            cp.start()
        for cp in copies:
            cp.wait()


def _ffn_weight_scratch(nj, d, ff):
    return [pltpu.VMEM((nj, d, ff), bf16), pltpu.VMEM((nj, d, ff), bf16), pltpu.VMEM((nj, ff, d), bf16),
            pltpu.SemaphoreType.DMA((3 * N_DEV,))]


def _ffn_fwd(name, x, g, w1, w3, w2, tm=512):
    t, d = x.shape
    nj, ff = N_DEV // 2, 2 * FF_PAD

    def body(x_ref, g_ref, w1_hbm, w3_hbm, w2_hbm, y_ref, a_ref, b_ref, w1_ref, w3_ref, w2_ref, sems):
        _load_ffn_weights((w1_hbm, w3_hbm, w2_hbm), (w1_ref, w3_ref, w2_ref), sems)
        xv = x_ref[...]
        h, _, _ = _rms_fwd(xv, g_ref[...])
        hb = h.astype(bf16)
        acc = jnp.zeros((tm, d), f32)
        for j in range(nj):
            a = _nn(hb, w1_ref[j])
            b = _nn(hb, w3_ref[j])
            a_ref[j] = a.astype(bf16)
            b_ref[j] = b.astype(bf16)
            u = (a * jax.nn.sigmoid(a) * b).astype(bf16)
            acc = acc + _nn(u, w2_ref[j])
        y_ref[...] = xv + 0.5 * acc

    row = pl.BlockSpec((tm, d), lambda i: (i, 0))
    mid = pl.BlockSpec((nj, tm, ff), lambda i: (0, i, 0))
    return pl.pallas_call(
        body, name=name, grid=(t // tm,),
        in_specs=[row, pl.BlockSpec((1, d), lambda i: (0, 0)), ANY_SPEC, ANY_SPEC, ANY_SPEC],
        out_specs=[row, mid, mid],
        out_shape=[jax.ShapeDtypeStruct((t, d), f32), jax.ShapeDtypeStruct((nj, t, ff), bf16),
                   jax.ShapeDtypeStruct((nj, t, ff), bf16)],
        scratch_shapes=_ffn_weight_scratch(nj, d, ff),
        compiler_params=_cp("arbitrary"),
    )(x, g, w1, w3, w2)


def _ffn_dx(name, dy, x, g, w1, w3, w2, a, b, tm=256):
    t, d = x.shape
    nj, ff = N_DEV // 2, 2 * FF_PAD

    def body(dy_ref, x_ref, g_ref, w1_hbm, w3_hbm, w2_hbm, a_ref, b_ref,
             dx_ref, dg_ref, hbt_ref, dyh_ref, ut_ref, da_ref, db_ref, w1_ref, w3_ref, w2_ref, sems):
        _load_ffn_weights((w1_hbm, w3_hbm, w2_hbm), (w1_ref, w3_ref, w2_ref), sems)
        gv = g_ref[...]
        h, xn, r = _rms_fwd(x_ref[...], gv)
        hbt_ref[...] = h.astype(bf16).T
        dyv = dy_ref[...]
        dyh = (0.5 * dyv).astype(bf16)
        dyh_ref[...] = dyh
        dh = jnp.zeros((tm, d), f32)
        for j in range(nj):
            av = a_ref[j].astype(f32)
            bv = b_ref[j].astype(f32)
            s = jax.nn.sigmoid(av)
            silu = av * s
            ut_ref[j] = (silu * bv).astype(bf16).T
            du = _nt(dyh, w2_ref[j])
            dab = (du * bv * (s * (1.0 + av * (1.0 - s)))).astype(bf16)
            dbb = (du * silu).astype(bf16)
            da_ref[j] = dab
            db_ref[j] = dbb
            dh = dh + _nt(dab, w1_ref[j]) + _nt(dbb, w3_ref[j])
        dx, dg = _rms_bwd(dh, xn, r, gv)
        dx_ref[...] = dyv + dx

        @pl.when(pl.program_id(0) == 0)
        def _():
            dg_ref[...] = jnp.zeros_like(dg_ref)
        dg_ref[...] += dg

    row = pl.BlockSpec((tm, d), lambda i: (i, 0))
    vec = pl.BlockSpec((1, d), lambda i: (0, 0))
    mid = pl.BlockSpec((nj, tm, ff), lambda i: (0, i, 0))
    mid_shape = jax.ShapeDtypeStruct((nj, t, ff), bf16)
    return pl.pallas_call(
        body, name=name, grid=(t // tm,),
        in_specs=[row, row, vec, ANY_SPEC, ANY_SPEC, ANY_SPEC, mid, mid],
        out_specs=[row, vec, pl.BlockSpec((d, tm), lambda i: (0, i)), row,
                   pl.BlockSpec((nj, ff, tm), lambda i: (0, 0, i)), mid, mid],
        out_shape=[jax.ShapeDtypeStruct((t, d), f32), jax.ShapeDtypeStruct((1, d), f32),
                   jax.ShapeDtypeStruct((d, t), bf16), jax.ShapeDtypeStruct((t, d), bf16),
                   jax.ShapeDtypeStruct((nj, ff, t), bf16), mid_shape, mid_shape],
        scratch_shapes=_ffn_weight_scratch(nj, d, ff),
        compiler_params=_cp("arbitrary"),
    )(dy, x, g, w1, w3, w2, a, b)


def _ffn_dw_one(name, xt, y, buf, l, h, tt=2048):
    t = y.shape[-2]
    tt = min(tt, t)
    cut_cols = xt.ndim == 2

    def body(x_ref, y_ref, buf_ref, o_ref, ob_ref, acc):
        s = pl.program_id(1)
        prod = _nn(x_ref[0] if xt.ndim == 3 else x_ref[...], y_ref[0] if y.ndim == 3 else y_ref[...])

        @pl.when(s == 0)
        def _():
            acc[...] = prod

        @pl.when(s > 0)
        def _():
            acc[...] += prod

        @pl.when(s == pl.num_programs(1) - 1)
        def _():
            total = acc[...]
            for e in range(2):
                lo = e * FF_PAD
                part = total[:, lo:lo + FF_SHARD] if cut_cols else total[lo:lo + FF_SHARD, :]
                o_ref[e] = part
                ob_ref[e] = part.astype(bf16)

    x_spec = (pl.BlockSpec((1, xt.shape[1], tt), lambda p, s: (p, 0, s)) if xt.ndim == 3
              else pl.BlockSpec((xt.shape[0], tt), lambda p, s: (0, s)))
    y_spec = (pl.BlockSpec((1, tt, y.shape[2]), lambda p, s: (p, s, 0)) if y.ndim == 3
              else pl.BlockSpec((tt, y.shape[1]), lambda p, s: (s, 0)))
    k_, n_ = buf.shape[-2:]
    return pl.pallas_call(
        body, name=name, grid=(N_DEV // 2, t // tt),
        in_specs=[x_spec, y_spec, ANY_SPEC],
        out_specs=[pl.BlockSpec((2, None, None, k_, n_), lambda p, s: (p, l, h, 0, 0)),
                   pl.BlockSpec((2, k_, n_), lambda p, s: (p, 0, 0))],
        out_shape=[jax.ShapeDtypeStruct(buf.shape, buf.dtype), jax.ShapeDtypeStruct((N_DEV, k_, n_), bf16)],
        input_output_aliases={2: 0},
        scratch_shapes=[pltpu.VMEM((xt.shape[-2], y.shape[-1]), f32)],
        compiler_params=_cp("parallel", "arbitrary"),
    )(xt, y, buf)


_LOG_GAMMA = [float(np.log1p(-np.float32(2.0) ** np.float32(-5.0 - h))) for h in range(RET_HEADS)]


def _ret_consts(h):
    lg = jnp.where(h == 0, _LOG_GAMMA[0], jnp.where(h == 1, _LOG_GAMMA[1],
                   jnp.where(h == 2, _LOG_GAMMA[2], _LOG_GAMMA[3]))).astype(f32)
    c = RET_CHUNK
    r = lax.broadcasted_iota(jnp.int32, (c, c), 0)
    cc = lax.broadcasted_iota(jnp.int32, (c, c), 1)
    decay = jnp.where(r >= cc, jnp.exp(lg * jnp.maximum((r - cc).astype(f32), 0.0)), 0.0)
    pos = lax.broadcasted_iota(jnp.int32, (c, 1), 0).astype(f32)
    kd = jnp.exp(lg * (c - 1.0 - pos))
    qd = jnp.exp(lg * (pos + 1.0))
    gc = jnp.exp(lg * c)
    return decay, kd, qd, gc


def _rope(x, cos, sin):
    return x * cos + pltpu.roll(x, HEAD_DIM // 2, 1) * sin


def _rope_t(g, cos, sin):
    return g * cos + pltpu.roll(g * sin, HEAD_DIM // 2, 1)


def _rope_tables(s):
    half = HEAD_DIM // 2
    inv = ROPE_BASE ** (-jnp.arange(half, dtype=f32) / half)
    ang = jnp.arange(s, dtype=f32)[:, None] * inv[None, :]
    cos, sin = jnp.cos(ang), jnp.sin(ang)
    return jnp.concatenate([cos, cos], axis=1), jnp.concatenate([-sin, sin], axis=1)


def _head_ln(o):
    mu = jnp.mean(o, axis=-1, keepdims=True)
    oc = o - mu
    rs = lax.rsqrt(jnp.mean(oc * oc, axis=-1, keepdims=True) + EPS)
    return oc * rs, rs


def _ret_fwd(proj, cos, sin, ret_g, nb, s):
    c = RET_CHUNK
    nc = s // c
    t = nb * s
    scale = HEAD_DIM ** -0.5

    def body(q_ref, k_ref, v_ref, gate_ref, cos_ref, sin_ref, g_ref, o_ref, rprev_ref, m_ref):
        decay, kd, qd, gc = _ret_consts(pl.program_id(1))
        gv = g_ref[...]

        def step(n, rv):
            rows = pl.ds(pl.multiple_of(n * c, c), c)
            cs, sn = cos_ref[rows, :], sin_ref[rows, :]
            q = _rope(q_ref[rows, :], cs, sn)
            k = _rope(k_ref[rows, :], cs, sn) * scale
            vb = v_ref[rows, :].astype(bf16)
            sc = _nt(q.astype(bf16), k.astype(bf16)) * decay
            rprev_ref[n] = rv
            o = _nn(sc.astype(bf16), vb) + _nn((q * qd).astype(bf16), rv.astype(bf16))
            o_ref[rows, :] = o
            y, _ = _head_ln(o)
            gate = gate_ref[rows, :]
            m_ref[rows, :] = y * gv * (gate * jax.nn.sigmoid(gate))
            return rv * gc + _tn((k * kd).astype(bf16), vb)

        lax.fori_loop(0, nc, step, jnp.zeros((HEAD_DIM, HEAD_DIM), f32))

    def col(off):
        return pl.BlockSpec((s, HEAD_DIM), lambda b, h: (b, off + h))

    tab = pl.BlockSpec((s, HEAD_DIM), lambda b, h: (0, 0))
    return pl.pallas_call(
        body, name="ret_fwd", grid=(nb, RET_HEADS),
        in_specs=[col(0), col(4), col(8), col(12), tab, tab, pl.BlockSpec((1, HEAD_DIM), lambda b, h: (0, h))],
        out_specs=[col(0), pl.BlockSpec((nc, HEAD_DIM, HEAD_DIM), lambda b, h: (b * RET_HEADS + h, 0, 0)), col(0)],
        out_shape=[jax.ShapeDtypeStruct((t, RET_WIDTH), f32),
                   jax.ShapeDtypeStruct((nb * RET_HEADS * nc, HEAD_DIM, HEAD_DIM), f32),
                   jax.ShapeDtypeStruct((t, RET_WIDTH), f32)],
        compiler_params=_cp("parallel", "parallel"),
    )(proj, proj, proj, proj, cos, sin, ret_g)


def _ret_bwd(dmerged, o_raw, rprev, proj, cos, sin, ret_g, nb, s):
    c = RET_CHUNK
    nc = s // c
    t = nb * s
    scale = HEAD_DIM ** -0.5

    def body(dm_ref, o_ref, rprev_ref, q_ref, k_ref, v_ref, gate_ref, cos_ref, sin_ref, g_ref,
             dq_ref, dk_ref, dv_ref, dgate_ref, dg_ref):
        @pl.when(pl.program_id(1) == 0)
        def _():
            dg_ref[...] = jnp.zeros_like(dg_ref)
        decay, kd, qd, gc = _ret_consts(pl.program_id(0))
        gv = g_ref[...]

        def step(i, carry):
            drn, dg = carry
            n = nc - 1 - i
            rows = pl.ds(pl.multiple_of(n * c, c), c)
            cs, sn = cos_ref[rows, :], sin_ref[rows, :]
            q = _rope(q_ref[rows, :], cs, sn)
            k = _rope(k_ref[rows, :], cs, sn) * scale
            qb, kb = q.astype(bf16), k.astype(bf16)
            vb = v_ref[rows, :].astype(bf16)
            sc = _nt(qb, kb) * decay
            y, rs = _head_ln(o_ref[rows, :])
            gate = gate_ref[rows, :]
            sg = jax.nn.sigmoid(gate)
            silu = gate * sg
            dm = dm_ref[rows, :]
            dgate_ref[rows, :] = dm * y * gv * (sg * (1.0 + gate * (1.0 - sg)))
            dyl = dm * gv * silu
            dg = dg + jnp.sum(dm * y * silu, axis=0, keepdims=True)
            do = rs * (dyl - jnp.mean(dyl, axis=-1, keepdims=True) - y * jnp.mean(dyl * y, axis=-1, keepdims=True))
            dob = do.astype(bf16)
            rv = rprev_ref[n]
            drb = drn.astype(bf16)
            ds = (_nt(dob, vb) * decay).astype(bf16)
            kdb = (k * kd).astype(bf16)
            qdb = (q * qd).astype(bf16)
            dq_r = _nn(ds, kb) + _nt(dob, rv.astype(bf16)) * qd
            dk_r = _tn(ds, qb) + _nt(vb, drb) * kd
            dv_ref[rows, :] = _tn(sc.astype(bf16), dob) + _nn(kdb, drb)
            dq_ref[rows, :] = _rope_t(dq_r, cs, sn)
            dk_ref[rows, :] = _rope_t(dk_r * scale, cs, sn)
            return drn * gc + _tn(qdb, dob), dg

        _, dg = lax.fori_loop(0, nc, step, (jnp.zeros((HEAD_DIM, HEAD_DIM), f32), jnp.zeros((1, HEAD_DIM), f32)))
        dg_ref[...] += dg

    def col(off):
        return pl.BlockSpec((s, HEAD_DIM), lambda h, b: (b, off + h))

    tab = pl.BlockSpec((s, HEAD_DIM), lambda h, b: (0, 0))
    gsp = pl.BlockSpec((1, HEAD_DIM), lambda h, b: (0, h))
    out_t = jax.ShapeDtypeStruct((t, RET_WIDTH), f32)
    return pl.pallas_call(
        body, name="ret_bwd", grid=(RET_HEADS, nb),
        in_specs=[col(0), col(0), pl.BlockSpec((nc, HEAD_DIM, HEAD_DIM), lambda h, b: (b * RET_HEADS + h, 0, 0)),
                  col(0), col(4), col(8), col(12), tab, tab, gsp],
        out_specs=[col(0), col(0), col(0), col(0), gsp],
        out_shape=[out_t, out_t, out_t, out_t, jax.ShapeDtypeStruct((1, RET_WIDTH), f32)],
        compiler_params=_cp("parallel", "arbitrary"),
    )(dmerged, o_raw, rprev, proj, proj, proj, proj, cos, sin, ret_g)


def _neg_expm1(z):
    series = -(z * (1.0 + z * (0.5 + z * (1.0 / 6.0 + z * (1.0 / 24.0)))))
    return jnp.where(z > -0.01, series, 1.0 - jnp.exp(z))


def _lru_gates(xc, pa, pi, lam):
    r = jax.nn.sigmoid(pa)
    i = jax.nn.sigmoid(pi)
    log_a = -LRU_C * r * jax.nn.softplus(-lam)
    a = jnp.exp(log_a)
    bx = jnp.sqrt(_neg_expm1(2.0 * log_a)) * i * xc
    return a, bx


def _scan_rows(a, b, row, up):
    sub = row[:SUBLANES] & (SUBLANES - 1)
    groups = list(range(a.shape[0] // SUBLANES))
    out = [None] * len(groups)
    edge = slice(0, 1) if up else slice(SUBLANES - 1, SUBLANES)
    carry = jnp.zeros((1, a.shape[1]), f32)
    for g in (reversed(groups) if up else groups):
        rows = slice(g * SUBLANES, (g + 1) * SUBLANES)
        xa, xb = a[rows], b[rows]
        d = 1
        while d < SUBLANES:
            keep = (sub < SUBLANES - d) if up else (sub >= d)
            shift = SUBLANES - d if up else d
            xb = xa * jnp.where(keep, pltpu.roll(xb, shift, 0), 0.0) + xb
            xa = xa * jnp.where(keep, pltpu.roll(xa, shift, 0), 1.0)
            d *= 2
        out[g] = xb + xa * carry
        carry = out[g][edge]
    return jnp.concatenate(out, axis=0)


def _scan_fwd(a, b, row):
    return _scan_rows(a, b, row, False)


def _scan_bwd(c, b, row):
    return _scan_rows(c, b, row, True)


def _conv_fwd(x, cw, cb, row):
    return (cb + cw[3:4] * x + cw[2:3] * _shift_dn(x, 1, row) + cw[1:2] * _shift_dn(x, 2, row)
            + cw[0:1] * _shift_dn(x, 3, row))


def _lru_specs(s, order):
    def im(f):
        return (lambda b, g: f(b, g)) if order == "bg" else (lambda g, b: f(b, g))
    seq = lambda off: pl.BlockSpec((s, 128), im(lambda b, g: (b, off + g)))
    vec = pl.BlockSpec((1, 128), im(lambda b, g: (0, g)))
    cw = pl.BlockSpec((4, 128), im(lambda b, g: (0, g)))
    mat = pl.BlockSpec((1, 128, 128), im(lambda b, g: (g, 0, 0)))
    return seq, vec, cw, mat


def _lru_fwd(proj, conv_w, conv_b, w_a, b_a, w_i, b_i, lam, nb, s):
    def body(x_ref, gt_ref, cw_ref, cb_ref, wa_ref, ba_ref, wi_ref, bi_ref, lam_ref, out_ref):
        row = lax.broadcasted_iota(jnp.int32, (s, 128), 0)
        xc = _conv_fwd(x_ref[...], cw_ref[...], cb_ref[...], row)
        xcb = xc.astype(bf16)
        pa = _nn(xcb, wa_ref[0].astype(bf16)) + ba_ref[...]
        pi = _nn(xcb, wi_ref[0].astype(bf16)) + bi_ref[...]
        a, bx = _lru_gates(xc, pa, pi, lam_ref[...])
        h = _scan_fwd(a, bx, row)
        out_ref[...] = h * jax.nn.gelu(gt_ref[...])

    seq, vec, cw, mat = _lru_specs(s, "bg")
    return pl.pallas_call(
        body, name="lru_fwd", grid=(nb, LRU_BLOCKS),
        in_specs=[seq(16), seq(20), cw, vec, mat, vec, mat, vec, vec],
        out_specs=seq(0),
        out_shape=jax.ShapeDtypeStruct((nb * s, LRU_WIDTH), f32),
        compiler_params=_cp("parallel", "parallel"),
    )(proj, proj, conv_w, conv_b, w_a, b_a, w_i, b_i, lam)


def _lru_bwd(dmerged, proj, conv_w, conv_b, w_a, b_a, w_i, b_i, lam, nb, s):
    def body(dout_ref, x_ref, gt_ref, cw_ref, cb_ref, wa_ref, ba_ref, wi_ref, bi_ref, lam_ref,
             dx_ref, dgt_ref, dcw_ref, dcb_ref, dwa_ref, dba_ref, dwi_ref, dbi_ref, dlam_ref):
        row = lax.broadcasted_iota(jnp.int32, (s, 128), 0)
        x = x_ref[...]
        cwv = cw_ref[...]
        xc = _conv_fwd(x, cwv, cb_ref[...], row)
        xcb = xc.astype(bf16)
        wab, wib = wa_ref[0].astype(bf16), wi_ref[0].astype(bf16)
        pa = _nn(xcb, wab) + ba_ref[...]
        pi = _nn(xcb, wib) + bi_ref[...]
        (a, bx), gates_vjp = jax.vjp(_lru_gates, xc, pa, pi, lam_ref[...])
        h = _scan_fwd(a, bx, row)
        ge, gelu_vjp = jax.vjp(jax.nn.gelu, gt_ref[...])
        dout = dout_ref[...]
        dgt_ref[...] = gelu_vjp(dout * h)[0]
        adj = _scan_bwd(_shift_up(a, 1, row), dout * ge, row)
        dxc, dpa, dpi, dlam = gates_vjp((adj * _shift_dn(h, 1, row), adj))
        dpab, dpib = dpa.astype(bf16), dpi.astype(bf16)
        dxc = dxc + _nt(dpab, wab) + _nt(dpib, wib)
        dx_ref[...] = (cwv[3:4] * dxc + cwv[2:3] * _shift_up(dxc, 1, row) + cwv[1:2] * _shift_up(dxc, 2, row)
                       + cwv[0:1] * _shift_up(dxc, 3, row))

        @pl.when(pl.program_id(1) == 0)
        def _():
            for r in (dcw_ref, dcb_ref, dwa_ref, dba_ref, dwi_ref, dbi_ref, dlam_ref):
                r[...] = jnp.zeros_like(r)
        rsum = lambda v: jnp.sum(v, axis=0, keepdims=True)
        dcw_ref[...] += jnp.concatenate([rsum(dxc * _shift_dn(x, 3, row)), rsum(dxc * _shift_dn(x, 2, row)),
                                         rsum(dxc * _shift_dn(x, 1, row)), rsum(dxc * x)], axis=0)
        dcb_ref[...] += rsum(dxc)
        dwa_ref[0] += _tn(xcb, dpab)
        dwi_ref[0] += _tn(xcb, dpib)
        dba_ref[...] += rsum(dpa)
        dbi_ref[...] += rsum(dpi)
        dlam_ref[...] += dlam

    seq, vec, cw, mat = _lru_specs(s, "gb")
    t = nb * s
    vshape = jax.ShapeDtypeStruct((1, LRU_WIDTH), f32)
    mshape = jax.ShapeDtypeStruct((LRU_BLOCKS, 128, 128), f32)
    return pl.pallas_call(
        body, name="lru_bwd", grid=(LRU_BLOCKS, nb),
        in_specs=[seq(4), seq(16), seq(20), cw, vec, mat, vec, mat, vec, vec],
        out_specs=[seq(0), seq(0), cw, vec, mat, vec, mat, vec, vec],
        out_shape=[jax.ShapeDtypeStruct((t, LRU_WIDTH), f32), jax.ShapeDtypeStruct((t, LRU_WIDTH), f32),
                   jax.ShapeDtypeStruct((4, LRU_WIDTH), f32), vshape, mshape, vshape, mshape, vshape, vshape],
        compiler_params=_cp("parallel", "arbitrary"),
    )(dmerged, proj, proj, conv_w, conv_b, w_a, b_a, w_i, b_i, lam)


def _s5_disc(lr, li, ldt, bre, bim):
    dt = jnp.exp(ldt)
    mag = jnp.exp(lr * dt)
    lbr = mag * jnp.cos(li * dt)
    lbi = mag * jnp.sin(li * dt)
    den = lr * lr + li * li
    nr = lbr - 1.0
    fr = (nr * lr + lbi * li) / den
    fi = (lbi * lr - nr * li) / den
    bbr = fr[:, None, :] * bre - fi[:, None, :] * bim
    bbi = fr[:, None, :] * bim + fi[:, None, :] * bre
    return lbr, lbi, bbr, bbi


def _s5_prep(lr, li, ldt, bre, bim):
    def body(lr_ref, li_ref, ldt_ref, bre_ref, bim_ref, o1, o2, o3, o4):
        o1[...], o2[...], o3[...], o4[...] = _s5_disc(lr_ref[...], li_ref[...], ldt_ref[...], bre_ref[...], bim_ref[...])

    return pl.pallas_call(
        body, name="s5_prep", in_specs=[VMEM_SPEC] * 5, out_specs=[VMEM_SPEC] * 4,
        out_shape=[jax.ShapeDtypeStruct(lr.shape, f32), jax.ShapeDtypeStruct(lr.shape, f32),
                   jax.ShapeDtypeStruct(bre.shape, f32), jax.ShapeDtypeStruct(bre.shape, f32)],
    )(lr, li, ldt, bre, bim)


def _s5_prep_bwd(lr, li, ldt, bre, bim, cts):
    def body(lr_ref, li_ref, ldt_ref, bre_ref, bim_ref, g1, g2, g3, g4, o1, o2, o3, o4, o5):
        _, vjp = jax.vjp(_s5_disc, lr_ref[...], li_ref[...], ldt_ref[...], bre_ref[...], bim_ref[...])
        o1[...], o2[...], o3[...], o4[...], o5[...] = vjp((g1[...], g2[...], g3[...], g4[...]))

    return pl.pallas_call(
        body, name="s5_prep_bwd", in_specs=[VMEM_SPEC] * 9, out_specs=[VMEM_SPEC] * 5,
        out_shape=[jax.ShapeDtypeStruct(v.shape, f32) for v in (lr, li, ldt, bre, bim)],
    )(lr, li, ldt, bre, bim, *cts)


def _cmul(ar, ai, br, bi):
    return ar * br - ai * bi, ar * bi + ai * br


def _s5_pow_table(lr, li, n, row, up):
    ar = jnp.broadcast_to(lr, (n, lr.shape[1]))
    ai = jnp.broadcast_to(li, (n, li.shape[1]))
    shift = _shift_up if up else _shift_dn
    d = 1
    while d < n:
        ar, ai = _cmul(ar, ai, shift(ar, d, row, 1.0), shift(ai, d, row, 0.0))
        d *= 2
    return ar, ai


def _s5_step_factors(lr, li, row, up):
    sub = row & (SUBLANES - 1)
    out, pr, pi, d = [], lr, li, 1
    while d < SUBLANES:
        keep = (sub < SUBLANES - d) if up else (sub >= d)
        out.append((jnp.where(keep, pr, 0.0), jnp.where(keep, pi, 0.0)))
        pr, pi = _cmul(pr, pi, pr, pi)
        d *= 2
    return out


def _s5_scan(br, bi, steps, tab_r, tab_i, cr, ci, up):
    groups = list(range(br.shape[0] // SUBLANES))
    out_r, out_i = [None] * len(groups), [None] * len(groups)
    edge = slice(0, 1) if up else slice(SUBLANES - 1, SUBLANES)
    for g in (reversed(groups) if up else groups):
        rows = slice(g * SUBLANES, (g + 1) * SUBLANES)
        xr, xi = br[rows], bi[rows]
        for k, (mr, mi) in enumerate(steps):
            shift = SUBLANES - (1 << k) if up else 1 << k
            tr, ti = _cmul(mr, mi, pltpu.roll(xr, shift, 0), pltpu.roll(xi, shift, 0))
            xr, xi = xr + tr, xi + ti
        tr, ti = _cmul(tab_r, tab_i, cr, ci)
        hr, hi = xr + tr, xi + ti
        out_r[g], out_i[g] = hr, hi
        cr, ci = hr[edge], hi[edge]
    return jnp.concatenate(out_r, axis=0), jnp.concatenate(out_i, axis=0)


def _s5_specs(t, nb, nc):
    seq = pl.BlockSpec((t, 128), lambda k: (0, k))
    lvec = pl.BlockSpec((1, S5_BLOCK_STATES), lambda k: (0, k))
    dvec = pl.BlockSpec((1, 128), lambda k: (0, k))
    wmat = pl.BlockSpec((1, 128, S5_BLOCK_STATES), lambda k: (k, 0, 0))
    h0 = pl.BlockSpec((nb, None, nc, 2, S5_BLOCK_STATES), lambda k: (0, k, 0, 0, 0))
    return seq, lvec, dvec, wmat, h0


def _s5_fwd(u, lbr, lbi, wbr, wbi, wcr, wci, dskip, nb, s):
    ln = S5_CHUNK
    nc = s // ln

    def body(u_ref, lr_ref, li_ref, wbr_ref, wbi_ref, wcr_ref, wci_ref, d_ref, yg_ref, y_ref, h0_ref):
        row = lax.broadcasted_iota(jnp.int32, (ln, S5_BLOCK_STATES), 0)
        lr, li = lr_ref[...], li_ref[...]
        pr, pi = _s5_pow_table(lr, li, SUBLANES, row[:SUBLANES], False)
        steps = _s5_step_factors(lr, li, row[:SUBLANES], False)
        dv = d_ref[...]

        def chunk(b, n, h0r, h0i):
            st = pl.multiple_of(b * s + n * ln, ln)
            uc = u_ref[pl.ds(st, ln), :]
            ub = uc.astype(bf16)
            hr, hi = _s5_scan(_nn(ub, wbr_ref[0]), _nn(ub, wbi_ref[0]), steps, pr, pi, h0r, h0i, False)
            h0_ref[b, n, 0:1, :] = h0r
            h0_ref[b, n, 1:2, :] = h0i
            y = _nt(hr.astype(bf16), wcr_ref[0]) - _nt(hi.astype(bf16), wci_ref[0]) + dv * uc
            y_ref[pl.ds(st, ln), :] = y
            yg_ref[pl.ds(st, ln), :] = jax.nn.gelu(y).astype(bf16)
            return hr[ln - 1:ln, :], hi[ln - 1:ln, :]

        def step(n, carry):
            return tuple(chunk(b, n, *carry[b]) for b in range(nb))

        z = jnp.zeros((1, S5_BLOCK_STATES), f32)
        lax.fori_loop(0, nc, step, ((z, z),) * nb)

    t = nb * s
    seq, lvec, dvec, wmat, h0 = _s5_specs(t, nb, nc)
    return pl.pallas_call(
        body, name="s5_fwd", grid=(S5_BLOCKS,),
        in_specs=[seq, lvec, lvec, wmat, wmat, wmat, wmat, dvec],
        out_specs=[seq, seq, h0],
        out_shape=[jax.ShapeDtypeStruct((t, D_MODEL), bf16), jax.ShapeDtypeStruct((t, D_MODEL), f32),
                   jax.ShapeDtypeStruct((nb, S5_BLOCKS, nc, 2, S5_BLOCK_STATES), f32)],
        compiler_params=_cp("parallel"),
    )(u, lbr, lbi, wbr, wbi, wcr, wci, dskip)


def _s5_bwd(dyg, y, u, h0, lbr, lbi, wbr, wbi, wcr, wci, dskip, nb, s):
    ln = S5_CHUNK
    nc = s // ln

    def body(dyg_ref, y_ref, u_ref, h0_ref, lr_ref, li_ref, wbr_ref, wbi_ref, wcr_ref, wci_ref, d_ref,
             du_ref, dlr_ref, dli_ref, dwbr_ref, dwbi_ref, dwcr_ref, dwci_ref, dd_ref):
        for r in (dlr_ref, dli_ref, dwbr_ref, dwbi_ref, dwcr_ref, dwci_ref, dd_ref):
            r[...] = jnp.zeros_like(r)
        row = lax.broadcasted_iota(jnp.int32, (ln, S5_BLOCK_STATES), 0)
        lr, li = lr_ref[...], li_ref[...]
        pr, pi = _s5_pow_table(lr, li, SUBLANES, row[:SUBLANES], False)
        qr, qi = _s5_pow_table(lr, -li, SUBLANES, row[:SUBLANES], True)
        row8 = row[:SUBLANES]
        steps_dn, steps_up = _s5_step_factors(lr, li, row8, False), _s5_step_factors(lr, -li, row8, True)
        dv = d_ref[...]
        rsum = lambda v: jnp.sum(v, axis=0, keepdims=True)

        def chunk(b, n, gnr, gni):
            st = pl.multiple_of(b * s + n * ln, ln)
            uc = u_ref[pl.ds(st, ln), :]
            ub = uc.astype(bf16)
            h0v = h0_ref[b, n]
            h0r, h0i = h0v[0:1], h0v[1:2]
            hr, hi = _s5_scan(_nn(ub, wbr_ref[0]), _nn(ub, wbi_ref[0]), steps_dn, pr, pi, h0r, h0i, False)
            dy = jax.vjp(jax.nn.gelu, y_ref[pl.ds(st, ln), :])[1](dyg_ref[pl.ds(st, ln), :])[0]
            dyb = dy.astype(bf16)
            dd_ref[...] += rsum(dy * uc)
            gr, gi = _s5_scan(_nn(dyb, wcr_ref[0]), -_nn(dyb, wci_ref[0]), steps_up, qr, qi, gnr, gni, True)
            hpr = jnp.where(row >= 1, pltpu.roll(hr, 1, 0), h0r)
            hpi = jnp.where(row >= 1, pltpu.roll(hi, 1, 0), h0i)
            dlr_ref[...] += rsum(gr * hpr + gi * hpi)
            dli_ref[...] += rsum(gi * hpr - gr * hpi)
            grb, gib = gr.astype(bf16), gi.astype(bf16)
            dwbr_ref[0] += _tn(ub, grb)
            dwbi_ref[0] += _tn(ub, gib)
            dwcr_ref[0] += _tn(dyb, hr.astype(bf16))
            dwci_ref[0] -= _tn(dyb, hi.astype(bf16))
            du_ref[pl.ds(st, ln), :] = _nt(grb, wbr_ref[0]) + _nt(gib, wbi_ref[0]) + dv * dy
            return gr[0:1, :], gi[0:1, :]

        def step(i, carry):
            return tuple(chunk(b, nc - 1 - i, *carry[b]) for b in range(nb))

        z = jnp.zeros((1, S5_BLOCK_STATES), f32)
        lax.fori_loop(0, nc, step, ((z, z),) * nb)

    t = nb * s
    seq, lvec, dvec, wmat, h0s = _s5_specs(t, nb, nc)
    lshape = jax.ShapeDtypeStruct((1, S5_BLOCKS * S5_BLOCK_STATES), f32)
    wshape = jax.ShapeDtypeStruct((S5_BLOCKS, 128, S5_BLOCK_STATES), f32)
    return pl.pallas_call(
        body, name="s5_bwd", grid=(S5_BLOCKS,),
        in_specs=[seq, seq, seq, h0s, lvec, lvec, wmat, wmat, wmat, wmat, dvec],
        out_specs=[seq, lvec, lvec, wmat, wmat, wmat, wmat, dvec],
        out_shape=[jax.ShapeDtypeStruct((t, D_MODEL), f32), lshape, lshape, wshape, wshape, wshape, wshape,
                   jax.ShapeDtypeStruct((1, D_MODEL), f32)],
        compiler_params=_cp("parallel"),
    )(dyg, y, u, h0, lbr, lbi, wbr, wbi, wcr, wci, dskip)


def _blockdiag(w):
    w4 = w.reshape(S5_BLOCKS, 8, S5_GROUP, S5_STATE)
    same_group = jnp.eye(8, dtype=bool)[None, :, None, :, None]
    return jnp.where(same_group, w4[:, :, :, None, :], 0.0).reshape(S5_BLOCKS, 128, S5_BLOCK_STATES)


def _blockdiag_t(dw):
    d5 = dw.reshape(S5_BLOCKS, 8, S5_GROUP, 8, S5_STATE)
    diag = jnp.diagonal(d5, axis1=1, axis2=3)
    return jnp.moveaxis(diag, 3, 1).reshape(S5_GROUPS, S5_GROUP, S5_STATE)


def _glu_fwd(ygb, wa, wb, x, tm=512, tn=1024):
    t, d = x.shape

    def body(y_ref, wa_ref, wb_ref, x_ref, o_ref, p_ref, q_ref):
        p = _nn(y_ref[...], wa_ref[...])
        q = _nn(y_ref[...], wb_ref[...])
        p_ref[...] = p
        q_ref[...] = q
        o_ref[...] = x_ref[...] + p * jax.nn.sigmoid(q)

    tile = pl.BlockSpec((tm, tn), lambda i, j: (i, j))
    wsp = pl.BlockSpec((d, tn), lambda i, j: (0, j))
    out = jax.ShapeDtypeStruct((t, d), f32)
    return pl.pallas_call(
        body, name="glu_fwd", grid=(t // tm, d // tn),
        in_specs=[pl.BlockSpec((tm, d), lambda i, j: (i, 0)), wsp, wsp, tile],
        out_specs=[tile, tile, tile], out_shape=[out, out, out],
        compiler_params=_cp("parallel", "parallel"),
    )(ygb, wa, wb, x)


def _place():
    x, y, c = lax.axis_index("x"), lax.axis_index("y"), lax.axis_index("c")
    return x, y, c, [(1 - x, y), (x, 1 - y), (1 - x, 1 - y)]


def _all_gather(name, arrays):
    n = len(arrays)

    def body(*refs):
        ins, outs = refs[:n], refs[n:2 * n]
        send_sems, recv_sems, local_sems = refs[2 * n:]
        x, y, c, chips = _place()
        me, sib = (x, y, c), (x, y, 1 - c)

        def copy(i, k, block, to, src=None):
            dst = outs[i].at[4 * block[0] + 2 * block[1] + block[2]]
            return pltpu.make_async_remote_copy(
                src_ref=dst if src is None else src, dst_ref=dst,
                send_sem=send_sems.at[i * 7 + k], recv_sem=recv_sems.at[i * 7 + k],
                device_id=to, device_id_type=MESH)

        mine = [pltpu.make_async_copy(ins[i], outs[i].at[4 * x + 2 * y + c], local_sems.at[i]) for i in range(n)]
        for m in mine:
            m.start()
        first = []
        for i in range(n):
            first.append(copy(i, 0, me, sib, src=ins[i]))
            first += [copy(i, 1 + j, me, (*chip, c), src=ins[i]) for j, chip in enumerate(chips)]
        for cp in first:
            cp.start()
        passed = []
        for j, chip in enumerate(chips):
            for i in range(n):
                copy(i, 1 + j, (*chip, c), me).wait_recv()
                fwd = copy(i, 4 + j, (*chip, c), sib)
                fwd.start()
                passed.append(fwd)
        for i in range(n):
            copy(i, 0, sib, me).wait_recv()
        for j, chip in enumerate(chips):
            for i in range(n):
                copy(i, 4 + j, (*chip, 1 - c), me).wait_recv()
        for cp in first + passed:
            cp.wait_send()
        for m in mine:
            m.wait()

    return pl.pallas_call(
        body, name=name,
        in_specs=[ANY_SPEC] * n, out_specs=[ANY_SPEC] * n,
        out_shape=[jax.ShapeDtypeStruct((N_DEV,) + a.shape, a.dtype) for a in arrays],
        scratch_shapes=[pltpu.SemaphoreType.DMA((7 * n,)), pltpu.SemaphoreType.DMA((7 * n,)),
                        pltpu.SemaphoreType.DMA((n,))],
    )(*arrays)


def _tie(name, x, deps):
    def body(*refs):
        pass

    return pl.pallas_call(
        body, name=name, in_specs=[ANY_SPEC] * (1 + len(deps)), out_specs=ANY_SPEC,
        out_shape=jax.ShapeDtypeStruct(x.shape, x.dtype), input_output_aliases={0: 0},
    )(x, *deps)


def _xchg_copies(kind, srcs, lands, suffixes, send_sems, recv_sems):
    x, y, c, _ = _place()
    copies = []
    for i, (src, land, sfx) in enumerate(zip(srcs, lands, suffixes)):
        for k in range(N_DEV - 1):
            r = k + 1
            peer = (1 - x if r & 4 else x, 1 - y if r & 2 else y, 1 - c if r & 1 else c)
            if kind == "gather":
                s_ref, d_ref = src, land.at[(4 * x + 2 * y + c,) + sfx]
            else:
                s_ref, d_ref = src.at[4 * peer[0] + 2 * peer[1] + peer[2]], land.at[(k,) + sfx]
            copies.append(pltpu.make_async_remote_copy(
                src_ref=s_ref, dst_ref=d_ref, send_sem=send_sems.at[i * 7 + k], recv_sem=recv_sems.at[i * 7 + k],
                device_id=peer, device_id_type=MESH))
    return copies


def _xchg_start(name, kind, srcs, lands, suffixes=None):
    n = len(srcs)
    suffixes = suffixes or [()] * n

    def body(*refs):
        src, land = refs[:n], refs[n:2 * n]
        send_sems, recv_sems, token = refs[2 * n], refs[2 * n + 1], refs[-1]
        for cp in _xchg_copies(kind, src, land, suffixes, send_sems, recv_sems):
            cp.start()
        token[...] = jnp.zeros_like(token)

    arrays = list(srcs) + list(lands)
    outs = pl.pallas_call(
        body, name=name,
        out_shape=(pltpu.SemaphoreType.DMA((7 * n,)), pltpu.SemaphoreType.DMA((7 * n,)),
                   *[pltpu.HBM(a.shape, a.dtype) for a in arrays], jax.ShapeDtypeStruct((8, 128), f32)),
        in_specs=[HBM_SPEC] * (2 * n),
        out_specs=(SEM_SPEC, SEM_SPEC, *[HBM_SPEC] * (2 * n), VMEM_SPEC),
        input_output_aliases={i: 2 + i for i in range(2 * n)},
        compiler_params=pltpu.CompilerParams(has_side_effects=SIDE_EFFECT),
    )(*[pltpu.with_memory_space_constraint(a, pltpu.HBM) for a in arrays])
    return dict(kind=kind, n=n, suffixes=suffixes, send=outs[0], recv=outs[1], srcs=list(outs[2:2 + n]),
                lands=list(outs[2 + n:2 + 2 * n]), token=outs[-1])


def _xchg_wait(name, h, after, lands=None):
    n = h["n"]
    lands = h["lands"] if lands is None else lands

    def body(*refs):
        src, land = refs[:n], refs[n:2 * n]
        for cp in _xchg_copies(h["kind"], src, land, h["suffixes"], refs[2 * n], refs[2 * n + 1]):
            cp.wait_send()
            cp.wait_recv()

    arrays = h["srcs"] + list(lands)
    outs = pl.pallas_call(
        body, name=name,
        out_shape=tuple(pltpu.HBM(a.shape, a.dtype) for a in arrays),
        in_specs=[HBM_SPEC] * (2 * n) + [SEM_SPEC, SEM_SPEC] + [ANY_SPEC] * len(after),
        out_specs=tuple([HBM_SPEC] * (2 * n)),
        input_output_aliases={i: i for i in range(2 * n)},
        compiler_params=pltpu.CompilerParams(has_side_effects=SIDE_EFFECT),
    )(*arrays, h["send"], h["recv"], *after)
    return list(outs[n:])


def _rows(a):
    return a.reshape(-1, a.shape[-1])


def _row_tile(r):
    for tm in (512, 256, 128, 64, 32, 16, 8):
        if r % tm == 0:
            return tm
    return r


def _sum8(name, gathered):
    _, r, n = gathered.shape
    tm = _row_tile(r)

    def body(g_ref, o_ref):
        acc = g_ref[0]
        for k in range(1, N_DEV):
            acc = acc + g_ref[k]
        o_ref[...] = acc

    return pl.pallas_call(
        body, name=name, grid=(r // tm,),
        in_specs=[pl.BlockSpec((N_DEV, tm, n), lambda i: (0, i, 0))],
        out_specs=pl.BlockSpec((tm, n), lambda i: (i, 0)),
        out_shape=jax.ShapeDtypeStruct((r, n), f32),
        compiler_params=_cp("parallel"),
    )(gathered)


def _adamw(name, w, m, v, own, landed=None, slot=None):
    shape = w.shape
    w2, m2, v2 = _rows(w), _rows(m), _rows(v)
    r, n = w2.shape
    tm = _row_tile(r)
    c1 = 1.0 - ADAM_B1 ** ADAM_STEP
    c2 = 1.0 - ADAM_B2 ** ADAM_STEP
    extra = [] if landed is None else [landed.reshape(landed.shape[0], r, n)]
    row = pl.BlockSpec((tm, n), lambda i, *_: (i, 0))
    if slot is None:
        o2, own_spec, scalars = _rows(own), row, []
    else:
        dev, kind = slot
        scalars = [dev.reshape(1).astype(jnp.int32)]
        if kind == "lead":
            o2, own_spec = own.reshape(N_DEV, r, n), pl.BlockSpec((None, tm, n), lambda i, d: (d[0], i, 0))
        elif kind == "rows":
            o2, own_spec = own, pl.BlockSpec((tm, n), lambda i, d: (d[0] * (r // tm) + i, 0))
        else:
            o2, own_spec = own, pl.BlockSpec((tm, n), lambda i, d: (i, d[0]))

    def body(*refs):
        w_ref, m_ref, v_ref, o_ref = refs[len(scalars):len(scalars) + 4]
        refs = refs[len(scalars) + 4:]
        g = o_ref[...]
        if extra:
            for k in range(extra[0].shape[0]):
                g = g + refs[0][k].astype(f32)
        g_ref, d_ref, mn_ref, vn_ref = refs[len(extra):]
        mn = ADAM_B1 * m_ref[...] + (1.0 - ADAM_B1) * g
        vn = ADAM_B2 * v_ref[...] + (1.0 - ADAM_B2) * (g * g)
        g_ref[...] = g
        d_ref[...] = -ADAM_LR * ((mn / c1) / (jnp.sqrt(vn / c2) + ADAM_EPS) + ADAM_WD * w_ref[...])
        mn_ref[...] = mn
        vn_ref[...] = vn

    outs = pl.pallas_call(
        body, name=name,
        grid_spec=pltpu.PrefetchScalarGridSpec(
            num_scalar_prefetch=len(scalars), grid=(r // tm,),
            in_specs=[row] * 3 + [own_spec] + [pl.BlockSpec((e.shape[0], tm, n), lambda i, *_: (0, i, 0)) for e in extra],
            out_specs=[row] * 4),
        out_shape=[jax.ShapeDtypeStruct((r, n), f32)] * 4,
        compiler_params=_cp("parallel"),
    )(*scalars, w2, m2, v2, o2, *extra)
    return [o.reshape(shape) for o in outs]


def _pack(arrays):
    flat = jnp.concatenate([a.reshape(-1).astype(f32) for a in arrays])
    pad = (-flat.shape[0]) % (128 * (512 if flat.shape[0] > 128 * 512 else 8))
    return jnp.pad(flat, (0, pad)).reshape(-1, 128)


def _unpack(packed, shapes):
    flat = packed.reshape(-1)
    out, off = [], 0
    for s in shapes:
        n = math.prod(s)
        out.append(flat[off:off + n].reshape(s))
        off += n
    return out


def _local_step(x, target, w, weights_of, send, last_small, nb, s):
    cos, sin = _rope_tables(s)
    g = {}
    ffn_saved = {}
    ffn_bufs = [lax.empty((N_DEV, 2, 2) + shp, f32)
                for shp in ((D_MODEL, FF_SHARD), (D_MODEL, FF_SHARD), (FF_SHARD, D_MODEL))]

    def ffn(xin, l, h, wts):
        y, a, b = _ffn_fwd(f"ffn_fwd_{l}{h}", xin, w["ffn_g"][l][h], *wts)
        ffn_saved[(l, h)] = (xin, a, b, wts)
        return y

    def ffn_back(dy, l, h):
        xin, a, b, wts = ffn_saved[(l, h)]
        dx, dg, hb, dyh, u, da, db = _ffn_dx(f"ffn_dx_{l}{h}", dy, xin, w["ffn_g"][l][h], *wts, a, b)
        g[f"ffn_g_{l}{h}"] = dg
        if (l, h) == (0, 0):
            hb = last_small(g, hb)
        ffn_bufs[0], half = _ffn_dw_one(f"ffn_dw_{l}{h}_w1", hb, da, ffn_bufs[0], l, h)
        hb = send(f"ffn_{l}{h}_w1", {"ffn_w1": half}, hb)
        ffn_bufs[1], half = _ffn_dw_one(f"ffn_dw_{l}{h}_w3", hb, db, ffn_bufs[1], l, h)
        u = send(f"ffn_{l}{h}_w3", {"ffn_w3": half}, u)
        ffn_bufs[2], half = _ffn_dw_one(f"ffn_dw_{l}{h}_w2", u, dyh, ffn_bufs[2], l, h)
        return send(f"ffn_{l}{h}_w2", {"ffn_w2": half}, dx)

    def slots(t):
        return t.reshape(N_DEV, D_MODEL // N_DEV, D_MODEL)

    x1 = ffn(x, 0, 0, weights_of(0, [])["ffn"])
    wg = weights_of(1, [x1])
    w_in, w_out = wg["w_in"], wg["w_out"]
    _, h0b = _norm_fwd("mix_norm_0", x1, w["mix_g"][0])
    proj = _mm("in_proj", h0b, w_in, "nn", tn=1536)[0]
    o_raw, rprev, mret = _ret_fwd(proj, cos, sin, w["ret_g"], nb, s)
    lru = _lru_fwd(proj, w["conv_w"], w["conv_b"], w["lru_w_a"], w["lru_b_a"], w["lru_w_i"], w["lru_b_i"], w["lru_lam"], nb, s)
    merged = _ew("merge", lambda a, b: (jnp.concatenate([a, b], axis=1),), [mret, lru], [(D_MODEL, bf16)])[0]
    x2 = _mm("out_proj", merged, w_out, "nn", extras=[x1], epilogue=lambda acc, r: (acc + r,))[0]
    x3 = ffn(x2, 0, 1, weights_of(2, [x2])["ffn"])
    wg = weights_of(3, [x3])
    glu_a, glu_b = wg["glu_a"], wg["glu_b"]
    x4 = ffn(x3, 1, 0, wg["ffn"])
    u, _ = _norm_fwd("mix_norm_1", x4, w["mix_g"][1])
    lbr, lbi, bbr, bbi = _s5_prep(w["s5_lr"], w["s5_li"], w["s5_ldt"], w["s5_bre"], w["s5_bim"])
    lbr_f, lbi_f = lbr.reshape(1, -1), lbi.reshape(1, -1)
    wbr, wbi = _blockdiag(bbr).astype(bf16), _blockdiag(bbi).astype(bf16)
    wcr, wci = _blockdiag(w["s5_cre"]).astype(bf16), _blockdiag(w["s5_cim"]).astype(bf16)
    ygb, ypre, h0s = _s5_fwd(u, lbr_f, lbi_f, wbr, wbi, wcr, wci, w["s5_d"], nb, s)
    x5, gp, gq = _glu_fwd(ygb, glu_a, glu_b, x4)
    x6 = ffn(x5, 1, 1, weights_of(4, [x5])["ffn"])
    loss, dx6, g["final_g"] = _final_loss(x6, w["final_g"], target)

    dx5 = ffn_back(dx6, 1, 1)

    def glu_bwd(d, p, q):
        sg = jax.nn.sigmoid(q)
        return d * sg, d * p * sg * (1.0 - sg)

    dp, dq = _ew("glu_bwd", glu_bwd, [dx5, gp, gq], [(D_MODEL, bf16), (D_MODEL, bf16)])
    dyg = _mm("glu_dy_a", dp, glu_a, "nt")[0]
    dyg = _mm("glu_dy_b", dq, glu_b, "nt", extras=[dyg], epilogue=lambda acc, r: (acc + r,))[0]
    g["glu_a"], ga_half = _mm_tn("glu_dw_a", ygb, dp)
    g["glu_b"], gb_half = _mm_tn("glu_dw_b", ygb, dq)
    dyg = send("glu", {"glu_a": slots(ga_half), "glu_b": slots(gb_half)}, dyg)
    du, dlr, dli, dwbr, dwbi, dwcr, dwci, g["s5_d"] = _s5_bwd(dyg, ypre, u, h0s, lbr_f, lbi_f, wbr, wbi, wcr, wci, w["s5_d"], nb, s)
    g["s5_cre"], g["s5_cim"] = _blockdiag_t(dwcr), _blockdiag_t(dwci)
    g["s5_lr"], g["s5_li"], g["s5_ldt"], g["s5_bre"], g["s5_bim"] = _s5_prep_bwd(
        w["s5_lr"], w["s5_li"], w["s5_ldt"], w["s5_bre"], w["s5_bim"],
        (dlr.reshape(S5_GROUPS, S5_STATE), dli.reshape(S5_GROUPS, S5_STATE), _blockdiag_t(dwbr), _blockdiag_t(dwbi)))
    dx4, g["mix_g_1"] = _norm_bwd("mix_norm_1_bwd", du, x4, w["mix_g"][1], dx5)
    dx3 = ffn_back(dx4, 1, 0)
    dx2 = ffn_back(dx3, 0, 1)
    dmerged = _mm("out_proj_dx", dx2, w_out, "nt")[0]
    g["w_out"], wo_half = _mm_tn("out_proj_dw", merged, dx2)
    dmerged = send("w_out", {"w_out": slots(wo_half)}, dmerged)
    dq_, dk_, dv_, dgate, g["ret_g"] = _ret_bwd(dmerged, o_raw, rprev, proj, cos, sin, w["ret_g"], nb, s)
    (dxl, dgl, g["conv_w"], g["conv_b"], g["lru_w_a"], g["lru_b_a"], g["lru_w_i"], g["lru_b_i"], g["lru_lam"]) = _lru_bwd(
        dmerged, proj, w["conv_w"], w["conv_b"], w["lru_w_a"], w["lru_b_a"], w["lru_w_i"], w["lru_b_i"], w["lru_lam"], nb, s)
    dproj = _ew("dproj", lambda *p: (jnp.concatenate(p, axis=1),), [dq_, dk_, dv_, dgate, dxl, dgl], [(3072, bf16)])[0]
    dh0 = _mm("in_proj_dx", dproj, w_in, "nt")[0]
    g["w_in"], wi_half = _mm_tn("in_proj_dw", h0b, dproj)
    dh0 = send("w_in", {"w_in": jnp.transpose(wi_half.reshape(D_MODEL, N_DEV, IN_SHARD), (1, 0, 2))}, dh0)
    dx1, g["mix_g_0"] = _norm_bwd("mix_norm_0_bwd", dh0, x1, w["mix_g"][0], dx2)
    dx0 = ffn_back(dx1, 0, 0)
    g["ffn_w1"], g["ffn_w3"], g["ffn_w2"] = ffn_bufs
    return loss, dx0, g


_WEIGHTS = ["ffn_norm_g", "ffn_w1", "ffn_w3", "ffn_w2", "mix_norm_g", "w_in_even", "w_out_even", "ret_norm_g", "conv_w",
            "conv_b", "lru_w_a", "lru_b_a", "lru_w_i", "lru_b_i", "lru_lambda", "s5_lambda_re", "s5_lambda_im", "s5_log_dt",
            "s5_b_re", "s5_b_im", "s5_c_re", "s5_c_im", "s5_d", "glu_w_a", "glu_w_b", "final_norm_g"]
_BIG = ["ffn_w1", "ffn_w3", "ffn_w2", "w_in_even", "w_out_even", "glu_w_a", "glu_w_b"]
_SMALL_SHARDED = ["ffn_norm_g", "conv_w", "s5_d"]
_SMALL = [n for n in _WEIGHTS if n not in _BIG]


def kernel(x, ffn_norm_g, ffn_w1, ffn_w3, ffn_w2, mix_norm_g, w_in_even, w_out_even, ret_norm_g, conv_w, conv_b, lru_w_a, lru_b_a, lru_w_i, lru_b_i, lru_lambda, s5_lambda_re, s5_lambda_im, s5_log_dt, s5_b_re, s5_b_im, s5_c_re, s5_c_im, s5_d, glu_w_a, glu_w_b, final_norm_g, loss_target, m_ffn_norm_g, m_ffn_w1, m_ffn_w3, m_ffn_w2, m_mix_norm_g, m_w_in_even, m_w_out_even, m_ret_norm_g, m_conv_w, m_conv_b, m_lru_w_a, m_lru_b_a, m_lru_w_i, m_lru_b_i, m_lru_lambda, m_s5_lambda_re, m_s5_lambda_im, m_s5_log_dt, m_s5_b_re, m_s5_b_im, m_s5_c_re, m_s5_c_im, m_s5_d, m_glu_w_a, m_glu_w_b, m_final_norm_g, v_ffn_norm_g, v_ffn_w1, v_ffn_w3, v_ffn_w2, v_mix_norm_g, v_w_in_even, v_w_out_even, v_ret_norm_g, v_conv_w, v_conv_b, v_lru_w_a, v_lru_b_a, v_lru_w_i, v_lru_b_i, v_lru_lambda, v_s5_lambda_re, v_s5_lambda_im, v_s5_log_dt, v_s5_b_re, v_s5_b_im, v_s5_c_re, v_s5_c_im, v_s5_d, v_glu_w_a, v_glu_w_b, v_final_norm_g):
    a = dict(locals())
    nb, s, d = x.shape
    ax, ay, ac = lax.axis_index("x"), lax.axis_index("y"), lax.axis_index("c")
    dev = 4 * ax + 2 * ay + ac
    chip = 2 * ax + ay

    def ffn_shards(l, h):
        extra = FF_PAD - FF_SHARD
        return [jnp.pad(ffn_w1[l, h].astype(bf16), ((0, 0), (0, extra))), jnp.pad(ffn_w3[l, h].astype(bf16), ((0, 0), (0, extra))),
                jnp.pad(ffn_w2[l, h].astype(bf16), ((0, extra), (0, 0)))]

    first = _all_gather("ag_first", ffn_shards(0, 0) + [_pack([ffn_norm_g, conv_w, s5_d])])
    sm = first[3].reshape(N_DEV, -1)
    ffn_g_full = jnp.transpose(sm[:, :512].reshape(N_DEV, 2, 2, 128), (1, 2, 0, 3)).reshape(2, 2, D_MODEL)
    conv_w_full = jnp.transpose(sm[:, 512:768].reshape(N_DEV, 4, 64), (1, 0, 2)).reshape(4, LRU_WIDTH)
    s5_d_full = sm[:, 768:896].reshape(1, D_MODEL)

    ag_src = [None, [w_in_even[0].astype(bf16), w_out_even[0].astype(bf16)], ffn_shards(0, 1),
              ffn_shards(1, 0) + [glu_w_a[0].astype(bf16), glu_w_b[0].astype(bf16)], ffn_shards(1, 1)]
    ag, token = [None], first[0]
    for k, grp in enumerate(ag_src):
        if grp is None:
            continue
        grp[0] = _tie(f"tie_ag_{k}", grp[0], [token])
        lands = [lax.dynamic_update_index_in_dim(lax.empty((N_DEV,) + t.shape, bf16), t, dev, 0) for t in grp]
        ag.append(_xchg_start(f"ag_start_{k}", "gather", grp, lands))
        token = ag[-1]["token"]

    def weights_of(k, after):
        if k == 0:
            return {"ffn": [first[0], _tie("tie_ag_started", first[1], [h["token"] for h in ag[1:]]), first[2]]}
        got = _xchg_wait(f"ag_wait_{k}", ag[k], after)
        if k == 1:
            return {"w_in": jnp.transpose(got[0], (1, 0, 2)).reshape(D_MODEL, N_DEV * IN_SHARD),
                    "w_out": got[1].reshape(D_MODEL, D_MODEL)}
        if k == 3:
            return {"ffn": got[:3], "glu_a": got[3].reshape(D_MODEL, D_MODEL), "glu_b": got[4].reshape(D_MODEL, D_MODEL)}
        return {"ffn": got}

    ffn_lands = [lax.empty((N_DEV - 1, 2, 2) + shp, bf16)
                 for shp in ((D_MODEL, FF_SHARD), (D_MODEL, FF_SHARD), (FF_SHARD, D_MODEL))]
    rs = []

    ffn_names = ("ffn_w1", "ffn_w3", "ffn_w2")

    def send(group, arrays, carry):
        srcs = list(arrays.values())
        if group.startswith("ffn_"):
            which = [ffn_names.index(n) for n in arrays]
            sfx = [(int(group[4]), int(group[5]))] * len(which)
            h = _xchg_start("rs_start_" + group, "scatter", srcs, [ffn_lands[k] for k in which], sfx)
            for k, land in zip(which, h["lands"]):
                ffn_lands[k] = land
        else:
            h = _xchg_start("rs_start_" + group, "scatter", srcs,
                            [lax.empty((N_DEV - 1,) + t.shape[1:], bf16) for t in srcs])
        rs.append((group, list(arrays), h))
        return _tie("tie_" + group, carry, [h["token"]])

    w = {
        "ffn_g": [[ffn_g_full[l, h].reshape(1, D_MODEL) for h in range(2)] for l in range(2)],
        "mix_g": [mix_norm_g[0:1], mix_norm_g[1:2]],
        "ret_g": ret_norm_g, "conv_w": conv_w_full, "conv_b": conv_b,
        "lru_w_a": lru_w_a[0], "lru_b_a": lru_b_a, "lru_w_i": lru_w_i[0], "lru_b_i": lru_b_i, "lru_lam": lru_lambda,
        "s5_lr": s5_lambda_re[0], "s5_li": s5_lambda_im[0], "s5_ldt": s5_log_dt.reshape(S5_GROUPS, 1),
        "s5_bre": jnp.swapaxes(s5_b_re[0], 1, 2), "s5_bim": jnp.swapaxes(s5_b_im[0], 1, 2),
        "s5_cre": s5_c_re[0], "s5_cim": s5_c_im[0], "s5_d": s5_d_full,
        "final_g": final_norm_g.reshape(1, D_MODEL),
    }

    small_grads = {}

    def last_small(g, carry):
        part = _small_partials(g)
        mine = _pack([part[n] for n in _SMALL])
        land = lax.dynamic_update_index_in_dim(lax.empty((N_DEV,) + mine.shape, f32), mine, dev, 0)
        h = _xchg_start("ag_start_small_grads", "gather", [mine], [land])
        small_grads.update(h=h, shapes=[part[n].shape for n in _SMALL])
        return _tie("tie_small_grads", carry, [h["token"]])

    loss_part, dx, g = _local_step(x.reshape(nb * s, d), loss_target.reshape(nb * s, d), w, weights_of, send, last_small,
                                   nb, s)
    loss = lax.psum(loss_part[0, 0], ("x", "y", "c"))
    (gath,) = _xchg_wait("ag_wait_small_grads", small_grads["h"], [dx])
    full = dict(zip(_SMALL, _unpack(_sum8("sum_small_grads", gath), small_grads["shapes"])))
    for n in _SMALL_SHARDED:
        width = a[n].shape[-1]
        full[n] = lax.dynamic_slice_in_dim(full[n], dev * width, width, axis=full[n].ndim - 1)
    shapes = [a[n].shape for n in _SMALL]
    packed = _adamw("adamw_small", _pack([a[n] for n in _SMALL]), _pack([a["m_" + n] for n in _SMALL]),
                    _pack([a["v_" + n] for n in _SMALL]), _pack([full[n] for n in _SMALL]))
    res = {n: vals for n, vals in zip(_SMALL, zip(*[_unpack(p, shapes) for p in packed]))}
    return _finish(a, g, dx, loss, res, packed, rs, ffn_lands, dev, nb, s, d)


def _small_partials(g):
    return {
        "ffn_norm_g": jnp.stack([jnp.stack([g[f"ffn_g_{l}{h}"][0] for h in range(2)]) for l in range(2)]),
        "mix_norm_g": jnp.concatenate([g["mix_g_0"], g["mix_g_1"]], axis=0),
        "ret_norm_g": g["ret_g"], "conv_w": g["conv_w"][None], "conv_b": g["conv_b"],
        "lru_w_a": g["lru_w_a"][None], "lru_b_a": g["lru_b_a"], "lru_w_i": g["lru_w_i"][None], "lru_b_i": g["lru_b_i"],
        "lru_lambda": g["lru_lam"], "s5_lambda_re": g["s5_lr"][None], "s5_lambda_im": g["s5_li"][None],
        "s5_log_dt": g["s5_ldt"].reshape(1, S5_GROUPS),
        "s5_b_re": jnp.swapaxes(g["s5_bre"], 1, 2)[None], "s5_b_im": jnp.swapaxes(g["s5_bim"], 1, 2)[None],
        "s5_c_re": g["s5_cre"][None], "s5_c_im": g["s5_cim"][None], "s5_d": g["s5_d"], "final_norm_g": g["final_g"][0],
    }


def _finish(a, g, dx, loss, res, packed, rs, ffn_lands, dev, nb, s, d):
    landed = {}
    for group, names, h in rs:
        if not group.startswith("ffn_"):
            landed.update(zip(names, _xchg_wait("rs_wait_" + group, h, [dx])))
    kinds = {"ffn_w1": "lead", "ffn_w3": "lead", "ffn_w2": "lead", "w_in": "cols", "w_out": "rows", "glu_a": "rows",
             "glu_b": "rows"}

    def update(n, short):
        res[n] = _adamw("adamw_" + n, a[n], a["m_" + n], a["v_" + n], g[short],
                        landed[short].reshape((N_DEV - 1,) + a[n].shape), slot=(dev, kinds[short]))

    for n, short in zip(_BIG[3:], ("w_in", "w_out", "glu_a", "glu_b")):
        update(n, short)
    after = [dx, packed[0]] + [res[n][0] for n in _BIG[3:]]
    for k, n in enumerate(("ffn_w1", "ffn_w3", "ffn_w2")):
        for group, names, h in rs:
            if group.startswith("ffn_") and names == [n]:
                (ffn_lands[k],) = _xchg_wait("rs_wait_" + group, h, after, [ffn_lands[k]])
        landed[n] = ffn_lands[k]
        update(n, n)
        after = after + [res[n][0]]

    out = [loss, dx.reshape(nb, s, d)]
    for k in range(4):
        out += [res[n][k] for n in _WEIGHTS]
    return tuple(out)
```

```python
import functools
import math

import numpy as np
import jax
import jax.numpy as jnp
from jax import lax
from jax.experimental import pallas as pl
from jax.experimental.pallas import tpu as pltpu

f32 = jnp.float32
bf16 = jnp.bfloat16

D_MODEL = 1024
N_DEV = 8
EPS = 1e-6
RET_HEADS = 4
HEAD_DIM = 128
RET_WIDTH = 512
RET_CHUNK = 128
ROPE_BASE = 10000.0
LRU_WIDTH = 512
LRU_BLOCKS = 4
LRU_C = 8.0
S5_GROUP = 16
S5_GROUPS = 64
S5_STATE = 64
S5_CHUNK = 256
S5_BLOCKS = 8
S5_BLOCK_STATES = 512
SUBLANES = 8
D_FF = 2816
FF_SHARD = D_FF // N_DEV
FF_PAD = 384
IN_SHARD = 3072 // N_DEV
ADAM_LR = 0.001
ADAM_B1 = 0.9
ADAM_B2 = 0.999
ADAM_EPS = 1e-08
ADAM_WD = 0.01
ADAM_STEP = 10

VMEM_LIMIT = 56 * 1024 * 1024
VMEM_SPEC = pl.BlockSpec(memory_space=pltpu.VMEM)
ANY_SPEC = pl.BlockSpec(memory_space=pl.ANY)
HBM_SPEC = pl.BlockSpec(memory_space=pltpu.HBM)
SEM_SPEC = pl.BlockSpec(memory_space=pltpu.SEMAPHORE)
SIDE_EFFECT = pltpu.SideEffectType.DATAFLOW_SIDE_EFFECTING
MESH = pl.DeviceIdType.MESH


def _cp(*sem):
    return pltpu.CompilerParams(dimension_semantics=sem, vmem_limit_bytes=VMEM_LIMIT)


def _nn(a, b):
    return jnp.dot(a, b, preferred_element_type=f32)


def _nt(a, b):
    return lax.dot_general(a, b, (((1,), (1,)), ((), ())), preferred_element_type=f32)


def _tn(a, b):
    return lax.dot_general(a, b, (((0,), (0,)), ((), ())), preferred_element_type=f32)


def _rms_fwd(x, g):
    r = lax.rsqrt(jnp.mean(x * x, axis=-1, keepdims=True) + EPS)
    xn = x * r
    return xn * g, xn, r


def _rms_bwd(dh, xn, r, g):
    dxn = dh * g
    dx = r * (dxn - xn * jnp.mean(dxn * xn, axis=-1, keepdims=True))
    dg = jnp.sum(dh * xn, axis=0, keepdims=True)
    return dx, dg


def _shift_dn(v, d, row, fill=0.0):
    return jnp.where(row >= d, pltpu.roll(v, d, 0), fill)


def _shift_up(v, d, row, fill=0.0):
    n = v.shape[0]
    return jnp.where(row < n - d, pltpu.roll(v, n - d, 0), fill)


def _ew(name, fn, ins, outs, tm=512):
    t = ins[0].shape[0]
    n_in = len(ins)

    def body(*refs):
        res = fn(*[r[...] for r in refs[:n_in]])
        for o, v in zip(refs[n_in:], res):
            o[...] = v.astype(o.dtype)

    return pl.pallas_call(
        body, name=name, grid=(t // tm,),
        in_specs=[pl.BlockSpec((tm, a.shape[1]), lambda i: (i, 0)) for a in ins],
        out_specs=[pl.BlockSpec((tm, n), lambda i: (i, 0)) for n, _ in outs],
        out_shape=[jax.ShapeDtypeStruct((t, n), dt) for n, dt in outs],
        compiler_params=_cp("parallel"),
    )(*ins)


def _mm(name, x, w, kind, extras=(), epilogue=None, outs=None, tm=512, tn=1024):
    t = x.shape[0]
    n = w.shape[1] if kind == "nn" else w.shape[0]
    tn = min(tn, n)
    outs = outs or [f32]
    n_ex = len(extras)

    def body(x_ref, w_ref, *refs):
        xb = x_ref[...].astype(bf16)
        acc = _nn(xb, w_ref[...]) if kind == "nn" else _nt(xb, w_ref[...])
        res = epilogue(acc, *[r[...] for r in refs[:n_ex]]) if epilogue else (acc,)
        for o, v in zip(refs[n_ex:], res):
            o[...] = v.astype(o.dtype)

    w_spec = (pl.BlockSpec((w.shape[0], tn), lambda i, j: (0, j)) if kind == "nn"
              else pl.BlockSpec((tn, w.shape[1]), lambda i, j: (j, 0)))
    tile = pl.BlockSpec((tm, tn), lambda i, j: (i, j))
    return pl.pallas_call(
        body, name=name, grid=(t // tm, n // tn),
        in_specs=[pl.BlockSpec((tm, x.shape[1]), lambda i, j: (i, 0)), w_spec] + [tile] * n_ex,
        out_specs=[tile] * len(outs),
        out_shape=[jax.ShapeDtypeStruct((t, n), dt) for dt in outs],
        compiler_params=_cp("parallel", "parallel"),
    )(x, w, *extras)


def _mm_tn(name, x, y, tk=1024, tn=1024, tt=1024):
    t, k = x.shape
    n = y.shape[1]
    tk, tn, tt = min(tk, k), min(tn, n), min(tt, t)

    def body(x_ref, y_ref, o_ref, ob_ref):
        @pl.when(pl.program_id(2) == 0)
        def _():
            o_ref[...] = jnp.zeros_like(o_ref)
        o_ref[...] += _tn(x_ref[...].astype(bf16), y_ref[...].astype(bf16))

        @pl.when(pl.program_id(2) == pl.num_programs(2) - 1)
        def _():
            ob_ref[...] = o_ref[...].astype(bf16)

    out = pl.BlockSpec((tk, tn), lambda i, j, s: (i, j))
    return pl.pallas_call(
        body, name=name, grid=(k // tk, n // tn, t // tt),
        in_specs=[pl.BlockSpec((tt, tk), lambda i, j, s: (s, i)), pl.BlockSpec((tt, tn), lambda i, j, s: (s, j))],
        out_specs=[out, out],
        out_shape=[jax.ShapeDtypeStruct((k, n), f32), jax.ShapeDtypeStruct((k, n), bf16)],
        compiler_params=_cp("parallel", "parallel", "arbitrary"),
    )(x, y)


def _norm_fwd(name, x, g, tm=512):
    t, d = x.shape

    def body(x_ref, g_ref, h_ref, hb_ref):
        h, _, _ = _rms_fwd(x_ref[...], g_ref[...])
        h_ref[...] = h
        hb_ref[...] = h.astype(bf16)

    row = pl.BlockSpec((tm, d), lambda i: (i, 0))
    return pl.pallas_call(
        body, name=name, grid=(t // tm,),
        in_specs=[row, pl.BlockSpec((1, d), lambda i: (0, 0))],
        out_specs=[row, row],
        out_shape=[jax.ShapeDtypeStruct((t, d), f32), jax.ShapeDtypeStruct((t, d), bf16)],
        compiler_params=_cp("parallel"),
    )(x, g)


def _norm_bwd(name, dh, x, g, dres, tm=512):
    t, d = x.shape

    def body(dh_ref, x_ref, g_ref, dres_ref, dx_ref, dg_ref):
        gv = g_ref[...]
        _, xn, r = _rms_fwd(x_ref[...], gv)
        dx, dg = _rms_bwd(dh_ref[...], xn, r, gv)
        dx_ref[...] = dres_ref[...] + dx

        @pl.when(pl.program_id(0) == 0)
        def _():
            dg_ref[...] = jnp.zeros_like(dg_ref)
        dg_ref[...] += dg

    row = pl.BlockSpec((tm, d), lambda i: (i, 0))
    vec = pl.BlockSpec((1, d), lambda i: (0, 0))
    return pl.pallas_call(
        body, name=name, grid=(t // tm,),
        in_specs=[row, row, vec, row],
        out_specs=[row, vec],
        out_shape=[jax.ShapeDtypeStruct((t, d), f32), jax.ShapeDtypeStruct((1, d), f32)],
        compiler_params=_cp("arbitrary"),
    )(dh, x, g, dres)


def _final_loss(x, g, target, tm=512):
    t, d = x.shape

    def body(x_ref, g_ref, t_ref, loss_ref, dx_ref, dg_ref):
        gv = g_ref[...]
        y, xn, r = _rms_fwd(x_ref[...], gv)
        err = y - t_ref[...]
        dy = err * (1.0 / d)
        dx, dg = _rms_bwd(dy, xn, r, gv)
        dx_ref[...] = dx

        @pl.when(pl.program_id(0) == 0)
        def _():
            dg_ref[...] = jnp.zeros_like(dg_ref)
            loss_ref[...] = jnp.zeros_like(loss_ref)
        dg_ref[...] += dg
        loss_ref[...] += jnp.full((1, 128), 0.5 / d, f32) * jnp.sum(err * err)

    row = pl.BlockSpec((tm, d), lambda i: (i, 0))
    vec = pl.BlockSpec((1, d), lambda i: (0, 0))
    return pl.pallas_call(
        body, name="final_loss", grid=(t // tm,),
        in_specs=[row, vec, row],
        out_specs=[pl.BlockSpec((1, 128), lambda i: (0, 0)), row, vec],
        out_shape=[jax.ShapeDtypeStruct((1, 128), f32), jax.ShapeDtypeStruct((t, d), f32),
                   jax.ShapeDtypeStruct((1, d), f32)],
        compiler_params=_cp("arbitrary"),
    )(x, g, target)


def _load_ffn_weights(hbm_refs, vmem_refs, sems):
    @pl.when(pl.program_id(0) == 0)
    def _():
        copies = []
        for k, (src, dst) in enumerate(zip(hbm_refs, vmem_refs)):
            for j in range(N_DEV):
                half = pl.ds((j % 2) * FF_PAD, FF_PAD)
                window = dst.at[j // 2, half, :] if k == 2 else dst.at[j // 2, :, half]
                copies.append(pltpu.make_async_copy(src.at[j], window, sems.at[k * N_DEV + j]))
        for cp in copies:
            cp.start()
        for cp in copies:
            cp.wait()


def _ffn_weight_scratch(nj, d, ff):
    return [pltpu.VMEM((nj, d, ff), bf16), pltpu.VMEM((nj, d, ff), bf16), pltpu.VMEM((nj, ff, d), bf16),
            pltpu.SemaphoreType.DMA((3 * N_DEV,))]


def _ffn_fwd(name, x, g, w1, w3, w2, tm=512):
    t, d = x.shape
    nj, ff = N_DEV // 2, 2 * FF_PAD

    def body(x_ref, g_ref, w1_hbm, w3_hbm, w2_hbm, y_ref, a_ref, b_ref, w1_ref, w3_ref, w2_ref, sems):
        _load_ffn_weights((w1_hbm, w3_hbm, w2_hbm), (w1_ref, w3_ref, w2_ref), sems)
        xv = x_ref[...]
        h, _, _ = _rms_fwd(xv, g_ref[...])
        hb = h.astype(bf16)
        acc = jnp.zeros((tm, d), f32)
        for j in range(nj):
            a = _nn(hb, w1_ref[j])
            b = _nn(hb, w3_ref[j])
            a_ref[j] = a.astype(bf16)
            b_ref[j] = b.astype(bf16)
            u = (a * jax.nn.sigmoid(a) * b).astype(bf16)
            acc = acc + _nn(u, w2_ref[j])
        y_ref[...] = xv + 0.5 * acc

    row = pl.BlockSpec((tm, d), lambda i: (i, 0))
    mid = pl.BlockSpec((nj, tm, ff), lambda i: (0, i, 0))
    return pl.pallas_call(
        body, name=name, grid=(t // tm,),
        in_specs=[row, pl.BlockSpec((1, d), lambda i: (0, 0)), ANY_SPEC, ANY_SPEC, ANY_SPEC],
        out_specs=[row, mid, mid],
        out_shape=[jax.ShapeDtypeStruct((t, d), f32), jax.ShapeDtypeStruct((nj, t, ff), bf16),
                   jax.ShapeDtypeStruct((nj, t, ff), bf16)],
        scratch_shapes=_ffn_weight_scratch(nj, d, ff),
        compiler_params=_cp("arbitrary"),
    )(x, g, w1, w3, w2)


def _ffn_dx(name, dy, x, g, w1, w3, w2, a, b, tm=256):
    t, d = x.shape
    nj, ff = N_DEV // 2, 2 * FF_PAD

    def body(dy_ref, x_ref, g_ref, w1_hbm, w3_hbm, w2_hbm, a_ref, b_ref,
             dx_ref, dg_ref, hbt_ref, dyh_ref, ut_ref, da_ref, db_ref, w1_ref, w3_ref, w2_ref, sems):
        _load_ffn_weights((w1_hbm, w3_hbm, w2_hbm), (w1_ref, w3_ref, w2_ref), sems)
        gv = g_ref[...]
        h, xn, r = _rms_fwd(x_ref[...], gv)
        hbt_ref[...] = h.astype(bf16).T
        dyv = dy_ref[...]
        dyh = (0.5 * dyv).astype(bf16)
        dyh_ref[...] = dyh
        dh = jnp.zeros((tm, d), f32)
        for j in range(nj):
            av = a_ref[j].astype(f32)
            bv = b_ref[j].astype(f32)
            s = jax.nn.sigmoid(av)
            silu = av * s
            ut_ref[j] = (silu * bv).astype(bf16).T
            du = _nt(dyh, w2_ref[j])
            dab = (du * bv * (s * (1.0 + av * (1.0 - s)))).astype(bf16)
            dbb = (du * silu).astype(bf16)
            da_ref[j] = dab
            db_ref[j] = dbb
            dh = dh + _nt(dab, w1_ref[j]) + _nt(dbb, w3_ref[j])
        dx, dg = _rms_bwd(dh, xn, r, gv)
        dx_ref[...] = dyv + dx

        @pl.when(pl.program_id(0) == 0)
        def _():
            dg_ref[...] = jnp.zeros_like(dg_ref)
        dg_ref[...] += dg

    row = pl.BlockSpec((tm, d), lambda i: (i, 0))
    vec = pl.BlockSpec((1, d), lambda i: (0, 0))
    mid = pl.BlockSpec((nj, tm, ff), lambda i: (0, i, 0))
    mid_shape = jax.ShapeDtypeStruct((nj, t, ff), bf16)
    return pl.pallas_call(
        body, name=name, grid=(t // tm,),
        in_specs=[row, row, vec, ANY_SPEC, ANY_SPEC, ANY_SPEC, mid, mid],
        out_specs=[row, vec, pl.BlockSpec((d, tm), lambda i: (0, i)), row,
                   pl.BlockSpec((nj, ff, tm), lambda i: (0, 0, i)), mid, mid],
        out_shape=[jax.ShapeDtypeStruct((t, d), f32), jax.ShapeDtypeStruct((1, d), f32),
                   jax.ShapeDtypeStruct((d, t), bf16), jax.ShapeDtypeStruct((t, d), bf16),
                   jax.ShapeDtypeStruct((nj, ff, t), bf16), mid_shape, mid_shape],
        scratch_shapes=_ffn_weight_scratch(nj, d, ff),
        compiler_params=_cp("arbitrary"),
    )(dy, x, g, w1, w3, w2, a, b)


def _ffn_dw_one(name, xt, y, buf, l, h, tt=2048):
    t = y.shape[-2]
    tt = min(tt, t)
    cut_cols = xt.ndim == 2

    def body(x_ref, y_ref, buf_ref, o_ref, ob_ref, acc):
        s = pl.program_id(1)
        prod = _nn(x_ref[0] if xt.ndim == 3 else x_ref[...], y_ref[0] if y.ndim == 3 else y_ref[...])

        @pl.when(s == 0)
        def _():
            acc[...] = prod

        @pl.when(s > 0)
        def _():
            acc[...] += prod

        @pl.when(s == pl.num_programs(1) - 1)
        def _():
            total = acc[...]
            for e in range(2):
                lo = e * FF_PAD
                part = total[:, lo:lo + FF_SHARD] if cut_cols else total[lo:lo + FF_SHARD, :]
                o_ref[e] = part
                ob_ref[e] = part.astype(bf16)

    x_spec = (pl.BlockSpec((1, xt.shape[1], tt), lambda p, s: (p, 0, s)) if xt.ndim == 3
              else pl.BlockSpec((xt.shape[0], tt), lambda p, s: (0, s)))
    y_spec = (pl.BlockSpec((1, tt, y.shape[2]), lambda p, s: (p, s, 0)) if y.ndim == 3
              else pl.BlockSpec((tt, y.shape[1]), lambda p, s: (s, 0)))
    k_, n_ = buf.shape[-2:]
    return pl.pallas_call(
        body, name=name, grid=(N_DEV // 2, t // tt),
        in_specs=[x_spec, y_spec, ANY_SPEC],
        out_specs=[pl.BlockSpec((2, None, None, k_, n_), lambda p, s: (p, l, h, 0, 0)),
                   pl.BlockSpec((2, k_, n_), lambda p, s: (p, 0, 0))],
        out_shape=[jax.ShapeDtypeStruct(buf.shape, buf.dtype), jax.ShapeDtypeStruct((N_DEV, k_, n_), bf16)],
        input_output_aliases={2: 0},
        scratch_shapes=[pltpu.VMEM((xt.shape[-2], y.shape[-1]), f32)],
        compiler_params=_cp("parallel", "arbitrary"),
    )(xt, y, buf)


_LOG_GAMMA = [float(np.log1p(-np.float32(2.0) ** np.float32(-5.0 - h))) for h in range(RET_HEADS)]


def _ret_consts(h):
    lg = jnp.where(h == 0, _LOG_GAMMA[0], jnp.where(h == 1, _LOG_GAMMA[1],
                   jnp.where(h == 2, _LOG_GAMMA[2], _LOG_GAMMA[3]))).astype(f32)
    c = RET_CHUNK
    r = lax.broadcasted_iota(jnp.int32, (c, c), 0)
    cc = lax.broadcasted_iota(jnp.int32, (c, c), 1)
    decay = jnp.where(r >= cc, jnp.exp(lg * jnp.maximum((r - cc).astype(f32), 0.0)), 0.0)
    pos = lax.broadcasted_iota(jnp.int32, (c, 1), 0).astype(f32)
    kd = jnp.exp(lg * (c - 1.0 - pos))
    qd = jnp.exp(lg * (pos + 1.0))
    gc = jnp.exp(lg * c)
    return decay, kd, qd, gc


def _rope(x, cos, sin):
    return x * cos + pltpu.roll(x, HEAD_DIM // 2, 1) * sin


def _rope_t(g, cos, sin):
    return g * cos + pltpu.roll(g * sin, HEAD_DIM // 2, 1)


def _rope_tables(s):
    half = HEAD_DIM // 2
    inv = ROPE_BASE ** (-jnp.arange(half, dtype=f32) / half)
    ang = jnp.arange(s, dtype=f32)[:, None] * inv[None, :]
    cos, sin = jnp.cos(ang), jnp.sin(ang)
    return jnp.concatenate([cos, cos], axis=1), jnp.concatenate([-sin, sin], axis=1)


def _head_ln(o):
    mu = jnp.mean(o, axis=-1, keepdims=True)
    oc = o - mu
    rs = lax.rsqrt(jnp.mean(oc * oc, axis=-1, keepdims=True) + EPS)
    return oc * rs, rs


def _ret_fwd(proj, cos, sin, ret_g, nb, s):
    c = RET_CHUNK
    nc = s // c
    t = nb * s
    scale = HEAD_DIM ** -0.5

    def body(q_ref, k_ref, v_ref, gate_ref, cos_ref, sin_ref, g_ref, o_ref, rprev_ref, m_ref):
        decay, kd, qd, gc = _ret_consts(pl.program_id(1))
        gv = g_ref[...]

        def step(n, rv):
            rows = pl.ds(pl.multiple_of(n * c, c), c)
            cs, sn = cos_ref[rows, :], sin_ref[rows, :]
            q = _rope(q_ref[rows, :], cs, sn)
            k = _rope(k_ref[rows, :], cs, sn) * scale
            vb = v_ref[rows, :].astype(bf16)
            sc = _nt(q.astype(bf16), k.astype(bf16)) * decay
            rprev_ref[n] = rv
            o = _nn(sc.astype(bf16), vb) + _nn((q * qd).astype(bf16), rv.astype(bf16))
            o_ref[rows, :] = o
            y, _ = _head_ln(o)
            gate = gate_ref[rows, :]
            m_ref[rows, :] = y * gv * (gate * jax.nn.sigmoid(gate))
            return rv * gc + _tn((k * kd).astype(bf16), vb)

        lax.fori_loop(0, nc, step, jnp.zeros((HEAD_DIM, HEAD_DIM), f32))

    def col(off):
        return pl.BlockSpec((s, HEAD_DIM), lambda b, h: (b, off + h))

    tab = pl.BlockSpec((s, HEAD_DIM), lambda b, h: (0, 0))
    return pl.pallas_call(
        body, name="ret_fwd", grid=(nb, RET_HEADS),
        in_specs=[col(0), col(4), col(8), col(12), tab, tab, pl.BlockSpec((1, HEAD_DIM), lambda b, h: (0, h))],
        out_specs=[col(0), pl.BlockSpec((nc, HEAD_DIM, HEAD_DIM), lambda b, h: (b * RET_HEADS + h, 0, 0)), col(0)],
        out_shape=[jax.ShapeDtypeStruct((t, RET_WIDTH), f32),
                   jax.ShapeDtypeStruct((nb * RET_HEADS * nc, HEAD_DIM, HEAD_DIM), f32),
                   jax.ShapeDtypeStruct((t, RET_WIDTH), f32)],
        compiler_params=_cp("parallel", "parallel"),
    )(proj, proj, proj, proj, cos, sin, ret_g)


def _ret_bwd(dmerged, o_raw, rprev, proj, cos, sin, ret_g, nb, s):
    c = RET_CHUNK
    nc = s // c
    t = nb * s
    scale = HEAD_DIM ** -0.5

    def body(dm_ref, o_ref, rprev_ref, q_ref, k_ref, v_ref, gate_ref, cos_ref, sin_ref, g_ref,
             dq_ref, dk_ref, dv_ref, dgate_ref, dg_ref):
        @pl.when(pl.program_id(1) == 0)
        def _():
            dg_ref[...] = jnp.zeros_like(dg_ref)
        decay, kd, qd, gc = _ret_consts(pl.program_id(0))
        gv = g_ref[...]

        def step(i, carry):
            drn, dg = carry
            n = nc - 1 - i
            rows = pl.ds(pl.multiple_of(n * c, c), c)
            cs, sn = cos_ref[rows, :], sin_ref[rows, :]
            q = _rope(q_ref[rows, :], cs, sn)
            k = _rope(k_ref[rows, :], cs, sn) * scale
            qb, kb = q.astype(bf16), k.astype(bf16)
            vb = v_ref[rows, :].astype(bf16)
            sc = _nt(qb, kb) * decay
            y, rs = _head_ln(o_ref[rows, :])
            gate = gate_ref[rows, :]
            sg = jax.nn.sigmoid(gate)
            silu = gate * sg
            dm = dm_ref[rows, :]
            dgate_ref[rows, :] = dm * y * gv * (sg * (1.0 + gate * (1.0 - sg)))
            dyl = dm * gv * silu
            dg = dg + jnp.sum(dm * y * silu, axis=0, keepdims=True)
            do = rs * (dyl - jnp.mean(dyl, axis=-1, keepdims=True) - y * jnp.mean(dyl * y, axis=-1, keepdims=True))
            dob = do.astype(bf16)
            rv = rprev_ref[n]
            drb = drn.astype(bf16)
            ds = (_nt(dob, vb) * decay).astype(bf16)
            kdb = (k * kd).astype(bf16)
            qdb = (q * qd).astype(bf16)
            dq_r = _nn(ds, kb) + _nt(dob, rv.astype(bf16)) * qd
            dk_r = _tn(ds, qb) + _nt(vb, drb) * kd
            dv_ref[rows, :] = _tn(sc.astype(bf16), dob) + _nn(kdb, drb)
            dq_ref[rows, :] = _rope_t(dq_r, cs, sn)
            dk_ref[rows, :] = _rope_t(dk_r * scale, cs, sn)
            return drn * gc + _tn(qdb, dob), dg

        _, dg = lax.fori_loop(0, nc, step, (jnp.zeros((HEAD_DIM, HEAD_DIM), f32), jnp.zeros((1, HEAD_DIM), f32)))
        dg_ref[...] += dg

    def col(off):
        return pl.BlockSpec((s, HEAD_DIM), lambda h, b: (b, off + h))

    tab = pl.BlockSpec((s, HEAD_DIM), lambda h, b: (0, 0))
    gsp = pl.BlockSpec((1, HEAD_DIM), lambda h, b: (0, h))
    out_t = jax.ShapeDtypeStruct((t, RET_WIDTH), f32)
    return pl.pallas_call(
        body, name="ret_bwd", grid=(RET_HEADS, nb),
        in_specs=[col(0), col(0), pl.BlockSpec((nc, HEAD_DIM, HEAD_DIM), lambda h, b: (b * RET_HEADS + h, 0, 0)),
                  col(0), col(4), col(8), col(12), tab, tab, gsp],
        out_specs=[col(0), col(0), col(0), col(0), gsp],
        out_shape=[out_t, out_t, out_t, out_t, jax.ShapeDtypeStruct((1, RET_WIDTH), f32)],
        compiler_params=_cp("parallel", "arbitrary"),
    )(dmerged, o_raw, rprev, proj, proj, proj, proj, cos, sin, ret_g)


def _neg_expm1(z):
    series = -(z * (1.0 + z * (0.5 + z * (1.0 / 6.0 + z * (1.0 / 24.0)))))
    return jnp.where(z > -0.01, series, 1.0 - jnp.exp(z))


def _lru_gates(xc, pa, pi, lam):
    r = jax.nn.sigmoid(pa)
    i = jax.nn.sigmoid(pi)
    log_a = -LRU_C * r * jax.nn.softplus(-lam)
    a = jnp.exp(log_a)
    bx = jnp.sqrt(_neg_expm1(2.0 * log_a)) * i * xc
    return a, bx


def _scan_rows(a, b, row, up):
    sub = row[:SUBLANES] & (SUBLANES - 1)
    groups = list(range(a.shape[0] // SUBLANES))
    out = [None] * len(groups)
    edge = slice(0, 1) if up else slice(SUBLANES - 1, SUBLANES)
    carry = jnp.zeros((1, a.shape[1]), f32)
    for g in (reversed(groups) if up else groups):
        rows = slice(g * SUBLANES, (g + 1) * SUBLANES)
        xa, xb = a[rows], b[rows]
        d = 1
        while d < SUBLANES:
            keep = (sub < SUBLANES - d) if up else (sub >= d)
            shift = SUBLANES - d if up else d
            xb = xa * jnp.where(keep, pltpu.roll(xb, shift, 0), 0.0) + xb
            xa = xa * jnp.where(keep, pltpu.roll(xa, shift, 0), 1.0)
            d *= 2
        out[g] = xb + xa * carry
        carry = out[g][edge]
    return jnp.concatenate(out, axis=0)


def _scan_fwd(a, b, row):
    return _scan_rows(a, b, row, False)


def _scan_bwd(c, b, row):
    return _scan_rows(c, b, row, True)


def _conv_fwd(x, cw, cb, row):
    return (cb + cw[3:4] * x + cw[2:3] * _shift_dn(x, 1, row) + cw[1:2] * _shift_dn(x, 2, row)
            + cw[0:1] * _shift_dn(x, 3, row))


def _lru_specs(s, order):
    def im(f):
        return (lambda b, g: f(b, g)) if order == "bg" else (lambda g, b: f(b, g))
    seq = lambda off: pl.BlockSpec((s, 128), im(lambda b, g: (b, off + g)))
    vec = pl.BlockSpec((1, 128), im(lambda b, g: (0, g)))
    cw = pl.BlockSpec((4, 128), im(lambda b, g: (0, g)))
    mat = pl.BlockSpec((1, 128, 128), im(lambda b, g: (g, 0, 0)))
    return seq, vec, cw, mat


def _lru_fwd(proj, conv_w, conv_b, w_a, b_a, w_i, b_i, lam, nb, s):
    def body(x_ref, gt_ref, cw_ref, cb_ref, wa_ref, ba_ref, wi_ref, bi_ref, lam_ref, out_ref):
        row = lax.broadcasted_iota(jnp.int32, (s, 128), 0)
        xc = _conv_fwd(x_ref[...], cw_ref[...], cb_ref[...], row)
        xcb = xc.astype(bf16)
        pa = _nn(xcb, wa_ref[0].astype(bf16)) + ba_ref[...]
        pi = _nn(xcb, wi_ref[0].astype(bf16)) + bi_ref[...]
        a, bx = _lru_gates(xc, pa, pi, lam_ref[...])
        h = _scan_fwd(a, bx, row)
        out_ref[...] = h * jax.nn.gelu(gt_ref[...])

    seq, vec, cw, mat = _lru_specs(s, "bg")
    return pl.pallas_call(
        body, name="lru_fwd", grid=(nb, LRU_BLOCKS),
        in_specs=[seq(16), seq(20), cw, vec, mat, vec, mat, vec, vec],
        out_specs=seq(0),
        out_shape=jax.ShapeDtypeStruct((nb * s, LRU_WIDTH), f32),
        compiler_params=_cp("parallel", "parallel"),
    )(proj, proj, conv_w, conv_b, w_a, b_a, w_i, b_i, lam)


def _lru_bwd(dmerged, proj, conv_w, conv_b, w_a, b_a, w_i, b_i, lam, nb, s):
    def body(dout_ref, x_ref, gt_ref, cw_ref, cb_ref, wa_ref, ba_ref, wi_ref, bi_ref, lam_ref,
             dx_ref, dgt_ref, dcw_ref, dcb_ref, dwa_ref, dba_ref, dwi_ref, dbi_ref, dlam_ref):
        row = lax.broadcasted_iota(jnp.int32, (s, 128), 0)
        x = x_ref[...]
        cwv = cw_ref[...]
        xc = _conv_fwd(x, cwv, cb_ref[...], row)
        xcb = xc.astype(bf16)
        wab, wib = wa_ref[0].astype(bf16), wi_ref[0].astype(bf16)
        pa = _nn(xcb, wab) + ba_ref[...]
        pi = _nn(xcb, wib) + bi_ref[...]
        (a, bx), gates_vjp = jax.vjp(_lru_gates, xc, pa, pi, lam_ref[...])
        h = _scan_fwd(a, bx, row)
        ge, gelu_vjp = jax.vjp(jax.nn.gelu, gt_ref[...])
        dout = dout_ref[...]
        dgt_ref[...] = gelu_vjp(dout * h)[0]
        adj = _scan_bwd(_shift_up(a, 1, row), dout * ge, row)
        dxc, dpa, dpi, dlam = gates_vjp((adj * _shift_dn(h, 1, row), adj))
        dpab, dpib = dpa.astype(bf16), dpi.astype(bf16)
        dxc = dxc + _nt(dpab, wab) + _nt(dpib, wib)
        dx_ref[...] = (cwv[3:4] * dxc + cwv[2:3] * _shift_up(dxc, 1, row) + cwv[1:2] * _shift_up(dxc, 2, row)
                       + cwv[0:1] * _shift_up(dxc, 3, row))

        @pl.when(pl.program_id(1) == 0)
        def _():
            for r in (dcw_ref, dcb_ref, dwa_ref, dba_ref, dwi_ref, dbi_ref, dlam_ref):
                r[...] = jnp.zeros_like(r)
        rsum = lambda v: jnp.sum(v, axis=0, keepdims=True)
        dcw_ref[...] += jnp.concatenate([rsum(dxc * _shift_dn(x, 3, row)), rsum(dxc * _shift_dn(x, 2, row)),
                                         rsum(dxc * _shift_dn(x, 1, row)), rsum(dxc * x)], axis=0)
        dcb_ref[...] += rsum(dxc)
        dwa_ref[0] += _tn(xcb, dpab)
        dwi_ref[0] += _tn(xcb, dpib)
        dba_ref[...] += rsum(dpa)
        dbi_ref[...] += rsum(dpi)
        dlam_ref[...] += dlam

    seq, vec, cw, mat = _lru_specs(s, "gb")
    t = nb * s
    vshape = jax.ShapeDtypeStruct((1, LRU_WIDTH), f32)
    mshape = jax.ShapeDtypeStruct((LRU_BLOCKS, 128, 128), f32)
    return pl.pallas_call(
        body, name="lru_bwd", grid=(LRU_BLOCKS, nb),
        in_specs=[seq(4), seq(16), seq(20), cw, vec, mat, vec, mat, vec, vec],
        out_specs=[seq(0), seq(0), cw, vec, mat, vec, mat, vec, vec],
        out_shape=[jax.ShapeDtypeStruct((t, LRU_WIDTH), f32), jax.ShapeDtypeStruct((t, LRU_WIDTH), f32),
                   jax.ShapeDtypeStruct((4, LRU_WIDTH), f32), vshape, mshape, vshape, mshape, vshape, vshape],
        compiler_params=_cp("parallel", "arbitrary"),
    )(dmerged, proj, proj, conv_w, conv_b, w_a, b_a, w_i, b_i, lam)


def _s5_disc(lr, li, ldt, bre, bim):
    dt = jnp.exp(ldt)
    mag = jnp.exp(lr * dt)
    lbr = mag * jnp.cos(li * dt)
    lbi = mag * jnp.sin(li * dt)
    den = lr * lr + li * li
    nr = lbr - 1.0
    fr = (nr * lr + lbi * li) / den
    fi = (lbi * lr - nr * li) / den
    bbr = fr[:, None, :] * bre - fi[:, None, :] * bim
    bbi = fr[:, None, :] * bim + fi[:, None, :] * bre
    return lbr, lbi, bbr, bbi


def _s5_prep(lr, li, ldt, bre, bim):
    def body(lr_ref, li_ref, ldt_ref, bre_ref, bim_ref, o1, o2, o3, o4):
        o1[...], o2[...], o3[...], o4[...] = _s5_disc(lr_ref[...], li_ref[...], ldt_ref[...], bre_ref[...], bim_ref[...])

    return pl.pallas_call(
        body, name="s5_prep", in_specs=[VMEM_SPEC] * 5, out_specs=[VMEM_SPEC] * 4,
        out_shape=[jax.ShapeDtypeStruct(lr.shape, f32), jax.ShapeDtypeStruct(lr.shape, f32),
                   jax.ShapeDtypeStruct(bre.shape, f32), jax.ShapeDtypeStruct(bre.shape, f32)],
    )(lr, li, ldt, bre, bim)


def _s5_prep_bwd(lr, li, ldt, bre, bim, cts):
    def body(lr_ref, li_ref, ldt_ref, bre_ref, bim_ref, g1, g2, g3, g4, o1, o2, o3, o4, o5):
        _, vjp = jax.vjp(_s5_disc, lr_ref[...], li_ref[...], ldt_ref[...], bre_ref[...], bim_ref[...])
        o1[...], o2[...], o3[...], o4[...], o5[...] = vjp((g1[...], g2[...], g3[...], g4[...]))

    return pl.pallas_call(
        body, name="s5_prep_bwd", in_specs=[VMEM_SPEC] * 9, out_specs=[VMEM_SPEC] * 5,
        out_shape=[jax.ShapeDtypeStruct(v.shape, f32) for v in (lr, li, ldt, bre, bim)],
    )(lr, li, ldt, bre, bim, *cts)


def _cmul(ar, ai, br, bi):
    return ar * br - ai * bi, ar * bi + ai * br


def _s5_pow_table(lr, li, n, row, up):
    ar = jnp.broadcast_to(lr, (n, lr.shape[1]))
    ai = jnp.broadcast_to(li, (n, li.shape[1]))
    shift = _shift_up if up else _shift_dn
    d = 1
    while d < n:
        ar, ai = _cmul(ar, ai, shift(ar, d, row, 1.0), shift(ai, d, row, 0.0))
        d *= 2
    return ar, ai


def _s5_step_factors(lr, li, row, up):
    sub = row & (SUBLANES - 1)
    out, pr, pi, d = [], lr, li, 1
    while d < SUBLANES:
        keep = (sub < SUBLANES - d) if up else (sub >= d)
        out.append((jnp.where(keep, pr, 0.0), jnp.where(keep, pi, 0.0)))
        pr, pi = _cmul(pr, pi, pr, pi)
        d *= 2
    return out


def _s5_scan(br, bi, steps, tab_r, tab_i, cr, ci, up):
    groups = list(range(br.shape[0] // SUBLANES))
    out_r, out_i = [None] * len(groups), [None] * len(groups)
    edge = slice(0, 1) if up else slice(SUBLANES - 1, SUBLANES)
    for g in (reversed(groups) if up else groups):
        rows = slice(g * SUBLANES, (g + 1) * SUBLANES)
        xr, xi = br[rows], bi[rows]
        for k, (mr, mi) in enumerate(steps):
            shift = SUBLANES - (1 << k) if up else 1 << k
            tr, ti = _cmul(mr, mi, pltpu.roll(xr, shift, 0), pltpu.roll(xi, shift, 0))
            xr, xi = xr + tr, xi + ti
        tr, ti = _cmul(tab_r, tab_i, cr, ci)
        hr, hi = xr + tr, xi + ti
        out_r[g], out_i[g] = hr, hi
        cr, ci = hr[edge], hi[edge]
    return jnp.concatenate(out_r, axis=0), jnp.concatenate(out_i, axis=0)


def _s5_specs(t, nb, nc):
    seq = pl.BlockSpec((t, 128), lambda k: (0, k))
    lvec = pl.BlockSpec((1, S5_BLOCK_STATES), lambda k: (0, k))
    dvec = pl.BlockSpec((1, 128), lambda k: (0, k))
    wmat = pl.BlockSpec((1, 128, S5_BLOCK_STATES), lambda k: (k, 0, 0))
    h0 = pl.BlockSpec((nb, None, nc, 2, S5_BLOCK_STATES), lambda k: (0, k, 0, 0, 0))
    return seq, lvec, dvec, wmat, h0


def _s5_fwd(u, lbr, lbi, wbr, wbi, wcr, wci, dskip, nb, s):
    ln = S5_CHUNK
    nc = s // ln

    def body(u_ref, lr_ref, li_ref, wbr_ref, wbi_ref, wcr_ref, wci_ref, d_ref, yg_ref, y_ref, h0_ref):
        row = lax.broadcasted_iota(jnp.int32, (ln, S5_BLOCK_STATES), 0)
        lr, li = lr_ref[...], li_ref[...]
        pr, pi = _s5_pow_table(lr, li, SUBLANES, row[:SUBLANES], False)
        steps = _s5_step_factors(lr, li, row[:SUBLANES], False)
        dv = d_ref[...]

        def chunk(b, n, h0r, h0i):
            st = pl.multiple_of(b * s + n * ln, ln)
            uc = u_ref[pl.ds(st, ln), :]
            ub = uc.astype(bf16)
            hr, hi = _s5_scan(_nn(ub, wbr_ref[0]), _nn(ub, wbi_ref[0]), steps, pr, pi, h0r, h0i, False)
            h0_ref[b, n, 0:1, :] = h0r
            h0_ref[b, n, 1:2, :] = h0i
            y = _nt(hr.astype(bf16), wcr_ref[0]) - _nt(hi.astype(bf16), wci_ref[0]) + dv * uc
            y_ref[pl.ds(st, ln), :] = y
            yg_ref[pl.ds(st, ln), :] = jax.nn.gelu(y).astype(bf16)
            return hr[ln - 1:ln, :], hi[ln - 1:ln, :]

        def step(n, carry):
            return tuple(chunk(b, n, *carry[b]) for b in range(nb))

        z = jnp.zeros((1, S5_BLOCK_STATES), f32)
        lax.fori_loop(0, nc, step, ((z, z),) * nb)

    t = nb * s
    seq, lvec, dvec, wmat, h0 = _s5_specs(t, nb, nc)
    return pl.pallas_call(
        body, name="s5_fwd", grid=(S5_BLOCKS,),
        in_specs=[seq, lvec, lvec, wmat, wmat, wmat, wmat, dvec],
        out_specs=[seq, seq, h0],
        out_shape=[jax.ShapeDtypeStruct((t, D_MODEL), bf16), jax.ShapeDtypeStruct((t, D_MODEL), f32),
                   jax.ShapeDtypeStruct((nb, S5_BLOCKS, nc, 2, S5_BLOCK_STATES), f32)],
        compiler_params=_cp("parallel"),
    )(u, lbr, lbi, wbr, wbi, wcr, wci, dskip)


def _s5_bwd(dyg, y, u, h0, lbr, lbi, wbr, wbi, wcr, wci, dskip, nb, s):
    ln = S5_CHUNK
    nc = s // ln

    def body(dyg_ref, y_ref, u_ref, h0_ref, lr_ref, li_ref, wbr_ref, wbi_ref, wcr_ref, wci_ref, d_ref,
             du_ref, dlr_ref, dli_ref, dwbr_ref, dwbi_ref, dwcr_ref, dwci_ref, dd_ref):
        for r in (dlr_ref, dli_ref, dwbr_ref, dwbi_ref, dwcr_ref, dwci_ref, dd_ref):
            r[...] = jnp.zeros_like(r)
        row = lax.broadcasted_iota(jnp.int32, (ln, S5_BLOCK_STATES), 0)
        lr, li = lr_ref[...], li_ref[...]
        pr, pi = _s5_pow_table(lr, li, SUBLANES, row[:SUBLANES], False)
        qr, qi = _s5_pow_table(lr, -li, SUBLANES, row[:SUBLANES], True)
        row8 = row[:SUBLANES]
        steps_dn, steps_up = _s5_step_factors(lr, li, row8, False), _s5_step_factors(lr, -li, row8, True)
        dv = d_ref[...]
        rsum = lambda v: jnp.sum(v, axis=0, keepdims=True)

        def chunk(b, n, gnr, gni):
            st = pl.multiple_of(b * s + n * ln, ln)
            uc = u_ref[pl.ds(st, ln), :]
            ub = uc.astype(bf16)
            h0v = h0_ref[b, n]
            h0r, h0i = h0v[0:1], h0v[1:2]
            hr, hi = _s5_scan(_nn(ub, wbr_ref[0]), _nn(ub, wbi_ref[0]), steps_dn, pr, pi, h0r, h0i, False)
            dy = jax.vjp(jax.nn.gelu, y_ref[pl.ds(st, ln), :])[1](dyg_ref[pl.ds(st, ln), :])[0]
            dyb = dy.astype(bf16)
            dd_ref[...] += rsum(dy * uc)
            gr, gi = _s5_scan(_nn(dyb, wcr_ref[0]), -_nn(dyb, wci_ref[0]), steps_up, qr, qi, gnr, gni, True)
            hpr = jnp.where(row >= 1, pltpu.roll(hr, 1, 0), h0r)
            hpi = jnp.where(row >= 1, pltpu.roll(hi, 1, 0), h0i)
            dlr_ref[...] += rsum(gr * hpr + gi * hpi)
            dli_ref[...] += rsum(gi * hpr - gr * hpi)
            grb, gib = gr.astype(bf16), gi.astype(bf16)
            dwbr_ref[0] += _tn(ub, grb)
            dwbi_ref[0] += _tn(ub, gib)
            dwcr_ref[0] += _tn(dyb, hr.astype(bf16))
            dwci_ref[0] -= _tn(dyb, hi.astype(bf16))
            du_ref[pl.ds(st, ln), :] = _nt(grb, wbr_ref[0]) + _nt(gib, wbi_ref[0]) + dv * dy
            return gr[0:1, :], gi[0:1, :]

        def step(i, carry):
            return tuple(chunk(b, nc - 1 - i, *carry[b]) for b in range(nb))

        z = jnp.zeros((1, S5_BLOCK_STATES), f32)
        lax.fori_loop(0, nc, step, ((z, z),) * nb)

    t = nb * s
    seq, lvec, dvec, wmat, h0s = _s5_specs(t, nb, nc)
    lshape = jax.ShapeDtypeStruct((1, S5_BLOCKS * S5_BLOCK_STATES), f32)
    wshape = jax.ShapeDtypeStruct((S5_BLOCKS, 128, S5_BLOCK_STATES), f32)
    return pl.pallas_call(
        body, name="s5_bwd", grid=(S5_BLOCKS,),
        in_specs=[seq, seq, seq, h0s, lvec, lvec, wmat, wmat, wmat, wmat, dvec],
        out_specs=[seq, lvec, lvec, wmat, wmat, wmat, wmat, dvec],
        out_shape=[jax.ShapeDtypeStruct((t, D_MODEL), f32), lshape, lshape, wshape, wshape, wshape, wshape,
                   jax.ShapeDtypeStruct((1, D_MODEL), f32)],
        compiler_params=_cp("parallel"),
    )(dyg, y, u, h0, lbr, lbi, wbr, wbi, wcr, wci, dskip)


def _blockdiag(w):
    w4 = w.reshape(S5_BLOCKS, 8, S5_GROUP, S5_STATE)
    same_group = jnp.eye(8, dtype=bool)[None, :, None, :, None]
    return jnp.where(same_group, w4[:, :, :, None, :], 0.0).reshape(S5_BLOCKS, 128, S5_BLOCK_STATES)


def _blockdiag_t(dw):
    d5 = dw.reshape(S5_BLOCKS, 8, S5_GROUP, 8, S5_STATE)
    diag = jnp.diagonal(d5, axis1=1, axis2=3)
    return jnp.moveaxis(diag, 3, 1).reshape(S5_GROUPS, S5_GROUP, S5_STATE)


def _glu_fwd(ygb, wa, wb, x, tm=512, tn=1024):
    t, d = x.shape

    def body(y_ref, wa_ref, wb_ref, x_ref, o_ref, p_ref, q_ref):
        p = _nn(y_ref[...], wa_ref[...])
        q = _nn(y_ref[...], wb_ref[...])
        p_ref[...] = p
        q_ref[...] = q
        o_ref[...] = x_ref[...] + p * jax.nn.sigmoid(q)

    tile = pl.BlockSpec((tm, tn), lambda i, j: (i, j))
    wsp = pl.BlockSpec((d, tn), lambda i, j: (0, j))
    out = jax.ShapeDtypeStruct((t, d), f32)
    return pl.pallas_call(
        body, name="glu_fwd", grid=(t // tm, d // tn),
        in_specs=[pl.BlockSpec((tm, d), lambda i, j: (i, 0)), wsp, wsp, tile],
        out_specs=[tile, tile, tile], out_shape=[out, out, out],
        compiler_params=_cp("parallel", "parallel"),
    )(ygb, wa, wb, x)


def _place():
    x, y, c = lax.axis_index("x"), lax.axis_index("y"), lax.axis_index("c")
    return x, y, c, [(1 - x, y), (x, 1 - y), (1 - x, 1 - y)]


def _all_gather(name, arrays):
    n = len(arrays)

    def body(*refs):
        ins, outs = refs[:n], refs[n:2 * n]
        send_sems, recv_sems, local_sems = refs[2 * n:]
        x, y, c, chips = _place()
        me, sib = (x, y, c), (x, y, 1 - c)

        def copy(i, k, block, to, src=None):
            dst = outs[i].at[4 * block[0] + 2 * block[1] + block[2]]
            return pltpu.make_async_remote_copy(
                src_ref=dst if src is None else src, dst_ref=dst,
                send_sem=send_sems.at[i * 7 + k], recv_sem=recv_sems.at[i * 7 + k],
                device_id=to, device_id_type=MESH)

        mine = [pltpu.make_async_copy(ins[i], outs[i].at[4 * x + 2 * y + c], local_sems.at[i]) for i in range(n)]
        for m in mine:
            m.start()
        first = []
        for i in range(n):
            first.append(copy(i, 0, me, sib, src=ins[i]))
            first += [copy(i, 1 + j, me, (*chip, c), src=ins[i]) for j, chip in enumerate(chips)]
        for cp in first:
            cp.start()
        passed = []
        for j, chip in enumerate(chips):
            for i in range(n):
                copy(i, 1 + j, (*chip, c), me).wait_recv()
                fwd = copy(i, 4 + j, (*chip, c), sib)
                fwd.start()
                passed.append(fwd)
        for i in range(n):
            copy(i, 0, sib, me).wait_recv()
        for j, chip in enumerate(chips):
            for i in range(n):
                copy(i, 4 + j, (*chip, 1 - c), me).wait_recv()
        for cp in first + passed:
            cp.wait_send()
        for m in mine:
            m.wait()

    return pl.pallas_call(
        body, name=name,
        in_specs=[ANY_SPEC] * n, out_specs=[ANY_SPEC] * n,
        out_shape=[jax.ShapeDtypeStruct((N_DEV,) + a.shape, a.dtype) for a in arrays],
        scratch_shapes=[pltpu.SemaphoreType.DMA((7 * n,)), pltpu.SemaphoreType.DMA((7 * n,)),
                        pltpu.SemaphoreType.DMA((n,))],
    )(*arrays)


def _tie(name, x, deps):
    def body(*refs):
        pass

    return pl.pallas_call(
        body, name=name, in_specs=[ANY_SPEC] * (1 + len(deps)), out_specs=ANY_SPEC,
        out_shape=jax.ShapeDtypeStruct(x.shape, x.dtype), input_output_aliases={0: 0},
    )(x, *deps)


def _xchg_copies(kind, srcs, lands, suffixes, send_sems, recv_sems):
    x, y, c, _ = _place()
    copies = []
    for i, (src, land, sfx) in enumerate(zip(srcs, lands, suffixes)):
        for k in range(N_DEV - 1):
            r = k + 1
            peer = (1 - x if r & 4 else x, 1 - y if r & 2 else y, 1 - c if r & 1 else c)
            if kind == "gather":
                s_ref, d_ref = src, land.at[(4 * x + 2 * y + c,) + sfx]
            else:
                s_ref, d_ref = src.at[4 * peer[0] + 2 * peer[1] + peer[2]], land.at[(k,) + sfx]
            copies.append(pltpu.make_async_remote_copy(
                src_ref=s_ref, dst_ref=d_ref, send_sem=send_sems.at[i * 7 + k], recv_sem=recv_sems.at[i * 7 + k],
                device_id=peer, device_id_type=MESH))
    return copies


def _xchg_start(name, kind, srcs, lands, suffixes=None):
    n = len(srcs)
    suffixes = suffixes or [()] * n

    def body(*refs):
        src, land = refs[:n], refs[n:2 * n]
        send_sems, recv_sems, token = refs[2 * n], refs[2 * n + 1], refs[-1]
        for cp in _xchg_copies(kind, src, land, suffixes, send_sems, recv_sems):
            cp.start()
        token[...] = jnp.zeros_like(token)

    arrays = list(srcs) + list(lands)
    outs = pl.pallas_call(
        body, name=name,
        out_shape=(pltpu.SemaphoreType.DMA((7 * n,)), pltpu.SemaphoreType.DMA((7 * n,)),
                   *[pltpu.HBM(a.shape, a.dtype) for a in arrays], jax.ShapeDtypeStruct((8, 128), f32)),
        in_specs=[HBM_SPEC] * (2 * n),
        out_specs=(SEM_SPEC, SEM_SPEC, *[HBM_SPEC] * (2 * n), VMEM_SPEC),
        input_output_aliases={i: 2 + i for i in range(2 * n)},
        compiler_params=pltpu.CompilerParams(has_side_effects=SIDE_EFFECT),
    )(*[pltpu.with_memory_space_constraint(a, pltpu.HBM) for a in arrays])
    return dict(kind=kind, n=n, suffixes=suffixes, send=outs[0], recv=outs[1], srcs=list(outs[2:2 + n]),
                lands=list(outs[2 + n:2 + 2 * n]), token=outs[-1])


def _xchg_wait(name, h, after, lands=None):
    n = h["n"]
    lands = h["lands"] if lands is None else lands

    def body(*refs):
        src, land = refs[:n], refs[n:2 * n]
        for cp in _xchg_copies(h["kind"], src, land, h["suffixes"], refs[2 * n], refs[2 * n + 1]):
            cp.wait_send()
            cp.wait_recv()

    arrays = h["srcs"] + list(lands)
    outs = pl.pallas_call(
        body, name=name,
        out_shape=tuple(pltpu.HBM(a.shape, a.dtype) for a in arrays),
        in_specs=[HBM_SPEC] * (2 * n) + [SEM_SPEC, SEM_SPEC] + [ANY_SPEC] * len(after),
        out_specs=tuple([HBM_SPEC] * (2 * n)),
        input_output_aliases={i: i for i in range(2 * n)},
        compiler_params=pltpu.CompilerParams(has_side_effects=SIDE_EFFECT),
    )(*arrays, h["send"], h["recv"], *after)
    return list(outs[n:])


def _rows(a):
    return a.reshape(-1, a.shape[-1])


def _row_tile(r):
    for tm in (512, 256, 128, 64, 32, 16, 8):
        if r % tm == 0:
            return tm
    return r


def _sum8(name, gathered):
    _, r, n = gathered.shape
    tm = _row_tile(r)

    def body(g_ref, o_ref):
        acc = g_ref[0]
        for k in range(1, N_DEV):
            acc = acc + g_ref[k]
        o_ref[...] = acc

    return pl.pallas_call(
        body, name=name, grid=(r // tm,),
        in_specs=[pl.BlockSpec((N_DEV, tm, n), lambda i: (0, i, 0))],
        out_specs=pl.BlockSpec((tm, n), lambda i: (i, 0)),
        out_shape=jax.ShapeDtypeStruct((r, n), f32),
        compiler_params=_cp("parallel"),
    )(gathered)


def _adamw(name, w, m, v, own, landed=None, slot=None):
    shape = w.shape
    w2, m2, v2 = _rows(w), _rows(m), _rows(v)
    r, n = w2.shape
    tm = _row_tile(r)
    c1 = 1.0 - ADAM_B1 ** ADAM_STEP
    c2 = 1.0 - ADAM_B2 ** ADAM_STEP
    extra = [] if landed is None else [landed.reshape(landed.shape[0], r, n)]
    row = pl.BlockSpec((tm, n), lambda i, *_: (i, 0))
    if slot is None:
        o2, own_spec, scalars = _rows(own), row, []
    else:
        dev, kind = slot
        scalars = [dev.reshape(1).astype(jnp.int32)]
        if kind == "lead":
            o2, own_spec = own.reshape(N_DEV, r, n), pl.BlockSpec((None, tm, n), lambda i, d: (d[0], i, 0))
        elif kind == "rows":
            o2, own_spec = own, pl.BlockSpec((tm, n), lambda i, d: (d[0] * (r // tm) + i, 0))
        else:
            o2, own_spec = own, pl.BlockSpec((tm, n), lambda i, d: (i, d[0]))

    def body(*refs):
        w_ref, m_ref, v_ref, o_ref = refs[len(scalars):len(scalars) + 4]
        refs = refs[len(scalars) + 4:]
        g = o_ref[...]
        if extra:
            for k in range(extra[0].shape[0]):
                g = g + refs[0][k].astype(f32)
        g_ref, d_ref, mn_ref, vn_ref = refs[len(extra):]
        mn = ADAM_B1 * m_ref[...] + (1.0 - ADAM_B1) * g
        vn = ADAM_B2 * v_ref[...] + (1.0 - ADAM_B2) * (g * g)
        g_ref[...] = g
        d_ref[...] = -ADAM_LR * ((mn / c1) / (jnp.sqrt(vn / c2) + ADAM_EPS) + ADAM_WD * w_ref[...])
        mn_ref[...] = mn
        vn_ref[...] = vn

    outs = pl.pallas_call(
        body, name=name,
        grid_spec=pltpu.PrefetchScalarGridSpec(
            num_scalar_prefetch=len(scalars), grid=(r // tm,),
            in_specs=[row] * 3 + [own_spec] + [pl.BlockSpec((e.shape[0], tm, n), lambda i, *_: (0, i, 0)) for e in extra],
            out_specs=[row] * 4),
        out_shape=[jax.ShapeDtypeStruct((r, n), f32)] * 4,
        compiler_params=_cp("parallel"),
    )(*scalars, w2, m2, v2, o2, *extra)
    return [o.reshape(shape) for o in outs]


def _pack(arrays):
    flat = jnp.concatenate([a.reshape(-1).astype(f32) for a in arrays])
    pad = (-flat.shape[0]) % (128 * (512 if flat.shape[0] > 128 * 512 else 8))
    return jnp.pad(flat, (0, pad)).reshape(-1, 128)


def _unpack(packed, shapes):
    flat = packed.reshape(-1)
    out, off = [], 0
    for s in shapes:
        n = math.prod(s)
        out.append(flat[off:off + n].reshape(s))
        off += n
    return out


def _local_step(x, target, w, weights_of, send, last_small, nb, s):
    cos, sin = _rope_tables(s)
    g = {}
    ffn_saved = {}
    ffn_bufs = [lax.empty((N_DEV, 2, 2) + shp, f32)
                for shp in ((D_MODEL, FF_SHARD), (D_MODEL, FF_SHARD), (FF_SHARD, D_MODEL))]

    def ffn(xin, l, h, wts):
        y, a, b = _ffn_fwd(f"ffn_fwd_{l}{h}", xin, w["ffn_g"][l][h], *wts)
        ffn_saved[(l, h)] = (xin, a, b, wts)
        return y

    def ffn_back(dy, l, h):
        xin, a, b, wts = ffn_saved[(l, h)]
        dx, dg, hb, dyh, u, da, db = _ffn_dx(f"ffn_dx_{l}{h}", dy, xin, w["ffn_g"][l][h], *wts, a, b)
        g[f"ffn_g_{l}{h}"] = dg
        if (l, h) == (0, 0):
            hb = last_small(g, hb)
        ffn_bufs[0], half = _ffn_dw_one(f"ffn_dw_{l}{h}_w1", hb, da, ffn_bufs[0], l, h)
        hb = send(f"ffn_{l}{h}_w1", {"ffn_w1": half}, hb)
        ffn_bufs[1], half = _ffn_dw_one(f"ffn_dw_{l}{h}_w3", hb, db, ffn_bufs[1], l, h)
        u = send(f"ffn_{l}{h}_w3", {"ffn_w3": half}, u)
        ffn_bufs[2], half = _ffn_dw_one(f"ffn_dw_{l}{h}_w2", u, dyh, ffn_bufs[2], l, h)
        return send(f"ffn_{l}{h}_w2", {"ffn_w2": half}, dx)

    def slots(t):
        return t.reshape(N_DEV, D_MODEL // N_DEV, D_MODEL)

    x1 = ffn(x, 0, 0, weights_of(0, [])["ffn"])
    wg = weights_of(1, [x1])
    w_in, w_out = wg["w_in"], wg["w_out"]
    _, h0b = _norm_fwd("mix_norm_0", x1, w["mix_g"][0])
    proj = _mm("in_proj", h0b, w_in, "nn", tn=1536)[0]
    o_raw, rprev, mret = _ret_fwd(proj, cos, sin, w["ret_g"], nb, s)
    lru = _lru_fwd(proj, w["conv_w"], w["conv_b"], w["lru_w_a"], w["lru_b_a"], w["lru_w_i"], w["lru_b_i"], w["lru_lam"], nb, s)
    merged = _ew("merge", lambda a, b: (jnp.concatenate([a, b], axis=1),), [mret, lru], [(D_MODEL, bf16)])[0]
    x2 = _mm("out_proj", merged, w_out, "nn", extras=[x1], epilogue=lambda acc, r: (acc + r,))[0]
    x3 = ffn(x2, 0, 1, weights_of(2, [x2])["ffn"])
    wg = weights_of(3, [x3])
    glu_a, glu_b = wg["glu_a"], wg["glu_b"]
    x4 = ffn(x3, 1, 0, wg["ffn"])
    u, _ = _norm_fwd("mix_norm_1", x4, w["mix_g"][1])
    lbr, lbi, bbr, bbi = _s5_prep(w["s5_lr"], w["s5_li"], w["s5_ldt"], w["s5_bre"], w["s5_bim"])
    lbr_f, lbi_f = lbr.reshape(1, -1), lbi.reshape(1, -1)
    wbr, wbi = _blockdiag(bbr).astype(bf16), _blockdiag(bbi).astype(bf16)
    wcr, wci = _blockdiag(w["s5_cre"]).astype(bf16), _blockdiag(w["s5_cim"]).astype(bf16)
    ygb, ypre, h0s = _s5_fwd(u, lbr_f, lbi_f, wbr, wbi, wcr, wci, w["s5_d"], nb, s)
    x5, gp, gq = _glu_fwd(ygb, glu_a, glu_b, x4)
    x6 = ffn(x5, 1, 1, weights_of(4, [x5])["ffn"])
    loss, dx6, g["final_g"] = _final_loss(x6, w["final_g"], target)

    dx5 = ffn_back(dx6, 1, 1)

    def glu_bwd(d, p, q):
        sg = jax.nn.sigmoid(q)
        return d * sg, d * p * sg * (1.0 - sg)

    dp, dq = _ew("glu_bwd", glu_bwd, [dx5, gp, gq], [(D_MODEL, bf16), (D_MODEL, bf16)])
    dyg = _mm("glu_dy_a", dp, glu_a, "nt")[0]
    dyg = _mm("glu_dy_b", dq, glu_b, "nt", extras=[dyg], epilogue=lambda acc, r: (acc + r,))[0]
    g["glu_a"], ga_half = _mm_tn("glu_dw_a", ygb, dp)
    g["glu_b"], gb_half = _mm_tn("glu_dw_b", ygb, dq)
    dyg = send("glu", {"glu_a": slots(ga_half), "glu_b": slots(gb_half)}, dyg)
    du, dlr, dli, dwbr, dwbi, dwcr, dwci, g["s5_d"] = _s5_bwd(dyg, ypre, u, h0s, lbr_f, lbi_f, wbr, wbi, wcr, wci, w["s5_d"], nb, s)
    g["s5_cre"], g["s5_cim"] = _blockdiag_t(dwcr), _blockdiag_t(dwci)
    g["s5_lr"], g["s5_li"], g["s5_ldt"], g["s5_bre"], g["s5_bim"] = _s5_prep_bwd(
        w["s5_lr"], w["s5_li"], w["s5_ldt"], w["s5_bre"], w["s5_bim"],
        (dlr.reshape(S5_GROUPS, S5_STATE), dli.reshape(S5_GROUPS, S5_STATE), _blockdiag_t(dwbr), _blockdiag_t(dwbi)))
    dx4, g["mix_g_1"] = _norm_bwd("mix_norm_1_bwd", du, x4, w["mix_g"][1], dx5)
    dx3 = ffn_back(dx4, 1, 0)
    dx2 = ffn_back(dx3, 0, 1)
    dmerged = _mm("out_proj_dx", dx2, w_out, "nt")[0]
    g["w_out"], wo_half = _mm_tn("out_proj_dw", merged, dx2)
    dmerged = send("w_out", {"w_out": slots(wo_half)}, dmerged)
    dq_, dk_, dv_, dgate, g["ret_g"] = _ret_bwd(dmerged, o_raw, rprev, proj, cos, sin, w["ret_g"], nb, s)
    (dxl, dgl, g["conv_w"], g["conv_b"], g["lru_w_a"], g["lru_b_a"], g["lru_w_i"], g["lru_b_i"], g["lru_lam"]) = _lru_bwd(
        dmerged, proj, w["conv_w"], w["conv_b"], w["lru_w_a"], w["lru_b_a"], w["lru_w_i"], w["lru_b_i"], w["lru_lam"], nb, s)
    dproj = _ew("dproj", lambda *p: (jnp.concatenate(p, axis=1),), [dq_, dk_, dv_, dgate, dxl, dgl], [(3072, bf16)])[0]
    dh0 = _mm("in_proj_dx", dproj, w_in, "nt")[0]
    g["w_in"], wi_half = _mm_tn("in_proj_dw", h0b, dproj)
    dh0 = send("w_in", {"w_in": jnp.transpose(wi_half.reshape(D_MODEL, N_DEV, IN_SHARD), (1, 0, 2))}, dh0)
    dx1, g["mix_g_0"] = _norm_bwd("mix_norm_0_bwd", dh0, x1, w["mix_g"][0], dx2)
    dx0 = ffn_back(dx1, 0, 0)
    g["ffn_w1"], g["ffn_w3"], g["ffn_w2"] = ffn_bufs
    return loss, dx0, g


_WEIGHTS = ["ffn_norm_g", "ffn_w1", "ffn_w3", "ffn_w2", "mix_norm_g", "w_in_even", "w_out_even", "ret_norm_g", "conv_w",
            "conv_b", "lru_w_a", "lru_b_a", "lru_w_i", "lru_b_i", "lru_lambda", "s5_lambda_re", "s5_lambda_im", "s5_log_dt",
            "s5_b_re", "s5_b_im", "s5_c_re", "s5_c_im", "s5_d", "glu_w_a", "glu_w_b", "final_norm_g"]
_BIG = ["ffn_w1", "ffn_w3", "ffn_w2", "w_in_even", "w_out_even", "glu_w_a", "glu_w_b"]
_SMALL_SHARDED = ["ffn_norm_g", "conv_w", "s5_d"]
_SMALL = [n for n in _WEIGHTS if n not in _BIG]
_MIDSIZE = ["lru_w_a", "lru_w_i", "s5_b_re", "s5_b_im", "s5_c_re", "s5_c_im"]


def kernel(x, ffn_norm_g, ffn_w1, ffn_w3, ffn_w2, mix_norm_g, w_in_even, w_out_even, ret_norm_g, conv_w, conv_b, lru_w_a, lru_b_a, lru_w_i, lru_b_i, lru_lambda, s5_lambda_re, s5_lambda_im, s5_log_dt, s5_b_re, s5_b_im, s5_c_re, s5_c_im, s5_d, glu_w_a, glu_w_b, final_norm_g, loss_target, m_ffn_norm_g, m_ffn_w1, m_ffn_w3, m_ffn_w2, m_mix_norm_g, m_w_in_even, m_w_out_even, m_ret_norm_g, m_conv_w, m_conv_b, m_lru_w_a, m_lru_b_a, m_lru_w_i, m_lru_b_i, m_lru_lambda, m_s5_lambda_re, m_s5_lambda_im, m_s5_log_dt, m_s5_b_re, m_s5_b_im, m_s5_c_re, m_s5_c_im, m_s5_d, m_glu_w_a, m_glu_w_b, m_final_norm_g, v_ffn_norm_g, v_ffn_w1, v_ffn_w3, v_ffn_w2, v_mix_norm_g, v_w_in_even, v_w_out_even, v_ret_norm_g, v_conv_w, v_conv_b, v_lru_w_a, v_lru_b_a, v_lru_w_i, v_lru_b_i, v_lru_lambda, v_s5_lambda_re, v_s5_lambda_im, v_s5_log_dt, v_s5_b_re, v_s5_b_im, v_s5_c_re, v_s5_c_im, v_s5_d, v_glu_w_a, v_glu_w_b, v_final_norm_g):
    a = dict(locals())
    nb, s, d = x.shape
    ax, ay, ac = lax.axis_index("x"), lax.axis_index("y"), lax.axis_index("c")
    dev = 4 * ax + 2 * ay + ac
    chip = 2 * ax + ay

    def ffn_shards(l, h):
        extra = FF_PAD - FF_SHARD
        return [jnp.pad(ffn_w1[l, h].astype(bf16), ((0, 0), (0, extra))), jnp.pad(ffn_w3[l, h].astype(bf16), ((0, 0), (0, extra))),
                jnp.pad(ffn_w2[l, h].astype(bf16), ((0, extra), (0, 0)))]

    first = _all_gather("ag_first", ffn_shards(0, 0) + [_pack([ffn_norm_g, conv_w, s5_d])])
    sm = first[3].reshape(N_DEV, -1)
    ffn_g_full = jnp.transpose(sm[:, :512].reshape(N_DEV, 2, 2, 128), (1, 2, 0, 3)).reshape(2, 2, D_MODEL)
    conv_w_full = jnp.transpose(sm[:, 512:768].reshape(N_DEV, 4, 64), (1, 0, 2)).reshape(4, LRU_WIDTH)
    s5_d_full = sm[:, 768:896].reshape(1, D_MODEL)

    ag_src = [None, [w_in_even[0].astype(bf16), w_out_even[0].astype(bf16)], ffn_shards(0, 1),
              ffn_shards(1, 0) + [glu_w_a[0].astype(bf16), glu_w_b[0].astype(bf16)], ffn_shards(1, 1)]
    ag, token = [None], first[0]
    for k, grp in enumerate(ag_src):
        if grp is None:
            continue
        grp[0] = _tie(f"tie_ag_{k}", grp[0], [token])
        lands = [lax.dynamic_update_index_in_dim(lax.empty((N_DEV,) + t.shape, bf16), t, dev, 0) for t in grp]
        ag.append(_xchg_start(f"ag_start_{k}", "gather", grp, lands))
        token = ag[-1]["token"]

    def weights_of(k, after):
        if k == 0:
            return {"ffn": [first[0], _tie("tie_ag_started", first[1], [h["token"] for h in ag[1:]]), first[2]]}
        got = _xchg_wait(f"ag_wait_{k}", ag[k], after)
        if k == 1:
            return {"w_in": jnp.transpose(got[0], (1, 0, 2)).reshape(D_MODEL, N_DEV * IN_SHARD),
                    "w_out": got[1].reshape(D_MODEL, D_MODEL)}
        if k == 3:
            return {"ffn": got[:3], "glu_a": got[3].reshape(D_MODEL, D_MODEL), "glu_b": got[4].reshape(D_MODEL, D_MODEL)}
        return {"ffn": got}

    ffn_lands = [lax.empty((N_DEV - 1, 2, 2) + shp, bf16)
                 for shp in ((D_MODEL, FF_SHARD), (D_MODEL, FF_SHARD), (FF_SHARD, D_MODEL))]
    rs = []

    ffn_names = ("ffn_w1", "ffn_w3", "ffn_w2")

    def send(group, arrays, carry):
        srcs = list(arrays.values())
        if group.startswith("ffn_"):
            which = [ffn_names.index(n) for n in arrays]
            sfx = [(int(group[4]), int(group[5]))] * len(which)
            h = _xchg_start("rs_start_" + group, "scatter", srcs, [ffn_lands[k] for k in which], sfx)
            for k, land in zip(which, h["lands"]):
                ffn_lands[k] = land
        else:
            h = _xchg_start("rs_start_" + group, "scatter", srcs,
                            [lax.empty((N_DEV - 1,) + t.shape[1:], bf16) for t in srcs])
        rs.append((group, list(arrays), h))
        return _tie("tie_" + group, carry, [h["token"]])

    w = {
        "ffn_g": [[ffn_g_full[l, h].reshape(1, D_MODEL) for h in range(2)] for l in range(2)],
        "mix_g": [mix_norm_g[0:1], mix_norm_g[1:2]],
        "ret_g": ret_norm_g, "conv_w": conv_w_full, "conv_b": conv_b,
        "lru_w_a": lru_w_a[0], "lru_b_a": lru_b_a, "lru_w_i": lru_w_i[0], "lru_b_i": lru_b_i, "lru_lam": lru_lambda,
        "s5_lr": s5_lambda_re[0], "s5_li": s5_lambda_im[0], "s5_ldt": s5_log_dt.reshape(S5_GROUPS, 1),
        "s5_bre": jnp.swapaxes(s5_b_re[0], 1, 2), "s5_bim": jnp.swapaxes(s5_b_im[0], 1, 2),
        "s5_cre": s5_c_re[0], "s5_cim": s5_c_im[0], "s5_d": s5_d_full,
        "final_g": final_norm_g.reshape(1, D_MODEL),
    }

    small_grads = {}

    def last_small(g, carry):
        part = _small_partials(g)
        mine = _pack([part[n] for n in _SMALL])
        land = lax.dynamic_update_index_in_dim(lax.empty((N_DEV,) + mine.shape, f32), mine, dev, 0)
        h = _xchg_start("ag_start_small_grads", "gather", [mine], [land])
        small_grads.update(h=h, shapes=[part[n].shape for n in _SMALL])
        return _tie("tie_small_grads", carry, [h["token"]])

    loss_part, dx, g = _local_step(x.reshape(nb * s, d), loss_target.reshape(nb * s, d), w, weights_of, send, last_small,
                                   nb, s)
    loss = lax.psum(loss_part[0, 0], ("x", "y", "c"))
    (gath,) = _xchg_wait("ag_wait_small_grads", small_grads["h"], [dx])
    full = dict(zip(_SMALL, _unpack(_sum8("sum_small_grads", gath), small_grads["shapes"])))
    for n in _SMALL_SHARDED:
        width = a[n].shape[-1]
        full[n] = lax.dynamic_slice_in_dim(full[n], dev * width, width, axis=full[n].ndim - 1)
    res = {n: _adamw("adamw_" + n, a[n], a["m_" + n], a["v_" + n], full[n]) for n in _MIDSIZE}
    tiny = [n for n in _SMALL if n not in _MIDSIZE]
    shapes = [a[n].shape for n in tiny]
    packed = _adamw("adamw_small", _pack([a[n] for n in tiny]), _pack([a["m_" + n] for n in tiny]),
                    _pack([a["v_" + n] for n in tiny]), _pack([full[n] for n in tiny]))
    res.update({n: vals for n, vals in zip(tiny, zip(*[_unpack(p, shapes) for p in packed]))})
    return _finish(a, g, dx, loss, res, packed, rs, ffn_lands, dev, nb, s, d)


def _small_partials(g):
    return {
        "ffn_norm_g": jnp.stack([jnp.stack([g[f"ffn_g_{l}{h}"][0] for h in range(2)]) for l in range(2)]),
        "mix_norm_g": jnp.concatenate([g["mix_g_0"], g["mix_g_1"]], axis=0),
        "ret_norm_g": g["ret_g"], "conv_w": g["conv_w"][None], "conv_b": g["conv_b"],
        "lru_w_a": g["lru_w_a"][None], "lru_b_a": g["lru_b_a"], "lru_w_i": g["lru_w_i"][None], "lru_b_i": g["lru_b_i"],
        "lru_lambda": g["lru_lam"], "s5_lambda_re": g["s5_lr"][None], "s5_lambda_im": g["s5_li"][None],
        "s5_log_dt": g["s5_ldt"].reshape(1, S5_GROUPS),
        "s5_b_re": jnp.swapaxes(g["s5_bre"], 1, 2)[None], "s5_b_im": jnp.swapaxes(g["s5_bim"], 1, 2)[None],
        "s5_c_re": g["s5_cre"][None], "s5_c_im": g["s5_cim"][None], "s5_d": g["s5_d"], "final_norm_g": g["final_g"][0],
    }


def _finish(a, g, dx, loss, res, packed, rs, ffn_lands, dev, nb, s, d):
    landed = {}
    for group, names, h in rs:
        if not group.startswith("ffn_"):
            landed.update(zip(names, _xchg_wait("rs_wait_" + group, h, [dx])))
    kinds = {"ffn_w1": "lead", "ffn_w3": "lead", "ffn_w2": "lead", "w_in": "cols", "w_out": "rows", "glu_a": "rows",
             "glu_b": "rows"}

    def update(n, short):
        res[n] = _adamw("adamw_" + n, a[n], a["m_" + n], a["v_" + n], g[short],
                        landed[short].reshape((N_DEV - 1,) + a[n].shape), slot=(dev, kinds[short]))

    for n, short in zip(_BIG[3:], ("w_in", "w_out", "glu_a", "glu_b")):
        update(n, short)
    after = [dx, packed[0]] + [res[n][0] for n in _BIG[3:]]
    for k, n in enumerate(("ffn_w1", "ffn_w3", "ffn_w2")):
        for group, names, h in rs:
            if group.startswith("ffn_") and names == [n]:
                (ffn_lands[k],) = _xchg_wait("rs_wait_" + group, h, after, [ffn_lands[k]])
        landed[n] = ffn_lands[k]
        update(n, n)
        after = after + [res[n][0]]

    out = [loss, dx.reshape(nb, s, d)]
    for k in range(4):
        out += [res[n][k] for n in _WEIGHTS]
    return tuple(out)
```

```python
import functools
import math

import numpy as np
import jax
import jax.numpy as jnp
from jax import lax
from jax.experimental import pallas as pl
from jax.experimental.pallas import tpu as pltpu

f32 = jnp.float32
bf16 = jnp.bfloat16

D_MODEL = 1024
N_DEV = 8
EPS = 1e-6
RET_HEADS = 4
HEAD_DIM = 128
RET_WIDTH = 512
RET_CHUNK = 128
ROPE_BASE = 10000.0
LRU_WIDTH = 512
LRU_BLOCKS = 4
LRU_C = 8.0
S5_GROUP = 16
S5_GROUPS = 64
S5_STATE = 64
S5_CHUNK = 512
S5_BLOCKS = 8
S5_BLOCK_STATES = 512
SUBLANES = 8
D_FF = 2816
FF_SHARD = D_FF // N_DEV
FF_PAD = 384
IN_SHARD = 3072 // N_DEV
ADAM_LR = 0.001
ADAM_B1 = 0.9
ADAM_B2 = 0.999
ADAM_EPS = 1e-08
ADAM_WD = 0.01
ADAM_STEP = 10

VMEM_LIMIT = 56 * 1024 * 1024
VMEM_SPEC = pl.BlockSpec(memory_space=pltpu.VMEM)
ANY_SPEC = pl.BlockSpec(memory_space=pl.ANY)
HBM_SPEC = pl.BlockSpec(memory_space=pltpu.HBM)
SEM_SPEC = pl.BlockSpec(memory_space=pltpu.SEMAPHORE)
SIDE_EFFECT = pltpu.SideEffectType.DATAFLOW_SIDE_EFFECTING
MESH = pl.DeviceIdType.MESH


def _cp(*sem):
    return pltpu.CompilerParams(dimension_semantics=sem, vmem_limit_bytes=VMEM_LIMIT)


def _nn(a, b):
    return jnp.dot(a, b, preferred_element_type=f32)


def _nt(a, b):
    return lax.dot_general(a, b, (((1,), (1,)), ((), ())), preferred_element_type=f32)


def _tn(a, b):
    return lax.dot_general(a, b, (((0,), (0,)), ((), ())), preferred_element_type=f32)


def _rms_fwd(x, g):
    r = lax.rsqrt(jnp.mean(x * x, axis=-1, keepdims=True) + EPS)
    xn = x * r
    return xn * g, xn, r


def _rms_bwd(dh, xn, r, g):
    dxn = dh * g
    dx = r * (dxn - xn * jnp.mean(dxn * xn, axis=-1, keepdims=True))
    dg = jnp.sum(dh * xn, axis=0, keepdims=True)
    return dx, dg


def _shift_dn(v, d, row, fill=0.0):
    return jnp.where(row >= d, pltpu.roll(v, d, 0), fill)


def _shift_up(v, d, row, fill=0.0):
    n = v.shape[0]
    return jnp.where(row < n - d, pltpu.roll(v, n - d, 0), fill)


def _ew(name, fn, ins, outs, tm=512):
    t = ins[0].shape[0]
    n_in = len(ins)

    def body(*refs):
        res = fn(*[r[...] for r in refs[:n_in]])
        for o, v in zip(refs[n_in:], res):
            o[...] = v.astype(o.dtype)

    return pl.pallas_call(
        body, name=name, grid=(t // tm,),
        in_specs=[pl.BlockSpec((tm, a.shape[1]), lambda i: (i, 0)) for a in ins],
        out_specs=[pl.BlockSpec((tm, n), lambda i: (i, 0)) for n, _ in outs],
        out_shape=[jax.ShapeDtypeStruct((t, n), dt) for n, dt in outs],
        compiler_params=_cp("parallel"),
    )(*ins)


def _mm(name, x, w, kind, extras=(), epilogue=None, outs=None, tm=512, tn=1024):
    t = x.shape[0]
    n = w.shape[1] if kind == "nn" else w.shape[0]
    tn = min(tn, n)
    outs = outs or [f32]
    n_ex = len(extras)

    def body(x_ref, w_ref, *refs):
        xb = x_ref[...].astype(bf16)
        acc = _nn(xb, w_ref[...]) if kind == "nn" else _nt(xb, w_ref[...])
        res = epilogue(acc, *[r[...] for r in refs[:n_ex]]) if epilogue else (acc,)
        for o, v in zip(refs[n_ex:], res):
            o[...] = v.astype(o.dtype)

    w_spec = (pl.BlockSpec((w.shape[0], tn), lambda i, j: (0, j)) if kind == "nn"
              else pl.BlockSpec((tn, w.shape[1]), lambda i, j: (j, 0)))
    tile = pl.BlockSpec((tm, tn), lambda i, j: (i, j))
    return pl.pallas_call(
        body, name=name, grid=(t // tm, n // tn),
        in_specs=[pl.BlockSpec((tm, x.shape[1]), lambda i, j: (i, 0)), w_spec] + [tile] * n_ex,
        out_specs=[tile] * len(outs),
        out_shape=[jax.ShapeDtypeStruct((t, n), dt) for dt in outs],
        compiler_params=_cp("parallel", "parallel"),
    )(x, w, *extras)


def _mm_tn(name, x, y, tk=1024, tn=1024, tt=1024):
    t, k = x.shape
    n = y.shape[1]
    tk, tn, tt = min(tk, k), min(tn, n), min(tt, t)

    def body(x_ref, y_ref, o_ref, ob_ref):
        @pl.when(pl.program_id(2) == 0)
        def _():
            o_ref[...] = jnp.zeros_like(o_ref)
        o_ref[...] += _tn(x_ref[...].astype(bf16), y_ref[...].astype(bf16))

        @pl.when(pl.program_id(2) == pl.num_programs(2) - 1)
        def _():
            ob_ref[...] = o_ref[...].astype(bf16)

    out = pl.BlockSpec((tk, tn), lambda i, j, s: (i, j))
    return pl.pallas_call(
        body, name=name, grid=(k // tk, n // tn, t // tt),
        in_specs=[pl.BlockSpec((tt, tk), lambda i, j, s: (s, i)), pl.BlockSpec((tt, tn), lambda i, j, s: (s, j))],
        out_specs=[out, out],
        out_shape=[jax.ShapeDtypeStruct((k, n), f32), jax.ShapeDtypeStruct((k, n), bf16)],
        compiler_params=_cp("parallel", "parallel", "arbitrary"),
    )(x, y)


def _norm_fwd(name, x, g, tm=512):
    t, d = x.shape

    def body(x_ref, g_ref, h_ref, hb_ref):
        h, _, _ = _rms_fwd(x_ref[...], g_ref[...])
        h_ref[...] = h
        hb_ref[...] = h.astype(bf16)

    row = pl.BlockSpec((tm, d), lambda i: (i, 0))
    return pl.pallas_call(
        body, name=name, grid=(t // tm,),
        in_specs=[row, pl.BlockSpec((1, d), lambda i: (0, 0))],
        out_specs=[row, row],
        out_shape=[jax.ShapeDtypeStruct((t, d), f32), jax.ShapeDtypeStruct((t, d), bf16)],
        compiler_params=_cp("parallel"),
    )(x, g)


def _norm_bwd(name, dh, x, g, dres, tm=512):
    t, d = x.shape

    def body(dh_ref, x_ref, g_ref, dres_ref, dx_ref, dg_ref):
        gv = g_ref[...]
        _, xn, r = _rms_fwd(x_ref[...], gv)
        dx, dg = _rms_bwd(dh_ref[...], xn, r, gv)
        dx_ref[...] = dres_ref[...] + dx

        @pl.when(pl.program_id(0) == 0)
        def _():
            dg_ref[...] = jnp.zeros_like(dg_ref)
        dg_ref[...] += dg

    row = pl.BlockSpec((tm, d), lambda i: (i, 0))
    vec = pl.BlockSpec((1, d), lambda i: (0, 0))
    return pl.pallas_call(
        body, name=name, grid=(t // tm,),
        in_specs=[row, row, vec, row],
        out_specs=[row, vec],
        out_shape=[jax.ShapeDtypeStruct((t, d), f32), jax.ShapeDtypeStruct((1, d), f32)],
        compiler_params=_cp("arbitrary"),
    )(dh, x, g, dres)


def _final_loss(x, g, target, tm=512):
    t, d = x.shape

    def body(x_ref, g_ref, t_ref, loss_ref, dx_ref, dg_ref):
        gv = g_ref[...]
        y, xn, r = _rms_fwd(x_ref[...], gv)
        err = y - t_ref[...]
        dy = err * (1.0 / d)
        dx, dg = _rms_bwd(dy, xn, r, gv)
        dx_ref[...] = dx

        @pl.when(pl.program_id(0) == 0)
        def _():
            dg_ref[...] = jnp.zeros_like(dg_ref)
            loss_ref[...] = jnp.zeros_like(loss_ref)
        dg_ref[...] += dg
        loss_ref[...] += jnp.full((1, 128), 0.5 / d, f32) * jnp.sum(err * err)

    row = pl.BlockSpec((tm, d), lambda i: (i, 0))
    vec = pl.BlockSpec((1, d), lambda i: (0, 0))
    return pl.pallas_call(
        body, name="final_loss", grid=(t // tm,),
        in_specs=[row, vec, row],
        out_specs=[pl.BlockSpec((1, 128), lambda i: (0, 0)), row, vec],
        out_shape=[jax.ShapeDtypeStruct((1, 128), f32), jax.ShapeDtypeStruct((t, d), f32),
                   jax.ShapeDtypeStruct((1, d), f32)],
        compiler_params=_cp("arbitrary"),
    )(x, g, target)


def _load_ffn_weights(hbm_refs, vmem_refs, sems):
    @pl.when(pl.program_id(0) == 0)
    def _():
        copies = []
        for k, (src, dst) in enumerate(zip(hbm_refs, vmem_refs)):
            for j in range(N_DEV):
                half = pl.ds((j % 2) * FF_PAD, FF_PAD)
                window = dst.at[j // 2, half, :] if k == 2 else dst.at[j // 2, :, half]
                copies.append(pltpu.make_async_copy(src.at[j], window, sems.at[k * N_DEV + j]))
        for cp in copies:
            cp.start()
        for cp in copies:
            cp.wait()


def _ffn_weight_scratch(nj, d, ff):
    return [pltpu.VMEM((nj, d, ff), bf16), pltpu.VMEM((nj, d, ff), bf16), pltpu.VMEM((nj, ff, d), bf16),
            pltpu.SemaphoreType.DMA((3 * N_DEV,))]


def _ffn_fwd(name, x, g, w1, w3, w2, tm=512):
    t, d = x.shape
    nj, ff = N_DEV // 2, 2 * FF_PAD

    def body(x_ref, g_ref, w1_hbm, w3_hbm, w2_hbm, y_ref, a_ref, b_ref, w1_ref, w3_ref, w2_ref, sems):
        _load_ffn_weights((w1_hbm, w3_hbm, w2_hbm), (w1_ref, w3_ref, w2_ref), sems)
        xv = x_ref[...]
        h, _, _ = _rms_fwd(xv, g_ref[...])
        hb = h.astype(bf16)
        acc = jnp.zeros((tm, d), f32)
        for j in range(nj):
            a = _nn(hb, w1_ref[j])
            b = _nn(hb, w3_ref[j])
            a_ref[j] = a.astype(bf16)
            b_ref[j] = b.astype(bf16)
            u = (a * jax.nn.sigmoid(a) * b).astype(bf16)
            acc = acc + _nn(u, w2_ref[j])
        y_ref[...] = xv + 0.5 * acc

    row = pl.BlockSpec((tm, d), lambda i: (i, 0))
    mid = pl.BlockSpec((nj, tm, ff), lambda i: (0, i, 0))
    return pl.pallas_call(
        body, name=name, grid=(t // tm,),
        in_specs=[row, pl.BlockSpec((1, d), lambda i: (0, 0)), ANY_SPEC, ANY_SPEC, ANY_SPEC],
        out_specs=[row, mid, mid],
        out_shape=[jax.ShapeDtypeStruct((t, d), f32), jax.ShapeDtypeStruct((nj, t, ff), bf16),
                   jax.ShapeDtypeStruct((nj, t, ff), bf16)],
        scratch_shapes=_ffn_weight_scratch(nj, d, ff),
        compiler_params=_cp("arbitrary"),
    )(x, g, w1, w3, w2)


def _ffn_dx(name, dy, x, g, w1, w3, w2, a, b, tm=256):
    t, d = x.shape
    nj, ff = N_DEV // 2, 2 * FF_PAD

    def body(dy_ref, x_ref, g_ref, w1_hbm, w3_hbm, w2_hbm, a_ref, b_ref,
             dx_ref, dg_ref, hbt_ref, dyh_ref, ut_ref, da_ref, db_ref, w1_ref, w3_ref, w2_ref, sems):
        _load_ffn_weights((w1_hbm, w3_hbm, w2_hbm), (w1_ref, w3_ref, w2_ref), sems)
        gv = g_ref[...]
        h, xn, r = _rms_fwd(x_ref[...], gv)
        hbt_ref[...] = h.astype(bf16).T
        dyv = dy_ref[...]
        dyh = (0.5 * dyv).astype(bf16)
        dyh_ref[...] = dyh
        dh = jnp.zeros((tm, d), f32)
        dus = [_nt(dyh, w2_ref[j]) for j in range(nj)]
        for j in range(nj):
            av = a_ref[j].astype(f32)
            bv = b_ref[j].astype(f32)
            s = jax.nn.sigmoid(av)
            silu = av * s
            ut_ref[j] = (silu * bv).astype(bf16).T
            du = dus[j]
            dab = (du * bv * (s * (1.0 + av * (1.0 - s)))).astype(bf16)
            dbb = (du * silu).astype(bf16)
            da_ref[j] = dab
            db_ref[j] = dbb
            dh = dh + _nt(dab, w1_ref[j]) + _nt(dbb, w3_ref[j])
        dx, dg = _rms_bwd(dh, xn, r, gv)
        dx_ref[...] = dyv + dx

        @pl.when(pl.program_id(0) == 0)
        def _():
            dg_ref[...] = jnp.zeros_like(dg_ref)
        dg_ref[...] += dg

    row = pl.BlockSpec((tm, d), lambda i: (i, 0))
    vec = pl.BlockSpec((1, d), lambda i: (0, 0))
    mid = pl.BlockSpec((nj, tm, ff), lambda i: (0, i, 0))
    mid_shape = jax.ShapeDtypeStruct((nj, t, ff), bf16)
    return pl.pallas_call(
        body, name=name, grid=(t // tm,),
        in_specs=[row, row, vec, ANY_SPEC, ANY_SPEC, ANY_SPEC, mid, mid],
        out_specs=[row, vec, pl.BlockSpec((d, tm), lambda i: (0, i)), row,
                   pl.BlockSpec((nj, ff, tm), lambda i: (0, 0, i)), mid, mid],
        out_shape=[jax.ShapeDtypeStruct((t, d), f32), jax.ShapeDtypeStruct((1, d), f32),
                   jax.ShapeDtypeStruct((d, t), bf16), jax.ShapeDtypeStruct((t, d), bf16),
                   jax.ShapeDtypeStruct((nj, ff, t), bf16), mid_shape, mid_shape],
        scratch_shapes=_ffn_weight_scratch(nj, d, ff),
        compiler_params=_cp("arbitrary"),
    )(dy, x, g, w1, w3, w2, a, b)


def _ffn_dw_one(name, xt, y, buf, l, h, tt=2048):
    t = y.shape[-2]
    tt = min(tt, t)
    cut_cols = xt.ndim == 2

    def body(x_ref, y_ref, buf_ref, o_ref, ob_ref, acc):
        s = pl.program_id(1)
        prod = _nn(x_ref[0] if xt.ndim == 3 else x_ref[...], y_ref[0] if y.ndim == 3 else y_ref[...])

        @pl.when(s == 0)
        def _():
            acc[...] = prod

        @pl.when(s > 0)
        def _():
            acc[...] += prod

        @pl.when(s == pl.num_programs(1) - 1)
        def _():
            total = acc[...]
            for e in range(2):
                lo = e * FF_PAD
                part = total[:, lo:lo + FF_SHARD] if cut_cols else total[lo:lo + FF_SHARD, :]
                o_ref[e] = part
                ob_ref[e] = part.astype(bf16)

    x_spec = (pl.BlockSpec((1, xt.shape[1], tt), lambda p, s: (p, 0, s)) if xt.ndim == 3
              else pl.BlockSpec((xt.shape[0], tt), lambda p, s: (0, s)))
    y_spec = (pl.BlockSpec((1, tt, y.shape[2]), lambda p, s: (p, s, 0)) if y.ndim == 3
              else pl.BlockSpec((tt, y.shape[1]), lambda p, s: (s, 0)))
    k_, n_ = buf.shape[-2:]
    return pl.pallas_call(
        body, name=name, grid=(N_DEV // 2, t // tt),
        in_specs=[x_spec, y_spec, ANY_SPEC],
        out_specs=[pl.BlockSpec((2, None, None, k_, n_), lambda p, s: (p, l, h, 0, 0)),
                   pl.BlockSpec((2, k_, n_), lambda p, s: (p, 0, 0))],
        out_shape=[jax.ShapeDtypeStruct(buf.shape, buf.dtype), jax.ShapeDtypeStruct((N_DEV, k_, n_), bf16)],
        input_output_aliases={2: 0},
        scratch_shapes=[pltpu.VMEM((xt.shape[-2], y.shape[-1]), f32)],
        compiler_params=_cp("parallel", "arbitrary"),
    )(xt, y, buf)


_LOG_GAMMA = [float(np.log1p(-np.float32(2.0) ** np.float32(-5.0 - h))) for h in range(RET_HEADS)]


def _ret_consts(h):
    lg = jnp.where(h == 0, _LOG_GAMMA[0], jnp.where(h == 1, _LOG_GAMMA[1],
                   jnp.where(h == 2, _LOG_GAMMA[2], _LOG_GAMMA[3]))).astype(f32)
    c = RET_CHUNK
    r = lax.broadcasted_iota(jnp.int32, (c, c), 0)
    cc = lax.broadcasted_iota(jnp.int32, (c, c), 1)
    decay = jnp.where(r >= cc, jnp.exp(lg * jnp.maximum((r - cc).astype(f32), 0.0)), 0.0)
    pos = lax.broadcasted_iota(jnp.int32, (c, 1), 0).astype(f32)
    kd = jnp.exp(lg * (c - 1.0 - pos))
    qd = jnp.exp(lg * (pos + 1.0))
    gc = jnp.exp(lg * c)
    return decay, kd, qd, gc


def _rope(x, cos, sin):
    return x * cos + pltpu.roll(x, HEAD_DIM // 2, 1) * sin


def _rope_t(g, cos, sin):
    return g * cos + pltpu.roll(g * sin, HEAD_DIM // 2, 1)


def _rope_tables(s):
    half = HEAD_DIM // 2
    inv = ROPE_BASE ** (-jnp.arange(half, dtype=f32) / half)
    ang = jnp.arange(s, dtype=f32)[:, None] * inv[None, :]
    cos, sin = jnp.cos(ang), jnp.sin(ang)
    return jnp.concatenate([cos, cos], axis=1), jnp.concatenate([-sin, sin], axis=1)


def _head_ln(o):
    mu = jnp.mean(o, axis=-1, keepdims=True)
    oc = o - mu
    rs = lax.rsqrt(jnp.mean(oc * oc, axis=-1, keepdims=True) + EPS)
    return oc * rs, rs


def _ret_fwd(proj, cos, sin, ret_g, nb, s):
    c = RET_CHUNK
    nc = s // c
    t = nb * s
    scale = HEAD_DIM ** -0.5

    def body(q_ref, k_ref, v_ref, gate_ref, cos_ref, sin_ref, g_ref, o_ref, rprev_ref, m_ref):
        decay, kd, qd, gc = _ret_consts(pl.program_id(1))
        gv = g_ref[...]

        def step(n, rv):
            rows = pl.ds(pl.multiple_of(n * c, c), c)
            cs, sn = cos_ref[rows, :], sin_ref[rows, :]
            q = _rope(q_ref[rows, :], cs, sn)
            k = _rope(k_ref[rows, :], cs, sn) * scale
            vb = v_ref[rows, :].astype(bf16)
            sc = _nt(q.astype(bf16), k.astype(bf16)) * decay
            rprev_ref[n] = rv
            o = _nn(sc.astype(bf16), vb) + _nn((q * qd).astype(bf16), rv.astype(bf16))
            o_ref[rows, :] = o
            y, _ = _head_ln(o)
            gate = gate_ref[rows, :]
            m_ref[rows, :] = y * gv * (gate * jax.nn.sigmoid(gate))
            return rv * gc + _tn((k * kd).astype(bf16), vb)

        lax.fori_loop(0, nc, step, jnp.zeros((HEAD_DIM, HEAD_DIM), f32))

    def col(off):
        return pl.BlockSpec((s, HEAD_DIM), lambda b, h: (b, off + h))

    tab = pl.BlockSpec((s, HEAD_DIM), lambda b, h: (0, 0))
    return pl.pallas_call(
        body, name="ret_fwd", grid=(nb, RET_HEADS),
        in_specs=[col(0), col(4), col(8), col(12), tab, tab, pl.BlockSpec((1, HEAD_DIM), lambda b, h: (0, h))],
        out_specs=[col(0), pl.BlockSpec((nc, HEAD_DIM, HEAD_DIM), lambda b, h: (b * RET_HEADS + h, 0, 0)), col(0)],
        out_shape=[jax.ShapeDtypeStruct((t, RET_WIDTH), f32),
                   jax.ShapeDtypeStruct((nb * RET_HEADS * nc, HEAD_DIM, HEAD_DIM), f32),
                   jax.ShapeDtypeStruct((t, RET_WIDTH), f32)],
        compiler_params=_cp("parallel", "parallel"),
    )(proj, proj, proj, proj, cos, sin, ret_g)


def _ret_bwd(dmerged, o_raw, rprev, proj, cos, sin, ret_g, nb, s):
    c = RET_CHUNK
    nc = s // c
    t = nb * s
    scale = HEAD_DIM ** -0.5

    def body(dm_ref, o_ref, rprev_ref, q_ref, k_ref, v_ref, gate_ref, cos_ref, sin_ref, g_ref,
             dq_ref, dk_ref, dv_ref, dgate_ref, dg_ref):
        @pl.when(pl.program_id(1) == 0)
        def _():
            dg_ref[...] = jnp.zeros_like(dg_ref)
        decay, kd, qd, gc = _ret_consts(pl.program_id(0))
        gv = g_ref[...]

        def step(i, carry):
            drn, dg = carry
            n = nc - 1 - i
            rows = pl.ds(pl.multiple_of(n * c, c), c)
            cs, sn = cos_ref[rows, :], sin_ref[rows, :]
            q = _rope(q_ref[rows, :], cs, sn)
            k = _rope(k_ref[rows, :], cs, sn) * scale
            qb, kb = q.astype(bf16), k.astype(bf16)
            vb = v_ref[rows, :].astype(bf16)
            sc = _nt(qb, kb) * decay
            y, rs = _head_ln(o_ref[rows, :])
            gate = gate_ref[rows, :]
            sg = jax.nn.sigmoid(gate)
            silu = gate * sg
            dm = dm_ref[rows, :]
            dgate_ref[rows, :] = dm * y * gv * (sg * (1.0 + gate * (1.0 - sg)))
            dyl = dm * gv * silu
            dg = dg + jnp.sum(dm * y * silu, axis=0, keepdims=True)
            do = rs * (dyl - jnp.mean(dyl, axis=-1, keepdims=True) - y * jnp.mean(dyl * y, axis=-1, keepdims=True))
            dob = do.astype(bf16)
            rv = rprev_ref[n]
            drb = drn.astype(bf16)
            ds = (_nt(dob, vb) * decay).astype(bf16)
            kdb = (k * kd).astype(bf16)
            qdb = (q * qd).astype(bf16)
            dq_r = _nn(ds, kb) + _nt(dob, rv.astype(bf16)) * qd
            dk_r = _tn(ds, qb) + _nt(vb, drb) * kd
            dv_ref[rows, :] = _tn(sc.astype(bf16), dob) + _nn(kdb, drb)
            dq_ref[rows, :] = _rope_t(dq_r, cs, sn)
            dk_ref[rows, :] = _rope_t(dk_r * scale, cs, sn)
            return drn * gc + _tn(qdb, dob), dg

        _, dg = lax.fori_loop(0, nc, step, (jnp.zeros((HEAD_DIM, HEAD_DIM), f32), jnp.zeros((1, HEAD_DIM), f32)))
        dg_ref[...] += dg

    def col(off):
        return pl.BlockSpec((s, HEAD_DIM), lambda h, b: (b, off + h))

    tab = pl.BlockSpec((s, HEAD_DIM), lambda h, b: (0, 0))
    gsp = pl.BlockSpec((1, HEAD_DIM), lambda h, b: (0, h))
    out_t = jax.ShapeDtypeStruct((t, RET_WIDTH), f32)
    return pl.pallas_call(
        body, name="ret_bwd", grid=(RET_HEADS, nb),
        in_specs=[col(0), col(0), pl.BlockSpec((nc, HEAD_DIM, HEAD_DIM), lambda h, b: (b * RET_HEADS + h, 0, 0)),
                  col(0), col(4), col(8), col(12), tab, tab, gsp],
        out_specs=[col(0), col(0), col(0), col(0), gsp],
        out_shape=[out_t, out_t, out_t, out_t, jax.ShapeDtypeStruct((1, RET_WIDTH), f32)],
        compiler_params=_cp("parallel", "arbitrary"),
    )(dmerged, o_raw, rprev, proj, proj, proj, proj, cos, sin, ret_g)


def _neg_expm1(z):
    series = -(z * (1.0 + z * (0.5 + z * (1.0 / 6.0 + z * (1.0 / 24.0)))))
    return jnp.where(z > -0.01, series, 1.0 - jnp.exp(z))


def _lru_gates(xc, pa, pi, lam):
    r = jax.nn.sigmoid(pa)
    i = jax.nn.sigmoid(pi)
    log_a = -LRU_C * r * jax.nn.softplus(-lam)
    a = jnp.exp(log_a)
    bx = jnp.sqrt(_neg_expm1(2.0 * log_a)) * i * xc
    return a, bx


def _scan_rows(a, b, row, up):
    sub = row[:SUBLANES] & (SUBLANES - 1)
    groups = list(range(a.shape[0] // SUBLANES))
    out = [None] * len(groups)
    edge = slice(0, 1) if up else slice(SUBLANES - 1, SUBLANES)
    carry = jnp.zeros((1, a.shape[1]), f32)
    for g in (reversed(groups) if up else groups):
        rows = slice(g * SUBLANES, (g + 1) * SUBLANES)
        xa, xb = a[rows], b[rows]
        d = 1
        while d < SUBLANES:
            keep = (sub < SUBLANES - d) if up else (sub >= d)
            shift = SUBLANES - d if up else d
            xb = xa * jnp.where(keep, pltpu.roll(xb, shift, 0), 0.0) + xb
            xa = xa * jnp.where(keep, pltpu.roll(xa, shift, 0), 1.0)
            d *= 2
        out[g] = xb + xa * carry
        carry = out[g][edge]
    return jnp.concatenate(out, axis=0)


def _scan_fwd(a, b, row):
    return _scan_rows(a, b, row, False)


def _scan_bwd(c, b, row):
    return _scan_rows(c, b, row, True)


def _conv_fwd(x, cw, cb, row):
    return (cb + cw[3:4] * x + cw[2:3] * _shift_dn(x, 1, row) + cw[1:2] * _shift_dn(x, 2, row)
            + cw[0:1] * _shift_dn(x, 3, row))


def _lru_specs(s, order):
    def im(f):
        return (lambda b, g: f(b, g)) if order == "bg" else (lambda g, b: f(b, g))
    seq = lambda off: pl.BlockSpec((s, 128), im(lambda b, g: (b, off + g)))
    vec = pl.BlockSpec((1, 128), im(lambda b, g: (0, g)))
    cw = pl.BlockSpec((4, 128), im(lambda b, g: (0, g)))
    mat = pl.BlockSpec((1, 128, 128), im(lambda b, g: (g, 0, 0)))
    return seq, vec, cw, mat


def _lru_fwd(proj, conv_w, conv_b, w_a, b_a, w_i, b_i, lam, nb, s):
    def body(x_ref, gt_ref, cw_ref, cb_ref, wa_ref, ba_ref, wi_ref, bi_ref, lam_ref, out_ref):
        row = lax.broadcasted_iota(jnp.int32, (s, 128), 0)
        xc = _conv_fwd(x_ref[...], cw_ref[...], cb_ref[...], row)
        xcb = xc.astype(bf16)
        pa = _nn(xcb, wa_ref[0].astype(bf16)) + ba_ref[...]
        pi = _nn(xcb, wi_ref[0].astype(bf16)) + bi_ref[...]
        a, bx = _lru_gates(xc, pa, pi, lam_ref[...])
        h = _scan_fwd(a, bx, row)
        out_ref[...] = h * jax.nn.gelu(gt_ref[...])

    seq, vec, cw, mat = _lru_specs(s, "bg")
    return pl.pallas_call(
        body, name="lru_fwd", grid=(nb, LRU_BLOCKS),
        in_specs=[seq(16), seq(20), cw, vec, mat, vec, mat, vec, vec],
        out_specs=seq(0),
        out_shape=jax.ShapeDtypeStruct((nb * s, LRU_WIDTH), f32),
        compiler_params=_cp("parallel", "parallel"),
    )(proj, proj, conv_w, conv_b, w_a, b_a, w_i, b_i, lam)


def _lru_bwd(dmerged, proj, conv_w, conv_b, w_a, b_a, w_i, b_i, lam, nb, s):
    def body(dout_ref, x_ref, gt_ref, cw_ref, cb_ref, wa_ref, ba_ref, wi_ref, bi_ref, lam_ref,
             dx_ref, dgt_ref, dcw_ref, dcb_ref, dwa_ref, dba_ref, dwi_ref, dbi_ref, dlam_ref):
        row = lax.broadcasted_iota(jnp.int32, (s, 128), 0)
        x = x_ref[...]
        cwv = cw_ref[...]
        xc = _conv_fwd(x, cwv, cb_ref[...], row)
        xcb = xc.astype(bf16)
        wab, wib = wa_ref[0].astype(bf16), wi_ref[0].astype(bf16)
        pa = _nn(xcb, wab) + ba_ref[...]
        pi = _nn(xcb, wib) + bi_ref[...]
        (a, bx), gates_vjp = jax.vjp(_lru_gates, xc, pa, pi, lam_ref[...])
        h = _scan_fwd(a, bx, row)
        ge, gelu_vjp = jax.vjp(jax.nn.gelu, gt_ref[...])
        dout = dout_ref[...]
        dgt_ref[...] = gelu_vjp(dout * h)[0]
        adj = _scan_bwd(_shift_up(a, 1, row), dout * ge, row)
        dxc, dpa, dpi, dlam = gates_vjp((adj * _shift_dn(h, 1, row), adj))
        dpab, dpib = dpa.astype(bf16), dpi.astype(bf16)
        dxc = dxc + _nt(dpab, wab) + _nt(dpib, wib)
        dx_ref[...] = (cwv[3:4] * dxc + cwv[2:3] * _shift_up(dxc, 1, row) + cwv[1:2] * _shift_up(dxc, 2, row)
                       + cwv[0:1] * _shift_up(dxc, 3, row))

        @pl.when(pl.program_id(1) == 0)
        def _():
            for r in (dcw_ref, dcb_ref, dwa_ref, dba_ref, dwi_ref, dbi_ref, dlam_ref):
                r[...] = jnp.zeros_like(r)
        rsum = lambda v: jnp.sum(v, axis=0, keepdims=True)
        dcw_ref[...] += jnp.concatenate([rsum(dxc * _shift_dn(x, 3, row)), rsum(dxc * _shift_dn(x, 2, row)),
                                         rsum(dxc * _shift_dn(x, 1, row)), rsum(dxc * x)], axis=0)
        dcb_ref[...] += rsum(dxc)
        dwa_ref[0] += _tn(xcb, dpab)
        dwi_ref[0] += _tn(xcb, dpib)
        dba_ref[...] += rsum(dpa)
        dbi_ref[...] += rsum(dpi)
        dlam_ref[...] += dlam

    seq, vec, cw, mat = _lru_specs(s, "gb")
    t = nb * s
    vshape = jax.ShapeDtypeStruct((1, LRU_WIDTH), f32)
    mshape = jax.ShapeDtypeStruct((LRU_BLOCKS, 128, 128), f32)
    return pl.pallas_call(
        body, name="lru_bwd", grid=(LRU_BLOCKS, nb),
        in_specs=[seq(4), seq(16), seq(20), cw, vec, mat, vec, mat, vec, vec],
        out_specs=[seq(0), seq(0), cw, vec, mat, vec, mat, vec, vec],
        out_shape=[jax.ShapeDtypeStruct((t, LRU_WIDTH), f32), jax.ShapeDtypeStruct((t, LRU_WIDTH), f32),
                   jax.ShapeDtypeStruct((4, LRU_WIDTH), f32), vshape, mshape, vshape, mshape, vshape, vshape],
        compiler_params=_cp("parallel", "arbitrary"),
    )(dmerged, proj, proj, conv_w, conv_b, w_a, b_a, w_i, b_i, lam)


def _s5_disc(lr, li, ldt, bre, bim):
    dt = jnp.exp(ldt)
    mag = jnp.exp(lr * dt)
    lbr = mag * jnp.cos(li * dt)
    lbi = mag * jnp.sin(li * dt)
    den = lr * lr + li * li
    nr = lbr - 1.0
    fr = (nr * lr + lbi * li) / den
    fi = (lbi * lr - nr * li) / den
    bbr = fr[:, None, :] * bre - fi[:, None, :] * bim
    bbi = fr[:, None, :] * bim + fi[:, None, :] * bre
    return lbr, lbi, bbr, bbi


def _s5_prep(lr, li, ldt, bre, bim):
    def body(lr_ref, li_ref, ldt_ref, bre_ref, bim_ref, o1, o2, o3, o4):
        o1[...], o2[...], o3[...], o4[...] = _s5_disc(lr_ref[...], li_ref[...], ldt_ref[...], bre_ref[...], bim_ref[...])

    return pl.pallas_call(
        body, name="s5_prep", in_specs=[VMEM_SPEC] * 5, out_specs=[VMEM_SPEC] * 4,
        out_shape=[jax.ShapeDtypeStruct(lr.shape, f32), jax.ShapeDtypeStruct(lr.shape, f32),
                   jax.ShapeDtypeStruct(bre.shape, f32), jax.ShapeDtypeStruct(bre.shape, f32)],
    )(lr, li, ldt, bre, bim)


def _s5_prep_bwd(lr, li, ldt, bre, bim, cts):
    def body(lr_ref, li_ref, ldt_ref, bre_ref, bim_ref, g1, g2, g3, g4, o1, o2, o3, o4, o5):
        _, vjp = jax.vjp(_s5_disc, lr_ref[...], li_ref[...], ldt_ref[...], bre_ref[...], bim_ref[...])
        o1[...], o2[...], o3[...], o4[...], o5[...] = vjp((g1[...], g2[...], g3[...], g4[...]))

    return pl.pallas_call(
        body, name="s5_prep_bwd", in_specs=[VMEM_SPEC] * 9, out_specs=[VMEM_SPEC] * 5,
        out_shape=[jax.ShapeDtypeStruct(v.shape, f32) for v in (lr, li, ldt, bre, bim)],
    )(lr, li, ldt, bre, bim, *cts)


def _cmul(ar, ai, br, bi):
    return ar * br - ai * bi, ar * bi + ai * br


def _s5_pow_table(lr, li, n, row, up):
    ar = jnp.broadcast_to(lr, (n, lr.shape[1]))
    ai = jnp.broadcast_to(li, (n, li.shape[1]))
    shift = _shift_up if up else _shift_dn
    d = 1
    while d < n:
        ar, ai = _cmul(ar, ai, shift(ar, d, row, 1.0), shift(ai, d, row, 0.0))
        d *= 2
    return ar, ai


def _s5_step_factors(lr, li, row, up):
    sub = row & (SUBLANES - 1)
    out, pr, pi, d = [], lr, li, 1
    while d < SUBLANES:
        keep = (sub < SUBLANES - d) if up else (sub >= d)
        out.append((jnp.where(keep, pr, 0.0), jnp.where(keep, pi, 0.0)))
        pr, pi = _cmul(pr, pi, pr, pi)
        d *= 2
    return out


def _s5_scan(br, bi, steps, tab_r, tab_i, cr, ci, up):
    groups = list(range(br.shape[0] // SUBLANES))
    out_r, out_i = [None] * len(groups), [None] * len(groups)
    edge = slice(0, 1) if up else slice(SUBLANES - 1, SUBLANES)
    for g in (reversed(groups) if up else groups):
        rows = slice(g * SUBLANES, (g + 1) * SUBLANES)
        xr, xi = br[rows], bi[rows]
        for k, (mr, mi) in enumerate(steps):
            shift = SUBLANES - (1 << k) if up else 1 << k
            tr, ti = _cmul(mr, mi, pltpu.roll(xr, shift, 0), pltpu.roll(xi, shift, 0))
            xr, xi = xr + tr, xi + ti
        tr, ti = _cmul(tab_r, tab_i, cr, ci)
        hr, hi = xr + tr, xi + ti
        out_r[g], out_i[g] = hr, hi
        cr, ci = hr[edge], hi[edge]
    return jnp.concatenate(out_r, axis=0), jnp.concatenate(out_i, axis=0)


def _s5_specs(t, nb, nc):
    seq = pl.BlockSpec((t, 128), lambda k: (0, k))
    lvec = pl.BlockSpec((1, S5_BLOCK_STATES), lambda k: (0, k))
    dvec = pl.BlockSpec((1, 128), lambda k: (0, k))
    wmat = pl.BlockSpec((1, 128, S5_BLOCK_STATES), lambda k: (k, 0, 0))
    h0 = pl.BlockSpec((nb, None, nc, 2, S5_BLOCK_STATES), lambda k: (0, k, 0, 0, 0))
    return seq, lvec, dvec, wmat, h0


def _s5_fwd(u, lbr, lbi, wbr, wbi, wcr, wci, dskip, nb, s):
    ln = S5_CHUNK
    nc = s // ln

    def body(u_ref, lr_ref, li_ref, wbr_ref, wbi_ref, wcr_ref, wci_ref, d_ref, yg_ref, y_ref, h0_ref):
        row = lax.broadcasted_iota(jnp.int32, (ln, S5_BLOCK_STATES), 0)
        lr, li = lr_ref[...], li_ref[...]
        pr, pi = _s5_pow_table(lr, li, SUBLANES, row[:SUBLANES], False)
        steps = _s5_step_factors(lr, li, row[:SUBLANES], False)
        dv = d_ref[...]

        def chunk(b, n, h0r, h0i):
            st = pl.multiple_of(b * s + n * ln, ln)
            uc = u_ref[pl.ds(st, ln), :]
            ub = uc.astype(bf16)
            hr, hi = _s5_scan(_nn(ub, wbr_ref[0]), _nn(ub, wbi_ref[0]), steps, pr, pi, h0r, h0i, False)
            h0_ref[b, n, 0:1, :] = h0r
            h0_ref[b, n, 1:2, :] = h0i
            y = _nt(hr.astype(bf16), wcr_ref[0]) - _nt(hi.astype(bf16), wci_ref[0]) + dv * uc
            y_ref[pl.ds(st, ln), :] = y
            yg_ref[pl.ds(st, ln), :] = jax.nn.gelu(y).astype(bf16)
            return hr[ln - 1:ln, :], hi[ln - 1:ln, :]

        def step(n, carry):
            return tuple(chunk(b, n, *carry[b]) for b in range(nb))

        z = jnp.zeros((1, S5_BLOCK_STATES), f32)
        lax.fori_loop(0, nc, step, ((z, z),) * nb)

    t = nb * s
    seq, lvec, dvec, wmat, h0 = _s5_specs(t, nb, nc)
    return pl.pallas_call(
        body, name="s5_fwd", grid=(S5_BLOCKS,),
        in_specs=[seq, lvec, lvec, wmat, wmat, wmat, wmat, dvec],
        out_specs=[seq, seq, h0],
        out_shape=[jax.ShapeDtypeStruct((t, D_MODEL), bf16), jax.ShapeDtypeStruct((t, D_MODEL), f32),
                   jax.ShapeDtypeStruct((nb, S5_BLOCKS, nc, 2, S5_BLOCK_STATES), f32)],
        compiler_params=_cp("parallel"),
    )(u, lbr, lbi, wbr, wbi, wcr, wci, dskip)


def _s5_bwd(dyg, y, u, h0, lbr, lbi, wbr, wbi, wcr, wci, dskip, nb, s):
    ln = S5_CHUNK
    nc = s // ln

    def body(dyg_ref, y_ref, u_ref, h0_ref, lr_ref, li_ref, wbr_ref, wbi_ref, wcr_ref, wci_ref, d_ref,
             du_ref, dlr_ref, dli_ref, dwbr_ref, dwbi_ref, dwcr_ref, dwci_ref, dd_ref):
        for r in (dlr_ref, dli_ref, dwbr_ref, dwbi_ref, dwcr_ref, dwci_ref, dd_ref):
            r[...] = jnp.zeros_like(r)
        row = lax.broadcasted_iota(jnp.int32, (ln, S5_BLOCK_STATES), 0)
        lr, li = lr_ref[...], li_ref[...]
        pr, pi = _s5_pow_table(lr, li, SUBLANES, row[:SUBLANES], False)
        qr, qi = _s5_pow_table(lr, -li, SUBLANES, row[:SUBLANES], True)
        row8 = row[:SUBLANES]
        steps_dn, steps_up = _s5_step_factors(lr, li, row8, False), _s5_step_factors(lr, -li, row8, True)
        dv = d_ref[...]
        rsum = lambda v: jnp.sum(v, axis=0, keepdims=True)

        def chunk(b, n, gnr, gni):
            st = pl.multiple_of(b * s + n * ln, ln)
            uc = u_ref[pl.ds(st, ln), :]
            ub = uc.astype(bf16)
            h0v = h0_ref[b, n]
            h0r, h0i = h0v[0:1], h0v[1:2]
            hr, hi = _s5_scan(_nn(ub, wbr_ref[0]), _nn(ub, wbi_ref[0]), steps_dn, pr, pi, h0r, h0i, False)
            dy = jax.vjp(jax.nn.gelu, y_ref[pl.ds(st, ln), :])[1](dyg_ref[pl.ds(st, ln), :])[0]
            dyb = dy.astype(bf16)
            dd_ref[...] += rsum(dy * uc)
            gr, gi = _s5_scan(_nn(dyb, wcr_ref[0]), -_nn(dyb, wci_ref[0]), steps_up, qr, qi, gnr, gni, True)
            hpr = jnp.where(row >= 1, pltpu.roll(hr, 1, 0), h0r)
            hpi = jnp.where(row >= 1, pltpu.roll(hi, 1, 0), h0i)
            dlr_ref[...] += rsum(gr * hpr + gi * hpi)
            dli_ref[...] += rsum(gi * hpr - gr * hpi)
            grb, gib = gr.astype(bf16), gi.astype(bf16)
            dwbr_ref[0] += _tn(ub, grb)
            dwbi_ref[0] += _tn(ub, gib)
            dwcr_ref[0] += _tn(dyb, hr.astype(bf16))
            dwci_ref[0] -= _tn(dyb, hi.astype(bf16))
            du_ref[pl.ds(st, ln), :] = _nt(grb, wbr_ref[0]) + _nt(gib, wbi_ref[0]) + dv * dy
            return gr[0:1, :], gi[0:1, :]

        def step(i, carry):
            return tuple(chunk(b, nc - 1 - i, *carry[b]) for b in range(nb))

        z = jnp.zeros((1, S5_BLOCK_STATES), f32)
        lax.fori_loop(0, nc, step, ((z, z),) * nb)

    t = nb * s
    seq, lvec, dvec, wmat, h0s = _s5_specs(t, nb, nc)
    lshape = jax.ShapeDtypeStruct((1, S5_BLOCKS * S5_BLOCK_STATES), f32)
    wshape = jax.ShapeDtypeStruct((S5_BLOCKS, 128, S5_BLOCK_STATES), f32)
    return pl.pallas_call(
        body, name="s5_bwd", grid=(S5_BLOCKS,),
        in_specs=[seq, seq, seq, h0s, lvec, lvec, wmat, wmat, wmat, wmat, dvec],
        out_specs=[seq, lvec, lvec, wmat, wmat, wmat, wmat, dvec],
        out_shape=[jax.ShapeDtypeStruct((t, D_MODEL), f32), lshape, lshape, wshape, wshape, wshape, wshape,
                   jax.ShapeDtypeStruct((1, D_MODEL), f32)],
        compiler_params=_cp("parallel"),
    )(dyg, y, u, h0, lbr, lbi, wbr, wbi, wcr, wci, dskip)


def _blockdiag(w):
    w4 = w.reshape(S5_BLOCKS, 8, S5_GROUP, S5_STATE)
    same_group = jnp.eye(8, dtype=bool)[None, :, None, :, None]
    return jnp.where(same_group, w4[:, :, :, None, :], 0.0).reshape(S5_BLOCKS, 128, S5_BLOCK_STATES)


def _blockdiag_t(dw):
    d5 = dw.reshape(S5_BLOCKS, 8, S5_GROUP, 8, S5_STATE)
    diag = jnp.diagonal(d5, axis1=1, axis2=3)
    return jnp.moveaxis(diag, 3, 1).reshape(S5_GROUPS, S5_GROUP, S5_STATE)


def _glu_fwd(ygb, wa, wb, x, tm=512, tn=1024):
    t, d = x.shape

    def body(y_ref, wa_ref, wb_ref, x_ref, o_ref, p_ref, q_ref):
        p = _nn(y_ref[...], wa_ref[...])
        q = _nn(y_ref[...], wb_ref[...])
        p_ref[...] = p
        q_ref[...] = q
        o_ref[...] = x_ref[...] + p * jax.nn.sigmoid(q)

    tile = pl.BlockSpec((tm, tn), lambda i, j: (i, j))
    wsp = pl.BlockSpec((d, tn), lambda i, j: (0, j))
    out = jax.ShapeDtypeStruct((t, d), f32)
    return pl.pallas_call(
        body, name="glu_fwd", grid=(t // tm, d // tn),
        in_specs=[pl.BlockSpec((tm, d), lambda i, j: (i, 0)), wsp, wsp, tile],
        out_specs=[tile, tile, tile], out_shape=[out, out, out],
        compiler_params=_cp("parallel", "parallel"),
    )(ygb, wa, wb, x)


def _place():
    x, y, c = lax.axis_index("x"), lax.axis_index("y"), lax.axis_index("c")
    return x, y, c, [(1 - x, y), (x, 1 - y), (1 - x, 1 - y)]


def _all_gather(name, arrays):
    n = len(arrays)

    def body(*refs):
        ins, outs = refs[:n], refs[n:2 * n]
        send_sems, recv_sems, local_sems = refs[2 * n:]
        x, y, c, chips = _place()
        me, sib = (x, y, c), (x, y, 1 - c)

        def copy(i, k, block, to, src=None):
            dst = outs[i].at[4 * block[0] + 2 * block[1] + block[2]]
            return pltpu.make_async_remote_copy(
                src_ref=dst if src is None else src, dst_ref=dst,
                send_sem=send_sems.at[i * 7 + k], recv_sem=recv_sems.at[i * 7 + k],
                device_id=to, device_id_type=MESH)

        mine = [pltpu.make_async_copy(ins[i], outs[i].at[4 * x + 2 * y + c], local_sems.at[i]) for i in range(n)]
        for m in mine:
            m.start()
        first = []
        for i in range(n):
            first.append(copy(i, 0, me, sib, src=ins[i]))
            first += [copy(i, 1 + j, me, (*chip, c), src=ins[i]) for j, chip in enumerate(chips)]
        for cp in first:
            cp.start()
        passed = []
        for j, chip in enumerate(chips):
            for i in range(n):
                copy(i, 1 + j, (*chip, c), me).wait_recv()
                fwd = copy(i, 4 + j, (*chip, c), sib)
                fwd.start()
                passed.append(fwd)
        for i in range(n):
            copy(i, 0, sib, me).wait_recv()
        for j, chip in enumerate(chips):
            for i in range(n):
                copy(i, 4 + j, (*chip, 1 - c), me).wait_recv()
        for cp in first + passed:
            cp.wait_send()
        for m in mine:
            m.wait()

    return pl.pallas_call(
        body, name=name,
        in_specs=[ANY_SPEC] * n, out_specs=[ANY_SPEC] * n,
        out_shape=[jax.ShapeDtypeStruct((N_DEV,) + a.shape, a.dtype) for a in arrays],
        scratch_shapes=[pltpu.SemaphoreType.DMA((7 * n,)), pltpu.SemaphoreType.DMA((7 * n,)),
                        pltpu.SemaphoreType.DMA((n,))],
    )(*arrays)


def _tie(name, x, deps):
    def body(*refs):
        pass

    return pl.pallas_call(
        body, name=name, in_specs=[ANY_SPEC] * (1 + len(deps)), out_specs=ANY_SPEC,
        out_shape=jax.ShapeDtypeStruct(x.shape, x.dtype), input_output_aliases={0: 0},
    )(x, *deps)


def _xchg_copies(kind, srcs, lands, suffixes, send_sems, recv_sems):
    x, y, c, _ = _place()
    copies = []
    for i, (src, land, sfx) in enumerate(zip(srcs, lands, suffixes)):
        for k in range(N_DEV - 1):
            r = k + 1
            peer = (1 - x if r & 4 else x, 1 - y if r & 2 else y, 1 - c if r & 1 else c)
            if kind == "gather":
                s_ref, d_ref = src, land.at[(4 * x + 2 * y + c,) + sfx]
            else:
                s_ref, d_ref = src.at[4 * peer[0] + 2 * peer[1] + peer[2]], land.at[(k,) + sfx]
            copies.append(pltpu.make_async_remote_copy(
                src_ref=s_ref, dst_ref=d_ref, send_sem=send_sems.at[i * 7 + k], recv_sem=recv_sems.at[i * 7 + k],
                device_id=peer, device_id_type=MESH))
    return copies


def _xchg_start(name, kind, srcs, lands, suffixes=None):
    n = len(srcs)
    suffixes = suffixes or [()] * n

    def body(*refs):
        src, land = refs[:n], refs[n:2 * n]
        send_sems, recv_sems, token = refs[2 * n], refs[2 * n + 1], refs[-1]
        for cp in _xchg_copies(kind, src, land, suffixes, send_sems, recv_sems):
            cp.start()
        token[...] = jnp.zeros_like(token)

    arrays = list(srcs) + list(lands)
    outs = pl.pallas_call(
        body, name=name,
        out_shape=(pltpu.SemaphoreType.DMA((7 * n,)), pltpu.SemaphoreType.DMA((7 * n,)),
                   *[pltpu.HBM(a.shape, a.dtype) for a in arrays], jax.ShapeDtypeStruct((8, 128), f32)),
        in_specs=[HBM_SPEC] * (2 * n),
        out_specs=(SEM_SPEC, SEM_SPEC, *[HBM_SPEC] * (2 * n), VMEM_SPEC),
        input_output_aliases={i: 2 + i for i in range(2 * n)},
        compiler_params=pltpu.CompilerParams(has_side_effects=SIDE_EFFECT),
    )(*[pltpu.with_memory_space_constraint(a, pltpu.HBM) for a in arrays])
    return dict(kind=kind, n=n, suffixes=suffixes, send=outs[0], recv=outs[1], srcs=list(outs[2:2 + n]),
                lands=list(outs[2 + n:2 + 2 * n]), token=outs[-1])


def _xchg_wait(name, h, after, lands=None):
    n = h["n"]
    lands = h["lands"] if lands is None else lands

    def body(*refs):
        src, land = refs[:n], refs[n:2 * n]
        for cp in _xchg_copies(h["kind"], src, land, h["suffixes"], refs[2 * n], refs[2 * n + 1]):
            cp.wait_send()
            cp.wait_recv()

    arrays = h["srcs"] + list(lands)
    outs = pl.pallas_call(
        body, name=name,
        out_shape=tuple(pltpu.HBM(a.shape, a.dtype) for a in arrays),
        in_specs=[HBM_SPEC] * (2 * n) + [SEM_SPEC, SEM_SPEC] + [ANY_SPEC] * len(after),
        out_specs=tuple([HBM_SPEC] * (2 * n)),
        input_output_aliases={i: i for i in range(2 * n)},
        compiler_params=pltpu.CompilerParams(has_side_effects=SIDE_EFFECT),
    )(*arrays, h["send"], h["recv"], *after)
    return list(outs[n:])


def _rows(a):
    return a.reshape(-1, a.shape[-1])


def _row_tile(r):
    for tm in (512, 256, 128, 64, 32, 16, 8):
        if r % tm == 0:
            return tm
    return r


def _sum8(name, gathered):
    _, r, n = gathered.shape
    tm = _row_tile(r)

    def body(g_ref, o_ref):
        acc = g_ref[0]
        for k in range(1, N_DEV):
            acc = acc + g_ref[k]
        o_ref[...] = acc

    return pl.pallas_call(
        body, name=name, grid=(r // tm,),
        in_specs=[pl.BlockSpec((N_DEV, tm, n), lambda i: (0, i, 0))],
        out_specs=pl.BlockSpec((tm, n), lambda i: (i, 0)),
        out_shape=jax.ShapeDtypeStruct((r, n), f32),
        compiler_params=_cp("parallel"),
    )(gathered)


def _adamw(name, w, m, v, own, landed=None, slot=None):
    shape = w.shape
    w2, m2, v2 = _rows(w), _rows(m), _rows(v)
    r, n = w2.shape
    tm = _row_tile(r)
    c1 = 1.0 - ADAM_B1 ** ADAM_STEP
    c2 = 1.0 - ADAM_B2 ** ADAM_STEP
    extra = [] if landed is None else [landed.reshape(landed.shape[0], r, n)]
    row = pl.BlockSpec((tm, n), lambda i, *_: (i, 0))
    if slot is None:
        o2, own_spec, scalars = _rows(own), row, []
    else:
        dev, kind = slot
        scalars = [dev.reshape(1).astype(jnp.int32)]
        if kind == "lead":
            o2, own_spec = own.reshape(N_DEV, r, n), pl.BlockSpec((None, tm, n), lambda i, d: (d[0], i, 0))
        elif kind == "rows":
            o2, own_spec = own, pl.BlockSpec((tm, n), lambda i, d: (d[0] * (r // tm) + i, 0))
        else:
            o2, own_spec = own, pl.BlockSpec((tm, n), lambda i, d: (i, d[0]))

    def body(*refs):
        w_ref, m_ref, v_ref, o_ref = refs[len(scalars):len(scalars) + 4]
        refs = refs[len(scalars) + 4:]
        g = o_ref[...]
        if extra:
            for k in range(extra[0].shape[0]):
                g = g + refs[0][k].astype(f32)
        g_ref, d_ref, mn_ref, vn_ref = refs[len(extra):]
        mn = ADAM_B1 * m_ref[...] + (1.0 - ADAM_B1) * g
        vn = ADAM_B2 * v_ref[...] + (1.0 - ADAM_B2) * (g * g)
        g_ref[...] = g
        d_ref[...] = -ADAM_LR * ((mn / c1) / (jnp.sqrt(vn / c2) + ADAM_EPS) + ADAM_WD * w_ref[...])
        mn_ref[...] = mn
        vn_ref[...] = vn

    outs = pl.pallas_call(
        body, name=name,
        grid_spec=pltpu.PrefetchScalarGridSpec(
            num_scalar_prefetch=len(scalars), grid=(r // tm,),
            in_specs=[row] * 3 + [own_spec] + [pl.BlockSpec((e.shape[0], tm, n), lambda i, *_: (0, i, 0)) for e in extra],
            out_specs=[row] * 4),
        out_shape=[jax.ShapeDtypeStruct((r, n), f32)] * 4,
        compiler_params=_cp("parallel"),
    )(*scalars, w2, m2, v2, o2, *extra)
    return [o.reshape(shape) for o in outs]


def _pack(arrays):
    flat = jnp.concatenate([a.reshape(-1).astype(f32) for a in arrays])
    pad = (-flat.shape[0]) % (128 * (512 if flat.shape[0] > 128 * 512 else 8))
    return jnp.pad(flat, (0, pad)).reshape(-1, 128)


def _unpack(packed, shapes):
    flat = packed.reshape(-1)
    out, off = [], 0
    for s in shapes:
        n = math.prod(s)
        out.append(flat[off:off + n].reshape(s))
        off += n
    return out


def _local_step(x, target, w, weights_of, send, last_small, nb, s):
    cos, sin = _rope_tables(s)
    g = {}
    ffn_saved = {}
    ffn_bufs = [lax.empty((N_DEV, 2, 2) + shp, f32)
                for shp in ((D_MODEL, FF_SHARD), (D_MODEL, FF_SHARD), (FF_SHARD, D_MODEL))]

    def ffn(xin, l, h, wts):
        y, a, b = _ffn_fwd(f"ffn_fwd_{l}{h}", xin, w["ffn_g"][l][h], *wts)
        ffn_saved[(l, h)] = (xin, a, b, wts)
        return y

    def ffn_back(dy, l, h):
        xin, a, b, wts = ffn_saved[(l, h)]
        dx, dg, hb, dyh, u, da, db = _ffn_dx(f"ffn_dx_{l}{h}", dy, xin, w["ffn_g"][l][h], *wts, a, b)
        g[f"ffn_g_{l}{h}"] = dg
        if (l, h) == (0, 0):
            hb = last_small(g, hb)
        ffn_bufs[0], half = _ffn_dw_one(f"ffn_dw_{l}{h}_w1", hb, da, ffn_bufs[0], l, h)
        hb = send(f"ffn_{l}{h}_w1", {"ffn_w1": half}, hb)
        ffn_bufs[1], half = _ffn_dw_one(f"ffn_dw_{l}{h}_w3", hb, db, ffn_bufs[1], l, h)
        u = send(f"ffn_{l}{h}_w3", {"ffn_w3": half}, u)
        ffn_bufs[2], half = _ffn_dw_one(f"ffn_dw_{l}{h}_w2", u, dyh, ffn_bufs[2], l, h)
        return send(f"ffn_{l}{h}_w2", {"ffn_w2": half}, dx)

    def slots(t):
        return t.reshape(N_DEV, D_MODEL // N_DEV, D_MODEL)

    x1 = ffn(x, 0, 0, weights_of(0, [])["ffn"])
    wg = weights_of(1, [x1])
    w_in, w_out = wg["w_in"], wg["w_out"]
    _, h0b = _norm_fwd("mix_norm_0", x1, w["mix_g"][0])
    proj = _mm("in_proj", h0b, w_in, "nn", tn=1536)[0]
    o_raw, rprev, mret = _ret_fwd(proj, cos, sin, w["ret_g"], nb, s)
    lru = _lru_fwd(proj, w["conv_w"], w["conv_b"], w["lru_w_a"], w["lru_b_a"], w["lru_w_i"], w["lru_b_i"], w["lru_lam"], nb, s)
    merged = _ew("merge", lambda a, b: (jnp.concatenate([a, b], axis=1),), [mret, lru], [(D_MODEL, bf16)])[0]
    x2 = _mm("out_proj", merged, w_out, "nn", extras=[x1], epilogue=lambda acc, r: (acc + r,))[0]
    x3 = ffn(x2, 0, 1, weights_of(2, [x2])["ffn"])
    wg = weights_of(3, [x3])
    glu_a, glu_b = wg["glu_a"], wg["glu_b"]
    x4 = ffn(x3, 1, 0, wg["ffn"])
    u, _ = _norm_fwd("mix_norm_1", x4, w["mix_g"][1])
    lbr, lbi, bbr, bbi = _s5_prep(w["s5_lr"], w["s5_li"], w["s5_ldt"], w["s5_bre"], w["s5_bim"])
    lbr_f, lbi_f = lbr.reshape(1, -1), lbi.reshape(1, -1)
    wbr, wbi = _blockdiag(bbr).astype(bf16), _blockdiag(bbi).astype(bf16)
    wcr, wci = _blockdiag(w["s5_cre"]).astype(bf16), _blockdiag(w["s5_cim"]).astype(bf16)
    ygb, ypre, h0s = _s5_fwd(u, lbr_f, lbi_f, wbr, wbi, wcr, wci, w["s5_d"], nb, s)
    x5, gp, gq = _glu_fwd(ygb, glu_a, glu_b, x4)
    x6 = ffn(x5, 1, 1, weights_of(4, [x5])["ffn"])
    loss, dx6, g["final_g"] = _final_loss(x6, w["final_g"], target)

    dx5 = ffn_back(dx6, 1, 1)

    def glu_bwd(d, p, q):
        sg = jax.nn.sigmoid(q)
        return d * sg, d * p * sg * (1.0 - sg)

    dp, dq = _ew("glu_bwd", glu_bwd, [dx5, gp, gq], [(D_MODEL, bf16), (D_MODEL, bf16)])
    dyg = _mm("glu_dy_a", dp, glu_a, "nt")[0]
    dyg = _mm("glu_dy_b", dq, glu_b, "nt", extras=[dyg], epilogue=lambda acc, r: (acc + r,))[0]
    g["glu_a"], ga_half = _mm_tn("glu_dw_a", ygb, dp)
    g["glu_b"], gb_half = _mm_tn("glu_dw_b", ygb, dq)
    dyg = send("glu", {"glu_a": slots(ga_half), "glu_b": slots(gb_half)}, dyg)
    du, dlr, dli, dwbr, dwbi, dwcr, dwci, g["s5_d"] = _s5_bwd(dyg, ypre, u, h0s, lbr_f, lbi_f, wbr, wbi, wcr, wci, w["s5_d"], nb, s)
    g["s5_cre"], g["s5_cim"] = _blockdiag_t(dwcr), _blockdiag_t(dwci)
    g["s5_lr"], g["s5_li"], g["s5_ldt"], g["s5_bre"], g["s5_bim"] = _s5_prep_bwd(
        w["s5_lr"], w["s5_li"], w["s5_ldt"], w["s5_bre"], w["s5_bim"],
        (dlr.reshape(S5_GROUPS, S5_STATE), dli.reshape(S5_GROUPS, S5_STATE), _blockdiag_t(dwbr), _blockdiag_t(dwbi)))
    dx4, g["mix_g_1"] = _norm_bwd("mix_norm_1_bwd", du, x4, w["mix_g"][1], dx5)
    dx3 = ffn_back(dx4, 1, 0)
    dx2 = ffn_back(dx3, 0, 1)
    dmerged = _mm("out_proj_dx", dx2, w_out, "nt")[0]
    g["w_out"], wo_half = _mm_tn("out_proj_dw", merged, dx2)
    dmerged = send("w_out", {"w_out": slots(wo_half)}, dmerged)
    dq_, dk_, dv_, dgate, g["ret_g"] = _ret_bwd(dmerged, o_raw, rprev, proj, cos, sin, w["ret_g"], nb, s)
    (dxl, dgl, g["conv_w"], g["conv_b"], g["lru_w_a"], g["lru_b_a"], g["lru_w_i"], g["lru_b_i"], g["lru_lam"]) = _lru_bwd(
        dmerged, proj, w["conv_w"], w["conv_b"], w["lru_w_a"], w["lru_b_a"], w["lru_w_i"], w["lru_b_i"], w["lru_lam"], nb, s)
    dproj = _ew("dproj", lambda *p: (jnp.concatenate(p, axis=1),), [dq_, dk_, dv_, dgate, dxl, dgl], [(3072, bf16)])[0]
    dh0 = _mm("in_proj_dx", dproj, w_in, "nt")[0]
    g["w_in"], wi_half = _mm_tn("in_proj_dw", h0b, dproj)
    dh0 = send("w_in", {"w_in": jnp.transpose(wi_half.reshape(D_MODEL, N_DEV, IN_SHARD), (1, 0, 2))}, dh0)
    dx1, g["mix_g_0"] = _norm_bwd("mix_norm_0_bwd", dh0, x1, w["mix_g"][0], dx2)
    dx0 = ffn_back(dx1, 0, 0)
    g["ffn_w1"], g["ffn_w3"], g["ffn_w2"] = ffn_bufs
    return loss, dx0, g


_WEIGHTS = ["ffn_norm_g", "ffn_w1", "ffn_w3", "ffn_w2", "mix_norm_g", "w_in_even", "w_out_even", "ret_norm_g", "conv_w",
            "conv_b", "lru_w_a", "lru_b_a", "lru_w_i", "lru_b_i", "lru_lambda", "s5_lambda_re", "s5_lambda_im", "s5_log_dt",
            "s5_b_re", "s5_b_im", "s5_c_re", "s5_c_im", "s5_d", "glu_w_a", "glu_w_b", "final_norm_g"]
_BIG = ["ffn_w1", "ffn_w3", "ffn_w2", "w_in_even", "w_out_even", "glu_w_a", "glu_w_b"]
_SMALL_SHARDED = ["ffn_norm_g", "conv_w", "s5_d"]
_SMALL = [n for n in _WEIGHTS if n not in _BIG]
_MIDSIZE = ["lru_w_a", "lru_w_i", "s5_b_re", "s5_b_im", "s5_c_re", "s5_c_im"]


def kernel(x, ffn_norm_g, ffn_w1, ffn_w3, ffn_w2, mix_norm_g, w_in_even, w_out_even, ret_norm_g, conv_w, conv_b, lru_w_a, lru_b_a, lru_w_i, lru_b_i, lru_lambda, s5_lambda_re, s5_lambda_im, s5_log_dt, s5_b_re, s5_b_im, s5_c_re, s5_c_im, s5_d, glu_w_a, glu_w_b, final_norm_g, loss_target, m_ffn_norm_g, m_ffn_w1, m_ffn_w3, m_ffn_w2, m_mix_norm_g, m_w_in_even, m_w_out_even, m_ret_norm_g, m_conv_w, m_conv_b, m_lru_w_a, m_lru_b_a, m_lru_w_i, m_lru_b_i, m_lru_lambda, m_s5_lambda_re, m_s5_lambda_im, m_s5_log_dt, m_s5_b_re, m_s5_b_im, m_s5_c_re, m_s5_c_im, m_s5_d, m_glu_w_a, m_glu_w_b, m_final_norm_g, v_ffn_norm_g, v_ffn_w1, v_ffn_w3, v_ffn_w2, v_mix_norm_g, v_w_in_even, v_w_out_even, v_ret_norm_g, v_conv_w, v_conv_b, v_lru_w_a, v_lru_b_a, v_lru_w_i, v_lru_b_i, v_lru_lambda, v_s5_lambda_re, v_s5_lambda_im, v_s5_log_dt, v_s5_b_re, v_s5_b_im, v_s5_c_re, v_s5_c_im, v_s5_d, v_glu_w_a, v_glu_w_b, v_final_norm_g):
    a = dict(locals())
    nb, s, d = x.shape
    ax, ay, ac = lax.axis_index("x"), lax.axis_index("y"), lax.axis_index("c")
    dev = 4 * ax + 2 * ay + ac
    chip = 2 * ax + ay

    def ffn_shards(l, h):
        extra = FF_PAD - FF_SHARD
        return [jnp.pad(ffn_w1[l, h].astype(bf16), ((0, 0), (0, extra))), jnp.pad(ffn_w3[l, h].astype(bf16), ((0, 0), (0, extra))),
                jnp.pad(ffn_w2[l, h].astype(bf16), ((0, extra), (0, 0)))]

    first = _all_gather("ag_first", ffn_shards(0, 0) + [_pack([ffn_norm_g, conv_w, s5_d])])
    sm = first[3].reshape(N_DEV, -1)
    ffn_g_full = jnp.transpose(sm[:, :512].reshape(N_DEV, 2, 2, 128), (1, 2, 0, 3)).reshape(2, 2, D_MODEL)
    conv_w_full = jnp.transpose(sm[:, 512:768].reshape(N_DEV, 4, 64), (1, 0, 2)).reshape(4, LRU_WIDTH)
    s5_d_full = sm[:, 768:896].reshape(1, D_MODEL)

    ag_src = [None, [w_in_even[0].astype(bf16), w_out_even[0].astype(bf16)], ffn_shards(0, 1),
              ffn_shards(1, 0) + [glu_w_a[0].astype(bf16), glu_w_b[0].astype(bf16)], ffn_shards(1, 1)]
    ag, token = [None], first[0]
    for k, grp in enumerate(ag_src):
        if grp is None:
            continue
        grp[0] = _tie(f"tie_ag_{k}", grp[0], [token])
        lands = [lax.dynamic_update_index_in_dim(lax.empty((N_DEV,) + t.shape, bf16), t, dev, 0) for t in grp]
        ag.append(_xchg_start(f"ag_start_{k}", "gather", grp, lands))
        token = ag[-1]["token"]

    def weights_of(k, after):
        if k == 0:
            return {"ffn": [first[0], _tie("tie_ag_started", first[1], [h["token"] for h in ag[1:]]), first[2]]}
        got = _xchg_wait(f"ag_wait_{k}", ag[k], after)
        if k == 1:
            return {"w_in": jnp.transpose(got[0], (1, 0, 2)).reshape(D_MODEL, N_DEV * IN_SHARD),
                    "w_out": got[1].reshape(D_MODEL, D_MODEL)}
        if k == 3:
            return {"ffn": got[:3], "glu_a": got[3].reshape(D_MODEL, D_MODEL), "glu_b": got[4].reshape(D_MODEL, D_MODEL)}
        return {"ffn": got}

    ffn_lands = [lax.empty((N_DEV - 1, 2, 2) + shp, bf16)
                 for shp in ((D_MODEL, FF_SHARD), (D_MODEL, FF_SHARD), (FF_SHARD, D_MODEL))]
    rs = []

    ffn_names = ("ffn_w1", "ffn_w3", "ffn_w2")

    def send(group, arrays, carry):
        srcs = list(arrays.values())
        if group.startswith("ffn_"):
            which = [ffn_names.index(n) for n in arrays]
            sfx = [(int(group[4]), int(group[5]))] * len(which)
            h = _xchg_start("rs_start_" + group, "scatter", srcs, [ffn_lands[k] for k in which], sfx)
            for k, land in zip(which, h["lands"]):
                ffn_lands[k] = land
        else:
            h = _xchg_start("rs_start_" + group, "scatter", srcs,
                            [lax.empty((N_DEV - 1,) + t.shape[1:], bf16) for t in srcs])
        rs.append((group, list(arrays), h))
        return _tie("tie_" + group, carry, [h["token"]])

    w = {
        "ffn_g": [[ffn_g_full[l, h].reshape(1, D_MODEL) for h in range(2)] for l in range(2)],
        "mix_g": [mix_norm_g[0:1], mix_norm_g[1:2]],
        "ret_g": ret_norm_g, "conv_w": conv_w_full, "conv_b": conv_b,
        "lru_w_a": lru_w_a[0], "lru_b_a": lru_b_a, "lru_w_i": lru_w_i[0], "lru_b_i": lru_b_i, "lru_lam": lru_lambda,
        "s5_lr": s5_lambda_re[0], "s5_li": s5_lambda_im[0], "s5_ldt": s5_log_dt.reshape(S5_GROUPS, 1),
        "s5_bre": jnp.swapaxes(s5_b_re[0], 1, 2), "s5_bim": jnp.swapaxes(s5_b_im[0], 1, 2),
        "s5_cre": s5_c_re[0], "s5_cim": s5_c_im[0], "s5_d": s5_d_full,
        "final_g": final_norm_g.reshape(1, D_MODEL),
    }

    small_grads = {}

    def last_small(g, carry):
        part = _small_partials(g)
        mine = _pack([part[n] for n in _SMALL])
        land = lax.dynamic_update_index_in_dim(lax.empty((N_DEV,) + mine.shape, f32), mine, dev, 0)
        h = _xchg_start("ag_start_small_grads", "gather", [mine], [land])
        small_grads.update(h=h, shapes=[part[n].shape for n in _SMALL])
        return _tie("tie_small_grads", carry, [h["token"]])

    loss_part, dx, g = _local_step(x.reshape(nb * s, d), loss_target.reshape(nb * s, d), w, weights_of, send, last_small,
                                   nb, s)
    loss = lax.psum(loss_part[0, 0], ("x", "y", "c"))
    (gath,) = _xchg_wait("ag_wait_small_grads", small_grads["h"], [dx])
    full = dict(zip(_SMALL, _unpack(_sum8("sum_small_grads", gath), small_grads["shapes"])))
    for n in _SMALL_SHARDED:
        width = a[n].shape[-1]
        full[n] = lax.dynamic_slice_in_dim(full[n], dev * width, width, axis=full[n].ndim - 1)
    res = {n: _adamw("adamw_" + n, a[n], a["m_" + n], a["v_" + n], full[n]) for n in _MIDSIZE}
    tiny = [n for n in _SMALL if n not in _MIDSIZE]
    shapes = [a[n].shape for n in tiny]
    packed = _adamw("adamw_small", _pack([a[n] for n in tiny]), _pack([a["m_" + n] for n in tiny]),
                    _pack([a["v_" + n] for n in tiny]), _pack([full[n] for n in tiny]))
    res.update({n: vals for n, vals in zip(tiny, zip(*[_unpack(p, shapes) for p in packed]))})
    return _finish(a, g, dx, loss, res, packed, rs, ffn_lands, dev, nb, s, d)


def _small_partials(g):
    return {
        "ffn_norm_g": jnp.stack([jnp.stack([g[f"ffn_g_{l}{h}"][0] for h in range(2)]) for l in range(2)]),
        "mix_norm_g": jnp.concatenate([g["mix_g_0"], g["mix_g_1"]], axis=0),
        "ret_norm_g": g["ret_g"], "conv_w": g["conv_w"][None], "conv_b": g["conv_b"],
        "lru_w_a": g["lru_w_a"][None], "lru_b_a": g["lru_b_a"], "lru_w_i": g["lru_w_i"][None], "lru_b_i": g["lru_b_i"],
        "lru_lambda": g["lru_lam"], "s5_lambda_re": g["s5_lr"][None], "s5_lambda_im": g["s5_li"][None],
        "s5_log_dt": g["s5_ldt"].reshape(1, S5_GROUPS),
        "s5_b_re": jnp.swapaxes(g["s5_bre"], 1, 2)[None], "s5_b_im": jnp.swapaxes(g["s5_bim"], 1, 2)[None],
        "s5_c_re": g["s5_cre"][None], "s5_c_im": g["s5_cim"][None], "s5_d": g["s5_d"], "final_norm_g": g["final_g"][0],
    }


def _finish(a, g, dx, loss, res, packed, rs, ffn_lands, dev, nb, s, d):
    landed = {}
    for group, names, h in rs:
        if not group.startswith("ffn_"):
            landed.update(zip(names, _xchg_wait("rs_wait_" + group, h, [dx])))
    kinds = {"ffn_w1": "lead", "ffn_w3": "lead", "ffn_w2": "lead", "w_in": "cols", "w_out": "rows", "glu_a": "rows",
             "glu_b": "rows"}

    def update(n, short):
        res[n] = _adamw("adamw_" + n, a[n], a["m_" + n], a["v_" + n], g[short],
                        landed[short].reshape((N_DEV - 1,) + a[n].shape), slot=(dev, kinds[short]))

    for n, short in zip(_BIG[3:], ("w_in", "w_out", "glu_a", "glu_b")):
        update(n, short)
    after = [dx, packed[0]] + [res[n][0] for n in _BIG[3:]]
    for k, n in enumerate(("ffn_w1", "ffn_w3", "ffn_w2")):
        for group, names, h in rs:
            if group.startswith("ffn_") and names == [n]:
                (ffn_lands[k],) = _xchg_wait("rs_wait_" + group, h, after, [ffn_lands[k]])
        landed[n] = ffn_lands[k]
        update(n, n)
        after = after + [res[n][0]]

    out = [loss, dx.reshape(nb, s, d)]
    for k in range(4):
        out += [res[n][k] for n in _WEIGHTS]
    return tuple(out)
```

```python
import functools
import math

import numpy as np
import jax
import jax.numpy as jnp
from jax import lax
from jax.experimental import pallas as pl
from jax.experimental.pallas import tpu as pltpu

f32 = jnp.float32
bf16 = jnp.bfloat16

D_MODEL = 1024
N_DEV = 8
EPS = 1e-6
RET_HEADS = 4
HEAD_DIM = 128
RET_WIDTH = 512
RET_CHUNK = 128
ROPE_BASE = 10000.0
LRU_WIDTH = 512
LRU_BLOCKS = 4
LRU_C = 8.0
S5_GROUP = 16
S5_GROUPS = 64
S5_STATE = 64
S5_CHUNK = 512
S5_BLOCKS = 8
S5_BLOCK_STATES = 512
SUBLANES = 8
D_FF = 2816
FF_SHARD = D_FF // N_DEV
FF_PAD = 384
IN_SHARD = 3072 // N_DEV
ADAM_LR = 0.001
ADAM_B1 = 0.9
ADAM_B2 = 0.999
ADAM_EPS = 1e-08
ADAM_WD = 0.01
ADAM_STEP = 10

VMEM_LIMIT = 56 * 1024 * 1024
VMEM_SPEC = pl.BlockSpec(memory_space=pltpu.VMEM)
ANY_SPEC = pl.BlockSpec(memory_space=pl.ANY)
HBM_SPEC = pl.BlockSpec(memory_space=pltpu.HBM)
SEM_SPEC = pl.BlockSpec(memory_space=pltpu.SEMAPHORE)
SIDE_EFFECT = pltpu.SideEffectType.DATAFLOW_SIDE_EFFECTING
MESH = pl.DeviceIdType.MESH


def _cp(*sem):
    return pltpu.CompilerParams(dimension_semantics=sem, vmem_limit_bytes=VMEM_LIMIT)


def _nn(a, b):
    return jnp.dot(a, b, preferred_element_type=f32)


def _nt(a, b):
    return lax.dot_general(a, b, (((1,), (1,)), ((), ())), preferred_element_type=f32)


def _tn(a, b):
    return lax.dot_general(a, b, (((0,), (0,)), ((), ())), preferred_element_type=f32)


def _rms_fwd(x, g):
    r = lax.rsqrt(jnp.mean(x * x, axis=-1, keepdims=True) + EPS)
    xn = x * r
    return xn * g, xn, r


def _rms_bwd(dh, xn, r, g):
    dxn = dh * g
    dx = r * (dxn - xn * jnp.mean(dxn * xn, axis=-1, keepdims=True))
    dg = jnp.sum(dh * xn, axis=0, keepdims=True)
    return dx, dg


def _shift_dn(v, d, row, fill=0.0):
    return jnp.where(row >= d, pltpu.roll(v, d, 0), fill)


def _shift_up(v, d, row, fill=0.0):
    n = v.shape[0]
    return jnp.where(row < n - d, pltpu.roll(v, n - d, 0), fill)


def _ew(name, fn, ins, outs, tm=512):
    t = ins[0].shape[0]
    n_in = len(ins)

    def body(*refs):
        res = fn(*[r[...] for r in refs[:n_in]])
        for o, v in zip(refs[n_in:], res):
            o[...] = v.astype(o.dtype)

    return pl.pallas_call(
        body, name=name, grid=(t // tm,),
        in_specs=[pl.BlockSpec((tm, a.shape[1]), lambda i: (i, 0)) for a in ins],
        out_specs=[pl.BlockSpec((tm, n), lambda i: (i, 0)) for n, _ in outs],
        out_shape=[jax.ShapeDtypeStruct((t, n), dt) for n, dt in outs],
        compiler_params=_cp("parallel"),
    )(*ins)


def _mm(name, x, w, kind, extras=(), epilogue=None, outs=None, tm=512, tn=1024):
    t = x.shape[0]
    n = w.shape[1] if kind == "nn" else w.shape[0]
    tn = min(tn, n)
    outs = outs or [f32]
    n_ex = len(extras)

    def body(x_ref, w_ref, *refs):
        xb = x_ref[...].astype(bf16)
        acc = _nn(xb, w_ref[...]) if kind == "nn" else _nt(xb, w_ref[...])
        res = epilogue(acc, *[r[...] for r in refs[:n_ex]]) if epilogue else (acc,)
        for o, v in zip(refs[n_ex:], res):
            o[...] = v.astype(o.dtype)

    w_spec = (pl.BlockSpec((w.shape[0], tn), lambda i, j: (0, j)) if kind == "nn"
              else pl.BlockSpec((tn, w.shape[1]), lambda i, j: (j, 0)))
    tile = pl.BlockSpec((tm, tn), lambda i, j: (i, j))
    return pl.pallas_call(
        body, name=name, grid=(t // tm, n // tn),
        in_specs=[pl.BlockSpec((tm, x.shape[1]), lambda i, j: (i, 0)), w_spec] + [tile] * n_ex,
        out_specs=[tile] * len(outs),
        out_shape=[jax.ShapeDtypeStruct((t, n), dt) for dt in outs],
        compiler_params=_cp("parallel", "parallel"),
    )(x, w, *extras)


def _mm_tn(name, x, y, tk=1024, tn=1024, tt=1024):
    t, k = x.shape
    n = y.shape[1]
    tk, tn, tt = min(tk, k), min(tn, n), min(tt, t)

    def body(x_ref, y_ref, o_ref, ob_ref):
        @pl.when(pl.program_id(2) == 0)
        def _():
            o_ref[...] = jnp.zeros_like(o_ref)
        o_ref[...] += _tn(x_ref[...].astype(bf16), y_ref[...].astype(bf16))

        @pl.when(pl.program_id(2) == pl.num_programs(2) - 1)
        def _():
            ob_ref[...] = o_ref[...].astype(bf16)

    out = pl.BlockSpec((tk, tn), lambda i, j, s: (i, j))
    return pl.pallas_call(
        body, name=name, grid=(k // tk, n // tn, t // tt),
        in_specs=[pl.BlockSpec((tt, tk), lambda i, j, s: (s, i)), pl.BlockSpec((tt, tn), lambda i, j, s: (s, j))],
        out_specs=[out, out],
        out_shape=[jax.ShapeDtypeStruct((k, n), f32), jax.ShapeDtypeStruct((k, n), bf16)],
        compiler_params=_cp("parallel", "parallel", "arbitrary"),
    )(x, y)


def _norm_fwd(name, x, g, tm=512):
    t, d = x.shape

    def body(x_ref, g_ref, h_ref, hb_ref):
        h, _, _ = _rms_fwd(x_ref[...], g_ref[...])
        h_ref[...] = h
        hb_ref[...] = h.astype(bf16)

    row = pl.BlockSpec((tm, d), lambda i: (i, 0))
    return pl.pallas_call(
        body, name=name, grid=(t // tm,),
        in_specs=[row, pl.BlockSpec((1, d), lambda i: (0, 0))],
        out_specs=[row, row],
        out_shape=[jax.ShapeDtypeStruct((t, d), f32), jax.ShapeDtypeStruct((t, d), bf16)],
        compiler_params=_cp("parallel"),
    )(x, g)


def _norm_bwd(name, dh, x, g, dres, tm=512):
    t, d = x.shape

    def body(dh_ref, x_ref, g_ref, dres_ref, dx_ref, dg_ref):
        gv = g_ref[...]
        _, xn, r = _rms_fwd(x_ref[...], gv)
        dx, dg = _rms_bwd(dh_ref[...], xn, r, gv)
        dx_ref[...] = dres_ref[...] + dx

        @pl.when(pl.program_id(0) == 0)
        def _():
            dg_ref[...] = jnp.zeros_like(dg_ref)
        dg_ref[...] += dg

    row = pl.BlockSpec((tm, d), lambda i: (i, 0))
    vec = pl.BlockSpec((1, d), lambda i: (0, 0))
    return pl.pallas_call(
        body, name=name, grid=(t // tm,),
        in_specs=[row, row, vec, row],
        out_specs=[row, vec],
        out_shape=[jax.ShapeDtypeStruct((t, d), f32), jax.ShapeDtypeStruct((1, d), f32)],
        compiler_params=_cp("arbitrary"),
    )(dh, x, g, dres)


def _final_loss(x, g, target, tm=512):
    t, d = x.shape

    def body(x_ref, g_ref, t_ref, loss_ref, dx_ref, dg_ref):
        gv = g_ref[...]
        y, xn, r = _rms_fwd(x_ref[...], gv)
        err = y - t_ref[...]
        dy = err * (1.0 / d)
        dx, dg = _rms_bwd(dy, xn, r, gv)
        dx_ref[...] = dx

        @pl.when(pl.program_id(0) == 0)
        def _():
            dg_ref[...] = jnp.zeros_like(dg_ref)
            loss_ref[...] = jnp.zeros_like(loss_ref)
        dg_ref[...] += dg
        loss_ref[...] += jnp.full((1, 128), 0.5 / d, f32) * jnp.sum(err * err)

    row = pl.BlockSpec((tm, d), lambda i: (i, 0))
    vec = pl.BlockSpec((1, d), lambda i: (0, 0))
    return pl.pallas_call(
        body, name="final_loss", grid=(t // tm,),
        in_specs=[row, vec, row],
        out_specs=[pl.BlockSpec((1, 128), lambda i: (0, 0)), row, vec],
        out_shape=[jax.ShapeDtypeStruct((1, 128), f32), jax.ShapeDtypeStruct((t, d), f32),
                   jax.ShapeDtypeStruct((1, d), f32)],
        compiler_params=_cp("arbitrary"),
    )(x, g, target)


def _load_ffn_weights(hbm_refs, vmem_refs, sems):
    @pl.when(pl.program_id(0) == 0)
    def _():
        copies = []
        for k, (src, dst) in enumerate(zip(hbm_refs, vmem_refs)):
            for j in range(N_DEV):
                half = pl.ds((j % 2) * FF_PAD, FF_PAD)
                window = dst.at[j // 2, half, :] if k == 2 else dst.at[j // 2, :, half]
                copies.append(pltpu.make_async_copy(src.at[j], window, sems.at[k * N_DEV + j]))
        for cp in copies:
            cp.start()
        for cp in copies:
            cp.wait()


def _ffn_weight_scratch(nj, d, ff):
    return [pltpu.VMEM((nj, d, ff), bf16), pltpu.VMEM((nj, d, ff), bf16), pltpu.VMEM((nj, ff, d), bf16),
            pltpu.SemaphoreType.DMA((3 * N_DEV,))]


def _ffn_fwd(name, x, g, w1, w3, w2, tm=512):
    t, d = x.shape
    nj, ff = N_DEV // 2, 2 * FF_PAD

    def body(x_ref, g_ref, w1_hbm, w3_hbm, w2_hbm, y_ref, a_ref, b_ref, w1_ref, w3_ref, w2_ref, sems):
        _load_ffn_weights((w1_hbm, w3_hbm, w2_hbm), (w1_ref, w3_ref, w2_ref), sems)
        xv = x_ref[...]
        h, _, _ = _rms_fwd(xv, g_ref[...])
        hb = h.astype(bf16)
        acc = jnp.zeros((tm, d), f32)
        for j in range(nj):
            a = _nn(hb, w1_ref[j])
            b = _nn(hb, w3_ref[j])
            a_ref[j] = a.astype(bf16)
            b_ref[j] = b.astype(bf16)
            u = (a * jax.nn.sigmoid(a) * b).astype(bf16)
            acc = acc + _nn(u, w2_ref[j])
        y_ref[...] = xv + 0.5 * acc

    row = pl.BlockSpec((tm, d), lambda i: (i, 0))
    mid = pl.BlockSpec((nj, tm, ff), lambda i: (0, i, 0))
    return pl.pallas_call(
        body, name=name, grid=(t // tm,),
        in_specs=[row, pl.BlockSpec((1, d), lambda i: (0, 0)), ANY_SPEC, ANY_SPEC, ANY_SPEC],
        out_specs=[row, mid, mid],
        out_shape=[jax.ShapeDtypeStruct((t, d), f32), jax.ShapeDtypeStruct((nj, t, ff), bf16),
                   jax.ShapeDtypeStruct((nj, t, ff), bf16)],
        scratch_shapes=_ffn_weight_scratch(nj, d, ff),
        compiler_params=_cp("arbitrary"),
    )(x, g, w1, w3, w2)


def _ffn_dx(name, dy, x, g, w1, w3, w2, a, b, tm=256):
    t, d = x.shape
    nj, ff = N_DEV // 2, 2 * FF_PAD

    def body(dy_ref, x_ref, g_ref, w1_hbm, w3_hbm, w2_hbm, a_ref, b_ref,
             dx_ref, dg_ref, hbt_ref, dyh_ref, ut_ref, da_ref, db_ref, w1_ref, w3_ref, w2_ref, sems):
        _load_ffn_weights((w1_hbm, w3_hbm, w2_hbm), (w1_ref, w3_ref, w2_ref), sems)
        gv = g_ref[...]
        h, xn, r = _rms_fwd(x_ref[...], gv)
        hbt_ref[...] = h.astype(bf16).T
        dyv = dy_ref[...]
        dyh = (0.5 * dyv).astype(bf16)
        dyh_ref[...] = dyh
        dh = jnp.zeros((tm, d), f32)
        dus = [_nt(dyh, w2_ref[j]) for j in range(nj)]
        for j in range(nj):
            av = a_ref[j].astype(f32)
            bv = b_ref[j].astype(f32)
            s = jax.nn.sigmoid(av)
            silu = av * s
            ut_ref[j] = (silu * bv).astype(bf16).T
            du = dus[j]
            dab = (du * bv * (s * (1.0 + av * (1.0 - s)))).astype(bf16)
            dbb = (du * silu).astype(bf16)
            da_ref[j] = dab
            db_ref[j] = dbb
            dh = dh + _nt(dab, w1_ref[j]) + _nt(dbb, w3_ref[j])
        dx, dg = _rms_bwd(dh, xn, r, gv)
        dx_ref[...] = dyv + dx

        @pl.when(pl.program_id(0) == 0)
        def _():
            dg_ref[...] = jnp.zeros_like(dg_ref)
        dg_ref[...] += dg

    row = pl.BlockSpec((tm, d), lambda i: (i, 0))
    vec = pl.BlockSpec((1, d), lambda i: (0, 0))
    mid = pl.BlockSpec((nj, tm, ff), lambda i: (0, i, 0))
    mid_shape = jax.ShapeDtypeStruct((nj, t, ff), bf16)
    return pl.pallas_call(
        body, name=name, grid=(t // tm,),
        in_specs=[row, row, vec, ANY_SPEC, ANY_SPEC, ANY_SPEC, mid, mid],
        out_specs=[row, vec, pl.BlockSpec((d, tm), lambda i: (0, i)), row,
                   pl.BlockSpec((nj, ff, tm), lambda i: (0, 0, i)), mid, mid],
        out_shape=[jax.ShapeDtypeStruct((t, d), f32), jax.ShapeDtypeStruct((1, d), f32),
                   jax.ShapeDtypeStruct((d, t), bf16), jax.ShapeDtypeStruct((t, d), bf16),
                   jax.ShapeDtypeStruct((nj, ff, t), bf16), mid_shape, mid_shape],
        scratch_shapes=_ffn_weight_scratch(nj, d, ff),
        compiler_params=_cp("arbitrary"),
    )(dy, x, g, w1, w3, w2, a, b)


def _ffn_dw_one(name, xt, y, buf, l, h, tt=2048):
    t = y.shape[-2]
    tt = min(tt, t)
    cut_cols = xt.ndim == 2

    def body(x_ref, y_ref, buf_ref, o_ref, ob_ref, acc):
        s = pl.program_id(1)
        prod = _nn(x_ref[0] if xt.ndim == 3 else x_ref[...], y_ref[0] if y.ndim == 3 else y_ref[...])

        @pl.when(s == 0)
        def _():
            acc[...] = prod

        @pl.when(s > 0)
        def _():
            acc[...] += prod

        @pl.when(s == pl.num_programs(1) - 1)
        def _():
            total = acc[...]
            for e in range(2):
                lo = e * FF_PAD
                part = total[:, lo:lo + FF_SHARD] if cut_cols else total[lo:lo + FF_SHARD, :]
                o_ref[e] = part
                ob_ref[e] = part.astype(bf16)

    x_spec = (pl.BlockSpec((1, xt.shape[1], tt), lambda p, s: (p, 0, s)) if xt.ndim == 3
              else pl.BlockSpec((xt.shape[0], tt), lambda p, s: (0, s)))
    y_spec = (pl.BlockSpec((1, tt, y.shape[2]), lambda p, s: (p, s, 0)) if y.ndim == 3
              else pl.BlockSpec((tt, y.shape[1]), lambda p, s: (s, 0)))
    k_, n_ = buf.shape[-2:]
    return pl.pallas_call(
        body, name=name, grid=(N_DEV // 2, t // tt),
        in_specs=[x_spec, y_spec, ANY_SPEC],
        out_specs=[pl.BlockSpec((2, None, None, k_, n_), lambda p, s: (p, l, h, 0, 0)),
                   pl.BlockSpec((2, k_, n_), lambda p, s: (p, 0, 0))],
        out_shape=[jax.ShapeDtypeStruct(buf.shape, buf.dtype), jax.ShapeDtypeStruct((N_DEV, k_, n_), bf16)],
        input_output_aliases={2: 0},
        scratch_shapes=[pltpu.VMEM((xt.shape[-2], y.shape[-1]), f32)],
        compiler_params=_cp("parallel", "arbitrary"),
    )(xt, y, buf)


_LOG_GAMMA = [float(np.log1p(-np.float32(2.0) ** np.float32(-5.0 - h))) for h in range(RET_HEADS)]


def _ret_consts(h):
    lg = jnp.where(h == 0, _LOG_GAMMA[0], jnp.where(h == 1, _LOG_GAMMA[1],
                   jnp.where(h == 2, _LOG_GAMMA[2], _LOG_GAMMA[3]))).astype(f32)
    c = RET_CHUNK
    r = lax.broadcasted_iota(jnp.int32, (c, c), 0)
    cc = lax.broadcasted_iota(jnp.int32, (c, c), 1)
    decay = jnp.where(r >= cc, jnp.exp(lg * jnp.maximum((r - cc).astype(f32), 0.0)), 0.0)
    pos = lax.broadcasted_iota(jnp.int32, (c, 1), 0).astype(f32)
    kd = jnp.exp(lg * (c - 1.0 - pos))
    qd = jnp.exp(lg * (pos + 1.0))
    gc = jnp.exp(lg * c)
    return decay, kd, qd, gc


def _rope(x, cos, sin):
    return x * cos + pltpu.roll(x, HEAD_DIM // 2, 1) * sin


def _rope_t(g, cos, sin):
    return g * cos + pltpu.roll(g * sin, HEAD_DIM // 2, 1)


def _rope_tables(s):
    half = HEAD_DIM // 2
    inv = ROPE_BASE ** (-jnp.arange(half, dtype=f32) / half)
    ang = jnp.arange(s, dtype=f32)[:, None] * inv[None, :]
    cos, sin = jnp.cos(ang), jnp.sin(ang)
    return jnp.concatenate([cos, cos], axis=1), jnp.concatenate([-sin, sin], axis=1)


def _head_ln(o):
    mu = jnp.mean(o, axis=-1, keepdims=True)
    oc = o - mu
    rs = lax.rsqrt(jnp.mean(oc * oc, axis=-1, keepdims=True) + EPS)
    return oc * rs, rs


def _ret_fwd(proj, cos, sin, ret_g, nb, s):
    c = RET_CHUNK
    nc = s // c
    t = nb * s
    scale = HEAD_DIM ** -0.5

    def body(q_ref, k_ref, v_ref, gate_ref, cos_ref, sin_ref, g_ref, o_ref, rprev_ref, m_ref):
        decay, kd, qd, gc = _ret_consts(pl.program_id(1))
        gv = g_ref[...]

        def step(n, rv):
            rows = pl.ds(pl.multiple_of(n * c, c), c)
            cs, sn = cos_ref[rows, :], sin_ref[rows, :]
            q = _rope(q_ref[rows, :], cs, sn)
            k = _rope(k_ref[rows, :], cs, sn) * scale
            vb = v_ref[rows, :].astype(bf16)
            sc = _nt(q.astype(bf16), k.astype(bf16)) * decay
            rprev_ref[n] = rv
            o = _nn(sc.astype(bf16), vb) + _nn((q * qd).astype(bf16), rv.astype(bf16))
            o_ref[rows, :] = o
            y, _ = _head_ln(o)
            gate = gate_ref[rows, :]
            m_ref[rows, :] = y * gv * (gate * jax.nn.sigmoid(gate))
            return rv * gc + _tn((k * kd).astype(bf16), vb)

        lax.fori_loop(0, nc, step, jnp.zeros((HEAD_DIM, HEAD_DIM), f32))

    def col(off):
        return pl.BlockSpec((s, HEAD_DIM), lambda b, h: (b, off + h))

    tab = pl.BlockSpec((s, HEAD_DIM), lambda b, h: (0, 0))
    return pl.pallas_call(
        body, name="ret_fwd", grid=(nb, RET_HEADS),
        in_specs=[col(0), col(4), col(8), col(12), tab, tab, pl.BlockSpec((1, HEAD_DIM), lambda b, h: (0, h))],
        out_specs=[col(0), pl.BlockSpec((nc, HEAD_DIM, HEAD_DIM), lambda b, h: (b * RET_HEADS + h, 0, 0)), col(0)],
        out_shape=[jax.ShapeDtypeStruct((t, RET_WIDTH), f32),
                   jax.ShapeDtypeStruct((nb * RET_HEADS * nc, HEAD_DIM, HEAD_DIM), f32),
                   jax.ShapeDtypeStruct((t, RET_WIDTH), f32)],
        compiler_params=_cp("parallel", "parallel"),
    )(proj, proj, proj, proj, cos, sin, ret_g)


def _ret_bwd(dmerged, o_raw, rprev, proj, cos, sin, ret_g, nb, s):
    c = RET_CHUNK
    nc = s // c
    t = nb * s
    scale = HEAD_DIM ** -0.5

    def body(dm_ref, o_ref, rprev_ref, q_ref, k_ref, v_ref, gate_ref, cos_ref, sin_ref, g_ref,
             dq_ref, dk_ref, dv_ref, dgate_ref, dg_ref):
        @pl.when(pl.program_id(1) == 0)
        def _():
            dg_ref[...] = jnp.zeros_like(dg_ref)
        decay, kd, qd, gc = _ret_consts(pl.program_id(0))
        gv = g_ref[...]

        def step(i, carry):
            drn, dg = carry
            n = nc - 1 - i
            rows = pl.ds(pl.multiple_of(n * c, c), c)
            cs, sn = cos_ref[rows, :], sin_ref[rows, :]
            q = _rope(q_ref[rows, :], cs, sn)
            k = _rope(k_ref[rows, :], cs, sn) * scale
            qb, kb = q.astype(bf16), k.astype(bf16)
            vb = v_ref[rows, :].astype(bf16)
            sc = _nt(qb, kb) * decay
            y, rs = _head_ln(o_ref[rows, :])
            gate = gate_ref[rows, :]
            sg = jax.nn.sigmoid(gate)
            silu = gate * sg
            dm = dm_ref[rows, :]
            dgate_ref[rows, :] = dm * y * gv * (sg * (1.0 + gate * (1.0 - sg)))
            dyl = dm * gv * silu
            dg = dg + jnp.sum(dm * y * silu, axis=0, keepdims=True)
            do = rs * (dyl - jnp.mean(dyl, axis=-1, keepdims=True) - y * jnp.mean(dyl * y, axis=-1, keepdims=True))
            dob = do.astype(bf16)
            rv = rprev_ref[n]
            drb = drn.astype(bf16)
            ds = (_nt(dob, vb) * decay).astype(bf16)
            kdb = (k * kd).astype(bf16)
            qdb = (q * qd).astype(bf16)
            dq_r = _nn(ds, kb) + _nt(dob, rv.astype(bf16)) * qd
            dk_r = _tn(ds, qb) + _nt(vb, drb) * kd
            dv_ref[rows, :] = _tn(sc.astype(bf16), dob) + _nn(kdb, drb)
            dq_ref[rows, :] = _rope_t(dq_r, cs, sn)
            dk_ref[rows, :] = _rope_t(dk_r * scale, cs, sn)
            return drn * gc + _tn(qdb, dob), dg

        _, dg = lax.fori_loop(0, nc, step, (jnp.zeros((HEAD_DIM, HEAD_DIM), f32), jnp.zeros((1, HEAD_DIM), f32)))
        dg_ref[...] += dg

    def col(off):
        return pl.BlockSpec((s, HEAD_DIM), lambda h, b: (b, off + h))

    tab = pl.BlockSpec((s, HEAD_DIM), lambda h, b: (0, 0))
    gsp = pl.BlockSpec((1, HEAD_DIM), lambda h, b: (0, h))
    out_t = jax.ShapeDtypeStruct((t, RET_WIDTH), f32)
    return pl.pallas_call(
        body, name="ret_bwd", grid=(RET_HEADS, nb),
        in_specs=[col(0), col(0), pl.BlockSpec((nc, HEAD_DIM, HEAD_DIM), lambda h, b: (b * RET_HEADS + h, 0, 0)),
                  col(0), col(4), col(8), col(12), tab, tab, gsp],
        out_specs=[col(0), col(0), col(0), col(0), gsp],
        out_shape=[out_t, out_t, out_t, out_t, jax.ShapeDtypeStruct((1, RET_WIDTH), f32)],
        compiler_params=_cp("parallel", "arbitrary"),
    )(dmerged, o_raw, rprev, proj, proj, proj, proj, cos, sin, ret_g)


def _neg_expm1(z):
    series = -(z * (1.0 + z * (0.5 + z * (1.0 / 6.0 + z * (1.0 / 24.0)))))
    return jnp.where(z > -0.01, series, 1.0 - jnp.exp(z))


def _lru_gates(xc, pa, pi, lam):
    r = jax.nn.sigmoid(pa)
    i = jax.nn.sigmoid(pi)
    log_a = -LRU_C * r * jax.nn.softplus(-lam)
    a = jnp.exp(log_a)
    bx = jnp.sqrt(_neg_expm1(2.0 * log_a)) * i * xc
    return a, bx


def _scan_rows(a, b, row, up):
    sub = row[:SUBLANES] & (SUBLANES - 1)
    groups = list(range(a.shape[0] // SUBLANES))
    out = [None] * len(groups)
    edge = slice(0, 1) if up else slice(SUBLANES - 1, SUBLANES)
    carry = jnp.zeros((1, a.shape[1]), f32)
    for g in (reversed(groups) if up else groups):
        rows = slice(g * SUBLANES, (g + 1) * SUBLANES)
        xa, xb = a[rows], b[rows]
        d = 1
        while d < SUBLANES:
            keep = (sub < SUBLANES - d) if up else (sub >= d)
            shift = SUBLANES - d if up else d
            xb = xa * jnp.where(keep, pltpu.roll(xb, shift, 0), 0.0) + xb
            xa = xa * jnp.where(keep, pltpu.roll(xa, shift, 0), 1.0)
            d *= 2
        out[g] = xb + xa * carry
        carry = out[g][edge]
    return jnp.concatenate(out, axis=0)


def _scan_fwd(a, b, row):
    return _scan_rows(a, b, row, False)


def _scan_bwd(c, b, row):
    return _scan_rows(c, b, row, True)


def _conv_fwd(x, cw, cb, row):
    return (cb + cw[3:4] * x + cw[2:3] * _shift_dn(x, 1, row) + cw[1:2] * _shift_dn(x, 2, row)
            + cw[0:1] * _shift_dn(x, 3, row))


def _lru_specs(s, order):
    def im(f):
        return (lambda b, g: f(b, g)) if order == "bg" else (lambda g, b: f(b, g))
    seq = lambda off: pl.BlockSpec((s, 128), im(lambda b, g: (b, off + g)))
    vec = pl.BlockSpec((1, 128), im(lambda b, g: (0, g)))
    cw = pl.BlockSpec((4, 128), im(lambda b, g: (0, g)))
    mat = pl.BlockSpec((1, 128, 128), im(lambda b, g: (g, 0, 0)))
    return seq, vec, cw, mat


def _lru_fwd(proj, conv_w, conv_b, w_a, b_a, w_i, b_i, lam, nb, s):
    def body(x_ref, gt_ref, cw_ref, cb_ref, wa_ref, ba_ref, wi_ref, bi_ref, lam_ref, out_ref):
        row = lax.broadcasted_iota(jnp.int32, (s, 128), 0)
        xc = _conv_fwd(x_ref[...], cw_ref[...], cb_ref[...], row)
        xcb = xc.astype(bf16)
        pa = _nn(xcb, wa_ref[0].astype(bf16)) + ba_ref[...]
        pi = _nn(xcb, wi_ref[0].astype(bf16)) + bi_ref[...]
        a, bx = _lru_gates(xc, pa, pi, lam_ref[...])
        h = _scan_fwd(a, bx, row)
        out_ref[...] = h * jax.nn.gelu(gt_ref[...])

    seq, vec, cw, mat = _lru_specs(s, "bg")
    return pl.pallas_call(
        body, name="lru_fwd", grid=(nb, LRU_BLOCKS),
        in_specs=[seq(16), seq(20), cw, vec, mat, vec, mat, vec, vec],
        out_specs=seq(0),
        out_shape=jax.ShapeDtypeStruct((nb * s, LRU_WIDTH), f32),
        compiler_params=_cp("parallel", "parallel"),
    )(proj, proj, conv_w, conv_b, w_a, b_a, w_i, b_i, lam)


def _lru_bwd(dmerged, proj, conv_w, conv_b, w_a, b_a, w_i, b_i, lam, nb, s):
    def body(dout_ref, x_ref, gt_ref, cw_ref, cb_ref, wa_ref, ba_ref, wi_ref, bi_ref, lam_ref,
             dx_ref, dgt_ref, dcw_ref, dcb_ref, dwa_ref, dba_ref, dwi_ref, dbi_ref, dlam_ref):
        row = lax.broadcasted_iota(jnp.int32, (s, 128), 0)
        x = x_ref[...]
        cwv = cw_ref[...]
        xc = _conv_fwd(x, cwv, cb_ref[...], row)
        xcb = xc.astype(bf16)
        wab, wib = wa_ref[0].astype(bf16), wi_ref[0].astype(bf16)
        pa = _nn(xcb, wab) + ba_ref[...]
        pi = _nn(xcb, wib) + bi_ref[...]
        (a, bx), gates_vjp = jax.vjp(_lru_gates, xc, pa, pi, lam_ref[...])
        h = _scan_fwd(a, bx, row)
        ge, gelu_vjp = jax.vjp(jax.nn.gelu, gt_ref[...])
        dout = dout_ref[...]
        dgt_ref[...] = gelu_vjp(dout * h)[0]
        adj = _scan_bwd(_shift_up(a, 1, row), dout * ge, row)
        dxc, dpa, dpi, dlam = gates_vjp((adj * _shift_dn(h, 1, row), adj))
        dpab, dpib = dpa.astype(bf16), dpi.astype(bf16)
        dxc = dxc + _nt(dpab, wab) + _nt(dpib, wib)
        dx_ref[...] = (cwv[3:4] * dxc + cwv[2:3] * _shift_up(dxc, 1, row) + cwv[1:2] * _shift_up(dxc, 2, row)
                       + cwv[0:1] * _shift_up(dxc, 3, row))

        @pl.when(pl.program_id(1) == 0)
        def _():
            for r in (dcw_ref, dcb_ref, dwa_ref, dba_ref, dwi_ref, dbi_ref, dlam_ref):
                r[...] = jnp.zeros_like(r)
        rsum = lambda v: jnp.sum(v, axis=0, keepdims=True)
        dcw_ref[...] += jnp.concatenate([rsum(dxc * _shift_dn(x, 3, row)), rsum(dxc * _shift_dn(x, 2, row)),
                                         rsum(dxc * _shift_dn(x, 1, row)), rsum(dxc * x)], axis=0)
        dcb_ref[...] += rsum(dxc)
        dwa_ref[0] += _tn(xcb, dpab)
        dwi_ref[0] += _tn(xcb, dpib)
        dba_ref[...] += rsum(dpa)
        dbi_ref[...] += rsum(dpi)
        dlam_ref[...] += dlam

    seq, vec, cw, mat = _lru_specs(s, "gb")
    t = nb * s
    vshape = jax.ShapeDtypeStruct((1, LRU_WIDTH), f32)
    mshape = jax.ShapeDtypeStruct((LRU_BLOCKS, 128, 128), f32)
    return pl.pallas_call(
        body, name="lru_bwd", grid=(LRU_BLOCKS, nb),
        in_specs=[seq(4), seq(16), seq(20), cw, vec, mat, vec, mat, vec, vec],
        out_specs=[seq(0), seq(0), cw, vec, mat, vec, mat, vec, vec],
        out_shape=[jax.ShapeDtypeStruct((t, LRU_WIDTH), f32), jax.ShapeDtypeStruct((t, LRU_WIDTH), f32),
                   jax.ShapeDtypeStruct((4, LRU_WIDTH), f32), vshape, mshape, vshape, mshape, vshape, vshape],
        compiler_params=_cp("parallel", "arbitrary"),
    )(dmerged, proj, proj, conv_w, conv_b, w_a, b_a, w_i, b_i, lam)


def _s5_disc(lr, li, ldt, bre, bim):
    dt = jnp.exp(ldt)
    mag = jnp.exp(lr * dt)
    lbr = mag * jnp.cos(li * dt)
    lbi = mag * jnp.sin(li * dt)
    den = lr * lr + li * li
    nr = lbr - 1.0
    fr = (nr * lr + lbi * li) / den
    fi = (lbi * lr - nr * li) / den
    bbr = fr[:, None, :] * bre - fi[:, None, :] * bim
    bbi = fr[:, None, :] * bim + fi[:, None, :] * bre
    return lbr, lbi, bbr, bbi


def _s5_prep(lr, li, ldt, bre, bim):
    def body(lr_ref, li_ref, ldt_ref, bre_ref, bim_ref, o1, o2, o3, o4):
        o1[...], o2[...], o3[...], o4[...] = _s5_disc(lr_ref[...], li_ref[...], ldt_ref[...], bre_ref[...], bim_ref[...])

    return pl.pallas_call(
        body, name="s5_prep", in_specs=[VMEM_SPEC] * 5, out_specs=[VMEM_SPEC] * 4,
        out_shape=[jax.ShapeDtypeStruct(lr.shape, f32), jax.ShapeDtypeStruct(lr.shape, f32),
                   jax.ShapeDtypeStruct(bre.shape, f32), jax.ShapeDtypeStruct(bre.shape, f32)],
    )(lr, li, ldt, bre, bim)


def _s5_prep_bwd(lr, li, ldt, bre, bim, cts):
    def body(lr_ref, li_ref, ldt_ref, bre_ref, bim_ref, g1, g2, g3, g4, o1, o2, o3, o4, o5):
        _, vjp = jax.vjp(_s5_disc, lr_ref[...], li_ref[...], ldt_ref[...], bre_ref[...], bim_ref[...])
        o1[...], o2[...], o3[...], o4[...], o5[...] = vjp((g1[...], g2[...], g3[...], g4[...]))

    return pl.pallas_call(
        body, name="s5_prep_bwd", in_specs=[VMEM_SPEC] * 9, out_specs=[VMEM_SPEC] * 5,
        out_shape=[jax.ShapeDtypeStruct(v.shape, f32) for v in (lr, li, ldt, bre, bim)],
    )(lr, li, ldt, bre, bim, *cts)


def _cmul(ar, ai, br, bi):
    return ar * br - ai * bi, ar * bi + ai * br


def _s5_pow_table(lr, li, n, row, up):
    ar = jnp.broadcast_to(lr, (n, lr.shape[1]))
    ai = jnp.broadcast_to(li, (n, li.shape[1]))
    shift = _shift_up if up else _shift_dn
    d = 1
    while d < n:
        ar, ai = _cmul(ar, ai, shift(ar, d, row, 1.0), shift(ai, d, row, 0.0))
        d *= 2
    return ar, ai


def _s5_step_factors(lr, li, row, up):
    sub = row & (SUBLANES - 1)
    out, pr, pi, d = [], lr, li, 1
    while d < SUBLANES:
        keep = (sub < SUBLANES - d) if up else (sub >= d)
        out.append((jnp.where(keep, pr, 0.0), jnp.where(keep, pi, 0.0)))
        pr, pi = _cmul(pr, pi, pr, pi)
        d *= 2
    return out


def _s5_scan(br, bi, steps, tab_r, tab_i, cr, ci, up):
    groups = list(range(br.shape[0] // SUBLANES))
    out_r, out_i = [None] * len(groups), [None] * len(groups)
    edge = slice(0, 1) if up else slice(SUBLANES - 1, SUBLANES)
    for g in (reversed(groups) if up else groups):
        rows = slice(g * SUBLANES, (g + 1) * SUBLANES)
        xr, xi = br[rows], bi[rows]
        for k, (mr, mi) in enumerate(steps):
            shift = SUBLANES - (1 << k) if up else 1 << k
            tr, ti = _cmul(mr, mi, pltpu.roll(xr, shift, 0), pltpu.roll(xi, shift, 0))
            xr, xi = xr + tr, xi + ti
        tr, ti = _cmul(tab_r, tab_i, cr, ci)
        hr, hi = xr + tr, xi + ti
        out_r[g], out_i[g] = hr, hi
        cr, ci = hr[edge], hi[edge]
    return jnp.concatenate(out_r, axis=0), jnp.concatenate(out_i, axis=0)


def _s5_specs(t, nb, nc):
    seq = pl.BlockSpec((t, 128), lambda k: (0, k))
    lvec = pl.BlockSpec((1, S5_BLOCK_STATES), lambda k: (0, k))
    dvec = pl.BlockSpec((1, 128), lambda k: (0, k))
    wmat = pl.BlockSpec((1, 128, S5_BLOCK_STATES), lambda k: (k, 0, 0))
    h0 = pl.BlockSpec((nb, None, nc, 2, S5_BLOCK_STATES), lambda k: (0, k, 0, 0, 0))
    return seq, lvec, dvec, wmat, h0


def _s5_fwd(u, lbr, lbi, wbr, wbi, wcr, wci, dskip, nb, s):
    ln = S5_CHUNK
    nc = s // ln

    def body(u_ref, lr_ref, li_ref, wbr_ref, wbi_ref, wcr_ref, wci_ref, d_ref, yg_ref, y_ref, h0_ref):
        row = lax.broadcasted_iota(jnp.int32, (ln, S5_BLOCK_STATES), 0)
        lr, li = lr_ref[...], li_ref[...]
        pr, pi = _s5_pow_table(lr, li, SUBLANES, row[:SUBLANES], False)
        steps = _s5_step_factors(lr, li, row[:SUBLANES], False)
        dv = d_ref[...]

        def chunk(b, n, h0r, h0i):
            st = pl.multiple_of(b * s + n * ln, ln)
            uc = u_ref[pl.ds(st, ln), :]
            ub = uc.astype(bf16)
            hr, hi = _s5_scan(_nn(ub, wbr_ref[0]), _nn(ub, wbi_ref[0]), steps, pr, pi, h0r, h0i, False)
            h0_ref[b, n, 0:1, :] = h0r
            h0_ref[b, n, 1:2, :] = h0i
            y = _nt(hr.astype(bf16), wcr_ref[0]) - _nt(hi.astype(bf16), wci_ref[0]) + dv * uc
            y_ref[pl.ds(st, ln), :] = y
            yg_ref[pl.ds(st, ln), :] = jax.nn.gelu(y).astype(bf16)
            return hr[ln - 1:ln, :], hi[ln - 1:ln, :]

        def step(n, carry):
            return tuple(chunk(b, n, *carry[b]) for b in range(nb))

        z = jnp.zeros((1, S5_BLOCK_STATES), f32)
        lax.fori_loop(0, nc, step, ((z, z),) * nb)

    t = nb * s
    seq, lvec, dvec, wmat, h0 = _s5_specs(t, nb, nc)
    return pl.pallas_call(
        body, name="s5_fwd", grid=(S5_BLOCKS,),
        in_specs=[seq, lvec, lvec, wmat, wmat, wmat, wmat, dvec],
        out_specs=[seq, seq, h0],
        out_shape=[jax.ShapeDtypeStruct((t, D_MODEL), bf16), jax.ShapeDtypeStruct((t, D_MODEL), f32),
                   jax.ShapeDtypeStruct((nb, S5_BLOCKS, nc, 2, S5_BLOCK_STATES), f32)],
        compiler_params=_cp("parallel"),
    )(u, lbr, lbi, wbr, wbi, wcr, wci, dskip)


def _s5_bwd(dyg, y, u, h0, lbr, lbi, wbr, wbi, wcr, wci, dskip, nb, s):
    ln = S5_CHUNK
    nc = s // ln

    def body(dyg_ref, y_ref, u_ref, h0_ref, lr_ref, li_ref, wbr_ref, wbi_ref, wcr_ref, wci_ref, d_ref,
             du_ref, dlr_ref, dli_ref, dwbr_ref, dwbi_ref, dwcr_ref, dwci_ref, dd_ref):
        for r in (dlr_ref, dli_ref, dwbr_ref, dwbi_ref, dwcr_ref, dwci_ref, dd_ref):
            r[...] = jnp.zeros_like(r)
        row = lax.broadcasted_iota(jnp.int32, (ln, S5_BLOCK_STATES), 0)
        lr, li = lr_ref[...], li_ref[...]
        pr, pi = _s5_pow_table(lr, li, SUBLANES, row[:SUBLANES], False)
        qr, qi = _s5_pow_table(lr, -li, SUBLANES, row[:SUBLANES], True)
        row8 = row[:SUBLANES]
        steps_dn, steps_up = _s5_step_factors(lr, li, row8, False), _s5_step_factors(lr, -li, row8, True)
        dv = d_ref[...]
        rsum = lambda v: jnp.sum(v, axis=0, keepdims=True)

        def chunk(b, n, gnr, gni):
            st = pl.multiple_of(b * s + n * ln, ln)
            uc = u_ref[pl.ds(st, ln), :]
            ub = uc.astype(bf16)
            h0v = h0_ref[b, n]
            h0r, h0i = h0v[0:1], h0v[1:2]
            hr, hi = _s5_scan(_nn(ub, wbr_ref[0]), _nn(ub, wbi_ref[0]), steps_dn, pr, pi, h0r, h0i, False)
            dy = jax.vjp(jax.nn.gelu, y_ref[pl.ds(st, ln), :])[1](dyg_ref[pl.ds(st, ln), :])[0]
            dyb = dy.astype(bf16)
            dd_ref[...] += rsum(dy * uc)
            gr, gi = _s5_scan(_nn(dyb, wcr_ref[0]), -_nn(dyb, wci_ref[0]), steps_up, qr, qi, gnr, gni, True)
            hpr = jnp.where(row >= 1, pltpu.roll(hr, 1, 0), h0r)
            hpi = jnp.where(row >= 1, pltpu.roll(hi, 1, 0), h0i)
            dlr_ref[...] += rsum(gr * hpr + gi * hpi)
            dli_ref[...] += rsum(gi * hpr - gr * hpi)
            grb, gib = gr.astype(bf16), gi.astype(bf16)
            dwbr_ref[0] += _tn(ub, grb)
            dwbi_ref[0] += _tn(ub, gib)
            dwcr_ref[0] += _tn(dyb, hr.astype(bf16))
            dwci_ref[0] -= _tn(dyb, hi.astype(bf16))
            du_ref[pl.ds(st, ln), :] = _nt(grb, wbr_ref[0]) + _nt(gib, wbi_ref[0]) + dv * dy
            return gr[0:1, :], gi[0:1, :]

        def step(i, carry):
            return tuple(chunk(b, nc - 1 - i, *carry[b]) for b in range(nb))

        z = jnp.zeros((1, S5_BLOCK_STATES), f32)
        lax.fori_loop(0, nc, step, ((z, z),) * nb)

    t = nb * s
    seq, lvec, dvec, wmat, h0s = _s5_specs(t, nb, nc)
    lshape = jax.ShapeDtypeStruct((1, S5_BLOCKS * S5_BLOCK_STATES), f32)
    wshape = jax.ShapeDtypeStruct((S5_BLOCKS, 128, S5_BLOCK_STATES), f32)
    return pl.pallas_call(
        body, name="s5_bwd", grid=(S5_BLOCKS,),
        in_specs=[seq, seq, seq, h0s, lvec, lvec, wmat, wmat, wmat, wmat, dvec],
        out_specs=[seq, lvec, lvec, wmat, wmat, wmat, wmat, dvec],
        out_shape=[jax.ShapeDtypeStruct((t, D_MODEL), f32), lshape, lshape, wshape, wshape, wshape, wshape,
                   jax.ShapeDtypeStruct((1, D_MODEL), f32)],
        compiler_params=_cp("parallel"),
    )(dyg, y, u, h0, lbr, lbi, wbr, wbi, wcr, wci, dskip)


def _blockdiag(w):
    w4 = w.reshape(S5_BLOCKS, 8, S5_GROUP, S5_STATE)
    same_group = jnp.eye(8, dtype=bool)[None, :, None, :, None]
    return jnp.where(same_group, w4[:, :, :, None, :], 0.0).reshape(S5_BLOCKS, 128, S5_BLOCK_STATES)


def _blockdiag_t(dw):
    d5 = dw.reshape(S5_BLOCKS, 8, S5_GROUP, 8, S5_STATE)
    diag = jnp.diagonal(d5, axis1=1, axis2=3)
    return jnp.moveaxis(diag, 3, 1).reshape(S5_GROUPS, S5_GROUP, S5_STATE)


def _glu_fwd(ygb, wa, wb, x, tm=512, tn=1024):
    t, d = x.shape

    def body(y_ref, wa_ref, wb_ref, x_ref, o_ref, p_ref, q_ref):
        p = _nn(y_ref[...], wa_ref[...])
        q = _nn(y_ref[...], wb_ref[...])
        p_ref[...] = p
        q_ref[...] = q
        o_ref[...] = x_ref[...] + p * jax.nn.sigmoid(q)

    tile = pl.BlockSpec((tm, tn), lambda i, j: (i, j))
    wsp = pl.BlockSpec((d, tn), lambda i, j: (0, j))
    out = jax.ShapeDtypeStruct((t, d), f32)
    return pl.pallas_call(
        body, name="glu_fwd", grid=(t // tm, d // tn),
        in_specs=[pl.BlockSpec((tm, d), lambda i, j: (i, 0)), wsp, wsp, tile],
        out_specs=[tile, tile, tile], out_shape=[out, out, out],
        compiler_params=_cp("parallel", "parallel"),
    )(ygb, wa, wb, x)


def _place():
    x, y, c = lax.axis_index("x"), lax.axis_index("y"), lax.axis_index("c")
    return x, y, c, [(1 - x, y), (x, 1 - y), (1 - x, 1 - y)]


def _all_gather(name, arrays):
    n = len(arrays)

    def body(*refs):
        ins, outs = refs[:n], refs[n:2 * n]
        send_sems, recv_sems, local_sems = refs[2 * n:]
        x, y, c, chips = _place()
        me, sib = (x, y, c), (x, y, 1 - c)

        def copy(i, k, block, to, src=None):
            dst = outs[i].at[4 * block[0] + 2 * block[1] + block[2]]
            return pltpu.make_async_remote_copy(
                src_ref=dst if src is None else src, dst_ref=dst,
                send_sem=send_sems.at[i * 7 + k], recv_sem=recv_sems.at[i * 7 + k],
                device_id=to, device_id_type=MESH)

        mine = [pltpu.make_async_copy(ins[i], outs[i].at[4 * x + 2 * y + c], local_sems.at[i]) for i in range(n)]
        for m in mine:
            m.start()
        first = []
        for i in range(n):
            first.append(copy(i, 0, me, sib, src=ins[i]))
            first += [copy(i, 1 + j, me, (*chip, c), src=ins[i]) for j, chip in enumerate(chips)]
        for cp in first:
            cp.start()
        passed = []
        for j, chip in enumerate(chips):
            for i in range(n):
                copy(i, 1 + j, (*chip, c), me).wait_recv()
                fwd = copy(i, 4 + j, (*chip, c), sib)
                fwd.start()
                passed.append(fwd)
        for i in range(n):
            copy(i, 0, sib, me).wait_recv()
        for j, chip in enumerate(chips):
            for i in range(n):
                copy(i, 4 + j, (*chip, 1 - c), me).wait_recv()
        for cp in first + passed:
            cp.wait_send()
        for m in mine:
            m.wait()

    return pl.pallas_call(
        body, name=name,
        in_specs=[ANY_SPEC] * n, out_specs=[ANY_SPEC] * n,
        out_shape=[jax.ShapeDtypeStruct((N_DEV,) + a.shape, a.dtype) for a in arrays],
        scratch_shapes=[pltpu.SemaphoreType.DMA((7 * n,)), pltpu.SemaphoreType.DMA((7 * n,)),
                        pltpu.SemaphoreType.DMA((n,))],
    )(*arrays)


def _tie(name, x, deps):
    def body(*refs):
        pass

    return pl.pallas_call(
        body, name=name, in_specs=[ANY_SPEC] * (1 + len(deps)), out_specs=ANY_SPEC,
        out_shape=jax.ShapeDtypeStruct(x.shape, x.dtype), input_output_aliases={0: 0},
    )(x, *deps)


def _xchg_copies(kind, srcs, lands, suffixes, send_sems, recv_sems):
    x, y, c, _ = _place()
    copies = []
    for i, (src, land, sfx) in enumerate(zip(srcs, lands, suffixes)):
        for k in range(N_DEV - 1):
            r = k + 1
            peer = (1 - x if r & 4 else x, 1 - y if r & 2 else y, 1 - c if r & 1 else c)
            if kind == "gather":
                s_ref, d_ref = src, land.at[(4 * x + 2 * y + c,) + sfx]
            else:
                s_ref, d_ref = src.at[4 * peer[0] + 2 * peer[1] + peer[2]], land.at[(k,) + sfx]
            copies.append(pltpu.make_async_remote_copy(
                src_ref=s_ref, dst_ref=d_ref, send_sem=send_sems.at[i * 7 + k], recv_sem=recv_sems.at[i * 7 + k],
                device_id=peer, device_id_type=MESH))
    return copies


def _xchg_start(name, kind, srcs, lands, suffixes=None):
    n = len(srcs)
    suffixes = suffixes or [()] * n

    def body(*refs):
        src, land = refs[:n], refs[n:2 * n]
        send_sems, recv_sems, token = refs[2 * n], refs[2 * n + 1], refs[-1]
        for cp in _xchg_copies(kind, src, land, suffixes, send_sems, recv_sems):
            cp.start()
        token[...] = jnp.zeros_like(token)

    arrays = list(srcs) + list(lands)
    outs = pl.pallas_call(
        body, name=name,
        out_shape=(pltpu.SemaphoreType.DMA((7 * n,)), pltpu.SemaphoreType.DMA((7 * n,)),
                   *[pltpu.HBM(a.shape, a.dtype) for a in arrays], jax.ShapeDtypeStruct((8, 128), f32)),
        in_specs=[HBM_SPEC] * (2 * n),
        out_specs=(SEM_SPEC, SEM_SPEC, *[HBM_SPEC] * (2 * n), VMEM_SPEC),
        input_output_aliases={i: 2 + i for i in range(2 * n)},
        compiler_params=pltpu.CompilerParams(has_side_effects=SIDE_EFFECT),
    )(*[pltpu.with_memory_space_constraint(a, pltpu.HBM) for a in arrays])
    return dict(kind=kind, n=n, suffixes=suffixes, send=outs[0], recv=outs[1], srcs=list(outs[2:2 + n]),
                lands=list(outs[2 + n:2 + 2 * n]), token=outs[-1])


def _xchg_wait(name, h, after, lands=None):
    n = h["n"]
    lands = h["lands"] if lands is None else lands

    def body(*refs):
        src, land = refs[:n], refs[n:2 * n]
        for cp in _xchg_copies(h["kind"], src, land, h["suffixes"], refs[2 * n], refs[2 * n + 1]):
            cp.wait_send()
            cp.wait_recv()

    arrays = h["srcs"] + list(lands)
    outs = pl.pallas_call(
        body, name=name,
        out_shape=tuple(pltpu.HBM(a.shape, a.dtype) for a in arrays),
        in_specs=[HBM_SPEC] * (2 * n) + [SEM_SPEC, SEM_SPEC] + [ANY_SPEC] * len(after),
        out_specs=tuple([HBM_SPEC] * (2 * n)),
        input_output_aliases={i: i for i in range(2 * n)},
        compiler_params=pltpu.CompilerParams(has_side_effects=SIDE_EFFECT),
    )(*arrays, h["send"], h["recv"], *after)
    return list(outs[n:])


def _rows(a):
    return a.reshape(-1, a.shape[-1])


def _row_tile(r):
    for tm in (512, 256, 128, 64, 32, 16, 8):
        if r % tm == 0:
            return tm
    return r


def _sum8(name, gathered):
    _, r, n = gathered.shape
    tm = _row_tile(r)

    def body(g_ref, o_ref):
        acc = g_ref[0]
        for k in range(1, N_DEV):
            acc = acc + g_ref[k]
        o_ref[...] = acc

    return pl.pallas_call(
        body, name=name, grid=(r // tm,),
        in_specs=[pl.BlockSpec((N_DEV, tm, n), lambda i: (0, i, 0))],
        out_specs=pl.BlockSpec((tm, n), lambda i: (i, 0)),
        out_shape=jax.ShapeDtypeStruct((r, n), f32),
        compiler_params=_cp("parallel"),
    )(gathered)


def _adamw(name, w, m, v, own, landed=None, slot=None):
    shape = w.shape
    w2, m2, v2 = _rows(w), _rows(m), _rows(v)
    r, n = w2.shape
    tm = _row_tile(r)
    c1 = 1.0 - ADAM_B1 ** ADAM_STEP
    c2 = 1.0 - ADAM_B2 ** ADAM_STEP
    extra = [] if landed is None else [landed.reshape(landed.shape[0], r, n)]
    row = pl.BlockSpec((tm, n), lambda i, *_: (i, 0))
    if slot is None:
        o2, own_spec, scalars = _rows(own), row, []
    else:
        dev, kind = slot
        scalars = [dev.reshape(1).astype(jnp.int32)]
        if kind == "lead":
            o2, own_spec = own.reshape(N_DEV, r, n), pl.BlockSpec((None, tm, n), lambda i, d: (d[0], i, 0))
        elif kind == "rows":
            o2, own_spec = own, pl.BlockSpec((tm, n), lambda i, d: (d[0] * (r // tm) + i, 0))
        else:
            o2, own_spec = own, pl.BlockSpec((tm, n), lambda i, d: (i, d[0]))

    def body(*refs):
        w_ref, m_ref, v_ref, o_ref = refs[len(scalars):len(scalars) + 4]
        refs = refs[len(scalars) + 4:]
        g = o_ref[...]
        if extra:
            for k in range(extra[0].shape[0]):
                g = g + refs[0][k].astype(f32)
        g_ref, d_ref, mn_ref, vn_ref = refs[len(extra):]
        mn = ADAM_B1 * m_ref[...] + (1.0 - ADAM_B1) * g
        vn = ADAM_B2 * v_ref[...] + (1.0 - ADAM_B2) * (g * g)
        g_ref[...] = g
        d_ref[...] = -ADAM_LR * ((mn / c1) / (jnp.sqrt(vn / c2) + ADAM_EPS) + ADAM_WD * w_ref[...])
        mn_ref[...] = mn
        vn_ref[...] = vn

    outs = pl.pallas_call(
        body, name=name,
        grid_spec=pltpu.PrefetchScalarGridSpec(
            num_scalar_prefetch=len(scalars), grid=(r // tm,),
            in_specs=[row] * 3 + [own_spec] + [pl.BlockSpec((e.shape[0], tm, n), lambda i, *_: (0, i, 0)) for e in extra],
            out_specs=[row] * 4),
        out_shape=[jax.ShapeDtypeStruct((r, n), f32)] * 4,
        compiler_params=_cp("parallel"),
    )(*scalars, w2, m2, v2, o2, *extra)
    return [o.reshape(shape) for o in outs]


def _pack(arrays):
    flat = jnp.concatenate([a.reshape(-1).astype(f32) for a in arrays])
    pad = (-flat.shape[0]) % (128 * (512 if flat.shape[0] > 128 * 512 else 8))
    return jnp.pad(flat, (0, pad)).reshape(-1, 128)


def _unpack(packed, shapes):
    flat = packed.reshape(-1)
    out, off = [], 0
    for s in shapes:
        n = math.prod(s)
        out.append(flat[off:off + n].reshape(s))
        off += n
    return out


def _local_step(x, target, w, weights_of, send, last_small, nb, s):
    cos, sin = _rope_tables(s)
    g = {}
    ffn_saved = {}
    ffn_bufs = [lax.empty((N_DEV, 2, 2) + shp, f32)
                for shp in ((D_MODEL, FF_SHARD), (D_MODEL, FF_SHARD), (FF_SHARD, D_MODEL))]

    def ffn(xin, l, h, wts):
        y, a, b = _ffn_fwd(f"ffn_fwd_{l}{h}", xin, w["ffn_g"][l][h], *wts)
        ffn_saved[(l, h)] = (xin, a, b, wts)
        return y

    def ffn_back(dy, l, h):
        xin, a, b, wts = ffn_saved[(l, h)]
        dx, dg, hb, dyh, u, da, db = _ffn_dx(f"ffn_dx_{l}{h}", dy, xin, w["ffn_g"][l][h], *wts, a, b)
        g[f"ffn_g_{l}{h}"] = dg
        if (l, h) == (0, 0):
            hb = last_small(g, hb)
        ffn_bufs[0], half = _ffn_dw_one(f"ffn_dw_{l}{h}_w1", hb, da, ffn_bufs[0], l, h)
        hb = send(f"ffn_{l}{h}_w1", {"ffn_w1": half}, hb)
        ffn_bufs[1], half = _ffn_dw_one(f"ffn_dw_{l}{h}_w3", hb, db, ffn_bufs[1], l, h)
        u = send(f"ffn_{l}{h}_w3", {"ffn_w3": half}, u)
        ffn_bufs[2], half = _ffn_dw_one(f"ffn_dw_{l}{h}_w2", u, dyh, ffn_bufs[2], l, h)
        return send(f"ffn_{l}{h}_w2", {"ffn_w2": half}, dx)

    def slots(t):
        return t.reshape(N_DEV, D_MODEL // N_DEV, D_MODEL)

    x1 = ffn(x, 0, 0, weights_of(0, [])["ffn"])
    wg = weights_of(1, [x1])
    w_in, w_out = wg["w_in"], wg["w_out"]
    _, h0b = _norm_fwd("mix_norm_0", x1, w["mix_g"][0])
    proj = _mm("in_proj", h0b, w_in, "nn", tn=1536)[0]
    o_raw, rprev, mret = _ret_fwd(proj, cos, sin, w["ret_g"], nb, s)
    lru = _lru_fwd(proj, w["conv_w"], w["conv_b"], w["lru_w_a"], w["lru_b_a"], w["lru_w_i"], w["lru_b_i"], w["lru_lam"], nb, s)
    merged = _ew("merge", lambda a, b: (jnp.concatenate([a, b], axis=1),), [mret, lru], [(D_MODEL, bf16)])[0]
    x2 = _mm("out_proj", merged, w_out, "nn", extras=[x1], epilogue=lambda acc, r: (acc + r,))[0]
    x3 = ffn(x2, 0, 1, weights_of(2, [x2])["ffn"])
    x4 = ffn(x3, 1, 0, weights_of(3, [x3])["ffn"])
    u, _ = _norm_fwd("mix_norm_1", x4, w["mix_g"][1])
    lbr, lbi, bbr, bbi = _s5_prep(w["s5_lr"], w["s5_li"], w["s5_ldt"], w["s5_bre"], w["s5_bim"])
    lbr_f, lbi_f = lbr.reshape(1, -1), lbi.reshape(1, -1)
    wbr, wbi = _blockdiag(bbr).astype(bf16), _blockdiag(bbi).astype(bf16)
    wcr, wci = _blockdiag(w["s5_cre"]).astype(bf16), _blockdiag(w["s5_cim"]).astype(bf16)
    ygb, ypre, h0s = _s5_fwd(u, lbr_f, lbi_f, wbr, wbi, wcr, wci, w["s5_d"], nb, s)
    wg = weights_of(4, [ygb])
    glu_a, glu_b = wg["glu_a"], wg["glu_b"]
    x5, gp, gq = _glu_fwd(ygb, glu_a, glu_b, x4)
    x6 = ffn(x5, 1, 1, weights_of(5, [x5])["ffn"])
    loss, dx6, g["final_g"] = _final_loss(x6, w["final_g"], target)

    dx5 = ffn_back(dx6, 1, 1)

    def glu_bwd(d, p, q):
        sg = jax.nn.sigmoid(q)
        return d * sg, d * p * sg * (1.0 - sg)

    dp, dq = _ew("glu_bwd", glu_bwd, [dx5, gp, gq], [(D_MODEL, bf16), (D_MODEL, bf16)])
    dyg = _mm("glu_dy_a", dp, glu_a, "nt")[0]
    dyg = _mm("glu_dy_b", dq, glu_b, "nt", extras=[dyg], epilogue=lambda acc, r: (acc + r,))[0]
    g["glu_a"], ga_half = _mm_tn("glu_dw_a", ygb, dp)
    g["glu_b"], gb_half = _mm_tn("glu_dw_b", ygb, dq)
    dyg = send("glu", {"glu_a": slots(ga_half), "glu_b": slots(gb_half)}, dyg)
    du, dlr, dli, dwbr, dwbi, dwcr, dwci, g["s5_d"] = _s5_bwd(dyg, ypre, u, h0s, lbr_f, lbi_f, wbr, wbi, wcr, wci, w["s5_d"], nb, s)
    g["s5_cre"], g["s5_cim"] = _blockdiag_t(dwcr), _blockdiag_t(dwci)
    g["s5_lr"], g["s5_li"], g["s5_ldt"], g["s5_bre"], g["s5_bim"] = _s5_prep_bwd(
        w["s5_lr"], w["s5_li"], w["s5_ldt"], w["s5_bre"], w["s5_bim"],
        (dlr.reshape(S5_GROUPS, S5_STATE), dli.reshape(S5_GROUPS, S5_STATE), _blockdiag_t(dwbr), _blockdiag_t(dwbi)))
    dx4, g["mix_g_1"] = _norm_bwd("mix_norm_1_bwd", du, x4, w["mix_g"][1], dx5)
    dx3 = ffn_back(dx4, 1, 0)
    dx2 = ffn_back(dx3, 0, 1)
    dmerged = _mm("out_proj_dx", dx2, w_out, "nt")[0]
    g["w_out"], wo_half = _mm_tn("out_proj_dw", merged, dx2)
    dmerged = send("w_out", {"w_out": slots(wo_half)}, dmerged)
    dq_, dk_, dv_, dgate, g["ret_g"] = _ret_bwd(dmerged, o_raw, rprev, proj, cos, sin, w["ret_g"], nb, s)
    (dxl, dgl, g["conv_w"], g["conv_b"], g["lru_w_a"], g["lru_b_a"], g["lru_w_i"], g["lru_b_i"], g["lru_lam"]) = _lru_bwd(
        dmerged, proj, w["conv_w"], w["conv_b"], w["lru_w_a"], w["lru_b_a"], w["lru_w_i"], w["lru_b_i"], w["lru_lam"], nb, s)
    dproj = _ew("dproj", lambda *p: (jnp.concatenate(p, axis=1),), [dq_, dk_, dv_, dgate, dxl, dgl], [(3072, bf16)])[0]
    dh0 = _mm("in_proj_dx", dproj, w_in, "nt")[0]
    g["w_in"], wi_half = _mm_tn("in_proj_dw", h0b, dproj)
    dh0 = send("w_in", {"w_in": jnp.transpose(wi_half.reshape(D_MODEL, N_DEV, IN_SHARD), (1, 0, 2))}, dh0)
    dx1, g["mix_g_0"] = _norm_bwd("mix_norm_0_bwd", dh0, x1, w["mix_g"][0], dx2)
    dx0 = ffn_back(dx1, 0, 0)
    g["ffn_w1"], g["ffn_w3"], g["ffn_w2"] = ffn_bufs
    return loss, dx0, g


_WEIGHTS = ["ffn_norm_g", "ffn_w1", "ffn_w3", "ffn_w2", "mix_norm_g", "w_in_even", "w_out_even", "ret_norm_g", "conv_w",
            "conv_b", "lru_w_a", "lru_b_a", "lru_w_i", "lru_b_i", "lru_lambda", "s5_lambda_re", "s5_lambda_im", "s5_log_dt",
            "s5_b_re", "s5_b_im", "s5_c_re", "s5_c_im", "s5_d", "glu_w_a", "glu_w_b", "final_norm_g"]
_BIG = ["ffn_w1", "ffn_w3", "ffn_w2", "w_in_even", "w_out_even", "glu_w_a", "glu_w_b"]
_SMALL_SHARDED = ["ffn_norm_g", "conv_w", "s5_d"]
_SMALL = [n for n in _WEIGHTS if n not in _BIG]
_MIDSIZE = ["lru_w_a", "lru_w_i", "s5_b_re", "s5_b_im", "s5_c_re", "s5_c_im"]


def kernel(x, ffn_norm_g, ffn_w1, ffn_w3, ffn_w2, mix_norm_g, w_in_even, w_out_even, ret_norm_g, conv_w, conv_b, lru_w_a, lru_b_a, lru_w_i, lru_b_i, lru_lambda, s5_lambda_re, s5_lambda_im, s5_log_dt, s5_b_re, s5_b_im, s5_c_re, s5_c_im, s5_d, glu_w_a, glu_w_b, final_norm_g, loss_target, m_ffn_norm_g, m_ffn_w1, m_ffn_w3, m_ffn_w2, m_mix_norm_g, m_w_in_even, m_w_out_even, m_ret_norm_g, m_conv_w, m_conv_b, m_lru_w_a, m_lru_b_a, m_lru_w_i, m_lru_b_i, m_lru_lambda, m_s5_lambda_re, m_s5_lambda_im, m_s5_log_dt, m_s5_b_re, m_s5_b_im, m_s5_c_re, m_s5_c_im, m_s5_d, m_glu_w_a, m_glu_w_b, m_final_norm_g, v_ffn_norm_g, v_ffn_w1, v_ffn_w3, v_ffn_w2, v_mix_norm_g, v_w_in_even, v_w_out_even, v_ret_norm_g, v_conv_w, v_conv_b, v_lru_w_a, v_lru_b_a, v_lru_w_i, v_lru_b_i, v_lru_lambda, v_s5_lambda_re, v_s5_lambda_im, v_s5_log_dt, v_s5_b_re, v_s5_b_im, v_s5_c_re, v_s5_c_im, v_s5_d, v_glu_w_a, v_glu_w_b, v_final_norm_g):
    a = dict(locals())
    nb, s, d = x.shape
    ax, ay, ac = lax.axis_index("x"), lax.axis_index("y"), lax.axis_index("c")
    dev = 4 * ax + 2 * ay + ac
    chip = 2 * ax + ay

    def ffn_shards(l, h):
        extra = FF_PAD - FF_SHARD
        return [jnp.pad(ffn_w1[l, h].astype(bf16), ((0, 0), (0, extra))), jnp.pad(ffn_w3[l, h].astype(bf16), ((0, 0), (0, extra))),
                jnp.pad(ffn_w2[l, h].astype(bf16), ((0, extra), (0, 0)))]

    first = _all_gather("ag_first", ffn_shards(0, 0) + [_pack([ffn_norm_g, conv_w, s5_d])])
    sm = first[3].reshape(N_DEV, -1)
    ffn_g_full = jnp.transpose(sm[:, :512].reshape(N_DEV, 2, 2, 128), (1, 2, 0, 3)).reshape(2, 2, D_MODEL)
    conv_w_full = jnp.transpose(sm[:, 512:768].reshape(N_DEV, 4, 64), (1, 0, 2)).reshape(4, LRU_WIDTH)
    s5_d_full = sm[:, 768:896].reshape(1, D_MODEL)

    ag_src = [None, [w_in_even[0].astype(bf16), w_out_even[0].astype(bf16)], ffn_shards(0, 1), ffn_shards(1, 0),
              [glu_w_a[0].astype(bf16), glu_w_b[0].astype(bf16)], ffn_shards(1, 1)]
    ag, token = [None], first[0]
    for k, grp in enumerate(ag_src):
        if grp is None:
            continue
        grp[0] = _tie(f"tie_ag_{k}", grp[0], [token])
        lands = [lax.dynamic_update_index_in_dim(lax.empty((N_DEV,) + t.shape, bf16), t, dev, 0) for t in grp]
        ag.append(_xchg_start(f"ag_start_{k}", "gather", grp, lands))
        token = ag[-1]["token"]

    def weights_of(k, after):
        if k == 0:
            return {"ffn": [first[0], _tie("tie_ag_started", first[1], [h["token"] for h in ag[1:]]), first[2]]}
        got = _xchg_wait(f"ag_wait_{k}", ag[k], after)
        if k == 1:
            return {"w_in": jnp.transpose(got[0], (1, 0, 2)).reshape(D_MODEL, N_DEV * IN_SHARD),
                    "w_out": got[1].reshape(D_MODEL, D_MODEL)}
        if k == 4:
            return {"glu_a": got[0].reshape(D_MODEL, D_MODEL), "glu_b": got[1].reshape(D_MODEL, D_MODEL)}
        return {"ffn": got}

    ffn_lands = [lax.empty((N_DEV - 1, 2, 2) + shp, bf16)
                 for shp in ((D_MODEL, FF_SHARD), (D_MODEL, FF_SHARD), (FF_SHARD, D_MODEL))]
    rs = []

    ffn_names = ("ffn_w1", "ffn_w3", "ffn_w2")

    def send(group, arrays, carry):
        srcs = list(arrays.values())
        if group.startswith("ffn_"):
            which = [ffn_names.index(n) for n in arrays]
            sfx = [(int(group[4]), int(group[5]))] * len(which)
            h = _xchg_start("rs_start_" + group, "scatter", srcs, [ffn_lands[k] for k in which], sfx)
            for k, land in zip(which, h["lands"]):
                ffn_lands[k] = land
        else:
            h = _xchg_start("rs_start_" + group, "scatter", srcs,
                            [lax.empty((N_DEV - 1,) + t.shape[1:], bf16) for t in srcs])
        rs.append((group, list(arrays), h))
        return _tie("tie_" + group, carry, [h["token"]])

    w = {
        "ffn_g": [[ffn_g_full[l, h].reshape(1, D_MODEL) for h in range(2)] for l in range(2)],
        "mix_g": [mix_norm_g[0:1], mix_norm_g[1:2]],
        "ret_g": ret_norm_g, "conv_w": conv_w_full, "conv_b": conv_b,
        "lru_w_a": lru_w_a[0], "lru_b_a": lru_b_a, "lru_w_i": lru_w_i[0], "lru_b_i": lru_b_i, "lru_lam": lru_lambda,
        "s5_lr": s5_lambda_re[0], "s5_li": s5_lambda_im[0], "s5_ldt": s5_log_dt.reshape(S5_GROUPS, 1),
        "s5_bre": jnp.swapaxes(s5_b_re[0], 1, 2), "s5_bim": jnp.swapaxes(s5_b_im[0], 1, 2),
        "s5_cre": s5_c_re[0], "s5_cim": s5_c_im[0], "s5_d": s5_d_full,
        "final_g": final_norm_g.reshape(1, D_MODEL),
    }

    small_grads = {}

    def last_small(g, carry):
        part = _small_partials(g)
        mine = _pack([part[n] for n in _SMALL])
        land = lax.dynamic_update_index_in_dim(lax.empty((N_DEV,) + mine.shape, f32), mine, dev, 0)
        h = _xchg_start("ag_start_small_grads", "gather", [mine], [land])
        small_grads.update(h=h, shapes=[part[n].shape for n in _SMALL])
        return _tie("tie_small_grads", carry, [h["token"]])

    loss_part, dx, g = _local_step(x.reshape(nb * s, d), loss_target.reshape(nb * s, d), w, weights_of, send, last_small,
                                   nb, s)
    loss = lax.psum(loss_part[0, 0], ("x", "y", "c"))
    (gath,) = _xchg_wait("ag_wait_small_grads", small_grads["h"], [dx])
    full = dict(zip(_SMALL, _unpack(_sum8("sum_small_grads", gath), small_grads["shapes"])))
    for n in _SMALL_SHARDED:
        width = a[n].shape[-1]
        full[n] = lax.dynamic_slice_in_dim(full[n], dev * width, width, axis=full[n].ndim - 1)
    res = {n: _adamw("adamw_" + n, a[n], a["m_" + n], a["v_" + n], full[n]) for n in _MIDSIZE}
    tiny = [n for n in _SMALL if n not in _MIDSIZE]
    shapes = [a[n].shape for n in tiny]
    packed = _adamw("adamw_small", _pack([a[n] for n in tiny]), _pack([a["m_" + n] for n in tiny]),
                    _pack([a["v_" + n] for n in tiny]), _pack([full[n] for n in tiny]))
    res.update({n: vals for n, vals in zip(tiny, zip(*[_unpack(p, shapes) for p in packed]))})
    return _finish(a, g, dx, loss, res, packed, rs, ffn_lands, dev, nb, s, d)


def _small_partials(g):
    return {
        "ffn_norm_g": jnp.stack([jnp.stack([g[f"ffn_g_{l}{h}"][0] for h in range(2)]) for l in range(2)]),
        "mix_norm_g": jnp.concatenate([g["mix_g_0"], g["mix_g_1"]], axis=0),
        "ret_norm_g": g["ret_g"], "conv_w": g["conv_w"][None], "conv_b": g["conv_b"],
        "lru_w_a": g["lru_w_a"][None], "lru_b_a": g["lru_b_a"], "lru_w_i": g["lru_w_i"][None], "lru_b_i": g["lru_b_i"],
        "lru_lambda": g["lru_lam"], "s5_lambda_re": g["s5_lr"][None], "s5_lambda_im": g["s5_li"][None],
        "s5_log_dt": g["s5_ldt"].reshape(1, S5_GROUPS),
        "s5_b_re": jnp.swapaxes(g["s5_bre"], 1, 2)[None], "s5_b_im": jnp.swapaxes(g["s5_bim"], 1, 2)[None],
        "s5_c_re": g["s5_cre"][None], "s5_c_im": g["s5_cim"][None], "s5_d": g["s5_d"], "final_norm_g": g["final_g"][0],
    }


def _finish(a, g, dx, loss, res, packed, rs, ffn_lands, dev, nb, s, d):
    landed = {}
    for group, names, h in rs:
        if not group.startswith("ffn_"):
            landed.update(zip(names, _xchg_wait("rs_wait_" + group, h, [dx])))
    kinds = {"ffn_w1": "lead", "ffn_w3": "lead", "ffn_w2": "lead", "w_in": "cols", "w_out": "rows", "glu_a": "rows",
             "glu_b": "rows"}

    def update(n, short):
        res[n] = _adamw("adamw_" + n, a[n], a["m_" + n], a["v_" + n], g[short],
                        landed[short].reshape((N_DEV - 1,) + a[n].shape), slot=(dev, kinds[short]))

    for n, short in zip(_BIG[3:], ("w_in", "w_out", "glu_a", "glu_b")):
        update(n, short)
    after = [dx, packed[0]] + [res[n][0] for n in _BIG[3:] + _MIDSIZE]
    for k, n in enumerate(("ffn_w1", "ffn_w3", "ffn_w2")):
        for group, names, h in rs:
            if group.startswith("ffn_") and names == [n]:
                (ffn_lands[k],) = _xchg_wait("rs_wait_" + group, h, after, [ffn_lands[k]])
        landed[n] = ffn_lands[k]
        update(n, n)
        after = after + [res[n][0]]

    out = [loss, dx.reshape(nb, s, d)]
    for k in range(4):
        out += [res[n][k] for n in _WEIGHTS]
    return tuple(out)
```

```python
import functools
import math

import numpy as np
import jax
import jax.numpy as jnp
from jax import lax
from jax.experimental import pallas as pl
from jax.experimental.pallas import tpu as pltpu

f32 = jnp.float32
bf16 = jnp.bfloat16

D_MODEL = 1024
N_DEV = 8
EPS = 1e-6
RET_HEADS = 4
HEAD_DIM = 128
RET_WIDTH = 512
RET_CHUNK = 128
ROPE_BASE = 10000.0
LRU_WIDTH = 512
LRU_BLOCKS = 4
LRU_C = 8.0
S5_GROUP = 16
S5_GROUPS = 64
S5_STATE = 64
S5_CHUNK = 1024
S5_BLOCKS = 8
S5_BLOCK_STATES = 512
SUBLANES = 8
D_FF = 2816
FF_SHARD = D_FF // N_DEV
FF_PAD = 384
IN_SHARD = 3072 // N_DEV
ADAM_LR = 0.001
ADAM_B1 = 0.9
ADAM_B2 = 0.999
ADAM_EPS = 1e-08
ADAM_WD = 0.01
ADAM_STEP = 10

VMEM_LIMIT = 56 * 1024 * 1024
VMEM_SPEC = pl.BlockSpec(memory_space=pltpu.VMEM)
ANY_SPEC = pl.BlockSpec(memory_space=pl.ANY)
HBM_SPEC = pl.BlockSpec(memory_space=pltpu.HBM)
SEM_SPEC = pl.BlockSpec(memory_space=pltpu.SEMAPHORE)
SIDE_EFFECT = pltpu.SideEffectType.DATAFLOW_SIDE_EFFECTING
MESH = pl.DeviceIdType.MESH


def _cp(*sem):
    return pltpu.CompilerParams(dimension_semantics=sem, vmem_limit_bytes=VMEM_LIMIT)


def _nn(a, b):
    return jnp.dot(a, b, preferred_element_type=f32)


def _nt(a, b):
    return lax.dot_general(a, b, (((1,), (1,)), ((), ())), preferred_element_type=f32)


def _tn(a, b):
    return lax.dot_general(a, b, (((0,), (0,)), ((), ())), preferred_element_type=f32)


def _rms_fwd(x, g):
    r = lax.rsqrt(jnp.mean(x * x, axis=-1, keepdims=True) + EPS)
    xn = x * r
    return xn * g, xn, r


def _rms_bwd(dh, xn, r, g):
    dxn = dh * g
    dx = r * (dxn - xn * jnp.mean(dxn * xn, axis=-1, keepdims=True))
    dg = jnp.sum(dh * xn, axis=0, keepdims=True)
    return dx, dg


def _shift_dn(v, d, row, fill=0.0):
    return jnp.where(row >= d, pltpu.roll(v, d, 0), fill)


def _shift_up(v, d, row, fill=0.0):
    n = v.shape[0]
    return jnp.where(row < n - d, pltpu.roll(v, n - d, 0), fill)


def _ew(name, fn, ins, outs, tm=512):
    t = ins[0].shape[0]
    n_in = len(ins)

    def body(*refs):
        res = fn(*[r[...] for r in refs[:n_in]])
        for o, v in zip(refs[n_in:], res):
            o[...] = v.astype(o.dtype)

    return pl.pallas_call(
        body, name=name, grid=(t // tm,),
        in_specs=[pl.BlockSpec((tm, a.shape[1]), lambda i: (i, 0)) for a in ins],
        out_specs=[pl.BlockSpec((tm, n), lambda i: (i, 0)) for n, _ in outs],
        out_shape=[jax.ShapeDtypeStruct((t, n), dt) for n, dt in outs],
        compiler_params=_cp("parallel"),
    )(*ins)


def _mm(name, x, w, kind, extras=(), epilogue=None, outs=None, tm=512, tn=1024):
    t = x.shape[0]
    n = w.shape[1] if kind == "nn" else w.shape[0]
    tn = min(tn, n)
    outs = outs or [f32]
    n_ex = len(extras)

    def body(x_ref, w_ref, *refs):
        xb = x_ref[...].astype(bf16)
        acc = _nn(xb, w_ref[...]) if kind == "nn" else _nt(xb, w_ref[...])
        res = epilogue(acc, *[r[...] for r in refs[:n_ex]]) if epilogue else (acc,)
        for o, v in zip(refs[n_ex:], res):
            o[...] = v.astype(o.dtype)

    w_spec = (pl.BlockSpec((w.shape[0], tn), lambda i, j: (0, j)) if kind == "nn"
              else pl.BlockSpec((tn, w.shape[1]), lambda i, j: (j, 0)))
    tile = pl.BlockSpec((tm, tn), lambda i, j: (i, j))
    return pl.pallas_call(
        body, name=name, grid=(t // tm, n // tn),
        in_specs=[pl.BlockSpec((tm, x.shape[1]), lambda i, j: (i, 0)), w_spec] + [tile] * n_ex,
        out_specs=[tile] * len(outs),
        out_shape=[jax.ShapeDtypeStruct((t, n), dt) for dt in outs],
        compiler_params=_cp("parallel", "parallel"),
    )(x, w, *extras)


def _mm_tn(name, x, y, tk=1024, tn=1024, tt=1024):
    t, k = x.shape
    n = y.shape[1]
    tk, tn, tt = min(tk, k), min(tn, n), min(tt, t)

    def body(x_ref, y_ref, o_ref, ob_ref):
        @pl.when(pl.program_id(2) == 0)
        def _():
            o_ref[...] = jnp.zeros_like(o_ref)
        o_ref[...] += _tn(x_ref[...].astype(bf16), y_ref[...].astype(bf16))

        @pl.when(pl.program_id(2) == pl.num_programs(2) - 1)
        def _():
            ob_ref[...] = o_ref[...].astype(bf16)

    out = pl.BlockSpec((tk, tn), lambda i, j, s: (i, j))
    return pl.pallas_call(
        body, name=name, grid=(k // tk, n // tn, t // tt),
        in_specs=[pl.BlockSpec((tt, tk), lambda i, j, s: (s, i)), pl.BlockSpec((tt, tn), lambda i, j, s: (s, j))],
        out_specs=[out, out],
        out_shape=[jax.ShapeDtypeStruct((k, n), f32), jax.ShapeDtypeStruct((k, n), bf16)],
        compiler_params=_cp("parallel", "parallel", "arbitrary"),
    )(x, y)


def _norm_fwd(name, x, g, tm=512):
    t, d = x.shape

    def body(x_ref, g_ref, h_ref, hb_ref):
        h, _, _ = _rms_fwd(x_ref[...], g_ref[...])
        h_ref[...] = h
        hb_ref[...] = h.astype(bf16)

    row = pl.BlockSpec((tm, d), lambda i: (i, 0))
    return pl.pallas_call(
        body, name=name, grid=(t // tm,),
        in_specs=[row, pl.BlockSpec((1, d), lambda i: (0, 0))],
        out_specs=[row, row],
        out_shape=[jax.ShapeDtypeStruct((t, d), f32), jax.ShapeDtypeStruct((t, d), bf16)],
        compiler_params=_cp("parallel"),
    )(x, g)


def _norm_bwd(name, dh, x, g, dres, tm=512):
    t, d = x.shape

    def body(dh_ref, x_ref, g_ref, dres_ref, dx_ref, dg_ref):
        gv = g_ref[...]
        _, xn, r = _rms_fwd(x_ref[...], gv)
        dx, dg = _rms_bwd(dh_ref[...], xn, r, gv)
        dx_ref[...] = dres_ref[...] + dx

        @pl.when(pl.program_id(0) == 0)
        def _():
            dg_ref[...] = jnp.zeros_like(dg_ref)
        dg_ref[...] += dg

    row = pl.BlockSpec((tm, d), lambda i: (i, 0))
    vec = pl.BlockSpec((1, d), lambda i: (0, 0))
    return pl.pallas_call(
        body, name=name, grid=(t // tm,),
        in_specs=[row, row, vec, row],
        out_specs=[row, vec],
        out_shape=[jax.ShapeDtypeStruct((t, d), f32), jax.ShapeDtypeStruct((1, d), f32)],
        compiler_params=_cp("arbitrary"),
    )(dh, x, g, dres)


def _final_loss(x, g, target, tm=512):
    t, d = x.shape

    def body(x_ref, g_ref, t_ref, loss_ref, dx_ref, dg_ref):
        gv = g_ref[...]
        y, xn, r = _rms_fwd(x_ref[...], gv)
        err = y - t_ref[...]
        dy = err * (1.0 / d)
        dx, dg = _rms_bwd(dy, xn, r, gv)
        dx_ref[...] = dx

        @pl.when(pl.program_id(0) == 0)
        def _():
            dg_ref[...] = jnp.zeros_like(dg_ref)
            loss_ref[...] = jnp.zeros_like(loss_ref)
        dg_ref[...] += dg
        loss_ref[...] += jnp.full((1, 128), 0.5 / d, f32) * jnp.sum(err * err)

    row = pl.BlockSpec((tm, d), lambda i: (i, 0))
    vec = pl.BlockSpec((1, d), lambda i: (0, 0))
    return pl.pallas_call(
        body, name="final_loss", grid=(t // tm,),
        in_specs=[row, vec, row],
        out_specs=[pl.BlockSpec((1, 128), lambda i: (0, 0)), row, vec],
        out_shape=[jax.ShapeDtypeStruct((1, 128), f32), jax.ShapeDtypeStruct((t, d), f32),
                   jax.ShapeDtypeStruct((1, d), f32)],
        compiler_params=_cp("arbitrary"),
    )(x, g, target)


def _load_ffn_weights(hbm_refs, vmem_refs, sems):
    @pl.when(pl.program_id(0) == 0)
    def _():
        copies = []
        for k, (src, dst) in enumerate(zip(hbm_refs, vmem_refs)):
            for j in range(N_DEV):
                half = pl.ds((j % 2) * FF_PAD, FF_PAD)
                window = dst.at[j // 2, half, :] if k == 2 else dst.at[j // 2, :, half]
                copies.append(pltpu.make_async_copy(src.at[j], window, sems.at[k * N_DEV + j]))
        for cp in copies:
            cp.start()
        for cp in copies:
            cp.wait()


def _ffn_weight_scratch(nj, d, ff):
    return [pltpu.VMEM((nj, d, ff), bf16), pltpu.VMEM((nj, d, ff), bf16), pltpu.VMEM((nj, ff, d), bf16),
            pltpu.SemaphoreType.DMA((3 * N_DEV,))]


def _ffn_fwd(name, x, g, w1, w3, w2, tm=512):
    t, d = x.shape
    nj, ff = N_DEV // 2, 2 * FF_PAD

    def body(x_ref, g_ref, w1_hbm, w3_hbm, w2_hbm, y_ref, a_ref, b_ref, w1_ref, w3_ref, w2_ref, sems):
        _load_ffn_weights((w1_hbm, w3_hbm, w2_hbm), (w1_ref, w3_ref, w2_ref), sems)
        xv = x_ref[...]
        h, _, _ = _rms_fwd(xv, g_ref[...])
        hb = h.astype(bf16)
        acc = jnp.zeros((tm, d), f32)
        for j in range(nj):
            a = _nn(hb, w1_ref[j])
            b = _nn(hb, w3_ref[j])
            a_ref[j] = a.astype(bf16)
            b_ref[j] = b.astype(bf16)
            u = (a * jax.nn.sigmoid(a) * b).astype(bf16)
            acc = acc + _nn(u, w2_ref[j])
        y_ref[...] = xv + 0.5 * acc

    row = pl.BlockSpec((tm, d), lambda i: (i, 0))
    mid = pl.BlockSpec((nj, tm, ff), lambda i: (0, i, 0))
    return pl.pallas_call(
        body, name=name, grid=(t // tm,),
        in_specs=[row, pl.BlockSpec((1, d), lambda i: (0, 0)), ANY_SPEC, ANY_SPEC, ANY_SPEC],
        out_specs=[row, mid, mid],
        out_shape=[jax.ShapeDtypeStruct((t, d), f32), jax.ShapeDtypeStruct((nj, t, ff), bf16),
                   jax.ShapeDtypeStruct((nj, t, ff), bf16)],
        scratch_shapes=_ffn_weight_scratch(nj, d, ff),
        compiler_params=_cp("arbitrary"),
    )(x, g, w1, w3, w2)


def _ffn_dx(name, dy, x, g, w1, w3, w2, a, b, tm=256):
    t, d = x.shape
    nj, ff = N_DEV // 2, 2 * FF_PAD

    def body(dy_ref, x_ref, g_ref, w1_hbm, w3_hbm, w2_hbm, a_ref, b_ref,
             dx_ref, dg_ref, hbt_ref, dyh_ref, ut_ref, da_ref, db_ref, w1_ref, w3_ref, w2_ref, sems):
        _load_ffn_weights((w1_hbm, w3_hbm, w2_hbm), (w1_ref, w3_ref, w2_ref), sems)
        gv = g_ref[...]
        h, xn, r = _rms_fwd(x_ref[...], gv)
        hbt_ref[...] = h.astype(bf16).T
        dyv = dy_ref[...]
        dyh = (0.5 * dyv).astype(bf16)
        dyh_ref[...] = dyh
        dh = jnp.zeros((tm, d), f32)
        dus = [_nt(dyh, w2_ref[j]) for j in range(nj)]
        for j in range(nj):
            av = a_ref[j].astype(f32)
            bv = b_ref[j].astype(f32)
            s = jax.nn.sigmoid(av)
            silu = av * s
            ut_ref[j] = (silu * bv).astype(bf16).T
            du = dus[j]
            dab = (du * bv * (s * (1.0 + av * (1.0 - s)))).astype(bf16)
            dbb = (du * silu).astype(bf16)
            da_ref[j] = dab
            db_ref[j] = dbb
            dh = dh + _nt(dab, w1_ref[j]) + _nt(dbb, w3_ref[j])
        dx, dg = _rms_bwd(dh, xn, r, gv)
        dx_ref[...] = dyv + dx

        @pl.when(pl.program_id(0) == 0)
        def _():
            dg_ref[...] = jnp.zeros_like(dg_ref)
        dg_ref[...] += dg

    row = pl.BlockSpec((tm, d), lambda i: (i, 0))
    vec = pl.BlockSpec((1, d), lambda i: (0, 0))
    mid = pl.BlockSpec((nj, tm, ff), lambda i: (0, i, 0))
    mid_shape = jax.ShapeDtypeStruct((nj, t, ff), bf16)
    return pl.pallas_call(
        body, name=name, grid=(t // tm,),
        in_specs=[row, row, vec, ANY_SPEC, ANY_SPEC, ANY_SPEC, mid, mid],
        out_specs=[row, vec, pl.BlockSpec((d, tm), lambda i: (0, i)), row,
                   pl.BlockSpec((nj, ff, tm), lambda i: (0, 0, i)), mid, mid],
        out_shape=[jax.ShapeDtypeStruct((t, d), f32), jax.ShapeDtypeStruct((1, d), f32),
                   jax.ShapeDtypeStruct((d, t), bf16), jax.ShapeDtypeStruct((t, d), bf16),
                   jax.ShapeDtypeStruct((nj, ff, t), bf16), mid_shape, mid_shape],
        scratch_shapes=_ffn_weight_scratch(nj, d, ff),
        compiler_params=_cp("arbitrary"),
    )(dy, x, g, w1, w3, w2, a, b)


def _ffn_dw_one(name, xt, y, buf, l, h, tt=2048):
    t = y.shape[-2]
    tt = min(tt, t)
    cut_cols = xt.ndim == 2

    def body(x_ref, y_ref, buf_ref, o_ref, ob_ref, acc):
        s = pl.program_id(1)
        prod = _nn(x_ref[0] if xt.ndim == 3 else x_ref[...], y_ref[0] if y.ndim == 3 else y_ref[...])

        @pl.when(s == 0)
        def _():
            acc[...] = prod

        @pl.when(s > 0)
        def _():
            acc[...] += prod

        @pl.when(s == pl.num_programs(1) - 1)
        def _():
            total = acc[...]
            for e in range(2):
                lo = e * FF_PAD
                part = total[:, lo:lo + FF_SHARD] if cut_cols else total[lo:lo + FF_SHARD, :]
                o_ref[e] = part
                ob_ref[e] = part.astype(bf16)

    x_spec = (pl.BlockSpec((1, xt.shape[1], tt), lambda p, s: (p, 0, s)) if xt.ndim == 3
              else pl.BlockSpec((xt.shape[0], tt), lambda p, s: (0, s)))
    y_spec = (pl.BlockSpec((1, tt, y.shape[2]), lambda p, s: (p, s, 0)) if y.ndim == 3
              else pl.BlockSpec((tt, y.shape[1]), lambda p, s: (s, 0)))
    k_, n_ = buf.shape[-2:]
    return pl.pallas_call(
        body, name=name, grid=(N_DEV // 2, t // tt),
        in_specs=[x_spec, y_spec, ANY_SPEC],
        out_specs=[pl.BlockSpec((2, None, None, k_, n_), lambda p, s: (p, l, h, 0, 0)),
                   pl.BlockSpec((2, k_, n_), lambda p, s: (p, 0, 0))],
        out_shape=[jax.ShapeDtypeStruct(buf.shape, buf.dtype), jax.ShapeDtypeStruct((N_DEV, k_, n_), bf16)],
        input_output_aliases={2: 0},
        scratch_shapes=[pltpu.VMEM((xt.shape[-2], y.shape[-1]), f32)],
        compiler_params=_cp("parallel", "arbitrary"),
    )(xt, y, buf)


_LOG_GAMMA = [float(np.log1p(-np.float32(2.0) ** np.float32(-5.0 - h))) for h in range(RET_HEADS)]


def _ret_consts(h):
    lg = jnp.where(h == 0, _LOG_GAMMA[0], jnp.where(h == 1, _LOG_GAMMA[1],
                   jnp.where(h == 2, _LOG_GAMMA[2], _LOG_GAMMA[3]))).astype(f32)
    c = RET_CHUNK
    r = lax.broadcasted_iota(jnp.int32, (c, c), 0)
    cc = lax.broadcasted_iota(jnp.int32, (c, c), 1)
    decay = jnp.where(r >= cc, jnp.exp(lg * jnp.maximum((r - cc).astype(f32), 0.0)), 0.0)
    pos = lax.broadcasted_iota(jnp.int32, (c, 1), 0).astype(f32)
    kd = jnp.exp(lg * (c - 1.0 - pos))
    qd = jnp.exp(lg * (pos + 1.0))
    gc = jnp.exp(lg * c)
    return decay, kd, qd, gc


def _rope(x, cos, sin):
    return x * cos + pltpu.roll(x, HEAD_DIM // 2, 1) * sin


def _rope_t(g, cos, sin):
    return g * cos + pltpu.roll(g * sin, HEAD_DIM // 2, 1)


def _rope_tables(s):
    half = HEAD_DIM // 2
    inv = ROPE_BASE ** (-jnp.arange(half, dtype=f32) / half)
    ang = jnp.arange(s, dtype=f32)[:, None] * inv[None, :]
    cos, sin = jnp.cos(ang), jnp.sin(ang)
    return jnp.concatenate([cos, cos], axis=1), jnp.concatenate([-sin, sin], axis=1)


def _head_ln(o):
    mu = jnp.mean(o, axis=-1, keepdims=True)
    oc = o - mu
    rs = lax.rsqrt(jnp.mean(oc * oc, axis=-1, keepdims=True) + EPS)
    return oc * rs, rs


def _ret_fwd(proj, cos, sin, ret_g, nb, s):
    c = RET_CHUNK
    nc = s // c
    t = nb * s
    scale = HEAD_DIM ** -0.5

    def body(q_ref, k_ref, v_ref, gate_ref, cos_ref, sin_ref, g_ref, o_ref, rprev_ref, m_ref):
        decay, kd, qd, gc = _ret_consts(pl.program_id(1))
        gv = g_ref[...]

        def step(n, rv):
            rows = pl.ds(pl.multiple_of(n * c, c), c)
            cs, sn = cos_ref[rows, :], sin_ref[rows, :]
            q = _rope(q_ref[rows, :], cs, sn)
            k = _rope(k_ref[rows, :], cs, sn) * scale
            vb = v_ref[rows, :].astype(bf16)
            sc = _nt(q.astype(bf16), k.astype(bf16)) * decay
            rprev_ref[n] = rv
            o = _nn(sc.astype(bf16), vb) + _nn((q * qd).astype(bf16), rv.astype(bf16))
            o_ref[rows, :] = o
            y, _ = _head_ln(o)
            gate = gate_ref[rows, :]
            m_ref[rows, :] = y * gv * (gate * jax.nn.sigmoid(gate))
            return rv * gc + _tn((k * kd).astype(bf16), vb)

        lax.fori_loop(0, nc, step, jnp.zeros((HEAD_DIM, HEAD_DIM), f32))

    def col(off):
        return pl.BlockSpec((s, HEAD_DIM), lambda b, h: (b, off + h))

    tab = pl.BlockSpec((s, HEAD_DIM), lambda b, h: (0, 0))
    return pl.pallas_call(
        body, name="ret_fwd", grid=(nb, RET_HEADS),
        in_specs=[col(0), col(4), col(8), col(12), tab, tab, pl.BlockSpec((1, HEAD_DIM), lambda b, h: (0, h))],
        out_specs=[col(0), pl.BlockSpec((nc, HEAD_DIM, HEAD_DIM), lambda b, h: (b * RET_HEADS + h, 0, 0)), col(0)],
        out_shape=[jax.ShapeDtypeStruct((t, RET_WIDTH), f32),
                   jax.ShapeDtypeStruct((nb * RET_HEADS * nc, HEAD_DIM, HEAD_DIM), f32),
                   jax.ShapeDtypeStruct((t, RET_WIDTH), f32)],
        compiler_params=_cp("parallel", "parallel"),
    )(proj, proj, proj, proj, cos, sin, ret_g)


def _ret_bwd(dmerged, o_raw, rprev, proj, cos, sin, ret_g, nb, s):
    c = RET_CHUNK
    nc = s // c
    t = nb * s
    scale = HEAD_DIM ** -0.5

    def body(dm_ref, o_ref, rprev_ref, q_ref, k_ref, v_ref, gate_ref, cos_ref, sin_ref, g_ref,
             dq_ref, dk_ref, dv_ref, dgate_ref, dg_ref):
        @pl.when(pl.program_id(1) == 0)
        def _():
            dg_ref[...] = jnp.zeros_like(dg_ref)
        decay, kd, qd, gc = _ret_consts(pl.program_id(0))
        gv = g_ref[...]

        def step(i, carry):
            drn, dg = carry
            n = nc - 1 - i
            rows = pl.ds(pl.multiple_of(n * c, c), c)
            cs, sn = cos_ref[rows, :], sin_ref[rows, :]
            q = _rope(q_ref[rows, :], cs, sn)
            k = _rope(k_ref[rows, :], cs, sn) * scale
            qb, kb = q.astype(bf16), k.astype(bf16)
            vb = v_ref[rows, :].astype(bf16)
            sc = _nt(qb, kb) * decay
            y, rs = _head_ln(o_ref[rows, :])
            gate = gate_ref[rows, :]
            sg = jax.nn.sigmoid(gate)
            silu = gate * sg
            dm = dm_ref[rows, :]
            dgate_ref[rows, :] = dm * y * gv * (sg * (1.0 + gate * (1.0 - sg)))
            dyl = dm * gv * silu
            dg = dg + jnp.sum(dm * y * silu, axis=0, keepdims=True)
            do = rs * (dyl - jnp.mean(dyl, axis=-1, keepdims=True) - y * jnp.mean(dyl * y, axis=-1, keepdims=True))
            dob = do.astype(bf16)
            rv = rprev_ref[n]
            drb = drn.astype(bf16)
            ds = (_nt(dob, vb) * decay).astype(bf16)
            kdb = (k * kd).astype(bf16)
            qdb = (q * qd).astype(bf16)
            dq_r = _nn(ds, kb) + _nt(dob, rv.astype(bf16)) * qd
            dk_r = _tn(ds, qb) + _nt(vb, drb) * kd
            dv_ref[rows, :] = _tn(sc.astype(bf16), dob) + _nn(kdb, drb)
            dq_ref[rows, :] = _rope_t(dq_r, cs, sn)
            dk_ref[rows, :] = _rope_t(dk_r * scale, cs, sn)
            return drn * gc + _tn(qdb, dob), dg

        _, dg = lax.fori_loop(0, nc, step, (jnp.zeros((HEAD_DIM, HEAD_DIM), f32), jnp.zeros((1, HEAD_DIM), f32)))
        dg_ref[...] += dg

    def col(off):
        return pl.BlockSpec((s, HEAD_DIM), lambda h, b: (b, off + h))

    tab = pl.BlockSpec((s, HEAD_DIM), lambda h, b: (0, 0))
    gsp = pl.BlockSpec((1, HEAD_DIM), lambda h, b: (0, h))
    out_t = jax.ShapeDtypeStruct((t, RET_WIDTH), f32)
    return pl.pallas_call(
        body, name="ret_bwd", grid=(RET_HEADS, nb),
        in_specs=[col(0), col(0), pl.BlockSpec((nc, HEAD_DIM, HEAD_DIM), lambda h, b: (b * RET_HEADS + h, 0, 0)),
                  col(0), col(4), col(8), col(12), tab, tab, gsp],
        out_specs=[col(0), col(0), col(0), col(0), gsp],
        out_shape=[out_t, out_t, out_t, out_t, jax.ShapeDtypeStruct((1, RET_WIDTH), f32)],
        compiler_params=_cp("parallel", "arbitrary"),
    )(dmerged, o_raw, rprev, proj, proj, proj, proj, cos, sin, ret_g)


def _neg_expm1(z):
    series = -(z * (1.0 + z * (0.5 + z * (1.0 / 6.0 + z * (1.0 / 24.0)))))
    return jnp.where(z > -0.01, series, 1.0 - jnp.exp(z))


def _lru_gates(xc, pa, pi, lam):
    r = jax.nn.sigmoid(pa)
    i = jax.nn.sigmoid(pi)
    log_a = -LRU_C * r * jax.nn.softplus(-lam)
    a = jnp.exp(log_a)
    bx = jnp.sqrt(_neg_expm1(2.0 * log_a)) * i * xc
    return a, bx


def _scan_rows(a, b, row, up):
    sub = row[:SUBLANES] & (SUBLANES - 1)
    groups = list(range(a.shape[0] // SUBLANES))
    out = [None] * len(groups)
    edge = slice(0, 1) if up else slice(SUBLANES - 1, SUBLANES)
    carry = jnp.zeros((1, a.shape[1]), f32)
    for g in (reversed(groups) if up else groups):
        rows = slice(g * SUBLANES, (g + 1) * SUBLANES)
        xa, xb = a[rows], b[rows]
        d = 1
        while d < SUBLANES:
            keep = (sub < SUBLANES - d) if up else (sub >= d)
            shift = SUBLANES - d if up else d
            xb = xa * jnp.where(keep, pltpu.roll(xb, shift, 0), 0.0) + xb
            xa = xa * jnp.where(keep, pltpu.roll(xa, shift, 0), 1.0)
            d *= 2
        out[g] = xb + xa * carry
        carry = out[g][edge]
    return jnp.concatenate(out, axis=0)


def _scan_fwd(a, b, row):
    return _scan_rows(a, b, row, False)


def _scan_bwd(c, b, row):
    return _scan_rows(c, b, row, True)


def _conv_fwd(x, cw, cb, row):
    return (cb + cw[3:4] * x + cw[2:3] * _shift_dn(x, 1, row) + cw[1:2] * _shift_dn(x, 2, row)
            + cw[0:1] * _shift_dn(x, 3, row))


def _lru_specs(s, order):
    def im(f):
        return (lambda b, g: f(b, g)) if order == "bg" else (lambda g, b: f(b, g))
    seq = lambda off: pl.BlockSpec((s, 128), im(lambda b, g: (b, off + g)))
    vec = pl.BlockSpec((1, 128), im(lambda b, g: (0, g)))
    cw = pl.BlockSpec((4, 128), im(lambda b, g: (0, g)))
    mat = pl.BlockSpec((1, 128, 128), im(lambda b, g: (g, 0, 0)))
    return seq, vec, cw, mat


def _lru_fwd(proj, conv_w, conv_b, w_a, b_a, w_i, b_i, lam, nb, s):
    def body(x_ref, gt_ref, cw_ref, cb_ref, wa_ref, ba_ref, wi_ref, bi_ref, lam_ref, out_ref):
        row = lax.broadcasted_iota(jnp.int32, (s, 128), 0)
        xc = _conv_fwd(x_ref[...], cw_ref[...], cb_ref[...], row)
        xcb = xc.astype(bf16)
        pa = _nn(xcb, wa_ref[0].astype(bf16)) + ba_ref[...]
        pi = _nn(xcb, wi_ref[0].astype(bf16)) + bi_ref[...]
        a, bx = _lru_gates(xc, pa, pi, lam_ref[...])
        h = _scan_fwd(a, bx, row)
        out_ref[...] = h * jax.nn.gelu(gt_ref[...])

    seq, vec, cw, mat = _lru_specs(s, "bg")
    return pl.pallas_call(
        body, name="lru_fwd", grid=(nb, LRU_BLOCKS),
        in_specs=[seq(16), seq(20), cw, vec, mat, vec, mat, vec, vec],
        out_specs=seq(0),
        out_shape=jax.ShapeDtypeStruct((nb * s, LRU_WIDTH), f32),
        compiler_params=_cp("parallel", "parallel"),
    )(proj, proj, conv_w, conv_b, w_a, b_a, w_i, b_i, lam)


def _lru_bwd(dmerged, proj, conv_w, conv_b, w_a, b_a, w_i, b_i, lam, nb, s):
    def body(dout_ref, x_ref, gt_ref, cw_ref, cb_ref, wa_ref, ba_ref, wi_ref, bi_ref, lam_ref,
             dx_ref, dgt_ref, dcw_ref, dcb_ref, dwa_ref, dba_ref, dwi_ref, dbi_ref, dlam_ref):
        row = lax.broadcasted_iota(jnp.int32, (s, 128), 0)
        x = x_ref[...]
        cwv = cw_ref[...]
        xc = _conv_fwd(x, cwv, cb_ref[...], row)
        xcb = xc.astype(bf16)
        wab, wib = wa_ref[0].astype(bf16), wi_ref[0].astype(bf16)
        pa = _nn(xcb, wab) + ba_ref[...]
        pi = _nn(xcb, wib) + bi_ref[...]
        (a, bx), gates_vjp = jax.vjp(_lru_gates, xc, pa, pi, lam_ref[...])
        h = _scan_fwd(a, bx, row)
        ge, gelu_vjp = jax.vjp(jax.nn.gelu, gt_ref[...])
        dout = dout_ref[...]
        dgt_ref[...] = gelu_vjp(dout * h)[0]
        adj = _scan_bwd(_shift_up(a, 1, row), dout * ge, row)
        dxc, dpa, dpi, dlam = gates_vjp((adj * _shift_dn(h, 1, row), adj))
        dpab, dpib = dpa.astype(bf16), dpi.astype(bf16)
        dxc = dxc + _nt(dpab, wab) + _nt(dpib, wib)
        dx_ref[...] = (cwv[3:4] * dxc + cwv[2:3] * _shift_up(dxc, 1, row) + cwv[1:2] * _shift_up(dxc, 2, row)
                       + cwv[0:1] * _shift_up(dxc, 3, row))

        @pl.when(pl.program_id(1) == 0)
        def _():
            for r in (dcw_ref, dcb_ref, dwa_ref, dba_ref, dwi_ref, dbi_ref, dlam_ref):
                r[...] = jnp.zeros_like(r)
        rsum = lambda v: jnp.sum(v, axis=0, keepdims=True)
        dcw_ref[...] += jnp.concatenate([rsum(dxc * _shift_dn(x, 3, row)), rsum(dxc * _shift_dn(x, 2, row)),
                                         rsum(dxc * _shift_dn(x, 1, row)), rsum(dxc * x)], axis=0)
        dcb_ref[...] += rsum(dxc)
        dwa_ref[0] += _tn(xcb, dpab)
        dwi_ref[0] += _tn(xcb, dpib)
        dba_ref[...] += rsum(dpa)
        dbi_ref[...] += rsum(dpi)
        dlam_ref[...] += dlam

    seq, vec, cw, mat = _lru_specs(s, "gb")
    t = nb * s
    vshape = jax.ShapeDtypeStruct((1, LRU_WIDTH), f32)
    mshape = jax.ShapeDtypeStruct((LRU_BLOCKS, 128, 128), f32)
    return pl.pallas_call(
        body, name="lru_bwd", grid=(LRU_BLOCKS, nb),
        in_specs=[seq(4), seq(16), seq(20), cw, vec, mat, vec, mat, vec, vec],
        out_specs=[seq(0), seq(0), cw, vec, mat, vec, mat, vec, vec],
        out_shape=[jax.ShapeDtypeStruct((t, LRU_WIDTH), f32), jax.ShapeDtypeStruct((t, LRU_WIDTH), f32),
                   jax.ShapeDtypeStruct((4, LRU_WIDTH), f32), vshape, mshape, vshape, mshape, vshape, vshape],
        compiler_params=_cp("parallel", "arbitrary"),
    )(dmerged, proj, proj, conv_w, conv_b, w_a, b_a, w_i, b_i, lam)


def _s5_disc(lr, li, ldt, bre, bim):
    dt = jnp.exp(ldt)
    mag = jnp.exp(lr * dt)
    lbr = mag * jnp.cos(li * dt)
    lbi = mag * jnp.sin(li * dt)
    den = lr * lr + li * li
    nr = lbr - 1.0
    fr = (nr * lr + lbi * li) / den
    fi = (lbi * lr - nr * li) / den
    bbr = fr[:, None, :] * bre - fi[:, None, :] * bim
    bbi = fr[:, None, :] * bim + fi[:, None, :] * bre
    return lbr, lbi, bbr, bbi


def _s5_prep(lr, li, ldt, bre, bim):
    def body(lr_ref, li_ref, ldt_ref, bre_ref, bim_ref, o1, o2, o3, o4):
        o1[...], o2[...], o3[...], o4[...] = _s5_disc(lr_ref[...], li_ref[...], ldt_ref[...], bre_ref[...], bim_ref[...])

    return pl.pallas_call(
        body, name="s5_prep", in_specs=[VMEM_SPEC] * 5, out_specs=[VMEM_SPEC] * 4,
        out_shape=[jax.ShapeDtypeStruct(lr.shape, f32), jax.ShapeDtypeStruct(lr.shape, f32),
                   jax.ShapeDtypeStruct(bre.shape, f32), jax.ShapeDtypeStruct(bre.shape, f32)],
    )(lr, li, ldt, bre, bim)


def _s5_prep_bwd(lr, li, ldt, bre, bim, cts):
    def body(lr_ref, li_ref, ldt_ref, bre_ref, bim_ref, g1, g2, g3, g4, o1, o2, o3, o4, o5):
        _, vjp = jax.vjp(_s5_disc, lr_ref[...], li_ref[...], ldt_ref[...], bre_ref[...], bim_ref[...])
        o1[...], o2[...], o3[...], o4[...], o5[...] = vjp((g1[...], g2[...], g3[...], g4[...]))

    return pl.pallas_call(
        body, name="s5_prep_bwd", in_specs=[VMEM_SPEC] * 9, out_specs=[VMEM_SPEC] * 5,
        out_shape=[jax.ShapeDtypeStruct(v.shape, f32) for v in (lr, li, ldt, bre, bim)],
    )(lr, li, ldt, bre, bim, *cts)


def _cmul(ar, ai, br, bi):
    return ar * br - ai * bi, ar * bi + ai * br


def _s5_pow_table(lr, li, n, row, up):
    ar = jnp.broadcast_to(lr, (n, lr.shape[1]))
    ai = jnp.broadcast_to(li, (n, li.shape[1]))
    shift = _shift_up if up else _shift_dn
    d = 1
    while d < n:
        ar, ai = _cmul(ar, ai, shift(ar, d, row, 1.0), shift(ai, d, row, 0.0))
        d *= 2
    return ar, ai


def _s5_step_factors(lr, li, row, up):
    sub = row & (SUBLANES - 1)
    out, pr, pi, d = [], lr, li, 1
    while d < SUBLANES:
        keep = (sub < SUBLANES - d) if up else (sub >= d)
        out.append((jnp.where(keep, pr, 0.0), jnp.where(keep, pi, 0.0)))
        pr, pi = _cmul(pr, pi, pr, pi)
        d *= 2
    return out


def _s5_scan(br, bi, steps, tab_r, tab_i, cr, ci, up):
    groups = list(range(br.shape[0] // SUBLANES))
    out_r, out_i = [None] * len(groups), [None] * len(groups)
    edge = slice(0, 1) if up else slice(SUBLANES - 1, SUBLANES)
    for g in (reversed(groups) if up else groups):
        rows = slice(g * SUBLANES, (g + 1) * SUBLANES)
        xr, xi = br[rows], bi[rows]
        for k, (mr, mi) in enumerate(steps):
            shift = SUBLANES - (1 << k) if up else 1 << k
            tr, ti = _cmul(mr, mi, pltpu.roll(xr, shift, 0), pltpu.roll(xi, shift, 0))
            xr, xi = xr + tr, xi + ti
        tr, ti = _cmul(tab_r, tab_i, cr, ci)
        hr, hi = xr + tr, xi + ti
        out_r[g], out_i[g] = hr, hi
        cr, ci = hr[edge], hi[edge]
    return jnp.concatenate(out_r, axis=0), jnp.concatenate(out_i, axis=0)


def _s5_specs(t, nb, nc):
    seq = pl.BlockSpec((t, 128), lambda k: (0, k))
    lvec = pl.BlockSpec((1, S5_BLOCK_STATES), lambda k: (0, k))
    dvec = pl.BlockSpec((1, 128), lambda k: (0, k))
    wmat = pl.BlockSpec((1, 128, S5_BLOCK_STATES), lambda k: (k, 0, 0))
    h0 = pl.BlockSpec((nb, None, nc, 2, S5_BLOCK_STATES), lambda k: (0, k, 0, 0, 0))
    return seq, lvec, dvec, wmat, h0


def _s5_fwd(u, lbr, lbi, wbr, wbi, wcr, wci, dskip, nb, s):
    ln = min(S5_CHUNK, s)
    nc = s // ln

    def body(u_ref, lr_ref, li_ref, wbr_ref, wbi_ref, wcr_ref, wci_ref, d_ref, yg_ref, y_ref, h0_ref):
        row = lax.broadcasted_iota(jnp.int32, (ln, S5_BLOCK_STATES), 0)
        lr, li = lr_ref[...], li_ref[...]
        pr, pi = _s5_pow_table(lr, li, SUBLANES, row[:SUBLANES], False)
        steps = _s5_step_factors(lr, li, row[:SUBLANES], False)
        dv = d_ref[...]

        def chunk(b, n, h0r, h0i):
            st = pl.multiple_of(b * s + n * ln, ln)
            uc = u_ref[pl.ds(st, ln), :]
            ub = uc.astype(bf16)
            hr, hi = _s5_scan(_nn(ub, wbr_ref[0]), _nn(ub, wbi_ref[0]), steps, pr, pi, h0r, h0i, False)
            h0_ref[b, n, 0:1, :] = h0r
            h0_ref[b, n, 1:2, :] = h0i
            y = _nt(hr.astype(bf16), wcr_ref[0]) - _nt(hi.astype(bf16), wci_ref[0]) + dv * uc
            y_ref[pl.ds(st, ln), :] = y
            yg_ref[pl.ds(st, ln), :] = jax.nn.gelu(y).astype(bf16)
            return hr[ln - 1:ln, :], hi[ln - 1:ln, :]

        def step(n, carry):
            return tuple(chunk(b, n, *carry[b]) for b in range(nb))

        z = jnp.zeros((1, S5_BLOCK_STATES), f32)
        lax.fori_loop(0, nc, step, ((z, z),) * nb)

    t = nb * s
    seq, lvec, dvec, wmat, h0 = _s5_specs(t, nb, nc)
    return pl.pallas_call(
        body, name="s5_fwd", grid=(S5_BLOCKS,),
        in_specs=[seq, lvec, lvec, wmat, wmat, wmat, wmat, dvec],
        out_specs=[seq, seq, h0],
        out_shape=[jax.ShapeDtypeStruct((t, D_MODEL), bf16), jax.ShapeDtypeStruct((t, D_MODEL), f32),
                   jax.ShapeDtypeStruct((nb, S5_BLOCKS, nc, 2, S5_BLOCK_STATES), f32)],
        compiler_params=_cp("parallel"),
    )(u, lbr, lbi, wbr, wbi, wcr, wci, dskip)


def _s5_bwd(dyg, y, u, h0, lbr, lbi, wbr, wbi, wcr, wci, dskip, nb, s):
    ln = min(S5_CHUNK, s)
    nc = s // ln

    def body(dyg_ref, y_ref, u_ref, h0_ref, lr_ref, li_ref, wbr_ref, wbi_ref, wcr_ref, wci_ref, d_ref,
             du_ref, dlr_ref, dli_ref, dwbr_ref, dwbi_ref, dwcr_ref, dwci_ref, dd_ref):
        for r in (dlr_ref, dli_ref, dwbr_ref, dwbi_ref, dwcr_ref, dwci_ref, dd_ref):
            r[...] = jnp.zeros_like(r)
        row = lax.broadcasted_iota(jnp.int32, (ln, S5_BLOCK_STATES), 0)
        lr, li = lr_ref[...], li_ref[...]
        pr, pi = _s5_pow_table(lr, li, SUBLANES, row[:SUBLANES], False)
        qr, qi = _s5_pow_table(lr, -li, SUBLANES, row[:SUBLANES], True)
        row8 = row[:SUBLANES]
        steps_dn, steps_up = _s5_step_factors(lr, li, row8, False), _s5_step_factors(lr, -li, row8, True)
        dv = d_ref[...]
        rsum = lambda v: jnp.sum(v, axis=0, keepdims=True)

        def chunk(b, n, gnr, gni):
            st = pl.multiple_of(b * s + n * ln, ln)
            uc = u_ref[pl.ds(st, ln), :]
            ub = uc.astype(bf16)
            h0v = h0_ref[b, n]
            h0r, h0i = h0v[0:1], h0v[1:2]
            hr, hi = _s5_scan(_nn(ub, wbr_ref[0]), _nn(ub, wbi_ref[0]), steps_dn, pr, pi, h0r, h0i, False)
            dy = jax.vjp(jax.nn.gelu, y_ref[pl.ds(st, ln), :])[1](dyg_ref[pl.ds(st, ln), :])[0]
            dyb = dy.astype(bf16)
            dd_ref[...] += rsum(dy * uc)
            gr, gi = _s5_scan(_nn(dyb, wcr_ref[0]), -_nn(dyb, wci_ref[0]), steps_up, qr, qi, gnr, gni, True)
            hpr = jnp.where(row >= 1, pltpu.roll(hr, 1, 0), h0r)
            hpi = jnp.where(row >= 1, pltpu.roll(hi, 1, 0), h0i)
            dlr_ref[...] += rsum(gr * hpr + gi * hpi)
            dli_ref[...] += rsum(gi * hpr - gr * hpi)
            grb, gib = gr.astype(bf16), gi.astype(bf16)
            dwbr_ref[0] += _tn(ub, grb)
            dwbi_ref[0] += _tn(ub, gib)
            dwcr_ref[0] += _tn(dyb, hr.astype(bf16))
            dwci_ref[0] -= _tn(dyb, hi.astype(bf16))
            du_ref[pl.ds(st, ln), :] = _nt(grb, wbr_ref[0]) + _nt(gib, wbi_ref[0]) + dv * dy
            return gr[0:1, :], gi[0:1, :]

        def step(i, carry):
            return tuple(chunk(b, nc - 1 - i, *carry[b]) for b in range(nb))

        z = jnp.zeros((1, S5_BLOCK_STATES), f32)
        lax.fori_loop(0, nc, step, ((z, z),) * nb)

    t = nb * s
    seq, lvec, dvec, wmat, h0s = _s5_specs(t, nb, nc)
    lshape = jax.ShapeDtypeStruct((1, S5_BLOCKS * S5_BLOCK_STATES), f32)
    wshape = jax.ShapeDtypeStruct((S5_BLOCKS, 128, S5_BLOCK_STATES), f32)
    return pl.pallas_call(
        body, name="s5_bwd", grid=(S5_BLOCKS,),
        in_specs=[seq, seq, seq, h0s, lvec, lvec, wmat, wmat, wmat, wmat, dvec],
        out_specs=[seq, lvec, lvec, wmat, wmat, wmat, wmat, dvec],
        out_shape=[jax.ShapeDtypeStruct((t, D_MODEL), f32), lshape, lshape, wshape, wshape, wshape, wshape,
                   jax.ShapeDtypeStruct((1, D_MODEL), f32)],
        compiler_params=_cp("parallel"),
    )(dyg, y, u, h0, lbr, lbi, wbr, wbi, wcr, wci, dskip)


def _blockdiag(w):
    w4 = w.reshape(S5_BLOCKS, 8, S5_GROUP, S5_STATE)
    same_group = jnp.eye(8, dtype=bool)[None, :, None, :, None]
    return jnp.where(same_group, w4[:, :, :, None, :], 0.0).reshape(S5_BLOCKS, 128, S5_BLOCK_STATES)


def _blockdiag_t(dw):
    d5 = dw.reshape(S5_BLOCKS, 8, S5_GROUP, 8, S5_STATE)
    diag = jnp.diagonal(d5, axis1=1, axis2=3)
    return jnp.moveaxis(diag, 3, 1).reshape(S5_GROUPS, S5_GROUP, S5_STATE)


def _glu_fwd(ygb, wa, wb, x, tm=512, tn=1024):
    t, d = x.shape

    def body(y_ref, wa_ref, wb_ref, x_ref, o_ref, p_ref, q_ref):
        p = _nn(y_ref[...], wa_ref[...])
        q = _nn(y_ref[...], wb_ref[...])
        p_ref[...] = p
        q_ref[...] = q
        o_ref[...] = x_ref[...] + p * jax.nn.sigmoid(q)

    tile = pl.BlockSpec((tm, tn), lambda i, j: (i, j))
    wsp = pl.BlockSpec((d, tn), lambda i, j: (0, j))
    out = jax.ShapeDtypeStruct((t, d), f32)
    return pl.pallas_call(
        body, name="glu_fwd", grid=(t // tm, d // tn),
        in_specs=[pl.BlockSpec((tm, d), lambda i, j: (i, 0)), wsp, wsp, tile],
        out_specs=[tile, tile, tile], out_shape=[out, out, out],
        compiler_params=_cp("parallel", "parallel"),
    )(ygb, wa, wb, x)


def _place():
    x, y, c = lax.axis_index("x"), lax.axis_index("y"), lax.axis_index("c")
    return x, y, c, [(1 - x, y), (x, 1 - y), (1 - x, 1 - y)]


def _all_gather(name, arrays):
    n = len(arrays)

    def body(*refs):
        ins, outs = refs[:n], refs[n:2 * n]
        send_sems, recv_sems, local_sems = refs[2 * n:]
        x, y, c, chips = _place()
        me, sib = (x, y, c), (x, y, 1 - c)

        def copy(i, k, block, to, src=None):
            dst = outs[i].at[4 * block[0] + 2 * block[1] + block[2]]
            return pltpu.make_async_remote_copy(
                src_ref=dst if src is None else src, dst_ref=dst,
                send_sem=send_sems.at[i * 7 + k], recv_sem=recv_sems.at[i * 7 + k],
                device_id=to, device_id_type=MESH)

        mine = [pltpu.make_async_copy(ins[i], outs[i].at[4 * x + 2 * y + c], local_sems.at[i]) for i in range(n)]
        for m in mine:
            m.start()
        first = []
        for i in range(n):
            first.append(copy(i, 0, me, sib, src=ins[i]))
            first += [copy(i, 1 + j, me, (*chip, c), src=ins[i]) for j, chip in enumerate(chips)]
        for cp in first:
            cp.start()
        passed = []
        for j, chip in enumerate(chips):
            for i in range(n):
                copy(i, 1 + j, (*chip, c), me).wait_recv()
                fwd = copy(i, 4 + j, (*chip, c), sib)
                fwd.start()
                passed.append(fwd)
        for i in range(n):
            copy(i, 0, sib, me).wait_recv()
        for j, chip in enumerate(chips):
            for i in range(n):
                copy(i, 4 + j, (*chip, 1 - c), me).wait_recv()
        for cp in first + passed:
            cp.wait_send()
        for m in mine:
            m.wait()

    return pl.pallas_call(
        body, name=name,
        in_specs=[ANY_SPEC] * n, out_specs=[ANY_SPEC] * n,
        out_shape=[jax.ShapeDtypeStruct((N_DEV,) + a.shape, a.dtype) for a in arrays],
        scratch_shapes=[pltpu.SemaphoreType.DMA((7 * n,)), pltpu.SemaphoreType.DMA((7 * n,)),
                        pltpu.SemaphoreType.DMA((n,))],
    )(*arrays)


def _tie(name, x, deps):
    def body(*refs):
        pass

    return pl.pallas_call(
        body, name=name, in_specs=[ANY_SPEC] * (1 + len(deps)), out_specs=ANY_SPEC,
        out_shape=jax.ShapeDtypeStruct(x.shape, x.dtype), input_output_aliases={0: 0},
    )(x, *deps)


def _xchg_copies(kind, srcs, lands, suffixes, send_sems, recv_sems):
    x, y, c, _ = _place()
    copies = []
    for i, (src, land, sfx) in enumerate(zip(srcs, lands, suffixes)):
        for k in range(N_DEV - 1):
            r = k + 1
            peer = (1 - x if r & 4 else x, 1 - y if r & 2 else y, 1 - c if r & 1 else c)
            if kind == "gather":
                s_ref, d_ref = src, land.at[(4 * x + 2 * y + c,) + sfx]
            else:
                s_ref, d_ref = src.at[4 * peer[0] + 2 * peer[1] + peer[2]], land.at[(k,) + sfx]
            copies.append(pltpu.make_async_remote_copy(
                src_ref=s_ref, dst_ref=d_ref, send_sem=send_sems.at[i * 7 + k], recv_sem=recv_sems.at[i * 7 + k],
                device_id=peer, device_id_type=MESH))
    return copies


def _xchg_start(name, kind, srcs, lands, suffixes=None):
    n = len(srcs)
    suffixes = suffixes or [()] * n

    def body(*refs):
        src, land = refs[:n], refs[n:2 * n]
        send_sems, recv_sems, token = refs[2 * n], refs[2 * n + 1], refs[-1]
        for cp in _xchg_copies(kind, src, land, suffixes, send_sems, recv_sems):
            cp.start()
        token[...] = jnp.zeros_like(token)

    arrays = list(srcs) + list(lands)
    outs = pl.pallas_call(
        body, name=name,
        out_shape=(pltpu.SemaphoreType.DMA((7 * n,)), pltpu.SemaphoreType.DMA((7 * n,)),
                   *[pltpu.HBM(a.shape, a.dtype) for a in arrays], jax.ShapeDtypeStruct((8, 128), f32)),
        in_specs=[HBM_SPEC] * (2 * n),
        out_specs=(SEM_SPEC, SEM_SPEC, *[HBM_SPEC] * (2 * n), VMEM_SPEC),
        input_output_aliases={i: 2 + i for i in range(2 * n)},
        compiler_params=pltpu.CompilerParams(has_side_effects=SIDE_EFFECT),
    )(*[pltpu.with_memory_space_constraint(a, pltpu.HBM) for a in arrays])
    return dict(kind=kind, n=n, suffixes=suffixes, send=outs[0], recv=outs[1], srcs=list(outs[2:2 + n]),
                lands=list(outs[2 + n:2 + 2 * n]), token=outs[-1])


def _xchg_wait(name, h, after, lands=None):
    n = h["n"]
    lands = h["lands"] if lands is None else lands

    def body(*refs):
        src, land = refs[:n], refs[n:2 * n]
        for cp in _xchg_copies(h["kind"], src, land, h["suffixes"], refs[2 * n], refs[2 * n + 1]):
            cp.wait_send()
            cp.wait_recv()

    arrays = h["srcs"] + list(lands)
    outs = pl.pallas_call(
        body, name=name,
        out_shape=tuple(pltpu.HBM(a.shape, a.dtype) for a in arrays),
        in_specs=[HBM_SPEC] * (2 * n) + [SEM_SPEC, SEM_SPEC] + [ANY_SPEC] * len(after),
        out_specs=tuple([HBM_SPEC] * (2 * n)),
        input_output_aliases={i: i for i in range(2 * n)},
        compiler_params=pltpu.CompilerParams(has_side_effects=SIDE_EFFECT),
    )(*arrays, h["send"], h["recv"], *after)
    return list(outs[n:])


def _rows(a):
    return a.reshape(-1, a.shape[-1])


def _row_tile(r):
    for tm in (512, 256, 128, 64, 32, 16, 8):
        if r % tm == 0:
            return tm
    return r


def _sum8(name, gathered):
    _, r, n = gathered.shape
    tm = _row_tile(r)

    def body(g_ref, o_ref):
        acc = g_ref[0]
        for k in range(1, N_DEV):
            acc = acc + g_ref[k]
        o_ref[...] = acc

    return pl.pallas_call(
        body, name=name, grid=(r // tm,),
        in_specs=[pl.BlockSpec((N_DEV, tm, n), lambda i: (0, i, 0))],
        out_specs=pl.BlockSpec((tm, n), lambda i: (i, 0)),
        out_shape=jax.ShapeDtypeStruct((r, n), f32),
        compiler_params=_cp("parallel"),
    )(gathered)


def _adamw(name, w, m, v, own, landed=None, slot=None):
    shape = w.shape
    w2, m2, v2 = _rows(w), _rows(m), _rows(v)
    r, n = w2.shape
    tm = _row_tile(r)
    c1 = 1.0 - ADAM_B1 ** ADAM_STEP
    c2 = 1.0 - ADAM_B2 ** ADAM_STEP
    extra = [] if landed is None else [landed.reshape(landed.shape[0], r, n)]
    row = pl.BlockSpec((tm, n), lambda i, *_: (i, 0))
    if slot is None:
        o2, own_spec, scalars = _rows(own), row, []
    else:
        dev, kind = slot
        scalars = [dev.reshape(1).astype(jnp.int32)]
        if kind == "lead":
            o2, own_spec = own.reshape(N_DEV, r, n), pl.BlockSpec((None, tm, n), lambda i, d: (d[0], i, 0))
        elif kind == "rows":
            o2, own_spec = own, pl.BlockSpec((tm, n), lambda i, d: (d[0] * (r // tm) + i, 0))
        else:
            o2, own_spec = own, pl.BlockSpec((tm, n), lambda i, d: (i, d[0]))

    def body(*refs):
        w_ref, m_ref, v_ref, o_ref = refs[len(scalars):len(scalars) + 4]
        refs = refs[len(scalars) + 4:]
        g = o_ref[...]
        if extra:
            for k in range(extra[0].shape[0]):
                g = g + refs[0][k].astype(f32)
        g_ref, d_ref, mn_ref, vn_ref = refs[len(extra):]
        mn = ADAM_B1 * m_ref[...] + (1.0 - ADAM_B1) * g
        vn = ADAM_B2 * v_ref[...] + (1.0 - ADAM_B2) * (g * g)
        g_ref[...] = g
        d_ref[...] = -ADAM_LR * ((mn / c1) / (jnp.sqrt(vn / c2) + ADAM_EPS) + ADAM_WD * w_ref[...])
        mn_ref[...] = mn
        vn_ref[...] = vn

    outs = pl.pallas_call(
        body, name=name,
        grid_spec=pltpu.PrefetchScalarGridSpec(
            num_scalar_prefetch=len(scalars), grid=(r // tm,),
            in_specs=[row] * 3 + [own_spec] + [pl.BlockSpec((e.shape[0], tm, n), lambda i, *_: (0, i, 0)) for e in extra],
            out_specs=[row] * 4),
        out_shape=[jax.ShapeDtypeStruct((r, n), f32)] * 4,
        compiler_params=_cp("parallel"),
    )(*scalars, w2, m2, v2, o2, *extra)
    return [o.reshape(shape) for o in outs]


def _pack(arrays):
    flat = jnp.concatenate([a.reshape(-1).astype(f32) for a in arrays])
    pad = (-flat.shape[0]) % (128 * (512 if flat.shape[0] > 128 * 512 else 8))
    return jnp.pad(flat, (0, pad)).reshape(-1, 128)


def _unpack(packed, shapes):
    flat = packed.reshape(-1)
    out, off = [], 0
    for s in shapes:
        n = math.prod(s)
        out.append(flat[off:off + n].reshape(s))
        off += n
    return out


def _local_step(x, target, w, weights_of, send, last_small, on_loss, nb, s):
    cos, sin = _rope_tables(s)
    g = {}
    ffn_saved = {}
    ffn_bufs = [lax.empty((N_DEV, 2, 2) + shp, f32)
                for shp in ((D_MODEL, FF_SHARD), (D_MODEL, FF_SHARD), (FF_SHARD, D_MODEL))]

    def ffn(xin, l, h, wts):
        y, a, b = _ffn_fwd(f"ffn_fwd_{l}{h}", xin, w["ffn_g"][l][h], *wts)
        ffn_saved[(l, h)] = (xin, a, b, wts)
        return y

    def ffn_back(dy, l, h):
        xin, a, b, wts = ffn_saved[(l, h)]
        dx, dg, hb, dyh, u, da, db = _ffn_dx(f"ffn_dx_{l}{h}", dy, xin, w["ffn_g"][l][h], *wts, a, b)
        g[f"ffn_g_{l}{h}"] = dg
        if (l, h) != (0, 0):
            halves = {}
            for k, (n, xt, y) in enumerate((("ffn_w1", hb, da), ("ffn_w3", hb, db), ("ffn_w2", u, dyh))):
                ffn_bufs[k], halves[n] = _ffn_dw_one(f"ffn_dw_{l}{h}_{n[4:]}", xt, y, ffn_bufs[k], l, h)
            return send(f"ffn_{l}{h}", halves, dx)
        hb = last_small(g, hb)
        ffn_bufs[0], half = _ffn_dw_one("ffn_dw_00_w1", hb, da, ffn_bufs[0], l, h)
        hb = send("ffn_00_w1", {"ffn_w1": half}, hb)
        ffn_bufs[1], half = _ffn_dw_one("ffn_dw_00_w3", hb, db, ffn_bufs[1], l, h)
        u = send("ffn_00_w3", {"ffn_w3": half}, u)
        ffn_bufs[2], half = _ffn_dw_one("ffn_dw_00_w2", u, dyh, ffn_bufs[2], l, h)
        return send("ffn_00_w2", {"ffn_w2": half}, dx)

    def slots(t):
        return t.reshape(N_DEV, D_MODEL // N_DEV, D_MODEL)

    x1 = ffn(x, 0, 0, weights_of(0, [])["ffn"])
    wg = weights_of(1, [x1])
    w_in, w_out = wg["w_in"], wg["w_out"]
    _, h0b = _norm_fwd("mix_norm_0", x1, w["mix_g"][0])
    proj = _mm("in_proj", h0b, w_in, "nn", tn=1536)[0]
    o_raw, rprev, mret = _ret_fwd(proj, cos, sin, w["ret_g"], nb, s)
    lru = _lru_fwd(proj, w["conv_w"], w["conv_b"], w["lru_w_a"], w["lru_b_a"], w["lru_w_i"], w["lru_b_i"], w["lru_lam"], nb, s)
    merged = _ew("merge", lambda a, b: (jnp.concatenate([a, b], axis=1),), [mret, lru], [(D_MODEL, bf16)])[0]
    x2 = _mm("out_proj", merged, w_out, "nn", extras=[x1], epilogue=lambda acc, r: (acc + r,))[0]
    x3 = ffn(x2, 0, 1, weights_of(2, [x2])["ffn"])
    x4 = ffn(x3, 1, 0, weights_of(3, [x3])["ffn"])
    u, _ = _norm_fwd("mix_norm_1", x4, w["mix_g"][1])
    lbr, lbi, bbr, bbi = _s5_prep(w["s5_lr"], w["s5_li"], w["s5_ldt"], w["s5_bre"], w["s5_bim"])
    lbr_f, lbi_f = lbr.reshape(1, -1), lbi.reshape(1, -1)
    wbr, wbi = _blockdiag(bbr).astype(bf16), _blockdiag(bbi).astype(bf16)
    wcr, wci = _blockdiag(w["s5_cre"]).astype(bf16), _blockdiag(w["s5_cim"]).astype(bf16)
    ygb, ypre, h0s = _s5_fwd(u, lbr_f, lbi_f, wbr, wbi, wcr, wci, w["s5_d"], nb, s)
    wg = weights_of(4, [ygb])
    glu_a, glu_b = wg["glu_a"], wg["glu_b"]
    x5, gp, gq = _glu_fwd(ygb, glu_a, glu_b, x4)
    x6 = ffn(x5, 1, 1, weights_of(5, [x5])["ffn"])
    loss, dx6, g["final_g"] = _final_loss(x6, w["final_g"], target)
    dx6 = on_loss(loss, dx6)

    dx5 = ffn_back(dx6, 1, 1)

    def glu_bwd(d, p, q):
        sg = jax.nn.sigmoid(q)
        return d * sg, d * p * sg * (1.0 - sg)

    dp, dq = _ew("glu_bwd", glu_bwd, [dx5, gp, gq], [(D_MODEL, bf16), (D_MODEL, bf16)])
    dyg = _mm("glu_dy_a", dp, glu_a, "nt")[0]
    dyg = _mm("glu_dy_b", dq, glu_b, "nt", extras=[dyg], epilogue=lambda acc, r: (acc + r,))[0]
    g["glu_a"], ga_half = _mm_tn("glu_dw_a", ygb, dp)
    g["glu_b"], gb_half = _mm_tn("glu_dw_b", ygb, dq)
    dyg = send("glu", {"glu_a": slots(ga_half), "glu_b": slots(gb_half)}, dyg)
    du, dlr, dli, dwbr, dwbi, dwcr, dwci, g["s5_d"] = _s5_bwd(dyg, ypre, u, h0s, lbr_f, lbi_f, wbr, wbi, wcr, wci, w["s5_d"], nb, s)
    g["s5_cre"], g["s5_cim"] = _blockdiag_t(dwcr), _blockdiag_t(dwci)
    g["s5_lr"], g["s5_li"], g["s5_ldt"], g["s5_bre"], g["s5_bim"] = _s5_prep_bwd(
        w["s5_lr"], w["s5_li"], w["s5_ldt"], w["s5_bre"], w["s5_bim"],
        (dlr.reshape(S5_GROUPS, S5_STATE), dli.reshape(S5_GROUPS, S5_STATE), _blockdiag_t(dwbr), _blockdiag_t(dwbi)))
    dx4, g["mix_g_1"] = _norm_bwd("mix_norm_1_bwd", du, x4, w["mix_g"][1], dx5)
    dx3 = ffn_back(dx4, 1, 0)
    dx2 = ffn_back(dx3, 0, 1)
    dmerged = _mm("out_proj_dx", dx2, w_out, "nt")[0]
    g["w_out"], wo_half = _mm_tn("out_proj_dw", merged, dx2)
    dmerged = send("w_out", {"w_out": slots(wo_half)}, dmerged)
    dq_, dk_, dv_, dgate, g["ret_g"] = _ret_bwd(dmerged, o_raw, rprev, proj, cos, sin, w["ret_g"], nb, s)
    (dxl, dgl, g["conv_w"], g["conv_b"], g["lru_w_a"], g["lru_b_a"], g["lru_w_i"], g["lru_b_i"], g["lru_lam"]) = _lru_bwd(
        dmerged, proj, w["conv_w"], w["conv_b"], w["lru_w_a"], w["lru_b_a"], w["lru_w_i"], w["lru_b_i"], w["lru_lam"], nb, s)
    dproj = _ew("dproj", lambda *p: (jnp.concatenate(p, axis=1),), [dq_, dk_, dv_, dgate, dxl, dgl], [(3072, bf16)])[0]
    dh0 = _mm("in_proj_dx", dproj, w_in, "nt")[0]
    g["w_in"], wi_half = _mm_tn("in_proj_dw", h0b, dproj)
    dh0 = send("w_in", {"w_in": jnp.transpose(wi_half.reshape(D_MODEL, N_DEV, IN_SHARD), (1, 0, 2))}, dh0)
    dx1, g["mix_g_0"] = _norm_bwd("mix_norm_0_bwd", dh0, x1, w["mix_g"][0], dx2)
    dx0 = ffn_back(dx1, 0, 0)
    g["ffn_w1"], g["ffn_w3"], g["ffn_w2"] = ffn_bufs
    return loss, dx0, g


_WEIGHTS = ["ffn_norm_g", "ffn_w1", "ffn_w3", "ffn_w2", "mix_norm_g", "w_in_even", "w_out_even", "ret_norm_g", "conv_w",
            "conv_b", "lru_w_a", "lru_b_a", "lru_w_i", "lru_b_i", "lru_lambda", "s5_lambda_re", "s5_lambda_im", "s5_log_dt",
            "s5_b_re", "s5_b_im", "s5_c_re", "s5_c_im", "s5_d", "glu_w_a", "glu_w_b", "final_norm_g"]
_BIG = ["ffn_w1", "ffn_w3", "ffn_w2", "w_in_even", "w_out_even", "glu_w_a", "glu_w_b"]
_SMALL_SHARDED = ["ffn_norm_g", "conv_w", "s5_d"]
_SMALL = [n for n in _WEIGHTS if n not in _BIG]
_MIDSIZE = ["lru_w_a", "lru_w_i", "s5_b_re", "s5_b_im", "s5_c_re", "s5_c_im"]


def kernel(x, ffn_norm_g, ffn_w1, ffn_w3, ffn_w2, mix_norm_g, w_in_even, w_out_even, ret_norm_g, conv_w, conv_b, lru_w_a, lru_b_a, lru_w_i, lru_b_i, lru_lambda, s5_lambda_re, s5_lambda_im, s5_log_dt, s5_b_re, s5_b_im, s5_c_re, s5_c_im, s5_d, glu_w_a, glu_w_b, final_norm_g, loss_target, m_ffn_norm_g, m_ffn_w1, m_ffn_w3, m_ffn_w2, m_mix_norm_g, m_w_in_even, m_w_out_even, m_ret_norm_g, m_conv_w, m_conv_b, m_lru_w_a, m_lru_b_a, m_lru_w_i, m_lru_b_i, m_lru_lambda, m_s5_lambda_re, m_s5_lambda_im, m_s5_log_dt, m_s5_b_re, m_s5_b_im, m_s5_c_re, m_s5_c_im, m_s5_d, m_glu_w_a, m_glu_w_b, m_final_norm_g, v_ffn_norm_g, v_ffn_w1, v_ffn_w3, v_ffn_w2, v_mix_norm_g, v_w_in_even, v_w_out_even, v_ret_norm_g, v_conv_w, v_conv_b, v_lru_w_a, v_lru_b_a, v_lru_w_i, v_lru_b_i, v_lru_lambda, v_s5_lambda_re, v_s5_lambda_im, v_s5_log_dt, v_s5_b_re, v_s5_b_im, v_s5_c_re, v_s5_c_im, v_s5_d, v_glu_w_a, v_glu_w_b, v_final_norm_g):
    a = dict(locals())
    nb, s, d = x.shape
    ax, ay, ac = lax.axis_index("x"), lax.axis_index("y"), lax.axis_index("c")
    dev = 4 * ax + 2 * ay + ac
    chip = 2 * ax + ay

    def ffn_shards(l, h):
        extra = FF_PAD - FF_SHARD
        return [jnp.pad(ffn_w1[l, h].astype(bf16), ((0, 0), (0, extra))), jnp.pad(ffn_w3[l, h].astype(bf16), ((0, 0), (0, extra))),
                jnp.pad(ffn_w2[l, h].astype(bf16), ((0, extra), (0, 0)))]

    first = _all_gather("ag_first", ffn_shards(0, 0) + [_pack([ffn_norm_g, conv_w, s5_d])])
    sm = first[3].reshape(N_DEV, -1)
    ffn_g_full = jnp.transpose(sm[:, :512].reshape(N_DEV, 2, 2, 128), (1, 2, 0, 3)).reshape(2, 2, D_MODEL)
    conv_w_full = jnp.transpose(sm[:, 512:768].reshape(N_DEV, 4, 64), (1, 0, 2)).reshape(4, LRU_WIDTH)
    s5_d_full = sm[:, 768:896].reshape(1, D_MODEL)

    ag_src = [None, [w_in_even[0].astype(bf16), w_out_even[0].astype(bf16)], ffn_shards(0, 1), ffn_shards(1, 0),
              [glu_w_a[0].astype(bf16), glu_w_b[0].astype(bf16)], ffn_shards(1, 1)]
    ag, token = [None], first[0]
    for k, grp in enumerate(ag_src):
        if grp is None:
            continue
        grp[0] = _tie(f"tie_ag_{k}", grp[0], [token])
        lands = [lax.dynamic_update_index_in_dim(lax.empty((N_DEV,) + t.shape, bf16), t, dev, 0) for t in grp]
        ag.append(_xchg_start(f"ag_start_{k}", "gather", grp, lands))
        token = ag[-1]["token"]

    def weights_of(k, after):
        if k == 0:
            return {"ffn": [first[0], _tie("tie_ag_started", first[1], [h["token"] for h in ag[1:]]), first[2]]}
        got = _xchg_wait(f"ag_wait_{k}", ag[k], after)
        if k == 1:
            return {"w_in": jnp.transpose(got[0], (1, 0, 2)).reshape(D_MODEL, N_DEV * IN_SHARD),
                    "w_out": got[1].reshape(D_MODEL, D_MODEL)}
        if k == 4:
            return {"glu_a": got[0].reshape(D_MODEL, D_MODEL), "glu_b": got[1].reshape(D_MODEL, D_MODEL)}
        return {"ffn": got}

    ffn_lands = [lax.empty((N_DEV - 1, 2, 2) + shp, bf16)
                 for shp in ((D_MODEL, FF_SHARD), (D_MODEL, FF_SHARD), (FF_SHARD, D_MODEL))]
    rs = []

    ffn_names = ("ffn_w1", "ffn_w3", "ffn_w2")

    def send(group, arrays, carry):
        srcs = list(arrays.values())
        if group.startswith("ffn_"):
            which = [ffn_names.index(n) for n in arrays]
            sfx = [(int(group[4]), int(group[5]))] * len(which)
            h = _xchg_start("rs_start_" + group, "scatter", srcs, [ffn_lands[k] for k in which], sfx)
            for k, land in zip(which, h["lands"]):
                ffn_lands[k] = land
        else:
            h = _xchg_start("rs_start_" + group, "scatter", srcs,
                            [lax.empty((N_DEV - 1,) + t.shape[1:], bf16) for t in srcs])
        rs.append((group, list(arrays), h))
        return _tie("tie_" + group, carry, [h["token"]])

    w = {
        "ffn_g": [[ffn_g_full[l, h].reshape(1, D_MODEL) for h in range(2)] for l in range(2)],
        "mix_g": [mix_norm_g[0:1], mix_norm_g[1:2]],
        "ret_g": ret_norm_g, "conv_w": conv_w_full, "conv_b": conv_b,
        "lru_w_a": lru_w_a[0], "lru_b_a": lru_b_a, "lru_w_i": lru_w_i[0], "lru_b_i": lru_b_i, "lru_lam": lru_lambda,
        "s5_lr": s5_lambda_re[0], "s5_li": s5_lambda_im[0], "s5_ldt": s5_log_dt.reshape(S5_GROUPS, 1),
        "s5_bre": jnp.swapaxes(s5_b_re[0], 1, 2), "s5_bim": jnp.swapaxes(s5_b_im[0], 1, 2),
        "s5_cre": s5_c_re[0], "s5_cim": s5_c_im[0], "s5_d": s5_d_full,
        "final_g": final_norm_g.reshape(1, D_MODEL),
    }

    small_grads = {}

    def last_small(g, carry):
        part = _small_partials(g)
        mine = _pack([part[n] for n in _SMALL])
        land = lax.dynamic_update_index_in_dim(lax.empty((N_DEV,) + mine.shape, f32), mine, dev, 0)
        h = _xchg_start("ag_start_small_grads", "gather", [mine], [land])
        small_grads.update(h=h, shapes=[part[n].shape for n in _SMALL])
        return _tie("tie_small_grads", carry, [h["token"]])

    total_loss = []

    def on_loss(part, carry):
        total_loss.append(lax.psum(part[0, 0], ("x", "y", "c")))
        return _tie("tie_loss", carry, [jnp.broadcast_to(total_loss[0], (8, 128))])

    _, dx, g = _local_step(x.reshape(nb * s, d), loss_target.reshape(nb * s, d), w, weights_of, send, last_small,
                           on_loss, nb, s)
    loss = total_loss[0]
    (gath,) = _xchg_wait("ag_wait_small_grads", small_grads["h"], [dx])
    full = dict(zip(_SMALL, _unpack(_sum8("sum_small_grads", gath), small_grads["shapes"])))
    for n in _SMALL_SHARDED:
        width = a[n].shape[-1]
        full[n] = lax.dynamic_slice_in_dim(full[n], dev * width, width, axis=full[n].ndim - 1)
    res = {n: _adamw("adamw_" + n, a[n], a["m_" + n], a["v_" + n], full[n]) for n in _MIDSIZE}
    tiny = [n for n in _SMALL if n not in _MIDSIZE]
    shapes = [a[n].shape for n in tiny]
    packed = _adamw("adamw_small", _pack([a[n] for n in tiny]), _pack([a["m_" + n] for n in tiny]),
                    _pack([a["v_" + n] for n in tiny]), _pack([full[n] for n in tiny]))
    res.update({n: vals for n, vals in zip(tiny, zip(*[_unpack(p, shapes) for p in packed]))})
    return _finish(a, g, dx, loss, res, packed, rs, ffn_lands, dev, nb, s, d)


def _small_partials(g):
    return {
        "ffn_norm_g": jnp.stack([jnp.stack([g[f"ffn_g_{l}{h}"][0] for h in range(2)]) for l in range(2)]),
        "mix_norm_g": jnp.concatenate([g["mix_g_0"], g["mix_g_1"]], axis=0),
        "ret_norm_g": g["ret_g"], "conv_w": g["conv_w"][None], "conv_b": g["conv_b"],
        "lru_w_a": g["lru_w_a"][None], "lru_b_a": g["lru_b_a"], "lru_w_i": g["lru_w_i"][None], "lru_b_i": g["lru_b_i"],
        "lru_lambda": g["lru_lam"], "s5_lambda_re": g["s5_lr"][None], "s5_lambda_im": g["s5_li"][None],
        "s5_log_dt": g["s5_ldt"].reshape(1, S5_GROUPS),
        "s5_b_re": jnp.swapaxes(g["s5_bre"], 1, 2)[None], "s5_b_im": jnp.swapaxes(g["s5_bim"], 1, 2)[None],
        "s5_c_re": g["s5_cre"][None], "s5_c_im": g["s5_cim"][None], "s5_d": g["s5_d"], "final_norm_g": g["final_g"][0],
    }


def _finish(a, g, dx, loss, res, packed, rs, ffn_lands, dev, nb, s, d):
    landed = {}
    for group, names, h in rs:
        if not group.startswith("ffn_"):
            landed.update(zip(names, _xchg_wait("rs_wait_" + group, h, [dx])))
    kinds = {"ffn_w1": "lead", "ffn_w3": "lead", "ffn_w2": "lead", "w_in": "cols", "w_out": "rows", "glu_a": "rows",
             "glu_b": "rows"}

    def update(n, short):
        res[n] = _adamw("adamw_" + n, a[n], a["m_" + n], a["v_" + n], g[short],
                        landed[short].reshape((N_DEV - 1,) + a[n].shape), slot=(dev, kinds[short]))

    for n, short in zip(_BIG[3:], ("w_in", "w_out", "glu_a", "glu_b")):
        update(n, short)
    after = [dx, packed[0]] + [res[n][0] for n in _BIG[3:] + _MIDSIZE]
    ffn_names = ("ffn_w1", "ffn_w3", "ffn_w2")
    for group, names, h in rs:
        if group.startswith("ffn_") and len(names) == 3:
            ffn_lands[:] = _xchg_wait("rs_wait_" + group, h, after, ffn_lands)
    for k, n in enumerate(ffn_names):
        for group, names, h in rs:
            if group.startswith("ffn_") and names == [n]:
                (ffn_lands[k],) = _xchg_wait("rs_wait_" + group, h, after, [ffn_lands[k]])
        landed[n] = ffn_lands[k]
        update(n, n)
        after = after + [res[n][0]]

    out = [loss, dx.reshape(nb, s, d)]
    for k in range(4):
        out += [res[n][k] for n in _WEIGHTS]
    return tuple(out)
```

```python
import functools
import math

import numpy as np
import jax
import jax.numpy as jnp
from jax import lax
from jax.experimental import pallas as pl
from jax.experimental.pallas import tpu as pltpu

f32 = jnp.float32
bf16 = jnp.bfloat16

D_MODEL = 1024
N_DEV = 8
EPS = 1e-6
RET_HEADS = 4
HEAD_DIM = 128
RET_WIDTH = 512
RET_CHUNK = 128
ROPE_BASE = 10000.0
LRU_WIDTH = 512
LRU_BLOCKS = 4
LRU_C = 8.0
S5_GROUP = 16
S5_GROUPS = 64
S5_STATE = 64
S5_CHUNK = 1024
S5_BLOCKS = 8
S5_BLOCK_STATES = 512
SUBLANES = 8
D_FF = 2816
FF_SHARD = D_FF // N_DEV
FF_PAD = 384
IN_SHARD = 3072 // N_DEV
ADAM_LR = 0.001
ADAM_B1 = 0.9
ADAM_B2 = 0.999
ADAM_EPS = 1e-08
ADAM_WD = 0.01
ADAM_STEP = 10

VMEM_LIMIT = 56 * 1024 * 1024
VMEM_SPEC = pl.BlockSpec(memory_space=pltpu.VMEM)
ANY_SPEC = pl.BlockSpec(memory_space=pl.ANY)
HBM_SPEC = pl.BlockSpec(memory_space=pltpu.HBM)
SEM_SPEC = pl.BlockSpec(memory_space=pltpu.SEMAPHORE)
SIDE_EFFECT = pltpu.SideEffectType.DATAFLOW_SIDE_EFFECTING
MESH = pl.DeviceIdType.MESH


def _cp(*sem):
    return pltpu.CompilerParams(dimension_semantics=sem, vmem_limit_bytes=VMEM_LIMIT)


def _nn(a, b):
    return jnp.dot(a, b, preferred_element_type=f32)


def _nt(a, b):
    return lax.dot_general(a, b, (((1,), (1,)), ((), ())), preferred_element_type=f32)


def _tn(a, b):
    return lax.dot_general(a, b, (((0,), (0,)), ((), ())), preferred_element_type=f32)


def _rms_fwd(x, g):
    r = lax.rsqrt(jnp.mean(x * x, axis=-1, keepdims=True) + EPS)
    xn = x * r
    return xn * g, xn, r


def _rms_bwd(dh, xn, r, g):
    dxn = dh * g
    dx = r * (dxn - xn * jnp.mean(dxn * xn, axis=-1, keepdims=True))
    dg = jnp.sum(dh * xn, axis=0, keepdims=True)
    return dx, dg


def _shift_dn(v, d, row, fill=0.0):
    return jnp.where(row >= d, pltpu.roll(v, d, 0), fill)


def _shift_up(v, d, row, fill=0.0):
    n = v.shape[0]
    return jnp.where(row < n - d, pltpu.roll(v, n - d, 0), fill)


def _ew(name, fn, ins, outs, tm=512):
    t = ins[0].shape[0]
    n_in = len(ins)

    def body(*refs):
        res = fn(*[r[...] for r in refs[:n_in]])
        for o, v in zip(refs[n_in:], res):
            o[...] = v.astype(o.dtype)

    return pl.pallas_call(
        body, name=name, grid=(t // tm,),
        in_specs=[pl.BlockSpec((tm, a.shape[1]), lambda i: (i, 0)) for a in ins],
        out_specs=[pl.BlockSpec((tm, n), lambda i: (i, 0)) for n, _ in outs],
        out_shape=[jax.ShapeDtypeStruct((t, n), dt) for n, dt in outs],
        compiler_params=_cp("parallel"),
    )(*ins)


def _mm(name, x, w, kind, extras=(), epilogue=None, outs=None, tm=512, tn=1024):
    t = x.shape[0]
    n = w.shape[1] if kind == "nn" else w.shape[0]
    tn = min(tn, n)
    outs = outs or [f32]
    n_ex = len(extras)

    def body(x_ref, w_ref, *refs):
        xb = x_ref[...].astype(bf16)
        acc = _nn(xb, w_ref[...]) if kind == "nn" else _nt(xb, w_ref[...])
        res = epilogue(acc, *[r[...] for r in refs[:n_ex]]) if epilogue else (acc,)
        for o, v in zip(refs[n_ex:], res):
            o[...] = v.astype(o.dtype)

    w_spec = (pl.BlockSpec((w.shape[0], tn), lambda i, j: (0, j)) if kind == "nn"
              else pl.BlockSpec((tn, w.shape[1]), lambda i, j: (j, 0)))
    tile = pl.BlockSpec((tm, tn), lambda i, j: (i, j))
    return pl.pallas_call(
        body, name=name, grid=(t // tm, n // tn),
        in_specs=[pl.BlockSpec((tm, x.shape[1]), lambda i, j: (i, 0)), w_spec] + [tile] * n_ex,
        out_specs=[tile] * len(outs),
        out_shape=[jax.ShapeDtypeStruct((t, n), dt) for dt in outs],
        compiler_params=_cp("parallel", "parallel"),
    )(x, w, *extras)


def _mm_tn(name, x, y, tk=1024, tn=1024, tt=1024):
    t, k = x.shape
    n = y.shape[1]
    tk, tn, tt = min(tk, k), min(tn, n), min(tt, t)

    def body(x_ref, y_ref, o_ref, ob_ref):
        @pl.when(pl.program_id(2) == 0)
        def _():
            o_ref[...] = jnp.zeros_like(o_ref)
        o_ref[...] += _tn(x_ref[...].astype(bf16), y_ref[...].astype(bf16))

        @pl.when(pl.program_id(2) == pl.num_programs(2) - 1)
        def _():
            ob_ref[...] = o_ref[...].astype(bf16)

    out = pl.BlockSpec((tk, tn), lambda i, j, s: (i, j))
    return pl.pallas_call(
        body, name=name, grid=(k // tk, n // tn, t // tt),
        in_specs=[pl.BlockSpec((tt, tk), lambda i, j, s: (s, i)), pl.BlockSpec((tt, tn), lambda i, j, s: (s, j))],
        out_specs=[out, out],
        out_shape=[jax.ShapeDtypeStruct((k, n), f32), jax.ShapeDtypeStruct((k, n), bf16)],
        compiler_params=_cp("parallel", "parallel", "arbitrary"),
    )(x, y)


def _norm_fwd(name, x, g, tm=512):
    t, d = x.shape

    def body(x_ref, g_ref, h_ref, hb_ref):
        h, _, _ = _rms_fwd(x_ref[...], g_ref[...])
        h_ref[...] = h
        hb_ref[...] = h.astype(bf16)

    row = pl.BlockSpec((tm, d), lambda i: (i, 0))
    return pl.pallas_call(
        body, name=name, grid=(t // tm,),
        in_specs=[row, pl.BlockSpec((1, d), lambda i: (0, 0))],
        out_specs=[row, row],
        out_shape=[jax.ShapeDtypeStruct((t, d), f32), jax.ShapeDtypeStruct((t, d), bf16)],
        compiler_params=_cp("parallel"),
    )(x, g)


def _norm_bwd(name, dh, x, g, dres, tm=512):
    t, d = x.shape

    def body(dh_ref, x_ref, g_ref, dres_ref, dx_ref, dg_ref):
        gv = g_ref[...]
        _, xn, r = _rms_fwd(x_ref[...], gv)
        dx, dg = _rms_bwd(dh_ref[...], xn, r, gv)
        dx_ref[...] = dres_ref[...] + dx

        @pl.when(pl.program_id(0) == 0)
        def _():
            dg_ref[...] = jnp.zeros_like(dg_ref)
        dg_ref[...] += dg

    row = pl.BlockSpec((tm, d), lambda i: (i, 0))
    vec = pl.BlockSpec((1, d), lambda i: (0, 0))
    return pl.pallas_call(
        body, name=name, grid=(t // tm,),
        in_specs=[row, row, vec, row],
        out_specs=[row, vec],
        out_shape=[jax.ShapeDtypeStruct((t, d), f32), jax.ShapeDtypeStruct((1, d), f32)],
        compiler_params=_cp("arbitrary"),
    )(dh, x, g, dres)


def _final_loss(x, g, target, tm=512):
    t, d = x.shape

    def body(x_ref, g_ref, t_ref, loss_ref, dx_ref, dg_ref):
        gv = g_ref[...]
        y, xn, r = _rms_fwd(x_ref[...], gv)
        err = y - t_ref[...]
        dy = err * (1.0 / d)
        dx, dg = _rms_bwd(dy, xn, r, gv)
        dx_ref[...] = dx

        @pl.when(pl.program_id(0) == 0)
        def _():
            dg_ref[...] = jnp.zeros_like(dg_ref)
            loss_ref[...] = jnp.zeros_like(loss_ref)
        dg_ref[...] += dg
        loss_ref[...] += jnp.full((1, 128), 0.5 / d, f32) * jnp.sum(err * err)

    row = pl.BlockSpec((tm, d), lambda i: (i, 0))
    vec = pl.BlockSpec((1, d), lambda i: (0, 0))
    return pl.pallas_call(
        body, name="final_loss", grid=(t // tm,),
        in_specs=[row, vec, row],
        out_specs=[pl.BlockSpec((1, 128), lambda i: (0, 0)), row, vec],
        out_shape=[jax.ShapeDtypeStruct((1, 128), f32), jax.ShapeDtypeStruct((t, d), f32),
                   jax.ShapeDtypeStruct((1, d), f32)],
        compiler_params=_cp("arbitrary"),
    )(x, g, target)


def _load_ffn_weights(hbm_refs, vmem_refs, sems):
    @pl.when(pl.program_id(0) == 0)
    def _():
        copies = []
        for k, (src, dst) in enumerate(zip(hbm_refs, vmem_refs)):
            for j in range(N_DEV):
                half = pl.ds((j % 2) * FF_PAD, FF_PAD)
                window = dst.at[j // 2, half, :] if k == 2 else dst.at[j // 2, :, half]
                copies.append(pltpu.make_async_copy(src.at[j], window, sems.at[k * N_DEV + j]))
        for cp in copies:
            cp.start()
        for cp in copies:
            cp.wait()


def _ffn_weight_scratch(nj, d, ff):
    return [pltpu.VMEM((nj, d, ff), bf16), pltpu.VMEM((nj, d, ff), bf16), pltpu.VMEM((nj, ff, d), bf16),
            pltpu.SemaphoreType.DMA((3 * N_DEV,))]


def _ffn_fwd(name, x, g, w1, w3, w2, tm=512):
    t, d = x.shape
    nj, ff = N_DEV // 2, 2 * FF_PAD

    def body(x_ref, g_ref, w1_hbm, w3_hbm, w2_hbm, y_ref, a_ref, b_ref, w1_ref, w3_ref, w2_ref, sems):
        _load_ffn_weights((w1_hbm, w3_hbm, w2_hbm), (w1_ref, w3_ref, w2_ref), sems)
        xv = x_ref[...]
        h, _, _ = _rms_fwd(xv, g_ref[...])
        hb = h.astype(bf16)
        acc = jnp.zeros((tm, d), f32)
        for j in range(nj):
            a = _nn(hb, w1_ref[j])
            b = _nn(hb, w3_ref[j])
            a_ref[j] = a.astype(bf16)
            b_ref[j] = b.astype(bf16)
            u = (a * jax.nn.sigmoid(a) * b).astype(bf16)
            acc = acc + _nn(u, w2_ref[j])
        y_ref[...] = xv + 0.5 * acc

    row = pl.BlockSpec((tm, d), lambda i: (i, 0))
    mid = pl.BlockSpec((nj, tm, ff), lambda i: (0, i, 0))
    return pl.pallas_call(
        body, name=name, grid=(t // tm,),
        in_specs=[row, pl.BlockSpec((1, d), lambda i: (0, 0)), ANY_SPEC, ANY_SPEC, ANY_SPEC],
        out_specs=[row, mid, mid],
        out_shape=[jax.ShapeDtypeStruct((t, d), f32), jax.ShapeDtypeStruct((nj, t, ff), bf16),
                   jax.ShapeDtypeStruct((nj, t, ff), bf16)],
        scratch_shapes=_ffn_weight_scratch(nj, d, ff),
        compiler_params=_cp("arbitrary"),
    )(x, g, w1, w3, w2)


def _ffn_dx(name, dy, x, g, w1, w3, w2, a, b, tm=256):
    t, d = x.shape
    nj, ff = N_DEV // 2, 2 * FF_PAD

    def body(dy_ref, x_ref, g_ref, w1_hbm, w3_hbm, w2_hbm, a_ref, b_ref,
             dx_ref, dg_ref, hbt_ref, dyh_ref, ut_ref, da_ref, db_ref, w1_ref, w3_ref, w2_ref, sems):
        _load_ffn_weights((w1_hbm, w3_hbm, w2_hbm), (w1_ref, w3_ref, w2_ref), sems)
        gv = g_ref[...]
        h, xn, r = _rms_fwd(x_ref[...], gv)
        hbt_ref[...] = h.astype(bf16).T
        dyv = dy_ref[...]
        dyh = (0.5 * dyv).astype(bf16)
        dyh_ref[...] = dyh
        dh = jnp.zeros((tm, d), f32)
        dus = [_nt(dyh, w2_ref[j]) for j in range(nj)]
        for j in range(nj):
            av = a_ref[j].astype(f32)
            bv = b_ref[j].astype(f32)
            s = jax.nn.sigmoid(av)
            silu = av * s
            ut_ref[j] = (silu * bv).astype(bf16).T
            du = dus[j]
            dab = (du * bv * (s * (1.0 + av * (1.0 - s)))).astype(bf16)
            dbb = (du * silu).astype(bf16)
            da_ref[j] = dab
            db_ref[j] = dbb
            dh = dh + _nt(dab, w1_ref[j]) + _nt(dbb, w3_ref[j])
        dx, dg = _rms_bwd(dh, xn, r, gv)
        dx_ref[...] = dyv + dx

        @pl.when(pl.program_id(0) == 0)
        def _():
            dg_ref[...] = jnp.zeros_like(dg_ref)
        dg_ref[...] += dg

    row = pl.BlockSpec((tm, d), lambda i: (i, 0))
    vec = pl.BlockSpec((1, d), lambda i: (0, 0))
    mid = pl.BlockSpec((nj, tm, ff), lambda i: (0, i, 0))
    mid_shape = jax.ShapeDtypeStruct((nj, t, ff), bf16)
    return pl.pallas_call(
        body, name=name, grid=(t // tm,),
        in_specs=[row, row, vec, ANY_SPEC, ANY_SPEC, ANY_SPEC, mid, mid],
        out_specs=[row, vec, pl.BlockSpec((d, tm), lambda i: (0, i)), row,
                   pl.BlockSpec((nj, ff, tm), lambda i: (0, 0, i)), mid, mid],
        out_shape=[jax.ShapeDtypeStruct((t, d), f32), jax.ShapeDtypeStruct((1, d), f32),
                   jax.ShapeDtypeStruct((d, t), bf16), jax.ShapeDtypeStruct((t, d), bf16),
                   jax.ShapeDtypeStruct((nj, ff, t), bf16), mid_shape, mid_shape],
        scratch_shapes=_ffn_weight_scratch(nj, d, ff),
        compiler_params=_cp("arbitrary"),
    )(dy, x, g, w1, w3, w2, a, b)


def _ffn_dw_one(name, xt, y, buf, l, h, tt=2048):
    t = y.shape[-2]
    tt = min(tt, t)
    cut_cols = xt.ndim == 2

    def body(x_ref, y_ref, buf_ref, o_ref, ob_ref, acc):
        s = pl.program_id(1)
        prod = _nn(x_ref[0] if xt.ndim == 3 else x_ref[...], y_ref[0] if y.ndim == 3 else y_ref[...])

        @pl.when(s == 0)
        def _():
            acc[...] = prod

        @pl.when(s > 0)
        def _():
            acc[...] += prod

        @pl.when(s == pl.num_programs(1) - 1)
        def _():
            total = acc[...]
            for e in range(2):
                lo = e * FF_PAD
                part = total[:, lo:lo + FF_SHARD] if cut_cols else total[lo:lo + FF_SHARD, :]
                o_ref[e] = part
                ob_ref[e] = part.astype(bf16)

    x_spec = (pl.BlockSpec((1, xt.shape[1], tt), lambda p, s: (p, 0, s)) if xt.ndim == 3
              else pl.BlockSpec((xt.shape[0], tt), lambda p, s: (0, s)))
    y_spec = (pl.BlockSpec((1, tt, y.shape[2]), lambda p, s: (p, s, 0)) if y.ndim == 3
              else pl.BlockSpec((tt, y.shape[1]), lambda p, s: (s, 0)))
    k_, n_ = buf.shape[-2:]
    return pl.pallas_call(
        body, name=name, grid=(N_DEV // 2, t // tt),
        in_specs=[x_spec, y_spec, ANY_SPEC],
        out_specs=[pl.BlockSpec((2, None, None, k_, n_), lambda p, s: (p, l, h, 0, 0)),
                   pl.BlockSpec((2, k_, n_), lambda p, s: (p, 0, 0))],
        out_shape=[jax.ShapeDtypeStruct(buf.shape, buf.dtype), jax.ShapeDtypeStruct((N_DEV, k_, n_), bf16)],
        input_output_aliases={2: 0},
        scratch_shapes=[pltpu.VMEM((xt.shape[-2], y.shape[-1]), f32)],
        compiler_params=_cp("parallel", "arbitrary"),
    )(xt, y, buf)


_LOG_GAMMA = [float(np.log1p(-np.float32(2.0) ** np.float32(-5.0 - h))) for h in range(RET_HEADS)]


def _ret_consts(h):
    lg = jnp.where(h == 0, _LOG_GAMMA[0], jnp.where(h == 1, _LOG_GAMMA[1],
                   jnp.where(h == 2, _LOG_GAMMA[2], _LOG_GAMMA[3]))).astype(f32)
    c = RET_CHUNK
    r = lax.broadcasted_iota(jnp.int32, (c, c), 0)
    cc = lax.broadcasted_iota(jnp.int32, (c, c), 1)
    decay = jnp.where(r >= cc, jnp.exp(lg * jnp.maximum((r - cc).astype(f32), 0.0)), 0.0)
    pos = lax.broadcasted_iota(jnp.int32, (c, 1), 0).astype(f32)
    kd = jnp.exp(lg * (c - 1.0 - pos))
    qd = jnp.exp(lg * (pos + 1.0))
    gc = jnp.exp(lg * c)
    return decay, kd, qd, gc


def _rope(x, cos, sin):
    return x * cos + pltpu.roll(x, HEAD_DIM // 2, 1) * sin


def _rope_t(g, cos, sin):
    return g * cos + pltpu.roll(g * sin, HEAD_DIM // 2, 1)


def _rope_tables(s):
    half = HEAD_DIM // 2
    inv = ROPE_BASE ** (-jnp.arange(half, dtype=f32) / half)
    ang = jnp.arange(s, dtype=f32)[:, None] * inv[None, :]
    cos, sin = jnp.cos(ang), jnp.sin(ang)
    return jnp.concatenate([cos, cos], axis=1), jnp.concatenate([-sin, sin], axis=1)


def _head_ln(o):
    mu = jnp.mean(o, axis=-1, keepdims=True)
    oc = o - mu
    rs = lax.rsqrt(jnp.mean(oc * oc, axis=-1, keepdims=True) + EPS)
    return oc * rs, rs


def _ret_fwd(proj, cos, sin, ret_g, nb, s):
    c = RET_CHUNK
    nc = s // c
    t = nb * s
    scale = HEAD_DIM ** -0.5

    def body(q_ref, k_ref, v_ref, gate_ref, cos_ref, sin_ref, g_ref, o_ref, rprev_ref, m_ref):
        decay, kd, qd, gc = _ret_consts(pl.program_id(0))
        gv = g_ref[...]

        def chunk(b, n, rv):
            rows = pl.ds(pl.multiple_of(b * s + n * c, c), c)
            pos = pl.ds(pl.multiple_of(n * c, c), c)
            cs, sn = cos_ref[pos, :], sin_ref[pos, :]
            q = _rope(q_ref[rows, :], cs, sn)
            k = _rope(k_ref[rows, :], cs, sn) * scale
            vb = v_ref[rows, :].astype(bf16)
            sc = _nt(q.astype(bf16), k.astype(bf16)) * decay
            rprev_ref[b, n] = rv
            o = _nn(sc.astype(bf16), vb) + _nn((q * qd).astype(bf16), rv.astype(bf16))
            o_ref[rows, :] = o
            y, _ = _head_ln(o)
            gate = gate_ref[rows, :]
            m_ref[rows, :] = y * gv * (gate * jax.nn.sigmoid(gate))
            return rv * gc + _tn((k * kd).astype(bf16), vb)

        def step(n, carry):
            return tuple(chunk(b, n, carry[b]) for b in range(nb))

        lax.fori_loop(0, nc, step, (jnp.zeros((HEAD_DIM, HEAD_DIM), f32),) * nb)

    def col(off):
        return pl.BlockSpec((t, HEAD_DIM), lambda h: (0, off + h))

    tab = pl.BlockSpec((s, HEAD_DIM), lambda h: (0, 0))
    return pl.pallas_call(
        body, name="ret_fwd", grid=(RET_HEADS,),
        in_specs=[col(0), col(4), col(8), col(12), tab, tab, pl.BlockSpec((1, HEAD_DIM), lambda h: (0, h))],
        out_specs=[col(0), pl.BlockSpec((nb, None, nc, HEAD_DIM, HEAD_DIM), lambda h: (0, h, 0, 0, 0)), col(0)],
        out_shape=[jax.ShapeDtypeStruct((t, RET_WIDTH), f32),
                   jax.ShapeDtypeStruct((nb, RET_HEADS, nc, HEAD_DIM, HEAD_DIM), f32),
                   jax.ShapeDtypeStruct((t, RET_WIDTH), f32)],
        compiler_params=_cp("parallel"),
    )(proj, proj, proj, proj, cos, sin, ret_g)


def _ret_bwd(dmerged, o_raw, rprev, proj, cos, sin, ret_g, nb, s):
    c = RET_CHUNK
    nc = s // c
    t = nb * s
    scale = HEAD_DIM ** -0.5

    def body(dm_ref, o_ref, rprev_ref, q_ref, k_ref, v_ref, gate_ref, cos_ref, sin_ref, g_ref,
             dq_ref, dk_ref, dv_ref, dgate_ref, dg_ref):
        decay, kd, qd, gc = _ret_consts(pl.program_id(0))
        gv = g_ref[...]

        def chunk(b, n, drn, dg):
            rows = pl.ds(pl.multiple_of(b * s + n * c, c), c)
            pos = pl.ds(pl.multiple_of(n * c, c), c)
            cs, sn = cos_ref[pos, :], sin_ref[pos, :]
            q = _rope(q_ref[rows, :], cs, sn)
            k = _rope(k_ref[rows, :], cs, sn) * scale
            qb, kb = q.astype(bf16), k.astype(bf16)
            vb = v_ref[rows, :].astype(bf16)
            sc = _nt(qb, kb) * decay
            y, rs = _head_ln(o_ref[rows, :])
            gate = gate_ref[rows, :]
            sg = jax.nn.sigmoid(gate)
            silu = gate * sg
            dm = dm_ref[rows, :]
            dgate_ref[rows, :] = dm * y * gv * (sg * (1.0 + gate * (1.0 - sg)))
            dyl = dm * gv * silu
            dg = dg + jnp.sum(dm * y * silu, axis=0, keepdims=True)
            do = rs * (dyl - jnp.mean(dyl, axis=-1, keepdims=True) - y * jnp.mean(dyl * y, axis=-1, keepdims=True))
            dob = do.astype(bf16)
            rv = rprev_ref[b, n]
            drb = drn.astype(bf16)
            ds = (_nt(dob, vb) * decay).astype(bf16)
            kdb = (k * kd).astype(bf16)
            qdb = (q * qd).astype(bf16)
            dq_r = _nn(ds, kb) + _nt(dob, rv.astype(bf16)) * qd
            dk_r = _tn(ds, qb) + _nt(vb, drb) * kd
            dv_ref[rows, :] = _tn(sc.astype(bf16), dob) + _nn(kdb, drb)
            dq_ref[rows, :] = _rope_t(dq_r, cs, sn)
            dk_ref[rows, :] = _rope_t(dk_r * scale, cs, sn)
            return drn * gc + _tn(qdb, dob), dg

        def step(i, carry):
            out = [chunk(b, nc - 1 - i, *carry[b]) for b in range(nb)]
            return tuple(out)

        zero = (jnp.zeros((HEAD_DIM, HEAD_DIM), f32), jnp.zeros((1, HEAD_DIM), f32))
        done = lax.fori_loop(0, nc, step, (zero,) * nb)
        dg_ref[...] = sum(dg for _, dg in done)

    def col(off):
        return pl.BlockSpec((t, HEAD_DIM), lambda h: (0, off + h))

    tab = pl.BlockSpec((s, HEAD_DIM), lambda h: (0, 0))
    gsp = pl.BlockSpec((1, HEAD_DIM), lambda h: (0, h))
    out_t = jax.ShapeDtypeStruct((t, RET_WIDTH), f32)
    return pl.pallas_call(
        body, name="ret_bwd", grid=(RET_HEADS,),
        in_specs=[col(0), col(0), pl.BlockSpec((nb, None, nc, HEAD_DIM, HEAD_DIM), lambda h: (0, h, 0, 0, 0)),
                  col(0), col(4), col(8), col(12), tab, tab, gsp],
        out_specs=[col(0), col(0), col(0), col(0), gsp],
        out_shape=[out_t, out_t, out_t, out_t, jax.ShapeDtypeStruct((1, RET_WIDTH), f32)],
        compiler_params=_cp("parallel"),
    )(dmerged, o_raw, rprev, proj, proj, proj, proj, cos, sin, ret_g)


def _neg_expm1(z):
    series = -(z * (1.0 + z * (0.5 + z * (1.0 / 6.0 + z * (1.0 / 24.0)))))
    return jnp.where(z > -0.01, series, 1.0 - jnp.exp(z))


def _lru_gates(xc, pa, pi, lam):
    r = jax.nn.sigmoid(pa)
    i = jax.nn.sigmoid(pi)
    log_a = -LRU_C * r * jax.nn.softplus(-lam)
    a = jnp.exp(log_a)
    bx = jnp.sqrt(_neg_expm1(2.0 * log_a)) * i * xc
    return a, bx


def _scan_rows(a, b, row, up):
    sub = row[:SUBLANES] & (SUBLANES - 1)
    groups = list(range(a.shape[0] // SUBLANES))
    out = [None] * len(groups)
    edge = slice(0, 1) if up else slice(SUBLANES - 1, SUBLANES)
    carry = jnp.zeros((1, a.shape[1]), f32)
    for g in (reversed(groups) if up else groups):
        rows = slice(g * SUBLANES, (g + 1) * SUBLANES)
        xa, xb = a[rows], b[rows]
        d = 1
        while d < SUBLANES:
            keep = (sub < SUBLANES - d) if up else (sub >= d)
            shift = SUBLANES - d if up else d
            xb = xa * jnp.where(keep, pltpu.roll(xb, shift, 0), 0.0) + xb
            xa = xa * jnp.where(keep, pltpu.roll(xa, shift, 0), 1.0)
            d *= 2
        out[g] = xb + xa * carry
        carry = out[g][edge]
    return jnp.concatenate(out, axis=0)


def _scan_fwd(a, b, row):
    return _scan_rows(a, b, row, False)


def _scan_bwd(c, b, row):
    return _scan_rows(c, b, row, True)


def _conv_fwd(x, cw, cb, row):
    return (cb + cw[3:4] * x + cw[2:3] * _shift_dn(x, 1, row) + cw[1:2] * _shift_dn(x, 2, row)
            + cw[0:1] * _shift_dn(x, 3, row))


def _lru_specs(s, order):
    def im(f):
        return (lambda b, g: f(b, g)) if order == "bg" else (lambda g, b: f(b, g))
    seq = lambda off: pl.BlockSpec((s, 128), im(lambda b, g: (b, off + g)))
    vec = pl.BlockSpec((1, 128), im(lambda b, g: (0, g)))
    cw = pl.BlockSpec((4, 128), im(lambda b, g: (0, g)))
    mat = pl.BlockSpec((1, 128, 128), im(lambda b, g: (g, 0, 0)))
    return seq, vec, cw, mat


def _lru_fwd(proj, conv_w, conv_b, w_a, b_a, w_i, b_i, lam, nb, s):
    def body(x_ref, gt_ref, cw_ref, cb_ref, wa_ref, ba_ref, wi_ref, bi_ref, lam_ref, out_ref):
        row = lax.broadcasted_iota(jnp.int32, (s, 128), 0)
        xc = _conv_fwd(x_ref[...], cw_ref[...], cb_ref[...], row)
        xcb = xc.astype(bf16)
        pa = _nn(xcb, wa_ref[0].astype(bf16)) + ba_ref[...]
        pi = _nn(xcb, wi_ref[0].astype(bf16)) + bi_ref[...]
        a, bx = _lru_gates(xc, pa, pi, lam_ref[...])
        h = _scan_fwd(a, bx, row)
        out_ref[...] = h * jax.nn.gelu(gt_ref[...])

    seq, vec, cw, mat = _lru_specs(s, "bg")
    return pl.pallas_call(
        body, name="lru_fwd", grid=(nb, LRU_BLOCKS),
        in_specs=[seq(16), seq(20), cw, vec, mat, vec, mat, vec, vec],
        out_specs=seq(0),
        out_shape=jax.ShapeDtypeStruct((nb * s, LRU_WIDTH), f32),
        compiler_params=_cp("parallel", "parallel"),
    )(proj, proj, conv_w, conv_b, w_a, b_a, w_i, b_i, lam)


def _lru_bwd(dmerged, proj, conv_w, conv_b, w_a, b_a, w_i, b_i, lam, nb, s):
    def body(dout_ref, x_ref, gt_ref, cw_ref, cb_ref, wa_ref, ba_ref, wi_ref, bi_ref, lam_ref,
             dx_ref, dgt_ref, dcw_ref, dcb_ref, dwa_ref, dba_ref, dwi_ref, dbi_ref, dlam_ref):
        row = lax.broadcasted_iota(jnp.int32, (s, 128), 0)
        x = x_ref[...]
        cwv = cw_ref[...]
        xc = _conv_fwd(x, cwv, cb_ref[...], row)
        xcb = xc.astype(bf16)
        wab, wib = wa_ref[0].astype(bf16), wi_ref[0].astype(bf16)
        pa = _nn(xcb, wab) + ba_ref[...]
        pi = _nn(xcb, wib) + bi_ref[...]
        (a, bx), gates_vjp = jax.vjp(_lru_gates, xc, pa, pi, lam_ref[...])
        h = _scan_fwd(a, bx, row)
        ge, gelu_vjp = jax.vjp(jax.nn.gelu, gt_ref[...])
        dout = dout_ref[...]
        dgt_ref[...] = gelu_vjp(dout * h)[0]
        adj = _scan_bwd(_shift_up(a, 1, row), dout * ge, row)
        dxc, dpa, dpi, dlam = gates_vjp((adj * _shift_dn(h, 1, row), adj))
        dpab, dpib = dpa.astype(bf16), dpi.astype(bf16)
        dxc = dxc + _nt(dpab, wab) + _nt(dpib, wib)
        dx_ref[...] = (cwv[3:4] * dxc + cwv[2:3] * _shift_up(dxc, 1, row) + cwv[1:2] * _shift_up(dxc, 2, row)
                       + cwv[0:1] * _shift_up(dxc, 3, row))

        @pl.when(pl.program_id(1) == 0)
        def _():
            for r in (dcw_ref, dcb_ref, dwa_ref, dba_ref, dwi_ref, dbi_ref, dlam_ref):
                r[...] = jnp.zeros_like(r)
        rsum = lambda v: jnp.sum(v, axis=0, keepdims=True)
        dcw_ref[...] += jnp.concatenate([rsum(dxc * _shift_dn(x, 3, row)), rsum(dxc * _shift_dn(x, 2, row)),
                                         rsum(dxc * _shift_dn(x, 1, row)), rsum(dxc * x)], axis=0)
        dcb_ref[...] += rsum(dxc)
        dwa_ref[0] += _tn(xcb, dpab)
        dwi_ref[0] += _tn(xcb, dpib)
        dba_ref[...] += rsum(dpa)
        dbi_ref[...] += rsum(dpi)
        dlam_ref[...] += dlam

    seq, vec, cw, mat = _lru_specs(s, "gb")
    t = nb * s
    vshape = jax.ShapeDtypeStruct((1, LRU_WIDTH), f32)
    mshape = jax.ShapeDtypeStruct((LRU_BLOCKS, 128, 128), f32)
    return pl.pallas_call(
        body, name="lru_bwd", grid=(LRU_BLOCKS, nb),
        in_specs=[seq(4), seq(16), seq(20), cw, vec, mat, vec, mat, vec, vec],
        out_specs=[seq(0), seq(0), cw, vec, mat, vec, mat, vec, vec],
        out_shape=[jax.ShapeDtypeStruct((t, LRU_WIDTH), f32), jax.ShapeDtypeStruct((t, LRU_WIDTH), f32),
                   jax.ShapeDtypeStruct((4, LRU_WIDTH), f32), vshape, mshape, vshape, mshape, vshape, vshape],
        compiler_params=_cp("parallel", "arbitrary"),
    )(dmerged, proj, proj, conv_w, conv_b, w_a, b_a, w_i, b_i, lam)


def _s5_disc(lr, li, ldt, bre, bim):
    dt = jnp.exp(ldt)
    mag = jnp.exp(lr * dt)
    lbr = mag * jnp.cos(li * dt)
    lbi = mag * jnp.sin(li * dt)
    den = lr * lr + li * li
    nr = lbr - 1.0
    fr = (nr * lr + lbi * li) / den
    fi = (lbi * lr - nr * li) / den
    bbr = fr[:, None, :] * bre - fi[:, None, :] * bim
    bbi = fr[:, None, :] * bim + fi[:, None, :] * bre
    return lbr, lbi, bbr, bbi


def _s5_prep(lr, li, ldt, bre, bim):
    def body(lr_ref, li_ref, ldt_ref, bre_ref, bim_ref, o1, o2, o3, o4):
        o1[...], o2[...], o3[...], o4[...] = _s5_disc(lr_ref[...], li_ref[...], ldt_ref[...], bre_ref[...], bim_ref[...])

    return pl.pallas_call(
        body, name="s5_prep", in_specs=[VMEM_SPEC] * 5, out_specs=[VMEM_SPEC] * 4,
        out_shape=[jax.ShapeDtypeStruct(lr.shape, f32), jax.ShapeDtypeStruct(lr.shape, f32),
                   jax.ShapeDtypeStruct(bre.shape, f32), jax.ShapeDtypeStruct(bre.shape, f32)],
    )(lr, li, ldt, bre, bim)


def _s5_prep_bwd(lr, li, ldt, bre, bim, cts):
    def body(lr_ref, li_ref, ldt_ref, bre_ref, bim_ref, g1, g2, g3, g4, o1, o2, o3, o4, o5):
        _, vjp = jax.vjp(_s5_disc, lr_ref[...], li_ref[...], ldt_ref[...], bre_ref[...], bim_ref[...])
        o1[...], o2[...], o3[...], o4[...], o5[...] = vjp((g1[...], g2[...], g3[...], g4[...]))

    return pl.pallas_call(
        body, name="s5_prep_bwd", in_specs=[VMEM_SPEC] * 9, out_specs=[VMEM_SPEC] * 5,
        out_shape=[jax.ShapeDtypeStruct(v.shape, f32) for v in (lr, li, ldt, bre, bim)],
    )(lr, li, ldt, bre, bim, *cts)


def _cmul(ar, ai, br, bi):
    return ar * br - ai * bi, ar * bi + ai * br


def _s5_pow_table(lr, li, n, row, up):
    ar = jnp.broadcast_to(lr, (n, lr.shape[1]))
    ai = jnp.broadcast_to(li, (n, li.shape[1]))
    shift = _shift_up if up else _shift_dn
    d = 1
    while d < n:
        ar, ai = _cmul(ar, ai, shift(ar, d, row, 1.0), shift(ai, d, row, 0.0))
        d *= 2
    return ar, ai


def _s5_step_factors(lr, li, row, up):
    sub = row & (SUBLANES - 1)
    out, pr, pi, d = [], lr, li, 1
    while d < SUBLANES:
        keep = (sub < SUBLANES - d) if up else (sub >= d)
        out.append((jnp.where(keep, pr, 0.0), jnp.where(keep, pi, 0.0)))
        pr, pi = _cmul(pr, pi, pr, pi)
        d *= 2
    return out


def _s5_scan(br, bi, steps, tab_r, tab_i, cr, ci, up):
    groups = list(range(br.shape[0] // SUBLANES))
    out_r, out_i = [None] * len(groups), [None] * len(groups)
    edge = slice(0, 1) if up else slice(SUBLANES - 1, SUBLANES)
    for g in (reversed(groups) if up else groups):
        rows = slice(g * SUBLANES, (g + 1) * SUBLANES)
        xr, xi = br[rows], bi[rows]
        for k, (mr, mi) in enumerate(steps):
            shift = SUBLANES - (1 << k) if up else 1 << k
            tr, ti = _cmul(mr, mi, pltpu.roll(xr, shift, 0), pltpu.roll(xi, shift, 0))
            xr, xi = xr + tr, xi + ti
        tr, ti = _cmul(tab_r, tab_i, cr, ci)
        hr, hi = xr + tr, xi + ti
        out_r[g], out_i[g] = hr, hi
        cr, ci = hr[edge], hi[edge]
    return jnp.concatenate(out_r, axis=0), jnp.concatenate(out_i, axis=0)


def _s5_specs(t, nb, nc):
    seq = pl.BlockSpec((t, 128), lambda k: (0, k))
    lvec = pl.BlockSpec((1, S5_BLOCK_STATES), lambda k: (0, k))
    dvec = pl.BlockSpec((1, 128), lambda k: (0, k))
    wmat = pl.BlockSpec((1, 128, S5_BLOCK_STATES), lambda k: (k, 0, 0))
    h0 = pl.BlockSpec((nb, None, nc, 2, S5_BLOCK_STATES), lambda k: (0, k, 0, 0, 0))
    return seq, lvec, dvec, wmat, h0


def _s5_fwd(u, lbr, lbi, wbr, wbi, wcr, wci, dskip, nb, s):
    ln = min(S5_CHUNK, s)
    nc = s // ln

    def body(u_ref, lr_ref, li_ref, wbr_ref, wbi_ref, wcr_ref, wci_ref, d_ref, yg_ref, y_ref, h0_ref):
        row = lax.broadcasted_iota(jnp.int32, (ln, S5_BLOCK_STATES), 0)
        lr, li = lr_ref[...], li_ref[...]
        pr, pi = _s5_pow_table(lr, li, SUBLANES, row[:SUBLANES], False)
        steps = _s5_step_factors(lr, li, row[:SUBLANES], False)
        dv = d_ref[...]

        def chunk(b, n, h0r, h0i):
            st = pl.multiple_of(b * s + n * ln, ln)
            uc = u_ref[pl.ds(st, ln), :]
            ub = uc.astype(bf16)
            hr, hi = _s5_scan(_nn(ub, wbr_ref[0]), _nn(ub, wbi_ref[0]), steps, pr, pi, h0r, h0i, False)
            h0_ref[b, n, 0:1, :] = h0r
            h0_ref[b, n, 1:2, :] = h0i
            y = _nt(hr.astype(bf16), wcr_ref[0]) - _nt(hi.astype(bf16), wci_ref[0]) + dv * uc
            y_ref[pl.ds(st, ln), :] = y
            yg_ref[pl.ds(st, ln), :] = jax.nn.gelu(y).astype(bf16)
            return hr[ln - 1:ln, :], hi[ln - 1:ln, :]

        def step(n, carry):
            return tuple(chunk(b, n, *carry[b]) for b in range(nb))

        z = jnp.zeros((1, S5_BLOCK_STATES), f32)
        lax.fori_loop(0, nc, step, ((z, z),) * nb)

    t = nb * s
    seq, lvec, dvec, wmat, h0 = _s5_specs(t, nb, nc)
    return pl.pallas_call(
        body, name="s5_fwd", grid=(S5_BLOCKS,),
        in_specs=[seq, lvec, lvec, wmat, wmat, wmat, wmat, dvec],
        out_specs=[seq, seq, h0],
        out_shape=[jax.ShapeDtypeStruct((t, D_MODEL), bf16), jax.ShapeDtypeStruct((t, D_MODEL), f32),
                   jax.ShapeDtypeStruct((nb, S5_BLOCKS, nc, 2, S5_BLOCK_STATES), f32)],
        compiler_params=_cp("parallel"),
    )(u, lbr, lbi, wbr, wbi, wcr, wci, dskip)


def _s5_bwd(dyg, y, u, h0, lbr, lbi, wbr, wbi, wcr, wci, dskip, nb, s):
    ln = min(S5_CHUNK, s)
    nc = s // ln

    def body(dyg_ref, y_ref, u_ref, h0_ref, lr_ref, li_ref, wbr_ref, wbi_ref, wcr_ref, wci_ref, d_ref,
             du_ref, dlr_ref, dli_ref, dwbr_ref, dwbi_ref, dwcr_ref, dwci_ref, dd_ref):
        for r in (dlr_ref, dli_ref, dwbr_ref, dwbi_ref, dwcr_ref, dwci_ref, dd_ref):
            r[...] = jnp.zeros_like(r)
        row = lax.broadcasted_iota(jnp.int32, (ln, S5_BLOCK_STATES), 0)
        lr, li = lr_ref[...], li_ref[...]
        pr, pi = _s5_pow_table(lr, li, SUBLANES, row[:SUBLANES], False)
        qr, qi = _s5_pow_table(lr, -li, SUBLANES, row[:SUBLANES], True)
        row8 = row[:SUBLANES]
        steps_dn, steps_up = _s5_step_factors(lr, li, row8, False), _s5_step_factors(lr, -li, row8, True)
        dv = d_ref[...]
        rsum = lambda v: jnp.sum(v, axis=0, keepdims=True)

        def chunk(b, n, gnr, gni):
            st = pl.multiple_of(b * s + n * ln, ln)
            uc = u_ref[pl.ds(st, ln), :]
            ub = uc.astype(bf16)
            h0v = h0_ref[b, n]
            h0r, h0i = h0v[0:1], h0v[1:2]
            hr, hi = _s5_scan(_nn(ub, wbr_ref[0]), _nn(ub, wbi_ref[0]), steps_dn, pr, pi, h0r, h0i, False)
            dy = jax.vjp(jax.nn.gelu, y_ref[pl.ds(st, ln), :])[1](dyg_ref[pl.ds(st, ln), :])[0]
            dyb = dy.astype(bf16)
            dd_ref[...] += rsum(dy * uc)
            gr, gi = _s5_scan(_nn(dyb, wcr_ref[0]), -_nn(dyb, wci_ref[0]), steps_up, qr, qi, gnr, gni, True)
            hpr = jnp.where(row >= 1, pltpu.roll(hr, 1, 0), h0r)
            hpi = jnp.where(row >= 1, pltpu.roll(hi, 1, 0), h0i)
            dlr_ref[...] += rsum(gr * hpr + gi * hpi)
            dli_ref[...] += rsum(gi * hpr - gr * hpi)
            grb, gib = gr.astype(bf16), gi.astype(bf16)
            dwbr_ref[0] += _tn(ub, grb)
            dwbi_ref[0] += _tn(ub, gib)
            dwcr_ref[0] += _tn(dyb, hr.astype(bf16))
            dwci_ref[0] -= _tn(dyb, hi.astype(bf16))
            du_ref[pl.ds(st, ln), :] = _nt(grb, wbr_ref[0]) + _nt(gib, wbi_ref[0]) + dv * dy
            return gr[0:1, :], gi[0:1, :]

        def step(i, carry):
            return tuple(chunk(b, nc - 1 - i, *carry[b]) for b in range(nb))

        z = jnp.zeros((1, S5_BLOCK_STATES), f32)
        lax.fori_loop(0, nc, step, ((z, z),) * nb)

    t = nb * s
    seq, lvec, dvec, wmat, h0s = _s5_specs(t, nb, nc)
    lshape = jax.ShapeDtypeStruct((1, S5_BLOCKS * S5_BLOCK_STATES), f32)
    wshape = jax.ShapeDtypeStruct((S5_BLOCKS, 128, S5_BLOCK_STATES), f32)
    return pl.pallas_call(
        body, name="s5_bwd", grid=(S5_BLOCKS,),
        in_specs=[seq, seq, seq, h0s, lvec, lvec, wmat, wmat, wmat, wmat, dvec],
        out_specs=[seq, lvec, lvec, wmat, wmat, wmat, wmat, dvec],
        out_shape=[jax.ShapeDtypeStruct((t, D_MODEL), f32), lshape, lshape, wshape, wshape, wshape, wshape,
                   jax.ShapeDtypeStruct((1, D_MODEL), f32)],
        compiler_params=_cp("parallel"),
    )(dyg, y, u, h0, lbr, lbi, wbr, wbi, wcr, wci, dskip)


def _blockdiag(w):
    w4 = w.reshape(S5_BLOCKS, 8, S5_GROUP, S5_STATE)
    same_group = jnp.eye(8, dtype=bool)[None, :, None, :, None]
    return jnp.where(same_group, w4[:, :, :, None, :], 0.0).reshape(S5_BLOCKS, 128, S5_BLOCK_STATES)


def _blockdiag_t(dw):
    d5 = dw.reshape(S5_BLOCKS, 8, S5_GROUP, 8, S5_STATE)
    diag = jnp.diagonal(d5, axis1=1, axis2=3)
    return jnp.moveaxis(diag, 3, 1).reshape(S5_GROUPS, S5_GROUP, S5_STATE)


def _glu_fwd(ygb, wa, wb, x, tm=512, tn=1024):
    t, d = x.shape

    def body(y_ref, wa_ref, wb_ref, x_ref, o_ref, p_ref, q_ref):
        p = _nn(y_ref[...], wa_ref[...])
        q = _nn(y_ref[...], wb_ref[...])
        p_ref[...] = p
        q_ref[...] = q
        o_ref[...] = x_ref[...] + p * jax.nn.sigmoid(q)

    tile = pl.BlockSpec((tm, tn), lambda i, j: (i, j))
    wsp = pl.BlockSpec((d, tn), lambda i, j: (0, j))
    out = jax.ShapeDtypeStruct((t, d), f32)
    return pl.pallas_call(
        body, name="glu_fwd", grid=(t // tm, d // tn),
        in_specs=[pl.BlockSpec((tm, d), lambda i, j: (i, 0)), wsp, wsp, tile],
        out_specs=[tile, tile, tile], out_shape=[out, out, out],
        compiler_params=_cp("parallel", "parallel"),
    )(ygb, wa, wb, x)


def _place():
    x, y, c = lax.axis_index("x"), lax.axis_index("y"), lax.axis_index("c")
    return x, y, c, [(1 - x, y), (x, 1 - y), (1 - x, 1 - y)]


def _all_gather(name, arrays):
    n = len(arrays)

    def body(*refs):
        ins, outs = refs[:n], refs[n:2 * n]
        send_sems, recv_sems, local_sems = refs[2 * n:]
        x, y, c, chips = _place()
        me, sib = (x, y, c), (x, y, 1 - c)

        def copy(i, k, block, to, src=None):
            dst = outs[i].at[4 * block[0] + 2 * block[1] + block[2]]
            return pltpu.make_async_remote_copy(
                src_ref=dst if src is None else src, dst_ref=dst,
                send_sem=send_sems.at[i * 7 + k], recv_sem=recv_sems.at[i * 7 + k],
                device_id=to, device_id_type=MESH)

        mine = [pltpu.make_async_copy(ins[i], outs[i].at[4 * x + 2 * y + c], local_sems.at[i]) for i in range(n)]
        for m in mine:
            m.start()
        first = []
        for i in range(n):
            first.append(copy(i, 0, me, sib, src=ins[i]))
            first += [copy(i, 1 + j, me, (*chip, c), src=ins[i]) for j, chip in enumerate(chips)]
        for cp in first:
            cp.start()
        passed = []
        for j, chip in enumerate(chips):
            for i in range(n):
                copy(i, 1 + j, (*chip, c), me).wait_recv()
                fwd = copy(i, 4 + j, (*chip, c), sib)
                fwd.start()
                passed.append(fwd)
        for i in range(n):
            copy(i, 0, sib, me).wait_recv()
        for j, chip in enumerate(chips):
            for i in range(n):
                copy(i, 4 + j, (*chip, 1 - c), me).wait_recv()
        for cp in first + passed:
            cp.wait_send()
        for m in mine:
            m.wait()

    return pl.pallas_call(
        body, name=name,
        in_specs=[ANY_SPEC] * n, out_specs=[ANY_SPEC] * n,
        out_shape=[jax.ShapeDtypeStruct((N_DEV,) + a.shape, a.dtype) for a in arrays],
        scratch_shapes=[pltpu.SemaphoreType.DMA((7 * n,)), pltpu.SemaphoreType.DMA((7 * n,)),
                        pltpu.SemaphoreType.DMA((n,))],
    )(*arrays)


def _tie(name, x, deps):
    def body(*refs):
        pass

    return pl.pallas_call(
        body, name=name, in_specs=[ANY_SPEC] * (1 + len(deps)), out_specs=ANY_SPEC,
        out_shape=jax.ShapeDtypeStruct(x.shape, x.dtype), input_output_aliases={0: 0},
    )(x, *deps)


def _xchg_copies(kind, srcs, lands, suffixes, send_sems, recv_sems):
    x, y, c, _ = _place()
    copies = []
    for i, (src, land, sfx) in enumerate(zip(srcs, lands, suffixes)):
        for k in range(N_DEV - 1):
            r = k + 1
            peer = (1 - x if r & 4 else x, 1 - y if r & 2 else y, 1 - c if r & 1 else c)
            if kind == "gather":
                s_ref, d_ref = src, land.at[(4 * x + 2 * y + c,) + sfx]
            else:
                s_ref, d_ref = src.at[4 * peer[0] + 2 * peer[1] + peer[2]], land.at[(k,) + sfx]
            copies.append(pltpu.make_async_remote_copy(
                src_ref=s_ref, dst_ref=d_ref, send_sem=send_sems.at[i * 7 + k], recv_sem=recv_sems.at[i * 7 + k],
                device_id=peer, device_id_type=MESH))
    return copies


def _xchg_start(name, kind, srcs, lands, suffixes=None):
    n = len(srcs)
    suffixes = suffixes or [()] * n

    def body(*refs):
        src, land = refs[:n], refs[n:2 * n]
        send_sems, recv_sems, token = refs[2 * n], refs[2 * n + 1], refs[-1]
        for cp in _xchg_copies(kind, src, land, suffixes, send_sems, recv_sems):
            cp.start()
        token[...] = jnp.zeros_like(token)

    arrays = list(srcs) + list(lands)
    outs = pl.pallas_call(
        body, name=name,
        out_shape=(pltpu.SemaphoreType.DMA((7 * n,)), pltpu.SemaphoreType.DMA((7 * n,)),
                   *[pltpu.HBM(a.shape, a.dtype) for a in arrays], jax.ShapeDtypeStruct((8, 128), f32)),
        in_specs=[HBM_SPEC] * (2 * n),
        out_specs=(SEM_SPEC, SEM_SPEC, *[HBM_SPEC] * (2 * n), VMEM_SPEC),
        input_output_aliases={i: 2 + i for i in range(2 * n)},
        compiler_params=pltpu.CompilerParams(has_side_effects=SIDE_EFFECT),
    )(*[pltpu.with_memory_space_constraint(a, pltpu.HBM) for a in arrays])
    return dict(kind=kind, n=n, suffixes=suffixes, send=outs[0], recv=outs[1], srcs=list(outs[2:2 + n]),
                lands=list(outs[2 + n:2 + 2 * n]), token=outs[-1])


def _xchg_wait(name, h, after, lands=None):
    n = h["n"]
    lands = h["lands"] if lands is None else lands

    def body(*refs):
        src, land = refs[:n], refs[n:2 * n]
        for cp in _xchg_copies(h["kind"], src, land, h["suffixes"], refs[2 * n], refs[2 * n + 1]):
            cp.wait_send()
            cp.wait_recv()

    arrays = h["srcs"] + list(lands)
    outs = pl.pallas_call(
        body, name=name,
        out_shape=tuple(pltpu.HBM(a.shape, a.dtype) for a in arrays),
        in_specs=[HBM_SPEC] * (2 * n) + [SEM_SPEC, SEM_SPEC] + [ANY_SPEC] * len(after),
        out_specs=tuple([HBM_SPEC] * (2 * n)),
        input_output_aliases={i: i for i in range(2 * n)},
        compiler_params=pltpu.CompilerParams(has_side_effects=SIDE_EFFECT),
    )(*arrays, h["send"], h["recv"], *after)
    return list(outs[n:])


def _rows(a):
    return a.reshape(-1, a.shape[-1])


def _row_tile(r):
    for tm in (512, 256, 128, 64, 32, 16, 8):
        if r % tm == 0:
            return tm
    return r


def _sum8(name, gathered):
    _, r, n = gathered.shape
    tm = _row_tile(r)

    def body(g_ref, o_ref):
        acc = g_ref[0]
        for k in range(1, N_DEV):
            acc = acc + g_ref[k]
        o_ref[...] = acc

    return pl.pallas_call(
        body, name=name, grid=(r // tm,),
        in_specs=[pl.BlockSpec((N_DEV, tm, n), lambda i: (0, i, 0))],
        out_specs=pl.BlockSpec((tm, n), lambda i: (i, 0)),
        out_shape=jax.ShapeDtypeStruct((r, n), f32),
        compiler_params=_cp("parallel"),
    )(gathered)


def _adamw(name, w, m, v, own, landed=None, slot=None):
    shape = w.shape
    w2, m2, v2 = _rows(w), _rows(m), _rows(v)
    r, n = w2.shape
    tm = _row_tile(r)
    c1 = 1.0 - ADAM_B1 ** ADAM_STEP
    c2 = 1.0 - ADAM_B2 ** ADAM_STEP
    extra = [] if landed is None else [landed.reshape(landed.shape[0], r, n)]
    row = pl.BlockSpec((tm, n), lambda i, *_: (i, 0))
    if slot is None:
        o2, own_spec, scalars = _rows(own), row, []
    else:
        dev, kind = slot
        scalars = [dev.reshape(1).astype(jnp.int32)]
        if kind == "lead":
            o2, own_spec = own.reshape(N_DEV, r, n), pl.BlockSpec((None, tm, n), lambda i, d: (d[0], i, 0))
        elif kind == "rows":
            o2, own_spec = own, pl.BlockSpec((tm, n), lambda i, d: (d[0] * (r // tm) + i, 0))
        else:
            o2, own_spec = own, pl.BlockSpec((tm, n), lambda i, d: (i, d[0]))

    def body(*refs):
        w_ref, m_ref, v_ref, o_ref = refs[len(scalars):len(scalars) + 4]
        refs = refs[len(scalars) + 4:]
        g = o_ref[...]
        if extra:
            for k in range(extra[0].shape[0]):
                g = g + refs[0][k].astype(f32)
        g_ref, d_ref, mn_ref, vn_ref = refs[len(extra):]
        mn = ADAM_B1 * m_ref[...] + (1.0 - ADAM_B1) * g
        vn = ADAM_B2 * v_ref[...] + (1.0 - ADAM_B2) * (g * g)
        g_ref[...] = g
        d_ref[...] = -ADAM_LR * ((mn / c1) / (jnp.sqrt(vn / c2) + ADAM_EPS) + ADAM_WD * w_ref[...])
        mn_ref[...] = mn
        vn_ref[...] = vn

    outs = pl.pallas_call(
        body, name=name,
        grid_spec=pltpu.PrefetchScalarGridSpec(
            num_scalar_prefetch=len(scalars), grid=(r // tm,),
            in_specs=[row] * 3 + [own_spec] + [pl.BlockSpec((e.shape[0], tm, n), lambda i, *_: (0, i, 0)) for e in extra],
            out_specs=[row] * 4),
        out_shape=[jax.ShapeDtypeStruct((r, n), f32)] * 4,
        compiler_params=_cp("parallel"),
    )(*scalars, w2, m2, v2, o2, *extra)
    return [o.reshape(shape) for o in outs]


def _pack(arrays):
    flat = jnp.concatenate([a.reshape(-1).astype(f32) for a in arrays])
    pad = (-flat.shape[0]) % (128 * (512 if flat.shape[0] > 128 * 512 else 8))
    return jnp.pad(flat, (0, pad)).reshape(-1, 128)


def _unpack(packed, shapes):
    flat = packed.reshape(-1)
    out, off = [], 0
    for s in shapes:
        n = math.prod(s)
        out.append(flat[off:off + n].reshape(s))
        off += n
    return out


def _local_step(x, target, w, weights_of, send, last_small, on_loss, nb, s):
    cos, sin = _rope_tables(s)
    g = {}
    ffn_saved = {}
    ffn_bufs = [lax.empty((N_DEV, 2, 2) + shp, f32)
                for shp in ((D_MODEL, FF_SHARD), (D_MODEL, FF_SHARD), (FF_SHARD, D_MODEL))]

    def ffn(xin, l, h, wts):
        y, a, b = _ffn_fwd(f"ffn_fwd_{l}{h}", xin, w["ffn_g"][l][h], *wts)
        ffn_saved[(l, h)] = (xin, a, b, wts)
        return y

    def ffn_back(dy, l, h):
        xin, a, b, wts = ffn_saved[(l, h)]
        dx, dg, hb, dyh, u, da, db = _ffn_dx(f"ffn_dx_{l}{h}", dy, xin, w["ffn_g"][l][h], *wts, a, b)
        g[f"ffn_g_{l}{h}"] = dg
        if (l, h) != (0, 0):
            halves = {}
            for k, (n, xt, y) in enumerate((("ffn_w1", hb, da), ("ffn_w3", hb, db), ("ffn_w2", u, dyh))):
                ffn_bufs[k], halves[n] = _ffn_dw_one(f"ffn_dw_{l}{h}_{n[4:]}", xt, y, ffn_bufs[k], l, h)
            return send(f"ffn_{l}{h}", halves, dx)
        hb = last_small(g, hb)
        ffn_bufs[0], half = _ffn_dw_one("ffn_dw_00_w1", hb, da, ffn_bufs[0], l, h)
        hb = send("ffn_00_w1", {"ffn_w1": half}, hb)
        ffn_bufs[1], half = _ffn_dw_one("ffn_dw_00_w3", hb, db, ffn_bufs[1], l, h)
        u = send("ffn_00_w3", {"ffn_w3": half}, u)
        ffn_bufs[2], half = _ffn_dw_one("ffn_dw_00_w2", u, dyh, ffn_bufs[2], l, h)
        return send("ffn_00_w2", {"ffn_w2": half}, dx)

    def slots(t):
        return t.reshape(N_DEV, D_MODEL // N_DEV, D_MODEL)

    x1 = ffn(x, 0, 0, weights_of(0, [])["ffn"])
    wg = weights_of(1, [x1])
    w_in, w_out = wg["w_in"], wg["w_out"]
    _, h0b = _norm_fwd("mix_norm_0", x1, w["mix_g"][0])
    proj = _mm("in_proj", h0b, w_in, "nn", tn=1536)[0]
    o_raw, rprev, mret = _ret_fwd(proj, cos, sin, w["ret_g"], nb, s)
    lru = _lru_fwd(proj, w["conv_w"], w["conv_b"], w["lru_w_a"], w["lru_b_a"], w["lru_w_i"], w["lru_b_i"], w["lru_lam"], nb, s)
    merged = _ew("merge", lambda a, b: (jnp.concatenate([a, b], axis=1),), [mret, lru], [(D_MODEL, bf16)])[0]
    x2 = _mm("out_proj", merged, w_out, "nn", extras=[x1], epilogue=lambda acc, r: (acc + r,))[0]
    x3 = ffn(x2, 0, 1, weights_of(2, [x2])["ffn"])
    x4 = ffn(x3, 1, 0, weights_of(3, [x3])["ffn"])
    u, _ = _norm_fwd("mix_norm_1", x4, w["mix_g"][1])
    lbr, lbi, bbr, bbi = _s5_prep(w["s5_lr"], w["s5_li"], w["s5_ldt"], w["s5_bre"], w["s5_bim"])
    lbr_f, lbi_f = lbr.reshape(1, -1), lbi.reshape(1, -1)
    wbr, wbi = _blockdiag(bbr).astype(bf16), _blockdiag(bbi).astype(bf16)
    wcr, wci = _blockdiag(w["s5_cre"]).astype(bf16), _blockdiag(w["s5_cim"]).astype(bf16)
    ygb, ypre, h0s = _s5_fwd(u, lbr_f, lbi_f, wbr, wbi, wcr, wci, w["s5_d"], nb, s)
    wg = weights_of(4, [ygb])
    glu_a, glu_b = wg["glu_a"], wg["glu_b"]
    x5, gp, gq = _glu_fwd(ygb, glu_a, glu_b, x4)
    x6 = ffn(x5, 1, 1, weights_of(5, [x5])["ffn"])
    loss, dx6, g["final_g"] = _final_loss(x6, w["final_g"], target)
    dx6 = on_loss(loss, dx6)

    dx5 = ffn_back(dx6, 1, 1)

    def glu_bwd(d, p, q):
        sg = jax.nn.sigmoid(q)
        return d * sg, d * p * sg * (1.0 - sg)

    dp, dq = _ew("glu_bwd", glu_bwd, [dx5, gp, gq], [(D_MODEL, bf16), (D_MODEL, bf16)])
    dyg = _mm("glu_dy_a", dp, glu_a, "nt")[0]
    dyg = _mm("glu_dy_b", dq, glu_b, "nt", extras=[dyg], epilogue=lambda acc, r: (acc + r,))[0]
    g["glu_a"], ga_half = _mm_tn("glu_dw_a", ygb, dp)
    g["glu_b"], gb_half = _mm_tn("glu_dw_b", ygb, dq)
    dyg = send("glu", {"glu_a": slots(ga_half), "glu_b": slots(gb_half)}, dyg)
    du, dlr, dli, dwbr, dwbi, dwcr, dwci, g["s5_d"] = _s5_bwd(dyg, ypre, u, h0s, lbr_f, lbi_f, wbr, wbi, wcr, wci, w["s5_d"], nb, s)
    g["s5_cre"], g["s5_cim"] = _blockdiag_t(dwcr), _blockdiag_t(dwci)
    g["s5_lr"], g["s5_li"], g["s5_ldt"], g["s5_bre"], g["s5_bim"] = _s5_prep_bwd(
        w["s5_lr"], w["s5_li"], w["s5_ldt"], w["s5_bre"], w["s5_bim"],
        (dlr.reshape(S5_GROUPS, S5_STATE), dli.reshape(S5_GROUPS, S5_STATE), _blockdiag_t(dwbr), _blockdiag_t(dwbi)))
    dx4, g["mix_g_1"] = _norm_bwd("mix_norm_1_bwd", du, x4, w["mix_g"][1], dx5)
    dx3 = ffn_back(dx4, 1, 0)
    dx2 = ffn_back(dx3, 0, 1)
    dmerged = _mm("out_proj_dx", dx2, w_out, "nt")[0]
    g["w_out"], wo_half = _mm_tn("out_proj_dw", merged, dx2)
    dmerged = send("w_out", {"w_out": slots(wo_half)}, dmerged)
    dq_, dk_, dv_, dgate, g["ret_g"] = _ret_bwd(dmerged, o_raw, rprev, proj, cos, sin, w["ret_g"], nb, s)
    (dxl, dgl, g["conv_w"], g["conv_b"], g["lru_w_a"], g["lru_b_a"], g["lru_w_i"], g["lru_b_i"], g["lru_lam"]) = _lru_bwd(
        dmerged, proj, w["conv_w"], w["conv_b"], w["lru_w_a"], w["lru_b_a"], w["lru_w_i"], w["lru_b_i"], w["lru_lam"], nb, s)
    dproj = _ew("dproj", lambda *p: (jnp.concatenate(p, axis=1),), [dq_, dk_, dv_, dgate, dxl, dgl], [(3072, bf16)])[0]
    dh0 = _mm("in_proj_dx", dproj, w_in, "nt")[0]
    g["w_in"], wi_half = _mm_tn("in_proj_dw", h0b, dproj)
    dh0 = send("w_in", {"w_in": jnp.transpose(wi_half.reshape(D_MODEL, N_DEV, IN_SHARD), (1, 0, 2))}, dh0)
    dx1, g["mix_g_0"] = _norm_bwd("mix_norm_0_bwd", dh0, x1, w["mix_g"][0], dx2)
    dx0 = ffn_back(dx1, 0, 0)
    g["ffn_w1"], g["ffn_w3"], g["ffn_w2"] = ffn_bufs
    return loss, dx0, g


_WEIGHTS = ["ffn_norm_g", "ffn_w1", "ffn_w3", "ffn_w2", "mix_norm_g", "w_in_even", "w_out_even", "ret_norm_g", "conv_w",
            "conv_b", "lru_w_a", "lru_b_a", "lru_w_i", "lru_b_i", "lru_lambda", "s5_lambda_re", "s5_lambda_im", "s5_log_dt",
            "s5_b_re", "s5_b_im", "s5_c_re", "s5_c_im", "s5_d", "glu_w_a", "glu_w_b", "final_norm_g"]
_BIG = ["ffn_w1", "ffn_w3", "ffn_w2", "w_in_even", "w_out_even", "glu_w_a", "glu_w_b"]
_SMALL_SHARDED = ["ffn_norm_g", "conv_w", "s5_d"]
_SMALL = [n for n in _WEIGHTS if n not in _BIG]
_MIDSIZE = ["lru_w_a", "lru_w_i", "s5_b_re", "s5_b_im", "s5_c_re", "s5_c_im"]


def kernel(x, ffn_norm_g, ffn_w1, ffn_w3, ffn_w2, mix_norm_g, w_in_even, w_out_even, ret_norm_g, conv_w, conv_b, lru_w_a, lru_b_a, lru_w_i, lru_b_i, lru_lambda, s5_lambda_re, s5_lambda_im, s5_log_dt, s5_b_re, s5_b_im, s5_c_re, s5_c_im, s5_d, glu_w_a, glu_w_b, final_norm_g, loss_target, m_ffn_norm_g, m_ffn_w1, m_ffn_w3, m_ffn_w2, m_mix_norm_g, m_w_in_even, m_w_out_even, m_ret_norm_g, m_conv_w, m_conv_b, m_lru_w_a, m_lru_b_a, m_lru_w_i, m_lru_b_i, m_lru_lambda, m_s5_lambda_re, m_s5_lambda_im, m_s5_log_dt, m_s5_b_re, m_s5_b_im, m_s5_c_re, m_s5_c_im, m_s5_d, m_glu_w_a, m_glu_w_b, m_final_norm_g, v_ffn_norm_g, v_ffn_w1, v_ffn_w3, v_ffn_w2, v_mix_norm_g, v_w_in_even, v_w_out_even, v_ret_norm_g, v_conv_w, v_conv_b, v_lru_w_a, v_lru_b_a, v_lru_w_i, v_lru_b_i, v_lru_lambda, v_s5_lambda_re, v_s5_lambda_im, v_s5_log_dt, v_s5_b_re, v_s5_b_im, v_s5_c_re, v_s5_c_im, v_s5_d, v_glu_w_a, v_glu_w_b, v_final_norm_g):
    a = dict(locals())
    nb, s, d = x.shape
    ax, ay, ac = lax.axis_index("x"), lax.axis_index("y"), lax.axis_index("c")
    dev = 4 * ax + 2 * ay + ac
    chip = 2 * ax + ay

    def ffn_shards(l, h):
        extra = FF_PAD - FF_SHARD
        return [jnp.pad(ffn_w1[l, h].astype(bf16), ((0, 0), (0, extra))), jnp.pad(ffn_w3[l, h].astype(bf16), ((0, 0), (0, extra))),
                jnp.pad(ffn_w2[l, h].astype(bf16), ((0, extra), (0, 0)))]

    first = _all_gather("ag_first", ffn_shards(0, 0) + [_pack([ffn_norm_g, conv_w, s5_d])])
    sm = first[3].reshape(N_DEV, -1)
    ffn_g_full = jnp.transpose(sm[:, :512].reshape(N_DEV, 2, 2, 128), (1, 2, 0, 3)).reshape(2, 2, D_MODEL)
    conv_w_full = jnp.transpose(sm[:, 512:768].reshape(N_DEV, 4, 64), (1, 0, 2)).reshape(4, LRU_WIDTH)
    s5_d_full = sm[:, 768:896].reshape(1, D_MODEL)

    ag_src = [None, [w_in_even[0].astype(bf16), w_out_even[0].astype(bf16)], ffn_shards(0, 1), ffn_shards(1, 0),
              [glu_w_a[0].astype(bf16), glu_w_b[0].astype(bf16)], ffn_shards(1, 1)]
    ag, token = [None], first[0]
    for k, grp in enumerate(ag_src):
        if grp is None:
            continue
        grp[0] = _tie(f"tie_ag_{k}", grp[0], [token])
        lands = [lax.dynamic_update_index_in_dim(lax.empty((N_DEV,) + t.shape, bf16), t, dev, 0) for t in grp]
        ag.append(_xchg_start(f"ag_start_{k}", "gather", grp, lands))
        token = ag[-1]["token"]

    def weights_of(k, after):
        if k == 0:
            return {"ffn": [first[0], _tie("tie_ag_started", first[1], [h["token"] for h in ag[1:]]), first[2]]}
        got = _xchg_wait(f"ag_wait_{k}", ag[k], after)
        if k == 1:
            return {"w_in": jnp.transpose(got[0], (1, 0, 2)).reshape(D_MODEL, N_DEV * IN_SHARD),
                    "w_out": got[1].reshape(D_MODEL, D_MODEL)}
        if k == 4:
            return {"glu_a": got[0].reshape(D_MODEL, D_MODEL), "glu_b": got[1].reshape(D_MODEL, D_MODEL)}
        return {"ffn": got}

    ffn_lands = [lax.empty((N_DEV - 1, 2, 2) + shp, bf16)
                 for shp in ((D_MODEL, FF_SHARD), (D_MODEL, FF_SHARD), (FF_SHARD, D_MODEL))]
    rs = []

    ffn_names = ("ffn_w1", "ffn_w3", "ffn_w2")

    def send(group, arrays, carry):
        srcs = list(arrays.values())
        if group.startswith("ffn_"):
            which = [ffn_names.index(n) for n in arrays]
            sfx = [(int(group[4]), int(group[5]))] * len(which)
            h = _xchg_start("rs_start_" + group, "scatter", srcs, [ffn_lands[k] for k in which], sfx)
            for k, land in zip(which, h["lands"]):
                ffn_lands[k] = land
        else:
            h = _xchg_start("rs_start_" + group, "scatter", srcs,
                            [lax.empty((N_DEV - 1,) + t.shape[1:], bf16) for t in srcs])
        rs.append((group, list(arrays), h))
        return _tie("tie_" + group, carry, [h["token"]])

    w = {
        "ffn_g": [[ffn_g_full[l, h].reshape(1, D_MODEL) for h in range(2)] for l in range(2)],
        "mix_g": [mix_norm_g[0:1], mix_norm_g[1:2]],
        "ret_g": ret_norm_g, "conv_w": conv_w_full, "conv_b": conv_b,
        "lru_w_a": lru_w_a[0], "lru_b_a": lru_b_a, "lru_w_i": lru_w_i[0], "lru_b_i": lru_b_i, "lru_lam": lru_lambda,
        "s5_lr": s5_lambda_re[0], "s5_li": s5_lambda_im[0], "s5_ldt": s5_log_dt.reshape(S5_GROUPS, 1),
        "s5_bre": jnp.swapaxes(s5_b_re[0], 1, 2), "s5_bim": jnp.swapaxes(s5_b_im[0], 1, 2),
        "s5_cre": s5_c_re[0], "s5_cim": s5_c_im[0], "s5_d": s5_d_full,
        "final_g": final_norm_g.reshape(1, D_MODEL),
    }

    small_grads = {}

    def last_small(g, carry):
        part = _small_partials(g)
        mine = _pack([part[n] for n in _SMALL])
        land = lax.dynamic_update_index_in_dim(lax.empty((N_DEV,) + mine.shape, f32), mine, dev, 0)
        h = _xchg_start("ag_start_small_grads", "gather", [mine], [land])
        small_grads.update(h=h, shapes=[part[n].shape for n in _SMALL])
        return _tie("tie_small_grads", carry, [h["token"]])

    total_loss = []

    def on_loss(part, carry):
        total_loss.append(lax.psum(part[0, 0], ("x", "y", "c")))
        return _tie("tie_loss", carry, [jnp.broadcast_to(total_loss[0], (8, 128))])

    _, dx, g = _local_step(x.reshape(nb * s, d), loss_target.reshape(nb * s, d), w, weights_of, send, last_small,
                           on_loss, nb, s)
    loss = total_loss[0]
    (gath,) = _xchg_wait("ag_wait_small_grads", small_grads["h"], [dx])
    full = dict(zip(_SMALL, _unpack(_sum8("sum_small_grads", gath), small_grads["shapes"])))
    for n in _SMALL_SHARDED:
        width = a[n].shape[-1]
        full[n] = lax.dynamic_slice_in_dim(full[n], dev * width, width, axis=full[n].ndim - 1)
    res = {n: _adamw("adamw_" + n, a[n], a["m_" + n], a["v_" + n], full[n]) for n in _MIDSIZE}
    tiny = [n for n in _SMALL if n not in _MIDSIZE]
    shapes = [a[n].shape for n in tiny]
    packed = _adamw("adamw_small", _pack([a[n] for n in tiny]), _pack([a["m_" + n] for n in tiny]),
                    _pack([a["v_" + n] for n in tiny]), _pack([full[n] for n in tiny]))
    res.update({n: vals for n, vals in zip(tiny, zip(*[_unpack(p, shapes) for p in packed]))})
    return _finish(a, g, dx, loss, res, packed, rs, ffn_lands, dev, nb, s, d)


def _small_partials(g):
    return {
        "ffn_norm_g": jnp.stack([jnp.stack([g[f"ffn_g_{l}{h}"][0] for h in range(2)]) for l in range(2)]),
        "mix_norm_g": jnp.concatenate([g["mix_g_0"], g["mix_g_1"]], axis=0),
        "ret_norm_g": g["ret_g"], "conv_w": g["conv_w"][None], "conv_b": g["conv_b"],
        "lru_w_a": g["lru_w_a"][None], "lru_b_a": g["lru_b_a"], "lru_w_i": g["lru_w_i"][None], "lru_b_i": g["lru_b_i"],
        "lru_lambda": g["lru_lam"], "s5_lambda_re": g["s5_lr"][None], "s5_lambda_im": g["s5_li"][None],
        "s5_log_dt": g["s5_ldt"].reshape(1, S5_GROUPS),
        "s5_b_re": jnp.swapaxes(g["s5_bre"], 1, 2)[None], "s5_b_im": jnp.swapaxes(g["s5_bim"], 1, 2)[None],
        "s5_c_re": g["s5_cre"][None], "s5_c_im": g["s5_cim"][None], "s5_d": g["s5_d"], "final_norm_g": g["final_g"][0],
    }


def _finish(a, g, dx, loss, res, packed, rs, ffn_lands, dev, nb, s, d):
    landed = {}
    for group, names, h in rs:
        if not group.startswith("ffn_"):
            landed.update(zip(names, _xchg_wait("rs_wait_" + group, h, [dx])))
    kinds = {"ffn_w1": "lead", "ffn_w3": "lead", "ffn_w2": "lead", "w_in": "cols", "w_out": "rows", "glu_a": "rows",
             "glu_b": "rows"}

    def update(n, short):
        res[n] = _adamw("adamw_" + n, a[n], a["m_" + n], a["v_" + n], g[short],
                        landed[short].reshape((N_DEV - 1,) + a[n].shape), slot=(dev, kinds[short]))

    for n, short in zip(_BIG[3:], ("w_in", "w_out", "glu_a", "glu_b")):
        update(n, short)
    after = [dx, packed[0]] + [res[n][0] for n in _BIG[3:] + _MIDSIZE]
    ffn_names = ("ffn_w1", "ffn_w3", "ffn_w2")
    for group, names, h in rs:
        if group.startswith("ffn_") and len(names) == 3:
            ffn_lands[:] = _xchg_wait("rs_wait_" + group, h, after, ffn_lands)
    for k, n in enumerate(ffn_names):
        for group, names, h in rs:
            if group.startswith("ffn_") and names == [n]:
                (ffn_lands[k],) = _xchg_wait("rs_wait_" + group, h, after, [ffn_lands[k]])
        landed[n] = ffn_lands[k]
        update(n, n)
        after = after + [res[n][0]]

    out = [loss, dx.reshape(nb, s, d)]
    for k in range(4):
        out += [res[n][k] for n in _WEIGHTS]
    return tuple(out)
```

```python
import functools
import math

import numpy as np
import jax
import jax.numpy as jnp
from jax import lax
from jax.experimental import pallas as pl
from jax.experimental.pallas import tpu as pltpu

f32 = jnp.float32
bf16 = jnp.bfloat16

D_MODEL = 1024
N_DEV = 8
EPS = 1e-6
RET_HEADS = 4
HEAD_DIM = 128
RET_WIDTH = 512
RET_CHUNK = 128
ROPE_BASE = 10000.0
LRU_WIDTH = 512
LRU_BLOCKS = 4
LRU_C = 8.0
S5_GROUP = 16
S5_GROUPS = 64
S5_STATE = 64
S5_CHUNK = 1024
S5_BLOCKS = 8
S5_BLOCK_STATES = 512
SUBLANES = 8
D_FF = 2816
FF_SHARD = D_FF // N_DEV
FF_PAD = 384
IN_SHARD = 3072 // N_DEV
ADAM_LR = 0.001
ADAM_B1 = 0.9
ADAM_B2 = 0.999
ADAM_EPS = 1e-08
ADAM_WD = 0.01
ADAM_STEP = 10

VMEM_LIMIT = 56 * 1024 * 1024
VMEM_SPEC = pl.BlockSpec(memory_space=pltpu.VMEM)
ANY_SPEC = pl.BlockSpec(memory_space=pl.ANY)
HBM_SPEC = pl.BlockSpec(memory_space=pltpu.HBM)
SEM_SPEC = pl.BlockSpec(memory_space=pltpu.SEMAPHORE)
SIDE_EFFECT = pltpu.SideEffectType.DATAFLOW_SIDE_EFFECTING
MESH = pl.DeviceIdType.MESH


def _cp(*sem):
    return pltpu.CompilerParams(dimension_semantics=sem, vmem_limit_bytes=VMEM_LIMIT)


def _nn(a, b):
    return jnp.dot(a, b, preferred_element_type=f32)


def _nt(a, b):
    return lax.dot_general(a, b, (((1,), (1,)), ((), ())), preferred_element_type=f32)


def _tn(a, b):
    return lax.dot_general(a, b, (((0,), (0,)), ((), ())), preferred_element_type=f32)


def _rms_fwd(x, g):
    r = lax.rsqrt(jnp.mean(x * x, axis=-1, keepdims=True) + EPS)
    xn = x * r
    return xn * g, xn, r


def _rms_bwd(dh, xn, r, g):
    dxn = dh * g
    dx = r * (dxn - xn * jnp.mean(dxn * xn, axis=-1, keepdims=True))
    dg = jnp.sum(dh * xn, axis=0, keepdims=True)
    return dx, dg


def _shift_dn(v, d, row, fill=0.0):
    return jnp.where(row >= d, pltpu.roll(v, d, 0), fill)


def _shift_up(v, d, row, fill=0.0):
    n = v.shape[0]
    return jnp.where(row < n - d, pltpu.roll(v, n - d, 0), fill)


def _ew(name, fn, ins, outs, tm=512):
    t = ins[0].shape[0]
    n_in = len(ins)

    def body(*refs):
        res = fn(*[r[...] for r in refs[:n_in]])
        for o, v in zip(refs[n_in:], res):
            o[...] = v.astype(o.dtype)

    return pl.pallas_call(
        body, name=name, grid=(t // tm,),
        in_specs=[pl.BlockSpec((tm, a.shape[1]), lambda i: (i, 0)) for a in ins],
        out_specs=[pl.BlockSpec((tm, n), lambda i: (i, 0)) for n, _ in outs],
        out_shape=[jax.ShapeDtypeStruct((t, n), dt) for n, dt in outs],
        compiler_params=_cp("parallel"),
    )(*ins)


def _mm(name, x, w, kind, extras=(), epilogue=None, outs=None, tm=512, tn=1024):
    t = x.shape[0]
    n = w.shape[1] if kind == "nn" else w.shape[0]
    tn = min(tn, n)
    outs = outs or [f32]
    n_ex = len(extras)

    def body(x_ref, w_ref, *refs):
        xb = x_ref[...].astype(bf16)
        acc = _nn(xb, w_ref[...]) if kind == "nn" else _nt(xb, w_ref[...])
        res = epilogue(acc, *[r[...] for r in refs[:n_ex]]) if epilogue else (acc,)
        for o, v in zip(refs[n_ex:], res):
            o[...] = v.astype(o.dtype)

    w_spec = (pl.BlockSpec((w.shape[0], tn), lambda i, j: (0, j)) if kind == "nn"
              else pl.BlockSpec((tn, w.shape[1]), lambda i, j: (j, 0)))
    tile = pl.BlockSpec((tm, tn), lambda i, j: (i, j))
    return pl.pallas_call(
        body, name=name, grid=(t // tm, n // tn),
        in_specs=[pl.BlockSpec((tm, x.shape[1]), lambda i, j: (i, 0)), w_spec] + [tile] * n_ex,
        out_specs=[tile] * len(outs),
        out_shape=[jax.ShapeDtypeStruct((t, n), dt) for dt in outs],
        compiler_params=_cp("parallel", "parallel"),
    )(x, w, *extras)


def _mm_tn(name, x, y, tk=1024, tn=1024, tt=1024):
    t, k = x.shape
    n = y.shape[1]
    tk, tn, tt = min(tk, k), min(tn, n), min(tt, t)

    def body(x_ref, y_ref, o_ref, ob_ref):
        @pl.when(pl.program_id(2) == 0)
        def _():
            o_ref[...] = jnp.zeros_like(o_ref)
        o_ref[...] += _tn(x_ref[...].astype(bf16), y_ref[...].astype(bf16))

        @pl.when(pl.program_id(2) == pl.num_programs(2) - 1)
        def _():
            ob_ref[...] = o_ref[...].astype(bf16)

    out = pl.BlockSpec((tk, tn), lambda i, j, s: (i, j))
    return pl.pallas_call(
        body, name=name, grid=(k // tk, n // tn, t // tt),
        in_specs=[pl.BlockSpec((tt, tk), lambda i, j, s: (s, i)), pl.BlockSpec((tt, tn), lambda i, j, s: (s, j))],
        out_specs=[out, out],
        out_shape=[jax.ShapeDtypeStruct((k, n), f32), jax.ShapeDtypeStruct((k, n), bf16)],
        compiler_params=_cp("parallel", "parallel", "arbitrary"),
    )(x, y)


def _norm_fwd(name, x, g, tm=512):
    t, d = x.shape

    def body(x_ref, g_ref, h_ref, hb_ref):
        h, _, _ = _rms_fwd(x_ref[...], g_ref[...])
        h_ref[...] = h
        hb_ref[...] = h.astype(bf16)

    row = pl.BlockSpec((tm, d), lambda i: (i, 0))
    return pl.pallas_call(
        body, name=name, grid=(t // tm,),
        in_specs=[row, pl.BlockSpec((1, d), lambda i: (0, 0))],
        out_specs=[row, row],
        out_shape=[jax.ShapeDtypeStruct((t, d), f32), jax.ShapeDtypeStruct((t, d), bf16)],
        compiler_params=_cp("parallel"),
    )(x, g)


def _norm_bwd(name, dh, x, g, dres, tm=512):
    t, d = x.shape

    def body(dh_ref, x_ref, g_ref, dres_ref, dx_ref, dg_ref):
        gv = g_ref[...]
        _, xn, r = _rms_fwd(x_ref[...], gv)
        dx, dg = _rms_bwd(dh_ref[...], xn, r, gv)
        dx_ref[...] = dres_ref[...] + dx

        @pl.when(pl.program_id(0) == 0)
        def _():
            dg_ref[...] = jnp.zeros_like(dg_ref)
        dg_ref[...] += dg

    row = pl.BlockSpec((tm, d), lambda i: (i, 0))
    vec = pl.BlockSpec((1, d), lambda i: (0, 0))
    return pl.pallas_call(
        body, name=name, grid=(t // tm,),
        in_specs=[row, row, vec, row],
        out_specs=[row, vec],
        out_shape=[jax.ShapeDtypeStruct((t, d), f32), jax.ShapeDtypeStruct((1, d), f32)],
        compiler_params=_cp("arbitrary"),
    )(dh, x, g, dres)


def _final_loss(x, g, target, tm=512):
    t, d = x.shape

    def body(x_ref, g_ref, t_ref, loss_ref, dx_ref, dg_ref):
        gv = g_ref[...]
        y, xn, r = _rms_fwd(x_ref[...], gv)
        err = y - t_ref[...]
        dy = err * (1.0 / d)
        dx, dg = _rms_bwd(dy, xn, r, gv)
        dx_ref[...] = dx

        @pl.when(pl.program_id(0) == 0)
        def _():
            dg_ref[...] = jnp.zeros_like(dg_ref)
            loss_ref[...] = jnp.zeros_like(loss_ref)
        dg_ref[...] += dg
        loss_ref[...] += jnp.full((1, 128), 0.5 / d, f32) * jnp.sum(err * err)

    row = pl.BlockSpec((tm, d), lambda i: (i, 0))
    vec = pl.BlockSpec((1, d), lambda i: (0, 0))
    return pl.pallas_call(
        body, name="final_loss", grid=(t // tm,),
        in_specs=[row, vec, row],
        out_specs=[pl.BlockSpec((1, 128), lambda i: (0, 0)), row, vec],
        out_shape=[jax.ShapeDtypeStruct((1, 128), f32), jax.ShapeDtypeStruct((t, d), f32),
                   jax.ShapeDtypeStruct((1, d), f32)],
        compiler_params=_cp("arbitrary"),
    )(x, g, target)


def _load_ffn_weights(hbm_refs, vmem_refs, sems):
    first_step = pl.program_id(0) == 0

    def copies_of(p):
        out = []
        for k, (src, dst) in enumerate(zip(hbm_refs, vmem_refs)):
            for j in (2 * p, 2 * p + 1):
                half = pl.ds((j % 2) * FF_PAD, FF_PAD)
                window = dst.at[p, half, :] if k == 2 else dst.at[p, :, half]
                out.append(pltpu.make_async_copy(src.at[j], window, sems.at[k * N_DEV + j]))
        return out

    @pl.when(first_step)
    def _():
        for p in range(N_DEV // 2):
            for cp in copies_of(p):
                cp.start()

    def ready(p):
        @pl.when(first_step)
        def _():
            for cp in copies_of(p):
                cp.wait()

    return ready


def _ffn_weight_scratch(nj, d, ff):
    return [pltpu.VMEM((nj, d, ff), bf16), pltpu.VMEM((nj, d, ff), bf16), pltpu.VMEM((nj, ff, d), bf16),
            pltpu.SemaphoreType.DMA((3 * N_DEV,))]


def _ffn_fwd(name, x, g, w1, w3, w2, tm=512):
    t, d = x.shape
    nj, ff = N_DEV // 2, 2 * FF_PAD

    def body(x_ref, g_ref, w1_hbm, w3_hbm, w2_hbm, y_ref, a_ref, b_ref, w1_ref, w3_ref, w2_ref, sems):
        ready = _load_ffn_weights((w1_hbm, w3_hbm, w2_hbm), (w1_ref, w3_ref, w2_ref), sems)
        xv = x_ref[...]
        h, _, _ = _rms_fwd(xv, g_ref[...])
        hb = h.astype(bf16)
        acc = jnp.zeros((tm, d), f32)
        for j in range(nj):
            ready(j)
            a = _nn(hb, w1_ref[j])
            b = _nn(hb, w3_ref[j])
            a_ref[j] = a.astype(bf16)
            b_ref[j] = b.astype(bf16)
            u = (a * jax.nn.sigmoid(a) * b).astype(bf16)
            acc = acc + _nn(u, w2_ref[j])
        y_ref[...] = xv + 0.5 * acc

    row = pl.BlockSpec((tm, d), lambda i: (i, 0))
    mid = pl.BlockSpec((nj, tm, ff), lambda i: (0, i, 0))
    return pl.pallas_call(
        body, name=name, grid=(t // tm,),
        in_specs=[row, pl.BlockSpec((1, d), lambda i: (0, 0)), ANY_SPEC, ANY_SPEC, ANY_SPEC],
        out_specs=[row, mid, mid],
        out_shape=[jax.ShapeDtypeStruct((t, d), f32), jax.ShapeDtypeStruct((nj, t, ff), bf16),
                   jax.ShapeDtypeStruct((nj, t, ff), bf16)],
        scratch_shapes=_ffn_weight_scratch(nj, d, ff),
        compiler_params=_cp("arbitrary"),
    )(x, g, w1, w3, w2)


def _ffn_dx(name, dy, x, g, w1, w3, w2, a, b, tm=256):
    t, d = x.shape
    nj, ff = N_DEV // 2, 2 * FF_PAD

    def body(dy_ref, x_ref, g_ref, w1_hbm, w3_hbm, w2_hbm, a_ref, b_ref,
             dx_ref, dg_ref, hbt_ref, dyh_ref, ut_ref, da_ref, db_ref, w1_ref, w3_ref, w2_ref, sems):
        ready = _load_ffn_weights((w1_hbm, w3_hbm, w2_hbm), (w1_ref, w3_ref, w2_ref), sems)
        gv = g_ref[...]
        h, xn, r = _rms_fwd(x_ref[...], gv)
        hbt_ref[...] = h.astype(bf16).T
        dyv = dy_ref[...]
        dyh = (0.5 * dyv).astype(bf16)
        dyh_ref[...] = dyh
        dh = jnp.zeros((tm, d), f32)
        dus = []
        for j in range(nj):
            ready(j)
            dus.append(_nt(dyh, w2_ref[j]))
        for j in range(nj):
            av = a_ref[j].astype(f32)
            bv = b_ref[j].astype(f32)
            s = jax.nn.sigmoid(av)
            silu = av * s
            ut_ref[j] = (silu * bv).astype(bf16).T
            du = dus[j]
            dab = (du * bv * (s * (1.0 + av * (1.0 - s)))).astype(bf16)
            dbb = (du * silu).astype(bf16)
            da_ref[j] = dab
            db_ref[j] = dbb
            dh = dh + _nt(dab, w1_ref[j]) + _nt(dbb, w3_ref[j])
        dx, dg = _rms_bwd(dh, xn, r, gv)
        dx_ref[...] = dyv + dx

        @pl.when(pl.program_id(0) == 0)
        def _():
            dg_ref[...] = jnp.zeros_like(dg_ref)
        dg_ref[...] += dg

    row = pl.BlockSpec((tm, d), lambda i: (i, 0))
    vec = pl.BlockSpec((1, d), lambda i: (0, 0))
    mid = pl.BlockSpec((nj, tm, ff), lambda i: (0, i, 0))
    mid_shape = jax.ShapeDtypeStruct((nj, t, ff), bf16)
    return pl.pallas_call(
        body, name=name, grid=(t // tm,),
        in_specs=[row, row, vec, ANY_SPEC, ANY_SPEC, ANY_SPEC, mid, mid],
        out_specs=[row, vec, pl.BlockSpec((d, tm), lambda i: (0, i)), row,
                   pl.BlockSpec((nj, ff, tm), lambda i: (0, 0, i)), mid, mid],
        out_shape=[jax.ShapeDtypeStruct((t, d), f32), jax.ShapeDtypeStruct((1, d), f32),
                   jax.ShapeDtypeStruct((d, t), bf16), jax.ShapeDtypeStruct((t, d), bf16),
                   jax.ShapeDtypeStruct((nj, ff, t), bf16), mid_shape, mid_shape],
        scratch_shapes=_ffn_weight_scratch(nj, d, ff),
        compiler_params=_cp("arbitrary"),
    )(dy, x, g, w1, w3, w2, a, b)


def _ffn_dw_one(name, xt, y, buf, l, h, tt=2048):
    t = y.shape[-2]
    tt = min(tt, t)
    cut_cols = xt.ndim == 2

    def body(x_ref, y_ref, buf_ref, o_ref, ob_ref, acc):
        s = pl.program_id(1)
        prod = _nn(x_ref[0] if xt.ndim == 3 else x_ref[...], y_ref[0] if y.ndim == 3 else y_ref[...])

        @pl.when(s == 0)
        def _():
            acc[...] = prod

        @pl.when(s > 0)
        def _():
            acc[...] += prod

        @pl.when(s == pl.num_programs(1) - 1)
        def _():
            total = acc[...]
            for e in range(2):
                lo = e * FF_PAD
                part = total[:, lo:lo + FF_SHARD] if cut_cols else total[lo:lo + FF_SHARD, :]
                o_ref[e] = part
                ob_ref[e] = part.astype(bf16)

    x_spec = (pl.BlockSpec((1, xt.shape[1], tt), lambda p, s: (p, 0, s)) if xt.ndim == 3
              else pl.BlockSpec((xt.shape[0], tt), lambda p, s: (0, s)))
    y_spec = (pl.BlockSpec((1, tt, y.shape[2]), lambda p, s: (p, s, 0)) if y.ndim == 3
              else pl.BlockSpec((tt, y.shape[1]), lambda p, s: (s, 0)))
    k_, n_ = buf.shape[-2:]
    return pl.pallas_call(
        body, name=name, grid=(N_DEV // 2, t // tt),
        in_specs=[x_spec, y_spec, ANY_SPEC],
        out_specs=[pl.BlockSpec((2, None, None, k_, n_), lambda p, s: (p, l, h, 0, 0)),
                   pl.BlockSpec((2, k_, n_), lambda p, s: (p, 0, 0))],
        out_shape=[jax.ShapeDtypeStruct(buf.shape, buf.dtype), jax.ShapeDtypeStruct((N_DEV, k_, n_), bf16)],
        input_output_aliases={2: 0},
        scratch_shapes=[pltpu.VMEM((xt.shape[-2], y.shape[-1]), f32)],
        compiler_params=_cp("parallel", "arbitrary"),
    )(xt, y, buf)


_LOG_GAMMA = [float(np.log1p(-np.float32(2.0) ** np.float32(-5.0 - h))) for h in range(RET_HEADS)]


def _ret_consts(h):
    lg = jnp.where(h == 0, _LOG_GAMMA[0], jnp.where(h == 1, _LOG_GAMMA[1],
                   jnp.where(h == 2, _LOG_GAMMA[2], _LOG_GAMMA[3]))).astype(f32)
    c = RET_CHUNK
    r = lax.broadcasted_iota(jnp.int32, (c, c), 0)
    cc = lax.broadcasted_iota(jnp.int32, (c, c), 1)
    decay = jnp.where(r >= cc, jnp.exp(lg * jnp.maximum((r - cc).astype(f32), 0.0)), 0.0)
    pos = lax.broadcasted_iota(jnp.int32, (c, 1), 0).astype(f32)
    kd = jnp.exp(lg * (c - 1.0 - pos))
    qd = jnp.exp(lg * (pos + 1.0))
    gc = jnp.exp(lg * c)
    return decay, kd, qd, gc


def _rope(x, cos, sin):
    return x * cos + pltpu.roll(x, HEAD_DIM // 2, 1) * sin


def _rope_t(g, cos, sin):
    return g * cos + pltpu.roll(g * sin, HEAD_DIM // 2, 1)


def _rope_tables(s):
    half = HEAD_DIM // 2
    inv = ROPE_BASE ** (-jnp.arange(half, dtype=f32) / half)
    ang = jnp.arange(s, dtype=f32)[:, None] * inv[None, :]
    cos, sin = jnp.cos(ang), jnp.sin(ang)
    return jnp.concatenate([cos, cos], axis=1), jnp.concatenate([-sin, sin], axis=1)


def _head_ln(o):
    mu = jnp.mean(o, axis=-1, keepdims=True)
    oc = o - mu
    rs = lax.rsqrt(jnp.mean(oc * oc, axis=-1, keepdims=True) + EPS)
    return oc * rs, rs


def _ret_fwd(proj, cos, sin, ret_g, nb, s):
    c = RET_CHUNK
    nc = s // c
    t = nb * s
    scale = HEAD_DIM ** -0.5

    def body(q_ref, k_ref, v_ref, gate_ref, cos_ref, sin_ref, g_ref, o_ref, rprev_ref, m_ref):
        decay, kd, qd, gc = _ret_consts(pl.program_id(0))
        gv = g_ref[...]

        def chunk(b, n, rv):
            rows = pl.ds(pl.multiple_of(b * s + n * c, c), c)
            pos = pl.ds(pl.multiple_of(n * c, c), c)
            cs, sn = cos_ref[pos, :], sin_ref[pos, :]
            q = _rope(q_ref[rows, :], cs, sn)
            k = _rope(k_ref[rows, :], cs, sn) * scale
            vb = v_ref[rows, :].astype(bf16)
            sc = _nt(q.astype(bf16), k.astype(bf16)) * decay
            rprev_ref[b, n] = rv
            o = _nn(sc.astype(bf16), vb) + _nn((q * qd).astype(bf16), rv.astype(bf16))
            o_ref[rows, :] = o
            y, _ = _head_ln(o)
            gate = gate_ref[rows, :]
            m_ref[rows, :] = y * gv * (gate * jax.nn.sigmoid(gate))
            return rv * gc + _tn((k * kd).astype(bf16), vb)

        def step(n, carry):
            return tuple(chunk(b, n, carry[b]) for b in range(nb))

        lax.fori_loop(0, nc, step, (jnp.zeros((HEAD_DIM, HEAD_DIM), f32),) * nb)

    def col(off):
        return pl.BlockSpec((t, HEAD_DIM), lambda h: (0, off + h))

    tab = pl.BlockSpec((s, HEAD_DIM), lambda h: (0, 0))
    return pl.pallas_call(
        body, name="ret_fwd", grid=(RET_HEADS,),
        in_specs=[col(0), col(4), col(8), col(12), tab, tab, pl.BlockSpec((1, HEAD_DIM), lambda h: (0, h))],
        out_specs=[col(0), pl.BlockSpec((nb, None, nc, HEAD_DIM, HEAD_DIM), lambda h: (0, h, 0, 0, 0)), col(0)],
        out_shape=[jax.ShapeDtypeStruct((t, RET_WIDTH), f32),
                   jax.ShapeDtypeStruct((nb, RET_HEADS, nc, HEAD_DIM, HEAD_DIM), f32),
                   jax.ShapeDtypeStruct((t, RET_WIDTH), f32)],
        compiler_params=_cp("parallel"),
    )(proj, proj, proj, proj, cos, sin, ret_g)


def _ret_bwd(dmerged, o_raw, rprev, proj, cos, sin, ret_g, nb, s):
    c = RET_CHUNK
    nc = s // c
    t = nb * s
    scale = HEAD_DIM ** -0.5

    def body(dm_ref, o_ref, rprev_ref, q_ref, k_ref, v_ref, gate_ref, cos_ref, sin_ref, g_ref,
             dq_ref, dk_ref, dv_ref, dgate_ref, dg_ref):
        decay, kd, qd, gc = _ret_consts(pl.program_id(0))
        gv = g_ref[...]

        def chunk(b, n, drn, dg):
            rows = pl.ds(pl.multiple_of(b * s + n * c, c), c)
            pos = pl.ds(pl.multiple_of(n * c, c), c)
            cs, sn = cos_ref[pos, :], sin_ref[pos, :]
            q = _rope(q_ref[rows, :], cs, sn)
            k = _rope(k_ref[rows, :], cs, sn) * scale
            qb, kb = q.astype(bf16), k.astype(bf16)
            vb = v_ref[rows, :].astype(bf16)
            sc = _nt(qb, kb) * decay
            y, rs = _head_ln(o_ref[rows, :])
            gate = gate_ref[rows, :]
            sg = jax.nn.sigmoid(gate)
            silu = gate * sg
            dm = dm_ref[rows, :]
            dgate_ref[rows, :] = dm * y * gv * (sg * (1.0 + gate * (1.0 - sg)))
            dyl = dm * gv * silu
            dg = dg + jnp.sum(dm * y * silu, axis=0, keepdims=True)
            do = rs * (dyl - jnp.mean(dyl, axis=-1, keepdims=True) - y * jnp.mean(dyl * y, axis=-1, keepdims=True))
            dob = do.astype(bf16)
            rv = rprev_ref[b, n]
            drb = drn.astype(bf16)
            ds = (_nt(dob, vb) * decay).astype(bf16)
            kdb = (k * kd).astype(bf16)
            qdb = (q * qd).astype(bf16)
            dq_r = _nn(ds, kb) + _nt(dob, rv.astype(bf16)) * qd
            dk_r = _tn(ds, qb) + _nt(vb, drb) * kd
            dv_ref[rows, :] = _tn(sc.astype(bf16), dob) + _nn(kdb, drb)
            dq_ref[rows, :] = _rope_t(dq_r, cs, sn)
            dk_ref[rows, :] = _rope_t(dk_r * scale, cs, sn)
            return drn * gc + _tn(qdb, dob), dg

        def step(i, carry):
            out = [chunk(b, nc - 1 - i, *carry[b]) for b in range(nb)]
            return tuple(out)

        zero = (jnp.zeros((HEAD_DIM, HEAD_DIM), f32), jnp.zeros((1, HEAD_DIM), f32))
        done = lax.fori_loop(0, nc, step, (zero,) * nb)
        dg_ref[...] = sum(dg for _, dg in done)

    def col(off):
        return pl.BlockSpec((t, HEAD_DIM), lambda h: (0, off + h))

    tab = pl.BlockSpec((s, HEAD_DIM), lambda h: (0, 0))
    gsp = pl.BlockSpec((1, HEAD_DIM), lambda h: (0, h))
    out_t = jax.ShapeDtypeStruct((t, RET_WIDTH), f32)
    return pl.pallas_call(
        body, name="ret_bwd", grid=(RET_HEADS,),
        in_specs=[col(0), col(0), pl.BlockSpec((nb, None, nc, HEAD_DIM, HEAD_DIM), lambda h: (0, h, 0, 0, 0)),
                  col(0), col(4), col(8), col(12), tab, tab, gsp],
        out_specs=[col(0), col(0), col(0), col(0), gsp],
        out_shape=[out_t, out_t, out_t, out_t, jax.ShapeDtypeStruct((1, RET_WIDTH), f32)],
        compiler_params=_cp("parallel"),
    )(dmerged, o_raw, rprev, proj, proj, proj, proj, cos, sin, ret_g)


def _neg_expm1(z):
    series = -(z * (1.0 + z * (0.5 + z * (1.0 / 6.0 + z * (1.0 / 24.0)))))
    return jnp.where(z > -0.01, series, 1.0 - jnp.exp(z))


def _lru_gates(xc, pa, pi, lam):
    r = jax.nn.sigmoid(pa)
    i = jax.nn.sigmoid(pi)
    log_a = -LRU_C * r * jax.nn.softplus(-lam)
    a = jnp.exp(log_a)
    bx = jnp.sqrt(_neg_expm1(2.0 * log_a)) * i * xc
    return a, bx


def _scan_rows(a, b, row, up):
    sub = row[:SUBLANES] & (SUBLANES - 1)
    groups = list(range(a.shape[0] // SUBLANES))
    out = [None] * len(groups)
    edge = slice(0, 1) if up else slice(SUBLANES - 1, SUBLANES)
    carry = jnp.zeros((1, a.shape[1]), f32)
    for g in (reversed(groups) if up else groups):
        rows = slice(g * SUBLANES, (g + 1) * SUBLANES)
        xa, xb = a[rows], b[rows]
        d = 1
        while d < SUBLANES:
            keep = (sub < SUBLANES - d) if up else (sub >= d)
            shift = SUBLANES - d if up else d
            xb = xa * jnp.where(keep, pltpu.roll(xb, shift, 0), 0.0) + xb
            xa = xa * jnp.where(keep, pltpu.roll(xa, shift, 0), 1.0)
            d *= 2
        out[g] = xb + xa * carry
        carry = out[g][edge]
    return jnp.concatenate(out, axis=0)


def _scan_fwd(a, b, row):
    return _scan_rows(a, b, row, False)


def _scan_bwd(c, b, row):
    return _scan_rows(c, b, row, True)


def _conv_fwd(x, cw, cb, row):
    return (cb + cw[3:4] * x + cw[2:3] * _shift_dn(x, 1, row) + cw[1:2] * _shift_dn(x, 2, row)
            + cw[0:1] * _shift_dn(x, 3, row))


def _lru_specs(s, order):
    def im(f):
        return (lambda b, g: f(b, g)) if order == "bg" else (lambda g, b: f(b, g))
    seq = lambda off: pl.BlockSpec((s, 128), im(lambda b, g: (b, off + g)))
    vec = pl.BlockSpec((1, 128), im(lambda b, g: (0, g)))
    cw = pl.BlockSpec((4, 128), im(lambda b, g: (0, g)))
    mat = pl.BlockSpec((1, 128, 128), im(lambda b, g: (g, 0, 0)))
    return seq, vec, cw, mat


def _lru_fwd(proj, conv_w, conv_b, w_a, b_a, w_i, b_i, lam, nb, s):
    def body(x_ref, gt_ref, cw_ref, cb_ref, wa_ref, ba_ref, wi_ref, bi_ref, lam_ref, out_ref):
        row = lax.broadcasted_iota(jnp.int32, (s, 128), 0)
        xc = _conv_fwd(x_ref[...], cw_ref[...], cb_ref[...], row)
        xcb = xc.astype(bf16)
        pa = _nn(xcb, wa_ref[0].astype(bf16)) + ba_ref[...]
        pi = _nn(xcb, wi_ref[0].astype(bf16)) + bi_ref[...]
        a, bx = _lru_gates(xc, pa, pi, lam_ref[...])
        h = _scan_fwd(a, bx, row)
        out_ref[...] = h * jax.nn.gelu(gt_ref[...])

    seq, vec, cw, mat = _lru_specs(s, "bg")
    return pl.pallas_call(
        body, name="lru_fwd", grid=(nb, LRU_BLOCKS),
        in_specs=[seq(16), seq(20), cw, vec, mat, vec, mat, vec, vec],
        out_specs=seq(0),
        out_shape=jax.ShapeDtypeStruct((nb * s, LRU_WIDTH), f32),
        compiler_params=_cp("parallel", "parallel"),
    )(proj, proj, conv_w, conv_b, w_a, b_a, w_i, b_i, lam)


def _lru_bwd(dmerged, proj, conv_w, conv_b, w_a, b_a, w_i, b_i, lam, nb, s):
    def body(dout_ref, x_ref, gt_ref, cw_ref, cb_ref, wa_ref, ba_ref, wi_ref, bi_ref, lam_ref,
             dx_ref, dgt_ref, dcw_ref, dcb_ref, dwa_ref, dba_ref, dwi_ref, dbi_ref, dlam_ref):
        row = lax.broadcasted_iota(jnp.int32, (s, 128), 0)
        x = x_ref[...]
        cwv = cw_ref[...]
        xc = _conv_fwd(x, cwv, cb_ref[...], row)
        xcb = xc.astype(bf16)
        wab, wib = wa_ref[0].astype(bf16), wi_ref[0].astype(bf16)
        pa = _nn(xcb, wab) + ba_ref[...]
        pi = _nn(xcb, wib) + bi_ref[...]
        (a, bx), gates_vjp = jax.vjp(_lru_gates, xc, pa, pi, lam_ref[...])
        h = _scan_fwd(a, bx, row)
        ge, gelu_vjp = jax.vjp(jax.nn.gelu, gt_ref[...])
        dout = dout_ref[...]
        dgt_ref[...] = gelu_vjp(dout * h)[0]
        adj = _scan_bwd(_shift_up(a, 1, row), dout * ge, row)
        dxc, dpa, dpi, dlam = gates_vjp((adj * _shift_dn(h, 1, row), adj))
        dpab, dpib = dpa.astype(bf16), dpi.astype(bf16)
        dxc = dxc + _nt(dpab, wab) + _nt(dpib, wib)
        dx_ref[...] = (cwv[3:4] * dxc + cwv[2:3] * _shift_up(dxc, 1, row) + cwv[1:2] * _shift_up(dxc, 2, row)
                       + cwv[0:1] * _shift_up(dxc, 3, row))

        @pl.when(pl.program_id(1) == 0)
        def _():
            for r in (dcw_ref, dcb_ref, dwa_ref, dba_ref, dwi_ref, dbi_ref, dlam_ref):
                r[...] = jnp.zeros_like(r)
        rsum = lambda v: jnp.sum(v, axis=0, keepdims=True)
        dcw_ref[...] += jnp.concatenate([rsum(dxc * _shift_dn(x, 3, row)), rsum(dxc * _shift_dn(x, 2, row)),
                                         rsum(dxc * _shift_dn(x, 1, row)), rsum(dxc * x)], axis=0)
        dcb_ref[...] += rsum(dxc)
        dwa_ref[0] += _tn(xcb, dpab)
        dwi_ref[0] += _tn(xcb, dpib)
        dba_ref[...] += rsum(dpa)
        dbi_ref[...] += rsum(dpi)
        dlam_ref[...] += dlam

    seq, vec, cw, mat = _lru_specs(s, "gb")
    t = nb * s
    vshape = jax.ShapeDtypeStruct((1, LRU_WIDTH), f32)
    mshape = jax.ShapeDtypeStruct((LRU_BLOCKS, 128, 128), f32)
    return pl.pallas_call(
        body, name="lru_bwd", grid=(LRU_BLOCKS, nb),
        in_specs=[seq(4), seq(16), seq(20), cw, vec, mat, vec, mat, vec, vec],
        out_specs=[seq(0), seq(0), cw, vec, mat, vec, mat, vec, vec],
        out_shape=[jax.ShapeDtypeStruct((t, LRU_WIDTH), f32), jax.ShapeDtypeStruct((t, LRU_WIDTH), f32),
                   jax.ShapeDtypeStruct((4, LRU_WIDTH), f32), vshape, mshape, vshape, mshape, vshape, vshape],
        compiler_params=_cp("parallel", "arbitrary"),
    )(dmerged, proj, proj, conv_w, conv_b, w_a, b_a, w_i, b_i, lam)


def _s5_disc(lr, li, ldt, bre, bim):
    dt = jnp.exp(ldt)
    mag = jnp.exp(lr * dt)
    lbr = mag * jnp.cos(li * dt)
    lbi = mag * jnp.sin(li * dt)
    den = lr * lr + li * li
    nr = lbr - 1.0
    fr = (nr * lr + lbi * li) / den
    fi = (lbi * lr - nr * li) / den
    bbr = fr[:, None, :] * bre - fi[:, None, :] * bim
    bbi = fr[:, None, :] * bim + fi[:, None, :] * bre
    return lbr, lbi, bbr, bbi


def _s5_prep(lr, li, ldt, bre, bim):
    def body(lr_ref, li_ref, ldt_ref, bre_ref, bim_ref, o1, o2, o3, o4):
        o1[...], o2[...], o3[...], o4[...] = _s5_disc(lr_ref[...], li_ref[...], ldt_ref[...], bre_ref[...], bim_ref[...])

    return pl.pallas_call(
        body, name="s5_prep", in_specs=[VMEM_SPEC] * 5, out_specs=[VMEM_SPEC] * 4,
        out_shape=[jax.ShapeDtypeStruct(lr.shape, f32), jax.ShapeDtypeStruct(lr.shape, f32),
                   jax.ShapeDtypeStruct(bre.shape, f32), jax.ShapeDtypeStruct(bre.shape, f32)],
    )(lr, li, ldt, bre, bim)


def _s5_prep_bwd(lr, li, ldt, bre, bim, cts):
    def body(lr_ref, li_ref, ldt_ref, bre_ref, bim_ref, g1, g2, g3, g4, o1, o2, o3, o4, o5):
        _, vjp = jax.vjp(_s5_disc, lr_ref[...], li_ref[...], ldt_ref[...], bre_ref[...], bim_ref[...])
        o1[...], o2[...], o3[...], o4[...], o5[...] = vjp((g1[...], g2[...], g3[...], g4[...]))

    return pl.pallas_call(
        body, name="s5_prep_bwd", in_specs=[VMEM_SPEC] * 9, out_specs=[VMEM_SPEC] * 5,
        out_shape=[jax.ShapeDtypeStruct(v.shape, f32) for v in (lr, li, ldt, bre, bim)],
    )(lr, li, ldt, bre, bim, *cts)


def _cmul(ar, ai, br, bi):
    return ar * br - ai * bi, ar * bi + ai * br


def _s5_pow_table(lr, li, n, row, up):
    ar = jnp.broadcast_to(lr, (n, lr.shape[1]))
    ai = jnp.broadcast_to(li, (n, li.shape[1]))
    shift = _shift_up if up else _shift_dn
    d = 1
    while d < n:
        ar, ai = _cmul(ar, ai, shift(ar, d, row, 1.0), shift(ai, d, row, 0.0))
        d *= 2
    return ar, ai


def _s5_step_factors(lr, li, row, up):
    sub = row & (SUBLANES - 1)
    out, pr, pi, d = [], lr, li, 1
    while d < SUBLANES:
        keep = (sub < SUBLANES - d) if up else (sub >= d)
        out.append((jnp.where(keep, pr, 0.0), jnp.where(keep, pi, 0.0)))
        pr, pi = _cmul(pr, pi, pr, pi)
        d *= 2
    return out


def _s5_scan(br, bi, steps, tab_r, tab_i, cr, ci, up):
    groups = list(range(br.shape[0] // SUBLANES))
    out_r, out_i = [None] * len(groups), [None] * len(groups)
    edge = slice(0, 1) if up else slice(SUBLANES - 1, SUBLANES)
    for g in (reversed(groups) if up else groups):
        rows = slice(g * SUBLANES, (g + 1) * SUBLANES)
        xr, xi = br[rows], bi[rows]
        for k, (mr, mi) in enumerate(steps):
            shift = SUBLANES - (1 << k) if up else 1 << k
            tr, ti = _cmul(mr, mi, pltpu.roll(xr, shift, 0), pltpu.roll(xi, shift, 0))
            xr, xi = xr + tr, xi + ti
        tr, ti = _cmul(tab_r, tab_i, cr, ci)
        hr, hi = xr + tr, xi + ti
        out_r[g], out_i[g] = hr, hi
        cr, ci = hr[edge], hi[edge]
    return jnp.concatenate(out_r, axis=0), jnp.concatenate(out_i, axis=0)


def _s5_specs(t, nb, nc):
    seq = pl.BlockSpec((t, 128), lambda k: (0, k))
    lvec = pl.BlockSpec((1, S5_BLOCK_STATES), lambda k: (0, k))
    dvec = pl.BlockSpec((1, 128), lambda k: (0, k))
    wmat = pl.BlockSpec((1, 128, S5_BLOCK_STATES), lambda k: (k, 0, 0))
    h0 = pl.BlockSpec((nb, None, nc, 2, S5_BLOCK_STATES), lambda k: (0, k, 0, 0, 0))
    return seq, lvec, dvec, wmat, h0


def _s5_fwd(u, lbr, lbi, wbr, wbi, wcr, wci, dskip, nb, s):
    ln = min(S5_CHUNK, s)
    nc = s // ln

    def body(u_ref, lr_ref, li_ref, wbr_ref, wbi_ref, wcr_ref, wci_ref, d_ref, yg_ref, y_ref, h0_ref):
        row = lax.broadcasted_iota(jnp.int32, (ln, S5_BLOCK_STATES), 0)
        lr, li = lr_ref[...], li_ref[...]
        pr, pi = _s5_pow_table(lr, li, SUBLANES, row[:SUBLANES], False)
        steps = _s5_step_factors(lr, li, row[:SUBLANES], False)
        dv = d_ref[...]

        def chunk(b, n, h0r, h0i):
            st = pl.multiple_of(b * s + n * ln, ln)
            uc = u_ref[pl.ds(st, ln), :]
            ub = uc.astype(bf16)
            hr, hi = _s5_scan(_nn(ub, wbr_ref[0]), _nn(ub, wbi_ref[0]), steps, pr, pi, h0r, h0i, False)
            h0_ref[b, n, 0:1, :] = h0r
            h0_ref[b, n, 1:2, :] = h0i
            y = _nt(hr.astype(bf16), wcr_ref[0]) - _nt(hi.astype(bf16), wci_ref[0]) + dv * uc
            y_ref[pl.ds(st, ln), :] = y
            yg_ref[pl.ds(st, ln), :] = jax.nn.gelu(y).astype(bf16)
            return hr[ln - 1:ln, :], hi[ln - 1:ln, :]

        def step(n, carry):
            return tuple(chunk(b, n, *carry[b]) for b in range(nb))

        z = jnp.zeros((1, S5_BLOCK_STATES), f32)
        lax.fori_loop(0, nc, step, ((z, z),) * nb)

    t = nb * s
    seq, lvec, dvec, wmat, h0 = _s5_specs(t, nb, nc)
    return pl.pallas_call(
        body, name="s5_fwd", grid=(S5_BLOCKS,),
        in_specs=[seq, lvec, lvec, wmat, wmat, wmat, wmat, dvec],
        out_specs=[seq, seq, h0],
        out_shape=[jax.ShapeDtypeStruct((t, D_MODEL), bf16), jax.ShapeDtypeStruct((t, D_MODEL), f32),
                   jax.ShapeDtypeStruct((nb, S5_BLOCKS, nc, 2, S5_BLOCK_STATES), f32)],
        compiler_params=_cp("parallel"),
    )(u, lbr, lbi, wbr, wbi, wcr, wci, dskip)


def _s5_bwd(dyg, y, u, h0, lbr, lbi, wbr, wbi, wcr, wci, dskip, nb, s):
    ln = min(S5_CHUNK, s)
    nc = s // ln

    def body(dyg_ref, y_ref, u_ref, h0_ref, lr_ref, li_ref, wbr_ref, wbi_ref, wcr_ref, wci_ref, d_ref,
             du_ref, dlr_ref, dli_ref, dwbr_ref, dwbi_ref, dwcr_ref, dwci_ref, dd_ref):
        for r in (dlr_ref, dli_ref, dwbr_ref, dwbi_ref, dwcr_ref, dwci_ref, dd_ref):
            r[...] = jnp.zeros_like(r)
        row = lax.broadcasted_iota(jnp.int32, (ln, S5_BLOCK_STATES), 0)
        lr, li = lr_ref[...], li_ref[...]
        pr, pi = _s5_pow_table(lr, li, SUBLANES, row[:SUBLANES], False)
        qr, qi = _s5_pow_table(lr, -li, SUBLANES, row[:SUBLANES], True)
        row8 = row[:SUBLANES]
        steps_dn, steps_up = _s5_step_factors(lr, li, row8, False), _s5_step_factors(lr, -li, row8, True)
        dv = d_ref[...]
        rsum = lambda v: jnp.sum(v, axis=0, keepdims=True)

        def chunk(b, n, gnr, gni):
            st = pl.multiple_of(b * s + n * ln, ln)
            uc = u_ref[pl.ds(st, ln), :]
            ub = uc.astype(bf16)
            h0v = h0_ref[b, n]
            h0r, h0i = h0v[0:1], h0v[1:2]
            hr, hi = _s5_scan(_nn(ub, wbr_ref[0]), _nn(ub, wbi_ref[0]), steps_dn, pr, pi, h0r, h0i, False)
            dy = jax.vjp(jax.nn.gelu, y_ref[pl.ds(st, ln), :])[1](dyg_ref[pl.ds(st, ln), :])[0]
            dyb = dy.astype(bf16)
            dd_ref[...] += rsum(dy * uc)
            gr, gi = _s5_scan(_nn(dyb, wcr_ref[0]), -_nn(dyb, wci_ref[0]), steps_up, qr, qi, gnr, gni, True)
            hpr = jnp.where(row >= 1, pltpu.roll(hr, 1, 0), h0r)
            hpi = jnp.where(row >= 1, pltpu.roll(hi, 1, 0), h0i)
            dlr_ref[...] += rsum(gr * hpr + gi * hpi)
            dli_ref[...] += rsum(gi * hpr - gr * hpi)
            grb, gib = gr.astype(bf16), gi.astype(bf16)
            dwbr_ref[0] += _tn(ub, grb)
            dwbi_ref[0] += _tn(ub, gib)
            dwcr_ref[0] += _tn(dyb, hr.astype(bf16))
            dwci_ref[0] -= _tn(dyb, hi.astype(bf16))
            du_ref[pl.ds(st, ln), :] = _nt(grb, wbr_ref[0]) + _nt(gib, wbi_ref[0]) + dv * dy
            return gr[0:1, :], gi[0:1, :]

        def step(i, carry):
            return tuple(chunk(b, nc - 1 - i, *carry[b]) for b in range(nb))

        z = jnp.zeros((1, S5_BLOCK_STATES), f32)
        lax.fori_loop(0, nc, step, ((z, z),) * nb)

    t = nb * s
    seq, lvec, dvec, wmat, h0s = _s5_specs(t, nb, nc)
    lshape = jax.ShapeDtypeStruct((1, S5_BLOCKS * S5_BLOCK_STATES), f32)
    wshape = jax.ShapeDtypeStruct((S5_BLOCKS, 128, S5_BLOCK_STATES), f32)
    return pl.pallas_call(
        body, name="s5_bwd", grid=(S5_BLOCKS,),
        in_specs=[seq, seq, seq, h0s, lvec, lvec, wmat, wmat, wmat, wmat, dvec],
        out_specs=[seq, lvec, lvec, wmat, wmat, wmat, wmat, dvec],
        out_shape=[jax.ShapeDtypeStruct((t, D_MODEL), f32), lshape, lshape, wshape, wshape, wshape, wshape,
                   jax.ShapeDtypeStruct((1, D_MODEL), f32)],
        compiler_params=_cp("parallel"),
    )(dyg, y, u, h0, lbr, lbi, wbr, wbi, wcr, wci, dskip)


def _blockdiag(w):
    w4 = w.reshape(S5_BLOCKS, 8, S5_GROUP, S5_STATE)
    same_group = jnp.eye(8, dtype=bool)[None, :, None, :, None]
    return jnp.where(same_group, w4[:, :, :, None, :], 0.0).reshape(S5_BLOCKS, 128, S5_BLOCK_STATES)


def _blockdiag_t(dw):
    d5 = dw.reshape(S5_BLOCKS, 8, S5_GROUP, 8, S5_STATE)
    diag = jnp.diagonal(d5, axis1=1, axis2=3)
    return jnp.moveaxis(diag, 3, 1).reshape(S5_GROUPS, S5_GROUP, S5_STATE)


def _glu_fwd(ygb, wa, wb, x, tm=512, tn=1024):
    t, d = x.shape

    def body(y_ref, wa_ref, wb_ref, x_ref, o_ref, p_ref, q_ref):
        p = _nn(y_ref[...], wa_ref[...])
        q = _nn(y_ref[...], wb_ref[...])
        p_ref[...] = p
        q_ref[...] = q
        o_ref[...] = x_ref[...] + p * jax.nn.sigmoid(q)

    tile = pl.BlockSpec((tm, tn), lambda i, j: (i, j))
    wsp = pl.BlockSpec((d, tn), lambda i, j: (0, j))
    out = jax.ShapeDtypeStruct((t, d), f32)
    return pl.pallas_call(
        body, name="glu_fwd", grid=(t // tm, d // tn),
        in_specs=[pl.BlockSpec((tm, d), lambda i, j: (i, 0)), wsp, wsp, tile],
        out_specs=[tile, tile, tile], out_shape=[out, out, out],
        compiler_params=_cp("parallel", "parallel"),
    )(ygb, wa, wb, x)


def _place():
    x, y, c = lax.axis_index("x"), lax.axis_index("y"), lax.axis_index("c")
    return x, y, c, [(1 - x, y), (x, 1 - y), (1 - x, 1 - y)]


def _all_gather(name, arrays):
    n = len(arrays)

    def body(*refs):
        ins, outs = refs[:n], refs[n:2 * n]
        send_sems, recv_sems, local_sems = refs[2 * n:]
        x, y, c, chips = _place()
        me, sib = (x, y, c), (x, y, 1 - c)

        def copy(i, k, block, to, src=None):
            dst = outs[i].at[4 * block[0] + 2 * block[1] + block[2]]
            return pltpu.make_async_remote_copy(
                src_ref=dst if src is None else src, dst_ref=dst,
                send_sem=send_sems.at[i * 7 + k], recv_sem=recv_sems.at[i * 7 + k],
                device_id=to, device_id_type=MESH)

        mine = [pltpu.make_async_copy(ins[i], outs[i].at[4 * x + 2 * y + c], local_sems.at[i]) for i in range(n)]
        for m in mine:
            m.start()
        first = []
        for i in range(n):
            first.append(copy(i, 0, me, sib, src=ins[i]))
            first += [copy(i, 1 + j, me, (*chip, c), src=ins[i]) for j, chip in enumerate(chips)]
        for cp in first:
            cp.start()
        passed = []
        for j, chip in enumerate(chips):
            for i in range(n):
                copy(i, 1 + j, (*chip, c), me).wait_recv()
                fwd = copy(i, 4 + j, (*chip, c), sib)
                fwd.start()
                passed.append(fwd)
        for i in range(n):
            copy(i, 0, sib, me).wait_recv()
        for j, chip in enumerate(chips):
            for i in range(n):
                copy(i, 4 + j, (*chip, 1 - c), me).wait_recv()
        for cp in first + passed:
            cp.wait_send()
        for m in mine:
            m.wait()

    return pl.pallas_call(
        body, name=name,
        in_specs=[ANY_SPEC] * n, out_specs=[ANY_SPEC] * n,
        out_shape=[jax.ShapeDtypeStruct((N_DEV,) + a.shape, a.dtype) for a in arrays],
        scratch_shapes=[pltpu.SemaphoreType.DMA((7 * n,)), pltpu.SemaphoreType.DMA((7 * n,)),
                        pltpu.SemaphoreType.DMA((n,))],
    )(*arrays)


def _tie(name, x, deps):
    def body(*refs):
        pass

    return pl.pallas_call(
        body, name=name, in_specs=[ANY_SPEC] * (1 + len(deps)), out_specs=ANY_SPEC,
        out_shape=jax.ShapeDtypeStruct(x.shape, x.dtype), input_output_aliases={0: 0},
    )(x, *deps)


def _xchg_copies(kind, srcs, lands, suffixes, send_sems, recv_sems):
    x, y, c, _ = _place()
    copies = []
    for i, (src, land, sfx) in enumerate(zip(srcs, lands, suffixes)):
        for k in range(N_DEV - 1):
            r = k + 1
            peer = (1 - x if r & 4 else x, 1 - y if r & 2 else y, 1 - c if r & 1 else c)
            if kind == "gather":
                s_ref, d_ref = src, land.at[(4 * x + 2 * y + c,) + sfx]
            else:
                s_ref, d_ref = src.at[4 * peer[0] + 2 * peer[1] + peer[2]], land.at[(k,) + sfx]
            copies.append(pltpu.make_async_remote_copy(
                src_ref=s_ref, dst_ref=d_ref, send_sem=send_sems.at[i * 7 + k], recv_sem=recv_sems.at[i * 7 + k],
                device_id=peer, device_id_type=MESH))
    return copies


def _xchg_start(name, kind, srcs, lands, suffixes=None):
    n = len(srcs)
    suffixes = suffixes or [()] * n

    def body(*refs):
        src, land = refs[:n], refs[n:2 * n]
        send_sems, recv_sems, token = refs[2 * n], refs[2 * n + 1], refs[-1]
        for cp in _xchg_copies(kind, src, land, suffixes, send_sems, recv_sems):
            cp.start()
        token[...] = jnp.zeros_like(token)

    arrays = list(srcs) + list(lands)
    outs = pl.pallas_call(
        body, name=name,
        out_shape=(pltpu.SemaphoreType.DMA((7 * n,)), pltpu.SemaphoreType.DMA((7 * n,)),
                   *[pltpu.HBM(a.shape, a.dtype) for a in arrays], jax.ShapeDtypeStruct((8, 128), f32)),
        in_specs=[HBM_SPEC] * (2 * n),
        out_specs=(SEM_SPEC, SEM_SPEC, *[HBM_SPEC] * (2 * n), VMEM_SPEC),
        input_output_aliases={i: 2 + i for i in range(2 * n)},
        compiler_params=pltpu.CompilerParams(has_side_effects=SIDE_EFFECT),
    )(*[pltpu.with_memory_space_constraint(a, pltpu.HBM) for a in arrays])
    return dict(kind=kind, n=n, suffixes=suffixes, send=outs[0], recv=outs[1], srcs=list(outs[2:2 + n]),
                lands=list(outs[2 + n:2 + 2 * n]), token=outs[-1])


def _xchg_wait(name, h, after, lands=None):
    n = h["n"]
    lands = h["lands"] if lands is None else lands

    def body(*refs):
        src, land = refs[:n], refs[n:2 * n]
        for cp in _xchg_copies(h["kind"], src, land, h["suffixes"], refs[2 * n], refs[2 * n + 1]):
            cp.wait_send()
            cp.wait_recv()

    arrays = h["srcs"] + list(lands)
    outs = pl.pallas_call(
        body, name=name,
        out_shape=tuple(pltpu.HBM(a.shape, a.dtype) for a in arrays),
        in_specs=[HBM_SPEC] * (2 * n) + [SEM_SPEC, SEM_SPEC] + [ANY_SPEC] * len(after),
        out_specs=tuple([HBM_SPEC] * (2 * n)),
        input_output_aliases={i: i for i in range(2 * n)},
        compiler_params=pltpu.CompilerParams(has_side_effects=SIDE_EFFECT),
    )(*arrays, h["send"], h["recv"], *after)
    return list(outs[n:])


def _rows(a):
    return a.reshape(-1, a.shape[-1])


def _row_tile(r):
    for tm in (512, 256, 128, 64, 32, 16, 8):
        if r % tm == 0:
            return tm
    return r


def _sum8(name, gathered):
    _, r, n = gathered.shape
    tm = _row_tile(r)

    def body(g_ref, o_ref):
        acc = g_ref[0]
        for k in range(1, N_DEV):
            acc = acc + g_ref[k]
        o_ref[...] = acc

    return pl.pallas_call(
        body, name=name, grid=(r // tm,),
        in_specs=[pl.BlockSpec((N_DEV, tm, n), lambda i: (0, i, 0))],
        out_specs=pl.BlockSpec((tm, n), lambda i: (i, 0)),
        out_shape=jax.ShapeDtypeStruct((r, n), f32),
        compiler_params=_cp("parallel"),
    )(gathered)


def _adamw(name, w, m, v, own, landed=None, slot=None):
    shape = w.shape
    w2, m2, v2 = _rows(w), _rows(m), _rows(v)
    r, n = w2.shape
    tm = _row_tile(r)
    c1 = 1.0 - ADAM_B1 ** ADAM_STEP
    c2 = 1.0 - ADAM_B2 ** ADAM_STEP
    extra = [] if landed is None else [landed.reshape(landed.shape[0], r, n)]
    row = pl.BlockSpec((tm, n), lambda i, *_: (i, 0))
    if slot is None:
        o2, own_spec, scalars = _rows(own), row, []
    else:
        dev, kind = slot
        scalars = [dev.reshape(1).astype(jnp.int32)]
        if kind == "lead":
            o2, own_spec = own.reshape(N_DEV, r, n), pl.BlockSpec((None, tm, n), lambda i, d: (d[0], i, 0))
        elif kind == "rows":
            o2, own_spec = own, pl.BlockSpec((tm, n), lambda i, d: (d[0] * (r // tm) + i, 0))
        else:
            o2, own_spec = own, pl.BlockSpec((tm, n), lambda i, d: (i, d[0]))

    def body(*refs):
        w_ref, m_ref, v_ref, o_ref = refs[len(scalars):len(scalars) + 4]
        refs = refs[len(scalars) + 4:]
        g = o_ref[...]
        if extra:
            for k in range(extra[0].shape[0]):
                g = g + refs[0][k].astype(f32)
        g_ref, d_ref, mn_ref, vn_ref = refs[len(extra):]
        mn = ADAM_B1 * m_ref[...] + (1.0 - ADAM_B1) * g
        vn = ADAM_B2 * v_ref[...] + (1.0 - ADAM_B2) * (g * g)
        g_ref[...] = g
        d_ref[...] = -ADAM_LR * ((mn / c1) / (jnp.sqrt(vn / c2) + ADAM_EPS) + ADAM_WD * w_ref[...])
        mn_ref[...] = mn
        vn_ref[...] = vn

    outs = pl.pallas_call(
        body, name=name,
        grid_spec=pltpu.PrefetchScalarGridSpec(
            num_scalar_prefetch=len(scalars), grid=(r // tm,),
            in_specs=[row] * 3 + [own_spec] + [pl.BlockSpec((e.shape[0], tm, n), lambda i, *_: (0, i, 0)) for e in extra],
            out_specs=[row] * 4),
        out_shape=[jax.ShapeDtypeStruct((r, n), f32)] * 4,
        compiler_params=_cp("parallel"),
    )(*scalars, w2, m2, v2, o2, *extra)
    return [o.reshape(shape) for o in outs]


def _pack(arrays):
    flat = jnp.concatenate([a.reshape(-1).astype(f32) for a in arrays])
    pad = (-flat.shape[0]) % (128 * (512 if flat.shape[0] > 128 * 512 else 8))
    return jnp.pad(flat, (0, pad)).reshape(-1, 128)


def _unpack(packed, shapes):
    flat = packed.reshape(-1)
    out, off = [], 0
    for s in shapes:
        n = math.prod(s)
        out.append(flat[off:off + n].reshape(s))
        off += n
    return out


def _local_step(x, target, w, weights_of, send, last_small, on_loss, nb, s):
    cos, sin = _rope_tables(s)
    g = {}
    ffn_saved = {}
    ffn_bufs = [lax.empty((N_DEV, 2, 2) + shp, f32)
                for shp in ((D_MODEL, FF_SHARD), (D_MODEL, FF_SHARD), (FF_SHARD, D_MODEL))]

    def ffn(xin, l, h, wts):
        y, a, b = _ffn_fwd(f"ffn_fwd_{l}{h}", xin, w["ffn_g"][l][h], *wts)
        ffn_saved[(l, h)] = (xin, a, b, wts)
        return y

    def ffn_back(dy, l, h):
        xin, a, b, wts = ffn_saved[(l, h)]
        dx, dg, hb, dyh, u, da, db = _ffn_dx(f"ffn_dx_{l}{h}", dy, xin, w["ffn_g"][l][h], *wts, a, b)
        g[f"ffn_g_{l}{h}"] = dg
        if (l, h) != (0, 0):
            halves = {}
            for k, (n, xt, y) in enumerate((("ffn_w1", hb, da), ("ffn_w3", hb, db), ("ffn_w2", u, dyh))):
                ffn_bufs[k], halves[n] = _ffn_dw_one(f"ffn_dw_{l}{h}_{n[4:]}", xt, y, ffn_bufs[k], l, h)
            return send(f"ffn_{l}{h}", halves, dx)
        hb = last_small(g, hb)
        ffn_bufs[0], half = _ffn_dw_one("ffn_dw_00_w1", hb, da, ffn_bufs[0], l, h)
        hb = send("ffn_00_w1", {"ffn_w1": half}, hb)
        ffn_bufs[1], half = _ffn_dw_one("ffn_dw_00_w3", hb, db, ffn_bufs[1], l, h)
        u = send("ffn_00_w3", {"ffn_w3": half}, u)
        ffn_bufs[2], half = _ffn_dw_one("ffn_dw_00_w2", u, dyh, ffn_bufs[2], l, h)
        return send("ffn_00_w2", {"ffn_w2": half}, dx)

    def slots(t):
        return t.reshape(N_DEV, D_MODEL // N_DEV, D_MODEL)

    x1 = ffn(x, 0, 0, weights_of(0, [])["ffn"])
    wg = weights_of(1, [x1])
    w_in, w_out = wg["w_in"], wg["w_out"]
    _, h0b = _norm_fwd("mix_norm_0", x1, w["mix_g"][0])
    proj = _mm("in_proj", h0b, w_in, "nn", tn=1536)[0]
    o_raw, rprev, mret = _ret_fwd(proj, cos, sin, w["ret_g"], nb, s)
    lru = _lru_fwd(proj, w["conv_w"], w["conv_b"], w["lru_w_a"], w["lru_b_a"], w["lru_w_i"], w["lru_b_i"], w["lru_lam"], nb, s)
    merged = _ew("merge", lambda a, b: (jnp.concatenate([a, b], axis=1),), [mret, lru], [(D_MODEL, bf16)])[0]
    x2 = _mm("out_proj", merged, w_out, "nn", extras=[x1], epilogue=lambda acc, r: (acc + r,))[0]
    x3 = ffn(x2, 0, 1, weights_of(2, [x2])["ffn"])
    x4 = ffn(x3, 1, 0, weights_of(3, [x3])["ffn"])
    u, _ = _norm_fwd("mix_norm_1", x4, w["mix_g"][1])
    lbr, lbi, bbr, bbi = _s5_prep(w["s5_lr"], w["s5_li"], w["s5_ldt"], w["s5_bre"], w["s5_bim"])
    lbr_f, lbi_f = lbr.reshape(1, -1), lbi.reshape(1, -1)
    wbr, wbi = _blockdiag(bbr).astype(bf16), _blockdiag(bbi).astype(bf16)
    wcr, wci = _blockdiag(w["s5_cre"]).astype(bf16), _blockdiag(w["s5_cim"]).astype(bf16)
    ygb, ypre, h0s = _s5_fwd(u, lbr_f, lbi_f, wbr, wbi, wcr, wci, w["s5_d"], nb, s)
    wg = weights_of(4, [ygb])
    glu_a, glu_b = wg["glu_a"], wg["glu_b"]
    x5, gp, gq = _glu_fwd(ygb, glu_a, glu_b, x4)
    x6 = ffn(x5, 1, 1, weights_of(5, [x5])["ffn"])
    loss, dx6, g["final_g"] = _final_loss(x6, w["final_g"], target)
    dx6 = on_loss(loss, dx6)

    dx5 = ffn_back(dx6, 1, 1)

    def glu_bwd(d, p, q):
        sg = jax.nn.sigmoid(q)
        return d * sg, d * p * sg * (1.0 - sg)

    dp, dq = _ew("glu_bwd", glu_bwd, [dx5, gp, gq], [(D_MODEL, bf16), (D_MODEL, bf16)])
    dyg = _mm("glu_dy_a", dp, glu_a, "nt")[0]
    dyg = _mm("glu_dy_b", dq, glu_b, "nt", extras=[dyg], epilogue=lambda acc, r: (acc + r,))[0]
    g["glu_a"], ga_half = _mm_tn("glu_dw_a", ygb, dp)
    g["glu_b"], gb_half = _mm_tn("glu_dw_b", ygb, dq)
    dyg = send("glu", {"glu_a": slots(ga_half), "glu_b": slots(gb_half)}, dyg)
    du, dlr, dli, dwbr, dwbi, dwcr, dwci, g["s5_d"] = _s5_bwd(dyg, ypre, u, h0s, lbr_f, lbi_f, wbr, wbi, wcr, wci, w["s5_d"], nb, s)
    g["s5_cre"], g["s5_cim"] = _blockdiag_t(dwcr), _blockdiag_t(dwci)
    g["s5_lr"], g["s5_li"], g["s5_ldt"], g["s5_bre"], g["s5_bim"] = _s5_prep_bwd(
        w["s5_lr"], w["s5_li"], w["s5_ldt"], w["s5_bre"], w["s5_bim"],
        (dlr.reshape(S5_GROUPS, S5_STATE), dli.reshape(S5_GROUPS, S5_STATE), _blockdiag_t(dwbr), _blockdiag_t(dwbi)))
    dx4, g["mix_g_1"] = _norm_bwd("mix_norm_1_bwd", du, x4, w["mix_g"][1], dx5)
    dx3 = ffn_back(dx4, 1, 0)
    dx2 = ffn_back(dx3, 0, 1)
    dmerged = _mm("out_proj_dx", dx2, w_out, "nt")[0]
    g["w_out"], wo_half = _mm_tn("out_proj_dw", merged, dx2)
    dmerged = send("w_out", {"w_out": slots(wo_half)}, dmerged)
    dq_, dk_, dv_, dgate, g["ret_g"] = _ret_bwd(dmerged, o_raw, rprev, proj, cos, sin, w["ret_g"], nb, s)
    (dxl, dgl, g["conv_w"], g["conv_b"], g["lru_w_a"], g["lru_b_a"], g["lru_w_i"], g["lru_b_i"], g["lru_lam"]) = _lru_bwd(
        dmerged, proj, w["conv_w"], w["conv_b"], w["lru_w_a"], w["lru_b_a"], w["lru_w_i"], w["lru_b_i"], w["lru_lam"], nb, s)
    dproj = _ew("dproj", lambda *p: (jnp.concatenate(p, axis=1),), [dq_, dk_, dv_, dgate, dxl, dgl], [(3072, bf16)])[0]
    dh0 = _mm("in_proj_dx", dproj, w_in, "nt")[0]
    g["w_in"], wi_half = _mm_tn("in_proj_dw", h0b, dproj)
    dh0 = send("w_in", {"w_in": jnp.transpose(wi_half.reshape(D_MODEL, N_DEV, IN_SHARD), (1, 0, 2))}, dh0)
    dx1, g["mix_g_0"] = _norm_bwd("mix_norm_0_bwd", dh0, x1, w["mix_g"][0], dx2)
    dx0 = ffn_back(dx1, 0, 0)
    g["ffn_w1"], g["ffn_w3"], g["ffn_w2"] = ffn_bufs
    return loss, dx0, g


_WEIGHTS = ["ffn_norm_g", "ffn_w1", "ffn_w3", "ffn_w2", "mix_norm_g", "w_in_even", "w_out_even", "ret_norm_g", "conv_w",
            "conv_b", "lru_w_a", "lru_b_a", "lru_w_i", "lru_b_i", "lru_lambda", "s5_lambda_re", "s5_lambda_im", "s5_log_dt",
            "s5_b_re", "s5_b_im", "s5_c_re", "s5_c_im", "s5_d", "glu_w_a", "glu_w_b", "final_norm_g"]
_BIG = ["ffn_w1", "ffn_w3", "ffn_w2", "w_in_even", "w_out_even", "glu_w_a", "glu_w_b"]
_SMALL_SHARDED = ["ffn_norm_g", "conv_w", "s5_d"]
_SMALL = [n for n in _WEIGHTS if n not in _BIG]
_MIDSIZE = ["lru_w_a", "lru_w_i", "s5_b_re", "s5_b_im", "s5_c_re", "s5_c_im"]


def kernel(x, ffn_norm_g, ffn_w1, ffn_w3, ffn_w2, mix_norm_g, w_in_even, w_out_even, ret_norm_g, conv_w, conv_b, lru_w_a, lru_b_a, lru_w_i, lru_b_i, lru_lambda, s5_lambda_re, s5_lambda_im, s5_log_dt, s5_b_re, s5_b_im, s5_c_re, s5_c_im, s5_d, glu_w_a, glu_w_b, final_norm_g, loss_target, m_ffn_norm_g, m_ffn_w1, m_ffn_w3, m_ffn_w2, m_mix_norm_g, m_w_in_even, m_w_out_even, m_ret_norm_g, m_conv_w, m_conv_b, m_lru_w_a, m_lru_b_a, m_lru_w_i, m_lru_b_i, m_lru_lambda, m_s5_lambda_re, m_s5_lambda_im, m_s5_log_dt, m_s5_b_re, m_s5_b_im, m_s5_c_re, m_s5_c_im, m_s5_d, m_glu_w_a, m_glu_w_b, m_final_norm_g, v_ffn_norm_g, v_ffn_w1, v_ffn_w3, v_ffn_w2, v_mix_norm_g, v_w_in_even, v_w_out_even, v_ret_norm_g, v_conv_w, v_conv_b, v_lru_w_a, v_lru_b_a, v_lru_w_i, v_lru_b_i, v_lru_lambda, v_s5_lambda_re, v_s5_lambda_im, v_s5_log_dt, v_s5_b_re, v_s5_b_im, v_s5_c_re, v_s5_c_im, v_s5_d, v_glu_w_a, v_glu_w_b, v_final_norm_g):
    a = dict(locals())
    nb, s, d = x.shape
    ax, ay, ac = lax.axis_index("x"), lax.axis_index("y"), lax.axis_index("c")
    dev = 4 * ax + 2 * ay + ac
    chip = 2 * ax + ay

    def ffn_shards(l, h):
        extra = FF_PAD - FF_SHARD
        return [jnp.pad(ffn_w1[l, h].astype(bf16), ((0, 0), (0, extra))), jnp.pad(ffn_w3[l, h].astype(bf16), ((0, 0), (0, extra))),
                jnp.pad(ffn_w2[l, h].astype(bf16), ((0, extra), (0, 0)))]

    first = _all_gather("ag_first", ffn_shards(0, 0) + [_pack([ffn_norm_g, conv_w, s5_d])])
    sm = first[3].reshape(N_DEV, -1)
    ffn_g_full = jnp.transpose(sm[:, :512].reshape(N_DEV, 2, 2, 128), (1, 2, 0, 3)).reshape(2, 2, D_MODEL)
    conv_w_full = jnp.transpose(sm[:, 512:768].reshape(N_DEV, 4, 64), (1, 0, 2)).reshape(4, LRU_WIDTH)
    s5_d_full = sm[:, 768:896].reshape(1, D_MODEL)

    ag_src = [None, [w_in_even[0].astype(bf16), w_out_even[0].astype(bf16)], ffn_shards(0, 1), ffn_shards(1, 0),
              [glu_w_a[0].astype(bf16), glu_w_b[0].astype(bf16)], ffn_shards(1, 1)]
    ag, token = [None], first[0]
    for k, grp in enumerate(ag_src):
        if grp is None:
            continue
        grp[0] = _tie(f"tie_ag_{k}", grp[0], [token])
        lands = [lax.dynamic_update_index_in_dim(lax.empty((N_DEV,) + t.shape, bf16), t, dev, 0) for t in grp]
        ag.append(_xchg_start(f"ag_start_{k}", "gather", grp, lands))
        token = ag[-1]["token"]

    def weights_of(k, after):
        if k == 0:
            return {"ffn": [first[0], _tie("tie_ag_started", first[1], [h["token"] for h in ag[1:]]), first[2]]}
        got = _xchg_wait(f"ag_wait_{k}", ag[k], after)
        if k == 1:
            return {"w_in": jnp.transpose(got[0], (1, 0, 2)).reshape(D_MODEL, N_DEV * IN_SHARD),
                    "w_out": got[1].reshape(D_MODEL, D_MODEL)}
        if k == 4:
            return {"glu_a": got[0].reshape(D_MODEL, D_MODEL), "glu_b": got[1].reshape(D_MODEL, D_MODEL)}
        return {"ffn": got}

    ffn_lands = [lax.empty((N_DEV - 1, 2, 2) + shp, bf16)
                 for shp in ((D_MODEL, FF_SHARD), (D_MODEL, FF_SHARD), (FF_SHARD, D_MODEL))]
    rs = []

    ffn_names = ("ffn_w1", "ffn_w3", "ffn_w2")

    def send(group, arrays, carry):
        srcs = list(arrays.values())
        if group.startswith("ffn_"):
            which = [ffn_names.index(n) for n in arrays]
            sfx = [(int(group[4]), int(group[5]))] * len(which)
            h = _xchg_start("rs_start_" + group, "scatter", srcs, [ffn_lands[k] for k in which], sfx)
            for k, land in zip(which, h["lands"]):
                ffn_lands[k] = land
        else:
            h = _xchg_start("rs_start_" + group, "scatter", srcs,
                            [lax.empty((N_DEV - 1,) + t.shape[1:], bf16) for t in srcs])
        rs.append((group, list(arrays), h))
        return _tie("tie_" + group, carry, [h["token"]])

    w = {
        "ffn_g": [[ffn_g_full[l, h].reshape(1, D_MODEL) for h in range(2)] for l in range(2)],
        "mix_g": [mix_norm_g[0:1], mix_norm_g[1:2]],
        "ret_g": ret_norm_g, "conv_w": conv_w_full, "conv_b": conv_b,
        "lru_w_a": lru_w_a[0], "lru_b_a": lru_b_a, "lru_w_i": lru_w_i[0], "lru_b_i": lru_b_i, "lru_lam": lru_lambda,
        "s5_lr": s5_lambda_re[0], "s5_li": s5_lambda_im[0], "s5_ldt": s5_log_dt.reshape(S5_GROUPS, 1),
        "s5_bre": jnp.swapaxes(s5_b_re[0], 1, 2), "s5_bim": jnp.swapaxes(s5_b_im[0], 1, 2),
        "s5_cre": s5_c_re[0], "s5_cim": s5_c_im[0], "s5_d": s5_d_full,
        "final_g": final_norm_g.reshape(1, D_MODEL),
    }

    small_grads = {}

    def last_small(g, carry):
        part = _small_partials(g)
        mine = _pack([part[n] for n in _SMALL])
        land = lax.dynamic_update_index_in_dim(lax.empty((N_DEV,) + mine.shape, f32), mine, dev, 0)
        h = _xchg_start("ag_start_small_grads", "gather", [mine], [land])
        small_grads.update(h=h, shapes=[part[n].shape for n in _SMALL])
        return _tie("tie_small_grads", carry, [h["token"]])

    total_loss = []

    def on_loss(part, carry):
        total_loss.append(lax.psum(part[0, 0], ("x", "y", "c")))
        return _tie("tie_loss", carry, [jnp.broadcast_to(total_loss[0], (8, 128))])

    _, dx, g = _local_step(x.reshape(nb * s, d), loss_target.reshape(nb * s, d), w, weights_of, send, last_small,
                           on_loss, nb, s)
    loss = total_loss[0]
    (gath,) = _xchg_wait("ag_wait_small_grads", small_grads["h"], [dx])
    full = dict(zip(_SMALL, _unpack(_sum8("sum_small_grads", gath), small_grads["shapes"])))
    for n in _SMALL_SHARDED:
        width = a[n].shape[-1]
        full[n] = lax.dynamic_slice_in_dim(full[n], dev * width, width, axis=full[n].ndim - 1)
    res = {n: _adamw("adamw_" + n, a[n], a["m_" + n], a["v_" + n], full[n]) for n in _MIDSIZE}
    tiny = [n for n in _SMALL if n not in _MIDSIZE]
    shapes = [a[n].shape for n in tiny]
    packed = _adamw("adamw_small", _pack([a[n] for n in tiny]), _pack([a["m_" + n] for n in tiny]),
                    _pack([a["v_" + n] for n in tiny]), _pack([full[n] for n in tiny]))
    res.update({n: vals for n, vals in zip(tiny, zip(*[_unpack(p, shapes) for p in packed]))})
    return _finish(a, g, dx, loss, res, packed, rs, ffn_lands, dev, nb, s, d)


def _small_partials(g):
    return {
        "ffn_norm_g": jnp.stack([jnp.stack([g[f"ffn_g_{l}{h}"][0] for h in range(2)]) for l in range(2)]),
        "mix_norm_g": jnp.concatenate([g["mix_g_0"], g["mix_g_1"]], axis=0),
        "ret_norm_g": g["ret_g"], "conv_w": g["conv_w"][None], "conv_b": g["conv_b"],
        "lru_w_a": g["lru_w_a"][None], "lru_b_a": g["lru_b_a"], "lru_w_i": g["lru_w_i"][None], "lru_b_i": g["lru_b_i"],
        "lru_lambda": g["lru_lam"], "s5_lambda_re": g["s5_lr"][None], "s5_lambda_im": g["s5_li"][None],
        "s5_log_dt": g["s5_ldt"].reshape(1, S5_GROUPS),
        "s5_b_re": jnp.swapaxes(g["s5_bre"], 1, 2)[None], "s5_b_im": jnp.swapaxes(g["s5_bim"], 1, 2)[None],
        "s5_c_re": g["s5_cre"][None], "s5_c_im": g["s5_cim"][None], "s5_d": g["s5_d"], "final_norm_g": g["final_g"][0],
    }


def _finish(a, g, dx, loss, res, packed, rs, ffn_lands, dev, nb, s, d):
    landed = {}
    for group, names, h in rs:
        if not group.startswith("ffn_"):
            landed.update(zip(names, _xchg_wait("rs_wait_" + group, h, [dx])))
    kinds = {"ffn_w1": "lead", "ffn_w3": "lead", "ffn_w2": "lead", "w_in": "cols", "w_out": "rows", "glu_a": "rows",
             "glu_b": "rows"}

    def update(n, short):
        res[n] = _adamw("adamw_" + n, a[n], a["m_" + n], a["v_" + n], g[short],
                        landed[short].reshape((N_DEV - 1,) + a[n].shape), slot=(dev, kinds[short]))

    for n, short in zip(_BIG[3:], ("w_in", "w_out", "glu_a", "glu_b")):
        update(n, short)
    after = [dx, packed[0]] + [res[n][0] for n in _BIG[3:] + _MIDSIZE]
    ffn_names = ("ffn_w1", "ffn_w3", "ffn_w2")
    for group, names, h in rs:
        if group.startswith("ffn_") and len(names) == 3:
            ffn_lands[:] = _xchg_wait("rs_wait_" + group, h, after, ffn_lands)
    for k, n in enumerate(ffn_names):
        for group, names, h in rs:
            if group.startswith("ffn_") and names == [n]:
                (ffn_lands[k],) = _xchg_wait("rs_wait_" + group, h, after, [ffn_lands[k]])
        landed[n] = ffn_lands[k]
        update(n, n)
        after = after + [res[n][0]]

    out = [loss, dx.reshape(nb, s, d)]
    for k in range(4):
        out += [res[n][k] for n in _WEIGHTS]
    return tuple(out)
```

```python
import functools
import math

import numpy as np
import jax
import jax.numpy as jnp
from jax import lax
from jax.experimental import pallas as pl
from jax.experimental.pallas import tpu as pltpu

f32 = jnp.float32
bf16 = jnp.bfloat16

D_MODEL = 1024
N_DEV = 8
EPS = 1e-6
RET_HEADS = 4
HEAD_DIM = 128
RET_WIDTH = 512
RET_CHUNK = 128
ROPE_BASE = 10000.0
LRU_WIDTH = 512
LRU_BLOCKS = 4
LRU_C = 8.0
S5_GROUP = 16
S5_GROUPS = 64
S5_STATE = 64
S5_CHUNK = 1024
S5_BLOCKS = 8
S5_BLOCK_STATES = 512
SUBLANES = 8
D_FF = 2816
FF_SHARD = D_FF // N_DEV
FF_PAD = 384
IN_SHARD = 3072 // N_DEV
ADAM_LR = 0.001
ADAM_B1 = 0.9
ADAM_B2 = 0.999
ADAM_EPS = 1e-08
ADAM_WD = 0.01
ADAM_STEP = 10

VMEM_LIMIT = 56 * 1024 * 1024
VMEM_SPEC = pl.BlockSpec(memory_space=pltpu.VMEM)
ANY_SPEC = pl.BlockSpec(memory_space=pl.ANY)
HBM_SPEC = pl.BlockSpec(memory_space=pltpu.HBM)
SEM_SPEC = pl.BlockSpec(memory_space=pltpu.SEMAPHORE)
SIDE_EFFECT = pltpu.SideEffectType.DATAFLOW_SIDE_EFFECTING
MESH = pl.DeviceIdType.MESH


def _cp(*sem):
    return pltpu.CompilerParams(dimension_semantics=sem, vmem_limit_bytes=VMEM_LIMIT)


def _nn(a, b):
    return jnp.dot(a, b, preferred_element_type=f32)


def _nt(a, b):
    return lax.dot_general(a, b, (((1,), (1,)), ((), ())), preferred_element_type=f32)


def _tn(a, b):
    return lax.dot_general(a, b, (((0,), (0,)), ((), ())), preferred_element_type=f32)


def _rms_fwd(x, g):
    r = lax.rsqrt(jnp.mean(x * x, axis=-1, keepdims=True) + EPS)
    xn = x * r
    return xn * g, xn, r


def _rms_bwd(dh, xn, r, g):
    dxn = dh * g
    dx = r * (dxn - xn * jnp.mean(dxn * xn, axis=-1, keepdims=True))
    dg = jnp.sum(dh * xn, axis=0, keepdims=True)
    return dx, dg


def _shift_dn(v, d, row, fill=0.0):
    return jnp.where(row >= d, pltpu.roll(v, d, 0), fill)


def _shift_up(v, d, row, fill=0.0):
    n = v.shape[0]
    return jnp.where(row < n - d, pltpu.roll(v, n - d, 0), fill)


def _ew(name, fn, ins, outs, tm=512):
    t = ins[0].shape[0]
    n_in = len(ins)

    def body(*refs):
        res = fn(*[r[...] for r in refs[:n_in]])
        for o, v in zip(refs[n_in:], res):
            o[...] = v.astype(o.dtype)

    return pl.pallas_call(
        body, name=name, grid=(t // tm,),
        in_specs=[pl.BlockSpec((tm, a.shape[1]), lambda i: (i, 0)) for a in ins],
        out_specs=[pl.BlockSpec((tm, n), lambda i: (i, 0)) for n, _ in outs],
        out_shape=[jax.ShapeDtypeStruct((t, n), dt) for n, dt in outs],
        compiler_params=_cp("parallel"),
    )(*ins)


def _mm(name, x, w, kind, extras=(), epilogue=None, outs=None, tm=512, tn=1024):
    t = x.shape[0]
    n = w.shape[1] if kind == "nn" else w.shape[0]
    tn = min(tn, n)
    outs = outs or [f32]
    n_ex = len(extras)

    def body(x_ref, w_ref, *refs):
        xb = x_ref[...].astype(bf16)
        acc = _nn(xb, w_ref[...]) if kind == "nn" else _nt(xb, w_ref[...])
        res = epilogue(acc, *[r[...] for r in refs[:n_ex]]) if epilogue else (acc,)
        for o, v in zip(refs[n_ex:], res):
            o[...] = v.astype(o.dtype)

    w_spec = (pl.BlockSpec((w.shape[0], tn), lambda i, j: (0, j)) if kind == "nn"
              else pl.BlockSpec((tn, w.shape[1]), lambda i, j: (j, 0)))
    tile = pl.BlockSpec((tm, tn), lambda i, j: (i, j))
    return pl.pallas_call(
        body, name=name, grid=(t // tm, n // tn),
        in_specs=[pl.BlockSpec((tm, x.shape[1]), lambda i, j: (i, 0)), w_spec] + [tile] * n_ex,
        out_specs=[tile] * len(outs),
        out_shape=[jax.ShapeDtypeStruct((t, n), dt) for dt in outs],
        compiler_params=_cp("parallel", "parallel"),
    )(x, w, *extras)


def _mm_tn(name, x, y, tk=1024, tn=1024, tt=1024):
    t, k = x.shape
    n = y.shape[1]
    tk, tn, tt = min(tk, k), min(tn, n), min(tt, t)

    def body(x_ref, y_ref, o_ref, ob_ref):
        @pl.when(pl.program_id(2) == 0)
        def _():
            o_ref[...] = jnp.zeros_like(o_ref)
        o_ref[...] += _tn(x_ref[...].astype(bf16), y_ref[...].astype(bf16))

        @pl.when(pl.program_id(2) == pl.num_programs(2) - 1)
        def _():
            ob_ref[...] = o_ref[...].astype(bf16)

    out = pl.BlockSpec((tk, tn), lambda i, j, s: (i, j))
    return pl.pallas_call(
        body, name=name, grid=(k // tk, n // tn, t // tt),
        in_specs=[pl.BlockSpec((tt, tk), lambda i, j, s: (s, i)), pl.BlockSpec((tt, tn), lambda i, j, s: (s, j))],
        out_specs=[out, out],
        out_shape=[jax.ShapeDtypeStruct((k, n), f32), jax.ShapeDtypeStruct((k, n), bf16)],
        compiler_params=_cp("parallel", "parallel", "arbitrary"),
    )(x, y)


def _norm_fwd(name, x, g, tm=512):
    t, d = x.shape

    def body(x_ref, g_ref, h_ref, hb_ref):
        h, _, _ = _rms_fwd(x_ref[...], g_ref[...])
        h_ref[...] = h
        hb_ref[...] = h.astype(bf16)

    row = pl.BlockSpec((tm, d), lambda i: (i, 0))
    return pl.pallas_call(
        body, name=name, grid=(t // tm,),
        in_specs=[row, pl.BlockSpec((1, d), lambda i: (0, 0))],
        out_specs=[row, row],
        out_shape=[jax.ShapeDtypeStruct((t, d), f32), jax.ShapeDtypeStruct((t, d), bf16)],
        compiler_params=_cp("parallel"),
    )(x, g)


def _norm_bwd(name, dh, x, g, dres, tm=512):
    t, d = x.shape

    def body(dh_ref, x_ref, g_ref, dres_ref, dx_ref, dg_ref):
        gv = g_ref[...]
        _, xn, r = _rms_fwd(x_ref[...], gv)
        dx, dg = _rms_bwd(dh_ref[...], xn, r, gv)
        dx_ref[...] = dres_ref[...] + dx

        @pl.when(pl.program_id(0) == 0)
        def _():
            dg_ref[...] = jnp.zeros_like(dg_ref)
        dg_ref[...] += dg

    row = pl.BlockSpec((tm, d), lambda i: (i, 0))
    vec = pl.BlockSpec((1, d), lambda i: (0, 0))
    return pl.pallas_call(
        body, name=name, grid=(t // tm,),
        in_specs=[row, row, vec, row],
        out_specs=[row, vec],
        out_shape=[jax.ShapeDtypeStruct((t, d), f32), jax.ShapeDtypeStruct((1, d), f32)],
        compiler_params=_cp("arbitrary"),
    )(dh, x, g, dres)


def _final_loss(x, g, target, tm=512):
    t, d = x.shape

    def body(x_ref, g_ref, t_ref, loss_ref, dx_ref, dg_ref):
        gv = g_ref[...]
        y, xn, r = _rms_fwd(x_ref[...], gv)
        err = y - t_ref[...]
        dy = err * (1.0 / d)
        dx, dg = _rms_bwd(dy, xn, r, gv)
        dx_ref[...] = dx

        @pl.when(pl.program_id(0) == 0)
        def _():
            dg_ref[...] = jnp.zeros_like(dg_ref)
            loss_ref[...] = jnp.zeros_like(loss_ref)
        dg_ref[...] += dg
        loss_ref[...] += jnp.full((1, 128), 0.5 / d, f32) * jnp.sum(err * err)

    row = pl.BlockSpec((tm, d), lambda i: (i, 0))
    vec = pl.BlockSpec((1, d), lambda i: (0, 0))
    return pl.pallas_call(
        body, name="final_loss", grid=(t // tm,),
        in_specs=[row, vec, row],
        out_specs=[pl.BlockSpec((1, 128), lambda i: (0, 0)), row, vec],
        out_shape=[jax.ShapeDtypeStruct((1, 128), f32), jax.ShapeDtypeStruct((t, d), f32),
                   jax.ShapeDtypeStruct((1, d), f32)],
        compiler_params=_cp("arbitrary"),
    )(x, g, target)


def _load_ffn_weights(hbm_refs, vmem_refs, sems):
    @pl.when(pl.program_id(0) == 0)
    def _():
        copies = []
        for k, (src, dst) in enumerate(zip(hbm_refs, vmem_refs)):
            for j in range(N_DEV):
                half = pl.ds((j % 2) * FF_PAD, FF_PAD)
                window = dst.at[j // 2, half, :] if k == 2 else dst.at[j // 2, :, half]
                copies.append(pltpu.make_async_copy(src.at[j], window, sems.at[k * N_DEV + j]))
        for cp in copies:
            cp.start()
        for cp in copies:
            cp.wait()


def _ffn_weight_scratch(nj, d, ff):
    return [pltpu.VMEM((nj, d, ff), bf16), pltpu.VMEM((nj, d, ff), bf16), pltpu.VMEM((nj, ff, d), bf16),
            pltpu.SemaphoreType.DMA((3 * N_DEV,))]


def _ffn_fwd(name, x, g, w1, w3, w2, tm=512):
    t, d = x.shape
    nj, ff = N_DEV // 2, 2 * FF_PAD

    def body(x_ref, g_ref, w1_hbm, w3_hbm, w2_hbm, y_ref, a_ref, b_ref, w1_ref, w3_ref, w2_ref, sems):
        _load_ffn_weights((w1_hbm, w3_hbm, w2_hbm), (w1_ref, w3_ref, w2_ref), sems)
        xv = x_ref[...]
        h, _, _ = _rms_fwd(xv, g_ref[...])
        hb = h.astype(bf16)
        acc = jnp.zeros((tm, d), f32)
        for j in range(nj):
            a = _nn(hb, w1_ref[j])
            b = _nn(hb, w3_ref[j])
            a_ref[j] = a.astype(bf16)
            b_ref[j] = b.astype(bf16)
            u = (a * jax.nn.sigmoid(a) * b).astype(bf16)
            acc = acc + _nn(u, w2_ref[j])
        y_ref[...] = xv + 0.5 * acc

    row = pl.BlockSpec((tm, d), lambda i: (i, 0))
    mid = pl.BlockSpec((nj, tm, ff), lambda i: (0, i, 0))
    return pl.pallas_call(
        body, name=name, grid=(t // tm,),
        in_specs=[row, pl.BlockSpec((1, d), lambda i: (0, 0)), ANY_SPEC, ANY_SPEC, ANY_SPEC],
        out_specs=[row, mid, mid],
        out_shape=[jax.ShapeDtypeStruct((t, d), f32), jax.ShapeDtypeStruct((nj, t, ff), bf16),
                   jax.ShapeDtypeStruct((nj, t, ff), bf16)],
        scratch_shapes=_ffn_weight_scratch(nj, d, ff),
        compiler_params=_cp("arbitrary"),
    )(x, g, w1, w3, w2)


def _ffn_dx(name, dy, x, g, w1, w3, w2, a, b, tm=256):
    t, d = x.shape
    nj, ff = N_DEV // 2, 2 * FF_PAD

    def body(dy_ref, x_ref, g_ref, w1_hbm, w3_hbm, w2_hbm, a_ref, b_ref,
             dx_ref, dg_ref, hbt_ref, dyh_ref, ut_ref, da_ref, db_ref, w1_ref, w3_ref, w2_ref, sems):
        _load_ffn_weights((w1_hbm, w3_hbm, w2_hbm), (w1_ref, w3_ref, w2_ref), sems)
        gv = g_ref[...]
        h, xn, r = _rms_fwd(x_ref[...], gv)
        hbt_ref[...] = h.astype(bf16).T
        dyv = dy_ref[...]
        dyh = (0.5 * dyv).astype(bf16)
        dyh_ref[...] = dyh
        dh = jnp.zeros((tm, d), f32)
        dus = [_nt(dyh, w2_ref[j]) for j in range(nj)]
        for j in range(nj):
            av = a_ref[j].astype(f32)
            bv = b_ref[j].astype(f32)
            s = jax.nn.sigmoid(av)
            silu = av * s
            ut_ref[j] = (silu * bv).astype(bf16).T
            du = dus[j]
            dab = (du * bv * (s * (1.0 + av * (1.0 - s)))).astype(bf16)
            dbb = (du * silu).astype(bf16)
            da_ref[j] = dab
            db_ref[j] = dbb
            dh = dh + _nt(dab, w1_ref[j]) + _nt(dbb, w3_ref[j])
        dx, dg = _rms_bwd(dh, xn, r, gv)
        dx_ref[...] = dyv + dx

        @pl.when(pl.program_id(0) == 0)
        def _():
            dg_ref[...] = jnp.zeros_like(dg_ref)
        dg_ref[...] += dg

    row = pl.BlockSpec((tm, d), lambda i: (i, 0))
    vec = pl.BlockSpec((1, d), lambda i: (0, 0))
    mid = pl.BlockSpec((nj, tm, ff), lambda i: (0, i, 0))
    mid_shape = jax.ShapeDtypeStruct((nj, t, ff), bf16)
    return pl.pallas_call(
        body, name=name, grid=(t // tm,),
        in_specs=[row, row, vec, ANY_SPEC, ANY_SPEC, ANY_SPEC, mid, mid],
        out_specs=[row, vec, pl.BlockSpec((d, tm), lambda i: (0, i)), row,
                   pl.BlockSpec((nj, ff, tm), lambda i: (0, 0, i)), mid, mid],
        out_shape=[jax.ShapeDtypeStruct((t, d), f32), jax.ShapeDtypeStruct((1, d), f32),
                   jax.ShapeDtypeStruct((d, t), bf16), jax.ShapeDtypeStruct((t, d), bf16),
                   jax.ShapeDtypeStruct((nj, ff, t), bf16), mid_shape, mid_shape],
        scratch_shapes=_ffn_weight_scratch(nj, d, ff),
        compiler_params=_cp("arbitrary"),
    )(dy, x, g, w1, w3, w2, a, b)


def _ffn_dw_one(name, xt, y, buf, l, h, tt=2048):
    t = y.shape[-2]
    tt = min(tt, t)
    cut_cols = xt.ndim == 2

    def body(x_ref, y_ref, buf_ref, o_ref, ob_ref, acc):
        s = pl.program_id(1)
        prod = _nn(x_ref[0] if xt.ndim == 3 else x_ref[...], y_ref[0] if y.ndim == 3 else y_ref[...])

        @pl.when(s == 0)
        def _():
            acc[...] = prod

        @pl.when(s > 0)
        def _():
            acc[...] += prod

        @pl.when(s == pl.num_programs(1) - 1)
        def _():
            total = acc[...]
            for e in range(2):
                lo = e * FF_PAD
                part = total[:, lo:lo + FF_SHARD] if cut_cols else total[lo:lo + FF_SHARD, :]
                o_ref[e] = part
                ob_ref[e] = part.astype(bf16)

    x_spec = (pl.BlockSpec((1, xt.shape[1], tt), lambda p, s: (p, 0, s)) if xt.ndim == 3
              else pl.BlockSpec((xt.shape[0], tt), lambda p, s: (0, s)))
    y_spec = (pl.BlockSpec((1, tt, y.shape[2]), lambda p, s: (p, s, 0)) if y.ndim == 3
              else pl.BlockSpec((tt, y.shape[1]), lambda p, s: (s, 0)))
    k_, n_ = buf.shape[-2:]
    return pl.pallas_call(
        body, name=name, grid=(N_DEV // 2, t // tt),
        in_specs=[x_spec, y_spec, ANY_SPEC],
        out_specs=[pl.BlockSpec((2, None, None, k_, n_), lambda p, s: (p, l, h, 0, 0)),
                   pl.BlockSpec((2, k_, n_), lambda p, s: (p, 0, 0))],
        out_shape=[jax.ShapeDtypeStruct(buf.shape, buf.dtype), jax.ShapeDtypeStruct((N_DEV, k_, n_), bf16)],
        input_output_aliases={2: 0},
        scratch_shapes=[pltpu.VMEM((xt.shape[-2], y.shape[-1]), f32)],
        compiler_params=_cp("parallel", "arbitrary"),
    )(xt, y, buf)


_LOG_GAMMA = [float(np.log1p(-np.float32(2.0) ** np.float32(-5.0 - h))) for h in range(RET_HEADS)]


def _ret_consts(h):
    lg = jnp.where(h == 0, _LOG_GAMMA[0], jnp.where(h == 1, _LOG_GAMMA[1],
                   jnp.where(h == 2, _LOG_GAMMA[2], _LOG_GAMMA[3]))).astype(f32)
    c = RET_CHUNK
    r = lax.broadcasted_iota(jnp.int32, (c, c), 0)
    cc = lax.broadcasted_iota(jnp.int32, (c, c), 1)
    decay = jnp.where(r >= cc, jnp.exp(lg * jnp.maximum((r - cc).astype(f32), 0.0)), 0.0)
    pos = lax.broadcasted_iota(jnp.int32, (c, 1), 0).astype(f32)
    kd = jnp.exp(lg * (c - 1.0 - pos))
    qd = jnp.exp(lg * (pos + 1.0))
    gc = jnp.exp(lg * c)
    return decay, kd, qd, gc


def _rope(x, cos, sin):
    return x * cos + pltpu.roll(x, HEAD_DIM // 2, 1) * sin


def _rope_t(g, cos, sin):
    return g * cos + pltpu.roll(g * sin, HEAD_DIM // 2, 1)


def _rope_tables(s):
    half = HEAD_DIM // 2
    inv = ROPE_BASE ** (-jnp.arange(half, dtype=f32) / half)
    ang = jnp.arange(s, dtype=f32)[:, None] * inv[None, :]
    cos, sin = jnp.cos(ang), jnp.sin(ang)
    return jnp.concatenate([cos, cos], axis=1), jnp.concatenate([-sin, sin], axis=1)


def _head_ln(o):
    mu = jnp.mean(o, axis=-1, keepdims=True)
    oc = o - mu
    rs = lax.rsqrt(jnp.mean(oc * oc, axis=-1, keepdims=True) + EPS)
    return oc * rs, rs


def _ret_fwd(proj, cos, sin, ret_g, nb, s):
    c = RET_CHUNK
    nc = s // c
    t = nb * s
    scale = HEAD_DIM ** -0.5

    def body(q_ref, k_ref, v_ref, gate_ref, cos_ref, sin_ref, g_ref, o_ref, rprev_ref, m_ref):
        decay, kd, qd, gc = _ret_consts(pl.program_id(0))
        gv = g_ref[...]

        def chunk(b, n, rv):
            rows = pl.ds(pl.multiple_of(b * s + n * c, c), c)
            pos = pl.ds(pl.multiple_of(n * c, c), c)
            cs, sn = cos_ref[pos, :], sin_ref[pos, :]
            q = _rope(q_ref[rows, :], cs, sn)
            k = _rope(k_ref[rows, :], cs, sn) * scale
            vb = v_ref[rows, :].astype(bf16)
            sc = _nt(q.astype(bf16), k.astype(bf16)) * decay
            rprev_ref[b, n] = rv
            o = _nn(sc.astype(bf16), vb) + _nn((q * qd).astype(bf16), rv.astype(bf16))
            o_ref[rows, :] = o
            y, _ = _head_ln(o)
            gate = gate_ref[rows, :]
            m_ref[rows, :] = y * gv * (gate * jax.nn.sigmoid(gate))
            return rv * gc + _tn((k * kd).astype(bf16), vb)

        def step(n, carry):
            return tuple(chunk(b, n, carry[b]) for b in range(nb))

        lax.fori_loop(0, nc, step, (jnp.zeros((HEAD_DIM, HEAD_DIM), f32),) * nb)

    def col(off):
        return pl.BlockSpec((t, HEAD_DIM), lambda h: (0, off + h))

    tab = pl.BlockSpec((s, HEAD_DIM), lambda h: (0, 0))
    return pl.pallas_call(
        body, name="ret_fwd", grid=(RET_HEADS,),
        in_specs=[col(0), col(4), col(8), col(12), tab, tab, pl.BlockSpec((1, HEAD_DIM), lambda h: (0, h))],
        out_specs=[col(0), pl.BlockSpec((nb, None, nc, HEAD_DIM, HEAD_DIM), lambda h: (0, h, 0, 0, 0)), col(0)],
        out_shape=[jax.ShapeDtypeStruct((t, RET_WIDTH), f32),
                   jax.ShapeDtypeStruct((nb, RET_HEADS, nc, HEAD_DIM, HEAD_DIM), f32),
                   jax.ShapeDtypeStruct((t, RET_WIDTH), f32)],
        compiler_params=_cp("parallel"),
    )(proj, proj, proj, proj, cos, sin, ret_g)


def _ret_bwd(dmerged, o_raw, rprev, proj, cos, sin, ret_g, nb, s):
    c = RET_CHUNK
    nc = s // c
    t = nb * s
    scale = HEAD_DIM ** -0.5

    def body(dm_ref, o_ref, rprev_ref, q_ref, k_ref, v_ref, gate_ref, cos_ref, sin_ref, g_ref,
             dq_ref, dk_ref, dv_ref, dgate_ref, dg_ref):
        decay, kd, qd, gc = _ret_consts(pl.program_id(0))
        gv = g_ref[...]

        def chunk(b, n, drn, dg):
            rows = pl.ds(pl.multiple_of(b * s + n * c, c), c)
            pos = pl.ds(pl.multiple_of(n * c, c), c)
            cs, sn = cos_ref[pos, :], sin_ref[pos, :]
            q = _rope(q_ref[rows, :], cs, sn)
            k = _rope(k_ref[rows, :], cs, sn) * scale
            qb, kb = q.astype(bf16), k.astype(bf16)
            vb = v_ref[rows, :].astype(bf16)
            sc = _nt(qb, kb) * decay
            y, rs = _head_ln(o_ref[rows, :])
            gate = gate_ref[rows, :]
            sg = jax.nn.sigmoid(gate)
            silu = gate * sg
            dm = dm_ref[rows, :]
            dgate_ref[rows, :] = dm * y * gv * (sg * (1.0 + gate * (1.0 - sg)))
            dyl = dm * gv * silu
            dg = dg + jnp.sum(dm * y * silu, axis=0, keepdims=True)
            do = rs * (dyl - jnp.mean(dyl, axis=-1, keepdims=True) - y * jnp.mean(dyl * y, axis=-1, keepdims=True))
            dob = do.astype(bf16)
            rv = rprev_ref[b, n]
            drb = drn.astype(bf16)
            ds = (_nt(dob, vb) * decay).astype(bf16)
            kdb = (k * kd).astype(bf16)
            qdb = (q * qd).astype(bf16)
            dq_r = _nn(ds, kb) + _nt(dob, rv.astype(bf16)) * qd
            dk_r = _tn(ds, qb) + _nt(vb, drb) * kd
            dv_ref[rows, :] = _tn(sc.astype(bf16), dob) + _nn(kdb, drb)
            dq_ref[rows, :] = _rope_t(dq_r, cs, sn)
            dk_ref[rows, :] = _rope_t(dk_r * scale, cs, sn)
            return drn * gc + _tn(qdb, dob), dg

        def step(i, carry):
            out = [chunk(b, nc - 1 - i, *carry[b]) for b in range(nb)]
            return tuple(out)

        zero = (jnp.zeros((HEAD_DIM, HEAD_DIM), f32), jnp.zeros((1, HEAD_DIM), f32))
        done = lax.fori_loop(0, nc, step, (zero,) * nb)
        dg_ref[...] = sum(dg for _, dg in done)

    def col(off):
        return pl.BlockSpec((t, HEAD_DIM), lambda h: (0, off + h))

    tab = pl.BlockSpec((s, HEAD_DIM), lambda h: (0, 0))
    gsp = pl.BlockSpec((1, HEAD_DIM), lambda h: (0, h))
    out_t = jax.ShapeDtypeStruct((t, RET_WIDTH), f32)
    return pl.pallas_call(
        body, name="ret_bwd", grid=(RET_HEADS,),
        in_specs=[col(0), col(0), pl.BlockSpec((nb, None, nc, HEAD_DIM, HEAD_DIM), lambda h: (0, h, 0, 0, 0)),
                  col(0), col(4), col(8), col(12), tab, tab, gsp],
        out_specs=[col(0), col(0), col(0), col(0), gsp],
        out_shape=[out_t, out_t, out_t, out_t, jax.ShapeDtypeStruct((1, RET_WIDTH), f32)],
        compiler_params=_cp("parallel"),
    )(dmerged, o_raw, rprev, proj, proj, proj, proj, cos, sin, ret_g)


def _neg_expm1(z):
    series = -(z * (1.0 + z * (0.5 + z * (1.0 / 6.0 + z * (1.0 / 24.0)))))
    return jnp.where(z > -0.01, series, 1.0 - jnp.exp(z))


def _lru_gates(xc, pa, pi, lam):
    r = jax.nn.sigmoid(pa)
    i = jax.nn.sigmoid(pi)
    log_a = -LRU_C * r * jax.nn.softplus(-lam)
    a = jnp.exp(log_a)
    bx = jnp.sqrt(_neg_expm1(2.0 * log_a)) * i * xc
    return a, bx


def _scan_rows(a, b, row, up):
    sub = row[:SUBLANES] & (SUBLANES - 1)
    groups = list(range(a.shape[0] // SUBLANES))
    out = [None] * len(groups)
    edge = slice(0, 1) if up else slice(SUBLANES - 1, SUBLANES)
    carry = jnp.zeros((1, a.shape[1]), f32)
    for g in (reversed(groups) if up else groups):
        rows = slice(g * SUBLANES, (g + 1) * SUBLANES)
        xa, xb = a[rows], b[rows]
        d = 1
        while d < SUBLANES:
            keep = (sub < SUBLANES - d) if up else (sub >= d)
            shift = SUBLANES - d if up else d
            xb = xa * jnp.where(keep, pltpu.roll(xb, shift, 0), 0.0) + xb
            xa = xa * jnp.where(keep, pltpu.roll(xa, shift, 0), 1.0)
            d *= 2
        out[g] = xb + xa * carry
        carry = out[g][edge]
    return jnp.concatenate(out, axis=0)


def _scan_fwd(a, b, row):
    return _scan_rows(a, b, row, False)


def _scan_bwd(c, b, row):
    return _scan_rows(c, b, row, True)


def _conv_fwd(x, cw, cb, row):
    return (cb + cw[3:4] * x + cw[2:3] * _shift_dn(x, 1, row) + cw[1:2] * _shift_dn(x, 2, row)
            + cw[0:1] * _shift_dn(x, 3, row))


def _lru_specs(s, order):
    def im(f):
        return (lambda b, g: f(b, g)) if order == "bg" else (lambda g, b: f(b, g))
    seq = lambda off: pl.BlockSpec((s, 128), im(lambda b, g: (b, off + g)))
    vec = pl.BlockSpec((1, 128), im(lambda b, g: (0, g)))
    cw = pl.BlockSpec((4, 128), im(lambda b, g: (0, g)))
    mat = pl.BlockSpec((1, 128, 128), im(lambda b, g: (g, 0, 0)))
    return seq, vec, cw, mat


def _lru_fwd(proj, conv_w, conv_b, w_a, b_a, w_i, b_i, lam, nb, s):
    def body(x_ref, gt_ref, cw_ref, cb_ref, wa_ref, ba_ref, wi_ref, bi_ref, lam_ref, out_ref):
        row = lax.broadcasted_iota(jnp.int32, (s, 128), 0)
        xc = _conv_fwd(x_ref[...], cw_ref[...], cb_ref[...], row)
        xcb = xc.astype(bf16)
        pa = _nn(xcb, wa_ref[0].astype(bf16)) + ba_ref[...]
        pi = _nn(xcb, wi_ref[0].astype(bf16)) + bi_ref[...]
        a, bx = _lru_gates(xc, pa, pi, lam_ref[...])
        h = _scan_fwd(a, bx, row)
        out_ref[...] = h * jax.nn.gelu(gt_ref[...])

    seq, vec, cw, mat = _lru_specs(s, "bg")
    return pl.pallas_call(
        body, name="lru_fwd", grid=(nb, LRU_BLOCKS),
        in_specs=[seq(16), seq(20), cw, vec, mat, vec, mat, vec, vec],
        out_specs=seq(0),
        out_shape=jax.ShapeDtypeStruct((nb * s, LRU_WIDTH), f32),
        compiler_params=_cp("parallel", "parallel"),
    )(proj, proj, conv_w, conv_b, w_a, b_a, w_i, b_i, lam)


def _lru_bwd(dmerged, proj, conv_w, conv_b, w_a, b_a, w_i, b_i, lam, nb, s):
    def body(dout_ref, x_ref, gt_ref, cw_ref, cb_ref, wa_ref, ba_ref, wi_ref, bi_ref, lam_ref,
             dx_ref, dgt_ref, dcw_ref, dcb_ref, dwa_ref, dba_ref, dwi_ref, dbi_ref, dlam_ref):
        row = lax.broadcasted_iota(jnp.int32, (s, 128), 0)
        x = x_ref[...]
        cwv = cw_ref[...]
        xc = _conv_fwd(x, cwv, cb_ref[...], row)
        xcb = xc.astype(bf16)
        wab, wib = wa_ref[0].astype(bf16), wi_ref[0].astype(bf16)
        pa = _nn(xcb, wab) + ba_ref[...]
        pi = _nn(xcb, wib) + bi_ref[...]
        (a, bx), gates_vjp = jax.vjp(_lru_gates, xc, pa, pi, lam_ref[...])
        h = _scan_fwd(a, bx, row)
        ge, gelu_vjp = jax.vjp(jax.nn.gelu, gt_ref[...])
        dout = dout_ref[...]
        dgt_ref[...] = gelu_vjp(dout * h)[0]
        adj = _scan_bwd(_shift_up(a, 1, row), dout * ge, row)
        dxc, dpa, dpi, dlam = gates_vjp((adj * _shift_dn(h, 1, row), adj))
        dpab, dpib = dpa.astype(bf16), dpi.astype(bf16)
        dxc = dxc + _nt(dpab, wab) + _nt(dpib, wib)
        dx_ref[...] = (cwv[3:4] * dxc + cwv[2:3] * _shift_up(dxc, 1, row) + cwv[1:2] * _shift_up(dxc, 2, row)
                       + cwv[0:1] * _shift_up(dxc, 3, row))

        @pl.when(pl.program_id(1) == 0)
        def _():
            for r in (dcw_ref, dcb_ref, dwa_ref, dba_ref, dwi_ref, dbi_ref, dlam_ref):
                r[...] = jnp.zeros_like(r)
        rsum = lambda v: jnp.sum(v, axis=0, keepdims=True)
        dcw_ref[...] += jnp.concatenate([rsum(dxc * _shift_dn(x, 3, row)), rsum(dxc * _shift_dn(x, 2, row)),
                                         rsum(dxc * _shift_dn(x, 1, row)), rsum(dxc * x)], axis=0)
        dcb_ref[...] += rsum(dxc)
        dwa_ref[0] += _tn(xcb, dpab)
        dwi_ref[0] += _tn(xcb, dpib)
        dba_ref[...] += rsum(dpa)
        dbi_ref[...] += rsum(dpi)
        dlam_ref[...] += dlam

    seq, vec, cw, mat = _lru_specs(s, "gb")
    t = nb * s
    vshape = jax.ShapeDtypeStruct((1, LRU_WIDTH), f32)
    mshape = jax.ShapeDtypeStruct((LRU_BLOCKS, 128, 128), f32)
    return pl.pallas_call(
        body, name="lru_bwd", grid=(LRU_BLOCKS, nb),
        in_specs=[seq(4), seq(16), seq(20), cw, vec, mat, vec, mat, vec, vec],
        out_specs=[seq(0), seq(0), cw, vec, mat, vec, mat, vec, vec],
        out_shape=[jax.ShapeDtypeStruct((t, LRU_WIDTH), f32), jax.ShapeDtypeStruct((t, LRU_WIDTH), f32),
                   jax.ShapeDtypeStruct((4, LRU_WIDTH), f32), vshape, mshape, vshape, mshape, vshape, vshape],
        compiler_params=_cp("parallel", "arbitrary"),
    )(dmerged, proj, proj, conv_w, conv_b, w_a, b_a, w_i, b_i, lam)


def _s5_disc(lr, li, ldt, bre, bim):
    dt = jnp.exp(ldt)
    mag = jnp.exp(lr * dt)
    lbr = mag * jnp.cos(li * dt)
    lbi = mag * jnp.sin(li * dt)
    den = lr * lr + li * li
    nr = lbr - 1.0
    fr = (nr * lr + lbi * li) / den
    fi = (lbi * lr - nr * li) / den
    bbr = fr[:, None, :] * bre - fi[:, None, :] * bim
    bbi = fr[:, None, :] * bim + fi[:, None, :] * bre
    return lbr, lbi, bbr, bbi


def _s5_prep(lr, li, ldt, bre, bim):
    def body(lr_ref, li_ref, ldt_ref, bre_ref, bim_ref, o1, o2, o3, o4):
        o1[...], o2[...], o3[...], o4[...] = _s5_disc(lr_ref[...], li_ref[...], ldt_ref[...], bre_ref[...], bim_ref[...])

    return pl.pallas_call(
        body, name="s5_prep", in_specs=[VMEM_SPEC] * 5, out_specs=[VMEM_SPEC] * 4,
        out_shape=[jax.ShapeDtypeStruct(lr.shape, f32), jax.ShapeDtypeStruct(lr.shape, f32),
                   jax.ShapeDtypeStruct(bre.shape, f32), jax.ShapeDtypeStruct(bre.shape, f32)],
    )(lr, li, ldt, bre, bim)


def _s5_prep_bwd(lr, li, ldt, bre, bim, cts):
    def body(lr_ref, li_ref, ldt_ref, bre_ref, bim_ref, g1, g2, g3, g4, o1, o2, o3, o4, o5):
        _, vjp = jax.vjp(_s5_disc, lr_ref[...], li_ref[...], ldt_ref[...], bre_ref[...], bim_ref[...])
        o1[...], o2[...], o3[...], o4[...], o5[...] = vjp((g1[...], g2[...], g3[...], g4[...]))

    return pl.pallas_call(
        body, name="s5_prep_bwd", in_specs=[VMEM_SPEC] * 9, out_specs=[VMEM_SPEC] * 5,
        out_shape=[jax.ShapeDtypeStruct(v.shape, f32) for v in (lr, li, ldt, bre, bim)],
    )(lr, li, ldt, bre, bim, *cts)


def _cmul(ar, ai, br, bi):
    return ar * br - ai * bi, ar * bi + ai * br


def _s5_pow_table(lr, li, n, row, up):
    ar = jnp.broadcast_to(lr, (n, lr.shape[1]))
    ai = jnp.broadcast_to(li, (n, li.shape[1]))
    shift = _shift_up if up else _shift_dn
    d = 1
    while d < n:
        ar, ai = _cmul(ar, ai, shift(ar, d, row, 1.0), shift(ai, d, row, 0.0))
        d *= 2
    return ar, ai


def _s5_step_factors(lr, li, row, up):
    sub = row & (SUBLANES - 1)
    out, pr, pi, d = [], lr, li, 1
    while d < SUBLANES:
        keep = (sub < SUBLANES - d) if up else (sub >= d)
        out.append((jnp.where(keep, pr, 0.0), jnp.where(keep, pi, 0.0)))
        pr, pi = _cmul(pr, pi, pr, pi)
        d *= 2
    return out


def _s5_scan(br, bi, steps, tab_r, tab_i, cr, ci, up):
    groups = list(range(br.shape[0] // SUBLANES))
    out_r, out_i = [None] * len(groups), [None] * len(groups)
    edge = slice(0, 1) if up else slice(SUBLANES - 1, SUBLANES)
    for g in (reversed(groups) if up else groups):
        rows = slice(g * SUBLANES, (g + 1) * SUBLANES)
        xr, xi = br[rows], bi[rows]
        for k, (mr, mi) in enumerate(steps):
            shift = SUBLANES - (1 << k) if up else 1 << k
            tr, ti = _cmul(mr, mi, pltpu.roll(xr, shift, 0), pltpu.roll(xi, shift, 0))
            xr, xi = xr + tr, xi + ti
        tr, ti = _cmul(tab_r, tab_i, cr, ci)
        hr, hi = xr + tr, xi + ti
        out_r[g], out_i[g] = hr, hi
        cr, ci = hr[edge], hi[edge]
    return jnp.concatenate(out_r, axis=0), jnp.concatenate(out_i, axis=0)


def _s5_specs(t, nb, nc):
    seq = pl.BlockSpec((t, 128), lambda k: (0, k))
    lvec = pl.BlockSpec((1, S5_BLOCK_STATES), lambda k: (0, k))
    dvec = pl.BlockSpec((1, 128), lambda k: (0, k))
    wmat = pl.BlockSpec((1, 128, S5_BLOCK_STATES), lambda k: (k, 0, 0))
    h0 = pl.BlockSpec((nb, None, nc, 2, S5_BLOCK_STATES), lambda k: (0, k, 0, 0, 0))
    states = pl.BlockSpec((t, S5_BLOCK_STATES), lambda k: (0, k))
    return seq, lvec, dvec, wmat, h0, states


def _s5_fwd(u, lbr, lbi, wbr, wbi, wcr, wci, dskip, nb, s):
    ln = min(S5_CHUNK, s)
    nc = s // ln

    def body(u_ref, lr_ref, li_ref, wbr_ref, wbi_ref, wcr_ref, wci_ref, d_ref, yg_ref, y_ref, h0_ref, hr_ref, hi_ref):
        row = lax.broadcasted_iota(jnp.int32, (ln, S5_BLOCK_STATES), 0)
        lr, li = lr_ref[...], li_ref[...]
        pr, pi = _s5_pow_table(lr, li, SUBLANES, row[:SUBLANES], False)
        steps = _s5_step_factors(lr, li, row[:SUBLANES], False)
        dv = d_ref[...]

        def chunk(b, n, h0r, h0i):
            st = pl.multiple_of(b * s + n * ln, ln)
            uc = u_ref[pl.ds(st, ln), :]
            ub = uc.astype(bf16)
            hr, hi = _s5_scan(_nn(ub, wbr_ref[0]), _nn(ub, wbi_ref[0]), steps, pr, pi, h0r, h0i, False)
            h0_ref[b, n, 0:1, :] = h0r
            h0_ref[b, n, 1:2, :] = h0i
            hrb, hib = hr.astype(bf16), hi.astype(bf16)
            hr_ref[pl.ds(st, ln), :] = hrb
            hi_ref[pl.ds(st, ln), :] = hib
            y = _nt(hrb, wcr_ref[0]) - _nt(hib, wci_ref[0]) + dv * uc
            y_ref[pl.ds(st, ln), :] = y
            yg_ref[pl.ds(st, ln), :] = jax.nn.gelu(y).astype(bf16)
            return hr[ln - 1:ln, :], hi[ln - 1:ln, :]

        def step(n, carry):
            return tuple(chunk(b, n, *carry[b]) for b in range(nb))

        z = jnp.zeros((1, S5_BLOCK_STATES), f32)
        lax.fori_loop(0, nc, step, ((z, z),) * nb)

    t = nb * s
    seq, lvec, dvec, wmat, h0, states = _s5_specs(t, nb, nc)
    return pl.pallas_call(
        body, name="s5_fwd", grid=(S5_BLOCKS,),
        in_specs=[seq, lvec, lvec, wmat, wmat, wmat, wmat, dvec],
        out_specs=[seq, seq, h0, states, states],
        out_shape=[jax.ShapeDtypeStruct((t, D_MODEL), bf16), jax.ShapeDtypeStruct((t, D_MODEL), f32),
                   jax.ShapeDtypeStruct((nb, S5_BLOCKS, nc, 2, S5_BLOCK_STATES), f32),
                   jax.ShapeDtypeStruct((t, S5_BLOCKS * S5_BLOCK_STATES), bf16),
                   jax.ShapeDtypeStruct((t, S5_BLOCKS * S5_BLOCK_STATES), bf16)],
        compiler_params=_cp("parallel"),
    )(u, lbr, lbi, wbr, wbi, wcr, wci, dskip)


def _s5_bwd(dyg, y, u, h0, hrs, his, lbr, lbi, wbr, wbi, wcr, wci, dskip, nb, s):
    ln = min(S5_CHUNK, s)
    nc = s // ln

    def body(dyg_ref, y_ref, u_ref, h0_ref, hr_ref, hi_ref, lr_ref, li_ref, wbr_ref, wbi_ref, wcr_ref, wci_ref, d_ref,
             du_ref, dlr_ref, dli_ref, dwbr_ref, dwbi_ref, dwcr_ref, dwci_ref, dd_ref):
        for r in (dlr_ref, dli_ref, dwbr_ref, dwbi_ref, dwcr_ref, dwci_ref, dd_ref):
            r[...] = jnp.zeros_like(r)
        row = lax.broadcasted_iota(jnp.int32, (ln, S5_BLOCK_STATES), 0)
        lr, li = lr_ref[...], li_ref[...]
        qr, qi = _s5_pow_table(lr, -li, SUBLANES, row[:SUBLANES], True)
        steps_up = _s5_step_factors(lr, -li, row[:SUBLANES], True)
        dv = d_ref[...]
        rsum = lambda v: jnp.sum(v, axis=0, keepdims=True)

        def chunk(b, n, gnr, gni):
            st = pl.multiple_of(b * s + n * ln, ln)
            uc = u_ref[pl.ds(st, ln), :]
            ub = uc.astype(bf16)
            h0v = h0_ref[b, n]
            h0r, h0i = h0v[0:1], h0v[1:2]
            hrb, hib = hr_ref[pl.ds(st, ln), :], hi_ref[pl.ds(st, ln), :]
            hr, hi = hrb.astype(f32), hib.astype(f32)
            dy = jax.vjp(jax.nn.gelu, y_ref[pl.ds(st, ln), :])[1](dyg_ref[pl.ds(st, ln), :])[0]
            dyb = dy.astype(bf16)
            dd_ref[...] += rsum(dy * uc)
            gr, gi = _s5_scan(_nn(dyb, wcr_ref[0]), -_nn(dyb, wci_ref[0]), steps_up, qr, qi, gnr, gni, True)
            hpr = jnp.where(row >= 1, pltpu.roll(hr, 1, 0), h0r)
            hpi = jnp.where(row >= 1, pltpu.roll(hi, 1, 0), h0i)
            dlr_ref[...] += rsum(gr * hpr + gi * hpi)
            dli_ref[...] += rsum(gi * hpr - gr * hpi)
            grb, gib = gr.astype(bf16), gi.astype(bf16)
            dwbr_ref[0] += _tn(ub, grb)
            dwbi_ref[0] += _tn(ub, gib)
            dwcr_ref[0] += _tn(dyb, hrb)
            dwci_ref[0] -= _tn(dyb, hib)
            du_ref[pl.ds(st, ln), :] = _nt(grb, wbr_ref[0]) + _nt(gib, wbi_ref[0]) + dv * dy
            return gr[0:1, :], gi[0:1, :]

        def step(i, carry):
            return tuple(chunk(b, nc - 1 - i, *carry[b]) for b in range(nb))

        z = jnp.zeros((1, S5_BLOCK_STATES), f32)
        lax.fori_loop(0, nc, step, ((z, z),) * nb)

    t = nb * s
    seq, lvec, dvec, wmat, h0s, states = _s5_specs(t, nb, nc)
    lshape = jax.ShapeDtypeStruct((1, S5_BLOCKS * S5_BLOCK_STATES), f32)
    wshape = jax.ShapeDtypeStruct((S5_BLOCKS, 128, S5_BLOCK_STATES), f32)
    return pl.pallas_call(
        body, name="s5_bwd", grid=(S5_BLOCKS,),
        in_specs=[seq, seq, seq, h0s, states, states, lvec, lvec, wmat, wmat, wmat, wmat, dvec],
        out_specs=[seq, lvec, lvec, wmat, wmat, wmat, wmat, dvec],
        out_shape=[jax.ShapeDtypeStruct((t, D_MODEL), f32), lshape, lshape, wshape, wshape, wshape, wshape,
                   jax.ShapeDtypeStruct((1, D_MODEL), f32)],
        compiler_params=_cp("parallel"),
    )(dyg, y, u, h0, hrs, his, lbr, lbi, wbr, wbi, wcr, wci, dskip)


def _blockdiag(w):
    w4 = w.reshape(S5_BLOCKS, 8, S5_GROUP, S5_STATE)
    same_group = jnp.eye(8, dtype=bool)[None, :, None, :, None]
    return jnp.where(same_group, w4[:, :, :, None, :], 0.0).reshape(S5_BLOCKS, 128, S5_BLOCK_STATES)


def _blockdiag_t(dw):
    d5 = dw.reshape(S5_BLOCKS, 8, S5_GROUP, 8, S5_STATE)
    diag = jnp.diagonal(d5, axis1=1, axis2=3)
    return jnp.moveaxis(diag, 3, 1).reshape(S5_GROUPS, S5_GROUP, S5_STATE)


def _glu_fwd(ygb, wa, wb, x, tm=512, tn=1024):
    t, d = x.shape

    def body(y_ref, wa_ref, wb_ref, x_ref, o_ref, p_ref, q_ref):
        p = _nn(y_ref[...], wa_ref[...])
        q = _nn(y_ref[...], wb_ref[...])
        p_ref[...] = p
        q_ref[...] = q
        o_ref[...] = x_ref[...] + p * jax.nn.sigmoid(q)

    tile = pl.BlockSpec((tm, tn), lambda i, j: (i, j))
    wsp = pl.BlockSpec((d, tn), lambda i, j: (0, j))
    out = jax.ShapeDtypeStruct((t, d), f32)
    return pl.pallas_call(
        body, name="glu_fwd", grid=(t // tm, d // tn),
        in_specs=[pl.BlockSpec((tm, d), lambda i, j: (i, 0)), wsp, wsp, tile],
        out_specs=[tile, tile, tile], out_shape=[out, out, out],
        compiler_params=_cp("parallel", "parallel"),
    )(ygb, wa, wb, x)


def _place():
    x, y, c = lax.axis_index("x"), lax.axis_index("y"), lax.axis_index("c")
    return x, y, c, [(1 - x, y), (x, 1 - y), (1 - x, 1 - y)]


def _all_gather(name, arrays):
    n = len(arrays)

    def body(*refs):
        ins, outs = refs[:n], refs[n:2 * n]
        send_sems, recv_sems, local_sems = refs[2 * n:]
        x, y, c, chips = _place()
        me, sib = (x, y, c), (x, y, 1 - c)

        def copy(i, k, block, to, src=None):
            dst = outs[i].at[4 * block[0] + 2 * block[1] + block[2]]
            return pltpu.make_async_remote_copy(
                src_ref=dst if src is None else src, dst_ref=dst,
                send_sem=send_sems.at[i * 7 + k], recv_sem=recv_sems.at[i * 7 + k],
                device_id=to, device_id_type=MESH)

        mine = [pltpu.make_async_copy(ins[i], outs[i].at[4 * x + 2 * y + c], local_sems.at[i]) for i in range(n)]
        for m in mine:
            m.start()
        first = []
        for i in range(n):
            first.append(copy(i, 0, me, sib, src=ins[i]))
            first += [copy(i, 1 + j, me, (*chip, c), src=ins[i]) for j, chip in enumerate(chips)]
        for cp in first:
            cp.start()
        passed = []
        for j, chip in enumerate(chips):
            for i in range(n):
                copy(i, 1 + j, (*chip, c), me).wait_recv()
                fwd = copy(i, 4 + j, (*chip, c), sib)
                fwd.start()
                passed.append(fwd)
        for i in range(n):
            copy(i, 0, sib, me).wait_recv()
        for j, chip in enumerate(chips):
            for i in range(n):
                copy(i, 4 + j, (*chip, 1 - c), me).wait_recv()
        for cp in first + passed:
            cp.wait_send()
        for m in mine:
            m.wait()

    return pl.pallas_call(
        body, name=name,
        in_specs=[ANY_SPEC] * n, out_specs=[ANY_SPEC] * n,
        out_shape=[jax.ShapeDtypeStruct((N_DEV,) + a.shape, a.dtype) for a in arrays],
        scratch_shapes=[pltpu.SemaphoreType.DMA((7 * n,)), pltpu.SemaphoreType.DMA((7 * n,)),
                        pltpu.SemaphoreType.DMA((n,))],
    )(*arrays)


def _tie(name, x, deps):
    def body(*refs):
        pass

    return pl.pallas_call(
        body, name=name, in_specs=[ANY_SPEC] * (1 + len(deps)), out_specs=ANY_SPEC,
        out_shape=jax.ShapeDtypeStruct(x.shape, x.dtype), input_output_aliases={0: 0},
    )(x, *deps)


def _xchg_copies(kind, srcs, lands, suffixes, send_sems, recv_sems):
    x, y, c, _ = _place()
    copies = []
    for i, (src, land, sfx) in enumerate(zip(srcs, lands, suffixes)):
        for k in range(N_DEV - 1):
            r = k + 1
            peer = (1 - x if r & 4 else x, 1 - y if r & 2 else y, 1 - c if r & 1 else c)
            if kind == "gather":
                s_ref, d_ref = src, land.at[(4 * x + 2 * y + c,) + sfx]
            else:
                s_ref, d_ref = src.at[4 * peer[0] + 2 * peer[1] + peer[2]], land.at[(k,) + sfx]
            copies.append(pltpu.make_async_remote_copy(
                src_ref=s_ref, dst_ref=d_ref, send_sem=send_sems.at[i * 7 + k], recv_sem=recv_sems.at[i * 7 + k],
                device_id=peer, device_id_type=MESH))
    return copies


def _xchg_start(name, kind, srcs, lands, suffixes=None):
    n = len(srcs)
    suffixes = suffixes or [()] * n

    def body(*refs):
        src, land = refs[:n], refs[n:2 * n]
        send_sems, recv_sems, token = refs[2 * n], refs[2 * n + 1], refs[-1]
        for cp in _xchg_copies(kind, src, land, suffixes, send_sems, recv_sems):
            cp.start()
        token[...] = jnp.zeros_like(token)

    arrays = list(srcs) + list(lands)
    outs = pl.pallas_call(
        body, name=name,
        out_shape=(pltpu.SemaphoreType.DMA((7 * n,)), pltpu.SemaphoreType.DMA((7 * n,)),
                   *[pltpu.HBM(a.shape, a.dtype) for a in arrays], jax.ShapeDtypeStruct((8, 128), f32)),
        in_specs=[HBM_SPEC] * (2 * n),
        out_specs=(SEM_SPEC, SEM_SPEC, *[HBM_SPEC] * (2 * n), VMEM_SPEC),
        input_output_aliases={i: 2 + i for i in range(2 * n)},
        compiler_params=pltpu.CompilerParams(has_side_effects=SIDE_EFFECT),
    )(*[pltpu.with_memory_space_constraint(a, pltpu.HBM) for a in arrays])
    return dict(kind=kind, n=n, suffixes=suffixes, send=outs[0], recv=outs[1], srcs=list(outs[2:2 + n]),
                lands=list(outs[2 + n:2 + 2 * n]), token=outs[-1])


def _xchg_wait(name, h, after, lands=None):
    n = h["n"]
    lands = h["lands"] if lands is None else lands

    def body(*refs):
        src, land = refs[:n], refs[n:2 * n]
        for cp in _xchg_copies(h["kind"], src, land, h["suffixes"], refs[2 * n], refs[2 * n + 1]):
            cp.wait_send()
            cp.wait_recv()

    arrays = h["srcs"] + list(lands)
    outs = pl.pallas_call(
        body, name=name,
        out_shape=tuple(pltpu.HBM(a.shape, a.dtype) for a in arrays),
        in_specs=[HBM_SPEC] * (2 * n) + [SEM_SPEC, SEM_SPEC] + [ANY_SPEC] * len(after),
        out_specs=tuple([HBM_SPEC] * (2 * n)),
        input_output_aliases={i: i for i in range(2 * n)},
        compiler_params=pltpu.CompilerParams(has_side_effects=SIDE_EFFECT),
    )(*arrays, h["send"], h["recv"], *after)
    return list(outs[n:])


def _rows(a):
    return a.reshape(-1, a.shape[-1])


def _row_tile(r):
    for tm in (512, 256, 128, 64, 32, 16, 8):
        if r % tm == 0:
            return tm
    return r


def _sum8(name, gathered):
    _, r, n = gathered.shape
    tm = _row_tile(r)

    def body(g_ref, o_ref):
        acc = g_ref[0]
        for k in range(1, N_DEV):
            acc = acc + g_ref[k]
        o_ref[...] = acc

    return pl.pallas_call(
        body, name=name, grid=(r // tm,),
        in_specs=[pl.BlockSpec((N_DEV, tm, n), lambda i: (0, i, 0))],
        out_specs=pl.BlockSpec((tm, n), lambda i: (i, 0)),
        out_shape=jax.ShapeDtypeStruct((r, n), f32),
        compiler_params=_cp("parallel"),
    )(gathered)


def _adamw(name, w, m, v, own, landed=None, slot=None):
    shape = w.shape
    w2, m2, v2 = _rows(w), _rows(m), _rows(v)
    r, n = w2.shape
    tm = _row_tile(r)
    c1 = 1.0 - ADAM_B1 ** ADAM_STEP
    c2 = 1.0 - ADAM_B2 ** ADAM_STEP
    extra = [] if landed is None else [landed.reshape(landed.shape[0], r, n)]
    row = pl.BlockSpec((tm, n), lambda i, *_: (i, 0))
    if slot is None:
        o2, own_spec, scalars = _rows(own), row, []
    else:
        dev, kind = slot
        scalars = [dev.reshape(1).astype(jnp.int32)]
        if kind == "lead":
            o2, own_spec = own.reshape(N_DEV, r, n), pl.BlockSpec((None, tm, n), lambda i, d: (d[0], i, 0))
        elif kind == "rows":
            o2, own_spec = own, pl.BlockSpec((tm, n), lambda i, d: (d[0] * (r // tm) + i, 0))
        else:
            o2, own_spec = own, pl.BlockSpec((tm, n), lambda i, d: (i, d[0]))

    def body(*refs):
        w_ref, m_ref, v_ref, o_ref = refs[len(scalars):len(scalars) + 4]
        refs = refs[len(scalars) + 4:]
        g = o_ref[...]
        if extra:
            for k in range(extra[0].shape[0]):
                g = g + refs[0][k].astype(f32)
        g_ref, d_ref, mn_ref, vn_ref = refs[len(extra):]
        mn = ADAM_B1 * m_ref[...] + (1.0 - ADAM_B1) * g
        vn = ADAM_B2 * v_ref[...] + (1.0 - ADAM_B2) * (g * g)
        g_ref[...] = g
        d_ref[...] = -ADAM_LR * ((mn / c1) / (jnp.sqrt(vn / c2) + ADAM_EPS) + ADAM_WD * w_ref[...])
        mn_ref[...] = mn
        vn_ref[...] = vn

    outs = pl.pallas_call(
        body, name=name,
        grid_spec=pltpu.PrefetchScalarGridSpec(
            num_scalar_prefetch=len(scalars), grid=(r // tm,),
            in_specs=[row] * 3 + [own_spec] + [pl.BlockSpec((e.shape[0], tm, n), lambda i, *_: (0, i, 0)) for e in extra],
            out_specs=[row] * 4),
        out_shape=[jax.ShapeDtypeStruct((r, n), f32)] * 4,
        compiler_params=_cp("parallel"),
    )(*scalars, w2, m2, v2, o2, *extra)
    return [o.reshape(shape) for o in outs]


def _pack(arrays):
    flat = jnp.concatenate([a.reshape(-1).astype(f32) for a in arrays])
    pad = (-flat.shape[0]) % (128 * (512 if flat.shape[0] > 128 * 512 else 8))
    return jnp.pad(flat, (0, pad)).reshape(-1, 128)


def _unpack(packed, shapes):
    flat = packed.reshape(-1)
    out, off = [], 0
    for s in shapes:
        n = math.prod(s)
        out.append(flat[off:off + n].reshape(s))
        off += n
    return out


def _local_step(x, target, w, weights_of, send, last_small, on_loss, nb, s):
    cos, sin = _rope_tables(s)
    g = {}
    ffn_saved = {}
    ffn_bufs = [lax.empty((N_DEV, 2, 2) + shp, f32)
                for shp in ((D_MODEL, FF_SHARD), (D_MODEL, FF_SHARD), (FF_SHARD, D_MODEL))]

    def ffn(xin, l, h, wts):
        y, a, b = _ffn_fwd(f"ffn_fwd_{l}{h}", xin, w["ffn_g"][l][h], *wts)
        ffn_saved[(l, h)] = (xin, a, b, wts)
        return y

    def ffn_back(dy, l, h):
        xin, a, b, wts = ffn_saved[(l, h)]
        dx, dg, hb, dyh, u, da, db = _ffn_dx(f"ffn_dx_{l}{h}", dy, xin, w["ffn_g"][l][h], *wts, a, b)
        g[f"ffn_g_{l}{h}"] = dg
        if (l, h) != (0, 0):
            halves = {}
            for k, (n, xt, y) in enumerate((("ffn_w1", hb, da), ("ffn_w3", hb, db), ("ffn_w2", u, dyh))):
                ffn_bufs[k], halves[n] = _ffn_dw_one(f"ffn_dw_{l}{h}_{n[4:]}", xt, y, ffn_bufs[k], l, h)
            return send(f"ffn_{l}{h}", halves, dx)
        hb = last_small(g, hb)
        ffn_bufs[0], half = _ffn_dw_one("ffn_dw_00_w1", hb, da, ffn_bufs[0], l, h)
        hb = send("ffn_00_w1", {"ffn_w1": half}, hb)
        ffn_bufs[1], half = _ffn_dw_one("ffn_dw_00_w3", hb, db, ffn_bufs[1], l, h)
        u = send("ffn_00_w3", {"ffn_w3": half}, u)
        ffn_bufs[2], half = _ffn_dw_one("ffn_dw_00_w2", u, dyh, ffn_bufs[2], l, h)
        return send("ffn_00_w2", {"ffn_w2": half}, dx)

    def slots(t):
        return t.reshape(N_DEV, D_MODEL // N_DEV, D_MODEL)

    x1 = ffn(x, 0, 0, weights_of(0, [])["ffn"])
    wg = weights_of(1, [x1])
    w_in, w_out = wg["w_in"], wg["w_out"]
    _, h0b = _norm_fwd("mix_norm_0", x1, w["mix_g"][0])
    proj = _mm("in_proj", h0b, w_in, "nn", tn=1536)[0]
    o_raw, rprev, mret = _ret_fwd(proj, cos, sin, w["ret_g"], nb, s)
    lru = _lru_fwd(proj, w["conv_w"], w["conv_b"], w["lru_w_a"], w["lru_b_a"], w["lru_w_i"], w["lru_b_i"], w["lru_lam"], nb, s)
    merged = _ew("merge", lambda a, b: (jnp.concatenate([a, b], axis=1),), [mret, lru], [(D_MODEL, bf16)])[0]
    x2 = _mm("out_proj", merged, w_out, "nn", extras=[x1], epilogue=lambda acc, r: (acc + r,))[0]
    x3 = ffn(x2, 0, 1, weights_of(2, [x2])["ffn"])
    x4 = ffn(x3, 1, 0, weights_of(3, [x3])["ffn"])
    u, _ = _norm_fwd("mix_norm_1", x4, w["mix_g"][1])
    lbr, lbi, bbr, bbi = _s5_prep(w["s5_lr"], w["s5_li"], w["s5_ldt"], w["s5_bre"], w["s5_bim"])
    lbr_f, lbi_f = lbr.reshape(1, -1), lbi.reshape(1, -1)
    wbr, wbi = _blockdiag(bbr).astype(bf16), _blockdiag(bbi).astype(bf16)
    wcr, wci = _blockdiag(w["s5_cre"]).astype(bf16), _blockdiag(w["s5_cim"]).astype(bf16)
    ygb, ypre, h0s, hrs, his = _s5_fwd(u, lbr_f, lbi_f, wbr, wbi, wcr, wci, w["s5_d"], nb, s)
    wg = weights_of(4, [ygb])
    glu_a, glu_b = wg["glu_a"], wg["glu_b"]
    x5, gp, gq = _glu_fwd(ygb, glu_a, glu_b, x4)
    x6 = ffn(x5, 1, 1, weights_of(5, [x5])["ffn"])
    loss, dx6, g["final_g"] = _final_loss(x6, w["final_g"], target)
    dx6 = on_loss(loss, dx6)

    dx5 = ffn_back(dx6, 1, 1)

    def glu_bwd(d, p, q):
        sg = jax.nn.sigmoid(q)
        return d * sg, d * p * sg * (1.0 - sg)

    dp, dq = _ew("glu_bwd", glu_bwd, [dx5, gp, gq], [(D_MODEL, bf16), (D_MODEL, bf16)])
    dyg = _mm("glu_dy_a", dp, glu_a, "nt")[0]
    dyg = _mm("glu_dy_b", dq, glu_b, "nt", extras=[dyg], epilogue=lambda acc, r: (acc + r,))[0]
    g["glu_a"], ga_half = _mm_tn("glu_dw_a", ygb, dp)
    g["glu_b"], gb_half = _mm_tn("glu_dw_b", ygb, dq)
    dyg = send("glu", {"glu_a": slots(ga_half), "glu_b": slots(gb_half)}, dyg)
    du, dlr, dli, dwbr, dwbi, dwcr, dwci, g["s5_d"] = _s5_bwd(dyg, ypre, u, h0s, hrs, his, lbr_f, lbi_f, wbr, wbi, wcr, wci, w["s5_d"], nb, s)
    g["s5_cre"], g["s5_cim"] = _blockdiag_t(dwcr), _blockdiag_t(dwci)
    g["s5_lr"], g["s5_li"], g["s5_ldt"], g["s5_bre"], g["s5_bim"] = _s5_prep_bwd(
        w["s5_lr"], w["s5_li"], w["s5_ldt"], w["s5_bre"], w["s5_bim"],
        (dlr.reshape(S5_GROUPS, S5_STATE), dli.reshape(S5_GROUPS, S5_STATE), _blockdiag_t(dwbr), _blockdiag_t(dwbi)))
    dx4, g["mix_g_1"] = _norm_bwd("mix_norm_1_bwd", du, x4, w["mix_g"][1], dx5)
    dx3 = ffn_back(dx4, 1, 0)
    dx2 = ffn_back(dx3, 0, 1)
    dmerged = _mm("out_proj_dx", dx2, w_out, "nt")[0]
    g["w_out"], wo_half = _mm_tn("out_proj_dw", merged, dx2)
    dmerged = send("w_out", {"w_out": slots(wo_half)}, dmerged)
    dq_, dk_, dv_, dgate, g["ret_g"] = _ret_bwd(dmerged, o_raw, rprev, proj, cos, sin, w["ret_g"], nb, s)
    (dxl, dgl, g["conv_w"], g["conv_b"], g["lru_w_a"], g["lru_b_a"], g["lru_w_i"], g["lru_b_i"], g["lru_lam"]) = _lru_bwd(
        dmerged, proj, w["conv_w"], w["conv_b"], w["lru_w_a"], w["lru_b_a"], w["lru_w_i"], w["lru_b_i"], w["lru_lam"], nb, s)
    dproj = _ew("dproj", lambda *p: (jnp.concatenate(p, axis=1),), [dq_, dk_, dv_, dgate, dxl, dgl], [(3072, bf16)])[0]
    dh0 = _mm("in_proj_dx", dproj, w_in, "nt")[0]
    g["w_in"], wi_half = _mm_tn("in_proj_dw", h0b, dproj)
    dh0 = send("w_in", {"w_in": jnp.transpose(wi_half.reshape(D_MODEL, N_DEV, IN_SHARD), (1, 0, 2))}, dh0)
    dx1, g["mix_g_0"] = _norm_bwd("mix_norm_0_bwd", dh0, x1, w["mix_g"][0], dx2)
    dx0 = ffn_back(dx1, 0, 0)
    g["ffn_w1"], g["ffn_w3"], g["ffn_w2"] = ffn_bufs
    return loss, dx0, g


_WEIGHTS = ["ffn_norm_g", "ffn_w1", "ffn_w3", "ffn_w2", "mix_norm_g", "w_in_even", "w_out_even", "ret_norm_g", "conv_w",
            "conv_b", "lru_w_a", "lru_b_a", "lru_w_i", "lru_b_i", "lru_lambda", "s5_lambda_re", "s5_lambda_im", "s5_log_dt",
            "s5_b_re", "s5_b_im", "s5_c_re", "s5_c_im", "s5_d", "glu_w_a", "glu_w_b", "final_norm_g"]
_BIG = ["ffn_w1", "ffn_w3", "ffn_w2", "w_in_even", "w_out_even", "glu_w_a", "glu_w_b"]
_SMALL_SHARDED = ["ffn_norm_g", "conv_w", "s5_d"]
_SMALL = [n for n in _WEIGHTS if n not in _BIG]
_MIDSIZE = ["lru_w_a", "lru_w_i", "s5_b_re", "s5_b_im", "s5_c_re", "s5_c_im"]


def kernel(x, ffn_norm_g, ffn_w1, ffn_w3, ffn_w2, mix_norm_g, w_in_even, w_out_even, ret_norm_g, conv_w, conv_b, lru_w_a, lru_b_a, lru_w_i, lru_b_i, lru_lambda, s5_lambda_re, s5_lambda_im, s5_log_dt, s5_b_re, s5_b_im, s5_c_re, s5_c_im, s5_d, glu_w_a, glu_w_b, final_norm_g, loss_target, m_ffn_norm_g, m_ffn_w1, m_ffn_w3, m_ffn_w2, m_mix_norm_g, m_w_in_even, m_w_out_even, m_ret_norm_g, m_conv_w, m_conv_b, m_lru_w_a, m_lru_b_a, m_lru_w_i, m_lru_b_i, m_lru_lambda, m_s5_lambda_re, m_s5_lambda_im, m_s5_log_dt, m_s5_b_re, m_s5_b_im, m_s5_c_re, m_s5_c_im, m_s5_d, m_glu_w_a, m_glu_w_b, m_final_norm_g, v_ffn_norm_g, v_ffn_w1, v_ffn_w3, v_ffn_w2, v_mix_norm_g, v_w_in_even, v_w_out_even, v_ret_norm_g, v_conv_w, v_conv_b, v_lru_w_a, v_lru_b_a, v_lru_w_i, v_lru_b_i, v_lru_lambda, v_s5_lambda_re, v_s5_lambda_im, v_s5_log_dt, v_s5_b_re, v_s5_b_im, v_s5_c_re, v_s5_c_im, v_s5_d, v_glu_w_a, v_glu_w_b, v_final_norm_g):
    a = dict(locals())
    nb, s, d = x.shape
    ax, ay, ac = lax.axis_index("x"), lax.axis_index("y"), lax.axis_index("c")
    dev = 4 * ax + 2 * ay + ac
    chip = 2 * ax + ay

    def ffn_shards(l, h):
        extra = FF_PAD - FF_SHARD
        return [jnp.pad(ffn_w1[l, h].astype(bf16), ((0, 0), (0, extra))), jnp.pad(ffn_w3[l, h].astype(bf16), ((0, 0), (0, extra))),
                jnp.pad(ffn_w2[l, h].astype(bf16), ((0, extra), (0, 0)))]

    first = _all_gather("ag_first", ffn_shards(0, 0) + [_pack([ffn_norm_g, conv_w, s5_d])])
    sm = first[3].reshape(N_DEV, -1)
    ffn_g_full = jnp.transpose(sm[:, :512].reshape(N_DEV, 2, 2, 128), (1, 2, 0, 3)).reshape(2, 2, D_MODEL)
    conv_w_full = jnp.transpose(sm[:, 512:768].reshape(N_DEV, 4, 64), (1, 0, 2)).reshape(4, LRU_WIDTH)
    s5_d_full = sm[:, 768:896].reshape(1, D_MODEL)

    ag_src = [None, [w_in_even[0].astype(bf16), w_out_even[0].astype(bf16)], ffn_shards(0, 1), ffn_shards(1, 0),
              [glu_w_a[0].astype(bf16), glu_w_b[0].astype(bf16)], ffn_shards(1, 1)]
    ag, token = [None], first[0]
    for k, grp in enumerate(ag_src):
        if grp is None:
            continue
        grp[0] = _tie(f"tie_ag_{k}", grp[0], [token])
        lands = [lax.dynamic_update_index_in_dim(lax.empty((N_DEV,) + t.shape, bf16), t, dev, 0) for t in grp]
        ag.append(_xchg_start(f"ag_start_{k}", "gather", grp, lands))
        token = ag[-1]["token"]

    def weights_of(k, after):
        if k == 0:
            return {"ffn": [first[0], _tie("tie_ag_started", first[1], [h["token"] for h in ag[1:]]), first[2]]}
        got = _xchg_wait(f"ag_wait_{k}", ag[k], after)
        if k == 1:
            return {"w_in": jnp.transpose(got[0], (1, 0, 2)).reshape(D_MODEL, N_DEV * IN_SHARD),
                    "w_out": got[1].reshape(D_MODEL, D_MODEL)}
        if k == 4:
            return {"glu_a": got[0].reshape(D_MODEL, D_MODEL), "glu_b": got[1].reshape(D_MODEL, D_MODEL)}
        return {"ffn": got}

    ffn_lands = [lax.empty((N_DEV - 1, 2, 2) + shp, bf16)
                 for shp in ((D_MODEL, FF_SHARD), (D_MODEL, FF_SHARD), (FF_SHARD, D_MODEL))]
    rs = []

    ffn_names = ("ffn_w1", "ffn_w3", "ffn_w2")

    def send(group, arrays, carry):
        srcs = list(arrays.values())
        if group.startswith("ffn_"):
            which = [ffn_names.index(n) for n in arrays]
            sfx = [(int(group[4]), int(group[5]))] * len(which)
            h = _xchg_start("rs_start_" + group, "scatter", srcs, [ffn_lands[k] for k in which], sfx)
            for k, land in zip(which, h["lands"]):
                ffn_lands[k] = land
        else:
            h = _xchg_start("rs_start_" + group, "scatter", srcs,
                            [lax.empty((N_DEV - 1,) + t.shape[1:], bf16) for t in srcs])
        rs.append((group, list(arrays), h))
        return _tie("tie_" + group, carry, [h["token"]])

    w = {
        "ffn_g": [[ffn_g_full[l, h].reshape(1, D_MODEL) for h in range(2)] for l in range(2)],
        "mix_g": [mix_norm_g[0:1], mix_norm_g[1:2]],
        "ret_g": ret_norm_g, "conv_w": conv_w_full, "conv_b": conv_b,
        "lru_w_a": lru_w_a[0], "lru_b_a": lru_b_a, "lru_w_i": lru_w_i[0], "lru_b_i": lru_b_i, "lru_lam": lru_lambda,
        "s5_lr": s5_lambda_re[0], "s5_li": s5_lambda_im[0], "s5_ldt": s5_log_dt.reshape(S5_GROUPS, 1),
        "s5_bre": jnp.swapaxes(s5_b_re[0], 1, 2), "s5_bim": jnp.swapaxes(s5_b_im[0], 1, 2),
        "s5_cre": s5_c_re[0], "s5_cim": s5_c_im[0], "s5_d": s5_d_full,
        "final_g": final_norm_g.reshape(1, D_MODEL),
    }

    small_grads = {}

    def last_small(g, carry):
        part = _small_partials(g)
        mine = _pack([part[n] for n in _SMALL])
        land = lax.dynamic_update_index_in_dim(lax.empty((N_DEV,) + mine.shape, f32), mine, dev, 0)
        h = _xchg_start("ag_start_small_grads", "gather", [mine], [land])
        small_grads.update(h=h, shapes=[part[n].shape for n in _SMALL])
        return _tie("tie_small_grads", carry, [h["token"]])

    total_loss = []

    def on_loss(part, carry):
        total_loss.append(lax.psum(part[0, 0], ("x", "y", "c")))
        return _tie("tie_loss", carry, [jnp.broadcast_to(total_loss[0], (8, 128))])

    _, dx, g = _local_step(x.reshape(nb * s, d), loss_target.reshape(nb * s, d), w, weights_of, send, last_small,
                           on_loss, nb, s)
    loss = total_loss[0]
    (gath,) = _xchg_wait("ag_wait_small_grads", small_grads["h"], [dx])
    full = dict(zip(_SMALL, _unpack(_sum8("sum_small_grads", gath), small_grads["shapes"])))
    for n in _SMALL_SHARDED:
        width = a[n].shape[-1]
        full[n] = lax.dynamic_slice_in_dim(full[n], dev * width, width, axis=full[n].ndim - 1)
    res = {n: _adamw("adamw_" + n, a[n], a["m_" + n], a["v_" + n], full[n]) for n in _MIDSIZE}
    tiny = [n for n in _SMALL if n not in _MIDSIZE]
    shapes = [a[n].shape for n in tiny]
    packed = _adamw("adamw_small", _pack([a[n] for n in tiny]), _pack([a["m_" + n] for n in tiny]),
                    _pack([a["v_" + n] for n in tiny]), _pack([full[n] for n in tiny]))
    res.update({n: vals for n, vals in zip(tiny, zip(*[_unpack(p, shapes) for p in packed]))})
    return _finish(a, g, dx, loss, res, packed, rs, ffn_lands, dev, nb, s, d)


def _small_partials(g):
    return {
        "ffn_norm_g": jnp.stack([jnp.stack([g[f"ffn_g_{l}{h}"][0] for h in range(2)]) for l in range(2)]),
        "mix_norm_g": jnp.concatenate([g["mix_g_0"], g["mix_g_1"]], axis=0),
        "ret_norm_g": g["ret_g"], "conv_w": g["conv_w"][None], "conv_b": g["conv_b"],
        "lru_w_a": g["lru_w_a"][None], "lru_b_a": g["lru_b_a"], "lru_w_i": g["lru_w_i"][None], "lru_b_i": g["lru_b_i"],
        "lru_lambda": g["lru_lam"], "s5_lambda_re": g["s5_lr"][None], "s5_lambda_im": g["s5_li"][None],
        "s5_log_dt": g["s5_ldt"].reshape(1, S5_GROUPS),
        "s5_b_re": jnp.swapaxes(g["s5_bre"], 1, 2)[None], "s5_b_im": jnp.swapaxes(g["s5_bim"], 1, 2)[None],
        "s5_c_re": g["s5_cre"][None], "s5_c_im": g["s5_cim"][None], "s5_d": g["s5_d"], "final_norm_g": g["final_g"][0],
    }


def _finish(a, g, dx, loss, res, packed, rs, ffn_lands, dev, nb, s, d):
    landed = {}
    for group, names, h in rs:
        if not group.startswith("ffn_"):
            landed.update(zip(names, _xchg_wait("rs_wait_" + group, h, [dx])))
    kinds = {"ffn_w1": "lead", "ffn_w3": "lead", "ffn_w2": "lead", "w_in": "cols", "w_out": "rows", "glu_a": "rows",
             "glu_b": "rows"}

    def update(n, short):
        res[n] = _adamw("adamw_" + n, a[n], a["m_" + n], a["v_" + n], g[short],
                        landed[short].reshape((N_DEV - 1,) + a[n].shape), slot=(dev, kinds[short]))

    for n, short in zip(_BIG[3:], ("w_in", "w_out", "glu_a", "glu_b")):
        update(n, short)
    after = [dx, packed[0]] + [res[n][0] for n in _BIG[3:] + _MIDSIZE]
    ffn_names = ("ffn_w1", "ffn_w3", "ffn_w2")
    for group, names, h in rs:
        if group.startswith("ffn_") and len(names) == 3:
            ffn_lands[:] = _xchg_wait("rs_wait_" + group, h, after, ffn_lands)
    for k, n in enumerate(ffn_names):
        for group, names, h in rs:
            if group.startswith("ffn_") and names == [n]:
                (ffn_lands[k],) = _xchg_wait("rs_wait_" + group, h, after, [ffn_lands[k]])
        landed[n] = ffn_lands[k]
        update(n, n)
        after = after + [res[n][0]]

    out = [loss, dx.reshape(nb, s, d)]
    for k in range(4):
        out += [res[n][k] for n in _WEIGHTS]
    return tuple(out)
```

```python
import functools
import math

import numpy as np
import jax
import jax.numpy as jnp
from jax import lax
from jax.experimental import pallas as pl
from jax.experimental.pallas import tpu as pltpu

f32 = jnp.float32
bf16 = jnp.bfloat16

D_MODEL = 1024
N_DEV = 8
EPS = 1e-6
RET_HEADS = 4
HEAD_DIM = 128
RET_WIDTH = 512
RET_CHUNK = 128
ROPE_BASE = 10000.0
LRU_WIDTH = 512
LRU_BLOCKS = 4
LRU_C = 8.0
S5_GROUP = 16
S5_GROUPS = 64
S5_STATE = 64
S5_CHUNK = 1024
S5_BLOCKS = 8
S5_BLOCK_STATES = 512
SUBLANES = 8
D_FF = 2816
FF_SHARD = D_FF // N_DEV
FF_PAD = 384
IN_SHARD = 3072 // N_DEV
ADAM_LR = 0.001
ADAM_B1 = 0.9
ADAM_B2 = 0.999
ADAM_EPS = 1e-08
ADAM_WD = 0.01
ADAM_STEP = 10

VMEM_LIMIT = 56 * 1024 * 1024
VMEM_SPEC = pl.BlockSpec(memory_space=pltpu.VMEM)
ANY_SPEC = pl.BlockSpec(memory_space=pl.ANY)
HBM_SPEC = pl.BlockSpec(memory_space=pltpu.HBM)
SEM_SPEC = pl.BlockSpec(memory_space=pltpu.SEMAPHORE)
SIDE_EFFECT = pltpu.SideEffectType.DATAFLOW_SIDE_EFFECTING
MESH = pl.DeviceIdType.MESH


def _cp(*sem):
    return pltpu.CompilerParams(dimension_semantics=sem, vmem_limit_bytes=VMEM_LIMIT)


def _nn(a, b):
    return jnp.dot(a, b, preferred_element_type=f32)


def _nt(a, b):
    return lax.dot_general(a, b, (((1,), (1,)), ((), ())), preferred_element_type=f32)


def _tn(a, b):
    return lax.dot_general(a, b, (((0,), (0,)), ((), ())), preferred_element_type=f32)


def _rms_fwd(x, g):
    r = lax.rsqrt(jnp.mean(x * x, axis=-1, keepdims=True) + EPS)
    xn = x * r
    return xn * g, xn, r


def _rms_bwd(dh, xn, r, g):
    dxn = dh * g
    dx = r * (dxn - xn * jnp.mean(dxn * xn, axis=-1, keepdims=True))
    dg = jnp.sum(dh * xn, axis=0, keepdims=True)
    return dx, dg


def _shift_dn(v, d, row, fill=0.0):
    return jnp.where(row >= d, pltpu.roll(v, d, 0), fill)


def _shift_up(v, d, row, fill=0.0):
    n = v.shape[0]
    return jnp.where(row < n - d, pltpu.roll(v, n - d, 0), fill)


def _ew(name, fn, ins, outs, tm=512):
    t = ins[0].shape[0]
    n_in = len(ins)

    def body(*refs):
        res = fn(*[r[...] for r in refs[:n_in]])
        for o, v in zip(refs[n_in:], res):
            o[...] = v.astype(o.dtype)

    return pl.pallas_call(
        body, name=name, grid=(t // tm,),
        in_specs=[pl.BlockSpec((tm, a.shape[1]), lambda i: (i, 0)) for a in ins],
        out_specs=[pl.BlockSpec((tm, n), lambda i: (i, 0)) for n, _ in outs],
        out_shape=[jax.ShapeDtypeStruct((t, n), dt) for n, dt in outs],
        compiler_params=_cp("parallel"),
    )(*ins)


def _mm(name, x, w, kind, extras=(), epilogue=None, outs=None, tm=512, tn=1024):
    t = x.shape[0]
    n = w.shape[1] if kind == "nn" else w.shape[0]
    tn = min(tn, n)
    outs = outs or [f32]
    n_ex = len(extras)

    def body(x_ref, w_ref, *refs):
        xb = x_ref[...].astype(bf16)
        acc = _nn(xb, w_ref[...]) if kind == "nn" else _nt(xb, w_ref[...])
        res = epilogue(acc, *[r[...] for r in refs[:n_ex]]) if epilogue else (acc,)
        for o, v in zip(refs[n_ex:], res):
            o[...] = v.astype(o.dtype)

    w_spec = (pl.BlockSpec((w.shape[0], tn), lambda i, j: (0, j)) if kind == "nn"
              else pl.BlockSpec((tn, w.shape[1]), lambda i, j: (j, 0)))
    tile = pl.BlockSpec((tm, tn), lambda i, j: (i, j))
    return pl.pallas_call(
        body, name=name, grid=(t // tm, n // tn),
        in_specs=[pl.BlockSpec((tm, x.shape[1]), lambda i, j: (i, 0)), w_spec] + [tile] * n_ex,
        out_specs=[tile] * len(outs),
        out_shape=[jax.ShapeDtypeStruct((t, n), dt) for dt in outs],
        compiler_params=_cp("parallel", "parallel"),
    )(x, w, *extras)


def _mm_tn(name, x, y, tk=1024, tn=1024, tt=1024):
    t, k = x.shape
    n = y.shape[1]
    tk, tn, tt = min(tk, k), min(tn, n), min(tt, t)

    def body(x_ref, y_ref, o_ref, ob_ref):
        @pl.when(pl.program_id(2) == 0)
        def _():
            o_ref[...] = jnp.zeros_like(o_ref)
        o_ref[...] += _tn(x_ref[...].astype(bf16), y_ref[...].astype(bf16))

        @pl.when(pl.program_id(2) == pl.num_programs(2) - 1)
        def _():
            ob_ref[...] = o_ref[...].astype(bf16)

    out = pl.BlockSpec((tk, tn), lambda i, j, s: (i, j))
    return pl.pallas_call(
        body, name=name, grid=(k // tk, n // tn, t // tt),
        in_specs=[pl.BlockSpec((tt, tk), lambda i, j, s: (s, i)), pl.BlockSpec((tt, tn), lambda i, j, s: (s, j))],
        out_specs=[out, out],
        out_shape=[jax.ShapeDtypeStruct((k, n), f32), jax.ShapeDtypeStruct((k, n), bf16)],
        compiler_params=_cp("parallel", "parallel", "arbitrary"),
    )(x, y)


def _norm_fwd(name, x, g, tm=512):
    t, d = x.shape

    def body(x_ref, g_ref, h_ref, hb_ref):
        h, _, _ = _rms_fwd(x_ref[...], g_ref[...])
        h_ref[...] = h
        hb_ref[...] = h.astype(bf16)

    row = pl.BlockSpec((tm, d), lambda i: (i, 0))
    return pl.pallas_call(
        body, name=name, grid=(t // tm,),
        in_specs=[row, pl.BlockSpec((1, d), lambda i: (0, 0))],
        out_specs=[row, row],
        out_shape=[jax.ShapeDtypeStruct((t, d), f32), jax.ShapeDtypeStruct((t, d), bf16)],
        compiler_params=_cp("parallel"),
    )(x, g)


def _norm_bwd(name, dh, x, g, dres, tm=512):
    t, d = x.shape

    def body(dh_ref, x_ref, g_ref, dres_ref, dx_ref, dg_ref):
        gv = g_ref[...]
        _, xn, r = _rms_fwd(x_ref[...], gv)
        dx, dg = _rms_bwd(dh_ref[...], xn, r, gv)
        dx_ref[...] = dres_ref[...] + dx

        @pl.when(pl.program_id(0) == 0)
        def _():
            dg_ref[...] = jnp.zeros_like(dg_ref)
        dg_ref[...] += dg

    row = pl.BlockSpec((tm, d), lambda i: (i, 0))
    vec = pl.BlockSpec((1, d), lambda i: (0, 0))
    return pl.pallas_call(
        body, name=name, grid=(t // tm,),
        in_specs=[row, row, vec, row],
        out_specs=[row, vec],
        out_shape=[jax.ShapeDtypeStruct((t, d), f32), jax.ShapeDtypeStruct((1, d), f32)],
        compiler_params=_cp("arbitrary"),
    )(dh, x, g, dres)


def _final_loss(x, g, target, tm=512):
    t, d = x.shape

    def body(x_ref, g_ref, t_ref, loss_ref, dx_ref, dg_ref):
        gv = g_ref[...]
        y, xn, r = _rms_fwd(x_ref[...], gv)
        err = y - t_ref[...]
        dy = err * (1.0 / d)
        dx, dg = _rms_bwd(dy, xn, r, gv)
        dx_ref[...] = dx

        @pl.when(pl.program_id(0) == 0)
        def _():
            dg_ref[...] = jnp.zeros_like(dg_ref)
            loss_ref[...] = jnp.zeros_like(loss_ref)
        dg_ref[...] += dg
        loss_ref[...] += jnp.full((1, 128), 0.5 / d, f32) * jnp.sum(err * err)

    row = pl.BlockSpec((tm, d), lambda i: (i, 0))
    vec = pl.BlockSpec((1, d), lambda i: (0, 0))
    return pl.pallas_call(
        body, name="final_loss", grid=(t // tm,),
        in_specs=[row, vec, row],
        out_specs=[pl.BlockSpec((1, 128), lambda i: (0, 0)), row, vec],
        out_shape=[jax.ShapeDtypeStruct((1, 128), f32), jax.ShapeDtypeStruct((t, d), f32),
                   jax.ShapeDtypeStruct((1, d), f32)],
        compiler_params=_cp("arbitrary"),
    )(x, g, target)


def _load_ffn_weights(hbm_refs, vmem_refs, sems):
    @pl.when(pl.program_id(0) == 0)
    def _():
        copies = []
        for k, (src, dst) in enumerate(zip(hbm_refs, vmem_refs)):
            for j in range(N_DEV):
                half = pl.ds((j % 2) * FF_PAD, FF_PAD)
                window = dst.at[j // 2, half, :] if k == 2 else dst.at[j // 2, :, half]
                copies.append(pltpu.make_async_copy(src.at[j], window, sems.at[k * N_DEV + j]))
        for cp in copies:
            cp.start()
        for cp in copies:
            cp.wait()


def _ffn_weight_scratch(nj, d, ff):
    return [pltpu.VMEM((nj, d, ff), bf16), pltpu.VMEM((nj, d, ff), bf16), pltpu.VMEM((nj, ff, d), bf16),
            pltpu.SemaphoreType.DMA((3 * N_DEV,))]


def _ffn_fwd(name, x, g, w1, w3, w2, tm=512):
    t, d = x.shape
    nj, ff = N_DEV // 2, 2 * FF_PAD

    def body(x_ref, g_ref, w1_hbm, w3_hbm, w2_hbm, y_ref, a_ref, b_ref, w1_ref, w3_ref, w2_ref, sems):
        _load_ffn_weights((w1_hbm, w3_hbm, w2_hbm), (w1_ref, w3_ref, w2_ref), sems)
        xv = x_ref[...]
        h, _, _ = _rms_fwd(xv, g_ref[...])
        hb = h.astype(bf16)
        acc = jnp.zeros((tm, d), f32)
        for j in range(nj):
            a = _nn(hb, w1_ref[j])
            b = _nn(hb, w3_ref[j])
            a_ref[j] = a.astype(bf16)
            b_ref[j] = b.astype(bf16)
            u = (a * jax.nn.sigmoid(a) * b).astype(bf16)
            acc = acc + _nn(u, w2_ref[j])
        y_ref[...] = xv + 0.5 * acc

    row = pl.BlockSpec((tm, d), lambda i: (i, 0))
    mid = pl.BlockSpec((nj, tm, ff), lambda i: (0, i, 0))
    return pl.pallas_call(
        body, name=name, grid=(t // tm,),
        in_specs=[row, pl.BlockSpec((1, d), lambda i: (0, 0)), ANY_SPEC, ANY_SPEC, ANY_SPEC],
        out_specs=[row, mid, mid],
        out_shape=[jax.ShapeDtypeStruct((t, d), f32), jax.ShapeDtypeStruct((nj, t, ff), bf16),
                   jax.ShapeDtypeStruct((nj, t, ff), bf16)],
        scratch_shapes=_ffn_weight_scratch(nj, d, ff),
        compiler_params=_cp("arbitrary"),
    )(x, g, w1, w3, w2)


def _ffn_dx(name, dy, x, g, w1, w3, w2, a, b, tm=256):
    t, d = x.shape
    nj, ff = N_DEV // 2, 2 * FF_PAD

    def body(dy_ref, x_ref, g_ref, w1_hbm, w3_hbm, w2_hbm, a_ref, b_ref,
             dx_ref, dg_ref, hbt_ref, dyh_ref, ut_ref, da_ref, db_ref, w1_ref, w3_ref, w2_ref, sems):
        _load_ffn_weights((w1_hbm, w3_hbm, w2_hbm), (w1_ref, w3_ref, w2_ref), sems)
        gv = g_ref[...]
        h, xn, r = _rms_fwd(x_ref[...], gv)
        hbt_ref[...] = h.astype(bf16).T
        dyv = dy_ref[...]
        dyh = (0.5 * dyv).astype(bf16)
        dyh_ref[...] = dyh
        dh = jnp.zeros((tm, d), f32)
        dus = [_nt(dyh, w2_ref[j]) for j in range(nj)]
        for j in range(nj):
            av = a_ref[j].astype(f32)
            bv = b_ref[j].astype(f32)
            s = jax.nn.sigmoid(av)
            silu = av * s
            ut_ref[j] = (silu * bv).astype(bf16).T
            du = dus[j]
            dab = (du * bv * (s * (1.0 + av * (1.0 - s)))).astype(bf16)
            dbb = (du * silu).astype(bf16)
            da_ref[j] = dab
            db_ref[j] = dbb
            dh = dh + _nt(dab, w1_ref[j]) + _nt(dbb, w3_ref[j])
        dx, dg = _rms_bwd(dh, xn, r, gv)
        dx_ref[...] = dyv + dx

        @pl.when(pl.program_id(0) == 0)
        def _():
            dg_ref[...] = jnp.zeros_like(dg_ref)
        dg_ref[...] += dg

    row = pl.BlockSpec((tm, d), lambda i: (i, 0))
    vec = pl.BlockSpec((1, d), lambda i: (0, 0))
    mid = pl.BlockSpec((nj, tm, ff), lambda i: (0, i, 0))
    mid_shape = jax.ShapeDtypeStruct((nj, t, ff), bf16)
    return pl.pallas_call(
        body, name=name, grid=(t // tm,),
        in_specs=[row, row, vec, ANY_SPEC, ANY_SPEC, ANY_SPEC, mid, mid],
        out_specs=[row, vec, pl.BlockSpec((d, tm), lambda i: (0, i)), row,
                   pl.BlockSpec((nj, ff, tm), lambda i: (0, 0, i)), mid, mid],
        out_shape=[jax.ShapeDtypeStruct((t, d), f32), jax.ShapeDtypeStruct((1, d), f32),
                   jax.ShapeDtypeStruct((d, t), bf16), jax.ShapeDtypeStruct((t, d), bf16),
                   jax.ShapeDtypeStruct((nj, ff, t), bf16), mid_shape, mid_shape],
        scratch_shapes=_ffn_weight_scratch(nj, d, ff),
        compiler_params=_cp("arbitrary"),
    )(dy, x, g, w1, w3, w2, a, b)


def _ffn_dw(name, xt, ys, bufs, l, h, tt=2048):
    n = len(ys)
    t = ys[0].shape[-2]
    tt = min(tt, t)
    cut_cols = xt.ndim == 2

    def body(x_ref, *refs):
        y_refs, outs, accs = refs[:n], refs[2 * n:4 * n], refs[4 * n:]
        s = pl.program_id(1)
        xv = x_ref[0] if xt.ndim == 3 else x_ref[...]
        for k in range(n):
            prod = _nn(xv, y_refs[k][0] if ys[k].ndim == 3 else y_refs[k][...])

            @pl.when(s == 0)
            def _():
                accs[k][...] = prod

            @pl.when(s > 0)
            def _():
                accs[k][...] += prod

        @pl.when(s == pl.num_programs(1) - 1)
        def _():
            for k in range(n):
                total = accs[k][...]
                for e in range(2):
                    lo = e * FF_PAD
                    part = total[:, lo:lo + FF_SHARD] if cut_cols else total[lo:lo + FF_SHARD, :]
                    outs[k][e] = part
                    outs[n + k][e] = part.astype(bf16)

    x_spec = (pl.BlockSpec((1, xt.shape[1], tt), lambda p, s: (p, 0, s)) if xt.ndim == 3
              else pl.BlockSpec((xt.shape[0], tt), lambda p, s: (0, s)))
    y_specs = [pl.BlockSpec((1, tt, y.shape[2]), lambda p, s: (p, s, 0)) if y.ndim == 3
               else pl.BlockSpec((tt, y.shape[1]), lambda p, s: (s, 0)) for y in ys]
    dims = [b.shape[-2:] for b in bufs]
    outs = pl.pallas_call(
        body, name=name, grid=(N_DEV // 2, t // tt),
        in_specs=[x_spec] + y_specs + [ANY_SPEC] * n,
        out_specs=[pl.BlockSpec((2, None, None, k_, n_), lambda p, s: (p, l, h, 0, 0)) for k_, n_ in dims]
        + [pl.BlockSpec((2, k_, n_), lambda p, s: (p, 0, 0)) for k_, n_ in dims],
        out_shape=[jax.ShapeDtypeStruct(b.shape, b.dtype) for b in bufs]
        + [jax.ShapeDtypeStruct((N_DEV, k_, n_), bf16) for k_, n_ in dims],
        input_output_aliases={1 + n + k: k for k in range(n)},
        scratch_shapes=[pltpu.VMEM((xt.shape[-2], y.shape[-1]), f32) for y in ys],
        compiler_params=_cp("parallel", "arbitrary"),
    )(xt, *ys, *bufs)
    return outs[:n], outs[n:]


_LOG_GAMMA = [float(np.log1p(-np.float32(2.0) ** np.float32(-5.0 - h))) for h in range(RET_HEADS)]


def _ret_consts(h):
    lg = jnp.where(h == 0, _LOG_GAMMA[0], jnp.where(h == 1, _LOG_GAMMA[1],
                   jnp.where(h == 2, _LOG_GAMMA[2], _LOG_GAMMA[3]))).astype(f32)
    c = RET_CHUNK
    r = lax.broadcasted_iota(jnp.int32, (c, c), 0)
    cc = lax.broadcasted_iota(jnp.int32, (c, c), 1)
    decay = jnp.where(r >= cc, jnp.exp(lg * jnp.maximum((r - cc).astype(f32), 0.0)), 0.0)
    pos = lax.broadcasted_iota(jnp.int32, (c, 1), 0).astype(f32)
    kd = jnp.exp(lg * (c - 1.0 - pos))
    qd = jnp.exp(lg * (pos + 1.0))
    gc = jnp.exp(lg * c)
    return decay, kd, qd, gc


def _rope(x, cos, sin):
    return x * cos + pltpu.roll(x, HEAD_DIM // 2, 1) * sin


def _rope_t(g, cos, sin):
    return g * cos + pltpu.roll(g * sin, HEAD_DIM // 2, 1)


def _rope_tables(s):
    half = HEAD_DIM // 2
    inv = ROPE_BASE ** (-jnp.arange(half, dtype=f32) / half)
    ang = jnp.arange(s, dtype=f32)[:, None] * inv[None, :]
    cos, sin = jnp.cos(ang), jnp.sin(ang)
    return jnp.concatenate([cos, cos], axis=1), jnp.concatenate([-sin, sin], axis=1)


def _head_ln(o):
    mu = jnp.mean(o, axis=-1, keepdims=True)
    oc = o - mu
    rs = lax.rsqrt(jnp.mean(oc * oc, axis=-1, keepdims=True) + EPS)
    return oc * rs, rs


def _ret_fwd(proj, cos, sin, ret_g, nb, s):
    c = RET_CHUNK
    nc = s // c
    t = nb * s
    scale = HEAD_DIM ** -0.5

    def body(q_ref, k_ref, v_ref, gate_ref, cos_ref, sin_ref, g_ref, o_ref, rprev_ref, m_ref):
        decay, kd, qd, gc = _ret_consts(pl.program_id(0))
        gv = g_ref[...]

        def chunk(b, n, rv):
            rows = pl.ds(pl.multiple_of(b * s + n * c, c), c)
            pos = pl.ds(pl.multiple_of(n * c, c), c)
            cs, sn = cos_ref[pos, :], sin_ref[pos, :]
            q = _rope(q_ref[rows, :], cs, sn)
            k = _rope(k_ref[rows, :], cs, sn) * scale
            vb = v_ref[rows, :].astype(bf16)
            sc = _nt(q.astype(bf16), k.astype(bf16)) * decay
            rprev_ref[b, n] = rv
            o = _nn(sc.astype(bf16), vb) + _nn((q * qd).astype(bf16), rv.astype(bf16))
            o_ref[rows, :] = o
            y, _ = _head_ln(o)
            gate = gate_ref[rows, :]
            m_ref[rows, :] = y * gv * (gate * jax.nn.sigmoid(gate))
            return rv * gc + _tn((k * kd).astype(bf16), vb)

        def step(n, carry):
            return tuple(chunk(b, n, carry[b]) for b in range(nb))

        lax.fori_loop(0, nc, step, (jnp.zeros((HEAD_DIM, HEAD_DIM), f32),) * nb)

    def col(off):
        return pl.BlockSpec((t, HEAD_DIM), lambda h: (0, off + h))

    tab = pl.BlockSpec((s, HEAD_DIM), lambda h: (0, 0))
    return pl.pallas_call(
        body, name="ret_fwd", grid=(RET_HEADS,),
        in_specs=[col(0), col(4), col(8), col(12), tab, tab, pl.BlockSpec((1, HEAD_DIM), lambda h: (0, h))],
        out_specs=[col(0), pl.BlockSpec((nb, None, nc, HEAD_DIM, HEAD_DIM), lambda h: (0, h, 0, 0, 0)), col(0)],
        out_shape=[jax.ShapeDtypeStruct((t, RET_WIDTH), f32),
                   jax.ShapeDtypeStruct((nb, RET_HEADS, nc, HEAD_DIM, HEAD_DIM), f32),
                   jax.ShapeDtypeStruct((t, RET_WIDTH), f32)],
        compiler_params=_cp("parallel"),
    )(proj, proj, proj, proj, cos, sin, ret_g)


def _ret_bwd(dmerged, o_raw, rprev, proj, cos, sin, ret_g, nb, s):
    c = RET_CHUNK
    nc = s // c
    t = nb * s
    scale = HEAD_DIM ** -0.5

    def body(dm_ref, o_ref, rprev_ref, q_ref, k_ref, v_ref, gate_ref, cos_ref, sin_ref, g_ref,
             dq_ref, dk_ref, dv_ref, dgate_ref, dg_ref):
        decay, kd, qd, gc = _ret_consts(pl.program_id(0))
        gv = g_ref[...]

        def chunk(b, n, drn, dg):
            rows = pl.ds(pl.multiple_of(b * s + n * c, c), c)
            pos = pl.ds(pl.multiple_of(n * c, c), c)
            cs, sn = cos_ref[pos, :], sin_ref[pos, :]
            q = _rope(q_ref[rows, :], cs, sn)
            k = _rope(k_ref[rows, :], cs, sn) * scale
            qb, kb = q.astype(bf16), k.astype(bf16)
            vb = v_ref[rows, :].astype(bf16)
            sc = _nt(qb, kb) * decay
            y, rs = _head_ln(o_ref[rows, :])
            gate = gate_ref[rows, :]
            sg = jax.nn.sigmoid(gate)
            silu = gate * sg
            dm = dm_ref[rows, :]
            dgate_ref[rows, :] = dm * y * gv * (sg * (1.0 + gate * (1.0 - sg)))
            dyl = dm * gv * silu
            dg = dg + jnp.sum(dm * y * silu, axis=0, keepdims=True)
            do = rs * (dyl - jnp.mean(dyl, axis=-1, keepdims=True) - y * jnp.mean(dyl * y, axis=-1, keepdims=True))
            dob = do.astype(bf16)
            rv = rprev_ref[b, n]
            drb = drn.astype(bf16)
            ds = (_nt(dob, vb) * decay).astype(bf16)
            kdb = (k * kd).astype(bf16)
            qdb = (q * qd).astype(bf16)
            dq_r = _nn(ds, kb) + _nt(dob, rv.astype(bf16)) * qd
            dk_r = _tn(ds, qb) + _nt(vb, drb) * kd
            dv_ref[rows, :] = _tn(sc.astype(bf16), dob) + _nn(kdb, drb)
            dq_ref[rows, :] = _rope_t(dq_r, cs, sn)
            dk_ref[rows, :] = _rope_t(dk_r * scale, cs, sn)
            return drn * gc + _tn(qdb, dob), dg

        def step(i, carry):
            out = [chunk(b, nc - 1 - i, *carry[b]) for b in range(nb)]
            return tuple(out)

        zero = (jnp.zeros((HEAD_DIM, HEAD_DIM), f32), jnp.zeros((1, HEAD_DIM), f32))
        done = lax.fori_loop(0, nc, step, (zero,) * nb)
        dg_ref[...] = sum(dg for _, dg in done)

    def col(off):
        return pl.BlockSpec((t, HEAD_DIM), lambda h: (0, off + h))

    tab = pl.BlockSpec((s, HEAD_DIM), lambda h: (0, 0))
    gsp = pl.BlockSpec((1, HEAD_DIM), lambda h: (0, h))
    out_t = jax.ShapeDtypeStruct((t, RET_WIDTH), f32)
    return pl.pallas_call(
        body, name="ret_bwd", grid=(RET_HEADS,),
        in_specs=[col(0), col(0), pl.BlockSpec((nb, None, nc, HEAD_DIM, HEAD_DIM), lambda h: (0, h, 0, 0, 0)),
                  col(0), col(4), col(8), col(12), tab, tab, gsp],
        out_specs=[col(0), col(0), col(0), col(0), gsp],
        out_shape=[out_t, out_t, out_t, out_t, jax.ShapeDtypeStruct((1, RET_WIDTH), f32)],
        compiler_params=_cp("parallel"),
    )(dmerged, o_raw, rprev, proj, proj, proj, proj, cos, sin, ret_g)


def _neg_expm1(z):
    series = -(z * (1.0 + z * (0.5 + z * (1.0 / 6.0 + z * (1.0 / 24.0)))))
    return jnp.where(z > -0.01, series, 1.0 - jnp.exp(z))


def _lru_gates(xc, pa, pi, lam):
    r = jax.nn.sigmoid(pa)
    i = jax.nn.sigmoid(pi)
    log_a = -LRU_C * r * jax.nn.softplus(-lam)
    a = jnp.exp(log_a)
    bx = jnp.sqrt(_neg_expm1(2.0 * log_a)) * i * xc
    return a, bx


def _scan_rows(a, b, row, up):
    sub = row[:SUBLANES] & (SUBLANES - 1)
    groups = list(range(a.shape[0] // SUBLANES))
    out = [None] * len(groups)
    edge = slice(0, 1) if up else slice(SUBLANES - 1, SUBLANES)
    carry = jnp.zeros((1, a.shape[1]), f32)
    for g in (reversed(groups) if up else groups):
        rows = slice(g * SUBLANES, (g + 1) * SUBLANES)
        xa, xb = a[rows], b[rows]
        d = 1
        while d < SUBLANES:
            keep = (sub < SUBLANES - d) if up else (sub >= d)
            shift = SUBLANES - d if up else d
            xb = xa * jnp.where(keep, pltpu.roll(xb, shift, 0), 0.0) + xb
            xa = xa * jnp.where(keep, pltpu.roll(xa, shift, 0), 1.0)
            d *= 2
        out[g] = xb + xa * carry
        carry = out[g][edge]
    return jnp.concatenate(out, axis=0)


def _scan_fwd(a, b, row):
    return _scan_rows(a, b, row, False)


def _scan_bwd(c, b, row):
    return _scan_rows(c, b, row, True)


def _conv_fwd(x, cw, cb, row):
    return (cb + cw[3:4] * x + cw[2:3] * _shift_dn(x, 1, row) + cw[1:2] * _shift_dn(x, 2, row)
            + cw[0:1] * _shift_dn(x, 3, row))


def _lru_specs(s, order):
    def im(f):
        return (lambda b, g: f(b, g)) if order == "bg" else (lambda g, b: f(b, g))
    seq = lambda off: pl.BlockSpec((s, 128), im(lambda b, g: (b, off + g)))
    vec = pl.BlockSpec((1, 128), im(lambda b, g: (0, g)))
    cw = pl.BlockSpec((4, 128), im(lambda b, g: (0, g)))
    mat = pl.BlockSpec((1, 128, 128), im(lambda b, g: (g, 0, 0)))
    return seq, vec, cw, mat


def _lru_fwd(proj, conv_w, conv_b, w_a, b_a, w_i, b_i, lam, nb, s):
    def body(x_ref, gt_ref, cw_ref, cb_ref, wa_ref, ba_ref, wi_ref, bi_ref, lam_ref, out_ref, h_ref):
        row = lax.broadcasted_iota(jnp.int32, (s, 128), 0)
        xc = _conv_fwd(x_ref[...], cw_ref[...], cb_ref[...], row)
        xcb = xc.astype(bf16)
        pa = _nn(xcb, wa_ref[0].astype(bf16)) + ba_ref[...]
        pi = _nn(xcb, wi_ref[0].astype(bf16)) + bi_ref[...]
        a, bx = _lru_gates(xc, pa, pi, lam_ref[...])
        h = _scan_fwd(a, bx, row)
        h_ref[...] = h
        out_ref[...] = h * jax.nn.gelu(gt_ref[...])

    seq, vec, cw, mat = _lru_specs(s, "bg")
    out = jax.ShapeDtypeStruct((nb * s, LRU_WIDTH), f32)
    return pl.pallas_call(
        body, name="lru_fwd", grid=(nb, LRU_BLOCKS),
        in_specs=[seq(16), seq(20), cw, vec, mat, vec, mat, vec, vec],
        out_specs=[seq(0), seq(0)], out_shape=[out, out],
        compiler_params=_cp("parallel", "parallel"),
    )(proj, proj, conv_w, conv_b, w_a, b_a, w_i, b_i, lam)


def _lru_bwd(dmerged, states, proj, conv_w, conv_b, w_a, b_a, w_i, b_i, lam, nb, s):
    def body(dout_ref, h_ref, x_ref, gt_ref, cw_ref, cb_ref, wa_ref, ba_ref, wi_ref, bi_ref, lam_ref,
             dx_ref, dgt_ref, dcw_ref, dcb_ref, dwa_ref, dba_ref, dwi_ref, dbi_ref, dlam_ref):
        row = lax.broadcasted_iota(jnp.int32, (s, 128), 0)
        x = x_ref[...]
        cwv = cw_ref[...]
        xc = _conv_fwd(x, cwv, cb_ref[...], row)
        xcb = xc.astype(bf16)
        wab, wib = wa_ref[0].astype(bf16), wi_ref[0].astype(bf16)
        pa = _nn(xcb, wab) + ba_ref[...]
        pi = _nn(xcb, wib) + bi_ref[...]
        (a, _), gates_vjp = jax.vjp(_lru_gates, xc, pa, pi, lam_ref[...])
        h = h_ref[...]
        ge, gelu_vjp = jax.vjp(jax.nn.gelu, gt_ref[...])
        dout = dout_ref[...]
        dgt_ref[...] = gelu_vjp(dout * h)[0]
        adj = _scan_bwd(_shift_up(a, 1, row), dout * ge, row)
        dxc, dpa, dpi, dlam = gates_vjp((adj * _shift_dn(h, 1, row), adj))
        dpab, dpib = dpa.astype(bf16), dpi.astype(bf16)
        dxc = dxc + _nt(dpab, wab) + _nt(dpib, wib)
        dx_ref[...] = (cwv[3:4] * dxc + cwv[2:3] * _shift_up(dxc, 1, row) + cwv[1:2] * _shift_up(dxc, 2, row)
                       + cwv[0:1] * _shift_up(dxc, 3, row))

        @pl.when(pl.program_id(1) == 0)
        def _():
            for r in (dcw_ref, dcb_ref, dwa_ref, dba_ref, dwi_ref, dbi_ref, dlam_ref):
                r[...] = jnp.zeros_like(r)
        rsum = lambda v: jnp.sum(v, axis=0, keepdims=True)
        dcw_ref[...] += jnp.concatenate([rsum(dxc * _shift_dn(x, 3, row)), rsum(dxc * _shift_dn(x, 2, row)),
                                         rsum(dxc * _shift_dn(x, 1, row)), rsum(dxc * x)], axis=0)
        dcb_ref[...] += rsum(dxc)
        dwa_ref[0] += _tn(xcb, dpab)
        dwi_ref[0] += _tn(xcb, dpib)
        dba_ref[...] += rsum(dpa)
        dbi_ref[...] += rsum(dpi)
        dlam_ref[...] += dlam

    seq, vec, cw, mat = _lru_specs(s, "gb")
    t = nb * s
    vshape = jax.ShapeDtypeStruct((1, LRU_WIDTH), f32)
    mshape = jax.ShapeDtypeStruct((LRU_BLOCKS, 128, 128), f32)
    return pl.pallas_call(
        body, name="lru_bwd", grid=(LRU_BLOCKS, nb),
        in_specs=[seq(4), seq(0), seq(16), seq(20), cw, vec, mat, vec, mat, vec, vec],
        out_specs=[seq(0), seq(0), cw, vec, mat, vec, mat, vec, vec],
        out_shape=[jax.ShapeDtypeStruct((t, LRU_WIDTH), f32), jax.ShapeDtypeStruct((t, LRU_WIDTH), f32),
                   jax.ShapeDtypeStruct((4, LRU_WIDTH), f32), vshape, mshape, vshape, mshape, vshape, vshape],
        compiler_params=_cp("parallel", "arbitrary"),
    )(dmerged, states, proj, proj, conv_w, conv_b, w_a, b_a, w_i, b_i, lam)


def _s5_disc(lr, li, ldt, bre, bim):
    dt = jnp.exp(ldt)
    mag = jnp.exp(lr * dt)
    lbr = mag * jnp.cos(li * dt)
    lbi = mag * jnp.sin(li * dt)
    den = lr * lr + li * li
    nr = lbr - 1.0
    fr = (nr * lr + lbi * li) / den
    fi = (lbi * lr - nr * li) / den
    bbr = fr[:, None, :] * bre - fi[:, None, :] * bim
    bbi = fr[:, None, :] * bim + fi[:, None, :] * bre
    return lbr, lbi, bbr, bbi


def _s5_prep(lr, li, ldt, bre, bim):
    def body(lr_ref, li_ref, ldt_ref, bre_ref, bim_ref, o1, o2, o3, o4):
        o1[...], o2[...], o3[...], o4[...] = _s5_disc(lr_ref[...], li_ref[...], ldt_ref[...], bre_ref[...], bim_ref[...])

    return pl.pallas_call(
        body, name="s5_prep", in_specs=[VMEM_SPEC] * 5, out_specs=[VMEM_SPEC] * 4,
        out_shape=[jax.ShapeDtypeStruct(lr.shape, f32), jax.ShapeDtypeStruct(lr.shape, f32),
                   jax.ShapeDtypeStruct(bre.shape, f32), jax.ShapeDtypeStruct(bre.shape, f32)],
    )(lr, li, ldt, bre, bim)


def _s5_prep_bwd(lr, li, ldt, bre, bim, cts):
    def body(lr_ref, li_ref, ldt_ref, bre_ref, bim_ref, g1, g2, g3, g4, o1, o2, o3, o4, o5):
        _, vjp = jax.vjp(_s5_disc, lr_ref[...], li_ref[...], ldt_ref[...], bre_ref[...], bim_ref[...])
        o1[...], o2[...], o3[...], o4[...], o5[...] = vjp((g1[...], g2[...], g3[...], g4[...]))

    return pl.pallas_call(
        body, name="s5_prep_bwd", in_specs=[VMEM_SPEC] * 9, out_specs=[VMEM_SPEC] * 5,
        out_shape=[jax.ShapeDtypeStruct(v.shape, f32) for v in (lr, li, ldt, bre, bim)],
    )(lr, li, ldt, bre, bim, *cts)


def _cmul(ar, ai, br, bi):
    return ar * br - ai * bi, ar * bi + ai * br


def _s5_pow_table(lr, li, n, row, up):
    ar = jnp.broadcast_to(lr, (n, lr.shape[1]))
    ai = jnp.broadcast_to(li, (n, li.shape[1]))
    shift = _shift_up if up else _shift_dn
    d = 1
    while d < n:
        ar, ai = _cmul(ar, ai, shift(ar, d, row, 1.0), shift(ai, d, row, 0.0))
        d *= 2
    return ar, ai


def _s5_step_factors(lr, li, row, up):
    sub = row & (SUBLANES - 1)
    out, pr, pi, d = [], lr, li, 1
    while d < SUBLANES:
        keep = (sub < SUBLANES - d) if up else (sub >= d)
        out.append((jnp.where(keep, pr, 0.0), jnp.where(keep, pi, 0.0)))
        pr, pi = _cmul(pr, pi, pr, pi)
        d *= 2
    return out


def _s5_scan(br, bi, steps, tab_r, tab_i, cr, ci, up):
    groups = list(range(br.shape[0] // SUBLANES))
    out_r, out_i = [None] * len(groups), [None] * len(groups)
    edge = slice(0, 1) if up else slice(SUBLANES - 1, SUBLANES)
    for g in (reversed(groups) if up else groups):
        rows = slice(g * SUBLANES, (g + 1) * SUBLANES)
        xr, xi = br[rows], bi[rows]
        for k, (mr, mi) in enumerate(steps):
            shift = SUBLANES - (1 << k) if up else 1 << k
            tr, ti = _cmul(mr, mi, pltpu.roll(xr, shift, 0), pltpu.roll(xi, shift, 0))
            xr, xi = xr + tr, xi + ti
        tr, ti = _cmul(tab_r, tab_i, cr, ci)
        hr, hi = xr + tr, xi + ti
        out_r[g], out_i[g] = hr, hi
        cr, ci = hr[edge], hi[edge]
    return jnp.concatenate(out_r, axis=0), jnp.concatenate(out_i, axis=0)


def _s5_specs(t, nb, nc):
    seq = pl.BlockSpec((t, 128), lambda k: (0, k))
    lvec = pl.BlockSpec((1, S5_BLOCK_STATES), lambda k: (0, k))
    dvec = pl.BlockSpec((1, 128), lambda k: (0, k))
    wmat = pl.BlockSpec((1, 128, S5_BLOCK_STATES), lambda k: (k, 0, 0))
    h0 = pl.BlockSpec((nb, None, nc, 2, S5_BLOCK_STATES), lambda k: (0, k, 0, 0, 0))
    states = pl.BlockSpec((t, S5_BLOCK_STATES), lambda k: (0, k))
    return seq, lvec, dvec, wmat, h0, states


def _s5_fwd(u, lbr, lbi, wbr, wbi, wcr, wci, dskip, nb, s):
    ln = min(S5_CHUNK, s)
    nc = s // ln

    def body(u_ref, lr_ref, li_ref, wbr_ref, wbi_ref, wcr_ref, wci_ref, d_ref, yg_ref, y_ref, h0_ref, hr_ref, hi_ref):
        row = lax.broadcasted_iota(jnp.int32, (ln, S5_BLOCK_STATES), 0)
        lr, li = lr_ref[...], li_ref[...]
        pr, pi = _s5_pow_table(lr, li, SUBLANES, row[:SUBLANES], False)
        steps = _s5_step_factors(lr, li, row[:SUBLANES], False)
        dv = d_ref[...]

        def chunk(b, n, h0r, h0i):
            st = pl.multiple_of(b * s + n * ln, ln)
            uc = u_ref[pl.ds(st, ln), :]
            ub = uc.astype(bf16)
            hr, hi = _s5_scan(_nn(ub, wbr_ref[0]), _nn(ub, wbi_ref[0]), steps, pr, pi, h0r, h0i, False)
            h0_ref[b, n, 0:1, :] = h0r
            h0_ref[b, n, 1:2, :] = h0i
            hrb, hib = hr.astype(bf16), hi.astype(bf16)
            hr_ref[pl.ds(st, ln), :] = hrb
            hi_ref[pl.ds(st, ln), :] = hib
            y = _nt(hrb, wcr_ref[0]) - _nt(hib, wci_ref[0]) + dv * uc
            y_ref[pl.ds(st, ln), :] = y
            yg_ref[pl.ds(st, ln), :] = jax.nn.gelu(y).astype(bf16)
            return hr[ln - 1:ln, :], hi[ln - 1:ln, :]

        def step(n, carry):
            return tuple(chunk(b, n, *carry[b]) for b in range(nb))

        z = jnp.zeros((1, S5_BLOCK_STATES), f32)
        lax.fori_loop(0, nc, step, ((z, z),) * nb)

    t = nb * s
    seq, lvec, dvec, wmat, h0, states = _s5_specs(t, nb, nc)
    return pl.pallas_call(
        body, name="s5_fwd", grid=(S5_BLOCKS,),
        in_specs=[seq, lvec, lvec, wmat, wmat, wmat, wmat, dvec],
        out_specs=[seq, seq, h0, states, states],
        out_shape=[jax.ShapeDtypeStruct((t, D_MODEL), bf16), jax.ShapeDtypeStruct((t, D_MODEL), f32),
                   jax.ShapeDtypeStruct((nb, S5_BLOCKS, nc, 2, S5_BLOCK_STATES), f32),
                   jax.ShapeDtypeStruct((t, S5_BLOCKS * S5_BLOCK_STATES), bf16),
                   jax.ShapeDtypeStruct((t, S5_BLOCKS * S5_BLOCK_STATES), bf16)],
        compiler_params=_cp("parallel"),
    )(u, lbr, lbi, wbr, wbi, wcr, wci, dskip)


def _s5_bwd(dyg, y, u, h0, hrs, his, lbr, lbi, wbr, wbi, wcr, wci, dskip, nb, s):
    ln = min(S5_CHUNK, s)
    nc = s // ln

    def body(dyg_ref, y_ref, u_ref, h0_ref, hr_ref, hi_ref, lr_ref, li_ref, wbr_ref, wbi_ref, wcr_ref, wci_ref, d_ref,
             du_ref, dlr_ref, dli_ref, dwbr_ref, dwbi_ref, dwcr_ref, dwci_ref, dd_ref):
        for r in (dlr_ref, dli_ref, dwbr_ref, dwbi_ref, dwcr_ref, dwci_ref, dd_ref):
            r[...] = jnp.zeros_like(r)
        row = lax.broadcasted_iota(jnp.int32, (ln, S5_BLOCK_STATES), 0)
        lr, li = lr_ref[...], li_ref[...]
        qr, qi = _s5_pow_table(lr, -li, SUBLANES, row[:SUBLANES], True)
        steps_up = _s5_step_factors(lr, -li, row[:SUBLANES], True)
        dv = d_ref[...]
        rsum = lambda v: jnp.sum(v, axis=0, keepdims=True)

        def chunk(b, n, gnr, gni):
            st = pl.multiple_of(b * s + n * ln, ln)
            uc = u_ref[pl.ds(st, ln), :]
            ub = uc.astype(bf16)
            h0v = h0_ref[b, n]
            h0r, h0i = h0v[0:1], h0v[1:2]
            hrb, hib = hr_ref[pl.ds(st, ln), :], hi_ref[pl.ds(st, ln), :]
            hr, hi = hrb.astype(f32), hib.astype(f32)
            dy = jax.vjp(jax.nn.gelu, y_ref[pl.ds(st, ln), :])[1](dyg_ref[pl.ds(st, ln), :])[0]
            dyb = dy.astype(bf16)
            dd_ref[...] += rsum(dy * uc)
            gr, gi = _s5_scan(_nn(dyb, wcr_ref[0]), -_nn(dyb, wci_ref[0]), steps_up, qr, qi, gnr, gni, True)
            hpr = jnp.where(row >= 1, pltpu.roll(hr, 1, 0), h0r)
            hpi = jnp.where(row >= 1, pltpu.roll(hi, 1, 0), h0i)
            dlr_ref[...] += rsum(gr * hpr + gi * hpi)
            dli_ref[...] += rsum(gi * hpr - gr * hpi)
            grb, gib = gr.astype(bf16), gi.astype(bf16)
            dwbr_ref[0] += _tn(ub, grb)
            dwbi_ref[0] += _tn(ub, gib)
            dwcr_ref[0] += _tn(dyb, hrb)
            dwci_ref[0] -= _tn(dyb, hib)
            du_ref[pl.ds(st, ln), :] = _nt(grb, wbr_ref[0]) + _nt(gib, wbi_ref[0]) + dv * dy
            return gr[0:1, :], gi[0:1, :]

        def step(i, carry):
            return tuple(chunk(b, nc - 1 - i, *carry[b]) for b in range(nb))

        z = jnp.zeros((1, S5_BLOCK_STATES), f32)
        lax.fori_loop(0, nc, step, ((z, z),) * nb)

    t = nb * s
    seq, lvec, dvec, wmat, h0s, states = _s5_specs(t, nb, nc)
    lshape = jax.ShapeDtypeStruct((1, S5_BLOCKS * S5_BLOCK_STATES), f32)
    wshape = jax.ShapeDtypeStruct((S5_BLOCKS, 128, S5_BLOCK_STATES), f32)
    return pl.pallas_call(
        body, name="s5_bwd", grid=(S5_BLOCKS,),
        in_specs=[seq, seq, seq, h0s, states, states, lvec, lvec, wmat, wmat, wmat, wmat, dvec],
        out_specs=[seq, lvec, lvec, wmat, wmat, wmat, wmat, dvec],
        out_shape=[jax.ShapeDtypeStruct((t, D_MODEL), f32), lshape, lshape, wshape, wshape, wshape, wshape,
                   jax.ShapeDtypeStruct((1, D_MODEL), f32)],
        compiler_params=_cp("parallel"),
    )(dyg, y, u, h0, hrs, his, lbr, lbi, wbr, wbi, wcr, wci, dskip)


def _blockdiag(w):
    w4 = w.reshape(S5_BLOCKS, 8, S5_GROUP, S5_STATE)
    same_group = jnp.eye(8, dtype=bool)[None, :, None, :, None]
    return jnp.where(same_group, w4[:, :, :, None, :], 0.0).reshape(S5_BLOCKS, 128, S5_BLOCK_STATES)


def _blockdiag_t(dw):
    d5 = dw.reshape(S5_BLOCKS, 8, S5_GROUP, 8, S5_STATE)
    diag = jnp.diagonal(d5, axis1=1, axis2=3)
    return jnp.moveaxis(diag, 3, 1).reshape(S5_GROUPS, S5_GROUP, S5_STATE)


def _glu_fwd(ygb, wa, wb, x, tm=512, tn=1024):
    t, d = x.shape

    def body(y_ref, wa_ref, wb_ref, x_ref, o_ref, p_ref, q_ref):
        p = _nn(y_ref[...], wa_ref[...])
        q = _nn(y_ref[...], wb_ref[...])
        p_ref[...] = p
        q_ref[...] = q
        o_ref[...] = x_ref[...] + p * jax.nn.sigmoid(q)

    tile = pl.BlockSpec((tm, tn), lambda i, j: (i, j))
    wsp = pl.BlockSpec((d, tn), lambda i, j: (0, j))
    out = jax.ShapeDtypeStruct((t, d), f32)
    return pl.pallas_call(
        body, name="glu_fwd", grid=(t // tm, d // tn),
        in_specs=[pl.BlockSpec((tm, d), lambda i, j: (i, 0)), wsp, wsp, tile],
        out_specs=[tile, tile, tile], out_shape=[out, out, out],
        compiler_params=_cp("parallel", "parallel"),
    )(ygb, wa, wb, x)


def _place():
    x, y, c = lax.axis_index("x"), lax.axis_index("y"), lax.axis_index("c")
    return x, y, c, [(1 - x, y), (x, 1 - y), (1 - x, 1 - y)]


def _all_gather(name, arrays):
    n = len(arrays)

    def body(*refs):
        ins, outs = refs[:n], refs[n:2 * n]
        send_sems, recv_sems, local_sems = refs[2 * n:]
        x, y, c, chips = _place()
        me, sib = (x, y, c), (x, y, 1 - c)

        def copy(i, k, block, to, src=None):
            dst = outs[i].at[4 * block[0] + 2 * block[1] + block[2]]
            return pltpu.make_async_remote_copy(
                src_ref=dst if src is None else src, dst_ref=dst,
                send_sem=send_sems.at[i * 7 + k], recv_sem=recv_sems.at[i * 7 + k],
                device_id=to, device_id_type=MESH)

        mine = [pltpu.make_async_copy(ins[i], outs[i].at[4 * x + 2 * y + c], local_sems.at[i]) for i in range(n)]
        for m in mine:
            m.start()
        first = []
        for i in range(n):
            first.append(copy(i, 0, me, sib, src=ins[i]))
            first += [copy(i, 1 + j, me, (*chip, c), src=ins[i]) for j, chip in enumerate(chips)]
        for cp in first:
            cp.start()
        passed = []
        for j, chip in enumerate(chips):
            for i in range(n):
                copy(i, 1 + j, (*chip, c), me).wait_recv()
                fwd = copy(i, 4 + j, (*chip, c), sib)
                fwd.start()
                passed.append(fwd)
        for i in range(n):
            copy(i, 0, sib, me).wait_recv()
        for j, chip in enumerate(chips):
            for i in range(n):
                copy(i, 4 + j, (*chip, 1 - c), me).wait_recv()
        for cp in first + passed:
            cp.wait_send()
        for m in mine:
            m.wait()

    return pl.pallas_call(
        body, name=name,
        in_specs=[ANY_SPEC] * n, out_specs=[ANY_SPEC] * n,
        out_shape=[jax.ShapeDtypeStruct((N_DEV,) + a.shape, a.dtype) for a in arrays],
        scratch_shapes=[pltpu.SemaphoreType.DMA((7 * n,)), pltpu.SemaphoreType.DMA((7 * n,)),
                        pltpu.SemaphoreType.DMA((n,))],
    )(*arrays)


def _tie(name, x, deps):
    def body(*refs):
        pass

    return pl.pallas_call(
        body, name=name, in_specs=[ANY_SPEC] * (1 + len(deps)), out_specs=ANY_SPEC,
        out_shape=jax.ShapeDtypeStruct(x.shape, x.dtype), input_output_aliases={0: 0},
    )(x, *deps)


def _xchg_copies(kind, srcs, lands, suffixes, send_sems, recv_sems):
    x, y, c, _ = _place()
    copies = []
    for i, (src, land, sfx) in enumerate(zip(srcs, lands, suffixes)):
        for k in range(N_DEV - 1):
            r = k + 1
            peer = (1 - x if r & 4 else x, 1 - y if r & 2 else y, 1 - c if r & 1 else c)
            if kind == "gather":
                s_ref, d_ref = src, land.at[(4 * x + 2 * y + c,) + sfx]
            else:
                s_ref, d_ref = src.at[4 * peer[0] + 2 * peer[1] + peer[2]], land.at[(k,) + sfx]
            copies.append(pltpu.make_async_remote_copy(
                src_ref=s_ref, dst_ref=d_ref, send_sem=send_sems.at[i * 7 + k], recv_sem=recv_sems.at[i * 7 + k],
                device_id=peer, device_id_type=MESH))
    return copies


def _xchg_start(name, kind, srcs, lands, suffixes=None):
    n = len(srcs)
    suffixes = suffixes or [()] * n

    def body(*refs):
        src, land = refs[:n], refs[n:2 * n]
        send_sems, recv_sems, token = refs[2 * n], refs[2 * n + 1], refs[-1]
        for cp in _xchg_copies(kind, src, land, suffixes, send_sems, recv_sems):
            cp.start()
        token[...] = jnp.zeros_like(token)

    arrays = list(srcs) + list(lands)
    outs = pl.pallas_call(
        body, name=name,
        out_shape=(pltpu.SemaphoreType.DMA((7 * n,)), pltpu.SemaphoreType.DMA((7 * n,)),
                   *[pltpu.HBM(a.shape, a.dtype) for a in arrays], jax.ShapeDtypeStruct((8, 128), f32)),
        in_specs=[HBM_SPEC] * (2 * n),
        out_specs=(SEM_SPEC, SEM_SPEC, *[HBM_SPEC] * (2 * n), VMEM_SPEC),
        input_output_aliases={i: 2 + i for i in range(2 * n)},
        compiler_params=pltpu.CompilerParams(has_side_effects=SIDE_EFFECT),
    )(*[pltpu.with_memory_space_constraint(a, pltpu.HBM) for a in arrays])
    return dict(kind=kind, n=n, suffixes=suffixes, send=outs[0], recv=outs[1], srcs=list(outs[2:2 + n]),
                lands=list(outs[2 + n:2 + 2 * n]), token=outs[-1])


def _xchg_wait(name, h, after, lands=None):
    n = h["n"]
    lands = h["lands"] if lands is None else lands

    def body(*refs):
        src, land = refs[:n], refs[n:2 * n]
        for cp in _xchg_copies(h["kind"], src, land, h["suffixes"], refs[2 * n], refs[2 * n + 1]):
            cp.wait_send()
            cp.wait_recv()

    arrays = h["srcs"] + list(lands)
    outs = pl.pallas_call(
        body, name=name,
        out_shape=tuple(pltpu.HBM(a.shape, a.dtype) for a in arrays),
        in_specs=[HBM_SPEC] * (2 * n) + [SEM_SPEC, SEM_SPEC] + [ANY_SPEC] * len(after),
        out_specs=tuple([HBM_SPEC] * (2 * n)),
        input_output_aliases={i: i for i in range(2 * n)},
        compiler_params=pltpu.CompilerParams(has_side_effects=SIDE_EFFECT),
    )(*arrays, h["send"], h["recv"], *after)
    return list(outs[n:])


def _rows(a):
    return a.reshape(-1, a.shape[-1])


def _row_tile(r):
    for tm in (512, 256, 128, 64, 32, 16, 8):
        if r % tm == 0:
            return tm
    return r


def _sum8(name, gathered):
    _, r, n = gathered.shape
    tm = _row_tile(r)

    def body(g_ref, o_ref):
        acc = g_ref[0]
        for k in range(1, N_DEV):
            acc = acc + g_ref[k]
        o_ref[...] = acc

    return pl.pallas_call(
        body, name=name, grid=(r // tm,),
        in_specs=[pl.BlockSpec((N_DEV, tm, n), lambda i: (0, i, 0))],
        out_specs=pl.BlockSpec((tm, n), lambda i: (i, 0)),
        out_shape=jax.ShapeDtypeStruct((r, n), f32),
        compiler_params=_cp("parallel"),
    )(gathered)


def _adamw(name, w, m, v, own, landed=None, slot=None):
    shape = w.shape
    w2, m2, v2 = _rows(w), _rows(m), _rows(v)
    r, n = w2.shape
    tm = _row_tile(r)
    c1 = 1.0 - ADAM_B1 ** ADAM_STEP
    c2 = 1.0 - ADAM_B2 ** ADAM_STEP
    extra = [] if landed is None else [landed.reshape(landed.shape[0], r, n)]
    row = pl.BlockSpec((tm, n), lambda i, *_: (i, 0))
    if slot is None:
        o2, own_spec, scalars = _rows(own), row, []
    else:
        dev, kind = slot
        scalars = [dev.reshape(1).astype(jnp.int32)]
        if kind == "lead":
            o2, own_spec = own.reshape(N_DEV, r, n), pl.BlockSpec((None, tm, n), lambda i, d: (d[0], i, 0))
        elif kind == "rows":
            o2, own_spec = own, pl.BlockSpec((tm, n), lambda i, d: (d[0] * (r // tm) + i, 0))
        else:
            o2, own_spec = own, pl.BlockSpec((tm, n), lambda i, d: (i, d[0]))

    def body(*refs):
        w_ref, m_ref, v_ref, o_ref = refs[len(scalars):len(scalars) + 4]
        refs = refs[len(scalars) + 4:]
        g = o_ref[...]
        if extra:
            for k in range(extra[0].shape[0]):
                g = g + refs[0][k].astype(f32)
        g_ref, d_ref, mn_ref, vn_ref = refs[len(extra):]
        mn = ADAM_B1 * m_ref[...] + (1.0 - ADAM_B1) * g
        vn = ADAM_B2 * v_ref[...] + (1.0 - ADAM_B2) * (g * g)
        g_ref[...] = g
        d_ref[...] = -ADAM_LR * ((mn / c1) / (jnp.sqrt(vn / c2) + ADAM_EPS) + ADAM_WD * w_ref[...])
        mn_ref[...] = mn
        vn_ref[...] = vn

    outs = pl.pallas_call(
        body, name=name,
        grid_spec=pltpu.PrefetchScalarGridSpec(
            num_scalar_prefetch=len(scalars), grid=(r // tm,),
            in_specs=[row] * 3 + [own_spec] + [pl.BlockSpec((e.shape[0], tm, n), lambda i, *_: (0, i, 0)) for e in extra],
            out_specs=[row] * 4),
        out_shape=[jax.ShapeDtypeStruct((r, n), f32)] * 4,
        compiler_params=_cp("parallel"),
    )(*scalars, w2, m2, v2, o2, *extra)
    return [o.reshape(shape) for o in outs]


def _pack(arrays):
    flat = jnp.concatenate([a.reshape(-1).astype(f32) for a in arrays])
    pad = (-flat.shape[0]) % (128 * (512 if flat.shape[0] > 128 * 512 else 8))
    return jnp.pad(flat, (0, pad)).reshape(-1, 128)


def _unpack(packed, shapes):
    flat = packed.reshape(-1)
    out, off = [], 0
    for s in shapes:
        n = math.prod(s)
        out.append(flat[off:off + n].reshape(s))
        off += n
    return out


def _local_step(x, target, w, weights_of, send, last_small, on_loss, nb, s):
    cos, sin = _rope_tables(s)
    g = {}
    ffn_saved = {}
    ffn_bufs = [lax.empty((N_DEV, 2, 2) + shp, f32)
                for shp in ((D_MODEL, FF_SHARD), (D_MODEL, FF_SHARD), (FF_SHARD, D_MODEL))]

    def ffn(xin, l, h, wts):
        y, a, b = _ffn_fwd(f"ffn_fwd_{l}{h}", xin, w["ffn_g"][l][h], *wts)
        ffn_saved[(l, h)] = (xin, a, b, wts)
        return y

    def ffn_back(dy, l, h):
        xin, a, b, wts = ffn_saved[(l, h)]
        dx, dg, hb, dyh, u, da, db = _ffn_dx(f"ffn_dx_{l}{h}", dy, xin, w["ffn_g"][l][h], *wts, a, b)
        g[f"ffn_g_{l}{h}"] = dg
        if (l, h) != (0, 0):
            ffn_bufs[:2], (h1, h3) = _ffn_dw(f"ffn_dw_{l}{h}_w13", hb, [da, db], ffn_bufs[:2], l, h)
            ffn_bufs[2:], (h2,) = _ffn_dw(f"ffn_dw_{l}{h}_w2", u, [dyh], ffn_bufs[2:], l, h)
            return send(f"ffn_{l}{h}", {"ffn_w1": h1, "ffn_w3": h3, "ffn_w2": h2}, dx)
        hb = last_small(g, hb)
        ffn_bufs[:1], (half,) = _ffn_dw("ffn_dw_00_w1", hb, [da], ffn_bufs[:1], l, h)
        hb = send("ffn_00_w1", {"ffn_w1": half}, hb)
        ffn_bufs[1:2], (half,) = _ffn_dw("ffn_dw_00_w3", hb, [db], ffn_bufs[1:2], l, h)
        u = send("ffn_00_w3", {"ffn_w3": half}, u)
        ffn_bufs[2:], (half,) = _ffn_dw("ffn_dw_00_w2", u, [dyh], ffn_bufs[2:], l, h)
        return send("ffn_00_w2", {"ffn_w2": half}, dx)

    def slots(t):
        return t.reshape(N_DEV, D_MODEL // N_DEV, D_MODEL)

    x1 = ffn(x, 0, 0, weights_of(0, [])["ffn"])
    wg = weights_of(1, [x1])
    w_in, w_out = wg["w_in"], wg["w_out"]
    _, h0b = _norm_fwd("mix_norm_0", x1, w["mix_g"][0])
    proj = _mm("in_proj", h0b, w_in, "nn", tn=1536)[0]
    o_raw, rprev, mret = _ret_fwd(proj, cos, sin, w["ret_g"], nb, s)
    lru, lru_h = _lru_fwd(proj, w["conv_w"], w["conv_b"], w["lru_w_a"], w["lru_b_a"], w["lru_w_i"], w["lru_b_i"], w["lru_lam"], nb, s)
    merged = _ew("merge", lambda a, b: (jnp.concatenate([a, b], axis=1),), [mret, lru], [(D_MODEL, bf16)])[0]
    x2 = _mm("out_proj", merged, w_out, "nn", extras=[x1], epilogue=lambda acc, r: (acc + r,))[0]
    x3 = ffn(x2, 0, 1, weights_of(2, [x2])["ffn"])
    x4 = ffn(x3, 1, 0, weights_of(3, [x3])["ffn"])
    u, _ = _norm_fwd("mix_norm_1", x4, w["mix_g"][1])
    lbr, lbi, bbr, bbi = _s5_prep(w["s5_lr"], w["s5_li"], w["s5_ldt"], w["s5_bre"], w["s5_bim"])
    lbr_f, lbi_f = lbr.reshape(1, -1), lbi.reshape(1, -1)
    wbr, wbi = _blockdiag(bbr).astype(bf16), _blockdiag(bbi).astype(bf16)
    wcr, wci = _blockdiag(w["s5_cre"]).astype(bf16), _blockdiag(w["s5_cim"]).astype(bf16)
    ygb, ypre, h0s, hrs, his = _s5_fwd(u, lbr_f, lbi_f, wbr, wbi, wcr, wci, w["s5_d"], nb, s)
    wg = weights_of(4, [ygb])
    glu_a, glu_b = wg["glu_a"], wg["glu_b"]
    x5, gp, gq = _glu_fwd(ygb, glu_a, glu_b, x4)
    x6 = ffn(x5, 1, 1, weights_of(5, [x5])["ffn"])
    loss, dx6, g["final_g"] = _final_loss(x6, w["final_g"], target)
    dx6 = on_loss(loss, dx6)

    dx5 = ffn_back(dx6, 1, 1)

    def glu_bwd(d, p, q):
        sg = jax.nn.sigmoid(q)
        return d * sg, d * p * sg * (1.0 - sg)

    dp, dq = _ew("glu_bwd", glu_bwd, [dx5, gp, gq], [(D_MODEL, bf16), (D_MODEL, bf16)])
    dyg = _mm("glu_dy_a", dp, glu_a, "nt")[0]
    dyg = _mm("glu_dy_b", dq, glu_b, "nt", extras=[dyg], epilogue=lambda acc, r: (acc + r,))[0]
    g["glu_a"], ga_half = _mm_tn("glu_dw_a", ygb, dp)
    g["glu_b"], gb_half = _mm_tn("glu_dw_b", ygb, dq)
    dyg = send("glu", {"glu_a": slots(ga_half), "glu_b": slots(gb_half)}, dyg)
    du, dlr, dli, dwbr, dwbi, dwcr, dwci, g["s5_d"] = _s5_bwd(dyg, ypre, u, h0s, hrs, his, lbr_f, lbi_f, wbr, wbi, wcr, wci, w["s5_d"], nb, s)
    g["s5_cre"], g["s5_cim"] = _blockdiag_t(dwcr), _blockdiag_t(dwci)
    g["s5_lr"], g["s5_li"], g["s5_ldt"], g["s5_bre"], g["s5_bim"] = _s5_prep_bwd(
        w["s5_lr"], w["s5_li"], w["s5_ldt"], w["s5_bre"], w["s5_bim"],
        (dlr.reshape(S5_GROUPS, S5_STATE), dli.reshape(S5_GROUPS, S5_STATE), _blockdiag_t(dwbr), _blockdiag_t(dwbi)))
    dx4, g["mix_g_1"] = _norm_bwd("mix_norm_1_bwd", du, x4, w["mix_g"][1], dx5)
    dx3 = ffn_back(dx4, 1, 0)
    dx2 = ffn_back(dx3, 0, 1)
    dmerged = _mm("out_proj_dx", dx2, w_out, "nt")[0]
    g["w_out"], wo_half = _mm_tn("out_proj_dw", merged, dx2)
    dmerged = send("w_out", {"w_out": slots(wo_half)}, dmerged)
    dq_, dk_, dv_, dgate, g["ret_g"] = _ret_bwd(dmerged, o_raw, rprev, proj, cos, sin, w["ret_g"], nb, s)
    (dxl, dgl, g["conv_w"], g["conv_b"], g["lru_w_a"], g["lru_b_a"], g["lru_w_i"], g["lru_b_i"], g["lru_lam"]) = _lru_bwd(
        dmerged, lru_h, proj, w["conv_w"], w["conv_b"], w["lru_w_a"], w["lru_b_a"], w["lru_w_i"], w["lru_b_i"], w["lru_lam"],
        nb, s)
    dproj = _ew("dproj", lambda *p: (jnp.concatenate(p, axis=1),), [dq_, dk_, dv_, dgate, dxl, dgl], [(3072, bf16)])[0]
    dh0 = _mm("in_proj_dx", dproj, w_in, "nt")[0]
    g["w_in"], wi_half = _mm_tn("in_proj_dw", h0b, dproj)
    dh0 = send("w_in", {"w_in": jnp.transpose(wi_half.reshape(D_MODEL, N_DEV, IN_SHARD), (1, 0, 2))}, dh0)
    dx1, g["mix_g_0"] = _norm_bwd("mix_norm_0_bwd", dh0, x1, w["mix_g"][0], dx2)
    dx0 = ffn_back(dx1, 0, 0)
    g["ffn_w1"], g["ffn_w3"], g["ffn_w2"] = ffn_bufs
    return loss, dx0, g


_WEIGHTS = ["ffn_norm_g", "ffn_w1", "ffn_w3", "ffn_w2", "mix_norm_g", "w_in_even", "w_out_even", "ret_norm_g", "conv_w",
            "conv_b", "lru_w_a", "lru_b_a", "lru_w_i", "lru_b_i", "lru_lambda", "s5_lambda_re", "s5_lambda_im", "s5_log_dt",
            "s5_b_re", "s5_b_im", "s5_c_re", "s5_c_im", "s5_d", "glu_w_a", "glu_w_b", "final_norm_g"]
_BIG = ["ffn_w1", "ffn_w3", "ffn_w2", "w_in_even", "w_out_even", "glu_w_a", "glu_w_b"]
_SMALL_SHARDED = ["ffn_norm_g", "conv_w", "s5_d"]
_SMALL = [n for n in _WEIGHTS if n not in _BIG]
_MIDSIZE = ["lru_w_a", "lru_w_i", "s5_b_re", "s5_b_im", "s5_c_re", "s5_c_im"]


def kernel(x, ffn_norm_g, ffn_w1, ffn_w3, ffn_w2, mix_norm_g, w_in_even, w_out_even, ret_norm_g, conv_w, conv_b, lru_w_a, lru_b_a, lru_w_i, lru_b_i, lru_lambda, s5_lambda_re, s5_lambda_im, s5_log_dt, s5_b_re, s5_b_im, s5_c_re, s5_c_im, s5_d, glu_w_a, glu_w_b, final_norm_g, loss_target, m_ffn_norm_g, m_ffn_w1, m_ffn_w3, m_ffn_w2, m_mix_norm_g, m_w_in_even, m_w_out_even, m_ret_norm_g, m_conv_w, m_conv_b, m_lru_w_a, m_lru_b_a, m_lru_w_i, m_lru_b_i, m_lru_lambda, m_s5_lambda_re, m_s5_lambda_im, m_s5_log_dt, m_s5_b_re, m_s5_b_im, m_s5_c_re, m_s5_c_im, m_s5_d, m_glu_w_a, m_glu_w_b, m_final_norm_g, v_ffn_norm_g, v_ffn_w1, v_ffn_w3, v_ffn_w2, v_mix_norm_g, v_w_in_even, v_w_out_even, v_ret_norm_g, v_conv_w, v_conv_b, v_lru_w_a, v_lru_b_a, v_lru_w_i, v_lru_b_i, v_lru_lambda, v_s5_lambda_re, v_s5_lambda_im, v_s5_log_dt, v_s5_b_re, v_s5_b_im, v_s5_c_re, v_s5_c_im, v_s5_d, v_glu_w_a, v_glu_w_b, v_final_norm_g):
    a = dict(locals())
    nb, s, d = x.shape
    ax, ay, ac = lax.axis_index("x"), lax.axis_index("y"), lax.axis_index("c")
    dev = 4 * ax + 2 * ay + ac
    chip = 2 * ax + ay

    def ffn_shards(l, h):
        extra = FF_PAD - FF_SHARD
        return [jnp.pad(ffn_w1[l, h].astype(bf16), ((0, 0), (0, extra))), jnp.pad(ffn_w3[l, h].astype(bf16), ((0, 0), (0, extra))),
                jnp.pad(ffn_w2[l, h].astype(bf16), ((0, extra), (0, 0)))]

    first = _all_gather("ag_first", ffn_shards(0, 0) + [_pack([ffn_norm_g, conv_w, s5_d])])
    sm = first[3].reshape(N_DEV, -1)
    ffn_g_full = jnp.transpose(sm[:, :512].reshape(N_DEV, 2, 2, 128), (1, 2, 0, 3)).reshape(2, 2, D_MODEL)
    conv_w_full = jnp.transpose(sm[:, 512:768].reshape(N_DEV, 4, 64), (1, 0, 2)).reshape(4, LRU_WIDTH)
    s5_d_full = sm[:, 768:896].reshape(1, D_MODEL)

    ag_src = [None, [w_in_even[0].astype(bf16), w_out_even[0].astype(bf16)], ffn_shards(0, 1), ffn_shards(1, 0),
              [glu_w_a[0].astype(bf16), glu_w_b[0].astype(bf16)], ffn_shards(1, 1)]
    ag, token = [None], first[0]
    for k, grp in enumerate(ag_src):
        if grp is None:
            continue
        grp[0] = _tie(f"tie_ag_{k}", grp[0], [token])
        lands = [lax.dynamic_update_index_in_dim(lax.empty((N_DEV,) + t.shape, bf16), t, dev, 0) for t in grp]
        ag.append(_xchg_start(f"ag_start_{k}", "gather", grp, lands))
        token = ag[-1]["token"]

    def weights_of(k, after):
        if k == 0:
            return {"ffn": [first[0], _tie("tie_ag_started", first[1], [h["token"] for h in ag[1:]]), first[2]]}
        got = _xchg_wait(f"ag_wait_{k}", ag[k], after)
        if k == 1:
            return {"w_in": jnp.transpose(got[0], (1, 0, 2)).reshape(D_MODEL, N_DEV * IN_SHARD),
                    "w_out": got[1].reshape(D_MODEL, D_MODEL)}
        if k == 4:
            return {"glu_a": got[0].reshape(D_MODEL, D_MODEL), "glu_b": got[1].reshape(D_MODEL, D_MODEL)}
        return {"ffn": got}

    ffn_lands = [lax.empty((N_DEV - 1, 2, 2) + shp, bf16)
                 for shp in ((D_MODEL, FF_SHARD), (D_MODEL, FF_SHARD), (FF_SHARD, D_MODEL))]
    rs = []

    ffn_names = ("ffn_w1", "ffn_w3", "ffn_w2")

    def send(group, arrays, carry):
        srcs = list(arrays.values())
        if group.startswith("ffn_"):
            which = [ffn_names.index(n) for n in arrays]
            sfx = [(int(group[4]), int(group[5]))] * len(which)
            h = _xchg_start("rs_start_" + group, "scatter", srcs, [ffn_lands[k] for k in which], sfx)
            for k, land in zip(which, h["lands"]):
                ffn_lands[k] = land
        else:
            h = _xchg_start("rs_start_" + group, "scatter", srcs,
                            [lax.empty((N_DEV - 1,) + t.shape[1:], bf16) for t in srcs])
        rs.append((group, list(arrays), h))
        return _tie("tie_" + group, carry, [h["token"]])

    w = {
        "ffn_g": [[ffn_g_full[l, h].reshape(1, D_MODEL) for h in range(2)] for l in range(2)],
        "mix_g": [mix_norm_g[0:1], mix_norm_g[1:2]],
        "ret_g": ret_norm_g, "conv_w": conv_w_full, "conv_b": conv_b,
        "lru_w_a": lru_w_a[0], "lru_b_a": lru_b_a, "lru_w_i": lru_w_i[0], "lru_b_i": lru_b_i, "lru_lam": lru_lambda,
        "s5_lr": s5_lambda_re[0], "s5_li": s5_lambda_im[0], "s5_ldt": s5_log_dt.reshape(S5_GROUPS, 1),
        "s5_bre": jnp.swapaxes(s5_b_re[0], 1, 2), "s5_bim": jnp.swapaxes(s5_b_im[0], 1, 2),
        "s5_cre": s5_c_re[0], "s5_cim": s5_c_im[0], "s5_d": s5_d_full,
        "final_g": final_norm_g.reshape(1, D_MODEL),
    }

    small_grads = {}

    def last_small(g, carry):
        part = _small_partials(g)
        mine = _pack([part[n] for n in _SMALL])
        land = lax.dynamic_update_index_in_dim(lax.empty((N_DEV,) + mine.shape, f32), mine, dev, 0)
        h = _xchg_start("ag_start_small_grads", "gather", [mine], [land])
        small_grads.update(h=h, shapes=[part[n].shape for n in _SMALL])
        return _tie("tie_small_grads", carry, [h["token"]])

    total_loss = []

    def on_loss(part, carry):
        total_loss.append(lax.psum(part[0, 0], ("x", "y", "c")))
        return _tie("tie_loss", carry, [jnp.broadcast_to(total_loss[0], (8, 128))])

    _, dx, g = _local_step(x.reshape(nb * s, d), loss_target.reshape(nb * s, d), w, weights_of, send, last_small,
                           on_loss, nb, s)
    loss = total_loss[0]
    (gath,) = _xchg_wait("ag_wait_small_grads", small_grads["h"], [dx])
    full = dict(zip(_SMALL, _unpack(_sum8("sum_small_grads", gath), small_grads["shapes"])))
    for n in _SMALL_SHARDED:
        width = a[n].shape[-1]
        full[n] = lax.dynamic_slice_in_dim(full[n], dev * width, width, axis=full[n].ndim - 1)
    res = {n: _adamw("adamw_" + n, a[n], a["m_" + n], a["v_" + n], full[n]) for n in _MIDSIZE}
    tiny = [n for n in _SMALL if n not in _MIDSIZE]
    shapes = [a[n].shape for n in tiny]
    packed = _adamw("adamw_small", _pack([a[n] for n in tiny]), _pack([a["m_" + n] for n in tiny]),
                    _pack([a["v_" + n] for n in tiny]), _pack([full[n] for n in tiny]))
    res.update({n: vals for n, vals in zip(tiny, zip(*[_unpack(p, shapes) for p in packed]))})
    return _finish(a, g, dx, loss, res, packed, rs, ffn_lands, dev, nb, s, d)


def _small_partials(g):
    return {
        "ffn_norm_g": jnp.stack([jnp.stack([g[f"ffn_g_{l}{h}"][0] for h in range(2)]) for l in range(2)]),
        "mix_norm_g": jnp.concatenate([g["mix_g_0"], g["mix_g_1"]], axis=0),
        "ret_norm_g": g["ret_g"], "conv_w": g["conv_w"][None], "conv_b": g["conv_b"],
        "lru_w_a": g["lru_w_a"][None], "lru_b_a": g["lru_b_a"], "lru_w_i": g["lru_w_i"][None], "lru_b_i": g["lru_b_i"],
        "lru_lambda": g["lru_lam"], "s5_lambda_re": g["s5_lr"][None], "s5_lambda_im": g["s5_li"][None],
        "s5_log_dt": g["s5_ldt"].reshape(1, S5_GROUPS),
        "s5_b_re": jnp.swapaxes(g["s5_bre"], 1, 2)[None], "s5_b_im": jnp.swapaxes(g["s5_bim"], 1, 2)[None],
        "s5_c_re": g["s5_cre"][None], "s5_c_im": g["s5_cim"][None], "s5_d": g["s5_d"], "final_norm_g": g["final_g"][0],
    }


def _finish(a, g, dx, loss, res, packed, rs, ffn_lands, dev, nb, s, d):
    landed = {}
    for group, names, h in rs:
        if not group.startswith("ffn_"):
            landed.update(zip(names, _xchg_wait("rs_wait_" + group, h, [dx])))
    kinds = {"ffn_w1": "lead", "ffn_w3": "lead", "ffn_w2": "lead", "w_in": "cols", "w_out": "rows", "glu_a": "rows",
             "glu_b": "rows"}

    def update(n, short):
        res[n] = _adamw("adamw_" + n, a[n], a["m_" + n], a["v_" + n], g[short],
                        landed[short].reshape((N_DEV - 1,) + a[n].shape), slot=(dev, kinds[short]))

    for n, short in zip(_BIG[3:], ("w_in", "w_out", "glu_a", "glu_b")):
        update(n, short)
    after = [dx, packed[0]] + [res[n][0] for n in _BIG[3:] + _MIDSIZE]
    ffn_names = ("ffn_w1", "ffn_w3", "ffn_w2")
    for group, names, h in rs:
        if group.startswith("ffn_") and len(names) == 3:
            ffn_lands[:] = _xchg_wait("rs_wait_" + group, h, after, ffn_lands)
    for k, n in enumerate(ffn_names):
        for group, names, h in rs:
            if group.startswith("ffn_") and names == [n]:
                (ffn_lands[k],) = _xchg_wait("rs_wait_" + group, h, after, [ffn_lands[k]])
        landed[n] = ffn_lands[k]
        update(n, n)
        after = after + [res[n][0]]

    out = [loss, dx.reshape(nb, s, d)]
    for k in range(4):
        out += [res[n][k] for n in _WEIGHTS]
    return tuple(out)
```

```python
import math

import numpy as np
import jax
import jax.numpy as jnp
from jax import lax
from jax.experimental import pallas as pl
from jax.experimental.pallas import tpu as pltpu

f32 = jnp.float32
bf16 = jnp.bfloat16

D_MODEL = 1024
N_DEV = 8
EPS = 1e-6
RET_HEADS = 4
HEAD_DIM = 128
RET_WIDTH = 512
RET_CHUNK = 128
ROPE_BASE = 10000.0
LRU_WIDTH = 512
LRU_BLOCKS = 4
LRU_C = 8.0
S5_GROUP = 16
S5_GROUPS = 64
S5_STATE = 64
S5_CHUNK = 1024
S5_BLOCKS = 8
S5_BLOCK_STATES = 512
SUBLANES = 8
D_FF = 2816
FF_SHARD = D_FF // N_DEV
FF_PAD = 384
IN_SHARD = 3072 // N_DEV
ADAM_LR = 0.001
ADAM_B1 = 0.9
ADAM_B2 = 0.999
ADAM_EPS = 1e-08
ADAM_WD = 0.01
ADAM_STEP = 10

VMEM_LIMIT = 56 * 1024 * 1024
VMEM_SPEC = pl.BlockSpec(memory_space=pltpu.VMEM)
ANY_SPEC = pl.BlockSpec(memory_space=pl.ANY)
HBM_SPEC = pl.BlockSpec(memory_space=pltpu.HBM)
SEM_SPEC = pl.BlockSpec(memory_space=pltpu.SEMAPHORE)
SIDE_EFFECT = pltpu.SideEffectType.DATAFLOW_SIDE_EFFECTING
MESH = pl.DeviceIdType.MESH


def _cp(*sem):
    return pltpu.CompilerParams(dimension_semantics=sem, vmem_limit_bytes=VMEM_LIMIT)


def _nn(a, b):
    return jnp.dot(a, b, preferred_element_type=f32)


def _nt(a, b):
    return lax.dot_general(a, b, (((1,), (1,)), ((), ())), preferred_element_type=f32)


def _tn(a, b):
    return lax.dot_general(a, b, (((0,), (0,)), ((), ())), preferred_element_type=f32)


def _rms_fwd(x, g):
    r = lax.rsqrt(jnp.mean(x * x, axis=-1, keepdims=True) + EPS)
    xn = x * r
    return xn * g, xn, r


def _rms_bwd(dh, xn, r, g):
    dxn = dh * g
    dx = r * (dxn - xn * jnp.mean(dxn * xn, axis=-1, keepdims=True))
    dg = jnp.sum(dh * xn, axis=0, keepdims=True)
    return dx, dg


def _shift_dn(v, d, row, fill=0.0):
    return jnp.where(row >= d, pltpu.roll(v, d, 0), fill)


def _shift_up(v, d, row, fill=0.0):
    n = v.shape[0]
    return jnp.where(row < n - d, pltpu.roll(v, n - d, 0), fill)


def _ew(name, fn, ins, outs, tm=512):
    t = ins[0].shape[0]
    n_in = len(ins)

    def body(*refs):
        res = fn(*[r[...] for r in refs[:n_in]])
        for o, v in zip(refs[n_in:], res):
            o[...] = v.astype(o.dtype)

    return pl.pallas_call(
        body, name=name, grid=(t // tm,),
        in_specs=[pl.BlockSpec((tm, a.shape[1]), lambda i: (i, 0)) for a in ins],
        out_specs=[pl.BlockSpec((tm, n), lambda i: (i, 0)) for n, _ in outs],
        out_shape=[jax.ShapeDtypeStruct((t, n), dt) for n, dt in outs],
        compiler_params=_cp("parallel"),
    )(*ins)


def _mm(name, x, w, kind, extras=(), epilogue=None, outs=None, tm=512, tn=1024):
    t = x.shape[0]
    n = w.shape[1] if kind == "nn" else w.shape[0]
    tn = min(tn, n)
    outs = outs or [f32]
    n_ex = len(extras)

    def body(x_ref, w_ref, *refs):
        xb = x_ref[...].astype(bf16)
        acc = _nn(xb, w_ref[...]) if kind == "nn" else _nt(xb, w_ref[...])
        res = epilogue(acc, *[r[...] for r in refs[:n_ex]]) if epilogue else (acc,)
        for o, v in zip(refs[n_ex:], res):
            o[...] = v.astype(o.dtype)

    w_spec = (pl.BlockSpec((w.shape[0], tn), lambda i, j: (0, j)) if kind == "nn"
              else pl.BlockSpec((tn, w.shape[1]), lambda i, j: (j, 0)))
    tile = pl.BlockSpec((tm, tn), lambda i, j: (i, j))
    return pl.pallas_call(
        body, name=name, grid=(t // tm, n // tn),
        in_specs=[pl.BlockSpec((tm, x.shape[1]), lambda i, j: (i, 0)), w_spec] + [tile] * n_ex,
        out_specs=[tile] * len(outs),
        out_shape=[jax.ShapeDtypeStruct((t, n), dt) for dt in outs],
        compiler_params=_cp("parallel", "parallel"),
    )(x, w, *extras)


def _mm_tn(name, x, y, tk=1024, tn=1024, tt=1024):
    t, k = x.shape
    n = y.shape[1]
    tk, tn, tt = min(tk, k), min(tn, n), min(tt, t)

    def body(x_ref, y_ref, o_ref, ob_ref):
        @pl.when(pl.program_id(2) == 0)
        def _():
            o_ref[...] = jnp.zeros_like(o_ref)
        o_ref[...] += _tn(x_ref[...].astype(bf16), y_ref[...].astype(bf16))

        @pl.when(pl.program_id(2) == pl.num_programs(2) - 1)
        def _():
            ob_ref[...] = o_ref[...].astype(bf16)

    out = pl.BlockSpec((tk, tn), lambda i, j, s: (i, j))
    return pl.pallas_call(
        body, name=name, grid=(k // tk, n // tn, t // tt),
        in_specs=[pl.BlockSpec((tt, tk), lambda i, j, s: (s, i)), pl.BlockSpec((tt, tn), lambda i, j, s: (s, j))],
        out_specs=[out, out],
        out_shape=[jax.ShapeDtypeStruct((k, n), f32), jax.ShapeDtypeStruct((k, n), bf16)],
        compiler_params=_cp("parallel", "parallel", "arbitrary"),
    )(x, y)


def _norm_fwd(name, x, g, dtype, tm=512):
    t, d = x.shape

    def body(x_ref, g_ref, h_ref):
        h_ref[...] = _rms_fwd(x_ref[...], g_ref[...])[0].astype(dtype)

    row = pl.BlockSpec((tm, d), lambda i: (i, 0))
    return pl.pallas_call(
        body, name=name, grid=(t // tm,),
        in_specs=[row, pl.BlockSpec((1, d), lambda i: (0, 0))],
        out_specs=row, out_shape=jax.ShapeDtypeStruct((t, d), dtype),
        compiler_params=_cp("parallel"),
    )(x, g)


def _norm_bwd(name, dh, x, g, dres, tm=512):
    t, d = x.shape

    def body(dh_ref, x_ref, g_ref, dres_ref, dx_ref, dg_ref):
        gv = g_ref[...]
        _, xn, r = _rms_fwd(x_ref[...], gv)
        dx, dg = _rms_bwd(dh_ref[...], xn, r, gv)
        dx_ref[...] = dres_ref[...] + dx

        @pl.when(pl.program_id(0) == 0)
        def _():
            dg_ref[...] = jnp.zeros_like(dg_ref)
        dg_ref[...] += dg

    row = pl.BlockSpec((tm, d), lambda i: (i, 0))
    vec = pl.BlockSpec((1, d), lambda i: (0, 0))
    return pl.pallas_call(
        body, name=name, grid=(t // tm,),
        in_specs=[row, row, vec, row],
        out_specs=[row, vec],
        out_shape=[jax.ShapeDtypeStruct((t, d), f32), jax.ShapeDtypeStruct((1, d), f32)],
        compiler_params=_cp("arbitrary"),
    )(dh, x, g, dres)


def _final_loss(x, g, target, tm=512):
    t, d = x.shape

    def body(x_ref, g_ref, t_ref, loss_ref, dx_ref, dg_ref):
        gv = g_ref[...]
        y, xn, r = _rms_fwd(x_ref[...], gv)
        err = y - t_ref[...]
        dy = err * (1.0 / d)
        dx, dg = _rms_bwd(dy, xn, r, gv)
        dx_ref[...] = dx

        @pl.when(pl.program_id(0) == 0)
        def _():
            dg_ref[...] = jnp.zeros_like(dg_ref)
            loss_ref[...] = jnp.zeros_like(loss_ref)
        dg_ref[...] += dg
        loss_ref[...] += jnp.full((1, 128), 0.5 / d, f32) * jnp.sum(err * err)

    row = pl.BlockSpec((tm, d), lambda i: (i, 0))
    vec = pl.BlockSpec((1, d), lambda i: (0, 0))
    return pl.pallas_call(
        body, name="final_loss", grid=(t // tm,),
        in_specs=[row, vec, row],
        out_specs=[pl.BlockSpec((1, 128), lambda i: (0, 0)), row, vec],
        out_shape=[jax.ShapeDtypeStruct((1, 128), f32), jax.ShapeDtypeStruct((t, d), f32),
                   jax.ShapeDtypeStruct((1, d), f32)],
        compiler_params=_cp("arbitrary"),
    )(x, g, target)


def _load_ffn_weights(hbm_refs, vmem_refs, sems):
    @pl.when(pl.program_id(0) == 0)
    def _():
        copies = []
        for k, (src, dst) in enumerate(zip(hbm_refs, vmem_refs)):
            for j in range(N_DEV):
                half = pl.ds((j % 2) * FF_PAD, FF_PAD)
                window = dst.at[j // 2, half, :] if k == 2 else dst.at[j // 2, :, half]
                copies.append(pltpu.make_async_copy(src.at[j], window, sems.at[k * N_DEV + j]))
        for cp in copies:
            cp.start()
        for cp in copies:
            cp.wait()


def _ffn_weight_scratch(nj, d, ff):
    return [pltpu.VMEM((nj, d, ff), bf16), pltpu.VMEM((nj, d, ff), bf16), pltpu.VMEM((nj, ff, d), bf16),
            pltpu.SemaphoreType.DMA((3 * N_DEV,))]


def _ffn_fwd(name, x, g, w1, w3, w2, tm=512):
    t, d = x.shape
    nj, ff = N_DEV // 2, 2 * FF_PAD

    def body(x_ref, g_ref, w1_hbm, w3_hbm, w2_hbm, y_ref, a_ref, b_ref, w1_ref, w3_ref, w2_ref, sems):
        _load_ffn_weights((w1_hbm, w3_hbm, w2_hbm), (w1_ref, w3_ref, w2_ref), sems)
        xv = x_ref[...]
        h, _, _ = _rms_fwd(xv, g_ref[...])
        hb = h.astype(bf16)
        acc = jnp.zeros((tm, d), f32)
        for j in range(nj):
            a = _nn(hb, w1_ref[j])
            b = _nn(hb, w3_ref[j])
            a_ref[j] = a.astype(bf16)
            b_ref[j] = b.astype(bf16)
            u = (a * jax.nn.sigmoid(a) * b).astype(bf16)
            acc = acc + _nn(u, w2_ref[j])
        y_ref[...] = xv + 0.5 * acc

    row = pl.BlockSpec((tm, d), lambda i: (i, 0))
    mid = pl.BlockSpec((nj, tm, ff), lambda i: (0, i, 0))
    return pl.pallas_call(
        body, name=name, grid=(t // tm,),
        in_specs=[row, pl.BlockSpec((1, d), lambda i: (0, 0)), ANY_SPEC, ANY_SPEC, ANY_SPEC],
        out_specs=[row, mid, mid],
        out_shape=[jax.ShapeDtypeStruct((t, d), f32), jax.ShapeDtypeStruct((nj, t, ff), bf16),
                   jax.ShapeDtypeStruct((nj, t, ff), bf16)],
        scratch_shapes=_ffn_weight_scratch(nj, d, ff),
        compiler_params=_cp("arbitrary"),
    )(x, g, w1, w3, w2)


def _ffn_dx(name, dy, x, g, w1, w3, w2, a, b, tm=256):
    t, d = x.shape
    nj, ff = N_DEV // 2, 2 * FF_PAD

    def body(dy_ref, x_ref, g_ref, w1_hbm, w3_hbm, w2_hbm, a_ref, b_ref,
             dx_ref, dg_ref, hbt_ref, dyh_ref, ut_ref, da_ref, db_ref, w1_ref, w3_ref, w2_ref, sems):
        _load_ffn_weights((w1_hbm, w3_hbm, w2_hbm), (w1_ref, w3_ref, w2_ref), sems)
        gv = g_ref[...]
        h, xn, r = _rms_fwd(x_ref[...], gv)
        hbt_ref[...] = h.astype(bf16).T
        dyv = dy_ref[...]
        dyh = (0.5 * dyv).astype(bf16)
        dyh_ref[...] = dyh
        dh = jnp.zeros((tm, d), f32)
        dus = [_nt(dyh, w2_ref[j]) for j in range(nj)]
        for j in range(nj):
            av = a_ref[j].astype(f32)
            bv = b_ref[j].astype(f32)
            s = jax.nn.sigmoid(av)
            silu = av * s
            ut_ref[j] = (silu * bv).astype(bf16).T
            du = dus[j]
            dab = (du * bv * (s * (1.0 + av * (1.0 - s)))).astype(bf16)
            dbb = (du * silu).astype(bf16)
            da_ref[j] = dab
            db_ref[j] = dbb
            dh = dh + _nt(dab, w1_ref[j]) + _nt(dbb, w3_ref[j])
        dx, dg = _rms_bwd(dh, xn, r, gv)
        dx_ref[...] = dyv + dx

        @pl.when(pl.program_id(0) == 0)
        def _():
            dg_ref[...] = jnp.zeros_like(dg_ref)
        dg_ref[...] += dg

    row = pl.BlockSpec((tm, d), lambda i: (i, 0))
    vec = pl.BlockSpec((1, d), lambda i: (0, 0))
    mid = pl.BlockSpec((nj, tm, ff), lambda i: (0, i, 0))
    mid_shape = jax.ShapeDtypeStruct((nj, t, ff), bf16)
    return pl.pallas_call(
        body, name=name, grid=(t // tm,),
        in_specs=[row, row, vec, ANY_SPEC, ANY_SPEC, ANY_SPEC, mid, mid],
        out_specs=[row, vec, pl.BlockSpec((d, tm), lambda i: (0, i)), row,
                   pl.BlockSpec((nj, ff, tm), lambda i: (0, 0, i)), mid, mid],
        out_shape=[jax.ShapeDtypeStruct((t, d), f32), jax.ShapeDtypeStruct((1, d), f32),
                   jax.ShapeDtypeStruct((d, t), bf16), jax.ShapeDtypeStruct((t, d), bf16),
                   jax.ShapeDtypeStruct((nj, ff, t), bf16), mid_shape, mid_shape],
        scratch_shapes=_ffn_weight_scratch(nj, d, ff),
        compiler_params=_cp("arbitrary"),
    )(dy, x, g, w1, w3, w2, a, b)


def _ffn_dw(name, xt, ys, bufs, l, h, tt=2048):
    n = len(ys)
    t = ys[0].shape[-2]
    tt = min(tt, t)
    cut_cols = xt.ndim == 2

    def body(x_ref, *refs):
        y_refs, outs, accs = refs[:n], refs[2 * n:4 * n], refs[4 * n:]
        s = pl.program_id(1)
        xv = x_ref[0] if xt.ndim == 3 else x_ref[...]
        for k in range(n):
            prod = _nn(xv, y_refs[k][0] if ys[k].ndim == 3 else y_refs[k][...])

            @pl.when(s == 0)
            def _():
                accs[k][...] = prod

            @pl.when(s > 0)
            def _():
                accs[k][...] += prod

        @pl.when(s == pl.num_programs(1) - 1)
        def _():
            for k in range(n):
                total = accs[k][...]
                for e in range(2):
                    lo = e * FF_PAD
                    part = total[:, lo:lo + FF_SHARD] if cut_cols else total[lo:lo + FF_SHARD, :]
                    outs[k][e] = part
                    outs[n + k][e] = part.astype(bf16)

    x_spec = (pl.BlockSpec((1, xt.shape[1], tt), lambda p, s: (p, 0, s)) if xt.ndim == 3
              else pl.BlockSpec((xt.shape[0], tt), lambda p, s: (0, s)))
    y_specs = [pl.BlockSpec((1, tt, y.shape[2]), lambda p, s: (p, s, 0)) if y.ndim == 3
               else pl.BlockSpec((tt, y.shape[1]), lambda p, s: (s, 0)) for y in ys]
    dims = [b.shape[-2:] for b in bufs]
    outs = pl.pallas_call(
        body, name=name, grid=(N_DEV // 2, t // tt),
        in_specs=[x_spec] + y_specs + [ANY_SPEC] * n,
        out_specs=[pl.BlockSpec((2, None, None, k_, n_), lambda p, s: (p, l, h, 0, 0)) for k_, n_ in dims]
        + [pl.BlockSpec((2, k_, n_), lambda p, s: (p, 0, 0)) for k_, n_ in dims],
        out_shape=[jax.ShapeDtypeStruct(b.shape, b.dtype) for b in bufs]
        + [jax.ShapeDtypeStruct((N_DEV, k_, n_), bf16) for k_, n_ in dims],
        input_output_aliases={1 + n + k: k for k in range(n)},
        scratch_shapes=[pltpu.VMEM((xt.shape[-2], y.shape[-1]), f32) for y in ys],
        compiler_params=_cp("parallel", "arbitrary"),
    )(xt, *ys, *bufs)
    return outs[:n], outs[n:]


_LOG_GAMMA = [float(np.log1p(-np.float32(2.0) ** np.float32(-5.0 - h))) for h in range(RET_HEADS)]


def _ret_consts(h):
    lg = jnp.where(h == 0, _LOG_GAMMA[0], jnp.where(h == 1, _LOG_GAMMA[1],
                   jnp.where(h == 2, _LOG_GAMMA[2], _LOG_GAMMA[3]))).astype(f32)
    c = RET_CHUNK
    r = lax.broadcasted_iota(jnp.int32, (c, c), 0)
    cc = lax.broadcasted_iota(jnp.int32, (c, c), 1)
    decay = jnp.where(r >= cc, jnp.exp(lg * jnp.maximum((r - cc).astype(f32), 0.0)), 0.0)
    pos = lax.broadcasted_iota(jnp.int32, (c, 1), 0).astype(f32)
    kd = jnp.exp(lg * (c - 1.0 - pos))
    qd = jnp.exp(lg * (pos + 1.0))
    gc = jnp.exp(lg * c)
    return decay, kd, qd, gc


def _rope(x, cos, sin):
    return x * cos + pltpu.roll(x, HEAD_DIM // 2, 1) * sin


def _rope_t(g, cos, sin):
    return g * cos + pltpu.roll(g * sin, HEAD_DIM // 2, 1)


def _rope_tables(s):
    half = HEAD_DIM // 2
    inv = ROPE_BASE ** (-jnp.arange(half, dtype=f32) / half)
    ang = jnp.arange(s, dtype=f32)[:, None] * inv[None, :]
    cos, sin = jnp.cos(ang), jnp.sin(ang)
    return jnp.concatenate([cos, cos], axis=1), jnp.concatenate([-sin, sin], axis=1)


def _head_ln(o):
    mu = jnp.mean(o, axis=-1, keepdims=True)
    oc = o - mu
    rs = lax.rsqrt(jnp.mean(oc * oc, axis=-1, keepdims=True) + EPS)
    return oc * rs, rs


def _ret_fwd(proj, cos, sin, ret_g, nb, s):
    c = RET_CHUNK
    nc = s // c
    t = nb * s
    scale = HEAD_DIM ** -0.5

    def body(q_ref, k_ref, v_ref, gate_ref, cos_ref, sin_ref, g_ref, o_ref, rprev_ref, m_ref):
        decay, kd, qd, gc = _ret_consts(pl.program_id(0))
        gv = g_ref[...]

        def chunk(b, n, rv):
            rows = pl.ds(pl.multiple_of(b * s + n * c, c), c)
            pos = pl.ds(pl.multiple_of(n * c, c), c)
            cs, sn = cos_ref[pos, :], sin_ref[pos, :]
            q = _rope(q_ref[rows, :], cs, sn)
            k = _rope(k_ref[rows, :], cs, sn) * scale
            vb = v_ref[rows, :].astype(bf16)
            sc = _nt(q.astype(bf16), k.astype(bf16)) * decay
            rprev_ref[b, n] = rv
            o = _nn(sc.astype(bf16), vb) + _nn((q * qd).astype(bf16), rv.astype(bf16))
            o_ref[rows, :] = o
            y, _ = _head_ln(o)
            gate = gate_ref[rows, :]
            m_ref[rows, :] = y * gv * (gate * jax.nn.sigmoid(gate))
            return rv * gc + _tn((k * kd).astype(bf16), vb)

        def step(n, carry):
            return tuple(chunk(b, n, carry[b]) for b in range(nb))

        lax.fori_loop(0, nc, step, (jnp.zeros((HEAD_DIM, HEAD_DIM), f32),) * nb)

    def col(off):
        return pl.BlockSpec((t, HEAD_DIM), lambda h: (0, off + h))

    tab = pl.BlockSpec((s, HEAD_DIM), lambda h: (0, 0))
    return pl.pallas_call(
        body, name="ret_fwd", grid=(RET_HEADS,),
        in_specs=[col(0), col(4), col(8), col(12), tab, tab, pl.BlockSpec((1, HEAD_DIM), lambda h: (0, h))],
        out_specs=[col(0), pl.BlockSpec((nb, None, nc, HEAD_DIM, HEAD_DIM), lambda h: (0, h, 0, 0, 0)), col(0)],
        out_shape=[jax.ShapeDtypeStruct((t, RET_WIDTH), f32),
                   jax.ShapeDtypeStruct((nb, RET_HEADS, nc, HEAD_DIM, HEAD_DIM), f32),
                   jax.ShapeDtypeStruct((t, RET_WIDTH), f32)],
        compiler_params=_cp("parallel"),
    )(proj, proj, proj, proj, cos, sin, ret_g)


def _ret_bwd(dmerged, o_raw, rprev, proj, cos, sin, ret_g, nb, s):
    c = RET_CHUNK
    nc = s // c
    t = nb * s
    scale = HEAD_DIM ** -0.5

    def body(dm_ref, o_ref, rprev_ref, q_ref, k_ref, v_ref, gate_ref, cos_ref, sin_ref, g_ref,
             dq_ref, dk_ref, dv_ref, dgate_ref, dg_ref):
        decay, kd, qd, gc = _ret_consts(pl.program_id(0))
        gv = g_ref[...]

        def chunk(b, n, drn, dg):
            rows = pl.ds(pl.multiple_of(b * s + n * c, c), c)
            pos = pl.ds(pl.multiple_of(n * c, c), c)
            cs, sn = cos_ref[pos, :], sin_ref[pos, :]
            q = _rope(q_ref[rows, :], cs, sn)
            k = _rope(k_ref[rows, :], cs, sn) * scale
            qb, kb = q.astype(bf16), k.astype(bf16)
            vb = v_ref[rows, :].astype(bf16)
            sc = _nt(qb, kb) * decay
            y, rs = _head_ln(o_ref[rows, :])
            gate = gate_ref[rows, :]
            sg = jax.nn.sigmoid(gate)
            silu = gate * sg
            dm = dm_ref[rows, :]
            dgate_ref[rows, :] = dm * y * gv * (sg * (1.0 + gate * (1.0 - sg)))
            dyl = dm * gv * silu
            dg = dg + jnp.sum(dm * y * silu, axis=0, keepdims=True)
            do = rs * (dyl - jnp.mean(dyl, axis=-1, keepdims=True) - y * jnp.mean(dyl * y, axis=-1, keepdims=True))
            dob = do.astype(bf16)
            rv = rprev_ref[b, n]
            drb = drn.astype(bf16)
            ds = (_nt(dob, vb) * decay).astype(bf16)
            kdb = (k * kd).astype(bf16)
            qdb = (q * qd).astype(bf16)
            dq_r = _nn(ds, kb) + _nt(dob, rv.astype(bf16)) * qd
            dk_r = _tn(ds, qb) + _nt(vb, drb) * kd
            dv_ref[rows, :] = _tn(sc.astype(bf16), dob) + _nn(kdb, drb)
            dq_ref[rows, :] = _rope_t(dq_r, cs, sn)
            dk_ref[rows, :] = _rope_t(dk_r * scale, cs, sn)
            return drn * gc + _tn(qdb, dob), dg

        def step(i, carry):
            out = [chunk(b, nc - 1 - i, *carry[b]) for b in range(nb)]
            return tuple(out)

        zero = (jnp.zeros((HEAD_DIM, HEAD_DIM), f32), jnp.zeros((1, HEAD_DIM), f32))
        done = lax.fori_loop(0, nc, step, (zero,) * nb)
        dg_ref[...] = sum(dg for _, dg in done)

    def col(off):
        return pl.BlockSpec((t, HEAD_DIM), lambda h: (0, off + h))

    tab = pl.BlockSpec((s, HEAD_DIM), lambda h: (0, 0))
    gsp = pl.BlockSpec((1, HEAD_DIM), lambda h: (0, h))
    out_t = jax.ShapeDtypeStruct((t, RET_WIDTH), f32)
    return pl.pallas_call(
        body, name="ret_bwd", grid=(RET_HEADS,),
        in_specs=[col(0), col(0), pl.BlockSpec((nb, None, nc, HEAD_DIM, HEAD_DIM), lambda h: (0, h, 0, 0, 0)),
                  col(0), col(4), col(8), col(12), tab, tab, gsp],
        out_specs=[col(0), col(0), col(0), col(0), gsp],
        out_shape=[out_t, out_t, out_t, out_t, jax.ShapeDtypeStruct((1, RET_WIDTH), f32)],
        compiler_params=_cp("parallel"),
    )(dmerged, o_raw, rprev, proj, proj, proj, proj, cos, sin, ret_g)


def _neg_expm1(z):
    series = -(z * (1.0 + z * (0.5 + z * (1.0 / 6.0 + z * (1.0 / 24.0)))))
    return jnp.where(z > -0.01, series, 1.0 - jnp.exp(z))


def _lru_gates(xc, pa, pi, lam):
    r = jax.nn.sigmoid(pa)
    i = jax.nn.sigmoid(pi)
    log_a = -LRU_C * r * jax.nn.softplus(-lam)
    a = jnp.exp(log_a)
    bx = jnp.sqrt(_neg_expm1(2.0 * log_a)) * i * xc
    return a, bx


def _scan_rows(a, b, row, up):
    sub = row[:SUBLANES] & (SUBLANES - 1)
    groups = list(range(a.shape[0] // SUBLANES))
    out = [None] * len(groups)
    edge = slice(0, 1) if up else slice(SUBLANES - 1, SUBLANES)
    carry = jnp.zeros((1, a.shape[1]), f32)
    for g in (reversed(groups) if up else groups):
        rows = slice(g * SUBLANES, (g + 1) * SUBLANES)
        xa, xb = a[rows], b[rows]
        d = 1
        while d < SUBLANES:
            keep = (sub < SUBLANES - d) if up else (sub >= d)
            shift = SUBLANES - d if up else d
            xb = xa * jnp.where(keep, pltpu.roll(xb, shift, 0), 0.0) + xb
            xa = xa * jnp.where(keep, pltpu.roll(xa, shift, 0), 1.0)
            d *= 2
        out[g] = xb + xa * carry
        carry = out[g][edge]
    return jnp.concatenate(out, axis=0)


def _scan_fwd(a, b, row):
    return _scan_rows(a, b, row, False)


def _scan_bwd(c, b, row):
    return _scan_rows(c, b, row, True)


def _conv_fwd(x, cw, cb, row):
    return (cb + cw[3:4] * x + cw[2:3] * _shift_dn(x, 1, row) + cw[1:2] * _shift_dn(x, 2, row)
            + cw[0:1] * _shift_dn(x, 3, row))


def _lru_specs(s, order):
    def im(f):
        return (lambda b, g: f(b, g)) if order == "bg" else (lambda g, b: f(b, g))
    seq = lambda off: pl.BlockSpec((s, 128), im(lambda b, g: (b, off + g)))
    vec = pl.BlockSpec((1, 128), im(lambda b, g: (0, g)))
    cw = pl.BlockSpec((4, 128), im(lambda b, g: (0, g)))
    mat = pl.BlockSpec((1, 128, 128), im(lambda b, g: (g, 0, 0)))
    return seq, vec, cw, mat


def _lru_fwd(proj, conv_w, conv_b, w_a, b_a, w_i, b_i, lam, nb, s):
    def body(x_ref, gt_ref, cw_ref, cb_ref, wa_ref, ba_ref, wi_ref, bi_ref, lam_ref, out_ref, h_ref):
        row = lax.broadcasted_iota(jnp.int32, (s, 128), 0)
        xc = _conv_fwd(x_ref[...], cw_ref[...], cb_ref[...], row)
        xcb = xc.astype(bf16)
        pa = _nn(xcb, wa_ref[0].astype(bf16)) + ba_ref[...]
        pi = _nn(xcb, wi_ref[0].astype(bf16)) + bi_ref[...]
        a, bx = _lru_gates(xc, pa, pi, lam_ref[...])
        h = _scan_fwd(a, bx, row)
        h_ref[...] = h
        out_ref[...] = h * jax.nn.gelu(gt_ref[...])

    seq, vec, cw, mat = _lru_specs(s, "bg")
    out = jax.ShapeDtypeStruct((nb * s, LRU_WIDTH), f32)
    return pl.pallas_call(
        body, name="lru_fwd", grid=(nb, LRU_BLOCKS),
        in_specs=[seq(16), seq(20), cw, vec, mat, vec, mat, vec, vec],
        out_specs=[seq(0), seq(0)], out_shape=[out, out],
        compiler_params=_cp("parallel", "parallel"),
    )(proj, proj, conv_w, conv_b, w_a, b_a, w_i, b_i, lam)


def _lru_bwd(dmerged, states, proj, conv_w, conv_b, w_a, b_a, w_i, b_i, lam, nb, s):
    def body(dout_ref, h_ref, x_ref, gt_ref, cw_ref, cb_ref, wa_ref, ba_ref, wi_ref, bi_ref, lam_ref,
             dx_ref, dgt_ref, dcw_ref, dcb_ref, dwa_ref, dba_ref, dwi_ref, dbi_ref, dlam_ref):
        row = lax.broadcasted_iota(jnp.int32, (s, 128), 0)
        x = x_ref[...]
        cwv = cw_ref[...]
        xc = _conv_fwd(x, cwv, cb_ref[...], row)
        xcb = xc.astype(bf16)
        wab, wib = wa_ref[0].astype(bf16), wi_ref[0].astype(bf16)
        pa = _nn(xcb, wab) + ba_ref[...]
        pi = _nn(xcb, wib) + bi_ref[...]
        (a, _), gates_vjp = jax.vjp(_lru_gates, xc, pa, pi, lam_ref[...])
        h = h_ref[...]
        ge, gelu_vjp = jax.vjp(jax.nn.gelu, gt_ref[...])
        dout = dout_ref[...]
        dgt_ref[...] = gelu_vjp(dout * h)[0]
        adj = _scan_bwd(_shift_up(a, 1, row), dout * ge, row)
        dxc, dpa, dpi, dlam = gates_vjp((adj * _shift_dn(h, 1, row), adj))
        dpab, dpib = dpa.astype(bf16), dpi.astype(bf16)
        dxc = dxc + _nt(dpab, wab) + _nt(dpib, wib)
        dx_ref[...] = (cwv[3:4] * dxc + cwv[2:3] * _shift_up(dxc, 1, row) + cwv[1:2] * _shift_up(dxc, 2, row)
                       + cwv[0:1] * _shift_up(dxc, 3, row))

        @pl.when(pl.program_id(1) == 0)
        def _():
            for r in (dcw_ref, dcb_ref, dwa_ref, dba_ref, dwi_ref, dbi_ref, dlam_ref):
                r[...] = jnp.zeros_like(r)
        rsum = lambda v: jnp.sum(v, axis=0, keepdims=True)
        dcw_ref[...] += jnp.concatenate([rsum(dxc * _shift_dn(x, 3, row)), rsum(dxc * _shift_dn(x, 2, row)),
                                         rsum(dxc * _shift_dn(x, 1, row)), rsum(dxc * x)], axis=0)
        dcb_ref[...] += rsum(dxc)
        dwa_ref[0] += _tn(xcb, dpab)
        dwi_ref[0] += _tn(xcb, dpib)
        dba_ref[...] += rsum(dpa)
        dbi_ref[...] += rsum(dpi)
        dlam_ref[...] += dlam

    seq, vec, cw, mat = _lru_specs(s, "gb")
    t = nb * s
    vshape = jax.ShapeDtypeStruct((1, LRU_WIDTH), f32)
    mshape = jax.ShapeDtypeStruct((LRU_BLOCKS, 128, 128), f32)
    return pl.pallas_call(
        body, name="lru_bwd", grid=(LRU_BLOCKS, nb),
        in_specs=[seq(4), seq(0), seq(16), seq(20), cw, vec, mat, vec, mat, vec, vec],
        out_specs=[seq(0), seq(0), cw, vec, mat, vec, mat, vec, vec],
        out_shape=[jax.ShapeDtypeStruct((t, LRU_WIDTH), f32), jax.ShapeDtypeStruct((t, LRU_WIDTH), f32),
                   jax.ShapeDtypeStruct((4, LRU_WIDTH), f32), vshape, mshape, vshape, mshape, vshape, vshape],
        compiler_params=_cp("parallel", "arbitrary"),
    )(dmerged, states, proj, proj, conv_w, conv_b, w_a, b_a, w_i, b_i, lam)


def _s5_disc(lr, li, ldt, bre, bim):
    dt = jnp.exp(ldt)
    mag = jnp.exp(lr * dt)
    lbr = mag * jnp.cos(li * dt)
    lbi = mag * jnp.sin(li * dt)
    den = lr * lr + li * li
    nr = lbr - 1.0
    fr = (nr * lr + lbi * li) / den
    fi = (lbi * lr - nr * li) / den
    bbr = fr[:, None, :] * bre - fi[:, None, :] * bim
    bbi = fr[:, None, :] * bim + fi[:, None, :] * bre
    return lbr, lbi, bbr, bbi


def _s5_prep(lr, li, ldt, bre, bim):
    def body(lr_ref, li_ref, ldt_ref, bre_ref, bim_ref, o1, o2, o3, o4):
        o1[...], o2[...], o3[...], o4[...] = _s5_disc(lr_ref[...], li_ref[...], ldt_ref[...], bre_ref[...], bim_ref[...])

    return pl.pallas_call(
        body, name="s5_prep", in_specs=[VMEM_SPEC] * 5, out_specs=[VMEM_SPEC] * 4,
        out_shape=[jax.ShapeDtypeStruct(lr.shape, f32), jax.ShapeDtypeStruct(lr.shape, f32),
                   jax.ShapeDtypeStruct(bre.shape, f32), jax.ShapeDtypeStruct(bre.shape, f32)],
    )(lr, li, ldt, bre, bim)


def _s5_prep_bwd(lr, li, ldt, bre, bim, cts):
    def body(lr_ref, li_ref, ldt_ref, bre_ref, bim_ref, g1, g2, g3, g4, o1, o2, o3, o4, o5):
        _, vjp = jax.vjp(_s5_disc, lr_ref[...], li_ref[...], ldt_ref[...], bre_ref[...], bim_ref[...])
        o1[...], o2[...], o3[...], o4[...], o5[...] = vjp((g1[...], g2[...], g3[...], g4[...]))

    return pl.pallas_call(
        body, name="s5_prep_bwd", in_specs=[VMEM_SPEC] * 9, out_specs=[VMEM_SPEC] * 5,
        out_shape=[jax.ShapeDtypeStruct(v.shape, f32) for v in (lr, li, ldt, bre, bim)],
    )(lr, li, ldt, bre, bim, *cts)


def _cmul(ar, ai, br, bi):
    return ar * br - ai * bi, ar * bi + ai * br


def _s5_pow_table(lr, li, n, row, up):
    ar = jnp.broadcast_to(lr, (n, lr.shape[1]))
    ai = jnp.broadcast_to(li, (n, li.shape[1]))
    shift = _shift_up if up else _shift_dn
    d = 1
    while d < n:
        ar, ai = _cmul(ar, ai, shift(ar, d, row, 1.0), shift(ai, d, row, 0.0))
        d *= 2
    return ar, ai


def _s5_step_factors(lr, li, row, up):
    sub = row & (SUBLANES - 1)
    out, pr, pi, d = [], lr, li, 1
    while d < SUBLANES:
        keep = (sub < SUBLANES - d) if up else (sub >= d)
        out.append((jnp.where(keep, pr, 0.0), jnp.where(keep, pi, 0.0)))
        pr, pi = _cmul(pr, pi, pr, pi)
        d *= 2
    return out


def _s5_scan(br, bi, steps, tab_r, tab_i, cr, ci, up):
    groups = list(range(br.shape[0] // SUBLANES))
    out_r, out_i = [None] * len(groups), [None] * len(groups)
    edge = slice(0, 1) if up else slice(SUBLANES - 1, SUBLANES)
    for g in (reversed(groups) if up else groups):
        rows = slice(g * SUBLANES, (g + 1) * SUBLANES)
        xr, xi = br[rows], bi[rows]
        for k, (mr, mi) in enumerate(steps):
            shift = SUBLANES - (1 << k) if up else 1 << k
            tr, ti = _cmul(mr, mi, pltpu.roll(xr, shift, 0), pltpu.roll(xi, shift, 0))
            xr, xi = xr + tr, xi + ti
        tr, ti = _cmul(tab_r, tab_i, cr, ci)
        hr, hi = xr + tr, xi + ti
        out_r[g], out_i[g] = hr, hi
        cr, ci = hr[edge], hi[edge]
    return jnp.concatenate(out_r, axis=0), jnp.concatenate(out_i, axis=0)


def _s5_specs(t, nb, nc):
    seq = pl.BlockSpec((t, 128), lambda k: (0, k))
    lvec = pl.BlockSpec((1, S5_BLOCK_STATES), lambda k: (0, k))
    dvec = pl.BlockSpec((1, 128), lambda k: (0, k))
    wmat = pl.BlockSpec((1, 128, S5_BLOCK_STATES), lambda k: (k, 0, 0))
    h0 = pl.BlockSpec((nb, None, nc, 2, S5_BLOCK_STATES), lambda k: (0, k, 0, 0, 0))
    states = pl.BlockSpec((t, S5_BLOCK_STATES), lambda k: (0, k))
    return seq, lvec, dvec, wmat, h0, states


def _s5_fwd(u, lbr, lbi, wbr, wbi, wcr, wci, dskip, nb, s):
    ln = min(S5_CHUNK, s)
    nc = s // ln

    def body(u_ref, lr_ref, li_ref, wbr_ref, wbi_ref, wcr_ref, wci_ref, d_ref, yg_ref, y_ref, h0_ref, hr_ref, hi_ref):
        row = lax.broadcasted_iota(jnp.int32, (ln, S5_BLOCK_STATES), 0)
        lr, li = lr_ref[...], li_ref[...]
        pr, pi = _s5_pow_table(lr, li, SUBLANES, row[:SUBLANES], False)
        steps = _s5_step_factors(lr, li, row[:SUBLANES], False)
        dv = d_ref[...]

        def chunk(b, n, h0r, h0i):
            st = pl.multiple_of(b * s + n * ln, ln)
            uc = u_ref[pl.ds(st, ln), :]
            ub = uc.astype(bf16)
            hr, hi = _s5_scan(_nn(ub, wbr_ref[0]), _nn(ub, wbi_ref[0]), steps, pr, pi, h0r, h0i, False)
            h0_ref[b, n, 0:1, :] = h0r
            h0_ref[b, n, 1:2, :] = h0i
            hrb, hib = hr.astype(bf16), hi.astype(bf16)
            hr_ref[pl.ds(st, ln), :] = hrb
            hi_ref[pl.ds(st, ln), :] = hib
            y = _nt(hrb, wcr_ref[0]) - _nt(hib, wci_ref[0]) + dv * uc
            y_ref[pl.ds(st, ln), :] = y
            yg_ref[pl.ds(st, ln), :] = jax.nn.gelu(y).astype(bf16)
            return hr[ln - 1:ln, :], hi[ln - 1:ln, :]

        def step(n, carry):
            return tuple(chunk(b, n, *carry[b]) for b in range(nb))

        z = jnp.zeros((1, S5_BLOCK_STATES), f32)
        lax.fori_loop(0, nc, step, ((z, z),) * nb)

    t = nb * s
    seq, lvec, dvec, wmat, h0, states = _s5_specs(t, nb, nc)
    return pl.pallas_call(
        body, name="s5_fwd", grid=(S5_BLOCKS,),
        in_specs=[seq, lvec, lvec, wmat, wmat, wmat, wmat, dvec],
        out_specs=[seq, seq, h0, states, states],
        out_shape=[jax.ShapeDtypeStruct((t, D_MODEL), bf16), jax.ShapeDtypeStruct((t, D_MODEL), f32),
                   jax.ShapeDtypeStruct((nb, S5_BLOCKS, nc, 2, S5_BLOCK_STATES), f32),
                   jax.ShapeDtypeStruct((t, S5_BLOCKS * S5_BLOCK_STATES), bf16),
                   jax.ShapeDtypeStruct((t, S5_BLOCKS * S5_BLOCK_STATES), bf16)],
        compiler_params=_cp("parallel"),
    )(u, lbr, lbi, wbr, wbi, wcr, wci, dskip)


def _s5_bwd(dyg, y, u, h0, hrs, his, lbr, lbi, wbr, wbi, wcr, wci, dskip, nb, s):
    ln = min(S5_CHUNK, s)
    nc = s // ln

    def body(dyg_ref, y_ref, u_ref, h0_ref, hr_ref, hi_ref, lr_ref, li_ref, wbr_ref, wbi_ref, wcr_ref, wci_ref, d_ref,
             du_ref, dlr_ref, dli_ref, dwbr_ref, dwbi_ref, dwcr_ref, dwci_ref, dd_ref):
        for r in (dlr_ref, dli_ref, dwbr_ref, dwbi_ref, dwcr_ref, dwci_ref, dd_ref):
            r[...] = jnp.zeros_like(r)
        row = lax.broadcasted_iota(jnp.int32, (ln, S5_BLOCK_STATES), 0)
        lr, li = lr_ref[...], li_ref[...]
        qr, qi = _s5_pow_table(lr, -li, SUBLANES, row[:SUBLANES], True)
        steps_up = _s5_step_factors(lr, -li, row[:SUBLANES], True)
        dv = d_ref[...]
        rsum = lambda v: jnp.sum(v, axis=0, keepdims=True)

        def chunk(b, n, gnr, gni):
            st = pl.multiple_of(b * s + n * ln, ln)
            uc = u_ref[pl.ds(st, ln), :]
            ub = uc.astype(bf16)
            h0v = h0_ref[b, n]
            h0r, h0i = h0v[0:1], h0v[1:2]
            hrb, hib = hr_ref[pl.ds(st, ln), :], hi_ref[pl.ds(st, ln), :]
            hr, hi = hrb.astype(f32), hib.astype(f32)
            dy = jax.vjp(jax.nn.gelu, y_ref[pl.ds(st, ln), :])[1](dyg_ref[pl.ds(st, ln), :])[0]
            dyb = dy.astype(bf16)
            dd_ref[...] += rsum(dy * uc)
            gr, gi = _s5_scan(_nn(dyb, wcr_ref[0]), -_nn(dyb, wci_ref[0]), steps_up, qr, qi, gnr, gni, True)
            hpr = jnp.where(row >= 1, pltpu.roll(hr, 1, 0), h0r)
            hpi = jnp.where(row >= 1, pltpu.roll(hi, 1, 0), h0i)
            dlr_ref[...] += rsum(gr * hpr + gi * hpi)
            dli_ref[...] += rsum(gi * hpr - gr * hpi)
            grb, gib = gr.astype(bf16), gi.astype(bf16)
            dwbr_ref[0] += _tn(ub, grb)
            dwbi_ref[0] += _tn(ub, gib)
            dwcr_ref[0] += _tn(dyb, hrb)
            dwci_ref[0] -= _tn(dyb, hib)
            du_ref[pl.ds(st, ln), :] = _nt(grb, wbr_ref[0]) + _nt(gib, wbi_ref[0]) + dv * dy
            return gr[0:1, :], gi[0:1, :]

        def step(i, carry):
            return tuple(chunk(b, nc - 1 - i, *carry[b]) for b in range(nb))

        z = jnp.zeros((1, S5_BLOCK_STATES), f32)
        lax.fori_loop(0, nc, step, ((z, z),) * nb)

    t = nb * s
    seq, lvec, dvec, wmat, h0s, states = _s5_specs(t, nb, nc)
    lshape = jax.ShapeDtypeStruct((1, S5_BLOCKS * S5_BLOCK_STATES), f32)
    wshape = jax.ShapeDtypeStruct((S5_BLOCKS, 128, S5_BLOCK_STATES), f32)
    return pl.pallas_call(
        body, name="s5_bwd", grid=(S5_BLOCKS,),
        in_specs=[seq, seq, seq, h0s, states, states, lvec, lvec, wmat, wmat, wmat, wmat, dvec],
        out_specs=[seq, lvec, lvec, wmat, wmat, wmat, wmat, dvec],
        out_shape=[jax.ShapeDtypeStruct((t, D_MODEL), f32), lshape, lshape, wshape, wshape, wshape, wshape,
                   jax.ShapeDtypeStruct((1, D_MODEL), f32)],
        compiler_params=_cp("parallel"),
    )(dyg, y, u, h0, hrs, his, lbr, lbi, wbr, wbi, wcr, wci, dskip)


def _blockdiag(w):
    w4 = w.reshape(S5_BLOCKS, 8, S5_GROUP, S5_STATE)
    same_group = jnp.eye(8, dtype=bool)[None, :, None, :, None]
    return jnp.where(same_group, w4[:, :, :, None, :], 0.0).reshape(S5_BLOCKS, 128, S5_BLOCK_STATES)


def _blockdiag_t(dw):
    d5 = dw.reshape(S5_BLOCKS, 8, S5_GROUP, 8, S5_STATE)
    diag = jnp.diagonal(d5, axis1=1, axis2=3)
    return jnp.moveaxis(diag, 3, 1).reshape(S5_GROUPS, S5_GROUP, S5_STATE)


def _glu_fwd(ygb, wa, wb, x, tm=512, tn=1024):
    t, d = x.shape

    def body(y_ref, wa_ref, wb_ref, x_ref, o_ref, p_ref, q_ref):
        p = _nn(y_ref[...], wa_ref[...])
        q = _nn(y_ref[...], wb_ref[...])
        p_ref[...] = p
        q_ref[...] = q
        o_ref[...] = x_ref[...] + p * jax.nn.sigmoid(q)

    tile = pl.BlockSpec((tm, tn), lambda i, j: (i, j))
    wsp = pl.BlockSpec((d, tn), lambda i, j: (0, j))
    out = jax.ShapeDtypeStruct((t, d), f32)
    return pl.pallas_call(
        body, name="glu_fwd", grid=(t // tm, d // tn),
        in_specs=[pl.BlockSpec((tm, d), lambda i, j: (i, 0)), wsp, wsp, tile],
        out_specs=[tile, tile, tile], out_shape=[out, out, out],
        compiler_params=_cp("parallel", "parallel"),
    )(ygb, wa, wb, x)


def _place():
    x, y, c = lax.axis_index("x"), lax.axis_index("y"), lax.axis_index("c")
    return x, y, c, [(1 - x, y), (x, 1 - y), (1 - x, 1 - y)]


def _all_gather(name, arrays):
    n = len(arrays)

    def body(*refs):
        ins, outs = refs[:n], refs[n:2 * n]
        send_sems, recv_sems, local_sems = refs[2 * n:]
        x, y, c, chips = _place()
        me, sib = (x, y, c), (x, y, 1 - c)

        def copy(i, k, block, to, src=None):
            dst = outs[i].at[4 * block[0] + 2 * block[1] + block[2]]
            return pltpu.make_async_remote_copy(
                src_ref=dst if src is None else src, dst_ref=dst,
                send_sem=send_sems.at[i * 7 + k], recv_sem=recv_sems.at[i * 7 + k],
                device_id=to, device_id_type=MESH)

        mine = [pltpu.make_async_copy(ins[i], outs[i].at[4 * x + 2 * y + c], local_sems.at[i]) for i in range(n)]
        for m in mine:
            m.start()
        first = []
        for i in range(n):
            first.append(copy(i, 0, me, sib, src=ins[i]))
            first += [copy(i, 1 + j, me, (*chip, c), src=ins[i]) for j, chip in enumerate(chips)]
        for cp in first:
            cp.start()
        passed = []
        for j, chip in enumerate(chips):
            for i in range(n):
                copy(i, 1 + j, (*chip, c), me).wait_recv()
                fwd = copy(i, 4 + j, (*chip, c), sib)
                fwd.start()
                passed.append(fwd)
        for i in range(n):
            copy(i, 0, sib, me).wait_recv()
        for j, chip in enumerate(chips):
            for i in range(n):
                copy(i, 4 + j, (*chip, 1 - c), me).wait_recv()
        for cp in first + passed:
            cp.wait_send()
        for m in mine:
            m.wait()

    return pl.pallas_call(
        body, name=name,
        in_specs=[ANY_SPEC] * n, out_specs=[ANY_SPEC] * n,
        out_shape=[jax.ShapeDtypeStruct((N_DEV,) + a.shape, a.dtype) for a in arrays],
        scratch_shapes=[pltpu.SemaphoreType.DMA((7 * n,)), pltpu.SemaphoreType.DMA((7 * n,)),
                        pltpu.SemaphoreType.DMA((n,))],
    )(*arrays)


def _tie(name, x, deps):
    def body(*refs):
        pass

    return pl.pallas_call(
        body, name=name, in_specs=[ANY_SPEC] * (1 + len(deps)), out_specs=ANY_SPEC,
        out_shape=jax.ShapeDtypeStruct(x.shape, x.dtype), input_output_aliases={0: 0},
    )(x, *deps)


def _xchg_copies(kind, srcs, lands, suffixes, send_sems, recv_sems):
    x, y, c, _ = _place()
    copies = []
    for i, (src, land, sfx) in enumerate(zip(srcs, lands, suffixes)):
        for k in range(N_DEV - 1):
            r = k + 1
            peer = (1 - x if r & 4 else x, 1 - y if r & 2 else y, 1 - c if r & 1 else c)
            if kind == "gather":
                s_ref, d_ref = src, land.at[(4 * x + 2 * y + c,) + sfx]
            else:
                s_ref, d_ref = src.at[4 * peer[0] + 2 * peer[1] + peer[2]], land.at[(k,) + sfx]
            copies.append(pltpu.make_async_remote_copy(
                src_ref=s_ref, dst_ref=d_ref, send_sem=send_sems.at[i * 7 + k], recv_sem=recv_sems.at[i * 7 + k],
                device_id=peer, device_id_type=MESH))
    return copies


def _xchg_start(name, kind, srcs, lands, suffixes=None):
    n = len(srcs)
    suffixes = suffixes or [()] * n

    def body(*refs):
        src, land = refs[:n], refs[n:2 * n]
        send_sems, recv_sems, token = refs[2 * n], refs[2 * n + 1], refs[-1]
        for cp in _xchg_copies(kind, src, land, suffixes, send_sems, recv_sems):
            cp.start()
        token[...] = jnp.zeros_like(token)

    arrays = list(srcs) + list(lands)
    outs = pl.pallas_call(
        body, name=name,
        out_shape=(pltpu.SemaphoreType.DMA((7 * n,)), pltpu.SemaphoreType.DMA((7 * n,)),
                   *[pltpu.HBM(a.shape, a.dtype) for a in arrays], jax.ShapeDtypeStruct((8, 128), f32)),
        in_specs=[HBM_SPEC] * (2 * n),
        out_specs=(SEM_SPEC, SEM_SPEC, *[HBM_SPEC] * (2 * n), VMEM_SPEC),
        input_output_aliases={i: 2 + i for i in range(2 * n)},
        compiler_params=pltpu.CompilerParams(has_side_effects=SIDE_EFFECT),
    )(*[pltpu.with_memory_space_constraint(a, pltpu.HBM) for a in arrays])
    return dict(kind=kind, n=n, suffixes=suffixes, send=outs[0], recv=outs[1], srcs=list(outs[2:2 + n]),
                lands=list(outs[2 + n:2 + 2 * n]), token=outs[-1])


def _xchg_wait(name, h, after, lands=None):
    n = h["n"]
    lands = h["lands"] if lands is None else lands

    def body(*refs):
        src, land = refs[:n], refs[n:2 * n]
        for cp in _xchg_copies(h["kind"], src, land, h["suffixes"], refs[2 * n], refs[2 * n + 1]):
            cp.wait_send()
            cp.wait_recv()

    arrays = h["srcs"] + list(lands)
    outs = pl.pallas_call(
        body, name=name,
        out_shape=tuple(pltpu.HBM(a.shape, a.dtype) for a in arrays),
        in_specs=[HBM_SPEC] * (2 * n) + [SEM_SPEC, SEM_SPEC] + [ANY_SPEC] * len(after),
        out_specs=tuple([HBM_SPEC] * (2 * n)),
        input_output_aliases={i: i for i in range(2 * n)},
        compiler_params=pltpu.CompilerParams(has_side_effects=SIDE_EFFECT),
    )(*arrays, h["send"], h["recv"], *after)
    return list(outs[n:])


def _rows(a):
    return a.reshape(-1, a.shape[-1])


def _row_tile(r):
    for tm in (512, 256, 128, 64, 32, 16, 8):
        if r % tm == 0:
            return tm
    return r


def _sum8(name, gathered):
    _, r, n = gathered.shape
    tm = _row_tile(r)

    def body(g_ref, o_ref):
        acc = g_ref[0]
        for k in range(1, N_DEV):
            acc = acc + g_ref[k]
        o_ref[...] = acc

    return pl.pallas_call(
        body, name=name, grid=(r // tm,),
        in_specs=[pl.BlockSpec((N_DEV, tm, n), lambda i: (0, i, 0))],
        out_specs=pl.BlockSpec((tm, n), lambda i: (i, 0)),
        out_shape=jax.ShapeDtypeStruct((r, n), f32),
        compiler_params=_cp("parallel"),
    )(gathered)


def _adamw(name, w, m, v, own, landed=None, slot=None):
    shape = w.shape
    w2, m2, v2 = _rows(w), _rows(m), _rows(v)
    r, n = w2.shape
    tm = _row_tile(r)
    c1 = 1.0 - ADAM_B1 ** ADAM_STEP
    c2 = 1.0 - ADAM_B2 ** ADAM_STEP
    extra = [] if landed is None else [landed.reshape(landed.shape[0], r, n)]
    row = pl.BlockSpec((tm, n), lambda i, *_: (i, 0))
    if slot is None:
        o2, own_spec, scalars = _rows(own), row, []
    else:
        dev, kind = slot
        scalars = [dev.reshape(1).astype(jnp.int32)]
        if kind == "lead":
            o2, own_spec = own.reshape(N_DEV, r, n), pl.BlockSpec((None, tm, n), lambda i, d: (d[0], i, 0))
        elif kind == "rows":
            o2, own_spec = own, pl.BlockSpec((tm, n), lambda i, d: (d[0] * (r // tm) + i, 0))
        else:
            o2, own_spec = own, pl.BlockSpec((tm, n), lambda i, d: (i, d[0]))

    def body(*refs):
        w_ref, m_ref, v_ref, o_ref = refs[len(scalars):len(scalars) + 4]
        refs = refs[len(scalars) + 4:]
        g = o_ref[...]
        if extra:
            for k in range(extra[0].shape[0]):
                g = g + refs[0][k].astype(f32)
        g_ref, d_ref, mn_ref, vn_ref = refs[len(extra):]
        mn = ADAM_B1 * m_ref[...] + (1.0 - ADAM_B1) * g
        vn = ADAM_B2 * v_ref[...] + (1.0 - ADAM_B2) * (g * g)
        g_ref[...] = g
        d_ref[...] = -ADAM_LR * ((mn / c1) / (jnp.sqrt(vn / c2) + ADAM_EPS) + ADAM_WD * w_ref[...])
        mn_ref[...] = mn
        vn_ref[...] = vn

    outs = pl.pallas_call(
        body, name=name,
        grid_spec=pltpu.PrefetchScalarGridSpec(
            num_scalar_prefetch=len(scalars), grid=(r // tm,),
            in_specs=[row] * 3 + [own_spec] + [pl.BlockSpec((e.shape[0], tm, n), lambda i, *_: (0, i, 0)) for e in extra],
            out_specs=[row] * 4),
        out_shape=[jax.ShapeDtypeStruct((r, n), f32)] * 4,
        compiler_params=_cp("parallel"),
    )(*scalars, w2, m2, v2, o2, *extra)
    return [o.reshape(shape) for o in outs]


def _pack(arrays):
    flat = jnp.concatenate([a.reshape(-1).astype(f32) for a in arrays])
    pad = (-flat.shape[0]) % (128 * (512 if flat.shape[0] > 128 * 512 else 8))
    return jnp.pad(flat, (0, pad)).reshape(-1, 128)


def _unpack(packed, shapes):
    flat = packed.reshape(-1)
    out, off = [], 0
    for s in shapes:
        n = math.prod(s)
        out.append(flat[off:off + n].reshape(s))
        off += n
    return out


def _local_step(x, target, w, weights_of, send, last_small, on_loss, nb, s):
    cos, sin = _rope_tables(s)
    g = {}
    ffn_saved = {}
    ffn_bufs = [lax.empty((N_DEV, 2, 2) + shp, f32)
                for shp in ((D_MODEL, FF_SHARD), (D_MODEL, FF_SHARD), (FF_SHARD, D_MODEL))]

    def ffn(xin, l, h, wts):
        y, a, b = _ffn_fwd(f"ffn_fwd_{l}{h}", xin, w["ffn_g"][l][h], *wts)
        ffn_saved[(l, h)] = (xin, a, b, wts)
        return y

    def ffn_back(dy, l, h):
        xin, a, b, wts = ffn_saved[(l, h)]
        dx, dg, hb, dyh, u, da, db = _ffn_dx(f"ffn_dx_{l}{h}", dy, xin, w["ffn_g"][l][h], *wts, a, b)
        g[f"ffn_g_{l}{h}"] = dg
        if (l, h) != (0, 0):
            ffn_bufs[:2], (h1, h3) = _ffn_dw(f"ffn_dw_{l}{h}_w13", hb, [da, db], ffn_bufs[:2], l, h)
            ffn_bufs[2:], (h2,) = _ffn_dw(f"ffn_dw_{l}{h}_w2", u, [dyh], ffn_bufs[2:], l, h)
            return send(f"ffn_{l}{h}", {"ffn_w1": h1, "ffn_w3": h3, "ffn_w2": h2}, dx)
        hb = last_small(g, hb)
        ffn_bufs[:1], (half,) = _ffn_dw("ffn_dw_00_w1", hb, [da], ffn_bufs[:1], l, h)
        hb = send("ffn_00_w1", {"ffn_w1": half}, hb)
        ffn_bufs[1:2], (half,) = _ffn_dw("ffn_dw_00_w3", hb, [db], ffn_bufs[1:2], l, h)
        u = send("ffn_00_w3", {"ffn_w3": half}, u)
        ffn_bufs[2:], (half,) = _ffn_dw("ffn_dw_00_w2", u, [dyh], ffn_bufs[2:], l, h)
        return send("ffn_00_w2", {"ffn_w2": half}, dx)

    def slots(t):
        return t.reshape(N_DEV, D_MODEL // N_DEV, D_MODEL)

    x1 = ffn(x, 0, 0, weights_of(0, [])["ffn"])
    wg = weights_of(1, [x1])
    w_in, w_out = wg["w_in"], wg["w_out"]
    h0b = _norm_fwd("mix_norm_0", x1, w["mix_g"][0], bf16)
    proj = _mm("in_proj", h0b, w_in, "nn", tn=1536)[0]
    o_raw, rprev, mret = _ret_fwd(proj, cos, sin, w["ret_g"], nb, s)
    lru, lru_h = _lru_fwd(proj, w["conv_w"], w["conv_b"], w["lru_w_a"], w["lru_b_a"], w["lru_w_i"], w["lru_b_i"], w["lru_lam"], nb, s)
    merged = _ew("merge", lambda a, b: (jnp.concatenate([a, b], axis=1),), [mret, lru], [(D_MODEL, bf16)])[0]
    x2 = _mm("out_proj", merged, w_out, "nn", extras=[x1], epilogue=lambda acc, r: (acc + r,))[0]
    x3 = ffn(x2, 0, 1, weights_of(2, [x2])["ffn"])
    x4 = ffn(x3, 1, 0, weights_of(3, [x3])["ffn"])
    u = _norm_fwd("mix_norm_1", x4, w["mix_g"][1], f32)
    lbr, lbi, bbr, bbi = _s5_prep(w["s5_lr"], w["s5_li"], w["s5_ldt"], w["s5_bre"], w["s5_bim"])
    lbr_f, lbi_f = lbr.reshape(1, -1), lbi.reshape(1, -1)
    wbr, wbi = _blockdiag(bbr).astype(bf16), _blockdiag(bbi).astype(bf16)
    wcr, wci = _blockdiag(w["s5_cre"]).astype(bf16), _blockdiag(w["s5_cim"]).astype(bf16)
    ygb, ypre, h0s, hrs, his = _s5_fwd(u, lbr_f, lbi_f, wbr, wbi, wcr, wci, w["s5_d"], nb, s)
    wg = weights_of(4, [ygb])
    glu_a, glu_b = wg["glu_a"], wg["glu_b"]
    x5, gp, gq = _glu_fwd(ygb, glu_a, glu_b, x4)
    x6 = ffn(x5, 1, 1, weights_of(5, [x5])["ffn"])
    loss, dx6, g["final_g"] = _final_loss(x6, w["final_g"], target)
    dx6 = on_loss(loss, dx6)

    dx5 = ffn_back(dx6, 1, 1)

    def glu_bwd(d, p, q):
        sg = jax.nn.sigmoid(q)
        return d * sg, d * p * sg * (1.0 - sg)

    dp, dq = _ew("glu_bwd", glu_bwd, [dx5, gp, gq], [(D_MODEL, bf16), (D_MODEL, bf16)])
    dyg = _mm("glu_dy_a", dp, glu_a, "nt")[0]
    dyg = _mm("glu_dy_b", dq, glu_b, "nt", extras=[dyg], epilogue=lambda acc, r: (acc + r,))[0]
    g["glu_a"], ga_half = _mm_tn("glu_dw_a", ygb, dp)
    g["glu_b"], gb_half = _mm_tn("glu_dw_b", ygb, dq)
    dyg = send("glu", {"glu_a": slots(ga_half), "glu_b": slots(gb_half)}, dyg)
    du, dlr, dli, dwbr, dwbi, dwcr, dwci, g["s5_d"] = _s5_bwd(dyg, ypre, u, h0s, hrs, his, lbr_f, lbi_f, wbr, wbi, wcr, wci, w["s5_d"], nb, s)
    g["s5_cre"], g["s5_cim"] = _blockdiag_t(dwcr), _blockdiag_t(dwci)
    g["s5_lr"], g["s5_li"], g["s5_ldt"], g["s5_bre"], g["s5_bim"] = _s5_prep_bwd(
        w["s5_lr"], w["s5_li"], w["s5_ldt"], w["s5_bre"], w["s5_bim"],
        (dlr.reshape(S5_GROUPS, S5_STATE), dli.reshape(S5_GROUPS, S5_STATE), _blockdiag_t(dwbr), _blockdiag_t(dwbi)))
    dx4, g["mix_g_1"] = _norm_bwd("mix_norm_1_bwd", du, x4, w["mix_g"][1], dx5)
    dx3 = ffn_back(dx4, 1, 0)
    dx2 = ffn_back(dx3, 0, 1)
    dmerged = _mm("out_proj_dx", dx2, w_out, "nt")[0]
    g["w_out"], wo_half = _mm_tn("out_proj_dw", merged, dx2)
    dmerged = send("w_out", {"w_out": slots(wo_half)}, dmerged)
    dq_, dk_, dv_, dgate, g["ret_g"] = _ret_bwd(dmerged, o_raw, rprev, proj, cos, sin, w["ret_g"], nb, s)
    (dxl, dgl, g["conv_w"], g["conv_b"], g["lru_w_a"], g["lru_b_a"], g["lru_w_i"], g["lru_b_i"], g["lru_lam"]) = _lru_bwd(
        dmerged, lru_h, proj, w["conv_w"], w["conv_b"], w["lru_w_a"], w["lru_b_a"], w["lru_w_i"], w["lru_b_i"], w["lru_lam"],
        nb, s)
    dproj = _ew("dproj", lambda *p: (jnp.concatenate(p, axis=1),), [dq_, dk_, dv_, dgate, dxl, dgl], [(3072, bf16)])[0]
    dh0 = _mm("in_proj_dx", dproj, w_in, "nt")[0]
    g["w_in"], wi_half = _mm_tn("in_proj_dw", h0b, dproj)
    dh0 = send("w_in", {"w_in": jnp.transpose(wi_half.reshape(D_MODEL, N_DEV, IN_SHARD), (1, 0, 2))}, dh0)
    dx1, g["mix_g_0"] = _norm_bwd("mix_norm_0_bwd", dh0, x1, w["mix_g"][0], dx2)
    dx0 = ffn_back(dx1, 0, 0)
    g["ffn_w1"], g["ffn_w3"], g["ffn_w2"] = ffn_bufs
    return loss, dx0, g


_WEIGHTS = ["ffn_norm_g", "ffn_w1", "ffn_w3", "ffn_w2", "mix_norm_g", "w_in_even", "w_out_even", "ret_norm_g", "conv_w",
            "conv_b", "lru_w_a", "lru_b_a", "lru_w_i", "lru_b_i", "lru_lambda", "s5_lambda_re", "s5_lambda_im", "s5_log_dt",
            "s5_b_re", "s5_b_im", "s5_c_re", "s5_c_im", "s5_d", "glu_w_a", "glu_w_b", "final_norm_g"]
_BIG = ["ffn_w1", "ffn_w3", "ffn_w2", "w_in_even", "w_out_even", "glu_w_a", "glu_w_b"]
_SMALL_SHARDED = ["ffn_norm_g", "conv_w", "s5_d"]
_SMALL = [n for n in _WEIGHTS if n not in _BIG]
_MIDSIZE = ["lru_w_a", "lru_w_i", "s5_b_re", "s5_b_im", "s5_c_re", "s5_c_im"]


def kernel(x, ffn_norm_g, ffn_w1, ffn_w3, ffn_w2, mix_norm_g, w_in_even, w_out_even, ret_norm_g, conv_w, conv_b, lru_w_a, lru_b_a, lru_w_i, lru_b_i, lru_lambda, s5_lambda_re, s5_lambda_im, s5_log_dt, s5_b_re, s5_b_im, s5_c_re, s5_c_im, s5_d, glu_w_a, glu_w_b, final_norm_g, loss_target, m_ffn_norm_g, m_ffn_w1, m_ffn_w3, m_ffn_w2, m_mix_norm_g, m_w_in_even, m_w_out_even, m_ret_norm_g, m_conv_w, m_conv_b, m_lru_w_a, m_lru_b_a, m_lru_w_i, m_lru_b_i, m_lru_lambda, m_s5_lambda_re, m_s5_lambda_im, m_s5_log_dt, m_s5_b_re, m_s5_b_im, m_s5_c_re, m_s5_c_im, m_s5_d, m_glu_w_a, m_glu_w_b, m_final_norm_g, v_ffn_norm_g, v_ffn_w1, v_ffn_w3, v_ffn_w2, v_mix_norm_g, v_w_in_even, v_w_out_even, v_ret_norm_g, v_conv_w, v_conv_b, v_lru_w_a, v_lru_b_a, v_lru_w_i, v_lru_b_i, v_lru_lambda, v_s5_lambda_re, v_s5_lambda_im, v_s5_log_dt, v_s5_b_re, v_s5_b_im, v_s5_c_re, v_s5_c_im, v_s5_d, v_glu_w_a, v_glu_w_b, v_final_norm_g):
    a = dict(locals())
    nb, s, d = x.shape
    dev = 4 * lax.axis_index("x") + 2 * lax.axis_index("y") + lax.axis_index("c")

    def ffn_shards(l, h):
        extra = FF_PAD - FF_SHARD
        return [jnp.pad(ffn_w1[l, h].astype(bf16), ((0, 0), (0, extra))), jnp.pad(ffn_w3[l, h].astype(bf16), ((0, 0), (0, extra))),
                jnp.pad(ffn_w2[l, h].astype(bf16), ((0, extra), (0, 0)))]

    first = _all_gather("ag_first", ffn_shards(0, 0) + [_pack([ffn_norm_g, conv_w, s5_d])])
    sm = first[3].reshape(N_DEV, -1)
    ffn_g_full = jnp.transpose(sm[:, :512].reshape(N_DEV, 2, 2, 128), (1, 2, 0, 3)).reshape(2, 2, D_MODEL)
    conv_w_full = jnp.transpose(sm[:, 512:768].reshape(N_DEV, 4, 64), (1, 0, 2)).reshape(4, LRU_WIDTH)
    s5_d_full = sm[:, 768:896].reshape(1, D_MODEL)

    ag_src = [None, [w_in_even[0].astype(bf16), w_out_even[0].astype(bf16)], ffn_shards(0, 1), ffn_shards(1, 0),
              [glu_w_a[0].astype(bf16), glu_w_b[0].astype(bf16)], ffn_shards(1, 1)]
    ag, token = [None], first[0]
    for k, grp in enumerate(ag_src):
        if grp is None:
            continue
        grp[0] = _tie(f"tie_ag_{k}", grp[0], [token])
        lands = [lax.dynamic_update_index_in_dim(lax.empty((N_DEV,) + t.shape, bf16), t, dev, 0) for t in grp]
        ag.append(_xchg_start(f"ag_start_{k}", "gather", grp, lands))
        token = ag[-1]["token"]

    def weights_of(k, after):
        if k == 0:
            return {"ffn": [first[0], _tie("tie_ag_started", first[1], [h["token"] for h in ag[1:]]), first[2]]}
        got = _xchg_wait(f"ag_wait_{k}", ag[k], after)
        if k == 1:
            return {"w_in": jnp.transpose(got[0], (1, 0, 2)).reshape(D_MODEL, N_DEV * IN_SHARD),
                    "w_out": got[1].reshape(D_MODEL, D_MODEL)}
        if k == 4:
            return {"glu_a": got[0].reshape(D_MODEL, D_MODEL), "glu_b": got[1].reshape(D_MODEL, D_MODEL)}
        return {"ffn": got}

    ffn_lands = [lax.empty((N_DEV - 1, 2, 2) + shp, bf16)
                 for shp in ((D_MODEL, FF_SHARD), (D_MODEL, FF_SHARD), (FF_SHARD, D_MODEL))]
    rs = []

    ffn_names = ("ffn_w1", "ffn_w3", "ffn_w2")

    def send(group, arrays, carry):
        srcs = list(arrays.values())
        if group.startswith("ffn_"):
            which = [ffn_names.index(n) for n in arrays]
            sfx = [(int(group[4]), int(group[5]))] * len(which)
            h = _xchg_start("rs_start_" + group, "scatter", srcs, [ffn_lands[k] for k in which], sfx)
            for k, land in zip(which, h["lands"]):
                ffn_lands[k] = land
        else:
            h = _xchg_start("rs_start_" + group, "scatter", srcs,
                            [lax.empty((N_DEV - 1,) + t.shape[1:], bf16) for t in srcs])
        rs.append((group, list(arrays), h))
        return _tie("tie_" + group, carry, [h["token"]])

    w = {
        "ffn_g": [[ffn_g_full[l, h].reshape(1, D_MODEL) for h in range(2)] for l in range(2)],
        "mix_g": [mix_norm_g[0:1], mix_norm_g[1:2]],
        "ret_g": ret_norm_g, "conv_w": conv_w_full, "conv_b": conv_b,
        "lru_w_a": lru_w_a[0], "lru_b_a": lru_b_a, "lru_w_i": lru_w_i[0], "lru_b_i": lru_b_i, "lru_lam": lru_lambda,
        "s5_lr": s5_lambda_re[0], "s5_li": s5_lambda_im[0], "s5_ldt": s5_log_dt.reshape(S5_GROUPS, 1),
        "s5_bre": jnp.swapaxes(s5_b_re[0], 1, 2), "s5_bim": jnp.swapaxes(s5_b_im[0], 1, 2),
        "s5_cre": s5_c_re[0], "s5_cim": s5_c_im[0], "s5_d": s5_d_full,
        "final_g": final_norm_g.reshape(1, D_MODEL),
    }

    small_grads = {}

    def last_small(g, carry):
        part = _small_partials(g)
        mine = _pack([part[n] for n in _SMALL])
        land = lax.dynamic_update_index_in_dim(lax.empty((N_DEV,) + mine.shape, f32), mine, dev, 0)
        h = _xchg_start("ag_start_small_grads", "gather", [mine], [land])
        small_grads.update(h=h, shapes=[part[n].shape for n in _SMALL])
        return _tie("tie_small_grads", carry, [h["token"]])

    total_loss = []

    def on_loss(part, carry):
        total_loss.append(lax.psum(part[0, 0], ("x", "y", "c")))
        return _tie("tie_loss", carry, [jnp.broadcast_to(total_loss[0], (8, 128))])

    _, dx, g = _local_step(x.reshape(nb * s, d), loss_target.reshape(nb * s, d), w, weights_of, send, last_small,
                           on_loss, nb, s)
    loss = total_loss[0]
    (gath,) = _xchg_wait("ag_wait_small_grads", small_grads["h"], [dx])
    full = dict(zip(_SMALL, _unpack(_sum8("sum_small_grads", gath), small_grads["shapes"])))
    for n in _SMALL_SHARDED:
        width = a[n].shape[-1]
        full[n] = lax.dynamic_slice_in_dim(full[n], dev * width, width, axis=full[n].ndim - 1)
    res = {n: _adamw("adamw_" + n, a[n], a["m_" + n], a["v_" + n], full[n]) for n in _MIDSIZE}
    tiny = [n for n in _SMALL if n not in _MIDSIZE]
    shapes = [a[n].shape for n in tiny]
    packed = _adamw("adamw_small", _pack([a[n] for n in tiny]), _pack([a["m_" + n] for n in tiny]),
                    _pack([a["v_" + n] for n in tiny]), _pack([full[n] for n in tiny]))
    res.update({n: vals for n, vals in zip(tiny, zip(*[_unpack(p, shapes) for p in packed]))})
    return _finish(a, g, dx, loss, res, packed, rs, ffn_lands, dev, nb, s, d)


def _small_partials(g):
    return {
        "ffn_norm_g": jnp.stack([jnp.stack([g[f"ffn_g_{l}{h}"][0] for h in range(2)]) for l in range(2)]),
        "mix_norm_g": jnp.concatenate([g["mix_g_0"], g["mix_g_1"]], axis=0),
        "ret_norm_g": g["ret_g"], "conv_w": g["conv_w"][None], "conv_b": g["conv_b"],
        "lru_w_a": g["lru_w_a"][None], "lru_b_a": g["lru_b_a"], "lru_w_i": g["lru_w_i"][None], "lru_b_i": g["lru_b_i"],
        "lru_lambda": g["lru_lam"], "s5_lambda_re": g["s5_lr"][None], "s5_lambda_im": g["s5_li"][None],
        "s5_log_dt": g["s5_ldt"].reshape(1, S5_GROUPS),
        "s5_b_re": jnp.swapaxes(g["s5_bre"], 1, 2)[None], "s5_b_im": jnp.swapaxes(g["s5_bim"], 1, 2)[None],
        "s5_c_re": g["s5_cre"][None], "s5_c_im": g["s5_cim"][None], "s5_d": g["s5_d"], "final_norm_g": g["final_g"][0],
    }


def _finish(a, g, dx, loss, res, packed, rs, ffn_lands, dev, nb, s, d):
    landed = {}
    for group, names, h in rs:
        if not group.startswith("ffn_"):
            landed.update(zip(names, _xchg_wait("rs_wait_" + group, h, [dx])))
    kinds = {"ffn_w1": "lead", "ffn_w3": "lead", "ffn_w2": "lead", "w_in": "cols", "w_out": "rows", "glu_a": "rows",
             "glu_b": "rows"}

    def update(n, short):
        res[n] = _adamw("adamw_" + n, a[n], a["m_" + n], a["v_" + n], g[short],
                        landed[short].reshape((N_DEV - 1,) + a[n].shape), slot=(dev, kinds[short]))

    for n, short in zip(_BIG[3:], ("w_in", "w_out", "glu_a", "glu_b")):
        update(n, short)
    after = [dx, packed[0]] + [res[n][0] for n in _BIG[3:] + _MIDSIZE]
    ffn_names = ("ffn_w1", "ffn_w3", "ffn_w2")
    for group, names, h in rs:
        if group.startswith("ffn_") and len(names) == 3:
            ffn_lands[:] = _xchg_wait("rs_wait_" + group, h, after, ffn_lands)
    for k, n in enumerate(ffn_names):
        for group, names, h in rs:
            if group.startswith("ffn_") and names == [n]:
                (ffn_lands[k],) = _xchg_wait("rs_wait_" + group, h, after, [ffn_lands[k]])
        landed[n] = ffn_lands[k]
        update(n, n)
        after = after + [res[n][0]]

    out = [loss, dx.reshape(nb, s, d)]
    for k in range(4):
        out += [res[n][k] for n in _WEIGHTS]
    return tuple(out)
```

```python
import math

import numpy as np
import jax
import jax.numpy as jnp
from jax import lax
from jax.experimental import pallas as pl
from jax.experimental.pallas import tpu as pltpu

f32 = jnp.float32
bf16 = jnp.bfloat16

D_MODEL = 1024
N_DEV = 8
EPS = 1e-6
RET_HEADS = 4
HEAD_DIM = 128
RET_WIDTH = 512
RET_CHUNK = 128
ROPE_BASE = 10000.0
LRU_WIDTH = 512
LRU_BLOCKS = 4
LRU_C = 8.0
S5_GROUP = 16
S5_GROUPS = 64
S5_STATE = 64
S5_CHUNK = 1024
S5_BLOCKS = 8
S5_BLOCK_STATES = 512
SUBLANES = 8
D_FF = 2816
FF_SHARD = D_FF // N_DEV
FF_PAD = 384
IN_SHARD = 3072 // N_DEV
ADAM_LR = 0.001
ADAM_B1 = 0.9
ADAM_B2 = 0.999
ADAM_EPS = 1e-08
ADAM_WD = 0.01
ADAM_STEP = 10

VMEM_LIMIT = 56 * 1024 * 1024
VMEM_SPEC = pl.BlockSpec(memory_space=pltpu.VMEM)
ANY_SPEC = pl.BlockSpec(memory_space=pl.ANY)
HBM_SPEC = pl.BlockSpec(memory_space=pltpu.HBM)
SEM_SPEC = pl.BlockSpec(memory_space=pltpu.SEMAPHORE)
SIDE_EFFECT = pltpu.SideEffectType.DATAFLOW_SIDE_EFFECTING
MESH = pl.DeviceIdType.MESH


def _cp(*sem):
    return pltpu.CompilerParams(dimension_semantics=sem, vmem_limit_bytes=VMEM_LIMIT)


def _nn(a, b):
    return jnp.dot(a, b, preferred_element_type=f32)


def _nt(a, b):
    return lax.dot_general(a, b, (((1,), (1,)), ((), ())), preferred_element_type=f32)


def _tn(a, b):
    return lax.dot_general(a, b, (((0,), (0,)), ((), ())), preferred_element_type=f32)


def _rms_fwd(x, g):
    r = lax.rsqrt(jnp.mean(x * x, axis=-1, keepdims=True) + EPS)
    xn = x * r
    return xn * g, xn, r


def _rms_bwd(dh, xn, r, g):
    dxn = dh * g
    dx = r * (dxn - xn * jnp.mean(dxn * xn, axis=-1, keepdims=True))
    dg = jnp.sum(dh * xn, axis=0, keepdims=True)
    return dx, dg


def _shift_dn(v, d, row, fill=0.0):
    return jnp.where(row >= d, pltpu.roll(v, d, 0), fill)


def _shift_up(v, d, row, fill=0.0):
    n = v.shape[0]
    return jnp.where(row < n - d, pltpu.roll(v, n - d, 0), fill)


def _ew(name, fn, ins, outs, tm=512):
    t = ins[0].shape[0]
    n_in = len(ins)

    def body(*refs):
        res = fn(*[r[...] for r in refs[:n_in]])
        for o, v in zip(refs[n_in:], res):
            o[...] = v.astype(o.dtype)

    return pl.pallas_call(
        body, name=name, grid=(t // tm,),
        in_specs=[pl.BlockSpec((tm, a.shape[1]), lambda i: (i, 0)) for a in ins],
        out_specs=[pl.BlockSpec((tm, n), lambda i: (i, 0)) for n, _ in outs],
        out_shape=[jax.ShapeDtypeStruct((t, n), dt) for n, dt in outs],
        compiler_params=_cp("parallel"),
    )(*ins)


def _mm(name, x, w, kind, extras=(), epilogue=None, outs=None, tm=512, tn=1024):
    t = x.shape[0]
    n = w.shape[1] if kind == "nn" else w.shape[0]
    tn = min(tn, n)
    outs = outs or [f32]
    n_ex = len(extras)

    def body(x_ref, w_ref, *refs):
        xb = x_ref[...].astype(bf16)
        acc = _nn(xb, w_ref[...]) if kind == "nn" else _nt(xb, w_ref[...])
        res = epilogue(acc, *[r[...] for r in refs[:n_ex]]) if epilogue else (acc,)
        for o, v in zip(refs[n_ex:], res):
            o[...] = v.astype(o.dtype)

    w_spec = (pl.BlockSpec((w.shape[0], tn), lambda i, j: (0, j)) if kind == "nn"
              else pl.BlockSpec((tn, w.shape[1]), lambda i, j: (j, 0)))
    tile = pl.BlockSpec((tm, tn), lambda i, j: (i, j))
    return pl.pallas_call(
        body, name=name, grid=(t // tm, n // tn),
        in_specs=[pl.BlockSpec((tm, x.shape[1]), lambda i, j: (i, 0)), w_spec] + [tile] * n_ex,
        out_specs=[tile] * len(outs),
        out_shape=[jax.ShapeDtypeStruct((t, n), dt) for dt in outs],
        compiler_params=_cp("parallel", "parallel"),
    )(x, w, *extras)


def _mm_tn(name, x, y, tk=1024, tn=1024, tt=1024):
    t, k = x.shape
    n = y.shape[1]
    tk, tn, tt = min(tk, k), min(tn, n), min(tt, t)

    def body(x_ref, y_ref, o_ref, ob_ref):
        @pl.when(pl.program_id(2) == 0)
        def _():
            o_ref[...] = jnp.zeros_like(o_ref)
        o_ref[...] += _tn(x_ref[...].astype(bf16), y_ref[...].astype(bf16))

        @pl.when(pl.program_id(2) == pl.num_programs(2) - 1)
        def _():
            ob_ref[...] = o_ref[...].astype(bf16)

    out = pl.BlockSpec((tk, tn), lambda i, j, s: (i, j))
    return pl.pallas_call(
        body, name=name, grid=(k // tk, n // tn, t // tt),
        in_specs=[pl.BlockSpec((tt, tk), lambda i, j, s: (s, i)), pl.BlockSpec((tt, tn), lambda i, j, s: (s, j))],
        out_specs=[out, out],
        out_shape=[jax.ShapeDtypeStruct((k, n), f32), jax.ShapeDtypeStruct((k, n), bf16)],
        compiler_params=_cp("parallel", "parallel", "arbitrary"),
    )(x, y)


def _norm_fwd(name, x, g, dtype, tm=512):
    t, d = x.shape

    def body(x_ref, g_ref, h_ref):
        h_ref[...] = _rms_fwd(x_ref[...], g_ref[...])[0].astype(dtype)

    row = pl.BlockSpec((tm, d), lambda i: (i, 0))
    return pl.pallas_call(
        body, name=name, grid=(t // tm,),
        in_specs=[row, pl.BlockSpec((1, d), lambda i: (0, 0))],
        out_specs=row, out_shape=jax.ShapeDtypeStruct((t, d), dtype),
        compiler_params=_cp("parallel"),
    )(x, g)


def _norm_bwd(name, dh, x, g, dres, tm=512):
    t, d = x.shape

    def body(dh_ref, x_ref, g_ref, dres_ref, dx_ref, dg_ref):
        gv = g_ref[...]
        _, xn, r = _rms_fwd(x_ref[...], gv)
        dx, dg = _rms_bwd(dh_ref[...], xn, r, gv)
        dx_ref[...] = dres_ref[...] + dx

        @pl.when(pl.program_id(0) == 0)
        def _():
            dg_ref[...] = jnp.zeros_like(dg_ref)
        dg_ref[...] += dg

    row = pl.BlockSpec((tm, d), lambda i: (i, 0))
    vec = pl.BlockSpec((1, d), lambda i: (0, 0))
    return pl.pallas_call(
        body, name=name, grid=(t // tm,),
        in_specs=[row, row, vec, row],
        out_specs=[row, vec],
        out_shape=[jax.ShapeDtypeStruct((t, d), f32), jax.ShapeDtypeStruct((1, d), f32)],
        compiler_params=_cp("arbitrary"),
    )(dh, x, g, dres)


def _final_loss(x, g, target, tm=512):
    t, d = x.shape

    def body(x_ref, g_ref, t_ref, loss_ref, dx_ref, dg_ref):
        gv = g_ref[...]
        y, xn, r = _rms_fwd(x_ref[...], gv)
        err = y - t_ref[...]
        dy = err * (1.0 / d)
        dx, dg = _rms_bwd(dy, xn, r, gv)
        dx_ref[...] = dx

        @pl.when(pl.program_id(0) == 0)
        def _():
            dg_ref[...] = jnp.zeros_like(dg_ref)
            loss_ref[...] = jnp.zeros_like(loss_ref)
        dg_ref[...] += dg
        loss_ref[...] += jnp.full((1, 128), 0.5 / d, f32) * jnp.sum(err * err)

    row = pl.BlockSpec((tm, d), lambda i: (i, 0))
    vec = pl.BlockSpec((1, d), lambda i: (0, 0))
    return pl.pallas_call(
        body, name="final_loss", grid=(t // tm,),
        in_specs=[row, vec, row],
        out_specs=[pl.BlockSpec((1, 128), lambda i: (0, 0)), row, vec],
        out_shape=[jax.ShapeDtypeStruct((1, 128), f32), jax.ShapeDtypeStruct((t, d), f32),
                   jax.ShapeDtypeStruct((1, d), f32)],
        compiler_params=_cp("arbitrary"),
    )(x, g, target)


def _load_ffn_weights(hbm_refs, vmem_refs, sems):
    @pl.when(pl.program_id(0) == 0)
    def _():
        copies = []
        for k, (src, dst) in enumerate(zip(hbm_refs, vmem_refs)):
            for j in range(N_DEV):
                half = pl.ds((j % 2) * FF_PAD, FF_PAD)
                window = dst.at[j // 2, half, :] if k == 2 else dst.at[j // 2, :, half]
                copies.append(pltpu.make_async_copy(src.at[j], window, sems.at[k * N_DEV + j]))
        for cp in copies:
            cp.start()
        for cp in copies:
            cp.wait()


def _ffn_weight_scratch(nj, d, ff):
    return [pltpu.VMEM((nj, d, ff), bf16), pltpu.VMEM((nj, d, ff), bf16), pltpu.VMEM((nj, ff, d), bf16),
            pltpu.SemaphoreType.DMA((3 * N_DEV,))]


def _ffn_fwd(name, x, g, w1, w3, w2, tm=512):
    t, d = x.shape
    nj, ff = N_DEV // 2, 2 * FF_PAD

    def body(x_ref, g_ref, w1_hbm, w3_hbm, w2_hbm, y_ref, a_ref, b_ref, w1_ref, w3_ref, w2_ref, sems):
        _load_ffn_weights((w1_hbm, w3_hbm, w2_hbm), (w1_ref, w3_ref, w2_ref), sems)
        xv = x_ref[...]
        h, _, _ = _rms_fwd(xv, g_ref[...])
        hb = h.astype(bf16)
        acc = jnp.zeros((tm, d), f32)
        for j in range(nj):
            a = _nn(hb, w1_ref[j])
            b = _nn(hb, w3_ref[j])
            a_ref[j] = a.astype(bf16)
            b_ref[j] = b.astype(bf16)
            u = (a * jax.nn.sigmoid(a) * b).astype(bf16)
            acc = acc + _nn(u, w2_ref[j])
        y_ref[...] = xv + 0.5 * acc

    row = pl.BlockSpec((tm, d), lambda i: (i, 0))
    mid = pl.BlockSpec((nj, tm, ff), lambda i: (0, i, 0))
    return pl.pallas_call(
        body, name=name, grid=(t // tm,),
        in_specs=[row, pl.BlockSpec((1, d), lambda i: (0, 0)), ANY_SPEC, ANY_SPEC, ANY_SPEC],
        out_specs=[row, mid, mid],
        out_shape=[jax.ShapeDtypeStruct((t, d), f32), jax.ShapeDtypeStruct((nj, t, ff), bf16),
                   jax.ShapeDtypeStruct((nj, t, ff), bf16)],
        scratch_shapes=_ffn_weight_scratch(nj, d, ff),
        compiler_params=_cp("arbitrary"),
    )(x, g, w1, w3, w2)


def _ffn_dx(name, dy, x, g, w1, w3, w2, a, b, tm=256):
    t, d = x.shape
    nj, ff = N_DEV // 2, 2 * FF_PAD

    def body(dy_ref, x_ref, g_ref, w1_hbm, w3_hbm, w2_hbm, a_ref, b_ref,
             dx_ref, dg_ref, hbt_ref, dyh_ref, ut_ref, da_ref, db_ref, w1_ref, w3_ref, w2_ref, sems):
        _load_ffn_weights((w1_hbm, w3_hbm, w2_hbm), (w1_ref, w3_ref, w2_ref), sems)
        gv = g_ref[...]
        h, xn, r = _rms_fwd(x_ref[...], gv)
        hbt_ref[...] = h.astype(bf16).T
        dyv = dy_ref[...]
        dyh = (0.5 * dyv).astype(bf16)
        dyh_ref[...] = dyh
        dh = jnp.zeros((tm, d), f32)
        dus = [_nt(dyh, w2_ref[j]) for j in range(nj)]
        for j in range(nj):
            av = a_ref[j].astype(f32)
            bv = b_ref[j].astype(f32)
            s = jax.nn.sigmoid(av)
            silu = av * s
            ut_ref[j] = (silu * bv).astype(bf16).T
            du = dus[j]
            dab = (du * bv * (s * (1.0 + av * (1.0 - s)))).astype(bf16)
            dbb = (du * silu).astype(bf16)
            da_ref[j] = dab
            db_ref[j] = dbb
            dh = dh + _nt(dab, w1_ref[j]) + _nt(dbb, w3_ref[j])
        dx, dg = _rms_bwd(dh, xn, r, gv)
        dx_ref[...] = dyv + dx

        @pl.when(pl.program_id(0) == 0)
        def _():
            dg_ref[...] = jnp.zeros_like(dg_ref)
        dg_ref[...] += dg

    row = pl.BlockSpec((tm, d), lambda i: (i, 0))
    vec = pl.BlockSpec((1, d), lambda i: (0, 0))
    mid = pl.BlockSpec((nj, tm, ff), lambda i: (0, i, 0))
    mid_shape = jax.ShapeDtypeStruct((nj, t, ff), bf16)
    return pl.pallas_call(
        body, name=name, grid=(t // tm,),
        in_specs=[row, row, vec, ANY_SPEC, ANY_SPEC, ANY_SPEC, mid, mid],
        out_specs=[row, vec, pl.BlockSpec((d, tm), lambda i: (0, i)), row,
                   pl.BlockSpec((nj, ff, tm), lambda i: (0, 0, i)), mid, mid],
        out_shape=[jax.ShapeDtypeStruct((t, d), f32), jax.ShapeDtypeStruct((1, d), f32),
                   jax.ShapeDtypeStruct((d, t), bf16), jax.ShapeDtypeStruct((t, d), bf16),
                   jax.ShapeDtypeStruct((nj, ff, t), bf16), mid_shape, mid_shape],
        scratch_shapes=_ffn_weight_scratch(nj, d, ff),
        compiler_params=_cp("arbitrary"),
    )(dy, x, g, w1, w3, w2, a, b)


def _ffn_dw(name, xt, ys, bufs, l, h, tt=2048):
    n = len(ys)
    t = ys[0].shape[-2]
    tt = min(tt, t)
    cut_cols = xt.ndim == 2

    def body(x_ref, *refs):
        y_refs, outs, accs = refs[:n], refs[2 * n:4 * n], refs[4 * n:]
        s = pl.program_id(1)
        xv = x_ref[0] if xt.ndim == 3 else x_ref[...]
        for k in range(n):
            prod = _nn(xv, y_refs[k][0] if ys[k].ndim == 3 else y_refs[k][...])

            @pl.when(s == 0)
            def _():
                accs[k][...] = prod

            @pl.when(s > 0)
            def _():
                accs[k][...] += prod

        @pl.when(s == pl.num_programs(1) - 1)
        def _():
            for k in range(n):
                total = accs[k][...]
                for e in range(2):
                    lo = e * FF_PAD
                    part = total[:, lo:lo + FF_SHARD] if cut_cols else total[lo:lo + FF_SHARD, :]
                    outs[k][e] = part
                    outs[n + k][e] = part.astype(bf16)

    x_spec = (pl.BlockSpec((1, xt.shape[1], tt), lambda p, s: (p, 0, s)) if xt.ndim == 3
              else pl.BlockSpec((xt.shape[0], tt), lambda p, s: (0, s)))
    y_specs = [pl.BlockSpec((1, tt, y.shape[2]), lambda p, s: (p, s, 0)) if y.ndim == 3
               else pl.BlockSpec((tt, y.shape[1]), lambda p, s: (s, 0)) for y in ys]
    dims = [b.shape[-2:] for b in bufs]
    outs = pl.pallas_call(
        body, name=name, grid=(N_DEV // 2, t // tt),
        in_specs=[x_spec] + y_specs + [ANY_SPEC] * n,
        out_specs=[pl.BlockSpec((2, None, None, k_, n_), lambda p, s: (p, l, h, 0, 0)) for k_, n_ in dims]
        + [pl.BlockSpec((2, k_, n_), lambda p, s: (p, 0, 0)) for k_, n_ in dims],
        out_shape=[jax.ShapeDtypeStruct(b.shape, b.dtype) for b in bufs]
        + [jax.ShapeDtypeStruct((N_DEV, k_, n_), bf16) for k_, n_ in dims],
        input_output_aliases={1 + n + k: k for k in range(n)},
        scratch_shapes=[pltpu.VMEM((xt.shape[-2], y.shape[-1]), f32) for y in ys],
        compiler_params=_cp("parallel", "arbitrary"),
    )(xt, *ys, *bufs)
    return outs[:n], outs[n:]


_LOG_GAMMA = [float(np.log1p(-np.float32(2.0) ** np.float32(-5.0 - h))) for h in range(RET_HEADS)]


def _ret_consts(h):
    lg = jnp.where(h == 0, _LOG_GAMMA[0], jnp.where(h == 1, _LOG_GAMMA[1],
                   jnp.where(h == 2, _LOG_GAMMA[2], _LOG_GAMMA[3]))).astype(f32)
    c = RET_CHUNK
    r = lax.broadcasted_iota(jnp.int32, (c, c), 0)
    cc = lax.broadcasted_iota(jnp.int32, (c, c), 1)
    decay = jnp.where(r >= cc, jnp.exp(lg * jnp.maximum((r - cc).astype(f32), 0.0)), 0.0)
    pos = lax.broadcasted_iota(jnp.int32, (c, 1), 0).astype(f32)
    kd = jnp.exp(lg * (c - 1.0 - pos))
    qd = jnp.exp(lg * (pos + 1.0))
    gc = jnp.exp(lg * c)
    return decay, kd, qd, gc


def _rope(x, cos, sin):
    return x * cos + pltpu.roll(x, HEAD_DIM // 2, 1) * sin


def _rope_t(g, cos, sin):
    return g * cos + pltpu.roll(g * sin, HEAD_DIM // 2, 1)


def _rope_tables(s):
    half = HEAD_DIM // 2
    inv = ROPE_BASE ** (-jnp.arange(half, dtype=f32) / half)
    ang = jnp.arange(s, dtype=f32)[:, None] * inv[None, :]
    cos, sin = jnp.cos(ang), jnp.sin(ang)
    return jnp.concatenate([cos, cos], axis=1), jnp.concatenate([-sin, sin], axis=1)


def _head_ln(o):
    mu = jnp.mean(o, axis=-1, keepdims=True)
    oc = o - mu
    rs = lax.rsqrt(jnp.mean(oc * oc, axis=-1, keepdims=True) + EPS)
    return oc * rs, rs


def _ret_fwd(proj, cos, sin, ret_g, nb, s):
    c = RET_CHUNK
    nc = s // c
    t = nb * s
    scale = HEAD_DIM ** -0.5

    def body(q_ref, k_ref, v_ref, gate_ref, cos_ref, sin_ref, g_ref, o_ref, rprev_ref, m_ref):
        decay, kd, qd, gc = _ret_consts(pl.program_id(0))
        gv = g_ref[...]

        def chunk(b, n, rv):
            rows = pl.ds(pl.multiple_of(b * s + n * c, c), c)
            pos = pl.ds(pl.multiple_of(n * c, c), c)
            cs, sn = cos_ref[pos, :], sin_ref[pos, :]
            q = _rope(q_ref[rows, :], cs, sn)
            k = _rope(k_ref[rows, :], cs, sn) * scale
            vb = v_ref[rows, :].astype(bf16)
            sc = _nt(q.astype(bf16), k.astype(bf16)) * decay
            rprev_ref[b, n] = rv
            o = _nn(sc.astype(bf16), vb) + _nn((q * qd).astype(bf16), rv.astype(bf16))
            o_ref[rows, :] = o
            y, _ = _head_ln(o)
            gate = gate_ref[rows, :]
            m_ref[rows, :] = y * gv * (gate * jax.nn.sigmoid(gate))
            return rv * gc + _tn((k * kd).astype(bf16), vb)

        def step(n, carry):
            return tuple(chunk(b, n, carry[b]) for b in range(nb))

        lax.fori_loop(0, nc, step, (jnp.zeros((HEAD_DIM, HEAD_DIM), f32),) * nb)

    def col(off):
        return pl.BlockSpec((t, HEAD_DIM), lambda h: (0, off + h))

    tab = pl.BlockSpec((s, HEAD_DIM), lambda h: (0, 0))
    return pl.pallas_call(
        body, name="ret_fwd", grid=(RET_HEADS,),
        in_specs=[col(0), col(4), col(8), col(12), tab, tab, pl.BlockSpec((1, HEAD_DIM), lambda h: (0, h))],
        out_specs=[col(0), pl.BlockSpec((nb, None, nc, HEAD_DIM, HEAD_DIM), lambda h: (0, h, 0, 0, 0)), col(0)],
        out_shape=[jax.ShapeDtypeStruct((t, RET_WIDTH), f32),
                   jax.ShapeDtypeStruct((nb, RET_HEADS, nc, HEAD_DIM, HEAD_DIM), f32),
                   jax.ShapeDtypeStruct((t, RET_WIDTH), f32)],
        compiler_params=_cp("parallel"),
    )(proj, proj, proj, proj, cos, sin, ret_g)


def _ret_bwd(dmerged, o_raw, rprev, proj, cos, sin, ret_g, nb, s):
    c = RET_CHUNK
    nc = s // c
    t = nb * s
    scale = HEAD_DIM ** -0.5

    def body(dm_ref, o_ref, rprev_ref, q_ref, k_ref, v_ref, gate_ref, cos_ref, sin_ref, g_ref,
             dq_ref, dk_ref, dv_ref, dgate_ref, dg_ref):
        decay, kd, qd, gc = _ret_consts(pl.program_id(0))
        gv = g_ref[...]

        def chunk(b, n, drn, dg):
            rows = pl.ds(pl.multiple_of(b * s + n * c, c), c)
            pos = pl.ds(pl.multiple_of(n * c, c), c)
            cs, sn = cos_ref[pos, :], sin_ref[pos, :]
            q = _rope(q_ref[rows, :], cs, sn)
            k = _rope(k_ref[rows, :], cs, sn) * scale
            qb, kb = q.astype(bf16), k.astype(bf16)
            vb = v_ref[rows, :].astype(bf16)
            sc = _nt(qb, kb) * decay
            y, rs = _head_ln(o_ref[rows, :])
            gate = gate_ref[rows, :]
            sg = jax.nn.sigmoid(gate)
            silu = gate * sg
            dm = dm_ref[rows, :]
            dgate_ref[rows, :] = dm * y * gv * (sg * (1.0 + gate * (1.0 - sg)))
            dyl = dm * gv * silu
            dg = dg + jnp.sum(dm * y * silu, axis=0, keepdims=True)
            do = rs * (dyl - jnp.mean(dyl, axis=-1, keepdims=True) - y * jnp.mean(dyl * y, axis=-1, keepdims=True))
            dob = do.astype(bf16)
            rv = rprev_ref[b, n]
            drb = drn.astype(bf16)
            ds = (_nt(dob, vb) * decay).astype(bf16)
            kdb = (k * kd).astype(bf16)
            qdb = (q * qd).astype(bf16)
            dq_r = _nn(ds, kb) + _nt(dob, rv.astype(bf16)) * qd
            dk_r = _tn(ds, qb) + _nt(vb, drb) * kd
            dv_ref[rows, :] = _tn(sc.astype(bf16), dob) + _nn(kdb, drb)
            dq_ref[rows, :] = _rope_t(dq_r, cs, sn)
            dk_ref[rows, :] = _rope_t(dk_r * scale, cs, sn)
            return drn * gc + _tn(qdb, dob), dg

        def step(i, carry):
            out = [chunk(b, nc - 1 - i, *carry[b]) for b in range(nb)]
            return tuple(out)

        zero = (jnp.zeros((HEAD_DIM, HEAD_DIM), f32), jnp.zeros((1, HEAD_DIM), f32))
        done = lax.fori_loop(0, nc, step, (zero,) * nb)
        dg_ref[...] = sum(dg for _, dg in done)

    def col(off):
        return pl.BlockSpec((t, HEAD_DIM), lambda h: (0, off + h))

    tab = pl.BlockSpec((s, HEAD_DIM), lambda h: (0, 0))
    gsp = pl.BlockSpec((1, HEAD_DIM), lambda h: (0, h))
    out_t = jax.ShapeDtypeStruct((t, RET_WIDTH), f32)
    return pl.pallas_call(
        body, name="ret_bwd", grid=(RET_HEADS,),
        in_specs=[col(0), col(0), pl.BlockSpec((nb, None, nc, HEAD_DIM, HEAD_DIM), lambda h: (0, h, 0, 0, 0)),
                  col(0), col(4), col(8), col(12), tab, tab, gsp],
        out_specs=[col(0), col(0), col(0), col(0), gsp],
        out_shape=[out_t, out_t, out_t, out_t, jax.ShapeDtypeStruct((1, RET_WIDTH), f32)],
        compiler_params=_cp("parallel"),
    )(dmerged, o_raw, rprev, proj, proj, proj, proj, cos, sin, ret_g)


def _neg_expm1(z):
    series = -(z * (1.0 + z * (0.5 + z * (1.0 / 6.0 + z * (1.0 / 24.0)))))
    return jnp.where(z > -0.01, series, 1.0 - jnp.exp(z))


def _lru_gates(xc, pa, pi, lam):
    r = jax.nn.sigmoid(pa)
    i = jax.nn.sigmoid(pi)
    log_a = -LRU_C * r * jax.nn.softplus(-lam)
    a = jnp.exp(log_a)
    bx = jnp.sqrt(_neg_expm1(2.0 * log_a)) * i * xc
    return a, bx


def _scan_rows(a, b, row, up):
    sub = row[:SUBLANES] & (SUBLANES - 1)
    groups = list(range(a.shape[0] // SUBLANES))
    out = [None] * len(groups)
    edge = slice(0, 1) if up else slice(SUBLANES - 1, SUBLANES)
    carry = jnp.zeros((1, a.shape[1]), f32)
    for g in (reversed(groups) if up else groups):
        rows = slice(g * SUBLANES, (g + 1) * SUBLANES)
        xa, xb = a[rows], b[rows]
        d = 1
        while d < SUBLANES:
            keep = (sub < SUBLANES - d) if up else (sub >= d)
            shift = SUBLANES - d if up else d
            xb = xa * jnp.where(keep, pltpu.roll(xb, shift, 0), 0.0) + xb
            xa = xa * jnp.where(keep, pltpu.roll(xa, shift, 0), 1.0)
            d *= 2
        out[g] = xb + xa * carry
        carry = out[g][edge]
    return jnp.concatenate(out, axis=0)


def _scan_fwd(a, b, row):
    return _scan_rows(a, b, row, False)


def _scan_bwd(c, b, row):
    return _scan_rows(c, b, row, True)


def _conv_fwd(x, cw, cb, row):
    return (cb + cw[3:4] * x + cw[2:3] * _shift_dn(x, 1, row) + cw[1:2] * _shift_dn(x, 2, row)
            + cw[0:1] * _shift_dn(x, 3, row))


def _lru_specs(s, order):
    def im(f):
        return (lambda b, g: f(b, g)) if order == "bg" else (lambda g, b: f(b, g))
    seq = lambda off: pl.BlockSpec((s, 128), im(lambda b, g: (b, off + g)))
    vec = pl.BlockSpec((1, 128), im(lambda b, g: (0, g)))
    cw = pl.BlockSpec((4, 128), im(lambda b, g: (0, g)))
    mat = pl.BlockSpec((1, 128, 128), im(lambda b, g: (g, 0, 0)))
    return seq, vec, cw, mat


def _lru_fwd(proj, conv_w, conv_b, w_a, b_a, w_i, b_i, lam, nb, s):
    def body(x_ref, gt_ref, cw_ref, cb_ref, wa_ref, ba_ref, wi_ref, bi_ref, lam_ref, out_ref, h_ref):
        row = lax.broadcasted_iota(jnp.int32, (s, 128), 0)
        xc = _conv_fwd(x_ref[...], cw_ref[...], cb_ref[...], row)
        xcb = xc.astype(bf16)
        pa = _nn(xcb, wa_ref[0].astype(bf16)) + ba_ref[...]
        pi = _nn(xcb, wi_ref[0].astype(bf16)) + bi_ref[...]
        a, bx = _lru_gates(xc, pa, pi, lam_ref[...])
        h = _scan_fwd(a, bx, row)
        h_ref[...] = h
        out_ref[...] = h * jax.nn.gelu(gt_ref[...])

    seq, vec, cw, mat = _lru_specs(s, "bg")
    out = jax.ShapeDtypeStruct((nb * s, LRU_WIDTH), f32)
    return pl.pallas_call(
        body, name="lru_fwd", grid=(nb, LRU_BLOCKS),
        in_specs=[seq(16), seq(20), cw, vec, mat, vec, mat, vec, vec],
        out_specs=[seq(0), seq(0)], out_shape=[out, out],
        compiler_params=_cp("parallel", "parallel"),
    )(proj, proj, conv_w, conv_b, w_a, b_a, w_i, b_i, lam)


def _lru_bwd(dmerged, states, proj, conv_w, conv_b, w_a, b_a, w_i, b_i, lam, nb, s):
    def body(dout_ref, h_ref, x_ref, gt_ref, cw_ref, cb_ref, wa_ref, ba_ref, wi_ref, bi_ref, lam_ref,
             dx_ref, dgt_ref, dcw_ref, dcb_ref, dwa_ref, dba_ref, dwi_ref, dbi_ref, dlam_ref):
        row = lax.broadcasted_iota(jnp.int32, (s, 128), 0)
        x = x_ref[...]
        cwv = cw_ref[...]
        xc = _conv_fwd(x, cwv, cb_ref[...], row)
        xcb = xc.astype(bf16)
        wab, wib = wa_ref[0].astype(bf16), wi_ref[0].astype(bf16)
        pa = _nn(xcb, wab) + ba_ref[...]
        pi = _nn(xcb, wib) + bi_ref[...]
        (a, _), gates_vjp = jax.vjp(_lru_gates, xc, pa, pi, lam_ref[...])
        h = h_ref[...]
        ge, gelu_vjp = jax.vjp(jax.nn.gelu, gt_ref[...])
        dout = dout_ref[...]
        dgt_ref[...] = gelu_vjp(dout * h)[0]
        adj = _scan_bwd(_shift_up(a, 1, row), dout * ge, row)
        dxc, dpa, dpi, dlam = gates_vjp((adj * _shift_dn(h, 1, row), adj))
        dpab, dpib = dpa.astype(bf16), dpi.astype(bf16)
        dxc = dxc + _nt(dpab, wab) + _nt(dpib, wib)
        dx_ref[...] = (cwv[3:4] * dxc + cwv[2:3] * _shift_up(dxc, 1, row) + cwv[1:2] * _shift_up(dxc, 2, row)
                       + cwv[0:1] * _shift_up(dxc, 3, row))

        @pl.when(pl.program_id(1) == 0)
        def _():
            for r in (dcw_ref, dcb_ref, dwa_ref, dba_ref, dwi_ref, dbi_ref, dlam_ref):
                r[...] = jnp.zeros_like(r)
        rsum = lambda v: jnp.sum(v, axis=0, keepdims=True)
        dcw_ref[...] += jnp.concatenate([rsum(dxc * _shift_dn(x, 3, row)), rsum(dxc * _shift_dn(x, 2, row)),
                                         rsum(dxc * _shift_dn(x, 1, row)), rsum(dxc * x)], axis=0)
        dcb_ref[...] += rsum(dxc)
        dwa_ref[0] += _tn(xcb, dpab)
        dwi_ref[0] += _tn(xcb, dpib)
        dba_ref[...] += rsum(dpa)
        dbi_ref[...] += rsum(dpi)
        dlam_ref[...] += dlam

    seq, vec, cw, mat = _lru_specs(s, "gb")
    t = nb * s
    vshape = jax.ShapeDtypeStruct((1, LRU_WIDTH), f32)
    mshape = jax.ShapeDtypeStruct((LRU_BLOCKS, 128, 128), f32)
    return pl.pallas_call(
        body, name="lru_bwd", grid=(LRU_BLOCKS, nb),
        in_specs=[seq(4), seq(0), seq(16), seq(20), cw, vec, mat, vec, mat, vec, vec],
        out_specs=[seq(0), seq(0), cw, vec, mat, vec, mat, vec, vec],
        out_shape=[jax.ShapeDtypeStruct((t, LRU_WIDTH), f32), jax.ShapeDtypeStruct((t, LRU_WIDTH), f32),
                   jax.ShapeDtypeStruct((4, LRU_WIDTH), f32), vshape, mshape, vshape, mshape, vshape, vshape],
        compiler_params=_cp("parallel", "arbitrary"),
    )(dmerged, states, proj, proj, conv_w, conv_b, w_a, b_a, w_i, b_i, lam)


def _s5_disc(lr, li, ldt, bre, bim):
    dt = jnp.exp(ldt)
    mag = jnp.exp(lr * dt)
    lbr = mag * jnp.cos(li * dt)
    lbi = mag * jnp.sin(li * dt)
    den = lr * lr + li * li
    nr = lbr - 1.0
    fr = (nr * lr + lbi * li) / den
    fi = (lbi * lr - nr * li) / den
    bbr = fr[:, None, :] * bre - fi[:, None, :] * bim
    bbi = fr[:, None, :] * bim + fi[:, None, :] * bre
    return lbr, lbi, bbr, bbi


def _s5_prep(lr, li, ldt, bre, bim):
    def body(lr_ref, li_ref, ldt_ref, bre_ref, bim_ref, o1, o2, o3, o4):
        o1[...], o2[...], o3[...], o4[...] = _s5_disc(lr_ref[...], li_ref[...], ldt_ref[...], bre_ref[...], bim_ref[...])

    return pl.pallas_call(
        body, name="s5_prep", in_specs=[VMEM_SPEC] * 5, out_specs=[VMEM_SPEC] * 4,
        out_shape=[jax.ShapeDtypeStruct(lr.shape, f32), jax.ShapeDtypeStruct(lr.shape, f32),
                   jax.ShapeDtypeStruct(bre.shape, f32), jax.ShapeDtypeStruct(bre.shape, f32)],
    )(lr, li, ldt, bre, bim)


def _s5_prep_bwd(lr, li, ldt, bre, bim, cts):
    def body(lr_ref, li_ref, ldt_ref, bre_ref, bim_ref, g1, g2, g3, g4, o1, o2, o3, o4, o5):
        _, vjp = jax.vjp(_s5_disc, lr_ref[...], li_ref[...], ldt_ref[...], bre_ref[...], bim_ref[...])
        o1[...], o2[...], o3[...], o4[...], o5[...] = vjp((g1[...], g2[...], g3[...], g4[...]))

    return pl.pallas_call(
        body, name="s5_prep_bwd", in_specs=[VMEM_SPEC] * 9, out_specs=[VMEM_SPEC] * 5,
        out_shape=[jax.ShapeDtypeStruct(v.shape, f32) for v in (lr, li, ldt, bre, bim)],
    )(lr, li, ldt, bre, bim, *cts)


def _cmul(ar, ai, br, bi):
    return ar * br - ai * bi, ar * bi + ai * br


def _s5_pow_table(lr, li, n, row, up):
    ar = jnp.broadcast_to(lr, (n, lr.shape[1]))
    ai = jnp.broadcast_to(li, (n, li.shape[1]))
    shift = _shift_up if up else _shift_dn
    d = 1
    while d < n:
        ar, ai = _cmul(ar, ai, shift(ar, d, row, 1.0), shift(ai, d, row, 0.0))
        d *= 2
    return ar, ai


def _s5_step_factors(lr, li, row, up):
    sub = row & (SUBLANES - 1)
    out, pr, pi, d = [], lr, li, 1
    while d < SUBLANES:
        keep = (sub < SUBLANES - d) if up else (sub >= d)
        out.append((jnp.where(keep, pr, 0.0), jnp.where(keep, pi, 0.0)))
        pr, pi = _cmul(pr, pi, pr, pi)
        d *= 2
    return out


def _s5_scan(br, bi, steps, tab_r, tab_i, cr, ci, up):
    groups = list(range(br.shape[0] // SUBLANES))
    out_r, out_i = [None] * len(groups), [None] * len(groups)
    edge = slice(0, 1) if up else slice(SUBLANES - 1, SUBLANES)
    for g in (reversed(groups) if up else groups):
        rows = slice(g * SUBLANES, (g + 1) * SUBLANES)
        xr, xi = br[rows], bi[rows]
        for k, (mr, mi) in enumerate(steps):
            shift = SUBLANES - (1 << k) if up else 1 << k
            tr, ti = _cmul(mr, mi, pltpu.roll(xr, shift, 0), pltpu.roll(xi, shift, 0))
            xr, xi = xr + tr, xi + ti
        tr, ti = _cmul(tab_r, tab_i, cr, ci)
        hr, hi = xr + tr, xi + ti
        out_r[g], out_i[g] = hr, hi
        cr, ci = hr[edge], hi[edge]
    return jnp.concatenate(out_r, axis=0), jnp.concatenate(out_i, axis=0)


def _s5_specs(t, nb, nc):
    seq = pl.BlockSpec((t, 128), lambda k: (0, k))
    lvec = pl.BlockSpec((1, S5_BLOCK_STATES), lambda k: (0, k))
    dvec = pl.BlockSpec((1, 128), lambda k: (0, k))
    wmat = pl.BlockSpec((1, 128, S5_BLOCK_STATES), lambda k: (k, 0, 0))
    h0 = pl.BlockSpec((nb, None, nc, 2, S5_BLOCK_STATES), lambda k: (0, k, 0, 0, 0))
    states = pl.BlockSpec((t, S5_BLOCK_STATES), lambda k: (0, k))
    return seq, lvec, dvec, wmat, h0, states


def _s5_fwd(u, lbr, lbi, wbr, wbi, wcr, wci, dskip, nb, s):
    ln = min(S5_CHUNK, s)
    nc = s // ln

    def body(u_ref, lr_ref, li_ref, wbr_ref, wbi_ref, wcr_ref, wci_ref, d_ref, yg_ref, y_ref, h0_ref, hr_ref, hi_ref):
        row = lax.broadcasted_iota(jnp.int32, (ln, S5_BLOCK_STATES), 0)
        lr, li = lr_ref[...], li_ref[...]
        pr, pi = _s5_pow_table(lr, li, SUBLANES, row[:SUBLANES], False)
        steps = _s5_step_factors(lr, li, row[:SUBLANES], False)
        dv = d_ref[...]

        def chunk(b, n, h0r, h0i):
            st = pl.multiple_of(b * s + n * ln, ln)
            uc = u_ref[pl.ds(st, ln), :]
            ub = uc.astype(bf16)
            hr, hi = _s5_scan(_nn(ub, wbr_ref[0]), _nn(ub, wbi_ref[0]), steps, pr, pi, h0r, h0i, False)
            h0_ref[b, n, 0:1, :] = h0r
            h0_ref[b, n, 1:2, :] = h0i
            hrb, hib = hr.astype(bf16), hi.astype(bf16)
            hr_ref[pl.ds(st, ln), :] = hrb
            hi_ref[pl.ds(st, ln), :] = hib
            y = _nt(hrb, wcr_ref[0]) - _nt(hib, wci_ref[0]) + dv * uc
            y_ref[pl.ds(st, ln), :] = y
            yg_ref[pl.ds(st, ln), :] = jax.nn.gelu(y).astype(bf16)
            return hr[ln - 1:ln, :], hi[ln - 1:ln, :]

        def step(n, carry):
            return tuple(chunk(b, n, *carry[b]) for b in range(nb))

        z = jnp.zeros((1, S5_BLOCK_STATES), f32)
        lax.fori_loop(0, nc, step, ((z, z),) * nb)

    t = nb * s
    seq, lvec, dvec, wmat, h0, states = _s5_specs(t, nb, nc)
    return pl.pallas_call(
        body, name="s5_fwd", grid=(S5_BLOCKS,),
        in_specs=[seq, lvec, lvec, wmat, wmat, wmat, wmat, dvec],
        out_specs=[seq, seq, h0, states, states],
        out_shape=[jax.ShapeDtypeStruct((t, D_MODEL), bf16), jax.ShapeDtypeStruct((t, D_MODEL), f32),
                   jax.ShapeDtypeStruct((nb, S5_BLOCKS, nc, 2, S5_BLOCK_STATES), f32),
                   jax.ShapeDtypeStruct((t, S5_BLOCKS * S5_BLOCK_STATES), bf16),
                   jax.ShapeDtypeStruct((t, S5_BLOCKS * S5_BLOCK_STATES), bf16)],
        compiler_params=_cp("parallel"),
    )(u, lbr, lbi, wbr, wbi, wcr, wci, dskip)


def _s5_bwd(dyg, y, u, h0, hrs, his, lbr, lbi, wbr, wbi, wcr, wci, dskip, nb, s):
    ln = min(S5_CHUNK, s)
    nc = s // ln

    def body(dyg_ref, y_ref, u_ref, h0_ref, hr_ref, hi_ref, lr_ref, li_ref, wbr_ref, wbi_ref, wcr_ref, wci_ref, d_ref,
             du_ref, dlr_ref, dli_ref, dwbr_ref, dwbi_ref, dwcr_ref, dwci_ref, dd_ref):
        for r in (dlr_ref, dli_ref, dwbr_ref, dwbi_ref, dwcr_ref, dwci_ref, dd_ref):
            r[...] = jnp.zeros_like(r)
        row = lax.broadcasted_iota(jnp.int32, (ln, S5_BLOCK_STATES), 0)
        lr, li = lr_ref[...], li_ref[...]
        qr, qi = _s5_pow_table(lr, -li, SUBLANES, row[:SUBLANES], True)
        steps_up = _s5_step_factors(lr, -li, row[:SUBLANES], True)
        dv = d_ref[...]
        rsum = lambda v: jnp.sum(v, axis=0, keepdims=True)

        def chunk(b, n, gnr, gni):
            st = pl.multiple_of(b * s + n * ln, ln)
            uc = u_ref[pl.ds(st, ln), :]
            ub = uc.astype(bf16)
            h0v = h0_ref[b, n]
            h0r, h0i = h0v[0:1], h0v[1:2]
            hrb, hib = hr_ref[pl.ds(st, ln), :], hi_ref[pl.ds(st, ln), :]
            hr, hi = hrb.astype(f32), hib.astype(f32)
            dy = jax.vjp(jax.nn.gelu, y_ref[pl.ds(st, ln), :])[1](dyg_ref[pl.ds(st, ln), :])[0]
            dyb = dy.astype(bf16)
            dd_ref[...] += rsum(dy * uc)
            gr, gi = _s5_scan(_nn(dyb, wcr_ref[0]), -_nn(dyb, wci_ref[0]), steps_up, qr, qi, gnr, gni, True)
            hpr = jnp.where(row >= 1, pltpu.roll(hr, 1, 0), h0r)
            hpi = jnp.where(row >= 1, pltpu.roll(hi, 1, 0), h0i)
            dlr_ref[...] += rsum(gr * hpr + gi * hpi)
            dli_ref[...] += rsum(gi * hpr - gr * hpi)
            grb, gib = gr.astype(bf16), gi.astype(bf16)
            dwbr_ref[0] += _tn(ub, grb)
            dwbi_ref[0] += _tn(ub, gib)
            dwcr_ref[0] += _tn(dyb, hrb)
            dwci_ref[0] -= _tn(dyb, hib)
            du_ref[pl.ds(st, ln), :] = _nt(grb, wbr_ref[0]) + _nt(gib, wbi_ref[0]) + dv * dy
            return gr[0:1, :], gi[0:1, :]

        def step(i, carry):
            return tuple(chunk(b, nc - 1 - i, *carry[b]) for b in range(nb))

        z = jnp.zeros((1, S5_BLOCK_STATES), f32)
        lax.fori_loop(0, nc, step, ((z, z),) * nb)

    t = nb * s
    seq, lvec, dvec, wmat, h0s, states = _s5_specs(t, nb, nc)
    lshape = jax.ShapeDtypeStruct((1, S5_BLOCKS * S5_BLOCK_STATES), f32)
    wshape = jax.ShapeDtypeStruct((S5_BLOCKS, 128, S5_BLOCK_STATES), f32)
    return pl.pallas_call(
        body, name="s5_bwd", grid=(S5_BLOCKS,),
        in_specs=[seq, seq, seq, h0s, states, states, lvec, lvec, wmat, wmat, wmat, wmat, dvec],
        out_specs=[seq, lvec, lvec, wmat, wmat, wmat, wmat, dvec],
        out_shape=[jax.ShapeDtypeStruct((t, D_MODEL), f32), lshape, lshape, wshape, wshape, wshape, wshape,
                   jax.ShapeDtypeStruct((1, D_MODEL), f32)],
        compiler_params=_cp("parallel"),
    )(dyg, y, u, h0, hrs, his, lbr, lbi, wbr, wbi, wcr, wci, dskip)


def _blockdiag(w):
    w4 = w.reshape(S5_BLOCKS, 8, S5_GROUP, S5_STATE)
    same_group = jnp.eye(8, dtype=bool)[None, :, None, :, None]
    return jnp.where(same_group, w4[:, :, :, None, :], 0.0).reshape(S5_BLOCKS, 128, S5_BLOCK_STATES)


def _blockdiag_t(dw):
    d5 = dw.reshape(S5_BLOCKS, 8, S5_GROUP, 8, S5_STATE)
    diag = jnp.diagonal(d5, axis1=1, axis2=3)
    return jnp.moveaxis(diag, 3, 1).reshape(S5_GROUPS, S5_GROUP, S5_STATE)


def _glu_fwd(ygb, wa, wb, x, tm=512, tn=1024):
    t, d = x.shape

    def body(y_ref, wa_ref, wb_ref, x_ref, o_ref, p_ref, q_ref):
        p = _nn(y_ref[...], wa_ref[...])
        q = _nn(y_ref[...], wb_ref[...])
        p_ref[...] = p
        q_ref[...] = q
        o_ref[...] = x_ref[...] + p * jax.nn.sigmoid(q)

    tile = pl.BlockSpec((tm, tn), lambda i, j: (i, j))
    wsp = pl.BlockSpec((d, tn), lambda i, j: (0, j))
    out = jax.ShapeDtypeStruct((t, d), f32)
    return pl.pallas_call(
        body, name="glu_fwd", grid=(t // tm, d // tn),
        in_specs=[pl.BlockSpec((tm, d), lambda i, j: (i, 0)), wsp, wsp, tile],
        out_specs=[tile, tile, tile], out_shape=[out, out, out],
        compiler_params=_cp("parallel", "parallel"),
    )(ygb, wa, wb, x)


def _place():
    x, y, c = lax.axis_index("x"), lax.axis_index("y"), lax.axis_index("c")
    return x, y, c, [(1 - x, y), (x, 1 - y), (1 - x, 1 - y)]


def _all_gather(name, arrays):
    n = len(arrays)

    def body(*refs):
        ins, outs = refs[:n], refs[n:2 * n]
        send_sems, recv_sems, local_sems = refs[2 * n:]
        x, y, c, chips = _place()
        me, sib = (x, y, c), (x, y, 1 - c)

        def copy(i, k, block, to, src=None):
            dst = outs[i].at[4 * block[0] + 2 * block[1] + block[2]]
            return pltpu.make_async_remote_copy(
                src_ref=dst if src is None else src, dst_ref=dst,
                send_sem=send_sems.at[i * 7 + k], recv_sem=recv_sems.at[i * 7 + k],
                device_id=to, device_id_type=MESH)

        mine = [pltpu.make_async_copy(ins[i], outs[i].at[4 * x + 2 * y + c], local_sems.at[i]) for i in range(n)]
        for m in mine:
            m.start()
        first = []
        for i in range(n):
            first.append(copy(i, 0, me, sib, src=ins[i]))
            first += [copy(i, 1 + j, me, (*chip, c), src=ins[i]) for j, chip in enumerate(chips)]
        for cp in first:
            cp.start()
        passed = []
        for j, chip in enumerate(chips):
            for i in range(n):
                copy(i, 1 + j, (*chip, c), me).wait_recv()
                fwd = copy(i, 4 + j, (*chip, c), sib)
                fwd.start()
                passed.append(fwd)
        for i in range(n):
            copy(i, 0, sib, me).wait_recv()
        for j, chip in enumerate(chips):
            for i in range(n):
                copy(i, 4 + j, (*chip, 1 - c), me).wait_recv()
        for cp in first + passed:
            cp.wait_send()
        for m in mine:
            m.wait()

    return pl.pallas_call(
        body, name=name,
        in_specs=[ANY_SPEC] * n, out_specs=[ANY_SPEC] * n,
        out_shape=[jax.ShapeDtypeStruct((N_DEV,) + a.shape, a.dtype) for a in arrays],
        scratch_shapes=[pltpu.SemaphoreType.DMA((7 * n,)), pltpu.SemaphoreType.DMA((7 * n,)),
                        pltpu.SemaphoreType.DMA((n,))],
    )(*arrays)


def _tie(name, x, deps):
    def body(*refs):
        pass

    return pl.pallas_call(
        body, name=name, in_specs=[ANY_SPEC] * (1 + len(deps)), out_specs=ANY_SPEC,
        out_shape=jax.ShapeDtypeStruct(x.shape, x.dtype), input_output_aliases={0: 0},
    )(x, *deps)


def _xchg_copies(kind, srcs, lands, suffixes, send_sems, recv_sems):
    x, y, c, _ = _place()
    copies = []
    for i, (src, land, sfx) in enumerate(zip(srcs, lands, suffixes)):
        for k in range(N_DEV - 1):
            r = k + 1
            peer = (1 - x if r & 4 else x, 1 - y if r & 2 else y, 1 - c if r & 1 else c)
            if kind == "gather":
                s_ref, d_ref = src, land.at[(4 * x + 2 * y + c,) + sfx]
            else:
                s_ref, d_ref = src.at[4 * peer[0] + 2 * peer[1] + peer[2]], land.at[(k,) + sfx]
            copies.append(pltpu.make_async_remote_copy(
                src_ref=s_ref, dst_ref=d_ref, send_sem=send_sems.at[i * 7 + k], recv_sem=recv_sems.at[i * 7 + k],
                device_id=peer, device_id_type=MESH))
    return copies


def _xchg_start(name, kind, srcs, lands, suffixes=None):
    n = len(srcs)
    suffixes = suffixes or [()] * n

    def body(*refs):
        src, land = refs[:n], refs[n:2 * n]
        send_sems, recv_sems, token = refs[2 * n], refs[2 * n + 1], refs[-1]
        for cp in _xchg_copies(kind, src, land, suffixes, send_sems, recv_sems):
            cp.start()
        token[...] = jnp.zeros_like(token)

    arrays = list(srcs) + list(lands)
    outs = pl.pallas_call(
        body, name=name,
        out_shape=(pltpu.SemaphoreType.DMA((7 * n,)), pltpu.SemaphoreType.DMA((7 * n,)),
                   *[pltpu.HBM(a.shape, a.dtype) for a in arrays], jax.ShapeDtypeStruct((8, 128), f32)),
        in_specs=[HBM_SPEC] * (2 * n),
        out_specs=(SEM_SPEC, SEM_SPEC, *[HBM_SPEC] * (2 * n), VMEM_SPEC),
        input_output_aliases={i: 2 + i for i in range(2 * n)},
        compiler_params=pltpu.CompilerParams(has_side_effects=SIDE_EFFECT),
    )(*[pltpu.with_memory_space_constraint(a, pltpu.HBM) for a in arrays])
    return dict(kind=kind, n=n, suffixes=suffixes, send=outs[0], recv=outs[1], srcs=list(outs[2:2 + n]),
                lands=list(outs[2 + n:2 + 2 * n]), token=outs[-1])


def _xchg_wait(name, h, after, lands=None):
    n = h["n"]
    lands = h["lands"] if lands is None else lands

    def body(*refs):
        src, land = refs[:n], refs[n:2 * n]
        for cp in _xchg_copies(h["kind"], src, land, h["suffixes"], refs[2 * n], refs[2 * n + 1]):
            cp.wait_send()
            cp.wait_recv()

    arrays = h["srcs"] + list(lands)
    outs = pl.pallas_call(
        body, name=name,
        out_shape=tuple(pltpu.HBM(a.shape, a.dtype) for a in arrays),
        in_specs=[HBM_SPEC] * (2 * n) + [SEM_SPEC, SEM_SPEC] + [ANY_SPEC] * len(after),
        out_specs=tuple([HBM_SPEC] * (2 * n)),
        input_output_aliases={i: i for i in range(2 * n)},
        compiler_params=pltpu.CompilerParams(has_side_effects=SIDE_EFFECT),
    )(*arrays, h["send"], h["recv"], *after)
    return list(outs[n:])


def _rows(a):
    return a.reshape(-1, a.shape[-1])


def _row_tile(r):
    for tm in (512, 256, 128, 64, 32, 16, 8):
        if r % tm == 0:
            return tm
    return r


def _sum8(name, gathered):
    _, r, n = gathered.shape
    tm = _row_tile(r)

    def body(g_ref, o_ref):
        acc = g_ref[0]
        for k in range(1, N_DEV):
            acc = acc + g_ref[k]
        o_ref[...] = acc

    return pl.pallas_call(
        body, name=name, grid=(r // tm,),
        in_specs=[pl.BlockSpec((N_DEV, tm, n), lambda i: (0, i, 0))],
        out_specs=pl.BlockSpec((tm, n), lambda i: (i, 0)),
        out_shape=jax.ShapeDtypeStruct((r, n), f32),
        compiler_params=_cp("parallel"),
    )(gathered)


def _adamw(name, w, m, v, own, landed=None, slot=None):
    shape = w.shape
    w2, m2, v2 = _rows(w), _rows(m), _rows(v)
    r, n = w2.shape
    tm = _row_tile(r)
    c1 = 1.0 - ADAM_B1 ** ADAM_STEP
    c2 = 1.0 - ADAM_B2 ** ADAM_STEP
    extra = [] if landed is None else [landed.reshape(landed.shape[0], r, n)]
    row = pl.BlockSpec((tm, n), lambda i, *_: (i, 0))
    if slot is None:
        o2, own_spec, scalars = _rows(own), row, []
    else:
        dev, kind = slot
        scalars = [dev.reshape(1).astype(jnp.int32)]
        if kind == "lead":
            o2, own_spec = own.reshape(N_DEV, r, n), pl.BlockSpec((None, tm, n), lambda i, d: (d[0], i, 0))
        elif kind == "rows":
            o2, own_spec = own, pl.BlockSpec((tm, n), lambda i, d: (d[0] * (r // tm) + i, 0))
        else:
            o2, own_spec = own, pl.BlockSpec((tm, n), lambda i, d: (i, d[0]))

    def body(*refs):
        w_ref, m_ref, v_ref, o_ref = refs[len(scalars):len(scalars) + 4]
        refs = refs[len(scalars) + 4:]
        g = o_ref[...]
        if extra:
            for k in range(extra[0].shape[0]):
                g = g + refs[0][k].astype(f32)
        g_ref, d_ref, mn_ref, vn_ref = refs[len(extra):]
        mn = ADAM_B1 * m_ref[...] + (1.0 - ADAM_B1) * g
        vn = ADAM_B2 * v_ref[...] + (1.0 - ADAM_B2) * (g * g)
        g_ref[...] = g
        d_ref[...] = -ADAM_LR * ((mn / c1) / (jnp.sqrt(vn / c2) + ADAM_EPS) + ADAM_WD * w_ref[...])
        mn_ref[...] = mn
        vn_ref[...] = vn

    outs = pl.pallas_call(
        body, name=name,
        grid_spec=pltpu.PrefetchScalarGridSpec(
            num_scalar_prefetch=len(scalars), grid=(r // tm,),
            in_specs=[row] * 3 + [own_spec] + [pl.BlockSpec((e.shape[0], tm, n), lambda i, *_: (0, i, 0)) for e in extra],
            out_specs=[row] * 4),
        out_shape=[jax.ShapeDtypeStruct((r, n), f32)] * 4,
        compiler_params=_cp("parallel"),
    )(*scalars, w2, m2, v2, o2, *extra)
    return [o.reshape(shape) for o in outs]


def _adamw_minor_d(name, w, m, v, own_all, landed, dev, tm=512):
    nl, nh, d, f = w.shape
    wt, mt, vt = (jnp.swapaxes(t, 2, 3) for t in (w, m, v))
    r = nl * nh * d
    per = d // tm
    c1 = 1.0 - ADAM_B1 ** ADAM_STEP
    c2 = 1.0 - ADAM_B2 ** ADAM_STEP

    def body(dev_ref, w_ref, m_ref, v_ref, o_ref, l_ref, g_ref, d_ref, mn_ref, vn_ref):
        g = o_ref[...]
        for k in range(N_DEV - 1):
            g = g + l_ref[k].astype(f32)
        g = g.T
        mn = ADAM_B1 * m_ref[...] + (1.0 - ADAM_B1) * g
        vn = ADAM_B2 * v_ref[...] + (1.0 - ADAM_B2) * (g * g)
        g_ref[...] = g
        d_ref[...] = -ADAM_LR * ((mn / c1) / (jnp.sqrt(vn / c2) + ADAM_EPS) + ADAM_WD * w_ref[...])
        mn_ref[...] = mn
        vn_ref[...] = vn

    par = pl.BlockSpec((None, None, f, tm), lambda i, _: (i // (nh * per), (i // per) % nh, 0, i % per))
    outs = pl.pallas_call(
        body, name=name,
        grid_spec=pltpu.PrefetchScalarGridSpec(
            num_scalar_prefetch=1, grid=(r // tm,),
            in_specs=[par, par, par, pl.BlockSpec((None, tm, f), lambda i, dv: (dv[0], i, 0)),
                      pl.BlockSpec((N_DEV - 1, tm, f), lambda i, _: (0, i, 0))],
            out_specs=[par] * 4),
        out_shape=[jax.ShapeDtypeStruct(wt.shape, f32)] * 4,
        compiler_params=_cp("parallel"),
    )(dev.reshape(1).astype(jnp.int32), wt, mt, vt, own_all.reshape(N_DEV, r, f), landed.reshape(N_DEV - 1, r, f))
    return [jnp.swapaxes(o, 2, 3) for o in outs]


def _pack(arrays):
    flat = jnp.concatenate([a.reshape(-1).astype(f32) for a in arrays])
    pad = (-flat.shape[0]) % (128 * (512 if flat.shape[0] > 128 * 512 else 8))
    return jnp.pad(flat, (0, pad)).reshape(-1, 128)


def _unpack(packed, shapes):
    flat = packed.reshape(-1)
    out, off = [], 0
    for s in shapes:
        n = math.prod(s)
        out.append(flat[off:off + n].reshape(s))
        off += n
    return out


def _local_step(x, target, w, weights_of, send, last_small, on_loss, nb, s):
    cos, sin = _rope_tables(s)
    g = {}
    ffn_saved = {}
    ffn_bufs = [lax.empty((N_DEV, 2, 2) + shp, f32)
                for shp in ((D_MODEL, FF_SHARD), (D_MODEL, FF_SHARD), (FF_SHARD, D_MODEL))]

    def ffn(xin, l, h, wts):
        y, a, b = _ffn_fwd(f"ffn_fwd_{l}{h}", xin, w["ffn_g"][l][h], *wts)
        ffn_saved[(l, h)] = (xin, a, b, wts)
        return y

    def ffn_back(dy, l, h):
        xin, a, b, wts = ffn_saved[(l, h)]
        dx, dg, hb, dyh, u, da, db = _ffn_dx(f"ffn_dx_{l}{h}", dy, xin, w["ffn_g"][l][h], *wts, a, b)
        g[f"ffn_g_{l}{h}"] = dg
        if (l, h) != (0, 0):
            ffn_bufs[:2], (h1, h3) = _ffn_dw(f"ffn_dw_{l}{h}_w13", hb, [da, db], ffn_bufs[:2], l, h)
            ffn_bufs[2:], (h2,) = _ffn_dw(f"ffn_dw_{l}{h}_w2", u, [dyh], ffn_bufs[2:], l, h)
            return send(f"ffn_{l}{h}", {"ffn_w1": h1, "ffn_w3": h3, "ffn_w2": h2}, dx)
        hb = last_small(g, hb)
        ffn_bufs[:1], (half,) = _ffn_dw("ffn_dw_00_w1", hb, [da], ffn_bufs[:1], l, h)
        hb = send("ffn_00_w1", {"ffn_w1": half}, hb)
        ffn_bufs[1:2], (half,) = _ffn_dw("ffn_dw_00_w3", hb, [db], ffn_bufs[1:2], l, h)
        u = send("ffn_00_w3", {"ffn_w3": half}, u)
        ffn_bufs[2:], (half,) = _ffn_dw("ffn_dw_00_w2", u, [dyh], ffn_bufs[2:], l, h)
        return send("ffn_00_w2", {"ffn_w2": half}, dx)

    def slots(t):
        return t.reshape(N_DEV, D_MODEL // N_DEV, D_MODEL)

    x1 = ffn(x, 0, 0, weights_of(0, [])["ffn"])
    wg = weights_of(1, [x1])
    w_in, w_out = wg["w_in"], wg["w_out"]
    h0b = _norm_fwd("mix_norm_0", x1, w["mix_g"][0], bf16)
    proj = _mm("in_proj", h0b, w_in, "nn", tn=1536)[0]
    o_raw, rprev, mret = _ret_fwd(proj, cos, sin, w["ret_g"], nb, s)
    lru, lru_h = _lru_fwd(proj, w["conv_w"], w["conv_b"], w["lru_w_a"], w["lru_b_a"], w["lru_w_i"], w["lru_b_i"], w["lru_lam"], nb, s)
    merged = _ew("merge", lambda a, b: (jnp.concatenate([a, b], axis=1),), [mret, lru], [(D_MODEL, bf16)])[0]
    x2 = _mm("out_proj", merged, w_out, "nn", extras=[x1], epilogue=lambda acc, r: (acc + r,))[0]
    x3 = ffn(x2, 0, 1, weights_of(2, [x2])["ffn"])
    x4 = ffn(x3, 1, 0, weights_of(3, [x3])["ffn"])
    u = _norm_fwd("mix_norm_1", x4, w["mix_g"][1], f32)
    lbr, lbi, bbr, bbi = _s5_prep(w["s5_lr"], w["s5_li"], w["s5_ldt"], w["s5_bre"], w["s5_bim"])
    lbr_f, lbi_f = lbr.reshape(1, -1), lbi.reshape(1, -1)
    wbr, wbi = _blockdiag(bbr).astype(bf16), _blockdiag(bbi).astype(bf16)
    wcr, wci = _blockdiag(w["s5_cre"]).astype(bf16), _blockdiag(w["s5_cim"]).astype(bf16)
    ygb, ypre, h0s, hrs, his = _s5_fwd(u, lbr_f, lbi_f, wbr, wbi, wcr, wci, w["s5_d"], nb, s)
    wg = weights_of(4, [ygb])
    glu_a, glu_b = wg["glu_a"], wg["glu_b"]
    x5, gp, gq = _glu_fwd(ygb, glu_a, glu_b, x4)
    x6 = ffn(x5, 1, 1, weights_of(5, [x5])["ffn"])
    loss, dx6, g["final_g"] = _final_loss(x6, w["final_g"], target)
    dx6 = on_loss(loss, dx6)

    dx5 = ffn_back(dx6, 1, 1)

    def glu_bwd(d, p, q):
        sg = jax.nn.sigmoid(q)
        return d * sg, d * p * sg * (1.0 - sg)

    dp, dq = _ew("glu_bwd", glu_bwd, [dx5, gp, gq], [(D_MODEL, bf16), (D_MODEL, bf16)])
    dyg = _mm("glu_dy_a", dp, glu_a, "nt")[0]
    dyg = _mm("glu_dy_b", dq, glu_b, "nt", extras=[dyg], epilogue=lambda acc, r: (acc + r,))[0]
    g["glu_a"], ga_half = _mm_tn("glu_dw_a", ygb, dp)
    g["glu_b"], gb_half = _mm_tn("glu_dw_b", ygb, dq)
    dyg = send("glu", {"glu_a": slots(ga_half), "glu_b": slots(gb_half)}, dyg)
    du, dlr, dli, dwbr, dwbi, dwcr, dwci, g["s5_d"] = _s5_bwd(dyg, ypre, u, h0s, hrs, his, lbr_f, lbi_f, wbr, wbi, wcr, wci, w["s5_d"], nb, s)
    g["s5_cre"], g["s5_cim"] = _blockdiag_t(dwcr), _blockdiag_t(dwci)
    g["s5_lr"], g["s5_li"], g["s5_ldt"], g["s5_bre"], g["s5_bim"] = _s5_prep_bwd(
        w["s5_lr"], w["s5_li"], w["s5_ldt"], w["s5_bre"], w["s5_bim"],
        (dlr.reshape(S5_GROUPS, S5_STATE), dli.reshape(S5_GROUPS, S5_STATE), _blockdiag_t(dwbr), _blockdiag_t(dwbi)))
    dx4, g["mix_g_1"] = _norm_bwd("mix_norm_1_bwd", du, x4, w["mix_g"][1], dx5)
    dx3 = ffn_back(dx4, 1, 0)
    dx2 = ffn_back(dx3, 0, 1)
    dmerged = _mm("out_proj_dx", dx2, w_out, "nt")[0]
    g["w_out"], wo_half = _mm_tn("out_proj_dw", merged, dx2)
    dmerged = send("w_out", {"w_out": slots(wo_half)}, dmerged)
    dq_, dk_, dv_, dgate, g["ret_g"] = _ret_bwd(dmerged, o_raw, rprev, proj, cos, sin, w["ret_g"], nb, s)
    (dxl, dgl, g["conv_w"], g["conv_b"], g["lru_w_a"], g["lru_b_a"], g["lru_w_i"], g["lru_b_i"], g["lru_lam"]) = _lru_bwd(
        dmerged, lru_h, proj, w["conv_w"], w["conv_b"], w["lru_w_a"], w["lru_b_a"], w["lru_w_i"], w["lru_b_i"], w["lru_lam"],
        nb, s)
    dproj = _ew("dproj", lambda *p: (jnp.concatenate(p, axis=1),), [dq_, dk_, dv_, dgate, dxl, dgl], [(3072, bf16)])[0]
    dh0 = _mm("in_proj_dx", dproj, w_in, "nt")[0]
    g["w_in"], wi_half = _mm_tn("in_proj_dw", h0b, dproj)
    dh0 = send("w_in", {"w_in": jnp.transpose(wi_half.reshape(D_MODEL, N_DEV, IN_SHARD), (1, 0, 2))}, dh0)
    dx1, g["mix_g_0"] = _norm_bwd("mix_norm_0_bwd", dh0, x1, w["mix_g"][0], dx2)
    dx0 = ffn_back(dx1, 0, 0)
    g["ffn_w1"], g["ffn_w3"], g["ffn_w2"] = ffn_bufs
    return loss, dx0, g


_WEIGHTS = ["ffn_norm_g", "ffn_w1", "ffn_w3", "ffn_w2", "mix_norm_g", "w_in_even", "w_out_even", "ret_norm_g", "conv_w",
            "conv_b", "lru_w_a", "lru_b_a", "lru_w_i", "lru_b_i", "lru_lambda", "s5_lambda_re", "s5_lambda_im", "s5_log_dt",
            "s5_b_re", "s5_b_im", "s5_c_re", "s5_c_im", "s5_d", "glu_w_a", "glu_w_b", "final_norm_g"]
_BIG = ["ffn_w1", "ffn_w3", "ffn_w2", "w_in_even", "w_out_even", "glu_w_a", "glu_w_b"]
_SMALL_SHARDED = ["ffn_norm_g", "conv_w", "s5_d"]
_SMALL = [n for n in _WEIGHTS if n not in _BIG]
_MIDSIZE = ["lru_w_a", "lru_w_i", "s5_b_re", "s5_b_im", "s5_c_re", "s5_c_im"]


def kernel(x, ffn_norm_g, ffn_w1, ffn_w3, ffn_w2, mix_norm_g, w_in_even, w_out_even, ret_norm_g, conv_w, conv_b, lru_w_a, lru_b_a, lru_w_i, lru_b_i, lru_lambda, s5_lambda_re, s5_lambda_im, s5_log_dt, s5_b_re, s5_b_im, s5_c_re, s5_c_im, s5_d, glu_w_a, glu_w_b, final_norm_g, loss_target, m_ffn_norm_g, m_ffn_w1, m_ffn_w3, m_ffn_w2, m_mix_norm_g, m_w_in_even, m_w_out_even, m_ret_norm_g, m_conv_w, m_conv_b, m_lru_w_a, m_lru_b_a, m_lru_w_i, m_lru_b_i, m_lru_lambda, m_s5_lambda_re, m_s5_lambda_im, m_s5_log_dt, m_s5_b_re, m_s5_b_im, m_s5_c_re, m_s5_c_im, m_s5_d, m_glu_w_a, m_glu_w_b, m_final_norm_g, v_ffn_norm_g, v_ffn_w1, v_ffn_w3, v_ffn_w2, v_mix_norm_g, v_w_in_even, v_w_out_even, v_ret_norm_g, v_conv_w, v_conv_b, v_lru_w_a, v_lru_b_a, v_lru_w_i, v_lru_b_i, v_lru_lambda, v_s5_lambda_re, v_s5_lambda_im, v_s5_log_dt, v_s5_b_re, v_s5_b_im, v_s5_c_re, v_s5_c_im, v_s5_d, v_glu_w_a, v_glu_w_b, v_final_norm_g):
    a = dict(locals())
    nb, s, d = x.shape
    dev = 4 * lax.axis_index("x") + 2 * lax.axis_index("y") + lax.axis_index("c")

    def ffn_shards(l, h):
        extra = FF_PAD - FF_SHARD
        return [jnp.pad(ffn_w1[l, h].astype(bf16), ((0, 0), (0, extra))), jnp.pad(ffn_w3[l, h].astype(bf16), ((0, 0), (0, extra))),
                jnp.pad(ffn_w2[l, h].astype(bf16), ((0, extra), (0, 0)))]

    first = _all_gather("ag_first", ffn_shards(0, 0) + [_pack([ffn_norm_g, conv_w, s5_d])])
    sm = first[3].reshape(N_DEV, -1)
    ffn_g_full = jnp.transpose(sm[:, :512].reshape(N_DEV, 2, 2, 128), (1, 2, 0, 3)).reshape(2, 2, D_MODEL)
    conv_w_full = jnp.transpose(sm[:, 512:768].reshape(N_DEV, 4, 64), (1, 0, 2)).reshape(4, LRU_WIDTH)
    s5_d_full = sm[:, 768:896].reshape(1, D_MODEL)

    ag_src = [None, [w_in_even[0].astype(bf16), w_out_even[0].astype(bf16)], ffn_shards(0, 1), ffn_shards(1, 0),
              [glu_w_a[0].astype(bf16), glu_w_b[0].astype(bf16)], ffn_shards(1, 1)]
    ag, token = [None], first[0]
    for k, grp in enumerate(ag_src):
        if grp is None:
            continue
        grp[0] = _tie(f"tie_ag_{k}", grp[0], [token])
        lands = [lax.dynamic_update_index_in_dim(lax.empty((N_DEV,) + t.shape, bf16), t, dev, 0) for t in grp]
        ag.append(_xchg_start(f"ag_start_{k}", "gather", grp, lands))
        token = ag[-1]["token"]

    def weights_of(k, after):
        if k == 0:
            return {"ffn": [first[0], _tie("tie_ag_started", first[1], [h["token"] for h in ag[1:]]), first[2]]}
        got = _xchg_wait(f"ag_wait_{k}", ag[k], after)
        if k == 1:
            return {"w_in": jnp.transpose(got[0], (1, 0, 2)).reshape(D_MODEL, N_DEV * IN_SHARD),
                    "w_out": got[1].reshape(D_MODEL, D_MODEL)}
        if k == 4:
            return {"glu_a": got[0].reshape(D_MODEL, D_MODEL), "glu_b": got[1].reshape(D_MODEL, D_MODEL)}
        return {"ffn": got}

    ffn_lands = [lax.empty((N_DEV - 1, 2, 2) + shp, bf16)
                 for shp in ((D_MODEL, FF_SHARD), (D_MODEL, FF_SHARD), (FF_SHARD, D_MODEL))]
    rs = []

    ffn_names = ("ffn_w1", "ffn_w3", "ffn_w2")

    def send(group, arrays, carry):
        srcs = list(arrays.values())
        if group.startswith("ffn_"):
            which = [ffn_names.index(n) for n in arrays]
            sfx = [(int(group[4]), int(group[5]))] * len(which)
            h = _xchg_start("rs_start_" + group, "scatter", srcs, [ffn_lands[k] for k in which], sfx)
            for k, land in zip(which, h["lands"]):
                ffn_lands[k] = land
        else:
            h = _xchg_start("rs_start_" + group, "scatter", srcs,
                            [lax.empty((N_DEV - 1,) + t.shape[1:], bf16) for t in srcs])
        rs.append((group, list(arrays), h))
        return _tie("tie_" + group, carry, [h["token"]])

    w = {
        "ffn_g": [[ffn_g_full[l, h].reshape(1, D_MODEL) for h in range(2)] for l in range(2)],
        "mix_g": [mix_norm_g[0:1], mix_norm_g[1:2]],
        "ret_g": ret_norm_g, "conv_w": conv_w_full, "conv_b": conv_b,
        "lru_w_a": lru_w_a[0], "lru_b_a": lru_b_a, "lru_w_i": lru_w_i[0], "lru_b_i": lru_b_i, "lru_lam": lru_lambda,
        "s5_lr": s5_lambda_re[0], "s5_li": s5_lambda_im[0], "s5_ldt": s5_log_dt.reshape(S5_GROUPS, 1),
        "s5_bre": jnp.swapaxes(s5_b_re[0], 1, 2), "s5_bim": jnp.swapaxes(s5_b_im[0], 1, 2),
        "s5_cre": s5_c_re[0], "s5_cim": s5_c_im[0], "s5_d": s5_d_full,
        "final_g": final_norm_g.reshape(1, D_MODEL),
    }

    small_grads = {}

    def last_small(g, carry):
        part = _small_partials(g)
        mine = _pack([part[n] for n in _SMALL])
        land = lax.dynamic_update_index_in_dim(lax.empty((N_DEV,) + mine.shape, f32), mine, dev, 0)
        h = _xchg_start("ag_start_small_grads", "gather", [mine], [land])
        small_grads.update(h=h, shapes=[part[n].shape for n in _SMALL])
        return _tie("tie_small_grads", carry, [h["token"]])

    total_loss = []

    def on_loss(part, carry):
        total_loss.append(lax.psum(part[0, 0], ("x", "y", "c")))
        return _tie("tie_loss", carry, [jnp.broadcast_to(total_loss[0], (8, 128))])

    _, dx, g = _local_step(x.reshape(nb * s, d), loss_target.reshape(nb * s, d), w, weights_of, send, last_small,
                           on_loss, nb, s)
    loss = total_loss[0]
    (gath,) = _xchg_wait("ag_wait_small_grads", small_grads["h"], [dx])
    full = dict(zip(_SMALL, _unpack(_sum8("sum_small_grads", gath), small_grads["shapes"])))
    for n in _SMALL_SHARDED:
        width = a[n].shape[-1]
        full[n] = lax.dynamic_slice_in_dim(full[n], dev * width, width, axis=full[n].ndim - 1)
    res = {n: _adamw("adamw_" + n, a[n], a["m_" + n], a["v_" + n], full[n]) for n in _MIDSIZE}
    tiny = [n for n in _SMALL if n not in _MIDSIZE]
    shapes = [a[n].shape for n in tiny]
    packed = _adamw("adamw_small", _pack([a[n] for n in tiny]), _pack([a["m_" + n] for n in tiny]),
                    _pack([a["v_" + n] for n in tiny]), _pack([full[n] for n in tiny]))
    res.update({n: vals for n, vals in zip(tiny, zip(*[_unpack(p, shapes) for p in packed]))})
    return _finish(a, g, dx, loss, res, packed, rs, ffn_lands, dev, nb, s, d)


def _small_partials(g):
    return {
        "ffn_norm_g": jnp.stack([jnp.stack([g[f"ffn_g_{l}{h}"][0] for h in range(2)]) for l in range(2)]),
        "mix_norm_g": jnp.concatenate([g["mix_g_0"], g["mix_g_1"]], axis=0),
        "ret_norm_g": g["ret_g"], "conv_w": g["conv_w"][None], "conv_b": g["conv_b"],
        "lru_w_a": g["lru_w_a"][None], "lru_b_a": g["lru_b_a"], "lru_w_i": g["lru_w_i"][None], "lru_b_i": g["lru_b_i"],
        "lru_lambda": g["lru_lam"], "s5_lambda_re": g["s5_lr"][None], "s5_lambda_im": g["s5_li"][None],
        "s5_log_dt": g["s5_ldt"].reshape(1, S5_GROUPS),
        "s5_b_re": jnp.swapaxes(g["s5_bre"], 1, 2)[None], "s5_b_im": jnp.swapaxes(g["s5_bim"], 1, 2)[None],
        "s5_c_re": g["s5_cre"][None], "s5_c_im": g["s5_cim"][None], "s5_d": g["s5_d"], "final_norm_g": g["final_g"][0],
    }


def _finish(a, g, dx, loss, res, packed, rs, ffn_lands, dev, nb, s, d):
    landed = {}
    for group, names, h in rs:
        if not group.startswith("ffn_"):
            landed.update(zip(names, _xchg_wait("rs_wait_" + group, h, [dx])))
    kinds = {"ffn_w1": "lead", "ffn_w3": "lead", "ffn_w2": "lead", "w_in": "cols", "w_out": "rows", "glu_a": "rows",
             "glu_b": "rows"}

    def update(n, short):
        if short in ("ffn_w1", "ffn_w3"):
            res[n] = _adamw_minor_d("adamw_" + n, a[n], a["m_" + n], a["v_" + n], g[short], landed[short], dev)
        else:
            res[n] = _adamw("adamw_" + n, a[n], a["m_" + n], a["v_" + n], g[short],
                            landed[short].reshape((N_DEV - 1,) + a[n].shape), slot=(dev, kinds[short]))

    for n, short in zip(_BIG[3:], ("w_in", "w_out", "glu_a", "glu_b")):
        update(n, short)
    after = [dx, packed[0]] + [res[n][0] for n in _BIG[3:] + _MIDSIZE]
    ffn_names = ("ffn_w1", "ffn_w3", "ffn_w2")
    for group, names, h in rs:
        if group.startswith("ffn_") and len(names) == 3:
            ffn_lands[:] = _xchg_wait("rs_wait_" + group, h, after, ffn_lands)
    for k, n in enumerate(ffn_names):
        for group, names, h in rs:
            if group.startswith("ffn_") and names == [n]:
                (ffn_lands[k],) = _xchg_wait("rs_wait_" + group, h, after, [ffn_lands[k]])
        landed[n] = ffn_lands[k]
        update(n, n)
        after = after + [res[n][0]]

    out = [loss, dx.reshape(nb, s, d)]
    for k in range(4):
        out += [res[n][k] for n in _WEIGHTS]
    return tuple(out)
```

```python
import math

import numpy as np
import jax
import jax.numpy as jnp
from jax import lax
from jax.experimental import pallas as pl
from jax.experimental.pallas import tpu as pltpu

f32 = jnp.float32
bf16 = jnp.bfloat16

D_MODEL = 1024
N_DEV = 8
EPS = 1e-6
RET_HEADS = 4
HEAD_DIM = 128
RET_WIDTH = 512
RET_CHUNK = 128
ROPE_BASE = 10000.0
LRU_WIDTH = 512
LRU_BLOCKS = 4
LRU_C = 8.0
S5_GROUP = 16
S5_GROUPS = 64
S5_STATE = 64
S5_CHUNK = 1024
S5_BLOCKS = 8
S5_BLOCK_STATES = 512
SUBLANES = 8
D_FF = 2816
FF_SHARD = D_FF // N_DEV
FF_PAD = 384
IN_SHARD = 3072 // N_DEV
ADAM_LR = 0.001
ADAM_B1 = 0.9
ADAM_B2 = 0.999
ADAM_EPS = 1e-08
ADAM_WD = 0.01
ADAM_STEP = 10

VMEM_LIMIT = 56 * 1024 * 1024
VMEM_SPEC = pl.BlockSpec(memory_space=pltpu.VMEM)
ANY_SPEC = pl.BlockSpec(memory_space=pl.ANY)
HBM_SPEC = pl.BlockSpec(memory_space=pltpu.HBM)
SEM_SPEC = pl.BlockSpec(memory_space=pltpu.SEMAPHORE)
SIDE_EFFECT = pltpu.SideEffectType.DATAFLOW_SIDE_EFFECTING
MESH = pl.DeviceIdType.MESH


def _cp(*sem):
    return pltpu.CompilerParams(dimension_semantics=sem, vmem_limit_bytes=VMEM_LIMIT)


def _nn(a, b):
    return jnp.dot(a, b, preferred_element_type=f32)


def _nt(a, b):
    return lax.dot_general(a, b, (((1,), (1,)), ((), ())), preferred_element_type=f32)


def _tn(a, b):
    return lax.dot_general(a, b, (((0,), (0,)), ((), ())), preferred_element_type=f32)


def _rms_fwd(x, g):
    r = lax.rsqrt(jnp.mean(x * x, axis=-1, keepdims=True) + EPS)
    xn = x * r
    return xn * g, xn, r


def _rms_bwd(dh, xn, r, g):
    dxn = dh * g
    dx = r * (dxn - xn * jnp.mean(dxn * xn, axis=-1, keepdims=True))
    dg = jnp.sum(dh * xn, axis=0, keepdims=True)
    return dx, dg


def _shift_dn(v, d, row, fill=0.0):
    return jnp.where(row >= d, pltpu.roll(v, d, 0), fill)


def _shift_up(v, d, row, fill=0.0):
    n = v.shape[0]
    return jnp.where(row < n - d, pltpu.roll(v, n - d, 0), fill)


def _ew(name, fn, ins, outs, tm=512):
    t = ins[0].shape[0]
    n_in = len(ins)

    def body(*refs):
        res = fn(*[r[...] for r in refs[:n_in]])
        for o, v in zip(refs[n_in:], res):
            o[...] = v.astype(o.dtype)

    return pl.pallas_call(
        body, name=name, grid=(t // tm,),
        in_specs=[pl.BlockSpec((tm, a.shape[1]), lambda i: (i, 0)) for a in ins],
        out_specs=[pl.BlockSpec((tm, n), lambda i: (i, 0)) for n, _ in outs],
        out_shape=[jax.ShapeDtypeStruct((t, n), dt) for n, dt in outs],
        compiler_params=_cp("parallel"),
    )(*ins)


def _mm(name, x, w, kind, extras=(), epilogue=None, outs=None, tm=512, tn=1024):
    t = x.shape[0]
    n = w.shape[1] if kind == "nn" else w.shape[0]
    tn = min(tn, n)
    outs = outs or [f32]
    n_ex = len(extras)

    def body(x_ref, w_ref, *refs):
        xb = x_ref[...].astype(bf16)
        acc = _nn(xb, w_ref[...]) if kind == "nn" else _nt(xb, w_ref[...])
        res = epilogue(acc, *[r[...] for r in refs[:n_ex]]) if epilogue else (acc,)
        for o, v in zip(refs[n_ex:], res):
            o[...] = v.astype(o.dtype)

    w_spec = (pl.BlockSpec((w.shape[0], tn), lambda i, j: (0, j)) if kind == "nn"
              else pl.BlockSpec((tn, w.shape[1]), lambda i, j: (j, 0)))
    tile = pl.BlockSpec((tm, tn), lambda i, j: (i, j))
    return pl.pallas_call(
        body, name=name, grid=(t // tm, n // tn),
        in_specs=[pl.BlockSpec((tm, x.shape[1]), lambda i, j: (i, 0)), w_spec] + [tile] * n_ex,
        out_specs=[tile] * len(outs),
        out_shape=[jax.ShapeDtypeStruct((t, n), dt) for dt in outs],
        compiler_params=_cp("parallel", "parallel"),
    )(x, w, *extras)


def _mm_tn(name, x, y, tk=1024, tn=1024, tt=1024):
    t, k = x.shape
    n = y.shape[1]
    tk, tn, tt = min(tk, k), min(tn, n), min(tt, t)

    def body(x_ref, y_ref, o_ref, ob_ref):
        @pl.when(pl.program_id(2) == 0)
        def _():
            o_ref[...] = jnp.zeros_like(o_ref)
        o_ref[...] += _tn(x_ref[...].astype(bf16), y_ref[...].astype(bf16))

        @pl.when(pl.program_id(2) == pl.num_programs(2) - 1)
        def _():
            ob_ref[...] = o_ref[...].astype(bf16)

    out = pl.BlockSpec((tk, tn), lambda i, j, s: (i, j))
    return pl.pallas_call(
        body, name=name, grid=(k // tk, n // tn, t // tt),
        in_specs=[pl.BlockSpec((tt, tk), lambda i, j, s: (s, i)), pl.BlockSpec((tt, tn), lambda i, j, s: (s, j))],
        out_specs=[out, out],
        out_shape=[jax.ShapeDtypeStruct((k, n), f32), jax.ShapeDtypeStruct((k, n), bf16)],
        compiler_params=_cp("parallel", "parallel", "arbitrary"),
    )(x, y)


def _norm_fwd(name, x, g, dtype, tm=512):
    t, d = x.shape

    def body(x_ref, g_ref, h_ref):
        h_ref[...] = _rms_fwd(x_ref[...], g_ref[...])[0].astype(dtype)

    row = pl.BlockSpec((tm, d), lambda i: (i, 0))
    return pl.pallas_call(
        body, name=name, grid=(t // tm,),
        in_specs=[row, pl.BlockSpec((1, d), lambda i: (0, 0))],
        out_specs=row, out_shape=jax.ShapeDtypeStruct((t, d), dtype),
        compiler_params=_cp("parallel"),
    )(x, g)


def _norm_bwd(name, dh, x, g, dres, tm=512):
    t, d = x.shape

    def body(dh_ref, x_ref, g_ref, dres_ref, dx_ref, dg_ref):
        gv = g_ref[...]
        _, xn, r = _rms_fwd(x_ref[...], gv)
        dx, dg = _rms_bwd(dh_ref[...], xn, r, gv)
        dx_ref[...] = dres_ref[...] + dx

        @pl.when(pl.program_id(0) == 0)
        def _():
            dg_ref[...] = jnp.zeros_like(dg_ref)
        dg_ref[...] += dg

    row = pl.BlockSpec((tm, d), lambda i: (i, 0))
    vec = pl.BlockSpec((1, d), lambda i: (0, 0))
    return pl.pallas_call(
        body, name=name, grid=(t // tm,),
        in_specs=[row, row, vec, row],
        out_specs=[row, vec],
        out_shape=[jax.ShapeDtypeStruct((t, d), f32), jax.ShapeDtypeStruct((1, d), f32)],
        compiler_params=_cp("arbitrary"),
    )(dh, x, g, dres)


def _final_loss(x, g, target, tm=512):
    t, d = x.shape

    def body(x_ref, g_ref, t_ref, loss_ref, dx_ref, dg_ref):
        gv = g_ref[...]
        y, xn, r = _rms_fwd(x_ref[...], gv)
        err = y - t_ref[...]
        dy = err * (1.0 / d)
        dx, dg = _rms_bwd(dy, xn, r, gv)
        dx_ref[...] = dx

        @pl.when(pl.program_id(0) == 0)
        def _():
            dg_ref[...] = jnp.zeros_like(dg_ref)
            loss_ref[...] = jnp.zeros_like(loss_ref)
        dg_ref[...] += dg
        loss_ref[...] += jnp.full((1, 128), 0.5 / d, f32) * jnp.sum(err * err)

    row = pl.BlockSpec((tm, d), lambda i: (i, 0))
    vec = pl.BlockSpec((1, d), lambda i: (0, 0))
    return pl.pallas_call(
        body, name="final_loss", grid=(t // tm,),
        in_specs=[row, vec, row],
        out_specs=[pl.BlockSpec((1, 128), lambda i: (0, 0)), row, vec],
        out_shape=[jax.ShapeDtypeStruct((1, 128), f32), jax.ShapeDtypeStruct((t, d), f32),
                   jax.ShapeDtypeStruct((1, d), f32)],
        compiler_params=_cp("arbitrary"),
    )(x, g, target)


def _load_ffn_weights(hbm_refs, vmem_refs, sems):
    @pl.when(pl.program_id(0) == 0)
    def _():
        copies = []
        for k, (src, dst) in enumerate(zip(hbm_refs, vmem_refs)):
            for j in range(N_DEV):
                half = pl.ds((j % 2) * FF_PAD, FF_PAD)
                window = dst.at[j // 2, half, :] if k == 2 else dst.at[j // 2, :, half]
                copies.append(pltpu.make_async_copy(src.at[j], window, sems.at[k * N_DEV + j]))
        for cp in copies:
            cp.start()
        for cp in copies:
            cp.wait()


def _ffn_weight_scratch(nj, d, ff):
    return [pltpu.VMEM((nj, d, ff), bf16), pltpu.VMEM((nj, d, ff), bf16), pltpu.VMEM((nj, ff, d), bf16),
            pltpu.SemaphoreType.DMA((3 * N_DEV,))]


def _ffn_fwd(name, x, g, w1, w3, w2, tm=512):
    t, d = x.shape
    nj, ff = N_DEV // 2, 2 * FF_PAD

    def body(x_ref, g_ref, w1_hbm, w3_hbm, w2_hbm, y_ref, a_ref, b_ref, hbt_ref, ut_ref, w1_ref, w3_ref, w2_ref, sems):
        _load_ffn_weights((w1_hbm, w3_hbm, w2_hbm), (w1_ref, w3_ref, w2_ref), sems)
        xv = x_ref[...]
        h, _, _ = _rms_fwd(xv, g_ref[...])
        hb = h.astype(bf16)
        hbt_ref[...] = hb.T
        acc = jnp.zeros((tm, d), f32)
        for j in range(nj):
            a = _nn(hb, w1_ref[j])
            b = _nn(hb, w3_ref[j])
            a_ref[j] = a.astype(bf16)
            b_ref[j] = b.astype(bf16)
            u = (a * jax.nn.sigmoid(a) * b).astype(bf16)
            ut_ref[j] = u.T
            acc = acc + _nn(u, w2_ref[j])
        y_ref[...] = xv + 0.5 * acc

    row = pl.BlockSpec((tm, d), lambda i: (i, 0))
    mid = pl.BlockSpec((nj, tm, ff), lambda i: (0, i, 0))
    return pl.pallas_call(
        body, name=name, grid=(t // tm,),
        in_specs=[row, pl.BlockSpec((1, d), lambda i: (0, 0)), ANY_SPEC, ANY_SPEC, ANY_SPEC],
        out_specs=[row, mid, mid, pl.BlockSpec((d, tm), lambda i: (0, i)), pl.BlockSpec((nj, ff, tm), lambda i: (0, 0, i))],
        out_shape=[jax.ShapeDtypeStruct((t, d), f32), jax.ShapeDtypeStruct((nj, t, ff), bf16),
                   jax.ShapeDtypeStruct((nj, t, ff), bf16), jax.ShapeDtypeStruct((d, t), bf16),
                   jax.ShapeDtypeStruct((nj, ff, t), bf16)],
        scratch_shapes=_ffn_weight_scratch(nj, d, ff),
        compiler_params=_cp("arbitrary"),
    )(x, g, w1, w3, w2)


def _ffn_dx(name, dy, x, g, w1, w3, w2, a, b, tm=256):
    t, d = x.shape
    nj, ff = N_DEV // 2, 2 * FF_PAD

    def body(dy_ref, x_ref, g_ref, w1_hbm, w3_hbm, w2_hbm, a_ref, b_ref,
             dx_ref, dg_ref, dyh_ref, da_ref, db_ref, w1_ref, w3_ref, w2_ref, sems):
        _load_ffn_weights((w1_hbm, w3_hbm, w2_hbm), (w1_ref, w3_ref, w2_ref), sems)
        gv = g_ref[...]
        _, xn, r = _rms_fwd(x_ref[...], gv)
        dyv = dy_ref[...]
        dyh = (0.5 * dyv).astype(bf16)
        dyh_ref[...] = dyh
        dh = jnp.zeros((tm, d), f32)
        dus = [_nt(dyh, w2_ref[j]) for j in range(nj)]
        for j in range(nj):
            av = a_ref[j].astype(f32)
            bv = b_ref[j].astype(f32)
            s = jax.nn.sigmoid(av)
            silu = av * s
            du = dus[j]
            dab = (du * bv * (s * (1.0 + av * (1.0 - s)))).astype(bf16)
            dbb = (du * silu).astype(bf16)
            da_ref[j] = dab
            db_ref[j] = dbb
            dh = dh + _nt(dab, w1_ref[j]) + _nt(dbb, w3_ref[j])
        dx, dg = _rms_bwd(dh, xn, r, gv)
        dx_ref[...] = dyv + dx

        @pl.when(pl.program_id(0) == 0)
        def _():
            dg_ref[...] = jnp.zeros_like(dg_ref)
        dg_ref[...] += dg

    row = pl.BlockSpec((tm, d), lambda i: (i, 0))
    vec = pl.BlockSpec((1, d), lambda i: (0, 0))
    mid = pl.BlockSpec((nj, tm, ff), lambda i: (0, i, 0))
    mid_shape = jax.ShapeDtypeStruct((nj, t, ff), bf16)
    return pl.pallas_call(
        body, name=name, grid=(t // tm,),
        in_specs=[row, row, vec, ANY_SPEC, ANY_SPEC, ANY_SPEC, mid, mid],
        out_specs=[row, vec, row, mid, mid],
        out_shape=[jax.ShapeDtypeStruct((t, d), f32), jax.ShapeDtypeStruct((1, d), f32),
                   jax.ShapeDtypeStruct((t, d), bf16), mid_shape, mid_shape],
        scratch_shapes=_ffn_weight_scratch(nj, d, ff),
        compiler_params=_cp("arbitrary"),
    )(dy, x, g, w1, w3, w2, a, b)


def _ffn_dw(name, xt, ys, bufs, l, h, tt=2048):
    n = len(ys)
    t = ys[0].shape[-2]
    tt = min(tt, t)
    cut_cols = xt.ndim == 2

    def body(x_ref, *refs):
        y_refs, outs, accs = refs[:n], refs[2 * n:4 * n], refs[4 * n:]
        s = pl.program_id(1)
        xv = x_ref[0] if xt.ndim == 3 else x_ref[...]
        for k in range(n):
            prod = _nn(xv, y_refs[k][0] if ys[k].ndim == 3 else y_refs[k][...])

            @pl.when(s == 0)
            def _():
                accs[k][...] = prod

            @pl.when(s > 0)
            def _():
                accs[k][...] += prod

        @pl.when(s == pl.num_programs(1) - 1)
        def _():
            for k in range(n):
                total = accs[k][...]
                for e in range(2):
                    lo = e * FF_PAD
                    part = total[:, lo:lo + FF_SHARD] if cut_cols else total[lo:lo + FF_SHARD, :]
                    outs[k][e] = part
                    outs[n + k][e] = part.astype(bf16)

    x_spec = (pl.BlockSpec((1, xt.shape[1], tt), lambda p, s: (p, 0, s)) if xt.ndim == 3
              else pl.BlockSpec((xt.shape[0], tt), lambda p, s: (0, s)))
    y_specs = [pl.BlockSpec((1, tt, y.shape[2]), lambda p, s: (p, s, 0)) if y.ndim == 3
               else pl.BlockSpec((tt, y.shape[1]), lambda p, s: (s, 0)) for y in ys]
    dims = [b.shape[-2:] for b in bufs]
    outs = pl.pallas_call(
        body, name=name, grid=(N_DEV // 2, t // tt),
        in_specs=[x_spec] + y_specs + [ANY_SPEC] * n,
        out_specs=[pl.BlockSpec((2, None, None, k_, n_), lambda p, s: (p, l, h, 0, 0)) for k_, n_ in dims]
        + [pl.BlockSpec((2, k_, n_), lambda p, s: (p, 0, 0)) for k_, n_ in dims],
        out_shape=[jax.ShapeDtypeStruct(b.shape, b.dtype) for b in bufs]
        + [jax.ShapeDtypeStruct((N_DEV, k_, n_), bf16) for k_, n_ in dims],
        input_output_aliases={1 + n + k: k for k in range(n)},
        scratch_shapes=[pltpu.VMEM((xt.shape[-2], y.shape[-1]), f32) for y in ys],
        compiler_params=_cp("parallel", "arbitrary"),
    )(xt, *ys, *bufs)
    return outs[:n], outs[n:]


_LOG_GAMMA = [float(np.log1p(-np.float32(2.0) ** np.float32(-5.0 - h))) for h in range(RET_HEADS)]


def _ret_consts(h):
    lg = jnp.where(h == 0, _LOG_GAMMA[0], jnp.where(h == 1, _LOG_GAMMA[1],
                   jnp.where(h == 2, _LOG_GAMMA[2], _LOG_GAMMA[3]))).astype(f32)
    c = RET_CHUNK
    r = lax.broadcasted_iota(jnp.int32, (c, c), 0)
    cc = lax.broadcasted_iota(jnp.int32, (c, c), 1)
    decay = jnp.where(r >= cc, jnp.exp(lg * jnp.maximum((r - cc).astype(f32), 0.0)), 0.0)
    pos = lax.broadcasted_iota(jnp.int32, (c, 1), 0).astype(f32)
    kd = jnp.exp(lg * (c - 1.0 - pos))
    qd = jnp.exp(lg * (pos + 1.0))
    gc = jnp.exp(lg * c)
    return decay, kd, qd, gc


def _rope(x, cos, sin):
    return x * cos + pltpu.roll(x, HEAD_DIM // 2, 1) * sin


def _rope_t(g, cos, sin):
    return g * cos + pltpu.roll(g * sin, HEAD_DIM // 2, 1)


def _rope_tables(s):
    half = HEAD_DIM // 2
    inv = ROPE_BASE ** (-jnp.arange(half, dtype=f32) / half)
    ang = jnp.arange(s, dtype=f32)[:, None] * inv[None, :]
    cos, sin = jnp.cos(ang), jnp.sin(ang)
    return jnp.concatenate([cos, cos], axis=1), jnp.concatenate([-sin, sin], axis=1)


def _head_ln(o):
    mu = jnp.mean(o, axis=-1, keepdims=True)
    oc = o - mu
    rs = lax.rsqrt(jnp.mean(oc * oc, axis=-1, keepdims=True) + EPS)
    return oc * rs, rs


def _ret_fwd(proj, cos, sin, ret_g, nb, s):
    c = RET_CHUNK
    nc = s // c
    t = nb * s
    scale = HEAD_DIM ** -0.5

    def body(q_ref, k_ref, v_ref, gate_ref, cos_ref, sin_ref, g_ref, o_ref, rprev_ref, m_ref):
        decay, kd, qd, gc = _ret_consts(pl.program_id(0))
        gv = g_ref[...]

        def chunk(b, n, rv):
            rows = pl.ds(pl.multiple_of(b * s + n * c, c), c)
            pos = pl.ds(pl.multiple_of(n * c, c), c)
            cs, sn = cos_ref[pos, :], sin_ref[pos, :]
            q = _rope(q_ref[rows, :], cs, sn)
            k = _rope(k_ref[rows, :], cs, sn) * scale
            vb = v_ref[rows, :].astype(bf16)
            sc = _nt(q.astype(bf16), k.astype(bf16)) * decay
            rprev_ref[b, n] = rv
            o = _nn(sc.astype(bf16), vb) + _nn((q * qd).astype(bf16), rv.astype(bf16))
            o_ref[rows, :] = o
            y, _ = _head_ln(o)
            gate = gate_ref[rows, :]
            m_ref[rows, :] = y * gv * (gate * jax.nn.sigmoid(gate))
            return rv * gc + _tn((k * kd).astype(bf16), vb)

        def step(n, carry):
            return tuple(chunk(b, n, carry[b]) for b in range(nb))

        lax.fori_loop(0, nc, step, (jnp.zeros((HEAD_DIM, HEAD_DIM), f32),) * nb)

    def col(off):
        return pl.BlockSpec((t, HEAD_DIM), lambda h: (0, off + h))

    tab = pl.BlockSpec((s, HEAD_DIM), lambda h: (0, 0))
    return pl.pallas_call(
        body, name="ret_fwd", grid=(RET_HEADS,),
        in_specs=[col(0), col(4), col(8), col(12), tab, tab, pl.BlockSpec((1, HEAD_DIM), lambda h: (0, h))],
        out_specs=[col(0), pl.BlockSpec((nb, None, nc, HEAD_DIM, HEAD_DIM), lambda h: (0, h, 0, 0, 0)), col(0)],
        out_shape=[jax.ShapeDtypeStruct((t, RET_WIDTH), f32),
                   jax.ShapeDtypeStruct((nb, RET_HEADS, nc, HEAD_DIM, HEAD_DIM), f32),
                   jax.ShapeDtypeStruct((t, RET_WIDTH), f32)],
        compiler_params=_cp("parallel"),
    )(proj, proj, proj, proj, cos, sin, ret_g)


def _ret_bwd(dmerged, o_raw, rprev, proj, cos, sin, ret_g, nb, s):
    c = RET_CHUNK
    nc = s // c
    t = nb * s
    scale = HEAD_DIM ** -0.5

    def body(dm_ref, o_ref, rprev_ref, q_ref, k_ref, v_ref, gate_ref, cos_ref, sin_ref, g_ref,
             dq_ref, dk_ref, dv_ref, dgate_ref, dg_ref):
        decay, kd, qd, gc = _ret_consts(pl.program_id(0))
        gv = g_ref[...]

        def chunk(b, n, drn, dg):
            rows = pl.ds(pl.multiple_of(b * s + n * c, c), c)
            pos = pl.ds(pl.multiple_of(n * c, c), c)
            cs, sn = cos_ref[pos, :], sin_ref[pos, :]
            q = _rope(q_ref[rows, :], cs, sn)
            k = _rope(k_ref[rows, :], cs, sn) * scale
            qb, kb = q.astype(bf16), k.astype(bf16)
            vb = v_ref[rows, :].astype(bf16)
            sc = _nt(qb, kb) * decay
            y, rs = _head_ln(o_ref[rows, :])
            gate = gate_ref[rows, :]
            sg = jax.nn.sigmoid(gate)
            silu = gate * sg
            dm = dm_ref[rows, :]
            dgate_ref[rows, :] = dm * y * gv * (sg * (1.0 + gate * (1.0 - sg)))
            dyl = dm * gv * silu
            dg = dg + jnp.sum(dm * y * silu, axis=0, keepdims=True)
            do = rs * (dyl - jnp.mean(dyl, axis=-1, keepdims=True) - y * jnp.mean(dyl * y, axis=-1, keepdims=True))
            dob = do.astype(bf16)
            rv = rprev_ref[b, n]
            drb = drn.astype(bf16)
            ds = (_nt(dob, vb) * decay).astype(bf16)
            kdb = (k * kd).astype(bf16)
            qdb = (q * qd).astype(bf16)
            dq_r = _nn(ds, kb) + _nt(dob, rv.astype(bf16)) * qd
            dk_r = _tn(ds, qb) + _nt(vb, drb) * kd
            dv_ref[rows, :] = _tn(sc.astype(bf16), dob) + _nn(kdb, drb)
            dq_ref[rows, :] = _rope_t(dq_r, cs, sn)
            dk_ref[rows, :] = _rope_t(dk_r * scale, cs, sn)
            return drn * gc + _tn(qdb, dob), dg

        def step(i, carry):
            out = [chunk(b, nc - 1 - i, *carry[b]) for b in range(nb)]
            return tuple(out)

        zero = (jnp.zeros((HEAD_DIM, HEAD_DIM), f32), jnp.zeros((1, HEAD_DIM), f32))
        done = lax.fori_loop(0, nc, step, (zero,) * nb)
        dg_ref[...] = sum(dg for _, dg in done)

    def col(off):
        return pl.BlockSpec((t, HEAD_DIM), lambda h: (0, off + h))

    tab = pl.BlockSpec((s, HEAD_DIM), lambda h: (0, 0))
    gsp = pl.BlockSpec((1, HEAD_DIM), lambda h: (0, h))
    out_t = jax.ShapeDtypeStruct((t, RET_WIDTH), f32)
    return pl.pallas_call(
        body, name="ret_bwd", grid=(RET_HEADS,),
        in_specs=[col(0), col(0), pl.BlockSpec((nb, None, nc, HEAD_DIM, HEAD_DIM), lambda h: (0, h, 0, 0, 0)),
                  col(0), col(4), col(8), col(12), tab, tab, gsp],
        out_specs=[col(0), col(0), col(0), col(0), gsp],
        out_shape=[out_t, out_t, out_t, out_t, jax.ShapeDtypeStruct((1, RET_WIDTH), f32)],
        compiler_params=_cp("parallel"),
    )(dmerged, o_raw, rprev, proj, proj, proj, proj, cos, sin, ret_g)


def _neg_expm1(z):
    series = -(z * (1.0 + z * (0.5 + z * (1.0 / 6.0 + z * (1.0 / 24.0)))))
    return jnp.where(z > -0.01, series, 1.0 - jnp.exp(z))


def _lru_gates(xc, pa, pi, lam):
    r = jax.nn.sigmoid(pa)
    i = jax.nn.sigmoid(pi)
    log_a = -LRU_C * r * jax.nn.softplus(-lam)
    a = jnp.exp(log_a)
    bx = jnp.sqrt(_neg_expm1(2.0 * log_a)) * i * xc
    return a, bx


def _scan_rows(a, b, row, up):
    sub = row[:SUBLANES] & (SUBLANES - 1)
    groups = list(range(a.shape[0] // SUBLANES))
    out = [None] * len(groups)
    edge = slice(0, 1) if up else slice(SUBLANES - 1, SUBLANES)
    carry = jnp.zeros((1, a.shape[1]), f32)
    for g in (reversed(groups) if up else groups):
        rows = slice(g * SUBLANES, (g + 1) * SUBLANES)
        xa, xb = a[rows], b[rows]
        d = 1
        while d < SUBLANES:
            keep = (sub < SUBLANES - d) if up else (sub >= d)
            shift = SUBLANES - d if up else d
            xb = xa * jnp.where(keep, pltpu.roll(xb, shift, 0), 0.0) + xb
            xa = xa * jnp.where(keep, pltpu.roll(xa, shift, 0), 1.0)
            d *= 2
        out[g] = xb + xa * carry
        carry = out[g][edge]
    return jnp.concatenate(out, axis=0)


def _scan_fwd(a, b, row):
    return _scan_rows(a, b, row, False)


def _scan_bwd(c, b, row):
    return _scan_rows(c, b, row, True)


def _conv_fwd(x, cw, cb, row):
    return (cb + cw[3:4] * x + cw[2:3] * _shift_dn(x, 1, row) + cw[1:2] * _shift_dn(x, 2, row)
            + cw[0:1] * _shift_dn(x, 3, row))


def _lru_specs(s, order):
    def im(f):
        return (lambda b, g: f(b, g)) if order == "bg" else (lambda g, b: f(b, g))
    seq = lambda off: pl.BlockSpec((s, 128), im(lambda b, g: (b, off + g)))
    vec = pl.BlockSpec((1, 128), im(lambda b, g: (0, g)))
    cw = pl.BlockSpec((4, 128), im(lambda b, g: (0, g)))
    mat = pl.BlockSpec((1, 128, 128), im(lambda b, g: (g, 0, 0)))
    return seq, vec, cw, mat


def _lru_fwd(proj, conv_w, conv_b, w_a, b_a, w_i, b_i, lam, nb, s):
    def body(x_ref, gt_ref, cw_ref, cb_ref, wa_ref, ba_ref, wi_ref, bi_ref, lam_ref, out_ref, h_ref):
        row = lax.broadcasted_iota(jnp.int32, (s, 128), 0)
        xc = _conv_fwd(x_ref[...], cw_ref[...], cb_ref[...], row)
        xcb = xc.astype(bf16)
        pa = _nn(xcb, wa_ref[0].astype(bf16)) + ba_ref[...]
        pi = _nn(xcb, wi_ref[0].astype(bf16)) + bi_ref[...]
        a, bx = _lru_gates(xc, pa, pi, lam_ref[...])
        h = _scan_fwd(a, bx, row)
        h_ref[...] = h
        out_ref[...] = h * jax.nn.gelu(gt_ref[...])

    seq, vec, cw, mat = _lru_specs(s, "bg")
    out = jax.ShapeDtypeStruct((nb * s, LRU_WIDTH), f32)
    return pl.pallas_call(
        body, name="lru_fwd", grid=(nb, LRU_BLOCKS),
        in_specs=[seq(16), seq(20), cw, vec, mat, vec, mat, vec, vec],
        out_specs=[seq(0), seq(0)], out_shape=[out, out],
        compiler_params=_cp("parallel", "parallel"),
    )(proj, proj, conv_w, conv_b, w_a, b_a, w_i, b_i, lam)


def _lru_bwd(dmerged, states, proj, conv_w, conv_b, w_a, b_a, w_i, b_i, lam, nb, s):
    def body(dout_ref, h_ref, x_ref, gt_ref, cw_ref, cb_ref, wa_ref, ba_ref, wi_ref, bi_ref, lam_ref,
             dx_ref, dgt_ref, dcw_ref, dcb_ref, dwa_ref, dba_ref, dwi_ref, dbi_ref, dlam_ref):
        row = lax.broadcasted_iota(jnp.int32, (s, 128), 0)
        x = x_ref[...]
        cwv = cw_ref[...]
        xc = _conv_fwd(x, cwv, cb_ref[...], row)
        xcb = xc.astype(bf16)
        wab, wib = wa_ref[0].astype(bf16), wi_ref[0].astype(bf16)
        pa = _nn(xcb, wab) + ba_ref[...]
        pi = _nn(xcb, wib) + bi_ref[...]
        (a, _), gates_vjp = jax.vjp(_lru_gates, xc, pa, pi, lam_ref[...])
        h = h_ref[...]
        ge, gelu_vjp = jax.vjp(jax.nn.gelu, gt_ref[...])
        dout = dout_ref[...]
        dgt_ref[...] = gelu_vjp(dout * h)[0]
        adj = _scan_bwd(_shift_up(a, 1, row), dout * ge, row)
        dxc, dpa, dpi, dlam = gates_vjp((adj * _shift_dn(h, 1, row), adj))
        dpab, dpib = dpa.astype(bf16), dpi.astype(bf16)
        dxc = dxc + _nt(dpab, wab) + _nt(dpib, wib)
        dx_ref[...] = (cwv[3:4] * dxc + cwv[2:3] * _shift_up(dxc, 1, row) + cwv[1:2] * _shift_up(dxc, 2, row)
                       + cwv[0:1] * _shift_up(dxc, 3, row))

        @pl.when(pl.program_id(1) == 0)
        def _():
            for r in (dcw_ref, dcb_ref, dwa_ref, dba_ref, dwi_ref, dbi_ref, dlam_ref):
                r[...] = jnp.zeros_like(r)
        rsum = lambda v: jnp.sum(v, axis=0, keepdims=True)
        dcw_ref[...] += jnp.concatenate([rsum(dxc * _shift_dn(x, 3, row)), rsum(dxc * _shift_dn(x, 2, row)),
                                         rsum(dxc * _shift_dn(x, 1, row)), rsum(dxc * x)], axis=0)
        dcb_ref[...] += rsum(dxc)
        dwa_ref[0] += _tn(xcb, dpab)
        dwi_ref[0] += _tn(xcb, dpib)
        dba_ref[...] += rsum(dpa)
        dbi_ref[...] += rsum(dpi)
        dlam_ref[...] += dlam

    seq, vec, cw, mat = _lru_specs(s, "gb")
    t = nb * s
    vshape = jax.ShapeDtypeStruct((1, LRU_WIDTH), f32)
    mshape = jax.ShapeDtypeStruct((LRU_BLOCKS, 128, 128), f32)
    return pl.pallas_call(
        body, name="lru_bwd", grid=(LRU_BLOCKS, nb),
        in_specs=[seq(4), seq(0), seq(16), seq(20), cw, vec, mat, vec, mat, vec, vec],
        out_specs=[seq(0), seq(0), cw, vec, mat, vec, mat, vec, vec],
        out_shape=[jax.ShapeDtypeStruct((t, LRU_WIDTH), f32), jax.ShapeDtypeStruct((t, LRU_WIDTH), f32),
                   jax.ShapeDtypeStruct((4, LRU_WIDTH), f32), vshape, mshape, vshape, mshape, vshape, vshape],
        compiler_params=_cp("parallel", "arbitrary"),
    )(dmerged, states, proj, proj, conv_w, conv_b, w_a, b_a, w_i, b_i, lam)


def _s5_disc(lr, li, ldt, bre, bim):
    dt = jnp.exp(ldt)
    mag = jnp.exp(lr * dt)
    lbr = mag * jnp.cos(li * dt)
    lbi = mag * jnp.sin(li * dt)
    den = lr * lr + li * li
    nr = lbr - 1.0
    fr = (nr * lr + lbi * li) / den
    fi = (lbi * lr - nr * li) / den
    bbr = fr[:, None, :] * bre - fi[:, None, :] * bim
    bbi = fr[:, None, :] * bim + fi[:, None, :] * bre
    return lbr, lbi, bbr, bbi


def _s5_prep(lr, li, ldt, bre, bim):
    def body(lr_ref, li_ref, ldt_ref, bre_ref, bim_ref, o1, o2, o3, o4):
        o1[...], o2[...], o3[...], o4[...] = _s5_disc(lr_ref[...], li_ref[...], ldt_ref[...], bre_ref[...], bim_ref[...])

    return pl.pallas_call(
        body, name="s5_prep", in_specs=[VMEM_SPEC] * 5, out_specs=[VMEM_SPEC] * 4,
        out_shape=[jax.ShapeDtypeStruct(lr.shape, f32), jax.ShapeDtypeStruct(lr.shape, f32),
                   jax.ShapeDtypeStruct(bre.shape, f32), jax.ShapeDtypeStruct(bre.shape, f32)],
    )(lr, li, ldt, bre, bim)


def _s5_prep_bwd(lr, li, ldt, bre, bim, cts):
    def body(lr_ref, li_ref, ldt_ref, bre_ref, bim_ref, g1, g2, g3, g4, o1, o2, o3, o4, o5):
        _, vjp = jax.vjp(_s5_disc, lr_ref[...], li_ref[...], ldt_ref[...], bre_ref[...], bim_ref[...])
        o1[...], o2[...], o3[...], o4[...], o5[...] = vjp((g1[...], g2[...], g3[...], g4[...]))

    return pl.pallas_call(
        body, name="s5_prep_bwd", in_specs=[VMEM_SPEC] * 9, out_specs=[VMEM_SPEC] * 5,
        out_shape=[jax.ShapeDtypeStruct(v.shape, f32) for v in (lr, li, ldt, bre, bim)],
    )(lr, li, ldt, bre, bim, *cts)


def _cmul(ar, ai, br, bi):
    return ar * br - ai * bi, ar * bi + ai * br


def _s5_pow_table(lr, li, n, row, up):
    ar = jnp.broadcast_to(lr, (n, lr.shape[1]))
    ai = jnp.broadcast_to(li, (n, li.shape[1]))
    shift = _shift_up if up else _shift_dn
    d = 1
    while d < n:
        ar, ai = _cmul(ar, ai, shift(ar, d, row, 1.0), shift(ai, d, row, 0.0))
        d *= 2
    return ar, ai


def _s5_step_factors(lr, li, row, up):
    sub = row & (SUBLANES - 1)
    out, pr, pi, d = [], lr, li, 1
    while d < SUBLANES:
        keep = (sub < SUBLANES - d) if up else (sub >= d)
        out.append((jnp.where(keep, pr, 0.0), jnp.where(keep, pi, 0.0)))
        pr, pi = _cmul(pr, pi, pr, pi)
        d *= 2
    return out


def _s5_scan(br, bi, steps, tab_r, tab_i, cr, ci, up):
    groups = list(range(br.shape[0] // SUBLANES))
    out_r, out_i = [None] * len(groups), [None] * len(groups)
    edge = slice(0, 1) if up else slice(SUBLANES - 1, SUBLANES)
    for g in (reversed(groups) if up else groups):
        rows = slice(g * SUBLANES, (g + 1) * SUBLANES)
        xr, xi = br[rows], bi[rows]
        for k, (mr, mi) in enumerate(steps):
            shift = SUBLANES - (1 << k) if up else 1 << k
            tr, ti = _cmul(mr, mi, pltpu.roll(xr, shift, 0), pltpu.roll(xi, shift, 0))
            xr, xi = xr + tr, xi + ti
        tr, ti = _cmul(tab_r, tab_i, cr, ci)
        hr, hi = xr + tr, xi + ti
        out_r[g], out_i[g] = hr, hi
        cr, ci = hr[edge], hi[edge]
    return jnp.concatenate(out_r, axis=0), jnp.concatenate(out_i, axis=0)


def _s5_specs(t, nb, nc):
    seq = pl.BlockSpec((t, 128), lambda k: (0, k))
    lvec = pl.BlockSpec((1, S5_BLOCK_STATES), lambda k: (0, k))
    dvec = pl.BlockSpec((1, 128), lambda k: (0, k))
    wmat = pl.BlockSpec((1, 128, S5_BLOCK_STATES), lambda k: (k, 0, 0))
    h0 = pl.BlockSpec((nb, None, nc, 2, S5_BLOCK_STATES), lambda k: (0, k, 0, 0, 0))
    states = pl.BlockSpec((t, S5_BLOCK_STATES), lambda k: (0, k))
    return seq, lvec, dvec, wmat, h0, states


def _s5_fwd(u, lbr, lbi, wbr, wbi, wcr, wci, dskip, nb, s):
    ln = min(S5_CHUNK, s)
    nc = s // ln

    def body(u_ref, lr_ref, li_ref, wbr_ref, wbi_ref, wcr_ref, wci_ref, d_ref, yg_ref, y_ref, h0_ref, hr_ref, hi_ref):
        row = lax.broadcasted_iota(jnp.int32, (ln, S5_BLOCK_STATES), 0)
        lr, li = lr_ref[...], li_ref[...]
        pr, pi = _s5_pow_table(lr, li, SUBLANES, row[:SUBLANES], False)
        steps = _s5_step_factors(lr, li, row[:SUBLANES], False)
        dv = d_ref[...]

        def chunk(b, n, h0r, h0i):
            st = pl.multiple_of(b * s + n * ln, ln)
            uc = u_ref[pl.ds(st, ln), :]
            ub = uc.astype(bf16)
            hr, hi = _s5_scan(_nn(ub, wbr_ref[0]), _nn(ub, wbi_ref[0]), steps, pr, pi, h0r, h0i, False)
            h0_ref[b, n, 0:1, :] = h0r
            h0_ref[b, n, 1:2, :] = h0i
            hrb, hib = hr.astype(bf16), hi.astype(bf16)
            hr_ref[pl.ds(st, ln), :] = hrb
            hi_ref[pl.ds(st, ln), :] = hib
            y = _nt(hrb, wcr_ref[0]) - _nt(hib, wci_ref[0]) + dv * uc
            y_ref[pl.ds(st, ln), :] = y
            yg_ref[pl.ds(st, ln), :] = jax.nn.gelu(y).astype(bf16)
            return hr[ln - 1:ln, :], hi[ln - 1:ln, :]

        def step(n, carry):
            return tuple(chunk(b, n, *carry[b]) for b in range(nb))

        z = jnp.zeros((1, S5_BLOCK_STATES), f32)
        lax.fori_loop(0, nc, step, ((z, z),) * nb)

    t = nb * s
    seq, lvec, dvec, wmat, h0, states = _s5_specs(t, nb, nc)
    return pl.pallas_call(
        body, name="s5_fwd", grid=(S5_BLOCKS,),
        in_specs=[seq, lvec, lvec, wmat, wmat, wmat, wmat, dvec],
        out_specs=[seq, seq, h0, states, states],
        out_shape=[jax.ShapeDtypeStruct((t, D_MODEL), bf16), jax.ShapeDtypeStruct((t, D_MODEL), f32),
                   jax.ShapeDtypeStruct((nb, S5_BLOCKS, nc, 2, S5_BLOCK_STATES), f32),
                   jax.ShapeDtypeStruct((t, S5_BLOCKS * S5_BLOCK_STATES), bf16),
                   jax.ShapeDtypeStruct((t, S5_BLOCKS * S5_BLOCK_STATES), bf16)],
        compiler_params=_cp("parallel"),
    )(u, lbr, lbi, wbr, wbi, wcr, wci, dskip)


def _s5_bwd(dyg, y, u, h0, hrs, his, lbr, lbi, wbr, wbi, wcr, wci, dskip, nb, s):
    ln = min(S5_CHUNK, s)
    nc = s // ln

    def body(dyg_ref, y_ref, u_ref, h0_ref, hr_ref, hi_ref, lr_ref, li_ref, wbr_ref, wbi_ref, wcr_ref, wci_ref, d_ref,
             du_ref, dlr_ref, dli_ref, dwbr_ref, dwbi_ref, dwcr_ref, dwci_ref, dd_ref):
        for r in (dlr_ref, dli_ref, dwbr_ref, dwbi_ref, dwcr_ref, dwci_ref, dd_ref):
            r[...] = jnp.zeros_like(r)
        row = lax.broadcasted_iota(jnp.int32, (ln, S5_BLOCK_STATES), 0)
        lr, li = lr_ref[...], li_ref[...]
        qr, qi = _s5_pow_table(lr, -li, SUBLANES, row[:SUBLANES], True)
        steps_up = _s5_step_factors(lr, -li, row[:SUBLANES], True)
        dv = d_ref[...]
        rsum = lambda v: jnp.sum(v, axis=0, keepdims=True)

        def chunk(b, n, gnr, gni):
            st = pl.multiple_of(b * s + n * ln, ln)
            uc = u_ref[pl.ds(st, ln), :]
            ub = uc.astype(bf16)
            h0v = h0_ref[b, n]
            h0r, h0i = h0v[0:1], h0v[1:2]
            hrb, hib = hr_ref[pl.ds(st, ln), :], hi_ref[pl.ds(st, ln), :]
            hr, hi = hrb.astype(f32), hib.astype(f32)
            dy = jax.vjp(jax.nn.gelu, y_ref[pl.ds(st, ln), :])[1](dyg_ref[pl.ds(st, ln), :])[0]
            dyb = dy.astype(bf16)
            dd_ref[...] += rsum(dy * uc)
            gr, gi = _s5_scan(_nn(dyb, wcr_ref[0]), -_nn(dyb, wci_ref[0]), steps_up, qr, qi, gnr, gni, True)
            hpr = jnp.where(row >= 1, pltpu.roll(hr, 1, 0), h0r)
            hpi = jnp.where(row >= 1, pltpu.roll(hi, 1, 0), h0i)
            dlr_ref[...] += rsum(gr * hpr + gi * hpi)
            dli_ref[...] += rsum(gi * hpr - gr * hpi)
            grb, gib = gr.astype(bf16), gi.astype(bf16)
            dwbr_ref[0] += _tn(ub, grb)
            dwbi_ref[0] += _tn(ub, gib)
            dwcr_ref[0] += _tn(dyb, hrb)
            dwci_ref[0] -= _tn(dyb, hib)
            du_ref[pl.ds(st, ln), :] = _nt(grb, wbr_ref[0]) + _nt(gib, wbi_ref[0]) + dv * dy
            return gr[0:1, :], gi[0:1, :]

        def step(i, carry):
            return tuple(chunk(b, nc - 1 - i, *carry[b]) for b in range(nb))

        z = jnp.zeros((1, S5_BLOCK_STATES), f32)
        lax.fori_loop(0, nc, step, ((z, z),) * nb)

    t = nb * s
    seq, lvec, dvec, wmat, h0s, states = _s5_specs(t, nb, nc)
    lshape = jax.ShapeDtypeStruct((1, S5_BLOCKS * S5_BLOCK_STATES), f32)
    wshape = jax.ShapeDtypeStruct((S5_BLOCKS, 128, S5_BLOCK_STATES), f32)
    return pl.pallas_call(
        body, name="s5_bwd", grid=(S5_BLOCKS,),
        in_specs=[seq, seq, seq, h0s, states, states, lvec, lvec, wmat, wmat, wmat, wmat, dvec],
        out_specs=[seq, lvec, lvec, wmat, wmat, wmat, wmat, dvec],
        out_shape=[jax.ShapeDtypeStruct((t, D_MODEL), f32), lshape, lshape, wshape, wshape, wshape, wshape,
                   jax.ShapeDtypeStruct((1, D_MODEL), f32)],
        compiler_params=_cp("parallel"),
    )(dyg, y, u, h0, hrs, his, lbr, lbi, wbr, wbi, wcr, wci, dskip)


def _blockdiag(w):
    w4 = w.reshape(S5_BLOCKS, 8, S5_GROUP, S5_STATE)
    same_group = jnp.eye(8, dtype=bool)[None, :, None, :, None]
    return jnp.where(same_group, w4[:, :, :, None, :], 0.0).reshape(S5_BLOCKS, 128, S5_BLOCK_STATES)


def _blockdiag_t(dw):
    d5 = dw.reshape(S5_BLOCKS, 8, S5_GROUP, 8, S5_STATE)
    diag = jnp.diagonal(d5, axis1=1, axis2=3)
    return jnp.moveaxis(diag, 3, 1).reshape(S5_GROUPS, S5_GROUP, S5_STATE)


def _glu_fwd(ygb, wa, wb, x, tm=512, tn=1024):
    t, d = x.shape

    def body(y_ref, wa_ref, wb_ref, x_ref, o_ref, p_ref, q_ref):
        p = _nn(y_ref[...], wa_ref[...])
        q = _nn(y_ref[...], wb_ref[...])
        p_ref[...] = p
        q_ref[...] = q
        o_ref[...] = x_ref[...] + p * jax.nn.sigmoid(q)

    tile = pl.BlockSpec((tm, tn), lambda i, j: (i, j))
    wsp = pl.BlockSpec((d, tn), lambda i, j: (0, j))
    out = jax.ShapeDtypeStruct((t, d), f32)
    return pl.pallas_call(
        body, name="glu_fwd", grid=(t // tm, d // tn),
        in_specs=[pl.BlockSpec((tm, d), lambda i, j: (i, 0)), wsp, wsp, tile],
        out_specs=[tile, tile, tile], out_shape=[out, out, out],
        compiler_params=_cp("parallel", "parallel"),
    )(ygb, wa, wb, x)


def _place():
    x, y, c = lax.axis_index("x"), lax.axis_index("y"), lax.axis_index("c")
    return x, y, c, [(1 - x, y), (x, 1 - y), (1 - x, 1 - y)]


def _all_gather(name, arrays):
    n = len(arrays)

    def body(*refs):
        ins, outs = refs[:n], refs[n:2 * n]
        send_sems, recv_sems, local_sems = refs[2 * n:]
        x, y, c, chips = _place()
        me, sib = (x, y, c), (x, y, 1 - c)

        def copy(i, k, block, to, src=None):
            dst = outs[i].at[4 * block[0] + 2 * block[1] + block[2]]
            return pltpu.make_async_remote_copy(
                src_ref=dst if src is None else src, dst_ref=dst,
                send_sem=send_sems.at[i * 7 + k], recv_sem=recv_sems.at[i * 7 + k],
                device_id=to, device_id_type=MESH)

        mine = [pltpu.make_async_copy(ins[i], outs[i].at[4 * x + 2 * y + c], local_sems.at[i]) for i in range(n)]
        for m in mine:
            m.start()
        first = []
        for i in range(n):
            first.append(copy(i, 0, me, sib, src=ins[i]))
            first += [copy(i, 1 + j, me, (*chip, c), src=ins[i]) for j, chip in enumerate(chips)]
        for cp in first:
            cp.start()
        passed = []
        for j, chip in enumerate(chips):
            for i in range(n):
                copy(i, 1 + j, (*chip, c), me).wait_recv()
                fwd = copy(i, 4 + j, (*chip, c), sib)
                fwd.start()
                passed.append(fwd)
        for i in range(n):
            copy(i, 0, sib, me).wait_recv()
        for j, chip in enumerate(chips):
            for i in range(n):
                copy(i, 4 + j, (*chip, 1 - c), me).wait_recv()
        for cp in first + passed:
            cp.wait_send()
        for m in mine:
            m.wait()

    return pl.pallas_call(
        body, name=name,
        in_specs=[ANY_SPEC] * n, out_specs=[ANY_SPEC] * n,
        out_shape=[jax.ShapeDtypeStruct((N_DEV,) + a.shape, a.dtype) for a in arrays],
        scratch_shapes=[pltpu.SemaphoreType.DMA((7 * n,)), pltpu.SemaphoreType.DMA((7 * n,)),
                        pltpu.SemaphoreType.DMA((n,))],
    )(*arrays)


def _tie(name, x, deps):
    def body(*refs):
        pass

    return pl.pallas_call(
        body, name=name, in_specs=[ANY_SPEC] * (1 + len(deps)), out_specs=ANY_SPEC,
        out_shape=jax.ShapeDtypeStruct(x.shape, x.dtype), input_output_aliases={0: 0},
    )(x, *deps)


def _xchg_copies(kind, srcs, lands, suffixes, send_sems, recv_sems):
    x, y, c, _ = _place()
    copies = []
    for i, (src, land, sfx) in enumerate(zip(srcs, lands, suffixes)):
        for k in range(N_DEV - 1):
            r = k + 1
            peer = (1 - x if r & 4 else x, 1 - y if r & 2 else y, 1 - c if r & 1 else c)
            if kind == "gather":
                s_ref, d_ref = src, land.at[(4 * x + 2 * y + c,) + sfx]
            else:
                s_ref, d_ref = src.at[4 * peer[0] + 2 * peer[1] + peer[2]], land.at[(k,) + sfx]
            copies.append(pltpu.make_async_remote_copy(
                src_ref=s_ref, dst_ref=d_ref, send_sem=send_sems.at[i * 7 + k], recv_sem=recv_sems.at[i * 7 + k],
                device_id=peer, device_id_type=MESH))
    return copies


def _xchg_start(name, kind, srcs, lands, suffixes=None):
    n = len(srcs)
    suffixes = suffixes or [()] * n

    def body(*refs):
        src, land = refs[:n], refs[n:2 * n]
        send_sems, recv_sems, token = refs[2 * n], refs[2 * n + 1], refs[-1]
        for cp in _xchg_copies(kind, src, land, suffixes, send_sems, recv_sems):
            cp.start()
        token[...] = jnp.zeros_like(token)

    arrays = list(srcs) + list(lands)
    outs = pl.pallas_call(
        body, name=name,
        out_shape=(pltpu.SemaphoreType.DMA((7 * n,)), pltpu.SemaphoreType.DMA((7 * n,)),
                   *[pltpu.HBM(a.shape, a.dtype) for a in arrays], jax.ShapeDtypeStruct((8, 128), f32)),
        in_specs=[HBM_SPEC] * (2 * n),
        out_specs=(SEM_SPEC, SEM_SPEC, *[HBM_SPEC] * (2 * n), VMEM_SPEC),
        input_output_aliases={i: 2 + i for i in range(2 * n)},
        compiler_params=pltpu.CompilerParams(has_side_effects=SIDE_EFFECT),
    )(*[pltpu.with_memory_space_constraint(a, pltpu.HBM) for a in arrays])
    return dict(kind=kind, n=n, suffixes=suffixes, send=outs[0], recv=outs[1], srcs=list(outs[2:2 + n]),
                lands=list(outs[2 + n:2 + 2 * n]), token=outs[-1])


def _xchg_wait(name, h, after, lands=None):
    n = h["n"]
    lands = h["lands"] if lands is None else lands

    def body(*refs):
        src, land = refs[:n], refs[n:2 * n]
        for cp in _xchg_copies(h["kind"], src, land, h["suffixes"], refs[2 * n], refs[2 * n + 1]):
            cp.wait_send()
            cp.wait_recv()

    arrays = h["srcs"] + list(lands)
    outs = pl.pallas_call(
        body, name=name,
        out_shape=tuple(pltpu.HBM(a.shape, a.dtype) for a in arrays),
        in_specs=[HBM_SPEC] * (2 * n) + [SEM_SPEC, SEM_SPEC] + [ANY_SPEC] * len(after),
        out_specs=tuple([HBM_SPEC] * (2 * n)),
        input_output_aliases={i: i for i in range(2 * n)},
        compiler_params=pltpu.CompilerParams(has_side_effects=SIDE_EFFECT),
    )(*arrays, h["send"], h["recv"], *after)
    return list(outs[n:])


def _rows(a):
    return a.reshape(-1, a.shape[-1])


def _row_tile(r):
    for tm in (512, 256, 128, 64, 32, 16, 8):
        if r % tm == 0:
            return tm
    return r


def _sum8(name, gathered):
    _, r, n = gathered.shape
    tm = _row_tile(r)

    def body(g_ref, o_ref):
        acc = g_ref[0]
        for k in range(1, N_DEV):
            acc = acc + g_ref[k]
        o_ref[...] = acc

    return pl.pallas_call(
        body, name=name, grid=(r // tm,),
        in_specs=[pl.BlockSpec((N_DEV, tm, n), lambda i: (0, i, 0))],
        out_specs=pl.BlockSpec((tm, n), lambda i: (i, 0)),
        out_shape=jax.ShapeDtypeStruct((r, n), f32),
        compiler_params=_cp("parallel"),
    )(gathered)


def _adamw(name, w, m, v, own, landed=None, slot=None):
    shape = w.shape
    w2, m2, v2 = _rows(w), _rows(m), _rows(v)
    r, n = w2.shape
    tm = _row_tile(r)
    c1 = 1.0 - ADAM_B1 ** ADAM_STEP
    c2 = 1.0 - ADAM_B2 ** ADAM_STEP
    extra = [] if landed is None else [landed.reshape(landed.shape[0], r, n)]
    row = pl.BlockSpec((tm, n), lambda i, *_: (i, 0))
    if slot is None:
        o2, own_spec, scalars = _rows(own), row, []
    else:
        dev, kind = slot
        scalars = [dev.reshape(1).astype(jnp.int32)]
        if kind == "lead":
            o2, own_spec = own.reshape(N_DEV, r, n), pl.BlockSpec((None, tm, n), lambda i, d: (d[0], i, 0))
        elif kind == "rows":
            o2, own_spec = own, pl.BlockSpec((tm, n), lambda i, d: (d[0] * (r // tm) + i, 0))
        else:
            o2, own_spec = own, pl.BlockSpec((tm, n), lambda i, d: (i, d[0]))

    def body(*refs):
        w_ref, m_ref, v_ref, o_ref = refs[len(scalars):len(scalars) + 4]
        refs = refs[len(scalars) + 4:]
        g = o_ref[...]
        if extra:
            for k in range(extra[0].shape[0]):
                g = g + refs[0][k].astype(f32)
        g_ref, d_ref, mn_ref, vn_ref = refs[len(extra):]
        mn = ADAM_B1 * m_ref[...] + (1.0 - ADAM_B1) * g
        vn = ADAM_B2 * v_ref[...] + (1.0 - ADAM_B2) * (g * g)
        g_ref[...] = g
        d_ref[...] = -ADAM_LR * ((mn / c1) / (jnp.sqrt(vn / c2) + ADAM_EPS) + ADAM_WD * w_ref[...])
        mn_ref[...] = mn
        vn_ref[...] = vn

    outs = pl.pallas_call(
        body, name=name,
        grid_spec=pltpu.PrefetchScalarGridSpec(
            num_scalar_prefetch=len(scalars), grid=(r // tm,),
            in_specs=[row] * 3 + [own_spec] + [pl.BlockSpec((e.shape[0], tm, n), lambda i, *_: (0, i, 0)) for e in extra],
            out_specs=[row] * 4),
        out_shape=[jax.ShapeDtypeStruct((r, n), f32)] * 4,
        compiler_params=_cp("parallel"),
    )(*scalars, w2, m2, v2, o2, *extra)
    return [o.reshape(shape) for o in outs]


def _adamw_minor_d(name, w, m, v, own_all, landed, dev, tm=512):
    nl, nh, d, f = w.shape
    wt, mt, vt = (jnp.swapaxes(t, 2, 3) for t in (w, m, v))
    r = nl * nh * d
    per = d // tm
    c1 = 1.0 - ADAM_B1 ** ADAM_STEP
    c2 = 1.0 - ADAM_B2 ** ADAM_STEP

    def body(dev_ref, w_ref, m_ref, v_ref, o_ref, l_ref, g_ref, d_ref, mn_ref, vn_ref):
        g = o_ref[...]
        for k in range(N_DEV - 1):
            g = g + l_ref[k].astype(f32)
        g = g.T
        mn = ADAM_B1 * m_ref[...] + (1.0 - ADAM_B1) * g
        vn = ADAM_B2 * v_ref[...] + (1.0 - ADAM_B2) * (g * g)
        g_ref[...] = g
        d_ref[...] = -ADAM_LR * ((mn / c1) / (jnp.sqrt(vn / c2) + ADAM_EPS) + ADAM_WD * w_ref[...])
        mn_ref[...] = mn
        vn_ref[...] = vn

    par = pl.BlockSpec((None, None, f, tm), lambda i, _: (i // (nh * per), (i // per) % nh, 0, i % per))
    outs = pl.pallas_call(
        body, name=name,
        grid_spec=pltpu.PrefetchScalarGridSpec(
            num_scalar_prefetch=1, grid=(r // tm,),
            in_specs=[par, par, par, pl.BlockSpec((None, tm, f), lambda i, dv: (dv[0], i, 0)),
                      pl.BlockSpec((N_DEV - 1, tm, f), lambda i, _: (0, i, 0))],
            out_specs=[par] * 4),
        out_shape=[jax.ShapeDtypeStruct(wt.shape, f32)] * 4,
        compiler_params=_cp("parallel"),
    )(dev.reshape(1).astype(jnp.int32), wt, mt, vt, own_all.reshape(N_DEV, r, f), landed.reshape(N_DEV - 1, r, f))
    return [jnp.swapaxes(o, 2, 3) for o in outs]


def _pack(arrays):
    flat = jnp.concatenate([a.reshape(-1).astype(f32) for a in arrays])
    pad = (-flat.shape[0]) % (128 * (512 if flat.shape[0] > 128 * 512 else 8))
    return jnp.pad(flat, (0, pad)).reshape(-1, 128)


def _unpack(packed, shapes):
    flat = packed.reshape(-1)
    out, off = [], 0
    for s in shapes:
        n = math.prod(s)
        out.append(flat[off:off + n].reshape(s))
        off += n
    return out


def _local_step(x, target, w, weights_of, send, last_small, on_loss, nb, s):
    cos, sin = _rope_tables(s)
    g = {}
    ffn_saved = {}
    ffn_bufs = [lax.empty((N_DEV, 2, 2) + shp, f32)
                for shp in ((D_MODEL, FF_SHARD), (D_MODEL, FF_SHARD), (FF_SHARD, D_MODEL))]

    def ffn(xin, l, h, wts):
        y, a, b, hbt, ut = _ffn_fwd(f"ffn_fwd_{l}{h}", xin, w["ffn_g"][l][h], *wts)
        ffn_saved[(l, h)] = (xin, a, b, hbt, ut, wts)
        return y

    def ffn_back(dy, l, h):
        xin, a, b, hb, u, wts = ffn_saved[(l, h)]
        dx, dg, dyh, da, db = _ffn_dx(f"ffn_dx_{l}{h}", dy, xin, w["ffn_g"][l][h], *wts, a, b)
        g[f"ffn_g_{l}{h}"] = dg
        if (l, h) != (0, 0):
            ffn_bufs[:2], (h1, h3) = _ffn_dw(f"ffn_dw_{l}{h}_w13", hb, [da, db], ffn_bufs[:2], l, h)
            ffn_bufs[2:], (h2,) = _ffn_dw(f"ffn_dw_{l}{h}_w2", u, [dyh], ffn_bufs[2:], l, h)
            return send(f"ffn_{l}{h}", {"ffn_w1": h1, "ffn_w3": h3, "ffn_w2": h2}, dx)
        hb = last_small(g, hb)
        ffn_bufs[:1], (half,) = _ffn_dw("ffn_dw_00_w1", hb, [da], ffn_bufs[:1], l, h)
        hb = send("ffn_00_w1", {"ffn_w1": half}, hb)
        ffn_bufs[1:2], (half,) = _ffn_dw("ffn_dw_00_w3", hb, [db], ffn_bufs[1:2], l, h)
        u = send("ffn_00_w3", {"ffn_w3": half}, u)
        ffn_bufs[2:], (half,) = _ffn_dw("ffn_dw_00_w2", u, [dyh], ffn_bufs[2:], l, h)
        return send("ffn_00_w2", {"ffn_w2": half}, dx)

    def slots(t):
        return t.reshape(N_DEV, D_MODEL // N_DEV, D_MODEL)

    x1 = ffn(x, 0, 0, weights_of(0, [])["ffn"])
    wg = weights_of(1, [x1])
    w_in, w_out = wg["w_in"], wg["w_out"]
    h0b = _norm_fwd("mix_norm_0", x1, w["mix_g"][0], bf16)
    proj = _mm("in_proj", h0b, w_in, "nn", tn=1536)[0]
    o_raw, rprev, mret = _ret_fwd(proj, cos, sin, w["ret_g"], nb, s)
    lru, lru_h = _lru_fwd(proj, w["conv_w"], w["conv_b"], w["lru_w_a"], w["lru_b_a"], w["lru_w_i"], w["lru_b_i"], w["lru_lam"], nb, s)
    merged = _ew("merge", lambda a, b: (jnp.concatenate([a, b], axis=1),), [mret, lru], [(D_MODEL, bf16)])[0]
    x2 = _mm("out_proj", merged, w_out, "nn", extras=[x1], epilogue=lambda acc, r: (acc + r,))[0]
    x3 = ffn(x2, 0, 1, weights_of(2, [x2])["ffn"])
    x4 = ffn(x3, 1, 0, weights_of(3, [x3])["ffn"])
    u = _norm_fwd("mix_norm_1", x4, w["mix_g"][1], f32)
    lbr, lbi, bbr, bbi = _s5_prep(w["s5_lr"], w["s5_li"], w["s5_ldt"], w["s5_bre"], w["s5_bim"])
    lbr_f, lbi_f = lbr.reshape(1, -1), lbi.reshape(1, -1)
    wbr, wbi = _blockdiag(bbr).astype(bf16), _blockdiag(bbi).astype(bf16)
    wcr, wci = _blockdiag(w["s5_cre"]).astype(bf16), _blockdiag(w["s5_cim"]).astype(bf16)
    ygb, ypre, h0s, hrs, his = _s5_fwd(u, lbr_f, lbi_f, wbr, wbi, wcr, wci, w["s5_d"], nb, s)
    wg = weights_of(4, [ygb])
    glu_a, glu_b = wg["glu_a"], wg["glu_b"]
    x5, gp, gq = _glu_fwd(ygb, glu_a, glu_b, x4)
    x6 = ffn(x5, 1, 1, weights_of(5, [x5])["ffn"])
    loss, dx6, g["final_g"] = _final_loss(x6, w["final_g"], target)
    dx6 = on_loss(loss, dx6)

    dx5 = ffn_back(dx6, 1, 1)

    def glu_bwd(d, p, q):
        sg = jax.nn.sigmoid(q)
        return d * sg, d * p * sg * (1.0 - sg)

    dp, dq = _ew("glu_bwd", glu_bwd, [dx5, gp, gq], [(D_MODEL, bf16), (D_MODEL, bf16)])
    dyg = _mm("glu_dy_a", dp, glu_a, "nt")[0]
    dyg = _mm("glu_dy_b", dq, glu_b, "nt", extras=[dyg], epilogue=lambda acc, r: (acc + r,))[0]
    g["glu_a"], ga_half = _mm_tn("glu_dw_a", ygb, dp)
    g["glu_b"], gb_half = _mm_tn("glu_dw_b", ygb, dq)
    dyg = send("glu", {"glu_a": slots(ga_half), "glu_b": slots(gb_half)}, dyg)
    du, dlr, dli, dwbr, dwbi, dwcr, dwci, g["s5_d"] = _s5_bwd(dyg, ypre, u, h0s, hrs, his, lbr_f, lbi_f, wbr, wbi, wcr, wci, w["s5_d"], nb, s)
    g["s5_cre"], g["s5_cim"] = _blockdiag_t(dwcr), _blockdiag_t(dwci)
    g["s5_lr"], g["s5_li"], g["s5_ldt"], g["s5_bre"], g["s5_bim"] = _s5_prep_bwd(
        w["s5_lr"], w["s5_li"], w["s5_ldt"], w["s5_bre"], w["s5_bim"],
        (dlr.reshape(S5_GROUPS, S5_STATE), dli.reshape(S5_GROUPS, S5_STATE), _blockdiag_t(dwbr), _blockdiag_t(dwbi)))
    dx4, g["mix_g_1"] = _norm_bwd("mix_norm_1_bwd", du, x4, w["mix_g"][1], dx5)
    dx3 = ffn_back(dx4, 1, 0)
    dx2 = ffn_back(dx3, 0, 1)
    dmerged = _mm("out_proj_dx", dx2, w_out, "nt")[0]
    g["w_out"], wo_half = _mm_tn("out_proj_dw", merged, dx2)
    dmerged = send("w_out", {"w_out": slots(wo_half)}, dmerged)
    dq_, dk_, dv_, dgate, g["ret_g"] = _ret_bwd(dmerged, o_raw, rprev, proj, cos, sin, w["ret_g"], nb, s)
    (dxl, dgl, g["conv_w"], g["conv_b"], g["lru_w_a"], g["lru_b_a"], g["lru_w_i"], g["lru_b_i"], g["lru_lam"]) = _lru_bwd(
        dmerged, lru_h, proj, w["conv_w"], w["conv_b"], w["lru_w_a"], w["lru_b_a"], w["lru_w_i"], w["lru_b_i"], w["lru_lam"],
        nb, s)
    dproj = _ew("dproj", lambda *p: (jnp.concatenate(p, axis=1),), [dq_, dk_, dv_, dgate, dxl, dgl], [(3072, bf16)])[0]
    dh0 = _mm("in_proj_dx", dproj, w_in, "nt")[0]
    g["w_in"], wi_half = _mm_tn("in_proj_dw", h0b, dproj)
    dh0 = send("w_in", {"w_in": jnp.transpose(wi_half.reshape(D_MODEL, N_DEV, IN_SHARD), (1, 0, 2))}, dh0)
    dx1, g["mix_g_0"] = _norm_bwd("mix_norm_0_bwd", dh0, x1, w["mix_g"][0], dx2)
    dx0 = ffn_back(dx1, 0, 0)
    g["ffn_w1"], g["ffn_w3"], g["ffn_w2"] = ffn_bufs
    return loss, dx0, g


_WEIGHTS = ["ffn_norm_g", "ffn_w1", "ffn_w3", "ffn_w2", "mix_norm_g", "w_in_even", "w_out_even", "ret_norm_g", "conv_w",
            "conv_b", "lru_w_a", "lru_b_a", "lru_w_i", "lru_b_i", "lru_lambda", "s5_lambda_re", "s5_lambda_im", "s5_log_dt",
            "s5_b_re", "s5_b_im", "s5_c_re", "s5_c_im", "s5_d", "glu_w_a", "glu_w_b", "final_norm_g"]
_BIG = ["ffn_w1", "ffn_w3", "ffn_w2", "w_in_even", "w_out_even", "glu_w_a", "glu_w_b"]
_SMALL_SHARDED = ["ffn_norm_g", "conv_w", "s5_d"]
_SMALL = [n for n in _WEIGHTS if n not in _BIG]
_MIDSIZE = ["lru_w_a", "lru_w_i", "s5_b_re", "s5_b_im", "s5_c_re", "s5_c_im"]


def kernel(x, ffn_norm_g, ffn_w1, ffn_w3, ffn_w2, mix_norm_g, w_in_even, w_out_even, ret_norm_g, conv_w, conv_b, lru_w_a, lru_b_a, lru_w_i, lru_b_i, lru_lambda, s5_lambda_re, s5_lambda_im, s5_log_dt, s5_b_re, s5_b_im, s5_c_re, s5_c_im, s5_d, glu_w_a, glu_w_b, final_norm_g, loss_target, m_ffn_norm_g, m_ffn_w1, m_ffn_w3, m_ffn_w2, m_mix_norm_g, m_w_in_even, m_w_out_even, m_ret_norm_g, m_conv_w, m_conv_b, m_lru_w_a, m_lru_b_a, m_lru_w_i, m_lru_b_i, m_lru_lambda, m_s5_lambda_re, m_s5_lambda_im, m_s5_log_dt, m_s5_b_re, m_s5_b_im, m_s5_c_re, m_s5_c_im, m_s5_d, m_glu_w_a, m_glu_w_b, m_final_norm_g, v_ffn_norm_g, v_ffn_w1, v_ffn_w3, v_ffn_w2, v_mix_norm_g, v_w_in_even, v_w_out_even, v_ret_norm_g, v_conv_w, v_conv_b, v_lru_w_a, v_lru_b_a, v_lru_w_i, v_lru_b_i, v_lru_lambda, v_s5_lambda_re, v_s5_lambda_im, v_s5_log_dt, v_s5_b_re, v_s5_b_im, v_s5_c_re, v_s5_c_im, v_s5_d, v_glu_w_a, v_glu_w_b, v_final_norm_g):
    a = dict(locals())
    nb, s, d = x.shape
    dev = 4 * lax.axis_index("x") + 2 * lax.axis_index("y") + lax.axis_index("c")

    def ffn_shards(l, h):
        extra = FF_PAD - FF_SHARD
        return [jnp.pad(ffn_w1[l, h].astype(bf16), ((0, 0), (0, extra))), jnp.pad(ffn_w3[l, h].astype(bf16), ((0, 0), (0, extra))),
                jnp.pad(ffn_w2[l, h].astype(bf16), ((0, extra), (0, 0)))]

    first = _all_gather("ag_first", ffn_shards(0, 0) + [_pack([ffn_norm_g, conv_w, s5_d])])
    sm = first[3].reshape(N_DEV, -1)
    ffn_g_full = jnp.transpose(sm[:, :512].reshape(N_DEV, 2, 2, 128), (1, 2, 0, 3)).reshape(2, 2, D_MODEL)
    conv_w_full = jnp.transpose(sm[:, 512:768].reshape(N_DEV, 4, 64), (1, 0, 2)).reshape(4, LRU_WIDTH)
    s5_d_full = sm[:, 768:896].reshape(1, D_MODEL)

    ag_src = [None, [w_in_even[0].astype(bf16), w_out_even[0].astype(bf16)], ffn_shards(0, 1), ffn_shards(1, 0),
              [glu_w_a[0].astype(bf16), glu_w_b[0].astype(bf16)], ffn_shards(1, 1)]
    ag, token = [None], first[0]
    for k, grp in enumerate(ag_src):
        if grp is None:
            continue
        grp[0] = _tie(f"tie_ag_{k}", grp[0], [token])
        lands = [lax.dynamic_update_index_in_dim(lax.empty((N_DEV,) + t.shape, bf16), t, dev, 0) for t in grp]
        ag.append(_xchg_start(f"ag_start_{k}", "gather", grp, lands))
        token = ag[-1]["token"]

    def weights_of(k, after):
        if k == 0:
            return {"ffn": [first[0], _tie("tie_ag_started", first[1], [h["token"] for h in ag[1:]]), first[2]]}
        got = _xchg_wait(f"ag_wait_{k}", ag[k], after)
        if k == 1:
            return {"w_in": jnp.transpose(got[0], (1, 0, 2)).reshape(D_MODEL, N_DEV * IN_SHARD),
                    "w_out": got[1].reshape(D_MODEL, D_MODEL)}
        if k == 4:
            return {"glu_a": got[0].reshape(D_MODEL, D_MODEL), "glu_b": got[1].reshape(D_MODEL, D_MODEL)}
        return {"ffn": got}

    ffn_lands = [lax.empty((N_DEV - 1, 2, 2) + shp, bf16)
                 for shp in ((D_MODEL, FF_SHARD), (D_MODEL, FF_SHARD), (FF_SHARD, D_MODEL))]
    rs = []

    ffn_names = ("ffn_w1", "ffn_w3", "ffn_w2")

    def send(group, arrays, carry):
        srcs = list(arrays.values())
        if group.startswith("ffn_"):
            which = [ffn_names.index(n) for n in arrays]
            sfx = [(int(group[4]), int(group[5]))] * len(which)
            h = _xchg_start("rs_start_" + group, "scatter", srcs, [ffn_lands[k] for k in which], sfx)
            for k, land in zip(which, h["lands"]):
                ffn_lands[k] = land
        else:
            h = _xchg_start("rs_start_" + group, "scatter", srcs,
                            [lax.empty((N_DEV - 1,) + t.shape[1:], bf16) for t in srcs])
        rs.append((group, list(arrays), h))
        return _tie("tie_" + group, carry, [h["token"]])

    w = {
        "ffn_g": [[ffn_g_full[l, h].reshape(1, D_MODEL) for h in range(2)] for l in range(2)],
        "mix_g": [mix_norm_g[0:1], mix_norm_g[1:2]],
        "ret_g": ret_norm_g, "conv_w": conv_w_full, "conv_b": conv_b,
        "lru_w_a": lru_w_a[0], "lru_b_a": lru_b_a, "lru_w_i": lru_w_i[0], "lru_b_i": lru_b_i, "lru_lam": lru_lambda,
        "s5_lr": s5_lambda_re[0], "s5_li": s5_lambda_im[0], "s5_ldt": s5_log_dt.reshape(S5_GROUPS, 1),
        "s5_bre": jnp.swapaxes(s5_b_re[0], 1, 2), "s5_bim": jnp.swapaxes(s5_b_im[0], 1, 2),
        "s5_cre": s5_c_re[0], "s5_cim": s5_c_im[0], "s5_d": s5_d_full,
        "final_g": final_norm_g.reshape(1, D_MODEL),
    }

    small_grads = {}

    def last_small(g, carry):
        part = _small_partials(g)
        mine = _pack([part[n] for n in _SMALL])
        land = lax.dynamic_update_index_in_dim(lax.empty((N_DEV,) + mine.shape, f32), mine, dev, 0)
        h = _xchg_start("ag_start_small_grads", "gather", [mine], [land])
        small_grads.update(h=h, shapes=[part[n].shape for n in _SMALL])
        return _tie("tie_small_grads", carry, [h["token"]])

    total_loss = []

    def on_loss(part, carry):
        total_loss.append(lax.psum(part[0, 0], ("x", "y", "c")))
        return _tie("tie_loss", carry, [jnp.broadcast_to(total_loss[0], (8, 128))])

    _, dx, g = _local_step(x.reshape(nb * s, d), loss_target.reshape(nb * s, d), w, weights_of, send, last_small,
                           on_loss, nb, s)
    loss = total_loss[0]
    (gath,) = _xchg_wait("ag_wait_small_grads", small_grads["h"], [dx])
    full = dict(zip(_SMALL, _unpack(_sum8("sum_small_grads", gath), small_grads["shapes"])))
    for n in _SMALL_SHARDED:
        width = a[n].shape[-1]
        full[n] = lax.dynamic_slice_in_dim(full[n], dev * width, width, axis=full[n].ndim - 1)
    res = {n: _adamw("adamw_" + n, a[n], a["m_" + n], a["v_" + n], full[n]) for n in _MIDSIZE}
    tiny = [n for n in _SMALL if n not in _MIDSIZE]
    shapes = [a[n].shape for n in tiny]
    packed = _adamw("adamw_small", _pack([a[n] for n in tiny]), _pack([a["m_" + n] for n in tiny]),
                    _pack([a["v_" + n] for n in tiny]), _pack([full[n] for n in tiny]))
    res.update({n: vals for n, vals in zip(tiny, zip(*[_unpack(p, shapes) for p in packed]))})
    return _finish(a, g, dx, loss, res, packed, rs, ffn_lands, dev, nb, s, d)


def _small_partials(g):
    return {
        "ffn_norm_g": jnp.stack([jnp.stack([g[f"ffn_g_{l}{h}"][0] for h in range(2)]) for l in range(2)]),
        "mix_norm_g": jnp.concatenate([g["mix_g_0"], g["mix_g_1"]], axis=0),
        "ret_norm_g": g["ret_g"], "conv_w": g["conv_w"][None], "conv_b": g["conv_b"],
        "lru_w_a": g["lru_w_a"][None], "lru_b_a": g["lru_b_a"], "lru_w_i": g["lru_w_i"][None], "lru_b_i": g["lru_b_i"],
        "lru_lambda": g["lru_lam"], "s5_lambda_re": g["s5_lr"][None], "s5_lambda_im": g["s5_li"][None],
        "s5_log_dt": g["s5_ldt"].reshape(1, S5_GROUPS),
        "s5_b_re": jnp.swapaxes(g["s5_bre"], 1, 2)[None], "s5_b_im": jnp.swapaxes(g["s5_bim"], 1, 2)[None],
        "s5_c_re": g["s5_cre"][None], "s5_c_im": g["s5_cim"][None], "s5_d": g["s5_d"], "final_norm_g": g["final_g"][0],
    }


def _finish(a, g, dx, loss, res, packed, rs, ffn_lands, dev, nb, s, d):
    landed = {}
    for group, names, h in rs:
        if not group.startswith("ffn_"):
            landed.update(zip(names, _xchg_wait("rs_wait_" + group, h, [dx])))
    kinds = {"ffn_w1": "lead", "ffn_w3": "lead", "ffn_w2": "lead", "w_in": "cols", "w_out": "rows", "glu_a": "rows",
             "glu_b": "rows"}

    def update(n, short):
        if short in ("ffn_w1", "ffn_w3"):
            res[n] = _adamw_minor_d("adamw_" + n, a[n], a["m_" + n], a["v_" + n], g[short], landed[short], dev)
        else:
            res[n] = _adamw("adamw_" + n, a[n], a["m_" + n], a["v_" + n], g[short],
                            landed[short].reshape((N_DEV - 1,) + a[n].shape), slot=(dev, kinds[short]))

    for n, short in zip(_BIG[3:], ("w_in", "w_out", "glu_a", "glu_b")):
        update(n, short)
    after = [dx, packed[0]] + [res[n][0] for n in _BIG[3:] + _MIDSIZE]
    ffn_names = ("ffn_w1", "ffn_w3", "ffn_w2")
    for group, names, h in rs:
        if group.startswith("ffn_") and len(names) == 3:
            ffn_lands[:] = _xchg_wait("rs_wait_" + group, h, after, ffn_lands)
    for k, n in enumerate(ffn_names):
        for group, names, h in rs:
            if group.startswith("ffn_") and names == [n]:
                (ffn_lands[k],) = _xchg_wait("rs_wait_" + group, h, after, [ffn_lands[k]])
        landed[n] = ffn_lands[k]
        update(n, n)
        after = after + [res[n][0]]

    out = [loss, dx.reshape(nb, s, d)]
    for k in range(4):
        out += [res[n][k] for n in _WEIGHTS]
    return tuple(out)
```

```python
import math

import numpy as np
import jax
import jax.numpy as jnp
from jax import lax
from jax.experimental import pallas as pl
from jax.experimental.pallas import tpu as pltpu

f32 = jnp.float32
bf16 = jnp.bfloat16

D_MODEL = 1024
N_DEV = 8
EPS = 1e-6
RET_HEADS = 4
HEAD_DIM = 128
RET_WIDTH = 512
RET_CHUNK = 128
ROPE_BASE = 10000.0
LRU_WIDTH = 512
LRU_BLOCKS = 4
LRU_C = 8.0
S5_GROUP = 16
S5_GROUPS = 64
S5_STATE = 64
S5_CHUNK = 1024
S5_BLOCKS = 8
S5_BLOCK_STATES = 512
SUBLANES = 8
D_FF = 2816
FF_SHARD = D_FF // N_DEV
FF_PAD = 384
IN_SHARD = 3072 // N_DEV
ADAM_LR = 0.001
ADAM_B1 = 0.9
ADAM_B2 = 0.999
ADAM_EPS = 1e-08
ADAM_WD = 0.01
ADAM_STEP = 10

VMEM_LIMIT = 56 * 1024 * 1024
VMEM_SPEC = pl.BlockSpec(memory_space=pltpu.VMEM)
ANY_SPEC = pl.BlockSpec(memory_space=pl.ANY)
HBM_SPEC = pl.BlockSpec(memory_space=pltpu.HBM)
SEM_SPEC = pl.BlockSpec(memory_space=pltpu.SEMAPHORE)
SIDE_EFFECT = pltpu.SideEffectType.DATAFLOW_SIDE_EFFECTING
MESH = pl.DeviceIdType.MESH


def _cp(*sem):
    return pltpu.CompilerParams(dimension_semantics=sem, vmem_limit_bytes=VMEM_LIMIT)


def _nn(a, b):
    return jnp.dot(a, b, preferred_element_type=f32)


def _nt(a, b):
    return lax.dot_general(a, b, (((1,), (1,)), ((), ())), preferred_element_type=f32)


def _tn(a, b):
    return lax.dot_general(a, b, (((0,), (0,)), ((), ())), preferred_element_type=f32)


def _rms_fwd(x, g):
    r = lax.rsqrt(jnp.mean(x * x, axis=-1, keepdims=True) + EPS)
    xn = x * r
    return xn * g, xn, r


def _rms_bwd(dh, xn, r, g):
    dxn = dh * g
    dx = r * (dxn - xn * jnp.mean(dxn * xn, axis=-1, keepdims=True))
    dg = jnp.sum(dh * xn, axis=0, keepdims=True)
    return dx, dg


def _shift_dn(v, d, row, fill=0.0):
    return jnp.where(row >= d, pltpu.roll(v, d, 0), fill)


def _shift_up(v, d, row, fill=0.0):
    n = v.shape[0]
    return jnp.where(row < n - d, pltpu.roll(v, n - d, 0), fill)


def _ew(name, fn, ins, outs, tm=512):
    t = ins[0].shape[0]
    n_in = len(ins)

    def body(*refs):
        res = fn(*[r[...] for r in refs[:n_in]])
        for o, v in zip(refs[n_in:], res):
            o[...] = v.astype(o.dtype)

    return pl.pallas_call(
        body, name=name, grid=(t // tm,),
        in_specs=[pl.BlockSpec((tm, a.shape[1]), lambda i: (i, 0)) for a in ins],
        out_specs=[pl.BlockSpec((tm, n), lambda i: (i, 0)) for n, _ in outs],
        out_shape=[jax.ShapeDtypeStruct((t, n), dt) for n, dt in outs],
        compiler_params=_cp("parallel"),
    )(*ins)


def _mm(name, x, w, kind, extras=(), epilogue=None, outs=None, tm=512, tn=1024):
    t = x.shape[0]
    n = w.shape[1] if kind == "nn" else w.shape[0]
    tn = min(tn, n)
    outs = outs or [f32]
    n_ex = len(extras)

    def body(x_ref, w_ref, *refs):
        xb = x_ref[...].astype(bf16)
        acc = _nn(xb, w_ref[...]) if kind == "nn" else _nt(xb, w_ref[...])
        res = epilogue(acc, *[r[...] for r in refs[:n_ex]]) if epilogue else (acc,)
        for o, v in zip(refs[n_ex:], res):
            o[...] = v.astype(o.dtype)

    w_spec = (pl.BlockSpec((w.shape[0], tn), lambda i, j: (0, j)) if kind == "nn"
              else pl.BlockSpec((tn, w.shape[1]), lambda i, j: (j, 0)))
    tile = pl.BlockSpec((tm, tn), lambda i, j: (i, j))
    return pl.pallas_call(
        body, name=name, grid=(t // tm, n // tn),
        in_specs=[pl.BlockSpec((tm, x.shape[1]), lambda i, j: (i, 0)), w_spec] + [tile] * n_ex,
        out_specs=[tile] * len(outs),
        out_shape=[jax.ShapeDtypeStruct((t, n), dt) for dt in outs],
        compiler_params=_cp("parallel", "parallel"),
    )(x, w, *extras)


def _mm_tn(name, x, y, tk=1024, tn=1024, tt=1024):
    t, k = x.shape
    n = y.shape[1]
    tk, tn, tt = min(tk, k), min(tn, n), min(tt, t)

    def body(x_ref, y_ref, o_ref, ob_ref):
        @pl.when(pl.program_id(2) == 0)
        def _():
            o_ref[...] = jnp.zeros_like(o_ref)
        o_ref[...] += _tn(x_ref[...].astype(bf16), y_ref[...].astype(bf16))

        @pl.when(pl.program_id(2) == pl.num_programs(2) - 1)
        def _():
            ob_ref[...] = o_ref[...].astype(bf16)

    out = pl.BlockSpec((tk, tn), lambda i, j, s: (i, j))
    return pl.pallas_call(
        body, name=name, grid=(k // tk, n // tn, t // tt),
        in_specs=[pl.BlockSpec((tt, tk), lambda i, j, s: (s, i)), pl.BlockSpec((tt, tn), lambda i, j, s: (s, j))],
        out_specs=[out, out],
        out_shape=[jax.ShapeDtypeStruct((k, n), f32), jax.ShapeDtypeStruct((k, n), bf16)],
        compiler_params=_cp("parallel", "parallel", "arbitrary"),
    )(x, y)


def _norm_fwd(name, x, g, dtype, tm=512):
    t, d = x.shape

    def body(x_ref, g_ref, h_ref):
        h_ref[...] = _rms_fwd(x_ref[...], g_ref[...])[0].astype(dtype)

    row = pl.BlockSpec((tm, d), lambda i: (i, 0))
    return pl.pallas_call(
        body, name=name, grid=(t // tm,),
        in_specs=[row, pl.BlockSpec((1, d), lambda i: (0, 0))],
        out_specs=row, out_shape=jax.ShapeDtypeStruct((t, d), dtype),
        compiler_params=_cp("parallel"),
    )(x, g)


def _norm_bwd(name, dh, x, g, dres, tm=512):
    t, d = x.shape

    def body(dh_ref, x_ref, g_ref, dres_ref, dx_ref, dg_ref):
        gv = g_ref[...]
        _, xn, r = _rms_fwd(x_ref[...], gv)
        dx, dg = _rms_bwd(dh_ref[...], xn, r, gv)
        dx_ref[...] = dres_ref[...] + dx

        @pl.when(pl.program_id(0) == 0)
        def _():
            dg_ref[...] = jnp.zeros_like(dg_ref)
        dg_ref[...] += dg

    row = pl.BlockSpec((tm, d), lambda i: (i, 0))
    vec = pl.BlockSpec((1, d), lambda i: (0, 0))
    return pl.pallas_call(
        body, name=name, grid=(t // tm,),
        in_specs=[row, row, vec, row],
        out_specs=[row, vec],
        out_shape=[jax.ShapeDtypeStruct((t, d), f32), jax.ShapeDtypeStruct((1, d), f32)],
        compiler_params=_cp("arbitrary"),
    )(dh, x, g, dres)


def _final_loss(x, g, target, tm=512):
    t, d = x.shape

    def body(x_ref, g_ref, t_ref, loss_ref, dx_ref, dg_ref):
        gv = g_ref[...]
        y, xn, r = _rms_fwd(x_ref[...], gv)
        err = y - t_ref[...]
        dy = err * (1.0 / d)
        dx, dg = _rms_bwd(dy, xn, r, gv)
        dx_ref[...] = dx

        @pl.when(pl.program_id(0) == 0)
        def _():
            dg_ref[...] = jnp.zeros_like(dg_ref)
            loss_ref[...] = jnp.zeros_like(loss_ref)
        dg_ref[...] += dg
        loss_ref[...] += jnp.full((1, 128), 0.5 / d, f32) * jnp.sum(err * err)

    row = pl.BlockSpec((tm, d), lambda i: (i, 0))
    vec = pl.BlockSpec((1, d), lambda i: (0, 0))
    return pl.pallas_call(
        body, name="final_loss", grid=(t // tm,),
        in_specs=[row, vec, row],
        out_specs=[pl.BlockSpec((1, 128), lambda i: (0, 0)), row, vec],
        out_shape=[jax.ShapeDtypeStruct((1, 128), f32), jax.ShapeDtypeStruct((t, d), f32),
                   jax.ShapeDtypeStruct((1, d), f32)],
        compiler_params=_cp("arbitrary"),
    )(x, g, target)


def _load_ffn_weights(hbm_refs, vmem_refs, sems):
    @pl.when(pl.program_id(0) == 0)
    def _():
        copies = []
        for k, (src, dst) in enumerate(zip(hbm_refs, vmem_refs)):
            for j in range(N_DEV):
                half = pl.ds((j % 2) * FF_PAD, FF_PAD)
                window = dst.at[j // 2, half, :]
                copies.append(pltpu.make_async_copy(src.at[j], window, sems.at[k * N_DEV + j]))
        for cp in copies:
            cp.start()
        for cp in copies:
            cp.wait()


def _ffn_weight_scratch(nj, d, ff):
    return [pltpu.VMEM((nj, ff, d), bf16), pltpu.VMEM((nj, ff, d), bf16), pltpu.VMEM((nj, ff, d), bf16),
            pltpu.SemaphoreType.DMA((3 * N_DEV,))]


def _ffn_fwd(name, x, g, w1, w3, w2, tm=512):
    t, d = x.shape
    nj, ff = N_DEV // 2, 2 * FF_PAD

    def body(x_ref, g_ref, w1_hbm, w3_hbm, w2_hbm, y_ref, a_ref, b_ref, w1_ref, w3_ref, w2_ref, sems):
        _load_ffn_weights((w1_hbm, w3_hbm, w2_hbm), (w1_ref, w3_ref, w2_ref), sems)
        xv = x_ref[...]
        h, _, _ = _rms_fwd(xv, g_ref[...])
        hb = h.astype(bf16)
        acc = jnp.zeros((tm, d), f32)
        for j in range(nj):
            a = _nt(hb, w1_ref[j])
            b = _nt(hb, w3_ref[j])
            a_ref[j] = a.astype(bf16)
            b_ref[j] = b.astype(bf16)
            u = (a * jax.nn.sigmoid(a) * b).astype(bf16)
            acc = acc + _nn(u, w2_ref[j])
        y_ref[...] = xv + 0.5 * acc

    row = pl.BlockSpec((tm, d), lambda i: (i, 0))
    mid = pl.BlockSpec((nj, tm, ff), lambda i: (0, i, 0))
    return pl.pallas_call(
        body, name=name, grid=(t // tm,),
        in_specs=[row, pl.BlockSpec((1, d), lambda i: (0, 0)), ANY_SPEC, ANY_SPEC, ANY_SPEC],
        out_specs=[row, mid, mid],
        out_shape=[jax.ShapeDtypeStruct((t, d), f32), jax.ShapeDtypeStruct((nj, t, ff), bf16),
                   jax.ShapeDtypeStruct((nj, t, ff), bf16)],
        scratch_shapes=_ffn_weight_scratch(nj, d, ff),
        compiler_params=_cp("arbitrary"),
    )(x, g, w1, w3, w2)


def _ffn_dx(name, dy, x, g, w1, w3, w2, a, b, tm=256):
    t, d = x.shape
    nj, ff = N_DEV // 2, 2 * FF_PAD

    def body(dy_ref, x_ref, g_ref, w1_hbm, w3_hbm, w2_hbm, a_ref, b_ref,
             dx_ref, dg_ref, hbt_ref, dyh_ref, ut_ref, da_ref, db_ref, w1_ref, w3_ref, w2_ref, sems):
        _load_ffn_weights((w1_hbm, w3_hbm, w2_hbm), (w1_ref, w3_ref, w2_ref), sems)
        gv = g_ref[...]
        h, xn, r = _rms_fwd(x_ref[...], gv)
        hbt_ref[...] = h.astype(bf16).T
        dyv = dy_ref[...]
        dyh = (0.5 * dyv).astype(bf16)
        dyh_ref[...] = dyh
        dh = jnp.zeros((tm, d), f32)
        dus = [_nt(dyh, w2_ref[j]) for j in range(nj)]
        for j in range(nj):
            av = a_ref[j].astype(f32)
            bv = b_ref[j].astype(f32)
            s = jax.nn.sigmoid(av)
            silu = av * s
            ut_ref[j] = (silu * bv).astype(bf16).T
            du = dus[j]
            dab = (du * bv * (s * (1.0 + av * (1.0 - s)))).astype(bf16)
            dbb = (du * silu).astype(bf16)
            da_ref[j] = dab
            db_ref[j] = dbb
            dh = dh + _nn(dab, w1_ref[j]) + _nn(dbb, w3_ref[j])
        dx, dg = _rms_bwd(dh, xn, r, gv)
        dx_ref[...] = dyv + dx

        @pl.when(pl.program_id(0) == 0)
        def _():
            dg_ref[...] = jnp.zeros_like(dg_ref)
        dg_ref[...] += dg

    row = pl.BlockSpec((tm, d), lambda i: (i, 0))
    vec = pl.BlockSpec((1, d), lambda i: (0, 0))
    mid = pl.BlockSpec((nj, tm, ff), lambda i: (0, i, 0))
    mid_shape = jax.ShapeDtypeStruct((nj, t, ff), bf16)
    return pl.pallas_call(
        body, name=name, grid=(t // tm,),
        in_specs=[row, row, vec, ANY_SPEC, ANY_SPEC, ANY_SPEC, mid, mid],
        out_specs=[row, vec, pl.BlockSpec((d, tm), lambda i: (0, i)), row,
                   pl.BlockSpec((nj, ff, tm), lambda i: (0, 0, i)), mid, mid],
        out_shape=[jax.ShapeDtypeStruct((t, d), f32), jax.ShapeDtypeStruct((1, d), f32),
                   jax.ShapeDtypeStruct((d, t), bf16), jax.ShapeDtypeStruct((t, d), bf16),
                   jax.ShapeDtypeStruct((nj, ff, t), bf16), mid_shape, mid_shape],
        scratch_shapes=_ffn_weight_scratch(nj, d, ff),
        compiler_params=_cp("arbitrary"),
    )(dy, x, g, w1, w3, w2, a, b)


def _ffn_dw(name, xt, ys, bufs, l, h, tt=2048):
    n = len(ys)
    t = ys[0].shape[-2]
    tt = min(tt, t)
    cut_cols = xt.ndim == 2

    def body(x_ref, *refs):
        y_refs, outs, accs = refs[:n], refs[2 * n:4 * n], refs[4 * n:]
        s = pl.program_id(1)
        xv = x_ref[0] if xt.ndim == 3 else x_ref[...]
        for k in range(n):
            prod = _nn(xv, y_refs[k][0] if ys[k].ndim == 3 else y_refs[k][...])

            @pl.when(s == 0)
            def _():
                accs[k][...] = prod

            @pl.when(s > 0)
            def _():
                accs[k][...] += prod

        @pl.when(s == pl.num_programs(1) - 1)
        def _():
            for k in range(n):
                total = accs[k][...]
                for e in range(2):
                    lo = e * FF_PAD
                    part = total[:, lo:lo + FF_SHARD] if cut_cols else total[lo:lo + FF_SHARD, :]
                    outs[k][e] = part
                    outs[n + k][e] = part.astype(bf16)

    x_spec = (pl.BlockSpec((1, xt.shape[1], tt), lambda p, s: (p, 0, s)) if xt.ndim == 3
              else pl.BlockSpec((xt.shape[0], tt), lambda p, s: (0, s)))
    y_specs = [pl.BlockSpec((1, tt, y.shape[2]), lambda p, s: (p, s, 0)) if y.ndim == 3
               else pl.BlockSpec((tt, y.shape[1]), lambda p, s: (s, 0)) for y in ys]
    dims = [b.shape[-2:] for b in bufs]
    outs = pl.pallas_call(
        body, name=name, grid=(N_DEV // 2, t // tt),
        in_specs=[x_spec] + y_specs + [ANY_SPEC] * n,
        out_specs=[pl.BlockSpec((2, None, None, k_, n_), lambda p, s: (p, l, h, 0, 0)) for k_, n_ in dims]
        + [pl.BlockSpec((2, k_, n_), lambda p, s: (p, 0, 0)) for k_, n_ in dims],
        out_shape=[jax.ShapeDtypeStruct(b.shape, b.dtype) for b in bufs]
        + [jax.ShapeDtypeStruct((N_DEV, k_, n_), bf16) for k_, n_ in dims],
        input_output_aliases={1 + n + k: k for k in range(n)},
        scratch_shapes=[pltpu.VMEM((xt.shape[-2], y.shape[-1]), f32) for y in ys],
        compiler_params=_cp("parallel", "arbitrary"),
    )(xt, *ys, *bufs)
    return outs[:n], outs[n:]


_LOG_GAMMA = [float(np.log1p(-np.float32(2.0) ** np.float32(-5.0 - h))) for h in range(RET_HEADS)]


def _ret_consts(h):
    lg = jnp.where(h == 0, _LOG_GAMMA[0], jnp.where(h == 1, _LOG_GAMMA[1],
                   jnp.where(h == 2, _LOG_GAMMA[2], _LOG_GAMMA[3]))).astype(f32)
    c = RET_CHUNK
    r = lax.broadcasted_iota(jnp.int32, (c, c), 0)
    cc = lax.broadcasted_iota(jnp.int32, (c, c), 1)
    decay = jnp.where(r >= cc, jnp.exp(lg * jnp.maximum((r - cc).astype(f32), 0.0)), 0.0)
    pos = lax.broadcasted_iota(jnp.int32, (c, 1), 0).astype(f32)
    kd = jnp.exp(lg * (c - 1.0 - pos))
    qd = jnp.exp(lg * (pos + 1.0))
    gc = jnp.exp(lg * c)
    return decay, kd, qd, gc


def _rope(x, cos, sin):
    return x * cos + pltpu.roll(x, HEAD_DIM // 2, 1) * sin


def _rope_t(g, cos, sin):
    return g * cos + pltpu.roll(g * sin, HEAD_DIM // 2, 1)


def _rope_tables(s):
    half = HEAD_DIM // 2
    inv = ROPE_BASE ** (-jnp.arange(half, dtype=f32) / half)
    ang = jnp.arange(s, dtype=f32)[:, None] * inv[None, :]
    cos, sin = jnp.cos(ang), jnp.sin(ang)
    return jnp.concatenate([cos, cos], axis=1), jnp.concatenate([-sin, sin], axis=1)


def _head_ln(o):
    mu = jnp.mean(o, axis=-1, keepdims=True)
    oc = o - mu
    rs = lax.rsqrt(jnp.mean(oc * oc, axis=-1, keepdims=True) + EPS)
    return oc * rs, rs


def _ret_fwd(proj, cos, sin, ret_g, nb, s):
    c = RET_CHUNK
    nc = s // c
    t = nb * s
    scale = HEAD_DIM ** -0.5

    def body(q_ref, k_ref, v_ref, gate_ref, cos_ref, sin_ref, g_ref, o_ref, rprev_ref, m_ref):
        decay, kd, qd, gc = _ret_consts(pl.program_id(0))
        gv = g_ref[...]

        def chunk(b, n, rv):
            rows = pl.ds(pl.multiple_of(b * s + n * c, c), c)
            pos = pl.ds(pl.multiple_of(n * c, c), c)
            cs, sn = cos_ref[pos, :], sin_ref[pos, :]
            q = _rope(q_ref[rows, :], cs, sn)
            k = _rope(k_ref[rows, :], cs, sn) * scale
            vb = v_ref[rows, :].astype(bf16)
            sc = _nt(q.astype(bf16), k.astype(bf16)) * decay
            rprev_ref[b, n] = rv
            o = _nn(sc.astype(bf16), vb) + _nn((q * qd).astype(bf16), rv.astype(bf16))
            o_ref[rows, :] = o
            y, _ = _head_ln(o)
            gate = gate_ref[rows, :]
            m_ref[rows, :] = y * gv * (gate * jax.nn.sigmoid(gate))
            return rv * gc + _tn((k * kd).astype(bf16), vb)

        def step(n, carry):
            return tuple(chunk(b, n, carry[b]) for b in range(nb))

        lax.fori_loop(0, nc, step, (jnp.zeros((HEAD_DIM, HEAD_DIM), f32),) * nb)

    def col(off):
        return pl.BlockSpec((t, HEAD_DIM), lambda h: (0, off + h))

    tab = pl.BlockSpec((s, HEAD_DIM), lambda h: (0, 0))
    return pl.pallas_call(
        body, name="ret_fwd", grid=(RET_HEADS,),
        in_specs=[col(0), col(4), col(8), col(12), tab, tab, pl.BlockSpec((1, HEAD_DIM), lambda h: (0, h))],
        out_specs=[col(0), pl.BlockSpec((nb, None, nc, HEAD_DIM, HEAD_DIM), lambda h: (0, h, 0, 0, 0)), col(0)],
        out_shape=[jax.ShapeDtypeStruct((t, RET_WIDTH), f32),
                   jax.ShapeDtypeStruct((nb, RET_HEADS, nc, HEAD_DIM, HEAD_DIM), f32),
                   jax.ShapeDtypeStruct((t, RET_WIDTH), f32)],
        compiler_params=_cp("parallel"),
    )(proj, proj, proj, proj, cos, sin, ret_g)


def _ret_bwd(dmerged, o_raw, rprev, proj, cos, sin, ret_g, nb, s):
    c = RET_CHUNK
    nc = s // c
    t = nb * s
    scale = HEAD_DIM ** -0.5

    def body(dm_ref, o_ref, rprev_ref, q_ref, k_ref, v_ref, gate_ref, cos_ref, sin_ref, g_ref,
             dq_ref, dk_ref, dv_ref, dgate_ref, dg_ref):
        decay, kd, qd, gc = _ret_consts(pl.program_id(0))
        gv = g_ref[...]

        def chunk(b, n, drn, dg):
            rows = pl.ds(pl.multiple_of(b * s + n * c, c), c)
            pos = pl.ds(pl.multiple_of(n * c, c), c)
            cs, sn = cos_ref[pos, :], sin_ref[pos, :]
            q = _rope(q_ref[rows, :], cs, sn)
            k = _rope(k_ref[rows, :], cs, sn) * scale
            qb, kb = q.astype(bf16), k.astype(bf16)
            vb = v_ref[rows, :].astype(bf16)
            sc = _nt(qb, kb) * decay
            y, rs = _head_ln(o_ref[rows, :])
            gate = gate_ref[rows, :]
            sg = jax.nn.sigmoid(gate)
            silu = gate * sg
            dm = dm_ref[rows, :]
            dgate_ref[rows, :] = dm * y * gv * (sg * (1.0 + gate * (1.0 - sg)))
            dyl = dm * gv * silu
            dg = dg + jnp.sum(dm * y * silu, axis=0, keepdims=True)
            do = rs * (dyl - jnp.mean(dyl, axis=-1, keepdims=True) - y * jnp.mean(dyl * y, axis=-1, keepdims=True))
            dob = do.astype(bf16)
            rv = rprev_ref[b, n]
            drb = drn.astype(bf16)
            ds = (_nt(dob, vb) * decay).astype(bf16)
            kdb = (k * kd).astype(bf16)
            qdb = (q * qd).astype(bf16)
            dq_r = _nn(ds, kb) + _nt(dob, rv.astype(bf16)) * qd
            dk_r = _tn(ds, qb) + _nt(vb, drb) * kd
            dv_ref[rows, :] = _tn(sc.astype(bf16), dob) + _nn(kdb, drb)
            dq_ref[rows, :] = _rope_t(dq_r, cs, sn)
            dk_ref[rows, :] = _rope_t(dk_r * scale, cs, sn)
            return drn * gc + _tn(qdb, dob), dg

        def step(i, carry):
            out = [chunk(b, nc - 1 - i, *carry[b]) for b in range(nb)]
            return tuple(out)

        zero = (jnp.zeros((HEAD_DIM, HEAD_DIM), f32), jnp.zeros((1, HEAD_DIM), f32))
        done = lax.fori_loop(0, nc, step, (zero,) * nb)
        dg_ref[...] = sum(dg for _, dg in done)

    def col(off):
        return pl.BlockSpec((t, HEAD_DIM), lambda h: (0, off + h))

    tab = pl.BlockSpec((s, HEAD_DIM), lambda h: (0, 0))
    gsp = pl.BlockSpec((1, HEAD_DIM), lambda h: (0, h))
    out_t = jax.ShapeDtypeStruct((t, RET_WIDTH), f32)
    return pl.pallas_call(
        body, name="ret_bwd", grid=(RET_HEADS,),
        in_specs=[col(0), col(0), pl.BlockSpec((nb, None, nc, HEAD_DIM, HEAD_DIM), lambda h: (0, h, 0, 0, 0)),
                  col(0), col(4), col(8), col(12), tab, tab, gsp],
        out_specs=[col(0), col(0), col(0), col(0), gsp],
        out_shape=[out_t, out_t, out_t, out_t, jax.ShapeDtypeStruct((1, RET_WIDTH), f32)],
        compiler_params=_cp("parallel"),
    )(dmerged, o_raw, rprev, proj, proj, proj, proj, cos, sin, ret_g)


def _neg_expm1(z):
    series = -(z * (1.0 + z * (0.5 + z * (1.0 / 6.0 + z * (1.0 / 24.0)))))
    return jnp.where(z > -0.01, series, 1.0 - jnp.exp(z))


def _lru_gates(xc, pa, pi, lam):
    r = jax.nn.sigmoid(pa)
    i = jax.nn.sigmoid(pi)
    log_a = -LRU_C * r * jax.nn.softplus(-lam)
    a = jnp.exp(log_a)
    bx = jnp.sqrt(_neg_expm1(2.0 * log_a)) * i * xc
    return a, bx


def _scan_rows(a, b, row, up):
    sub = row[:SUBLANES] & (SUBLANES - 1)
    groups = list(range(a.shape[0] // SUBLANES))
    out = [None] * len(groups)
    edge = slice(0, 1) if up else slice(SUBLANES - 1, SUBLANES)
    carry = jnp.zeros((1, a.shape[1]), f32)
    for g in (reversed(groups) if up else groups):
        rows = slice(g * SUBLANES, (g + 1) * SUBLANES)
        xa, xb = a[rows], b[rows]
        d = 1
        while d < SUBLANES:
            keep = (sub < SUBLANES - d) if up else (sub >= d)
            shift = SUBLANES - d if up else d
            xb = xa * jnp.where(keep, pltpu.roll(xb, shift, 0), 0.0) + xb
            xa = xa * jnp.where(keep, pltpu.roll(xa, shift, 0), 1.0)
            d *= 2
        out[g] = xb + xa * carry
        carry = out[g][edge]
    return jnp.concatenate(out, axis=0)


def _scan_fwd(a, b, row):
    return _scan_rows(a, b, row, False)


def _scan_bwd(c, b, row):
    return _scan_rows(c, b, row, True)


def _conv_fwd(x, cw, cb, row):
    return (cb + cw[3:4] * x + cw[2:3] * _shift_dn(x, 1, row) + cw[1:2] * _shift_dn(x, 2, row)
            + cw[0:1] * _shift_dn(x, 3, row))


def _lru_specs(s, order):
    def im(f):
        return (lambda b, g: f(b, g)) if order == "bg" else (lambda g, b: f(b, g))
    seq = lambda off: pl.BlockSpec((s, 128), im(lambda b, g: (b, off + g)))
    vec = pl.BlockSpec((1, 128), im(lambda b, g: (0, g)))
    cw = pl.BlockSpec((4, 128), im(lambda b, g: (0, g)))
    mat = pl.BlockSpec((1, 128, 128), im(lambda b, g: (g, 0, 0)))
    return seq, vec, cw, mat


def _lru_fwd(proj, conv_w, conv_b, w_a, b_a, w_i, b_i, lam, nb, s):
    def body(x_ref, gt_ref, cw_ref, cb_ref, wa_ref, ba_ref, wi_ref, bi_ref, lam_ref, out_ref, h_ref):
        row = lax.broadcasted_iota(jnp.int32, (s, 128), 0)
        xc = _conv_fwd(x_ref[...], cw_ref[...], cb_ref[...], row)
        xcb = xc.astype(bf16)
        pa = _nn(xcb, wa_ref[0].astype(bf16)) + ba_ref[...]
        pi = _nn(xcb, wi_ref[0].astype(bf16)) + bi_ref[...]
        a, bx = _lru_gates(xc, pa, pi, lam_ref[...])
        h = _scan_fwd(a, bx, row)
        h_ref[...] = h
        out_ref[...] = h * jax.nn.gelu(gt_ref[...])

    seq, vec, cw, mat = _lru_specs(s, "bg")
    out = jax.ShapeDtypeStruct((nb * s, LRU_WIDTH), f32)
    return pl.pallas_call(
        body, name="lru_fwd", grid=(nb, LRU_BLOCKS),
        in_specs=[seq(16), seq(20), cw, vec, mat, vec, mat, vec, vec],
        out_specs=[seq(0), seq(0)], out_shape=[out, out],
        compiler_params=_cp("parallel", "parallel"),
    )(proj, proj, conv_w, conv_b, w_a, b_a, w_i, b_i, lam)


def _lru_bwd(dmerged, states, proj, conv_w, conv_b, w_a, b_a, w_i, b_i, lam, nb, s):
    def body(dout_ref, h_ref, x_ref, gt_ref, cw_ref, cb_ref, wa_ref, ba_ref, wi_ref, bi_ref, lam_ref,
             dx_ref, dgt_ref, dcw_ref, dcb_ref, dwa_ref, dba_ref, dwi_ref, dbi_ref, dlam_ref):
        row = lax.broadcasted_iota(jnp.int32, (s, 128), 0)
        x = x_ref[...]
        cwv = cw_ref[...]
        xc = _conv_fwd(x, cwv, cb_ref[...], row)
        xcb = xc.astype(bf16)
        wab, wib = wa_ref[0].astype(bf16), wi_ref[0].astype(bf16)
        pa = _nn(xcb, wab) + ba_ref[...]
        pi = _nn(xcb, wib) + bi_ref[...]
        (a, _), gates_vjp = jax.vjp(_lru_gates, xc, pa, pi, lam_ref[...])
        h = h_ref[...]
        ge, gelu_vjp = jax.vjp(jax.nn.gelu, gt_ref[...])
        dout = dout_ref[...]
        dgt_ref[...] = gelu_vjp(dout * h)[0]
        adj = _scan_bwd(_shift_up(a, 1, row), dout * ge, row)
        dxc, dpa, dpi, dlam = gates_vjp((adj * _shift_dn(h, 1, row), adj))
        dpab, dpib = dpa.astype(bf16), dpi.astype(bf16)
        dxc = dxc + _nt(dpab, wab) + _nt(dpib, wib)
        dx_ref[...] = (cwv[3:4] * dxc + cwv[2:3] * _shift_up(dxc, 1, row) + cwv[1:2] * _shift_up(dxc, 2, row)
                       + cwv[0:1] * _shift_up(dxc, 3, row))

        @pl.when(pl.program_id(1) == 0)
        def _():
            for r in (dcw_ref, dcb_ref, dwa_ref, dba_ref, dwi_ref, dbi_ref, dlam_ref):
                r[...] = jnp.zeros_like(r)
        rsum = lambda v: jnp.sum(v, axis=0, keepdims=True)
        dcw_ref[...] += jnp.concatenate([rsum(dxc * _shift_dn(x, 3, row)), rsum(dxc * _shift_dn(x, 2, row)),
                                         rsum(dxc * _shift_dn(x, 1, row)), rsum(dxc * x)], axis=0)
        dcb_ref[...] += rsum(dxc)
        dwa_ref[0] += _tn(xcb, dpab)
        dwi_ref[0] += _tn(xcb, dpib)
        dba_ref[...] += rsum(dpa)
        dbi_ref[...] += rsum(dpi)
        dlam_ref[...] += dlam

    seq, vec, cw, mat = _lru_specs(s, "gb")
    t = nb * s
    vshape = jax.ShapeDtypeStruct((1, LRU_WIDTH), f32)
    mshape = jax.ShapeDtypeStruct((LRU_BLOCKS, 128, 128), f32)
    return pl.pallas_call(
        body, name="lru_bwd", grid=(LRU_BLOCKS, nb),
        in_specs=[seq(4), seq(0), seq(16), seq(20), cw, vec, mat, vec, mat, vec, vec],
        out_specs=[seq(0), seq(0), cw, vec, mat, vec, mat, vec, vec],
        out_shape=[jax.ShapeDtypeStruct((t, LRU_WIDTH), f32), jax.ShapeDtypeStruct((t, LRU_WIDTH), f32),
                   jax.ShapeDtypeStruct((4, LRU_WIDTH), f32), vshape, mshape, vshape, mshape, vshape, vshape],
        compiler_params=_cp("parallel", "arbitrary"),
    )(dmerged, states, proj, proj, conv_w, conv_b, w_a, b_a, w_i, b_i, lam)


def _s5_disc(lr, li, ldt, bre, bim):
    dt = jnp.exp(ldt)
    mag = jnp.exp(lr * dt)
    lbr = mag * jnp.cos(li * dt)
    lbi = mag * jnp.sin(li * dt)
    den = lr * lr + li * li
    nr = lbr - 1.0
    fr = (nr * lr + lbi * li) / den
    fi = (lbi * lr - nr * li) / den
    bbr = fr[:, None, :] * bre - fi[:, None, :] * bim
    bbi = fr[:, None, :] * bim + fi[:, None, :] * bre
    return lbr, lbi, bbr, bbi


def _s5_prep(lr, li, ldt, bre, bim):
    def body(lr_ref, li_ref, ldt_ref, bre_ref, bim_ref, o1, o2, o3, o4):
        o1[...], o2[...], o3[...], o4[...] = _s5_disc(lr_ref[...], li_ref[...], ldt_ref[...], bre_ref[...], bim_ref[...])

    return pl.pallas_call(
        body, name="s5_prep", in_specs=[VMEM_SPEC] * 5, out_specs=[VMEM_SPEC] * 4,
        out_shape=[jax.ShapeDtypeStruct(lr.shape, f32), jax.ShapeDtypeStruct(lr.shape, f32),
                   jax.ShapeDtypeStruct(bre.shape, f32), jax.ShapeDtypeStruct(bre.shape, f32)],
    )(lr, li, ldt, bre, bim)


def _s5_prep_bwd(lr, li, ldt, bre, bim, cts):
    def body(lr_ref, li_ref, ldt_ref, bre_ref, bim_ref, g1, g2, g3, g4, o1, o2, o3, o4, o5):
        _, vjp = jax.vjp(_s5_disc, lr_ref[...], li_ref[...], ldt_ref[...], bre_ref[...], bim_ref[...])
        o1[...], o2[...], o3[...], o4[...], o5[...] = vjp((g1[...], g2[...], g3[...], g4[...]))

    return pl.pallas_call(
        body, name="s5_prep_bwd", in_specs=[VMEM_SPEC] * 9, out_specs=[VMEM_SPEC] * 5,
        out_shape=[jax.ShapeDtypeStruct(v.shape, f32) for v in (lr, li, ldt, bre, bim)],
    )(lr, li, ldt, bre, bim, *cts)


def _cmul(ar, ai, br, bi):
    return ar * br - ai * bi, ar * bi + ai * br


def _s5_pow_table(lr, li, n, row, up):
    ar = jnp.broadcast_to(lr, (n, lr.shape[1]))
    ai = jnp.broadcast_to(li, (n, li.shape[1]))
    shift = _shift_up if up else _shift_dn
    d = 1
    while d < n:
        ar, ai = _cmul(ar, ai, shift(ar, d, row, 1.0), shift(ai, d, row, 0.0))
        d *= 2
    return ar, ai


def _s5_step_factors(lr, li, row, up):
    sub = row & (SUBLANES - 1)
    out, pr, pi, d = [], lr, li, 1
    while d < SUBLANES:
        keep = (sub < SUBLANES - d) if up else (sub >= d)
        out.append((jnp.where(keep, pr, 0.0), jnp.where(keep, pi, 0.0)))
        pr, pi = _cmul(pr, pi, pr, pi)
        d *= 2
    return out


def _s5_scan(br, bi, steps, tab_r, tab_i, cr, ci, up):
    groups = list(range(br.shape[0] // SUBLANES))
    out_r, out_i = [None] * len(groups), [None] * len(groups)
    edge = slice(0, 1) if up else slice(SUBLANES - 1, SUBLANES)
    for g in (reversed(groups) if up else groups):
        rows = slice(g * SUBLANES, (g + 1) * SUBLANES)
        xr, xi = br[rows], bi[rows]
        for k, (mr, mi) in enumerate(steps):
            shift = SUBLANES - (1 << k) if up else 1 << k
            tr, ti = _cmul(mr, mi, pltpu.roll(xr, shift, 0), pltpu.roll(xi, shift, 0))
            xr, xi = xr + tr, xi + ti
        tr, ti = _cmul(tab_r, tab_i, cr, ci)
        hr, hi = xr + tr, xi + ti
        out_r[g], out_i[g] = hr, hi
        cr, ci = hr[edge], hi[edge]
    return jnp.concatenate(out_r, axis=0), jnp.concatenate(out_i, axis=0)


def _s5_specs(t, nb, nc):
    seq = pl.BlockSpec((t, 128), lambda k: (0, k))
    lvec = pl.BlockSpec((1, S5_BLOCK_STATES), lambda k: (0, k))
    dvec = pl.BlockSpec((1, 128), lambda k: (0, k))
    wmat = pl.BlockSpec((1, 128, S5_BLOCK_STATES), lambda k: (k, 0, 0))
    h0 = pl.BlockSpec((nb, None, nc, 2, S5_BLOCK_STATES), lambda k: (0, k, 0, 0, 0))
    states = pl.BlockSpec((t, S5_BLOCK_STATES), lambda k: (0, k))
    return seq, lvec, dvec, wmat, h0, states


def _s5_fwd(u, lbr, lbi, wbr, wbi, wcr, wci, dskip, nb, s):
    ln = min(S5_CHUNK, s)
    nc = s // ln

    def body(u_ref, lr_ref, li_ref, wbr_ref, wbi_ref, wcr_ref, wci_ref, d_ref, yg_ref, y_ref, h0_ref, hr_ref, hi_ref):
        row = lax.broadcasted_iota(jnp.int32, (ln, S5_BLOCK_STATES), 0)
        lr, li = lr_ref[...], li_ref[...]
        pr, pi = _s5_pow_table(lr, li, SUBLANES, row[:SUBLANES], False)
        steps = _s5_step_factors(lr, li, row[:SUBLANES], False)
        dv = d_ref[...]

        def chunk(b, n, h0r, h0i):
            st = pl.multiple_of(b * s + n * ln, ln)
            uc = u_ref[pl.ds(st, ln), :]
            ub = uc.astype(bf16)
            hr, hi = _s5_scan(_nn(ub, wbr_ref[0]), _nn(ub, wbi_ref[0]), steps, pr, pi, h0r, h0i, False)
            h0_ref[b, n, 0:1, :] = h0r
            h0_ref[b, n, 1:2, :] = h0i
            hrb, hib = hr.astype(bf16), hi.astype(bf16)
            hr_ref[pl.ds(st, ln), :] = hrb
            hi_ref[pl.ds(st, ln), :] = hib
            y = _nt(hrb, wcr_ref[0]) - _nt(hib, wci_ref[0]) + dv * uc
            y_ref[pl.ds(st, ln), :] = y
            yg_ref[pl.ds(st, ln), :] = jax.nn.gelu(y).astype(bf16)
            return hr[ln - 1:ln, :], hi[ln - 1:ln, :]

        def step(n, carry):
            return tuple(chunk(b, n, *carry[b]) for b in range(nb))

        z = jnp.zeros((1, S5_BLOCK_STATES), f32)
        lax.fori_loop(0, nc, step, ((z, z),) * nb)

    t = nb * s
    seq, lvec, dvec, wmat, h0, states = _s5_specs(t, nb, nc)
    return pl.pallas_call(
        body, name="s5_fwd", grid=(S5_BLOCKS,),
        in_specs=[seq, lvec, lvec, wmat, wmat, wmat, wmat, dvec],
        out_specs=[seq, seq, h0, states, states],
        out_shape=[jax.ShapeDtypeStruct((t, D_MODEL), bf16), jax.ShapeDtypeStruct((t, D_MODEL), f32),
                   jax.ShapeDtypeStruct((nb, S5_BLOCKS, nc, 2, S5_BLOCK_STATES), f32),
                   jax.ShapeDtypeStruct((t, S5_BLOCKS * S5_BLOCK_STATES), bf16),
                   jax.ShapeDtypeStruct((t, S5_BLOCKS * S5_BLOCK_STATES), bf16)],
        compiler_params=_cp("parallel"),
    )(u, lbr, lbi, wbr, wbi, wcr, wci, dskip)


def _s5_bwd(dyg, y, u, h0, hrs, his, lbr, lbi, wbr, wbi, wcr, wci, dskip, nb, s):
    ln = min(S5_CHUNK, s)
    nc = s // ln

    def body(dyg_ref, y_ref, u_ref, h0_ref, hr_ref, hi_ref, lr_ref, li_ref, wbr_ref, wbi_ref, wcr_ref, wci_ref, d_ref,
             du_ref, dlr_ref, dli_ref, dwbr_ref, dwbi_ref, dwcr_ref, dwci_ref, dd_ref):
        for r in (dlr_ref, dli_ref, dwbr_ref, dwbi_ref, dwcr_ref, dwci_ref, dd_ref):
            r[...] = jnp.zeros_like(r)
        row = lax.broadcasted_iota(jnp.int32, (ln, S5_BLOCK_STATES), 0)
        lr, li = lr_ref[...], li_ref[...]
        qr, qi = _s5_pow_table(lr, -li, SUBLANES, row[:SUBLANES], True)
        steps_up = _s5_step_factors(lr, -li, row[:SUBLANES], True)
        dv = d_ref[...]
        rsum = lambda v: jnp.sum(v, axis=0, keepdims=True)

        def chunk(b, n, gnr, gni):
            st = pl.multiple_of(b * s + n * ln, ln)
            uc = u_ref[pl.ds(st, ln), :]
            ub = uc.astype(bf16)
            h0v = h0_ref[b, n]
            h0r, h0i = h0v[0:1], h0v[1:2]
            hrb, hib = hr_ref[pl.ds(st, ln), :], hi_ref[pl.ds(st, ln), :]
            hr, hi = hrb.astype(f32), hib.astype(f32)
            dy = jax.vjp(jax.nn.gelu, y_ref[pl.ds(st, ln), :])[1](dyg_ref[pl.ds(st, ln), :])[0]
            dyb = dy.astype(bf16)
            dd_ref[...] += rsum(dy * uc)
            gr, gi = _s5_scan(_nn(dyb, wcr_ref[0]), -_nn(dyb, wci_ref[0]), steps_up, qr, qi, gnr, gni, True)
            hpr = jnp.where(row >= 1, pltpu.roll(hr, 1, 0), h0r)
            hpi = jnp.where(row >= 1, pltpu.roll(hi, 1, 0), h0i)
            dlr_ref[...] += rsum(gr * hpr + gi * hpi)
            dli_ref[...] += rsum(gi * hpr - gr * hpi)
            grb, gib = gr.astype(bf16), gi.astype(bf16)
            dwbr_ref[0] += _tn(ub, grb)
            dwbi_ref[0] += _tn(ub, gib)
            dwcr_ref[0] += _tn(dyb, hrb)
            dwci_ref[0] -= _tn(dyb, hib)
            du_ref[pl.ds(st, ln), :] = _nt(grb, wbr_ref[0]) + _nt(gib, wbi_ref[0]) + dv * dy
            return gr[0:1, :], gi[0:1, :]

        def step(i, carry):
            return tuple(chunk(b, nc - 1 - i, *carry[b]) for b in range(nb))

        z = jnp.zeros((1, S5_BLOCK_STATES), f32)
        lax.fori_loop(0, nc, step, ((z, z),) * nb)

    t = nb * s
    seq, lvec, dvec, wmat, h0s, states = _s5_specs(t, nb, nc)
    lshape = jax.ShapeDtypeStruct((1, S5_BLOCKS * S5_BLOCK_STATES), f32)
    wshape = jax.ShapeDtypeStruct((S5_BLOCKS, 128, S5_BLOCK_STATES), f32)
    return pl.pallas_call(
        body, name="s5_bwd", grid=(S5_BLOCKS,),
        in_specs=[seq, seq, seq, h0s, states, states, lvec, lvec, wmat, wmat, wmat, wmat, dvec],
        out_specs=[seq, lvec, lvec, wmat, wmat, wmat, wmat, dvec],
        out_shape=[jax.ShapeDtypeStruct((t, D_MODEL), f32), lshape, lshape, wshape, wshape, wshape, wshape,
                   jax.ShapeDtypeStruct((1, D_MODEL), f32)],
        compiler_params=_cp("parallel"),
    )(dyg, y, u, h0, hrs, his, lbr, lbi, wbr, wbi, wcr, wci, dskip)


def _blockdiag(w):
    w4 = w.reshape(S5_BLOCKS, 8, S5_GROUP, S5_STATE)
    same_group = jnp.eye(8, dtype=bool)[None, :, None, :, None]
    return jnp.where(same_group, w4[:, :, :, None, :], 0.0).reshape(S5_BLOCKS, 128, S5_BLOCK_STATES)


def _blockdiag_t(dw):
    d5 = dw.reshape(S5_BLOCKS, 8, S5_GROUP, 8, S5_STATE)
    diag = jnp.diagonal(d5, axis1=1, axis2=3)
    return jnp.moveaxis(diag, 3, 1).reshape(S5_GROUPS, S5_GROUP, S5_STATE)


def _glu_fwd(ygb, wa, wb, x, tm=512, tn=1024):
    t, d = x.shape

    def body(y_ref, wa_ref, wb_ref, x_ref, o_ref, p_ref, q_ref):
        p = _nn(y_ref[...], wa_ref[...])
        q = _nn(y_ref[...], wb_ref[...])
        p_ref[...] = p
        q_ref[...] = q
        o_ref[...] = x_ref[...] + p * jax.nn.sigmoid(q)

    tile = pl.BlockSpec((tm, tn), lambda i, j: (i, j))
    wsp = pl.BlockSpec((d, tn), lambda i, j: (0, j))
    out = jax.ShapeDtypeStruct((t, d), f32)
    return pl.pallas_call(
        body, name="glu_fwd", grid=(t // tm, d // tn),
        in_specs=[pl.BlockSpec((tm, d), lambda i, j: (i, 0)), wsp, wsp, tile],
        out_specs=[tile, tile, tile], out_shape=[out, out, out],
        compiler_params=_cp("parallel", "parallel"),
    )(ygb, wa, wb, x)


def _place():
    x, y, c = lax.axis_index("x"), lax.axis_index("y"), lax.axis_index("c")
    return x, y, c, [(1 - x, y), (x, 1 - y), (1 - x, 1 - y)]


def _all_gather(name, arrays):
    n = len(arrays)

    def body(*refs):
        ins, outs = refs[:n], refs[n:2 * n]
        send_sems, recv_sems, local_sems = refs[2 * n:]
        x, y, c, chips = _place()
        me, sib = (x, y, c), (x, y, 1 - c)

        def copy(i, k, block, to, src=None):
            dst = outs[i].at[4 * block[0] + 2 * block[1] + block[2]]
            return pltpu.make_async_remote_copy(
                src_ref=dst if src is None else src, dst_ref=dst,
                send_sem=send_sems.at[i * 7 + k], recv_sem=recv_sems.at[i * 7 + k],
                device_id=to, device_id_type=MESH)

        mine = [pltpu.make_async_copy(ins[i], outs[i].at[4 * x + 2 * y + c], local_sems.at[i]) for i in range(n)]
        for m in mine:
            m.start()
        first = []
        for i in range(n):
            first.append(copy(i, 0, me, sib, src=ins[i]))
            first += [copy(i, 1 + j, me, (*chip, c), src=ins[i]) for j, chip in enumerate(chips)]
        for cp in first:
            cp.start()
        passed = []
        for j, chip in enumerate(chips):
            for i in range(n):
                copy(i, 1 + j, (*chip, c), me).wait_recv()
                fwd = copy(i, 4 + j, (*chip, c), sib)
                fwd.start()
                passed.append(fwd)
        for i in range(n):
            copy(i, 0, sib, me).wait_recv()
        for j, chip in enumerate(chips):
            for i in range(n):
                copy(i, 4 + j, (*chip, 1 - c), me).wait_recv()
        for cp in first + passed:
            cp.wait_send()
        for m in mine:
            m.wait()

    return pl.pallas_call(
        body, name=name,
        in_specs=[ANY_SPEC] * n, out_specs=[ANY_SPEC] * n,
        out_shape=[jax.ShapeDtypeStruct((N_DEV,) + a.shape, a.dtype) for a in arrays],
        scratch_shapes=[pltpu.SemaphoreType.DMA((7 * n,)), pltpu.SemaphoreType.DMA((7 * n,)),
                        pltpu.SemaphoreType.DMA((n,))],
    )(*arrays)


def _tie(name, x, deps):
    def body(*refs):
        pass

    return pl.pallas_call(
        body, name=name, in_specs=[ANY_SPEC] * (1 + len(deps)), out_specs=ANY_SPEC,
        out_shape=jax.ShapeDtypeStruct(x.shape, x.dtype), input_output_aliases={0: 0},
    )(x, *deps)


def _xchg_copies(kind, srcs, lands, suffixes, send_sems, recv_sems):
    x, y, c, _ = _place()
    copies = []
    for i, (src, land, sfx) in enumerate(zip(srcs, lands, suffixes)):
        for k in range(N_DEV - 1):
            r = k + 1
            peer = (1 - x if r & 4 else x, 1 - y if r & 2 else y, 1 - c if r & 1 else c)
            if kind == "gather":
                s_ref, d_ref = src, land.at[(4 * x + 2 * y + c,) + sfx]
            else:
                s_ref, d_ref = src.at[4 * peer[0] + 2 * peer[1] + peer[2]], land.at[(k,) + sfx]
            copies.append(pltpu.make_async_remote_copy(
                src_ref=s_ref, dst_ref=d_ref, send_sem=send_sems.at[i * 7 + k], recv_sem=recv_sems.at[i * 7 + k],
                device_id=peer, device_id_type=MESH))
    return copies


def _xchg_start(name, kind, srcs, lands, suffixes=None):
    n = len(srcs)
    suffixes = suffixes or [()] * n

    def body(*refs):
        src, land = refs[:n], refs[n:2 * n]
        send_sems, recv_sems, token = refs[2 * n], refs[2 * n + 1], refs[-1]
        for cp in _xchg_copies(kind, src, land, suffixes, send_sems, recv_sems):
            cp.start()
        token[...] = jnp.zeros_like(token)

    arrays = list(srcs) + list(lands)
    outs = pl.pallas_call(
        body, name=name,
        out_shape=(pltpu.SemaphoreType.DMA((7 * n,)), pltpu.SemaphoreType.DMA((7 * n,)),
                   *[pltpu.HBM(a.shape, a.dtype) for a in arrays], jax.ShapeDtypeStruct((8, 128), f32)),
        in_specs=[HBM_SPEC] * (2 * n),
        out_specs=(SEM_SPEC, SEM_SPEC, *[HBM_SPEC] * (2 * n), VMEM_SPEC),
        input_output_aliases={i: 2 + i for i in range(2 * n)},
        compiler_params=pltpu.CompilerParams(has_side_effects=SIDE_EFFECT),
    )(*[pltpu.with_memory_space_constraint(a, pltpu.HBM) for a in arrays])
    return dict(kind=kind, n=n, suffixes=suffixes, send=outs[0], recv=outs[1], srcs=list(outs[2:2 + n]),
                lands=list(outs[2 + n:2 + 2 * n]), token=outs[-1])


def _xchg_wait(name, h, after, lands=None):
    n = h["n"]
    lands = h["lands"] if lands is None else lands

    def body(*refs):
        src, land = refs[:n], refs[n:2 * n]
        for cp in _xchg_copies(h["kind"], src, land, h["suffixes"], refs[2 * n], refs[2 * n + 1]):
            cp.wait_send()
            cp.wait_recv()

    arrays = h["srcs"] + list(lands)
    outs = pl.pallas_call(
        body, name=name,
        out_shape=tuple(pltpu.HBM(a.shape, a.dtype) for a in arrays),
        in_specs=[HBM_SPEC] * (2 * n) + [SEM_SPEC, SEM_SPEC] + [ANY_SPEC] * len(after),
        out_specs=tuple([HBM_SPEC] * (2 * n)),
        input_output_aliases={i: i for i in range(2 * n)},
        compiler_params=pltpu.CompilerParams(has_side_effects=SIDE_EFFECT),
    )(*arrays, h["send"], h["recv"], *after)
    return list(outs[n:])


def _rows(a):
    return a.reshape(-1, a.shape[-1])


def _row_tile(r):
    for tm in (512, 256, 128, 64, 32, 16, 8):
        if r % tm == 0:
            return tm
    return r


def _sum8(name, gathered):
    _, r, n = gathered.shape
    tm = _row_tile(r)

    def body(g_ref, o_ref):
        acc = g_ref[0]
        for k in range(1, N_DEV):
            acc = acc + g_ref[k]
        o_ref[...] = acc

    return pl.pallas_call(
        body, name=name, grid=(r // tm,),
        in_specs=[pl.BlockSpec((N_DEV, tm, n), lambda i: (0, i, 0))],
        out_specs=pl.BlockSpec((tm, n), lambda i: (i, 0)),
        out_shape=jax.ShapeDtypeStruct((r, n), f32),
        compiler_params=_cp("parallel"),
    )(gathered)


def _adamw(name, w, m, v, own, landed=None, slot=None):
    shape = w.shape
    w2, m2, v2 = _rows(w), _rows(m), _rows(v)
    r, n = w2.shape
    tm = _row_tile(r)
    c1 = 1.0 - ADAM_B1 ** ADAM_STEP
    c2 = 1.0 - ADAM_B2 ** ADAM_STEP
    extra = [] if landed is None else [landed.reshape(landed.shape[0], r, n)]
    row = pl.BlockSpec((tm, n), lambda i, *_: (i, 0))
    if slot is None:
        o2, own_spec, scalars = _rows(own), row, []
    else:
        dev, kind = slot
        scalars = [dev.reshape(1).astype(jnp.int32)]
        if kind == "lead":
            o2, own_spec = own.reshape(N_DEV, r, n), pl.BlockSpec((None, tm, n), lambda i, d: (d[0], i, 0))
        elif kind == "rows":
            o2, own_spec = own, pl.BlockSpec((tm, n), lambda i, d: (d[0] * (r // tm) + i, 0))
        else:
            o2, own_spec = own, pl.BlockSpec((tm, n), lambda i, d: (i, d[0]))

    def body(*refs):
        w_ref, m_ref, v_ref, o_ref = refs[len(scalars):len(scalars) + 4]
        refs = refs[len(scalars) + 4:]
        g = o_ref[...]
        if extra:
            for k in range(extra[0].shape[0]):
                g = g + refs[0][k].astype(f32)
        g_ref, d_ref, mn_ref, vn_ref = refs[len(extra):]
        mn = ADAM_B1 * m_ref[...] + (1.0 - ADAM_B1) * g
        vn = ADAM_B2 * v_ref[...] + (1.0 - ADAM_B2) * (g * g)
        g_ref[...] = g
        d_ref[...] = -ADAM_LR * ((mn / c1) / (jnp.sqrt(vn / c2) + ADAM_EPS) + ADAM_WD * w_ref[...])
        mn_ref[...] = mn
        vn_ref[...] = vn

    outs = pl.pallas_call(
        body, name=name,
        grid_spec=pltpu.PrefetchScalarGridSpec(
            num_scalar_prefetch=len(scalars), grid=(r // tm,),
            in_specs=[row] * 3 + [own_spec] + [pl.BlockSpec((e.shape[0], tm, n), lambda i, *_: (0, i, 0)) for e in extra],
            out_specs=[row] * 4),
        out_shape=[jax.ShapeDtypeStruct((r, n), f32)] * 4,
        compiler_params=_cp("parallel"),
    )(*scalars, w2, m2, v2, o2, *extra)
    return [o.reshape(shape) for o in outs]


def _adamw_minor_d(name, w, m, v, own_all, landed, dev, tm=512):
    nl, nh, d, f = w.shape
    wt, mt, vt = (jnp.swapaxes(t, 2, 3) for t in (w, m, v))
    r = nl * nh * d
    per = d // tm
    c1 = 1.0 - ADAM_B1 ** ADAM_STEP
    c2 = 1.0 - ADAM_B2 ** ADAM_STEP

    def body(dev_ref, w_ref, m_ref, v_ref, o_ref, l_ref, g_ref, d_ref, mn_ref, vn_ref):
        g = o_ref[...]
        for k in range(N_DEV - 1):
            g = g + l_ref[k].astype(f32)
        g = g.T
        mn = ADAM_B1 * m_ref[...] + (1.0 - ADAM_B1) * g
        vn = ADAM_B2 * v_ref[...] + (1.0 - ADAM_B2) * (g * g)
        g_ref[...] = g
        d_ref[...] = -ADAM_LR * ((mn / c1) / (jnp.sqrt(vn / c2) + ADAM_EPS) + ADAM_WD * w_ref[...])
        mn_ref[...] = mn
        vn_ref[...] = vn

    par = pl.BlockSpec((None, None, f, tm), lambda i, _: (i // (nh * per), (i // per) % nh, 0, i % per))
    outs = pl.pallas_call(
        body, name=name,
        grid_spec=pltpu.PrefetchScalarGridSpec(
            num_scalar_prefetch=1, grid=(r // tm,),
            in_specs=[par, par, par, pl.BlockSpec((None, tm, f), lambda i, dv: (dv[0], i, 0)),
                      pl.BlockSpec((N_DEV - 1, tm, f), lambda i, _: (0, i, 0))],
            out_specs=[par] * 4),
        out_shape=[jax.ShapeDtypeStruct(wt.shape, f32)] * 4,
        compiler_params=_cp("parallel"),
    )(dev.reshape(1).astype(jnp.int32), wt, mt, vt, own_all.reshape(N_DEV, r, f), landed.reshape(N_DEV - 1, r, f))
    return [jnp.swapaxes(o, 2, 3) for o in outs]


def _pack(arrays):
    flat = jnp.concatenate([a.reshape(-1).astype(f32) for a in arrays])
    pad = (-flat.shape[0]) % (128 * (512 if flat.shape[0] > 128 * 512 else 8))
    return jnp.pad(flat, (0, pad)).reshape(-1, 128)


def _unpack(packed, shapes):
    flat = packed.reshape(-1)
    out, off = [], 0
    for s in shapes:
        n = math.prod(s)
        out.append(flat[off:off + n].reshape(s))
        off += n
    return out


def _local_step(x, target, w, weights_of, send, last_small, on_loss, nb, s):
    cos, sin = _rope_tables(s)
    g = {}
    ffn_saved = {}
    ffn_bufs = [lax.empty((N_DEV, 2, 2) + shp, f32)
                for shp in ((D_MODEL, FF_SHARD), (D_MODEL, FF_SHARD), (FF_SHARD, D_MODEL))]

    def ffn(xin, l, h, wts):
        y, a, b = _ffn_fwd(f"ffn_fwd_{l}{h}", xin, w["ffn_g"][l][h], *wts)
        ffn_saved[(l, h)] = (xin, a, b, wts)
        return y

    def ffn_back(dy, l, h):
        xin, a, b, wts = ffn_saved[(l, h)]
        dx, dg, hb, dyh, u, da, db = _ffn_dx(f"ffn_dx_{l}{h}", dy, xin, w["ffn_g"][l][h], *wts, a, b)
        g[f"ffn_g_{l}{h}"] = dg
        if (l, h) != (0, 0):
            ffn_bufs[:2], (h1, h3) = _ffn_dw(f"ffn_dw_{l}{h}_w13", hb, [da, db], ffn_bufs[:2], l, h)
            ffn_bufs[2:], (h2,) = _ffn_dw(f"ffn_dw_{l}{h}_w2", u, [dyh], ffn_bufs[2:], l, h)
            return send(f"ffn_{l}{h}", {"ffn_w1": h1, "ffn_w3": h3, "ffn_w2": h2}, dx)
        hb = last_small(g, hb)
        ffn_bufs[:1], (half,) = _ffn_dw("ffn_dw_00_w1", hb, [da], ffn_bufs[:1], l, h)
        hb = send("ffn_00_w1", {"ffn_w1": half}, hb)
        ffn_bufs[1:2], (half,) = _ffn_dw("ffn_dw_00_w3", hb, [db], ffn_bufs[1:2], l, h)
        u = send("ffn_00_w3", {"ffn_w3": half}, u)
        ffn_bufs[2:], (half,) = _ffn_dw("ffn_dw_00_w2", u, [dyh], ffn_bufs[2:], l, h)
        return send("ffn_00_w2", {"ffn_w2": half}, dx)

    def slots(t):
        return t.reshape(N_DEV, D_MODEL // N_DEV, D_MODEL)

    x1 = ffn(x, 0, 0, weights_of(0, [])["ffn"])
    wg = weights_of(1, [x1])
    w_in, w_out = wg["w_in"], wg["w_out"]
    h0b = _norm_fwd("mix_norm_0", x1, w["mix_g"][0], bf16)
    proj = _mm("in_proj", h0b, w_in, "nn", tn=1536)[0]
    o_raw, rprev, mret = _ret_fwd(proj, cos, sin, w["ret_g"], nb, s)
    lru, lru_h = _lru_fwd(proj, w["conv_w"], w["conv_b"], w["lru_w_a"], w["lru_b_a"], w["lru_w_i"], w["lru_b_i"], w["lru_lam"], nb, s)
    merged = _ew("merge", lambda a, b: (jnp.concatenate([a, b], axis=1),), [mret, lru], [(D_MODEL, bf16)])[0]
    x2 = _mm("out_proj", merged, w_out, "nn", extras=[x1], epilogue=lambda acc, r: (acc + r,))[0]
    x3 = ffn(x2, 0, 1, weights_of(2, [x2])["ffn"])
    x4 = ffn(x3, 1, 0, weights_of(3, [x3])["ffn"])
    u = _norm_fwd("mix_norm_1", x4, w["mix_g"][1], f32)
    lbr, lbi, bbr, bbi = _s5_prep(w["s5_lr"], w["s5_li"], w["s5_ldt"], w["s5_bre"], w["s5_bim"])
    lbr_f, lbi_f = lbr.reshape(1, -1), lbi.reshape(1, -1)
    wbr, wbi = _blockdiag(bbr).astype(bf16), _blockdiag(bbi).astype(bf16)
    wcr, wci = _blockdiag(w["s5_cre"]).astype(bf16), _blockdiag(w["s5_cim"]).astype(bf16)
    ygb, ypre, h0s, hrs, his = _s5_fwd(u, lbr_f, lbi_f, wbr, wbi, wcr, wci, w["s5_d"], nb, s)
    wg = weights_of(4, [ygb])
    glu_a, glu_b = wg["glu_a"], wg["glu_b"]
    x5, gp, gq = _glu_fwd(ygb, glu_a, glu_b, x4)
    x6 = ffn(x5, 1, 1, weights_of(5, [x5])["ffn"])
    loss, dx6, g["final_g"] = _final_loss(x6, w["final_g"], target)
    dx6 = on_loss(loss, dx6)

    dx5 = ffn_back(dx6, 1, 1)

    def glu_bwd(d, p, q):
        sg = jax.nn.sigmoid(q)
        return d * sg, d * p * sg * (1.0 - sg)

    dp, dq = _ew("glu_bwd", glu_bwd, [dx5, gp, gq], [(D_MODEL, bf16), (D_MODEL, bf16)])
    dyg = _mm("glu_dy_a", dp, glu_a, "nt")[0]
    dyg = _mm("glu_dy_b", dq, glu_b, "nt", extras=[dyg], epilogue=lambda acc, r: (acc + r,))[0]
    g["glu_a"], ga_half = _mm_tn("glu_dw_a", ygb, dp)
    g["glu_b"], gb_half = _mm_tn("glu_dw_b", ygb, dq)
    dyg = send("glu", {"glu_a": slots(ga_half), "glu_b": slots(gb_half)}, dyg)
    du, dlr, dli, dwbr, dwbi, dwcr, dwci, g["s5_d"] = _s5_bwd(dyg, ypre, u, h0s, hrs, his, lbr_f, lbi_f, wbr, wbi, wcr, wci, w["s5_d"], nb, s)
    g["s5_cre"], g["s5_cim"] = _blockdiag_t(dwcr), _blockdiag_t(dwci)
    g["s5_lr"], g["s5_li"], g["s5_ldt"], g["s5_bre"], g["s5_bim"] = _s5_prep_bwd(
        w["s5_lr"], w["s5_li"], w["s5_ldt"], w["s5_bre"], w["s5_bim"],
        (dlr.reshape(S5_GROUPS, S5_STATE), dli.reshape(S5_GROUPS, S5_STATE), _blockdiag_t(dwbr), _blockdiag_t(dwbi)))
    dx4, g["mix_g_1"] = _norm_bwd("mix_norm_1_bwd", du, x4, w["mix_g"][1], dx5)
    dx3 = ffn_back(dx4, 1, 0)
    dx2 = ffn_back(dx3, 0, 1)
    dmerged = _mm("out_proj_dx", dx2, w_out, "nt")[0]
    g["w_out"], wo_half = _mm_tn("out_proj_dw", merged, dx2)
    dmerged = send("w_out", {"w_out": slots(wo_half)}, dmerged)
    dq_, dk_, dv_, dgate, g["ret_g"] = _ret_bwd(dmerged, o_raw, rprev, proj, cos, sin, w["ret_g"], nb, s)
    (dxl, dgl, g["conv_w"], g["conv_b"], g["lru_w_a"], g["lru_b_a"], g["lru_w_i"], g["lru_b_i"], g["lru_lam"]) = _lru_bwd(
        dmerged, lru_h, proj, w["conv_w"], w["conv_b"], w["lru_w_a"], w["lru_b_a"], w["lru_w_i"], w["lru_b_i"], w["lru_lam"],
        nb, s)
    dproj = _ew("dproj", lambda *p: (jnp.concatenate(p, axis=1),), [dq_, dk_, dv_, dgate, dxl, dgl], [(3072, bf16)])[0]
    dh0 = _mm("in_proj_dx", dproj, w_in, "nt")[0]
    g["w_in"], wi_half = _mm_tn("in_proj_dw", h0b, dproj)
    dh0 = send("w_in", {"w_in": jnp.transpose(wi_half.reshape(D_MODEL, N_DEV, IN_SHARD), (1, 0, 2))}, dh0)
    dx1, g["mix_g_0"] = _norm_bwd("mix_norm_0_bwd", dh0, x1, w["mix_g"][0], dx2)
    dx0 = ffn_back(dx1, 0, 0)
    g["ffn_w1"], g["ffn_w3"], g["ffn_w2"] = ffn_bufs
    return loss, dx0, g


_WEIGHTS = ["ffn_norm_g", "ffn_w1", "ffn_w3", "ffn_w2", "mix_norm_g", "w_in_even", "w_out_even", "ret_norm_g", "conv_w",
            "conv_b", "lru_w_a", "lru_b_a", "lru_w_i", "lru_b_i", "lru_lambda", "s5_lambda_re", "s5_lambda_im", "s5_log_dt",
            "s5_b_re", "s5_b_im", "s5_c_re", "s5_c_im", "s5_d", "glu_w_a", "glu_w_b", "final_norm_g"]
_BIG = ["ffn_w1", "ffn_w3", "ffn_w2", "w_in_even", "w_out_even", "glu_w_a", "glu_w_b"]
_SMALL_SHARDED = ["ffn_norm_g", "conv_w", "s5_d"]
_SMALL = [n for n in _WEIGHTS if n not in _BIG]
_MIDSIZE = ["lru_w_a", "lru_w_i", "s5_b_re", "s5_b_im", "s5_c_re", "s5_c_im"]


def kernel(x, ffn_norm_g, ffn_w1, ffn_w3, ffn_w2, mix_norm_g, w_in_even, w_out_even, ret_norm_g, conv_w, conv_b, lru_w_a, lru_b_a, lru_w_i, lru_b_i, lru_lambda, s5_lambda_re, s5_lambda_im, s5_log_dt, s5_b_re, s5_b_im, s5_c_re, s5_c_im, s5_d, glu_w_a, glu_w_b, final_norm_g, loss_target, m_ffn_norm_g, m_ffn_w1, m_ffn_w3, m_ffn_w2, m_mix_norm_g, m_w_in_even, m_w_out_even, m_ret_norm_g, m_conv_w, m_conv_b, m_lru_w_a, m_lru_b_a, m_lru_w_i, m_lru_b_i, m_lru_lambda, m_s5_lambda_re, m_s5_lambda_im, m_s5_log_dt, m_s5_b_re, m_s5_b_im, m_s5_c_re, m_s5_c_im, m_s5_d, m_glu_w_a, m_glu_w_b, m_final_norm_g, v_ffn_norm_g, v_ffn_w1, v_ffn_w3, v_ffn_w2, v_mix_norm_g, v_w_in_even, v_w_out_even, v_ret_norm_g, v_conv_w, v_conv_b, v_lru_w_a, v_lru_b_a, v_lru_w_i, v_lru_b_i, v_lru_lambda, v_s5_lambda_re, v_s5_lambda_im, v_s5_log_dt, v_s5_b_re, v_s5_b_im, v_s5_c_re, v_s5_c_im, v_s5_d, v_glu_w_a, v_glu_w_b, v_final_norm_g):
    a = dict(locals())
    nb, s, d = x.shape
    dev = 4 * lax.axis_index("x") + 2 * lax.axis_index("y") + lax.axis_index("c")

    def ffn_shards(l, h, pad):
        out = [jnp.swapaxes(ffn_w1[l, h], 0, 1).astype(bf16), jnp.swapaxes(ffn_w3[l, h], 0, 1).astype(bf16),
               ffn_w2[l, h].astype(bf16)]
        return [jnp.pad(t, ((0, FF_PAD - FF_SHARD), (0, 0))) for t in out] if pad else out

    first = _all_gather("ag_first", ffn_shards(0, 0, True) + [_pack([ffn_norm_g, conv_w, s5_d])])
    sm = first[3].reshape(N_DEV, -1)
    ffn_g_full = jnp.transpose(sm[:, :512].reshape(N_DEV, 2, 2, 128), (1, 2, 0, 3)).reshape(2, 2, D_MODEL)
    conv_w_full = jnp.transpose(sm[:, 512:768].reshape(N_DEV, 4, 64), (1, 0, 2)).reshape(4, LRU_WIDTH)
    s5_d_full = sm[:, 768:896].reshape(1, D_MODEL)

    ag_src = [None, [w_in_even[0].astype(bf16), w_out_even[0].astype(bf16)], ffn_shards(0, 1, False),
              ffn_shards(1, 0, False), [glu_w_a[0].astype(bf16), glu_w_b[0].astype(bf16)], ffn_shards(1, 1, False)]
    ag, token = [None], first[0]
    for k, grp in enumerate(ag_src):
        if grp is None:
            continue
        grp[0] = _tie(f"tie_ag_{k}", grp[0], [token])
        if grp[0].shape[0] == FF_SHARD:
            lands = [lax.dynamic_update_slice(jnp.zeros((N_DEV, FF_PAD, D_MODEL), bf16), t[None], (dev, 0, 0)) for t in grp]
            sfx = [(pl.ds(0, FF_SHARD),)] * len(grp)
        else:
            lands = [lax.dynamic_update_index_in_dim(lax.empty((N_DEV,) + t.shape, bf16), t, dev, 0) for t in grp]
            sfx = None
        ag.append(_xchg_start(f"ag_start_{k}", "gather", grp, lands, sfx))
        token = ag[-1]["token"]

    def weights_of(k, after):
        if k == 0:
            return {"ffn": [first[0], _tie("tie_ag_started", first[1], [h["token"] for h in ag[1:]]), first[2]]}
        got = _xchg_wait(f"ag_wait_{k}", ag[k], after)
        if k == 1:
            return {"w_in": jnp.transpose(got[0], (1, 0, 2)).reshape(D_MODEL, N_DEV * IN_SHARD),
                    "w_out": got[1].reshape(D_MODEL, D_MODEL)}
        if k == 4:
            return {"glu_a": got[0].reshape(D_MODEL, D_MODEL), "glu_b": got[1].reshape(D_MODEL, D_MODEL)}
        return {"ffn": got}

    ffn_lands = [lax.empty((N_DEV - 1, 2, 2) + shp, bf16)
                 for shp in ((D_MODEL, FF_SHARD), (D_MODEL, FF_SHARD), (FF_SHARD, D_MODEL))]
    rs = []

    ffn_names = ("ffn_w1", "ffn_w3", "ffn_w2")

    def send(group, arrays, carry):
        srcs = list(arrays.values())
        if group.startswith("ffn_"):
            which = [ffn_names.index(n) for n in arrays]
            sfx = [(int(group[4]), int(group[5]))] * len(which)
            h = _xchg_start("rs_start_" + group, "scatter", srcs, [ffn_lands[k] for k in which], sfx)
            for k, land in zip(which, h["lands"]):
                ffn_lands[k] = land
        else:
            h = _xchg_start("rs_start_" + group, "scatter", srcs,
                            [lax.empty((N_DEV - 1,) + t.shape[1:], bf16) for t in srcs])
        rs.append((group, list(arrays), h))
        return _tie("tie_" + group, carry, [h["token"]])

    w = {
        "ffn_g": [[ffn_g_full[l, h].reshape(1, D_MODEL) for h in range(2)] for l in range(2)],
        "mix_g": [mix_norm_g[0:1], mix_norm_g[1:2]],
        "ret_g": ret_norm_g, "conv_w": conv_w_full, "conv_b": conv_b,
        "lru_w_a": lru_w_a[0], "lru_b_a": lru_b_a, "lru_w_i": lru_w_i[0], "lru_b_i": lru_b_i, "lru_lam": lru_lambda,
        "s5_lr": s5_lambda_re[0], "s5_li": s5_lambda_im[0], "s5_ldt": s5_log_dt.reshape(S5_GROUPS, 1),
        "s5_bre": jnp.swapaxes(s5_b_re[0], 1, 2), "s5_bim": jnp.swapaxes(s5_b_im[0], 1, 2),
        "s5_cre": s5_c_re[0], "s5_cim": s5_c_im[0], "s5_d": s5_d_full,
        "final_g": final_norm_g.reshape(1, D_MODEL),
    }

    small_grads = {}

    def last_small(g, carry):
        part = _small_partials(g)
        mine = _pack([part[n] for n in _SMALL])
        land = lax.dynamic_update_index_in_dim(lax.empty((N_DEV,) + mine.shape, f32), mine, dev, 0)
        h = _xchg_start("ag_start_small_grads", "gather", [mine], [land])
        small_grads.update(h=h, shapes=[part[n].shape for n in _SMALL])
        return _tie("tie_small_grads", carry, [h["token"]])

    total_loss = []

    def on_loss(part, carry):
        total_loss.append(lax.psum(part[0, 0], ("x", "y", "c")))
        return _tie("tie_loss", carry, [jnp.broadcast_to(total_loss[0], (8, 128))])

    _, dx, g = _local_step(x.reshape(nb * s, d), loss_target.reshape(nb * s, d), w, weights_of, send, last_small,
                           on_loss, nb, s)
    loss = total_loss[0]
    (gath,) = _xchg_wait("ag_wait_small_grads", small_grads["h"], [dx])
    full = dict(zip(_SMALL, _unpack(_sum8("sum_small_grads", gath), small_grads["shapes"])))
    for n in _SMALL_SHARDED:
        width = a[n].shape[-1]
        full[n] = lax.dynamic_slice_in_dim(full[n], dev * width, width, axis=full[n].ndim - 1)
    res = {n: _adamw("adamw_" + n, a[n], a["m_" + n], a["v_" + n], full[n]) for n in _MIDSIZE}
    tiny = [n for n in _SMALL if n not in _MIDSIZE]
    shapes = [a[n].shape for n in tiny]
    packed = _adamw("adamw_small", _pack([a[n] for n in tiny]), _pack([a["m_" + n] for n in tiny]),
                    _pack([a["v_" + n] for n in tiny]), _pack([full[n] for n in tiny]))
    res.update({n: vals for n, vals in zip(tiny, zip(*[_unpack(p, shapes) for p in packed]))})
    return _finish(a, g, dx, loss, res, packed, rs, ffn_lands, dev, nb, s, d)


def _small_partials(g):
    return {
        "ffn_norm_g": jnp.stack([jnp.stack([g[f"ffn_g_{l}{h}"][0] for h in range(2)]) for l in range(2)]),
        "mix_norm_g": jnp.concatenate([g["mix_g_0"], g["mix_g_1"]], axis=0),
        "ret_norm_g": g["ret_g"], "conv_w": g["conv_w"][None], "conv_b": g["conv_b"],
        "lru_w_a": g["lru_w_a"][None], "lru_b_a": g["lru_b_a"], "lru_w_i": g["lru_w_i"][None], "lru_b_i": g["lru_b_i"],
        "lru_lambda": g["lru_lam"], "s5_lambda_re": g["s5_lr"][None], "s5_lambda_im": g["s5_li"][None],
        "s5_log_dt": g["s5_ldt"].reshape(1, S5_GROUPS),
        "s5_b_re": jnp.swapaxes(g["s5_bre"], 1, 2)[None], "s5_b_im": jnp.swapaxes(g["s5_bim"], 1, 2)[None],
        "s5_c_re": g["s5_cre"][None], "s5_c_im": g["s5_cim"][None], "s5_d": g["s5_d"], "final_norm_g": g["final_g"][0],
    }


def _finish(a, g, dx, loss, res, packed, rs, ffn_lands, dev, nb, s, d):
    landed = {}
    for group, names, h in rs:
        if not group.startswith("ffn_"):
            landed.update(zip(names, _xchg_wait("rs_wait_" + group, h, [dx])))
    kinds = {"ffn_w1": "lead", "ffn_w3": "lead", "ffn_w2": "lead", "w_in": "cols", "w_out": "rows", "glu_a": "rows",
             "glu_b": "rows"}

    def update(n, short):
        if short in ("ffn_w1", "ffn_w3"):
            res[n] = _adamw_minor_d("adamw_" + n, a[n], a["m_" + n], a["v_" + n], g[short], landed[short], dev)
        else:
            res[n] = _adamw("adamw_" + n, a[n], a["m_" + n], a["v_" + n], g[short],
                            landed[short].reshape((N_DEV - 1,) + a[n].shape), slot=(dev, kinds[short]))

    for n, short in zip(_BIG[3:], ("w_in", "w_out", "glu_a", "glu_b")):
        update(n, short)
    after = [dx, packed[0]] + [res[n][0] for n in _BIG[3:] + _MIDSIZE]
    ffn_names = ("ffn_w1", "ffn_w3", "ffn_w2")
    for group, names, h in rs:
        if group.startswith("ffn_") and len(names) == 3:
            ffn_lands[:] = _xchg_wait("rs_wait_" + group, h, after, ffn_lands)
    for k, n in enumerate(ffn_names):
        for group, names, h in rs:
            if group.startswith("ffn_") and names == [n]:
                (ffn_lands[k],) = _xchg_wait("rs_wait_" + group, h, after, [ffn_lands[k]])
        landed[n] = ffn_lands[k]
        update(n, n)
        after = after + [res[n][0]]

    out = [loss, dx.reshape(nb, s, d)]
    for k in range(4):
        out += [res[n][k] for n in _WEIGHTS]
    return tuple(out)
```

```python
import math

import numpy as np
import jax
import jax.numpy as jnp
from jax import lax
from jax.experimental import pallas as pl
from jax.experimental.pallas import tpu as pltpu

f32 = jnp.float32
bf16 = jnp.bfloat16

D_MODEL = 1024
N_DEV = 8
EPS = 1e-6
RET_HEADS = 4
HEAD_DIM = 128
RET_WIDTH = 512
RET_CHUNK = 128
ROPE_BASE = 10000.0
LRU_WIDTH = 512
LRU_BLOCKS = 4
LRU_C = 8.0
S5_GROUP = 16
S5_GROUPS = 64
S5_STATE = 64
S5_CHUNK = 1024
S5_BLOCKS = 8
S5_BLOCK_STATES = 512
SUBLANES = 8
D_FF = 2816
FF_SHARD = D_FF // N_DEV
FF_PAD = 384
IN_SHARD = 3072 // N_DEV
ADAM_LR = 0.001
ADAM_B1 = 0.9
ADAM_B2 = 0.999
ADAM_EPS = 1e-08
ADAM_WD = 0.01
ADAM_STEP = 10

VMEM_LIMIT = 56 * 1024 * 1024
VMEM_SPEC = pl.BlockSpec(memory_space=pltpu.VMEM)
ANY_SPEC = pl.BlockSpec(memory_space=pl.ANY)
HBM_SPEC = pl.BlockSpec(memory_space=pltpu.HBM)
SEM_SPEC = pl.BlockSpec(memory_space=pltpu.SEMAPHORE)
SIDE_EFFECT = pltpu.SideEffectType.DATAFLOW_SIDE_EFFECTING
MESH = pl.DeviceIdType.MESH


def _cp(*sem):
    return pltpu.CompilerParams(dimension_semantics=sem, vmem_limit_bytes=VMEM_LIMIT)


def _nn(a, b):
    return jnp.dot(a, b, preferred_element_type=f32)


def _nt(a, b):
    return lax.dot_general(a, b, (((1,), (1,)), ((), ())), preferred_element_type=f32)


def _tn(a, b):
    return lax.dot_general(a, b, (((0,), (0,)), ((), ())), preferred_element_type=f32)


def _rms_fwd(x, g):
    r = lax.rsqrt(jnp.mean(x * x, axis=-1, keepdims=True) + EPS)
    xn = x * r
    return xn * g, xn, r


def _rms_bwd(dh, xn, r, g):
    dxn = dh * g
    dx = r * (dxn - xn * jnp.mean(dxn * xn, axis=-1, keepdims=True))
    dg = jnp.sum(dh * xn, axis=0, keepdims=True)
    return dx, dg


def _shift_dn(v, d, row, fill=0.0):
    return jnp.where(row >= d, pltpu.roll(v, d, 0), fill)


def _shift_up(v, d, row, fill=0.0):
    n = v.shape[0]
    return jnp.where(row < n - d, pltpu.roll(v, n - d, 0), fill)


def _ew(name, fn, ins, outs, tm=512):
    t = ins[0].shape[0]
    n_in = len(ins)

    def body(*refs):
        res = fn(*[r[...] for r in refs[:n_in]])
        for o, v in zip(refs[n_in:], res):
            o[...] = v.astype(o.dtype)

    return pl.pallas_call(
        body, name=name, grid=(t // tm,),
        in_specs=[pl.BlockSpec((tm, a.shape[1]), lambda i: (i, 0)) for a in ins],
        out_specs=[pl.BlockSpec((tm, n), lambda i: (i, 0)) for n, _ in outs],
        out_shape=[jax.ShapeDtypeStruct((t, n), dt) for n, dt in outs],
        compiler_params=_cp("parallel"),
    )(*ins)


def _mm(name, x, w, kind, extras=(), epilogue=None, outs=None, tm=512, tn=1024):
    t = x.shape[0]
    n = w.shape[1] if kind == "nn" else w.shape[0]
    tn = min(tn, n)
    outs = outs or [f32]
    n_ex = len(extras)

    def body(x_ref, w_ref, *refs):
        xb = x_ref[...].astype(bf16)
        acc = _nn(xb, w_ref[...]) if kind == "nn" else _nt(xb, w_ref[...])
        res = epilogue(acc, *[r[...] for r in refs[:n_ex]]) if epilogue else (acc,)
        for o, v in zip(refs[n_ex:], res):
            o[...] = v.astype(o.dtype)

    w_spec = (pl.BlockSpec((w.shape[0], tn), lambda i, j: (0, j)) if kind == "nn"
              else pl.BlockSpec((tn, w.shape[1]), lambda i, j: (j, 0)))
    tile = pl.BlockSpec((tm, tn), lambda i, j: (i, j))
    return pl.pallas_call(
        body, name=name, grid=(t // tm, n // tn),
        in_specs=[pl.BlockSpec((tm, x.shape[1]), lambda i, j: (i, 0)), w_spec] + [tile] * n_ex,
        out_specs=[tile] * len(outs),
        out_shape=[jax.ShapeDtypeStruct((t, n), dt) for dt in outs],
        compiler_params=_cp("parallel", "parallel"),
    )(x, w, *extras)


def _mm_tn(name, x, y, tk=1024, tn=1024, tt=1024):
    t, k = x.shape
    n = y.shape[1]
    tk, tn, tt = min(tk, k), min(tn, n), min(tt, t)

    def body(x_ref, y_ref, o_ref, ob_ref):
        @pl.when(pl.program_id(2) == 0)
        def _():
            o_ref[...] = jnp.zeros_like(o_ref)
        o_ref[...] += _tn(x_ref[...].astype(bf16), y_ref[...].astype(bf16))

        @pl.when(pl.program_id(2) == pl.num_programs(2) - 1)
        def _():
            ob_ref[...] = o_ref[...].astype(bf16)

    out = pl.BlockSpec((tk, tn), lambda i, j, s: (i, j))
    return pl.pallas_call(
        body, name=name, grid=(k // tk, n // tn, t // tt),
        in_specs=[pl.BlockSpec((tt, tk), lambda i, j, s: (s, i)), pl.BlockSpec((tt, tn), lambda i, j, s: (s, j))],
        out_specs=[out, out],
        out_shape=[jax.ShapeDtypeStruct((k, n), f32), jax.ShapeDtypeStruct((k, n), bf16)],
        compiler_params=_cp("parallel", "parallel", "arbitrary"),
    )(x, y)


def _norm_fwd(name, x, g, dtype, tm=512):
    t, d = x.shape

    def body(x_ref, g_ref, h_ref):
        h_ref[...] = _rms_fwd(x_ref[...], g_ref[...])[0].astype(dtype)

    row = pl.BlockSpec((tm, d), lambda i: (i, 0))
    return pl.pallas_call(
        body, name=name, grid=(t // tm,),
        in_specs=[row, pl.BlockSpec((1, d), lambda i: (0, 0))],
        out_specs=row, out_shape=jax.ShapeDtypeStruct((t, d), dtype),
        compiler_params=_cp("parallel"),
    )(x, g)


def _norm_bwd(name, dh, x, g, dres, tm=512):
    t, d = x.shape

    def body(dh_ref, x_ref, g_ref, dres_ref, dx_ref, dg_ref):
        gv = g_ref[...]
        _, xn, r = _rms_fwd(x_ref[...], gv)
        dx, dg = _rms_bwd(dh_ref[...], xn, r, gv)
        dx_ref[...] = dres_ref[...] + dx

        @pl.when(pl.program_id(0) == 0)
        def _():
            dg_ref[...] = jnp.zeros_like(dg_ref)
        dg_ref[...] += dg

    row = pl.BlockSpec((tm, d), lambda i: (i, 0))
    vec = pl.BlockSpec((1, d), lambda i: (0, 0))
    return pl.pallas_call(
        body, name=name, grid=(t // tm,),
        in_specs=[row, row, vec, row],
        out_specs=[row, vec],
        out_shape=[jax.ShapeDtypeStruct((t, d), f32), jax.ShapeDtypeStruct((1, d), f32)],
        compiler_params=_cp("arbitrary"),
    )(dh, x, g, dres)


def _final_loss(x, g, target, tm=512):
    t, d = x.shape

    def body(x_ref, g_ref, t_ref, loss_ref, dx_ref, dg_ref):
        gv = g_ref[...]
        y, xn, r = _rms_fwd(x_ref[...], gv)
        err = y - t_ref[...]
        dy = err * (1.0 / d)
        dx, dg = _rms_bwd(dy, xn, r, gv)
        dx_ref[...] = dx

        @pl.when(pl.program_id(0) == 0)
        def _():
            dg_ref[...] = jnp.zeros_like(dg_ref)
            loss_ref[...] = jnp.zeros_like(loss_ref)
        dg_ref[...] += dg
        loss_ref[...] += jnp.full((1, 128), 0.5 / d, f32) * jnp.sum(err * err)

    row = pl.BlockSpec((tm, d), lambda i: (i, 0))
    vec = pl.BlockSpec((1, d), lambda i: (0, 0))
    return pl.pallas_call(
        body, name="final_loss", grid=(t // tm,),
        in_specs=[row, vec, row],
        out_specs=[pl.BlockSpec((1, 128), lambda i: (0, 0)), row, vec],
        out_shape=[jax.ShapeDtypeStruct((1, 128), f32), jax.ShapeDtypeStruct((t, d), f32),
                   jax.ShapeDtypeStruct((1, d), f32)],
        compiler_params=_cp("arbitrary"),
    )(x, g, target)


def _load_ffn_weights(hbm_refs, vmem_refs, sems):
    @pl.when(pl.program_id(0) == 0)
    def _():
        copies = []
        for k, (src, dst) in enumerate(zip(hbm_refs, vmem_refs)):
            for j in range(N_DEV):
                half = pl.ds((j % 2) * FF_PAD, FF_PAD)
                window = dst.at[j // 2, half, :]
                copies.append(pltpu.make_async_copy(src.at[j], window, sems.at[k * N_DEV + j]))
        for cp in copies:
            cp.start()
        for cp in copies:
            cp.wait()


def _ffn_weight_scratch(nj, d, ff):
    return [pltpu.VMEM((nj, ff, d), bf16), pltpu.VMEM((nj, ff, d), bf16), pltpu.VMEM((nj, ff, d), bf16),
            pltpu.SemaphoreType.DMA((3 * N_DEV,))]


def _ffn_fwd(name, x, g, w1, w3, w2, tm=512):
    t, d = x.shape
    nj, ff = N_DEV // 2, 2 * FF_PAD

    def body(x_ref, g_ref, w1_hbm, w3_hbm, w2_hbm, y_ref, a_ref, b_ref, w1_ref, w3_ref, w2_ref, sems):
        _load_ffn_weights((w1_hbm, w3_hbm, w2_hbm), (w1_ref, w3_ref, w2_ref), sems)
        xv = x_ref[...]
        h, _, _ = _rms_fwd(xv, g_ref[...])
        hb = h.astype(bf16)
        acc = jnp.zeros((tm, d), f32)
        for j in range(nj):
            a = _nt(hb, w1_ref[j])
            b = _nt(hb, w3_ref[j])
            a_ref[j] = a.astype(bf16)
            b_ref[j] = b.astype(bf16)
            u = (a * jax.nn.sigmoid(a) * b).astype(bf16)
            acc = acc + _nn(u, w2_ref[j])
        y_ref[...] = xv + 0.5 * acc

    row = pl.BlockSpec((tm, d), lambda i: (i, 0))
    mid = pl.BlockSpec((nj, tm, ff), lambda i: (0, i, 0))
    return pl.pallas_call(
        body, name=name, grid=(t // tm,),
        in_specs=[row, pl.BlockSpec((1, d), lambda i: (0, 0)), ANY_SPEC, ANY_SPEC, ANY_SPEC],
        out_specs=[row, mid, mid],
        out_shape=[jax.ShapeDtypeStruct((t, d), f32), jax.ShapeDtypeStruct((nj, t, ff), bf16),
                   jax.ShapeDtypeStruct((nj, t, ff), bf16)],
        scratch_shapes=_ffn_weight_scratch(nj, d, ff),
        compiler_params=_cp("arbitrary"),
    )(x, g, w1, w3, w2)


def _ffn_dx(name, dy, x, g, w1, w3, w2, a, b, tm=256):
    t, d = x.shape
    nj, ff = N_DEV // 2, 2 * FF_PAD

    def body(dy_ref, x_ref, g_ref, w1_hbm, w3_hbm, w2_hbm, a_ref, b_ref,
             dx_ref, dg_ref, hbt_ref, dyh_ref, ut_ref, da_ref, db_ref, w1_ref, w3_ref, w2_ref, sems):
        _load_ffn_weights((w1_hbm, w3_hbm, w2_hbm), (w1_ref, w3_ref, w2_ref), sems)
        gv = g_ref[...]
        h, xn, r = _rms_fwd(x_ref[...], gv)
        hbt_ref[...] = h.astype(bf16).T
        dyv = dy_ref[...]
        dyh = (0.5 * dyv).astype(bf16)
        dyh_ref[...] = dyh
        dh = jnp.zeros((tm, d), f32)
        dus = [_nt(dyh, w2_ref[j]) for j in range(nj)]
        for j in range(nj):
            av = a_ref[j].astype(f32)
            bv = b_ref[j].astype(f32)
            s = jax.nn.sigmoid(av)
            silu = av * s
            ut_ref[j] = (silu * bv).astype(bf16).T
            du = dus[j]
            dab = (du * bv * (s * (1.0 + av * (1.0 - s)))).astype(bf16)
            dbb = (du * silu).astype(bf16)
            da_ref[j] = dab
            db_ref[j] = dbb
            dh = dh + _nn(dab, w1_ref[j]) + _nn(dbb, w3_ref[j])
        dx, dg = _rms_bwd(dh, xn, r, gv)
        dx_ref[...] = dyv + dx

        @pl.when(pl.program_id(0) == 0)
        def _():
            dg_ref[...] = jnp.zeros_like(dg_ref)
        dg_ref[...] += dg

    row = pl.BlockSpec((tm, d), lambda i: (i, 0))
    vec = pl.BlockSpec((1, d), lambda i: (0, 0))
    mid = pl.BlockSpec((nj, tm, ff), lambda i: (0, i, 0))
    mid_shape = jax.ShapeDtypeStruct((nj, t, ff), bf16)
    return pl.pallas_call(
        body, name=name, grid=(t // tm,),
        in_specs=[row, row, vec, ANY_SPEC, ANY_SPEC, ANY_SPEC, mid, mid],
        out_specs=[row, vec, pl.BlockSpec((d, tm), lambda i: (0, i)), row,
                   pl.BlockSpec((nj, ff, tm), lambda i: (0, 0, i)), mid, mid],
        out_shape=[jax.ShapeDtypeStruct((t, d), f32), jax.ShapeDtypeStruct((1, d), f32),
                   jax.ShapeDtypeStruct((d, t), bf16), jax.ShapeDtypeStruct((t, d), bf16),
                   jax.ShapeDtypeStruct((nj, ff, t), bf16), mid_shape, mid_shape],
        scratch_shapes=_ffn_weight_scratch(nj, d, ff),
        compiler_params=_cp("arbitrary"),
    )(dy, x, g, w1, w3, w2, a, b)


def _ffn_dw(name, xt, ys, bufs, l, h, tt=2048):
    n = len(ys)
    t = ys[0].shape[-2]
    tt = min(tt, t)
    cut_cols = xt.ndim == 2

    def body(x_ref, *refs):
        y_refs, outs, accs = refs[:n], refs[2 * n:4 * n], refs[4 * n:]
        s = pl.program_id(1)
        xv = x_ref[0] if xt.ndim == 3 else x_ref[...]
        for k in range(n):
            prod = _nn(xv, y_refs[k][0] if ys[k].ndim == 3 else y_refs[k][...])

            @pl.when(s == 0)
            def _():
                accs[k][...] = prod

            @pl.when(s > 0)
            def _():
                accs[k][...] += prod

        @pl.when(s == pl.num_programs(1) - 1)
        def _():
            for k in range(n):
                total = accs[k][...]
                for e in range(2):
                    lo = e * FF_PAD
                    part = total[:, lo:lo + FF_SHARD] if cut_cols else total[lo:lo + FF_SHARD, :]
                    outs[k][e] = part
                    outs[n + k][e] = part.astype(bf16)

    x_spec = (pl.BlockSpec((1, xt.shape[1], tt), lambda p, s: (p, 0, s)) if xt.ndim == 3
              else pl.BlockSpec((xt.shape[0], tt), lambda p, s: (0, s)))
    y_specs = [pl.BlockSpec((1, tt, y.shape[2]), lambda p, s: (p, s, 0)) if y.ndim == 3
               else pl.BlockSpec((tt, y.shape[1]), lambda p, s: (s, 0)) for y in ys]
    dims = [b.shape[-2:] for b in bufs]
    outs = pl.pallas_call(
        body, name=name, grid=(N_DEV // 2, t // tt),
        in_specs=[x_spec] + y_specs + [ANY_SPEC] * n,
        out_specs=[pl.BlockSpec((2, None, None, k_, n_), lambda p, s: (p, l, h, 0, 0)) for k_, n_ in dims]
        + [pl.BlockSpec((2, k_, n_), lambda p, s: (p, 0, 0)) for k_, n_ in dims],
        out_shape=[jax.ShapeDtypeStruct(b.shape, b.dtype) for b in bufs]
        + [jax.ShapeDtypeStruct((N_DEV, k_, n_), bf16) for k_, n_ in dims],
        input_output_aliases={1 + n + k: k for k in range(n)},
        scratch_shapes=[pltpu.VMEM((xt.shape[-2], y.shape[-1]), f32) for y in ys],
        compiler_params=_cp("parallel", "arbitrary"),
    )(xt, *ys, *bufs)
    return outs[:n], outs[n:]


_LOG_GAMMA = [float(np.log1p(-np.float32(2.0) ** np.float32(-5.0 - h))) for h in range(RET_HEADS)]


def _ret_consts(h):
    lg = jnp.where(h == 0, _LOG_GAMMA[0], jnp.where(h == 1, _LOG_GAMMA[1],
                   jnp.where(h == 2, _LOG_GAMMA[2], _LOG_GAMMA[3]))).astype(f32)
    c = RET_CHUNK
    r = lax.broadcasted_iota(jnp.int32, (c, c), 0)
    cc = lax.broadcasted_iota(jnp.int32, (c, c), 1)
    decay = jnp.where(r >= cc, jnp.exp(lg * jnp.maximum((r - cc).astype(f32), 0.0)), 0.0)
    pos = lax.broadcasted_iota(jnp.int32, (c, 1), 0).astype(f32)
    kd = jnp.exp(lg * (c - 1.0 - pos))
    qd = jnp.exp(lg * (pos + 1.0))
    gc = jnp.exp(lg * c)
    return decay, kd, qd, gc


def _rope(x, cos, sin):
    return x * cos + pltpu.roll(x, HEAD_DIM // 2, 1) * sin


def _rope_t(g, cos, sin):
    return g * cos + pltpu.roll(g * sin, HEAD_DIM // 2, 1)


def _rope_tables(s):
    half = HEAD_DIM // 2
    inv = ROPE_BASE ** (-jnp.arange(half, dtype=f32) / half)
    ang = jnp.arange(s, dtype=f32)[:, None] * inv[None, :]
    cos, sin = jnp.cos(ang), jnp.sin(ang)
    return jnp.concatenate([cos, cos], axis=1), jnp.concatenate([-sin, sin], axis=1)


def _head_ln(o):
    mu = jnp.mean(o, axis=-1, keepdims=True)
    oc = o - mu
    rs = lax.rsqrt(jnp.mean(oc * oc, axis=-1, keepdims=True) + EPS)
    return oc * rs, rs


def _ret_fwd(proj, cos, sin, ret_g, nb, s):
    c = RET_CHUNK
    nc = s // c
    t = nb * s
    scale = HEAD_DIM ** -0.5

    def body(q_ref, k_ref, v_ref, gate_ref, cos_ref, sin_ref, g_ref, o_ref, rprev_ref, m_ref):
        decay, kd, qd, gc = _ret_consts(pl.program_id(0))
        gv = g_ref[...]

        def chunk(b, n, rv):
            rows = pl.ds(pl.multiple_of(b * s + n * c, c), c)
            pos = pl.ds(pl.multiple_of(n * c, c), c)
            cs, sn = cos_ref[pos, :], sin_ref[pos, :]
            q = _rope(q_ref[rows, :], cs, sn)
            k = _rope(k_ref[rows, :], cs, sn) * scale
            vb = v_ref[rows, :].astype(bf16)
            sc = _nt(q.astype(bf16), k.astype(bf16)) * decay
            rprev_ref[b, n] = rv
            o = _nn(sc.astype(bf16), vb) + _nn((q * qd).astype(bf16), rv.astype(bf16))
            o_ref[rows, :] = o
            y, _ = _head_ln(o)
            gate = gate_ref[rows, :]
            m_ref[rows, :] = y * gv * (gate * jax.nn.sigmoid(gate))
            return rv * gc + _tn((k * kd).astype(bf16), vb)

        def step(n, carry):
            return tuple(chunk(b, n, carry[b]) for b in range(nb))

        lax.fori_loop(0, nc, step, (jnp.zeros((HEAD_DIM, HEAD_DIM), f32),) * nb)

    def col(off):
        return pl.BlockSpec((t, HEAD_DIM), lambda h: (0, off + h))

    tab = pl.BlockSpec((s, HEAD_DIM), lambda h: (0, 0))
    return pl.pallas_call(
        body, name="ret_fwd", grid=(RET_HEADS,),
        in_specs=[col(0), col(4), col(8), col(12), tab, tab, pl.BlockSpec((1, HEAD_DIM), lambda h: (0, h))],
        out_specs=[col(0), pl.BlockSpec((nb, None, nc, HEAD_DIM, HEAD_DIM), lambda h: (0, h, 0, 0, 0)), col(0)],
        out_shape=[jax.ShapeDtypeStruct((t, RET_WIDTH), f32),
                   jax.ShapeDtypeStruct((nb, RET_HEADS, nc, HEAD_DIM, HEAD_DIM), f32),
                   jax.ShapeDtypeStruct((t, RET_WIDTH), f32)],
        compiler_params=_cp("parallel"),
    )(proj, proj, proj, proj, cos, sin, ret_g)


def _ret_bwd(dmerged, o_raw, rprev, proj, cos, sin, ret_g, nb, s):
    c = RET_CHUNK
    nc = s // c
    t = nb * s
    scale = HEAD_DIM ** -0.5

    def body(dm_ref, o_ref, rprev_ref, q_ref, k_ref, v_ref, gate_ref, cos_ref, sin_ref, g_ref,
             dq_ref, dk_ref, dv_ref, dgate_ref, dg_ref):
        decay, kd, qd, gc = _ret_consts(pl.program_id(0))
        gv = g_ref[...]

        def chunk(b, n, drn, dg):
            rows = pl.ds(pl.multiple_of(b * s + n * c, c), c)
            pos = pl.ds(pl.multiple_of(n * c, c), c)
            cs, sn = cos_ref[pos, :], sin_ref[pos, :]
            q = _rope(q_ref[rows, :], cs, sn)
            k = _rope(k_ref[rows, :], cs, sn) * scale
            qb, kb = q.astype(bf16), k.astype(bf16)
            vb = v_ref[rows, :].astype(bf16)
            sc = _nt(qb, kb) * decay
            y, rs = _head_ln(o_ref[rows, :])
            gate = gate_ref[rows, :]
            sg = jax.nn.sigmoid(gate)
            silu = gate * sg
            dm = dm_ref[rows, :]
            dgate_ref[rows, :] = dm * y * gv * (sg * (1.0 + gate * (1.0 - sg)))
            dyl = dm * gv * silu
            dg = dg + jnp.sum(dm * y * silu, axis=0, keepdims=True)
            do = rs * (dyl - jnp.mean(dyl, axis=-1, keepdims=True) - y * jnp.mean(dyl * y, axis=-1, keepdims=True))
            dob = do.astype(bf16)
            rv = rprev_ref[b, n]
            drb = drn.astype(bf16)
            ds = (_nt(dob, vb) * decay).astype(bf16)
            kdb = (k * kd).astype(bf16)
            qdb = (q * qd).astype(bf16)
            dq_r = _nn(ds, kb) + _nt(dob, rv.astype(bf16)) * qd
            dk_r = _tn(ds, qb) + _nt(vb, drb) * kd
            dv_ref[rows, :] = _tn(sc.astype(bf16), dob) + _nn(kdb, drb)
            dq_ref[rows, :] = _rope_t(dq_r, cs, sn)
            dk_ref[rows, :] = _rope_t(dk_r * scale, cs, sn)
            return drn * gc + _tn(qdb, dob), dg

        def step(i, carry):
            out = [chunk(b, nc - 1 - i, *carry[b]) for b in range(nb)]
            return tuple(out)

        zero = (jnp.zeros((HEAD_DIM, HEAD_DIM), f32), jnp.zeros((1, HEAD_DIM), f32))
        done = lax.fori_loop(0, nc, step, (zero,) * nb)
        dg_ref[...] = sum(dg for _, dg in done)

    def col(off):
        return pl.BlockSpec((t, HEAD_DIM), lambda h: (0, off + h))

    tab = pl.BlockSpec((s, HEAD_DIM), lambda h: (0, 0))
    gsp = pl.BlockSpec((1, HEAD_DIM), lambda h: (0, h))
    out_t = jax.ShapeDtypeStruct((t, RET_WIDTH), f32)
    return pl.pallas_call(
        body, name="ret_bwd", grid=(RET_HEADS,),
        in_specs=[col(0), col(0), pl.BlockSpec((nb, None, nc, HEAD_DIM, HEAD_DIM), lambda h: (0, h, 0, 0, 0)),
                  col(0), col(4), col(8), col(12), tab, tab, gsp],
        out_specs=[col(0), col(0), col(0), col(0), gsp],
        out_shape=[out_t, out_t, out_t, out_t, jax.ShapeDtypeStruct((1, RET_WIDTH), f32)],
        compiler_params=_cp("parallel"),
    )(dmerged, o_raw, rprev, proj, proj, proj, proj, cos, sin, ret_g)


def _neg_expm1(z):
    series = -(z * (1.0 + z * (0.5 + z * (1.0 / 6.0 + z * (1.0 / 24.0)))))
    return jnp.where(z > -0.01, series, 1.0 - jnp.exp(z))


def _lru_gates(xc, pa, pi, lam):
    r = jax.nn.sigmoid(pa)
    i = jax.nn.sigmoid(pi)
    log_a = -LRU_C * r * jax.nn.softplus(-lam)
    a = jnp.exp(log_a)
    bx = jnp.sqrt(_neg_expm1(2.0 * log_a)) * i * xc
    return a, bx


def _scan_rows(a, b, row, up):
    sub = row[:SUBLANES] & (SUBLANES - 1)
    groups = list(range(a.shape[0] // SUBLANES))
    out = [None] * len(groups)
    edge = slice(0, 1) if up else slice(SUBLANES - 1, SUBLANES)
    carry = jnp.zeros((1, a.shape[1]), f32)
    for g in (reversed(groups) if up else groups):
        rows = slice(g * SUBLANES, (g + 1) * SUBLANES)
        xa, xb = a[rows], b[rows]
        d = 1
        while d < SUBLANES:
            keep = (sub < SUBLANES - d) if up else (sub >= d)
            shift = SUBLANES - d if up else d
            xb = xa * jnp.where(keep, pltpu.roll(xb, shift, 0), 0.0) + xb
            xa = xa * jnp.where(keep, pltpu.roll(xa, shift, 0), 1.0)
            d *= 2
        out[g] = xb + xa * carry
        carry = out[g][edge]
    return jnp.concatenate(out, axis=0)


def _scan_fwd(a, b, row):
    return _scan_rows(a, b, row, False)


def _scan_bwd(c, b, row):
    return _scan_rows(c, b, row, True)


def _conv_fwd(x, cw, cb, row):
    return (cb + cw[3:4] * x + cw[2:3] * _shift_dn(x, 1, row) + cw[1:2] * _shift_dn(x, 2, row)
            + cw[0:1] * _shift_dn(x, 3, row))


def _lru_specs(s, order):
    def im(f):
        return (lambda b, g: f(b, g)) if order == "bg" else (lambda g, b: f(b, g))
    seq = lambda off: pl.BlockSpec((s, 128), im(lambda b, g: (b, off + g)))
    vec = pl.BlockSpec((1, 128), im(lambda b, g: (0, g)))
    cw = pl.BlockSpec((4, 128), im(lambda b, g: (0, g)))
    mat = pl.BlockSpec((1, 128, 128), im(lambda b, g: (g, 0, 0)))
    return seq, vec, cw, mat


def _lru_fwd(proj, conv_w, conv_b, w_a, b_a, w_i, b_i, lam, nb, s):
    def body(x_ref, gt_ref, cw_ref, cb_ref, wa_ref, ba_ref, wi_ref, bi_ref, lam_ref, out_ref, h_ref):
        row = lax.broadcasted_iota(jnp.int32, (s, 128), 0)
        xc = _conv_fwd(x_ref[...], cw_ref[...], cb_ref[...], row)
        xcb = xc.astype(bf16)
        pa = _nn(xcb, wa_ref[0].astype(bf16)) + ba_ref[...]
        pi = _nn(xcb, wi_ref[0].astype(bf16)) + bi_ref[...]
        a, bx = _lru_gates(xc, pa, pi, lam_ref[...])
        h = _scan_fwd(a, bx, row)
        h_ref[...] = h
        out_ref[...] = h * jax.nn.gelu(gt_ref[...])

    seq, vec, cw, mat = _lru_specs(s, "bg")
    out = jax.ShapeDtypeStruct((nb * s, LRU_WIDTH), f32)
    return pl.pallas_call(
        body, name="lru_fwd", grid=(nb, LRU_BLOCKS),
        in_specs=[seq(16), seq(20), cw, vec, mat, vec, mat, vec, vec],
        out_specs=[seq(0), seq(0)], out_shape=[out, out],
        compiler_params=_cp("parallel", "parallel"),
    )(proj, proj, conv_w, conv_b, w_a, b_a, w_i, b_i, lam)


def _lru_bwd(dmerged, states, proj, conv_w, conv_b, w_a, b_a, w_i, b_i, lam, nb, s):
    def body(dout_ref, h_ref, x_ref, gt_ref, cw_ref, cb_ref, wa_ref, ba_ref, wi_ref, bi_ref, lam_ref,
             dx_ref, dgt_ref, dcw_ref, dcb_ref, dwa_ref, dba_ref, dwi_ref, dbi_ref, dlam_ref):
        row = lax.broadcasted_iota(jnp.int32, (s, 128), 0)
        x = x_ref[...]
        cwv = cw_ref[...]
        xc = _conv_fwd(x, cwv, cb_ref[...], row)
        xcb = xc.astype(bf16)
        wab, wib = wa_ref[0].astype(bf16), wi_ref[0].astype(bf16)
        pa = _nn(xcb, wab) + ba_ref[...]
        pi = _nn(xcb, wib) + bi_ref[...]
        (a, _), gates_vjp = jax.vjp(_lru_gates, xc, pa, pi, lam_ref[...])
        h = h_ref[...]
        ge, gelu_vjp = jax.vjp(jax.nn.gelu, gt_ref[...])
        dout = dout_ref[...]
        dgt_ref[...] = gelu_vjp(dout * h)[0]
        adj = _scan_bwd(_shift_up(a, 1, row), dout * ge, row)
        dxc, dpa, dpi, dlam = gates_vjp((adj * _shift_dn(h, 1, row), adj))
        dpab, dpib = dpa.astype(bf16), dpi.astype(bf16)
        dxc = dxc + _nt(dpab, wab) + _nt(dpib, wib)
        dx_ref[...] = (cwv[3:4] * dxc + cwv[2:3] * _shift_up(dxc, 1, row) + cwv[1:2] * _shift_up(dxc, 2, row)
                       + cwv[0:1] * _shift_up(dxc, 3, row))

        @pl.when(pl.program_id(1) == 0)
        def _():
            for r in (dcw_ref, dcb_ref, dwa_ref, dba_ref, dwi_ref, dbi_ref, dlam_ref):
                r[...] = jnp.zeros_like(r)
        rsum = lambda v: jnp.sum(v, axis=0, keepdims=True)
        dcw_ref[...] += jnp.concatenate([rsum(dxc * _shift_dn(x, 3, row)), rsum(dxc * _shift_dn(x, 2, row)),
                                         rsum(dxc * _shift_dn(x, 1, row)), rsum(dxc * x)], axis=0)
        dcb_ref[...] += rsum(dxc)
        dwa_ref[0] += _tn(xcb, dpab)
        dwi_ref[0] += _tn(xcb, dpib)
        dba_ref[...] += rsum(dpa)
        dbi_ref[...] += rsum(dpi)
        dlam_ref[...] += dlam

    seq, vec, cw, mat = _lru_specs(s, "gb")
    t = nb * s
    vshape = jax.ShapeDtypeStruct((1, LRU_WIDTH), f32)
    mshape = jax.ShapeDtypeStruct((LRU_BLOCKS, 128, 128), f32)
    return pl.pallas_call(
        body, name="lru_bwd", grid=(LRU_BLOCKS, nb),
        in_specs=[seq(4), seq(0), seq(16), seq(20), cw, vec, mat, vec, mat, vec, vec],
        out_specs=[seq(0), seq(0), cw, vec, mat, vec, mat, vec, vec],
        out_shape=[jax.ShapeDtypeStruct((t, LRU_WIDTH), f32), jax.ShapeDtypeStruct((t, LRU_WIDTH), f32),
                   jax.ShapeDtypeStruct((4, LRU_WIDTH), f32), vshape, mshape, vshape, mshape, vshape, vshape],
        compiler_params=_cp("parallel", "arbitrary"),
    )(dmerged, states, proj, proj, conv_w, conv_b, w_a, b_a, w_i, b_i, lam)


def _s5_disc(lr, li, ldt, bre, bim):
    dt = jnp.exp(ldt)
    mag = jnp.exp(lr * dt)
    lbr = mag * jnp.cos(li * dt)
    lbi = mag * jnp.sin(li * dt)
    den = lr * lr + li * li
    nr = lbr - 1.0
    fr = (nr * lr + lbi * li) / den
    fi = (lbi * lr - nr * li) / den
    bbr = fr[:, None, :] * bre - fi[:, None, :] * bim
    bbi = fr[:, None, :] * bim + fi[:, None, :] * bre
    return lbr, lbi, bbr, bbi


def _s5_prep(lr, li, ldt, bre, bim):
    def body(lr_ref, li_ref, ldt_ref, bre_ref, bim_ref, o1, o2, o3, o4):
        o1[...], o2[...], o3[...], o4[...] = _s5_disc(lr_ref[...], li_ref[...], ldt_ref[...], bre_ref[...], bim_ref[...])

    return pl.pallas_call(
        body, name="s5_prep", in_specs=[VMEM_SPEC] * 5, out_specs=[VMEM_SPEC] * 4,
        out_shape=[jax.ShapeDtypeStruct(lr.shape, f32), jax.ShapeDtypeStruct(lr.shape, f32),
                   jax.ShapeDtypeStruct(bre.shape, f32), jax.ShapeDtypeStruct(bre.shape, f32)],
    )(lr, li, ldt, bre, bim)


def _s5_prep_bwd(lr, li, ldt, bre, bim, cts):
    def body(lr_ref, li_ref, ldt_ref, bre_ref, bim_ref, g1, g2, g3, g4, o1, o2, o3, o4, o5):
        _, vjp = jax.vjp(_s5_disc, lr_ref[...], li_ref[...], ldt_ref[...], bre_ref[...], bim_ref[...])
        o1[...], o2[...], o3[...], o4[...], o5[...] = vjp((g1[...], g2[...], g3[...], g4[...]))

    return pl.pallas_call(
        body, name="s5_prep_bwd", in_specs=[VMEM_SPEC] * 9, out_specs=[VMEM_SPEC] * 5,
        out_shape=[jax.ShapeDtypeStruct(v.shape, f32) for v in (lr, li, ldt, bre, bim)],
    )(lr, li, ldt, bre, bim, *cts)


def _cmul(ar, ai, br, bi):
    return ar * br - ai * bi, ar * bi + ai * br


def _s5_pow_table(lr, li, n, row, up):
    ar = jnp.broadcast_to(lr, (n, lr.shape[1]))
    ai = jnp.broadcast_to(li, (n, li.shape[1]))
    shift = _shift_up if up else _shift_dn
    d = 1
    while d < n:
        ar, ai = _cmul(ar, ai, shift(ar, d, row, 1.0), shift(ai, d, row, 0.0))
        d *= 2
    return ar, ai


def _s5_step_factors(lr, li, row, up):
    sub = row & (SUBLANES - 1)
    out, pr, pi, d = [], lr, li, 1
    while d < SUBLANES:
        keep = (sub < SUBLANES - d) if up else (sub >= d)
        out.append((jnp.where(keep, pr, 0.0), jnp.where(keep, pi, 0.0)))
        pr, pi = _cmul(pr, pi, pr, pi)
        d *= 2
    return out


def _s5_scan(br, bi, steps, tab_r, tab_i, cr, ci, up):
    groups = list(range(br.shape[0] // SUBLANES))
    out_r, out_i = [None] * len(groups), [None] * len(groups)
    edge = slice(0, 1) if up else slice(SUBLANES - 1, SUBLANES)
    for g in (reversed(groups) if up else groups):
        rows = slice(g * SUBLANES, (g + 1) * SUBLANES)
        xr, xi = br[rows], bi[rows]
        for k, (mr, mi) in enumerate(steps):
            shift = SUBLANES - (1 << k) if up else 1 << k
            tr, ti = _cmul(mr, mi, pltpu.roll(xr, shift, 0), pltpu.roll(xi, shift, 0))
            xr, xi = xr + tr, xi + ti
        tr, ti = _cmul(tab_r, tab_i, cr, ci)
        hr, hi = xr + tr, xi + ti
        out_r[g], out_i[g] = hr, hi
        cr, ci = hr[edge], hi[edge]
    return jnp.concatenate(out_r, axis=0), jnp.concatenate(out_i, axis=0)


def _s5_specs(t, nb, nc):
    seq = pl.BlockSpec((t, 128), lambda k: (0, k))
    lvec = pl.BlockSpec((1, S5_BLOCK_STATES), lambda k: (0, k))
    dvec = pl.BlockSpec((1, 128), lambda k: (0, k))
    wmat = pl.BlockSpec((1, 128, S5_BLOCK_STATES), lambda k: (k, 0, 0))
    h0 = pl.BlockSpec((nb, None, nc, 2, S5_BLOCK_STATES), lambda k: (0, k, 0, 0, 0))
    states = pl.BlockSpec((t, S5_BLOCK_STATES), lambda k: (0, k))
    return seq, lvec, dvec, wmat, h0, states


def _s5_fwd(u, lbr, lbi, wbr, wbi, wcr, wci, dskip, nb, s):
    ln = min(S5_CHUNK, s)
    nc = s // ln

    def body(u_ref, lr_ref, li_ref, wbr_ref, wbi_ref, wcr_ref, wci_ref, d_ref, yg_ref, y_ref, h0_ref, hr_ref, hi_ref):
        row = lax.broadcasted_iota(jnp.int32, (ln, S5_BLOCK_STATES), 0)
        lr, li = lr_ref[...], li_ref[...]
        pr, pi = _s5_pow_table(lr, li, SUBLANES, row[:SUBLANES], False)
        steps = _s5_step_factors(lr, li, row[:SUBLANES], False)
        dv = d_ref[...]

        def chunk(b, n, h0r, h0i):
            st = pl.multiple_of(b * s + n * ln, ln)
            uc = u_ref[pl.ds(st, ln), :]
            ub = uc.astype(bf16)
            hr, hi = _s5_scan(_nn(ub, wbr_ref[0]), _nn(ub, wbi_ref[0]), steps, pr, pi, h0r, h0i, False)
            h0_ref[b, n, 0:1, :] = h0r
            h0_ref[b, n, 1:2, :] = h0i
            hrb, hib = hr.astype(bf16), hi.astype(bf16)
            hr_ref[pl.ds(st, ln), :] = hrb
            hi_ref[pl.ds(st, ln), :] = hib
            y = _nt(hrb, wcr_ref[0]) - _nt(hib, wci_ref[0]) + dv * uc
            y_ref[pl.ds(st, ln), :] = y
            yg_ref[pl.ds(st, ln), :] = jax.nn.gelu(y).astype(bf16)
            return hr[ln - 1:ln, :], hi[ln - 1:ln, :]

        def step(n, carry):
            return tuple(chunk(b, n, *carry[b]) for b in range(nb))

        z = jnp.zeros((1, S5_BLOCK_STATES), f32)
        lax.fori_loop(0, nc, step, ((z, z),) * nb)

    t = nb * s
    seq, lvec, dvec, wmat, h0, states = _s5_specs(t, nb, nc)
    return pl.pallas_call(
        body, name="s5_fwd", grid=(S5_BLOCKS,),
        in_specs=[seq, lvec, lvec, wmat, wmat, wmat, wmat, dvec],
        out_specs=[seq, seq, h0, states, states],
        out_shape=[jax.ShapeDtypeStruct((t, D_MODEL), bf16), jax.ShapeDtypeStruct((t, D_MODEL), f32),
                   jax.ShapeDtypeStruct((nb, S5_BLOCKS, nc, 2, S5_BLOCK_STATES), f32),
                   jax.ShapeDtypeStruct((t, S5_BLOCKS * S5_BLOCK_STATES), bf16),
                   jax.ShapeDtypeStruct((t, S5_BLOCKS * S5_BLOCK_STATES), bf16)],
        compiler_params=_cp("parallel"),
    )(u, lbr, lbi, wbr, wbi, wcr, wci, dskip)


def _s5_bwd(dyg, y, u, h0, hrs, his, lbr, lbi, wbr, wbi, wcr, wci, dskip, nb, s):
    ln = min(S5_CHUNK, s)
    nc = s // ln

    def body(dyg_ref, y_ref, u_ref, h0_ref, hr_ref, hi_ref, lr_ref, li_ref, wbr_ref, wbi_ref, wcr_ref, wci_ref, d_ref,
             du_ref, dlr_ref, dli_ref, dwbr_ref, dwbi_ref, dwcr_ref, dwci_ref, dd_ref):
        for r in (dlr_ref, dli_ref, dwbr_ref, dwbi_ref, dwcr_ref, dwci_ref, dd_ref):
            r[...] = jnp.zeros_like(r)
        row = lax.broadcasted_iota(jnp.int32, (ln, S5_BLOCK_STATES), 0)
        lr, li = lr_ref[...], li_ref[...]
        qr, qi = _s5_pow_table(lr, -li, SUBLANES, row[:SUBLANES], True)
        steps_up = _s5_step_factors(lr, -li, row[:SUBLANES], True)
        dv = d_ref[...]
        rsum = lambda v: jnp.sum(v, axis=0, keepdims=True)

        def chunk(b, n, gnr, gni):
            st = pl.multiple_of(b * s + n * ln, ln)
            uc = u_ref[pl.ds(st, ln), :]
            ub = uc.astype(bf16)
            h0v = h0_ref[b, n]
            h0r, h0i = h0v[0:1], h0v[1:2]
            hrb, hib = hr_ref[pl.ds(st, ln), :], hi_ref[pl.ds(st, ln), :]
            hr, hi = hrb.astype(f32), hib.astype(f32)
            dy = jax.vjp(jax.nn.gelu, y_ref[pl.ds(st, ln), :])[1](dyg_ref[pl.ds(st, ln), :])[0]
            dyb = dy.astype(bf16)
            dd_ref[...] += rsum(dy * uc)
            gr, gi = _s5_scan(_nn(dyb, wcr_ref[0]), -_nn(dyb, wci_ref[0]), steps_up, qr, qi, gnr, gni, True)
            hpr = jnp.where(row >= 1, pltpu.roll(hr, 1, 0), h0r)
            hpi = jnp.where(row >= 1, pltpu.roll(hi, 1, 0), h0i)
            dlr_ref[...] += rsum(gr * hpr + gi * hpi)
            dli_ref[...] += rsum(gi * hpr - gr * hpi)
            grb, gib = gr.astype(bf16), gi.astype(bf16)
            dwbr_ref[0] += _tn(ub, grb)
            dwbi_ref[0] += _tn(ub, gib)
            dwcr_ref[0] += _tn(dyb, hrb)
            dwci_ref[0] -= _tn(dyb, hib)
            du_ref[pl.ds(st, ln), :] = _nt(grb, wbr_ref[0]) + _nt(gib, wbi_ref[0]) + dv * dy
            return gr[0:1, :], gi[0:1, :]

        def step(i, carry):
            return tuple(chunk(b, nc - 1 - i, *carry[b]) for b in range(nb))

        z = jnp.zeros((1, S5_BLOCK_STATES), f32)
        lax.fori_loop(0, nc, step, ((z, z),) * nb)

    t = nb * s
    seq, lvec, dvec, wmat, h0s, states = _s5_specs(t, nb, nc)
    lshape = jax.ShapeDtypeStruct((1, S5_BLOCKS * S5_BLOCK_STATES), f32)
    wshape = jax.ShapeDtypeStruct((S5_BLOCKS, 128, S5_BLOCK_STATES), f32)
    return pl.pallas_call(
        body, name="s5_bwd", grid=(S5_BLOCKS,),
        in_specs=[seq, seq, seq, h0s, states, states, lvec, lvec, wmat, wmat, wmat, wmat, dvec],
        out_specs=[seq, lvec, lvec, wmat, wmat, wmat, wmat, dvec],
        out_shape=[jax.ShapeDtypeStruct((t, D_MODEL), f32), lshape, lshape, wshape, wshape, wshape, wshape,
                   jax.ShapeDtypeStruct((1, D_MODEL), f32)],
        compiler_params=_cp("parallel"),
    )(dyg, y, u, h0, hrs, his, lbr, lbi, wbr, wbi, wcr, wci, dskip)


def _blockdiag(w):
    w4 = w.reshape(S5_BLOCKS, 8, S5_GROUP, S5_STATE)
    same_group = jnp.eye(8, dtype=bool)[None, :, None, :, None]
    return jnp.where(same_group, w4[:, :, :, None, :], 0.0).reshape(S5_BLOCKS, 128, S5_BLOCK_STATES)


def _blockdiag_t(dw):
    d5 = dw.reshape(S5_BLOCKS, 8, S5_GROUP, 8, S5_STATE)
    diag = jnp.diagonal(d5, axis1=1, axis2=3)
    return jnp.moveaxis(diag, 3, 1).reshape(S5_GROUPS, S5_GROUP, S5_STATE)


def _glu_fwd(ygb, wa, wb, x, tm=512, tn=1024):
    t, d = x.shape

    def body(y_ref, wa_ref, wb_ref, x_ref, o_ref, p_ref, q_ref):
        p = _nn(y_ref[...], wa_ref[...])
        q = _nn(y_ref[...], wb_ref[...])
        p_ref[...] = p
        q_ref[...] = q
        o_ref[...] = x_ref[...] + p * jax.nn.sigmoid(q)

    tile = pl.BlockSpec((tm, tn), lambda i, j: (i, j))
    wsp = pl.BlockSpec((d, tn), lambda i, j: (0, j))
    out = jax.ShapeDtypeStruct((t, d), f32)
    return pl.pallas_call(
        body, name="glu_fwd", grid=(t // tm, d // tn),
        in_specs=[pl.BlockSpec((tm, d), lambda i, j: (i, 0)), wsp, wsp, tile],
        out_specs=[tile, tile, tile], out_shape=[out, out, out],
        compiler_params=_cp("parallel", "parallel"),
    )(ygb, wa, wb, x)


def _place():
    x, y, c = lax.axis_index("x"), lax.axis_index("y"), lax.axis_index("c")
    return x, y, c, [(1 - x, y), (x, 1 - y), (1 - x, 1 - y)]


def _all_gather(name, arrays):
    n = len(arrays)

    def body(*refs):
        ins, outs = refs[:n], refs[n:2 * n]
        send_sems, recv_sems, local_sems = refs[2 * n:]
        x, y, c, chips = _place()
        me, sib = (x, y, c), (x, y, 1 - c)

        def copy(i, k, block, to, src=None):
            dst = outs[i].at[4 * block[0] + 2 * block[1] + block[2]]
            return pltpu.make_async_remote_copy(
                src_ref=dst if src is None else src, dst_ref=dst,
                send_sem=send_sems.at[i * 7 + k], recv_sem=recv_sems.at[i * 7 + k],
                device_id=to, device_id_type=MESH)

        mine = [pltpu.make_async_copy(ins[i], outs[i].at[4 * x + 2 * y + c], local_sems.at[i]) for i in range(n)]
        for m in mine:
            m.start()
        first = []
        for i in range(n):
            first.append(copy(i, 0, me, sib, src=ins[i]))
            first += [copy(i, 1 + j, me, (*chip, c), src=ins[i]) for j, chip in enumerate(chips)]
        for cp in first:
            cp.start()
        passed = []
        for j, chip in enumerate(chips):
            for i in range(n):
                copy(i, 1 + j, (*chip, c), me).wait_recv()
                fwd = copy(i, 4 + j, (*chip, c), sib)
                fwd.start()
                passed.append(fwd)
        for i in range(n):
            copy(i, 0, sib, me).wait_recv()
        for j, chip in enumerate(chips):
            for i in range(n):
                copy(i, 4 + j, (*chip, 1 - c), me).wait_recv()
        for cp in first + passed:
            cp.wait_send()
        for m in mine:
            m.wait()

    return pl.pallas_call(
        body, name=name,
        in_specs=[ANY_SPEC] * n, out_specs=[ANY_SPEC] * n,
        out_shape=[jax.ShapeDtypeStruct((N_DEV,) + a.shape, a.dtype) for a in arrays],
        scratch_shapes=[pltpu.SemaphoreType.DMA((7 * n,)), pltpu.SemaphoreType.DMA((7 * n,)),
                        pltpu.SemaphoreType.DMA((n,))],
    )(*arrays)


def _tie(name, x, deps):
    def body(*refs):
        pass

    return pl.pallas_call(
        body, name=name, in_specs=[ANY_SPEC] * (1 + len(deps)), out_specs=ANY_SPEC,
        out_shape=jax.ShapeDtypeStruct(x.shape, x.dtype), input_output_aliases={0: 0},
    )(x, *deps)


def _xchg_copies(kind, srcs, lands, suffixes, send_sems, recv_sems):
    x, y, c, _ = _place()
    copies = []
    for i, (src, land, sfx) in enumerate(zip(srcs, lands, suffixes)):
        for k in range(N_DEV - 1):
            r = k + 1
            peer = (1 - x if r & 4 else x, 1 - y if r & 2 else y, 1 - c if r & 1 else c)
            if kind == "gather":
                s_ref, d_ref = src, land.at[(4 * x + 2 * y + c,) + sfx]
            else:
                s_ref, d_ref = src.at[4 * peer[0] + 2 * peer[1] + peer[2]], land.at[(k,) + sfx]
            copies.append(pltpu.make_async_remote_copy(
                src_ref=s_ref, dst_ref=d_ref, send_sem=send_sems.at[i * 7 + k], recv_sem=recv_sems.at[i * 7 + k],
                device_id=peer, device_id_type=MESH))
    return copies


def _xchg_start(name, kind, srcs, lands, suffixes=None):
    n = len(srcs)
    suffixes = suffixes or [()] * n

    def body(*refs):
        src, land = refs[:n], refs[n:2 * n]
        send_sems, recv_sems, token = refs[2 * n], refs[2 * n + 1], refs[-1]
        for cp in _xchg_copies(kind, src, land, suffixes, send_sems, recv_sems):
            cp.start()
        token[...] = jnp.zeros_like(token)

    arrays = list(srcs) + list(lands)
    outs = pl.pallas_call(
        body, name=name,
        out_shape=(pltpu.SemaphoreType.DMA((7 * n,)), pltpu.SemaphoreType.DMA((7 * n,)),
                   *[pltpu.HBM(a.shape, a.dtype) for a in arrays], jax.ShapeDtypeStruct((8, 128), f32)),
        in_specs=[HBM_SPEC] * (2 * n),
        out_specs=(SEM_SPEC, SEM_SPEC, *[HBM_SPEC] * (2 * n), VMEM_SPEC),
        input_output_aliases={i: 2 + i for i in range(2 * n)},
        compiler_params=pltpu.CompilerParams(has_side_effects=SIDE_EFFECT),
    )(*[pltpu.with_memory_space_constraint(a, pltpu.HBM) for a in arrays])
    return dict(kind=kind, n=n, suffixes=suffixes, send=outs[0], recv=outs[1], srcs=list(outs[2:2 + n]),
                lands=list(outs[2 + n:2 + 2 * n]), token=outs[-1])


def _xchg_wait(name, h, after, lands=None):
    n = h["n"]
    lands = h["lands"] if lands is None else lands

    def body(*refs):
        src, land = refs[:n], refs[n:2 * n]
        for cp in _xchg_copies(h["kind"], src, land, h["suffixes"], refs[2 * n], refs[2 * n + 1]):
            cp.wait_send()
            cp.wait_recv()

    arrays = h["srcs"] + list(lands)
    outs = pl.pallas_call(
        body, name=name,
        out_shape=tuple(pltpu.HBM(a.shape, a.dtype) for a in arrays),
        in_specs=[HBM_SPEC] * (2 * n) + [SEM_SPEC, SEM_SPEC] + [ANY_SPEC] * len(after),
        out_specs=tuple([HBM_SPEC] * (2 * n)),
        input_output_aliases={i: i for i in range(2 * n)},
        compiler_params=pltpu.CompilerParams(has_side_effects=SIDE_EFFECT),
    )(*arrays, h["send"], h["recv"], *after)
    return list(outs[n:])


def _rows(a):
    return a.reshape(-1, a.shape[-1])


def _row_tile(r):
    for tm in (512, 256, 128, 64, 32, 16, 8):
        if r % tm == 0:
            return tm
    return r


def _sum8(name, gathered):
    _, r, n = gathered.shape
    tm = _row_tile(r)

    def body(g_ref, o_ref):
        acc = g_ref[0]
        for k in range(1, N_DEV):
            acc = acc + g_ref[k]
        o_ref[...] = acc

    return pl.pallas_call(
        body, name=name, grid=(r // tm,),
        in_specs=[pl.BlockSpec((N_DEV, tm, n), lambda i: (0, i, 0))],
        out_specs=pl.BlockSpec((tm, n), lambda i: (i, 0)),
        out_shape=jax.ShapeDtypeStruct((r, n), f32),
        compiler_params=_cp("parallel"),
    )(gathered)


def _adamw(name, w, m, v, own, landed=None, slot=None):
    shape = w.shape
    w2, m2, v2 = _rows(w), _rows(m), _rows(v)
    r, n = w2.shape
    tm = _row_tile(r)
    c1 = 1.0 - ADAM_B1 ** ADAM_STEP
    c2 = 1.0 - ADAM_B2 ** ADAM_STEP
    extra = [] if landed is None else [landed.reshape(landed.shape[0], r, n)]
    row = pl.BlockSpec((tm, n), lambda i, *_: (i, 0))
    if slot is None:
        o2, own_spec, scalars = _rows(own), row, []
    else:
        dev, kind = slot
        scalars = [dev.reshape(1).astype(jnp.int32)]
        if kind == "lead":
            o2, own_spec = own.reshape(N_DEV, r, n), pl.BlockSpec((None, tm, n), lambda i, d: (d[0], i, 0))
        elif kind == "rows":
            o2, own_spec = own, pl.BlockSpec((tm, n), lambda i, d: (d[0] * (r // tm) + i, 0))
        else:
            o2, own_spec = own, pl.BlockSpec((tm, n), lambda i, d: (i, d[0]))

    def body(*refs):
        w_ref, m_ref, v_ref, o_ref = refs[len(scalars):len(scalars) + 4]
        refs = refs[len(scalars) + 4:]
        g = o_ref[...]
        if extra:
            for k in range(extra[0].shape[0]):
                g = g + refs[0][k].astype(f32)
        g_ref, d_ref, mn_ref, vn_ref = refs[len(extra):]
        mn = ADAM_B1 * m_ref[...] + (1.0 - ADAM_B1) * g
        vn = ADAM_B2 * v_ref[...] + (1.0 - ADAM_B2) * (g * g)
        g_ref[...] = g
        d_ref[...] = -ADAM_LR * ((mn / c1) / (jnp.sqrt(vn / c2) + ADAM_EPS) + ADAM_WD * w_ref[...])
        mn_ref[...] = mn
        vn_ref[...] = vn

    outs = pl.pallas_call(
        body, name=name,
        grid_spec=pltpu.PrefetchScalarGridSpec(
            num_scalar_prefetch=len(scalars), grid=(r // tm,),
            in_specs=[row] * 3 + [own_spec] + [pl.BlockSpec((e.shape[0], tm, n), lambda i, *_: (0, i, 0)) for e in extra],
            out_specs=[row] * 4),
        out_shape=[jax.ShapeDtypeStruct((r, n), f32)] * 4,
        compiler_params=_cp("parallel"),
    )(*scalars, w2, m2, v2, o2, *extra)
    return [o.reshape(shape) for o in outs]


def _adamw_minor_d(name, w, m, v, own_all, landed, dev, tm=512):
    nl, nh, d, f = w.shape
    wt, mt, vt = (jnp.swapaxes(t, 2, 3) for t in (w, m, v))
    r = nl * nh * d
    per = d // tm
    c1 = 1.0 - ADAM_B1 ** ADAM_STEP
    c2 = 1.0 - ADAM_B2 ** ADAM_STEP

    def body(dev_ref, w_ref, m_ref, v_ref, o_ref, l_ref, g_ref, d_ref, mn_ref, vn_ref):
        g = o_ref[...]
        for k in range(N_DEV - 1):
            g = g + l_ref[k].astype(f32)
        g = g.T
        mn = ADAM_B1 * m_ref[...] + (1.0 - ADAM_B1) * g
        vn = ADAM_B2 * v_ref[...] + (1.0 - ADAM_B2) * (g * g)
        g_ref[...] = g
        d_ref[...] = -ADAM_LR * ((mn / c1) / (jnp.sqrt(vn / c2) + ADAM_EPS) + ADAM_WD * w_ref[...])
        mn_ref[...] = mn
        vn_ref[...] = vn

    par = pl.BlockSpec((None, None, f, tm), lambda i, _: (i // (nh * per), (i // per) % nh, 0, i % per))
    outs = pl.pallas_call(
        body, name=name,
        grid_spec=pltpu.PrefetchScalarGridSpec(
            num_scalar_prefetch=1, grid=(r // tm,),
            in_specs=[par, par, par, pl.BlockSpec((None, tm, f), lambda i, dv: (dv[0], i, 0)),
                      pl.BlockSpec((N_DEV - 1, tm, f), lambda i, _: (0, i, 0))],
            out_specs=[par] * 4),
        out_shape=[jax.ShapeDtypeStruct(wt.shape, f32)] * 4,
        compiler_params=_cp("parallel"),
    )(dev.reshape(1).astype(jnp.int32), wt, mt, vt, own_all.reshape(N_DEV, r, f), landed.reshape(N_DEV - 1, r, f))
    return [jnp.swapaxes(o, 2, 3) for o in outs]


def _pack(arrays):
    flat = jnp.concatenate([a.reshape(-1).astype(f32) for a in arrays])
    pad = (-flat.shape[0]) % (128 * (512 if flat.shape[0] > 128 * 512 else 8))
    return jnp.pad(flat, (0, pad)).reshape(-1, 128)


def _unpack(packed, shapes):
    flat = packed.reshape(-1)
    out, off = [], 0
    for s in shapes:
        n = math.prod(s)
        out.append(flat[off:off + n].reshape(s))
        off += n
    return out


def _local_step(x, target, w, weights_of, send, last_small, on_loss, nb, s):
    cos, sin = _rope_tables(s)
    g = {}
    ffn_saved = {}
    ffn_bufs = [lax.empty((N_DEV, 2, 2) + shp, f32)
                for shp in ((D_MODEL, FF_SHARD), (D_MODEL, FF_SHARD), (FF_SHARD, D_MODEL))]

    def ffn(xin, l, h, wts):
        y, a, b = _ffn_fwd(f"ffn_fwd_{l}{h}", xin, w["ffn_g"][l][h], *wts)
        ffn_saved[(l, h)] = (xin, a, b, wts)
        return y

    def ffn_back(dy, l, h):
        xin, a, b, wts = ffn_saved[(l, h)]
        dx, dg, hb, dyh, u, da, db = _ffn_dx(f"ffn_dx_{l}{h}", dy, xin, w["ffn_g"][l][h], *wts, a, b)
        g[f"ffn_g_{l}{h}"] = dg
        if (l, h) != (0, 0):
            ffn_bufs[:2], (h1, h3) = _ffn_dw(f"ffn_dw_{l}{h}_w13", hb, [da, db], ffn_bufs[:2], l, h)
            ffn_bufs[2:], (h2,) = _ffn_dw(f"ffn_dw_{l}{h}_w2", u, [dyh], ffn_bufs[2:], l, h)
            return send(f"ffn_{l}{h}", {"ffn_w1": h1, "ffn_w3": h3, "ffn_w2": h2}, dx)
        hb = last_small(g, hb)
        ffn_bufs[:1], (half,) = _ffn_dw("ffn_dw_00_w1", hb, [da], ffn_bufs[:1], l, h)
        hb = send("ffn_00_w1", {"ffn_w1": half}, hb)
        ffn_bufs[1:2], (half,) = _ffn_dw("ffn_dw_00_w3", hb, [db], ffn_bufs[1:2], l, h)
        u = send("ffn_00_w3", {"ffn_w3": half}, u)
        ffn_bufs[2:], (half,) = _ffn_dw("ffn_dw_00_w2", u, [dyh], ffn_bufs[2:], l, h)
        return send("ffn_00_w2", {"ffn_w2": half}, dx)

    def slots(t):
        return t.reshape(N_DEV, D_MODEL // N_DEV, D_MODEL)

    x1 = ffn(x, 0, 0, weights_of(0, [])["ffn"])
    wg = weights_of(1, [x1])
    w_in, w_out = wg["w_in"], wg["w_out"]
    h0b = _norm_fwd("mix_norm_0", x1, w["mix_g"][0], bf16)
    proj = _mm("in_proj", h0b, w_in, "nn", tn=1536)[0]
    o_raw, rprev, mret = _ret_fwd(proj, cos, sin, w["ret_g"], nb, s)
    lru, lru_h = _lru_fwd(proj, w["conv_w"], w["conv_b"], w["lru_w_a"], w["lru_b_a"], w["lru_w_i"], w["lru_b_i"], w["lru_lam"], nb, s)
    merged = _ew("merge", lambda a, b: (jnp.concatenate([a, b], axis=1),), [mret, lru], [(D_MODEL, bf16)])[0]
    x2 = _mm("out_proj", merged, w_out, "nn", extras=[x1], epilogue=lambda acc, r: (acc + r,))[0]
    x3 = ffn(x2, 0, 1, weights_of(2, [x2])["ffn"])
    x4 = ffn(x3, 1, 0, weights_of(3, [x3])["ffn"])
    u = _norm_fwd("mix_norm_1", x4, w["mix_g"][1], f32)
    lbr, lbi, bbr, bbi = _s5_prep(w["s5_lr"], w["s5_li"], w["s5_ldt"], w["s5_bre"], w["s5_bim"])
    lbr_f, lbi_f = lbr.reshape(1, -1), lbi.reshape(1, -1)
    wbr, wbi = _blockdiag(bbr).astype(bf16), _blockdiag(bbi).astype(bf16)
    wcr, wci = _blockdiag(w["s5_cre"]).astype(bf16), _blockdiag(w["s5_cim"]).astype(bf16)
    ygb, ypre, h0s, hrs, his = _s5_fwd(u, lbr_f, lbi_f, wbr, wbi, wcr, wci, w["s5_d"], nb, s)
    wg = weights_of(4, [ygb])
    glu_a, glu_b = wg["glu_a"], wg["glu_b"]
    x5, gp, gq = _glu_fwd(ygb, glu_a, glu_b, x4)
    x6 = ffn(x5, 1, 1, weights_of(5, [x5])["ffn"])
    loss, dx6, g["final_g"] = _final_loss(x6, w["final_g"], target)
    dx6 = on_loss(loss, dx6)

    dx5 = ffn_back(dx6, 1, 1)

    def glu_bwd(d, p, q):
        sg = jax.nn.sigmoid(q)
        return d * sg, d * p * sg * (1.0 - sg)

    dp, dq = _ew("glu_bwd", glu_bwd, [dx5, gp, gq], [(D_MODEL, bf16), (D_MODEL, bf16)])
    dyg = _mm("glu_dy_a", dp, glu_a, "nt")[0]
    dyg = _mm("glu_dy_b", dq, glu_b, "nt", extras=[dyg], epilogue=lambda acc, r: (acc + r,))[0]
    g["glu_a"], ga_half = _mm_tn("glu_dw_a", ygb, dp)
    g["glu_b"], gb_half = _mm_tn("glu_dw_b", ygb, dq)
    dyg = send("glu", {"glu_a": slots(ga_half), "glu_b": slots(gb_half)}, dyg)
    du, dlr, dli, dwbr, dwbi, dwcr, dwci, g["s5_d"] = _s5_bwd(dyg, ypre, u, h0s, hrs, his, lbr_f, lbi_f, wbr, wbi, wcr, wci, w["s5_d"], nb, s)
    g["s5_cre"], g["s5_cim"] = _blockdiag_t(dwcr), _blockdiag_t(dwci)
    g["s5_lr"], g["s5_li"], g["s5_ldt"], g["s5_bre"], g["s5_bim"] = _s5_prep_bwd(
        w["s5_lr"], w["s5_li"], w["s5_ldt"], w["s5_bre"], w["s5_bim"],
        (dlr.reshape(S5_GROUPS, S5_STATE), dli.reshape(S5_GROUPS, S5_STATE), _blockdiag_t(dwbr), _blockdiag_t(dwbi)))
    dx4, g["mix_g_1"] = _norm_bwd("mix_norm_1_bwd", du, x4, w["mix_g"][1], dx5)
    dx3 = ffn_back(dx4, 1, 0)
    dx2 = ffn_back(dx3, 0, 1)
    dmerged = _mm("out_proj_dx", dx2, w_out, "nt")[0]
    g["w_out"], wo_half = _mm_tn("out_proj_dw", merged, dx2)
    dmerged = send("w_out", {"w_out": slots(wo_half)}, dmerged)
    dq_, dk_, dv_, dgate, g["ret_g"] = _ret_bwd(dmerged, o_raw, rprev, proj, cos, sin, w["ret_g"], nb, s)
    (dxl, dgl, g["conv_w"], g["conv_b"], g["lru_w_a"], g["lru_b_a"], g["lru_w_i"], g["lru_b_i"], g["lru_lam"]) = _lru_bwd(
        dmerged, lru_h, proj, w["conv_w"], w["conv_b"], w["lru_w_a"], w["lru_b_a"], w["lru_w_i"], w["lru_b_i"], w["lru_lam"],
        nb, s)
    dproj = _ew("dproj", lambda *p: (jnp.concatenate(p, axis=1),), [dq_, dk_, dv_, dgate, dxl, dgl], [(3072, bf16)])[0]
    dh0 = _mm("in_proj_dx", dproj, w_in, "nt")[0]
    g["w_in"], wi_half = _mm_tn("in_proj_dw", h0b, dproj)
    dh0 = send("w_in", {"w_in": jnp.transpose(wi_half.reshape(D_MODEL, N_DEV, IN_SHARD), (1, 0, 2))}, dh0)
    dx1, g["mix_g_0"] = _norm_bwd("mix_norm_0_bwd", dh0, x1, w["mix_g"][0], dx2)
    dx0 = ffn_back(dx1, 0, 0)
    g["ffn_w1"], g["ffn_w3"], g["ffn_w2"] = ffn_bufs
    return loss, dx0, g


_WEIGHTS = ["ffn_norm_g", "ffn_w1", "ffn_w3", "ffn_w2", "mix_norm_g", "w_in_even", "w_out_even", "ret_norm_g", "conv_w",
            "conv_b", "lru_w_a", "lru_b_a", "lru_w_i", "lru_b_i", "lru_lambda", "s5_lambda_re", "s5_lambda_im", "s5_log_dt",
            "s5_b_re", "s5_b_im", "s5_c_re", "s5_c_im", "s5_d", "glu_w_a", "glu_w_b", "final_norm_g"]
_BIG = ["ffn_w1", "ffn_w3", "ffn_w2", "w_in_even", "w_out_even", "glu_w_a", "glu_w_b"]
_SMALL_SHARDED = ["ffn_norm_g", "conv_w", "s5_d"]
_SMALL = [n for n in _WEIGHTS if n not in _BIG]
_MIDSIZE = ["lru_w_a", "lru_w_i", "s5_b_re", "s5_b_im", "s5_c_re", "s5_c_im"]


def kernel(x, ffn_norm_g, ffn_w1, ffn_w3, ffn_w2, mix_norm_g, w_in_even, w_out_even, ret_norm_g, conv_w, conv_b, lru_w_a, lru_b_a, lru_w_i, lru_b_i, lru_lambda, s5_lambda_re, s5_lambda_im, s5_log_dt, s5_b_re, s5_b_im, s5_c_re, s5_c_im, s5_d, glu_w_a, glu_w_b, final_norm_g, loss_target, m_ffn_norm_g, m_ffn_w1, m_ffn_w3, m_ffn_w2, m_mix_norm_g, m_w_in_even, m_w_out_even, m_ret_norm_g, m_conv_w, m_conv_b, m_lru_w_a, m_lru_b_a, m_lru_w_i, m_lru_b_i, m_lru_lambda, m_s5_lambda_re, m_s5_lambda_im, m_s5_log_dt, m_s5_b_re, m_s5_b_im, m_s5_c_re, m_s5_c_im, m_s5_d, m_glu_w_a, m_glu_w_b, m_final_norm_g, v_ffn_norm_g, v_ffn_w1, v_ffn_w3, v_ffn_w2, v_mix_norm_g, v_w_in_even, v_w_out_even, v_ret_norm_g, v_conv_w, v_conv_b, v_lru_w_a, v_lru_b_a, v_lru_w_i, v_lru_b_i, v_lru_lambda, v_s5_lambda_re, v_s5_lambda_im, v_s5_log_dt, v_s5_b_re, v_s5_b_im, v_s5_c_re, v_s5_c_im, v_s5_d, v_glu_w_a, v_glu_w_b, v_final_norm_g):
    a = dict(locals())
    nb, s, d = x.shape
    dev = 4 * lax.axis_index("x") + 2 * lax.axis_index("y") + lax.axis_index("c")

    def ffn_shards(l, h, pad):
        out = [jnp.swapaxes(ffn_w1[l, h], 0, 1).astype(bf16), jnp.swapaxes(ffn_w3[l, h], 0, 1).astype(bf16),
               ffn_w2[l, h].astype(bf16)]
        return [jnp.pad(t, ((0, FF_PAD - FF_SHARD), (0, 0))) for t in out] if pad else out

    first = _all_gather("ag_first", ffn_shards(0, 0, True) + [_pack([ffn_norm_g, conv_w, s5_d])])
    sm = first[3].reshape(N_DEV, -1)
    ffn_g_full = jnp.transpose(sm[:, :512].reshape(N_DEV, 2, 2, 128), (1, 2, 0, 3)).reshape(2, 2, D_MODEL)
    conv_w_full = jnp.transpose(sm[:, 512:768].reshape(N_DEV, 4, 64), (1, 0, 2)).reshape(4, LRU_WIDTH)
    s5_d_full = sm[:, 768:896].reshape(1, D_MODEL)

    ag_src = [None, [w_in_even[0].astype(bf16), w_out_even[0].astype(bf16)], ffn_shards(0, 1, False),
              ffn_shards(1, 0, False), [glu_w_a[0].astype(bf16), glu_w_b[0].astype(bf16)], ffn_shards(1, 1, False)]
    ag, token = [None], first[0]
    for k, grp in enumerate(ag_src):
        if grp is None:
            continue
        grp[0] = _tie(f"tie_ag_{k}", grp[0], [token])
        if grp[0].shape[0] == FF_SHARD:
            lands = [lax.dynamic_update_slice(jnp.zeros((N_DEV, FF_PAD, D_MODEL), bf16), t[None], (dev, 0, 0)) for t in grp]
            sfx = [(pl.ds(0, FF_SHARD),)] * len(grp)
        else:
            lands = [lax.dynamic_update_index_in_dim(lax.empty((N_DEV,) + t.shape, bf16), t, dev, 0) for t in grp]
            sfx = None
        ag.append(_xchg_start(f"ag_start_{k}", "gather", grp, lands, sfx))
        token = ag[-1]["token"]

    def weights_of(k, after):
        if k == 0:
            return {"ffn": [first[0], _tie("tie_ag_started", first[1], [h["token"] for h in ag[1:]]), first[2]]}
        got = _xchg_wait(f"ag_wait_{k}", ag[k], after)
        if k == 1:
            return {"w_in": jnp.transpose(got[0], (1, 0, 2)).reshape(D_MODEL, N_DEV * IN_SHARD),
                    "w_out": got[1].reshape(D_MODEL, D_MODEL)}
        if k == 4:
            return {"glu_a": got[0].reshape(D_MODEL, D_MODEL), "glu_b": got[1].reshape(D_MODEL, D_MODEL)}
        return {"ffn": got}

    ffn_lands = [lax.empty((N_DEV - 1, 2, 2) + shp, bf16)
                 for shp in ((D_MODEL, FF_SHARD), (D_MODEL, FF_SHARD), (FF_SHARD, D_MODEL))]
    rs = []

    ffn_names = ("ffn_w1", "ffn_w3", "ffn_w2")

    def send(group, arrays, carry):
        srcs = list(arrays.values())
        if group.startswith("ffn_"):
            which = [ffn_names.index(n) for n in arrays]
            sfx = [(int(group[4]), int(group[5]))] * len(which)
            h = _xchg_start("rs_start_" + group, "scatter", srcs, [ffn_lands[k] for k in which], sfx)
            for k, land in zip(which, h["lands"]):
                ffn_lands[k] = land
        else:
            h = _xchg_start("rs_start_" + group, "scatter", srcs,
                            [lax.empty((N_DEV - 1,) + t.shape[1:], bf16) for t in srcs])
        rs.append((group, list(arrays), h))
        return _tie("tie_" + group, carry, [h["token"]])

    w = {
        "ffn_g": [[ffn_g_full[l, h].reshape(1, D_MODEL) for h in range(2)] for l in range(2)],
        "mix_g": [mix_norm_g[0:1], mix_norm_g[1:2]],
        "ret_g": ret_norm_g, "conv_w": conv_w_full, "conv_b": conv_b,
        "lru_w_a": lru_w_a[0], "lru_b_a": lru_b_a, "lru_w_i": lru_w_i[0], "lru_b_i": lru_b_i, "lru_lam": lru_lambda,
        "s5_lr": s5_lambda_re[0], "s5_li": s5_lambda_im[0], "s5_ldt": s5_log_dt.reshape(S5_GROUPS, 1),
        "s5_bre": jnp.swapaxes(s5_b_re[0], 1, 2), "s5_bim": jnp.swapaxes(s5_b_im[0], 1, 2),
        "s5_cre": s5_c_re[0], "s5_cim": s5_c_im[0], "s5_d": s5_d_full,
        "final_g": final_norm_g.reshape(1, D_MODEL),
    }

    small_grads = {}

    def last_small(g, carry):
        part = _small_partials(g)
        mine = _pack([part[n] for n in _SMALL])
        land = lax.dynamic_update_index_in_dim(lax.empty((N_DEV,) + mine.shape, f32), mine, dev, 0)
        h = _xchg_start("ag_start_small_grads", "gather", [mine], [land])
        small_grads.update(h=h, shapes=[part[n].shape for n in _SMALL])
        return _tie("tie_small_grads", carry, [h["token"]])

    total_loss = []

    def on_loss(part, carry):
        total_loss.append(lax.psum(part[0, 0], ("x", "y", "c")))
        return _tie("tie_loss", carry, [jnp.broadcast_to(total_loss[0], (8, 128))])

    _, dx, g = _local_step(x.reshape(nb * s, d), loss_target.reshape(nb * s, d), w, weights_of, send, last_small,
                           on_loss, nb, s)
    loss = total_loss[0]
    (gath,) = _xchg_wait("ag_wait_small_grads", small_grads["h"], [dx])
    full = dict(zip(_SMALL, _unpack(_sum8("sum_small_grads", gath), small_grads["shapes"])))
    for n in _SMALL_SHARDED:
        width = a[n].shape[-1]
        full[n] = lax.dynamic_slice_in_dim(full[n], dev * width, width, axis=full[n].ndim - 1)
    res = {}
    for n in _MIDSIZE:
        if n.startswith("s5_b_"):
            swap = lambda t: jnp.swapaxes(t, 2, 3)
            res[n] = [swap(t) for t in _adamw("adamw_" + n, swap(a[n]), swap(a["m_" + n]), swap(a["v_" + n]), full[n])]
        else:
            res[n] = _adamw("adamw_" + n, a[n], a["m_" + n], a["v_" + n], full[n])
    tiny = [n for n in _SMALL if n not in _MIDSIZE]
    shapes = [a[n].shape for n in tiny]
    packed = _adamw("adamw_small", _pack([a[n] for n in tiny]), _pack([a["m_" + n] for n in tiny]),
                    _pack([a["v_" + n] for n in tiny]), _pack([full[n] for n in tiny]))
    res.update({n: vals for n, vals in zip(tiny, zip(*[_unpack(p, shapes) for p in packed]))})
    return _finish(a, g, dx, loss, res, packed, rs, ffn_lands, dev, nb, s, d)


def _small_partials(g):
    return {
        "ffn_norm_g": jnp.stack([jnp.stack([g[f"ffn_g_{l}{h}"][0] for h in range(2)]) for l in range(2)]),
        "mix_norm_g": jnp.concatenate([g["mix_g_0"], g["mix_g_1"]], axis=0),
        "ret_norm_g": g["ret_g"], "conv_w": g["conv_w"][None], "conv_b": g["conv_b"],
        "lru_w_a": g["lru_w_a"][None], "lru_b_a": g["lru_b_a"], "lru_w_i": g["lru_w_i"][None], "lru_b_i": g["lru_b_i"],
        "lru_lambda": g["lru_lam"], "s5_lambda_re": g["s5_lr"][None], "s5_lambda_im": g["s5_li"][None],
        "s5_log_dt": g["s5_ldt"].reshape(1, S5_GROUPS),
        "s5_b_re": g["s5_bre"][None], "s5_b_im": g["s5_bim"][None],
        "s5_c_re": g["s5_cre"][None], "s5_c_im": g["s5_cim"][None], "s5_d": g["s5_d"], "final_norm_g": g["final_g"][0],
    }


def _finish(a, g, dx, loss, res, packed, rs, ffn_lands, dev, nb, s, d):
    landed = {}
    for group, names, h in rs:
        if not group.startswith("ffn_"):
            landed.update(zip(names, _xchg_wait("rs_wait_" + group, h, [dx])))
    kinds = {"ffn_w1": "lead", "ffn_w3": "lead", "ffn_w2": "lead", "w_in": "cols", "w_out": "rows", "glu_a": "rows",
             "glu_b": "rows"}

    def update(n, short):
        if short in ("ffn_w1", "ffn_w3"):
            res[n] = _adamw_minor_d("adamw_" + n, a[n], a["m_" + n], a["v_" + n], g[short], landed[short], dev)
        else:
            res[n] = _adamw("adamw_" + n, a[n], a["m_" + n], a["v_" + n], g[short],
                            landed[short].reshape((N_DEV - 1,) + a[n].shape), slot=(dev, kinds[short]))

    for n, short in zip(_BIG[3:], ("w_in", "w_out", "glu_a", "glu_b")):
        update(n, short)
    after = [dx, packed[0]] + [res[n][0] for n in _BIG[3:] + _MIDSIZE]
    ffn_names = ("ffn_w1", "ffn_w3", "ffn_w2")
    for group, names, h in rs:
        if group.startswith("ffn_") and len(names) == 3:
            ffn_lands[:] = _xchg_wait("rs_wait_" + group, h, after, ffn_lands)
    for k, n in enumerate(ffn_names):
        for group, names, h in rs:
            if group.startswith("ffn_") and names == [n]:
                (ffn_lands[k],) = _xchg_wait("rs_wait_" + group, h, after, [ffn_lands[k]])
        landed[n] = ffn_lands[k]
        update(n, n)
        after = after + [res[n][0]]

    out = [loss, dx.reshape(nb, s, d)]
    for k in range(4):
        out += [res[n][k] for n in _WEIGHTS]
    return tuple(out)
```

```python
import math

import numpy as np
import jax
import jax.numpy as jnp
from jax import lax
from jax.experimental import pallas as pl
from jax.experimental.pallas import tpu as pltpu

f32 = jnp.float32
bf16 = jnp.bfloat16

D_MODEL = 1024
N_DEV = 8
EPS = 1e-6
RET_HEADS = 4
HEAD_DIM = 128
RET_WIDTH = 512
RET_CHUNK = 128
ROPE_BASE = 10000.0
LRU_WIDTH = 512
LRU_BLOCKS = 4
LRU_C = 8.0
S5_GROUP = 16
S5_GROUPS = 64
S5_STATE = 64
S5_CHUNK = 1024
S5_BLOCKS = 8
S5_BLOCK_STATES = 512
SUBLANES = 8
D_FF = 2816
FF_SHARD = D_FF // N_DEV
FF_PAD = 384
IN_SHARD = 3072 // N_DEV
ADAM_LR = 0.001
ADAM_B1 = 0.9
ADAM_B2 = 0.999
ADAM_EPS = 1e-08
ADAM_WD = 0.01
ADAM_STEP = 10

VMEM_LIMIT = 56 * 1024 * 1024
VMEM_SPEC = pl.BlockSpec(memory_space=pltpu.VMEM)
ANY_SPEC = pl.BlockSpec(memory_space=pl.ANY)
HBM_SPEC = pl.BlockSpec(memory_space=pltpu.HBM)
SEM_SPEC = pl.BlockSpec(memory_space=pltpu.SEMAPHORE)
SIDE_EFFECT = pltpu.SideEffectType.DATAFLOW_SIDE_EFFECTING
MESH = pl.DeviceIdType.MESH


def _cp(*sem):
    return pltpu.CompilerParams(dimension_semantics=sem, vmem_limit_bytes=VMEM_LIMIT)


def _nn(a, b):
    return jnp.dot(a, b, preferred_element_type=f32)


def _nt(a, b):
    return lax.dot_general(a, b, (((1,), (1,)), ((), ())), preferred_element_type=f32)


def _tn(a, b):
    return lax.dot_general(a, b, (((0,), (0,)), ((), ())), preferred_element_type=f32)


def _rms_fwd(x, g):
    r = lax.rsqrt(jnp.mean(x * x, axis=-1, keepdims=True) + EPS)
    xn = x * r
    return xn * g, xn, r


def _rms_bwd(dh, xn, r, g):
    dxn = dh * g
    dx = r * (dxn - xn * jnp.mean(dxn * xn, axis=-1, keepdims=True))
    dg = jnp.sum(dh * xn, axis=0, keepdims=True)
    return dx, dg


def _shift_dn(v, d, row, fill=0.0):
    return jnp.where(row >= d, pltpu.roll(v, d, 0), fill)


def _shift_up(v, d, row, fill=0.0):
    n = v.shape[0]
    return jnp.where(row < n - d, pltpu.roll(v, n - d, 0), fill)


def _ew(name, fn, ins, outs, tm=512):
    t = ins[0].shape[0]
    n_in = len(ins)

    def body(*refs):
        res = fn(*[r[...] for r in refs[:n_in]])
        for o, v in zip(refs[n_in:], res):
            o[...] = v.astype(o.dtype)

    return pl.pallas_call(
        body, name=name, grid=(t // tm,),
        in_specs=[pl.BlockSpec((tm, a.shape[1]), lambda i: (i, 0)) for a in ins],
        out_specs=[pl.BlockSpec((tm, n), lambda i: (i, 0)) for n, _ in outs],
        out_shape=[jax.ShapeDtypeStruct((t, n), dt) for n, dt in outs],
        compiler_params=_cp("parallel"),
    )(*ins)


def _mm(name, x, w, kind, extras=(), epilogue=None, outs=None, tm=512, tn=1024):
    t = x.shape[0]
    n = w.shape[1] if kind == "nn" else w.shape[0]
    tn = min(tn, n)
    outs = outs or [f32]
    n_ex = len(extras)

    def body(x_ref, w_ref, *refs):
        xb = x_ref[...].astype(bf16)
        acc = _nn(xb, w_ref[...]) if kind == "nn" else _nt(xb, w_ref[...])
        res = epilogue(acc, *[r[...] for r in refs[:n_ex]]) if epilogue else (acc,)
        for o, v in zip(refs[n_ex:], res):
            o[...] = v.astype(o.dtype)

    w_spec = (pl.BlockSpec((w.shape[0], tn), lambda i, j: (0, j)) if kind == "nn"
              else pl.BlockSpec((tn, w.shape[1]), lambda i, j: (j, 0)))
    tile = pl.BlockSpec((tm, tn), lambda i, j: (i, j))
    return pl.pallas_call(
        body, name=name, grid=(t // tm, n // tn),
        in_specs=[pl.BlockSpec((tm, x.shape[1]), lambda i, j: (i, 0)), w_spec] + [tile] * n_ex,
        out_specs=[tile] * len(outs),
        out_shape=[jax.ShapeDtypeStruct((t, n), dt) for dt in outs],
        compiler_params=_cp("parallel", "parallel"),
    )(x, w, *extras)


def _mm_tn(name, x, y, tk=1024, tn=1024, tt=1024):
    t, k = x.shape
    n = y.shape[1]
    tk, tn, tt = min(tk, k), min(tn, n), min(tt, t)

    def body(x_ref, y_ref, o_ref, ob_ref):
        @pl.when(pl.program_id(2) == 0)
        def _():
            o_ref[...] = jnp.zeros_like(o_ref)
        o_ref[...] += _tn(x_ref[...].astype(bf16), y_ref[...].astype(bf16))

        @pl.when(pl.program_id(2) == pl.num_programs(2) - 1)
        def _():
            ob_ref[...] = o_ref[...].astype(bf16)

    out = pl.BlockSpec((tk, tn), lambda i, j, s: (i, j))
    return pl.pallas_call(
        body, name=name, grid=(k // tk, n // tn, t // tt),
        in_specs=[pl.BlockSpec((tt, tk), lambda i, j, s: (s, i)), pl.BlockSpec((tt, tn), lambda i, j, s: (s, j))],
        out_specs=[out, out],
        out_shape=[jax.ShapeDtypeStruct((k, n), f32), jax.ShapeDtypeStruct((k, n), bf16)],
        compiler_params=_cp("parallel", "parallel", "arbitrary"),
    )(x, y)


def _norm_fwd(name, x, g, dtype, tm=512):
    t, d = x.shape

    def body(x_ref, g_ref, h_ref):
        h_ref[...] = _rms_fwd(x_ref[...], g_ref[...])[0].astype(dtype)

    row = pl.BlockSpec((tm, d), lambda i: (i, 0))
    return pl.pallas_call(
        body, name=name, grid=(t // tm,),
        in_specs=[row, pl.BlockSpec((1, d), lambda i: (0, 0))],
        out_specs=row, out_shape=jax.ShapeDtypeStruct((t, d), dtype),
        compiler_params=_cp("parallel"),
    )(x, g)


def _norm_bwd(name, dh, x, g, dres, tm=512):
    t, d = x.shape

    def body(dh_ref, x_ref, g_ref, dres_ref, dx_ref, dg_ref):
        gv = g_ref[...]
        _, xn, r = _rms_fwd(x_ref[...], gv)
        dx, dg = _rms_bwd(dh_ref[...], xn, r, gv)
        dx_ref[...] = dres_ref[...] + dx

        @pl.when(pl.program_id(0) == 0)
        def _():
            dg_ref[...] = jnp.zeros_like(dg_ref)
        dg_ref[...] += dg

    row = pl.BlockSpec((tm, d), lambda i: (i, 0))
    vec = pl.BlockSpec((1, d), lambda i: (0, 0))
    return pl.pallas_call(
        body, name=name, grid=(t // tm,),
        in_specs=[row, row, vec, row],
        out_specs=[row, vec],
        out_shape=[jax.ShapeDtypeStruct((t, d), f32), jax.ShapeDtypeStruct((1, d), f32)],
        compiler_params=_cp("arbitrary"),
    )(dh, x, g, dres)


def _final_loss(x, g, target, tm=512):
    t, d = x.shape

    def body(x_ref, g_ref, t_ref, loss_ref, dx_ref, dg_ref):
        gv = g_ref[...]
        y, xn, r = _rms_fwd(x_ref[...], gv)
        err = y - t_ref[...]
        dy = err * (1.0 / d)
        dx, dg = _rms_bwd(dy, xn, r, gv)
        dx_ref[...] = dx

        @pl.when(pl.program_id(0) == 0)
        def _():
            dg_ref[...] = jnp.zeros_like(dg_ref)
            loss_ref[...] = jnp.zeros_like(loss_ref)
        dg_ref[...] += dg
        loss_ref[...] += jnp.full((1, 128), 0.5 / d, f32) * jnp.sum(err * err)

    row = pl.BlockSpec((tm, d), lambda i: (i, 0))
    vec = pl.BlockSpec((1, d), lambda i: (0, 0))
    return pl.pallas_call(
        body, name="final_loss", grid=(t // tm,),
        in_specs=[row, vec, row],
        out_specs=[pl.BlockSpec((1, 128), lambda i: (0, 0)), row, vec],
        out_shape=[jax.ShapeDtypeStruct((1, 128), f32), jax.ShapeDtypeStruct((t, d), f32),
                   jax.ShapeDtypeStruct((1, d), f32)],
        compiler_params=_cp("arbitrary"),
    )(x, g, target)


def _load_ffn_weights(hbm_refs, vmem_refs, sems):
    @pl.when(pl.program_id(0) == 0)
    def _():
        copies = []
        for k, (src, dst) in enumerate(zip(hbm_refs, vmem_refs)):
            for j in range(N_DEV):
                half = pl.ds((j % 2) * FF_PAD, FF_PAD)
                window = dst.at[j // 2, half, :]
                copies.append(pltpu.make_async_copy(src.at[j], window, sems.at[k * N_DEV + j]))
        for cp in copies:
            cp.start()
        for cp in copies:
            cp.wait()


def _ffn_weight_scratch(nj, d, ff):
    return [pltpu.VMEM((nj, ff, d), bf16), pltpu.VMEM((nj, ff, d), bf16), pltpu.VMEM((nj, ff, d), bf16),
            pltpu.SemaphoreType.DMA((3 * N_DEV,))]


def _ffn_fwd(name, x, g, w1, w3, w2, tm=512):
    t, d = x.shape
    nj, ff = N_DEV // 2, 2 * FF_PAD

    def body(x_ref, g_ref, w1_hbm, w3_hbm, w2_hbm, y_ref, a_ref, b_ref, w1_ref, w3_ref, w2_ref, sems):
        _load_ffn_weights((w1_hbm, w3_hbm, w2_hbm), (w1_ref, w3_ref, w2_ref), sems)
        xv = x_ref[...]
        h, _, _ = _rms_fwd(xv, g_ref[...])
        hb = h.astype(bf16)
        acc = jnp.zeros((tm, d), f32)
        for j in range(nj):
            a = _nt(hb, w1_ref[j])
            b = _nt(hb, w3_ref[j])
            a_ref[j] = a.astype(bf16)
            b_ref[j] = b.astype(bf16)
            u = (a * jax.nn.sigmoid(a) * b).astype(bf16)
            acc = acc + _nn(u, w2_ref[j])
        y_ref[...] = xv + 0.5 * acc

    row = pl.BlockSpec((tm, d), lambda i: (i, 0))
    mid = pl.BlockSpec((nj, tm, ff), lambda i: (0, i, 0))
    return pl.pallas_call(
        body, name=name, grid=(t // tm,),
        in_specs=[row, pl.BlockSpec((1, d), lambda i: (0, 0)), ANY_SPEC, ANY_SPEC, ANY_SPEC],
        out_specs=[row, mid, mid],
        out_shape=[jax.ShapeDtypeStruct((t, d), f32), jax.ShapeDtypeStruct((nj, t, ff), bf16),
                   jax.ShapeDtypeStruct((nj, t, ff), bf16)],
        scratch_shapes=_ffn_weight_scratch(nj, d, ff),
        compiler_params=_cp("arbitrary"),
    )(x, g, w1, w3, w2)


def _ffn_dx(name, dy, x, g, w1, w3, w2, a, b, tm=256):
    t, d = x.shape
    nj, ff = N_DEV // 2, 2 * FF_PAD

    def body(dy_ref, x_ref, g_ref, w1_hbm, w3_hbm, w2_hbm, a_ref, b_ref,
             dx_ref, dg_ref, hbt_ref, dyh_ref, ut_ref, da_ref, db_ref, w1_ref, w3_ref, w2_ref, sems):
        _load_ffn_weights((w1_hbm, w3_hbm, w2_hbm), (w1_ref, w3_ref, w2_ref), sems)
        gv = g_ref[...]
        h, xn, r = _rms_fwd(x_ref[...], gv)
        hbt_ref[...] = h.astype(bf16).T
        dyv = dy_ref[...]
        dyh = (0.5 * dyv).astype(bf16)
        dyh_ref[...] = dyh
        dh = jnp.zeros((tm, d), f32)
        dus = [_nt(dyh, w2_ref[j]) for j in range(nj)]
        for j in range(nj):
            av = a_ref[j].astype(f32)
            bv = b_ref[j].astype(f32)
            s = jax.nn.sigmoid(av)
            silu = av * s
            ut_ref[j] = (silu * bv).astype(bf16).T
            du = dus[j]
            dab = (du * bv * (s * (1.0 + av * (1.0 - s)))).astype(bf16)
            dbb = (du * silu).astype(bf16)
            da_ref[j] = dab
            db_ref[j] = dbb
            dh = dh + _nn(dab, w1_ref[j]) + _nn(dbb, w3_ref[j])
        dx, dg = _rms_bwd(dh, xn, r, gv)
        dx_ref[...] = dyv + dx

        @pl.when(pl.program_id(0) == 0)
        def _():
            dg_ref[...] = jnp.zeros_like(dg_ref)
        dg_ref[...] += dg

    row = pl.BlockSpec((tm, d), lambda i: (i, 0))
    vec = pl.BlockSpec((1, d), lambda i: (0, 0))
    mid = pl.BlockSpec((nj, tm, ff), lambda i: (0, i, 0))
    mid_shape = jax.ShapeDtypeStruct((nj, t, ff), bf16)
    return pl.pallas_call(
        body, name=name, grid=(t // tm,),
        in_specs=[row, row, vec, ANY_SPEC, ANY_SPEC, ANY_SPEC, mid, mid],
        out_specs=[row, vec, pl.BlockSpec((d, tm), lambda i: (0, i)), row,
                   pl.BlockSpec((nj, ff, tm), lambda i: (0, 0, i)), mid, mid],
        out_shape=[jax.ShapeDtypeStruct((t, d), f32), jax.ShapeDtypeStruct((1, d), f32),
                   jax.ShapeDtypeStruct((d, t), bf16), jax.ShapeDtypeStruct((t, d), bf16),
                   jax.ShapeDtypeStruct((nj, ff, t), bf16), mid_shape, mid_shape],
        scratch_shapes=_ffn_weight_scratch(nj, d, ff),
        compiler_params=_cp("arbitrary"),
    )(dy, x, g, w1, w3, w2, a, b)


def _ffn_dw(name, xt, ys, bufs, l, h, tt=2048):
    n = len(ys)
    t = ys[0].shape[-2]
    tt = min(tt, t)
    cut_cols = xt.ndim == 2

    def body(x_ref, *refs):
        y_refs, outs, accs = refs[:n], refs[2 * n:4 * n], refs[4 * n:]
        s = pl.program_id(1)
        xv = x_ref[0] if xt.ndim == 3 else x_ref[...]
        for k in range(n):
            prod = _nn(xv, y_refs[k][0] if ys[k].ndim == 3 else y_refs[k][...])

            @pl.when(s == 0)
            def _():
                accs[k][...] = prod

            @pl.when(s > 0)
            def _():
                accs[k][...] += prod

        @pl.when(s == pl.num_programs(1) - 1)
        def _():
            for k in range(n):
                total = accs[k][...]
                for e in range(2):
                    lo = e * FF_PAD
                    part = total[:, lo:lo + FF_SHARD] if cut_cols else total[lo:lo + FF_SHARD, :]
                    outs[k][e] = part
                    outs[n + k][e] = part.astype(bf16)

    x_spec = (pl.BlockSpec((1, xt.shape[1], tt), lambda p, s: (p, 0, s)) if xt.ndim == 3
              else pl.BlockSpec((xt.shape[0], tt), lambda p, s: (0, s)))
    y_specs = [pl.BlockSpec((1, tt, y.shape[2]), lambda p, s: (p, s, 0)) if y.ndim == 3
               else pl.BlockSpec((tt, y.shape[1]), lambda p, s: (s, 0)) for y in ys]
    dims = [b.shape[-2:] for b in bufs]
    outs = pl.pallas_call(
        body, name=name, grid=(N_DEV // 2, t // tt),
        in_specs=[x_spec] + y_specs + [ANY_SPEC] * n,
        out_specs=[pl.BlockSpec((2, None, None, k_, n_), lambda p, s: (p, l, h, 0, 0)) for k_, n_ in dims]
        + [pl.BlockSpec((2, k_, n_), lambda p, s: (p, 0, 0)) for k_, n_ in dims],
        out_shape=[jax.ShapeDtypeStruct(b.shape, b.dtype) for b in bufs]
        + [jax.ShapeDtypeStruct((N_DEV, k_, n_), bf16) for k_, n_ in dims],
        input_output_aliases={1 + n + k: k for k in range(n)},
        scratch_shapes=[pltpu.VMEM((xt.shape[-2], y.shape[-1]), f32) for y in ys],
        compiler_params=_cp("parallel", "arbitrary"),
    )(xt, *ys, *bufs)
    return outs[:n], outs[n:]


_LOG_GAMMA = [float(np.log1p(-np.float32(2.0) ** np.float32(-5.0 - h))) for h in range(RET_HEADS)]


def _ret_consts(h):
    lg = jnp.where(h == 0, _LOG_GAMMA[0], jnp.where(h == 1, _LOG_GAMMA[1],
                   jnp.where(h == 2, _LOG_GAMMA[2], _LOG_GAMMA[3]))).astype(f32)
    c = RET_CHUNK
    r = lax.broadcasted_iota(jnp.int32, (c, c), 0)
    cc = lax.broadcasted_iota(jnp.int32, (c, c), 1)
    decay = jnp.where(r >= cc, jnp.exp(lg * jnp.maximum((r - cc).astype(f32), 0.0)), 0.0)
    pos = lax.broadcasted_iota(jnp.int32, (c, 1), 0).astype(f32)
    kd = jnp.exp(lg * (c - 1.0 - pos))
    qd = jnp.exp(lg * (pos + 1.0))
    gc = jnp.exp(lg * c)
    return decay, kd, qd, gc


def _rope(x, cos, sin):
    return x * cos + pltpu.roll(x, HEAD_DIM // 2, 1) * sin


def _rope_t(g, cos, sin):
    return g * cos + pltpu.roll(g * sin, HEAD_DIM // 2, 1)


def _rope_tables(s):
    half = HEAD_DIM // 2
    inv = ROPE_BASE ** (-jnp.arange(half, dtype=f32) / half)
    ang = jnp.arange(s, dtype=f32)[:, None] * inv[None, :]
    cos, sin = jnp.cos(ang), jnp.sin(ang)
    return jnp.concatenate([cos, cos], axis=1), jnp.concatenate([-sin, sin], axis=1)


def _head_ln(o):
    mu = jnp.mean(o, axis=-1, keepdims=True)
    oc = o - mu
    rs = lax.rsqrt(jnp.mean(oc * oc, axis=-1, keepdims=True) + EPS)
    return oc * rs, rs


def _ret_fwd(proj, cos, sin, ret_g, nb, s):
    c = RET_CHUNK
    nc = s // c
    t = nb * s
    scale = HEAD_DIM ** -0.5

    def body(q_ref, k_ref, v_ref, gate_ref, cos_ref, sin_ref, g_ref, o_ref, rprev_ref, m_ref):
        decay, kd, qd, gc = _ret_consts(pl.program_id(0))
        gv = g_ref[...]

        def chunk(b, n, rv):
            rows = pl.ds(pl.multiple_of(b * s + n * c, c), c)
            pos = pl.ds(pl.multiple_of(n * c, c), c)
            cs, sn = cos_ref[pos, :], sin_ref[pos, :]
            q = _rope(q_ref[rows, :], cs, sn)
            k = _rope(k_ref[rows, :], cs, sn) * scale
            vb = v_ref[rows, :].astype(bf16)
            sc = _nt(q.astype(bf16), k.astype(bf16)) * decay
            rprev_ref[b, n] = rv
            o = _nn(sc.astype(bf16), vb) + _nn((q * qd).astype(bf16), rv.astype(bf16))
            o_ref[rows, :] = o
            y, _ = _head_ln(o)
            gate = gate_ref[rows, :]
            m_ref[rows, :] = y * gv * (gate * jax.nn.sigmoid(gate))
            return rv * gc + _tn((k * kd).astype(bf16), vb)

        def step(n, carry):
            return tuple(chunk(b, n, carry[b]) for b in range(nb))

        lax.fori_loop(0, nc, step, (jnp.zeros((HEAD_DIM, HEAD_DIM), f32),) * nb)

    def col(off):
        return pl.BlockSpec((t, HEAD_DIM), lambda h: (0, off + h))

    tab = pl.BlockSpec((s, HEAD_DIM), lambda h: (0, 0))
    return pl.pallas_call(
        body, name="ret_fwd", grid=(RET_HEADS,),
        in_specs=[col(0), col(4), col(8), col(12), tab, tab, pl.BlockSpec((1, HEAD_DIM), lambda h: (0, h))],
        out_specs=[col(0), pl.BlockSpec((nb, None, nc, HEAD_DIM, HEAD_DIM), lambda h: (0, h, 0, 0, 0)), col(0)],
        out_shape=[jax.ShapeDtypeStruct((t, RET_WIDTH), f32),
                   jax.ShapeDtypeStruct((nb, RET_HEADS, nc, HEAD_DIM, HEAD_DIM), f32),
                   jax.ShapeDtypeStruct((t, RET_WIDTH), f32)],
        compiler_params=_cp("parallel"),
    )(proj, proj, proj, proj, cos, sin, ret_g)


def _ret_bwd(dmerged, o_raw, rprev, proj, cos, sin, ret_g, nb, s):
    c = RET_CHUNK
    nc = s // c
    t = nb * s
    scale = HEAD_DIM ** -0.5

    def body(dm_ref, o_ref, rprev_ref, q_ref, k_ref, v_ref, gate_ref, cos_ref, sin_ref, g_ref,
             dq_ref, dk_ref, dv_ref, dgate_ref, dg_ref):
        decay, kd, qd, gc = _ret_consts(pl.program_id(0))
        gv = g_ref[...]

        def chunk(b, n, drn, dg):
            rows = pl.ds(pl.multiple_of(b * s + n * c, c), c)
            pos = pl.ds(pl.multiple_of(n * c, c), c)
            cs, sn = cos_ref[pos, :], sin_ref[pos, :]
            q = _rope(q_ref[rows, :], cs, sn)
            k = _rope(k_ref[rows, :], cs, sn) * scale
            qb, kb = q.astype(bf16), k.astype(bf16)
            vb = v_ref[rows, :].astype(bf16)
            sc = _nt(qb, kb) * decay
            y, rs = _head_ln(o_ref[rows, :])
            gate = gate_ref[rows, :]
            sg = jax.nn.sigmoid(gate)
            silu = gate * sg
            dm = dm_ref[rows, :]
            dgate_ref[rows, :] = dm * y * gv * (sg * (1.0 + gate * (1.0 - sg)))
            dyl = dm * gv * silu
            dg = dg + jnp.sum(dm * y * silu, axis=0, keepdims=True)
            do = rs * (dyl - jnp.mean(dyl, axis=-1, keepdims=True) - y * jnp.mean(dyl * y, axis=-1, keepdims=True))
            dob = do.astype(bf16)
            rv = rprev_ref[b, n]
            drb = drn.astype(bf16)
            ds = (_nt(dob, vb) * decay).astype(bf16)
            kdb = (k * kd).astype(bf16)
            qdb = (q * qd).astype(bf16)
            dq_r = _nn(ds, kb) + _nt(dob, rv.astype(bf16)) * qd
            dk_r = _tn(ds, qb) + _nt(vb, drb) * kd
            dv_ref[rows, :] = _tn(sc.astype(bf16), dob) + _nn(kdb, drb)
            dq_ref[rows, :] = _rope_t(dq_r, cs, sn)
            dk_ref[rows, :] = _rope_t(dk_r * scale, cs, sn)
            return drn * gc + _tn(qdb, dob), dg

        def step(i, carry):
            out = [chunk(b, nc - 1 - i, *carry[b]) for b in range(nb)]
            return tuple(out)

        zero = (jnp.zeros((HEAD_DIM, HEAD_DIM), f32), jnp.zeros((1, HEAD_DIM), f32))
        done = lax.fori_loop(0, nc, step, (zero,) * nb)
        dg_ref[...] = sum(dg for _, dg in done)

    def col(off):
        return pl.BlockSpec((t, HEAD_DIM), lambda h: (0, off + h))

    tab = pl.BlockSpec((s, HEAD_DIM), lambda h: (0, 0))
    gsp = pl.BlockSpec((1, HEAD_DIM), lambda h: (0, h))
    out_t = jax.ShapeDtypeStruct((t, RET_WIDTH), f32)
    return pl.pallas_call(
        body, name="ret_bwd", grid=(RET_HEADS,),
        in_specs=[col(0), col(0), pl.BlockSpec((nb, None, nc, HEAD_DIM, HEAD_DIM), lambda h: (0, h, 0, 0, 0)),
                  col(0), col(4), col(8), col(12), tab, tab, gsp],
        out_specs=[col(0), col(0), col(0), col(0), gsp],
        out_shape=[out_t, out_t, out_t, out_t, jax.ShapeDtypeStruct((1, RET_WIDTH), f32)],
        compiler_params=_cp("parallel"),
    )(dmerged, o_raw, rprev, proj, proj, proj, proj, cos, sin, ret_g)


def _neg_expm1(z):
    series = -(z * (1.0 + z * (0.5 + z * (1.0 / 6.0 + z * (1.0 / 24.0)))))
    return jnp.where(z > -0.01, series, 1.0 - jnp.exp(z))


def _lru_gates(xc, pa, pi, lam):
    r = jax.nn.sigmoid(pa)
    i = jax.nn.sigmoid(pi)
    log_a = -LRU_C * r * jax.nn.softplus(-lam)
    a = jnp.exp(log_a)
    bx = jnp.sqrt(_neg_expm1(2.0 * log_a)) * i * xc
    return a, bx


def _scan_rows(a, b, row, up):
    sub = row[:SUBLANES] & (SUBLANES - 1)
    groups = list(range(a.shape[0] // SUBLANES))
    out = [None] * len(groups)
    edge = slice(0, 1) if up else slice(SUBLANES - 1, SUBLANES)
    carry = jnp.zeros((1, a.shape[1]), f32)
    for g in (reversed(groups) if up else groups):
        rows = slice(g * SUBLANES, (g + 1) * SUBLANES)
        xa, xb = a[rows], b[rows]
        d = 1
        while d < SUBLANES:
            keep = (sub < SUBLANES - d) if up else (sub >= d)
            shift = SUBLANES - d if up else d
            xb = xa * jnp.where(keep, pltpu.roll(xb, shift, 0), 0.0) + xb
            xa = xa * jnp.where(keep, pltpu.roll(xa, shift, 0), 1.0)
            d *= 2
        out[g] = xb + xa * carry
        carry = out[g][edge]
    return jnp.concatenate(out, axis=0)


def _scan_fwd(a, b, row):
    return _scan_rows(a, b, row, False)


def _scan_bwd(c, b, row):
    return _scan_rows(c, b, row, True)


def _conv_fwd(x, cw, cb, row):
    return (cb + cw[3:4] * x + cw[2:3] * _shift_dn(x, 1, row) + cw[1:2] * _shift_dn(x, 2, row)
            + cw[0:1] * _shift_dn(x, 3, row))


def _lru_specs(s, order):
    def im(f):
        return (lambda b, g: f(b, g)) if order == "bg" else (lambda g, b: f(b, g))
    seq = lambda off: pl.BlockSpec((s, 128), im(lambda b, g: (b, off + g)))
    vec = pl.BlockSpec((1, 128), im(lambda b, g: (0, g)))
    cw = pl.BlockSpec((4, 128), im(lambda b, g: (0, g)))
    mat = pl.BlockSpec((1, 128, 128), im(lambda b, g: (g, 0, 0)))
    return seq, vec, cw, mat


def _lru_fwd(proj, conv_w, conv_b, w_a, b_a, w_i, b_i, lam, nb, s):
    def body(x_ref, gt_ref, cw_ref, cb_ref, wa_ref, ba_ref, wi_ref, bi_ref, lam_ref, out_ref, h_ref):
        row = lax.broadcasted_iota(jnp.int32, (s, 128), 0)
        xc = _conv_fwd(x_ref[...], cw_ref[...], cb_ref[...], row)
        xcb = xc.astype(bf16)
        pa = _nn(xcb, wa_ref[0].astype(bf16)) + ba_ref[...]
        pi = _nn(xcb, wi_ref[0].astype(bf16)) + bi_ref[...]
        a, bx = _lru_gates(xc, pa, pi, lam_ref[...])
        h = _scan_fwd(a, bx, row)
        h_ref[...] = h
        out_ref[...] = h * jax.nn.gelu(gt_ref[...])

    seq, vec, cw, mat = _lru_specs(s, "bg")
    out = jax.ShapeDtypeStruct((nb * s, LRU_WIDTH), f32)
    return pl.pallas_call(
        body, name="lru_fwd", grid=(nb, LRU_BLOCKS),
        in_specs=[seq(16), seq(20), cw, vec, mat, vec, mat, vec, vec],
        out_specs=[seq(0), seq(0)], out_shape=[out, out],
        compiler_params=_cp("parallel", "parallel"),
    )(proj, proj, conv_w, conv_b, w_a, b_a, w_i, b_i, lam)


def _lru_bwd(dmerged, states, proj, conv_w, conv_b, w_a, b_a, w_i, b_i, lam, nb, s):
    def body(dout_ref, h_ref, x_ref, gt_ref, cw_ref, cb_ref, wa_ref, ba_ref, wi_ref, bi_ref, lam_ref,
             dx_ref, dgt_ref, dcw_ref, dcb_ref, dwa_ref, dba_ref, dwi_ref, dbi_ref, dlam_ref):
        row = lax.broadcasted_iota(jnp.int32, (s, 128), 0)
        x = x_ref[...]
        cwv = cw_ref[...]
        xc = _conv_fwd(x, cwv, cb_ref[...], row)
        xcb = xc.astype(bf16)
        wab, wib = wa_ref[0].astype(bf16), wi_ref[0].astype(bf16)
        pa = _nn(xcb, wab) + ba_ref[...]
        pi = _nn(xcb, wib) + bi_ref[...]
        (a, _), gates_vjp = jax.vjp(_lru_gates, xc, pa, pi, lam_ref[...])
        h = h_ref[...]
        ge, gelu_vjp = jax.vjp(jax.nn.gelu, gt_ref[...])
        dout = dout_ref[...]
        dgt_ref[...] = gelu_vjp(dout * h)[0]
        adj = _scan_bwd(_shift_up(a, 1, row), dout * ge, row)
        dxc, dpa, dpi, dlam = gates_vjp((adj * _shift_dn(h, 1, row), adj))
        dpab, dpib = dpa.astype(bf16), dpi.astype(bf16)
        dxc = dxc + _nt(dpab, wab) + _nt(dpib, wib)
        dx_ref[...] = (cwv[3:4] * dxc + cwv[2:3] * _shift_up(dxc, 1, row) + cwv[1:2] * _shift_up(dxc, 2, row)
                       + cwv[0:1] * _shift_up(dxc, 3, row))

        @pl.when(pl.program_id(1) == 0)
        def _():
            for r in (dcw_ref, dcb_ref, dwa_ref, dba_ref, dwi_ref, dbi_ref, dlam_ref):
                r[...] = jnp.zeros_like(r)
        rsum = lambda v: jnp.sum(v, axis=0, keepdims=True)
        dcw_ref[...] += jnp.concatenate([rsum(dxc * _shift_dn(x, 3, row)), rsum(dxc * _shift_dn(x, 2, row)),
                                         rsum(dxc * _shift_dn(x, 1, row)), rsum(dxc * x)], axis=0)
        dcb_ref[...] += rsum(dxc)
        dwa_ref[0] += _tn(xcb, dpab)
        dwi_ref[0] += _tn(xcb, dpib)
        dba_ref[...] += rsum(dpa)
        dbi_ref[...] += rsum(dpi)
        dlam_ref[...] += dlam

    seq, vec, cw, mat = _lru_specs(s, "gb")
    t = nb * s
    vshape = jax.ShapeDtypeStruct((1, LRU_WIDTH), f32)
    mshape = jax.ShapeDtypeStruct((LRU_BLOCKS, 128, 128), f32)
    return pl.pallas_call(
        body, name="lru_bwd", grid=(LRU_BLOCKS, nb),
        in_specs=[seq(4), seq(0), seq(16), seq(20), cw, vec, mat, vec, mat, vec, vec],
        out_specs=[seq(0), seq(0), cw, vec, mat, vec, mat, vec, vec],
        out_shape=[jax.ShapeDtypeStruct((t, LRU_WIDTH), f32), jax.ShapeDtypeStruct((t, LRU_WIDTH), f32),
                   jax.ShapeDtypeStruct((4, LRU_WIDTH), f32), vshape, mshape, vshape, mshape, vshape, vshape],
        compiler_params=_cp("parallel", "arbitrary"),
    )(dmerged, states, proj, proj, conv_w, conv_b, w_a, b_a, w_i, b_i, lam)


def _s5_disc(lr, li, ldt, bre, bim):
    dt = jnp.exp(ldt)
    mag = jnp.exp(lr * dt)
    lbr = mag * jnp.cos(li * dt)
    lbi = mag * jnp.sin(li * dt)
    den = lr * lr + li * li
    nr = lbr - 1.0
    fr = (nr * lr + lbi * li) / den
    fi = (lbi * lr - nr * li) / den
    bbr = fr[:, None, :] * bre - fi[:, None, :] * bim
    bbi = fr[:, None, :] * bim + fi[:, None, :] * bre
    return lbr, lbi, bbr, bbi


def _s5_prep(lr, li, ldt, bre, bim):
    def body(lr_ref, li_ref, ldt_ref, bre_ref, bim_ref, o1, o2, o3, o4):
        o1[...], o2[...], o3[...], o4[...] = _s5_disc(lr_ref[...], li_ref[...], ldt_ref[...], bre_ref[...], bim_ref[...])

    return pl.pallas_call(
        body, name="s5_prep", in_specs=[VMEM_SPEC] * 5, out_specs=[VMEM_SPEC] * 4,
        out_shape=[jax.ShapeDtypeStruct(lr.shape, f32), jax.ShapeDtypeStruct(lr.shape, f32),
                   jax.ShapeDtypeStruct(bre.shape, f32), jax.ShapeDtypeStruct(bre.shape, f32)],
    )(lr, li, ldt, bre, bim)


def _s5_prep_bwd(lr, li, ldt, bre, bim, cts):
    def body(lr_ref, li_ref, ldt_ref, bre_ref, bim_ref, g1, g2, g3, g4, o1, o2, o3, o4, o5):
        _, vjp = jax.vjp(_s5_disc, lr_ref[...], li_ref[...], ldt_ref[...], bre_ref[...], bim_ref[...])
        o1[...], o2[...], o3[...], o4[...], o5[...] = vjp((g1[...], g2[...], g3[...], g4[...]))

    return pl.pallas_call(
        body, name="s5_prep_bwd", in_specs=[VMEM_SPEC] * 9, out_specs=[VMEM_SPEC] * 5,
        out_shape=[jax.ShapeDtypeStruct(v.shape, f32) for v in (lr, li, ldt, bre, bim)],
    )(lr, li, ldt, bre, bim, *cts)


def _cmul(ar, ai, br, bi):
    return ar * br - ai * bi, ar * bi + ai * br


def _s5_pow_table(lr, li, n, row, up):
    ar = jnp.broadcast_to(lr, (n, lr.shape[1]))
    ai = jnp.broadcast_to(li, (n, li.shape[1]))
    shift = _shift_up if up else _shift_dn
    d = 1
    while d < n:
        ar, ai = _cmul(ar, ai, shift(ar, d, row, 1.0), shift(ai, d, row, 0.0))
        d *= 2
    return ar, ai


def _s5_step_factors(lr, li, row, up):
    sub = row & (SUBLANES - 1)
    out, pr, pi, d = [], lr, li, 1
    while d < SUBLANES:
        keep = (sub < SUBLANES - d) if up else (sub >= d)
        out.append((jnp.where(keep, pr, 0.0), jnp.where(keep, pi, 0.0)))
        pr, pi = _cmul(pr, pi, pr, pi)
        d *= 2
    return out


def _s5_scan(br, bi, steps, tab_r, tab_i, cr, ci, up):
    groups = list(range(br.shape[0] // SUBLANES))
    out_r, out_i = [None] * len(groups), [None] * len(groups)
    edge = slice(0, 1) if up else slice(SUBLANES - 1, SUBLANES)
    for g in (reversed(groups) if up else groups):
        rows = slice(g * SUBLANES, (g + 1) * SUBLANES)
        xr, xi = br[rows], bi[rows]
        for k, (mr, mi) in enumerate(steps):
            shift = SUBLANES - (1 << k) if up else 1 << k
            tr, ti = _cmul(mr, mi, pltpu.roll(xr, shift, 0), pltpu.roll(xi, shift, 0))
            xr, xi = xr + tr, xi + ti
        tr, ti = _cmul(tab_r, tab_i, cr, ci)
        hr, hi = xr + tr, xi + ti
        out_r[g], out_i[g] = hr, hi
        cr, ci = hr[edge], hi[edge]
    return jnp.concatenate(out_r, axis=0), jnp.concatenate(out_i, axis=0)


def _s5_specs(t, nb, nc):
    seq = pl.BlockSpec((t, 128), lambda k: (0, k))
    lvec = pl.BlockSpec((1, S5_BLOCK_STATES), lambda k: (0, k))
    dvec = pl.BlockSpec((1, 128), lambda k: (0, k))
    wmat = pl.BlockSpec((1, 128, S5_BLOCK_STATES), lambda k: (k, 0, 0))
    h0 = pl.BlockSpec((nb, None, nc, 2, S5_BLOCK_STATES), lambda k: (0, k, 0, 0, 0))
    states = pl.BlockSpec((t, S5_BLOCK_STATES), lambda k: (0, k))
    return seq, lvec, dvec, wmat, h0, states


def _s5_fwd(u, lbr, lbi, wbr, wbi, wcr, wci, dskip, nb, s):
    ln = min(S5_CHUNK, s)
    nc = s // ln

    def body(u_ref, lr_ref, li_ref, wbr_ref, wbi_ref, wcr_ref, wci_ref, d_ref, yg_ref, y_ref, h0_ref, hr_ref, hi_ref):
        row = lax.broadcasted_iota(jnp.int32, (ln, S5_BLOCK_STATES), 0)
        lr, li = lr_ref[...], li_ref[...]
        pr, pi = _s5_pow_table(lr, li, SUBLANES, row[:SUBLANES], False)
        steps = _s5_step_factors(lr, li, row[:SUBLANES], False)
        dv = d_ref[...]

        def chunk(b, n, h0r, h0i):
            st = pl.multiple_of(b * s + n * ln, ln)
            uc = u_ref[pl.ds(st, ln), :]
            ub = uc.astype(bf16)
            hr, hi = _s5_scan(_nn(ub, wbr_ref[0]), _nn(ub, wbi_ref[0]), steps, pr, pi, h0r, h0i, False)
            h0_ref[b, n, 0:1, :] = h0r
            h0_ref[b, n, 1:2, :] = h0i
            hrb, hib = hr.astype(bf16), hi.astype(bf16)
            hr_ref[pl.ds(st, ln), :] = hrb
            hi_ref[pl.ds(st, ln), :] = hib
            y = _nt(hrb, wcr_ref[0]) - _nt(hib, wci_ref[0]) + dv * uc
            y_ref[pl.ds(st, ln), :] = y
            yg_ref[pl.ds(st, ln), :] = jax.nn.gelu(y).astype(bf16)
            return hr[ln - 1:ln, :], hi[ln - 1:ln, :]

        def step(n, carry):
            return tuple(chunk(b, n, *carry[b]) for b in range(nb))

        z = jnp.zeros((1, S5_BLOCK_STATES), f32)
        lax.fori_loop(0, nc, step, ((z, z),) * nb)

    t = nb * s
    seq, lvec, dvec, wmat, h0, states = _s5_specs(t, nb, nc)
    return pl.pallas_call(
        body, name="s5_fwd", grid=(S5_BLOCKS,),
        in_specs=[seq, lvec, lvec, wmat, wmat, wmat, wmat, dvec],
        out_specs=[seq, seq, h0, states, states],
        out_shape=[jax.ShapeDtypeStruct((t, D_MODEL), bf16), jax.ShapeDtypeStruct((t, D_MODEL), f32),
                   jax.ShapeDtypeStruct((nb, S5_BLOCKS, nc, 2, S5_BLOCK_STATES), f32),
                   jax.ShapeDtypeStruct((t, S5_BLOCKS * S5_BLOCK_STATES), bf16),
                   jax.ShapeDtypeStruct((t, S5_BLOCKS * S5_BLOCK_STATES), bf16)],
        compiler_params=_cp("parallel"),
    )(u, lbr, lbi, wbr, wbi, wcr, wci, dskip)


def _s5_bwd(dyg, y, u, h0, hrs, his, lbr, lbi, wbr, wbi, wcr, wci, dskip, nb, s):
    ln = min(S5_CHUNK, s)
    nc = s // ln

    def body(dyg_ref, y_ref, u_ref, h0_ref, hr_ref, hi_ref, lr_ref, li_ref, wbr_ref, wbi_ref, wcr_ref, wci_ref, d_ref,
             du_ref, dlr_ref, dli_ref, dwbr_ref, dwbi_ref, dwcr_ref, dwci_ref, dd_ref):
        for r in (dlr_ref, dli_ref, dwbr_ref, dwbi_ref, dwcr_ref, dwci_ref, dd_ref):
            r[...] = jnp.zeros_like(r)
        row = lax.broadcasted_iota(jnp.int32, (ln, S5_BLOCK_STATES), 0)
        lr, li = lr_ref[...], li_ref[...]
        qr, qi = _s5_pow_table(lr, -li, SUBLANES, row[:SUBLANES], True)
        steps_up = _s5_step_factors(lr, -li, row[:SUBLANES], True)
        dv = d_ref[...]
        rsum = lambda v: jnp.sum(v, axis=0, keepdims=True)

        def chunk(b, n, gnr, gni):
            st = pl.multiple_of(b * s + n * ln, ln)
            uc = u_ref[pl.ds(st, ln), :]
            ub = uc.astype(bf16)
            h0v = h0_ref[b, n]
            h0r, h0i = h0v[0:1], h0v[1:2]
            hrb, hib = hr_ref[pl.ds(st, ln), :], hi_ref[pl.ds(st, ln), :]
            hr, hi = hrb.astype(f32), hib.astype(f32)
            dy = jax.vjp(jax.nn.gelu, y_ref[pl.ds(st, ln), :])[1](dyg_ref[pl.ds(st, ln), :])[0]
            dyb = dy.astype(bf16)
            dd_ref[...] += rsum(dy * uc)
            gr, gi = _s5_scan(_nn(dyb, wcr_ref[0]), -_nn(dyb, wci_ref[0]), steps_up, qr, qi, gnr, gni, True)
            hpr = jnp.where(row >= 1, pltpu.roll(hr, 1, 0), h0r)
            hpi = jnp.where(row >= 1, pltpu.roll(hi, 1, 0), h0i)
            dlr_ref[...] += rsum(gr * hpr + gi * hpi)
            dli_ref[...] += rsum(gi * hpr - gr * hpi)
            grb, gib = gr.astype(bf16), gi.astype(bf16)
            dwbr_ref[0] += _tn(ub, grb)
            dwbi_ref[0] += _tn(ub, gib)
            dwcr_ref[0] += _tn(dyb, hrb)
            dwci_ref[0] -= _tn(dyb, hib)
            du_ref[pl.ds(st, ln), :] = _nt(grb, wbr_ref[0]) + _nt(gib, wbi_ref[0]) + dv * dy
            return gr[0:1, :], gi[0:1, :]

        def step(i, carry):
            return tuple(chunk(b, nc - 1 - i, *carry[b]) for b in range(nb))

        z = jnp.zeros((1, S5_BLOCK_STATES), f32)
        lax.fori_loop(0, nc, step, ((z, z),) * nb)

    t = nb * s
    seq, lvec, dvec, wmat, h0s, states = _s5_specs(t, nb, nc)
    lshape = jax.ShapeDtypeStruct((1, S5_BLOCKS * S5_BLOCK_STATES), f32)
    wshape = jax.ShapeDtypeStruct((S5_BLOCKS, 128, S5_BLOCK_STATES), f32)
    return pl.pallas_call(
        body, name="s5_bwd", grid=(S5_BLOCKS,),
        in_specs=[seq, seq, seq, h0s, states, states, lvec, lvec, wmat, wmat, wmat, wmat, dvec],
        out_specs=[seq, lvec, lvec, wmat, wmat, wmat, wmat, dvec],
        out_shape=[jax.ShapeDtypeStruct((t, D_MODEL), f32), lshape, lshape, wshape, wshape, wshape, wshape,
                   jax.ShapeDtypeStruct((1, D_MODEL), f32)],
        compiler_params=_cp("parallel"),
    )(dyg, y, u, h0, hrs, his, lbr, lbi, wbr, wbi, wcr, wci, dskip)


def _blockdiag(w):
    w4 = w.reshape(S5_BLOCKS, 8, S5_GROUP, S5_STATE)
    same_group = jnp.eye(8, dtype=bool)[None, :, None, :, None]
    return jnp.where(same_group, w4[:, :, :, None, :], 0.0).reshape(S5_BLOCKS, 128, S5_BLOCK_STATES)


def _blockdiag_t(dw):
    d5 = dw.reshape(S5_BLOCKS, 8, S5_GROUP, 8, S5_STATE)
    diag = jnp.diagonal(d5, axis1=1, axis2=3)
    return jnp.moveaxis(diag, 3, 1).reshape(S5_GROUPS, S5_GROUP, S5_STATE)


def _glu_fwd(ygb, wa, wb, x, tm=512, tn=1024):
    t, d = x.shape

    def body(y_ref, wa_ref, wb_ref, x_ref, o_ref, p_ref, q_ref):
        p = _nn(y_ref[...], wa_ref[...])
        q = _nn(y_ref[...], wb_ref[...])
        p_ref[...] = p
        q_ref[...] = q
        o_ref[...] = x_ref[...] + p * jax.nn.sigmoid(q)

    tile = pl.BlockSpec((tm, tn), lambda i, j: (i, j))
    wsp = pl.BlockSpec((d, tn), lambda i, j: (0, j))
    out = jax.ShapeDtypeStruct((t, d), f32)
    return pl.pallas_call(
        body, name="glu_fwd", grid=(t // tm, d // tn),
        in_specs=[pl.BlockSpec((tm, d), lambda i, j: (i, 0)), wsp, wsp, tile],
        out_specs=[tile, tile, tile], out_shape=[out, out, out],
        compiler_params=_cp("parallel", "parallel"),
    )(ygb, wa, wb, x)


def _place():
    x, y, c = lax.axis_index("x"), lax.axis_index("y"), lax.axis_index("c")
    return x, y, c, [(1 - x, y), (x, 1 - y), (1 - x, 1 - y)]


def _all_gather(name, arrays):
    n = len(arrays)

    def body(*refs):
        ins, outs = refs[:n], refs[n:2 * n]
        send_sems, recv_sems, local_sems = refs[2 * n:]
        x, y, c, chips = _place()
        me, sib = (x, y, c), (x, y, 1 - c)

        def copy(i, k, block, to, src=None):
            dst = outs[i].at[4 * block[0] + 2 * block[1] + block[2]]
            return pltpu.make_async_remote_copy(
                src_ref=dst if src is None else src, dst_ref=dst,
                send_sem=send_sems.at[i * 7 + k], recv_sem=recv_sems.at[i * 7 + k],
                device_id=to, device_id_type=MESH)

        mine = [pltpu.make_async_copy(ins[i], outs[i].at[4 * x + 2 * y + c], local_sems.at[i]) for i in range(n)]
        for m in mine:
            m.start()
        first = []
        for i in range(n):
            first.append(copy(i, 0, me, sib, src=ins[i]))
            first += [copy(i, 1 + j, me, (*chip, c), src=ins[i]) for j, chip in enumerate(chips)]
        for cp in first:
            cp.start()
        passed = []
        for j, chip in enumerate(chips):
            for i in range(n):
                copy(i, 1 + j, (*chip, c), me).wait_recv()
                fwd = copy(i, 4 + j, (*chip, c), sib)
                fwd.start()
                passed.append(fwd)
        for i in range(n):
            copy(i, 0, sib, me).wait_recv()
        for j, chip in enumerate(chips):
            for i in range(n):
                copy(i, 4 + j, (*chip, 1 - c), me).wait_recv()
        for cp in first + passed:
            cp.wait_send()
        for m in mine:
            m.wait()

    return pl.pallas_call(
        body, name=name,
        in_specs=[ANY_SPEC] * n, out_specs=[ANY_SPEC] * n,
        out_shape=[jax.ShapeDtypeStruct((N_DEV,) + a.shape, a.dtype) for a in arrays],
        scratch_shapes=[pltpu.SemaphoreType.DMA((7 * n,)), pltpu.SemaphoreType.DMA((7 * n,)),
                        pltpu.SemaphoreType.DMA((n,))],
    )(*arrays)


def _tie(name, x, deps):
    def body(*refs):
        pass

    return pl.pallas_call(
        body, name=name, in_specs=[ANY_SPEC] * (1 + len(deps)), out_specs=ANY_SPEC,
        out_shape=jax.ShapeDtypeStruct(x.shape, x.dtype), input_output_aliases={0: 0},
    )(x, *deps)


def _xchg_copies(kind, srcs, lands, suffixes, send_sems, recv_sems):
    x, y, c, _ = _place()
    copies = []
    for i, (src, land, sfx) in enumerate(zip(srcs, lands, suffixes)):
        for k in range(N_DEV - 1):
            r = k + 1
            peer = (1 - x if r & 4 else x, 1 - y if r & 2 else y, 1 - c if r & 1 else c)
            if kind == "gather":
                s_ref, d_ref = src, land.at[(4 * x + 2 * y + c,) + sfx]
            else:
                s_ref, d_ref = src.at[4 * peer[0] + 2 * peer[1] + peer[2]], land.at[(k,) + sfx]
            copies.append(pltpu.make_async_remote_copy(
                src_ref=s_ref, dst_ref=d_ref, send_sem=send_sems.at[i * 7 + k], recv_sem=recv_sems.at[i * 7 + k],
                device_id=peer, device_id_type=MESH))
    return copies


def _xchg_start(name, kind, srcs, lands, suffixes=None):
    n = len(srcs)
    suffixes = suffixes or [()] * n

    def body(*refs):
        src, land = refs[:n], refs[n:2 * n]
        send_sems, recv_sems, token = refs[2 * n], refs[2 * n + 1], refs[-1]
        for cp in _xchg_copies(kind, src, land, suffixes, send_sems, recv_sems):
            cp.start()
        token[...] = jnp.zeros_like(token)

    arrays = list(srcs) + list(lands)
    outs = pl.pallas_call(
        body, name=name,
        out_shape=(pltpu.SemaphoreType.DMA((7 * n,)), pltpu.SemaphoreType.DMA((7 * n,)),
                   *[pltpu.HBM(a.shape, a.dtype) for a in arrays], jax.ShapeDtypeStruct((8, 128), f32)),
        in_specs=[HBM_SPEC] * (2 * n),
        out_specs=(SEM_SPEC, SEM_SPEC, *[HBM_SPEC] * (2 * n), VMEM_SPEC),
        input_output_aliases={i: 2 + i for i in range(2 * n)},
        compiler_params=pltpu.CompilerParams(has_side_effects=SIDE_EFFECT),
    )(*[pltpu.with_memory_space_constraint(a, pltpu.HBM) for a in arrays])
    return dict(kind=kind, n=n, suffixes=suffixes, send=outs[0], recv=outs[1], srcs=list(outs[2:2 + n]),
                lands=list(outs[2 + n:2 + 2 * n]), token=outs[-1])


def _xchg_wait(name, h, after, lands=None):
    n = h["n"]
    lands = h["lands"] if lands is None else lands

    def body(*refs):
        src, land = refs[:n], refs[n:2 * n]
        for cp in _xchg_copies(h["kind"], src, land, h["suffixes"], refs[2 * n], refs[2 * n + 1]):
            cp.wait_send()
            cp.wait_recv()

    arrays = h["srcs"] + list(lands)
    outs = pl.pallas_call(
        body, name=name,
        out_shape=tuple(pltpu.HBM(a.shape, a.dtype) for a in arrays),
        in_specs=[HBM_SPEC] * (2 * n) + [SEM_SPEC, SEM_SPEC] + [ANY_SPEC] * len(after),
        out_specs=tuple([HBM_SPEC] * (2 * n)),
        input_output_aliases={i: i for i in range(2 * n)},
        compiler_params=pltpu.CompilerParams(has_side_effects=SIDE_EFFECT),
    )(*arrays, h["send"], h["recv"], *after)
    return list(outs[n:])


def _rows(a):
    return a.reshape(-1, a.shape[-1])


def _row_tile(r):
    for tm in (512, 256, 128, 64, 32, 16, 8):
        if r % tm == 0:
            return tm
    return r


def _sum8(name, gathered):
    _, r, n = gathered.shape
    tm = _row_tile(r)

    def body(g_ref, o_ref):
        acc = g_ref[0]
        for k in range(1, N_DEV):
            acc = acc + g_ref[k]
        o_ref[...] = acc

    return pl.pallas_call(
        body, name=name, grid=(r // tm,),
        in_specs=[pl.BlockSpec((N_DEV, tm, n), lambda i: (0, i, 0))],
        out_specs=pl.BlockSpec((tm, n), lambda i: (i, 0)),
        out_shape=jax.ShapeDtypeStruct((r, n), f32),
        compiler_params=_cp("parallel"),
    )(gathered)


def _adamw(name, w, m, v, own, landed=None, slot=None):
    shape = w.shape
    w2, m2, v2 = _rows(w), _rows(m), _rows(v)
    r, n = w2.shape
    tm = _row_tile(r)
    c1 = 1.0 - ADAM_B1 ** ADAM_STEP
    c2 = 1.0 - ADAM_B2 ** ADAM_STEP
    extra = [] if landed is None else [landed.reshape(landed.shape[0], r, n)]
    row = pl.BlockSpec((tm, n), lambda i, *_: (i, 0))
    if slot is None:
        o2, own_spec, scalars = _rows(own), row, []
    else:
        dev, kind = slot
        scalars = [dev.reshape(1).astype(jnp.int32)]
        if kind == "lead":
            o2, own_spec = own.reshape(N_DEV, r, n), pl.BlockSpec((None, tm, n), lambda i, d: (d[0], i, 0))
        elif kind == "rows":
            o2, own_spec = own, pl.BlockSpec((tm, n), lambda i, d: (d[0] * (r // tm) + i, 0))
        else:
            o2, own_spec = own, pl.BlockSpec((tm, n), lambda i, d: (i, d[0]))

    def body(*refs):
        w_ref, m_ref, v_ref, o_ref = refs[len(scalars):len(scalars) + 4]
        refs = refs[len(scalars) + 4:]
        g = o_ref[...]
        if extra:
            for k in range(extra[0].shape[0]):
                g = g + refs[0][k].astype(f32)
        g_ref, d_ref, mn_ref, vn_ref = refs[len(extra):]
        mn = ADAM_B1 * m_ref[...] + (1.0 - ADAM_B1) * g
        vn = ADAM_B2 * v_ref[...] + (1.0 - ADAM_B2) * (g * g)
        g_ref[...] = g
        d_ref[...] = -ADAM_LR * ((mn / c1) / (jnp.sqrt(vn / c2) + ADAM_EPS) + ADAM_WD * w_ref[...])
        mn_ref[...] = mn
        vn_ref[...] = vn

    outs = pl.pallas_call(
        body, name=name,
        grid_spec=pltpu.PrefetchScalarGridSpec(
            num_scalar_prefetch=len(scalars), grid=(r // tm,),
            in_specs=[row] * 3 + [own_spec] + [pl.BlockSpec((e.shape[0], tm, n), lambda i, *_: (0, i, 0)) for e in extra],
            out_specs=[row] * 4),
        out_shape=[jax.ShapeDtypeStruct((r, n), f32)] * 4,
        compiler_params=_cp("parallel"),
    )(*scalars, w2, m2, v2, o2, *extra)
    return [o.reshape(shape) for o in outs]


def _adamw_minor_d(name, w, m, v, own_all, landed, dev, tm=512):
    nl, nh, d, f = w.shape
    wt, mt, vt = (jnp.swapaxes(t, 2, 3) for t in (w, m, v))
    r = nl * nh * d
    per = d // tm
    c1 = 1.0 - ADAM_B1 ** ADAM_STEP
    c2 = 1.0 - ADAM_B2 ** ADAM_STEP

    def body(dev_ref, w_ref, m_ref, v_ref, o_ref, l_ref, g_ref, d_ref, mn_ref, vn_ref):
        g = o_ref[...]
        for k in range(N_DEV - 1):
            g = g + l_ref[k].astype(f32)
        g = g.T
        mn = ADAM_B1 * m_ref[...] + (1.0 - ADAM_B1) * g
        vn = ADAM_B2 * v_ref[...] + (1.0 - ADAM_B2) * (g * g)
        g_ref[...] = g
        d_ref[...] = -ADAM_LR * ((mn / c1) / (jnp.sqrt(vn / c2) + ADAM_EPS) + ADAM_WD * w_ref[...])
        mn_ref[...] = mn
        vn_ref[...] = vn

    par = pl.BlockSpec((None, None, f, tm), lambda i, _: (i // (nh * per), (i // per) % nh, 0, i % per))
    outs = pl.pallas_call(
        body, name=name,
        grid_spec=pltpu.PrefetchScalarGridSpec(
            num_scalar_prefetch=1, grid=(r // tm,),
            in_specs=[par, par, par, pl.BlockSpec((None, tm, f), lambda i, dv: (dv[0], i, 0)),
                      pl.BlockSpec((N_DEV - 1, tm, f), lambda i, _: (0, i, 0))],
            out_specs=[par] * 4),
        out_shape=[jax.ShapeDtypeStruct(wt.shape, f32)] * 4,
        compiler_params=_cp("parallel"),
    )(dev.reshape(1).astype(jnp.int32), wt, mt, vt, own_all.reshape(N_DEV, r, f), landed.reshape(N_DEV - 1, r, f))
    return [jnp.swapaxes(o, 2, 3) for o in outs]


def _pack(arrays):
    flat = jnp.concatenate([a.reshape(-1).astype(f32) for a in arrays])
    pad = (-flat.shape[0]) % (128 * (512 if flat.shape[0] > 128 * 512 else 8))
    return jnp.pad(flat, (0, pad)).reshape(-1, 128)


def _unpack(packed, shapes):
    flat = packed.reshape(-1)
    out, off = [], 0
    for s in shapes:
        n = math.prod(s)
        out.append(flat[off:off + n].reshape(s))
        off += n
    return out


def _local_step(x, target, w, weights_of, send, last_small, on_loss, nb, s):
    cos, sin = _rope_tables(s)
    g = {}
    ffn_saved = {}
    ffn_bufs = [lax.empty((N_DEV, 2, 2) + shp, f32)
                for shp in ((D_MODEL, FF_SHARD), (D_MODEL, FF_SHARD), (FF_SHARD, D_MODEL))]

    def ffn(xin, l, h, wts):
        y, a, b = _ffn_fwd(f"ffn_fwd_{l}{h}", xin, w["ffn_g"][l][h], *wts)
        ffn_saved[(l, h)] = (xin, a, b, wts)
        return y

    def ffn_back(dy, l, h):
        xin, a, b, wts = ffn_saved[(l, h)]
        dx, dg, hb, dyh, u, da, db = _ffn_dx(f"ffn_dx_{l}{h}", dy, xin, w["ffn_g"][l][h], *wts, a, b)
        g[f"ffn_g_{l}{h}"] = dg
        if (l, h) != (0, 0):
            ffn_bufs[:2], (h1, h3) = _ffn_dw(f"ffn_dw_{l}{h}_w13", hb, [da, db], ffn_bufs[:2], l, h)
            ffn_bufs[2:], (h2,) = _ffn_dw(f"ffn_dw_{l}{h}_w2", u, [dyh], ffn_bufs[2:], l, h)
            return send(f"ffn_{l}{h}", {"ffn_w1": h1, "ffn_w3": h3, "ffn_w2": h2}, dx)
        hb = last_small(g, hb)
        ffn_bufs[:1], (half,) = _ffn_dw("ffn_dw_00_w1", hb, [da], ffn_bufs[:1], l, h)
        hb = send("ffn_00_w1", {"ffn_w1": half}, hb)
        ffn_bufs[1:2], (half,) = _ffn_dw("ffn_dw_00_w3", hb, [db], ffn_bufs[1:2], l, h)
        u = send("ffn_00_w3", {"ffn_w3": half}, u)
        ffn_bufs[2:], (half,) = _ffn_dw("ffn_dw_00_w2", u, [dyh], ffn_bufs[2:], l, h)
        return send("ffn_00_w2", {"ffn_w2": half}, dx)

    def slots(t):
        return t.reshape(N_DEV, D_MODEL // N_DEV, D_MODEL)

    x1 = ffn(x, 0, 0, weights_of(0, [])["ffn"])
    wg = weights_of(1, [x1])
    w_in, w_out = wg["w_in"], wg["w_out"]
    h0b = _norm_fwd("mix_norm_0", x1, w["mix_g"][0], bf16)
    proj = _mm("in_proj", h0b, w_in, "nn", tn=1536)[0]
    o_raw, rprev, mret = _ret_fwd(proj, cos, sin, w["ret_g"], nb, s)
    lru, lru_h = _lru_fwd(proj, w["conv_w"], w["conv_b"], w["lru_w_a"], w["lru_b_a"], w["lru_w_i"], w["lru_b_i"], w["lru_lam"], nb, s)
    merged = _ew("merge", lambda a, b: (jnp.concatenate([a, b], axis=1),), [mret, lru], [(D_MODEL, bf16)])[0]
    x2 = _mm("out_proj", merged, w_out, "nn", extras=[x1], epilogue=lambda acc, r: (acc + r,))[0]
    x3 = ffn(x2, 0, 1, weights_of(2, [x2])["ffn"])
    x4 = ffn(x3, 1, 0, weights_of(3, [x3])["ffn"])
    u = _norm_fwd("mix_norm_1", x4, w["mix_g"][1], f32)
    lbr, lbi, bbr, bbi = _s5_prep(w["s5_lr"], w["s5_li"], w["s5_ldt"], w["s5_bre"], w["s5_bim"])
    lbr_f, lbi_f = lbr.reshape(1, -1), lbi.reshape(1, -1)
    wbr, wbi = _blockdiag(bbr).astype(bf16), _blockdiag(bbi).astype(bf16)
    wcr, wci = _blockdiag(w["s5_cre"]).astype(bf16), _blockdiag(w["s5_cim"]).astype(bf16)
    ygb, ypre, h0s, hrs, his = _s5_fwd(u, lbr_f, lbi_f, wbr, wbi, wcr, wci, w["s5_d"], nb, s)
    wg = weights_of(4, [ygb])
    glu_a, glu_b = wg["glu_a"], wg["glu_b"]
    x5, gp, gq = _glu_fwd(ygb, glu_a, glu_b, x4)
    x6 = ffn(x5, 1, 1, weights_of(5, [x5])["ffn"])
    loss, dx6, g["final_g"] = _final_loss(x6, w["final_g"], target)
    dx6 = on_loss(loss, dx6)

    dx5 = ffn_back(dx6, 1, 1)

    def glu_bwd(d, p, q):
        sg = jax.nn.sigmoid(q)
        return d * sg, d * p * sg * (1.0 - sg)

    dp, dq = _ew("glu_bwd", glu_bwd, [dx5, gp, gq], [(D_MODEL, bf16), (D_MODEL, bf16)])
    dyg = _mm("glu_dy_a", dp, glu_a, "nt")[0]
    dyg = _mm("glu_dy_b", dq, glu_b, "nt", extras=[dyg], epilogue=lambda acc, r: (acc + r,))[0]
    g["glu_a"], ga_half = _mm_tn("glu_dw_a", ygb, dp)
    g["glu_b"], gb_half = _mm_tn("glu_dw_b", ygb, dq)
    dyg = send("glu", {"glu_a": slots(ga_half), "glu_b": slots(gb_half)}, dyg)
    du, dlr, dli, dwbr, dwbi, dwcr, dwci, g["s5_d"] = _s5_bwd(dyg, ypre, u, h0s, hrs, his, lbr_f, lbi_f, wbr, wbi, wcr, wci, w["s5_d"], nb, s)
    g["s5_cre"], g["s5_cim"] = _blockdiag_t(dwcr), _blockdiag_t(dwci)
    g["s5_lr"], g["s5_li"], g["s5_ldt"], g["s5_bre"], g["s5_bim"] = _s5_prep_bwd(
        w["s5_lr"], w["s5_li"], w["s5_ldt"], w["s5_bre"], w["s5_bim"],
        (dlr.reshape(S5_GROUPS, S5_STATE), dli.reshape(S5_GROUPS, S5_STATE), _blockdiag_t(dwbr), _blockdiag_t(dwbi)))
    dx4, g["mix_g_1"] = _norm_bwd("mix_norm_1_bwd", du, x4, w["mix_g"][1], dx5)
    dx3 = ffn_back(dx4, 1, 0)
    dx2 = ffn_back(dx3, 0, 1)
    dmerged = _mm("out_proj_dx", dx2, w_out, "nt")[0]
    g["w_out"], wo_half = _mm_tn("out_proj_dw", merged, dx2)
    dmerged = send("w_out", {"w_out": slots(wo_half)}, dmerged)
    dq_, dk_, dv_, dgate, g["ret_g"] = _ret_bwd(dmerged, o_raw, rprev, proj, cos, sin, w["ret_g"], nb, s)
    (dxl, dgl, g["conv_w"], g["conv_b"], g["lru_w_a"], g["lru_b_a"], g["lru_w_i"], g["lru_b_i"], g["lru_lam"]) = _lru_bwd(
        dmerged, lru_h, proj, w["conv_w"], w["conv_b"], w["lru_w_a"], w["lru_b_a"], w["lru_w_i"], w["lru_b_i"], w["lru_lam"],
        nb, s)
    dproj = _ew("dproj", lambda *p: (jnp.concatenate(p, axis=1),), [dq_, dk_, dv_, dgate, dxl, dgl], [(3072, bf16)])[0]
    dh0 = _mm("in_proj_dx", dproj, w_in, "nt")[0]
    g["w_in"], wi_half = _mm_tn("in_proj_dw", h0b, dproj)
    dh0 = send("w_in", {"w_in": jnp.transpose(wi_half.reshape(D_MODEL, N_DEV, IN_SHARD), (1, 0, 2))}, dh0)
    dx1, g["mix_g_0"] = _norm_bwd("mix_norm_0_bwd", dh0, x1, w["mix_g"][0], dx2)
    dx0 = ffn_back(dx1, 0, 0)
    g["ffn_w1"], g["ffn_w3"], g["ffn_w2"] = ffn_bufs
    return loss, dx0, g


_WEIGHTS = ["ffn_norm_g", "ffn_w1", "ffn_w3", "ffn_w2", "mix_norm_g", "w_in_even", "w_out_even", "ret_norm_g", "conv_w",
            "conv_b", "lru_w_a", "lru_b_a", "lru_w_i", "lru_b_i", "lru_lambda", "s5_lambda_re", "s5_lambda_im", "s5_log_dt",
            "s5_b_re", "s5_b_im", "s5_c_re", "s5_c_im", "s5_d", "glu_w_a", "glu_w_b", "final_norm_g"]
_BIG = ["ffn_w1", "ffn_w3", "ffn_w2", "w_in_even", "w_out_even", "glu_w_a", "glu_w_b"]
_SMALL_SHARDED = ["ffn_norm_g", "conv_w", "s5_d"]
_SMALL = [n for n in _WEIGHTS if n not in _BIG]
_MIDSIZE = ["lru_w_a", "lru_w_i", "s5_b_re", "s5_b_im", "s5_c_re", "s5_c_im"]


def kernel(x, ffn_norm_g, ffn_w1, ffn_w3, ffn_w2, mix_norm_g, w_in_even, w_out_even, ret_norm_g, conv_w, conv_b, lru_w_a, lru_b_a, lru_w_i, lru_b_i, lru_lambda, s5_lambda_re, s5_lambda_im, s5_log_dt, s5_b_re, s5_b_im, s5_c_re, s5_c_im, s5_d, glu_w_a, glu_w_b, final_norm_g, loss_target, m_ffn_norm_g, m_ffn_w1, m_ffn_w3, m_ffn_w2, m_mix_norm_g, m_w_in_even, m_w_out_even, m_ret_norm_g, m_conv_w, m_conv_b, m_lru_w_a, m_lru_b_a, m_lru_w_i, m_lru_b_i, m_lru_lambda, m_s5_lambda_re, m_s5_lambda_im, m_s5_log_dt, m_s5_b_re, m_s5_b_im, m_s5_c_re, m_s5_c_im, m_s5_d, m_glu_w_a, m_glu_w_b, m_final_norm_g, v_ffn_norm_g, v_ffn_w1, v_ffn_w3, v_ffn_w2, v_mix_norm_g, v_w_in_even, v_w_out_even, v_ret_norm_g, v_conv_w, v_conv_b, v_lru_w_a, v_lru_b_a, v_lru_w_i, v_lru_b_i, v_lru_lambda, v_s5_lambda_re, v_s5_lambda_im, v_s5_log_dt, v_s5_b_re, v_s5_b_im, v_s5_c_re, v_s5_c_im, v_s5_d, v_glu_w_a, v_glu_w_b, v_final_norm_g):
    a = dict(locals())
    nb, s, d = x.shape
    dev = 4 * lax.axis_index("x") + 2 * lax.axis_index("y") + lax.axis_index("c")

    def ffn_shards(l, h, pad):
        out = [jnp.swapaxes(ffn_w1[l, h], 0, 1).astype(bf16), jnp.swapaxes(ffn_w3[l, h], 0, 1).astype(bf16),
               ffn_w2[l, h].astype(bf16)]
        return [jnp.pad(t, ((0, FF_PAD - FF_SHARD), (0, 0))) for t in out] if pad else out

    first = _all_gather("ag_first", ffn_shards(0, 0, True) + [_pack([ffn_norm_g, conv_w, s5_d])])
    sm = first[3].reshape(N_DEV, -1)
    ffn_g_full = jnp.transpose(sm[:, :512].reshape(N_DEV, 2, 2, 128), (1, 2, 0, 3)).reshape(2, 2, D_MODEL)
    conv_w_full = jnp.transpose(sm[:, 512:768].reshape(N_DEV, 4, 64), (1, 0, 2)).reshape(4, LRU_WIDTH)
    s5_d_full = sm[:, 768:896].reshape(1, D_MODEL)

    ag_src = [None, [w_in_even[0].astype(bf16), w_out_even[0].astype(bf16)], ffn_shards(0, 1, False),
              ffn_shards(1, 0, False), [glu_w_a[0].astype(bf16), glu_w_b[0].astype(bf16)], ffn_shards(1, 1, False)]
    ag, token = [None], first[0]
    for k, grp in enumerate(ag_src):
        if grp is None:
            continue
        grp[0] = _tie(f"tie_ag_{k}", grp[0], [token])
        if grp[0].shape[0] == FF_SHARD:
            zero_rows = jnp.zeros((N_DEV, FF_PAD - FF_SHARD, D_MODEL), bf16)
            lands = [lax.dynamic_update_slice(
                lax.dynamic_update_slice(lax.empty((N_DEV, FF_PAD, D_MODEL), bf16), zero_rows, (0, FF_SHARD, 0)),
                t[None], (dev, 0, 0)) for t in grp]
            sfx = [(pl.ds(0, FF_SHARD),)] * len(grp)
        else:
            lands = [lax.dynamic_update_index_in_dim(lax.empty((N_DEV,) + t.shape, bf16), t, dev, 0) for t in grp]
            sfx = None
        ag.append(_xchg_start(f"ag_start_{k}", "gather", grp, lands, sfx))
        token = ag[-1]["token"]

    def weights_of(k, after):
        if k == 0:
            return {"ffn": [first[0], _tie("tie_ag_started", first[1], [h["token"] for h in ag[1:]]), first[2]]}
        got = _xchg_wait(f"ag_wait_{k}", ag[k], after)
        if k == 1:
            return {"w_in": jnp.transpose(got[0], (1, 0, 2)).reshape(D_MODEL, N_DEV * IN_SHARD),
                    "w_out": got[1].reshape(D_MODEL, D_MODEL)}
        if k == 4:
            return {"glu_a": got[0].reshape(D_MODEL, D_MODEL), "glu_b": got[1].reshape(D_MODEL, D_MODEL)}
        return {"ffn": got}

    ffn_lands = [lax.empty((N_DEV - 1, 2, 2) + shp, bf16)
                 for shp in ((D_MODEL, FF_SHARD), (D_MODEL, FF_SHARD), (FF_SHARD, D_MODEL))]
    rs = []

    ffn_names = ("ffn_w1", "ffn_w3", "ffn_w2")

    def send(group, arrays, carry):
        srcs = list(arrays.values())
        if group.startswith("ffn_"):
            which = [ffn_names.index(n) for n in arrays]
            sfx = [(int(group[4]), int(group[5]))] * len(which)
            h = _xchg_start("rs_start_" + group, "scatter", srcs, [ffn_lands[k] for k in which], sfx)
            for k, land in zip(which, h["lands"]):
                ffn_lands[k] = land
        else:
            h = _xchg_start("rs_start_" + group, "scatter", srcs,
                            [lax.empty((N_DEV - 1,) + t.shape[1:], bf16) for t in srcs])
        rs.append((group, list(arrays), h))
        return _tie("tie_" + group, carry, [h["token"]])

    w = {
        "ffn_g": [[ffn_g_full[l, h].reshape(1, D_MODEL) for h in range(2)] for l in range(2)],
        "mix_g": [mix_norm_g[0:1], mix_norm_g[1:2]],
        "ret_g": ret_norm_g, "conv_w": conv_w_full, "conv_b": conv_b,
        "lru_w_a": lru_w_a[0], "lru_b_a": lru_b_a, "lru_w_i": lru_w_i[0], "lru_b_i": lru_b_i, "lru_lam": lru_lambda,
        "s5_lr": s5_lambda_re[0], "s5_li": s5_lambda_im[0], "s5_ldt": s5_log_dt.reshape(S5_GROUPS, 1),
        "s5_bre": jnp.swapaxes(s5_b_re[0], 1, 2), "s5_bim": jnp.swapaxes(s5_b_im[0], 1, 2),
        "s5_cre": s5_c_re[0], "s5_cim": s5_c_im[0], "s5_d": s5_d_full,
        "final_g": final_norm_g.reshape(1, D_MODEL),
    }

    small_grads = {}

    def last_small(g, carry):
        part = _small_partials(g)
        mine = _pack([part[n] for n in _SMALL])
        land = lax.dynamic_update_index_in_dim(lax.empty((N_DEV,) + mine.shape, f32), mine, dev, 0)
        h = _xchg_start("ag_start_small_grads", "gather", [mine], [land])
        small_grads.update(h=h, shapes=[part[n].shape for n in _SMALL])
        return _tie("tie_small_grads", carry, [h["token"]])

    total_loss = []

    def on_loss(part, carry):
        total_loss.append(lax.psum(part[0, 0], ("x", "y", "c")))
        return _tie("tie_loss", carry, [jnp.broadcast_to(total_loss[0], (8, 128))])

    _, dx, g = _local_step(x.reshape(nb * s, d), loss_target.reshape(nb * s, d), w, weights_of, send, last_small,
                           on_loss, nb, s)
    loss = total_loss[0]
    (gath,) = _xchg_wait("ag_wait_small_grads", small_grads["h"], [dx])
    full = dict(zip(_SMALL, _unpack(_sum8("sum_small_grads", gath), small_grads["shapes"])))
    for n in _SMALL_SHARDED:
        width = a[n].shape[-1]
        full[n] = lax.dynamic_slice_in_dim(full[n], dev * width, width, axis=full[n].ndim - 1)
    res = {}
    for n in _MIDSIZE:
        if n.startswith("s5_b_"):
            swap = lambda t: jnp.swapaxes(t, 2, 3)
            res[n] = [swap(t) for t in _adamw("adamw_" + n, swap(a[n]), swap(a["m_" + n]), swap(a["v_" + n]), full[n])]
        else:
            res[n] = _adamw("adamw_" + n, a[n], a["m_" + n], a["v_" + n], full[n])
    tiny = [n for n in _SMALL if n not in _MIDSIZE]
    shapes = [a[n].shape for n in tiny]
    packed = _adamw("adamw_small", _pack([a[n] for n in tiny]), _pack([a["m_" + n] for n in tiny]),
                    _pack([a["v_" + n] for n in tiny]), _pack([full[n] for n in tiny]))
    res.update({n: vals for n, vals in zip(tiny, zip(*[_unpack(p, shapes) for p in packed]))})
    return _finish(a, g, dx, loss, res, packed, rs, ffn_lands, dev, nb, s, d)


def _small_partials(g):
    return {
        "ffn_norm_g": jnp.stack([jnp.stack([g[f"ffn_g_{l}{h}"][0] for h in range(2)]) for l in range(2)]),
        "mix_norm_g": jnp.concatenate([g["mix_g_0"], g["mix_g_1"]], axis=0),
        "ret_norm_g": g["ret_g"], "conv_w": g["conv_w"][None], "conv_b": g["conv_b"],
        "lru_w_a": g["lru_w_a"][None], "lru_b_a": g["lru_b_a"], "lru_w_i": g["lru_w_i"][None], "lru_b_i": g["lru_b_i"],
        "lru_lambda": g["lru_lam"], "s5_lambda_re": g["s5_lr"][None], "s5_lambda_im": g["s5_li"][None],
        "s5_log_dt": g["s5_ldt"].reshape(1, S5_GROUPS),
        "s5_b_re": g["s5_bre"][None], "s5_b_im": g["s5_bim"][None],
        "s5_c_re": g["s5_cre"][None], "s5_c_im": g["s5_cim"][None], "s5_d": g["s5_d"], "final_norm_g": g["final_g"][0],
    }


def _finish(a, g, dx, loss, res, packed, rs, ffn_lands, dev, nb, s, d):
    landed = {}
    for group, names, h in rs:
        if not group.startswith("ffn_"):
            landed.update(zip(names, _xchg_wait("rs_wait_" + group, h, [dx])))
    kinds = {"ffn_w1": "lead", "ffn_w3": "lead", "ffn_w2": "lead", "w_in": "cols", "w_out": "rows", "glu_a": "rows",
             "glu_b": "rows"}

    def update(n, short):
        if short in ("ffn_w1", "ffn_w3"):
            res[n] = _adamw_minor_d("adamw_" + n, a[n], a["m_" + n], a["v_" + n], g[short], landed[short], dev)
        else:
            res[n] = _adamw("adamw_" + n, a[n], a["m_" + n], a["v_" + n], g[short],
                            landed[short].reshape((N_DEV - 1,) + a[n].shape), slot=(dev, kinds[short]))

    for n, short in zip(_BIG[3:], ("w_in", "w_out", "glu_a", "glu_b")):
        update(n, short)
    after = [dx, packed[0]] + [res[n][0] for n in _BIG[3:] + _MIDSIZE]
    ffn_names = ("ffn_w1", "ffn_w3", "ffn_w2")
    for group, names, h in rs:
        if group.startswith("ffn_") and len(names) == 3:
            ffn_lands[:] = _xchg_wait("rs_wait_" + group, h, after, ffn_lands)
    for k, n in enumerate(ffn_names):
        for group, names, h in rs:
            if group.startswith("ffn_") and names == [n]:
                (ffn_lands[k],) = _xchg_wait("rs_wait_" + group, h, after, [ffn_lands[k]])
        landed[n] = ffn_lands[k]
        update(n, n)
        after = after + [res[n][0]]

    out = [loss, dx.reshape(nb, s, d)]
    for k in range(4):
        out += [res[n][k] for n in _WEIGHTS]
    return tuple(out)
```

```python
import math

import numpy as np
import jax
import jax.numpy as jnp
from jax import lax
from jax.experimental import pallas as pl
from jax.experimental.pallas import tpu as pltpu

f32 = jnp.float32
bf16 = jnp.bfloat16

D_MODEL = 1024
N_DEV = 8
EPS = 1e-6
RET_HEADS = 4
HEAD_DIM = 128
RET_WIDTH = 512
RET_CHUNK = 128
ROPE_BASE = 10000.0
LRU_WIDTH = 512
LRU_BLOCKS = 4
LRU_C = 8.0
S5_GROUP = 16
S5_GROUPS = 64
S5_STATE = 64
S5_CHUNK = 1024
S5_BLOCKS = 8
S5_BLOCK_STATES = 512
SUBLANES = 8
D_FF = 2816
FF_SHARD = D_FF // N_DEV
FF_PAD = 384
IN_SHARD = 3072 // N_DEV
ADAM_LR = 0.001
ADAM_B1 = 0.9
ADAM_B2 = 0.999
ADAM_EPS = 1e-08
ADAM_WD = 0.01
ADAM_STEP = 10

VMEM_LIMIT = 56 * 1024 * 1024
VMEM_SPEC = pl.BlockSpec(memory_space=pltpu.VMEM)
ANY_SPEC = pl.BlockSpec(memory_space=pl.ANY)
HBM_SPEC = pl.BlockSpec(memory_space=pltpu.HBM)
SEM_SPEC = pl.BlockSpec(memory_space=pltpu.SEMAPHORE)
SIDE_EFFECT = pltpu.SideEffectType.DATAFLOW_SIDE_EFFECTING
MESH = pl.DeviceIdType.MESH


def _cp(*sem):
    return pltpu.CompilerParams(dimension_semantics=sem, vmem_limit_bytes=VMEM_LIMIT)


def _nn(a, b):
    return jnp.dot(a, b, preferred_element_type=f32)


def _nt(a, b):
    return lax.dot_general(a, b, (((1,), (1,)), ((), ())), preferred_element_type=f32)


def _tn(a, b):
    return lax.dot_general(a, b, (((0,), (0,)), ((), ())), preferred_element_type=f32)


def _rms_fwd(x, g):
    r = lax.rsqrt(jnp.mean(x * x, axis=-1, keepdims=True) + EPS)
    xn = x * r
    return xn * g, xn, r


def _rms_bwd(dh, xn, r, g):
    dxn = dh * g
    dx = r * (dxn - xn * jnp.mean(dxn * xn, axis=-1, keepdims=True))
    dg = jnp.sum(dh * xn, axis=0, keepdims=True)
    return dx, dg


def _shift_dn(v, d, row, fill=0.0):
    return jnp.where(row >= d, pltpu.roll(v, d, 0), fill)


def _shift_up(v, d, row, fill=0.0):
    n = v.shape[0]
    return jnp.where(row < n - d, pltpu.roll(v, n - d, 0), fill)


def _ew(name, fn, ins, outs, tm=512):
    t = ins[0].shape[0]
    n_in = len(ins)

    def body(*refs):
        res = fn(*[r[...] for r in refs[:n_in]])
        for o, v in zip(refs[n_in:], res):
            o[...] = v.astype(o.dtype)

    return pl.pallas_call(
        body, name=name, grid=(t // tm,),
        in_specs=[pl.BlockSpec((tm, a.shape[1]), lambda i: (i, 0)) for a in ins],
        out_specs=[pl.BlockSpec((tm, n), lambda i: (i, 0)) for n, _ in outs],
        out_shape=[jax.ShapeDtypeStruct((t, n), dt) for n, dt in outs],
        compiler_params=_cp("parallel"),
    )(*ins)


def _mm(name, x, w, kind, extras=(), epilogue=None, outs=None, tm=512, tn=1024):
    t = x.shape[0]
    n = w.shape[1] if kind == "nn" else w.shape[0]
    tn = min(tn, n)
    outs = outs or [f32]
    n_ex = len(extras)

    def body(x_ref, w_ref, *refs):
        xb = x_ref[...].astype(bf16)
        acc = _nn(xb, w_ref[...]) if kind == "nn" else _nt(xb, w_ref[...])
        res = epilogue(acc, *[r[...] for r in refs[:n_ex]]) if epilogue else (acc,)
        for o, v in zip(refs[n_ex:], res):
            o[...] = v.astype(o.dtype)

    w_spec = (pl.BlockSpec((w.shape[0], tn), lambda i, j: (0, j)) if kind == "nn"
              else pl.BlockSpec((tn, w.shape[1]), lambda i, j: (j, 0)))
    tile = pl.BlockSpec((tm, tn), lambda i, j: (i, j))
    return pl.pallas_call(
        body, name=name, grid=(t // tm, n // tn),
        in_specs=[pl.BlockSpec((tm, x.shape[1]), lambda i, j: (i, 0)), w_spec] + [tile] * n_ex,
        out_specs=[tile] * len(outs),
        out_shape=[jax.ShapeDtypeStruct((t, n), dt) for dt in outs],
        compiler_params=_cp("parallel", "parallel"),
    )(x, w, *extras)


def _mm_tn(name, x, y, tk=1024, tn=1024, tt=1024):
    t, k = x.shape
    n = y.shape[1]
    tk, tn, tt = min(tk, k), min(tn, n), min(tt, t)

    def body(x_ref, y_ref, o_ref, ob_ref):
        @pl.when(pl.program_id(2) == 0)
        def _():
            o_ref[...] = jnp.zeros_like(o_ref)
        o_ref[...] += _tn(x_ref[...].astype(bf16), y_ref[...].astype(bf16))

        @pl.when(pl.program_id(2) == pl.num_programs(2) - 1)
        def _():
            ob_ref[...] = o_ref[...].astype(bf16)

    out = pl.BlockSpec((tk, tn), lambda i, j, s: (i, j))
    return pl.pallas_call(
        body, name=name, grid=(k // tk, n // tn, t // tt),
        in_specs=[pl.BlockSpec((tt, tk), lambda i, j, s: (s, i)), pl.BlockSpec((tt, tn), lambda i, j, s: (s, j))],
        out_specs=[out, out],
        out_shape=[jax.ShapeDtypeStruct((k, n), f32), jax.ShapeDtypeStruct((k, n), bf16)],
        compiler_params=_cp("parallel", "parallel", "arbitrary"),
    )(x, y)


def _norm_fwd(name, x, g, dtype, tm=512):
    t, d = x.shape

    def body(x_ref, g_ref, h_ref):
        h_ref[...] = _rms_fwd(x_ref[...], g_ref[...])[0].astype(dtype)

    row = pl.BlockSpec((tm, d), lambda i: (i, 0))
    return pl.pallas_call(
        body, name=name, grid=(t // tm,),
        in_specs=[row, pl.BlockSpec((1, d), lambda i: (0, 0))],
        out_specs=row, out_shape=jax.ShapeDtypeStruct((t, d), dtype),
        compiler_params=_cp("parallel"),
    )(x, g)


def _norm_bwd(name, dh, x, g, dres, tm=512):
    t, d = x.shape

    def body(dh_ref, x_ref, g_ref, dres_ref, dx_ref, dg_ref):
        gv = g_ref[...]
        _, xn, r = _rms_fwd(x_ref[...], gv)
        dx, dg = _rms_bwd(dh_ref[...], xn, r, gv)
        dx_ref[...] = dres_ref[...] + dx

        @pl.when(pl.program_id(0) == 0)
        def _():
            dg_ref[...] = jnp.zeros_like(dg_ref)
        dg_ref[...] += dg

    row = pl.BlockSpec((tm, d), lambda i: (i, 0))
    vec = pl.BlockSpec((1, d), lambda i: (0, 0))
    return pl.pallas_call(
        body, name=name, grid=(t // tm,),
        in_specs=[row, row, vec, row],
        out_specs=[row, vec],
        out_shape=[jax.ShapeDtypeStruct((t, d), f32), jax.ShapeDtypeStruct((1, d), f32)],
        compiler_params=_cp("arbitrary"),
    )(dh, x, g, dres)


def _final_loss(x, g, target, tm=512):
    t, d = x.shape

    def body(x_ref, g_ref, t_ref, loss_ref, dx_ref, dg_ref):
        gv = g_ref[...]
        y, xn, r = _rms_fwd(x_ref[...], gv)
        err = y - t_ref[...]
        dy = err * (1.0 / d)
        dx, dg = _rms_bwd(dy, xn, r, gv)
        dx_ref[...] = dx

        @pl.when(pl.program_id(0) == 0)
        def _():
            dg_ref[...] = jnp.zeros_like(dg_ref)
            loss_ref[...] = jnp.zeros_like(loss_ref)
        dg_ref[...] += dg
        loss_ref[...] += jnp.full((1, 128), 0.5 / d, f32) * jnp.sum(err * err)

    row = pl.BlockSpec((tm, d), lambda i: (i, 0))
    vec = pl.BlockSpec((1, d), lambda i: (0, 0))
    return pl.pallas_call(
        body, name="final_loss", grid=(t // tm,),
        in_specs=[row, vec, row],
        out_specs=[pl.BlockSpec((1, 128), lambda i: (0, 0)), row, vec],
        out_shape=[jax.ShapeDtypeStruct((1, 128), f32), jax.ShapeDtypeStruct((t, d), f32),
                   jax.ShapeDtypeStruct((1, d), f32)],
        compiler_params=_cp("arbitrary"),
    )(x, g, target)


def _load_ffn_weights(hbm_refs, vmem_refs, sems):
    @pl.when(pl.program_id(0) == 0)
    def _():
        copies = []
        for k, (src, dst) in enumerate(zip(hbm_refs, vmem_refs)):
            for j in range(N_DEV):
                half = pl.ds((j % 2) * FF_PAD, FF_PAD)
                window = dst.at[j // 2, half, :]
                copies.append(pltpu.make_async_copy(src.at[j], window, sems.at[k * N_DEV + j]))
        for cp in copies:
            cp.start()
        for cp in copies:
            cp.wait()


def _ffn_weight_scratch(nj, d, ff):
    return [pltpu.VMEM((nj, ff, d), bf16), pltpu.VMEM((nj, ff, d), bf16), pltpu.VMEM((nj, ff, d), bf16),
            pltpu.SemaphoreType.DMA((3 * N_DEV,))]


def _ffn_fwd(name, x, g, w1, w3, w2, tm=512):
    t, d = x.shape
    nj, ff = N_DEV // 2, 2 * FF_PAD

    def body(x_ref, g_ref, w1_hbm, w3_hbm, w2_hbm, y_ref, a_ref, b_ref, w1_ref, w3_ref, w2_ref, sems):
        _load_ffn_weights((w1_hbm, w3_hbm, w2_hbm), (w1_ref, w3_ref, w2_ref), sems)
        xv = x_ref[...]
        h, _, _ = _rms_fwd(xv, g_ref[...])
        hb = h.astype(bf16)
        acc = jnp.zeros((tm, d), f32)
        for j in range(nj):
            a = _nt(hb, w1_ref[j])
            b = _nt(hb, w3_ref[j])
            a_ref[j] = a.astype(bf16)
            b_ref[j] = b.astype(bf16)
            u = (a * jax.nn.sigmoid(a) * b).astype(bf16)
            acc = acc + _nn(u, w2_ref[j])
        y_ref[...] = xv + 0.5 * acc

    row = pl.BlockSpec((tm, d), lambda i: (i, 0))
    mid = pl.BlockSpec((nj, tm, ff), lambda i: (0, i, 0))
    return pl.pallas_call(
        body, name=name, grid=(t // tm,),
        in_specs=[row, pl.BlockSpec((1, d), lambda i: (0, 0)), ANY_SPEC, ANY_SPEC, ANY_SPEC],
        out_specs=[row, mid, mid],
        out_shape=[jax.ShapeDtypeStruct((t, d), f32), jax.ShapeDtypeStruct((nj, t, ff), bf16),
                   jax.ShapeDtypeStruct((nj, t, ff), bf16)],
        scratch_shapes=_ffn_weight_scratch(nj, d, ff),
        compiler_params=_cp("arbitrary"),
    )(x, g, w1, w3, w2)


def _ffn_dx(name, dy, x, g, w1, w3, w2, a, b, tm=256):
    t, d = x.shape
    nj, ff = N_DEV // 2, 2 * FF_PAD

    def body(dy_ref, x_ref, g_ref, w1_hbm, w3_hbm, w2_hbm, a_ref, b_ref,
             dx_ref, dg_ref, hbt_ref, dyh_ref, ut_ref, da_ref, db_ref, w1_ref, w3_ref, w2_ref, sems):
        _load_ffn_weights((w1_hbm, w3_hbm, w2_hbm), (w1_ref, w3_ref, w2_ref), sems)
        gv = g_ref[...]
        h, xn, r = _rms_fwd(x_ref[...], gv)
        hbt_ref[...] = h.astype(bf16).T
        dyv = dy_ref[...]
        dyh = (0.5 * dyv).astype(bf16)
        dyh_ref[...] = dyh
        dh = jnp.zeros((tm, d), f32)
        dus = [_nt(dyh, w2_ref[j]) for j in range(nj)]
        for j in range(nj):
            av = a_ref[j].astype(f32)
            bv = b_ref[j].astype(f32)
            s = jax.nn.sigmoid(av)
            silu = av * s
            ut_ref[j] = (silu * bv).astype(bf16).T
            du = dus[j]
            dab = (du * bv * (s * (1.0 + av * (1.0 - s)))).astype(bf16)
            dbb = (du * silu).astype(bf16)
            da_ref[j] = dab
            db_ref[j] = dbb
            dh = dh + _nn(dab, w1_ref[j]) + _nn(dbb, w3_ref[j])
        dx, dg = _rms_bwd(dh, xn, r, gv)
        dx_ref[...] = dyv + dx

        @pl.when(pl.program_id(0) == 0)
        def _():
            dg_ref[...] = jnp.zeros_like(dg_ref)
        dg_ref[...] += dg

    row = pl.BlockSpec((tm, d), lambda i: (i, 0))
    vec = pl.BlockSpec((1, d), lambda i: (0, 0))
    mid = pl.BlockSpec((nj, tm, ff), lambda i: (0, i, 0))
    mid_shape = jax.ShapeDtypeStruct((nj, t, ff), bf16)
    return pl.pallas_call(
        body, name=name, grid=(t // tm,),
        in_specs=[row, row, vec, ANY_SPEC, ANY_SPEC, ANY_SPEC, mid, mid],
        out_specs=[row, vec, pl.BlockSpec((d, tm), lambda i: (0, i)), row,
                   pl.BlockSpec((nj, ff, tm), lambda i: (0, 0, i)), mid, mid],
        out_shape=[jax.ShapeDtypeStruct((t, d), f32), jax.ShapeDtypeStruct((1, d), f32),
                   jax.ShapeDtypeStruct((d, t), bf16), jax.ShapeDtypeStruct((t, d), bf16),
                   jax.ShapeDtypeStruct((nj, ff, t), bf16), mid_shape, mid_shape],
        scratch_shapes=_ffn_weight_scratch(nj, d, ff),
        compiler_params=_cp("arbitrary"),
    )(dy, x, g, w1, w3, w2, a, b)


def _ffn_dw(name, xt, ys, bufs, l, h, tt=2048):
    n = len(ys)
    t = ys[0].shape[-2]
    tt = min(tt, t)
    cut_cols = xt.ndim == 2

    def body(x_ref, *refs):
        y_refs, outs, accs = refs[:n], refs[2 * n:4 * n], refs[4 * n:]
        s = pl.program_id(1)
        xv = x_ref[0] if xt.ndim == 3 else x_ref[...]
        for k in range(n):
            prod = _nn(xv, y_refs[k][0] if ys[k].ndim == 3 else y_refs[k][...])

            @pl.when(s == 0)
            def _():
                accs[k][...] = prod

            @pl.when(s > 0)
            def _():
                accs[k][...] += prod

        @pl.when(s == pl.num_programs(1) - 1)
        def _():
            for k in range(n):
                total = accs[k][...]
                for e in range(2):
                    lo = e * FF_PAD
                    part = total[:, lo:lo + FF_SHARD] if cut_cols else total[lo:lo + FF_SHARD, :]
                    outs[k][e] = part
                    outs[n + k][e] = part.astype(bf16)

    x_spec = (pl.BlockSpec((1, xt.shape[1], tt), lambda p, s: (p, 0, s)) if xt.ndim == 3
              else pl.BlockSpec((xt.shape[0], tt), lambda p, s: (0, s)))
    y_specs = [pl.BlockSpec((1, tt, y.shape[2]), lambda p, s: (p, s, 0)) if y.ndim == 3
               else pl.BlockSpec((tt, y.shape[1]), lambda p, s: (s, 0)) for y in ys]
    dims = [b.shape[-2:] for b in bufs]
    outs = pl.pallas_call(
        body, name=name, grid=(N_DEV // 2, t // tt),
        in_specs=[x_spec] + y_specs + [ANY_SPEC] * n,
        out_specs=[pl.BlockSpec((2, None, None, k_, n_), lambda p, s: (p, l, h, 0, 0)) for k_, n_ in dims]
        + [pl.BlockSpec((2, k_, n_), lambda p, s: (p, 0, 0)) for k_, n_ in dims],
        out_shape=[jax.ShapeDtypeStruct(b.shape, b.dtype) for b in bufs]
        + [jax.ShapeDtypeStruct((N_DEV, k_, n_), bf16) for k_, n_ in dims],
        input_output_aliases={1 + n + k: k for k in range(n)},
        scratch_shapes=[pltpu.VMEM((xt.shape[-2], y.shape[-1]), f32) for y in ys],
        compiler_params=_cp("parallel", "arbitrary"),
    )(xt, *ys, *bufs)
    return outs[:n], outs[n:]


_LOG_GAMMA = [float(np.log1p(-np.float32(2.0) ** np.float32(-5.0 - h))) for h in range(RET_HEADS)]


def _ret_consts(h):
    lg = jnp.where(h == 0, _LOG_GAMMA[0], jnp.where(h == 1, _LOG_GAMMA[1],
                   jnp.where(h == 2, _LOG_GAMMA[2], _LOG_GAMMA[3]))).astype(f32)
    c = RET_CHUNK
    r = lax.broadcasted_iota(jnp.int32, (c, c), 0)
    cc = lax.broadcasted_iota(jnp.int32, (c, c), 1)
    decay = jnp.where(r >= cc, jnp.exp(lg * jnp.maximum((r - cc).astype(f32), 0.0)), 0.0)
    pos = lax.broadcasted_iota(jnp.int32, (c, 1), 0).astype(f32)
    kd = jnp.exp(lg * (c - 1.0 - pos))
    qd = jnp.exp(lg * (pos + 1.0))
    gc = jnp.exp(lg * c)
    return decay, kd, qd, gc


def _rope(x, cos, sin):
    return x * cos + pltpu.roll(x, HEAD_DIM // 2, 1) * sin


def _rope_t(g, cos, sin):
    return g * cos + pltpu.roll(g * sin, HEAD_DIM // 2, 1)


def _rope_tables(s):
    half = HEAD_DIM // 2
    inv = ROPE_BASE ** (-jnp.arange(half, dtype=f32) / half)
    ang = jnp.arange(s, dtype=f32)[:, None] * inv[None, :]
    cos, sin = jnp.cos(ang), jnp.sin(ang)
    return jnp.concatenate([cos, cos], axis=1), jnp.concatenate([-sin, sin], axis=1)


def _head_ln(o):
    mu = jnp.mean(o, axis=-1, keepdims=True)
    oc = o - mu
    rs = lax.rsqrt(jnp.mean(oc * oc, axis=-1, keepdims=True) + EPS)
    return oc * rs, rs


def _ret_fwd(proj, cos, sin, ret_g, nb, s):
    c = RET_CHUNK
    nc = s // c
    t = nb * s
    scale = HEAD_DIM ** -0.5

    def body(q_ref, k_ref, v_ref, gate_ref, cos_ref, sin_ref, g_ref, o_ref, rprev_ref, m_ref):
        decay, kd, qd, gc = _ret_consts(pl.program_id(0))
        gv = g_ref[...]

        def chunk(b, n, rv):
            rows = pl.ds(pl.multiple_of(b * s + n * c, c), c)
            pos = pl.ds(pl.multiple_of(n * c, c), c)
            cs, sn = cos_ref[pos, :], sin_ref[pos, :]
            q = _rope(q_ref[rows, :], cs, sn)
            k = _rope(k_ref[rows, :], cs, sn) * scale
            vb = v_ref[rows, :].astype(bf16)
            sc = _nt(q.astype(bf16), k.astype(bf16)) * decay
            rprev_ref[b, n] = rv
            o = _nn(sc.astype(bf16), vb) + _nn((q * qd).astype(bf16), rv.astype(bf16))
            o_ref[rows, :] = o
            y, _ = _head_ln(o)
            gate = gate_ref[rows, :]
            m_ref[rows, :] = y * gv * (gate * jax.nn.sigmoid(gate))
            return rv * gc + _tn((k * kd).astype(bf16), vb)

        def step(n, carry):
            return tuple(chunk(b, n, carry[b]) for b in range(nb))

        lax.fori_loop(0, nc, step, (jnp.zeros((HEAD_DIM, HEAD_DIM), f32),) * nb)

    def col(off):
        return pl.BlockSpec((t, HEAD_DIM), lambda h: (0, off + h))

    tab = pl.BlockSpec((s, HEAD_DIM), lambda h: (0, 0))
    return pl.pallas_call(
        body, name="ret_fwd", grid=(RET_HEADS,),
        in_specs=[col(0), col(4), col(8), col(12), tab, tab, pl.BlockSpec((1, HEAD_DIM), lambda h: (0, h))],
        out_specs=[col(0), pl.BlockSpec((nb, None, nc, HEAD_DIM, HEAD_DIM), lambda h: (0, h, 0, 0, 0)), col(0)],
        out_shape=[jax.ShapeDtypeStruct((t, RET_WIDTH), f32),
                   jax.ShapeDtypeStruct((nb, RET_HEADS, nc, HEAD_DIM, HEAD_DIM), f32),
                   jax.ShapeDtypeStruct((t, RET_WIDTH), f32)],
        compiler_params=_cp("parallel"),
    )(proj, proj, proj, proj, cos, sin, ret_g)


def _ret_bwd(dmerged, o_raw, rprev, proj, cos, sin, ret_g, nb, s):
    c = RET_CHUNK
    nc = s // c
    t = nb * s
    scale = HEAD_DIM ** -0.5

    def body(dm_ref, o_ref, rprev_ref, q_ref, k_ref, v_ref, gate_ref, cos_ref, sin_ref, g_ref,
             dq_ref, dk_ref, dv_ref, dgate_ref, dg_ref):
        decay, kd, qd, gc = _ret_consts(pl.program_id(0))
        gv = g_ref[...]

        def chunk(b, n, drn, dg):
            rows = pl.ds(pl.multiple_of(b * s + n * c, c), c)
            pos = pl.ds(pl.multiple_of(n * c, c), c)
            cs, sn = cos_ref[pos, :], sin_ref[pos, :]
            q = _rope(q_ref[rows, :], cs, sn)
            k = _rope(k_ref[rows, :], cs, sn) * scale
            qb, kb = q.astype(bf16), k.astype(bf16)
            vb = v_ref[rows, :].astype(bf16)
            sc = _nt(qb, kb) * decay
            y, rs = _head_ln(o_ref[rows, :])
            gate = gate_ref[rows, :]
            sg = jax.nn.sigmoid(gate)
            silu = gate * sg
            dm = dm_ref[rows, :]
            dgate_ref[rows, :] = dm * y * gv * (sg * (1.0 + gate * (1.0 - sg)))
            dyl = dm * gv * silu
            dg = dg + jnp.sum(dm * y * silu, axis=0, keepdims=True)
            do = rs * (dyl - jnp.mean(dyl, axis=-1, keepdims=True) - y * jnp.mean(dyl * y, axis=-1, keepdims=True))
            dob = do.astype(bf16)
            rv = rprev_ref[b, n]
            drb = drn.astype(bf16)
            ds = (_nt(dob, vb) * decay).astype(bf16)
            kdb = (k * kd).astype(bf16)
            qdb = (q * qd).astype(bf16)
            dq_r = _nn(ds, kb) + _nt(dob, rv.astype(bf16)) * qd
            dk_r = _tn(ds, qb) + _nt(vb, drb) * kd
            dv_ref[rows, :] = _tn(sc.astype(bf16), dob) + _nn(kdb, drb)
            dq_ref[rows, :] = _rope_t(dq_r, cs, sn)
            dk_ref[rows, :] = _rope_t(dk_r * scale, cs, sn)
            return drn * gc + _tn(qdb, dob), dg

        def step(i, carry):
            out = [chunk(b, nc - 1 - i, *carry[b]) for b in range(nb)]
            return tuple(out)

        zero = (jnp.zeros((HEAD_DIM, HEAD_DIM), f32), jnp.zeros((1, HEAD_DIM), f32))
        done = lax.fori_loop(0, nc, step, (zero,) * nb)
        dg_ref[...] = sum(dg for _, dg in done)

    def col(off):
        return pl.BlockSpec((t, HEAD_DIM), lambda h: (0, off + h))

    tab = pl.BlockSpec((s, HEAD_DIM), lambda h: (0, 0))
    gsp = pl.BlockSpec((1, HEAD_DIM), lambda h: (0, h))
    out_t = jax.ShapeDtypeStruct((t, RET_WIDTH), f32)
    return pl.pallas_call(
        body, name="ret_bwd", grid=(RET_HEADS,),
        in_specs=[col(0), col(0), pl.BlockSpec((nb, None, nc, HEAD_DIM, HEAD_DIM), lambda h: (0, h, 0, 0, 0)),
                  col(0), col(4), col(8), col(12), tab, tab, gsp],
        out_specs=[col(0), col(0), col(0), col(0), gsp],
        out_shape=[out_t, out_t, out_t, out_t, jax.ShapeDtypeStruct((1, RET_WIDTH), f32)],
        compiler_params=_cp("parallel"),
    )(dmerged, o_raw, rprev, proj, proj, proj, proj, cos, sin, ret_g)


def _neg_expm1(z):
    series = -(z * (1.0 + z * (0.5 + z * (1.0 / 6.0 + z * (1.0 / 24.0)))))
    return jnp.where(z > -0.01, series, 1.0 - jnp.exp(z))


def _lru_gates(xc, pa, pi, lam):
    r = jax.nn.sigmoid(pa)
    i = jax.nn.sigmoid(pi)
    log_a = -LRU_C * r * jax.nn.softplus(-lam)
    a = jnp.exp(log_a)
    bx = jnp.sqrt(_neg_expm1(2.0 * log_a)) * i * xc
    return a, bx


def _scan_rows(a, b, row, up):
    sub = row[:SUBLANES] & (SUBLANES - 1)
    groups = list(range(a.shape[0] // SUBLANES))
    out = [None] * len(groups)
    edge = slice(0, 1) if up else slice(SUBLANES - 1, SUBLANES)
    carry = jnp.zeros((1, a.shape[1]), f32)
    for g in (reversed(groups) if up else groups):
        rows = slice(g * SUBLANES, (g + 1) * SUBLANES)
        xa, xb = a[rows], b[rows]
        d = 1
        while d < SUBLANES:
            keep = (sub < SUBLANES - d) if up else (sub >= d)
            shift = SUBLANES - d if up else d
            xb = xa * jnp.where(keep, pltpu.roll(xb, shift, 0), 0.0) + xb
            xa = xa * jnp.where(keep, pltpu.roll(xa, shift, 0), 1.0)
            d *= 2
        out[g] = xb + xa * carry
        carry = out[g][edge]
    return jnp.concatenate(out, axis=0)


def _scan_fwd(a, b, row):
    return _scan_rows(a, b, row, False)


def _scan_bwd(c, b, row):
    return _scan_rows(c, b, row, True)


def _conv_fwd(x, cw, cb, row):
    return (cb + cw[3:4] * x + cw[2:3] * _shift_dn(x, 1, row) + cw[1:2] * _shift_dn(x, 2, row)
            + cw[0:1] * _shift_dn(x, 3, row))


def _lru_specs(s, order):
    def im(f):
        return (lambda b, g: f(b, g)) if order == "bg" else (lambda g, b: f(b, g))
    seq = lambda off: pl.BlockSpec((s, 128), im(lambda b, g: (b, off + g)))
    vec = pl.BlockSpec((1, 128), im(lambda b, g: (0, g)))
    cw = pl.BlockSpec((4, 128), im(lambda b, g: (0, g)))
    mat = pl.BlockSpec((1, 128, 128), im(lambda b, g: (g, 0, 0)))
    return seq, vec, cw, mat


def _lru_fwd(proj, conv_w, conv_b, w_a, b_a, w_i, b_i, lam, nb, s):
    def body(x_ref, gt_ref, cw_ref, cb_ref, wa_ref, ba_ref, wi_ref, bi_ref, lam_ref, out_ref, h_ref):
        row = lax.broadcasted_iota(jnp.int32, (s, 128), 0)
        xc = _conv_fwd(x_ref[...], cw_ref[...], cb_ref[...], row)
        xcb = xc.astype(bf16)
        pa = _nn(xcb, wa_ref[0].astype(bf16)) + ba_ref[...]
        pi = _nn(xcb, wi_ref[0].astype(bf16)) + bi_ref[...]
        a, bx = _lru_gates(xc, pa, pi, lam_ref[...])
        h = _scan_fwd(a, bx, row)
        h_ref[...] = h
        out_ref[...] = h * jax.nn.gelu(gt_ref[...])

    seq, vec, cw, mat = _lru_specs(s, "bg")
    out = jax.ShapeDtypeStruct((nb * s, LRU_WIDTH), f32)
    return pl.pallas_call(
        body, name="lru_fwd", grid=(nb, LRU_BLOCKS),
        in_specs=[seq(16), seq(20), cw, vec, mat, vec, mat, vec, vec],
        out_specs=[seq(0), seq(0)], out_shape=[out, out],
        compiler_params=_cp("parallel", "parallel"),
    )(proj, proj, conv_w, conv_b, w_a, b_a, w_i, b_i, lam)


def _lru_bwd(dmerged, states, proj, conv_w, conv_b, w_a, b_a, w_i, b_i, lam, nb, s):
    def body(dout_ref, h_ref, x_ref, gt_ref, cw_ref, cb_ref, wa_ref, ba_ref, wi_ref, bi_ref, lam_ref,
             dx_ref, dgt_ref, dcw_ref, dcb_ref, dwa_ref, dba_ref, dwi_ref, dbi_ref, dlam_ref):
        row = lax.broadcasted_iota(jnp.int32, (s, 128), 0)
        x = x_ref[...]
        cwv = cw_ref[...]
        xc = _conv_fwd(x, cwv, cb_ref[...], row)
        xcb = xc.astype(bf16)
        wab, wib = wa_ref[0].astype(bf16), wi_ref[0].astype(bf16)
        pa = _nn(xcb, wab) + ba_ref[...]
        pi = _nn(xcb, wib) + bi_ref[...]
        (a, _), gates_vjp = jax.vjp(_lru_gates, xc, pa, pi, lam_ref[...])
        h = h_ref[...]
        ge, gelu_vjp = jax.vjp(jax.nn.gelu, gt_ref[...])
        dout = dout_ref[...]
        dgt_ref[...] = gelu_vjp(dout * h)[0]
        adj = _scan_bwd(_shift_up(a, 1, row), dout * ge, row)
        dxc, dpa, dpi, dlam = gates_vjp((adj * _shift_dn(h, 1, row), adj))
        dpab, dpib = dpa.astype(bf16), dpi.astype(bf16)
        dxc = dxc + _nt(dpab, wab) + _nt(dpib, wib)
        dx_ref[...] = (cwv[3:4] * dxc + cwv[2:3] * _shift_up(dxc, 1, row) + cwv[1:2] * _shift_up(dxc, 2, row)
                       + cwv[0:1] * _shift_up(dxc, 3, row))

        @pl.when(pl.program_id(1) == 0)
        def _():
            for r in (dcw_ref, dcb_ref, dwa_ref, dba_ref, dwi_ref, dbi_ref, dlam_ref):
                r[...] = jnp.zeros_like(r)
        rsum = lambda v: jnp.sum(v, axis=0, keepdims=True)
        dcw_ref[...] += jnp.concatenate([rsum(dxc * _shift_dn(x, 3, row)), rsum(dxc * _shift_dn(x, 2, row)),
                                         rsum(dxc * _shift_dn(x, 1, row)), rsum(dxc * x)], axis=0)
        dcb_ref[...] += rsum(dxc)
        dwa_ref[0] += _tn(xcb, dpab)
        dwi_ref[0] += _tn(xcb, dpib)
        dba_ref[...] += rsum(dpa)
        dbi_ref[...] += rsum(dpi)
        dlam_ref[...] += dlam

    seq, vec, cw, mat = _lru_specs(s, "gb")
    t = nb * s
    vshape = jax.ShapeDtypeStruct((1, LRU_WIDTH), f32)
    mshape = jax.ShapeDtypeStruct((LRU_BLOCKS, 128, 128), f32)
    return pl.pallas_call(
        body, name="lru_bwd", grid=(LRU_BLOCKS, nb),
        in_specs=[seq(4), seq(0), seq(16), seq(20), cw, vec, mat, vec, mat, vec, vec],
        out_specs=[seq(0), seq(0), cw, vec, mat, vec, mat, vec, vec],
        out_shape=[jax.ShapeDtypeStruct((t, LRU_WIDTH), f32), jax.ShapeDtypeStruct((t, LRU_WIDTH), f32),
                   jax.ShapeDtypeStruct((4, LRU_WIDTH), f32), vshape, mshape, vshape, mshape, vshape, vshape],
        compiler_params=_cp("parallel", "arbitrary"),
    )(dmerged, states, proj, proj, conv_w, conv_b, w_a, b_a, w_i, b_i, lam)


def _s5_disc(lr, li, ldt, bre, bim):
    dt = jnp.exp(ldt)
    mag = jnp.exp(lr * dt)
    lbr = mag * jnp.cos(li * dt)
    lbi = mag * jnp.sin(li * dt)
    den = lr * lr + li * li
    nr = lbr - 1.0
    fr = (nr * lr + lbi * li) / den
    fi = (lbi * lr - nr * li) / den
    bbr = fr[:, None, :] * bre - fi[:, None, :] * bim
    bbi = fr[:, None, :] * bim + fi[:, None, :] * bre
    return lbr, lbi, bbr, bbi


def _s5_prep(lr, li, ldt, bre, bim):
    def body(lr_ref, li_ref, ldt_ref, bre_ref, bim_ref, o1, o2, o3, o4):
        o1[...], o2[...], o3[...], o4[...] = _s5_disc(lr_ref[...], li_ref[...], ldt_ref[...], bre_ref[...], bim_ref[...])

    return pl.pallas_call(
        body, name="s5_prep", in_specs=[VMEM_SPEC] * 5, out_specs=[VMEM_SPEC] * 4,
        out_shape=[jax.ShapeDtypeStruct(lr.shape, f32), jax.ShapeDtypeStruct(lr.shape, f32),
                   jax.ShapeDtypeStruct(bre.shape, f32), jax.ShapeDtypeStruct(bre.shape, f32)],
    )(lr, li, ldt, bre, bim)


def _s5_prep_bwd(lr, li, ldt, bre, bim, cts):
    def body(lr_ref, li_ref, ldt_ref, bre_ref, bim_ref, g1, g2, g3, g4, o1, o2, o3, o4, o5):
        _, vjp = jax.vjp(_s5_disc, lr_ref[...], li_ref[...], ldt_ref[...], bre_ref[...], bim_ref[...])
        o1[...], o2[...], o3[...], o4[...], o5[...] = vjp((g1[...], g2[...], g3[...], g4[...]))

    return pl.pallas_call(
        body, name="s5_prep_bwd", in_specs=[VMEM_SPEC] * 9, out_specs=[VMEM_SPEC] * 5,
        out_shape=[jax.ShapeDtypeStruct(v.shape, f32) for v in (lr, li, ldt, bre, bim)],
    )(lr, li, ldt, bre, bim, *cts)


def _cmul(ar, ai, br, bi):
    return ar * br - ai * bi, ar * bi + ai * br


def _s5_pow_table(lr, li, n, row, up):
    ar = jnp.broadcast_to(lr, (n, lr.shape[1]))
    ai = jnp.broadcast_to(li, (n, li.shape[1]))
    shift = _shift_up if up else _shift_dn
    d = 1
    while d < n:
        ar, ai = _cmul(ar, ai, shift(ar, d, row, 1.0), shift(ai, d, row, 0.0))
        d *= 2
    return ar, ai


def _s5_step_factors(lr, li, row, up):
    sub = row & (SUBLANES - 1)
    out, pr, pi, d = [], lr, li, 1
    while d < SUBLANES:
        keep = (sub < SUBLANES - d) if up else (sub >= d)
        out.append((jnp.where(keep, pr, 0.0), jnp.where(keep, pi, 0.0)))
        pr, pi = _cmul(pr, pi, pr, pi)
        d *= 2
    return out


def _s5_scan(br, bi, steps, tab_r, tab_i, cr, ci, up):
    groups = list(range(br.shape[0] // SUBLANES))
    out_r, out_i = [None] * len(groups), [None] * len(groups)
    edge = slice(0, 1) if up else slice(SUBLANES - 1, SUBLANES)
    for g in (reversed(groups) if up else groups):
        rows = slice(g * SUBLANES, (g + 1) * SUBLANES)
        xr, xi = br[rows], bi[rows]
        for k, (mr, mi) in enumerate(steps):
            shift = SUBLANES - (1 << k) if up else 1 << k
            tr, ti = _cmul(mr, mi, pltpu.roll(xr, shift, 0), pltpu.roll(xi, shift, 0))
            xr, xi = xr + tr, xi + ti
        tr, ti = _cmul(tab_r, tab_i, cr, ci)
        hr, hi = xr + tr, xi + ti
        out_r[g], out_i[g] = hr, hi
        cr, ci = hr[edge], hi[edge]
    return jnp.concatenate(out_r, axis=0), jnp.concatenate(out_i, axis=0)


def _s5_specs(t, nb, nc):
    seq = pl.BlockSpec((t, 128), lambda k: (0, k))
    lvec = pl.BlockSpec((1, S5_BLOCK_STATES), lambda k: (0, k))
    dvec = pl.BlockSpec((1, 128), lambda k: (0, k))
    wmat = pl.BlockSpec((1, 128, S5_BLOCK_STATES), lambda k: (k, 0, 0))
    h0 = pl.BlockSpec((nb, None, nc, 2, S5_BLOCK_STATES), lambda k: (0, k, 0, 0, 0))
    states = pl.BlockSpec((t, S5_BLOCK_STATES), lambda k: (0, k))
    return seq, lvec, dvec, wmat, h0, states


def _s5_fwd(u, lbr, lbi, wbr, wbi, wcr, wci, dskip, nb, s):
    ln = min(S5_CHUNK, s)
    nc = s // ln

    def body(u_ref, lr_ref, li_ref, wbr_ref, wbi_ref, wcr_ref, wci_ref, d_ref, yg_ref, y_ref, h0_ref, hr_ref, hi_ref):
        row = lax.broadcasted_iota(jnp.int32, (ln, S5_BLOCK_STATES), 0)
        lr, li = lr_ref[...], li_ref[...]
        pr, pi = _s5_pow_table(lr, li, SUBLANES, row[:SUBLANES], False)
        steps = _s5_step_factors(lr, li, row[:SUBLANES], False)
        dv = d_ref[...]

        def chunk(b, n, h0r, h0i):
            st = pl.multiple_of(b * s + n * ln, ln)
            uc = u_ref[pl.ds(st, ln), :]
            ub = uc.astype(bf16)
            hr, hi = _s5_scan(_nn(ub, wbr_ref[0]), _nn(ub, wbi_ref[0]), steps, pr, pi, h0r, h0i, False)
            h0_ref[b, n, 0:1, :] = h0r
            h0_ref[b, n, 1:2, :] = h0i
            hrb, hib = hr.astype(bf16), hi.astype(bf16)
            hr_ref[pl.ds(st, ln), :] = hrb
            hi_ref[pl.ds(st, ln), :] = hib
            y = _nt(hrb, wcr_ref[0]) - _nt(hib, wci_ref[0]) + dv * uc
            y_ref[pl.ds(st, ln), :] = y
            yg_ref[pl.ds(st, ln), :] = jax.nn.gelu(y).astype(bf16)
            return hr[ln - 1:ln, :], hi[ln - 1:ln, :]

        def step(n, carry):
            return tuple(chunk(b, n, *carry[b]) for b in range(nb))

        z = jnp.zeros((1, S5_BLOCK_STATES), f32)
        lax.fori_loop(0, nc, step, ((z, z),) * nb)

    t = nb * s
    seq, lvec, dvec, wmat, h0, states = _s5_specs(t, nb, nc)
    return pl.pallas_call(
        body, name="s5_fwd", grid=(S5_BLOCKS,),
        in_specs=[seq, lvec, lvec, wmat, wmat, wmat, wmat, dvec],
        out_specs=[seq, seq, h0, states, states],
        out_shape=[jax.ShapeDtypeStruct((t, D_MODEL), bf16), jax.ShapeDtypeStruct((t, D_MODEL), f32),
                   jax.ShapeDtypeStruct((nb, S5_BLOCKS, nc, 2, S5_BLOCK_STATES), f32),
                   jax.ShapeDtypeStruct((t, S5_BLOCKS * S5_BLOCK_STATES), bf16),
                   jax.ShapeDtypeStruct((t, S5_BLOCKS * S5_BLOCK_STATES), bf16)],
        compiler_params=_cp("parallel"),
    )(u, lbr, lbi, wbr, wbi, wcr, wci, dskip)


def _s5_bwd(dyg, y, u, h0, hrs, his, lbr, lbi, wbr, wbi, wcr, wci, dskip, nb, s):
    ln = min(S5_CHUNK, s)
    nc = s // ln

    def body(dyg_ref, y_ref, u_ref, h0_ref, hr_ref, hi_ref, lr_ref, li_ref, wbr_ref, wbi_ref, wcr_ref, wci_ref, d_ref,
             du_ref, dlr_ref, dli_ref, dwbr_ref, dwbi_ref, dwcr_ref, dwci_ref, dd_ref):
        for r in (dlr_ref, dli_ref, dwbr_ref, dwbi_ref, dwcr_ref, dwci_ref, dd_ref):
            r[...] = jnp.zeros_like(r)
        row = lax.broadcasted_iota(jnp.int32, (ln, S5_BLOCK_STATES), 0)
        lr, li = lr_ref[...], li_ref[...]
        qr, qi = _s5_pow_table(lr, -li, SUBLANES, row[:SUBLANES], True)
        steps_up = _s5_step_factors(lr, -li, row[:SUBLANES], True)
        dv = d_ref[...]
        rsum = lambda v: jnp.sum(v, axis=0, keepdims=True)

        def chunk(b, n, gnr, gni):
            st = pl.multiple_of(b * s + n * ln, ln)
            uc = u_ref[pl.ds(st, ln), :]
            ub = uc.astype(bf16)
            h0v = h0_ref[b, n]
            h0r, h0i = h0v[0:1], h0v[1:2]
            hrb, hib = hr_ref[pl.ds(st, ln), :], hi_ref[pl.ds(st, ln), :]
            hr, hi = hrb.astype(f32), hib.astype(f32)
            dy = jax.vjp(jax.nn.gelu, y_ref[pl.ds(st, ln), :])[1](dyg_ref[pl.ds(st, ln), :])[0]
            dyb = dy.astype(bf16)
            dd_ref[...] += rsum(dy * uc)
            gr, gi = _s5_scan(_nn(dyb, wcr_ref[0]), -_nn(dyb, wci_ref[0]), steps_up, qr, qi, gnr, gni, True)
            hpr = jnp.where(row >= 1, pltpu.roll(hr, 1, 0), h0r)
            hpi = jnp.where(row >= 1, pltpu.roll(hi, 1, 0), h0i)
            dlr_ref[...] += rsum(gr * hpr + gi * hpi)
            dli_ref[...] += rsum(gi * hpr - gr * hpi)
            grb, gib = gr.astype(bf16), gi.astype(bf16)
            dwbr_ref[0] += _tn(ub, grb)
            dwbi_ref[0] += _tn(ub, gib)
            dwcr_ref[0] += _tn(dyb, hrb)
            dwci_ref[0] -= _tn(dyb, hib)
            du_ref[pl.ds(st, ln), :] = _nt(grb, wbr_ref[0]) + _nt(gib, wbi_ref[0]) + dv * dy
            return gr[0:1, :], gi[0:1, :]

        def step(i, carry):
            return tuple(chunk(b, nc - 1 - i, *carry[b]) for b in range(nb))

        z = jnp.zeros((1, S5_BLOCK_STATES), f32)
        lax.fori_loop(0, nc, step, ((z, z),) * nb)

    t = nb * s
    seq, lvec, dvec, wmat, h0s, states = _s5_specs(t, nb, nc)
    lshape = jax.ShapeDtypeStruct((1, S5_BLOCKS * S5_BLOCK_STATES), f32)
    wshape = jax.ShapeDtypeStruct((S5_BLOCKS, 128, S5_BLOCK_STATES), f32)
    return pl.pallas_call(
        body, name="s5_bwd", grid=(S5_BLOCKS,),
        in_specs=[seq, seq, seq, h0s, states, states, lvec, lvec, wmat, wmat, wmat, wmat, dvec],
        out_specs=[seq, lvec, lvec, wmat, wmat, wmat, wmat, dvec],
        out_shape=[jax.ShapeDtypeStruct((t, D_MODEL), f32), lshape, lshape, wshape, wshape, wshape, wshape,
                   jax.ShapeDtypeStruct((1, D_MODEL), f32)],
        compiler_params=_cp("parallel"),
    )(dyg, y, u, h0, hrs, his, lbr, lbi, wbr, wbi, wcr, wci, dskip)


def _blockdiag(w):
    w4 = w.reshape(S5_BLOCKS, 8, S5_GROUP, S5_STATE)
    same_group = jnp.eye(8, dtype=bool)[None, :, None, :, None]
    return jnp.where(same_group, w4[:, :, :, None, :], 0.0).reshape(S5_BLOCKS, 128, S5_BLOCK_STATES)


def _blockdiag_t(dw):
    d5 = dw.reshape(S5_BLOCKS, 8, S5_GROUP, 8, S5_STATE)
    diag = jnp.diagonal(d5, axis1=1, axis2=3)
    return jnp.moveaxis(diag, 3, 1).reshape(S5_GROUPS, S5_GROUP, S5_STATE)


def _glu_fwd(ygb, wa, wb, x, tm=512, tn=1024):
    t, d = x.shape

    def body(y_ref, wa_ref, wb_ref, x_ref, o_ref, p_ref, q_ref):
        p = _nn(y_ref[...], wa_ref[...])
        q = _nn(y_ref[...], wb_ref[...])
        p_ref[...] = p
        q_ref[...] = q
        o_ref[...] = x_ref[...] + p * jax.nn.sigmoid(q)

    tile = pl.BlockSpec((tm, tn), lambda i, j: (i, j))
    wsp = pl.BlockSpec((d, tn), lambda i, j: (0, j))
    out = jax.ShapeDtypeStruct((t, d), f32)
    return pl.pallas_call(
        body, name="glu_fwd", grid=(t // tm, d // tn),
        in_specs=[pl.BlockSpec((tm, d), lambda i, j: (i, 0)), wsp, wsp, tile],
        out_specs=[tile, tile, tile], out_shape=[out, out, out],
        compiler_params=_cp("parallel", "parallel"),
    )(ygb, wa, wb, x)


def _place():
    x, y, c = lax.axis_index("x"), lax.axis_index("y"), lax.axis_index("c")
    return x, y, c, [(1 - x, y), (x, 1 - y), (1 - x, 1 - y)]


def _all_gather(name, arrays):
    n = len(arrays)

    def body(*refs):
        ins, outs = refs[:n], refs[n:2 * n]
        send_sems, recv_sems, local_sems = refs[2 * n:]
        x, y, c, chips = _place()
        me, sib = (x, y, c), (x, y, 1 - c)

        def copy(i, k, block, to, src=None):
            dst = outs[i].at[4 * block[0] + 2 * block[1] + block[2]]
            return pltpu.make_async_remote_copy(
                src_ref=dst if src is None else src, dst_ref=dst,
                send_sem=send_sems.at[i * 7 + k], recv_sem=recv_sems.at[i * 7 + k],
                device_id=to, device_id_type=MESH)

        mine = [pltpu.make_async_copy(ins[i], outs[i].at[4 * x + 2 * y + c], local_sems.at[i]) for i in range(n)]
        for m in mine:
            m.start()
        first = []
        for i in range(n):
            first.append(copy(i, 0, me, sib, src=ins[i]))
            first += [copy(i, 1 + j, me, (*chip, c), src=ins[i]) for j, chip in enumerate(chips)]
        for cp in first:
            cp.start()
        passed = []
        for j, chip in enumerate(chips):
            for i in range(n):
                copy(i, 1 + j, (*chip, c), me).wait_recv()
                fwd = copy(i, 4 + j, (*chip, c), sib)
                fwd.start()
                passed.append(fwd)
        for i in range(n):
            copy(i, 0, sib, me).wait_recv()
        for j, chip in enumerate(chips):
            for i in range(n):
                copy(i, 4 + j, (*chip, 1 - c), me).wait_recv()
        for cp in first + passed:
            cp.wait_send()
        for m in mine:
            m.wait()

    return pl.pallas_call(
        body, name=name,
        in_specs=[ANY_SPEC] * n, out_specs=[ANY_SPEC] * n,
        out_shape=[jax.ShapeDtypeStruct((N_DEV,) + a.shape, a.dtype) for a in arrays],
        scratch_shapes=[pltpu.SemaphoreType.DMA((7 * n,)), pltpu.SemaphoreType.DMA((7 * n,)),
                        pltpu.SemaphoreType.DMA((n,))],
    )(*arrays)


def _tie(name, x, deps):
    def body(*refs):
        pass

    return pl.pallas_call(
        body, name=name, in_specs=[ANY_SPEC] * (1 + len(deps)), out_specs=ANY_SPEC,
        out_shape=jax.ShapeDtypeStruct(x.shape, x.dtype), input_output_aliases={0: 0},
    )(x, *deps)


def _xchg_copies(kind, srcs, lands, suffixes, send_sems, recv_sems):
    x, y, c, _ = _place()
    copies = []
    for i, (src, land, sfx) in enumerate(zip(srcs, lands, suffixes)):
        for k in range(N_DEV - 1):
            r = k + 1
            peer = (1 - x if r & 4 else x, 1 - y if r & 2 else y, 1 - c if r & 1 else c)
            if kind == "gather":
                s_ref, d_ref = src, land.at[(4 * x + 2 * y + c,) + sfx]
            elif len(src.shape) > 2:
                s_ref, d_ref = src.at[4 * peer[0] + 2 * peer[1] + peer[2]], land.at[(k,) + sfx]
            else:
                width = land.shape[-1]
                first_col = pl.multiple_of((4 * peer[0] + 2 * peer[1] + peer[2]) * width, 128)
                s_ref, d_ref = src.at[:, pl.ds(first_col, width)], land.at[(k,) + sfx]
            copies.append(pltpu.make_async_remote_copy(
                src_ref=s_ref, dst_ref=d_ref, send_sem=send_sems.at[i * 7 + k], recv_sem=recv_sems.at[i * 7 + k],
                device_id=peer, device_id_type=MESH))
    return copies


def _xchg_start(name, kind, srcs, lands, suffixes=None):
    n = len(srcs)
    suffixes = suffixes or [()] * n

    def body(*refs):
        src, land = refs[:n], refs[n:2 * n]
        send_sems, recv_sems, token = refs[2 * n], refs[2 * n + 1], refs[-1]
        for cp in _xchg_copies(kind, src, land, suffixes, send_sems, recv_sems):
            cp.start()
        token[...] = jnp.zeros_like(token)

    arrays = list(srcs) + list(lands)
    outs = pl.pallas_call(
        body, name=name,
        out_shape=(pltpu.SemaphoreType.DMA((7 * n,)), pltpu.SemaphoreType.DMA((7 * n,)),
                   *[pltpu.HBM(a.shape, a.dtype) for a in arrays], jax.ShapeDtypeStruct((8, 128), f32)),
        in_specs=[HBM_SPEC] * (2 * n),
        out_specs=(SEM_SPEC, SEM_SPEC, *[HBM_SPEC] * (2 * n), VMEM_SPEC),
        input_output_aliases={i: 2 + i for i in range(2 * n)},
        compiler_params=pltpu.CompilerParams(has_side_effects=SIDE_EFFECT),
    )(*[pltpu.with_memory_space_constraint(a, pltpu.HBM) for a in arrays])
    return dict(kind=kind, n=n, suffixes=suffixes, send=outs[0], recv=outs[1], srcs=list(outs[2:2 + n]),
                lands=list(outs[2 + n:2 + 2 * n]), token=outs[-1])


def _xchg_wait(name, h, after, lands=None):
    n = h["n"]
    lands = h["lands"] if lands is None else lands

    def body(*refs):
        src, land = refs[:n], refs[n:2 * n]
        for cp in _xchg_copies(h["kind"], src, land, h["suffixes"], refs[2 * n], refs[2 * n + 1]):
            cp.wait_send()
            cp.wait_recv()

    arrays = h["srcs"] + list(lands)
    outs = pl.pallas_call(
        body, name=name,
        out_shape=tuple(pltpu.HBM(a.shape, a.dtype) for a in arrays),
        in_specs=[HBM_SPEC] * (2 * n) + [SEM_SPEC, SEM_SPEC] + [ANY_SPEC] * len(after),
        out_specs=tuple([HBM_SPEC] * (2 * n)),
        input_output_aliases={i: i for i in range(2 * n)},
        compiler_params=pltpu.CompilerParams(has_side_effects=SIDE_EFFECT),
    )(*arrays, h["send"], h["recv"], *after)
    return list(outs[n:])


def _rows(a):
    return a.reshape(-1, a.shape[-1])


def _row_tile(r):
    for tm in (512, 256, 128, 64, 32, 16, 8):
        if r % tm == 0:
            return tm
    return r


def _sum8(name, gathered):
    _, r, n = gathered.shape
    tm = _row_tile(r)

    def body(g_ref, o_ref):
        acc = g_ref[0]
        for k in range(1, N_DEV):
            acc = acc + g_ref[k]
        o_ref[...] = acc

    return pl.pallas_call(
        body, name=name, grid=(r // tm,),
        in_specs=[pl.BlockSpec((N_DEV, tm, n), lambda i: (0, i, 0))],
        out_specs=pl.BlockSpec((tm, n), lambda i: (i, 0)),
        out_shape=jax.ShapeDtypeStruct((r, n), f32),
        compiler_params=_cp("parallel"),
    )(gathered)


def _adamw(name, w, m, v, own, landed=None, slot=None):
    shape = w.shape
    w2, m2, v2 = _rows(w), _rows(m), _rows(v)
    r, n = w2.shape
    tm = _row_tile(r)
    c1 = 1.0 - ADAM_B1 ** ADAM_STEP
    c2 = 1.0 - ADAM_B2 ** ADAM_STEP
    extra = [] if landed is None else [landed.reshape(landed.shape[0], r, n)]
    row = pl.BlockSpec((tm, n), lambda i, *_: (i, 0))
    if slot is None:
        o2, own_spec, scalars = _rows(own), row, []
    else:
        dev, kind = slot
        scalars = [dev.reshape(1).astype(jnp.int32)]
        if kind == "lead":
            o2, own_spec = own.reshape(N_DEV, r, n), pl.BlockSpec((None, tm, n), lambda i, d: (d[0], i, 0))
        elif kind == "rows":
            o2, own_spec = own, pl.BlockSpec((tm, n), lambda i, d: (d[0] * (r // tm) + i, 0))
        else:
            o2, own_spec = own, pl.BlockSpec((tm, n), lambda i, d: (i, d[0]))

    def body(*refs):
        w_ref, m_ref, v_ref, o_ref = refs[len(scalars):len(scalars) + 4]
        refs = refs[len(scalars) + 4:]
        g = o_ref[...]
        if extra:
            for k in range(extra[0].shape[0]):
                g = g + refs[0][k].astype(f32)
        g_ref, d_ref, mn_ref, vn_ref = refs[len(extra):]
        mn = ADAM_B1 * m_ref[...] + (1.0 - ADAM_B1) * g
        vn = ADAM_B2 * v_ref[...] + (1.0 - ADAM_B2) * (g * g)
        g_ref[...] = g
        d_ref[...] = -ADAM_LR * ((mn / c1) / (jnp.sqrt(vn / c2) + ADAM_EPS) + ADAM_WD * w_ref[...])
        mn_ref[...] = mn
        vn_ref[...] = vn

    outs = pl.pallas_call(
        body, name=name,
        grid_spec=pltpu.PrefetchScalarGridSpec(
            num_scalar_prefetch=len(scalars), grid=(r // tm,),
            in_specs=[row] * 3 + [own_spec] + [pl.BlockSpec((e.shape[0], tm, n), lambda i, *_: (0, i, 0)) for e in extra],
            out_specs=[row] * 4),
        out_shape=[jax.ShapeDtypeStruct((r, n), f32)] * 4,
        compiler_params=_cp("parallel"),
    )(*scalars, w2, m2, v2, o2, *extra)
    return [o.reshape(shape) for o in outs]


def _adamw_minor_d(name, w, m, v, own_all, landed, dev, tm=512):
    nl, nh, d, f = w.shape
    wt, mt, vt = (jnp.swapaxes(t, 2, 3) for t in (w, m, v))
    r = nl * nh * d
    per = d // tm
    c1 = 1.0 - ADAM_B1 ** ADAM_STEP
    c2 = 1.0 - ADAM_B2 ** ADAM_STEP

    def body(dev_ref, w_ref, m_ref, v_ref, o_ref, l_ref, g_ref, d_ref, mn_ref, vn_ref):
        g = o_ref[...]
        for k in range(N_DEV - 1):
            g = g + l_ref[k].astype(f32)
        g = g.T
        mn = ADAM_B1 * m_ref[...] + (1.0 - ADAM_B1) * g
        vn = ADAM_B2 * v_ref[...] + (1.0 - ADAM_B2) * (g * g)
        g_ref[...] = g
        d_ref[...] = -ADAM_LR * ((mn / c1) / (jnp.sqrt(vn / c2) + ADAM_EPS) + ADAM_WD * w_ref[...])
        mn_ref[...] = mn
        vn_ref[...] = vn

    par = pl.BlockSpec((None, None, f, tm), lambda i, _: (i // (nh * per), (i // per) % nh, 0, i % per))
    outs = pl.pallas_call(
        body, name=name,
        grid_spec=pltpu.PrefetchScalarGridSpec(
            num_scalar_prefetch=1, grid=(r // tm,),
            in_specs=[par, par, par, pl.BlockSpec((None, tm, f), lambda i, dv: (dv[0], i, 0)),
                      pl.BlockSpec((N_DEV - 1, tm, f), lambda i, _: (0, i, 0))],
            out_specs=[par] * 4),
        out_shape=[jax.ShapeDtypeStruct(wt.shape, f32)] * 4,
        compiler_params=_cp("parallel"),
    )(dev.reshape(1).astype(jnp.int32), wt, mt, vt, own_all.reshape(N_DEV, r, f), landed.reshape(N_DEV - 1, r, f))
    return [jnp.swapaxes(o, 2, 3) for o in outs]


def _pack(arrays):
    flat = jnp.concatenate([a.reshape(-1).astype(f32) for a in arrays])
    pad = (-flat.shape[0]) % (128 * (512 if flat.shape[0] > 128 * 512 else 8))
    return jnp.pad(flat, (0, pad)).reshape(-1, 128)


def _unpack(packed, shapes):
    flat = packed.reshape(-1)
    out, off = [], 0
    for s in shapes:
        n = math.prod(s)
        out.append(flat[off:off + n].reshape(s))
        off += n
    return out


def _local_step(x, target, w, weights_of, send, last_small, on_loss, nb, s):
    cos, sin = _rope_tables(s)
    g = {}
    ffn_saved = {}
    ffn_bufs = [lax.empty((N_DEV, 2, 2) + shp, f32)
                for shp in ((D_MODEL, FF_SHARD), (D_MODEL, FF_SHARD), (FF_SHARD, D_MODEL))]

    def ffn(xin, l, h, wts):
        y, a, b = _ffn_fwd(f"ffn_fwd_{l}{h}", xin, w["ffn_g"][l][h], *wts)
        ffn_saved[(l, h)] = (xin, a, b, wts)
        return y

    def ffn_back(dy, l, h):
        xin, a, b, wts = ffn_saved[(l, h)]
        dx, dg, hb, dyh, u, da, db = _ffn_dx(f"ffn_dx_{l}{h}", dy, xin, w["ffn_g"][l][h], *wts, a, b)
        g[f"ffn_g_{l}{h}"] = dg
        if (l, h) != (0, 0):
            ffn_bufs[:2], (h1, h3) = _ffn_dw(f"ffn_dw_{l}{h}_w13", hb, [da, db], ffn_bufs[:2], l, h)
            ffn_bufs[2:], (h2,) = _ffn_dw(f"ffn_dw_{l}{h}_w2", u, [dyh], ffn_bufs[2:], l, h)
            return send(f"ffn_{l}{h}", {"ffn_w1": h1, "ffn_w3": h3, "ffn_w2": h2}, dx)
        hb = last_small(g, hb)
        ffn_bufs[:1], (half,) = _ffn_dw("ffn_dw_00_w1", hb, [da], ffn_bufs[:1], l, h)
        hb = send("ffn_00_w1", {"ffn_w1": half}, hb)
        ffn_bufs[1:2], (half,) = _ffn_dw("ffn_dw_00_w3", hb, [db], ffn_bufs[1:2], l, h)
        u = send("ffn_00_w3", {"ffn_w3": half}, u)
        ffn_bufs[2:], (half,) = _ffn_dw("ffn_dw_00_w2", u, [dyh], ffn_bufs[2:], l, h)
        return send("ffn_00_w2", {"ffn_w2": half}, dx)

    def slots(t):
        return t.reshape(N_DEV, D_MODEL // N_DEV, D_MODEL)

    x1 = ffn(x, 0, 0, weights_of(0, [])["ffn"])
    wg = weights_of(1, [x1])
    w_in, w_out = wg["w_in"], wg["w_out"]
    h0b = _norm_fwd("mix_norm_0", x1, w["mix_g"][0], bf16)
    proj = _mm("in_proj", h0b, w_in, "nn", tn=1536)[0]
    o_raw, rprev, mret = _ret_fwd(proj, cos, sin, w["ret_g"], nb, s)
    lru, lru_h = _lru_fwd(proj, w["conv_w"], w["conv_b"], w["lru_w_a"], w["lru_b_a"], w["lru_w_i"], w["lru_b_i"], w["lru_lam"], nb, s)
    merged = _ew("merge", lambda a, b: (jnp.concatenate([a, b], axis=1),), [mret, lru], [(D_MODEL, bf16)])[0]
    x2 = _mm("out_proj", merged, w_out, "nn", extras=[x1], epilogue=lambda acc, r: (acc + r,))[0]
    x3 = ffn(x2, 0, 1, weights_of(2, [x2])["ffn"])
    x4 = ffn(x3, 1, 0, weights_of(3, [x3])["ffn"])
    u = _norm_fwd("mix_norm_1", x4, w["mix_g"][1], f32)
    lbr, lbi, bbr, bbi = _s5_prep(w["s5_lr"], w["s5_li"], w["s5_ldt"], w["s5_bre"], w["s5_bim"])
    lbr_f, lbi_f = lbr.reshape(1, -1), lbi.reshape(1, -1)
    wbr, wbi = _blockdiag(bbr).astype(bf16), _blockdiag(bbi).astype(bf16)
    wcr, wci = _blockdiag(w["s5_cre"]).astype(bf16), _blockdiag(w["s5_cim"]).astype(bf16)
    ygb, ypre, h0s, hrs, his = _s5_fwd(u, lbr_f, lbi_f, wbr, wbi, wcr, wci, w["s5_d"], nb, s)
    wg = weights_of(4, [ygb])
    glu_a, glu_b = wg["glu_a"], wg["glu_b"]
    x5, gp, gq = _glu_fwd(ygb, glu_a, glu_b, x4)
    x6 = ffn(x5, 1, 1, weights_of(5, [x5])["ffn"])
    loss, dx6, g["final_g"] = _final_loss(x6, w["final_g"], target)
    dx6 = on_loss(loss, dx6)

    dx5 = ffn_back(dx6, 1, 1)

    def glu_bwd(d, p, q):
        sg = jax.nn.sigmoid(q)
        return d * sg, d * p * sg * (1.0 - sg)

    dp, dq = _ew("glu_bwd", glu_bwd, [dx5, gp, gq], [(D_MODEL, bf16), (D_MODEL, bf16)])
    dyg = _mm("glu_dy_a", dp, glu_a, "nt")[0]
    dyg = _mm("glu_dy_b", dq, glu_b, "nt", extras=[dyg], epilogue=lambda acc, r: (acc + r,))[0]
    g["glu_a"], ga_half = _mm_tn("glu_dw_a", ygb, dp)
    g["glu_b"], gb_half = _mm_tn("glu_dw_b", ygb, dq)
    dyg = send("glu", {"glu_a": slots(ga_half), "glu_b": slots(gb_half)}, dyg)
    du, dlr, dli, dwbr, dwbi, dwcr, dwci, g["s5_d"] = _s5_bwd(dyg, ypre, u, h0s, hrs, his, lbr_f, lbi_f, wbr, wbi, wcr, wci, w["s5_d"], nb, s)
    g["s5_cre"], g["s5_cim"] = _blockdiag_t(dwcr), _blockdiag_t(dwci)
    g["s5_lr"], g["s5_li"], g["s5_ldt"], g["s5_bre"], g["s5_bim"] = _s5_prep_bwd(
        w["s5_lr"], w["s5_li"], w["s5_ldt"], w["s5_bre"], w["s5_bim"],
        (dlr.reshape(S5_GROUPS, S5_STATE), dli.reshape(S5_GROUPS, S5_STATE), _blockdiag_t(dwbr), _blockdiag_t(dwbi)))
    dx4, g["mix_g_1"] = _norm_bwd("mix_norm_1_bwd", du, x4, w["mix_g"][1], dx5)
    dx3 = ffn_back(dx4, 1, 0)
    dx2 = ffn_back(dx3, 0, 1)
    dmerged = _mm("out_proj_dx", dx2, w_out, "nt")[0]
    g["w_out"], wo_half = _mm_tn("out_proj_dw", merged, dx2)
    dmerged = send("w_out", {"w_out": slots(wo_half)}, dmerged)
    dq_, dk_, dv_, dgate, g["ret_g"] = _ret_bwd(dmerged, o_raw, rprev, proj, cos, sin, w["ret_g"], nb, s)
    (dxl, dgl, g["conv_w"], g["conv_b"], g["lru_w_a"], g["lru_b_a"], g["lru_w_i"], g["lru_b_i"], g["lru_lam"]) = _lru_bwd(
        dmerged, lru_h, proj, w["conv_w"], w["conv_b"], w["lru_w_a"], w["lru_b_a"], w["lru_w_i"], w["lru_b_i"], w["lru_lam"],
        nb, s)
    dproj = _ew("dproj", lambda *p: (jnp.concatenate(p, axis=1),), [dq_, dk_, dv_, dgate, dxl, dgl], [(3072, bf16)])[0]
    dh0 = _mm("in_proj_dx", dproj, w_in, "nt")[0]
    g["w_in"], wi_half = _mm_tn("in_proj_dw", h0b, dproj)
    dh0 = send("w_in", {"w_in": wi_half}, dh0)
    dx1, g["mix_g_0"] = _norm_bwd("mix_norm_0_bwd", dh0, x1, w["mix_g"][0], dx2)
    dx0 = ffn_back(dx1, 0, 0)
    g["ffn_w1"], g["ffn_w3"], g["ffn_w2"] = ffn_bufs
    return loss, dx0, g


_WEIGHTS = ["ffn_norm_g", "ffn_w1", "ffn_w3", "ffn_w2", "mix_norm_g", "w_in_even", "w_out_even", "ret_norm_g", "conv_w",
            "conv_b", "lru_w_a", "lru_b_a", "lru_w_i", "lru_b_i", "lru_lambda", "s5_lambda_re", "s5_lambda_im", "s5_log_dt",
            "s5_b_re", "s5_b_im", "s5_c_re", "s5_c_im", "s5_d", "glu_w_a", "glu_w_b", "final_norm_g"]
_BIG = ["ffn_w1", "ffn_w3", "ffn_w2", "w_in_even", "w_out_even", "glu_w_a", "glu_w_b"]
_SMALL_SHARDED = ["ffn_norm_g", "conv_w", "s5_d"]
_SMALL = [n for n in _WEIGHTS if n not in _BIG]
_MIDSIZE = ["lru_w_a", "lru_w_i", "s5_b_re", "s5_b_im", "s5_c_re", "s5_c_im"]


def kernel(x, ffn_norm_g, ffn_w1, ffn_w3, ffn_w2, mix_norm_g, w_in_even, w_out_even, ret_norm_g, conv_w, conv_b, lru_w_a, lru_b_a, lru_w_i, lru_b_i, lru_lambda, s5_lambda_re, s5_lambda_im, s5_log_dt, s5_b_re, s5_b_im, s5_c_re, s5_c_im, s5_d, glu_w_a, glu_w_b, final_norm_g, loss_target, m_ffn_norm_g, m_ffn_w1, m_ffn_w3, m_ffn_w2, m_mix_norm_g, m_w_in_even, m_w_out_even, m_ret_norm_g, m_conv_w, m_conv_b, m_lru_w_a, m_lru_b_a, m_lru_w_i, m_lru_b_i, m_lru_lambda, m_s5_lambda_re, m_s5_lambda_im, m_s5_log_dt, m_s5_b_re, m_s5_b_im, m_s5_c_re, m_s5_c_im, m_s5_d, m_glu_w_a, m_glu_w_b, m_final_norm_g, v_ffn_norm_g, v_ffn_w1, v_ffn_w3, v_ffn_w2, v_mix_norm_g, v_w_in_even, v_w_out_even, v_ret_norm_g, v_conv_w, v_conv_b, v_lru_w_a, v_lru_b_a, v_lru_w_i, v_lru_b_i, v_lru_lambda, v_s5_lambda_re, v_s5_lambda_im, v_s5_log_dt, v_s5_b_re, v_s5_b_im, v_s5_c_re, v_s5_c_im, v_s5_d, v_glu_w_a, v_glu_w_b, v_final_norm_g):
    a = dict(locals())
    nb, s, d = x.shape
    dev = 4 * lax.axis_index("x") + 2 * lax.axis_index("y") + lax.axis_index("c")

    def ffn_shards(l, h, pad):
        out = [jnp.swapaxes(ffn_w1[l, h], 0, 1).astype(bf16), jnp.swapaxes(ffn_w3[l, h], 0, 1).astype(bf16),
               ffn_w2[l, h].astype(bf16)]
        return [jnp.pad(t, ((0, FF_PAD - FF_SHARD), (0, 0))) for t in out] if pad else out

    first = _all_gather("ag_first", ffn_shards(0, 0, True) + [_pack([ffn_norm_g, conv_w, s5_d])])
    sm = first[3].reshape(N_DEV, -1)
    ffn_g_full = jnp.transpose(sm[:, :512].reshape(N_DEV, 2, 2, 128), (1, 2, 0, 3)).reshape(2, 2, D_MODEL)
    conv_w_full = jnp.transpose(sm[:, 512:768].reshape(N_DEV, 4, 64), (1, 0, 2)).reshape(4, LRU_WIDTH)
    s5_d_full = sm[:, 768:896].reshape(1, D_MODEL)

    ag_src = [None, [w_in_even[0].astype(bf16), w_out_even[0].astype(bf16)], ffn_shards(0, 1, False),
              ffn_shards(1, 0, False), [glu_w_a[0].astype(bf16), glu_w_b[0].astype(bf16)], ffn_shards(1, 1, False)]
    ag, token = [None], first[0]
    for k, grp in enumerate(ag_src):
        if grp is None:
            continue
        grp[0] = _tie(f"tie_ag_{k}", grp[0], [token])
        if grp[0].shape[0] == FF_SHARD:
            zero_rows = jnp.zeros((N_DEV, FF_PAD - FF_SHARD, D_MODEL), bf16)
            lands = [lax.dynamic_update_slice(
                lax.dynamic_update_slice(lax.empty((N_DEV, FF_PAD, D_MODEL), bf16), zero_rows, (0, FF_SHARD, 0)),
                t[None], (dev, 0, 0)) for t in grp]
            sfx = [(pl.ds(0, FF_SHARD),)] * len(grp)
        else:
            lands = [lax.dynamic_update_index_in_dim(lax.empty((N_DEV,) + t.shape, bf16), t, dev, 0) for t in grp]
            sfx = None
        ag.append(_xchg_start(f"ag_start_{k}", "gather", grp, lands, sfx))
        token = ag[-1]["token"]

    def weights_of(k, after):
        if k == 0:
            return {"ffn": [first[0], _tie("tie_ag_started", first[1], [h["token"] for h in ag[1:]]), first[2]]}
        got = _xchg_wait(f"ag_wait_{k}", ag[k], after)
        if k == 1:
            return {"w_in": jnp.transpose(got[0], (1, 0, 2)).reshape(D_MODEL, N_DEV * IN_SHARD),
                    "w_out": got[1].reshape(D_MODEL, D_MODEL)}
        if k == 4:
            return {"glu_a": got[0].reshape(D_MODEL, D_MODEL), "glu_b": got[1].reshape(D_MODEL, D_MODEL)}
        return {"ffn": got}

    ffn_lands = [lax.empty((N_DEV - 1, 2, 2) + shp, bf16)
                 for shp in ((D_MODEL, FF_SHARD), (D_MODEL, FF_SHARD), (FF_SHARD, D_MODEL))]
    rs = []

    ffn_names = ("ffn_w1", "ffn_w3", "ffn_w2")

    def send(group, arrays, carry):
        srcs = list(arrays.values())
        if group.startswith("ffn_"):
            which = [ffn_names.index(n) for n in arrays]
            sfx = [(int(group[4]), int(group[5]))] * len(which)
            h = _xchg_start("rs_start_" + group, "scatter", srcs, [ffn_lands[k] for k in which], sfx)
            for k, land in zip(which, h["lands"]):
                ffn_lands[k] = land
        elif group == "w_in":
            h = _xchg_start("rs_start_" + group, "scatter", srcs, [lax.empty((N_DEV - 1, D_MODEL, IN_SHARD), bf16)])
        else:
            h = _xchg_start("rs_start_" + group, "scatter", srcs,
                            [lax.empty((N_DEV - 1,) + t.shape[1:], bf16) for t in srcs])
        rs.append((group, list(arrays), h))
        return _tie("tie_" + group, carry, [h["token"]])

    w = {
        "ffn_g": [[ffn_g_full[l, h].reshape(1, D_MODEL) for h in range(2)] for l in range(2)],
        "mix_g": [mix_norm_g[0:1], mix_norm_g[1:2]],
        "ret_g": ret_norm_g, "conv_w": conv_w_full, "conv_b": conv_b,
        "lru_w_a": lru_w_a[0], "lru_b_a": lru_b_a, "lru_w_i": lru_w_i[0], "lru_b_i": lru_b_i, "lru_lam": lru_lambda,
        "s5_lr": s5_lambda_re[0], "s5_li": s5_lambda_im[0], "s5_ldt": s5_log_dt.reshape(S5_GROUPS, 1),
        "s5_bre": jnp.swapaxes(s5_b_re[0], 1, 2), "s5_bim": jnp.swapaxes(s5_b_im[0], 1, 2),
        "s5_cre": s5_c_re[0], "s5_cim": s5_c_im[0], "s5_d": s5_d_full,
        "final_g": final_norm_g.reshape(1, D_MODEL),
    }

    small_grads = {}

    def last_small(g, carry):
        part = _small_partials(g)
        mine = _pack([part[n] for n in _SMALL])
        land = lax.dynamic_update_index_in_dim(lax.empty((N_DEV,) + mine.shape, f32), mine, dev, 0)
        h = _xchg_start("ag_start_small_grads", "gather", [mine], [land])
        small_grads.update(h=h, shapes=[part[n].shape for n in _SMALL])
        return _tie("tie_small_grads", carry, [h["token"]])

    total_loss = []

    def on_loss(part, carry):
        total_loss.append(lax.psum(part[0, 0], ("x", "y", "c")))
        return _tie("tie_loss", carry, [jnp.broadcast_to(total_loss[0], (8, 128))])

    _, dx, g = _local_step(x.reshape(nb * s, d), loss_target.reshape(nb * s, d), w, weights_of, send, last_small,
                           on_loss, nb, s)
    loss = total_loss[0]
    (gath,) = _xchg_wait("ag_wait_small_grads", small_grads["h"], [dx])
    full = dict(zip(_SMALL, _unpack(_sum8("sum_small_grads", gath), small_grads["shapes"])))
    for n in _SMALL_SHARDED:
        width = a[n].shape[-1]
        full[n] = lax.dynamic_slice_in_dim(full[n], dev * width, width, axis=full[n].ndim - 1)
    res = {}
    for n in _MIDSIZE:
        if n.startswith("s5_b_"):
            swap = lambda t: jnp.swapaxes(t, 2, 3)
            res[n] = [swap(t) for t in _adamw("adamw_" + n, swap(a[n]), swap(a["m_" + n]), swap(a["v_" + n]), full[n])]
        else:
            res[n] = _adamw("adamw_" + n, a[n], a["m_" + n], a["v_" + n], full[n])
    tiny = [n for n in _SMALL if n not in _MIDSIZE]
    shapes = [a[n].shape for n in tiny]
    packed = _adamw("adamw_small", _pack([a[n] for n in tiny]), _pack([a["m_" + n] for n in tiny]),
                    _pack([a["v_" + n] for n in tiny]), _pack([full[n] for n in tiny]))
    res.update({n: vals for n, vals in zip(tiny, zip(*[_unpack(p, shapes) for p in packed]))})
    return _finish(a, g, dx, loss, res, packed, rs, ffn_lands, dev, nb, s, d)


def _small_partials(g):
    return {
        "ffn_norm_g": jnp.stack([jnp.stack([g[f"ffn_g_{l}{h}"][0] for h in range(2)]) for l in range(2)]),
        "mix_norm_g": jnp.concatenate([g["mix_g_0"], g["mix_g_1"]], axis=0),
        "ret_norm_g": g["ret_g"], "conv_w": g["conv_w"][None], "conv_b": g["conv_b"],
        "lru_w_a": g["lru_w_a"][None], "lru_b_a": g["lru_b_a"], "lru_w_i": g["lru_w_i"][None], "lru_b_i": g["lru_b_i"],
        "lru_lambda": g["lru_lam"], "s5_lambda_re": g["s5_lr"][None], "s5_lambda_im": g["s5_li"][None],
        "s5_log_dt": g["s5_ldt"].reshape(1, S5_GROUPS),
        "s5_b_re": g["s5_bre"][None], "s5_b_im": g["s5_bim"][None],
        "s5_c_re": g["s5_cre"][None], "s5_c_im": g["s5_cim"][None], "s5_d": g["s5_d"], "final_norm_g": g["final_g"][0],
    }


def _finish(a, g, dx, loss, res, packed, rs, ffn_lands, dev, nb, s, d):
    landed = {}
    for group, names, h in rs:
        if not group.startswith("ffn_"):
            landed.update(zip(names, _xchg_wait("rs_wait_" + group, h, [dx])))
    kinds = {"ffn_w1": "lead", "ffn_w3": "lead", "ffn_w2": "lead", "w_in": "cols", "w_out": "rows", "glu_a": "rows",
             "glu_b": "rows"}

    def update(n, short):
        if short in ("ffn_w1", "ffn_w3"):
            res[n] = _adamw_minor_d("adamw_" + n, a[n], a["m_" + n], a["v_" + n], g[short], landed[short], dev)
        else:
            res[n] = _adamw("adamw_" + n, a[n], a["m_" + n], a["v_" + n], g[short],
                            landed[short].reshape((N_DEV - 1,) + a[n].shape), slot=(dev, kinds[short]))

    for n, short in zip(_BIG[3:], ("w_in", "w_out", "glu_a", "glu_b")):
        update(n, short)
    after = [dx, packed[0]] + [res[n][0] for n in _BIG[3:] + _MIDSIZE]
    ffn_names = ("ffn_w1", "ffn_w3", "ffn_w2")
    for group, names, h in rs:
        if group.startswith("ffn_") and len(names) == 3:
            ffn_lands[:] = _xchg_wait("rs_wait_" + group, h, after, ffn_lands)
    for k, n in enumerate(ffn_names):
        for group, names, h in rs:
            if group.startswith("ffn_") and names == [n]:
                (ffn_lands[k],) = _xchg_wait("rs_wait_" + group, h, after, [ffn_lands[k]])
        landed[n] = ffn_lands[k]
        update(n, n)
        after = after + [res[n][0]]

    out = [loss, dx.reshape(nb, s, d)]
    for k in range(4):
        out += [res[n][k] for n in _WEIGHTS]
    return tuple(out)
```

```python
import math

import numpy as np
import jax
import jax.numpy as jnp
from jax import lax
from jax.experimental import pallas as pl
from jax.experimental.pallas import tpu as pltpu

f32 = jnp.float32
bf16 = jnp.bfloat16

D_MODEL = 1024
N_DEV = 8
EPS = 1e-6
RET_HEADS = 4
HEAD_DIM = 128
RET_WIDTH = 512
RET_CHUNK = 128
ROPE_BASE = 10000.0
LRU_WIDTH = 512
LRU_BLOCKS = 4
LRU_C = 8.0
S5_GROUP = 16
S5_GROUPS = 64
S5_STATE = 64
S5_CHUNK = 1024
S5_BLOCKS = 8
S5_BLOCK_STATES = 512
SUBLANES = 8
D_FF = 2816
FF_SHARD = D_FF // N_DEV
FF_PAD = 384
IN_SHARD = 3072 // N_DEV
ADAM_LR = 0.001
ADAM_B1 = 0.9
ADAM_B2 = 0.999
ADAM_EPS = 1e-08
ADAM_WD = 0.01
ADAM_STEP = 10

VMEM_LIMIT = 56 * 1024 * 1024
VMEM_SPEC = pl.BlockSpec(memory_space=pltpu.VMEM)
ANY_SPEC = pl.BlockSpec(memory_space=pl.ANY)
HBM_SPEC = pl.BlockSpec(memory_space=pltpu.HBM)
SEM_SPEC = pl.BlockSpec(memory_space=pltpu.SEMAPHORE)
SIDE_EFFECT = pltpu.SideEffectType.DATAFLOW_SIDE_EFFECTING
MESH = pl.DeviceIdType.MESH


def _cp(*sem):
    return pltpu.CompilerParams(dimension_semantics=sem, vmem_limit_bytes=VMEM_LIMIT)


def _nn(a, b):
    return jnp.dot(a, b, preferred_element_type=f32)


def _nt(a, b):
    return lax.dot_general(a, b, (((1,), (1,)), ((), ())), preferred_element_type=f32)


def _tn(a, b):
    return lax.dot_general(a, b, (((0,), (0,)), ((), ())), preferred_element_type=f32)


def _rms_fwd(x, g):
    r = lax.rsqrt(jnp.mean(x * x, axis=-1, keepdims=True) + EPS)
    xn = x * r
    return xn * g, xn, r


def _rms_bwd(dh, xn, r, g):
    dxn = dh * g
    dx = r * (dxn - xn * jnp.mean(dxn * xn, axis=-1, keepdims=True))
    dg = jnp.sum(dh * xn, axis=0, keepdims=True)
    return dx, dg


def _shift_dn(v, d, row, fill=0.0):
    return jnp.where(row >= d, pltpu.roll(v, d, 0), fill)


def _shift_up(v, d, row, fill=0.0):
    n = v.shape[0]
    return jnp.where(row < n - d, pltpu.roll(v, n - d, 0), fill)


def _ew(name, fn, ins, outs, tm=512):
    t = ins[0].shape[0]
    n_in = len(ins)

    def body(*refs):
        res = fn(*[r[...] for r in refs[:n_in]])
        for o, v in zip(refs[n_in:], res):
            o[...] = v.astype(o.dtype)

    return pl.pallas_call(
        body, name=name, grid=(t // tm,),
        in_specs=[pl.BlockSpec((tm, a.shape[1]), lambda i: (i, 0)) for a in ins],
        out_specs=[pl.BlockSpec((tm, n), lambda i: (i, 0)) for n, _ in outs],
        out_shape=[jax.ShapeDtypeStruct((t, n), dt) for n, dt in outs],
        compiler_params=_cp("parallel"),
    )(*ins)


def _mm(name, x, w, kind, extras=(), epilogue=None, outs=None, tm=512, tn=1024):
    t = x.shape[0]
    n = w.shape[1] if kind == "nn" else w.shape[0]
    tn = min(tn, n)
    outs = outs or [f32]
    n_ex = len(extras)

    def body(x_ref, w_ref, *refs):
        xb = x_ref[...].astype(bf16)
        acc = _nn(xb, w_ref[...]) if kind == "nn" else _nt(xb, w_ref[...])
        res = epilogue(acc, *[r[...] for r in refs[:n_ex]]) if epilogue else (acc,)
        for o, v in zip(refs[n_ex:], res):
            o[...] = v.astype(o.dtype)

    w_spec = (pl.BlockSpec((w.shape[0], tn), lambda i, j: (0, j)) if kind == "nn"
              else pl.BlockSpec((tn, w.shape[1]), lambda i, j: (j, 0)))
    tile = pl.BlockSpec((tm, tn), lambda i, j: (i, j))
    return pl.pallas_call(
        body, name=name, grid=(t // tm, n // tn),
        in_specs=[pl.BlockSpec((tm, x.shape[1]), lambda i, j: (i, 0)), w_spec] + [tile] * n_ex,
        out_specs=[tile] * len(outs),
        out_shape=[jax.ShapeDtypeStruct((t, n), dt) for dt in outs],
        compiler_params=_cp("parallel", "parallel"),
    )(x, w, *extras)


def _mm_tn(name, x, y, tk=1024, tn=1024, tt=1024):
    t, k = x.shape
    n = y.shape[1]
    tk, tn, tt = min(tk, k), min(tn, n), min(tt, t)

    def body(x_ref, y_ref, o_ref, ob_ref):
        @pl.when(pl.program_id(2) == 0)
        def _():
            o_ref[...] = jnp.zeros_like(o_ref)
        o_ref[...] += _tn(x_ref[...].astype(bf16), y_ref[...].astype(bf16))

        @pl.when(pl.program_id(2) == pl.num_programs(2) - 1)
        def _():
            ob_ref[...] = o_ref[...].astype(bf16)

    out = pl.BlockSpec((tk, tn), lambda i, j, s: (i, j))
    return pl.pallas_call(
        body, name=name, grid=(k // tk, n // tn, t // tt),
        in_specs=[pl.BlockSpec((tt, tk), lambda i, j, s: (s, i)), pl.BlockSpec((tt, tn), lambda i, j, s: (s, j))],
        out_specs=[out, out],
        out_shape=[jax.ShapeDtypeStruct((k, n), f32), jax.ShapeDtypeStruct((k, n), bf16)],
        compiler_params=_cp("parallel", "parallel", "arbitrary"),
    )(x, y)


def _norm_fwd(name, x, g, dtype, tm=512):
    t, d = x.shape

    def body(x_ref, g_ref, h_ref):
        h_ref[...] = _rms_fwd(x_ref[...], g_ref[...])[0].astype(dtype)

    row = pl.BlockSpec((tm, d), lambda i: (i, 0))
    return pl.pallas_call(
        body, name=name, grid=(t // tm,),
        in_specs=[row, pl.BlockSpec((1, d), lambda i: (0, 0))],
        out_specs=row, out_shape=jax.ShapeDtypeStruct((t, d), dtype),
        compiler_params=_cp("parallel"),
    )(x, g)


def _norm_bwd(name, dh, x, g, dres, tm=512):
    t, d = x.shape

    def body(dh_ref, x_ref, g_ref, dres_ref, dx_ref, dg_ref):
        gv = g_ref[...]
        _, xn, r = _rms_fwd(x_ref[...], gv)
        dx, dg = _rms_bwd(dh_ref[...], xn, r, gv)
        dx_ref[...] = dres_ref[...] + dx

        @pl.when(pl.program_id(0) == 0)
        def _():
            dg_ref[...] = jnp.zeros_like(dg_ref)
        dg_ref[...] += dg

    row = pl.BlockSpec((tm, d), lambda i: (i, 0))
    vec = pl.BlockSpec((1, d), lambda i: (0, 0))
    return pl.pallas_call(
        body, name=name, grid=(t // tm,),
        in_specs=[row, row, vec, row],
        out_specs=[row, vec],
        out_shape=[jax.ShapeDtypeStruct((t, d), f32), jax.ShapeDtypeStruct((1, d), f32)],
        compiler_params=_cp("arbitrary"),
    )(dh, x, g, dres)


def _final_loss(x, g, target, tm=512):
    t, d = x.shape

    def body(x_ref, g_ref, t_ref, loss_ref, dx_ref, dg_ref):
        gv = g_ref[...]
        y, xn, r = _rms_fwd(x_ref[...], gv)
        err = y - t_ref[...]
        dy = err * (1.0 / d)
        dx, dg = _rms_bwd(dy, xn, r, gv)
        dx_ref[...] = dx

        @pl.when(pl.program_id(0) == 0)
        def _():
            dg_ref[...] = jnp.zeros_like(dg_ref)
            loss_ref[...] = jnp.zeros_like(loss_ref)
        dg_ref[...] += dg
        loss_ref[...] += jnp.full((1, 128), 0.5 / d, f32) * jnp.sum(err * err)

    row = pl.BlockSpec((tm, d), lambda i: (i, 0))
    vec = pl.BlockSpec((1, d), lambda i: (0, 0))
    return pl.pallas_call(
        body, name="final_loss", grid=(t // tm,),
        in_specs=[row, vec, row],
        out_specs=[pl.BlockSpec((1, 128), lambda i: (0, 0)), row, vec],
        out_shape=[jax.ShapeDtypeStruct((1, 128), f32), jax.ShapeDtypeStruct((t, d), f32),
                   jax.ShapeDtypeStruct((1, d), f32)],
        compiler_params=_cp("arbitrary"),
    )(x, g, target)


def _load_ffn_weights(hbm_refs, vmem_refs, sems):
    @pl.when(pl.program_id(0) == 0)
    def _():
        copies = []
        for k, (src, dst) in enumerate(zip(hbm_refs, vmem_refs)):
            for j in range(N_DEV):
                half = pl.ds((j % 2) * FF_PAD, FF_PAD)
                window = dst.at[j // 2, half, :]
                copies.append(pltpu.make_async_copy(src.at[j], window, sems.at[k * N_DEV + j]))
        for cp in copies:
            cp.start()
        for cp in copies:
            cp.wait()


def _ffn_weight_scratch(nj, d, ff):
    return [pltpu.VMEM((nj, ff, d), bf16), pltpu.VMEM((nj, ff, d), bf16), pltpu.VMEM((nj, ff, d), bf16),
            pltpu.SemaphoreType.DMA((3 * N_DEV,))]


def _ffn_fwd(name, x, g, w1, w3, w2, tm=512):
    t, d = x.shape
    nj, ff = N_DEV // 2, 2 * FF_PAD

    def body(x_ref, g_ref, w1_hbm, w3_hbm, w2_hbm, y_ref, a_ref, b_ref, w1_ref, w3_ref, w2_ref, sems):
        _load_ffn_weights((w1_hbm, w3_hbm, w2_hbm), (w1_ref, w3_ref, w2_ref), sems)
        xv = x_ref[...]
        h, _, _ = _rms_fwd(xv, g_ref[...])
        hb = h.astype(bf16)
        acc = jnp.zeros((tm, d), f32)
        for j in range(nj):
            a = _nt(hb, w1_ref[j])
            b = _nt(hb, w3_ref[j])
            a_ref[j] = a.astype(bf16)
            b_ref[j] = b.astype(bf16)
            u = (a * jax.nn.sigmoid(a) * b).astype(bf16)
            acc = acc + _nn(u, w2_ref[j])
        y_ref[...] = xv + 0.5 * acc

    row = pl.BlockSpec((tm, d), lambda i: (i, 0))
    mid = pl.BlockSpec((nj, tm, ff), lambda i: (0, i, 0))
    return pl.pallas_call(
        body, name=name, grid=(t // tm,),
        in_specs=[row, pl.BlockSpec((1, d), lambda i: (0, 0)), ANY_SPEC, ANY_SPEC, ANY_SPEC],
        out_specs=[row, mid, mid],
        out_shape=[jax.ShapeDtypeStruct((t, d), f32), jax.ShapeDtypeStruct((nj, t, ff), bf16),
                   jax.ShapeDtypeStruct((nj, t, ff), bf16)],
        scratch_shapes=_ffn_weight_scratch(nj, d, ff),
        compiler_params=_cp("arbitrary"),
    )(x, g, w1, w3, w2)


def _ffn_dx(name, dy, x, g, w1, w3, w2, a, b, tm=256):
    t, d = x.shape
    nj, ff = N_DEV // 2, 2 * FF_PAD

    def body(dy_ref, x_ref, g_ref, w1_hbm, w3_hbm, w2_hbm, a_ref, b_ref,
             dx_ref, dg_ref, hbt_ref, dyh_ref, ut_ref, da_ref, db_ref, w1_ref, w3_ref, w2_ref, sems):
        _load_ffn_weights((w1_hbm, w3_hbm, w2_hbm), (w1_ref, w3_ref, w2_ref), sems)
        gv = g_ref[...]
        h, xn, r = _rms_fwd(x_ref[...], gv)
        hbt_ref[...] = h.astype(bf16).T
        dyv = dy_ref[...]
        dyh = (0.5 * dyv).astype(bf16)
        dyh_ref[...] = dyh
        dh = jnp.zeros((tm, d), f32)
        dus = [_nt(dyh, w2_ref[j]) for j in range(nj)]
        for j in range(nj):
            av = a_ref[j].astype(f32)
            bv = b_ref[j].astype(f32)
            s = jax.nn.sigmoid(av)
            silu = av * s
            ut_ref[j] = (silu * bv).astype(bf16).T
            du = dus[j]
            dab = (du * bv * (s * (1.0 + av * (1.0 - s)))).astype(bf16)
            dbb = (du * silu).astype(bf16)
            da_ref[j] = dab
            db_ref[j] = dbb
            dh = dh + _nn(dab, w1_ref[j]) + _nn(dbb, w3_ref[j])
        dx, dg = _rms_bwd(dh, xn, r, gv)
        dx_ref[...] = dyv + dx

        @pl.when(pl.program_id(0) == 0)
        def _():
            dg_ref[...] = jnp.zeros_like(dg_ref)
        dg_ref[...] += dg

    row = pl.BlockSpec((tm, d), lambda i: (i, 0))
    vec = pl.BlockSpec((1, d), lambda i: (0, 0))
    mid = pl.BlockSpec((nj, tm, ff), lambda i: (0, i, 0))
    mid_shape = jax.ShapeDtypeStruct((nj, t, ff), bf16)
    return pl.pallas_call(
        body, name=name, grid=(t // tm,),
        in_specs=[row, row, vec, ANY_SPEC, ANY_SPEC, ANY_SPEC, mid, mid],
        out_specs=[row, vec, pl.BlockSpec((d, tm), lambda i: (0, i)), row,
                   pl.BlockSpec((nj, ff, tm), lambda i: (0, 0, i)), mid, mid],
        out_shape=[jax.ShapeDtypeStruct((t, d), f32), jax.ShapeDtypeStruct((1, d), f32),
                   jax.ShapeDtypeStruct((d, t), bf16), jax.ShapeDtypeStruct((t, d), bf16),
                   jax.ShapeDtypeStruct((nj, ff, t), bf16), mid_shape, mid_shape],
        scratch_shapes=_ffn_weight_scratch(nj, d, ff),
        compiler_params=_cp("arbitrary"),
    )(dy, x, g, w1, w3, w2, a, b)


def _ffn_dw(name, xt, ys, bufs, l, h, tt=2048):
    n = len(ys)
    t = ys[0].shape[-2]
    tt = min(tt, t)
    cut_cols = xt.ndim == 2

    def body(x_ref, *refs):
        y_refs, outs, accs = refs[:n], refs[2 * n:4 * n], refs[4 * n:]
        s = pl.program_id(1)
        xv = x_ref[0] if xt.ndim == 3 else x_ref[...]
        for k in range(n):
            prod = _nn(xv, y_refs[k][0] if ys[k].ndim == 3 else y_refs[k][...])

            @pl.when(s == 0)
            def _():
                accs[k][...] = prod

            @pl.when(s > 0)
            def _():
                accs[k][...] += prod

        @pl.when(s == pl.num_programs(1) - 1)
        def _():
            for k in range(n):
                total = accs[k][...]
                for e in range(2):
                    lo = e * FF_PAD
                    part = total[:, lo:lo + FF_SHARD] if cut_cols else total[lo:lo + FF_SHARD, :]
                    outs[k][e] = part
                    outs[n + k][e] = part.astype(bf16)

    x_spec = (pl.BlockSpec((1, xt.shape[1], tt), lambda p, s: (p, 0, s)) if xt.ndim == 3
              else pl.BlockSpec((xt.shape[0], tt), lambda p, s: (0, s)))
    y_specs = [pl.BlockSpec((1, tt, y.shape[2]), lambda p, s: (p, s, 0)) if y.ndim == 3
               else pl.BlockSpec((tt, y.shape[1]), lambda p, s: (s, 0)) for y in ys]
    dims = [b.shape[-2:] for b in bufs]
    outs = pl.pallas_call(
        body, name=name, grid=(N_DEV // 2, t // tt),
        in_specs=[x_spec] + y_specs + [ANY_SPEC] * n,
        out_specs=[pl.BlockSpec((2, None, None, k_, n_), lambda p, s: (p, l, h, 0, 0)) for k_, n_ in dims]
        + [pl.BlockSpec((2, k_, n_), lambda p, s: (p, 0, 0)) for k_, n_ in dims],
        out_shape=[jax.ShapeDtypeStruct(b.shape, b.dtype) for b in bufs]
        + [jax.ShapeDtypeStruct((N_DEV, k_, n_), bf16) for k_, n_ in dims],
        input_output_aliases={1 + n + k: k for k in range(n)},
        scratch_shapes=[pltpu.VMEM((xt.shape[-2], y.shape[-1]), f32) for y in ys],
        compiler_params=_cp("parallel", "arbitrary"),
    )(xt, *ys, *bufs)
    return outs[:n], outs[n:]


_LOG_GAMMA = [float(np.log1p(-np.float32(2.0) ** np.float32(-5.0 - h))) for h in range(RET_HEADS)]


def _ret_consts(h):
    lg = jnp.where(h == 0, _LOG_GAMMA[0], jnp.where(h == 1, _LOG_GAMMA[1],
                   jnp.where(h == 2, _LOG_GAMMA[2], _LOG_GAMMA[3]))).astype(f32)
    c = RET_CHUNK
    r = lax.broadcasted_iota(jnp.int32, (c, c), 0)
    cc = lax.broadcasted_iota(jnp.int32, (c, c), 1)
    decay = jnp.where(r >= cc, jnp.exp(lg * jnp.maximum((r - cc).astype(f32), 0.0)), 0.0)
    pos = lax.broadcasted_iota(jnp.int32, (c, 1), 0).astype(f32)
    kd = jnp.exp(lg * (c - 1.0 - pos))
    qd = jnp.exp(lg * (pos + 1.0))
    gc = jnp.exp(lg * c)
    return decay, kd, qd, gc


def _rope(x, cos, sin):
    return x * cos + pltpu.roll(x, HEAD_DIM // 2, 1) * sin


def _rope_t(g, cos, sin):
    return g * cos + pltpu.roll(g * sin, HEAD_DIM // 2, 1)


def _rope_tables(s):
    half = HEAD_DIM // 2
    inv = ROPE_BASE ** (-jnp.arange(half, dtype=f32) / half)
    ang = jnp.arange(s, dtype=f32)[:, None] * inv[None, :]
    cos, sin = jnp.cos(ang), jnp.sin(ang)
    return jnp.concatenate([cos, cos], axis=1), jnp.concatenate([-sin, sin], axis=1)


def _head_ln(o):
    mu = jnp.mean(o, axis=-1, keepdims=True)
    oc = o - mu
    rs = lax.rsqrt(jnp.mean(oc * oc, axis=-1, keepdims=True) + EPS)
    return oc * rs, rs


def _ret_fwd(proj, cos, sin, ret_g, nb, s):
    c = RET_CHUNK
    nc = s // c
    t = nb * s
    scale = HEAD_DIM ** -0.5

    def body(q_ref, k_ref, v_ref, gate_ref, cos_ref, sin_ref, g_ref, o_ref, rprev_ref, m_ref):
        decay, kd, qd, gc = _ret_consts(pl.program_id(0))
        gv = g_ref[...]

        def chunk(b, n, rv):
            rows = pl.ds(pl.multiple_of(b * s + n * c, c), c)
            pos = pl.ds(pl.multiple_of(n * c, c), c)
            cs, sn = cos_ref[pos, :], sin_ref[pos, :]
            q = _rope(q_ref[rows, :], cs, sn)
            k = _rope(k_ref[rows, :], cs, sn) * scale
            vb = v_ref[rows, :].astype(bf16)
            sc = _nt(q.astype(bf16), k.astype(bf16)) * decay
            rprev_ref[b, n] = rv
            o = _nn(sc.astype(bf16), vb) + _nn((q * qd).astype(bf16), rv.astype(bf16))
            o_ref[rows, :] = o
            y, _ = _head_ln(o)
            gate = gate_ref[rows, :]
            m_ref[rows, :] = y * gv * (gate * jax.nn.sigmoid(gate))
            return rv * gc + _tn((k * kd).astype(bf16), vb)

        def step(n, carry):
            return tuple(chunk(b, n, carry[b]) for b in range(nb))

        lax.fori_loop(0, nc, step, (jnp.zeros((HEAD_DIM, HEAD_DIM), f32),) * nb)

    def col(off):
        return pl.BlockSpec((t, HEAD_DIM), lambda h: (0, off + h))

    tab = pl.BlockSpec((s, HEAD_DIM), lambda h: (0, 0))
    return pl.pallas_call(
        body, name="ret_fwd", grid=(RET_HEADS,),
        in_specs=[col(0), col(4), col(8), col(12), tab, tab, pl.BlockSpec((1, HEAD_DIM), lambda h: (0, h))],
        out_specs=[col(0), pl.BlockSpec((nb, None, nc, HEAD_DIM, HEAD_DIM), lambda h: (0, h, 0, 0, 0)), col(0)],
        out_shape=[jax.ShapeDtypeStruct((t, RET_WIDTH), f32),
                   jax.ShapeDtypeStruct((nb, RET_HEADS, nc, HEAD_DIM, HEAD_DIM), f32),
                   jax.ShapeDtypeStruct((t, RET_WIDTH), f32)],
        compiler_params=_cp("parallel"),
    )(proj, proj, proj, proj, cos, sin, ret_g)


def _ret_bwd(dmerged, o_raw, rprev, proj, cos, sin, ret_g, nb, s):
    c = RET_CHUNK
    nc = s // c
    t = nb * s
    scale = HEAD_DIM ** -0.5

    def body(dm_ref, o_ref, rprev_ref, q_ref, k_ref, v_ref, gate_ref, cos_ref, sin_ref, g_ref,
             dq_ref, dk_ref, dv_ref, dgate_ref, dg_ref):
        decay, kd, qd, gc = _ret_consts(pl.program_id(0))
        gv = g_ref[...]

        def chunk(b, n, drn, dg):
            rows = pl.ds(pl.multiple_of(b * s + n * c, c), c)
            pos = pl.ds(pl.multiple_of(n * c, c), c)
            cs, sn = cos_ref[pos, :], sin_ref[pos, :]
            q = _rope(q_ref[rows, :], cs, sn)
            k = _rope(k_ref[rows, :], cs, sn) * scale
            qb, kb = q.astype(bf16), k.astype(bf16)
            vb = v_ref[rows, :].astype(bf16)
            sc = _nt(qb, kb) * decay
            y, rs = _head_ln(o_ref[rows, :])
            gate = gate_ref[rows, :]
            sg = jax.nn.sigmoid(gate)
            silu = gate * sg
            dm = dm_ref[rows, :]
            dgate_ref[rows, :] = dm * y * gv * (sg * (1.0 + gate * (1.0 - sg)))
            dyl = dm * gv * silu
            dg = dg + jnp.sum(dm * y * silu, axis=0, keepdims=True)
            do = rs * (dyl - jnp.mean(dyl, axis=-1, keepdims=True) - y * jnp.mean(dyl * y, axis=-1, keepdims=True))
            dob = do.astype(bf16)
            rv = rprev_ref[b, n]
            drb = drn.astype(bf16)
            ds = (_nt(dob, vb) * decay).astype(bf16)
            kdb = (k * kd).astype(bf16)
            qdb = (q * qd).astype(bf16)
            dq_r = _nn(ds, kb) + _nt(dob, rv.astype(bf16)) * qd
            dk_r = _tn(ds, qb) + _nt(vb, drb) * kd
            dv_ref[rows, :] = _tn(sc.astype(bf16), dob) + _nn(kdb, drb)
            dq_ref[rows, :] = _rope_t(dq_r, cs, sn)
            dk_ref[rows, :] = _rope_t(dk_r * scale, cs, sn)
            return drn * gc + _tn(qdb, dob), dg

        def step(i, carry):
            out = [chunk(b, nc - 1 - i, *carry[b]) for b in range(nb)]
            return tuple(out)

        zero = (jnp.zeros((HEAD_DIM, HEAD_DIM), f32), jnp.zeros((1, HEAD_DIM), f32))
        done = lax.fori_loop(0, nc, step, (zero,) * nb)
        dg_ref[...] = sum(dg for _, dg in done)

    def col(off):
        return pl.BlockSpec((t, HEAD_DIM), lambda h: (0, off + h))

    tab = pl.BlockSpec((s, HEAD_DIM), lambda h: (0, 0))
    gsp = pl.BlockSpec((1, HEAD_DIM), lambda h: (0, h))
    out_t = jax.ShapeDtypeStruct((t, RET_WIDTH), f32)
    return pl.pallas_call(
        body, name="ret_bwd", grid=(RET_HEADS,),
        in_specs=[col(0), col(0), pl.BlockSpec((nb, None, nc, HEAD_DIM, HEAD_DIM), lambda h: (0, h, 0, 0, 0)),
                  col(0), col(4), col(8), col(12), tab, tab, gsp],
        out_specs=[col(0), col(0), col(0), col(0), gsp],
        out_shape=[out_t, out_t, out_t, out_t, jax.ShapeDtypeStruct((1, RET_WIDTH), f32)],
        compiler_params=_cp("parallel"),
    )(dmerged, o_raw, rprev, proj, proj, proj, proj, cos, sin, ret_g)


def _neg_expm1(z):
    series = -(z * (1.0 + z * (0.5 + z * (1.0 / 6.0 + z * (1.0 / 24.0)))))
    return jnp.where(z > -0.01, series, 1.0 - jnp.exp(z))


def _lru_gates(xc, pa, pi, lam):
    r = jax.nn.sigmoid(pa)
    i = jax.nn.sigmoid(pi)
    log_a = -LRU_C * r * jax.nn.softplus(-lam)
    a = jnp.exp(log_a)
    bx = jnp.sqrt(_neg_expm1(2.0 * log_a)) * i * xc
    return a, bx


def _scan_rows(a, b, row, up):
    sub = row[:SUBLANES] & (SUBLANES - 1)
    groups = list(range(a.shape[0] // SUBLANES))
    out = [None] * len(groups)
    edge = slice(0, 1) if up else slice(SUBLANES - 1, SUBLANES)
    carry = jnp.zeros((1, a.shape[1]), f32)
    for g in (reversed(groups) if up else groups):
        rows = slice(g * SUBLANES, (g + 1) * SUBLANES)
        xa, xb = a[rows], b[rows]
        d = 1
        while d < SUBLANES:
            keep = (sub < SUBLANES - d) if up else (sub >= d)
            shift = SUBLANES - d if up else d
            xb = xa * jnp.where(keep, pltpu.roll(xb, shift, 0), 0.0) + xb
            xa = xa * jnp.where(keep, pltpu.roll(xa, shift, 0), 1.0)
            d *= 2
        out[g] = xb + xa * carry
        carry = out[g][edge]
    return jnp.concatenate(out, axis=0)


def _scan_fwd(a, b, row):
    return _scan_rows(a, b, row, False)


def _scan_bwd(c, b, row):
    return _scan_rows(c, b, row, True)


def _conv_fwd(x, cw, cb, row):
    return (cb + cw[3:4] * x + cw[2:3] * _shift_dn(x, 1, row) + cw[1:2] * _shift_dn(x, 2, row)
            + cw[0:1] * _shift_dn(x, 3, row))


def _lru_specs(s, order):
    def im(f):
        return (lambda b, g: f(b, g)) if order == "bg" else (lambda g, b: f(b, g))
    seq = lambda off: pl.BlockSpec((s, 128), im(lambda b, g: (b, off + g)))
    vec = pl.BlockSpec((1, 128), im(lambda b, g: (0, g)))
    cw = pl.BlockSpec((4, 128), im(lambda b, g: (0, g)))
    mat = pl.BlockSpec((1, 128, 128), im(lambda b, g: (g, 0, 0)))
    return seq, vec, cw, mat


def _lru_fwd(proj, conv_w, conv_b, w_a, b_a, w_i, b_i, lam, nb, s):
    def body(x_ref, gt_ref, cw_ref, cb_ref, wa_ref, ba_ref, wi_ref, bi_ref, lam_ref, out_ref, h_ref):
        row = lax.broadcasted_iota(jnp.int32, (s, 128), 0)
        xc = _conv_fwd(x_ref[...], cw_ref[...], cb_ref[...], row)
        xcb = xc.astype(bf16)
        pa = _nn(xcb, wa_ref[0].astype(bf16)) + ba_ref[...]
        pi = _nn(xcb, wi_ref[0].astype(bf16)) + bi_ref[...]
        a, bx = _lru_gates(xc, pa, pi, lam_ref[...])
        h = _scan_fwd(a, bx, row)
        h_ref[...] = h
        out_ref[...] = h * jax.nn.gelu(gt_ref[...])

    seq, vec, cw, mat = _lru_specs(s, "bg")
    out = jax.ShapeDtypeStruct((nb * s, LRU_WIDTH), f32)
    return pl.pallas_call(
        body, name="lru_fwd", grid=(nb, LRU_BLOCKS),
        in_specs=[seq(16), seq(20), cw, vec, mat, vec, mat, vec, vec],
        out_specs=[seq(0), seq(0)], out_shape=[out, out],
        compiler_params=_cp("parallel", "parallel"),
    )(proj, proj, conv_w, conv_b, w_a, b_a, w_i, b_i, lam)


def _lru_bwd(dmerged, states, proj, conv_w, conv_b, w_a, b_a, w_i, b_i, lam, nb, s):
    def body(dout_ref, h_ref, x_ref, gt_ref, cw_ref, cb_ref, wa_ref, ba_ref, wi_ref, bi_ref, lam_ref,
             dx_ref, dgt_ref, dcw_ref, dcb_ref, dwa_ref, dba_ref, dwi_ref, dbi_ref, dlam_ref):
        row = lax.broadcasted_iota(jnp.int32, (s, 128), 0)
        x = x_ref[...]
        cwv = cw_ref[...]
        xc = _conv_fwd(x, cwv, cb_ref[...], row)
        xcb = xc.astype(bf16)
        wab, wib = wa_ref[0].astype(bf16), wi_ref[0].astype(bf16)
        pa = _nn(xcb, wab) + ba_ref[...]
        pi = _nn(xcb, wib) + bi_ref[...]
        (a, _), gates_vjp = jax.vjp(_lru_gates, xc, pa, pi, lam_ref[...])
        h = h_ref[...]
        ge, gelu_vjp = jax.vjp(jax.nn.gelu, gt_ref[...])
        dout = dout_ref[...]
        dgt_ref[...] = gelu_vjp(dout * h)[0]
        adj = _scan_bwd(_shift_up(a, 1, row), dout * ge, row)
        dxc, dpa, dpi, dlam = gates_vjp((adj * _shift_dn(h, 1, row), adj))
        dpab, dpib = dpa.astype(bf16), dpi.astype(bf16)
        dxc = dxc + _nt(dpab, wab) + _nt(dpib, wib)
        dx_ref[...] = (cwv[3:4] * dxc + cwv[2:3] * _shift_up(dxc, 1, row) + cwv[1:2] * _shift_up(dxc, 2, row)
                       + cwv[0:1] * _shift_up(dxc, 3, row))

        @pl.when(pl.program_id(1) == 0)
        def _():
            for r in (dcw_ref, dcb_ref, dwa_ref, dba_ref, dwi_ref, dbi_ref, dlam_ref):
                r[...] = jnp.zeros_like(r)
        rsum = lambda v: jnp.sum(v, axis=0, keepdims=True)
        dcw_ref[...] += jnp.concatenate([rsum(dxc * _shift_dn(x, 3, row)), rsum(dxc * _shift_dn(x, 2, row)),
                                         rsum(dxc * _shift_dn(x, 1, row)), rsum(dxc * x)], axis=0)
        dcb_ref[...] += rsum(dxc)
        dwa_ref[0] += _tn(xcb, dpab)
        dwi_ref[0] += _tn(xcb, dpib)
        dba_ref[...] += rsum(dpa)
        dbi_ref[...] += rsum(dpi)
        dlam_ref[...] += dlam

    seq, vec, cw, mat = _lru_specs(s, "gb")
    t = nb * s
    vshape = jax.ShapeDtypeStruct((1, LRU_WIDTH), f32)
    mshape = jax.ShapeDtypeStruct((LRU_BLOCKS, 128, 128), f32)
    return pl.pallas_call(
        body, name="lru_bwd", grid=(LRU_BLOCKS, nb),
        in_specs=[seq(4), seq(0), seq(16), seq(20), cw, vec, mat, vec, mat, vec, vec],
        out_specs=[seq(0), seq(0), cw, vec, mat, vec, mat, vec, vec],
        out_shape=[jax.ShapeDtypeStruct((t, LRU_WIDTH), f32), jax.ShapeDtypeStruct((t, LRU_WIDTH), f32),
                   jax.ShapeDtypeStruct((4, LRU_WIDTH), f32), vshape, mshape, vshape, mshape, vshape, vshape],
        compiler_params=_cp("parallel", "arbitrary"),
    )(dmerged, states, proj, proj, conv_w, conv_b, w_a, b_a, w_i, b_i, lam)


def _s5_disc(lr, li, ldt, bre, bim):
    dt = jnp.exp(ldt)
    mag = jnp.exp(lr * dt)
    lbr = mag * jnp.cos(li * dt)
    lbi = mag * jnp.sin(li * dt)
    den = lr * lr + li * li
    nr = lbr - 1.0
    fr = (nr * lr + lbi * li) / den
    fi = (lbi * lr - nr * li) / den
    bbr = fr[:, None, :] * bre - fi[:, None, :] * bim
    bbi = fr[:, None, :] * bim + fi[:, None, :] * bre
    return lbr, lbi, bbr, bbi


def _s5_prep(lr, li, ldt, bre, bim):
    def body(lr_ref, li_ref, ldt_ref, bre_ref, bim_ref, o1, o2, o3, o4):
        o1[...], o2[...], o3[...], o4[...] = _s5_disc(lr_ref[...], li_ref[...], ldt_ref[...], bre_ref[...], bim_ref[...])

    return pl.pallas_call(
        body, name="s5_prep", in_specs=[VMEM_SPEC] * 5, out_specs=[VMEM_SPEC] * 4,
        out_shape=[jax.ShapeDtypeStruct(lr.shape, f32), jax.ShapeDtypeStruct(lr.shape, f32),
                   jax.ShapeDtypeStruct(bre.shape, f32), jax.ShapeDtypeStruct(bre.shape, f32)],
    )(lr, li, ldt, bre, bim)


def _s5_prep_bwd(lr, li, ldt, bre, bim, cts):
    def body(lr_ref, li_ref, ldt_ref, bre_ref, bim_ref, g1, g2, g3, g4, o1, o2, o3, o4, o5):
        _, vjp = jax.vjp(_s5_disc, lr_ref[...], li_ref[...], ldt_ref[...], bre_ref[...], bim_ref[...])
        o1[...], o2[...], o3[...], o4[...], o5[...] = vjp((g1[...], g2[...], g3[...], g4[...]))

    return pl.pallas_call(
        body, name="s5_prep_bwd", in_specs=[VMEM_SPEC] * 9, out_specs=[VMEM_SPEC] * 5,
        out_shape=[jax.ShapeDtypeStruct(v.shape, f32) for v in (lr, li, ldt, bre, bim)],
    )(lr, li, ldt, bre, bim, *cts)


def _cmul(ar, ai, br, bi):
    return ar * br - ai * bi, ar * bi + ai * br


def _s5_pow_table(lr, li, n, row, up):
    ar = jnp.broadcast_to(lr, (n, lr.shape[1]))
    ai = jnp.broadcast_to(li, (n, li.shape[1]))
    shift = _shift_up if up else _shift_dn
    d = 1
    while d < n:
        ar, ai = _cmul(ar, ai, shift(ar, d, row, 1.0), shift(ai, d, row, 0.0))
        d *= 2
    return ar, ai


def _s5_step_factors(lr, li, row, up):
    sub = row & (SUBLANES - 1)
    out, pr, pi, d = [], lr, li, 1
    while d < SUBLANES:
        keep = (sub < SUBLANES - d) if up else (sub >= d)
        out.append((jnp.where(keep, pr, 0.0), jnp.where(keep, pi, 0.0)))
        pr, pi = _cmul(pr, pi, pr, pi)
        d *= 2
    return out


def _s5_scan(br, bi, steps, tab_r, tab_i, cr, ci, up):
    groups = list(range(br.shape[0] // SUBLANES))
    out_r, out_i = [None] * len(groups), [None] * len(groups)
    edge = slice(0, 1) if up else slice(SUBLANES - 1, SUBLANES)
    for g in (reversed(groups) if up else groups):
        rows = slice(g * SUBLANES, (g + 1) * SUBLANES)
        xr, xi = br[rows], bi[rows]
        for k, (mr, mi) in enumerate(steps):
            shift = SUBLANES - (1 << k) if up else 1 << k
            tr, ti = _cmul(mr, mi, pltpu.roll(xr, shift, 0), pltpu.roll(xi, shift, 0))
            xr, xi = xr + tr, xi + ti
        tr, ti = _cmul(tab_r, tab_i, cr, ci)
        hr, hi = xr + tr, xi + ti
        out_r[g], out_i[g] = hr, hi
        cr, ci = hr[edge], hi[edge]
    return jnp.concatenate(out_r, axis=0), jnp.concatenate(out_i, axis=0)


def _s5_specs(t, nb, nc):
    seq = pl.BlockSpec((t, 128), lambda k: (0, k))
    lvec = pl.BlockSpec((1, S5_BLOCK_STATES), lambda k: (0, k))
    dvec = pl.BlockSpec((1, 128), lambda k: (0, k))
    wmat = pl.BlockSpec((1, 128, S5_BLOCK_STATES), lambda k: (k, 0, 0))
    h0 = pl.BlockSpec((nb, None, nc, 2, S5_BLOCK_STATES), lambda k: (0, k, 0, 0, 0))
    states = pl.BlockSpec((t, S5_BLOCK_STATES), lambda k: (0, k))
    return seq, lvec, dvec, wmat, h0, states


def _s5_fwd(u, lbr, lbi, wbr, wbi, wcr, wci, dskip, nb, s):
    ln = min(S5_CHUNK, s)
    nc = s // ln

    def body(u_ref, lr_ref, li_ref, wbr_ref, wbi_ref, wcr_ref, wci_ref, d_ref, yg_ref, y_ref, h0_ref, hr_ref, hi_ref):
        row = lax.broadcasted_iota(jnp.int32, (ln, S5_BLOCK_STATES), 0)
        lr, li = lr_ref[...], li_ref[...]
        pr, pi = _s5_pow_table(lr, li, SUBLANES, row[:SUBLANES], False)
        steps = _s5_step_factors(lr, li, row[:SUBLANES], False)
        dv = d_ref[...]

        def chunk(b, n, h0r, h0i):
            st = pl.multiple_of(b * s + n * ln, ln)
            uc = u_ref[pl.ds(st, ln), :]
            ub = uc.astype(bf16)
            hr, hi = _s5_scan(_nn(ub, wbr_ref[0]), _nn(ub, wbi_ref[0]), steps, pr, pi, h0r, h0i, False)
            h0_ref[b, n, 0:1, :] = h0r
            h0_ref[b, n, 1:2, :] = h0i
            hrb, hib = hr.astype(bf16), hi.astype(bf16)
            hr_ref[pl.ds(st, ln), :] = hrb
            hi_ref[pl.ds(st, ln), :] = hib
            y = _nt(hrb, wcr_ref[0]) - _nt(hib, wci_ref[0]) + dv * uc
            y_ref[pl.ds(st, ln), :] = y
            yg_ref[pl.ds(st, ln), :] = jax.nn.gelu(y).astype(bf16)
            return hr[ln - 1:ln, :], hi[ln - 1:ln, :]

        def step(n, carry):
            return tuple(chunk(b, n, *carry[b]) for b in range(nb))

        z = jnp.zeros((1, S5_BLOCK_STATES), f32)
        lax.fori_loop(0, nc, step, ((z, z),) * nb)

    t = nb * s
    seq, lvec, dvec, wmat, h0, states = _s5_specs(t, nb, nc)
    return pl.pallas_call(
        body, name="s5_fwd", grid=(S5_BLOCKS,),
        in_specs=[seq, lvec, lvec, wmat, wmat, wmat, wmat, dvec],
        out_specs=[seq, seq, h0, states, states],
        out_shape=[jax.ShapeDtypeStruct((t, D_MODEL), bf16), jax.ShapeDtypeStruct((t, D_MODEL), f32),
                   jax.ShapeDtypeStruct((nb, S5_BLOCKS, nc, 2, S5_BLOCK_STATES), f32),
                   jax.ShapeDtypeStruct((t, S5_BLOCKS * S5_BLOCK_STATES), bf16),
                   jax.ShapeDtypeStruct((t, S5_BLOCKS * S5_BLOCK_STATES), bf16)],
        compiler_params=_cp("parallel"),
    )(u, lbr, lbi, wbr, wbi, wcr, wci, dskip)


def _s5_bwd(dyg, y, u, h0, hrs, his, lbr, lbi, wbr, wbi, wcr, wci, dskip, nb, s):
    ln = min(S5_CHUNK, s)
    nc = s // ln

    def body(dyg_ref, y_ref, u_ref, h0_ref, hr_ref, hi_ref, lr_ref, li_ref, wbr_ref, wbi_ref, wcr_ref, wci_ref, d_ref,
             du_ref, dlr_ref, dli_ref, dwbr_ref, dwbi_ref, dwcr_ref, dwci_ref, dd_ref):
        for r in (dlr_ref, dli_ref, dwbr_ref, dwbi_ref, dwcr_ref, dwci_ref, dd_ref):
            r[...] = jnp.zeros_like(r)
        row = lax.broadcasted_iota(jnp.int32, (ln, S5_BLOCK_STATES), 0)
        lr, li = lr_ref[...], li_ref[...]
        qr, qi = _s5_pow_table(lr, -li, SUBLANES, row[:SUBLANES], True)
        steps_up = _s5_step_factors(lr, -li, row[:SUBLANES], True)
        dv = d_ref[...]
        rsum = lambda v: jnp.sum(v, axis=0, keepdims=True)

        def chunk(b, n, gnr, gni):
            st = pl.multiple_of(b * s + n * ln, ln)
            uc = u_ref[pl.ds(st, ln), :]
            ub = uc.astype(bf16)
            h0v = h0_ref[b, n]
            h0r, h0i = h0v[0:1], h0v[1:2]
            hrb, hib = hr_ref[pl.ds(st, ln), :], hi_ref[pl.ds(st, ln), :]
            hr, hi = hrb.astype(f32), hib.astype(f32)
            dy = jax.vjp(jax.nn.gelu, y_ref[pl.ds(st, ln), :])[1](dyg_ref[pl.ds(st, ln), :])[0]
            dyb = dy.astype(bf16)
            dd_ref[...] += rsum(dy * uc)
            gr, gi = _s5_scan(_nn(dyb, wcr_ref[0]), -_nn(dyb, wci_ref[0]), steps_up, qr, qi, gnr, gni, True)
            hpr = jnp.where(row >= 1, pltpu.roll(hr, 1, 0), h0r)
            hpi = jnp.where(row >= 1, pltpu.roll(hi, 1, 0), h0i)
            dlr_ref[...] += rsum(gr * hpr + gi * hpi)
            dli_ref[...] += rsum(gi * hpr - gr * hpi)
            grb, gib = gr.astype(bf16), gi.astype(bf16)
            dwbr_ref[0] += _tn(ub, grb)
            dwbi_ref[0] += _tn(ub, gib)
            dwcr_ref[0] += _tn(dyb, hrb)
            dwci_ref[0] -= _tn(dyb, hib)
            du_ref[pl.ds(st, ln), :] = _nt(grb, wbr_ref[0]) + _nt(gib, wbi_ref[0]) + dv * dy
            return gr[0:1, :], gi[0:1, :]

        def step(i, carry):
            return tuple(chunk(b, nc - 1 - i, *carry[b]) for b in range(nb))

        z = jnp.zeros((1, S5_BLOCK_STATES), f32)
        lax.fori_loop(0, nc, step, ((z, z),) * nb)

    t = nb * s
    seq, lvec, dvec, wmat, h0s, states = _s5_specs(t, nb, nc)
    lshape = jax.ShapeDtypeStruct((1, S5_BLOCKS * S5_BLOCK_STATES), f32)
    wshape = jax.ShapeDtypeStruct((S5_BLOCKS, 128, S5_BLOCK_STATES), f32)
    return pl.pallas_call(
        body, name="s5_bwd", grid=(S5_BLOCKS,),
        in_specs=[seq, seq, seq, h0s, states, states, lvec, lvec, wmat, wmat, wmat, wmat, dvec],
        out_specs=[seq, lvec, lvec, wmat, wmat, wmat, wmat, dvec],
        out_shape=[jax.ShapeDtypeStruct((t, D_MODEL), f32), lshape, lshape, wshape, wshape, wshape, wshape,
                   jax.ShapeDtypeStruct((1, D_MODEL), f32)],
        compiler_params=_cp("parallel"),
    )(dyg, y, u, h0, hrs, his, lbr, lbi, wbr, wbi, wcr, wci, dskip)


def _blockdiag(w):
    w4 = w.reshape(S5_BLOCKS, 8, S5_GROUP, S5_STATE)
    same_group = jnp.eye(8, dtype=bool)[None, :, None, :, None]
    return jnp.where(same_group, w4[:, :, :, None, :], 0.0).reshape(S5_BLOCKS, 128, S5_BLOCK_STATES)


def _blockdiag_t(dw):
    d5 = dw.reshape(S5_BLOCKS, 8, S5_GROUP, 8, S5_STATE)
    diag = jnp.diagonal(d5, axis1=1, axis2=3)
    return jnp.moveaxis(diag, 3, 1).reshape(S5_GROUPS, S5_GROUP, S5_STATE)


def _glu_fwd(ygb, wa, wb, x, tm=512, tn=1024):
    t, d = x.shape

    def body(y_ref, wa_ref, wb_ref, x_ref, o_ref, p_ref, q_ref):
        p = _nn(y_ref[...], wa_ref[...])
        q = _nn(y_ref[...], wb_ref[...])
        p_ref[...] = p
        q_ref[...] = q
        o_ref[...] = x_ref[...] + p * jax.nn.sigmoid(q)

    tile = pl.BlockSpec((tm, tn), lambda i, j: (i, j))
    wsp = pl.BlockSpec((d, tn), lambda i, j: (0, j))
    out = jax.ShapeDtypeStruct((t, d), f32)
    return pl.pallas_call(
        body, name="glu_fwd", grid=(t // tm, d // tn),
        in_specs=[pl.BlockSpec((tm, d), lambda i, j: (i, 0)), wsp, wsp, tile],
        out_specs=[tile, tile, tile], out_shape=[out, out, out],
        compiler_params=_cp("parallel", "parallel"),
    )(ygb, wa, wb, x)


def _place():
    x, y, c = lax.axis_index("x"), lax.axis_index("y"), lax.axis_index("c")
    return x, y, c, [(1 - x, y), (x, 1 - y), (1 - x, 1 - y)]


def _all_gather(name, arrays):
    n = len(arrays)

    def body(*refs):
        ins, outs = refs[:n], refs[n:2 * n]
        send_sems, recv_sems, local_sems = refs[2 * n:]
        x, y, c, chips = _place()
        me, sib = (x, y, c), (x, y, 1 - c)

        def copy(i, k, block, to, src=None):
            dst = outs[i].at[4 * block[0] + 2 * block[1] + block[2]]
            return pltpu.make_async_remote_copy(
                src_ref=dst if src is None else src, dst_ref=dst,
                send_sem=send_sems.at[i * 7 + k], recv_sem=recv_sems.at[i * 7 + k],
                device_id=to, device_id_type=MESH)

        mine = [pltpu.make_async_copy(ins[i], outs[i].at[4 * x + 2 * y + c], local_sems.at[i]) for i in range(n)]
        for m in mine:
            m.start()
        first = []
        for i in range(n):
            first.append(copy(i, 0, me, sib, src=ins[i]))
            first += [copy(i, 1 + j, me, (*chip, c), src=ins[i]) for j, chip in enumerate(chips)]
        for cp in first:
            cp.start()
        passed = []
        for j, chip in enumerate(chips):
            for i in range(n):
                copy(i, 1 + j, (*chip, c), me).wait_recv()
                fwd = copy(i, 4 + j, (*chip, c), sib)
                fwd.start()
                passed.append(fwd)
        for i in range(n):
            copy(i, 0, sib, me).wait_recv()
        for j, chip in enumerate(chips):
            for i in range(n):
                copy(i, 4 + j, (*chip, 1 - c), me).wait_recv()
        for cp in first + passed:
            cp.wait_send()
        for m in mine:
            m.wait()

    return pl.pallas_call(
        body, name=name,
        in_specs=[ANY_SPEC] * n, out_specs=[ANY_SPEC] * n,
        out_shape=[jax.ShapeDtypeStruct((N_DEV,) + a.shape, a.dtype) for a in arrays],
        scratch_shapes=[pltpu.SemaphoreType.DMA((7 * n,)), pltpu.SemaphoreType.DMA((7 * n,)),
                        pltpu.SemaphoreType.DMA((n,))],
    )(*arrays)


def _tie(name, x, deps):
    def body(*refs):
        pass

    return pl.pallas_call(
        body, name=name, in_specs=[ANY_SPEC] * (1 + len(deps)), out_specs=ANY_SPEC,
        out_shape=jax.ShapeDtypeStruct(x.shape, x.dtype), input_output_aliases={0: 0},
    )(x, *deps)


def _xchg_copies(kind, srcs, lands, suffixes, send_sems, recv_sems):
    x, y, c, _ = _place()
    copies = []
    for i, (src, land, sfx) in enumerate(zip(srcs, lands, suffixes)):
        for k in range(N_DEV - 1):
            r = k + 1
            peer = (1 - x if r & 4 else x, 1 - y if r & 2 else y, 1 - c if r & 1 else c)
            if kind == "gather":
                s_ref, d_ref = src, land.at[(4 * x + 2 * y + c,) + sfx]
            elif len(src.shape) > 2:
                s_ref, d_ref = src.at[4 * peer[0] + 2 * peer[1] + peer[2]], land.at[(k,) + sfx]
            else:
                width = land.shape[-1]
                first_col = pl.multiple_of((4 * peer[0] + 2 * peer[1] + peer[2]) * width, 128)
                s_ref, d_ref = src.at[:, pl.ds(first_col, width)], land.at[(k,) + sfx]
            copies.append(pltpu.make_async_remote_copy(
                src_ref=s_ref, dst_ref=d_ref, send_sem=send_sems.at[i * 7 + k], recv_sem=recv_sems.at[i * 7 + k],
                device_id=peer, device_id_type=MESH))
    return copies


def _xchg_start(name, kind, srcs, lands, suffixes=None):
    n = len(srcs)
    suffixes = suffixes or [()] * n

    def body(*refs):
        src, land = refs[:n], refs[n:2 * n]
        send_sems, recv_sems, token = refs[2 * n], refs[2 * n + 1], refs[-1]
        for cp in _xchg_copies(kind, src, land, suffixes, send_sems, recv_sems):
            cp.start()
        token[...] = jnp.zeros_like(token)

    arrays = list(srcs) + list(lands)
    outs = pl.pallas_call(
        body, name=name,
        out_shape=(pltpu.SemaphoreType.DMA((7 * n,)), pltpu.SemaphoreType.DMA((7 * n,)),
                   *[pltpu.HBM(a.shape, a.dtype) for a in arrays], jax.ShapeDtypeStruct((8, 128), f32)),
        in_specs=[HBM_SPEC] * (2 * n),
        out_specs=(SEM_SPEC, SEM_SPEC, *[HBM_SPEC] * (2 * n), VMEM_SPEC),
        input_output_aliases={i: 2 + i for i in range(2 * n)},
        compiler_params=pltpu.CompilerParams(has_side_effects=SIDE_EFFECT),
    )(*[pltpu.with_memory_space_constraint(a, pltpu.HBM) for a in arrays])
    return dict(kind=kind, n=n, suffixes=suffixes, send=outs[0], recv=outs[1], srcs=list(outs[2:2 + n]),
                lands=list(outs[2 + n:2 + 2 * n]), token=outs[-1])


def _xchg_wait(name, h, after, lands=None):
    n = h["n"]
    lands = h["lands"] if lands is None else lands

    def body(*refs):
        src, land = refs[:n], refs[n:2 * n]
        for cp in _xchg_copies(h["kind"], src, land, h["suffixes"], refs[2 * n], refs[2 * n + 1]):
            cp.wait_send()
            cp.wait_recv()

    arrays = h["srcs"] + list(lands)
    outs = pl.pallas_call(
        body, name=name,
        out_shape=tuple(pltpu.HBM(a.shape, a.dtype) for a in arrays),
        in_specs=[HBM_SPEC] * (2 * n) + [SEM_SPEC, SEM_SPEC] + [ANY_SPEC] * len(after),
        out_specs=tuple([HBM_SPEC] * (2 * n)),
        input_output_aliases={i: i for i in range(2 * n)},
        compiler_params=pltpu.CompilerParams(has_side_effects=SIDE_EFFECT),
    )(*arrays, h["send"], h["recv"], *after)
    return list(outs[n:])


def _rows(a):
    return a.reshape(-1, a.shape[-1])


def _row_tile(r):
    for tm in (512, 256, 128, 64, 32, 16, 8):
        if r % tm == 0:
            return tm
    return r


def _sum8(name, gathered):
    _, r, n = gathered.shape
    tm = _row_tile(r)

    def body(g_ref, o_ref):
        acc = g_ref[0]
        for k in range(1, N_DEV):
            acc = acc + g_ref[k]
        o_ref[...] = acc

    return pl.pallas_call(
        body, name=name, grid=(r // tm,),
        in_specs=[pl.BlockSpec((N_DEV, tm, n), lambda i: (0, i, 0))],
        out_specs=pl.BlockSpec((tm, n), lambda i: (i, 0)),
        out_shape=jax.ShapeDtypeStruct((r, n), f32),
        compiler_params=_cp("parallel"),
    )(gathered)


def _adamw(name, w, m, v, own, landed=None, slot=None):
    shape = w.shape
    w2, m2, v2 = _rows(w), _rows(m), _rows(v)
    r, n = w2.shape
    tm = _row_tile(r)
    c1 = 1.0 - ADAM_B1 ** ADAM_STEP
    c2 = 1.0 - ADAM_B2 ** ADAM_STEP
    extra = [] if landed is None else [landed.reshape(landed.shape[0], r, n)]
    row = pl.BlockSpec((tm, n), lambda i, *_: (i, 0))
    if slot is None:
        o2, own_spec, scalars = _rows(own), row, []
    else:
        dev, kind = slot
        scalars = [dev.reshape(1).astype(jnp.int32)]
        if kind == "lead":
            o2, own_spec = own.reshape(N_DEV, r, n), pl.BlockSpec((None, tm, n), lambda i, d: (d[0], i, 0))
        elif kind == "rows":
            o2, own_spec = own, pl.BlockSpec((tm, n), lambda i, d: (d[0] * (r // tm) + i, 0))
        else:
            o2, own_spec = own, pl.BlockSpec((tm, n), lambda i, d: (i, d[0]))

    def body(*refs):
        w_ref, m_ref, v_ref, o_ref = refs[len(scalars):len(scalars) + 4]
        refs = refs[len(scalars) + 4:]
        g = o_ref[...]
        if extra:
            for k in range(extra[0].shape[0]):
                g = g + refs[0][k].astype(f32)
        g_ref, d_ref, mn_ref, vn_ref = refs[len(extra):]
        mn = ADAM_B1 * m_ref[...] + (1.0 - ADAM_B1) * g
        vn = ADAM_B2 * v_ref[...] + (1.0 - ADAM_B2) * (g * g)
        g_ref[...] = g
        d_ref[...] = -ADAM_LR * ((mn / c1) / (jnp.sqrt(vn / c2) + ADAM_EPS) + ADAM_WD * w_ref[...])
        mn_ref[...] = mn
        vn_ref[...] = vn

    outs = pl.pallas_call(
        body, name=name,
        grid_spec=pltpu.PrefetchScalarGridSpec(
            num_scalar_prefetch=len(scalars), grid=(r // tm,),
            in_specs=[row] * 3 + [own_spec] + [pl.BlockSpec((e.shape[0], tm, n), lambda i, *_: (0, i, 0)) for e in extra],
            out_specs=[row] * 4),
        out_shape=[jax.ShapeDtypeStruct((r, n), f32)] * 4,
        compiler_params=_cp("parallel"),
    )(*scalars, w2, m2, v2, o2, *extra)
    return [o.reshape(shape) for o in outs]


def _adamw_minor_d(name, w, m, v, own_all, landed, dev, tm=512):
    nl, nh, d, f = w.shape
    wt, mt, vt = (jnp.swapaxes(t, 2, 3) for t in (w, m, v))
    r = nl * nh * d
    per = d // tm
    c1 = 1.0 - ADAM_B1 ** ADAM_STEP
    c2 = 1.0 - ADAM_B2 ** ADAM_STEP

    def body(dev_ref, w_ref, m_ref, v_ref, o_ref, l_ref, g_ref, d_ref, mn_ref, vn_ref):
        g = o_ref[...]
        for k in range(N_DEV - 1):
            g = g + l_ref[k].astype(f32)
        g = g.T
        mn = ADAM_B1 * m_ref[...] + (1.0 - ADAM_B1) * g
        vn = ADAM_B2 * v_ref[...] + (1.0 - ADAM_B2) * (g * g)
        g_ref[...] = g
        d_ref[...] = -ADAM_LR * ((mn / c1) / (jnp.sqrt(vn / c2) + ADAM_EPS) + ADAM_WD * w_ref[...])
        mn_ref[...] = mn
        vn_ref[...] = vn

    par = pl.BlockSpec((None, None, f, tm), lambda i, _: (i // (nh * per), (i // per) % nh, 0, i % per))
    outs = pl.pallas_call(
        body, name=name,
        grid_spec=pltpu.PrefetchScalarGridSpec(
            num_scalar_prefetch=1, grid=(r // tm,),
            in_specs=[par, par, par, pl.BlockSpec((None, tm, f), lambda i, dv: (dv[0], i, 0)),
                      pl.BlockSpec((N_DEV - 1, tm, f), lambda i, _: (0, i, 0))],
            out_specs=[par] * 4),
        out_shape=[jax.ShapeDtypeStruct(wt.shape, f32)] * 4,
        compiler_params=_cp("parallel"),
    )(dev.reshape(1).astype(jnp.int32), wt, mt, vt, own_all.reshape(N_DEV, r, f), landed.reshape(N_DEV - 1, r, f))
    return [jnp.swapaxes(o, 2, 3) for o in outs]


def _pack(arrays):
    flat = jnp.concatenate([a.reshape(-1).astype(f32) for a in arrays])
    pad = (-flat.shape[0]) % (128 * (512 if flat.shape[0] > 128 * 512 else 8))
    return jnp.pad(flat, (0, pad)).reshape(-1, 128)


def _unpack(packed, shapes):
    flat = packed.reshape(-1)
    out, off = [], 0
    for s in shapes:
        n = math.prod(s)
        out.append(flat[off:off + n].reshape(s))
        off += n
    return out


def _local_step(x, target, w, weights_of, send, last_small, on_loss, nb, s):
    cos, sin = _rope_tables(s)
    g = {}
    ffn_saved = {}
    ffn_bufs = [lax.empty((N_DEV, 2, 2) + shp, f32)
                for shp in ((D_MODEL, FF_SHARD), (D_MODEL, FF_SHARD), (FF_SHARD, D_MODEL))]

    def ffn(xin, l, h, wts):
        y, a, b = _ffn_fwd(f"ffn_fwd_{l}{h}", xin, w["ffn_g"][l][h], *wts)
        ffn_saved[(l, h)] = (xin, a, b, wts)
        return y

    def ffn_back(dy, l, h):
        xin, a, b, wts = ffn_saved[(l, h)]
        dx, dg, hb, dyh, u, da, db = _ffn_dx(f"ffn_dx_{l}{h}", dy, xin, w["ffn_g"][l][h], *wts, a, b)
        g[f"ffn_g_{l}{h}"] = dg
        if (l, h) != (0, 0):
            ffn_bufs[:1], (h1,) = _ffn_dw(f"ffn_dw_{l}{h}_w1", hb, [da], ffn_bufs[:1], l, h)
            ffn_bufs[1:2], (h3,) = _ffn_dw(f"ffn_dw_{l}{h}_w3", hb, [db], ffn_bufs[1:2], l, h)
            ffn_bufs[2:], (h2,) = _ffn_dw(f"ffn_dw_{l}{h}_w2", u, [dyh], ffn_bufs[2:], l, h)
            return send(f"ffn_{l}{h}", {"ffn_w1": h1, "ffn_w3": h3, "ffn_w2": h2}, dx)
        hb = last_small(g, hb)
        ffn_bufs[:1], (half,) = _ffn_dw("ffn_dw_00_w1", hb, [da], ffn_bufs[:1], l, h)
        hb = send("ffn_00_w1", {"ffn_w1": half}, hb)
        ffn_bufs[1:2], (half,) = _ffn_dw("ffn_dw_00_w3", hb, [db], ffn_bufs[1:2], l, h)
        u = send("ffn_00_w3", {"ffn_w3": half}, u)
        ffn_bufs[2:], (half,) = _ffn_dw("ffn_dw_00_w2", u, [dyh], ffn_bufs[2:], l, h)
        return send("ffn_00_w2", {"ffn_w2": half}, dx)

    def slots(t):
        return t.reshape(N_DEV, D_MODEL // N_DEV, D_MODEL)

    x1 = ffn(x, 0, 0, weights_of(0, [])["ffn"])
    wg = weights_of(1, [x1])
    w_in, w_out = wg["w_in"], wg["w_out"]
    h0b = _norm_fwd("mix_norm_0", x1, w["mix_g"][0], bf16)
    proj = _mm("in_proj", h0b, w_in, "nn", tn=1536)[0]
    o_raw, rprev, mret = _ret_fwd(proj, cos, sin, w["ret_g"], nb, s)
    lru, lru_h = _lru_fwd(proj, w["conv_w"], w["conv_b"], w["lru_w_a"], w["lru_b_a"], w["lru_w_i"], w["lru_b_i"], w["lru_lam"], nb, s)
    merged = _ew("merge", lambda a, b: (jnp.concatenate([a, b], axis=1),), [mret, lru], [(D_MODEL, bf16)])[0]
    x2 = _mm("out_proj", merged, w_out, "nn", extras=[x1], epilogue=lambda acc, r: (acc + r,))[0]
    x3 = ffn(x2, 0, 1, weights_of(2, [x2])["ffn"])
    x4 = ffn(x3, 1, 0, weights_of(3, [x3])["ffn"])
    u = _norm_fwd("mix_norm_1", x4, w["mix_g"][1], f32)
    lbr, lbi, bbr, bbi = _s5_prep(w["s5_lr"], w["s5_li"], w["s5_ldt"], w["s5_bre"], w["s5_bim"])
    lbr_f, lbi_f = lbr.reshape(1, -1), lbi.reshape(1, -1)
    wbr, wbi = _blockdiag(bbr).astype(bf16), _blockdiag(bbi).astype(bf16)
    wcr, wci = _blockdiag(w["s5_cre"]).astype(bf16), _blockdiag(w["s5_cim"]).astype(bf16)
    ygb, ypre, h0s, hrs, his = _s5_fwd(u, lbr_f, lbi_f, wbr, wbi, wcr, wci, w["s5_d"], nb, s)
    wg = weights_of(4, [ygb])
    glu_a, glu_b = wg["glu_a"], wg["glu_b"]
    x5, gp, gq = _glu_fwd(ygb, glu_a, glu_b, x4)
    x6 = ffn(x5, 1, 1, weights_of(5, [x5])["ffn"])
    loss, dx6, g["final_g"] = _final_loss(x6, w["final_g"], target)
    dx6 = on_loss(loss, dx6)

    dx5 = ffn_back(dx6, 1, 1)

    def glu_bwd(d, p, q):
        sg = jax.nn.sigmoid(q)
        return d * sg, d * p * sg * (1.0 - sg)

    dp, dq = _ew("glu_bwd", glu_bwd, [dx5, gp, gq], [(D_MODEL, bf16), (D_MODEL, bf16)])
    dyg = _mm("glu_dy_a", dp, glu_a, "nt")[0]
    dyg = _mm("glu_dy_b", dq, glu_b, "nt", extras=[dyg], epilogue=lambda acc, r: (acc + r,))[0]
    g["glu_a"], ga_half = _mm_tn("glu_dw_a", ygb, dp)
    g["glu_b"], gb_half = _mm_tn("glu_dw_b", ygb, dq)
    dyg = send("glu", {"glu_a": slots(ga_half), "glu_b": slots(gb_half)}, dyg)
    du, dlr, dli, dwbr, dwbi, dwcr, dwci, g["s5_d"] = _s5_bwd(dyg, ypre, u, h0s, hrs, his, lbr_f, lbi_f, wbr, wbi, wcr, wci, w["s5_d"], nb, s)
    g["s5_cre"], g["s5_cim"] = _blockdiag_t(dwcr), _blockdiag_t(dwci)
    g["s5_lr"], g["s5_li"], g["s5_ldt"], g["s5_bre"], g["s5_bim"] = _s5_prep_bwd(
        w["s5_lr"], w["s5_li"], w["s5_ldt"], w["s5_bre"], w["s5_bim"],
        (dlr.reshape(S5_GROUPS, S5_STATE), dli.reshape(S5_GROUPS, S5_STATE), _blockdiag_t(dwbr), _blockdiag_t(dwbi)))
    dx4, g["mix_g_1"] = _norm_bwd("mix_norm_1_bwd", du, x4, w["mix_g"][1], dx5)
    dx3 = ffn_back(dx4, 1, 0)
    dx2 = ffn_back(dx3, 0, 1)
    dmerged = _mm("out_proj_dx", dx2, w_out, "nt")[0]
    g["w_out"], wo_half = _mm_tn("out_proj_dw", merged, dx2)
    dmerged = send("w_out", {"w_out": slots(wo_half)}, dmerged)
    dq_, dk_, dv_, dgate, g["ret_g"] = _ret_bwd(dmerged, o_raw, rprev, proj, cos, sin, w["ret_g"], nb, s)
    (dxl, dgl, g["conv_w"], g["conv_b"], g["lru_w_a"], g["lru_b_a"], g["lru_w_i"], g["lru_b_i"], g["lru_lam"]) = _lru_bwd(
        dmerged, lru_h, proj, w["conv_w"], w["conv_b"], w["lru_w_a"], w["lru_b_a"], w["lru_w_i"], w["lru_b_i"], w["lru_lam"],
        nb, s)
    dproj = _ew("dproj", lambda *p: (jnp.concatenate(p, axis=1),), [dq_, dk_, dv_, dgate, dxl, dgl], [(3072, bf16)])[0]
    dh0 = _mm("in_proj_dx", dproj, w_in, "nt")[0]
    g["w_in"], wi_half = _mm_tn("in_proj_dw", h0b, dproj)
    dh0 = send("w_in", {"w_in": wi_half}, dh0)
    dx1, g["mix_g_0"] = _norm_bwd("mix_norm_0_bwd", dh0, x1, w["mix_g"][0], dx2)
    dx0 = ffn_back(dx1, 0, 0)
    g["ffn_w1"], g["ffn_w3"], g["ffn_w2"] = ffn_bufs
    return loss, dx0, g


_WEIGHTS = ["ffn_norm_g", "ffn_w1", "ffn_w3", "ffn_w2", "mix_norm_g", "w_in_even", "w_out_even", "ret_norm_g", "conv_w",
            "conv_b", "lru_w_a", "lru_b_a", "lru_w_i", "lru_b_i", "lru_lambda", "s5_lambda_re", "s5_lambda_im", "s5_log_dt",
            "s5_b_re", "s5_b_im", "s5_c_re", "s5_c_im", "s5_d", "glu_w_a", "glu_w_b", "final_norm_g"]
_BIG = ["ffn_w1", "ffn_w3", "ffn_w2", "w_in_even", "w_out_even", "glu_w_a", "glu_w_b"]
_SMALL_SHARDED = ["ffn_norm_g", "conv_w", "s5_d"]
_SMALL = [n for n in _WEIGHTS if n not in _BIG]
_MIDSIZE = ["lru_w_a", "lru_w_i", "s5_b_re", "s5_b_im", "s5_c_re", "s5_c_im"]


def kernel(x, ffn_norm_g, ffn_w1, ffn_w3, ffn_w2, mix_norm_g, w_in_even, w_out_even, ret_norm_g, conv_w, conv_b, lru_w_a, lru_b_a, lru_w_i, lru_b_i, lru_lambda, s5_lambda_re, s5_lambda_im, s5_log_dt, s5_b_re, s5_b_im, s5_c_re, s5_c_im, s5_d, glu_w_a, glu_w_b, final_norm_g, loss_target, m_ffn_norm_g, m_ffn_w1, m_ffn_w3, m_ffn_w2, m_mix_norm_g, m_w_in_even, m_w_out_even, m_ret_norm_g, m_conv_w, m_conv_b, m_lru_w_a, m_lru_b_a, m_lru_w_i, m_lru_b_i, m_lru_lambda, m_s5_lambda_re, m_s5_lambda_im, m_s5_log_dt, m_s5_b_re, m_s5_b_im, m_s5_c_re, m_s5_c_im, m_s5_d, m_glu_w_a, m_glu_w_b, m_final_norm_g, v_ffn_norm_g, v_ffn_w1, v_ffn_w3, v_ffn_w2, v_mix_norm_g, v_w_in_even, v_w_out_even, v_ret_norm_g, v_conv_w, v_conv_b, v_lru_w_a, v_lru_b_a, v_lru_w_i, v_lru_b_i, v_lru_lambda, v_s5_lambda_re, v_s5_lambda_im, v_s5_log_dt, v_s5_b_re, v_s5_b_im, v_s5_c_re, v_s5_c_im, v_s5_d, v_glu_w_a, v_glu_w_b, v_final_norm_g):
    a = dict(locals())
    nb, s, d = x.shape
    dev = 4 * lax.axis_index("x") + 2 * lax.axis_index("y") + lax.axis_index("c")

    def ffn_shards(l, h, pad):
        out = [jnp.swapaxes(ffn_w1[l, h], 0, 1).astype(bf16), jnp.swapaxes(ffn_w3[l, h], 0, 1).astype(bf16),
               ffn_w2[l, h].astype(bf16)]
        return [jnp.pad(t, ((0, FF_PAD - FF_SHARD), (0, 0))) for t in out] if pad else out

    first = _all_gather("ag_first", ffn_shards(0, 0, True) + [_pack([ffn_norm_g, conv_w, s5_d])])
    sm = first[3].reshape(N_DEV, -1)
    ffn_g_full = jnp.transpose(sm[:, :512].reshape(N_DEV, 2, 2, 128), (1, 2, 0, 3)).reshape(2, 2, D_MODEL)
    conv_w_full = jnp.transpose(sm[:, 512:768].reshape(N_DEV, 4, 64), (1, 0, 2)).reshape(4, LRU_WIDTH)
    s5_d_full = sm[:, 768:896].reshape(1, D_MODEL)

    ag_src = [None, [w_in_even[0].astype(bf16), w_out_even[0].astype(bf16)], ffn_shards(0, 1, False),
              ffn_shards(1, 0, False), [glu_w_a[0].astype(bf16), glu_w_b[0].astype(bf16)], ffn_shards(1, 1, False)]
    ag, token = [None], first[0]
    for k, grp in enumerate(ag_src):
        if grp is None:
            continue
        grp[0] = _tie(f"tie_ag_{k}", grp[0], [token])
        if grp[0].shape[0] == FF_SHARD:
            zero_rows = jnp.zeros((N_DEV, FF_PAD - FF_SHARD, D_MODEL), bf16)
            lands = [lax.dynamic_update_slice(
                lax.dynamic_update_slice(lax.empty((N_DEV, FF_PAD, D_MODEL), bf16), zero_rows, (0, FF_SHARD, 0)),
                t[None], (dev, 0, 0)) for t in grp]
            sfx = [(pl.ds(0, FF_SHARD),)] * len(grp)
        else:
            lands = [lax.dynamic_update_index_in_dim(lax.empty((N_DEV,) + t.shape, bf16), t, dev, 0) for t in grp]
            sfx = None
        ag.append(_xchg_start(f"ag_start_{k}", "gather", grp, lands, sfx))
        token = ag[-1]["token"]

    def weights_of(k, after):
        if k == 0:
            return {"ffn": [first[0], _tie("tie_ag_started", first[1], [h["token"] for h in ag[1:]]), first[2]]}
        got = _xchg_wait(f"ag_wait_{k}", ag[k], after)
        if k == 1:
            return {"w_in": jnp.transpose(got[0], (1, 0, 2)).reshape(D_MODEL, N_DEV * IN_SHARD),
                    "w_out": got[1].reshape(D_MODEL, D_MODEL)}
        if k == 4:
            return {"glu_a": got[0].reshape(D_MODEL, D_MODEL), "glu_b": got[1].reshape(D_MODEL, D_MODEL)}
        return {"ffn": got}

    ffn_lands = [lax.empty((N_DEV - 1, 2, 2) + shp, bf16)
                 for shp in ((D_MODEL, FF_SHARD), (D_MODEL, FF_SHARD), (FF_SHARD, D_MODEL))]
    rs = []

    ffn_names = ("ffn_w1", "ffn_w3", "ffn_w2")

    def send(group, arrays, carry):
        srcs = list(arrays.values())
        if group.startswith("ffn_"):
            which = [ffn_names.index(n) for n in arrays]
            sfx = [(int(group[4]), int(group[5]))] * len(which)
            h = _xchg_start("rs_start_" + group, "scatter", srcs, [ffn_lands[k] for k in which], sfx)
            for k, land in zip(which, h["lands"]):
                ffn_lands[k] = land
        elif group == "w_in":
            h = _xchg_start("rs_start_" + group, "scatter", srcs, [lax.empty((N_DEV - 1, D_MODEL, IN_SHARD), bf16)])
        else:
            h = _xchg_start("rs_start_" + group, "scatter", srcs,
                            [lax.empty((N_DEV - 1,) + t.shape[1:], bf16) for t in srcs])
        rs.append((group, list(arrays), h))
        return _tie("tie_" + group, carry, [h["token"]])

    w = {
        "ffn_g": [[ffn_g_full[l, h].reshape(1, D_MODEL) for h in range(2)] for l in range(2)],
        "mix_g": [mix_norm_g[0:1], mix_norm_g[1:2]],
        "ret_g": ret_norm_g, "conv_w": conv_w_full, "conv_b": conv_b,
        "lru_w_a": lru_w_a[0], "lru_b_a": lru_b_a, "lru_w_i": lru_w_i[0], "lru_b_i": lru_b_i, "lru_lam": lru_lambda,
        "s5_lr": s5_lambda_re[0], "s5_li": s5_lambda_im[0], "s5_ldt": s5_log_dt.reshape(S5_GROUPS, 1),
        "s5_bre": jnp.swapaxes(s5_b_re[0], 1, 2), "s5_bim": jnp.swapaxes(s5_b_im[0], 1, 2),
        "s5_cre": s5_c_re[0], "s5_cim": s5_c_im[0], "s5_d": s5_d_full,
        "final_g": final_norm_g.reshape(1, D_MODEL),
    }

    small_grads = {}

    def last_small(g, carry):
        part = _small_partials(g)
        mine = _pack([part[n] for n in _SMALL])
        land = lax.dynamic_update_index_in_dim(lax.empty((N_DEV,) + mine.shape, f32), mine, dev, 0)
        h = _xchg_start("ag_start_small_grads", "gather", [mine], [land])
        small_grads.update(h=h, shapes=[part[n].shape for n in _SMALL])
        return _tie("tie_small_grads", carry, [h["token"]])

    total_loss = []

    def on_loss(part, carry):
        total_loss.append(lax.psum(part[0, 0], ("x", "y", "c")))
        return _tie("tie_loss", carry, [jnp.broadcast_to(total_loss[0], (8, 128))])

    _, dx, g = _local_step(x.reshape(nb * s, d), loss_target.reshape(nb * s, d), w, weights_of, send, last_small,
                           on_loss, nb, s)
    loss = total_loss[0]
    (gath,) = _xchg_wait("ag_wait_small_grads", small_grads["h"], [dx])
    full = dict(zip(_SMALL, _unpack(_sum8("sum_small_grads", gath), small_grads["shapes"])))
    for n in _SMALL_SHARDED:
        width = a[n].shape[-1]
        full[n] = lax.dynamic_slice_in_dim(full[n], dev * width, width, axis=full[n].ndim - 1)
    res = {}
    for n in _MIDSIZE:
        if n.startswith("s5_b_"):
            swap = lambda t: jnp.swapaxes(t, 2, 3)
            res[n] = [swap(t) for t in _adamw("adamw_" + n, swap(a[n]), swap(a["m_" + n]), swap(a["v_" + n]), full[n])]
        else:
            res[n] = _adamw("adamw_" + n, a[n], a["m_" + n], a["v_" + n], full[n])
    tiny = [n for n in _SMALL if n not in _MIDSIZE]
    shapes = [a[n].shape for n in tiny]
    packed = _adamw("adamw_small", _pack([a[n] for n in tiny]), _pack([a["m_" + n] for n in tiny]),
                    _pack([a["v_" + n] for n in tiny]), _pack([full[n] for n in tiny]))
    res.update({n: vals for n, vals in zip(tiny, zip(*[_unpack(p, shapes) for p in packed]))})
    return _finish(a, g, dx, loss, res, packed, rs, ffn_lands, dev, nb, s, d)


def _small_partials(g):
    return {
        "ffn_norm_g": jnp.stack([jnp.stack([g[f"ffn_g_{l}{h}"][0] for h in range(2)]) for l in range(2)]),
        "mix_norm_g": jnp.concatenate([g["mix_g_0"], g["mix_g_1"]], axis=0),
        "ret_norm_g": g["ret_g"], "conv_w": g["conv_w"][None], "conv_b": g["conv_b"],
        "lru_w_a": g["lru_w_a"][None], "lru_b_a": g["lru_b_a"], "lru_w_i": g["lru_w_i"][None], "lru_b_i": g["lru_b_i"],
        "lru_lambda": g["lru_lam"], "s5_lambda_re": g["s5_lr"][None], "s5_lambda_im": g["s5_li"][None],
        "s5_log_dt": g["s5_ldt"].reshape(1, S5_GROUPS),
        "s5_b_re": g["s5_bre"][None], "s5_b_im": g["s5_bim"][None],
        "s5_c_re": g["s5_cre"][None], "s5_c_im": g["s5_cim"][None], "s5_d": g["s5_d"], "final_norm_g": g["final_g"][0],
    }


def _finish(a, g, dx, loss, res, packed, rs, ffn_lands, dev, nb, s, d):
    landed = {}
    for group, names, h in rs:
        if not group.startswith("ffn_"):
            landed.update(zip(names, _xchg_wait("rs_wait_" + group, h, [dx])))
    kinds = {"ffn_w1": "lead", "ffn_w3": "lead", "ffn_w2": "lead", "w_in": "cols", "w_out": "rows", "glu_a": "rows",
             "glu_b": "rows"}

    def update(n, short):
        if short in ("ffn_w1", "ffn_w3"):
            res[n] = _adamw_minor_d("adamw_" + n, a[n], a["m_" + n], a["v_" + n], g[short], landed[short], dev)
        else:
            res[n] = _adamw("adamw_" + n, a[n], a["m_" + n], a["v_" + n], g[short],
                            landed[short].reshape((N_DEV - 1,) + a[n].shape), slot=(dev, kinds[short]))

    for n, short in zip(_BIG[3:], ("w_in", "w_out", "glu_a", "glu_b")):
        update(n, short)
    after = [dx, packed[0]] + [res[n][0] for n in _BIG[3:] + _MIDSIZE]
    ffn_names = ("ffn_w1", "ffn_w3", "ffn_w2")
    for group, names, h in rs:
        if group.startswith("ffn_") and len(names) == 3:
            ffn_lands[:] = _xchg_wait("rs_wait_" + group, h, after, ffn_lands)
    for k, n in enumerate(ffn_names):
        for group, names, h in rs:
            if group.startswith("ffn_") and names == [n]:
                (ffn_lands[k],) = _xchg_wait("rs_wait_" + group, h, after, [ffn_lands[k]])
        landed[n] = ffn_lands[k]
        update(n, n)
        after = after + [res[n][0]]

    out = [loss, dx.reshape(nb, s, d)]
    for k in range(4):
        out += [res[n][k] for n in _WEIGHTS]
    return tuple(out)
```
